```python
import jax, jax.numpy as jnp
from jax import lax
import numpy as np

D_MODEL = 1024
BATCH = 8
SEQ = 4096
DEPTH = 1

D_PLE = 256
D_FF = 2816
NH_M = 4
DV_M = D_MODEL // 8
DK_M = DV_M // 2
CHUNK_M = 64
CONV_W = 4
NH_F = 8
DH_F = D_MODEL // 16
Q_BLOCK = 128
D_MIX = NH_M * DV_M + NH_F * DH_F
IN_SIZES = (2 * NH_M * DK_M, NH_M * DV_M, NH_M * DV_M, 2 * NH_M,
            NH_F * DH_F, NH_F * DH_F, NH_F * DH_F, NH_F)
N_IN = sum(IN_SIZES)
EPS = 1e-6
MLSTM_F_BIAS = 3.0
FOX_F_BIAS = 2.0

kernel_name = "hymba_mlstm_fox_macaron"


def rmsnorm(x, g):
    xf = x.astype(jnp.float32)
    y = xf * lax.rsqrt(jnp.mean(xf * xf, axis=-1, keepdims=True) + EPS)
    return (y * g.astype(jnp.float32)).astype(x.dtype)


def swiglu(x, w_gate, w_up, w_down):
    return (jax.nn.silu(x @ w_gate) * (x @ w_up)) @ w_down


def causal_depthwise_conv(x, w):
    return lax.conv_general_dilated(
        x, w[:, None, :].astype(x.dtype), window_strides=(1,),
        padding=[(w.shape[0] - 1, 0)], dimension_numbers=("NWC", "WIO", "NWC"),
        feature_group_count=x.shape[-1])


def head_rmsnorm(t, g):
    B, S, NH, DH = t.shape
    return rmsnorm(t, g.reshape(NH, DH)).reshape(B, S, NH * DH)


def mlstm_chunkwise(q, k, v, i_pre, f_pre):
    B, NH, S, DK = q.shape
    DV = v.shape[-1]
    L = CHUNK_M
    NC = S // L
    q = q.reshape(B, NH, NC, L, DK) * DK ** -0.5
    k = k.reshape(B, NH, NC, L, DK)
    v = v.reshape(B, NH, NC, L, DV)
    log_i = i_pre.reshape(B, NH, NC, L)
    log_f = jax.nn.log_sigmoid(f_pre).reshape(B, NH, NC, L)
    b = jnp.cumsum(log_f, axis=-1)
    g = b[..., -1]
    a = g[..., None] - b + log_i

    def step(carry, xs):
        C, n, m = carry
        k_c, v_c, a_c, g_c = xs
        m_new = jnp.maximum(g_c + m, jnp.max(a_c, axis=-1))
        decay = jnp.exp(g_c + m - m_new)
        w = jnp.exp(a_c - m_new[..., None])
        C_new = decay[..., None, None] * C + jnp.einsum("bhl,bhld,bhle->bhde", w, k_c, v_c)
        n_new = decay[..., None] * n + jnp.einsum("bhl,bhld->bhd", w, k_c)
        return (C_new, n_new, m_new), (C, n, m)

    init = (jnp.zeros((B, NH, DK, DV), jnp.float32), jnp.zeros((B, NH, DK), jnp.float32),
            jnp.zeros((B, NH), jnp.float32))
    xs = (jnp.moveaxis(k, 2, 0), jnp.moveaxis(v, 2, 0), jnp.moveaxis(a, 2, 0), jnp.moveaxis(g, 2, 0))
    _, (C_prev, n_prev, m_prev) = lax.scan(step, init, xs)
    C_prev = jnp.moveaxis(C_prev, 0, 2)
    n_prev = jnp.moveaxis(n_prev, 0, 2)
    m_prev = jnp.moveaxis(m_prev, 0, 2)

    inter = b + m_prev[..., None]
    causal = jnp.tril(jnp.ones((L, L), dtype=bool))
    D = jnp.where(causal, b[..., :, None] - b[..., None, :] + log_i[..., None, :], -jnp.inf)
    m_t = jnp.maximum(inter, jnp.max(D, axis=-1))
    w_inter = jnp.exp(inter - m_t)
    P = jnp.exp(D - m_t[..., None]) * jnp.einsum("bhcld,bhcsd->bhcls", q, k)
    num = (w_inter[..., None] * jnp.einsum("bhcld,bhcde->bhcle", q, C_prev)
           + jnp.einsum("bhcls,bhcse->bhcle", P, v))
    den = w_inter * jnp.einsum("bhcld,bhcd->bhcl", q, n_prev) + jnp.sum(P, axis=-1)
    h = num / jnp.maximum(jnp.abs(den), jnp.exp(-m_t))[..., None]
    return h.reshape(B, NH, S, DV)


def forgetting_attention(q, k, v, f_pre):
    S = q.shape[2]
    scale = q.shape[-1] ** -0.5
    c = jnp.cumsum(jax.nn.log_sigmoid(f_pre.astype(jnp.float32)), axis=-1)
    outs = []
    for blk in range(S // Q_BLOCK):
        q0, q1 = blk * Q_BLOCK, (blk + 1) * Q_BLOCK
        logits = jnp.einsum("bhqd,bhkd->bhqk", q[:, :, q0:q1], k[:, :, :q1]).astype(jnp.float32) * scale
        logits = logits + c[:, :, q0:q1, None] - c[:, :, None, :q1]
        mask = (q0 + jnp.arange(Q_BLOCK))[:, None] >= jnp.arange(q1)[None, :]
        probs = jax.nn.softmax(jnp.where(mask, logits, -jnp.inf), axis=-1).astype(v.dtype)
        outs.append(jnp.einsum("bhqk,bhkd->bhqd", probs, v[:, :, :q1]))
    return jnp.concatenate(outs, axis=2)


def _fwd_setup_inputs(seed: int = 0) -> dict:
    key = jax.random.key(seed)
    ks = jax.random.split(key, 24)
    f32 = jnp.float32

    def nrm(k, shape, fan_in):
        return jax.random.normal(k, shape, f32) * fan_in ** -0.5

    def gain(k, n):
        return 1.0 + 0.02 * jax.random.normal(k, (DEPTH, n), f32)

    b_gates = jnp.concatenate(
        [0.1 * jax.random.normal(ks[9], (DEPTH, NH_M), f32),
         MLSTM_F_BIAS + 0.1 * jax.random.normal(ks[10], (DEPTH, NH_M), f32)], axis=-1)
    return {
        "x": jax.random.normal(ks[0], (BATCH, SEQ, D_MODEL), f32),
        "p": jax.random.normal(ks[1], (DEPTH, BATCH, SEQ, D_PLE), f32),
        "ffn1_norm": gain(ks[2], D_MODEL),
        "ffn1_w_gate": nrm(ks[3], (DEPTH, D_MODEL, D_FF), D_MODEL),
        "ffn1_w_up": nrm(ks[4], (DEPTH, D_MODEL, D_FF), D_MODEL),
        "ffn1_w_down": nrm(ks[5], (DEPTH, D_FF, D_MODEL), D_FF),
        "mix_norm": gain(ks[6], D_MODEL),
        "w_in": nrm(ks[7], (DEPTH, D_MODEL, N_IN), D_MODEL),
        "conv_qk": nrm(ks[8], (DEPTH, CONV_W, 2 * NH_M * DK_M), CONV_W),
        "b_mlstm_gates": b_gates,
        "b_fox_f": FOX_F_BIAS + 0.1 * jax.random.normal(ks[11], (DEPTH, NH_F), f32),
        "mlstm_out_norm": gain(ks[12], NH_M * DV_M),
        "fox_out_norm": gain(ks[13], NH_F * DH_F),
        "w_out": nrm(ks[14], (DEPTH, D_MIX, D_MODEL), D_MIX),
        "ffn2_norm": gain(ks[15], D_MODEL),
        "ffn2_w_gate": nrm(ks[16], (DEPTH, D_MODEL, D_FF), D_MODEL),
        "ffn2_w_up": nrm(ks[17], (DEPTH, D_MODEL, D_FF), D_MODEL),
        "ffn2_w_down": nrm(ks[18], (DEPTH, D_FF, D_MODEL), D_FF),
        "ple_gate_norm": gain(ks[19], D_MODEL),
        "w_ple_gate": nrm(ks[20], (DEPTH, D_MODEL, D_MODEL), D_MODEL),
        "w_ple_proj": nrm(ks[21], (DEPTH, D_PLE, D_MODEL), D_PLE),
        "ple_proj_norm": gain(ks[22], D_MODEL),
        "final_norm": 1.0 + 0.02 * jax.random.normal(ks[23], (D_MODEL,), f32),
    }


def _fwd_reference(x, p, ffn1_norm, ffn1_w_gate, ffn1_w_up, ffn1_w_down, mix_norm, w_in, conv_qk,
              b_mlstm_gates, b_fox_f, mlstm_out_norm, fox_out_norm, w_out, ffn2_norm,
              ffn2_w_gate, ffn2_w_up, ffn2_w_down, ple_gate_norm, w_ple_gate, w_ple_proj,
              ple_proj_norm, final_norm):
    B, S, _ = x.shape
    split_idx = [int(s) for s in np.cumsum(IN_SIZES)[:-1]]

    def heads(t, nh):
        return t.reshape(B, S, nh, -1).transpose(0, 2, 1, 3)

    h = x
    for i in range(DEPTH):
        h = h + 0.5 * swiglu(rmsnorm(h, ffn1_norm[i]), ffn1_w_gate[i], ffn1_w_up[i], ffn1_w_down[i])

        u = rmsnorm(h, mix_norm[i])
        z = u @ w_in[i]
        m_qk, m_v, m_o, m_if, f_q, f_k, f_v, f_f = jnp.split(z, split_idx, axis=-1)

        m_qk = jax.nn.silu(causal_depthwise_conv(m_qk, conv_qk[i]))
        m_q, m_k = jnp.split(m_qk, 2, axis=-1)
        gates = (m_if.astype(jnp.float32) + b_mlstm_gates[i].astype(jnp.float32)).transpose(0, 2, 1)
        h_m = mlstm_chunkwise(heads(m_q, NH_M).astype(jnp.float32), heads(m_k, NH_M).astype(jnp.float32),
                              heads(m_v, NH_M).astype(jnp.float32), gates[:, :NH_M], gates[:, NH_M:])
        h_m = h_m.astype(u.dtype).transpose(0, 2, 1, 3)
        y_m = head_rmsnorm(h_m, mlstm_out_norm[i]) * jax.nn.sigmoid(m_o)

        f_gate = (f_f + b_fox_f[i]).transpose(0, 2, 1)
        h_f = forgetting_attention(heads(f_q, NH_F), heads(f_k, NH_F), heads(f_v, NH_F), f_gate)
        y_f = head_rmsnorm(h_f.transpose(0, 2, 1, 3), fox_out_norm[i])

        h = h + jnp.concatenate([y_m, y_f], axis=-1) @ w_out[i]

        h = h + 0.5 * swiglu(rmsnorm(h, ffn2_norm[i]), ffn2_w_gate[i], ffn2_w_up[i], ffn2_w_down[i])

        gate = jax.nn.sigmoid(rmsnorm(h, ple_gate_norm[i]) @ w_ple_gate[i])
        h = h + gate * rmsnorm(p[i] @ w_ple_proj[i], ple_proj_norm[i])

    return rmsnorm(h, final_norm)


import jax as _jax
import jax.numpy as _jnp

TWIN_FORMAT = 'train_step'
FWD_PARAMS = ['x', 'p', 'ffn1_norm', 'ffn1_w_gate', 'ffn1_w_up', 'ffn1_w_down', 'mix_norm', 'w_in', 'conv_qk', 'b_mlstm_gates', 'b_fox_f', 'mlstm_out_norm', 'fox_out_norm', 'w_out', 'ffn2_norm', 'ffn2_w_gate', 'ffn2_w_up', 'ffn2_w_down', 'ple_gate_norm', 'w_ple_gate', 'w_ple_proj', 'ple_proj_norm', 'final_norm']
TWIN_WEIGHTS = ['ffn1_norm', 'ffn1_w_gate', 'ffn1_w_up', 'ffn1_w_down', 'mix_norm', 'w_in', 'conv_qk', 'b_mlstm_gates', 'b_fox_f', 'mlstm_out_norm', 'fox_out_norm', 'w_out', 'ffn2_norm', 'ffn2_w_gate', 'ffn2_w_up', 'ffn2_w_down', 'ple_gate_norm', 'w_ple_gate', 'w_ple_proj', 'ple_proj_norm', 'final_norm']
TWIN_DIFF_INPUT = 'x'
TWIN_INPUTS = ['x', 'p', 'ffn1_norm', 'ffn1_w_gate', 'ffn1_w_up', 'ffn1_w_down', 'mix_norm', 'w_in', 'conv_qk', 'b_mlstm_gates', 'b_fox_f', 'mlstm_out_norm', 'fox_out_norm', 'w_out', 'ffn2_norm', 'ffn2_w_gate', 'ffn2_w_up', 'ffn2_w_down', 'ple_gate_norm', 'w_ple_gate', 'w_ple_proj', 'ple_proj_norm', 'final_norm', 'loss_target', 'm_ffn1_norm', 'm_ffn1_w_gate', 'm_ffn1_w_up', 'm_ffn1_w_down', 'm_mix_norm', 'm_w_in', 'm_conv_qk', 'm_b_mlstm_gates', 'm_b_fox_f', 'm_mlstm_out_norm', 'm_fox_out_norm', 'm_w_out', 'm_ffn2_norm', 'm_ffn2_w_gate', 'm_ffn2_w_up', 'm_ffn2_w_down', 'm_ple_gate_norm', 'm_w_ple_gate', 'm_w_ple_proj', 'm_ple_proj_norm', 'm_final_norm', 'v_ffn1_norm', 'v_ffn1_w_gate', 'v_ffn1_w_up', 'v_ffn1_w_down', 'v_mix_norm', 'v_w_in', 'v_conv_qk', 'v_b_mlstm_gates', 'v_b_fox_f', 'v_mlstm_out_norm', 'v_fox_out_norm', 'v_w_out', 'v_ffn2_norm', 'v_ffn2_w_gate', 'v_ffn2_w_up', 'v_ffn2_w_down', 'v_ple_gate_norm', 'v_w_ple_gate', 'v_w_ple_proj', 'v_ple_proj_norm', 'v_final_norm']
TWIN_OUTPUTS = ['loss', 'grad_x', 'grad_ffn1_norm', 'grad_ffn1_w_gate', 'grad_ffn1_w_up', 'grad_ffn1_w_down', 'grad_mix_norm', 'grad_w_in', 'grad_conv_qk', 'grad_b_mlstm_gates', 'grad_b_fox_f', 'grad_mlstm_out_norm', 'grad_fox_out_norm', 'grad_w_out', 'grad_ffn2_norm', 'grad_ffn2_w_gate', 'grad_ffn2_w_up', 'grad_ffn2_w_down', 'grad_ple_gate_norm', 'grad_w_ple_gate', 'grad_w_ple_proj', 'grad_ple_proj_norm', 'grad_final_norm', 'delta_ffn1_norm', 'delta_ffn1_w_gate', 'delta_ffn1_w_up', 'delta_ffn1_w_down', 'delta_mix_norm', 'delta_w_in', 'delta_conv_qk', 'delta_b_mlstm_gates', 'delta_b_fox_f', 'delta_mlstm_out_norm', 'delta_fox_out_norm', 'delta_w_out', 'delta_ffn2_norm', 'delta_ffn2_w_gate', 'delta_ffn2_w_up', 'delta_ffn2_w_down', 'delta_ple_gate_norm', 'delta_w_ple_gate', 'delta_w_ple_proj', 'delta_ple_proj_norm', 'delta_final_norm', 'new_m_ffn1_norm', 'new_m_ffn1_w_gate', 'new_m_ffn1_w_up', 'new_m_ffn1_w_down', 'new_m_mix_norm', 'new_m_w_in', 'new_m_conv_qk', 'new_m_b_mlstm_gates', 'new_m_b_fox_f', 'new_m_mlstm_out_norm', 'new_m_fox_out_norm', 'new_m_w_out', 'new_m_ffn2_norm', 'new_m_ffn2_w_gate', 'new_m_ffn2_w_up', 'new_m_ffn2_w_down', 'new_m_ple_gate_norm', 'new_m_w_ple_gate', 'new_m_w_ple_proj', 'new_m_ple_proj_norm', 'new_m_final_norm', 'new_v_ffn1_norm', 'new_v_ffn1_w_gate', 'new_v_ffn1_w_up', 'new_v_ffn1_w_down', 'new_v_mix_norm', 'new_v_w_in', 'new_v_conv_qk', 'new_v_b_mlstm_gates', 'new_v_b_fox_f', 'new_v_mlstm_out_norm', 'new_v_fox_out_norm', 'new_v_w_out', 'new_v_ffn2_norm', 'new_v_ffn2_w_gate', 'new_v_ffn2_w_up', 'new_v_ffn2_w_down', 'new_v_ple_gate_norm', 'new_v_w_ple_gate', 'new_v_w_ple_proj', 'new_v_ple_proj_norm', 'new_v_final_norm']
TWIN_LEAF_KINDS = {'loss': 'loss', 'grad_x': 'grad_x', 'grad_ffn1_norm': 'grad_w', 'grad_ffn1_w_gate': 'grad_w', 'grad_ffn1_w_up': 'grad_w', 'grad_ffn1_w_down': 'grad_w', 'grad_mix_norm': 'grad_w', 'grad_w_in': 'grad_w', 'grad_conv_qk': 'grad_w', 'grad_b_mlstm_gates': 'grad_w', 'grad_b_fox_f': 'grad_w', 'grad_mlstm_out_norm': 'grad_w', 'grad_fox_out_norm': 'grad_w', 'grad_w_out': 'grad_w', 'grad_ffn2_norm': 'grad_w', 'grad_ffn2_w_gate': 'grad_w', 'grad_ffn2_w_up': 'grad_w', 'grad_ffn2_w_down': 'grad_w', 'grad_ple_gate_norm': 'grad_w', 'grad_w_ple_gate': 'grad_w', 'grad_w_ple_proj': 'grad_w', 'grad_ple_proj_norm': 'grad_w', 'grad_final_norm': 'grad_w', 'delta_ffn1_norm': 'delta_w', 'delta_ffn1_w_gate': 'delta_w', 'delta_ffn1_w_up': 'delta_w', 'delta_ffn1_w_down': 'delta_w', 'delta_mix_norm': 'delta_w', 'delta_w_in': 'delta_w', 'delta_conv_qk': 'delta_w', 'delta_b_mlstm_gates': 'delta_w', 'delta_b_fox_f': 'delta_w', 'delta_mlstm_out_norm': 'delta_w', 'delta_fox_out_norm': 'delta_w', 'delta_w_out': 'delta_w', 'delta_ffn2_norm': 'delta_w', 'delta_ffn2_w_gate': 'delta_w', 'delta_ffn2_w_up': 'delta_w', 'delta_ffn2_w_down': 'delta_w', 'delta_ple_gate_norm': 'delta_w', 'delta_w_ple_gate': 'delta_w', 'delta_w_ple_proj': 'delta_w', 'delta_ple_proj_norm': 'delta_w', 'delta_final_norm': 'delta_w', 'new_m_ffn1_norm': 'new_m', 'new_m_ffn1_w_gate': 'new_m', 'new_m_ffn1_w_up': 'new_m', 'new_m_ffn1_w_down': 'new_m', 'new_m_mix_norm': 'new_m', 'new_m_w_in': 'new_m', 'new_m_conv_qk': 'new_m', 'new_m_b_mlstm_gates': 'new_m', 'new_m_b_fox_f': 'new_m', 'new_m_mlstm_out_norm': 'new_m', 'new_m_fox_out_norm': 'new_m', 'new_m_w_out': 'new_m', 'new_m_ffn2_norm': 'new_m', 'new_m_ffn2_w_gate': 'new_m', 'new_m_ffn2_w_up': 'new_m', 'new_m_ffn2_w_down': 'new_m', 'new_m_ple_gate_norm': 'new_m', 'new_m_w_ple_gate': 'new_m', 'new_m_w_ple_proj': 'new_m', 'new_m_ple_proj_norm': 'new_m', 'new_m_final_norm': 'new_m', 'new_v_ffn1_norm': 'new_v', 'new_v_ffn1_w_gate': 'new_v', 'new_v_ffn1_w_up': 'new_v', 'new_v_ffn1_w_down': 'new_v', 'new_v_mix_norm': 'new_v', 'new_v_w_in': 'new_v', 'new_v_conv_qk': 'new_v', 'new_v_b_mlstm_gates': 'new_v', 'new_v_b_fox_f': 'new_v', 'new_v_mlstm_out_norm': 'new_v', 'new_v_fox_out_norm': 'new_v', 'new_v_w_out': 'new_v', 'new_v_ffn2_norm': 'new_v', 'new_v_ffn2_w_gate': 'new_v', 'new_v_ffn2_w_up': 'new_v', 'new_v_ffn2_w_down': 'new_v', 'new_v_ple_gate_norm': 'new_v', 'new_v_w_ple_gate': 'new_v', 'new_v_w_ple_proj': 'new_v', 'new_v_ple_proj_norm': 'new_v', 'new_v_final_norm': 'new_v'}


def _forward(args):
    return _fwd_reference(*[args[k] for k in FWD_PARAMS])


def _output_shape():
    out = _jax.eval_shape(lambda: _forward(_fwd_setup_inputs(0)))
    return out.shape, out.dtype

N_MICROBATCH = 1
ADAM_LR = 0.001
ADAM_B1 = 0.9
ADAM_B2 = 0.999
ADAM_EPS = 1e-08
ADAM_WD = 0.01
ADAM_STEP = 10
PER_EXAMPLE_BATCH_AXIS = {'x': 0, 'p': 1, 'loss_target': 0}
SHARED_INPUTS = []
_WEIGHT_DTYPES = {'ffn1_norm': _jnp.float32, 'ffn1_w_gate': _jnp.float32, 'ffn1_w_up': _jnp.float32, 'ffn1_w_down': _jnp.float32, 'mix_norm': _jnp.float32, 'w_in': _jnp.float32, 'conv_qk': _jnp.float32, 'b_mlstm_gates': _jnp.float32, 'b_fox_f': _jnp.float32, 'mlstm_out_norm': _jnp.float32, 'fox_out_norm': _jnp.float32, 'w_out': _jnp.float32, 'ffn2_norm': _jnp.float32, 'ffn2_w_gate': _jnp.float32, 'ffn2_w_up': _jnp.float32, 'ffn2_w_down': _jnp.float32, 'ple_gate_norm': _jnp.float32, 'w_ple_gate': _jnp.float32, 'w_ple_proj': _jnp.float32, 'ple_proj_norm': _jnp.float32, 'final_norm': _jnp.float32}
MOMENT_SCALE = {'ffn1_norm': 9.023420e-02, 'ffn1_w_gate': 3.832362e-02, 'ffn1_w_up': 3.719345e-02, 'ffn1_w_down': 6.142394e-02, 'mix_norm': 1.712476e-01, 'w_in': 9.677595e-02, 'conv_qk': 1.036300e-01, 'b_mlstm_gates': 8.434274e-01, 'b_fox_f': 7.948263e-01, 'mlstm_out_norm': 7.243530e-02, 'fox_out_norm': 1.347883e-01, 'w_out': 1.046253e-01, 'ffn2_norm': 5.888253e-02, 'ffn2_w_gate': 2.284222e-02, 'ffn2_w_up': 2.219570e-02, 'ffn2_w_down': 3.667379e-02, 'ple_gate_norm': 2.880958e-02, 'w_ple_gate': 2.580345e-02, 'w_ple_proj': 6.644054e-02, 'ple_proj_norm': 1.207291e-01, 'final_norm': 3.188270e+01}


def _to_microbatches(a, axis):
    t = _jnp.moveaxis(a, axis, 0)
    t = t.reshape((N_MICROBATCH, t.shape[0] // N_MICROBATCH) + t.shape[1:])
    return _jnp.moveaxis(t, 1, axis + 1)


def setup_inputs(seed: int = 0) -> dict:
    inp = _fwd_setup_inputs(seed)
    key = _jax.random.fold_in(_jax.random.key(seed), 7919)
    shape, _ = _output_shape()
    out = dict(inp)
    out["loss_target"] = _jax.random.normal(_jax.random.fold_in(key, 0), shape, _jnp.float32)
    for i, name in enumerate(TWIN_WEIGHTS):
        w = inp[name].astype(_jnp.float32)
        if MOMENT_SCALE is None:
            s = _jnp.sqrt(_jnp.mean(_jnp.square(w)) + 1e-30)
        else:
            s = MOMENT_SCALE[name]
        km, kv = _jax.random.split(_jax.random.fold_in(key, i + 1))
        out[name] = w
        out["m_" + name] = s * _jax.random.normal(km, w.shape, _jnp.float32)
        out["v_" + name] = (s * s) * _jax.random.uniform(kv, w.shape, _jnp.float32, 0.5, 1.5)
    if N_MICROBATCH > 1:
        for name, axis in PER_EXAMPLE_BATCH_AXIS.items():
            out[name] = _to_microbatches(out[name], axis)
    return {'x': out['x'], 'p': out['p'], 'ffn1_norm': out['ffn1_norm'], 'ffn1_w_gate': out['ffn1_w_gate'], 'ffn1_w_up': out['ffn1_w_up'], 'ffn1_w_down': out['ffn1_w_down'], 'mix_norm': out['mix_norm'], 'w_in': out['w_in'], 'conv_qk': out['conv_qk'], 'b_mlstm_gates': out['b_mlstm_gates'], 'b_fox_f': out['b_fox_f'], 'mlstm_out_norm': out['mlstm_out_norm'], 'fox_out_norm': out['fox_out_norm'], 'w_out': out['w_out'], 'ffn2_norm': out['ffn2_norm'], 'ffn2_w_gate': out['ffn2_w_gate'], 'ffn2_w_up': out['ffn2_w_up'], 'ffn2_w_down': out['ffn2_w_down'], 'ple_gate_norm': out['ple_gate_norm'], 'w_ple_gate': out['w_ple_gate'], 'w_ple_proj': out['w_ple_proj'], 'ple_proj_norm': out['ple_proj_norm'], 'final_norm': out['final_norm'], 'loss_target': out['loss_target'], 'm_ffn1_norm': out['m_ffn1_norm'], 'm_ffn1_w_gate': out['m_ffn1_w_gate'], 'm_ffn1_w_up': out['m_ffn1_w_up'], 'm_ffn1_w_down': out['m_ffn1_w_down'], 'm_mix_norm': out['m_mix_norm'], 'm_w_in': out['m_w_in'], 'm_conv_qk': out['m_conv_qk'], 'm_b_mlstm_gates': out['m_b_mlstm_gates'], 'm_b_fox_f': out['m_b_fox_f'], 'm_mlstm_out_norm': out['m_mlstm_out_norm'], 'm_fox_out_norm': out['m_fox_out_norm'], 'm_w_out': out['m_w_out'], 'm_ffn2_norm': out['m_ffn2_norm'], 'm_ffn2_w_gate': out['m_ffn2_w_gate'], 'm_ffn2_w_up': out['m_ffn2_w_up'], 'm_ffn2_w_down': out['m_ffn2_w_down'], 'm_ple_gate_norm': out['m_ple_gate_norm'], 'm_w_ple_gate': out['m_w_ple_gate'], 'm_w_ple_proj': out['m_w_ple_proj'], 'm_ple_proj_norm': out['m_ple_proj_norm'], 'm_final_norm': out['m_final_norm'], 'v_ffn1_norm': out['v_ffn1_norm'], 'v_ffn1_w_gate': out['v_ffn1_w_gate'], 'v_ffn1_w_up': out['v_ffn1_w_up'], 'v_ffn1_w_down': out['v_ffn1_w_down'], 'v_mix_norm': out['v_mix_norm'], 'v_w_in': out['v_w_in'], 'v_conv_qk': out['v_conv_qk'], 'v_b_mlstm_gates': out['v_b_mlstm_gates'], 'v_b_fox_f': out['v_b_fox_f'], 'v_mlstm_out_norm': out['v_mlstm_out_norm'], 'v_fox_out_norm': out['v_fox_out_norm'], 'v_w_out': out['v_w_out'], 'v_ffn2_norm': out['v_ffn2_norm'], 'v_ffn2_w_gate': out['v_ffn2_w_gate'], 'v_ffn2_w_up': out['v_ffn2_w_up'], 'v_ffn2_w_down': out['v_ffn2_w_down'], 'v_ple_gate_norm': out['v_ple_gate_norm'], 'v_w_ple_gate': out['v_w_ple_gate'], 'v_w_ple_proj': out['v_w_ple_proj'], 'v_ple_proj_norm': out['v_ple_proj_norm'], 'v_final_norm': out['v_final_norm']}


def _loss(weights, diff, rest, loss_target):
    with _jax.named_scope("forward"):
        args = {**rest, TWIN_DIFF_INPUT: diff, **{k: w.astype(_WEIGHT_DTYPES[k]) for k, w in weights.items()}}
        y = _forward(args)
    with _jax.named_scope("loss_head"):
        err = _jnp.square(y.astype(_jnp.float32) - loss_target)
        return 0.5 * _jnp.sum(_jnp.mean(err, axis=-1)) if err.ndim else 0.5 * err


def _adamw(w, g, m, v):
    m = ADAM_B1 * m + (1.0 - ADAM_B1) * g
    v = ADAM_B2 * v + (1.0 - ADAM_B2) * _jnp.square(g)
    m_hat = m / (1.0 - ADAM_B1 ** ADAM_STEP)
    v_hat = v / (1.0 - ADAM_B2 ** ADAM_STEP)
    delta = -ADAM_LR * (m_hat / (_jnp.sqrt(v_hat) + ADAM_EPS) + ADAM_WD * w)
    return delta, m, v


def reference(x, p, ffn1_norm, ffn1_w_gate, ffn1_w_up, ffn1_w_down, mix_norm, w_in, conv_qk, b_mlstm_gates, b_fox_f, mlstm_out_norm, fox_out_norm, w_out, ffn2_norm, ffn2_w_gate, ffn2_w_up, ffn2_w_down, ple_gate_norm, w_ple_gate, w_ple_proj, ple_proj_norm, final_norm, loss_target, m_ffn1_norm, m_ffn1_w_gate, m_ffn1_w_up, m_ffn1_w_down, m_mix_norm, m_w_in, m_conv_qk, m_b_mlstm_gates, m_b_fox_f, m_mlstm_out_norm, m_fox_out_norm, m_w_out, m_ffn2_norm, m_ffn2_w_gate, m_ffn2_w_up, m_ffn2_w_down, m_ple_gate_norm, m_w_ple_gate, m_w_ple_proj, m_ple_proj_norm, m_final_norm, v_ffn1_norm, v_ffn1_w_gate, v_ffn1_w_up, v_ffn1_w_down, v_mix_norm, v_w_in, v_conv_qk, v_b_mlstm_gates, v_b_fox_f, v_mlstm_out_norm, v_fox_out_norm, v_w_out, v_ffn2_norm, v_ffn2_w_gate, v_ffn2_w_up, v_ffn2_w_down, v_ple_gate_norm, v_w_ple_gate, v_w_ple_proj, v_ple_proj_norm, v_final_norm):
    given = dict(x=x, p=p, ffn1_norm=ffn1_norm, ffn1_w_gate=ffn1_w_gate, ffn1_w_up=ffn1_w_up, ffn1_w_down=ffn1_w_down, mix_norm=mix_norm, w_in=w_in, conv_qk=conv_qk, b_mlstm_gates=b_mlstm_gates, b_fox_f=b_fox_f, mlstm_out_norm=mlstm_out_norm, fox_out_norm=fox_out_norm, w_out=w_out, ffn2_norm=ffn2_norm, ffn2_w_gate=ffn2_w_gate, ffn2_w_up=ffn2_w_up, ffn2_w_down=ffn2_w_down, ple_gate_norm=ple_gate_norm, w_ple_gate=w_ple_gate, w_ple_proj=w_ple_proj, ple_proj_norm=ple_proj_norm, final_norm=final_norm, loss_target=loss_target, m_ffn1_norm=m_ffn1_norm, m_ffn1_w_gate=m_ffn1_w_gate, m_ffn1_w_up=m_ffn1_w_up, m_ffn1_w_down=m_ffn1_w_down, m_mix_norm=m_mix_norm, m_w_in=m_w_in, m_conv_qk=m_conv_qk, m_b_mlstm_gates=m_b_mlstm_gates, m_b_fox_f=m_b_fox_f, m_mlstm_out_norm=m_mlstm_out_norm, m_fox_out_norm=m_fox_out_norm, m_w_out=m_w_out, m_ffn2_norm=m_ffn2_norm, m_ffn2_w_gate=m_ffn2_w_gate, m_ffn2_w_up=m_ffn2_w_up, m_ffn2_w_down=m_ffn2_w_down, m_ple_gate_norm=m_ple_gate_norm, m_w_ple_gate=m_w_ple_gate, m_w_ple_proj=m_w_ple_proj, m_ple_proj_norm=m_ple_proj_norm, m_final_norm=m_final_norm, v_ffn1_norm=v_ffn1_norm, v_ffn1_w_gate=v_ffn1_w_gate, v_ffn1_w_up=v_ffn1_w_up, v_ffn1_w_down=v_ffn1_w_down, v_mix_norm=v_mix_norm, v_w_in=v_w_in, v_conv_qk=v_conv_qk, v_b_mlstm_gates=v_b_mlstm_gates, v_b_fox_f=v_b_fox_f, v_mlstm_out_norm=v_mlstm_out_norm, v_fox_out_norm=v_fox_out_norm, v_w_out=v_w_out, v_ffn2_norm=v_ffn2_norm, v_ffn2_w_gate=v_ffn2_w_gate, v_ffn2_w_up=v_ffn2_w_up, v_ffn2_w_down=v_ffn2_w_down, v_ple_gate_norm=v_ple_gate_norm, v_w_ple_gate=v_w_ple_gate, v_w_ple_proj=v_w_ple_proj, v_ple_proj_norm=v_ple_proj_norm, v_final_norm=v_final_norm)
    weights = {n: given[n] for n in TWIN_WEIGHTS}
    shared = {n: given[n] for n in SHARED_INPUTS}
    per_example = {n: given[n] for n in ['x', 'p']}
    grad_fn = _jax.value_and_grad(_loss, argnums=(0, 1))

    def one_microbatch(ex, loss_target):
        ex = dict(ex)
        diff = ex.pop(TWIN_DIFF_INPUT)
        return grad_fn(weights, diff, {**shared, **ex}, loss_target)

    if N_MICROBATCH == 1:
        loss, (grad_w, grad_x) = one_microbatch(per_example, given["loss_target"])
    else:
        def body(carry, xs):
            loss_sum, grad_sum = carry
            l_k, (gw_k, gx_k) = one_microbatch(xs[0], xs[1])
            with _jax.named_scope("update"):
                return (loss_sum + l_k, _jax.tree.map(_jnp.add, grad_sum, gw_k)), gx_k

        init = (_jnp.zeros((), _jnp.float32), _jax.tree.map(_jnp.zeros_like, weights))
        (loss, grad_w), grad_x = _jax.lax.scan(body, init, (per_example, given["loss_target"]))
    with _jax.named_scope("update"):
        delta_w, new_m, new_v = {}, {}, {}
        for n in TWIN_WEIGHTS:
            delta_w[n], new_m[n], new_v[n] = _adamw(weights[n], grad_w[n], given["m_" + n], given["v_" + n])
    return (loss, grad_x, *[grad_w[n] for n in TWIN_WEIGHTS], *[delta_w[n] for n in TWIN_WEIGHTS],
            *[new_m[n] for n in TWIN_WEIGHTS], *[new_v[n] for n in TWIN_WEIGHTS])
```

```python
import functools
import math

import jax
import jax.numpy as jnp
from jax import lax
from jax.experimental import pallas as pl
from jax.experimental.pallas import tpu as pltpu

F32 = jnp.float32
BF16 = jnp.bfloat16
EPS = 1e-6
NH_M, DK_M, DV_M = 4, 64, 128
NH_F, DH_F = 8, 64
CONV_W = 4
ADAM_LR, ADAM_B1, ADAM_B2, ADAM_EPS, ADAM_WD, ADAM_STEP = 0.001, 0.9, 0.999, 1e-08, 0.01, 10
VMEM_LIMIT = 56 * 1024 * 1024


def _cparams(sem):
    return pltpu.CompilerParams(dimension_semantics=sem, vmem_limit_bytes=VMEM_LIMIT)


def _sigmoid(x):
    return 1.0 / (1.0 + jnp.exp(-x))


def _dot(a, b, ca, cb):
    return lax.dot_general(a.astype(BF16), b.astype(BF16), (((ca,), (cb,)), ((), ())), preferred_element_type=F32)


def _rowwise(name, fn, tiled, full, outs, accs=(), tm=256):
    rows = tiled[0].shape[0]
    tm = min(tm, rows)
    assert rows % tm == 0
    n_t, n_f, n_o, n_a = len(tiled), len(full), len(outs), len(accs)

    def body(*refs):
        ins = [r[...] for r in refs[: n_t + n_f]]
        res = fn(*ins)
        if not isinstance(res, (tuple, list)):
            res = (res,)
        orefs = refs[n_t + n_f:]
        for r, v in zip(orefs[:n_o], res[:n_o]):
            r[...] = v.astype(r.dtype)
        if n_a:
            @pl.when(pl.program_id(0) == 0)
            def _():
                for r in orefs[n_o:]:
                    r[...] = jnp.zeros_like(r)
            for r, v in zip(orefs[n_o:], res[n_o:]):
                r[...] += v.astype(r.dtype)

    in_specs = [pl.BlockSpec((tm, a.shape[1]), lambda i: (i, 0)) for a in tiled]
    in_specs += [pl.BlockSpec(a.shape, lambda i: (0, 0)) for a in full]
    out_specs = [pl.BlockSpec((tm, c), lambda i: (i, 0)) for c, _ in outs]
    out_specs += [pl.BlockSpec(s, lambda i: (0, 0)) for s, _ in accs]
    out_shape = [jax.ShapeDtypeStruct((rows, c), d) for c, d in outs]
    out_shape += [jax.ShapeDtypeStruct(s, d) for s, d in accs]
    res = pl.pallas_call(
        body, name=name, grid=(rows // tm,), in_specs=in_specs, out_specs=out_specs, out_shape=out_shape,
        compiler_params=_cparams(("arbitrary",) if n_a else ("parallel",)),
    )(*tiled, *full)
    return res


def _colsum(v):
    return jnp.sum(v, axis=0, keepdims=True)


def _rms_fwd_val(x, g):
    r = lax.rsqrt(jnp.mean(x * x, axis=-1, keepdims=True) + EPS)
    return x * r * g


def _rms_bwd_val(dy, x, g):
    r = lax.rsqrt(jnp.mean(x * x, axis=-1, keepdims=True) + EPS)
    xh = x * r
    dxh = dy * g
    dx = r * (dxh - xh * jnp.mean(dxh * xh, axis=-1, keepdims=True))
    return dx, _colsum(dy * xh)


def _mm(name, pairs, out_shape, out_block, out_map, grid, kaxis, ta=False, tb=False, scale=None, res=None,
        out_dtype=F32):
    nk = grid[kaxis]
    npairs = len(pairs)
    ca, cb = (0 if ta else 1), (1 if tb else 0)
    acc_shape = tuple(d for d in out_block if d is not None)

    def body(*refs):
        in_refs = refs[: 2 * npairs]
        res_ref = refs[2 * npairs] if res is not None else None
        o_ref = refs[2 * npairs + (1 if res is not None else 0)]
        acc_ref = refs[-1]
        k = pl.program_id(kaxis)

        @pl.when(k == 0)
        def _():
            acc_ref[...] = jnp.zeros_like(acc_ref)

        part = None
        for p in range(npairs):
            d = _dot(in_refs[2 * p][...], in_refs[2 * p + 1][...], ca, cb)
            part = d if part is None else part + d
        acc_ref[...] += part

        @pl.when(k == nk - 1)
        def _():
            v = acc_ref[...]
            if scale is not None:
                v = v * scale
            if res_ref is not None:
                v = v + res_ref[...].astype(F32)
            o_ref[...] = v.astype(o_ref.dtype)

    in_specs, args = [], []
    for a, ab, am, b, bb, bm in pairs:
        in_specs += [pl.BlockSpec(ab, am), pl.BlockSpec(bb, bm)]
        args += [a, b]
    if res is not None:
        in_specs.append(pl.BlockSpec(out_block, out_map))
        args.append(res)
    sem = tuple("arbitrary" if i == kaxis else "parallel" for i in range(len(grid)))
    return pl.pallas_call(
        body, name=name, grid=grid, in_specs=in_specs, out_specs=pl.BlockSpec(out_block, out_map),
        out_shape=jax.ShapeDtypeStruct(out_shape, out_dtype), scratch_shapes=[pltpu.VMEM(acc_shape, F32)],
        compiler_params=_cparams(sem),
    )(*args)


def _pick(n, pref):
    for t in pref:
        if n % t == 0:
            return t
    return n


def _mm_nn(name, a, b, tm=512, tn=512, tk=512, **kw):
    (m, k), n = a.shape, b.shape[1]
    tm, tn, tk = _pick(m, (tm, 256, 128)), _pick(n, (tn, 256, 128)), _pick(k, (tk, 256, 128))
    return _mm(name, [(a, (tm, tk), lambda i, j, kk: (i, kk), b, (tk, tn), lambda i, j, kk: (kk, j))],
               (m, n), (tm, tn), lambda i, j, kk: (i, j), (m // tm, n // tn, k // tk), 2, **kw)


def _mm_nt(name, a, b, tm=512, tn=512, tk=512, **kw):
    (m, k), n = a.shape, b.shape[0]
    tm, tn, tk = _pick(m, (tm, 256, 128)), _pick(n, (tn, 256, 128)), _pick(k, (tk, 256, 128))
    return _mm(name, [(a, (tm, tk), lambda i, j, kk: (i, kk), b, (tn, tk), lambda i, j, kk: (j, kk))],
               (m, n), (tm, tn), lambda i, j, kk: (i, j), (m // tm, n // tn, k // tk), 2, tb=True, **kw)


def _mm_tn(name, a, b, tm=512, tn=512, tk=512, **kw):
    (k, m), n = a.shape, b.shape[1]
    tm, tn, tk = _pick(m, (tm, 256, 128)), _pick(n, (tn, 256, 128)), _pick(k, (tk, 256, 128))
    return _mm(name, [(a, (tk, tm), lambda i, j, kk: (kk, i), b, (tk, tn), lambda i, j, kk: (kk, j))],
               (m, n), (tm, tn), lambda i, j, kk: (i, j), (m // tm, n // tn, k // tk), 2, ta=True, **kw)


def _ffn_fwd(pfx, h, gamma, wg, wu, wd):
    t, d = h.shape
    nb, _, f = wg.shape
    tm = _pick(t, (1024, 512, 256))
    (xn,) = _rowwise(pfx + "_norm", _rms_fwd_val, [h], [gamma], [(d, BF16)])

    def body(x_ref, wg_ref, wu_ref, g_ref, u_ref, a_ref):
        x = x_ref[...]
        g = _dot(x, wg_ref[...], 1, 0)
        u = _dot(x, wu_ref[...], 1, 0)
        g_ref[...] = g.astype(BF16)
        u_ref[...] = u.astype(BF16)
        a_ref[...] = (g * _sigmoid(g) * u).astype(BF16)

    blk = pl.BlockSpec((None, tm, f), lambda j, i: (j, i, 0))
    wspec = pl.BlockSpec((None, d, f), lambda j, i: (j, 0, 0))
    g_all, u_all, a_all = pl.pallas_call(
        body, name=pfx + "_gu", grid=(nb, t // tm),
        in_specs=[pl.BlockSpec((tm, d), lambda j, i: (i, 0)), wspec, wspec], out_specs=[blk] * 3,
        out_shape=[jax.ShapeDtypeStruct((nb, t, f), BF16)] * 3, compiler_params=_cparams(("parallel", "parallel")),
    )(xn, wg, wu)
    h_out = _mm(pfx + "_down", [(a_all, (None, tm, f), lambda i, j, k: (k, i, 0), wd, (None, f, d), lambda i, j, k: (k, 0, 0))],
                (t, d), (tm, d), lambda i, j, k: (i, 0), (t // tm, 1, nb), 2, scale=0.5, res=h)
    return h_out, xn, g_all, u_all


def _ffn_bwd(pfx, dh_out, h, gamma, xn, g_all, u_all, wg, wu, wd):
    t, d = h.shape
    nb, _, f = wg.shape
    tm = _pick(t, (1024, 512, 256))
    tk = _pick(t, (512, 256))

    def body(dy_ref, wd_ref, g_ref, u_ref, dg_ref, du_ref, a_ref):
        da = _dot(dy_ref[...], wd_ref[...], 1, 1) * 0.5
        g = g_ref[...].astype(F32)
        u = u_ref[...].astype(F32)
        s = _sigmoid(g)
        sl = g * s
        du_ref[...] = (da * sl).astype(BF16)
        dg_ref[...] = (da * u * (s * (1.0 + g * (1.0 - s)))).astype(BF16)
        a_ref[...] = (sl * u).astype(BF16)

    blk = pl.BlockSpec((None, tm, f), lambda j, i: (j, i, 0))
    dg_all, du_all, a_all = pl.pallas_call(
        body, name=pfx + "_bwd_gu", grid=(nb, t // tm),
        in_specs=[pl.BlockSpec((tm, d), lambda j, i: (i, 0)), pl.BlockSpec((None, f, d), lambda j, i: (j, 0, 0)), blk, blk],
        out_specs=[blk] * 3, out_shape=[jax.ShapeDtypeStruct((nb, t, f), BF16)] * 3,
        compiler_params=_cparams(("parallel", "parallel")),
    )(dh_out, wd, g_all, u_all)

    amap, wmap = (lambda i, j, k: (k, i, 0)), (lambda i, j, k: (k, 0, 0))
    dxn = _mm(pfx + "_bwd_dxn", [(dg_all, (None, tm, f), amap, wg, (None, d, f), wmap),
                                 (du_all, (None, tm, f), amap, wu, (None, d, f), wmap)],
              (t, d), (tm, d), lambda i, j, k: (i, 0), (t // tm, 1, nb), 2, tb=True)

    def nb_fn(dxn_t, h_t, dho_t, g):
        dx, dg = _rms_bwd_val(dxn_t, h_t, g)
        return dho_t + dx, dg

    dh, dgamma = _rowwise(pfx + "_bwd_norm", nb_fn, [dxn, h, dh_out], [gamma], [(d, F32)], [((1, d), F32)])

    xmap, bmap, omap = (lambda b, k: (k, 0)), (lambda b, k: (b, k, 0)), (lambda b, k: (b, 0, 0))
    dwg = _mm(pfx + "_dwg", [(xn, (tk, d), xmap, dg_all, (None, tk, f), bmap)], (nb, d, f), (None, d, f), omap,
              (nb, t // tk), 1, ta=True)
    dwu = _mm(pfx + "_dwu", [(xn, (tk, d), xmap, du_all, (None, tk, f), bmap)], (nb, d, f), (None, d, f), omap,
              (nb, t // tk), 1, ta=True)
    dwd = _mm(pfx + "_dwd", [(a_all, (None, tk, f), bmap, dh_out, (tk, d), xmap)], (nb, f, d), (None, f, d), omap,
              (nb, t // tk), 1, ta=True, scale=0.5)
    return dh, dgamma, dwg, dwu, dwd


HALO = 16


def _silu_grad(y):
    s = _sigmoid(y)
    return s * (1.0 + y * (1.0 - s))


def _conv_fwd(x_pad, w):
    tp, c = x_pad.shape
    t = tp - HALO
    tm = _pick(t, (512, 256))

    def body(x_ref, w_ref, o_ref):
        r0 = pl.multiple_of(pl.program_id(0) * tm, tm)
        xe = x_ref[pl.ds(r0, tm + HALO), :].astype(F32)
        wv = w_ref[...]
        y = xe * wv[3:4, :]
        for i in range(CONV_W - 1):
            y = y + pltpu.roll(xe, CONV_W - 1 - i, 0) * wv[i:i + 1, :]
        y = y[HALO:, :]
        o_ref[...] = (y * _sigmoid(y)).astype(o_ref.dtype)

    return pl.pallas_call(
        body, name="conv_fwd", grid=(t // tm,),
        in_specs=[pl.BlockSpec((tp, c), lambda i: (0, 0)), pl.BlockSpec(w.shape, lambda i: (0, 0))],
        out_specs=pl.BlockSpec((tm, c), lambda i: (i, 0)), out_shape=jax.ShapeDtypeStruct((t, c), BF16),
        compiler_params=_cparams(("parallel",)),
    )(x_pad, w)


def _conv_bwd(x_pad2, dact_pad, w):
    tp, c = x_pad2.shape
    t = tp - 2 * HALO
    tm = _pick(t, (512, 256))
    n = tm + HALO

    def body(x_ref, d_ref, w_ref, dx_ref, dw_ref):
        r0 = pl.multiple_of(pl.program_id(0) * tm, tm)
        xe = x_ref[pl.ds(r0, tm + 2 * HALO), :].astype(F32)
        de = d_ref[pl.ds(r0, n), :].astype(F32)
        wv = w_ref[...]
        sh = [pltpu.roll(xe, CONV_W - 1 - i, 0)[HALO:, :] if i < CONV_W - 1 else xe[HALO:, :] for i in range(CONV_W)]
        y = sh[0] * wv[0:1, :]
        for i in range(1, CONV_W):
            y = y + sh[i] * wv[i:i + 1, :]
        dy = de * _silu_grad(y)
        dx = dy * wv[3:4, :]
        for i in range(CONV_W - 1):
            dx = dx + pltpu.roll(dy, n - (CONV_W - 1 - i), 0) * wv[i:i + 1, :]
        dx_ref[...] = dx[:tm, :].astype(dx_ref.dtype)
        dyc = dy[:tm, :]
        dwp = jnp.concatenate([_colsum(dyc * sh[i][:tm, :]) for i in range(CONV_W)], axis=0)

        @pl.when(pl.program_id(0) == 0)
        def _():
            dw_ref[...] = jnp.zeros_like(dw_ref)
        dw_ref[...] += dwp

    return pl.pallas_call(
        body, name="conv_bwd", grid=(t // tm,),
        in_specs=[pl.BlockSpec((tp, c), lambda i: (0, 0)), pl.BlockSpec(dact_pad.shape, lambda i: (0, 0)),
                  pl.BlockSpec(w.shape, lambda i: (0, 0))],
        out_specs=[pl.BlockSpec((tm, c), lambda i: (i, 0)), pl.BlockSpec(w.shape, lambda i: (0, 0))],
        out_shape=[jax.ShapeDtypeStruct((t, c), BF16), jax.ShapeDtypeStruct(w.shape, F32)],
        compiler_params=_cparams(("arbitrary",)),
    )(x_pad2, dact_pad, w)


LM = 256
HI = lax.Precision.HIGHEST


def _logsig(x):
    return jnp.minimum(x, 0.0) - jnp.log(1.0 + jnp.exp(-jnp.abs(x)))


def _tri(n, lower):
    r = lax.broadcasted_iota(jnp.int32, (n, n), 0)
    c = lax.broadcasted_iota(jnp.int32, (n, n), 1)
    return (r >= c) if lower else (r <= c)


def _f32dot(a, b):
    return lax.dot_general(a, b, (((1,), (0,)), ((), ())), precision=HI, preferred_element_type=F32)


def _mlstm_chunk(h, q_ref, k_ref, v_ref, zs_ref, zsr_ref, bc_ref, br_ref, c_prev, m_prev):
    l = LM
    q = q_ref[h].astype(F32) * (DK_M ** -0.5)
    k = k_ref[h]
    v = v_ref[:, h * DV_M:(h + 1) * DV_M]
    lane = lax.broadcasted_iota(jnp.int32, (l, DV_M), 1)
    v1 = jnp.concatenate([v, (lane == 0).astype(v.dtype)], axis=1)
    zs, zsr = zs_ref[...], zsr_ref[...]
    li_c = zs[:, h:h + 1] + bc_ref[:, h:h + 1]
    fp_c = zs[:, NH_M + h:NH_M + h + 1] + bc_ref[:, NH_M + h:NH_M + h + 1]
    li_r = zsr[h:h + 1, :] + br_ref[h:h + 1, :]
    fp_r = zsr[NH_M + h:NH_M + h + 1, :] + br_ref[NH_M + h:NH_M + h + 1, :]
    lf_c, lf_r = _logsig(fp_c), _logsig(fp_r)
    low = _tri(l, True)
    b_c = _f32dot(low.astype(F32), lf_c)
    b_r = _f32dot(lf_r, _tri(l, False).astype(F32))
    g = b_r[:, l - 1:l]
    dmat = jnp.where(low, b_c - b_r + li_r, -jnp.inf)
    inter = b_c + m_prev
    m_t = jnp.maximum(inter, jnp.max(dmat, axis=1, keepdims=True))
    w_inter = jnp.exp(inter - m_t)
    amat = jnp.exp(dmat - m_t)
    s = _dot(q, k, 1, 1)
    p = amat * s
    qc = _dot(q, c_prev, 1, 0)
    qc_w = w_inter * qc
    num1 = qc_w + _dot(p, v1, 1, 0)
    den = num1[:, DV_M:DV_M + 1]
    mx = jnp.maximum(jnp.abs(den), jnp.exp(-m_t))
    hh = num1[:, :DV_M] / mx
    a_c = g - b_c + li_c
    return dict(q=q, k=k, v1=v1, fp_c=fp_c, fp_r=fp_r, b_c=b_c, g=g, m_t=m_t, w_inter=w_inter, amat=amat, s=s, p=p,
                qc_w=qc_w, den=den, mx=mx, hh=hh, a_c=a_c)


def _mlstm_fwd(q, k, zbig, zs, zsr, bc, br, gm):
    t = zs.shape[0]
    l = LM
    nc = t // l
    dm = NH_M * DV_M

    def body(q_ref, k_ref, v_ref, o_ref, zs_ref, zsr_ref, bc_ref, br_ref, gm_ref, y_ref, cst_ref, mst_ref, c_scr, m_scr):
        @pl.when(pl.program_id(0) == 0)
        def _():
            c_scr[...] = jnp.zeros_like(c_scr)
            m_scr[...] = jnp.zeros_like(m_scr)

        cst_ref[...] = c_scr[...]
        mst_ref[...] = m_scr[...]
        ys = []
        for h in range(NH_M):
            c_prev = c_scr[h]
            m_prev = m_scr[h:h + 1, 0:1]
            r = _mlstm_chunk(h, q_ref, k_ref, v_ref, zs_ref, zsr_ref, bc_ref, br_ref, c_prev, m_prev)
            hh = r["hh"]
            gh = gm_ref[:, h * DV_M:(h + 1) * DV_M]
            hn = hh * lax.rsqrt(jnp.mean(hh * hh, axis=-1, keepdims=True) + EPS) * gh
            og = o_ref[:, h * DV_M:(h + 1) * DV_M].astype(F32)
            ys.append(hn * _sigmoid(og))
            m_new = jnp.maximum(r["g"] + m_prev, jnp.max(r["a_c"], axis=0, keepdims=True))
            decay = jnp.exp(r["g"] + m_prev - m_new)
            wk = r["k"].astype(F32) * jnp.exp(r["a_c"] - m_new)
            c_scr[h] = decay * c_prev + _dot(wk, r["v1"], 0, 0)
            m_scr[h:h + 1, :] = jnp.broadcast_to(m_new, (1, 128))
        y_ref[...] = jnp.concatenate(ys, axis=1).astype(y_ref.dtype)

    return pl.pallas_call(
        body, name="mlstm_fwd", grid=(nc,),
        in_specs=[pl.BlockSpec((NH_M, l, DK_M), lambda i: (0, i, 0)), pl.BlockSpec((NH_M, l, DK_M), lambda i: (0, i, 0)),
                  pl.BlockSpec((l, dm), lambda i: (i, 1)), pl.BlockSpec((l, dm), lambda i: (i, 2)),
                  pl.BlockSpec((l, 128), lambda i: (i, 0)), pl.BlockSpec((8, l), lambda i: (0, i)),
                  pl.BlockSpec((1, 8), lambda i: (0, 0)), pl.BlockSpec((8, 1), lambda i: (0, 0)),
                  pl.BlockSpec((1, dm), lambda i: (0, 0))],
        out_specs=[pl.BlockSpec((l, dm), lambda i: (i, 0)), pl.BlockSpec((None, NH_M, DK_M, 2 * DV_M), lambda i: (i, 0, 0, 0)),
                   pl.BlockSpec((None, 8, 128), lambda i: (i, 0, 0))],
        out_shape=[jax.ShapeDtypeStruct((t, dm), BF16), jax.ShapeDtypeStruct((nc, NH_M, DK_M, 2 * DV_M), F32),
                   jax.ShapeDtypeStruct((nc, 8, 128), F32)],
        scratch_shapes=[pltpu.VMEM((NH_M, DK_M, 2 * DV_M), F32), pltpu.VMEM((8, 128), F32)],
        compiler_params=_cparams(("arbitrary",)),
    )(q, k, zbig, zbig, zs, zsr, bc, br, gm)


def _mlstm_bwd(q, k, zbig, zs, zsr, bc, br, gm, cst, mst, dycat):
    t = zs.shape[0]
    l = LM
    nc = t // l
    dm = NH_M * DV_M

    def body(q_ref, k_ref, v_ref, o_ref, zs_ref, zsr_ref, bc_ref, br_ref, gm_ref, cst_ref, mst_ref, cnx_ref, mnx_ref,
             dy_ref, dq_ref, dk_ref, dv_ref, do_ref, dzs_ref, dzr_ref, dgm_ref, dc_scr):
        @pl.when(pl.program_id(0) == 0)
        def _():
            dc_scr[...] = jnp.zeros_like(dc_scr)
            dgm_ref[...] = jnp.zeros_like(dgm_ref)

        lane = lax.broadcasted_iota(jnp.int32, (l, 128), 1)
        upper = _tri(l, False).astype(F32)
        lower = _tri(l, True).astype(F32)
        dzr_rows = [None] * 8
        dvs, dos, dgs = [], [], []
        dzs = jnp.zeros((l, 128), F32)
        for h in range(NH_M):
            c_prev = cst_ref[h]
            m_prev = mst_ref[h:h + 1, 0:1]
            r = _mlstm_chunk(h, q_ref, k_ref, v_ref, zs_ref, zsr_ref, bc_ref, br_ref, c_prev, m_prev)
            hh, mx, den, m_t, v1, amat = r["hh"], r["mx"], r["den"], r["m_t"], r["v1"], r["amat"]
            gh = gm_ref[:, h * DV_M:(h + 1) * DV_M]
            rs = lax.rsqrt(jnp.mean(hh * hh, axis=-1, keepdims=True) + EPS)
            xh = hh * rs
            sg = _sigmoid(o_ref[:, h * DV_M:(h + 1) * DV_M].astype(F32))
            dyh = dy_ref[:, h * DV_M:(h + 1) * DV_M]
            dos.append(dyh * xh * gh * sg * (1.0 - sg))
            dhn = dyh * sg
            dgs.append(_colsum(dhn * xh))
            dxh = dhn * gh
            dh = rs * (dxh - xh * jnp.mean(dxh * xh, axis=-1, keepdims=True))
            g1 = dh / mx
            hd = jnp.sum(hh * dh, axis=-1, keepdims=True)
            dden = jnp.where(jnp.abs(den) > jnp.exp(-m_t), -hd / mx * jnp.sign(den), 0.0)
            g256 = jnp.concatenate([g1, jnp.where(lane == 0, dden, 0.0)], axis=1)
            dc_h = dc_scr[h]
            ea = jnp.exp(r["a_c"])
            dp = _dot(g256, v1, 1, 1)
            ds = dp * amat
            dq_ref[h] = (r["w_inter"] * _dot(g256, c_prev, 1, 1) + _dot(ds, r["k"], 1, 0)) * (DK_M ** -0.5)
            dk_ref[h] = _dot(ds, r["q"], 0, 0) + ea * _dot(v1, dc_h, 1, 1)
            dv_st = ea * _dot(r["k"], dc_h, 1, 0)
            dv1 = _dot(r["p"], g256, 0, 0) + dv_st
            dvs.append(dv1[:, :DV_M])
            wmat = dp * r["p"]
            c_in = _colsum(wmat)
            c_st = jnp.sum(v1.astype(F32) * dv_st, axis=-1, keepdims=True)
            r_t = jnp.sum(wmat, axis=1, keepdims=True) + jnp.sum(g256 * r["qc_w"], axis=-1, keepdims=True)
            db = r_t - c_st
            carry = jnp.exp(mnx_ref[h:h + 1, 0:1]) * jnp.sum(
                jnp.sum(dc_h * cnx_ref[h], axis=1, keepdims=True), axis=0, keepdims=True)
            dlf_c = _f32dot(upper, db) + carry
            dlf_r = -_f32dot(c_in, lower)
            dfp = dlf_c * _sigmoid(-r["fp_c"])
            dzs = dzs + jnp.where(lane == h, c_st, 0.0) + jnp.where(lane == NH_M + h, dfp, 0.0)
            dzr_rows[h] = c_in
            dzr_rows[NH_M + h] = dlf_r * _sigmoid(-r["fp_r"])
            wq = r["q"] * jnp.exp(r["b_c"] - m_t)
            dc_scr[h] = jnp.exp(r["g"]) * dc_h + _dot(wq, g256, 0, 0)
        dv_ref[...] = jnp.concatenate(dvs, axis=1).astype(dv_ref.dtype)
        do_ref[...] = jnp.concatenate(dos, axis=1).astype(do_ref.dtype)
        dzs_ref[...] = dzs
        dzr_ref[...] = jnp.concatenate(dzr_rows, axis=0)
        dgm_ref[...] += jnp.concatenate(dgs, axis=1)

    rev = lambda i: nc - 1 - i
    nxt = lambda i: jnp.minimum(nc - i, nc - 1)
    return pl.pallas_call(
        body, name="mlstm_bwd", grid=(nc,),
        in_specs=[pl.BlockSpec((NH_M, l, DK_M), lambda i: (0, rev(i), 0)), pl.BlockSpec((NH_M, l, DK_M), lambda i: (0, rev(i), 0)),
                  pl.BlockSpec((l, dm), lambda i: (rev(i), 1)), pl.BlockSpec((l, dm), lambda i: (rev(i), 2)),
                  pl.BlockSpec((l, 128), lambda i: (rev(i), 0)), pl.BlockSpec((8, l), lambda i: (0, rev(i))),
                  pl.BlockSpec((1, 8), lambda i: (0, 0)), pl.BlockSpec((8, 1), lambda i: (0, 0)),
                  pl.BlockSpec((1, dm), lambda i: (0, 0)),
                  pl.BlockSpec((None, NH_M, DK_M, 2 * DV_M), lambda i: (rev(i), 0, 0, 0)),
                  pl.BlockSpec((None, 8, 128), lambda i: (rev(i), 0, 0)),
                  pl.BlockSpec((None, NH_M, DK_M, 2 * DV_M), lambda i: (nxt(i), 0, 0, 0)),
                  pl.BlockSpec((None, 8, 128), lambda i: (nxt(i), 0, 0)),
                  pl.BlockSpec((l, dm), lambda i: (rev(i), 0))],
        out_specs=[pl.BlockSpec((NH_M, l, DK_M), lambda i: (0, rev(i), 0)), pl.BlockSpec((NH_M, l, DK_M), lambda i: (0, rev(i), 0)),
                   pl.BlockSpec((l, dm), lambda i: (rev(i), 0)), pl.BlockSpec((l, dm), lambda i: (rev(i), 0)),
                   pl.BlockSpec((l, 128), lambda i: (rev(i), 0)), pl.BlockSpec((8, l), lambda i: (0, rev(i))),
                   pl.BlockSpec((1, dm), lambda i: (0, 0))],
        out_shape=[jax.ShapeDtypeStruct((NH_M, t, DK_M), F32), jax.ShapeDtypeStruct((NH_M, t, DK_M), F32),
                   jax.ShapeDtypeStruct((t, dm), BF16), jax.ShapeDtypeStruct((t, dm), BF16),
                   jax.ShapeDtypeStruct((t, 128), F32), jax.ShapeDtypeStruct((8, t), F32),
                   jax.ShapeDtypeStruct((1, dm), F32)],
        scratch_shapes=[pltpu.VMEM((NH_M, DK_M, 2 * DV_M), F32)],
        compiler_params=_cparams(("arbitrary",)),
    )(q, k, zbig, zbig, zs, zsr, bc, br, gm, cst, mst, cst, mst, dycat)


def _fox_cumsum(zsr, bf_r):
    t = zsr.shape[1]
    cw = _pick(t, (512, 256))

    def body(z_ref, b_ref, c_ref):
        up = _tri(cw, False).astype(F32)
        carry = jnp.zeros((NH_F, 1), F32)
        for j in range(t // cw):
            cs = _f32dot(_logsig(z_ref[:, j * cw:(j + 1) * cw] + b_ref[...]), up) + carry
            c_ref[:, j * cw:(j + 1) * cw] = cs
            carry = cs[:, cw - 1:cw]

    return pl.pallas_call(
        body, name="fox_cumsum", grid=(1,),
        in_specs=[pl.BlockSpec((NH_F, t), lambda i: (1, 0)), pl.BlockSpec((NH_F, 1), lambda i: (0, 0))],
        out_specs=pl.BlockSpec((NH_F, t), lambda i: (0, 0)), out_shape=jax.ShapeDtypeStruct((NH_F, t), F32),
        compiler_params=_cparams(("arbitrary",)),
    )(zsr, bf_r)


def _fox_gate_bwd(zsr, bf_r, dc):
    t = zsr.shape[1]
    cw = _pick(t, (512, 256))

    def body(z_ref, b_ref, dc_ref, o_ref):
        low = _tri(cw, True).astype(F32)
        carry = jnp.zeros((NH_F, 1), F32)
        for j in reversed(range(t // cw)):
            sl = slice(j * cw, (j + 1) * cw)
            dlf = _f32dot(dc_ref[:, sl], low) + carry
            o_ref[:, sl] = dlf * _sigmoid(-(z_ref[:, sl] + b_ref[...]))
            carry = dlf[:, 0:1]

    return pl.pallas_call(
        body, name="fox_gate_bwd", grid=(1,),
        in_specs=[pl.BlockSpec((NH_F, t), lambda i: (1, 0)), pl.BlockSpec((NH_F, 1), lambda i: (0, 0)),
                  pl.BlockSpec((NH_F, t), lambda i: (0, 0))],
        out_specs=pl.BlockSpec((NH_F, t), lambda i: (0, 0)), out_shape=jax.ShapeDtypeStruct((NH_F, t), F32),
        compiler_params=_cparams(("arbitrary",)),
    )(zsr, bf_r, dc)


def _causal_mask(n):
    return _tri(n, True)


def _fox_fwd(q, k, v, c_col, c_row, gf):
    nh, t, dh = q.shape
    tq = _pick(t, (512, 256))
    scale = dh ** -0.5

    def body(q_ref, k_ref, v_ref, cc_ref, cr_ref, g_ref, o_ref, lse_ref, y_ref):
        i = pl.program_id(1)
        qv = q_ref[...]
        cq = cc_ref[...]

        def blk(j, carry, masked):
            m, l, acc = carry
            k0 = pl.multiple_of(j * tq, tq)
            kb = k_ref[pl.ds(k0, tq), :]
            vb = v_ref[pl.ds(k0, tq), :]
            s = _dot(qv, kb, 1, 1) * scale + cq - cr_ref[:, pl.ds(k0, tq)]
            if masked:
                s = jnp.where(_causal_mask(tq), s, -jnp.inf)
            m_new = jnp.maximum(m, jnp.max(s, axis=1, keepdims=True))
            alpha = jnp.exp(m - m_new)
            p = jnp.exp(s - m_new)
            return m_new, alpha * l + jnp.sum(p, axis=1, keepdims=True), alpha * acc + _dot(p, vb, 1, 0)

        init = (jnp.full((tq, 1), -jnp.inf, F32), jnp.zeros((tq, 1), F32), jnp.zeros((tq, dh), F32))
        carry = lax.fori_loop(0, i, lambda j, c: blk(j, c, False), init)
        m, l, acc = blk(i, carry, True)
        o = acc / l
        o_ref[...] = o
        lse_ref[...] = m + jnp.log(l)
        y_ref[...] = (o * lax.rsqrt(jnp.mean(o * o, axis=-1, keepdims=True) + EPS) * g_ref[...]).astype(y_ref.dtype)

    full = lambda w: pl.BlockSpec((None, t, w), lambda h, i: (h, 0, 0))
    tile = lambda w: pl.BlockSpec((None, tq, w), lambda h, i: (h, i, 0))
    return pl.pallas_call(
        body, name="fox_fwd", grid=(nh, t // tq),
        in_specs=[tile(dh), full(dh), full(dh), tile(1), pl.BlockSpec((None, 1, t), lambda h, i: (h, 0, 0)),
                  pl.BlockSpec((None, 1, dh), lambda h, i: (h, 0, 0))],
        out_specs=[tile(dh), tile(1), tile(dh)],
        out_shape=[jax.ShapeDtypeStruct((nh, t, dh), F32), jax.ShapeDtypeStruct((nh, t, 1), F32),
                   jax.ShapeDtypeStruct((nh, t, dh), BF16)],
        compiler_params=_cparams(("parallel", "parallel")),
    )(q, k, v, c_col, c_row, gf)


def _fox_norm_bwd(dy, o, gf):
    nh, t, dh = o.shape
    tm = _pick(t, (512, 256))

    def body(dy_ref, o_ref, g_ref, do_ref, dl_ref, dg_ref):
        ov = o_ref[...]
        dx, dg = _rms_bwd_val(dy_ref[...], ov, g_ref[...])
        do_ref[...] = dx
        dl_ref[...] = jnp.sum(dx * ov, axis=-1, keepdims=True)

        @pl.when(pl.program_id(1) == 0)
        def _():
            dg_ref[...] = jnp.zeros_like(dg_ref)
        dg_ref[...] += dg

    tile = lambda w: pl.BlockSpec((None, tm, w), lambda h, i: (h, i, 0))
    gspec = pl.BlockSpec((None, 1, dh), lambda h, i: (h, 0, 0))
    return pl.pallas_call(
        body, name="fox_norm_bwd", grid=(nh, t // tm), in_specs=[tile(dh), tile(dh), gspec],
        out_specs=[tile(dh), tile(1), gspec],
        out_shape=[jax.ShapeDtypeStruct((nh, t, dh), F32), jax.ShapeDtypeStruct((nh, t, 1), F32),
                   jax.ShapeDtypeStruct((nh, 1, dh), F32)],
        compiler_params=_cparams(("parallel", "arbitrary")),
    )(dy, o, gf)


def _fox_bwd(q, k, v, c_col, c_row, do, lse, delta):
    nh, t, dh = q.shape
    tq = _pick(t, (512, 256))
    nq = t // tq
    scale = dh ** -0.5

    def body(q_ref, k_ref, v_ref, cc_ref, cr_ref, do_ref, lse_ref, dl_ref, dq_ref, dk_ref, dv_ref, dc_ref, dcq_ref):
        j = pl.program_id(1)

        @pl.when(j == 0)
        def _():
            dq_ref[...] = jnp.zeros_like(dq_ref)
            dcq_ref[...] = jnp.zeros_like(dcq_ref)

        kb, vb, crb = k_ref[...], v_ref[...], cr_ref[...]

        def blk(i, carry, masked):
            dk, dv, dc = carry
            rows = pl.ds(pl.multiple_of(i * tq, tq), tq)
            qb = q_ref[rows, :]
            dob = do_ref[rows, :].astype(BF16)
            s = _dot(qb, kb, 1, 1) * scale + cc_ref[rows, :] - crb
            if masked:
                s = jnp.where(_causal_mask(tq), s, -jnp.inf)
            p = jnp.exp(s - lse_ref[rows, :])
            dv = dv + _dot(p, dob, 0, 0)
            ds = p * (_dot(dob, vb, 1, 1) - dl_ref[rows, :])
            dc = dc + _colsum(ds)
            dk = dk + _dot(ds, qb, 0, 0) * scale
            dq_ref[rows, :] += _dot(ds, kb, 1, 0) * scale
            dcq_ref[rows, :] += jnp.sum(ds, axis=1, keepdims=True)
            return dk, dv, dc

        init = (jnp.zeros((tq, dh), F32), jnp.zeros((tq, dh), F32), jnp.zeros((1, tq), F32))
        carry = blk(j, init, True)
        dk, dv, dc = lax.fori_loop(j + 1, nq, lambda i, c: blk(i, c, False), carry)
        dk_ref[...] = dk
        dv_ref[...] = dv
        dc_ref[...] = -dc

    full = lambda w: pl.BlockSpec((None, t, w), lambda h, j: (h, 0, 0))
    tile = lambda w: pl.BlockSpec((None, tq, w), lambda h, j: (h, j, 0))
    crow = pl.BlockSpec((None, 1, tq), lambda h, j: (h, 0, j))
    return pl.pallas_call(
        body, name="fox_bwd", grid=(nh, nq),
        in_specs=[full(dh), tile(dh), tile(dh), full(1), crow, full(dh), full(1), full(1)],
        out_specs=[full(dh), tile(dh), tile(dh), crow, full(1)],
        out_shape=[jax.ShapeDtypeStruct((nh, t, dh), F32)] * 3 + [jax.ShapeDtypeStruct((nh, 1, t), F32),
                                                                jax.ShapeDtypeStruct((nh, t, 1), F32)],
        compiler_params=_cparams(("parallel", "arbitrary")),
    )(q, k, v, c_col, c_row, do, lse, delta)


W_BIG = 6 * 512
IN_OFF = (0, 512, 1024, 1544, 2056, 2568)
IN_GATES = (1536, 3080)


def _heads(a, nh):
    t = a.shape[0]
    return a.reshape(t, nh, -1).transpose(1, 0, 2)


def _unheads(a):
    nh, t, dh = a.shape
    return a.transpose(1, 0, 2).reshape(t, nh * dh)


def _local_step(x, p, tgt, sp, wg1, wu1, wd1, w_in, conv_w, w_out, wg2, wu2, wd2, w_pg, w_pp):
    t, d = x.shape
    w_big = jnp.concatenate([w_in[:, o:o + 512] for o in IN_OFF], axis=1)
    w_small = jnp.concatenate([w_in[:, IN_GATES[0]:IN_GATES[0] + 8], w_in[:, IN_GATES[1]:IN_GATES[1] + 8],
                               jnp.zeros((d, 112), w_in.dtype)], axis=1)
    h1, xn1, g1, u1 = _ffn_fwd("ffn1", x, sp["ffn1_norm"], wg1, wu1, wd1)
    (u,) = _rowwise("mix_norm", _rms_fwd_val, [h1], [sp["mix_norm"]], [(d, BF16)])
    zbig = _mm_nn("in_big", u, w_big, tm=1024, tk=1024, out_dtype=BF16)
    zs = _mm_nn("in_small", u, w_small, tm=1024, tk=1024)
    zsr = zs.T
    qk_in = zbig[:, :512]
    qk_act = _conv_fwd(jnp.pad(qk_in, ((HALO, 0), (0, 0))), conv_w)
    q_m, k_m = _heads(qk_act[:, :256], NH_M), _heads(qk_act[:, 256:], NH_M)
    bm_c, bf_c = sp["b_mlstm_gates"], sp["b_fox_f"]
    y_m, cst, mst = _mlstm_fwd(q_m, k_m, zbig, zs, zsr, bm_c, bm_c.T, sp["mlstm_out_norm"])
    q_f, k_f, v_f = (_heads(zbig[:, o:o + 512], NH_F) for o in (1536, 2048, 2560))
    c = _fox_cumsum(zsr, bf_c.T)
    c_col, c_row = c[:, :, None], c[:, None, :]
    gf = sp["fox_out_norm"].reshape(NH_F, 1, DH_F)
    o_f, lse, y_f = _fox_fwd(q_f, k_f, v_f, c_col, c_row, gf)
    y_ft = _unheads(y_f)
    tm = _pick(t, (1024, 512, 256))
    h2 = _mm("out_proj", [(y_m, (tm, 512), lambda i, j, k: (i, 0), w_out, (512, d), lambda i, j, k: (0, 0)),
                          (y_ft, (tm, 512), lambda i, j, k: (i, 0), w_out, (512, d), lambda i, j, k: (1, 0))],
             (t, d), (tm, d), lambda i, j, k: (i, 0), (t // tm, 1, 1), 2, res=h1)
    h3, xn2, g2, u2 = _ffn_fwd("ffn2", h2, sp["ffn2_norm"], wg2, wu2, wd2)
    (hn3,) = _rowwise("ple_norm", _rms_fwd_val, [h3], [sp["ple_gate_norm"]], [(d, BF16)])
    gate_pre = _mm_nn("ple_gate", hn3, w_pg, tm=1024, tk=1024)
    pp = _mm_nn("ple_proj", p, w_pp, tm=1024)

    def head_fn(h3_t, gp_t, pp_t, tgt_t, g_pp, g_fin):
        gate = _sigmoid(gp_t)
        ppn = _rms_fwd_val(pp_t, g_pp)
        h4 = h3_t + gate * ppn
        err = _rms_fwd_val(h4, g_fin) - tgt_t
        loss = 0.5 * jnp.sum(jnp.mean(err * err, axis=-1, keepdims=True), axis=0, keepdims=True)
        dh4, dg_fin = _rms_bwd_val(err * (1.0 / d), h4, g_fin)
        dpp, dg_pp = _rms_bwd_val(dh4 * gate, pp_t, g_pp)
        dgp = dh4 * ppn * gate * (1.0 - gate)
        return dh4, dgp, dpp, jnp.broadcast_to(loss, (1, 128)), dg_fin, dg_pp

    dh4, dgp, dpp, loss_part, dg_fin, dg_pp = _rowwise(
        "loss_head", head_fn, [h3, gate_pre, pp, tgt], [sp["ple_proj_norm"], sp["final_norm"]],
        [(d, F32), (d, BF16), (d, BF16)], [((1, 128), F32), ((1, d), F32), ((1, d), F32)])
    gw, gs = {}, {"final_norm": dg_fin, "ple_proj_norm": dg_pp}
    gw["w_ple_gate"] = _mm_tn("d_w_pg", hn3, dgp, tm=1024, tn=1024)
    gw["w_ple_proj"] = _mm_tn("d_w_pp", p, dpp, tn=1024)
    dhn3 = _mm_nt("d_hn3", dgp, w_pg, tm=1024, tn=1024, tk=1024)

    def res_norm_bwd(dn_t, h_t, dres_t, g):
        dx, dg = _rms_bwd_val(dn_t, h_t, g)
        return dres_t + dx, dg

    dh3, gs["ple_gate_norm"] = _rowwise("ple_norm_bwd", res_norm_bwd, [dhn3, h3, dh4], [sp["ple_gate_norm"]],
                                        [(d, F32)], [((1, d), F32)])
    dh2, gs["ffn2_norm"], gw["ffn2_w_gate"], gw["ffn2_w_up"], gw["ffn2_w_down"] = _ffn_bwd(
        "ffn2", dh3, h2, sp["ffn2_norm"], xn2, g2, u2, wg2, wu2, wd2)
    dycat = _mm_nt("d_ycat", dh2, w_out, tm=1024, tn=1024, tk=1024)
    gw["w_out"] = jnp.concatenate([_mm_tn("d_w_out_m", y_m, dh2, tn=1024), _mm_tn("d_w_out_f", y_ft, dh2, tn=1024)], axis=0)
    do_f, delta, dgf = _fox_norm_bwd(_heads(dycat[:, 512:], NH_F), o_f, gf)
    gs["fox_out_norm"] = dgf.reshape(1, NH_F * DH_F)
    dq_f, dk_f, dv_f, dc_r, dc_c = _fox_bwd(q_f, k_f, v_f, c_col, c_row, do_f, lse, delta)
    dfp = _fox_gate_bwd(zsr, bf_c.T, dc_r[:, 0, :] + dc_c[:, :, 0])
    dq_m, dk_m, dv_m, do_m, dzs_m, dzr_m, gs["mlstm_out_norm"] = _mlstm_bwd(
        q_m, k_m, zbig, zs, zsr, bm_c, bm_c.T, sp["mlstm_out_norm"], cst, mst, dycat)
    dact = jnp.concatenate([_unheads(dq_m), _unheads(dk_m)], axis=1)
    dqk, gw["conv_qk"] = _conv_bwd(jnp.pad(qk_in, ((HALO, HALO), (0, 0))), jnp.pad(dact, ((0, HALO), (0, 0))), conv_w)
    dz_big = jnp.concatenate([dqk, dv_m, do_m] + [_unheads(a).astype(BF16) for a in (dq_f, dk_f, dv_f)], axis=1)
    dzs = dzs_m + jnp.pad(jnp.concatenate([dzr_m, dfp], axis=0).T, ((0, 0), (0, 112)))
    dw_big = _mm_tn("d_w_big", u, dz_big, tm=1024)
    dw_small = _mm_tn("d_w_small", u, dzs, tm=1024)
    gw["w_in"] = jnp.concatenate([dw_big[:, 0:1536], dw_small[:, 0:8], dw_big[:, 1536:3072], dw_small[:, 8:16]], axis=1)
    du_a = _mm_nt("d_u_big", dz_big, w_big, tm=1024, tn=1024, tk=1024)
    du_b = _mm_nt("d_u_small", dzs, w_small, tm=1024, tn=1024)

    def mix_norm_bwd(da_t, db_t, h_t, dres_t, dzs_t, g):
        dx, dg = _rms_bwd_val(da_t + db_t, h_t, g)
        return dres_t + dx, dg, _colsum(dzs_t)

    dh1, gs["mix_norm"], dbias = _rowwise("mix_norm_bwd", mix_norm_bwd, [du_a, du_b, h1, dh2, dzs], [sp["mix_norm"]],
                                          [(d, F32)], [((1, d), F32), ((1, 128), F32)])
    gs["b_mlstm_gates"], gs["b_fox_f"] = dbias[:, 0:8], dbias[:, 8:16]
    grad_x, gs["ffn1_norm"], gw["ffn1_w_gate"], gw["ffn1_w_up"], gw["ffn1_w_down"] = _ffn_bwd(
        "ffn1", dh1, x, sp["ffn1_norm"], xn1, g1, u1, wg1, wu1, wd1)
    return loss_part, grad_x, gw, gs


ANY = pl.BlockSpec(memory_space=pl.ANY)
MESH = pl.DeviceIdType.MESH


def _place():
    x, y, c = lax.axis_index("x"), lax.axis_index("y"), lax.axis_index("c")
    chips = [(1 - x, y), (x, 1 - y), (1 - x, 1 - y)]
    return x, y, c, 2 * x + y, (x, y, 1 - c), chips


def _rcopy(src, dst, ssem, rsem, dev):
    return pltpu.make_async_remote_copy(src_ref=src, dst_ref=dst, send_sem=ssem, recv_sem=rsem, device_id=dev,
                                        device_id_type=MESH)


def _gather4(name, arrs, split):
    n = len(arrs)

    def body(*refs):
        ins, outs = refs[:n], refs[n:2 * n]
        lsem, isend, irecv, dsend, drecv = refs[2 * n:]
        x, y, c, me, sib, chips = _place()

        def part(ref, a):
            if not split[a]:
                return ref
            half = arrs[a].shape[0] // 2
            return ref.at[pl.ds(c * half, half)]

        def other(ref, a):
            half = arrs[a].shape[0] // 2
            return ref.at[pl.ds((1 - c) * half, half)]

        local = [pltpu.make_async_copy(ins[a], outs[a].at[me], lsem.at[a]) for a in range(n)]
        for cp in local:
            cp.start()
        sends = []
        for a in range(n):
            for j, chip in enumerate(chips):
                cp = _rcopy(part(ins[a], a), part(outs[a].at[me], a), isend.at[3 * a + j], irecv.at[3 * a + j], (*chip, c))
                cp.start()
                sends.append(cp)
        for j, (px, py) in enumerate(chips):
            src_chip = 2 * px + py
            for a in range(n):
                blk = part(outs[a].at[src_chip], a)
                _rcopy(blk, blk, isend.at[3 * a + j], irecv.at[3 * a + j], sib).wait_recv()
                if split[a]:
                    cp = _rcopy(blk, blk, dsend.at[3 * a + j], drecv.at[3 * a + j], sib)
                    cp.start()
                    sends.append(cp)
        for j, (px, py) in enumerate(chips):
            for a in range(n):
                if split[a]:
                    blk = other(outs[a].at[2 * px + py], a)
                    _rcopy(blk, blk, dsend.at[3 * a + j], drecv.at[3 * a + j], sib).wait_recv()
        for cp in sends:
            cp.wait_send()
        for cp in local:
            cp.wait()

    return pl.pallas_call(
        body, name=name, in_specs=[ANY] * n, out_specs=[ANY] * n,
        out_shape=[jax.ShapeDtypeStruct((4,) + a.shape, a.dtype) for a in arrs],
        scratch_shapes=[pltpu.SemaphoreType.DMA((n,))] + [pltpu.SemaphoreType.DMA((3 * n,))] * 4,
    )(*arrs)


def _swap_halves(name, arrs):
    n = len(arrs)

    def body(*refs):
        ins, outs = refs[:n], refs[n:2 * n]
        ssem, rsem = refs[2 * n:]
        x, y, c, me, sib, chips = _place()
        cps = []
        for a in range(n):
            half = arrs[a].shape[1] // 2
            cp = _rcopy(ins[a].at[:, pl.ds((1 - c) * half, half)], outs[a], ssem.at[a], rsem.at[a], sib)
            cp.start()
            cps.append(cp)
        for cp in cps:
            cp.wait()

    return pl.pallas_call(
        body, name=name, in_specs=[ANY] * n, out_specs=[ANY] * n,
        out_shape=[jax.ShapeDtypeStruct((4, a.shape[1] // 2) + a.shape[2:], a.dtype) for a in arrs],
        scratch_shapes=[pltpu.SemaphoreType.DMA((n,))] * 2,
    )(*arrs)


def _scatter4(name, arrs):
    n = len(arrs)

    def body(*refs):
        ins, outs = refs[:n], refs[n:2 * n]
        lsem, ssem, rsem = refs[2 * n:]
        x, y, c, me, sib, chips = _place()
        local = [pltpu.make_async_copy(ins[a].at[me], outs[a].at[me], lsem.at[a]) for a in range(n)]
        for cp in local:
            cp.start()
        cps = []
        for a in range(n):
            for j, (px, py) in enumerate(chips):
                cp = _rcopy(ins[a].at[2 * px + py], outs[a].at[me], ssem.at[3 * a + j], rsem.at[3 * a + j], (px, py, c))
                cp.start()
                cps.append(cp)
        for a in range(n):
            for j, (px, py) in enumerate(chips):
                blk = outs[a].at[2 * px + py]
                _rcopy(blk, blk, ssem.at[3 * a + j], rsem.at[3 * a + j], sib).wait_recv()
        for cp in cps:
            cp.wait_send()
        for cp in local:
            cp.wait()

    return pl.pallas_call(
        body, name=name, in_specs=[ANY] * n, out_specs=[ANY] * n,
        out_shape=[jax.ShapeDtypeStruct(a.shape, a.dtype) for a in arrs],
        scratch_shapes=[pltpu.SemaphoreType.DMA((n,))] + [pltpu.SemaphoreType.DMA((3 * n,))] * 2,
    )(*arrs)


def _join_halves(name, arrs):
    n = len(arrs)

    def body(*refs):
        ins, outs = refs[:n], refs[n:2 * n]
        lsem, ssem, rsem = refs[2 * n:]
        x, y, c, me, sib, chips = _place()
        cps, local = [], []
        for a in range(n):
            half = arrs[a].shape[0]
            mine = outs[a].at[pl.ds(c * half, half)]
            local.append(pltpu.make_async_copy(ins[a], mine, lsem.at[a]))
            local[-1].start()
            cp = _rcopy(ins[a], mine, ssem.at[a], rsem.at[a], sib)
            cp.start()
            cps.append(cp)
        for a in range(n):
            half = arrs[a].shape[0]
            blk = outs[a].at[pl.ds((1 - c) * half, half)]
            _rcopy(blk, blk, ssem.at[a], rsem.at[a], sib).wait_recv()
        for cp in cps:
            cp.wait_send()
        for cp in local:
            cp.wait()

    return pl.pallas_call(
        body, name=name, in_specs=[ANY] * n, out_specs=[ANY] * n,
        out_shape=[jax.ShapeDtypeStruct((2 * a.shape[0],) + a.shape[1:], a.dtype) for a in arrs],
        scratch_shapes=[pltpu.SemaphoreType.DMA((n,))] * 3,
    )(*arrs)


def _allreduce_small(s):
    r, cdim = s.shape

    def body(s_ref, o_ref, buf, ssem, rsem):
        x, y, c, me, sib, chips = _place()
        me8 = 4 * x + 2 * y + c
        buf[me8] = s_ref[...]
        flips = [(fx, fy, fc) for fx in (0, 1) for fy in (0, 1) for fc in (0, 1)][1:]
        cps = []
        for k, (fx, fy, fc) in enumerate(flips):
            peer = (x ^ fx if fx else x, y ^ fy if fy else y, c ^ fc if fc else c)
            cp = _rcopy(s_ref, buf.at[me8], ssem.at[k], rsem.at[k], peer)
            cp.start()
            cps.append(cp)
        for k, (fx, fy, fc) in enumerate(flips):
            src = 4 * (x ^ fx if fx else x) + 2 * (y ^ fy if fy else y) + (c ^ fc if fc else c)
            _rcopy(s_ref, buf.at[src], ssem.at[k], rsem.at[k], sib).wait_recv()
        for cp in cps:
            cp.wait_send()
        acc = buf[0]
        for k in range(1, 8):
            acc = acc + buf[k]
        o_ref[...] = acc

    vm = pl.BlockSpec(memory_space=pltpu.VMEM)
    return pl.pallas_call(
        body, name="allreduce_small", in_specs=[vm], out_specs=vm, out_shape=jax.ShapeDtypeStruct((r, cdim), F32),
        scratch_shapes=[pltpu.VMEM((8, r, cdim), F32), pltpu.SemaphoreType.DMA((7,)), pltpu.SemaphoreType.DMA((7,))],
    )(s)


def _add_my_half(name, g, recv, c_idx):
    nb, r, cdim = g.shape
    half = r // 2
    tr = _pick(half, (256, 176, 128, 64))
    g4 = g.reshape(nb, 2, half, cdim)

    def body(c_ref, g_ref, r_ref, o_ref):
        o_ref[...] = g_ref[...] + r_ref[...]

    return pl.pallas_call(
        body, name=name,
        grid_spec=pltpu.PrefetchScalarGridSpec(
            num_scalar_prefetch=1, grid=(nb, half // tr),
            in_specs=[pl.BlockSpec((None, None, tr, cdim), lambda b, i, c_ref: (b, c_ref[0], i, 0)),
                      pl.BlockSpec((None, tr, cdim), lambda b, i, c_ref: (b, i, 0))],
            out_specs=pl.BlockSpec((None, tr, cdim), lambda b, i, c_ref: (b, i, 0))),
        out_shape=jax.ShapeDtypeStruct((nb, half, cdim), g.dtype), compiler_params=_cparams(("parallel", "parallel")),
    )(c_idx, g4, recv)


def _sum4(name, a):
    nb, h, cdim = a.shape
    tr = _pick(h, (256, 176, 128, 64))

    def body(a_ref, o_ref):
        o_ref[...] = ((a_ref[0] + a_ref[1]) + a_ref[2]) + a_ref[3]

    return pl.pallas_call(
        body, name=name, grid=(h // tr,), in_specs=[pl.BlockSpec((nb, tr, cdim), lambda i: (0, i, 0))],
        out_specs=pl.BlockSpec((tr, cdim), lambda i: (i, 0)), out_shape=jax.ShapeDtypeStruct((h, cdim), a.dtype),
        compiler_params=_cparams(("parallel",)),
    )(a)


def _adamw(name, w, g, m, v):
    c1 = 1.0 - ADAM_B1 ** ADAM_STEP
    c2 = 1.0 - ADAM_B2 ** ADAM_STEP

    def fn(w_t, g_t, m_t, v_t):
        m_n = ADAM_B1 * m_t + (1.0 - ADAM_B1) * g_t
        v_n = ADAM_B2 * v_t + (1.0 - ADAM_B2) * (g_t * g_t)
        delta = -ADAM_LR * ((m_n / c1) / (jnp.sqrt(v_n / c2) + ADAM_EPS) + ADAM_WD * w_t)
        return delta, m_n, v_n

    cdim = w.shape[1]
    return _rowwise(name, fn, [w, g, m, v], [], [(cdim, F32)] * 3, tm=_pick(w.shape[0], (256, 176, 128, 64, 8)))


BIG = ("ffn1_w_gate", "ffn1_w_up", "ffn1_w_down", "w_in", "w_out", "ffn2_w_gate", "ffn2_w_up", "ffn2_w_down",
       "w_ple_gate", "w_ple_proj")
SMALL = ("ffn1_norm", "mix_norm", "b_mlstm_gates", "b_fox_f", "mlstm_out_norm", "fox_out_norm", "ffn2_norm",
         "ple_gate_norm", "ple_proj_norm", "final_norm")
WEIGHTS = ("ffn1_norm", "ffn1_w_gate", "ffn1_w_up", "ffn1_w_down", "mix_norm", "w_in", "conv_qk", "b_mlstm_gates",
           "b_fox_f", "mlstm_out_norm", "fox_out_norm", "w_out", "ffn2_norm", "ffn2_w_gate", "ffn2_w_up", "ffn2_w_down",
           "ple_gate_norm", "w_ple_gate", "w_ple_proj", "ple_proj_norm", "final_norm")
PACK_W = 1024


def _chip_blocks(a):
    r, c4 = a.shape
    return a.reshape(r, 4, c4 // 4).transpose(1, 0, 2)


def _from_chip_blocks(a):
    nb, r, c = a.shape
    return a.transpose(1, 0, 2).reshape(r, nb * c)


def kernel(x, p, ffn1_norm, ffn1_w_gate, ffn1_w_up, ffn1_w_down, mix_norm, w_in, conv_qk, b_mlstm_gates, b_fox_f, mlstm_out_norm, fox_out_norm, w_out, ffn2_norm, ffn2_w_gate, ffn2_w_up, ffn2_w_down, ple_gate_norm, w_ple_gate, w_ple_proj, ple_proj_norm, final_norm, loss_target, m_ffn1_norm, m_ffn1_w_gate, m_ffn1_w_up, m_ffn1_w_down, m_mix_norm, m_w_in, m_conv_qk, m_b_mlstm_gates, m_b_fox_f, m_mlstm_out_norm, m_fox_out_norm, m_w_out, m_ffn2_norm, m_ffn2_w_gate, m_ffn2_w_up, m_ffn2_w_down, m_ple_gate_norm, m_w_ple_gate, m_w_ple_proj, m_ple_proj_norm, m_final_norm, v_ffn1_norm, v_ffn1_w_gate, v_ffn1_w_up, v_ffn1_w_down, v_mix_norm, v_w_in, v_conv_qk, v_b_mlstm_gates, v_b_fox_f, v_mlstm_out_norm, v_fox_out_norm, v_w_out, v_ffn2_norm, v_ffn2_w_gate, v_ffn2_w_up, v_ffn2_w_down, v_ple_gate_norm, v_w_ple_gate, v_w_ple_proj, v_ple_proj_norm, v_final_norm):
    w = dict(ffn1_norm=ffn1_norm, ffn1_w_gate=ffn1_w_gate, ffn1_w_up=ffn1_w_up, ffn1_w_down=ffn1_w_down, mix_norm=mix_norm,
             w_in=w_in, conv_qk=conv_qk, b_mlstm_gates=b_mlstm_gates, b_fox_f=b_fox_f, mlstm_out_norm=mlstm_out_norm,
             fox_out_norm=fox_out_norm, w_out=w_out, ffn2_norm=ffn2_norm, ffn2_w_gate=ffn2_w_gate, ffn2_w_up=ffn2_w_up,
             ffn2_w_down=ffn2_w_down, ple_gate_norm=ple_gate_norm, w_ple_gate=w_ple_gate, w_ple_proj=w_ple_proj,
             ple_proj_norm=ple_proj_norm, final_norm=final_norm)
    m = dict(ffn1_norm=m_ffn1_norm, ffn1_w_gate=m_ffn1_w_gate, ffn1_w_up=m_ffn1_w_up, ffn1_w_down=m_ffn1_w_down,
             mix_norm=m_mix_norm, w_in=m_w_in, conv_qk=m_conv_qk, b_mlstm_gates=m_b_mlstm_gates, b_fox_f=m_b_fox_f,
             mlstm_out_norm=m_mlstm_out_norm, fox_out_norm=m_fox_out_norm, w_out=m_w_out, ffn2_norm=m_ffn2_norm,
             ffn2_w_gate=m_ffn2_w_gate, ffn2_w_up=m_ffn2_w_up, ffn2_w_down=m_ffn2_w_down, ple_gate_norm=m_ple_gate_norm,
             w_ple_gate=m_w_ple_gate, w_ple_proj=m_w_ple_proj, ple_proj_norm=m_ple_proj_norm, final_norm=m_final_norm)
    v = dict(ffn1_norm=v_ffn1_norm, ffn1_w_gate=v_ffn1_w_gate, ffn1_w_up=v_ffn1_w_up, ffn1_w_down=v_ffn1_w_down,
             mix_norm=v_mix_norm, w_in=v_w_in, conv_qk=v_conv_qk, b_mlstm_gates=v_b_mlstm_gates, b_fox_f=v_b_fox_f,
             mlstm_out_norm=v_mlstm_out_norm, fox_out_norm=v_fox_out_norm, w_out=v_w_out, ffn2_norm=v_ffn2_norm,
             ffn2_w_gate=v_ffn2_w_gate, ffn2_w_up=v_ffn2_w_up, ffn2_w_down=v_ffn2_w_down, ple_gate_norm=v_ple_gate_norm,
             w_ple_gate=v_w_ple_gate, w_ple_proj=v_w_ple_proj, ple_proj_norm=v_ple_proj_norm, final_norm=v_final_norm)
    shapes = {n: w[n].shape for n in WEIGHTS}
    w2 = {n: a.reshape(-1, a.shape[-1]) for n, a in w.items()}

    shards = [w2[n].astype(BF16) for n in BIG] + [w2["conv_qk"]]
    full = dict(zip(BIG + ("conv_qk",), _gather4("gather_weights", shards, [True] * len(BIG) + [False])))
    sp = {n: w2[n] for n in SMALL}
    loss_part, grad_x, gw, gs = _local_step(
        x[0], p[0, 0], loss_target[0], sp, full["ffn1_w_gate"], full["ffn1_w_up"], full["ffn1_w_down"],
        _from_chip_blocks(full["w_in"]), _from_chip_blocks(full["conv_qk"]), full["w_out"].reshape(-1, w_out.shape[-1]),
        full["ffn2_w_gate"], full["ffn2_w_up"], full["ffn2_w_down"], full["w_ple_gate"].reshape(-1, w_ple_gate.shape[-1]),
        _from_chip_blocks(full["w_ple_proj"]))
    loss = lax.psum(loss_part[0, 0], ("x", "y", "c"))

    for n in ("w_in", "w_ple_proj"):
        gw[n] = _chip_blocks(gw[n])
    for n in ("w_out", "w_ple_gate"):
        gw[n] = gw[n].reshape(4, -1, gw[n].shape[-1])
    blocks = [gw[n] for n in BIG]
    c_idx = lax.axis_index("c").astype(jnp.int32).reshape(1)
    swapped = _swap_halves("rs_swap", blocks)
    partial = [_add_my_half("rs_add_" + n, g, r, c_idx) for n, g, r in zip(BIG, blocks, swapped)]
    landed = _scatter4("rs_scatter", partial)
    halves = [_sum4("rs_sum_" + n, a) for n, a in zip(BIG, landed)]
    grads = dict(zip(BIG, _join_halves("rs_join", halves)))

    small = [gs[n].reshape(1, -1) for n in SMALL] + [gw["conv_qk"]]
    rows = [jnp.pad(a, ((0, 0), (0, PACK_W - a.shape[1]))) for a in small]
    packed = jnp.concatenate(rows, axis=0)
    packed = jnp.pad(packed, ((0, -packed.shape[0] % 8), (0, 0)))
    red = _allreduce_small(packed)
    for i, n in enumerate(SMALL):
        grads[n] = red[i:i + 1, :gs[n].size]
    dconv = red[len(SMALL):len(SMALL) + CONV_W, :gw["conv_qk"].shape[1]]
    cw = conv_qk.shape[-1]
    grads["conv_qk"] = lax.dynamic_slice_in_dim(dconv, (2 * lax.axis_index("x") + lax.axis_index("y")) * cw, cw, axis=1)

    outs = {}
    for n in WEIGHTS:
        g2 = grads[n].reshape(w2[n].shape)
        d, nm, nv = _adamw("adamw_" + n, w2[n], g2, m[n].reshape(w2[n].shape), v[n].reshape(w2[n].shape))
        outs[n] = tuple(a.reshape(shapes[n]) for a in (g2, d, nm, nv))
    return (loss, grad_x[None], *[outs[n][0] for n in WEIGHTS], *[outs[n][1] for n in WEIGHTS],
            *[outs[n][2] for n in WEIGHTS], *[outs[n][3] for n in WEIGHTS])
```

```python
import functools
import math

import jax
import jax.numpy as jnp
from jax import lax
from jax.experimental import pallas as pl
from jax.experimental.pallas import tpu as pltpu

F32 = jnp.float32
BF16 = jnp.bfloat16
EPS = 1e-6
NH_M, DK_M, DV_M = 4, 64, 128
NH_F, DH_F = 8, 64
CONV_W = 4
ADAM_LR, ADAM_B1, ADAM_B2, ADAM_EPS, ADAM_WD, ADAM_STEP = 0.001, 0.9, 0.999, 1e-08, 0.01, 10
VMEM_LIMIT = 56 * 1024 * 1024


def _cparams(sem):
    return pltpu.CompilerParams(dimension_semantics=sem, vmem_limit_bytes=VMEM_LIMIT)


def _sigmoid(x):
    return 1.0 / (1.0 + jnp.exp(-x))


def _dot(a, b, ca, cb):
    return lax.dot_general(a.astype(BF16), b.astype(BF16), (((ca,), (cb,)), ((), ())), preferred_element_type=F32)


def _rowwise(name, fn, tiled, full, outs, accs=(), tm=256):
    rows = tiled[0].shape[0]
    tm = min(tm, rows)
    assert rows % tm == 0
    n_t, n_f, n_o, n_a = len(tiled), len(full), len(outs), len(accs)

    def body(*refs):
        ins = [r[...] for r in refs[: n_t + n_f]]
        res = fn(*ins)
        if not isinstance(res, (tuple, list)):
            res = (res,)
        orefs = refs[n_t + n_f:]
        for r, v in zip(orefs[:n_o], res[:n_o]):
            r[...] = v.astype(r.dtype)
        if n_a:
            @pl.when(pl.program_id(0) == 0)
            def _():
                for r in orefs[n_o:]:
                    r[...] = jnp.zeros_like(r)
            for r, v in zip(orefs[n_o:], res[n_o:]):
                r[...] += v.astype(r.dtype)

    in_specs = [pl.BlockSpec((tm, a.shape[1]), lambda i: (i, 0)) for a in tiled]
    in_specs += [pl.BlockSpec(a.shape, lambda i: (0, 0)) for a in full]
    out_specs = [pl.BlockSpec((tm, c), lambda i: (i, 0)) for c, _ in outs]
    out_specs += [pl.BlockSpec(s, lambda i: (0, 0)) for s, _ in accs]
    out_shape = [jax.ShapeDtypeStruct((rows, c), d) for c, d in outs]
    out_shape += [jax.ShapeDtypeStruct(s, d) for s, d in accs]
    res = pl.pallas_call(
        body, name=name, grid=(rows // tm,), in_specs=in_specs, out_specs=out_specs, out_shape=out_shape,
        compiler_params=_cparams(("arbitrary",) if n_a else ("parallel",)),
    )(*tiled, *full)
    return res


def _colsum(v):
    return jnp.sum(v, axis=0, keepdims=True)


def _rms_fwd_val(x, g):
    r = lax.rsqrt(jnp.mean(x * x, axis=-1, keepdims=True) + EPS)
    return x * r * g


def _rms_bwd_val(dy, x, g):
    r = lax.rsqrt(jnp.mean(x * x, axis=-1, keepdims=True) + EPS)
    xh = x * r
    dxh = dy * g
    dx = r * (dxh - xh * jnp.mean(dxh * xh, axis=-1, keepdims=True))
    return dx, _colsum(dy * xh)


def _mm(name, pairs, out_shape, out_block, out_map, grid, kaxis, ta=False, tb=False, scale=None, res=None,
        out_dtype=F32):
    nk = grid[kaxis]
    npairs = len(pairs)
    ca, cb = (0 if ta else 1), (1 if tb else 0)
    acc_shape = tuple(d for d in out_block if d is not None)

    def body(*refs):
        in_refs = refs[: 2 * npairs]
        res_ref = refs[2 * npairs] if res is not None else None
        o_ref = refs[2 * npairs + (1 if res is not None else 0)]
        acc_ref = refs[-1]
        k = pl.program_id(kaxis)

        @pl.when(k == 0)
        def _():
            acc_ref[...] = jnp.zeros_like(acc_ref)

        part = None
        for p in range(npairs):
            d = _dot(in_refs[2 * p][...], in_refs[2 * p + 1][...], ca, cb)
            part = d if part is None else part + d
        acc_ref[...] += part

        @pl.when(k == nk - 1)
        def _():
            v = acc_ref[...]
            if scale is not None:
                v = v * scale
            if res_ref is not None:
                v = v + res_ref[...].astype(F32)
            o_ref[...] = v.astype(o_ref.dtype)

    in_specs, args = [], []
    for a, ab, am, b, bb, bm in pairs:
        in_specs += [pl.BlockSpec(ab, am), pl.BlockSpec(bb, bm)]
        args += [a, b]
    if res is not None:
        in_specs.append(pl.BlockSpec(out_block, out_map))
        args.append(res)
    sem = tuple("arbitrary" if i == kaxis else "parallel" for i in range(len(grid)))
    return pl.pallas_call(
        body, name=name, grid=grid, in_specs=in_specs, out_specs=pl.BlockSpec(out_block, out_map),
        out_shape=jax.ShapeDtypeStruct(out_shape, out_dtype), scratch_shapes=[pltpu.VMEM(acc_shape, F32)],
        compiler_params=_cparams(sem),
    )(*args)


def _pick(n, pref):
    for t in pref:
        if n % t == 0:
            return t
    return n


def _mm_nn(name, a, b, tm=512, tn=512, tk=512, **kw):
    (m, k), n = a.shape, b.shape[1]
    tm, tn, tk = _pick(m, (tm, 256, 128)), _pick(n, (tn, 256, 128)), _pick(k, (tk, 256, 128))
    return _mm(name, [(a, (tm, tk), lambda i, j, kk: (i, kk), b, (tk, tn), lambda i, j, kk: (kk, j))],
               (m, n), (tm, tn), lambda i, j, kk: (i, j), (m // tm, n // tn, k // tk), 2, **kw)


def _mm_nt(name, a, b, tm=512, tn=512, tk=512, **kw):
    (m, k), n = a.shape, b.shape[0]
    tm, tn, tk = _pick(m, (tm, 256, 128)), _pick(n, (tn, 256, 128)), _pick(k, (tk, 256, 128))
    return _mm(name, [(a, (tm, tk), lambda i, j, kk: (i, kk), b, (tn, tk), lambda i, j, kk: (j, kk))],
               (m, n), (tm, tn), lambda i, j, kk: (i, j), (m // tm, n // tn, k // tk), 2, tb=True, **kw)


def _mm_tn(name, a, b, tm=512, tn=512, tk=512, **kw):
    (k, m), n = a.shape, b.shape[1]
    tm, tn, tk = _pick(m, (tm, 256, 128)), _pick(n, (tn, 256, 128)), _pick(k, (tk, 256, 128))
    return _mm(name, [(a, (tk, tm), lambda i, j, kk: (kk, i), b, (tk, tn), lambda i, j, kk: (kk, j))],
               (m, n), (tm, tn), lambda i, j, kk: (i, j), (m // tm, n // tn, k // tk), 2, ta=True, **kw)


def _norm_mm(name, h, gamma, w, w_transposed, out_dtype):
    t, d = h.shape
    n = w.shape[0] if w_transposed else w.shape[1]
    tm, tn = _pick(t, (512, 256)), _pick(n, (1024, 512, 256, 128))

    def body(h_ref, gam_ref, w_ref, xn_ref, o_ref, xn_scr):
        @pl.when(pl.program_id(1) == 0)
        def _():
            xn = _rms_fwd_val(h_ref[...], gam_ref[...]).astype(BF16)
            xn_scr[...] = xn
            xn_ref[...] = xn

        o_ref[...] = _dot(xn_scr[...], w_ref[...], 1, 1 if w_transposed else 0).astype(o_ref.dtype)

    wspec = pl.BlockSpec((tn, d), lambda i, j: (j, 0)) if w_transposed else pl.BlockSpec((d, tn), lambda i, j: (0, j))
    return pl.pallas_call(
        body, name=name, grid=(t // tm, n // tn),
        in_specs=[pl.BlockSpec((tm, d), lambda i, j: (i, 0)), pl.BlockSpec((1, d), lambda i, j: (0, 0)), wspec],
        out_specs=[pl.BlockSpec((tm, d), lambda i, j: (i, 0)), pl.BlockSpec((tm, tn), lambda i, j: (i, j))],
        out_shape=[jax.ShapeDtypeStruct((t, d), BF16), jax.ShapeDtypeStruct((t, n), out_dtype)],
        scratch_shapes=[pltpu.VMEM((tm, d), BF16)], compiler_params=_cparams(("parallel", "arbitrary")),
    )(h, gamma, w)


def _ffn_fwd(pfx, h, gamma, wg, wu, wd):
    t, d = h.shape
    nb, f, _ = wg.shape
    tm = _pick(t, (512, 256))

    def body(h_ref, gam_ref, wg_ref, wu_ref, wd_ref, ho_ref, xn_ref, g_ref, u_ref, xn_scr, acc_ref):
        j = pl.program_id(1)

        @pl.when(j == 0)
        def _():
            xn = _rms_fwd_val(h_ref[...], gam_ref[...]).astype(BF16)
            xn_scr[...] = xn
            xn_ref[...] = xn
            acc_ref[...] = jnp.zeros_like(acc_ref)

        x = xn_scr[...]
        g = _dot(x, wg_ref[...], 1, 1)
        u = _dot(x, wu_ref[...], 1, 1)
        g_ref[...] = g.astype(BF16)
        u_ref[...] = u.astype(BF16)
        acc_ref[...] += _dot(g * _sigmoid(g) * u, wd_ref[...], 1, 0)

        @pl.when(j == nb - 1)
        def _():
            ho_ref[...] = h_ref[...] + 0.5 * acc_ref[...]

    row = pl.BlockSpec((tm, d), lambda i, j: (i, 0))
    blk = pl.BlockSpec((None, tm, f), lambda i, j: (j, i, 0))
    wspec = pl.BlockSpec((None, f, d), lambda i, j: (j, 0, 0))
    return pl.pallas_call(
        body, name=pfx + "_fwd", grid=(t // tm, nb),
        in_specs=[row, pl.BlockSpec((1, d), lambda i, j: (0, 0)), wspec, wspec, wspec], out_specs=[row, row, blk, blk],
        out_shape=[jax.ShapeDtypeStruct((t, d), F32), jax.ShapeDtypeStruct((t, d), BF16),
                   jax.ShapeDtypeStruct((nb, t, f), BF16), jax.ShapeDtypeStruct((nb, t, f), BF16)],
        scratch_shapes=[pltpu.VMEM((tm, d), BF16), pltpu.VMEM((tm, d), F32)],
        compiler_params=_cparams(("parallel", "arbitrary")),
    )(h, gamma, wg, wu, wd)


def _ffn_bwd(pfx, dh_out, h, gamma, xn, g_all, u_all, wg, wu, wd):
    t, d = h.shape
    nb, f, _ = wg.shape
    tm = _pick(t, (512, 256))
    tk = _pick(t, (512, 256))

    def body(dy_ref, h_ref, gam_ref, wg_ref, wu_ref, wd_ref, g_ref, u_ref, dh_ref, dgam_ref, dg_ref, du_ref, a_ref, acc_ref):
        i, j = pl.program_id(0), pl.program_id(1)

        @pl.when((i == 0) & (j == 0))
        def _():
            dgam_ref[...] = jnp.zeros_like(dgam_ref)

        @pl.when(j == 0)
        def _():
            acc_ref[...] = jnp.zeros_like(acc_ref)

        da = _dot(dy_ref[...], wd_ref[...], 1, 1) * 0.5
        g = g_ref[...].astype(F32)
        u = u_ref[...].astype(F32)
        s = _sigmoid(g)
        sl = g * s
        du = (da * sl).astype(BF16)
        dg = (da * u * (s * (1.0 + g * (1.0 - s)))).astype(BF16)
        du_ref[...] = du
        dg_ref[...] = dg
        a_ref[...] = (sl * u).astype(BF16)
        acc_ref[...] += _dot(dg, wg_ref[...], 1, 0) + _dot(du, wu_ref[...], 1, 0)

        @pl.when(j == nb - 1)
        def _():
            dx, dgam = _rms_bwd_val(acc_ref[...], h_ref[...], gam_ref[...])
            dh_ref[...] = dy_ref[...] + dx
            dgam_ref[...] += dgam

    row = pl.BlockSpec((tm, d), lambda i, j: (i, 0))
    vec = pl.BlockSpec((1, d), lambda i, j: (0, 0))
    blk = pl.BlockSpec((None, tm, f), lambda i, j: (j, i, 0))
    wspec = pl.BlockSpec((None, f, d), lambda i, j: (j, 0, 0))
    dh, dgamma, dg_all, du_all, a_all = pl.pallas_call(
        body, name=pfx + "_bwd", grid=(t // tm, nb),
        in_specs=[row, row, vec, wspec, wspec, wspec, blk, blk], out_specs=[row, vec, blk, blk, blk],
        out_shape=[jax.ShapeDtypeStruct((t, d), F32), jax.ShapeDtypeStruct((1, d), F32)]
        + [jax.ShapeDtypeStruct((nb, t, f), BF16)] * 3,
        scratch_shapes=[pltpu.VMEM((tm, d), F32)], compiler_params=_cparams(("arbitrary", "arbitrary")),
    )(dh_out, h, gamma, wg, wu, wd, g_all, u_all)

    xmap, bmap, omap = (lambda b, k: (k, 0)), (lambda b, k: (b, k, 0)), (lambda b, k: (b, 0, 0))
    dwg = _mm(pfx + "_dwg", [(dg_all, (None, tk, f), bmap, xn, (tk, d), xmap)], (nb, f, d), (None, f, d), omap,
              (nb, t // tk), 1, ta=True)
    dwu = _mm(pfx + "_dwu", [(du_all, (None, tk, f), bmap, xn, (tk, d), xmap)], (nb, f, d), (None, f, d), omap,
              (nb, t // tk), 1, ta=True)
    dwd = _mm(pfx + "_dwd", [(a_all, (None, tk, f), bmap, dh_out, (tk, d), xmap)], (nb, f, d), (None, f, d), omap,
              (nb, t // tk), 1, ta=True, scale=0.5)
    return dh, dgamma, dwg, dwu, dwd


HALO = 16


def _silu_grad(y):
    s = _sigmoid(y)
    return s * (1.0 + y * (1.0 - s))


def _with_halo(ref, i, n_tiles, tm, before, after):
    t = ref.shape[0]
    r0 = pl.multiple_of(i * tm, tm)
    parts = [ref[pl.ds(r0, tm), :].astype(F32)]
    if before:
        prev = ref[pl.ds(pl.multiple_of(jnp.maximum(r0 - HALO, 0), HALO), HALO), :].astype(F32)
        parts.insert(0, jnp.where(i > 0, prev, 0.0))
    if after:
        nxt = ref[pl.ds(pl.multiple_of(jnp.minimum(r0 + tm, t - HALO), HALO), HALO), :].astype(F32)
        parts.append(jnp.where(i < n_tiles - 1, nxt, 0.0))
    return jnp.concatenate(parts, axis=0)


def _conv_fwd(zbig, w):
    t, c = zbig.shape[0], w.shape[1]
    tm = _pick(t, (512, 256))
    nt = t // tm

    def body(x_ref, w_ref, o_ref):
        xe = _with_halo(x_ref, pl.program_id(0), nt, tm, True, False)
        wv = w_ref[...]
        y = xe * wv[3:4, :]
        for i in range(CONV_W - 1):
            y = y + pltpu.roll(xe, CONV_W - 1 - i, 0) * wv[i:i + 1, :]
        y = y[HALO:, :]
        o_ref[...] = (y * _sigmoid(y)).astype(o_ref.dtype)

    return pl.pallas_call(
        body, name="conv_fwd", grid=(nt,),
        in_specs=[pl.BlockSpec((t, c), lambda i: (0, 0)), pl.BlockSpec(w.shape, lambda i: (0, 0))],
        out_specs=pl.BlockSpec((tm, c), lambda i: (i, 0)), out_shape=jax.ShapeDtypeStruct((t, c), BF16),
        compiler_params=_cparams(("parallel",)),
    )(zbig, w)


def _conv_bwd(zbig, dact, w):
    t, c = dact.shape
    tm = _pick(t, (512, 256))
    nt = t // tm
    n = tm + HALO

    def body(x_ref, d_ref, w_ref, dx_ref, dw_ref):
        xe = _with_halo(x_ref, pl.program_id(0), nt, tm, True, True)
        de = _with_halo(d_ref, pl.program_id(0), nt, tm, False, True)
        wv = w_ref[...]
        sh = [pltpu.roll(xe, CONV_W - 1 - i, 0)[HALO:, :] if i < CONV_W - 1 else xe[HALO:, :] for i in range(CONV_W)]
        y = sh[0] * wv[0:1, :]
        for i in range(1, CONV_W):
            y = y + sh[i] * wv[i:i + 1, :]
        dy = de * _silu_grad(y)
        dx = dy * wv[3:4, :]
        for i in range(CONV_W - 1):
            dx = dx + pltpu.roll(dy, n - (CONV_W - 1 - i), 0) * wv[i:i + 1, :]
        dx_ref[...] = dx[:tm, :].astype(dx_ref.dtype)
        dyc = dy[:tm, :]
        dwp = jnp.concatenate([_colsum(dyc * sh[i][:tm, :]) for i in range(CONV_W)], axis=0)

        @pl.when(pl.program_id(0) == 0)
        def _():
            dw_ref[...] = jnp.zeros_like(dw_ref)
        dw_ref[...] += dwp

    return pl.pallas_call(
        body, name="conv_bwd", grid=(nt,),
        in_specs=[pl.BlockSpec((t, c), lambda i: (0, 0)), pl.BlockSpec((t, c), lambda i: (0, 0)),
                  pl.BlockSpec(w.shape, lambda i: (0, 0))],
        out_specs=[pl.BlockSpec((tm, c), lambda i: (i, 0)), pl.BlockSpec(w.shape, lambda i: (0, 0))],
        out_shape=[jax.ShapeDtypeStruct((t, c), BF16), jax.ShapeDtypeStruct(w.shape, F32)],
        compiler_params=_cparams(("arbitrary",)),
    )(zbig, dact, w)


LM = 256
HI = lax.Precision.HIGHEST


def _logsig(x):
    return jnp.minimum(x, 0.0) - jnp.log(1.0 + jnp.exp(-jnp.abs(x)))


def _tri(n, lower):
    r = lax.broadcasted_iota(jnp.int32, (n, n), 0)
    c = lax.broadcasted_iota(jnp.int32, (n, n), 1)
    return (r >= c) if lower else (r <= c)


def _f32dot(a, b):
    return lax.dot_general(a, b, (((1,), (0,)), ((), ())), precision=HI, preferred_element_type=F32)


def _mlstm_chunk(h, q_ref, k_ref, v_ref, zs_ref, zsr_ref, bc_ref, br_ref, c_prev, m_prev):
    l = LM
    q = q_ref[:, h * DK_M:(h + 1) * DK_M].astype(F32) * (DK_M ** -0.5)
    k = k_ref[:, h * DK_M:(h + 1) * DK_M]
    v = v_ref[:, h * DV_M:(h + 1) * DV_M]
    lane = lax.broadcasted_iota(jnp.int32, (l, DV_M), 1)
    v1 = jnp.concatenate([v, (lane == 0).astype(v.dtype)], axis=1)
    zs, zsr = zs_ref[...], zsr_ref[...]
    li_c = zs[:, h:h + 1] + bc_ref[:, h:h + 1]
    fp_c = zs[:, NH_M + h:NH_M + h + 1] + bc_ref[:, NH_M + h:NH_M + h + 1]
    li_r = zsr[h:h + 1, :] + br_ref[h:h + 1, :]
    fp_r = zsr[NH_M + h:NH_M + h + 1, :] + br_ref[NH_M + h:NH_M + h + 1, :]
    lf_c, lf_r = _logsig(fp_c), _logsig(fp_r)
    low = _tri(l, True)
    b_c = _f32dot(low.astype(F32), lf_c)
    b_r = _f32dot(lf_r, _tri(l, False).astype(F32))
    g = b_r[:, l - 1:l]
    dmat = jnp.where(low, b_c - b_r + li_r, -jnp.inf)
    inter = b_c + m_prev
    m_t = jnp.maximum(inter, jnp.max(dmat, axis=1, keepdims=True))
    w_inter = jnp.exp(inter - m_t)
    amat = jnp.exp(dmat - m_t)
    s = _dot(q, k, 1, 1)
    p = amat * s
    qc = _dot(q, c_prev, 1, 0)
    qc_w = w_inter * qc
    num1 = qc_w + _dot(p, v1, 1, 0)
    den = num1[:, DV_M:DV_M + 1]
    mx = jnp.maximum(jnp.abs(den), jnp.exp(-m_t))
    hh = num1[:, :DV_M] / mx
    a_c = g - b_c + li_c
    return dict(q=q, k=k, v1=v1, fp_c=fp_c, fp_r=fp_r, b_c=b_c, g=g, m_t=m_t, w_inter=w_inter, amat=amat, s=s, p=p,
                qc_w=qc_w, den=den, mx=mx, hh=hh, a_c=a_c)


def _mlstm_fwd(qk, zbig, zs, zsr, bc, br, gm):
    t = zs.shape[0]
    l = LM
    nc = t // l
    dm = NH_M * DV_M

    def body(q_ref, k_ref, v_ref, o_ref, zs_ref, zsr_ref, bc_ref, br_ref, gm_ref, y_ref, cst_ref, mst_ref, c_scr, m_scr):
        @pl.when(pl.program_id(0) == 0)
        def _():
            c_scr[...] = jnp.zeros_like(c_scr)
            m_scr[...] = jnp.zeros_like(m_scr)

        cst_ref[...] = c_scr[...]
        mst_ref[...] = m_scr[...]
        ys = []
        for h in range(NH_M):
            c_prev = c_scr[h]
            m_prev = m_scr[h:h + 1, 0:1]
            r = _mlstm_chunk(h, q_ref, k_ref, v_ref, zs_ref, zsr_ref, bc_ref, br_ref, c_prev, m_prev)
            hh = r["hh"]
            gh = gm_ref[:, h * DV_M:(h + 1) * DV_M]
            hn = hh * lax.rsqrt(jnp.mean(hh * hh, axis=-1, keepdims=True) + EPS) * gh
            og = o_ref[:, h * DV_M:(h + 1) * DV_M].astype(F32)
            ys.append(hn * _sigmoid(og))
            m_new = jnp.maximum(r["g"] + m_prev, jnp.max(r["a_c"], axis=0, keepdims=True))
            decay = jnp.exp(r["g"] + m_prev - m_new)
            wk = r["k"].astype(F32) * jnp.exp(r["a_c"] - m_new)
            c_scr[h] = decay * c_prev + _dot(wk, r["v1"], 0, 0)
            m_scr[h:h + 1, :] = jnp.broadcast_to(m_new, (1, 128))
        y_ref[...] = jnp.concatenate(ys, axis=1).astype(y_ref.dtype)

    return pl.pallas_call(
        body, name="mlstm_fwd", grid=(nc,),
        in_specs=[pl.BlockSpec((l, NH_M * DK_M), lambda i: (i, 0)), pl.BlockSpec((l, NH_M * DK_M), lambda i: (i, 1)),
                  pl.BlockSpec((l, dm), lambda i: (i, 1)), pl.BlockSpec((l, dm), lambda i: (i, 2)),
                  pl.BlockSpec((l, 128), lambda i: (i, 0)), pl.BlockSpec((8, l), lambda i: (0, i)),
                  pl.BlockSpec((1, 8), lambda i: (0, 0)), pl.BlockSpec((8, 1), lambda i: (0, 0)),
                  pl.BlockSpec((1, dm), lambda i: (0, 0))],
        out_specs=[pl.BlockSpec((l, dm), lambda i: (i, 0)), pl.BlockSpec((None, NH_M, DK_M, 2 * DV_M), lambda i: (i, 0, 0, 0)),
                   pl.BlockSpec((None, 8, 128), lambda i: (i, 0, 0))],
        out_shape=[jax.ShapeDtypeStruct((t, dm), BF16), jax.ShapeDtypeStruct((nc, NH_M, DK_M, 2 * DV_M), F32),
                   jax.ShapeDtypeStruct((nc, 8, 128), F32)],
        scratch_shapes=[pltpu.VMEM((NH_M, DK_M, 2 * DV_M), F32), pltpu.VMEM((8, 128), F32)],
        compiler_params=_cparams(("arbitrary",)),
    )(qk, qk, zbig, zbig, zs, zsr, bc, br, gm)


def _mlstm_bwd(qk, zbig, zs, zsr, bc, br, gm, cst, mst, dycat):
    t = zs.shape[0]
    l = LM
    nc = t // l
    dm = NH_M * DV_M

    def body(q_ref, k_ref, v_ref, o_ref, zs_ref, zsr_ref, bc_ref, br_ref, gm_ref, cst_ref, mst_ref, cnx_ref, mnx_ref,
             dy_ref, dqk_ref, dv_ref, do_ref, dzs_ref, dzr_ref, dgm_ref, dc_scr):
        @pl.when(pl.program_id(0) == 0)
        def _():
            dc_scr[...] = jnp.zeros_like(dc_scr)
            dgm_ref[...] = jnp.zeros_like(dgm_ref)

        lane = lax.broadcasted_iota(jnp.int32, (l, 128), 1)
        upper = _tri(l, False).astype(F32)
        lower = _tri(l, True).astype(F32)
        dzr_rows = [None] * 8
        dvs, dos, dgs, dqs, dks = [], [], [], [], []
        dzs = jnp.zeros((l, 128), F32)
        for h in range(NH_M):
            c_prev = cst_ref[h]
            m_prev = mst_ref[h:h + 1, 0:1]
            r = _mlstm_chunk(h, q_ref, k_ref, v_ref, zs_ref, zsr_ref, bc_ref, br_ref, c_prev, m_prev)
            hh, mx, den, m_t, v1, amat = r["hh"], r["mx"], r["den"], r["m_t"], r["v1"], r["amat"]
            gh = gm_ref[:, h * DV_M:(h + 1) * DV_M]
            rs = lax.rsqrt(jnp.mean(hh * hh, axis=-1, keepdims=True) + EPS)
            xh = hh * rs
            sg = _sigmoid(o_ref[:, h * DV_M:(h + 1) * DV_M].astype(F32))
            dyh = dy_ref[:, h * DV_M:(h + 1) * DV_M]
            dos.append(dyh * xh * gh * sg * (1.0 - sg))
            dhn = dyh * sg
            dgs.append(_colsum(dhn * xh))
            dxh = dhn * gh
            dh = rs * (dxh - xh * jnp.mean(dxh * xh, axis=-1, keepdims=True))
            g1 = dh / mx
            hd = jnp.sum(hh * dh, axis=-1, keepdims=True)
            dden = jnp.where(jnp.abs(den) > jnp.exp(-m_t), -hd / mx * jnp.sign(den), 0.0)
            g256 = jnp.concatenate([g1, jnp.where(lane == 0, dden, 0.0)], axis=1)
            dc_h = dc_scr[h]
            ea = jnp.exp(r["a_c"])
            dp = _dot(g256, v1, 1, 1)
            ds = dp * amat
            dqs.append((r["w_inter"] * _dot(g256, c_prev, 1, 1) + _dot(ds, r["k"], 1, 0)) * (DK_M ** -0.5))
            dks.append(_dot(ds, r["q"], 0, 0) + ea * _dot(v1, dc_h, 1, 1))
            dv_st = ea * _dot(r["k"], dc_h, 1, 0)
            dv1 = _dot(r["p"], g256, 0, 0) + dv_st
            dvs.append(dv1[:, :DV_M])
            wmat = dp * r["p"]
            c_in = _colsum(wmat)
            c_st = jnp.sum(v1.astype(F32) * dv_st, axis=-1, keepdims=True)
            r_t = jnp.sum(wmat, axis=1, keepdims=True) + jnp.sum(g256 * r["qc_w"], axis=-1, keepdims=True)
            db = r_t - c_st
            carry = jnp.exp(mnx_ref[h:h + 1, 0:1]) * jnp.sum(
                jnp.sum(dc_h * cnx_ref[h], axis=1, keepdims=True), axis=0, keepdims=True)
            dlf_c = _f32dot(upper, db) + carry
            dlf_r = -_f32dot(c_in, lower)
            dfp = dlf_c * _sigmoid(-r["fp_c"])
            dzs = dzs + jnp.where(lane == h, c_st, 0.0) + jnp.where(lane == NH_M + h, dfp, 0.0)
            dzr_rows[h] = c_in
            dzr_rows[NH_M + h] = dlf_r * _sigmoid(-r["fp_r"])
            wq = r["q"] * jnp.exp(r["b_c"] - m_t)
            dc_scr[h] = jnp.exp(r["g"]) * dc_h + _dot(wq, g256, 0, 0)
        dqk_ref[...] = jnp.concatenate(dqs + dks, axis=1)
        dv_ref[...] = jnp.concatenate(dvs, axis=1).astype(dv_ref.dtype)
        do_ref[...] = jnp.concatenate(dos, axis=1).astype(do_ref.dtype)
        dzs_ref[...] = dzs
        dzr_ref[...] = jnp.concatenate(dzr_rows, axis=0)
        dgm_ref[...] += jnp.concatenate(dgs, axis=1)

    rev = lambda i: nc - 1 - i
    nxt = lambda i: jnp.minimum(nc - i, nc - 1)
    return pl.pallas_call(
        body, name="mlstm_bwd", grid=(nc,),
        in_specs=[pl.BlockSpec((l, NH_M * DK_M), lambda i: (rev(i), 0)), pl.BlockSpec((l, NH_M * DK_M), lambda i: (rev(i), 1)),
                  pl.BlockSpec((l, dm), lambda i: (rev(i), 1)), pl.BlockSpec((l, dm), lambda i: (rev(i), 2)),
                  pl.BlockSpec((l, 128), lambda i: (rev(i), 0)), pl.BlockSpec((8, l), lambda i: (0, rev(i))),
                  pl.BlockSpec((1, 8), lambda i: (0, 0)), pl.BlockSpec((8, 1), lambda i: (0, 0)),
                  pl.BlockSpec((1, dm), lambda i: (0, 0)),
                  pl.BlockSpec((None, NH_M, DK_M, 2 * DV_M), lambda i: (rev(i), 0, 0, 0)),
                  pl.BlockSpec((None, 8, 128), lambda i: (rev(i), 0, 0)),
                  pl.BlockSpec((None, NH_M, DK_M, 2 * DV_M), lambda i: (nxt(i), 0, 0, 0)),
                  pl.BlockSpec((None, 8, 128), lambda i: (nxt(i), 0, 0)),
                  pl.BlockSpec((l, dm), lambda i: (rev(i), 0))],
        out_specs=[pl.BlockSpec((l, dm), lambda i: (rev(i), 0)),
                   pl.BlockSpec((l, dm), lambda i: (rev(i), 0)), pl.BlockSpec((l, dm), lambda i: (rev(i), 0)),
                   pl.BlockSpec((l, 128), lambda i: (rev(i), 0)), pl.BlockSpec((8, l), lambda i: (0, rev(i))),
                   pl.BlockSpec((1, dm), lambda i: (0, 0))],
        out_shape=[jax.ShapeDtypeStruct((t, dm), F32),
                   jax.ShapeDtypeStruct((t, dm), BF16), jax.ShapeDtypeStruct((t, dm), BF16),
                   jax.ShapeDtypeStruct((t, 128), F32), jax.ShapeDtypeStruct((8, t), F32),
                   jax.ShapeDtypeStruct((1, dm), F32)],
        scratch_shapes=[pltpu.VMEM((NH_M, DK_M, 2 * DV_M), F32)],
        compiler_params=_cparams(("arbitrary",)),
    )(qk, qk, zbig, zbig, zs, zsr, bc, br, gm, cst, mst, cst, mst, dycat)


def _fox_cumsum(zsr, bf_r):
    t = zsr.shape[1]
    cw = _pick(t, (512, 256))

    def body(z_ref, b_ref, c_ref):
        up = _tri(cw, False).astype(F32)
        carry = jnp.zeros((NH_F, 1), F32)
        for j in range(t // cw):
            cs = _f32dot(_logsig(z_ref[:, j * cw:(j + 1) * cw] + b_ref[...]), up) + carry
            c_ref[:, j * cw:(j + 1) * cw] = cs
            carry = cs[:, cw - 1:cw]

    return pl.pallas_call(
        body, name="fox_cumsum", grid=(1,),
        in_specs=[pl.BlockSpec((NH_F, t), lambda i: (1, 0)), pl.BlockSpec((NH_F, 1), lambda i: (0, 0))],
        out_specs=pl.BlockSpec((NH_F, t), lambda i: (0, 0)), out_shape=jax.ShapeDtypeStruct((NH_F, t), F32),
        compiler_params=_cparams(("arbitrary",)),
    )(zsr, bf_r)


def _fox_gate_bwd(zsr, bf_r, dc):
    t = zsr.shape[1]
    cw = _pick(t, (512, 256))

    def body(z_ref, b_ref, dc_ref, o_ref):
        low = _tri(cw, True).astype(F32)
        carry = jnp.zeros((NH_F, 1), F32)
        for j in reversed(range(t // cw)):
            sl = slice(j * cw, (j + 1) * cw)
            dlf = _f32dot(dc_ref[:, sl], low) + carry
            o_ref[:, sl] = dlf * _sigmoid(-(z_ref[:, sl] + b_ref[...]))
            carry = dlf[:, 0:1]

    return pl.pallas_call(
        body, name="fox_gate_bwd", grid=(1,),
        in_specs=[pl.BlockSpec((NH_F, t), lambda i: (1, 0)), pl.BlockSpec((NH_F, 1), lambda i: (0, 0)),
                  pl.BlockSpec((NH_F, t), lambda i: (0, 0))],
        out_specs=pl.BlockSpec((NH_F, t), lambda i: (0, 0)), out_shape=jax.ShapeDtypeStruct((NH_F, t), F32),
        compiler_params=_cparams(("arbitrary",)),
    )(zsr, bf_r, dc)


def _causal_mask(n):
    return _tri(n, True)


def _fox_fwd(q, k, v, c_col, c_row, gf):
    nh, t, dh = q.shape
    tq = _pick(t, (512, 256))
    scale = dh ** -0.5

    def body(q_ref, k_ref, v_ref, cc_ref, cr_ref, g_ref, o_ref, lse_ref, y_ref):
        i = pl.program_id(1)
        qv = q_ref[...]
        cq = cc_ref[...]

        def blk(j, carry, masked):
            m, l, acc = carry
            k0 = pl.multiple_of(j * tq, tq)
            kb = k_ref[pl.ds(k0, tq), :]
            vb = v_ref[pl.ds(k0, tq), :]
            s = _dot(qv, kb, 1, 1) * scale + cq - cr_ref[:, pl.ds(k0, tq)]
            if masked:
                s = jnp.where(_causal_mask(tq), s, -jnp.inf)
            m_new = jnp.maximum(m, jnp.max(s, axis=1, keepdims=True))
            alpha = jnp.exp(m - m_new)
            p = jnp.exp(s - m_new)
            return m_new, alpha * l + jnp.sum(p, axis=1, keepdims=True), alpha * acc + _dot(p, vb, 1, 0)

        init = (jnp.full((tq, 1), -jnp.inf, F32), jnp.zeros((tq, 1), F32), jnp.zeros((tq, dh), F32))
        carry = lax.fori_loop(0, i, lambda j, c: blk(j, c, False), init)
        m, l, acc = blk(i, carry, True)
        o = acc / l
        o_ref[...] = o
        lse_ref[...] = m + jnp.log(l)
        y_ref[...] = (o * lax.rsqrt(jnp.mean(o * o, axis=-1, keepdims=True) + EPS) * g_ref[...]).astype(y_ref.dtype)

    full = lambda w: pl.BlockSpec((None, t, w), lambda h, i: (h, 0, 0))
    tile = lambda w: pl.BlockSpec((None, tq, w), lambda h, i: (h, i, 0))
    return pl.pallas_call(
        body, name="fox_fwd", grid=(nh, t // tq),
        in_specs=[tile(dh), full(dh), full(dh), tile(1), pl.BlockSpec((None, 1, t), lambda h, i: (h, 0, 0)),
                  pl.BlockSpec((None, 1, dh), lambda h, i: (h, 0, 0))],
        out_specs=[tile(dh), tile(1), tile(dh)],
        out_shape=[jax.ShapeDtypeStruct((nh, t, dh), F32), jax.ShapeDtypeStruct((nh, t, 1), F32),
                   jax.ShapeDtypeStruct((nh, t, dh), BF16)],
        compiler_params=_cparams(("parallel", "parallel")),
    )(q, k, v, c_col, c_row, gf)


def _fox_norm_bwd(dy, o, gf):
    nh, t, dh = o.shape
    tm = _pick(t, (512, 256))

    def body(dy_ref, o_ref, g_ref, do_ref, dl_ref, dg_ref):
        ov = o_ref[...]
        dx, dg = _rms_bwd_val(dy_ref[...], ov, g_ref[...])
        do_ref[...] = dx
        dl_ref[...] = jnp.sum(dx * ov, axis=-1, keepdims=True)

        @pl.when(pl.program_id(1) == 0)
        def _():
            dg_ref[...] = jnp.zeros_like(dg_ref)
        dg_ref[...] += dg

    tile = lambda w: pl.BlockSpec((None, tm, w), lambda h, i: (h, i, 0))
    gspec = pl.BlockSpec((None, 1, dh), lambda h, i: (h, 0, 0))
    return pl.pallas_call(
        body, name="fox_norm_bwd", grid=(nh, t // tm), in_specs=[tile(dh), tile(dh), gspec],
        out_specs=[tile(dh), tile(1), gspec],
        out_shape=[jax.ShapeDtypeStruct((nh, t, dh), F32), jax.ShapeDtypeStruct((nh, t, 1), F32),
                   jax.ShapeDtypeStruct((nh, 1, dh), F32)],
        compiler_params=_cparams(("parallel", "arbitrary")),
    )(dy, o, gf)


def _fox_bwd(q, k, v, c_col, c_row, do, lse, delta):
    nh, t, dh = q.shape
    tq = _pick(t, (512, 256))
    nq = t // tq
    scale = dh ** -0.5

    def body(q_ref, k_ref, v_ref, cc_ref, cr_ref, do_ref, lse_ref, dl_ref, dq_ref, dk_ref, dv_ref, dc_ref, dcq_ref):
        j = pl.program_id(1)

        @pl.when(j == 0)
        def _():
            dq_ref[...] = jnp.zeros_like(dq_ref)
            dcq_ref[...] = jnp.zeros_like(dcq_ref)

        kb, vb, crb = k_ref[...], v_ref[...], cr_ref[...]

        def blk(i, carry, masked):
            dk, dv, dc = carry
            rows = pl.ds(pl.multiple_of(i * tq, tq), tq)
            qb = q_ref[rows, :]
            dob = do_ref[rows, :].astype(BF16)
            s = _dot(qb, kb, 1, 1) * scale + cc_ref[rows, :] - crb
            if masked:
                s = jnp.where(_causal_mask(tq), s, -jnp.inf)
            p = jnp.exp(s - lse_ref[rows, :])
            dv = dv + _dot(p, dob, 0, 0)
            ds = p * (_dot(dob, vb, 1, 1) - dl_ref[rows, :])
            dc = dc + _colsum(ds)
            dk = dk + _dot(ds, qb, 0, 0) * scale
            dq_ref[rows, :] += _dot(ds, kb, 1, 0) * scale
            dcq_ref[rows, :] += jnp.sum(ds, axis=1, keepdims=True)
            return dk, dv, dc

        init = (jnp.zeros((tq, dh), F32), jnp.zeros((tq, dh), F32), jnp.zeros((1, tq), F32))
        carry = blk(j, init, True)
        dk, dv, dc = lax.fori_loop(j + 1, nq, lambda i, c: blk(i, c, False), carry)
        dk_ref[...] = dk
        dv_ref[...] = dv
        dc_ref[...] = -dc

    full = lambda w: pl.BlockSpec((None, t, w), lambda h, j: (h, 0, 0))
    tile = lambda w: pl.BlockSpec((None, tq, w), lambda h, j: (h, j, 0))
    crow = pl.BlockSpec((None, 1, tq), lambda h, j: (h, 0, j))
    return pl.pallas_call(
        body, name="fox_bwd", grid=(nh, nq),
        in_specs=[full(dh), tile(dh), tile(dh), full(1), crow, full(dh), full(1), full(1)],
        out_specs=[full(dh), tile(dh), tile(dh), crow, full(1)],
        out_shape=[jax.ShapeDtypeStruct((nh, t, dh), F32)] * 3 + [jax.ShapeDtypeStruct((nh, 1, t), F32),
                                                                jax.ShapeDtypeStruct((nh, t, 1), F32)],
        compiler_params=_cparams(("parallel", "arbitrary")),
    )(q, k, v, c_col, c_row, do, lse, delta)


AUG = 64


def _split3(c):
    hi = c.astype(BF16).astype(F32)
    r1 = c - hi
    mid = r1.astype(BF16).astype(F32)
    return hi, mid, r1 - mid


def _fox_prep(zbig, ct):
    t = zbig.shape[0]
    tm = _pick(t, (512, 256))

    def body(q_ref, k_ref, v_ref, c_ref, qo_ref, ko_ref, vo_ref):
        lane = lax.broadcasted_iota(jnp.int32, (tm, AUG), 1)
        qv, kv, vv, cv = q_ref[...], k_ref[...], v_ref[...], c_ref[...]
        one = (lane == 0).astype(BF16)
        for h in range(NH_F):
            hi, mid, lo = _split3(cv[:, h:h + 1])
            aq = jnp.where(lane == 0, hi, jnp.where(lane == 1, mid, jnp.where(lane == 2, lo, jnp.where(lane < 6, 1.0, 0.0))))
            ak = jnp.where(lane < 3, 1.0, jnp.where(lane == 3, -hi, jnp.where(lane == 4, -mid, jnp.where(lane == 5, -lo, 0.0))))
            sl = slice(h * DH_F, (h + 1) * DH_F)
            qo_ref[h] = jnp.concatenate([qv[:, sl] * (DH_F ** -0.5), aq.astype(BF16)], axis=1).astype(BF16)
            ko_ref[h] = jnp.concatenate([kv[:, sl], ak.astype(BF16)], axis=1)
            vo_ref[h] = jnp.concatenate([vv[:, sl], one], axis=1)

    ospec = pl.BlockSpec((NH_F, tm, 128), lambda i: (0, i, 0))
    return pl.pallas_call(
        body, name="fox_prep", grid=(t // tm,),
        in_specs=[pl.BlockSpec((tm, 512), lambda i: (i, 3)), pl.BlockSpec((tm, 512), lambda i: (i, 4)),
                  pl.BlockSpec((tm, 512), lambda i: (i, 5)), pl.BlockSpec((tm, NH_F), lambda i: (i, 0))],
        out_specs=[ospec] * 3, out_shape=[jax.ShapeDtypeStruct((NH_F, t, 128), BF16)] * 3,
        compiler_params=_cparams(("parallel",)),
    )(zbig, zbig, zbig, ct)


def _fox_fwd2(qa, ka, va, gf):
    nh, t, _ = qa.shape
    tq = _pick(t, (512, 256))

    def body(q_ref, k_ref, v_ref, g_ref, y_ref, o_ref, lse_ref):
        i = pl.program_id(0)
        lane = lax.broadcasted_iota(jnp.int32, (tq, 128), 1)
        ys, os_ = [], []
        lse_all = jnp.zeros((tq, 128), F32)
        for h in range(nh):
            qv = q_ref[h]

            def blk(j, carry, masked, h=h, qv=qv):
                m, acc = carry
                k0 = pl.multiple_of(j * tq, tq)
                s = lax.dot_general(qv, k_ref[h, pl.ds(k0, tq), :], (((1,), (1,)), ((), ())), preferred_element_type=F32)
                if masked:
                    s = jnp.where(_causal_mask(tq), s, -jnp.inf)
                m_new = jnp.maximum(m, jnp.max(s, axis=1, keepdims=True))
                p = jnp.exp(s - m_new).astype(BF16)
                pv = lax.dot_general(p, v_ref[h, pl.ds(k0, tq), :], (((1,), (0,)), ((), ())), preferred_element_type=F32)
                return m_new, jnp.exp(m - m_new) * acc + pv

            init = (jnp.full((tq, 1), -jnp.inf, F32), jnp.zeros((tq, 128), F32))
            carry = lax.fori_loop(0, i, lambda j, c: blk(j, c, False), init)
            m, acc = blk(i, carry, True)
            l = acc[:, DH_F:DH_F + 1]
            o = acc[:, :DH_F] / l
            os_.append(o)
            gh = g_ref[:, h * DH_F:(h + 1) * DH_F]
            ys.append(o * lax.rsqrt(jnp.mean(o * o, axis=-1, keepdims=True) + EPS) * gh)
            lse_all = lse_all + jnp.where(lane == h, m + jnp.log(l), 0.0)
        y_ref[...] = jnp.concatenate(ys, axis=1).astype(y_ref.dtype)
        o_ref[...] = jnp.concatenate(os_, axis=1)
        lse_ref[...] = lse_all

    full = pl.BlockSpec((nh, t, 128), lambda i: (0, 0, 0))
    return pl.pallas_call(
        body, name="fox_fwd", grid=(t // tq,),
        in_specs=[pl.BlockSpec((nh, tq, 128), lambda i: (0, i, 0)), full, full, pl.BlockSpec((1, nh * DH_F), lambda i: (0, 0))],
        out_specs=[pl.BlockSpec((tq, nh * DH_F), lambda i: (i, 0)), pl.BlockSpec((tq, nh * DH_F), lambda i: (i, 0)),
                   pl.BlockSpec((tq, 128), lambda i: (i, 0))],
        out_shape=[jax.ShapeDtypeStruct((t, nh * DH_F), BF16), jax.ShapeDtypeStruct((t, nh * DH_F), F32),
                   jax.ShapeDtypeStruct((t, 128), F32)],
        compiler_params=_cparams(("parallel",)),
    )(qa, ka, va, gf)


def _fox_bwd_prep(dycat, o, gf):
    t = o.shape[0]
    tm = _pick(t, (512, 256))

    def body(dy_ref, o_ref, g_ref, do_ref, dl_ref, dg_ref):
        lane = lax.broadcasted_iota(jnp.int32, (tm, 128), 1)
        dyv, ov, gv = dy_ref[...], o_ref[...], g_ref[...]
        dgs = []
        dl = jnp.zeros((tm, 128), F32)
        pad = jnp.zeros((tm, AUG), BF16)
        for h in range(NH_F):
            sl = slice(h * DH_F, (h + 1) * DH_F)
            dx, dg = _rms_bwd_val(dyv[:, sl], ov[:, sl], gv[:, sl])
            dgs.append(dg)
            do_ref[h] = jnp.concatenate([dx.astype(BF16), pad], axis=1)
            dl = dl + jnp.where(lane == h, jnp.sum(dx * ov[:, sl], axis=-1, keepdims=True), 0.0)
        dl_ref[...] = dl

        @pl.when(pl.program_id(0) == 0)
        def _():
            dg_ref[...] = jnp.zeros_like(dg_ref)
        dg_ref[...] += jnp.concatenate(dgs, axis=1)

    return pl.pallas_call(
        body, name="fox_bwd_prep", grid=(t // tm,),
        in_specs=[pl.BlockSpec((tm, 512), lambda i: (i, 1)), pl.BlockSpec((tm, 512), lambda i: (i, 0)),
                  pl.BlockSpec((1, 512), lambda i: (0, 0))],
        out_specs=[pl.BlockSpec((NH_F, tm, 128), lambda i: (0, i, 0)), pl.BlockSpec((tm, 128), lambda i: (i, 0)),
                   pl.BlockSpec((1, 512), lambda i: (0, 0))],
        out_shape=[jax.ShapeDtypeStruct((NH_F, t, 128), BF16), jax.ShapeDtypeStruct((t, 128), F32),
                   jax.ShapeDtypeStruct((1, 512), F32)],
        compiler_params=_cparams(("arbitrary",)),
    )(dycat, o, gf)


def _fox_bwd2(qa, ka, va, doa, lse, delta):
    nh, t, _ = qa.shape
    tq = _pick(t, (512, 256))
    nq = t // tq

    def body(q_ref, k_ref, v_ref, do_ref, lse_ref, dl_ref, dq_ref, dk_ref, dv_ref):
        h, j = pl.program_id(0), pl.program_id(1)

        @pl.when(j == 0)
        def _():
            dq_ref[...] = jnp.zeros_like(dq_ref)

        kb, vb = k_ref[...], v_ref[...]
        lane = lax.broadcasted_iota(jnp.int32, (tq, 128), 1)

        def blk(i, carry, masked):
            dk, dv = carry
            rows = pl.ds(pl.multiple_of(i * tq, tq), tq)
            qb, dob = q_ref[rows, :], do_ref[rows, :]
            lse_h = jnp.sum(jnp.where(lane == h, lse_ref[rows, :], 0.0), axis=1, keepdims=True)
            dl_h = jnp.sum(jnp.where(lane == h, dl_ref[rows, :], 0.0), axis=1, keepdims=True)
            s = lax.dot_general(qb, kb, (((1,), (1,)), ((), ())), preferred_element_type=F32)
            if masked:
                s = jnp.where(_causal_mask(tq), s, -jnp.inf)
            p = jnp.exp(s - lse_h)
            dp = lax.dot_general(dob, vb, (((1,), (1,)), ((), ())), preferred_element_type=F32)
            ds = (p * (dp - dl_h)).astype(BF16)
            dv = dv + lax.dot_general(p.astype(BF16), dob, (((0,), (0,)), ((), ())), preferred_element_type=F32)
            dk = dk + lax.dot_general(ds, qb, (((0,), (0,)), ((), ())), preferred_element_type=F32)
            dq_ref[rows, :] += lax.dot_general(ds, kb, (((1,), (0,)), ((), ())), preferred_element_type=F32)
            return dk, dv

        init = (jnp.zeros((tq, 128), F32), jnp.zeros((tq, 128), F32))
        carry = blk(j, init, True)
        dk, dv = lax.fori_loop(j + 1, nq, lambda i, c: blk(i, c, False), carry)
        dk_ref[...] = dk
        dv_ref[...] = dv

    full = pl.BlockSpec((None, t, 128), lambda h, j: (h, 0, 0))
    tile = pl.BlockSpec((None, tq, 128), lambda h, j: (h, j, 0))
    cols = pl.BlockSpec((t, 128), lambda h, j: (0, 0))
    return pl.pallas_call(
        body, name="fox_bwd", grid=(nh, nq), in_specs=[full, tile, tile, full, cols, cols], out_specs=[full, tile, tile],
        out_shape=[jax.ShapeDtypeStruct((nh, t, 128), F32)] * 3, compiler_params=_cparams(("parallel", "arbitrary")),
    )(qa, ka, va, doa, lse, delta)


def _fox_bwd_post(dqa, dka, dva):
    nh, t, _ = dqa.shape
    tm = _pick(t, (512, 256))

    def body(dq_ref, dk_ref, dv_ref, oq_ref, ok_ref, ov_ref, dc_ref):
        lane = lax.broadcasted_iota(jnp.int32, (tm, 128), 1)
        dc = jnp.zeros((tm, 128), F32)
        qs, ks, vs = [], [], []
        for h in range(nh):
            dq, dk = dq_ref[h], dk_ref[h]
            qs.append(dq[:, :DH_F] * (DH_F ** -0.5))
            ks.append(dk[:, :DH_F])
            vs.append(dv_ref[h][:, :DH_F])
            dc = dc + jnp.where(lane == h, dq[:, DH_F:DH_F + 1] - dk[:, DH_F + 3:DH_F + 4], 0.0)
        oq_ref[...] = jnp.concatenate(qs, axis=1).astype(BF16)
        ok_ref[...] = jnp.concatenate(ks, axis=1).astype(BF16)
        ov_ref[...] = jnp.concatenate(vs, axis=1).astype(BF16)
        dc_ref[...] = dc

    ispec = pl.BlockSpec((nh, tm, 128), lambda i: (0, i, 0))
    ospec = pl.BlockSpec((tm, nh * DH_F), lambda i: (i, 0))
    return pl.pallas_call(
        body, name="fox_bwd_post", grid=(t // tm,), in_specs=[ispec] * 3,
        out_specs=[ospec] * 3 + [pl.BlockSpec((tm, 128), lambda i: (i, 0))],
        out_shape=[jax.ShapeDtypeStruct((t, nh * DH_F), BF16)] * 3 + [jax.ShapeDtypeStruct((t, 128), F32)],
        compiler_params=_cparams(("parallel",)),
    )(dqa, dka, dva)


W_BIG = 6 * 512
IN_OFF = (0, 512, 1024, 1544, 2056, 2568)
IN_GATES = (1536, 3080)


def _heads(a, nh):
    t = a.shape[0]
    return a.reshape(t, nh, -1).transpose(1, 0, 2)


def _unheads(a):
    nh, t, dh = a.shape
    return a.transpose(1, 0, 2).reshape(t, nh * dh)


def _local_step(x, p, tgt, sp, wg1, wu1, wd1, w_in, conv_w, w_out, wg2, wu2, wd2, w_pg, w_pp):
    t, d = x.shape
    w_big = jnp.concatenate([w_in[o:o + 512] for o in IN_OFF], axis=0)
    w_small = jnp.concatenate([w_in[IN_GATES[0]:IN_GATES[0] + 8], w_in[IN_GATES[1]:IN_GATES[1] + 8],
                               jnp.zeros((112, d), w_in.dtype)], axis=0)
    h1, xn1, g1, u1 = _ffn_fwd("ffn1", x, sp["ffn1_norm"], wg1, wu1, wd1)
    u, zbig = _norm_mm("in_big", h1, sp["mix_norm"], w_big, True, BF16)
    zs = _mm_nt("in_small", u, w_small, tm=1024, tk=1024)
    zsr = zs.T
    qk_act = _conv_fwd(zbig, conv_w)
    bm_c, bf_c = sp["b_mlstm_gates"], sp["b_fox_f"]
    y_m, cst, mst = _mlstm_fwd(qk_act, zbig, zs, zsr, bm_c, bm_c.T, sp["mlstm_out_norm"])
    c = _fox_cumsum(zsr, bf_c.T)
    qa, ka, va = _fox_prep(zbig, c.T)
    y_ft, o_f, lse = _fox_fwd2(qa, ka, va, sp["fox_out_norm"])
    tm = _pick(t, (1024, 512, 256))
    h2 = _mm("out_proj", [(y_m, (tm, 512), lambda i, j, k: (i, 0), w_out, (512, d), lambda i, j, k: (0, 0)),
                          (y_ft, (tm, 512), lambda i, j, k: (i, 0), w_out, (512, d), lambda i, j, k: (1, 0))],
             (t, d), (tm, d), lambda i, j, k: (i, 0), (t // tm, 1, 1), 2, res=h1)
    h3, xn2, g2, u2 = _ffn_fwd("ffn2", h2, sp["ffn2_norm"], wg2, wu2, wd2)
    hn3, gate_pre = _norm_mm("ple_gate", h3, sp["ple_gate_norm"], w_pg, False, F32)
    pp = _mm_nn("ple_proj", p, w_pp, tm=1024)

    def head_fn(h3_t, gp_t, pp_t, tgt_t, g_pp, g_fin):
        gate = _sigmoid(gp_t)
        ppn = _rms_fwd_val(pp_t, g_pp)
        h4 = h3_t + gate * ppn
        err = _rms_fwd_val(h4, g_fin) - tgt_t
        loss = 0.5 * jnp.sum(jnp.mean(err * err, axis=-1, keepdims=True), axis=0, keepdims=True)
        dh4, dg_fin = _rms_bwd_val(err * (1.0 / d), h4, g_fin)
        dpp, dg_pp = _rms_bwd_val(dh4 * gate, pp_t, g_pp)
        dgp = dh4 * ppn * gate * (1.0 - gate)
        return dh4, dgp, dpp, jnp.broadcast_to(loss, (1, 128)), dg_fin, dg_pp

    dh4, dgp, dpp, loss_part, dg_fin, dg_pp = _rowwise(
        "loss_head", head_fn, [h3, gate_pre, pp, tgt], [sp["ple_proj_norm"], sp["final_norm"]],
        [(d, F32), (d, BF16), (d, BF16)], [((1, 128), F32), ((1, d), F32), ((1, d), F32)])
    gw, gs = {}, {"final_norm": dg_fin, "ple_proj_norm": dg_pp}
    gw["w_ple_gate"] = _mm_tn("d_w_pg", hn3, dgp, tm=1024, tn=1024)
    gw["w_ple_proj"] = _mm_tn("d_w_pp", p, dpp, tn=1024)
    dhn3 = _mm_nt("d_hn3", dgp, w_pg, tm=1024, tn=1024, tk=1024)

    def res_norm_bwd(dn_t, h_t, dres_t, g):
        dx, dg = _rms_bwd_val(dn_t, h_t, g)
        return dres_t + dx, dg

    dh3, gs["ple_gate_norm"] = _rowwise("ple_norm_bwd", res_norm_bwd, [dhn3, h3, dh4], [sp["ple_gate_norm"]],
                                        [(d, F32)], [((1, d), F32)])
    dh2, gs["ffn2_norm"], gw["ffn2_w_gate"], gw["ffn2_w_up"], gw["ffn2_w_down"] = _ffn_bwd(
        "ffn2", dh3, h2, sp["ffn2_norm"], xn2, g2, u2, wg2, wu2, wd2)
    dycat = _mm_nt("d_ycat", dh2, w_out, tm=1024, tn=1024, tk=1024)
    gw["w_out"] = jnp.concatenate([_mm_tn("d_w_out_m", y_m, dh2, tn=1024), _mm_tn("d_w_out_f", y_ft, dh2, tn=1024)], axis=0)
    doa, delta, gs["fox_out_norm"] = _fox_bwd_prep(dycat, o_f, sp["fox_out_norm"])
    dq_f, dk_f, dv_f, dct = _fox_bwd_post(*_fox_bwd2(qa, ka, va, doa, lse, delta))
    dfp = _fox_gate_bwd(zsr, bf_c.T, dct[:, :NH_F].T)
    dact, dv_m, do_m, dzs_m, dzr_m, gs["mlstm_out_norm"] = _mlstm_bwd(
        qk_act, zbig, zs, zsr, bm_c, bm_c.T, sp["mlstm_out_norm"], cst, mst, dycat)
    dqk, gw["conv_qk"] = _conv_bwd(zbig, dact, conv_w)
    dz_big = jnp.concatenate([dqk, dv_m, do_m, dq_f, dk_f, dv_f], axis=1)
    dzs = dzs_m + jnp.pad(jnp.concatenate([dzr_m, dfp], axis=0).T, ((0, 0), (0, 112)))
    dw_big = _mm_tn("d_w_big", dz_big, u, tn=1024)
    dw_small = _mm_tn("d_w_small", dzs, u, tn=1024)
    gw["w_in"] = jnp.concatenate([dw_big[0:1536], dw_small[0:8], dw_big[1536:3072], dw_small[8:16]], axis=0)
    du_a = _mm_nn("d_u_big", dz_big, w_big, tm=1024, tn=1024, tk=1024)
    du_b = _mm_nn("d_u_small", dzs, w_small, tm=1024, tn=1024)

    def mix_norm_bwd(da_t, db_t, h_t, dres_t, dzs_t, g):
        dx, dg = _rms_bwd_val(da_t + db_t, h_t, g)
        return dres_t + dx, dg, _colsum(dzs_t)

    dh1, gs["mix_norm"], dbias = _rowwise("mix_norm_bwd", mix_norm_bwd, [du_a, du_b, h1, dh2, dzs], [sp["mix_norm"]],
                                          [(d, F32)], [((1, d), F32), ((1, 128), F32)])
    gs["b_mlstm_gates"], gs["b_fox_f"] = dbias[:, 0:8], dbias[:, 8:16]
    grad_x, gs["ffn1_norm"], gw["ffn1_w_gate"], gw["ffn1_w_up"], gw["ffn1_w_down"] = _ffn_bwd(
        "ffn1", dh1, x, sp["ffn1_norm"], xn1, g1, u1, wg1, wu1, wd1)
    return loss_part, grad_x, gw, gs


ANY = pl.BlockSpec(memory_space=pl.ANY)
MESH = pl.DeviceIdType.MESH


def _place():
    x, y, c = lax.axis_index("x"), lax.axis_index("y"), lax.axis_index("c")
    chips = [(1 - x, y), (x, 1 - y), (1 - x, 1 - y)]
    return x, y, c, 2 * x + y, (x, y, 1 - c), chips


def _rcopy(src, dst, ssem, rsem, dev):
    return pltpu.make_async_remote_copy(src_ref=src, dst_ref=dst, send_sem=ssem, recv_sem=rsem, device_id=dev,
                                        device_id_type=MESH)


def _half(ref, lead, axis, idx, half):
    return ref.at[(slice(None),) * (lead + axis) + (pl.ds(idx * half, half),)]


def _gather4(name, arrs, split):
    n = len(arrs)

    def body(*refs):
        ins, outs = refs[:n], refs[n:2 * n]
        lsem, isend, irecv, dsend, drecv = refs[2 * n:]
        x, y, c, me, sib, chips = _place()

        def part(ref, a):
            if split[a] is None:
                return ref
            return _half(ref, 0, split[a], c, arrs[a].shape[split[a]] // 2)

        def other(ref, a):
            return _half(ref, 0, split[a], 1 - c, arrs[a].shape[split[a]] // 2)

        local = [pltpu.make_async_copy(ins[a], outs[a].at[me], lsem.at[a]) for a in range(n)]
        for cp in local:
            cp.start()
        sends = []
        for a in range(n):
            for j, chip in enumerate(chips):
                cp = _rcopy(part(ins[a], a), part(outs[a].at[me], a), isend.at[3 * a + j], irecv.at[3 * a + j], (*chip, c))
                cp.start()
                sends.append(cp)
        for j, (px, py) in enumerate(chips):
            src_chip = 2 * px + py
            for a in range(n):
                blk = part(outs[a].at[src_chip], a)
                _rcopy(blk, blk, isend.at[3 * a + j], irecv.at[3 * a + j], sib).wait_recv()
                if split[a] is not None:
                    cp = _rcopy(blk, blk, dsend.at[3 * a + j], drecv.at[3 * a + j], sib)
                    cp.start()
                    sends.append(cp)
        for j, (px, py) in enumerate(chips):
            for a in range(n):
                if split[a] is not None:
                    blk = other(outs[a].at[2 * px + py], a)
                    _rcopy(blk, blk, dsend.at[3 * a + j], drecv.at[3 * a + j], sib).wait_recv()
        for cp in sends:
            cp.wait_send()
        for cp in local:
            cp.wait()

    return pl.pallas_call(
        body, name=name, in_specs=[ANY] * n, out_specs=[ANY] * n,
        out_shape=[jax.ShapeDtypeStruct((4,) + a.shape, a.dtype) for a in arrs],
        scratch_shapes=[pltpu.SemaphoreType.DMA((n,))] + [pltpu.SemaphoreType.DMA((3 * n,))] * 4,
    )(*arrs)


def _halved(shape, axis):
    return tuple(d // 2 if i == axis else d for i, d in enumerate(shape))


def _swap_halves(name, arrs, split):
    n = len(arrs)

    def body(*refs):
        ins, outs = refs[:n], refs[n:2 * n]
        ssem, rsem = refs[2 * n:]
        x, y, c, me, sib, chips = _place()
        cps = []
        for a in range(n):
            src = _half(ins[a], 1, split[a], 1 - c, arrs[a].shape[1 + split[a]] // 2)
            cp = _rcopy(src, outs[a], ssem.at[a], rsem.at[a], sib)
            cp.start()
            cps.append(cp)
        for cp in cps:
            cp.wait()

    return pl.pallas_call(
        body, name=name, in_specs=[ANY] * n, out_specs=[ANY] * n,
        out_shape=[jax.ShapeDtypeStruct((4,) + _halved(a.shape[1:], s), a.dtype) for a, s in zip(arrs, split)],
        scratch_shapes=[pltpu.SemaphoreType.DMA((n,))] * 2,
    )(*arrs)


def _scatter4(name, arrs):
    n = len(arrs)

    def body(*refs):
        ins, outs = refs[:n], refs[n:2 * n]
        lsem, ssem, rsem = refs[2 * n:]
        x, y, c, me, sib, chips = _place()
        local = [pltpu.make_async_copy(ins[a].at[me], outs[a].at[me], lsem.at[a]) for a in range(n)]
        for cp in local:
            cp.start()
        cps = []
        for a in range(n):
            for j, (px, py) in enumerate(chips):
                cp = _rcopy(ins[a].at[2 * px + py], outs[a].at[me], ssem.at[3 * a + j], rsem.at[3 * a + j], (px, py, c))
                cp.start()
                cps.append(cp)
        for a in range(n):
            for j, (px, py) in enumerate(chips):
                blk = outs[a].at[2 * px + py]
                _rcopy(blk, blk, ssem.at[3 * a + j], rsem.at[3 * a + j], sib).wait_recv()
        for cp in cps:
            cp.wait_send()
        for cp in local:
            cp.wait()

    return pl.pallas_call(
        body, name=name, in_specs=[ANY] * n, out_specs=[ANY] * n,
        out_shape=[jax.ShapeDtypeStruct(a.shape, a.dtype) for a in arrs],
        scratch_shapes=[pltpu.SemaphoreType.DMA((n,))] + [pltpu.SemaphoreType.DMA((3 * n,))] * 2,
    )(*arrs)


def _join_halves(name, arrs, split):
    n = len(arrs)

    def body(*refs):
        ins, outs = refs[:n], refs[n:2 * n]
        lsem, ssem, rsem = refs[2 * n:]
        x, y, c, me, sib, chips = _place()
        cps, local = [], []
        for a in range(n):
            mine = _half(outs[a], 0, split[a], c, arrs[a].shape[split[a]])
            local.append(pltpu.make_async_copy(ins[a], mine, lsem.at[a]))
            local[-1].start()
            cp = _rcopy(ins[a], mine, ssem.at[a], rsem.at[a], sib)
            cp.start()
            cps.append(cp)
        for a in range(n):
            blk = _half(outs[a], 0, split[a], 1 - c, arrs[a].shape[split[a]])
            _rcopy(blk, blk, ssem.at[a], rsem.at[a], sib).wait_recv()
        for cp in cps:
            cp.wait_send()
        for cp in local:
            cp.wait()

    return pl.pallas_call(
        body, name=name, in_specs=[ANY] * n, out_specs=[ANY] * n,
        out_shape=[jax.ShapeDtypeStruct(tuple(2 * d if i == s else d for i, d in enumerate(a.shape)), a.dtype)
                   for a, s in zip(arrs, split)],
        scratch_shapes=[pltpu.SemaphoreType.DMA((n,))] * 3,
    )(*arrs)


def _allreduce_small(s):
    r, cdim = s.shape

    def body(s_ref, o_ref, buf, ssem, rsem):
        x, y, c, me, sib, chips = _place()
        me8 = 4 * x + 2 * y + c
        buf[me8] = s_ref[...]
        flips = [(fx, fy, fc) for fx in (0, 1) for fy in (0, 1) for fc in (0, 1)][1:]
        cps = []
        for k, (fx, fy, fc) in enumerate(flips):
            peer = (x ^ fx if fx else x, y ^ fy if fy else y, c ^ fc if fc else c)
            cp = _rcopy(s_ref, buf.at[me8], ssem.at[k], rsem.at[k], peer)
            cp.start()
            cps.append(cp)
        for k, (fx, fy, fc) in enumerate(flips):
            src = 4 * (x ^ fx if fx else x) + 2 * (y ^ fy if fy else y) + (c ^ fc if fc else c)
            _rcopy(s_ref, buf.at[src], ssem.at[k], rsem.at[k], sib).wait_recv()
        for cp in cps:
            cp.wait_send()
        acc = buf[0]
        for k in range(1, 8):
            acc = acc + buf[k]
        o_ref[...] = acc

    vm = pl.BlockSpec(memory_space=pltpu.VMEM)
    return pl.pallas_call(
        body, name="allreduce_small", in_specs=[vm], out_specs=vm, out_shape=jax.ShapeDtypeStruct((r, cdim), F32),
        scratch_shapes=[pltpu.VMEM((8, r, cdim), F32), pltpu.SemaphoreType.DMA((7,)), pltpu.SemaphoreType.DMA((7,))],
    )(s)


def _add_my_half(name, g, recv, c_idx, axis):
    nb, hr, hc = recv.shape
    tr = _pick(hr, (256, 176, 128, 64))
    if axis == 0:
        g4 = g.reshape(nb, 2, hr, hc)
        gspec = pl.BlockSpec((None, None, tr, hc), lambda b, i, c_ref: (b, c_ref[0], i, 0))
    else:
        g4 = g
        gspec = pl.BlockSpec((None, tr, hc), lambda b, i, c_ref: (b, i, c_ref[0]))

    def body(c_ref, g_ref, r_ref, o_ref):
        o_ref[...] = g_ref[...] + r_ref[...]

    return pl.pallas_call(
        body, name=name,
        grid_spec=pltpu.PrefetchScalarGridSpec(
            num_scalar_prefetch=1, grid=(nb, hr // tr),
            in_specs=[gspec, pl.BlockSpec((None, tr, hc), lambda b, i, c_ref: (b, i, 0))],
            out_specs=pl.BlockSpec((None, tr, hc), lambda b, i, c_ref: (b, i, 0))),
        out_shape=jax.ShapeDtypeStruct((nb, hr, hc), g.dtype), compiler_params=_cparams(("parallel", "parallel")),
    )(c_idx, g4, recv)


def _sum4(name, a):
    nb, h, cdim = a.shape
    tr = _pick(h, (256, 176, 128, 64))

    def body(a_ref, o_ref):
        o_ref[...] = ((a_ref[0] + a_ref[1]) + a_ref[2]) + a_ref[3]

    return pl.pallas_call(
        body, name=name, grid=(h // tr,), in_specs=[pl.BlockSpec((nb, tr, cdim), lambda i: (0, i, 0))],
        out_specs=pl.BlockSpec((tr, cdim), lambda i: (i, 0)), out_shape=jax.ShapeDtypeStruct((h, cdim), a.dtype),
        compiler_params=_cparams(("parallel",)),
    )(a)


def _adamw(name, w, g, m, v):
    c1 = 1.0 - ADAM_B1 ** ADAM_STEP
    c2 = 1.0 - ADAM_B2 ** ADAM_STEP

    def fn(w_t, g_t, m_t, v_t):
        m_n = ADAM_B1 * m_t + (1.0 - ADAM_B1) * g_t
        v_n = ADAM_B2 * v_t + (1.0 - ADAM_B2) * (g_t * g_t)
        delta = -ADAM_LR * ((m_n / c1) / (jnp.sqrt(v_n / c2) + ADAM_EPS) + ADAM_WD * w_t)
        return delta, m_n, v_n

    cdim = w.shape[1]
    return _rowwise(name, fn, [w, g, m, v], [], [(cdim, F32)] * 3, tm=_pick(w.shape[0], (256, 176, 128, 64, 8)))


BIG = ("ffn1_w_gate", "ffn1_w_up", "ffn1_w_down", "w_in", "w_out", "ffn2_w_gate", "ffn2_w_up", "ffn2_w_down",
       "w_ple_gate", "w_ple_proj")
SMALL = ("ffn1_norm", "mix_norm", "b_mlstm_gates", "b_fox_f", "mlstm_out_norm", "fox_out_norm", "ffn2_norm",
         "ple_gate_norm", "ple_proj_norm", "final_norm")
WEIGHTS = ("ffn1_norm", "ffn1_w_gate", "ffn1_w_up", "ffn1_w_down", "mix_norm", "w_in", "conv_qk", "b_mlstm_gates",
           "b_fox_f", "mlstm_out_norm", "fox_out_norm", "w_out", "ffn2_norm", "ffn2_w_gate", "ffn2_w_up", "ffn2_w_down",
           "ple_gate_norm", "w_ple_gate", "w_ple_proj", "ple_proj_norm", "final_norm")
TRANSPOSED = ("ffn1_w_gate", "ffn1_w_up", "w_in", "ffn2_w_gate", "ffn2_w_up")
PACK_W = 1024


def _chip_blocks(a):
    r, c4 = a.shape
    return a.reshape(r, 4, c4 // 4).transpose(1, 0, 2)


def _from_chip_blocks(a):
    nb, r, c = a.shape
    return a.transpose(1, 0, 2).reshape(r, nb * c)


def kernel(x, p, ffn1_norm, ffn1_w_gate, ffn1_w_up, ffn1_w_down, mix_norm, w_in, conv_qk, b_mlstm_gates, b_fox_f, mlstm_out_norm, fox_out_norm, w_out, ffn2_norm, ffn2_w_gate, ffn2_w_up, ffn2_w_down, ple_gate_norm, w_ple_gate, w_ple_proj, ple_proj_norm, final_norm, loss_target, m_ffn1_norm, m_ffn1_w_gate, m_ffn1_w_up, m_ffn1_w_down, m_mix_norm, m_w_in, m_conv_qk, m_b_mlstm_gates, m_b_fox_f, m_mlstm_out_norm, m_fox_out_norm, m_w_out, m_ffn2_norm, m_ffn2_w_gate, m_ffn2_w_up, m_ffn2_w_down, m_ple_gate_norm, m_w_ple_gate, m_w_ple_proj, m_ple_proj_norm, m_final_norm, v_ffn1_norm, v_ffn1_w_gate, v_ffn1_w_up, v_ffn1_w_down, v_mix_norm, v_w_in, v_conv_qk, v_b_mlstm_gates, v_b_fox_f, v_mlstm_out_norm, v_fox_out_norm, v_w_out, v_ffn2_norm, v_ffn2_w_gate, v_ffn2_w_up, v_ffn2_w_down, v_ple_gate_norm, v_w_ple_gate, v_w_ple_proj, v_ple_proj_norm, v_final_norm):
    w = dict(ffn1_norm=ffn1_norm, ffn1_w_gate=ffn1_w_gate, ffn1_w_up=ffn1_w_up, ffn1_w_down=ffn1_w_down, mix_norm=mix_norm,
             w_in=w_in, conv_qk=conv_qk, b_mlstm_gates=b_mlstm_gates, b_fox_f=b_fox_f, mlstm_out_norm=mlstm_out_norm,
             fox_out_norm=fox_out_norm, w_out=w_out, ffn2_norm=ffn2_norm, ffn2_w_gate=ffn2_w_gate, ffn2_w_up=ffn2_w_up,
             ffn2_w_down=ffn2_w_down, ple_gate_norm=ple_gate_norm, w_ple_gate=w_ple_gate, w_ple_proj=w_ple_proj,
             ple_proj_norm=ple_proj_norm, final_norm=final_norm)
    m = dict(ffn1_norm=m_ffn1_norm, ffn1_w_gate=m_ffn1_w_gate, ffn1_w_up=m_ffn1_w_up, ffn1_w_down=m_ffn1_w_down,
             mix_norm=m_mix_norm, w_in=m_w_in, conv_qk=m_conv_qk, b_mlstm_gates=m_b_mlstm_gates, b_fox_f=m_b_fox_f,
             mlstm_out_norm=m_mlstm_out_norm, fox_out_norm=m_fox_out_norm, w_out=m_w_out, ffn2_norm=m_ffn2_norm,
             ffn2_w_gate=m_ffn2_w_gate, ffn2_w_up=m_ffn2_w_up, ffn2_w_down=m_ffn2_w_down, ple_gate_norm=m_ple_gate_norm,
             w_ple_gate=m_w_ple_gate, w_ple_proj=m_w_ple_proj, ple_proj_norm=m_ple_proj_norm, final_norm=m_final_norm)
    v = dict(ffn1_norm=v_ffn1_norm, ffn1_w_gate=v_ffn1_w_gate, ffn1_w_up=v_ffn1_w_up, ffn1_w_down=v_ffn1_w_down,
             mix_norm=v_mix_norm, w_in=v_w_in, conv_qk=v_conv_qk, b_mlstm_gates=v_b_mlstm_gates, b_fox_f=v_b_fox_f,
             mlstm_out_norm=v_mlstm_out_norm, fox_out_norm=v_fox_out_norm, w_out=v_w_out, ffn2_norm=v_ffn2_norm,
             ffn2_w_gate=v_ffn2_w_gate, ffn2_w_up=v_ffn2_w_up, ffn2_w_down=v_ffn2_w_down, ple_gate_norm=v_ple_gate_norm,
             w_ple_gate=v_w_ple_gate, w_ple_proj=v_w_ple_proj, ple_proj_norm=v_ple_proj_norm, final_norm=v_final_norm)
    shapes = {n: w[n].shape for n in WEIGHTS}

    def view(a, n):
        return a[0].T if n in TRANSPOSED else a.reshape(-1, a.shape[-1])

    def unview(a, n):
        return (a.T if n in TRANSPOSED else a).reshape(shapes[n])

    w2, m2, v2 = ({n: view(a, n) for n, a in d.items()} for d in (w, m, v))

    shards = [w2[n].astype(BF16) for n in BIG] + [w2["conv_qk"]]
    split = [1 if n == "w_in" else 0 for n in BIG]
    full = dict(zip(BIG + ("conv_qk",), _gather4("gather_weights", shards, split + [None])))
    sp = {n: w2[n] for n in SMALL}
    loss_part, grad_x, gw, gs = _local_step(
        x[0], p[0, 0], loss_target[0], sp, full["ffn1_w_gate"], full["ffn1_w_up"], full["ffn1_w_down"],
        full["w_in"].reshape(-1, w_in.shape[1]), _from_chip_blocks(full["conv_qk"]), full["w_out"].reshape(-1, w_out.shape[-1]),
        full["ffn2_w_gate"], full["ffn2_w_up"], full["ffn2_w_down"], full["w_ple_gate"].reshape(-1, w_ple_gate.shape[-1]),
        _from_chip_blocks(full["w_ple_proj"]))
    loss = lax.psum(loss_part[0, 0], ("x", "y", "c"))

    gw["w_ple_proj"] = _chip_blocks(gw["w_ple_proj"])
    for n in ("w_in", "w_out", "w_ple_gate"):
        gw[n] = gw[n].reshape(4, -1, gw[n].shape[-1])
    blocks = [gw[n] for n in BIG]
    c_idx = lax.axis_index("c").astype(jnp.int32).reshape(1)
    swapped = _swap_halves("rs_swap", blocks, split)
    partial = [_add_my_half("rs_add_" + n, g, r, c_idx, s) for n, g, r, s in zip(BIG, blocks, swapped, split)]
    landed = _scatter4("rs_scatter", partial)
    halves = [_sum4("rs_sum_" + n, a) for n, a in zip(BIG, landed)]
    grads = dict(zip(BIG, _join_halves("rs_join", halves, split)))

    small = [gs[n].reshape(1, -1) for n in SMALL] + [gw["conv_qk"]]
    rows = [jnp.pad(a, ((0, 0), (0, PACK_W - a.shape[1]))) for a in small]
    packed = jnp.concatenate(rows, axis=0)
    packed = jnp.pad(packed, ((0, -packed.shape[0] % 8), (0, 0)))
    red = _allreduce_small(packed)
    for i, n in enumerate(SMALL):
        grads[n] = red[i:i + 1, :gs[n].size]
    dconv = red[len(SMALL):len(SMALL) + CONV_W, :gw["conv_qk"].shape[1]]
    cw = conv_qk.shape[-1]
    grads["conv_qk"] = lax.dynamic_slice_in_dim(dconv, (2 * lax.axis_index("x") + lax.axis_index("y")) * cw, cw, axis=1)

    outs = {}
    for n in WEIGHTS:
        g2 = grads[n].reshape(w2[n].shape)
        d, nm, nv = _adamw("adamw_" + n, w2[n], g2, m2[n], v2[n])
        outs[n] = tuple(unview(a, n) for a in (g2, d, nm, nv))
    return (loss, grad_x[None], *[outs[n][0] for n in WEIGHTS], *[outs[n][1] for n in WEIGHTS],
            *[outs[n][2] for n in WEIGHTS], *[outs[n][3] for n in WEIGHTS])
```

```python
import functools
import math

import jax
import jax.numpy as jnp
from jax import lax
from jax.experimental import pallas as pl
from jax.experimental.pallas import tpu as pltpu

F32 = jnp.float32
BF16 = jnp.bfloat16
EPS = 1e-6
NH_M, DK_M, DV_M = 4, 64, 128
NH_F, DH_F = 8, 64
CONV_W = 4
ADAM_LR, ADAM_B1, ADAM_B2, ADAM_EPS, ADAM_WD, ADAM_STEP = 0.001, 0.9, 0.999, 1e-08, 0.01, 10
VMEM_LIMIT = 56 * 1024 * 1024


def _cparams(sem):
    return pltpu.CompilerParams(dimension_semantics=sem, vmem_limit_bytes=VMEM_LIMIT)


def _sigmoid(x):
    return 1.0 / (1.0 + jnp.exp(-x))


def _dot(a, b, ca, cb):
    return lax.dot_general(a.astype(BF16), b.astype(BF16), (((ca,), (cb,)), ((), ())), preferred_element_type=F32)


def _rowwise(name, fn, tiled, full, outs, accs=(), tm=256):
    rows = tiled[0].shape[0]
    tm = min(tm, rows)
    assert rows % tm == 0
    n_t, n_f, n_o, n_a = len(tiled), len(full), len(outs), len(accs)

    def body(*refs):
        ins = [r[...] for r in refs[: n_t + n_f]]
        res = fn(*ins)
        if not isinstance(res, (tuple, list)):
            res = (res,)
        orefs = refs[n_t + n_f:]
        for r, v in zip(orefs[:n_o], res[:n_o]):
            r[...] = v.astype(r.dtype)
        if n_a:
            @pl.when(pl.program_id(0) == 0)
            def _():
                for r in orefs[n_o:]:
                    r[...] = jnp.zeros_like(r)
            for r, v in zip(orefs[n_o:], res[n_o:]):
                r[...] += v.astype(r.dtype)

    in_specs = [pl.BlockSpec((tm, a.shape[1]), lambda i: (i, 0)) for a in tiled]
    in_specs += [pl.BlockSpec(a.shape, lambda i: (0, 0)) for a in full]
    out_specs = [pl.BlockSpec((tm, c), lambda i: (i, 0)) for c, _ in outs]
    out_specs += [pl.BlockSpec(s, lambda i: (0, 0)) for s, _ in accs]
    out_shape = [jax.ShapeDtypeStruct((rows, c), d) for c, d in outs]
    out_shape += [jax.ShapeDtypeStruct(s, d) for s, d in accs]
    res = pl.pallas_call(
        body, name=name, grid=(rows // tm,), in_specs=in_specs, out_specs=out_specs, out_shape=out_shape,
        compiler_params=_cparams(("arbitrary",) if n_a else ("parallel",)),
    )(*tiled, *full)
    return res


def _colsum(v):
    return jnp.sum(v, axis=0, keepdims=True)


def _rms_fwd_val(x, g):
    r = lax.rsqrt(jnp.mean(x * x, axis=-1, keepdims=True) + EPS)
    return x * r * g


def _rms_bwd_val(dy, x, g):
    r = lax.rsqrt(jnp.mean(x * x, axis=-1, keepdims=True) + EPS)
    xh = x * r
    dxh = dy * g
    dx = r * (dxh - xh * jnp.mean(dxh * xh, axis=-1, keepdims=True))
    return dx, _colsum(dy * xh)


def _mm(name, pairs, out_shape, out_block, out_map, grid, kaxis, ta=False, tb=False, scale=None, res=None,
        out_dtype=F32):
    nk = grid[kaxis]
    npairs = len(pairs)
    ca, cb = (0 if ta else 1), (1 if tb else 0)
    acc_shape = tuple(d for d in out_block if d is not None)

    def body(*refs):
        in_refs = refs[: 2 * npairs]
        res_ref = refs[2 * npairs] if res is not None else None
        o_ref = refs[2 * npairs + (1 if res is not None else 0)]
        acc_ref = refs[-1]
        k = pl.program_id(kaxis)

        @pl.when(k == 0)
        def _():
            acc_ref[...] = jnp.zeros_like(acc_ref)

        part = None
        for p in range(npairs):
            d = _dot(in_refs[2 * p][...], in_refs[2 * p + 1][...], ca, cb)
            part = d if part is None else part + d
        acc_ref[...] += part

        @pl.when(k == nk - 1)
        def _():
            v = acc_ref[...]
            if scale is not None:
                v = v * scale
            if res_ref is not None:
                v = v + res_ref[...].astype(F32)
            o_ref[...] = v.astype(o_ref.dtype)

    in_specs, args = [], []
    for a, ab, am, b, bb, bm in pairs:
        in_specs += [pl.BlockSpec(ab, am), pl.BlockSpec(bb, bm)]
        args += [a, b]
    if res is not None:
        in_specs.append(pl.BlockSpec(out_block, out_map))
        args.append(res)
    sem = tuple("arbitrary" if i == kaxis else "parallel" for i in range(len(grid)))
    return pl.pallas_call(
        body, name=name, grid=grid, in_specs=in_specs, out_specs=pl.BlockSpec(out_block, out_map),
        out_shape=jax.ShapeDtypeStruct(out_shape, out_dtype), scratch_shapes=[pltpu.VMEM(acc_shape, F32)],
        compiler_params=_cparams(sem),
    )(*args)


def _pick(n, pref):
    for t in pref:
        if n % t == 0:
            return t
    return n


def _mm_nn(name, a, b, tm=512, tn=512, tk=512, **kw):
    (m, k), n = a.shape, b.shape[1]
    tm, tn, tk = _pick(m, (tm, 256, 128)), _pick(n, (tn, 256, 128)), _pick(k, (tk, 256, 128))
    return _mm(name, [(a, (tm, tk), lambda i, j, kk: (i, kk), b, (tk, tn), lambda i, j, kk: (kk, j))],
               (m, n), (tm, tn), lambda i, j, kk: (i, j), (m // tm, n // tn, k // tk), 2, **kw)


def _mm_nt(name, a, b, tm=512, tn=512, tk=512, **kw):
    (m, k), n = a.shape, b.shape[0]
    tm, tn, tk = _pick(m, (tm, 256, 128)), _pick(n, (tn, 256, 128)), _pick(k, (tk, 256, 128))
    return _mm(name, [(a, (tm, tk), lambda i, j, kk: (i, kk), b, (tn, tk), lambda i, j, kk: (j, kk))],
               (m, n), (tm, tn), lambda i, j, kk: (i, j), (m // tm, n // tn, k // tk), 2, tb=True, **kw)


def _mm_tn(name, a, b, tm=512, tn=512, tk=512, **kw):
    (k, m), n = a.shape, b.shape[1]
    tm, tn, tk = _pick(m, (tm, 256, 128)), _pick(n, (tn, 256, 128)), _pick(k, (tk, 256, 128))
    return _mm(name, [(a, (tk, tm), lambda i, j, kk: (kk, i), b, (tk, tn), lambda i, j, kk: (kk, j))],
               (m, n), (tm, tn), lambda i, j, kk: (i, j), (m // tm, n // tn, k // tk), 2, ta=True, **kw)


def _norm_mm(name, h, gamma, w, w_transposed, out_dtype):
    t, d = h.shape
    n = w.shape[0] if w_transposed else w.shape[1]
    tm, tn = _pick(t, (512, 256)), _pick(n, (1024, 512, 256, 128))

    def body(h_ref, gam_ref, w_ref, xn_ref, o_ref, xn_scr):
        @pl.when(pl.program_id(1) == 0)
        def _():
            xn = _rms_fwd_val(h_ref[...], gam_ref[...]).astype(BF16)
            xn_scr[...] = xn
            xn_ref[...] = xn

        o_ref[...] = _dot(xn_scr[...], w_ref[...], 1, 1 if w_transposed else 0).astype(o_ref.dtype)

    wspec = pl.BlockSpec((tn, d), lambda i, j: (j, 0)) if w_transposed else pl.BlockSpec((d, tn), lambda i, j: (0, j))
    return pl.pallas_call(
        body, name=name, grid=(t // tm, n // tn),
        in_specs=[pl.BlockSpec((tm, d), lambda i, j: (i, 0)), pl.BlockSpec((1, d), lambda i, j: (0, 0)), wspec],
        out_specs=[pl.BlockSpec((tm, d), lambda i, j: (i, 0)), pl.BlockSpec((tm, tn), lambda i, j: (i, j))],
        out_shape=[jax.ShapeDtypeStruct((t, d), BF16), jax.ShapeDtypeStruct((t, n), out_dtype)],
        scratch_shapes=[pltpu.VMEM((tm, d), BF16)], compiler_params=_cparams(("parallel", "arbitrary")),
    )(h, gamma, w)


def _ffn_fwd(pfx, h, gamma, wg, wu, wd):
    t, d = h.shape
    nb, f, _ = wg.shape
    tm = _pick(t, (512, 256))

    def body(h_ref, gam_ref, wg_ref, wu_ref, wd_ref, ho_ref, xn_ref, g_ref, u_ref, xn_scr, acc_ref):
        j = pl.program_id(1)

        @pl.when(j == 0)
        def _():
            xn = _rms_fwd_val(h_ref[...], gam_ref[...]).astype(BF16)
            xn_scr[...] = xn
            xn_ref[...] = xn
            acc_ref[...] = jnp.zeros_like(acc_ref)

        x = xn_scr[...]
        g = _dot(x, wg_ref[...], 1, 1)
        u = _dot(x, wu_ref[...], 1, 1)
        g_ref[...] = g.astype(BF16)
        u_ref[...] = u.astype(BF16)
        acc_ref[...] += _dot(g * _sigmoid(g) * u, wd_ref[...], 1, 0)

        @pl.when(j == nb - 1)
        def _():
            ho_ref[...] = h_ref[...] + 0.5 * acc_ref[...]

    row = pl.BlockSpec((tm, d), lambda i, j: (i, 0))
    blk = pl.BlockSpec((None, tm, f), lambda i, j: (j, i, 0))
    wspec = pl.BlockSpec((None, f, d), lambda i, j: (j, 0, 0))
    return pl.pallas_call(
        body, name=pfx + "_fwd", grid=(t // tm, nb),
        in_specs=[row, pl.BlockSpec((1, d), lambda i, j: (0, 0)), wspec, wspec, wspec], out_specs=[row, row, blk, blk],
        out_shape=[jax.ShapeDtypeStruct((t, d), F32), jax.ShapeDtypeStruct((t, d), BF16),
                   jax.ShapeDtypeStruct((nb, t, f), BF16), jax.ShapeDtypeStruct((nb, t, f), BF16)],
        scratch_shapes=[pltpu.VMEM((tm, d), BF16), pltpu.VMEM((tm, d), F32)],
        compiler_params=_cparams(("parallel", "arbitrary")),
    )(h, gamma, wg, wu, wd)


def _ffn_bwd(pfx, dh_out, h, gamma, xn, g_all, u_all, wg, wu, wd):
    t, d = h.shape
    nb, f, _ = wg.shape
    tm = _pick(t, (512, 256))
    tk = _pick(t, (512, 256))

    def body(dy_ref, h_ref, gam_ref, wg_ref, wu_ref, wd_ref, g_ref, u_ref, dh_ref, dgam_ref, dg_ref, du_ref, a_ref, acc_ref):
        i, j = pl.program_id(0), pl.program_id(1)

        @pl.when((i == 0) & (j == 0))
        def _():
            dgam_ref[...] = jnp.zeros_like(dgam_ref)

        @pl.when(j == 0)
        def _():
            acc_ref[...] = jnp.zeros_like(acc_ref)

        da = _dot(dy_ref[...], wd_ref[...], 1, 1) * 0.5
        g = g_ref[...].astype(F32)
        u = u_ref[...].astype(F32)
        s = _sigmoid(g)
        sl = g * s
        du = (da * sl).astype(BF16)
        dg = (da * u * (s * (1.0 + g * (1.0 - s)))).astype(BF16)
        du_ref[...] = du
        dg_ref[...] = dg
        a_ref[...] = (sl * u).astype(BF16)
        acc_ref[...] += _dot(dg, wg_ref[...], 1, 0) + _dot(du, wu_ref[...], 1, 0)

        @pl.when(j == nb - 1)
        def _():
            dx, dgam = _rms_bwd_val(acc_ref[...], h_ref[...], gam_ref[...])
            dh_ref[...] = dy_ref[...] + dx
            dgam_ref[...] += dgam

    row = pl.BlockSpec((tm, d), lambda i, j: (i, 0))
    vec = pl.BlockSpec((1, d), lambda i, j: (0, 0))
    blk = pl.BlockSpec((None, tm, f), lambda i, j: (j, i, 0))
    wspec = pl.BlockSpec((None, f, d), lambda i, j: (j, 0, 0))
    dh, dgamma, dg_all, du_all, a_all = pl.pallas_call(
        body, name=pfx + "_bwd", grid=(t // tm, nb),
        in_specs=[row, row, vec, wspec, wspec, wspec, blk, blk], out_specs=[row, vec, blk, blk, blk],
        out_shape=[jax.ShapeDtypeStruct((t, d), F32), jax.ShapeDtypeStruct((1, d), F32)]
        + [jax.ShapeDtypeStruct((nb, t, f), BF16)] * 3,
        scratch_shapes=[pltpu.VMEM((tm, d), F32)], compiler_params=_cparams(("arbitrary", "arbitrary")),
    )(dh_out, h, gamma, wg, wu, wd, g_all, u_all)

    xmap, bmap, omap = (lambda b, k: (k, 0)), (lambda b, k: (b, k, 0)), (lambda b, k: (b, 0, 0))
    dwg = _mm(pfx + "_dwg", [(dg_all, (None, tk, f), bmap, xn, (tk, d), xmap)], (nb, f, d), (None, f, d), omap,
              (nb, t // tk), 1, ta=True)
    dwu = _mm(pfx + "_dwu", [(du_all, (None, tk, f), bmap, xn, (tk, d), xmap)], (nb, f, d), (None, f, d), omap,
              (nb, t // tk), 1, ta=True)
    dwd = _mm(pfx + "_dwd", [(a_all, (None, tk, f), bmap, dh_out, (tk, d), xmap)], (nb, f, d), (None, f, d), omap,
              (nb, t // tk), 1, ta=True, scale=0.5)
    return dh, dgamma, dwg, dwu, dwd


HALO = 16


def _silu_grad(y):
    s = _sigmoid(y)
    return s * (1.0 + y * (1.0 - s))


def _with_halo(ref, i, n_tiles, tm, before, after):
    t = ref.shape[0]
    r0 = pl.multiple_of(i * tm, tm)
    parts = [ref[pl.ds(r0, tm), :].astype(F32)]
    if before:
        prev = ref[pl.ds(pl.multiple_of(jnp.maximum(r0 - HALO, 0), HALO), HALO), :].astype(F32)
        parts.insert(0, jnp.where(i > 0, prev, 0.0))
    if after:
        nxt = ref[pl.ds(pl.multiple_of(jnp.minimum(r0 + tm, t - HALO), HALO), HALO), :].astype(F32)
        parts.append(jnp.where(i < n_tiles - 1, nxt, 0.0))
    return jnp.concatenate(parts, axis=0)


def _conv_fwd(zbig, w):
    t, c = zbig.shape[0], w.shape[1]
    tm = _pick(t, (512, 256))
    nt = t // tm

    def body(x_ref, w_ref, o_ref):
        xe = _with_halo(x_ref, pl.program_id(0), nt, tm, True, False)
        wv = w_ref[...]
        y = xe * wv[3:4, :]
        for i in range(CONV_W - 1):
            y = y + pltpu.roll(xe, CONV_W - 1 - i, 0) * wv[i:i + 1, :]
        y = y[HALO:, :]
        o_ref[...] = (y * _sigmoid(y)).astype(o_ref.dtype)

    return pl.pallas_call(
        body, name="conv_fwd", grid=(nt,),
        in_specs=[pl.BlockSpec((t, c), lambda i: (0, 0)), pl.BlockSpec(w.shape, lambda i: (0, 0))],
        out_specs=pl.BlockSpec((tm, c), lambda i: (i, 0)), out_shape=jax.ShapeDtypeStruct((t, c), BF16),
        compiler_params=_cparams(("parallel",)),
    )(zbig, w)


def _conv_bwd(zbig, dact, w):
    t, c = dact.shape
    tm = _pick(t, (512, 256))
    nt = t // tm
    n = tm + HALO

    def body(x_ref, d_ref, w_ref, dx_ref, dw_ref):
        xe = _with_halo(x_ref, pl.program_id(0), nt, tm, True, True)
        de = _with_halo(d_ref, pl.program_id(0), nt, tm, False, True)
        wv = w_ref[...]
        sh = [pltpu.roll(xe, CONV_W - 1 - i, 0)[HALO:, :] if i < CONV_W - 1 else xe[HALO:, :] for i in range(CONV_W)]
        y = sh[0] * wv[0:1, :]
        for i in range(1, CONV_W):
            y = y + sh[i] * wv[i:i + 1, :]
        dy = de * _silu_grad(y)
        dx = dy * wv[3:4, :]
        for i in range(CONV_W - 1):
            dx = dx + pltpu.roll(dy, n - (CONV_W - 1 - i), 0) * wv[i:i + 1, :]
        dx_ref[...] = dx[:tm, :].astype(dx_ref.dtype)
        dyc = dy[:tm, :]
        dwp = jnp.concatenate([_colsum(dyc * sh[i][:tm, :]) for i in range(CONV_W)], axis=0)

        @pl.when(pl.program_id(0) == 0)
        def _():
            dw_ref[...] = jnp.zeros_like(dw_ref)
        dw_ref[...] += dwp

    return pl.pallas_call(
        body, name="conv_bwd", grid=(nt,),
        in_specs=[pl.BlockSpec((t, c), lambda i: (0, 0)), pl.BlockSpec((t, c), lambda i: (0, 0)),
                  pl.BlockSpec(w.shape, lambda i: (0, 0))],
        out_specs=[pl.BlockSpec((tm, c), lambda i: (i, 0)), pl.BlockSpec(w.shape, lambda i: (0, 0))],
        out_shape=[jax.ShapeDtypeStruct((t, c), BF16), jax.ShapeDtypeStruct(w.shape, F32)],
        compiler_params=_cparams(("arbitrary",)),
    )(zbig, dact, w)


LM = 256
HI = lax.Precision.HIGHEST


def _logsig(x):
    return jnp.minimum(x, 0.0) - jnp.log(1.0 + jnp.exp(-jnp.abs(x)))


def _tri(n, lower):
    r = lax.broadcasted_iota(jnp.int32, (n, n), 0)
    c = lax.broadcasted_iota(jnp.int32, (n, n), 1)
    return (r >= c) if lower else (r <= c)


def _f32dot(a, b):
    return lax.dot_general(a, b, (((1,), (0,)), ((), ())), precision=HI, preferred_element_type=F32)


def _mlstm_chunk(h, q_ref, k_ref, v_ref, zs_ref, zsr_ref, bc_ref, br_ref, c_prev, m_prev):
    l = LM
    q = q_ref[:, h * DK_M:(h + 1) * DK_M].astype(F32) * (DK_M ** -0.5)
    k = k_ref[:, h * DK_M:(h + 1) * DK_M]
    v = v_ref[:, h * DV_M:(h + 1) * DV_M]
    lane = lax.broadcasted_iota(jnp.int32, (l, DV_M), 1)
    v1 = jnp.concatenate([v, (lane == 0).astype(v.dtype)], axis=1)
    zs, zsr = zs_ref[...], zsr_ref[...]
    li_c = zs[:, h:h + 1] + bc_ref[:, h:h + 1]
    fp_c = zs[:, NH_M + h:NH_M + h + 1] + bc_ref[:, NH_M + h:NH_M + h + 1]
    li_r = zsr[h:h + 1, :] + br_ref[h:h + 1, :]
    fp_r = zsr[NH_M + h:NH_M + h + 1, :] + br_ref[NH_M + h:NH_M + h + 1, :]
    lf_c, lf_r = _logsig(fp_c), _logsig(fp_r)
    low = _tri(l, True)
    b_c = _f32dot(low.astype(F32), lf_c)
    b_r = _f32dot(lf_r, _tri(l, False).astype(F32))
    g = b_r[:, l - 1:l]
    dmat = jnp.where(low, b_c - b_r + li_r, -jnp.inf)
    inter = b_c + m_prev
    m_t = jnp.maximum(inter, jnp.max(dmat, axis=1, keepdims=True))
    w_inter = jnp.exp(inter - m_t)
    amat = jnp.exp(dmat - m_t)
    s = _dot(q, k, 1, 1)
    p = amat * s
    qc = _dot(q, c_prev, 1, 0)
    qc_w = w_inter * qc
    num1 = qc_w + _dot(p, v1, 1, 0)
    den = num1[:, DV_M:DV_M + 1]
    mx = jnp.maximum(jnp.abs(den), jnp.exp(-m_t))
    hh = num1[:, :DV_M] / mx
    a_c = g - b_c + li_c
    return dict(q=q, k=k, v1=v1, fp_c=fp_c, fp_r=fp_r, b_c=b_c, g=g, m_t=m_t, w_inter=w_inter, amat=amat, s=s, p=p,
                qc_w=qc_w, den=den, mx=mx, hh=hh, a_c=a_c)


def _mlstm_fwd(qk, zbig, zs, zsr, bc, br, gm):
    t = zs.shape[0]
    l = LM
    nc = t // l
    dm = NH_M * DV_M

    def body(q_ref, k_ref, v_ref, o_ref, zs_ref, zsr_ref, bc_ref, br_ref, gm_ref, y_ref, cst_ref, mst_ref, c_scr, m_scr):
        @pl.when(pl.program_id(0) == 0)
        def _():
            c_scr[...] = jnp.zeros_like(c_scr)
            m_scr[...] = jnp.zeros_like(m_scr)

        cst_ref[...] = c_scr[...]
        mst_ref[...] = m_scr[...]
        ys = []
        for h in range(NH_M):
            c_prev = c_scr[h]
            m_prev = m_scr[h:h + 1, 0:1]
            r = _mlstm_chunk(h, q_ref, k_ref, v_ref, zs_ref, zsr_ref, bc_ref, br_ref, c_prev, m_prev)
            hh = r["hh"]
            gh = gm_ref[:, h * DV_M:(h + 1) * DV_M]
            hn = hh * lax.rsqrt(jnp.mean(hh * hh, axis=-1, keepdims=True) + EPS) * gh
            og = o_ref[:, h * DV_M:(h + 1) * DV_M].astype(F32)
            ys.append(hn * _sigmoid(og))
            m_new = jnp.maximum(r["g"] + m_prev, jnp.max(r["a_c"], axis=0, keepdims=True))
            decay = jnp.exp(r["g"] + m_prev - m_new)
            wk = r["k"].astype(F32) * jnp.exp(r["a_c"] - m_new)
            c_scr[h] = decay * c_prev + _dot(wk, r["v1"], 0, 0)
            m_scr[h:h + 1, :] = jnp.broadcast_to(m_new, (1, 128))
        y_ref[...] = jnp.concatenate(ys, axis=1).astype(y_ref.dtype)

    return pl.pallas_call(
        body, name="mlstm_fwd", grid=(nc,),
        in_specs=[pl.BlockSpec((l, NH_M * DK_M), lambda i: (i, 0)), pl.BlockSpec((l, NH_M * DK_M), lambda i: (i, 1)),
                  pl.BlockSpec((l, dm), lambda i: (i, 1)), pl.BlockSpec((l, dm), lambda i: (i, 2)),
                  pl.BlockSpec((l, 128), lambda i: (i, 0)), pl.BlockSpec((8, l), lambda i: (0, i)),
                  pl.BlockSpec((1, 8), lambda i: (0, 0)), pl.BlockSpec((8, 1), lambda i: (0, 0)),
                  pl.BlockSpec((1, dm), lambda i: (0, 0))],
        out_specs=[pl.BlockSpec((l, dm), lambda i: (i, 0)), pl.BlockSpec((None, NH_M, DK_M, 2 * DV_M), lambda i: (i, 0, 0, 0)),
                   pl.BlockSpec((None, 8, 128), lambda i: (i, 0, 0))],
        out_shape=[jax.ShapeDtypeStruct((t, dm), BF16), jax.ShapeDtypeStruct((nc, NH_M, DK_M, 2 * DV_M), F32),
                   jax.ShapeDtypeStruct((nc, 8, 128), F32)],
        scratch_shapes=[pltpu.VMEM((NH_M, DK_M, 2 * DV_M), F32), pltpu.VMEM((8, 128), F32)],
        compiler_params=_cparams(("arbitrary",)),
    )(qk, qk, zbig, zbig, zs, zsr, bc, br, gm)


def _mlstm_bwd(qk, zbig, zs, zsr, bc, br, gm, cst, mst, dycat):
    t = zs.shape[0]
    l = LM
    nc = t // l
    dm = NH_M * DV_M

    def body(q_ref, k_ref, v_ref, o_ref, zs_ref, zsr_ref, bc_ref, br_ref, gm_ref, cst_ref, mst_ref, cnx_ref, mnx_ref,
             dy_ref, dqk_ref, dv_ref, do_ref, dzs_ref, dzr_ref, dgm_ref, dc_scr):
        @pl.when(pl.program_id(0) == 0)
        def _():
            dc_scr[...] = jnp.zeros_like(dc_scr)
            dgm_ref[...] = jnp.zeros_like(dgm_ref)

        lane = lax.broadcasted_iota(jnp.int32, (l, 128), 1)
        upper = _tri(l, False).astype(F32)
        lower = _tri(l, True).astype(F32)
        dzr_rows = [None] * 8
        dvs, dos, dgs, dqs, dks = [], [], [], [], []
        dzs = jnp.zeros((l, 128), F32)
        for h in range(NH_M):
            c_prev = cst_ref[h]
            m_prev = mst_ref[h:h + 1, 0:1]
            r = _mlstm_chunk(h, q_ref, k_ref, v_ref, zs_ref, zsr_ref, bc_ref, br_ref, c_prev, m_prev)
            hh, mx, den, m_t, v1, amat = r["hh"], r["mx"], r["den"], r["m_t"], r["v1"], r["amat"]
            gh = gm_ref[:, h * DV_M:(h + 1) * DV_M]
            rs = lax.rsqrt(jnp.mean(hh * hh, axis=-1, keepdims=True) + EPS)
            xh = hh * rs
            sg = _sigmoid(o_ref[:, h * DV_M:(h + 1) * DV_M].astype(F32))
            dyh = dy_ref[:, h * DV_M:(h + 1) * DV_M]
            dos.append(dyh * xh * gh * sg * (1.0 - sg))
            dhn = dyh * sg
            dgs.append(_colsum(dhn * xh))
            dxh = dhn * gh
            dh = rs * (dxh - xh * jnp.mean(dxh * xh, axis=-1, keepdims=True))
            g1 = dh / mx
            hd = jnp.sum(hh * dh, axis=-1, keepdims=True)
            dden = jnp.where(jnp.abs(den) > jnp.exp(-m_t), -hd / mx * jnp.sign(den), 0.0)
            g256 = jnp.concatenate([g1, jnp.where(lane == 0, dden, 0.0)], axis=1)
            dc_h = dc_scr[h]
            ea = jnp.exp(r["a_c"])
            dp = _dot(g256, v1, 1, 1)
            ds = dp * amat
            dqs.append((r["w_inter"] * _dot(g256, c_prev, 1, 1) + _dot(ds, r["k"], 1, 0)) * (DK_M ** -0.5))
            dks.append(_dot(ds, r["q"], 0, 0) + ea * _dot(v1, dc_h, 1, 1))
            dv_st = ea * _dot(r["k"], dc_h, 1, 0)
            dv1 = _dot(r["p"], g256, 0, 0) + dv_st
            dvs.append(dv1[:, :DV_M])
            wmat = dp * r["p"]
            c_in = _colsum(wmat)
            c_st = jnp.sum(v1.astype(F32) * dv_st, axis=-1, keepdims=True)
            r_t = jnp.sum(wmat, axis=1, keepdims=True) + jnp.sum(g256 * r["qc_w"], axis=-1, keepdims=True)
            db = r_t - c_st
            carry = jnp.exp(mnx_ref[h:h + 1, 0:1]) * jnp.sum(
                jnp.sum(dc_h * cnx_ref[h], axis=1, keepdims=True), axis=0, keepdims=True)
            dlf_c = _f32dot(upper, db) + carry
            dlf_r = -_f32dot(c_in, lower)
            dfp = dlf_c * _sigmoid(-r["fp_c"])
            dzs = dzs + jnp.where(lane == h, c_st, 0.0) + jnp.where(lane == NH_M + h, dfp, 0.0)
            dzr_rows[h] = c_in
            dzr_rows[NH_M + h] = dlf_r * _sigmoid(-r["fp_r"])
            wq = r["q"] * jnp.exp(r["b_c"] - m_t)
            dc_scr[h] = jnp.exp(r["g"]) * dc_h + _dot(wq, g256, 0, 0)
        dqk_ref[...] = jnp.concatenate(dqs + dks, axis=1)
        dv_ref[...] = jnp.concatenate(dvs, axis=1).astype(dv_ref.dtype)
        do_ref[...] = jnp.concatenate(dos, axis=1).astype(do_ref.dtype)
        dzs_ref[...] = dzs
        dzr_ref[...] = jnp.concatenate(dzr_rows, axis=0)
        dgm_ref[...] += jnp.concatenate(dgs, axis=1)

    rev = lambda i: nc - 1 - i
    nxt = lambda i: jnp.minimum(nc - i, nc - 1)
    return pl.pallas_call(
        body, name="mlstm_bwd", grid=(nc,),
        in_specs=[pl.BlockSpec((l, NH_M * DK_M), lambda i: (rev(i), 0)), pl.BlockSpec((l, NH_M * DK_M), lambda i: (rev(i), 1)),
                  pl.BlockSpec((l, dm), lambda i: (rev(i), 1)), pl.BlockSpec((l, dm), lambda i: (rev(i), 2)),
                  pl.BlockSpec((l, 128), lambda i: (rev(i), 0)), pl.BlockSpec((8, l), lambda i: (0, rev(i))),
                  pl.BlockSpec((1, 8), lambda i: (0, 0)), pl.BlockSpec((8, 1), lambda i: (0, 0)),
                  pl.BlockSpec((1, dm), lambda i: (0, 0)),
                  pl.BlockSpec((None, NH_M, DK_M, 2 * DV_M), lambda i: (rev(i), 0, 0, 0)),
                  pl.BlockSpec((None, 8, 128), lambda i: (rev(i), 0, 0)),
                  pl.BlockSpec((None, NH_M, DK_M, 2 * DV_M), lambda i: (nxt(i), 0, 0, 0)),
                  pl.BlockSpec((None, 8, 128), lambda i: (nxt(i), 0, 0)),
                  pl.BlockSpec((l, dm), lambda i: (rev(i), 0))],
        out_specs=[pl.BlockSpec((l, dm), lambda i: (rev(i), 0)),
                   pl.BlockSpec((l, dm), lambda i: (rev(i), 0)), pl.BlockSpec((l, dm), lambda i: (rev(i), 0)),
                   pl.BlockSpec((l, 128), lambda i: (rev(i), 0)), pl.BlockSpec((8, l), lambda i: (0, rev(i))),
                   pl.BlockSpec((1, dm), lambda i: (0, 0))],
        out_shape=[jax.ShapeDtypeStruct((t, dm), F32),
                   jax.ShapeDtypeStruct((t, dm), BF16), jax.ShapeDtypeStruct((t, dm), BF16),
                   jax.ShapeDtypeStruct((t, 128), F32), jax.ShapeDtypeStruct((8, t), F32),
                   jax.ShapeDtypeStruct((1, dm), F32)],
        scratch_shapes=[pltpu.VMEM((NH_M, DK_M, 2 * DV_M), F32)],
        compiler_params=_cparams(("arbitrary",)),
    )(qk, qk, zbig, zbig, zs, zsr, bc, br, gm, cst, mst, cst, mst, dycat)


def _fox_cumsum(zsr, bf_r):
    t = zsr.shape[1]
    cw = _pick(t, (512, 256))

    def body(z_ref, b_ref, c_ref):
        up = _tri(cw, False).astype(F32)
        carry = jnp.zeros((NH_F, 1), F32)
        for j in range(t // cw):
            cs = _f32dot(_logsig(z_ref[:, j * cw:(j + 1) * cw] + b_ref[...]), up) + carry
            c_ref[:, j * cw:(j + 1) * cw] = cs
            carry = cs[:, cw - 1:cw]

    return pl.pallas_call(
        body, name="fox_cumsum", grid=(1,),
        in_specs=[pl.BlockSpec((NH_F, t), lambda i: (1, 0)), pl.BlockSpec((NH_F, 1), lambda i: (0, 0))],
        out_specs=pl.BlockSpec((NH_F, t), lambda i: (0, 0)), out_shape=jax.ShapeDtypeStruct((NH_F, t), F32),
        compiler_params=_cparams(("arbitrary",)),
    )(zsr, bf_r)


def _fox_gate_bwd(zsr, bf_r, dc):
    t = zsr.shape[1]
    cw = _pick(t, (512, 256))

    def body(z_ref, b_ref, dc_ref, o_ref):
        low = _tri(cw, True).astype(F32)
        carry = jnp.zeros((NH_F, 1), F32)
        for j in reversed(range(t // cw)):
            sl = slice(j * cw, (j + 1) * cw)
            dlf = _f32dot(dc_ref[:, sl], low) + carry
            o_ref[:, sl] = dlf * _sigmoid(-(z_ref[:, sl] + b_ref[...]))
            carry = dlf[:, 0:1]

    return pl.pallas_call(
        body, name="fox_gate_bwd", grid=(1,),
        in_specs=[pl.BlockSpec((NH_F, t), lambda i: (1, 0)), pl.BlockSpec((NH_F, 1), lambda i: (0, 0)),
                  pl.BlockSpec((NH_F, t), lambda i: (0, 0))],
        out_specs=pl.BlockSpec((NH_F, t), lambda i: (0, 0)), out_shape=jax.ShapeDtypeStruct((NH_F, t), F32),
        compiler_params=_cparams(("arbitrary",)),
    )(zsr, bf_r, dc)


def _causal_mask(n):
    return _tri(n, True)


def _fox_fwd(q, k, v, c_col, c_row, gf):
    nh, t, dh = q.shape
    tq = _pick(t, (512, 256))
    scale = dh ** -0.5

    def body(q_ref, k_ref, v_ref, cc_ref, cr_ref, g_ref, o_ref, lse_ref, y_ref):
        i = pl.program_id(1)
        qv = q_ref[...]
        cq = cc_ref[...]

        def blk(j, carry, masked):
            m, l, acc = carry
            k0 = pl.multiple_of(j * tq, tq)
            kb = k_ref[pl.ds(k0, tq), :]
            vb = v_ref[pl.ds(k0, tq), :]
            s = _dot(qv, kb, 1, 1) * scale + cq - cr_ref[:, pl.ds(k0, tq)]
            if masked:
                s = jnp.where(_causal_mask(tq), s, -jnp.inf)
            m_new = jnp.maximum(m, jnp.max(s, axis=1, keepdims=True))
            alpha = jnp.exp(m - m_new)
            p = jnp.exp(s - m_new)
            return m_new, alpha * l + jnp.sum(p, axis=1, keepdims=True), alpha * acc + _dot(p, vb, 1, 0)

        init = (jnp.full((tq, 1), -jnp.inf, F32), jnp.zeros((tq, 1), F32), jnp.zeros((tq, dh), F32))
        carry = lax.fori_loop(0, i, lambda j, c: blk(j, c, False), init)
        m, l, acc = blk(i, carry, True)
        o = acc / l
        o_ref[...] = o
        lse_ref[...] = m + jnp.log(l)
        y_ref[...] = (o * lax.rsqrt(jnp.mean(o * o, axis=-1, keepdims=True) + EPS) * g_ref[...]).astype(y_ref.dtype)

    full = lambda w: pl.BlockSpec((None, t, w), lambda h, i: (h, 0, 0))
    tile = lambda w: pl.BlockSpec((None, tq, w), lambda h, i: (h, i, 0))
    return pl.pallas_call(
        body, name="fox_fwd", grid=(nh, t // tq),
        in_specs=[tile(dh), full(dh), full(dh), tile(1), pl.BlockSpec((None, 1, t), lambda h, i: (h, 0, 0)),
                  pl.BlockSpec((None, 1, dh), lambda h, i: (h, 0, 0))],
        out_specs=[tile(dh), tile(1), tile(dh)],
        out_shape=[jax.ShapeDtypeStruct((nh, t, dh), F32), jax.ShapeDtypeStruct((nh, t, 1), F32),
                   jax.ShapeDtypeStruct((nh, t, dh), BF16)],
        compiler_params=_cparams(("parallel", "parallel")),
    )(q, k, v, c_col, c_row, gf)


def _fox_norm_bwd(dy, o, gf):
    nh, t, dh = o.shape
    tm = _pick(t, (512, 256))

    def body(dy_ref, o_ref, g_ref, do_ref, dl_ref, dg_ref):
        ov = o_ref[...]
        dx, dg = _rms_bwd_val(dy_ref[...], ov, g_ref[...])
        do_ref[...] = dx
        dl_ref[...] = jnp.sum(dx * ov, axis=-1, keepdims=True)

        @pl.when(pl.program_id(1) == 0)
        def _():
            dg_ref[...] = jnp.zeros_like(dg_ref)
        dg_ref[...] += dg

    tile = lambda w: pl.BlockSpec((None, tm, w), lambda h, i: (h, i, 0))
    gspec = pl.BlockSpec((None, 1, dh), lambda h, i: (h, 0, 0))
    return pl.pallas_call(
        body, name="fox_norm_bwd", grid=(nh, t // tm), in_specs=[tile(dh), tile(dh), gspec],
        out_specs=[tile(dh), tile(1), gspec],
        out_shape=[jax.ShapeDtypeStruct((nh, t, dh), F32), jax.ShapeDtypeStruct((nh, t, 1), F32),
                   jax.ShapeDtypeStruct((nh, 1, dh), F32)],
        compiler_params=_cparams(("parallel", "arbitrary")),
    )(dy, o, gf)


def _fox_bwd(q, k, v, c_col, c_row, do, lse, delta):
    nh, t, dh = q.shape
    tq = _pick(t, (512, 256))
    nq = t // tq
    scale = dh ** -0.5

    def body(q_ref, k_ref, v_ref, cc_ref, cr_ref, do_ref, lse_ref, dl_ref, dq_ref, dk_ref, dv_ref, dc_ref, dcq_ref):
        j = pl.program_id(1)

        @pl.when(j == 0)
        def _():
            dq_ref[...] = jnp.zeros_like(dq_ref)
            dcq_ref[...] = jnp.zeros_like(dcq_ref)

        kb, vb, crb = k_ref[...], v_ref[...], cr_ref[...]

        def blk(i, carry, masked):
            dk, dv, dc = carry
            rows = pl.ds(pl.multiple_of(i * tq, tq), tq)
            qb = q_ref[rows, :]
            dob = do_ref[rows, :].astype(BF16)
            s = _dot(qb, kb, 1, 1) * scale + cc_ref[rows, :] - crb
            if masked:
                s = jnp.where(_causal_mask(tq), s, -jnp.inf)
            p = jnp.exp(s - lse_ref[rows, :])
            dv = dv + _dot(p, dob, 0, 0)
            ds = p * (_dot(dob, vb, 1, 1) - dl_ref[rows, :])
            dc = dc + _colsum(ds)
            dk = dk + _dot(ds, qb, 0, 0) * scale
            dq_ref[rows, :] += _dot(ds, kb, 1, 0) * scale
            dcq_ref[rows, :] += jnp.sum(ds, axis=1, keepdims=True)
            return dk, dv, dc

        init = (jnp.zeros((tq, dh), F32), jnp.zeros((tq, dh), F32), jnp.zeros((1, tq), F32))
        carry = blk(j, init, True)
        dk, dv, dc = lax.fori_loop(j + 1, nq, lambda i, c: blk(i, c, False), carry)
        dk_ref[...] = dk
        dv_ref[...] = dv
        dc_ref[...] = -dc

    full = lambda w: pl.BlockSpec((None, t, w), lambda h, j: (h, 0, 0))
    tile = lambda w: pl.BlockSpec((None, tq, w), lambda h, j: (h, j, 0))
    crow = pl.BlockSpec((None, 1, tq), lambda h, j: (h, 0, j))
    return pl.pallas_call(
        body, name="fox_bwd", grid=(nh, nq),
        in_specs=[full(dh), tile(dh), tile(dh), full(1), crow, full(dh), full(1), full(1)],
        out_specs=[full(dh), tile(dh), tile(dh), crow, full(1)],
        out_shape=[jax.ShapeDtypeStruct((nh, t, dh), F32)] * 3 + [jax.ShapeDtypeStruct((nh, 1, t), F32),
                                                                jax.ShapeDtypeStruct((nh, t, 1), F32)],
        compiler_params=_cparams(("parallel", "arbitrary")),
    )(q, k, v, c_col, c_row, do, lse, delta)


AUG = 64


def _split3(c):
    hi = c.astype(BF16).astype(F32)
    r1 = c - hi
    mid = r1.astype(BF16).astype(F32)
    return hi, mid, r1 - mid


def _fox_prep(zbig, ct):
    t = zbig.shape[0]
    tm = _pick(t, (512, 256))

    def body(q_ref, k_ref, v_ref, c_ref, qo_ref, ko_ref, vo_ref):
        lane = lax.broadcasted_iota(jnp.int32, (tm, AUG), 1)
        qv, kv, vv, cv = q_ref[...], k_ref[...], v_ref[...], c_ref[...]
        one = (lane == 0).astype(BF16)
        for h in range(NH_F):
            hi, mid, lo = _split3(cv[:, h:h + 1])
            aq = jnp.where(lane == 0, hi, jnp.where(lane == 1, mid, jnp.where(lane == 2, lo, jnp.where(lane < 6, 1.0, 0.0))))
            ak = jnp.where(lane < 3, 1.0, jnp.where(lane == 3, -hi, jnp.where(lane == 4, -mid, jnp.where(lane == 5, -lo, 0.0))))
            sl = slice(h * DH_F, (h + 1) * DH_F)
            qo_ref[h] = jnp.concatenate([qv[:, sl] * (DH_F ** -0.5), aq.astype(BF16)], axis=1).astype(BF16)
            ko_ref[h] = jnp.concatenate([kv[:, sl], ak.astype(BF16)], axis=1)
            vo_ref[h] = jnp.concatenate([vv[:, sl], one], axis=1)

    ospec = pl.BlockSpec((NH_F, tm, 128), lambda i: (0, i, 0))
    return pl.pallas_call(
        body, name="fox_prep", grid=(t // tm,),
        in_specs=[pl.BlockSpec((tm, 512), lambda i: (i, 3)), pl.BlockSpec((tm, 512), lambda i: (i, 4)),
                  pl.BlockSpec((tm, 512), lambda i: (i, 5)), pl.BlockSpec((tm, NH_F), lambda i: (i, 0))],
        out_specs=[ospec] * 3, out_shape=[jax.ShapeDtypeStruct((NH_F, t, 128), BF16)] * 3,
        compiler_params=_cparams(("parallel",)),
    )(zbig, zbig, zbig, ct)


def _fox_fwd2(qa, ka, va, gf):
    nh, t, _ = qa.shape
    tq = _pick(t, (512, 256))

    def body(q_ref, k_ref, v_ref, g_ref, y_ref, o_ref, lse_ref):
        i = pl.program_id(0)
        lane = lax.broadcasted_iota(jnp.int32, (tq, 128), 1)
        ys, os_ = [], []
        lse_all = jnp.zeros((tq, 128), F32)
        for h in range(nh):
            qv = q_ref[h]

            def blk(j, carry, masked, h=h, qv=qv):
                m, acc = carry
                k0 = pl.multiple_of(j * tq, tq)
                s = lax.dot_general(qv, k_ref[h, pl.ds(k0, tq), :], (((1,), (1,)), ((), ())), preferred_element_type=F32)
                if masked:
                    s = jnp.where(_causal_mask(tq), s, -jnp.inf)
                m_new = jnp.maximum(m, jnp.max(s, axis=1, keepdims=True))
                p = jnp.exp(s - m_new).astype(BF16)
                pv = lax.dot_general(p, v_ref[h, pl.ds(k0, tq), :], (((1,), (0,)), ((), ())), preferred_element_type=F32)
                return m_new, jnp.exp(m - m_new) * acc + pv

            init = (jnp.full((tq, 1), -jnp.inf, F32), jnp.zeros((tq, 128), F32))
            carry = lax.fori_loop(0, i, lambda j, c: blk(j, c, False), init)
            m, acc = blk(i, carry, True)
            l = acc[:, DH_F:DH_F + 1]
            o = acc[:, :DH_F] / l
            os_.append(o)
            gh = g_ref[:, h * DH_F:(h + 1) * DH_F]
            ys.append(o * lax.rsqrt(jnp.mean(o * o, axis=-1, keepdims=True) + EPS) * gh)
            lse_all = lse_all + jnp.where(lane == h, m + jnp.log(l), 0.0)
        y_ref[...] = jnp.concatenate(ys, axis=1).astype(y_ref.dtype)
        o_ref[...] = jnp.concatenate(os_, axis=1)
        lse_ref[...] = lse_all

    full = pl.BlockSpec((nh, t, 128), lambda i: (0, 0, 0))
    return pl.pallas_call(
        body, name="fox_fwd", grid=(t // tq,),
        in_specs=[pl.BlockSpec((nh, tq, 128), lambda i: (0, i, 0)), full, full, pl.BlockSpec((1, nh * DH_F), lambda i: (0, 0))],
        out_specs=[pl.BlockSpec((tq, nh * DH_F), lambda i: (i, 0)), pl.BlockSpec((tq, nh * DH_F), lambda i: (i, 0)),
                   pl.BlockSpec((tq, 128), lambda i: (i, 0))],
        out_shape=[jax.ShapeDtypeStruct((t, nh * DH_F), BF16), jax.ShapeDtypeStruct((t, nh * DH_F), F32),
                   jax.ShapeDtypeStruct((t, 128), F32)],
        compiler_params=_cparams(("parallel",)),
    )(qa, ka, va, gf)


def _fox_bwd_prep(dycat, o, gf):
    t = o.shape[0]
    tm = _pick(t, (512, 256))

    def body(dy_ref, o_ref, g_ref, do_ref, dl_ref, dg_ref):
        lane = lax.broadcasted_iota(jnp.int32, (tm, 128), 1)
        dyv, ov, gv = dy_ref[...], o_ref[...], g_ref[...]
        dgs = []
        dl = jnp.zeros((tm, 128), F32)
        pad = jnp.zeros((tm, AUG), BF16)
        for h in range(NH_F):
            sl = slice(h * DH_F, (h + 1) * DH_F)
            dx, dg = _rms_bwd_val(dyv[:, sl], ov[:, sl], gv[:, sl])
            dgs.append(dg)
            do_ref[h] = jnp.concatenate([dx.astype(BF16), pad], axis=1)
            dl = dl + jnp.where(lane == h, jnp.sum(dx * ov[:, sl], axis=-1, keepdims=True), 0.0)
        dl_ref[...] = dl

        @pl.when(pl.program_id(0) == 0)
        def _():
            dg_ref[...] = jnp.zeros_like(dg_ref)
        dg_ref[...] += jnp.concatenate(dgs, axis=1)

    return pl.pallas_call(
        body, name="fox_bwd_prep", grid=(t // tm,),
        in_specs=[pl.BlockSpec((tm, 512), lambda i: (i, 1)), pl.BlockSpec((tm, 512), lambda i: (i, 0)),
                  pl.BlockSpec((1, 512), lambda i: (0, 0))],
        out_specs=[pl.BlockSpec((NH_F, tm, 128), lambda i: (0, i, 0)), pl.BlockSpec((tm, 128), lambda i: (i, 0)),
                   pl.BlockSpec((1, 512), lambda i: (0, 0))],
        out_shape=[jax.ShapeDtypeStruct((NH_F, t, 128), BF16), jax.ShapeDtypeStruct((t, 128), F32),
                   jax.ShapeDtypeStruct((1, 512), F32)],
        compiler_params=_cparams(("arbitrary",)),
    )(dycat, o, gf)


def _fox_bwd2(qa, ka, va, doa, lse, delta):
    nh, t, _ = qa.shape
    tq = _pick(t, (512, 256))
    nq = t // tq

    def body(q_ref, k_ref, v_ref, do_ref, lse_ref, dl_ref, dq_ref, dk_ref, dv_ref):
        h, j = pl.program_id(0), pl.program_id(1)

        @pl.when(j == 0)
        def _():
            dq_ref[...] = jnp.zeros_like(dq_ref)

        kb, vb = k_ref[...], v_ref[...]
        lane = lax.broadcasted_iota(jnp.int32, (tq, 128), 1)

        def blk(i, carry, masked):
            dk, dv = carry
            rows = pl.ds(pl.multiple_of(i * tq, tq), tq)
            qb, dob = q_ref[rows, :], do_ref[rows, :]
            lse_h = jnp.sum(jnp.where(lane == h, lse_ref[rows, :], 0.0), axis=1, keepdims=True)
            dl_h = jnp.sum(jnp.where(lane == h, dl_ref[rows, :], 0.0), axis=1, keepdims=True)
            s = lax.dot_general(qb, kb, (((1,), (1,)), ((), ())), preferred_element_type=F32)
            if masked:
                s = jnp.where(_causal_mask(tq), s, -jnp.inf)
            p = jnp.exp(s - lse_h)
            dp = lax.dot_general(dob, vb, (((1,), (1,)), ((), ())), preferred_element_type=F32)
            ds = (p * (dp - dl_h)).astype(BF16)
            dv = dv + lax.dot_general(p.astype(BF16), dob, (((0,), (0,)), ((), ())), preferred_element_type=F32)
            dk = dk + lax.dot_general(ds, qb, (((0,), (0,)), ((), ())), preferred_element_type=F32)
            dq_ref[rows, :] += lax.dot_general(ds, kb, (((1,), (0,)), ((), ())), preferred_element_type=F32)
            return dk, dv

        init = (jnp.zeros((tq, 128), F32), jnp.zeros((tq, 128), F32))
        carry = blk(j, init, True)
        dk, dv = lax.fori_loop(j + 1, nq, lambda i, c: blk(i, c, False), carry)
        dk_ref[...] = dk
        dv_ref[...] = dv

    full = pl.BlockSpec((None, t, 128), lambda h, j: (h, 0, 0))
    tile = pl.BlockSpec((None, tq, 128), lambda h, j: (h, j, 0))
    cols = pl.BlockSpec((t, 128), lambda h, j: (0, 0))
    return pl.pallas_call(
        body, name="fox_bwd", grid=(nh, nq), in_specs=[full, tile, tile, full, cols, cols], out_specs=[full, tile, tile],
        out_shape=[jax.ShapeDtypeStruct((nh, t, 128), F32)] * 3, compiler_params=_cparams(("parallel", "arbitrary")),
    )(qa, ka, va, doa, lse, delta)


def _fox_bwd_post(dqa, dka, dva):
    nh, t, _ = dqa.shape
    tm = _pick(t, (512, 256))

    def body(dq_ref, dk_ref, dv_ref, oq_ref, ok_ref, ov_ref, dc_ref):
        lane = lax.broadcasted_iota(jnp.int32, (tm, 128), 1)
        dc = jnp.zeros((tm, 128), F32)
        qs, ks, vs = [], [], []
        for h in range(nh):
            dq, dk = dq_ref[h], dk_ref[h]
            qs.append(dq[:, :DH_F] * (DH_F ** -0.5))
            ks.append(dk[:, :DH_F])
            vs.append(dv_ref[h][:, :DH_F])
            dc = dc + jnp.where(lane == h, dq[:, DH_F:DH_F + 1] - dk[:, DH_F + 3:DH_F + 4], 0.0)
        oq_ref[...] = jnp.concatenate(qs, axis=1).astype(BF16)
        ok_ref[...] = jnp.concatenate(ks, axis=1).astype(BF16)
        ov_ref[...] = jnp.concatenate(vs, axis=1).astype(BF16)
        dc_ref[...] = dc

    ispec = pl.BlockSpec((nh, tm, 128), lambda i: (0, i, 0))
    ospec = pl.BlockSpec((tm, nh * DH_F), lambda i: (i, 0))
    return pl.pallas_call(
        body, name="fox_bwd_post", grid=(t // tm,), in_specs=[ispec] * 3,
        out_specs=[ospec] * 3 + [pl.BlockSpec((tm, 128), lambda i: (i, 0))],
        out_shape=[jax.ShapeDtypeStruct((t, nh * DH_F), BF16)] * 3 + [jax.ShapeDtypeStruct((t, 128), F32)],
        compiler_params=_cparams(("parallel",)),
    )(dqa, dka, dva)


W_BIG = 6 * 512
IN_OFF = (0, 512, 1024, 1544, 2056, 2568)
IN_GATES = (1536, 3080)


def _heads(a, nh):
    t = a.shape[0]
    return a.reshape(t, nh, -1).transpose(1, 0, 2)


def _unheads(a):
    nh, t, dh = a.shape
    return a.transpose(1, 0, 2).reshape(t, nh * dh)


def _local_step(x, p, tgt, sp, wg1, wu1, wd1, w_in, conv_w, w_out, wg2, wu2, wd2, w_pg, w_pp):
    t, d = x.shape
    w_big = jnp.concatenate([w_in[o:o + 512] for o in IN_OFF], axis=0)
    w_small = jnp.concatenate([w_in[IN_GATES[0]:IN_GATES[0] + 8], w_in[IN_GATES[1]:IN_GATES[1] + 8],
                               jnp.zeros((112, d), w_in.dtype)], axis=0)
    h1, xn1, g1, u1 = _ffn_fwd("ffn1", x, sp["ffn1_norm"], wg1, wu1, wd1)
    u, zbig = _norm_mm("in_big", h1, sp["mix_norm"], w_big, True, BF16)
    zs = _mm_nt("in_small", u, w_small, tm=1024, tk=1024)
    zsr = zs.T
    qk_act = _conv_fwd(zbig, conv_w)
    bm_c, bf_c = sp["b_mlstm_gates"], sp["b_fox_f"]
    y_m, cst, mst = _mlstm_fwd(qk_act, zbig, zs, zsr, bm_c, bm_c.T, sp["mlstm_out_norm"])
    c = _fox_cumsum(zsr, bf_c.T)
    qa, ka, va = _fox_prep(zbig, c.T)
    y_ft, o_f, lse = _fox_fwd2(qa, ka, va, sp["fox_out_norm"])
    tm = _pick(t, (1024, 512, 256))
    h2 = _mm("out_proj", [(y_m, (tm, 512), lambda i, j, k: (i, 0), w_out, (512, d), lambda i, j, k: (0, 0)),
                          (y_ft, (tm, 512), lambda i, j, k: (i, 0), w_out, (512, d), lambda i, j, k: (1, 0))],
             (t, d), (tm, d), lambda i, j, k: (i, 0), (t // tm, 1, 1), 2, res=h1)
    h3, xn2, g2, u2 = _ffn_fwd("ffn2", h2, sp["ffn2_norm"], wg2, wu2, wd2)
    hn3, gate_pre = _norm_mm("ple_gate", h3, sp["ple_gate_norm"], w_pg, False, F32)
    pp = _mm_nn("ple_proj", p, w_pp, tm=1024)

    def head_fn(h3_t, gp_t, pp_t, tgt_t, g_pp, g_fin):
        gate = _sigmoid(gp_t)
        ppn = _rms_fwd_val(pp_t, g_pp)
        h4 = h3_t + gate * ppn
        err = _rms_fwd_val(h4, g_fin) - tgt_t
        loss = 0.5 * jnp.sum(jnp.mean(err * err, axis=-1, keepdims=True), axis=0, keepdims=True)
        dh4, dg_fin = _rms_bwd_val(err * (1.0 / d), h4, g_fin)
        dpp, dg_pp = _rms_bwd_val(dh4 * gate, pp_t, g_pp)
        dgp = dh4 * ppn * gate * (1.0 - gate)
        return dh4, dgp, dpp, jnp.broadcast_to(loss, (1, 128)), dg_fin, dg_pp

    dh4, dgp, dpp, loss_part, dg_fin, dg_pp = _rowwise(
        "loss_head", head_fn, [h3, gate_pre, pp, tgt], [sp["ple_proj_norm"], sp["final_norm"]],
        [(d, F32), (d, BF16), (d, BF16)], [((1, 128), F32), ((1, d), F32), ((1, d), F32)])
    gw, gs = {}, {"final_norm": dg_fin, "ple_proj_norm": dg_pp}
    gw["w_ple_gate"] = _mm_tn("d_w_pg", hn3, dgp, tm=1024, tn=1024)
    gw["w_ple_proj"] = _mm_tn("d_w_pp", p, dpp, tn=1024)
    dhn3 = _mm_nt("d_hn3", dgp, w_pg, tm=1024, tn=1024, tk=1024)

    def res_norm_bwd(dn_t, h_t, dres_t, g):
        dx, dg = _rms_bwd_val(dn_t, h_t, g)
        return dres_t + dx, dg

    dh3, gs["ple_gate_norm"] = _rowwise("ple_norm_bwd", res_norm_bwd, [dhn3, h3, dh4], [sp["ple_gate_norm"]],
                                        [(d, F32)], [((1, d), F32)])
    dh2, gs["ffn2_norm"], gw["ffn2_w_gate"], gw["ffn2_w_up"], gw["ffn2_w_down"] = _ffn_bwd(
        "ffn2", dh3, h2, sp["ffn2_norm"], xn2, g2, u2, wg2, wu2, wd2)
    dycat = _mm_nt("d_ycat", dh2, w_out, tm=1024, tn=1024, tk=1024)
    gw["w_out"] = jnp.concatenate([_mm_tn("d_w_out_m", y_m, dh2, tn=1024), _mm_tn("d_w_out_f", y_ft, dh2, tn=1024)], axis=0)
    doa, delta, gs["fox_out_norm"] = _fox_bwd_prep(dycat, o_f, sp["fox_out_norm"])
    dq_f, dk_f, dv_f, dct = _fox_bwd_post(*_fox_bwd2(qa, ka, va, doa, lse, delta))
    dfp = _fox_gate_bwd(zsr, bf_c.T, dct[:, :NH_F].T)
    dact, dv_m, do_m, dzs_m, dzr_m, gs["mlstm_out_norm"] = _mlstm_bwd(
        qk_act, zbig, zs, zsr, bm_c, bm_c.T, sp["mlstm_out_norm"], cst, mst, dycat)
    dqk, gw["conv_qk"] = _conv_bwd(zbig, dact, conv_w)
    dz_big = jnp.concatenate([dqk, dv_m, do_m, dq_f, dk_f, dv_f], axis=1)
    dzs = dzs_m + jnp.pad(jnp.concatenate([dzr_m, dfp], axis=0).T, ((0, 0), (0, 112)))
    dw_big = _mm_tn("d_w_big", dz_big, u, tn=1024)
    dw_small = _mm_tn("d_w_small", dzs, u, tn=1024)
    gw["w_in"] = jnp.concatenate([dw_big[0:1536], dw_small[0:8], dw_big[1536:3072], dw_small[8:16]], axis=0)
    du_a = _mm_nn("d_u_big", dz_big, w_big, tm=1024, tn=1024, tk=1024)
    du_b = _mm_nn("d_u_small", dzs, w_small, tm=1024, tn=1024)

    def mix_norm_bwd(da_t, db_t, h_t, dres_t, dzs_t, g):
        dx, dg = _rms_bwd_val(da_t + db_t, h_t, g)
        return dres_t + dx, dg, _colsum(dzs_t)

    dh1, gs["mix_norm"], dbias = _rowwise("mix_norm_bwd", mix_norm_bwd, [du_a, du_b, h1, dh2, dzs], [sp["mix_norm"]],
                                          [(d, F32)], [((1, d), F32), ((1, 128), F32)])
    gs["b_mlstm_gates"], gs["b_fox_f"] = dbias[:, 0:8], dbias[:, 8:16]
    grad_x, gs["ffn1_norm"], gw["ffn1_w_gate"], gw["ffn1_w_up"], gw["ffn1_w_down"] = _ffn_bwd(
        "ffn1", dh1, x, sp["ffn1_norm"], xn1, g1, u1, wg1, wu1, wd1)
    return loss_part, grad_x, gw, gs


ANY = pl.BlockSpec(memory_space=pl.ANY)
MESH = pl.DeviceIdType.MESH


def _place():
    x, y, c = lax.axis_index("x"), lax.axis_index("y"), lax.axis_index("c")
    chips = [(1 - x, y), (x, 1 - y), (1 - x, 1 - y)]
    return x, y, c, 2 * x + y, (x, y, 1 - c), chips


def _rcopy(src, dst, ssem, rsem, dev):
    return pltpu.make_async_remote_copy(src_ref=src, dst_ref=dst, send_sem=ssem, recv_sem=rsem, device_id=dev,
                                        device_id_type=MESH)


def _half(ref, lead, axis, idx, half):
    return ref.at[(slice(None),) * (lead + axis) + (pl.ds(idx * half, half),)]


def _gather4(name, arrs, split):
    n = len(arrs)

    def body(*refs):
        ins, outs = refs[:n], refs[n:2 * n]
        lsem, isend, irecv, dsend, drecv = refs[2 * n:]
        x, y, c, me, sib, chips = _place()

        def part(ref, a):
            if split[a] is None:
                return ref
            return _half(ref, 0, split[a], c, arrs[a].shape[split[a]] // 2)

        def other(ref, a):
            return _half(ref, 0, split[a], 1 - c, arrs[a].shape[split[a]] // 2)

        local = [pltpu.make_async_copy(ins[a], outs[a].at[me], lsem.at[a]) for a in range(n)]
        for cp in local:
            cp.start()
        sends = []
        for a in range(n):
            for j, chip in enumerate(chips):
                cp = _rcopy(part(ins[a], a), part(outs[a].at[me], a), isend.at[3 * a + j], irecv.at[3 * a + j], (*chip, c))
                cp.start()
                sends.append(cp)
        for j, (px, py) in enumerate(chips):
            src_chip = 2 * px + py
            for a in range(n):
                blk = part(outs[a].at[src_chip], a)
                _rcopy(blk, blk, isend.at[3 * a + j], irecv.at[3 * a + j], sib).wait_recv()
                if split[a] is not None:
                    cp = _rcopy(blk, blk, dsend.at[3 * a + j], drecv.at[3 * a + j], sib)
                    cp.start()
                    sends.append(cp)
        for j, (px, py) in enumerate(chips):
            for a in range(n):
                if split[a] is not None:
                    blk = other(outs[a].at[2 * px + py], a)
                    _rcopy(blk, blk, dsend.at[3 * a + j], drecv.at[3 * a + j], sib).wait_recv()
        for cp in sends:
            cp.wait_send()
        for cp in local:
            cp.wait()

    return pl.pallas_call(
        body, name=name, in_specs=[ANY] * n, out_specs=[ANY] * n,
        out_shape=[jax.ShapeDtypeStruct((4,) + a.shape, a.dtype) for a in arrs],
        scratch_shapes=[pltpu.SemaphoreType.DMA((n,))] + [pltpu.SemaphoreType.DMA((3 * n,))] * 4,
    )(*arrs)


def _halved(shape, axis):
    return tuple(d // 2 if i == axis else d for i, d in enumerate(shape))


def _swap(name, arrs):
    n = len(arrs)

    def body(*refs):
        ins, outs = refs[:n], refs[n:2 * n]
        ssem, rsem = refs[2 * n:]
        x, y, c, me, sib, chips = _place()
        cps = [_rcopy(ins[a], outs[a], ssem.at[a], rsem.at[a], sib) for a in range(n)]
        for cp in cps:
            cp.start()
        for cp in cps:
            cp.wait()

    return pl.pallas_call(
        body, name=name, in_specs=[ANY] * n, out_specs=[ANY] * n,
        out_shape=[jax.ShapeDtypeStruct(a.shape, a.dtype) for a in arrs],
        scratch_shapes=[pltpu.SemaphoreType.DMA((n,))] * 2,
    )(*arrs)


def _scatter4(name, arrs):
    n = len(arrs)

    def body(*refs):
        ins, outs = refs[:n], refs[n:2 * n]
        lsem, ssem, rsem = refs[2 * n:]
        x, y, c, me, sib, chips = _place()
        local = [pltpu.make_async_copy(ins[a].at[me], outs[a].at[me], lsem.at[a]) for a in range(n)]
        for cp in local:
            cp.start()
        cps = []
        for a in range(n):
            for j, (px, py) in enumerate(chips):
                cp = _rcopy(ins[a].at[2 * px + py], outs[a].at[me], ssem.at[3 * a + j], rsem.at[3 * a + j], (px, py, c))
                cp.start()
                cps.append(cp)
        for a in range(n):
            for j, (px, py) in enumerate(chips):
                blk = outs[a].at[2 * px + py]
                _rcopy(blk, blk, ssem.at[3 * a + j], rsem.at[3 * a + j], sib).wait_recv()
        for cp in cps:
            cp.wait_send()
        for cp in local:
            cp.wait()

    return pl.pallas_call(
        body, name=name, in_specs=[ANY] * n, out_specs=[ANY] * n,
        out_shape=[jax.ShapeDtypeStruct(a.shape, a.dtype) for a in arrs],
        scratch_shapes=[pltpu.SemaphoreType.DMA((n,))] + [pltpu.SemaphoreType.DMA((3 * n,))] * 2,
    )(*arrs)


def _join_halves(name, arrs, split):
    n = len(arrs)

    def body(*refs):
        ins, outs = refs[:n], refs[n:2 * n]
        lsem, ssem, rsem = refs[2 * n:]
        x, y, c, me, sib, chips = _place()
        cps, local = [], []
        for a in range(n):
            mine = _half(outs[a], 0, split[a], c, arrs[a].shape[split[a]])
            local.append(pltpu.make_async_copy(ins[a], mine, lsem.at[a]))
            local[-1].start()
            cp = _rcopy(ins[a], mine, ssem.at[a], rsem.at[a], sib)
            cp.start()
            cps.append(cp)
        for a in range(n):
            blk = _half(outs[a], 0, split[a], 1 - c, arrs[a].shape[split[a]])
            _rcopy(blk, blk, ssem.at[a], rsem.at[a], sib).wait_recv()
        for cp in cps:
            cp.wait_send()
        for cp in local:
            cp.wait()

    return pl.pallas_call(
        body, name=name, in_specs=[ANY] * n, out_specs=[ANY] * n,
        out_shape=[jax.ShapeDtypeStruct(tuple(2 * d if i == s else d for i, d in enumerate(a.shape)), a.dtype)
                   for a, s in zip(arrs, split)],
        scratch_shapes=[pltpu.SemaphoreType.DMA((n,))] * 3,
    )(*arrs)


def _allreduce_small(s):
    r, cdim = s.shape

    def body(s_ref, o_ref, buf, ssem, rsem):
        x, y, c, me, sib, chips = _place()
        me8 = 4 * x + 2 * y + c
        buf[me8] = s_ref[...]
        flips = [(fx, fy, fc) for fx in (0, 1) for fy in (0, 1) for fc in (0, 1)][1:]
        cps = []
        for k, (fx, fy, fc) in enumerate(flips):
            peer = (x ^ fx if fx else x, y ^ fy if fy else y, c ^ fc if fc else c)
            cp = _rcopy(s_ref, buf.at[me8], ssem.at[k], rsem.at[k], peer)
            cp.start()
            cps.append(cp)
        for k, (fx, fy, fc) in enumerate(flips):
            src = 4 * (x ^ fx if fx else x) + 2 * (y ^ fy if fy else y) + (c ^ fc if fc else c)
            _rcopy(s_ref, buf.at[src], ssem.at[k], rsem.at[k], sib).wait_recv()
        for cp in cps:
            cp.wait_send()
        acc = buf[0]
        for k in range(1, 8):
            acc = acc + buf[k]
        o_ref[...] = acc

    vm = pl.BlockSpec(memory_space=pltpu.VMEM)
    return pl.pallas_call(
        body, name="allreduce_small", in_specs=[vm], out_specs=vm, out_shape=jax.ShapeDtypeStruct((r, cdim), F32),
        scratch_shapes=[pltpu.VMEM((8, r, cdim), F32), pltpu.SemaphoreType.DMA((7,)), pltpu.SemaphoreType.DMA((7,))],
    )(s)


def _add_my_half(name, g, recv, c_idx, axis):
    nb, hr, hc = recv.shape
    tr = _pick(hr, (256, 176, 128, 64))
    if axis == 0:
        g4 = g.reshape(nb, 2, hr, hc)
        gspec = pl.BlockSpec((None, None, tr, hc), lambda b, i, c_ref: (b, c_ref[0], i, 0))
    else:
        g4 = g
        gspec = pl.BlockSpec((None, tr, hc), lambda b, i, c_ref: (b, i, c_ref[0]))

    def body(c_ref, g_ref, r_ref, o_ref, ob_ref):
        s = g_ref[...] + r_ref[...].astype(F32)
        o_ref[...] = s
        ob_ref[...] = s.astype(BF16)

    ospec = pl.BlockSpec((None, tr, hc), lambda b, i, c_ref: (b, i, 0))
    return pl.pallas_call(
        body, name=name,
        grid_spec=pltpu.PrefetchScalarGridSpec(
            num_scalar_prefetch=1, grid=(nb, hr // tr), in_specs=[gspec, ospec], out_specs=[ospec, ospec]),
        out_shape=[jax.ShapeDtypeStruct((nb, hr, hc), F32), jax.ShapeDtypeStruct((nb, hr, hc), BF16)],
        compiler_params=_cparams(("parallel", "parallel")),
    )(c_idx, g4, recv)


def _sum4(name, landed, own, me_idx):
    nb, h, cdim = landed.shape
    tr = _pick(h, (256, 176, 128, 64))

    def body(me_ref, a_ref, own_ref, o_ref):
        acc = None
        for k in range(nb):
            term = jnp.where(me_ref[0] == k, own_ref[...], a_ref[k].astype(F32))
            acc = term if acc is None else acc + term
        o_ref[...] = acc

    return pl.pallas_call(
        body, name=name,
        grid_spec=pltpu.PrefetchScalarGridSpec(
            num_scalar_prefetch=1, grid=(h // tr,),
            in_specs=[pl.BlockSpec((nb, tr, cdim), lambda i, me_ref: (0, i, 0)),
                      pl.BlockSpec((None, tr, cdim), lambda i, me_ref: (me_ref[0], i, 0))],
            out_specs=pl.BlockSpec((tr, cdim), lambda i, me_ref: (i, 0))),
        out_shape=jax.ShapeDtypeStruct((h, cdim), F32), compiler_params=_cparams(("parallel",)),
    )(me_idx, landed, own)


def _cast_other_half(name, g, c_idx, axis):
    nb, r, cdim = g.shape
    hr, hc = (r // 2, cdim) if axis == 0 else (r, cdim // 2)
    tr = _pick(hr, (256, 176, 128, 64))
    if axis == 0:
        g4 = g.reshape(nb, 2, hr, hc)
        gspec = pl.BlockSpec((None, None, tr, hc), lambda b, i, c_ref: (b, 1 - c_ref[0], i, 0))
    else:
        g4 = g
        gspec = pl.BlockSpec((None, tr, hc), lambda b, i, c_ref: (b, i, 1 - c_ref[0]))

    def body(c_ref, g_ref, o_ref):
        o_ref[...] = g_ref[...].astype(BF16)

    return pl.pallas_call(
        body, name=name,
        grid_spec=pltpu.PrefetchScalarGridSpec(
            num_scalar_prefetch=1, grid=(nb, hr // tr), in_specs=[gspec],
            out_specs=pl.BlockSpec((None, tr, hc), lambda b, i, c_ref: (b, i, 0))),
        out_shape=jax.ShapeDtypeStruct((nb, hr, hc), BF16), compiler_params=_cparams(("parallel", "parallel")),
    )(c_idx, g4)


def _adamw(name, w, g, m, v):
    c1 = 1.0 - ADAM_B1 ** ADAM_STEP
    c2 = 1.0 - ADAM_B2 ** ADAM_STEP

    def fn(w_t, g_t, m_t, v_t):
        m_n = ADAM_B1 * m_t + (1.0 - ADAM_B1) * g_t
        v_n = ADAM_B2 * v_t + (1.0 - ADAM_B2) * (g_t * g_t)
        delta = -ADAM_LR * ((m_n / c1) / (jnp.sqrt(v_n / c2) + ADAM_EPS) + ADAM_WD * w_t)
        return delta, m_n, v_n

    cdim = w.shape[1]
    return _rowwise(name, fn, [w, g, m, v], [], [(cdim, F32)] * 3, tm=_pick(w.shape[0], (256, 176, 128, 64, 8)))


BIG = ("ffn1_w_gate", "ffn1_w_up", "ffn1_w_down", "w_in", "w_out", "ffn2_w_gate", "ffn2_w_up", "ffn2_w_down",
       "w_ple_gate", "w_ple_proj")
SMALL = ("ffn1_norm", "mix_norm", "b_mlstm_gates", "b_fox_f", "mlstm_out_norm", "fox_out_norm", "ffn2_norm",
         "ple_gate_norm", "ple_proj_norm", "final_norm")
WEIGHTS = ("ffn1_norm", "ffn1_w_gate", "ffn1_w_up", "ffn1_w_down", "mix_norm", "w_in", "conv_qk", "b_mlstm_gates",
           "b_fox_f", "mlstm_out_norm", "fox_out_norm", "w_out", "ffn2_norm", "ffn2_w_gate", "ffn2_w_up", "ffn2_w_down",
           "ple_gate_norm", "w_ple_gate", "w_ple_proj", "ple_proj_norm", "final_norm")
TRANSPOSED = ("ffn1_w_gate", "ffn1_w_up", "w_in", "ffn2_w_gate", "ffn2_w_up")
PACK_W = 1024


def _chip_blocks(a):
    r, c4 = a.shape
    return a.reshape(r, 4, c4 // 4).transpose(1, 0, 2)


def _from_chip_blocks(a):
    nb, r, c = a.shape
    return a.transpose(1, 0, 2).reshape(r, nb * c)


def kernel(x, p, ffn1_norm, ffn1_w_gate, ffn1_w_up, ffn1_w_down, mix_norm, w_in, conv_qk, b_mlstm_gates, b_fox_f, mlstm_out_norm, fox_out_norm, w_out, ffn2_norm, ffn2_w_gate, ffn2_w_up, ffn2_w_down, ple_gate_norm, w_ple_gate, w_ple_proj, ple_proj_norm, final_norm, loss_target, m_ffn1_norm, m_ffn1_w_gate, m_ffn1_w_up, m_ffn1_w_down, m_mix_norm, m_w_in, m_conv_qk, m_b_mlstm_gates, m_b_fox_f, m_mlstm_out_norm, m_fox_out_norm, m_w_out, m_ffn2_norm, m_ffn2_w_gate, m_ffn2_w_up, m_ffn2_w_down, m_ple_gate_norm, m_w_ple_gate, m_w_ple_proj, m_ple_proj_norm, m_final_norm, v_ffn1_norm, v_ffn1_w_gate, v_ffn1_w_up, v_ffn1_w_down, v_mix_norm, v_w_in, v_conv_qk, v_b_mlstm_gates, v_b_fox_f, v_mlstm_out_norm, v_fox_out_norm, v_w_out, v_ffn2_norm, v_ffn2_w_gate, v_ffn2_w_up, v_ffn2_w_down, v_ple_gate_norm, v_w_ple_gate, v_w_ple_proj, v_ple_proj_norm, v_final_norm):
    w = dict(ffn1_norm=ffn1_norm, ffn1_w_gate=ffn1_w_gate, ffn1_w_up=ffn1_w_up, ffn1_w_down=ffn1_w_down, mix_norm=mix_norm,
             w_in=w_in, conv_qk=conv_qk, b_mlstm_gates=b_mlstm_gates, b_fox_f=b_fox_f, mlstm_out_norm=mlstm_out_norm,
             fox_out_norm=fox_out_norm, w_out=w_out, ffn2_norm=ffn2_norm, ffn2_w_gate=ffn2_w_gate, ffn2_w_up=ffn2_w_up,
             ffn2_w_down=ffn2_w_down, ple_gate_norm=ple_gate_norm, w_ple_gate=w_ple_gate, w_ple_proj=w_ple_proj,
             ple_proj_norm=ple_proj_norm, final_norm=final_norm)
    m = dict(ffn1_norm=m_ffn1_norm, ffn1_w_gate=m_ffn1_w_gate, ffn1_w_up=m_ffn1_w_up, ffn1_w_down=m_ffn1_w_down,
             mix_norm=m_mix_norm, w_in=m_w_in, conv_qk=m_conv_qk, b_mlstm_gates=m_b_mlstm_gates, b_fox_f=m_b_fox_f,
             mlstm_out_norm=m_mlstm_out_norm, fox_out_norm=m_fox_out_norm, w_out=m_w_out, ffn2_norm=m_ffn2_norm,
             ffn2_w_gate=m_ffn2_w_gate, ffn2_w_up=m_ffn2_w_up, ffn2_w_down=m_ffn2_w_down, ple_gate_norm=m_ple_gate_norm,
             w_ple_gate=m_w_ple_gate, w_ple_proj=m_w_ple_proj, ple_proj_norm=m_ple_proj_norm, final_norm=m_final_norm)
    v = dict(ffn1_norm=v_ffn1_norm, ffn1_w_gate=v_ffn1_w_gate, ffn1_w_up=v_ffn1_w_up, ffn1_w_down=v_ffn1_w_down,
             mix_norm=v_mix_norm, w_in=v_w_in, conv_qk=v_conv_qk, b_mlstm_gates=v_b_mlstm_gates, b_fox_f=v_b_fox_f,
             mlstm_out_norm=v_mlstm_out_norm, fox_out_norm=v_fox_out_norm, w_out=v_w_out, ffn2_norm=v_ffn2_norm,
             ffn2_w_gate=v_ffn2_w_gate, ffn2_w_up=v_ffn2_w_up, ffn2_w_down=v_ffn2_w_down, ple_gate_norm=v_ple_gate_norm,
             w_ple_gate=v_w_ple_gate, w_ple_proj=v_w_ple_proj, ple_proj_norm=v_ple_proj_norm, final_norm=v_final_norm)
    shapes = {n: w[n].shape for n in WEIGHTS}

    def view(a, n):
        return a[0].T if n in TRANSPOSED else a.reshape(-1, a.shape[-1])

    def unview(a, n):
        return (a.T if n in TRANSPOSED else a).reshape(shapes[n])

    w2, m2, v2 = ({n: view(a, n) for n, a in d.items()} for d in (w, m, v))

    shards = [w2[n].astype(BF16) for n in BIG] + [w2["conv_qk"]]
    split = [1 if n == "w_in" else 0 for n in BIG]
    full = dict(zip(BIG + ("conv_qk",), _gather4("gather_weights", shards, split + [None])))
    sp = {n: w2[n] for n in SMALL}
    loss_part, grad_x, gw, gs = _local_step(
        x[0], p[0, 0], loss_target[0], sp, full["ffn1_w_gate"], full["ffn1_w_up"], full["ffn1_w_down"],
        full["w_in"].reshape(-1, w_in.shape[1]), _from_chip_blocks(full["conv_qk"]), full["w_out"].reshape(-1, w_out.shape[-1]),
        full["ffn2_w_gate"], full["ffn2_w_up"], full["ffn2_w_down"], full["w_ple_gate"].reshape(-1, w_ple_gate.shape[-1]),
        _from_chip_blocks(full["w_ple_proj"]))
    loss = lax.psum(loss_part[0, 0], ("x", "y", "c"))

    gw["w_ple_proj"] = _chip_blocks(gw["w_ple_proj"])
    for n in ("w_in", "w_out", "w_ple_gate"):
        gw[n] = gw[n].reshape(4, -1, gw[n].shape[-1])
    blocks = [gw[n] for n in BIG]
    c_idx = lax.axis_index("c").astype(jnp.int32).reshape(1)
    me_idx = (2 * lax.axis_index("x") + lax.axis_index("y")).astype(jnp.int32).reshape(1)
    wire = [_cast_other_half("rs_cast_" + n, g, c_idx, s) for n, g, s in zip(BIG, blocks, split)]
    swapped = _swap("rs_swap", wire)
    partial = [_add_my_half("rs_add_" + n, g, r, c_idx, s) for n, g, r, s in zip(BIG, blocks, swapped, split)]
    landed = _scatter4("rs_scatter", [pb for _, pb in partial])
    halves = [_sum4("rs_sum_" + n, a, pf, me_idx) for n, a, (pf, _) in zip(BIG, landed, partial)]
    grads = dict(zip(BIG, _join_halves("rs_join", halves, split)))

    small = [gs[n].reshape(1, -1) for n in SMALL] + [gw["conv_qk"]]
    rows = [jnp.pad(a, ((0, 0), (0, PACK_W - a.shape[1]))) for a in small]
    packed = jnp.concatenate(rows, axis=0)
    packed = jnp.pad(packed, ((0, -packed.shape[0] % 8), (0, 0)))
    red = _allreduce_small(packed)
    for i, n in enumerate(SMALL):
        grads[n] = red[i:i + 1, :gs[n].size]
    dconv = red[len(SMALL):len(SMALL) + CONV_W, :gw["conv_qk"].shape[1]]
    cw = conv_qk.shape[-1]
    grads["conv_qk"] = lax.dynamic_slice_in_dim(dconv, (2 * lax.axis_index("x") + lax.axis_index("y")) * cw, cw, axis=1)

    outs = {}
    for n in WEIGHTS:
        g2 = grads[n].reshape(w2[n].shape)
        d, nm, nv = _adamw("adamw_" + n, w2[n], g2, m2[n], v2[n])
        outs[n] = tuple(unview(a, n) for a in (g2, d, nm, nv))
    return (loss, grad_x[None], *[outs[n][0] for n in WEIGHTS], *[outs[n][1] for n in WEIGHTS],
            *[outs[n][2] for n in WEIGHTS], *[outs[n][3] for n in WEIGHTS])
```

```python
import functools
import math

import jax
import jax.numpy as jnp
from jax import lax
from jax.experimental import pallas as pl
from jax.experimental.pallas import tpu as pltpu

F32 = jnp.float32
BF16 = jnp.bfloat16
EPS = 1e-6
NH_M, DK_M, DV_M = 4, 64, 128
NH_F, DH_F = 8, 64
CONV_W = 4
ADAM_LR, ADAM_B1, ADAM_B2, ADAM_EPS, ADAM_WD, ADAM_STEP = 0.001, 0.9, 0.999, 1e-08, 0.01, 10
VMEM_LIMIT = 56 * 1024 * 1024


def _cparams(sem):
    return pltpu.CompilerParams(dimension_semantics=sem, vmem_limit_bytes=VMEM_LIMIT)


def _sigmoid(x):
    return 1.0 / (1.0 + jnp.exp(-x))


def _dot(a, b, ca, cb):
    return lax.dot_general(a.astype(BF16), b.astype(BF16), (((ca,), (cb,)), ((), ())), preferred_element_type=F32)


def _rowwise(name, fn, tiled, full, outs, accs=(), tm=256):
    rows = tiled[0].shape[0]
    tm = min(tm, rows)
    assert rows % tm == 0
    n_t, n_f, n_o, n_a = len(tiled), len(full), len(outs), len(accs)

    def body(*refs):
        ins = [r[...] for r in refs[: n_t + n_f]]
        res = fn(*ins)
        if not isinstance(res, (tuple, list)):
            res = (res,)
        orefs = refs[n_t + n_f:]
        for r, v in zip(orefs[:n_o], res[:n_o]):
            r[...] = v.astype(r.dtype)
        if n_a:
            @pl.when(pl.program_id(0) == 0)
            def _():
                for r in orefs[n_o:]:
                    r[...] = jnp.zeros_like(r)
            for r, v in zip(orefs[n_o:], res[n_o:]):
                r[...] += v.astype(r.dtype)

    in_specs = [pl.BlockSpec((tm, a.shape[1]), lambda i: (i, 0)) for a in tiled]
    in_specs += [pl.BlockSpec(a.shape, lambda i: (0, 0)) for a in full]
    out_specs = [pl.BlockSpec((tm, c), lambda i: (i, 0)) for c, _ in outs]
    out_specs += [pl.BlockSpec(s, lambda i: (0, 0)) for s, _ in accs]
    out_shape = [jax.ShapeDtypeStruct((rows, c), d) for c, d in outs]
    out_shape += [jax.ShapeDtypeStruct(s, d) for s, d in accs]
    res = pl.pallas_call(
        body, name=name, grid=(rows // tm,), in_specs=in_specs, out_specs=out_specs, out_shape=out_shape,
        compiler_params=_cparams(("arbitrary",) if n_a else ("parallel",)),
    )(*tiled, *full)
    return res


def _colsum(v):
    return jnp.sum(v, axis=0, keepdims=True)


def _rms_fwd_val(x, g):
    r = lax.rsqrt(jnp.mean(x * x, axis=-1, keepdims=True) + EPS)
    return x * r * g


def _rms_bwd_val(dy, x, g):
    r = lax.rsqrt(jnp.mean(x * x, axis=-1, keepdims=True) + EPS)
    xh = x * r
    dxh = dy * g
    dx = r * (dxh - xh * jnp.mean(dxh * xh, axis=-1, keepdims=True))
    return dx, _colsum(dy * xh)


def _mm(name, pairs, out_shape, out_block, out_map, grid, kaxis, ta=False, tb=False, scale=None, res=None,
        out_dtype=F32):
    nk = grid[kaxis]
    npairs = len(pairs)
    ca, cb = (0 if ta else 1), (1 if tb else 0)
    acc_shape = tuple(d for d in out_block if d is not None)

    def body(*refs):
        in_refs = refs[: 2 * npairs]
        res_ref = refs[2 * npairs] if res is not None else None
        o_ref = refs[2 * npairs + (1 if res is not None else 0)]
        acc_ref = refs[-1]
        k = pl.program_id(kaxis)

        @pl.when(k == 0)
        def _():
            acc_ref[...] = jnp.zeros_like(acc_ref)

        part = None
        for p in range(npairs):
            d = _dot(in_refs[2 * p][...], in_refs[2 * p + 1][...], ca, cb)
            part = d if part is None else part + d
        acc_ref[...] += part

        @pl.when(k == nk - 1)
        def _():
            v = acc_ref[...]
            if scale is not None:
                v = v * scale
            if res_ref is not None:
                v = v + res_ref[...].astype(F32)
            o_ref[...] = v.astype(o_ref.dtype)

    in_specs, args = [], []
    for a, ab, am, b, bb, bm in pairs:
        in_specs += [pl.BlockSpec(ab, am), pl.BlockSpec(bb, bm)]
        args += [a, b]
    if res is not None:
        in_specs.append(pl.BlockSpec(out_block, out_map))
        args.append(res)
    sem = tuple("arbitrary" if i == kaxis else "parallel" for i in range(len(grid)))
    return pl.pallas_call(
        body, name=name, grid=grid, in_specs=in_specs, out_specs=pl.BlockSpec(out_block, out_map),
        out_shape=jax.ShapeDtypeStruct(out_shape, out_dtype), scratch_shapes=[pltpu.VMEM(acc_shape, F32)],
        compiler_params=_cparams(sem),
    )(*args)


def _pick(n, pref):
    for t in pref:
        if n % t == 0:
            return t
    return n


def _mm_nn(name, a, b, tm=512, tn=512, tk=512, **kw):
    (m, k), n = a.shape, b.shape[1]
    tm, tn, tk = _pick(m, (tm, 256, 128)), _pick(n, (tn, 256, 128)), _pick(k, (tk, 256, 128))
    return _mm(name, [(a, (tm, tk), lambda i, j, kk: (i, kk), b, (tk, tn), lambda i, j, kk: (kk, j))],
               (m, n), (tm, tn), lambda i, j, kk: (i, j), (m // tm, n // tn, k // tk), 2, **kw)


def _mm_nt(name, a, b, tm=512, tn=512, tk=512, **kw):
    (m, k), n = a.shape, b.shape[0]
    tm, tn, tk = _pick(m, (tm, 256, 128)), _pick(n, (tn, 256, 128)), _pick(k, (tk, 256, 128))
    return _mm(name, [(a, (tm, tk), lambda i, j, kk: (i, kk), b, (tn, tk), lambda i, j, kk: (j, kk))],
               (m, n), (tm, tn), lambda i, j, kk: (i, j), (m // tm, n // tn, k // tk), 2, tb=True, **kw)


def _mm_tn(name, a, b, tm=512, tn=512, tk=512, **kw):
    (k, m), n = a.shape, b.shape[1]
    tm, tn, tk = _pick(m, (tm, 256, 128)), _pick(n, (tn, 256, 128)), _pick(k, (tk, 256, 128))
    return _mm(name, [(a, (tk, tm), lambda i, j, kk: (kk, i), b, (tk, tn), lambda i, j, kk: (kk, j))],
               (m, n), (tm, tn), lambda i, j, kk: (i, j), (m // tm, n // tn, k // tk), 2, ta=True, **kw)


def _norm_mm(name, h, gamma, w, w_transposed, out_dtype):
    t, d = h.shape
    n = w.shape[0] if w_transposed else w.shape[1]
    tm, tn = _pick(t, (512, 256)), _pick(n, (1024, 512, 256, 128))

    def body(h_ref, gam_ref, w_ref, xn_ref, o_ref, xn_scr):
        @pl.when(pl.program_id(1) == 0)
        def _():
            xn = _rms_fwd_val(h_ref[...], gam_ref[...]).astype(BF16)
            xn_scr[...] = xn
            xn_ref[...] = xn

        o_ref[...] = _dot(xn_scr[...], w_ref[...], 1, 1 if w_transposed else 0).astype(o_ref.dtype)

    wspec = pl.BlockSpec((tn, d), lambda i, j: (j, 0)) if w_transposed else pl.BlockSpec((d, tn), lambda i, j: (0, j))
    return pl.pallas_call(
        body, name=name, grid=(t // tm, n // tn),
        in_specs=[pl.BlockSpec((tm, d), lambda i, j: (i, 0)), pl.BlockSpec((1, d), lambda i, j: (0, 0)), wspec],
        out_specs=[pl.BlockSpec((tm, d), lambda i, j: (i, 0)), pl.BlockSpec((tm, tn), lambda i, j: (i, j))],
        out_shape=[jax.ShapeDtypeStruct((t, d), BF16), jax.ShapeDtypeStruct((t, n), out_dtype)],
        scratch_shapes=[pltpu.VMEM((tm, d), BF16)], compiler_params=_cparams(("parallel", "arbitrary")),
    )(h, gamma, w)


def _ffn_fwd(pfx, h, gamma, wg, wu, wd):
    t, d = h.shape
    nb, f, _ = wg.shape
    tm = _pick(t, (512, 256))

    def body(h_ref, gam_ref, wg_ref, wu_ref, wd_ref, ho_ref, xn_ref, g_ref, u_ref, xn_scr, acc_ref):
        j = pl.program_id(1)

        @pl.when(j == 0)
        def _():
            xn = _rms_fwd_val(h_ref[...], gam_ref[...]).astype(BF16)
            xn_scr[...] = xn
            xn_ref[...] = xn
            acc_ref[...] = jnp.zeros_like(acc_ref)

        x = xn_scr[...]
        g = _dot(x, wg_ref[...], 1, 1)
        u = _dot(x, wu_ref[...], 1, 1)
        g_ref[...] = g.astype(BF16)
        u_ref[...] = u.astype(BF16)
        acc_ref[...] += _dot(g * _sigmoid(g) * u, wd_ref[...], 1, 0)

        @pl.when(j == nb - 1)
        def _():
            ho_ref[...] = h_ref[...] + 0.5 * acc_ref[...]

    row = pl.BlockSpec((tm, d), lambda i, j: (i, 0))
    blk = pl.BlockSpec((None, tm, f), lambda i, j: (j, i, 0))
    wspec = pl.BlockSpec((None, f, d), lambda i, j: (j, 0, 0))
    return pl.pallas_call(
        body, name=pfx + "_fwd", grid=(t // tm, nb),
        in_specs=[row, pl.BlockSpec((1, d), lambda i, j: (0, 0)), wspec, wspec, wspec], out_specs=[row, row, blk, blk],
        out_shape=[jax.ShapeDtypeStruct((t, d), F32), jax.ShapeDtypeStruct((t, d), BF16),
                   jax.ShapeDtypeStruct((nb, t, f), BF16), jax.ShapeDtypeStruct((nb, t, f), BF16)],
        scratch_shapes=[pltpu.VMEM((tm, d), BF16), pltpu.VMEM((tm, d), F32)],
        compiler_params=_cparams(("parallel", "arbitrary")),
    )(h, gamma, wg, wu, wd)


def _ffn_bwd(pfx, dh_out, h, gamma, xn, g_all, u_all, wg, wu, wd):
    t, d = h.shape
    nb, f, _ = wg.shape
    tm = _pick(t, (512, 256))
    tk = _pick(t, (512, 256))

    def body(dy_ref, h_ref, gam_ref, wg_ref, wu_ref, wd_ref, g_ref, u_ref, dh_ref, dgam_ref, dg_ref, du_ref, a_ref, acc_ref):
        i, j = pl.program_id(0), pl.program_id(1)

        @pl.when((i == 0) & (j == 0))
        def _():
            dgam_ref[...] = jnp.zeros_like(dgam_ref)

        @pl.when(j == 0)
        def _():
            acc_ref[...] = jnp.zeros_like(acc_ref)

        da = _dot(dy_ref[...], wd_ref[...], 1, 1) * 0.5
        g = g_ref[...].astype(F32)
        u = u_ref[...].astype(F32)
        s = _sigmoid(g)
        sl = g * s
        du = (da * sl).astype(BF16)
        dg = (da * u * (s * (1.0 + g * (1.0 - s)))).astype(BF16)
        du_ref[...] = du
        dg_ref[...] = dg
        a_ref[...] = (sl * u).astype(BF16)
        acc_ref[...] += _dot(dg, wg_ref[...], 1, 0) + _dot(du, wu_ref[...], 1, 0)

        @pl.when(j == nb - 1)
        def _():
            dx, dgam = _rms_bwd_val(acc_ref[...], h_ref[...], gam_ref[...])
            dh_ref[...] = dy_ref[...] + dx
            dgam_ref[...] += dgam

    row = pl.BlockSpec((tm, d), lambda i, j: (i, 0))
    vec = pl.BlockSpec((1, d), lambda i, j: (0, 0))
    blk = pl.BlockSpec((None, tm, f), lambda i, j: (j, i, 0))
    wspec = pl.BlockSpec((None, f, d), lambda i, j: (j, 0, 0))
    dh, dgamma, dg_all, du_all, a_all = pl.pallas_call(
        body, name=pfx + "_bwd", grid=(t // tm, nb),
        in_specs=[row, row, vec, wspec, wspec, wspec, blk, blk], out_specs=[row, vec, blk, blk, blk],
        out_shape=[jax.ShapeDtypeStruct((t, d), F32), jax.ShapeDtypeStruct((1, d), F32)]
        + [jax.ShapeDtypeStruct((nb, t, f), BF16)] * 3,
        scratch_shapes=[pltpu.VMEM((tm, d), F32)], compiler_params=_cparams(("arbitrary", "arbitrary")),
    )(dh_out, h, gamma, wg, wu, wd, g_all, u_all)

    xmap, bmap, omap = (lambda b, k: (k, 0)), (lambda b, k: (b, k, 0)), (lambda b, k: (b, 0, 0))
    dwg = _mm(pfx + "_dwg", [(dg_all, (None, tk, f), bmap, xn, (tk, d), xmap)], (nb, f, d), (None, f, d), omap,
              (nb, t // tk), 1, ta=True)
    dwu = _mm(pfx + "_dwu", [(du_all, (None, tk, f), bmap, xn, (tk, d), xmap)], (nb, f, d), (None, f, d), omap,
              (nb, t // tk), 1, ta=True)
    dwd = _mm(pfx + "_dwd", [(a_all, (None, tk, f), bmap, dh_out, (tk, d), xmap)], (nb, f, d), (None, f, d), omap,
              (nb, t // tk), 1, ta=True, scale=0.5)
    return dh, dgamma, dwg, dwu, dwd


HALO = 16


def _silu_grad(y):
    s = _sigmoid(y)
    return s * (1.0 + y * (1.0 - s))


def _with_halo(ref, i, n_tiles, tm, before, after):
    t = ref.shape[0]
    r0 = pl.multiple_of(i * tm, tm)
    parts = [ref[pl.ds(r0, tm), :].astype(F32)]
    if before:
        prev = ref[pl.ds(pl.multiple_of(jnp.maximum(r0 - HALO, 0), HALO), HALO), :].astype(F32)
        parts.insert(0, jnp.where(i > 0, prev, 0.0))
    if after:
        nxt = ref[pl.ds(pl.multiple_of(jnp.minimum(r0 + tm, t - HALO), HALO), HALO), :].astype(F32)
        parts.append(jnp.where(i < n_tiles - 1, nxt, 0.0))
    return jnp.concatenate(parts, axis=0)


def _conv_fwd(zbig, w):
    t, c = zbig.shape[0], w.shape[1]
    tm = _pick(t, (512, 256))
    nt = t // tm

    def body(x_ref, w_ref, o_ref):
        xe = _with_halo(x_ref, pl.program_id(0), nt, tm, True, False)
        wv = w_ref[...]
        y = xe * wv[3:4, :]
        for i in range(CONV_W - 1):
            y = y + pltpu.roll(xe, CONV_W - 1 - i, 0) * wv[i:i + 1, :]
        y = y[HALO:, :]
        o_ref[...] = (y * _sigmoid(y)).astype(o_ref.dtype)

    return pl.pallas_call(
        body, name="conv_fwd", grid=(nt,),
        in_specs=[pl.BlockSpec((t, c), lambda i: (0, 0)), pl.BlockSpec(w.shape, lambda i: (0, 0))],
        out_specs=pl.BlockSpec((tm, c), lambda i: (i, 0)), out_shape=jax.ShapeDtypeStruct((t, c), BF16),
        compiler_params=_cparams(("parallel",)),
    )(zbig, w)


def _conv_bwd(zbig, dact, w):
    t, c = dact.shape
    tm = _pick(t, (512, 256))
    nt = t // tm
    n = tm + HALO

    def body(x_ref, d_ref, w_ref, dx_ref, dw_ref):
        xe = _with_halo(x_ref, pl.program_id(0), nt, tm, True, True)
        de = _with_halo(d_ref, pl.program_id(0), nt, tm, False, True)
        wv = w_ref[...]
        sh = [pltpu.roll(xe, CONV_W - 1 - i, 0)[HALO:, :] if i < CONV_W - 1 else xe[HALO:, :] for i in range(CONV_W)]
        y = sh[0] * wv[0:1, :]
        for i in range(1, CONV_W):
            y = y + sh[i] * wv[i:i + 1, :]
        dy = de * _silu_grad(y)
        dx = dy * wv[3:4, :]
        for i in range(CONV_W - 1):
            dx = dx + pltpu.roll(dy, n - (CONV_W - 1 - i), 0) * wv[i:i + 1, :]
        dx_ref[...] = dx[:tm, :].astype(dx_ref.dtype)
        dyc = dy[:tm, :]
        dwp = jnp.concatenate([_colsum(dyc * sh[i][:tm, :]) for i in range(CONV_W)], axis=0)

        @pl.when(pl.program_id(0) == 0)
        def _():
            dw_ref[...] = jnp.zeros_like(dw_ref)
        dw_ref[...] += dwp

    return pl.pallas_call(
        body, name="conv_bwd", grid=(nt,),
        in_specs=[pl.BlockSpec((t, c), lambda i: (0, 0)), pl.BlockSpec((t, c), lambda i: (0, 0)),
                  pl.BlockSpec(w.shape, lambda i: (0, 0))],
        out_specs=[pl.BlockSpec((tm, c), lambda i: (i, 0)), pl.BlockSpec(w.shape, lambda i: (0, 0))],
        out_shape=[jax.ShapeDtypeStruct((t, c), BF16), jax.ShapeDtypeStruct(w.shape, F32)],
        compiler_params=_cparams(("arbitrary",)),
    )(zbig, dact, w)


LM = 256
HI = lax.Precision.HIGHEST


def _logsig(x):
    return jnp.minimum(x, 0.0) - jnp.log(1.0 + jnp.exp(-jnp.abs(x)))


def _tri(n, lower):
    r = lax.broadcasted_iota(jnp.int32, (n, n), 0)
    c = lax.broadcasted_iota(jnp.int32, (n, n), 1)
    return (r >= c) if lower else (r <= c)


def _f32dot(a, b):
    return lax.dot_general(a, b, (((1,), (0,)), ((), ())), precision=HI, preferred_element_type=F32)


def _mlstm_chunk(h, q_ref, k_ref, v_ref, zs_ref, zsr_ref, bc_ref, br_ref, c_prev, m_prev):
    l = LM
    q = q_ref[:, h * DK_M:(h + 1) * DK_M].astype(F32) * (DK_M ** -0.5)
    k = k_ref[:, h * DK_M:(h + 1) * DK_M]
    v = v_ref[:, h * DV_M:(h + 1) * DV_M]
    lane = lax.broadcasted_iota(jnp.int32, (l, DV_M), 1)
    v1 = jnp.concatenate([v, (lane == 0).astype(v.dtype)], axis=1)
    zs, zsr = zs_ref[...], zsr_ref[...]
    li_c = zs[:, h:h + 1] + bc_ref[:, h:h + 1]
    fp_c = zs[:, NH_M + h:NH_M + h + 1] + bc_ref[:, NH_M + h:NH_M + h + 1]
    li_r = zsr[h:h + 1, :] + br_ref[h:h + 1, :]
    fp_r = zsr[NH_M + h:NH_M + h + 1, :] + br_ref[NH_M + h:NH_M + h + 1, :]
    lf_c, lf_r = _logsig(fp_c), _logsig(fp_r)
    low = _tri(l, True)
    b_c = _f32dot(low.astype(F32), lf_c)
    b_r = _f32dot(lf_r, _tri(l, False).astype(F32))
    g = b_r[:, l - 1:l]
    dmat = jnp.where(low, b_c - b_r + li_r, -jnp.inf)
    inter = b_c + m_prev
    m_t = jnp.maximum(inter, jnp.max(dmat, axis=1, keepdims=True))
    w_inter = jnp.exp(inter - m_t)
    amat = jnp.exp(dmat - m_t)
    s = _dot(q, k, 1, 1)
    p = amat * s
    qc = _dot(q, c_prev, 1, 0)
    qc_w = w_inter * qc
    num1 = qc_w + _dot(p, v1, 1, 0)
    den = num1[:, DV_M:DV_M + 1]
    mx = jnp.maximum(jnp.abs(den), jnp.exp(-m_t))
    hh = num1[:, :DV_M] / mx
    a_c = g - b_c + li_c
    return dict(q=q, k=k, v1=v1, fp_c=fp_c, fp_r=fp_r, b_c=b_c, g=g, m_t=m_t, w_inter=w_inter, amat=amat, s=s, p=p,
                qc_w=qc_w, den=den, mx=mx, hh=hh, a_c=a_c)


def _mlstm_fwd(qk, zbig, zs, zsr, bc, br, gm):
    t = zs.shape[0]
    l = LM
    nc = t // l
    dm = NH_M * DV_M

    def body(q_ref, k_ref, v_ref, o_ref, zs_ref, zsr_ref, bc_ref, br_ref, gm_ref, y_ref, cst_ref, mst_ref, c_scr, m_scr):
        @pl.when(pl.program_id(0) == 0)
        def _():
            c_scr[...] = jnp.zeros_like(c_scr)
            m_scr[...] = jnp.zeros_like(m_scr)

        cst_ref[...] = c_scr[...]
        mst_ref[...] = m_scr[...]
        ys = []
        for h in range(NH_M):
            c_prev = c_scr[h]
            m_prev = m_scr[h:h + 1, 0:1]
            r = _mlstm_chunk(h, q_ref, k_ref, v_ref, zs_ref, zsr_ref, bc_ref, br_ref, c_prev, m_prev)
            hh = r["hh"]
            gh = gm_ref[:, h * DV_M:(h + 1) * DV_M]
            hn = hh * lax.rsqrt(jnp.mean(hh * hh, axis=-1, keepdims=True) + EPS) * gh
            og = o_ref[:, h * DV_M:(h + 1) * DV_M].astype(F32)
            ys.append(hn * _sigmoid(og))
            m_new = jnp.maximum(r["g"] + m_prev, jnp.max(r["a_c"], axis=0, keepdims=True))
            decay = jnp.exp(r["g"] + m_prev - m_new)
            wk = r["k"].astype(F32) * jnp.exp(r["a_c"] - m_new)
            c_scr[h] = decay * c_prev + _dot(wk, r["v1"], 0, 0)
            m_scr[h:h + 1, :] = jnp.broadcast_to(m_new, (1, 128))
        y_ref[...] = jnp.concatenate(ys, axis=1).astype(y_ref.dtype)

    return pl.pallas_call(
        body, name="mlstm_fwd", grid=(nc,),
        in_specs=[pl.BlockSpec((l, NH_M * DK_M), lambda i: (i, 0)), pl.BlockSpec((l, NH_M * DK_M), lambda i: (i, 1)),
                  pl.BlockSpec((l, dm), lambda i: (i, 1)), pl.BlockSpec((l, dm), lambda i: (i, 2)),
                  pl.BlockSpec((l, 128), lambda i: (i, 0)), pl.BlockSpec((8, l), lambda i: (0, i)),
                  pl.BlockSpec((1, 8), lambda i: (0, 0)), pl.BlockSpec((8, 1), lambda i: (0, 0)),
                  pl.BlockSpec((1, dm), lambda i: (0, 0))],
        out_specs=[pl.BlockSpec((l, dm), lambda i: (i, 0)), pl.BlockSpec((None, NH_M, DK_M, 2 * DV_M), lambda i: (i, 0, 0, 0)),
                   pl.BlockSpec((None, 8, 128), lambda i: (i, 0, 0))],
        out_shape=[jax.ShapeDtypeStruct((t, dm), BF16), jax.ShapeDtypeStruct((nc, NH_M, DK_M, 2 * DV_M), F32),
                   jax.ShapeDtypeStruct((nc, 8, 128), F32)],
        scratch_shapes=[pltpu.VMEM((NH_M, DK_M, 2 * DV_M), F32), pltpu.VMEM((8, 128), F32)],
        compiler_params=_cparams(("arbitrary",)),
    )(qk, qk, zbig, zbig, zs, zsr, bc, br, gm)


def _mlstm_bwd(qk, zbig, zs, zsr, bc, br, gm, cst, mst, dycat):
    t = zs.shape[0]
    l = LM
    nc = t // l
    dm = NH_M * DV_M

    def body(q_ref, k_ref, v_ref, o_ref, zs_ref, zsr_ref, bc_ref, br_ref, gm_ref, cst_ref, mst_ref, cnx_ref, mnx_ref,
             dy_ref, dqk_ref, dv_ref, do_ref, dzs_ref, dzr_ref, dgm_ref, dc_scr):
        @pl.when(pl.program_id(0) == 0)
        def _():
            dc_scr[...] = jnp.zeros_like(dc_scr)
            dgm_ref[...] = jnp.zeros_like(dgm_ref)

        lane = lax.broadcasted_iota(jnp.int32, (l, 128), 1)
        upper = _tri(l, False).astype(F32)
        lower = _tri(l, True).astype(F32)
        dzr_rows = [None] * 8
        dvs, dos, dgs, dqs, dks = [], [], [], [], []
        dzs = jnp.zeros((l, 128), F32)
        for h in range(NH_M):
            c_prev = cst_ref[h]
            m_prev = mst_ref[h:h + 1, 0:1]
            r = _mlstm_chunk(h, q_ref, k_ref, v_ref, zs_ref, zsr_ref, bc_ref, br_ref, c_prev, m_prev)
            hh, mx, den, m_t, v1, amat = r["hh"], r["mx"], r["den"], r["m_t"], r["v1"], r["amat"]
            gh = gm_ref[:, h * DV_M:(h + 1) * DV_M]
            rs = lax.rsqrt(jnp.mean(hh * hh, axis=-1, keepdims=True) + EPS)
            xh = hh * rs
            sg = _sigmoid(o_ref[:, h * DV_M:(h + 1) * DV_M].astype(F32))
            dyh = dy_ref[:, h * DV_M:(h + 1) * DV_M]
            dos.append(dyh * xh * gh * sg * (1.0 - sg))
            dhn = dyh * sg
            dgs.append(_colsum(dhn * xh))
            dxh = dhn * gh
            dh = rs * (dxh - xh * jnp.mean(dxh * xh, axis=-1, keepdims=True))
            g1 = dh / mx
            hd = jnp.sum(hh * dh, axis=-1, keepdims=True)
            dden = jnp.where(jnp.abs(den) > jnp.exp(-m_t), -hd / mx * jnp.sign(den), 0.0)
            g256 = jnp.concatenate([g1, jnp.where(lane == 0, dden, 0.0)], axis=1)
            dc_h = dc_scr[h]
            ea = jnp.exp(r["a_c"])
            dp = _dot(g256, v1, 1, 1)
            ds = dp * amat
            dqs.append((r["w_inter"] * _dot(g256, c_prev, 1, 1) + _dot(ds, r["k"], 1, 0)) * (DK_M ** -0.5))
            dks.append(_dot(ds, r["q"], 0, 0) + ea * _dot(v1, dc_h, 1, 1))
            dv_st = ea * _dot(r["k"], dc_h, 1, 0)
            dv1 = _dot(r["p"], g256, 0, 0) + dv_st
            dvs.append(dv1[:, :DV_M])
            wmat = dp * r["p"]
            c_in = _colsum(wmat)
            c_st = jnp.sum(v1.astype(F32) * dv_st, axis=-1, keepdims=True)
            r_t = jnp.sum(wmat, axis=1, keepdims=True) + jnp.sum(g256 * r["qc_w"], axis=-1, keepdims=True)
            db = r_t - c_st
            carry = jnp.exp(mnx_ref[h:h + 1, 0:1]) * jnp.sum(
                jnp.sum(dc_h * cnx_ref[h], axis=1, keepdims=True), axis=0, keepdims=True)
            dlf_c = _f32dot(upper, db) + carry
            dlf_r = -_f32dot(c_in, lower)
            dfp = dlf_c * _sigmoid(-r["fp_c"])
            dzs = dzs + jnp.where(lane == h, c_st, 0.0) + jnp.where(lane == NH_M + h, dfp, 0.0)
            dzr_rows[h] = c_in
            dzr_rows[NH_M + h] = dlf_r * _sigmoid(-r["fp_r"])
            wq = r["q"] * jnp.exp(r["b_c"] - m_t)
            dc_scr[h] = jnp.exp(r["g"]) * dc_h + _dot(wq, g256, 0, 0)
        dqk_ref[...] = jnp.concatenate(dqs + dks, axis=1)
        dv_ref[...] = jnp.concatenate(dvs, axis=1).astype(dv_ref.dtype)
        do_ref[...] = jnp.concatenate(dos, axis=1).astype(do_ref.dtype)
        dzs_ref[...] = dzs
        dzr_ref[...] = jnp.concatenate(dzr_rows, axis=0)
        dgm_ref[...] += jnp.concatenate(dgs, axis=1)

    rev = lambda i: nc - 1 - i
    nxt = lambda i: jnp.minimum(nc - i, nc - 1)
    return pl.pallas_call(
        body, name="mlstm_bwd", grid=(nc,),
        in_specs=[pl.BlockSpec((l, NH_M * DK_M), lambda i: (rev(i), 0)), pl.BlockSpec((l, NH_M * DK_M), lambda i: (rev(i), 1)),
                  pl.BlockSpec((l, dm), lambda i: (rev(i), 1)), pl.BlockSpec((l, dm), lambda i: (rev(i), 2)),
                  pl.BlockSpec((l, 128), lambda i: (rev(i), 0)), pl.BlockSpec((8, l), lambda i: (0, rev(i))),
                  pl.BlockSpec((1, 8), lambda i: (0, 0)), pl.BlockSpec((8, 1), lambda i: (0, 0)),
                  pl.BlockSpec((1, dm), lambda i: (0, 0)),
                  pl.BlockSpec((None, NH_M, DK_M, 2 * DV_M), lambda i: (rev(i), 0, 0, 0)),
                  pl.BlockSpec((None, 8, 128), lambda i: (rev(i), 0, 0)),
                  pl.BlockSpec((None, NH_M, DK_M, 2 * DV_M), lambda i: (nxt(i), 0, 0, 0)),
                  pl.BlockSpec((None, 8, 128), lambda i: (nxt(i), 0, 0)),
                  pl.BlockSpec((l, dm), lambda i: (rev(i), 0))],
        out_specs=[pl.BlockSpec((l, dm), lambda i: (rev(i), 0)),
                   pl.BlockSpec((l, dm), lambda i: (rev(i), 0)), pl.BlockSpec((l, dm), lambda i: (rev(i), 0)),
                   pl.BlockSpec((l, 128), lambda i: (rev(i), 0)), pl.BlockSpec((8, l), lambda i: (0, rev(i))),
                   pl.BlockSpec((1, dm), lambda i: (0, 0))],
        out_shape=[jax.ShapeDtypeStruct((t, dm), F32),
                   jax.ShapeDtypeStruct((t, dm), BF16), jax.ShapeDtypeStruct((t, dm), BF16),
                   jax.ShapeDtypeStruct((t, 128), F32), jax.ShapeDtypeStruct((8, t), F32),
                   jax.ShapeDtypeStruct((1, dm), F32)],
        scratch_shapes=[pltpu.VMEM((NH_M, DK_M, 2 * DV_M), F32)],
        compiler_params=_cparams(("arbitrary",)),
    )(qk, qk, zbig, zbig, zs, zsr, bc, br, gm, cst, mst, cst, mst, dycat)


def _fox_cumsum(zsr, bf_r):
    t = zsr.shape[1]
    cw = _pick(t, (512, 256))

    def body(z_ref, b_ref, c_ref):
        up = _tri(cw, False).astype(F32)
        carry = jnp.zeros((NH_F, 1), F32)
        for j in range(t // cw):
            cs = _f32dot(_logsig(z_ref[:, j * cw:(j + 1) * cw] + b_ref[...]), up) + carry
            c_ref[:, j * cw:(j + 1) * cw] = cs
            carry = cs[:, cw - 1:cw]

    return pl.pallas_call(
        body, name="fox_cumsum", grid=(1,),
        in_specs=[pl.BlockSpec((NH_F, t), lambda i: (1, 0)), pl.BlockSpec((NH_F, 1), lambda i: (0, 0))],
        out_specs=pl.BlockSpec((NH_F, t), lambda i: (0, 0)), out_shape=jax.ShapeDtypeStruct((NH_F, t), F32),
        compiler_params=_cparams(("arbitrary",)),
    )(zsr, bf_r)


def _fox_gate_bwd(zsr, bf_r, dc):
    t = zsr.shape[1]
    cw = _pick(t, (512, 256))

    def body(z_ref, b_ref, dc_ref, o_ref):
        low = _tri(cw, True).astype(F32)
        carry = jnp.zeros((NH_F, 1), F32)
        for j in reversed(range(t // cw)):
            sl = slice(j * cw, (j + 1) * cw)
            dlf = _f32dot(dc_ref[:, sl], low) + carry
            o_ref[:, sl] = dlf * _sigmoid(-(z_ref[:, sl] + b_ref[...]))
            carry = dlf[:, 0:1]

    return pl.pallas_call(
        body, name="fox_gate_bwd", grid=(1,),
        in_specs=[pl.BlockSpec((NH_F, t), lambda i: (1, 0)), pl.BlockSpec((NH_F, 1), lambda i: (0, 0)),
                  pl.BlockSpec((NH_F, t), lambda i: (0, 0))],
        out_specs=pl.BlockSpec((NH_F, t), lambda i: (0, 0)), out_shape=jax.ShapeDtypeStruct((NH_F, t), F32),
        compiler_params=_cparams(("arbitrary",)),
    )(zsr, bf_r, dc)


def _causal_mask(n):
    return _tri(n, True)


def _fox_fwd(q, k, v, c_col, c_row, gf):
    nh, t, dh = q.shape
    tq = _pick(t, (512, 256))
    scale = dh ** -0.5

    def body(q_ref, k_ref, v_ref, cc_ref, cr_ref, g_ref, o_ref, lse_ref, y_ref):
        i = pl.program_id(1)
        qv = q_ref[...]
        cq = cc_ref[...]

        def blk(j, carry, masked):
            m, l, acc = carry
            k0 = pl.multiple_of(j * tq, tq)
            kb = k_ref[pl.ds(k0, tq), :]
            vb = v_ref[pl.ds(k0, tq), :]
            s = _dot(qv, kb, 1, 1) * scale + cq - cr_ref[:, pl.ds(k0, tq)]
            if masked:
                s = jnp.where(_causal_mask(tq), s, -jnp.inf)
            m_new = jnp.maximum(m, jnp.max(s, axis=1, keepdims=True))
            alpha = jnp.exp(m - m_new)
            p = jnp.exp(s - m_new)
            return m_new, alpha * l + jnp.sum(p, axis=1, keepdims=True), alpha * acc + _dot(p, vb, 1, 0)

        init = (jnp.full((tq, 1), -jnp.inf, F32), jnp.zeros((tq, 1), F32), jnp.zeros((tq, dh), F32))
        carry = lax.fori_loop(0, i, lambda j, c: blk(j, c, False), init)
        m, l, acc = blk(i, carry, True)
        o = acc / l
        o_ref[...] = o
        lse_ref[...] = m + jnp.log(l)
        y_ref[...] = (o * lax.rsqrt(jnp.mean(o * o, axis=-1, keepdims=True) + EPS) * g_ref[...]).astype(y_ref.dtype)

    full = lambda w: pl.BlockSpec((None, t, w), lambda h, i: (h, 0, 0))
    tile = lambda w: pl.BlockSpec((None, tq, w), lambda h, i: (h, i, 0))
    return pl.pallas_call(
        body, name="fox_fwd", grid=(nh, t // tq),
        in_specs=[tile(dh), full(dh), full(dh), tile(1), pl.BlockSpec((None, 1, t), lambda h, i: (h, 0, 0)),
                  pl.BlockSpec((None, 1, dh), lambda h, i: (h, 0, 0))],
        out_specs=[tile(dh), tile(1), tile(dh)],
        out_shape=[jax.ShapeDtypeStruct((nh, t, dh), F32), jax.ShapeDtypeStruct((nh, t, 1), F32),
                   jax.ShapeDtypeStruct((nh, t, dh), BF16)],
        compiler_params=_cparams(("parallel", "parallel")),
    )(q, k, v, c_col, c_row, gf)


def _fox_norm_bwd(dy, o, gf):
    nh, t, dh = o.shape
    tm = _pick(t, (512, 256))

    def body(dy_ref, o_ref, g_ref, do_ref, dl_ref, dg_ref):
        ov = o_ref[...]
        dx, dg = _rms_bwd_val(dy_ref[...], ov, g_ref[...])
        do_ref[...] = dx
        dl_ref[...] = jnp.sum(dx * ov, axis=-1, keepdims=True)

        @pl.when(pl.program_id(1) == 0)
        def _():
            dg_ref[...] = jnp.zeros_like(dg_ref)
        dg_ref[...] += dg

    tile = lambda w: pl.BlockSpec((None, tm, w), lambda h, i: (h, i, 0))
    gspec = pl.BlockSpec((None, 1, dh), lambda h, i: (h, 0, 0))
    return pl.pallas_call(
        body, name="fox_norm_bwd", grid=(nh, t // tm), in_specs=[tile(dh), tile(dh), gspec],
        out_specs=[tile(dh), tile(1), gspec],
        out_shape=[jax.ShapeDtypeStruct((nh, t, dh), F32), jax.ShapeDtypeStruct((nh, t, 1), F32),
                   jax.ShapeDtypeStruct((nh, 1, dh), F32)],
        compiler_params=_cparams(("parallel", "arbitrary")),
    )(dy, o, gf)


def _fox_bwd(q, k, v, c_col, c_row, do, lse, delta):
    nh, t, dh = q.shape
    tq = _pick(t, (512, 256))
    nq = t // tq
    scale = dh ** -0.5

    def body(q_ref, k_ref, v_ref, cc_ref, cr_ref, do_ref, lse_ref, dl_ref, dq_ref, dk_ref, dv_ref, dc_ref, dcq_ref):
        j = pl.program_id(1)

        @pl.when(j == 0)
        def _():
            dq_ref[...] = jnp.zeros_like(dq_ref)
            dcq_ref[...] = jnp.zeros_like(dcq_ref)

        kb, vb, crb = k_ref[...], v_ref[...], cr_ref[...]

        def blk(i, carry, masked):
            dk, dv, dc = carry
            rows = pl.ds(pl.multiple_of(i * tq, tq), tq)
            qb = q_ref[rows, :]
            dob = do_ref[rows, :].astype(BF16)
            s = _dot(qb, kb, 1, 1) * scale + cc_ref[rows, :] - crb
            if masked:
                s = jnp.where(_causal_mask(tq), s, -jnp.inf)
            p = jnp.exp(s - lse_ref[rows, :])
            dv = dv + _dot(p, dob, 0, 0)
            ds = p * (_dot(dob, vb, 1, 1) - dl_ref[rows, :])
            dc = dc + _colsum(ds)
            dk = dk + _dot(ds, qb, 0, 0) * scale
            dq_ref[rows, :] += _dot(ds, kb, 1, 0) * scale
            dcq_ref[rows, :] += jnp.sum(ds, axis=1, keepdims=True)
            return dk, dv, dc

        init = (jnp.zeros((tq, dh), F32), jnp.zeros((tq, dh), F32), jnp.zeros((1, tq), F32))
        carry = blk(j, init, True)
        dk, dv, dc = lax.fori_loop(j + 1, nq, lambda i, c: blk(i, c, False), carry)
        dk_ref[...] = dk
        dv_ref[...] = dv
        dc_ref[...] = -dc

    full = lambda w: pl.BlockSpec((None, t, w), lambda h, j: (h, 0, 0))
    tile = lambda w: pl.BlockSpec((None, tq, w), lambda h, j: (h, j, 0))
    crow = pl.BlockSpec((None, 1, tq), lambda h, j: (h, 0, j))
    return pl.pallas_call(
        body, name="fox_bwd", grid=(nh, nq),
        in_specs=[full(dh), tile(dh), tile(dh), full(1), crow, full(dh), full(1), full(1)],
        out_specs=[full(dh), tile(dh), tile(dh), crow, full(1)],
        out_shape=[jax.ShapeDtypeStruct((nh, t, dh), F32)] * 3 + [jax.ShapeDtypeStruct((nh, 1, t), F32),
                                                                jax.ShapeDtypeStruct((nh, t, 1), F32)],
        compiler_params=_cparams(("parallel", "arbitrary")),
    )(q, k, v, c_col, c_row, do, lse, delta)


AUG = 64


def _split3(c):
    hi = c.astype(BF16).astype(F32)
    r1 = c - hi
    mid = r1.astype(BF16).astype(F32)
    return hi, mid, r1 - mid


def _fox_prep(zbig, ct):
    t = zbig.shape[0]
    tm = _pick(t, (512, 256))

    def body(q_ref, k_ref, v_ref, c_ref, qo_ref, ko_ref, vo_ref):
        lane = lax.broadcasted_iota(jnp.int32, (tm, AUG), 1)
        qv, kv, vv, cv = q_ref[...], k_ref[...], v_ref[...], c_ref[...]
        one = (lane == 0).astype(BF16)
        for h in range(NH_F):
            hi, mid, lo = _split3(cv[:, h:h + 1])
            aq = jnp.where(lane == 0, hi, jnp.where(lane == 1, mid, jnp.where(lane == 2, lo, jnp.where(lane < 6, 1.0, 0.0))))
            ak = jnp.where(lane < 3, 1.0, jnp.where(lane == 3, -hi, jnp.where(lane == 4, -mid, jnp.where(lane == 5, -lo, 0.0))))
            sl = slice(h * DH_F, (h + 1) * DH_F)
            qo_ref[h] = jnp.concatenate([qv[:, sl] * (DH_F ** -0.5), aq.astype(BF16)], axis=1).astype(BF16)
            ko_ref[h] = jnp.concatenate([kv[:, sl], ak.astype(BF16)], axis=1)
            vo_ref[h] = jnp.concatenate([vv[:, sl], one], axis=1)

    ospec = pl.BlockSpec((NH_F, tm, 128), lambda i: (0, i, 0))
    return pl.pallas_call(
        body, name="fox_prep", grid=(t // tm,),
        in_specs=[pl.BlockSpec((tm, 512), lambda i: (i, 3)), pl.BlockSpec((tm, 512), lambda i: (i, 4)),
                  pl.BlockSpec((tm, 512), lambda i: (i, 5)), pl.BlockSpec((tm, NH_F), lambda i: (i, 0))],
        out_specs=[ospec] * 3, out_shape=[jax.ShapeDtypeStruct((NH_F, t, 128), BF16)] * 3,
        compiler_params=_cparams(("parallel",)),
    )(zbig, zbig, zbig, ct)


def _fox_fwd2(qa, ka, va, gf):
    nh, t, _ = qa.shape
    tq = _pick(t, (512, 256))

    def body(q_ref, k_ref, v_ref, g_ref, y_ref, o_ref, lse_ref):
        i = pl.program_id(0)
        lane = lax.broadcasted_iota(jnp.int32, (tq, 128), 1)
        ys, os_ = [], []
        lse_all = jnp.zeros((tq, 128), F32)
        for h in range(nh):
            qv = q_ref[h]

            def blk(j, carry, masked, h=h, qv=qv):
                m, acc = carry
                k0 = pl.multiple_of(j * tq, tq)
                s = lax.dot_general(qv, k_ref[h, pl.ds(k0, tq), :], (((1,), (1,)), ((), ())), preferred_element_type=F32)
                if masked:
                    s = jnp.where(_causal_mask(tq), s, -jnp.inf)
                m_new = jnp.maximum(m, jnp.max(s, axis=1, keepdims=True))
                p = jnp.exp(s - m_new).astype(BF16)
                pv = lax.dot_general(p, v_ref[h, pl.ds(k0, tq), :], (((1,), (0,)), ((), ())), preferred_element_type=F32)
                return m_new, jnp.exp(m - m_new) * acc + pv

            init = (jnp.full((tq, 1), -jnp.inf, F32), jnp.zeros((tq, 128), F32))
            carry = lax.fori_loop(0, i, lambda j, c: blk(j, c, False), init)
            m, acc = blk(i, carry, True)
            l = acc[:, DH_F:DH_F + 1]
            o = acc[:, :DH_F] / l
            os_.append(o)
            gh = g_ref[:, h * DH_F:(h + 1) * DH_F]
            ys.append(o * lax.rsqrt(jnp.mean(o * o, axis=-1, keepdims=True) + EPS) * gh)
            lse_all = lse_all + jnp.where(lane == h, m + jnp.log(l), 0.0)
        y_ref[...] = jnp.concatenate(ys, axis=1).astype(y_ref.dtype)
        o_ref[...] = jnp.concatenate(os_, axis=1)
        lse_ref[...] = lse_all

    full = pl.BlockSpec((nh, t, 128), lambda i: (0, 0, 0))
    return pl.pallas_call(
        body, name="fox_fwd", grid=(t // tq,),
        in_specs=[pl.BlockSpec((nh, tq, 128), lambda i: (0, i, 0)), full, full, pl.BlockSpec((1, nh * DH_F), lambda i: (0, 0))],
        out_specs=[pl.BlockSpec((tq, nh * DH_F), lambda i: (i, 0)), pl.BlockSpec((tq, nh * DH_F), lambda i: (i, 0)),
                   pl.BlockSpec((tq, 128), lambda i: (i, 0))],
        out_shape=[jax.ShapeDtypeStruct((t, nh * DH_F), BF16), jax.ShapeDtypeStruct((t, nh * DH_F), F32),
                   jax.ShapeDtypeStruct((t, 128), F32)],
        compiler_params=_cparams(("parallel",)),
    )(qa, ka, va, gf)


def _fox_bwd_prep(dycat, o, gf):
    t = o.shape[0]
    tm = _pick(t, (512, 256))

    def body(dy_ref, o_ref, g_ref, do_ref, dl_ref, dg_ref):
        lane = lax.broadcasted_iota(jnp.int32, (tm, 128), 1)
        dyv, ov, gv = dy_ref[...], o_ref[...], g_ref[...]
        dgs = []
        dl = jnp.zeros((tm, 128), F32)
        pad = jnp.zeros((tm, AUG), BF16)
        for h in range(NH_F):
            sl = slice(h * DH_F, (h + 1) * DH_F)
            dx, dg = _rms_bwd_val(dyv[:, sl], ov[:, sl], gv[:, sl])
            dgs.append(dg)
            do_ref[h] = jnp.concatenate([dx.astype(BF16), pad], axis=1)
            dl = dl + jnp.where(lane == h, jnp.sum(dx * ov[:, sl], axis=-1, keepdims=True), 0.0)
        dl_ref[...] = dl

        @pl.when(pl.program_id(0) == 0)
        def _():
            dg_ref[...] = jnp.zeros_like(dg_ref)
        dg_ref[...] += jnp.concatenate(dgs, axis=1)

    return pl.pallas_call(
        body, name="fox_bwd_prep", grid=(t // tm,),
        in_specs=[pl.BlockSpec((tm, 512), lambda i: (i, 1)), pl.BlockSpec((tm, 512), lambda i: (i, 0)),
                  pl.BlockSpec((1, 512), lambda i: (0, 0))],
        out_specs=[pl.BlockSpec((NH_F, tm, 128), lambda i: (0, i, 0)), pl.BlockSpec((tm, 128), lambda i: (i, 0)),
                   pl.BlockSpec((1, 512), lambda i: (0, 0))],
        out_shape=[jax.ShapeDtypeStruct((NH_F, t, 128), BF16), jax.ShapeDtypeStruct((t, 128), F32),
                   jax.ShapeDtypeStruct((1, 512), F32)],
        compiler_params=_cparams(("arbitrary",)),
    )(dycat, o, gf)


def _fox_bwd2(qa, ka, va, doa, lse, delta):
    nh, t, _ = qa.shape
    tq = _pick(t, (512, 256))
    nq = t // tq

    def body(q_ref, k_ref, v_ref, do_ref, lse_ref, dl_ref, dq_ref, dk_ref, dv_ref):
        h, j = pl.program_id(0), pl.program_id(1)

        @pl.when(j == 0)
        def _():
            dq_ref[...] = jnp.zeros_like(dq_ref)

        kb, vb = k_ref[...], v_ref[...]
        lane = lax.broadcasted_iota(jnp.int32, (tq, 128), 1)

        def blk(i, carry, masked):
            dk, dv = carry
            rows = pl.ds(pl.multiple_of(i * tq, tq), tq)
            qb, dob = q_ref[rows, :], do_ref[rows, :]
            lse_h = jnp.sum(jnp.where(lane == h, lse_ref[rows, :], 0.0), axis=1, keepdims=True)
            dl_h = jnp.sum(jnp.where(lane == h, dl_ref[rows, :], 0.0), axis=1, keepdims=True)
            s = lax.dot_general(qb, kb, (((1,), (1,)), ((), ())), preferred_element_type=F32)
            if masked:
                s = jnp.where(_causal_mask(tq), s, -jnp.inf)
            p = jnp.exp(s - lse_h)
            dp = lax.dot_general(dob, vb, (((1,), (1,)), ((), ())), preferred_element_type=F32)
            ds = (p * (dp - dl_h)).astype(BF16)
            dv = dv + lax.dot_general(p.astype(BF16), dob, (((0,), (0,)), ((), ())), preferred_element_type=F32)
            dk = dk + lax.dot_general(ds, qb, (((0,), (0,)), ((), ())), preferred_element_type=F32)
            dq_ref[rows, :] += lax.dot_general(ds, kb, (((1,), (0,)), ((), ())), preferred_element_type=F32)
            return dk, dv

        init = (jnp.zeros((tq, 128), F32), jnp.zeros((tq, 128), F32))
        carry = blk(j, init, True)
        dk, dv = lax.fori_loop(j + 1, nq, lambda i, c: blk(i, c, False), carry)
        dk_ref[...] = dk
        dv_ref[...] = dv

    full = pl.BlockSpec((None, t, 128), lambda h, j: (h, 0, 0))
    tile = pl.BlockSpec((None, tq, 128), lambda h, j: (h, j, 0))
    cols = pl.BlockSpec((t, 128), lambda h, j: (0, 0))
    return pl.pallas_call(
        body, name="fox_bwd", grid=(nh, nq), in_specs=[full, tile, tile, full, cols, cols], out_specs=[full, tile, tile],
        out_shape=[jax.ShapeDtypeStruct((nh, t, 128), F32)] * 3, compiler_params=_cparams(("parallel", "arbitrary")),
    )(qa, ka, va, doa, lse, delta)


def _fox_bwd_post(dqa, dka, dva):
    nh, t, _ = dqa.shape
    tm = _pick(t, (512, 256))

    def body(dq_ref, dk_ref, dv_ref, oq_ref, ok_ref, ov_ref, dc_ref):
        lane = lax.broadcasted_iota(jnp.int32, (tm, 128), 1)
        dc = jnp.zeros((tm, 128), F32)
        qs, ks, vs = [], [], []
        for h in range(nh):
            dq, dk = dq_ref[h], dk_ref[h]
            qs.append(dq[:, :DH_F] * (DH_F ** -0.5))
            ks.append(dk[:, :DH_F])
            vs.append(dv_ref[h][:, :DH_F])
            dc = dc + jnp.where(lane == h, dq[:, DH_F:DH_F + 1] - dk[:, DH_F + 3:DH_F + 4], 0.0)
        oq_ref[...] = jnp.concatenate(qs, axis=1).astype(BF16)
        ok_ref[...] = jnp.concatenate(ks, axis=1).astype(BF16)
        ov_ref[...] = jnp.concatenate(vs, axis=1).astype(BF16)
        dc_ref[...] = dc

    ispec = pl.BlockSpec((nh, tm, 128), lambda i: (0, i, 0))
    ospec = pl.BlockSpec((tm, nh * DH_F), lambda i: (i, 0))
    return pl.pallas_call(
        body, name="fox_bwd_post", grid=(t // tm,), in_specs=[ispec] * 3,
        out_specs=[ospec] * 3 + [pl.BlockSpec((tm, 128), lambda i: (i, 0))],
        out_shape=[jax.ShapeDtypeStruct((t, nh * DH_F), BF16)] * 3 + [jax.ShapeDtypeStruct((t, 128), F32)],
        compiler_params=_cparams(("parallel",)),
    )(dqa, dka, dva)


W_BIG = 6 * 512
IN_OFF = (0, 512, 1024, 1544, 2056, 2568)
IN_GATES = (1536, 3080)


def _heads(a, nh):
    t = a.shape[0]
    return a.reshape(t, nh, -1).transpose(1, 0, 2)


def _unheads(a):
    nh, t, dh = a.shape
    return a.transpose(1, 0, 2).reshape(t, nh * dh)


def _local_step(x, p, tgt, sp, wg1, wu1, wd1, w_in, conv_w, w_out, wg2, wu2, wd2, w_pg, w_pp):
    t, d = x.shape
    w_big = jnp.concatenate([w_in[o:o + 512] for o in IN_OFF], axis=0)
    w_small = jnp.concatenate([w_in[IN_GATES[0]:IN_GATES[0] + 8], w_in[IN_GATES[1]:IN_GATES[1] + 8],
                               jnp.zeros((112, d), w_in.dtype)], axis=0)
    h1, xn1, g1, u1 = _ffn_fwd("ffn1", x, sp["ffn1_norm"], wg1, wu1, wd1)
    u, zbig = _norm_mm("in_big", h1, sp["mix_norm"], w_big, True, BF16)
    zs = _mm_nt("in_small", u, w_small, tm=1024, tk=1024)
    zsr = zs.T
    qk_act = _conv_fwd(zbig, conv_w)
    bm_c, bf_c = sp["b_mlstm_gates"], sp["b_fox_f"]
    y_m, cst, mst = _mlstm_fwd(qk_act, zbig, zs, zsr, bm_c, bm_c.T, sp["mlstm_out_norm"])
    c = _fox_cumsum(zsr, bf_c.T)
    qa, ka, va = _fox_prep(zbig, c.T)
    y_ft, o_f, lse = _fox_fwd2(qa, ka, va, sp["fox_out_norm"])
    tm = _pick(t, (1024, 512, 256))
    h2 = _mm("out_proj", [(y_m, (tm, 512), lambda i, j, k: (i, 0), w_out, (512, d), lambda i, j, k: (0, 0)),
                          (y_ft, (tm, 512), lambda i, j, k: (i, 0), w_out, (512, d), lambda i, j, k: (1, 0))],
             (t, d), (tm, d), lambda i, j, k: (i, 0), (t // tm, 1, 1), 2, res=h1)
    h3, xn2, g2, u2 = _ffn_fwd("ffn2", h2, sp["ffn2_norm"], wg2, wu2, wd2)
    hn3, gate_pre = _norm_mm("ple_gate", h3, sp["ple_gate_norm"], w_pg, False, F32)
    pp = _mm_nn("ple_proj", p, w_pp, tm=1024)

    def head_fn(h3_t, gp_t, pp_t, tgt_t, g_pp, g_fin):
        gate = _sigmoid(gp_t)
        ppn = _rms_fwd_val(pp_t, g_pp)
        h4 = h3_t + gate * ppn
        err = _rms_fwd_val(h4, g_fin) - tgt_t
        loss = 0.5 * jnp.sum(jnp.mean(err * err, axis=-1, keepdims=True), axis=0, keepdims=True)
        dh4, dg_fin = _rms_bwd_val(err * (1.0 / d), h4, g_fin)
        dpp, dg_pp = _rms_bwd_val(dh4 * gate, pp_t, g_pp)
        dgp = dh4 * ppn * gate * (1.0 - gate)
        return dh4, dgp, dpp, jnp.broadcast_to(loss, (1, 128)), dg_fin, dg_pp

    dh4, dgp, dpp, loss_part, dg_fin, dg_pp = _rowwise(
        "loss_head", head_fn, [h3, gate_pre, pp, tgt], [sp["ple_proj_norm"], sp["final_norm"]],
        [(d, F32), (d, BF16), (d, BF16)], [((1, 128), F32), ((1, d), F32), ((1, d), F32)])
    gw, gs = {}, {"final_norm": dg_fin, "ple_proj_norm": dg_pp}
    gw["w_ple_gate"] = _mm_tn("d_w_pg", hn3, dgp, tm=1024, tn=1024)
    gw["w_ple_proj"] = _mm_tn("d_w_pp", p, dpp, tn=1024)
    dhn3 = _mm_nt("d_hn3", dgp, w_pg, tm=1024, tn=1024, tk=1024)

    def res_norm_bwd(dn_t, h_t, dres_t, g):
        dx, dg = _rms_bwd_val(dn_t, h_t, g)
        return dres_t + dx, dg

    dh3, gs["ple_gate_norm"] = _rowwise("ple_norm_bwd", res_norm_bwd, [dhn3, h3, dh4], [sp["ple_gate_norm"]],
                                        [(d, F32)], [((1, d), F32)])
    dh2, gs["ffn2_norm"], gw["ffn2_w_gate"], gw["ffn2_w_up"], gw["ffn2_w_down"] = _ffn_bwd(
        "ffn2", dh3, h2, sp["ffn2_norm"], xn2, g2, u2, wg2, wu2, wd2)
    dycat = _mm_nt("d_ycat", dh2, w_out, tm=1024, tn=1024, tk=1024)
    gw["w_out"] = jnp.concatenate([_mm_tn("d_w_out_m", y_m, dh2, tn=1024), _mm_tn("d_w_out_f", y_ft, dh2, tn=1024)], axis=0)
    doa, delta, gs["fox_out_norm"] = _fox_bwd_prep(dycat, o_f, sp["fox_out_norm"])
    dq_f, dk_f, dv_f, dct = _fox_bwd_post(*_fox_bwd2(qa, ka, va, doa, lse, delta))
    dfp = _fox_gate_bwd(zsr, bf_c.T, dct[:, :NH_F].T)
    dact, dv_m, do_m, dzs_m, dzr_m, gs["mlstm_out_norm"] = _mlstm_bwd(
        qk_act, zbig, zs, zsr, bm_c, bm_c.T, sp["mlstm_out_norm"], cst, mst, dycat)
    dqk, gw["conv_qk"] = _conv_bwd(zbig, dact, conv_w)
    dz_big = jnp.concatenate([dqk, dv_m, do_m, dq_f, dk_f, dv_f], axis=1)
    dzs = dzs_m + jnp.pad(jnp.concatenate([dzr_m, dfp], axis=0).T, ((0, 0), (0, 112)))
    dw_big = _mm_tn("d_w_big", dz_big, u, tn=1024)
    dw_small = _mm_tn("d_w_small", dzs, u, tn=1024)
    gw["w_in"] = jnp.concatenate([dw_big[0:1536], dw_small[0:8], dw_big[1536:3072], dw_small[8:16]], axis=0)
    du_a = _mm_nn("d_u_big", dz_big, w_big, tm=1024, tn=1024, tk=1024)
    du_b = _mm_nn("d_u_small", dzs, w_small, tm=1024, tn=1024)

    def mix_norm_bwd(da_t, db_t, h_t, dres_t, dzs_t, g):
        dx, dg = _rms_bwd_val(da_t + db_t, h_t, g)
        return dres_t + dx, dg, _colsum(dzs_t)

    dh1, gs["mix_norm"], dbias = _rowwise("mix_norm_bwd", mix_norm_bwd, [du_a, du_b, h1, dh2, dzs], [sp["mix_norm"]],
                                          [(d, F32)], [((1, d), F32), ((1, 128), F32)])
    gs["b_mlstm_gates"], gs["b_fox_f"] = dbias[:, 0:8], dbias[:, 8:16]
    grad_x, gs["ffn1_norm"], gw["ffn1_w_gate"], gw["ffn1_w_up"], gw["ffn1_w_down"] = _ffn_bwd(
        "ffn1", dh1, x, sp["ffn1_norm"], xn1, g1, u1, wg1, wu1, wd1)
    return loss_part, grad_x, gw, gs


ANY = pl.BlockSpec(memory_space=pl.ANY)
MESH = pl.DeviceIdType.MESH


def _place():
    x, y, c = lax.axis_index("x"), lax.axis_index("y"), lax.axis_index("c")
    chips = [(1 - x, y), (x, 1 - y), (1 - x, 1 - y)]
    return x, y, c, 2 * x + y, (x, y, 1 - c), chips


def _rcopy(src, dst, ssem, rsem, dev):
    return pltpu.make_async_remote_copy(src_ref=src, dst_ref=dst, send_sem=ssem, recv_sem=rsem, device_id=dev,
                                        device_id_type=MESH)


def _half(ref, lead, axis, idx, half):
    return ref.at[(slice(None),) * (lead + axis) + (pl.ds(idx * half, half),)]


def _to_slot(name, a, me_idx, dtype):
    r, cdim = a.shape
    tr = _pick(r, (256, 176, 128, 64))

    def body(me_ref, a_ref, o_ref):
        o_ref[...] = a_ref[...].astype(o_ref.dtype)

    return pl.pallas_call(
        body, name=name,
        grid_spec=pltpu.PrefetchScalarGridSpec(
            num_scalar_prefetch=1, grid=(r // tr,), in_specs=[pl.BlockSpec((tr, cdim), lambda i, me_ref: (i, 0))],
            out_specs=pl.BlockSpec((None, tr, cdim), lambda i, me_ref: (me_ref[0], i, 0))),
        out_shape=jax.ShapeDtypeStruct((4, r, cdim), dtype), compiler_params=_cparams(("parallel",)),
    )(me_idx, a)


def _gather4(name, bufs, split):
    n = len(bufs)
    arrs = [jax.ShapeDtypeStruct(b.shape[1:], b.dtype) for b in bufs]

    def body(*refs):
        outs = refs[n:2 * n]
        isend, irecv, dsend, drecv = refs[2 * n:]
        x, y, c, me, sib, chips = _place()

        def part(ref, a):
            if split[a] is None:
                return ref
            return _half(ref, 0, split[a], c, arrs[a].shape[split[a]] // 2)

        def other(ref, a):
            return _half(ref, 0, split[a], 1 - c, arrs[a].shape[split[a]] // 2)

        sends = []
        for a in range(n):
            for j, chip in enumerate(chips):
                mine = part(outs[a].at[me], a)
                cp = _rcopy(mine, mine, isend.at[3 * a + j], irecv.at[3 * a + j], (*chip, c))
                cp.start()
                sends.append(cp)
        for j, (px, py) in enumerate(chips):
            src_chip = 2 * px + py
            for a in range(n):
                blk = part(outs[a].at[src_chip], a)
                _rcopy(blk, blk, isend.at[3 * a + j], irecv.at[3 * a + j], sib).wait_recv()
                if split[a] is not None:
                    cp = _rcopy(blk, blk, dsend.at[3 * a + j], drecv.at[3 * a + j], sib)
                    cp.start()
                    sends.append(cp)
        for j, (px, py) in enumerate(chips):
            for a in range(n):
                if split[a] is not None:
                    blk = other(outs[a].at[2 * px + py], a)
                    _rcopy(blk, blk, dsend.at[3 * a + j], drecv.at[3 * a + j], sib).wait_recv()
        for cp in sends:
            cp.wait_send()

    return pl.pallas_call(
        body, name=name, in_specs=[ANY] * n, out_specs=[ANY] * n,
        out_shape=[jax.ShapeDtypeStruct(b.shape, b.dtype) for b in bufs],
        input_output_aliases={a: a for a in range(n)},
        scratch_shapes=[pltpu.SemaphoreType.DMA((3 * n,))] * 4,
    )(*bufs)


def _halved(shape, axis):
    return tuple(d // 2 if i == axis else d for i, d in enumerate(shape))


def _swap(name, arrs):
    n = len(arrs)

    def body(*refs):
        ins, outs = refs[:n], refs[n:2 * n]
        ssem, rsem = refs[2 * n:]
        x, y, c, me, sib, chips = _place()
        cps = [_rcopy(ins[a], outs[a], ssem.at[a], rsem.at[a], sib) for a in range(n)]
        for cp in cps:
            cp.start()
        for cp in cps:
            cp.wait()

    return pl.pallas_call(
        body, name=name, in_specs=[ANY] * n, out_specs=[ANY] * n,
        out_shape=[jax.ShapeDtypeStruct(a.shape, a.dtype) for a in arrs],
        scratch_shapes=[pltpu.SemaphoreType.DMA((n,))] * 2,
    )(*arrs)


def _scatter4(name, arrs):
    n = len(arrs)

    def body(*refs):
        ins, outs = refs[:n], refs[n:2 * n]
        ssem, rsem = refs[2 * n:]
        x, y, c, me, sib, chips = _place()
        cps = []
        for a in range(n):
            for j, (px, py) in enumerate(chips):
                cp = _rcopy(ins[a].at[2 * px + py], outs[a].at[me], ssem.at[3 * a + j], rsem.at[3 * a + j], (px, py, c))
                cp.start()
                cps.append(cp)
        for a in range(n):
            for j, (px, py) in enumerate(chips):
                blk = outs[a].at[2 * px + py]
                _rcopy(blk, blk, ssem.at[3 * a + j], rsem.at[3 * a + j], sib).wait_recv()
        for cp in cps:
            cp.wait_send()

    return pl.pallas_call(
        body, name=name, in_specs=[ANY] * n, out_specs=[ANY] * n,
        out_shape=[jax.ShapeDtypeStruct(a.shape, a.dtype) for a in arrs],
        scratch_shapes=[pltpu.SemaphoreType.DMA((3 * n,))] * 2,
    )(*arrs)


def _join_halves(name, arrs, split):
    n = len(arrs)

    def body(*refs):
        outs = refs[n:2 * n]
        ssem, rsem = refs[2 * n:]
        x, y, c, me, sib, chips = _place()
        cps = []
        for a in range(n):
            mine = _half(outs[a], 0, split[a], c, arrs[a].shape[split[a]] // 2)
            cp = _rcopy(mine, mine, ssem.at[a], rsem.at[a], sib)
            cp.start()
            cps.append(cp)
        for a in range(n):
            blk = _half(outs[a], 0, split[a], 1 - c, arrs[a].shape[split[a]] // 2)
            _rcopy(blk, blk, ssem.at[a], rsem.at[a], sib).wait_recv()
        for cp in cps:
            cp.wait_send()

    return pl.pallas_call(
        body, name=name, in_specs=[ANY] * n, out_specs=[ANY] * n,
        out_shape=[jax.ShapeDtypeStruct(a.shape, a.dtype) for a in arrs],
        input_output_aliases={a: a for a in range(n)}, scratch_shapes=[pltpu.SemaphoreType.DMA((n,))] * 2,
    )(*arrs)


def _allreduce_small(s):
    r, cdim = s.shape

    def body(s_ref, o_ref, buf, ssem, rsem):
        x, y, c, me, sib, chips = _place()
        me8 = 4 * x + 2 * y + c
        buf[me8] = s_ref[...]
        flips = [(fx, fy, fc) for fx in (0, 1) for fy in (0, 1) for fc in (0, 1)][1:]
        cps = []
        for k, (fx, fy, fc) in enumerate(flips):
            peer = (x ^ fx if fx else x, y ^ fy if fy else y, c ^ fc if fc else c)
            cp = _rcopy(s_ref, buf.at[me8], ssem.at[k], rsem.at[k], peer)
            cp.start()
            cps.append(cp)
        for k, (fx, fy, fc) in enumerate(flips):
            src = 4 * (x ^ fx if fx else x) + 2 * (y ^ fy if fy else y) + (c ^ fc if fc else c)
            _rcopy(s_ref, buf.at[src], ssem.at[k], rsem.at[k], sib).wait_recv()
        for cp in cps:
            cp.wait_send()
        acc = buf[0]
        for k in range(1, 8):
            acc = acc + buf[k]
        o_ref[...] = acc

    vm = pl.BlockSpec(memory_space=pltpu.VMEM)
    return pl.pallas_call(
        body, name="allreduce_small", in_specs=[vm], out_specs=vm, out_shape=jax.ShapeDtypeStruct((r, cdim), F32),
        scratch_shapes=[pltpu.VMEM((8, r, cdim), F32), pltpu.SemaphoreType.DMA((7,)), pltpu.SemaphoreType.DMA((7,))],
    )(s)


def _add_my_half(name, g, recv, c_idx, axis):
    nb, hr, hc = recv.shape
    tr = _pick(hr, (256, 176, 128, 64))
    if axis == 0:
        g4 = g.reshape(nb, 2, hr, hc)
        gspec = pl.BlockSpec((None, None, tr, hc), lambda b, i, c_ref: (b, c_ref[0], i, 0))
    else:
        g4 = g
        gspec = pl.BlockSpec((None, tr, hc), lambda b, i, c_ref: (b, i, c_ref[0]))

    def body(c_ref, g_ref, r_ref, o_ref, ob_ref):
        s = g_ref[...] + r_ref[...].astype(F32)
        o_ref[...] = s
        ob_ref[...] = s.astype(BF16)

    ospec = pl.BlockSpec((None, tr, hc), lambda b, i, c_ref: (b, i, 0))
    return pl.pallas_call(
        body, name=name,
        grid_spec=pltpu.PrefetchScalarGridSpec(
            num_scalar_prefetch=1, grid=(nb, hr // tr), in_specs=[gspec, ospec], out_specs=[ospec, ospec]),
        out_shape=[jax.ShapeDtypeStruct((nb, hr, hc), F32), jax.ShapeDtypeStruct((nb, hr, hc), BF16)],
        compiler_params=_cparams(("parallel", "parallel")),
    )(c_idx, g4, recv)


def _sum4(name, landed, own, place, axis):
    nb, h, cdim = landed.shape
    tr = _pick(h, (256, 176, 128, 64))
    nt = h // tr

    def body(p_ref, a1_ref, a2_ref, a3_ref, own_ref, o_ref):
        o_ref[...] = ((own_ref[...] + a1_ref[...].astype(F32)) + a2_ref[...].astype(F32)) + a3_ref[...].astype(F32)

    def nxt(k):
        return pl.BlockSpec((None, tr, cdim), lambda i, p_ref: ((p_ref[0] + k) % nb, i, 0))

    if axis == 0:
        ospec = pl.BlockSpec((tr, cdim), lambda i, p_ref: (p_ref[1] * nt + i, 0))
        oshape = (2 * h, cdim)
    else:
        ospec = pl.BlockSpec((tr, cdim), lambda i, p_ref: (i, p_ref[1]))
        oshape = (h, 2 * cdim)
    return pl.pallas_call(
        body, name=name,
        grid_spec=pltpu.PrefetchScalarGridSpec(
            num_scalar_prefetch=1, grid=(nt,), in_specs=[nxt(1), nxt(2), nxt(3), nxt(0)], out_specs=ospec),
        out_shape=jax.ShapeDtypeStruct(oshape, F32), compiler_params=_cparams(("parallel",)),
    )(place, landed, landed, landed, own)


def _cast_other_half(name, g, c_idx, axis):
    nb, r, cdim = g.shape
    hr, hc = (r // 2, cdim) if axis == 0 else (r, cdim // 2)
    tr = _pick(hr, (256, 176, 128, 64))
    if axis == 0:
        g4 = g.reshape(nb, 2, hr, hc)
        gspec = pl.BlockSpec((None, None, tr, hc), lambda b, i, c_ref: (b, 1 - c_ref[0], i, 0))
    else:
        g4 = g
        gspec = pl.BlockSpec((None, tr, hc), lambda b, i, c_ref: (b, i, 1 - c_ref[0]))

    def body(c_ref, g_ref, o_ref):
        o_ref[...] = g_ref[...].astype(BF16)

    return pl.pallas_call(
        body, name=name,
        grid_spec=pltpu.PrefetchScalarGridSpec(
            num_scalar_prefetch=1, grid=(nb, hr // tr), in_specs=[gspec],
            out_specs=pl.BlockSpec((None, tr, hc), lambda b, i, c_ref: (b, i, 0))),
        out_shape=jax.ShapeDtypeStruct((nb, hr, hc), BF16), compiler_params=_cparams(("parallel", "parallel")),
    )(c_idx, g4)


def _adamw(name, w, g, m, v):
    c1 = 1.0 - ADAM_B1 ** ADAM_STEP
    c2 = 1.0 - ADAM_B2 ** ADAM_STEP

    def fn(w_t, g_t, m_t, v_t):
        m_n = ADAM_B1 * m_t + (1.0 - ADAM_B1) * g_t
        v_n = ADAM_B2 * v_t + (1.0 - ADAM_B2) * (g_t * g_t)
        delta = -ADAM_LR * ((m_n / c1) / (jnp.sqrt(v_n / c2) + ADAM_EPS) + ADAM_WD * w_t)
        return delta, m_n, v_n

    cdim = w.shape[1]
    return _rowwise(name, fn, [w, g, m, v], [], [(cdim, F32)] * 3, tm=_pick(w.shape[0], (256, 176, 128, 64, 8)))


BIG = ("ffn1_w_gate", "ffn1_w_up", "ffn1_w_down", "w_in", "w_out", "ffn2_w_gate", "ffn2_w_up", "ffn2_w_down",
       "w_ple_gate", "w_ple_proj")
SMALL = ("ffn1_norm", "mix_norm", "b_mlstm_gates", "b_fox_f", "mlstm_out_norm", "fox_out_norm", "ffn2_norm",
         "ple_gate_norm", "ple_proj_norm", "final_norm")
WEIGHTS = ("ffn1_norm", "ffn1_w_gate", "ffn1_w_up", "ffn1_w_down", "mix_norm", "w_in", "conv_qk", "b_mlstm_gates",
           "b_fox_f", "mlstm_out_norm", "fox_out_norm", "w_out", "ffn2_norm", "ffn2_w_gate", "ffn2_w_up", "ffn2_w_down",
           "ple_gate_norm", "w_ple_gate", "w_ple_proj", "ple_proj_norm", "final_norm")
TRANSPOSED = ("ffn1_w_gate", "ffn1_w_up", "w_in", "ffn2_w_gate", "ffn2_w_up")
PACK_W = 1024


def _chip_blocks(a):
    r, c4 = a.shape
    return a.reshape(r, 4, c4 // 4).transpose(1, 0, 2)


def _from_chip_blocks(a):
    nb, r, c = a.shape
    return a.transpose(1, 0, 2).reshape(r, nb * c)


def kernel(x, p, ffn1_norm, ffn1_w_gate, ffn1_w_up, ffn1_w_down, mix_norm, w_in, conv_qk, b_mlstm_gates, b_fox_f, mlstm_out_norm, fox_out_norm, w_out, ffn2_norm, ffn2_w_gate, ffn2_w_up, ffn2_w_down, ple_gate_norm, w_ple_gate, w_ple_proj, ple_proj_norm, final_norm, loss_target, m_ffn1_norm, m_ffn1_w_gate, m_ffn1_w_up, m_ffn1_w_down, m_mix_norm, m_w_in, m_conv_qk, m_b_mlstm_gates, m_b_fox_f, m_mlstm_out_norm, m_fox_out_norm, m_w_out, m_ffn2_norm, m_ffn2_w_gate, m_ffn2_w_up, m_ffn2_w_down, m_ple_gate_norm, m_w_ple_gate, m_w_ple_proj, m_ple_proj_norm, m_final_norm, v_ffn1_norm, v_ffn1_w_gate, v_ffn1_w_up, v_ffn1_w_down, v_mix_norm, v_w_in, v_conv_qk, v_b_mlstm_gates, v_b_fox_f, v_mlstm_out_norm, v_fox_out_norm, v_w_out, v_ffn2_norm, v_ffn2_w_gate, v_ffn2_w_up, v_ffn2_w_down, v_ple_gate_norm, v_w_ple_gate, v_w_ple_proj, v_ple_proj_norm, v_final_norm):
    w = dict(ffn1_norm=ffn1_norm, ffn1_w_gate=ffn1_w_gate, ffn1_w_up=ffn1_w_up, ffn1_w_down=ffn1_w_down, mix_norm=mix_norm,
             w_in=w_in, conv_qk=conv_qk, b_mlstm_gates=b_mlstm_gates, b_fox_f=b_fox_f, mlstm_out_norm=mlstm_out_norm,
             fox_out_norm=fox_out_norm, w_out=w_out, ffn2_norm=ffn2_norm, ffn2_w_gate=ffn2_w_gate, ffn2_w_up=ffn2_w_up,
             ffn2_w_down=ffn2_w_down, ple_gate_norm=ple_gate_norm, w_ple_gate=w_ple_gate, w_ple_proj=w_ple_proj,
             ple_proj_norm=ple_proj_norm, final_norm=final_norm)
    m = dict(ffn1_norm=m_ffn1_norm, ffn1_w_gate=m_ffn1_w_gate, ffn1_w_up=m_ffn1_w_up, ffn1_w_down=m_ffn1_w_down,
             mix_norm=m_mix_norm, w_in=m_w_in, conv_qk=m_conv_qk, b_mlstm_gates=m_b_mlstm_gates, b_fox_f=m_b_fox_f,
             mlstm_out_norm=m_mlstm_out_norm, fox_out_norm=m_fox_out_norm, w_out=m_w_out, ffn2_norm=m_ffn2_norm,
             ffn2_w_gate=m_ffn2_w_gate, ffn2_w_up=m_ffn2_w_up, ffn2_w_down=m_ffn2_w_down, ple_gate_norm=m_ple_gate_norm,
             w_ple_gate=m_w_ple_gate, w_ple_proj=m_w_ple_proj, ple_proj_norm=m_ple_proj_norm, final_norm=m_final_norm)
    v = dict(ffn1_norm=v_ffn1_norm, ffn1_w_gate=v_ffn1_w_gate, ffn1_w_up=v_ffn1_w_up, ffn1_w_down=v_ffn1_w_down,
             mix_norm=v_mix_norm, w_in=v_w_in, conv_qk=v_conv_qk, b_mlstm_gates=v_b_mlstm_gates, b_fox_f=v_b_fox_f,
             mlstm_out_norm=v_mlstm_out_norm, fox_out_norm=v_fox_out_norm, w_out=v_w_out, ffn2_norm=v_ffn2_norm,
             ffn2_w_gate=v_ffn2_w_gate, ffn2_w_up=v_ffn2_w_up, ffn2_w_down=v_ffn2_w_down, ple_gate_norm=v_ple_gate_norm,
             w_ple_gate=v_w_ple_gate, w_ple_proj=v_w_ple_proj, ple_proj_norm=v_ple_proj_norm, final_norm=v_final_norm)
    shapes = {n: w[n].shape for n in WEIGHTS}

    def view(a, n):
        return a[0].T if n in TRANSPOSED else a.reshape(-1, a.shape[-1])

    def unview(a, n):
        return (a.T if n in TRANSPOSED else a).reshape(shapes[n])

    w2, m2, v2 = ({n: view(a, n) for n, a in d.items()} for d in (w, m, v))

    c_idx = lax.axis_index("c").astype(jnp.int32).reshape(1)
    me_idx = (2 * lax.axis_index("x") + lax.axis_index("y")).astype(jnp.int32).reshape(1)
    slots = [_to_slot("slot_" + n, w2[n], me_idx, BF16) for n in BIG] + [_to_slot("slot_conv_qk", w2["conv_qk"], me_idx, F32)]
    split = [1 if n == "w_in" else 0 for n in BIG]
    full = dict(zip(BIG + ("conv_qk",), _gather4("gather_weights", slots, split + [None])))
    sp = {n: w2[n] for n in SMALL}
    loss_part, grad_x, gw, gs = _local_step(
        x[0], p[0, 0], loss_target[0], sp, full["ffn1_w_gate"], full["ffn1_w_up"], full["ffn1_w_down"],
        full["w_in"].reshape(-1, w_in.shape[1]), _from_chip_blocks(full["conv_qk"]), full["w_out"].reshape(-1, w_out.shape[-1]),
        full["ffn2_w_gate"], full["ffn2_w_up"], full["ffn2_w_down"], full["w_ple_gate"].reshape(-1, w_ple_gate.shape[-1]),
        _from_chip_blocks(full["w_ple_proj"]))
    loss = lax.psum(loss_part[0, 0], ("x", "y", "c"))

    gw["w_ple_proj"] = _chip_blocks(gw["w_ple_proj"])
    for n in ("w_in", "w_out", "w_ple_gate"):
        gw[n] = gw[n].reshape(4, -1, gw[n].shape[-1])
    blocks = [gw[n] for n in BIG]
    place = jnp.concatenate([me_idx, c_idx])
    wire = [_cast_other_half("rs_cast_" + n, g, c_idx, s) for n, g, s in zip(BIG, blocks, split)]
    swapped = _swap("rs_swap", wire)
    partial = [_add_my_half("rs_add_" + n, g, r, c_idx, s) for n, g, r, s in zip(BIG, blocks, swapped, split)]
    landed = _scatter4("rs_scatter", [pb for _, pb in partial])
    mine = [_sum4("rs_sum_" + n, a, pf, place, s) for n, a, (pf, _), s in zip(BIG, landed, partial, split)]
    grads = dict(zip(BIG, _join_halves("rs_join", mine, split)))

    small = [gs[n].reshape(1, -1) for n in SMALL] + [gw["conv_qk"]]
    rows = [jnp.pad(a, ((0, 0), (0, PACK_W - a.shape[1]))) for a in small]
    packed = jnp.concatenate(rows, axis=0)
    packed = jnp.pad(packed, ((0, -packed.shape[0] % 8), (0, 0)))
    red = _allreduce_small(packed)
    for i, n in enumerate(SMALL):
        grads[n] = red[i:i + 1, :gs[n].size]
    dconv = red[len(SMALL):len(SMALL) + CONV_W, :gw["conv_qk"].shape[1]]
    cw = conv_qk.shape[-1]
    grads["conv_qk"] = lax.dynamic_slice_in_dim(dconv, (2 * lax.axis_index("x") + lax.axis_index("y")) * cw, cw, axis=1)

    outs = {}
    for n in WEIGHTS:
        g2 = grads[n].reshape(w2[n].shape)
        d, nm, nv = _adamw("adamw_" + n, w2[n], g2, m2[n], v2[n])
        outs[n] = tuple(unview(a, n) for a in (g2, d, nm, nv))
    return (loss, grad_x[None], *[outs[n][0] for n in WEIGHTS], *[outs[n][1] for n in WEIGHTS],
            *[outs[n][2] for n in WEIGHTS], *[outs[n][3] for n in WEIGHTS])
```

```python
import functools
import math

import jax
import jax.numpy as jnp
from jax import lax
from jax.experimental import pallas as pl
from jax.experimental.pallas import tpu as pltpu

F32 = jnp.float32
BF16 = jnp.bfloat16
EPS = 1e-6
NH_M, DK_M, DV_M = 4, 64, 128
NH_F, DH_F = 8, 64
CONV_W = 4
ADAM_LR, ADAM_B1, ADAM_B2, ADAM_EPS, ADAM_WD, ADAM_STEP = 0.001, 0.9, 0.999, 1e-08, 0.01, 10
VMEM_LIMIT = 56 * 1024 * 1024


def _cparams(sem):
    return pltpu.CompilerParams(dimension_semantics=sem, vmem_limit_bytes=VMEM_LIMIT)


def _sigmoid(x):
    return 1.0 / (1.0 + jnp.exp(-x))


def _dot(a, b, ca, cb):
    return lax.dot_general(a.astype(BF16), b.astype(BF16), (((ca,), (cb,)), ((), ())), preferred_element_type=F32)


def _rowwise(name, fn, tiled, full, outs, accs=(), tm=256):
    rows = tiled[0].shape[0]
    tm = min(tm, rows)
    assert rows % tm == 0
    n_t, n_f, n_o, n_a = len(tiled), len(full), len(outs), len(accs)

    def body(*refs):
        ins = [r[...] for r in refs[: n_t + n_f]]
        res = fn(*ins)
        if not isinstance(res, (tuple, list)):
            res = (res,)
        orefs = refs[n_t + n_f:]
        for r, v in zip(orefs[:n_o], res[:n_o]):
            r[...] = v.astype(r.dtype)
        if n_a:
            @pl.when(pl.program_id(0) == 0)
            def _():
                for r in orefs[n_o:]:
                    r[...] = jnp.zeros_like(r)
            for r, v in zip(orefs[n_o:], res[n_o:]):
                r[...] += v.astype(r.dtype)

    in_specs = [pl.BlockSpec((tm, a.shape[1]), lambda i: (i, 0)) for a in tiled]
    in_specs += [pl.BlockSpec(a.shape, lambda i: (0, 0)) for a in full]
    out_specs = [pl.BlockSpec((tm, c), lambda i: (i, 0)) for c, _ in outs]
    out_specs += [pl.BlockSpec(s, lambda i: (0, 0)) for s, _ in accs]
    out_shape = [jax.ShapeDtypeStruct((rows, c), d) for c, d in outs]
    out_shape += [jax.ShapeDtypeStruct(s, d) for s, d in accs]
    res = pl.pallas_call(
        body, name=name, grid=(rows // tm,), in_specs=in_specs, out_specs=out_specs, out_shape=out_shape,
        compiler_params=_cparams(("arbitrary",) if n_a else ("parallel",)),
    )(*tiled, *full)
    return res


def _colsum(v):
    return jnp.sum(v, axis=0, keepdims=True)


def _rms_fwd_val(x, g):
    r = lax.rsqrt(jnp.mean(x * x, axis=-1, keepdims=True) + EPS)
    return x * r * g


def _rms_bwd_val(dy, x, g):
    r = lax.rsqrt(jnp.mean(x * x, axis=-1, keepdims=True) + EPS)
    xh = x * r
    dxh = dy * g
    dx = r * (dxh - xh * jnp.mean(dxh * xh, axis=-1, keepdims=True))
    return dx, _colsum(dy * xh)


def _mm(name, pairs, out_shape, out_block, out_map, grid, kaxis, ta=False, tb=False, scale=None, res=None,
        out_dtype=F32):
    nk = grid[kaxis]
    npairs = len(pairs)
    ca, cb = (0 if ta else 1), (1 if tb else 0)
    acc_shape = tuple(d for d in out_block if d is not None)

    def body(*refs):
        in_refs = refs[: 2 * npairs]
        res_ref = refs[2 * npairs] if res is not None else None
        o_ref = refs[2 * npairs + (1 if res is not None else 0)]
        acc_ref = refs[-1]
        k = pl.program_id(kaxis)

        @pl.when(k == 0)
        def _():
            acc_ref[...] = jnp.zeros_like(acc_ref)

        part = None
        for p in range(npairs):
            d = _dot(in_refs[2 * p][...], in_refs[2 * p + 1][...], ca, cb)
            part = d if part is None else part + d
        acc_ref[...] += part

        @pl.when(k == nk - 1)
        def _():
            v = acc_ref[...]
            if scale is not None:
                v = v * scale
            if res_ref is not None:
                v = v + res_ref[...].astype(F32)
            o_ref[...] = v.astype(o_ref.dtype)

    in_specs, args = [], []
    for a, ab, am, b, bb, bm in pairs:
        in_specs += [pl.BlockSpec(ab, am), pl.BlockSpec(bb, bm)]
        args += [a, b]
    if res is not None:
        in_specs.append(pl.BlockSpec(out_block, out_map))
        args.append(res)
    sem = tuple("arbitrary" if i == kaxis else "parallel" for i in range(len(grid)))
    return pl.pallas_call(
        body, name=name, grid=grid, in_specs=in_specs, out_specs=pl.BlockSpec(out_block, out_map),
        out_shape=jax.ShapeDtypeStruct(out_shape, out_dtype), scratch_shapes=[pltpu.VMEM(acc_shape, F32)],
        compiler_params=_cparams(sem),
    )(*args)


def _pick(n, pref):
    for t in pref:
        if n % t == 0:
            return t
    return n


def _mm_nn(name, a, b, tm=512, tn=512, tk=512, **kw):
    (m, k), n = a.shape, b.shape[1]
    tm, tn, tk = _pick(m, (tm, 256, 128)), _pick(n, (tn, 256, 128)), _pick(k, (tk, 256, 128))
    return _mm(name, [(a, (tm, tk), lambda i, j, kk: (i, kk), b, (tk, tn), lambda i, j, kk: (kk, j))],
               (m, n), (tm, tn), lambda i, j, kk: (i, j), (m // tm, n // tn, k // tk), 2, **kw)


def _mm_nt(name, a, b, tm=512, tn=512, tk=512, **kw):
    (m, k), n = a.shape, b.shape[0]
    tm, tn, tk = _pick(m, (tm, 256, 128)), _pick(n, (tn, 256, 128)), _pick(k, (tk, 256, 128))
    return _mm(name, [(a, (tm, tk), lambda i, j, kk: (i, kk), b, (tn, tk), lambda i, j, kk: (j, kk))],
               (m, n), (tm, tn), lambda i, j, kk: (i, j), (m // tm, n // tn, k // tk), 2, tb=True, **kw)


def _mm_tn(name, a, b, tm=512, tn=512, tk=512, **kw):
    (k, m), n = a.shape, b.shape[1]
    tm, tn, tk = _pick(m, (tm, 256, 128)), _pick(n, (tn, 256, 128)), _pick(k, (tk, 256, 128))
    return _mm(name, [(a, (tk, tm), lambda i, j, kk: (kk, i), b, (tk, tn), lambda i, j, kk: (kk, j))],
               (m, n), (tm, tn), lambda i, j, kk: (i, j), (m // tm, n // tn, k // tk), 2, ta=True, **kw)


def _norm_mm(name, h, gamma, w, w_transposed, out_dtype):
    t, d = h.shape
    n = w.shape[0] if w_transposed else w.shape[1]
    tm, tn = _pick(t, (512, 256)), _pick(n, (1024, 512, 256, 128))

    def body(h_ref, gam_ref, w_ref, xn_ref, o_ref, xn_scr):
        @pl.when(pl.program_id(1) == 0)
        def _():
            xn = _rms_fwd_val(h_ref[...], gam_ref[...]).astype(BF16)
            xn_scr[...] = xn
            xn_ref[...] = xn

        o_ref[...] = _dot(xn_scr[...], w_ref[...], 1, 1 if w_transposed else 0).astype(o_ref.dtype)

    wspec = pl.BlockSpec((tn, d), lambda i, j: (j, 0)) if w_transposed else pl.BlockSpec((d, tn), lambda i, j: (0, j))
    return pl.pallas_call(
        body, name=name, grid=(t // tm, n // tn),
        in_specs=[pl.BlockSpec((tm, d), lambda i, j: (i, 0)), pl.BlockSpec((1, d), lambda i, j: (0, 0)), wspec],
        out_specs=[pl.BlockSpec((tm, d), lambda i, j: (i, 0)), pl.BlockSpec((tm, tn), lambda i, j: (i, j))],
        out_shape=[jax.ShapeDtypeStruct((t, d), BF16), jax.ShapeDtypeStruct((t, n), out_dtype)],
        scratch_shapes=[pltpu.VMEM((tm, d), BF16)], compiler_params=_cparams(("parallel", "arbitrary")),
    )(h, gamma, w)


def _ffn_fwd(pfx, h, gamma, wg, wu, wd, plan=None):
    t, d = h.shape
    nb, f, _ = wg.shape
    tm = _pick(t, (512, 256))
    nt = t // tm
    host = _Hosted(plan, 5, 4)

    def body(*refs):
        (h_ref, gam_ref, wg_ref, wu_ref, wd_ref), (ho_ref, xn_ref, g_ref, u_ref), (xn_scr, acc_ref), prefs = host.split(refs)
        i, j = pl.program_id(0), pl.program_id(1)
        host.run(0, (i == 0) & (j == 0), prefs)
        host.run(1, (i == nt // 2) & (j == 0), prefs)

        @pl.when(j == 0)
        def _():
            xn = _rms_fwd_val(h_ref[...], gam_ref[...]).astype(BF16)
            xn_scr[...] = xn
            xn_ref[...] = xn
            acc_ref[...] = jnp.zeros_like(acc_ref)

        x = xn_scr[...]
        g = _dot(x, wg_ref[...], 1, 1)
        u = _dot(x, wu_ref[...], 1, 1)
        g_ref[...] = g.astype(BF16)
        u_ref[...] = u.astype(BF16)
        acc_ref[...] += _dot(g * _sigmoid(g) * u, wd_ref[...], 1, 0)

        @pl.when(j == nb - 1)
        def _():
            ho_ref[...] = h_ref[...] + 0.5 * acc_ref[...]

        host.run(2, (i == nt - 1) & (j == nb - 1), prefs)

    row = pl.BlockSpec((tm, d), lambda i, j: (i, 0))
    blk = pl.BlockSpec((None, tm, f), lambda i, j: (j, i, 0))
    wspec = pl.BlockSpec((None, f, d), lambda i, j: (j, 0, 0))
    hc = host.call_args()
    res = pl.pallas_call(
        body, name=pfx + "_fwd", grid=(nt, nb),
        in_specs=[row, pl.BlockSpec((1, d), lambda i, j: (0, 0)), wspec, wspec, wspec] + hc["in_specs"],
        out_specs=[row, row, blk, blk] + hc["out_specs"],
        out_shape=[jax.ShapeDtypeStruct((t, d), F32), jax.ShapeDtypeStruct((t, d), BF16),
                   jax.ShapeDtypeStruct((nb, t, f), BF16), jax.ShapeDtypeStruct((nb, t, f), BF16)] + hc["out_shape"],
        scratch_shapes=[pltpu.VMEM((tm, d), BF16), pltpu.VMEM((tm, d), F32)] + hc["scratch"],
        input_output_aliases=hc["aliases"], compiler_params=_cparams(("arbitrary", "arbitrary")),
    )(h, gamma, wg, wu, wd, *hc["args"])
    return res[:4], res[4:]


def _ffn_bwd(pfx, dh_out, h, gamma, xn, g_all, u_all, wg, wu, wd, plan=None):
    t, d = h.shape
    nb, f, _ = wg.shape
    tm = _pick(t, (512, 256))
    tk = _pick(t, (512, 256))

    nt = t // tm
    host = _Hosted(plan, 8, 5)

    def body(*refs):
        ((dy_ref, h_ref, gam_ref, wg_ref, wu_ref, wd_ref, g_ref, u_ref), (dh_ref, dgam_ref, dg_ref, du_ref, a_ref),
         (acc_ref,), prefs) = host.split(refs)
        i, j = pl.program_id(0), pl.program_id(1)
        host.run(0, (i == 0) & (j == 0), prefs)
        host.run(1, (i == nt // 2) & (j == 0), prefs)

        @pl.when((i == 0) & (j == 0))
        def _():
            dgam_ref[...] = jnp.zeros_like(dgam_ref)

        @pl.when(j == 0)
        def _():
            acc_ref[...] = jnp.zeros_like(acc_ref)

        da = _dot(dy_ref[...], wd_ref[...], 1, 1) * 0.5
        g = g_ref[...].astype(F32)
        u = u_ref[...].astype(F32)
        s = _sigmoid(g)
        sl = g * s
        du = (da * sl).astype(BF16)
        dg = (da * u * (s * (1.0 + g * (1.0 - s)))).astype(BF16)
        du_ref[...] = du
        dg_ref[...] = dg
        a_ref[...] = (sl * u).astype(BF16)
        acc_ref[...] += _dot(dg, wg_ref[...], 1, 0) + _dot(du, wu_ref[...], 1, 0)

        @pl.when(j == nb - 1)
        def _():
            dx, dgam = _rms_bwd_val(acc_ref[...], h_ref[...], gam_ref[...])
            dh_ref[...] = dy_ref[...] + dx
            dgam_ref[...] += dgam

        host.run(2, (i == nt - 1) & (j == nb - 1), prefs)

    row = pl.BlockSpec((tm, d), lambda i, j: (i, 0))
    vec = pl.BlockSpec((1, d), lambda i, j: (0, 0))
    blk = pl.BlockSpec((None, tm, f), lambda i, j: (j, i, 0))
    wspec = pl.BlockSpec((None, f, d), lambda i, j: (j, 0, 0))
    hc = host.call_args()
    res = pl.pallas_call(
        body, name=pfx + "_bwd", grid=(nt, nb),
        in_specs=[row, row, vec, wspec, wspec, wspec, blk, blk] + hc["in_specs"],
        out_specs=[row, vec, blk, blk, blk] + hc["out_specs"],
        out_shape=[jax.ShapeDtypeStruct((t, d), F32), jax.ShapeDtypeStruct((1, d), F32)]
        + [jax.ShapeDtypeStruct((nb, t, f), BF16)] * 3 + hc["out_shape"],
        scratch_shapes=[pltpu.VMEM((tm, d), F32)] + hc["scratch"], input_output_aliases=hc["aliases"],
        compiler_params=_cparams(("arbitrary", "arbitrary")),
    )(dh_out, h, gamma, wg, wu, wd, g_all, u_all, *hc["args"])
    dh, dgamma, dg_all, du_all, a_all = res[:5]

    xmap, bmap, omap = (lambda b, k: (k, 0)), (lambda b, k: (b, k, 0)), (lambda b, k: (b, 0, 0))
    dwg = _mm(pfx + "_dwg", [(dg_all, (None, tk, f), bmap, xn, (tk, d), xmap)], (nb, f, d), (None, f, d), omap,
              (nb, t // tk), 1, ta=True)
    dwu = _mm(pfx + "_dwu", [(du_all, (None, tk, f), bmap, xn, (tk, d), xmap)], (nb, f, d), (None, f, d), omap,
              (nb, t // tk), 1, ta=True)
    dwd = _mm(pfx + "_dwd", [(a_all, (None, tk, f), bmap, dh_out, (tk, d), xmap)], (nb, f, d), (None, f, d), omap,
              (nb, t // tk), 1, ta=True, scale=0.5)
    return (dh, dgamma, dwg, dwu, dwd), res[5:]


HALO = 16


def _silu_grad(y):
    s = _sigmoid(y)
    return s * (1.0 + y * (1.0 - s))


def _with_halo(ref, i, n_tiles, tm, before, after):
    t = ref.shape[0]
    r0 = pl.multiple_of(i * tm, tm)
    parts = [ref[pl.ds(r0, tm), :].astype(F32)]
    if before:
        prev = ref[pl.ds(pl.multiple_of(jnp.maximum(r0 - HALO, 0), HALO), HALO), :].astype(F32)
        parts.insert(0, jnp.where(i > 0, prev, 0.0))
    if after:
        nxt = ref[pl.ds(pl.multiple_of(jnp.minimum(r0 + tm, t - HALO), HALO), HALO), :].astype(F32)
        parts.append(jnp.where(i < n_tiles - 1, nxt, 0.0))
    return jnp.concatenate(parts, axis=0)


def _conv_fwd(zbig, w):
    t, c = zbig.shape[0], w.shape[1]
    tm = _pick(t, (512, 256))
    nt = t // tm

    def body(x_ref, w_ref, o_ref):
        xe = _with_halo(x_ref, pl.program_id(0), nt, tm, True, False)
        wv = w_ref[...]
        y = xe * wv[3:4, :]
        for i in range(CONV_W - 1):
            y = y + pltpu.roll(xe, CONV_W - 1 - i, 0) * wv[i:i + 1, :]
        y = y[HALO:, :]
        o_ref[...] = (y * _sigmoid(y)).astype(o_ref.dtype)

    return pl.pallas_call(
        body, name="conv_fwd", grid=(nt,),
        in_specs=[pl.BlockSpec((t, c), lambda i: (0, 0)), pl.BlockSpec(w.shape, lambda i: (0, 0))],
        out_specs=pl.BlockSpec((tm, c), lambda i: (i, 0)), out_shape=jax.ShapeDtypeStruct((t, c), BF16),
        compiler_params=_cparams(("parallel",)),
    )(zbig, w)


def _conv_bwd(zbig, dact, w):
    t, c = dact.shape
    tm = _pick(t, (512, 256))
    nt = t // tm
    n = tm + HALO

    def body(x_ref, d_ref, w_ref, dx_ref, dw_ref):
        xe = _with_halo(x_ref, pl.program_id(0), nt, tm, True, True)
        de = _with_halo(d_ref, pl.program_id(0), nt, tm, False, True)
        wv = w_ref[...]
        sh = [pltpu.roll(xe, CONV_W - 1 - i, 0)[HALO:, :] if i < CONV_W - 1 else xe[HALO:, :] for i in range(CONV_W)]
        y = sh[0] * wv[0:1, :]
        for i in range(1, CONV_W):
            y = y + sh[i] * wv[i:i + 1, :]
        dy = de * _silu_grad(y)
        dx = dy * wv[3:4, :]
        for i in range(CONV_W - 1):
            dx = dx + pltpu.roll(dy, n - (CONV_W - 1 - i), 0) * wv[i:i + 1, :]
        dx_ref[...] = dx[:tm, :].astype(dx_ref.dtype)
        dyc = dy[:tm, :]
        dwp = jnp.concatenate([_colsum(dyc * sh[i][:tm, :]) for i in range(CONV_W)], axis=0)

        @pl.when(pl.program_id(0) == 0)
        def _():
            dw_ref[...] = jnp.zeros_like(dw_ref)
        dw_ref[...] += dwp

    return pl.pallas_call(
        body, name="conv_bwd", grid=(nt,),
        in_specs=[pl.BlockSpec((t, c), lambda i: (0, 0)), pl.BlockSpec((t, c), lambda i: (0, 0)),
                  pl.BlockSpec(w.shape, lambda i: (0, 0))],
        out_specs=[pl.BlockSpec((tm, c), lambda i: (i, 0)), pl.BlockSpec(w.shape, lambda i: (0, 0))],
        out_shape=[jax.ShapeDtypeStruct((t, c), BF16), jax.ShapeDtypeStruct(w.shape, F32)],
        compiler_params=_cparams(("arbitrary",)),
    )(zbig, dact, w)


LM = 256
HI = lax.Precision.HIGHEST


def _logsig(x):
    return jnp.minimum(x, 0.0) - jnp.log(1.0 + jnp.exp(-jnp.abs(x)))


def _tri(n, lower):
    r = lax.broadcasted_iota(jnp.int32, (n, n), 0)
    c = lax.broadcasted_iota(jnp.int32, (n, n), 1)
    return (r >= c) if lower else (r <= c)


def _f32dot(a, b):
    return lax.dot_general(a, b, (((1,), (0,)), ((), ())), precision=HI, preferred_element_type=F32)


def _mlstm_chunk(h, q_ref, k_ref, v_ref, zs_ref, zsr_ref, bc_ref, br_ref, c_prev, m_prev):
    l = LM
    q = q_ref[:, h * DK_M:(h + 1) * DK_M].astype(F32) * (DK_M ** -0.5)
    k = k_ref[:, h * DK_M:(h + 1) * DK_M]
    v = v_ref[:, h * DV_M:(h + 1) * DV_M]
    lane = lax.broadcasted_iota(jnp.int32, (l, DV_M), 1)
    v1 = jnp.concatenate([v, (lane == 0).astype(v.dtype)], axis=1)
    zs, zsr = zs_ref[...], zsr_ref[...]
    li_c = zs[:, h:h + 1] + bc_ref[:, h:h + 1]
    fp_c = zs[:, NH_M + h:NH_M + h + 1] + bc_ref[:, NH_M + h:NH_M + h + 1]
    li_r = zsr[h:h + 1, :] + br_ref[h:h + 1, :]
    fp_r = zsr[NH_M + h:NH_M + h + 1, :] + br_ref[NH_M + h:NH_M + h + 1, :]
    lf_c, lf_r = _logsig(fp_c), _logsig(fp_r)
    low = _tri(l, True)
    b_c = _f32dot(low.astype(F32), lf_c)
    b_r = _f32dot(lf_r, _tri(l, False).astype(F32))
    g = b_r[:, l - 1:l]
    dmat = jnp.where(low, b_c - b_r + li_r, -jnp.inf)
    inter = b_c + m_prev
    m_t = jnp.maximum(inter, jnp.max(dmat, axis=1, keepdims=True))
    w_inter = jnp.exp(inter - m_t)
    amat = jnp.exp(dmat - m_t)
    s = _dot(q, k, 1, 1)
    p = amat * s
    qc = _dot(q, c_prev, 1, 0)
    qc_w = w_inter * qc
    num1 = qc_w + _dot(p, v1, 1, 0)
    den = num1[:, DV_M:DV_M + 1]
    mx = jnp.maximum(jnp.abs(den), jnp.exp(-m_t))
    hh = num1[:, :DV_M] / mx
    a_c = g - b_c + li_c
    return dict(q=q, k=k, v1=v1, fp_c=fp_c, fp_r=fp_r, b_c=b_c, g=g, m_t=m_t, w_inter=w_inter, amat=amat, s=s, p=p,
                qc_w=qc_w, den=den, mx=mx, hh=hh, a_c=a_c)


def _mlstm_fwd(qk, zbig, zs, zsr, bc, br, gm):
    t = zs.shape[0]
    l = LM
    nc = t // l
    dm = NH_M * DV_M

    def body(q_ref, k_ref, v_ref, o_ref, zs_ref, zsr_ref, bc_ref, br_ref, gm_ref, y_ref, cst_ref, mst_ref, c_scr, m_scr):
        @pl.when(pl.program_id(0) == 0)
        def _():
            c_scr[...] = jnp.zeros_like(c_scr)
            m_scr[...] = jnp.zeros_like(m_scr)

        cst_ref[...] = c_scr[...]
        mst_ref[...] = m_scr[...]
        ys = []
        for h in range(NH_M):
            c_prev = c_scr[h]
            m_prev = m_scr[h:h + 1, 0:1]
            r = _mlstm_chunk(h, q_ref, k_ref, v_ref, zs_ref, zsr_ref, bc_ref, br_ref, c_prev, m_prev)
            hh = r["hh"]
            gh = gm_ref[:, h * DV_M:(h + 1) * DV_M]
            hn = hh * lax.rsqrt(jnp.mean(hh * hh, axis=-1, keepdims=True) + EPS) * gh
            og = o_ref[:, h * DV_M:(h + 1) * DV_M].astype(F32)
            ys.append(hn * _sigmoid(og))
            m_new = jnp.maximum(r["g"] + m_prev, jnp.max(r["a_c"], axis=0, keepdims=True))
            decay = jnp.exp(r["g"] + m_prev - m_new)
            wk = r["k"].astype(F32) * jnp.exp(r["a_c"] - m_new)
            c_scr[h] = decay * c_prev + _dot(wk, r["v1"], 0, 0)
            m_scr[h:h + 1, :] = jnp.broadcast_to(m_new, (1, 128))
        y_ref[...] = jnp.concatenate(ys, axis=1).astype(y_ref.dtype)

    return pl.pallas_call(
        body, name="mlstm_fwd", grid=(nc,),
        in_specs=[pl.BlockSpec((l, NH_M * DK_M), lambda i: (i, 0)), pl.BlockSpec((l, NH_M * DK_M), lambda i: (i, 1)),
                  pl.BlockSpec((l, dm), lambda i: (i, 1)), pl.BlockSpec((l, dm), lambda i: (i, 2)),
                  pl.BlockSpec((l, 128), lambda i: (i, 0)), pl.BlockSpec((8, l), lambda i: (0, i)),
                  pl.BlockSpec((1, 8), lambda i: (0, 0)), pl.BlockSpec((8, 1), lambda i: (0, 0)),
                  pl.BlockSpec((1, dm), lambda i: (0, 0))],
        out_specs=[pl.BlockSpec((l, dm), lambda i: (i, 0)), pl.BlockSpec((None, NH_M, DK_M, 2 * DV_M), lambda i: (i, 0, 0, 0)),
                   pl.BlockSpec((None, 8, 128), lambda i: (i, 0, 0))],
        out_shape=[jax.ShapeDtypeStruct((t, dm), BF16), jax.ShapeDtypeStruct((nc, NH_M, DK_M, 2 * DV_M), F32),
                   jax.ShapeDtypeStruct((nc, 8, 128), F32)],
        scratch_shapes=[pltpu.VMEM((NH_M, DK_M, 2 * DV_M), F32), pltpu.VMEM((8, 128), F32)],
        compiler_params=_cparams(("arbitrary",)),
    )(qk, qk, zbig, zbig, zs, zsr, bc, br, gm)


def _mlstm_bwd(qk, zbig, zs, zsr, bc, br, gm, cst, mst, dycat):
    t = zs.shape[0]
    l = LM
    nc = t // l
    dm = NH_M * DV_M

    def body(q_ref, k_ref, v_ref, o_ref, zs_ref, zsr_ref, bc_ref, br_ref, gm_ref, cst_ref, mst_ref, cnx_ref, mnx_ref,
             dy_ref, dqk_ref, dv_ref, do_ref, dzs_ref, dzr_ref, dgm_ref, dc_scr):
        @pl.when(pl.program_id(0) == 0)
        def _():
            dc_scr[...] = jnp.zeros_like(dc_scr)
            dgm_ref[...] = jnp.zeros_like(dgm_ref)

        lane = lax.broadcasted_iota(jnp.int32, (l, 128), 1)
        upper = _tri(l, False).astype(F32)
        lower = _tri(l, True).astype(F32)
        dzr_rows = [None] * 8
        dvs, dos, dgs, dqs, dks = [], [], [], [], []
        dzs = jnp.zeros((l, 128), F32)
        for h in range(NH_M):
            c_prev = cst_ref[h]
            m_prev = mst_ref[h:h + 1, 0:1]
            r = _mlstm_chunk(h, q_ref, k_ref, v_ref, zs_ref, zsr_ref, bc_ref, br_ref, c_prev, m_prev)
            hh, mx, den, m_t, v1, amat = r["hh"], r["mx"], r["den"], r["m_t"], r["v1"], r["amat"]
            gh = gm_ref[:, h * DV_M:(h + 1) * DV_M]
            rs = lax.rsqrt(jnp.mean(hh * hh, axis=-1, keepdims=True) + EPS)
            xh = hh * rs
            sg = _sigmoid(o_ref[:, h * DV_M:(h + 1) * DV_M].astype(F32))
            dyh = dy_ref[:, h * DV_M:(h + 1) * DV_M]
            dos.append(dyh * xh * gh * sg * (1.0 - sg))
            dhn = dyh * sg
            dgs.append(_colsum(dhn * xh))
            dxh = dhn * gh
            dh = rs * (dxh - xh * jnp.mean(dxh * xh, axis=-1, keepdims=True))
            g1 = dh / mx
            hd = jnp.sum(hh * dh, axis=-1, keepdims=True)
            dden = jnp.where(jnp.abs(den) > jnp.exp(-m_t), -hd / mx * jnp.sign(den), 0.0)
            g256 = jnp.concatenate([g1, jnp.where(lane == 0, dden, 0.0)], axis=1)
            dc_h = dc_scr[h]
            ea = jnp.exp(r["a_c"])
            dp = _dot(g256, v1, 1, 1)
            ds = dp * amat
            dqs.append((r["w_inter"] * _dot(g256, c_prev, 1, 1) + _dot(ds, r["k"], 1, 0)) * (DK_M ** -0.5))
            dks.append(_dot(ds, r["q"], 0, 0) + ea * _dot(v1, dc_h, 1, 1))
            dv_st = ea * _dot(r["k"], dc_h, 1, 0)
            dv1 = _dot(r["p"], g256, 0, 0) + dv_st
            dvs.append(dv1[:, :DV_M])
            wmat = dp * r["p"]
            c_in = _colsum(wmat)
            c_st = jnp.sum(v1.astype(F32) * dv_st, axis=-1, keepdims=True)
            r_t = jnp.sum(wmat, axis=1, keepdims=True) + jnp.sum(g256 * r["qc_w"], axis=-1, keepdims=True)
            db = r_t - c_st
            carry = jnp.exp(mnx_ref[h:h + 1, 0:1]) * jnp.sum(
                jnp.sum(dc_h * cnx_ref[h], axis=1, keepdims=True), axis=0, keepdims=True)
            dlf_c = _f32dot(upper, db) + carry
            dlf_r = -_f32dot(c_in, lower)
            dfp = dlf_c * _sigmoid(-r["fp_c"])
            dzs = dzs + jnp.where(lane == h, c_st, 0.0) + jnp.where(lane == NH_M + h, dfp, 0.0)
            dzr_rows[h] = c_in
            dzr_rows[NH_M + h] = dlf_r * _sigmoid(-r["fp_r"])
            wq = r["q"] * jnp.exp(r["b_c"] - m_t)
            dc_scr[h] = jnp.exp(r["g"]) * dc_h + _dot(wq, g256, 0, 0)
        dqk_ref[...] = jnp.concatenate(dqs + dks, axis=1)
        dv_ref[...] = jnp.concatenate(dvs, axis=1).astype(dv_ref.dtype)
        do_ref[...] = jnp.concatenate(dos, axis=1).astype(do_ref.dtype)
        dzs_ref[...] = dzs
        dzr_ref[...] = jnp.concatenate(dzr_rows, axis=0)
        dgm_ref[...] += jnp.concatenate(dgs, axis=1)

    rev = lambda i: nc - 1 - i
    nxt = lambda i: jnp.minimum(nc - i, nc - 1)
    return pl.pallas_call(
        body, name="mlstm_bwd", grid=(nc,),
        in_specs=[pl.BlockSpec((l, NH_M * DK_M), lambda i: (rev(i), 0)), pl.BlockSpec((l, NH_M * DK_M), lambda i: (rev(i), 1)),
                  pl.BlockSpec((l, dm), lambda i: (rev(i), 1)), pl.BlockSpec((l, dm), lambda i: (rev(i), 2)),
                  pl.BlockSpec((l, 128), lambda i: (rev(i), 0)), pl.BlockSpec((8, l), lambda i: (0, rev(i))),
                  pl.BlockSpec((1, 8), lambda i: (0, 0)), pl.BlockSpec((8, 1), lambda i: (0, 0)),
                  pl.BlockSpec((1, dm), lambda i: (0, 0)),
                  pl.BlockSpec((None, NH_M, DK_M, 2 * DV_M), lambda i: (rev(i), 0, 0, 0)),
                  pl.BlockSpec((None, 8, 128), lambda i: (rev(i), 0, 0)),
                  pl.BlockSpec((None, NH_M, DK_M, 2 * DV_M), lambda i: (nxt(i), 0, 0, 0)),
                  pl.BlockSpec((None, 8, 128), lambda i: (nxt(i), 0, 0)),
                  pl.BlockSpec((l, dm), lambda i: (rev(i), 0))],
        out_specs=[pl.BlockSpec((l, dm), lambda i: (rev(i), 0)),
                   pl.BlockSpec((l, dm), lambda i: (rev(i), 0)), pl.BlockSpec((l, dm), lambda i: (rev(i), 0)),
                   pl.BlockSpec((l, 128), lambda i: (rev(i), 0)), pl.BlockSpec((8, l), lambda i: (0, rev(i))),
                   pl.BlockSpec((1, dm), lambda i: (0, 0))],
        out_shape=[jax.ShapeDtypeStruct((t, dm), F32),
                   jax.ShapeDtypeStruct((t, dm), BF16), jax.ShapeDtypeStruct((t, dm), BF16),
                   jax.ShapeDtypeStruct((t, 128), F32), jax.ShapeDtypeStruct((8, t), F32),
                   jax.ShapeDtypeStruct((1, dm), F32)],
        scratch_shapes=[pltpu.VMEM((NH_M, DK_M, 2 * DV_M), F32)],
        compiler_params=_cparams(("arbitrary",)),
    )(qk, qk, zbig, zbig, zs, zsr, bc, br, gm, cst, mst, cst, mst, dycat)


def _fox_cumsum(zsr, bf_r):
    t = zsr.shape[1]
    cw = _pick(t, (512, 256))

    def body(z_ref, b_ref, c_ref):
        up = _tri(cw, False).astype(F32)
        carry = jnp.zeros((NH_F, 1), F32)
        for j in range(t // cw):
            cs = _f32dot(_logsig(z_ref[:, j * cw:(j + 1) * cw] + b_ref[...]), up) + carry
            c_ref[:, j * cw:(j + 1) * cw] = cs
            carry = cs[:, cw - 1:cw]

    return pl.pallas_call(
        body, name="fox_cumsum", grid=(1,),
        in_specs=[pl.BlockSpec((NH_F, t), lambda i: (1, 0)), pl.BlockSpec((NH_F, 1), lambda i: (0, 0))],
        out_specs=pl.BlockSpec((NH_F, t), lambda i: (0, 0)), out_shape=jax.ShapeDtypeStruct((NH_F, t), F32),
        compiler_params=_cparams(("arbitrary",)),
    )(zsr, bf_r)


def _fox_gate_bwd(zsr, bf_r, dc):
    t = zsr.shape[1]
    cw = _pick(t, (512, 256))

    def body(z_ref, b_ref, dc_ref, o_ref):
        low = _tri(cw, True).astype(F32)
        carry = jnp.zeros((NH_F, 1), F32)
        for j in reversed(range(t // cw)):
            sl = slice(j * cw, (j + 1) * cw)
            dlf = _f32dot(dc_ref[:, sl], low) + carry
            o_ref[:, sl] = dlf * _sigmoid(-(z_ref[:, sl] + b_ref[...]))
            carry = dlf[:, 0:1]

    return pl.pallas_call(
        body, name="fox_gate_bwd", grid=(1,),
        in_specs=[pl.BlockSpec((NH_F, t), lambda i: (1, 0)), pl.BlockSpec((NH_F, 1), lambda i: (0, 0)),
                  pl.BlockSpec((NH_F, t), lambda i: (0, 0))],
        out_specs=pl.BlockSpec((NH_F, t), lambda i: (0, 0)), out_shape=jax.ShapeDtypeStruct((NH_F, t), F32),
        compiler_params=_cparams(("arbitrary",)),
    )(zsr, bf_r, dc)


def _causal_mask(n):
    return _tri(n, True)


def _fox_fwd(q, k, v, c_col, c_row, gf):
    nh, t, dh = q.shape
    tq = _pick(t, (512, 256))
    scale = dh ** -0.5

    def body(q_ref, k_ref, v_ref, cc_ref, cr_ref, g_ref, o_ref, lse_ref, y_ref):
        i = pl.program_id(1)
        qv = q_ref[...]
        cq = cc_ref[...]

        def blk(j, carry, masked):
            m, l, acc = carry
            k0 = pl.multiple_of(j * tq, tq)
            kb = k_ref[pl.ds(k0, tq), :]
            vb = v_ref[pl.ds(k0, tq), :]
            s = _dot(qv, kb, 1, 1) * scale + cq - cr_ref[:, pl.ds(k0, tq)]
            if masked:
                s = jnp.where(_causal_mask(tq), s, -jnp.inf)
            m_new = jnp.maximum(m, jnp.max(s, axis=1, keepdims=True))
            alpha = jnp.exp(m - m_new)
            p = jnp.exp(s - m_new)
            return m_new, alpha * l + jnp.sum(p, axis=1, keepdims=True), alpha * acc + _dot(p, vb, 1, 0)

        init = (jnp.full((tq, 1), -jnp.inf, F32), jnp.zeros((tq, 1), F32), jnp.zeros((tq, dh), F32))
        carry = lax.fori_loop(0, i, lambda j, c: blk(j, c, False), init)
        m, l, acc = blk(i, carry, True)
        o = acc / l
        o_ref[...] = o
        lse_ref[...] = m + jnp.log(l)
        y_ref[...] = (o * lax.rsqrt(jnp.mean(o * o, axis=-1, keepdims=True) + EPS) * g_ref[...]).astype(y_ref.dtype)

    full = lambda w: pl.BlockSpec((None, t, w), lambda h, i: (h, 0, 0))
    tile = lambda w: pl.BlockSpec((None, tq, w), lambda h, i: (h, i, 0))
    return pl.pallas_call(
        body, name="fox_fwd", grid=(nh, t // tq),
        in_specs=[tile(dh), full(dh), full(dh), tile(1), pl.BlockSpec((None, 1, t), lambda h, i: (h, 0, 0)),
                  pl.BlockSpec((None, 1, dh), lambda h, i: (h, 0, 0))],
        out_specs=[tile(dh), tile(1), tile(dh)],
        out_shape=[jax.ShapeDtypeStruct((nh, t, dh), F32), jax.ShapeDtypeStruct((nh, t, 1), F32),
                   jax.ShapeDtypeStruct((nh, t, dh), BF16)],
        compiler_params=_cparams(("parallel", "parallel")),
    )(q, k, v, c_col, c_row, gf)


def _fox_norm_bwd(dy, o, gf):
    nh, t, dh = o.shape
    tm = _pick(t, (512, 256))

    def body(dy_ref, o_ref, g_ref, do_ref, dl_ref, dg_ref):
        ov = o_ref[...]
        dx, dg = _rms_bwd_val(dy_ref[...], ov, g_ref[...])
        do_ref[...] = dx
        dl_ref[...] = jnp.sum(dx * ov, axis=-1, keepdims=True)

        @pl.when(pl.program_id(1) == 0)
        def _():
            dg_ref[...] = jnp.zeros_like(dg_ref)
        dg_ref[...] += dg

    tile = lambda w: pl.BlockSpec((None, tm, w), lambda h, i: (h, i, 0))
    gspec = pl.BlockSpec((None, 1, dh), lambda h, i: (h, 0, 0))
    return pl.pallas_call(
        body, name="fox_norm_bwd", grid=(nh, t // tm), in_specs=[tile(dh), tile(dh), gspec],
        out_specs=[tile(dh), tile(1), gspec],
        out_shape=[jax.ShapeDtypeStruct((nh, t, dh), F32), jax.ShapeDtypeStruct((nh, t, 1), F32),
                   jax.ShapeDtypeStruct((nh, 1, dh), F32)],
        compiler_params=_cparams(("parallel", "arbitrary")),
    )(dy, o, gf)


def _fox_bwd(q, k, v, c_col, c_row, do, lse, delta):
    nh, t, dh = q.shape
    tq = _pick(t, (512, 256))
    nq = t // tq
    scale = dh ** -0.5

    def body(q_ref, k_ref, v_ref, cc_ref, cr_ref, do_ref, lse_ref, dl_ref, dq_ref, dk_ref, dv_ref, dc_ref, dcq_ref):
        j = pl.program_id(1)

        @pl.when(j == 0)
        def _():
            dq_ref[...] = jnp.zeros_like(dq_ref)
            dcq_ref[...] = jnp.zeros_like(dcq_ref)

        kb, vb, crb = k_ref[...], v_ref[...], cr_ref[...]

        def blk(i, carry, masked):
            dk, dv, dc = carry
            rows = pl.ds(pl.multiple_of(i * tq, tq), tq)
            qb = q_ref[rows, :]
            dob = do_ref[rows, :].astype(BF16)
            s = _dot(qb, kb, 1, 1) * scale + cc_ref[rows, :] - crb
            if masked:
                s = jnp.where(_causal_mask(tq), s, -jnp.inf)
            p = jnp.exp(s - lse_ref[rows, :])
            dv = dv + _dot(p, dob, 0, 0)
            ds = p * (_dot(dob, vb, 1, 1) - dl_ref[rows, :])
            dc = dc + _colsum(ds)
            dk = dk + _dot(ds, qb, 0, 0) * scale
            dq_ref[rows, :] += _dot(ds, kb, 1, 0) * scale
            dcq_ref[rows, :] += jnp.sum(ds, axis=1, keepdims=True)
            return dk, dv, dc

        init = (jnp.zeros((tq, dh), F32), jnp.zeros((tq, dh), F32), jnp.zeros((1, tq), F32))
        carry = blk(j, init, True)
        dk, dv, dc = lax.fori_loop(j + 1, nq, lambda i, c: blk(i, c, False), carry)
        dk_ref[...] = dk
        dv_ref[...] = dv
        dc_ref[...] = -dc

    full = lambda w: pl.BlockSpec((None, t, w), lambda h, j: (h, 0, 0))
    tile = lambda w: pl.BlockSpec((None, tq, w), lambda h, j: (h, j, 0))
    crow = pl.BlockSpec((None, 1, tq), lambda h, j: (h, 0, j))
    return pl.pallas_call(
        body, name="fox_bwd", grid=(nh, nq),
        in_specs=[full(dh), tile(dh), tile(dh), full(1), crow, full(dh), full(1), full(1)],
        out_specs=[full(dh), tile(dh), tile(dh), crow, full(1)],
        out_shape=[jax.ShapeDtypeStruct((nh, t, dh), F32)] * 3 + [jax.ShapeDtypeStruct((nh, 1, t), F32),
                                                                jax.ShapeDtypeStruct((nh, t, 1), F32)],
        compiler_params=_cparams(("parallel", "arbitrary")),
    )(q, k, v, c_col, c_row, do, lse, delta)


AUG = 64


def _split3(c):
    hi = c.astype(BF16).astype(F32)
    r1 = c - hi
    mid = r1.astype(BF16).astype(F32)
    return hi, mid, r1 - mid


def _fox_prep(zbig, ct):
    t = zbig.shape[0]
    tm = _pick(t, (512, 256))

    def body(q_ref, k_ref, v_ref, c_ref, qo_ref, ko_ref, vo_ref):
        lane = lax.broadcasted_iota(jnp.int32, (tm, AUG), 1)
        qv, kv, vv, cv = q_ref[...], k_ref[...], v_ref[...], c_ref[...]
        one = (lane == 0).astype(BF16)
        for h in range(NH_F):
            hi, mid, lo = _split3(cv[:, h:h + 1])
            aq = jnp.where(lane == 0, hi, jnp.where(lane == 1, mid, jnp.where(lane == 2, lo, jnp.where(lane < 6, 1.0, 0.0))))
            ak = jnp.where(lane < 3, 1.0, jnp.where(lane == 3, -hi, jnp.where(lane == 4, -mid, jnp.where(lane == 5, -lo, 0.0))))
            sl = slice(h * DH_F, (h + 1) * DH_F)
            qo_ref[h] = jnp.concatenate([qv[:, sl] * (DH_F ** -0.5), aq.astype(BF16)], axis=1).astype(BF16)
            ko_ref[h] = jnp.concatenate([kv[:, sl], ak.astype(BF16)], axis=1)
            vo_ref[h] = jnp.concatenate([vv[:, sl], one], axis=1)

    ospec = pl.BlockSpec((NH_F, tm, 128), lambda i: (0, i, 0))
    return pl.pallas_call(
        body, name="fox_prep", grid=(t // tm,),
        in_specs=[pl.BlockSpec((tm, 512), lambda i: (i, 3)), pl.BlockSpec((tm, 512), lambda i: (i, 4)),
                  pl.BlockSpec((tm, 512), lambda i: (i, 5)), pl.BlockSpec((tm, NH_F), lambda i: (i, 0))],
        out_specs=[ospec] * 3, out_shape=[jax.ShapeDtypeStruct((NH_F, t, 128), BF16)] * 3,
        compiler_params=_cparams(("parallel",)),
    )(zbig, zbig, zbig, ct)


def _fox_fwd2(qa, ka, va, gf):
    nh, t, _ = qa.shape
    tq = _pick(t, (512, 256))

    def body(q_ref, k_ref, v_ref, g_ref, y_ref, o_ref, lse_ref):
        i = pl.program_id(0)
        lane = lax.broadcasted_iota(jnp.int32, (tq, 128), 1)
        ys, os_ = [], []
        lse_all = jnp.zeros((tq, 128), F32)
        for h in range(nh):
            qv = q_ref[h]

            def blk(j, carry, masked, h=h, qv=qv):
                m, acc = carry
                k0 = pl.multiple_of(j * tq, tq)
                s = lax.dot_general(qv, k_ref[h, pl.ds(k0, tq), :], (((1,), (1,)), ((), ())), preferred_element_type=F32)
                if masked:
                    s = jnp.where(_causal_mask(tq), s, -jnp.inf)
                m_new = jnp.maximum(m, jnp.max(s, axis=1, keepdims=True))
                p = jnp.exp(s - m_new).astype(BF16)
                pv = lax.dot_general(p, v_ref[h, pl.ds(k0, tq), :], (((1,), (0,)), ((), ())), preferred_element_type=F32)
                return m_new, jnp.exp(m - m_new) * acc + pv

            init = (jnp.full((tq, 1), -jnp.inf, F32), jnp.zeros((tq, 128), F32))
            carry = lax.fori_loop(0, i, lambda j, c: blk(j, c, False), init)
            m, acc = blk(i, carry, True)
            l = acc[:, DH_F:DH_F + 1]
            o = acc[:, :DH_F] / l
            os_.append(o)
            gh = g_ref[:, h * DH_F:(h + 1) * DH_F]
            ys.append(o * lax.rsqrt(jnp.mean(o * o, axis=-1, keepdims=True) + EPS) * gh)
            lse_all = lse_all + jnp.where(lane == h, m + jnp.log(l), 0.0)
        y_ref[...] = jnp.concatenate(ys, axis=1).astype(y_ref.dtype)
        o_ref[...] = jnp.concatenate(os_, axis=1)
        lse_ref[...] = lse_all

    full = pl.BlockSpec((nh, t, 128), lambda i: (0, 0, 0))
    return pl.pallas_call(
        body, name="fox_fwd", grid=(t // tq,),
        in_specs=[pl.BlockSpec((nh, tq, 128), lambda i: (0, i, 0)), full, full, pl.BlockSpec((1, nh * DH_F), lambda i: (0, 0))],
        out_specs=[pl.BlockSpec((tq, nh * DH_F), lambda i: (i, 0)), pl.BlockSpec((tq, nh * DH_F), lambda i: (i, 0)),
                   pl.BlockSpec((tq, 128), lambda i: (i, 0))],
        out_shape=[jax.ShapeDtypeStruct((t, nh * DH_F), BF16), jax.ShapeDtypeStruct((t, nh * DH_F), F32),
                   jax.ShapeDtypeStruct((t, 128), F32)],
        compiler_params=_cparams(("parallel",)),
    )(qa, ka, va, gf)


def _fox_bwd_prep(dycat, o, gf):
    t = o.shape[0]
    tm = _pick(t, (512, 256))

    def body(dy_ref, o_ref, g_ref, do_ref, dl_ref, dg_ref):
        lane = lax.broadcasted_iota(jnp.int32, (tm, 128), 1)
        dyv, ov, gv = dy_ref[...], o_ref[...], g_ref[...]
        dgs = []
        dl = jnp.zeros((tm, 128), F32)
        pad = jnp.zeros((tm, AUG), BF16)
        for h in range(NH_F):
            sl = slice(h * DH_F, (h + 1) * DH_F)
            dx, dg = _rms_bwd_val(dyv[:, sl], ov[:, sl], gv[:, sl])
            dgs.append(dg)
            do_ref[h] = jnp.concatenate([dx.astype(BF16), pad], axis=1)
            dl = dl + jnp.where(lane == h, jnp.sum(dx * ov[:, sl], axis=-1, keepdims=True), 0.0)
        dl_ref[...] = dl

        @pl.when(pl.program_id(0) == 0)
        def _():
            dg_ref[...] = jnp.zeros_like(dg_ref)
        dg_ref[...] += jnp.concatenate(dgs, axis=1)

    return pl.pallas_call(
        body, name="fox_bwd_prep", grid=(t // tm,),
        in_specs=[pl.BlockSpec((tm, 512), lambda i: (i, 1)), pl.BlockSpec((tm, 512), lambda i: (i, 0)),
                  pl.BlockSpec((1, 512), lambda i: (0, 0))],
        out_specs=[pl.BlockSpec((NH_F, tm, 128), lambda i: (0, i, 0)), pl.BlockSpec((tm, 128), lambda i: (i, 0)),
                   pl.BlockSpec((1, 512), lambda i: (0, 0))],
        out_shape=[jax.ShapeDtypeStruct((NH_F, t, 128), BF16), jax.ShapeDtypeStruct((t, 128), F32),
                   jax.ShapeDtypeStruct((1, 512), F32)],
        compiler_params=_cparams(("arbitrary",)),
    )(dycat, o, gf)


def _fox_bwd2(qa, ka, va, doa, lse, delta):
    nh, t, _ = qa.shape
    tq = _pick(t, (512, 256))
    nq = t // tq

    def body(q_ref, k_ref, v_ref, do_ref, lse_ref, dl_ref, dq_ref, dk_ref, dv_ref):
        h, j = pl.program_id(0), pl.program_id(1)

        @pl.when(j == 0)
        def _():
            dq_ref[...] = jnp.zeros_like(dq_ref)

        kb, vb = k_ref[...], v_ref[...]
        lane = lax.broadcasted_iota(jnp.int32, (tq, 128), 1)

        def blk(i, carry, masked):
            dk, dv = carry
            rows = pl.ds(pl.multiple_of(i * tq, tq), tq)
            qb, dob = q_ref[rows, :], do_ref[rows, :]
            lse_h = jnp.sum(jnp.where(lane == h, lse_ref[rows, :], 0.0), axis=1, keepdims=True)
            dl_h = jnp.sum(jnp.where(lane == h, dl_ref[rows, :], 0.0), axis=1, keepdims=True)
            s = lax.dot_general(qb, kb, (((1,), (1,)), ((), ())), preferred_element_type=F32)
            if masked:
                s = jnp.where(_causal_mask(tq), s, -jnp.inf)
            p = jnp.exp(s - lse_h)
            dp = lax.dot_general(dob, vb, (((1,), (1,)), ((), ())), preferred_element_type=F32)
            ds = (p * (dp - dl_h)).astype(BF16)
            dv = dv + lax.dot_general(p.astype(BF16), dob, (((0,), (0,)), ((), ())), preferred_element_type=F32)
            dk = dk + lax.dot_general(ds, qb, (((0,), (0,)), ((), ())), preferred_element_type=F32)
            dq_ref[rows, :] += lax.dot_general(ds, kb, (((1,), (0,)), ((), ())), preferred_element_type=F32)
            return dk, dv

        init = (jnp.zeros((tq, 128), F32), jnp.zeros((tq, 128), F32))
        carry = blk(j, init, True)
        dk, dv = lax.fori_loop(j + 1, nq, lambda i, c: blk(i, c, False), carry)
        dk_ref[...] = dk
        dv_ref[...] = dv

    full = pl.BlockSpec((None, t, 128), lambda h, j: (h, 0, 0))
    tile = pl.BlockSpec((None, tq, 128), lambda h, j: (h, j, 0))
    cols = pl.BlockSpec((t, 128), lambda h, j: (0, 0))
    return pl.pallas_call(
        body, name="fox_bwd", grid=(nh, nq), in_specs=[full, tile, tile, full, cols, cols], out_specs=[full, tile, tile],
        out_shape=[jax.ShapeDtypeStruct((nh, t, 128), F32)] * 3, compiler_params=_cparams(("parallel", "arbitrary")),
    )(qa, ka, va, doa, lse, delta)


def _fox_bwd_post(dqa, dka, dva):
    nh, t, _ = dqa.shape
    tm = _pick(t, (512, 256))

    def body(dq_ref, dk_ref, dv_ref, oq_ref, ok_ref, ov_ref, dc_ref):
        lane = lax.broadcasted_iota(jnp.int32, (tm, 128), 1)
        dc = jnp.zeros((tm, 128), F32)
        qs, ks, vs = [], [], []
        for h in range(nh):
            dq, dk = dq_ref[h], dk_ref[h]
            qs.append(dq[:, :DH_F] * (DH_F ** -0.5))
            ks.append(dk[:, :DH_F])
            vs.append(dv_ref[h][:, :DH_F])
            dc = dc + jnp.where(lane == h, dq[:, DH_F:DH_F + 1] - dk[:, DH_F + 3:DH_F + 4], 0.0)
        oq_ref[...] = jnp.concatenate(qs, axis=1).astype(BF16)
        ok_ref[...] = jnp.concatenate(ks, axis=1).astype(BF16)
        ov_ref[...] = jnp.concatenate(vs, axis=1).astype(BF16)
        dc_ref[...] = dc

    ispec = pl.BlockSpec((nh, tm, 128), lambda i: (0, i, 0))
    ospec = pl.BlockSpec((tm, nh * DH_F), lambda i: (i, 0))
    return pl.pallas_call(
        body, name="fox_bwd_post", grid=(t // tm,), in_specs=[ispec] * 3,
        out_specs=[ospec] * 3 + [pl.BlockSpec((tm, 128), lambda i: (i, 0))],
        out_shape=[jax.ShapeDtypeStruct((t, nh * DH_F), BF16)] * 3 + [jax.ShapeDtypeStruct((t, 128), F32)],
        compiler_params=_cparams(("parallel",)),
    )(dqa, dka, dva)


W_BIG = 6 * 512
IN_OFF = (0, 512, 1024, 1544, 2056, 2568)
IN_GATES = (1536, 3080)


def _heads(a, nh):
    t = a.shape[0]
    return a.reshape(t, nh, -1).transpose(1, 0, 2)


def _unheads(a):
    nh, t, dh = a.shape
    return a.transpose(1, 0, 2).reshape(t, nh * dh)


FFN1 = ("ffn1_w_gate", "ffn1_w_up", "ffn1_w_down")
REST = ("w_in", "w_out", "ffn2_w_gate", "ffn2_w_up", "ffn2_w_down", "w_ple_gate", "w_ple_proj")
SPLIT = {n: 1 if n == "w_in" else 0 for n in FFN1 + REST}


def _rs_partials(names, gw, c_idx):
    wire = [_cast_other_half("rs_cast_" + n, gw[n], c_idx, SPLIT[n]) for n in names]
    swapped = _swap("rs_swap_" + names[0], wire)
    return [_add_my_half("rs_add_" + n, gw[n], r, c_idx, SPLIT[n]) for n, r in zip(names, swapped)]


def _local_step(x, p, tgt, sp, wg1, wu1, wd1, rest_slots, c_idx, place):
    t, d = x.shape
    (h1, xn1, g1, u1), rest = _ffn_fwd("ffn1", x, sp["ffn1_norm"], wg1, wu1, wd1,
                                       plan=_gather_plan(rest_slots, [SPLIT[n] for n in REST] + [None]))
    full = dict(zip(REST + ("conv_qk",), rest))
    w_in, w_out, w_pg = (full[n].reshape(-1, d) for n in ("w_in", "w_out", "w_ple_gate"))
    wg2, wu2, wd2 = full["ffn2_w_gate"], full["ffn2_w_up"], full["ffn2_w_down"]
    w_pp, conv_w = _from_chip_blocks(full["w_ple_proj"]), _from_chip_blocks(full["conv_qk"])
    w_big = jnp.concatenate([w_in[o:o + 512] for o in IN_OFF], axis=0)
    w_small = jnp.concatenate([w_in[IN_GATES[0]:IN_GATES[0] + 8], w_in[IN_GATES[1]:IN_GATES[1] + 8],
                               jnp.zeros((112, d), w_in.dtype)], axis=0)
    u, zbig = _norm_mm("in_big", h1, sp["mix_norm"], w_big, True, BF16)
    zs = _mm_nt("in_small", u, w_small, tm=1024, tk=1024)
    zsr = zs.T
    qk_act = _conv_fwd(zbig, conv_w)
    bm_c, bf_c = sp["b_mlstm_gates"], sp["b_fox_f"]
    y_m, cst, mst = _mlstm_fwd(qk_act, zbig, zs, zsr, bm_c, bm_c.T, sp["mlstm_out_norm"])
    c = _fox_cumsum(zsr, bf_c.T)
    qa, ka, va = _fox_prep(zbig, c.T)
    y_ft, o_f, lse = _fox_fwd2(qa, ka, va, sp["fox_out_norm"])
    tm = _pick(t, (1024, 512, 256))
    h2 = _mm("out_proj", [(y_m, (tm, 512), lambda i, j, k: (i, 0), w_out, (512, d), lambda i, j, k: (0, 0)),
                          (y_ft, (tm, 512), lambda i, j, k: (i, 0), w_out, (512, d), lambda i, j, k: (1, 0))],
             (t, d), (tm, d), lambda i, j, k: (i, 0), (t // tm, 1, 1), 2, res=h1)
    (h3, xn2, g2, u2), _ = _ffn_fwd("ffn2", h2, sp["ffn2_norm"], wg2, wu2, wd2)
    hn3, gate_pre = _norm_mm("ple_gate", h3, sp["ple_gate_norm"], w_pg, False, F32)
    pp = _mm_nn("ple_proj", p, w_pp, tm=1024)

    def head_fn(h3_t, gp_t, pp_t, tgt_t, g_pp, g_fin):
        gate = _sigmoid(gp_t)
        ppn = _rms_fwd_val(pp_t, g_pp)
        h4 = h3_t + gate * ppn
        err = _rms_fwd_val(h4, g_fin) - tgt_t
        loss = 0.5 * jnp.sum(jnp.mean(err * err, axis=-1, keepdims=True), axis=0, keepdims=True)
        dh4, dg_fin = _rms_bwd_val(err * (1.0 / d), h4, g_fin)
        dpp, dg_pp = _rms_bwd_val(dh4 * gate, pp_t, g_pp)
        dgp = dh4 * ppn * gate * (1.0 - gate)
        return dh4, dgp, dpp, jnp.broadcast_to(loss, (1, 128)), dg_fin, dg_pp

    dh4, dgp, dpp, loss_part, dg_fin, dg_pp = _rowwise(
        "loss_head", head_fn, [h3, gate_pre, pp, tgt], [sp["ple_proj_norm"], sp["final_norm"]],
        [(d, F32), (d, BF16), (d, BF16)], [((1, 128), F32), ((1, d), F32), ((1, d), F32)])
    gw, gs = {}, {"final_norm": dg_fin, "ple_proj_norm": dg_pp}
    gw["w_ple_gate"] = _mm_tn("d_w_pg", hn3, dgp, tm=1024, tn=1024)
    gw["w_ple_proj"] = _mm_tn("d_w_pp", p, dpp, tn=1024)
    dhn3 = _mm_nt("d_hn3", dgp, w_pg, tm=1024, tn=1024, tk=1024)

    def res_norm_bwd(dn_t, h_t, dres_t, g):
        dx, dg = _rms_bwd_val(dn_t, h_t, g)
        return dres_t + dx, dg

    dh3, gs["ple_gate_norm"] = _rowwise("ple_norm_bwd", res_norm_bwd, [dhn3, h3, dh4], [sp["ple_gate_norm"]],
                                        [(d, F32)], [((1, d), F32)])
    (dh2, gs["ffn2_norm"], gw["ffn2_w_gate"], gw["ffn2_w_up"], gw["ffn2_w_down"]), _ = _ffn_bwd(
        "ffn2", dh3, h2, sp["ffn2_norm"], xn2, g2, u2, wg2, wu2, wd2)
    dycat = _mm_nt("d_ycat", dh2, w_out, tm=1024, tn=1024, tk=1024)
    gw["w_out"] = jnp.concatenate([_mm_tn("d_w_out_m", y_m, dh2, tn=1024), _mm_tn("d_w_out_f", y_ft, dh2, tn=1024)], axis=0)
    doa, delta, gs["fox_out_norm"] = _fox_bwd_prep(dycat, o_f, sp["fox_out_norm"])
    dq_f, dk_f, dv_f, dct = _fox_bwd_post(*_fox_bwd2(qa, ka, va, doa, lse, delta))
    dfp = _fox_gate_bwd(zsr, bf_c.T, dct[:, :NH_F].T)
    dact, dv_m, do_m, dzs_m, dzr_m, gs["mlstm_out_norm"] = _mlstm_bwd(
        qk_act, zbig, zs, zsr, bm_c, bm_c.T, sp["mlstm_out_norm"], cst, mst, dycat)
    dqk, gw["conv_qk"] = _conv_bwd(zbig, dact, conv_w)
    dz_big = jnp.concatenate([dqk, dv_m, do_m, dq_f, dk_f, dv_f], axis=1)
    dzs = dzs_m + jnp.pad(jnp.concatenate([dzr_m, dfp], axis=0).T, ((0, 0), (0, 112)))
    dw_big = _mm_tn("d_w_big", dz_big, u, tn=1024)
    dw_small = _mm_tn("d_w_small", dzs, u, tn=1024)
    gw["w_in"] = jnp.concatenate([dw_big[0:1536], dw_small[0:8], dw_big[1536:3072], dw_small[8:16]], axis=0)
    du_a = _mm_nn("d_u_big", dz_big, w_big, tm=1024, tn=1024, tk=1024)
    du_b = _mm_nn("d_u_small", dzs, w_small, tm=1024, tn=1024)

    def mix_norm_bwd(da_t, db_t, h_t, dres_t, dzs_t, g):
        dx, dg = _rms_bwd_val(da_t + db_t, h_t, g)
        return dres_t + dx, dg, _colsum(dzs_t)

    dh1, gs["mix_norm"], dbias = _rowwise("mix_norm_bwd", mix_norm_bwd, [du_a, du_b, h1, dh2, dzs], [sp["mix_norm"]],
                                          [(d, F32)], [((1, d), F32), ((1, 128), F32)])
    gs["b_mlstm_gates"], gs["b_fox_f"] = dbias[:, 0:8], dbias[:, 8:16]
    conv_grad = gw.pop("conv_qk")
    gw["w_ple_proj"] = _chip_blocks(gw["w_ple_proj"])
    for n in ("w_in", "w_out", "w_ple_gate"):
        gw[n] = gw[n].reshape(4, -1, gw[n].shape[-1])
    part_rest = _rs_partials(REST, gw, c_idx)
    (grad_x, gs["ffn1_norm"], gw["ffn1_w_gate"], gw["ffn1_w_up"], gw["ffn1_w_down"]), landed_rest = _ffn_bwd(
        "ffn1", dh1, x, sp["ffn1_norm"], xn1, g1, u1, wg1, wu1, wd1, plan=_scatter_plan([pb for _, pb in part_rest]))
    part_ffn1 = _rs_partials(FFN1, gw, c_idx)
    landed_ffn1 = _scatter4("rs_scatter_ffn1", [pb for _, pb in part_ffn1])
    names, parts, landed = REST + FFN1, part_rest + part_ffn1, list(landed_rest) + list(landed_ffn1)
    mine = [_sum4("rs_sum_" + n, a, pf, place, SPLIT[n]) for n, a, (pf, _) in zip(names, landed, parts)]
    grads = dict(zip(names, _join_halves("rs_join", mine, [SPLIT[n] for n in names])))
    return loss_part, grad_x, grads, gs, conv_grad


ANY = pl.BlockSpec(memory_space=pl.ANY)
MESH = pl.DeviceIdType.MESH


def _place():
    x, y, c = lax.axis_index("x"), lax.axis_index("y"), lax.axis_index("c")
    chips = [(1 - x, y), (x, 1 - y), (1 - x, 1 - y)]
    return x, y, c, 2 * x + y, (x, y, 1 - c), chips


def _rcopy(src, dst, ssem, rsem, dev):
    return pltpu.make_async_remote_copy(src_ref=src, dst_ref=dst, send_sem=ssem, recv_sem=rsem, device_id=dev,
                                        device_id_type=MESH)


def _half(ref, lead, axis, idx, half):
    return ref.at[(slice(None),) * (lead + axis) + (pl.ds(idx * half, half),)]


def _to_slot(name, a, me_idx, dtype):
    r, cdim = a.shape
    tr = _pick(r, (256, 176, 128, 64))

    def body(me_ref, a_ref, o_ref):
        o_ref[...] = a_ref[...].astype(o_ref.dtype)

    return pl.pallas_call(
        body, name=name,
        grid_spec=pltpu.PrefetchScalarGridSpec(
            num_scalar_prefetch=1, grid=(r // tr,), in_specs=[pl.BlockSpec((tr, cdim), lambda i, me_ref: (i, 0))],
            out_specs=pl.BlockSpec((None, tr, cdim), lambda i, me_ref: (me_ref[0], i, 0))),
        out_shape=jax.ShapeDtypeStruct((4, r, cdim), dtype), compiler_params=_cparams(("parallel",)),
    )(me_idx, a)


def _gather4(name, bufs, split):
    return _run_plan(name, _gather_plan(bufs, split))


def _gather_plan(bufs, split):
    n = len(bufs)
    shapes = [b.shape[1:] for b in bufs]

    def ctx(outs):
        x, y, c, me, sib, chips = _place()

        def part(ref, a, which):
            if split[a] is None:
                return ref
            return _half(ref, 0, split[a], which, shapes[a][split[a]] // 2)

        return c, me, sib, chips, part

    def ici(outs, sems, a, j, chip, c, me, part):
        mine = part(outs[a].at[me], a, c)
        return _rcopy(mine, mine, sems[0].at[3 * a + j], sems[1].at[3 * a + j], (*chip, c))

    def fwd(outs, sems, a, j, chip, c, sib, part, which):
        blk = part(outs[a].at[2 * chip[0] + chip[1]], a, which)
        return _rcopy(blk, blk, sems[2].at[3 * a + j], sems[3].at[3 * a + j], sib)

    def start(ins, outs, sems):
        c, me, sib, chips, part = ctx(outs)
        for a in range(n):
            for j, chip in enumerate(chips):
                ici(outs, sems, a, j, chip, c, me, part).start()

    def mid(ins, outs, sems):
        c, me, sib, chips, part = ctx(outs)
        for j, chip in enumerate(chips):
            for a in range(n):
                blk = part(outs[a].at[2 * chip[0] + chip[1]], a, c)
                _rcopy(blk, blk, sems[0].at[3 * a + j], sems[1].at[3 * a + j], sib).wait_recv()
                if split[a] is not None:
                    fwd(outs, sems, a, j, chip, c, sib, part, c).start()

    def end(ins, outs, sems):
        c, me, sib, chips, part = ctx(outs)
        for j, chip in enumerate(chips):
            for a in range(n):
                if split[a] is not None:
                    fwd(outs, sems, a, j, chip, c, sib, part, 1 - c).wait_recv()
        for a in range(n):
            for j, chip in enumerate(chips):
                ici(outs, sems, a, j, chip, c, me, part).wait_send()
                if split[a] is not None:
                    fwd(outs, sems, a, j, chip, c, sib, part, c).wait_send()

    return dict(ins=list(bufs), outs=[jax.ShapeDtypeStruct(b.shape, b.dtype) for b in bufs], alias=True,
                sems=[pltpu.SemaphoreType.DMA((3 * n,))] * 4, phases=(start, mid, end))


def _run_plan(name, plan):
    ni, no = len(plan["ins"]), len(plan["outs"])

    def body(*refs):
        ins, outs, sems = refs[:ni], refs[ni:ni + no], refs[ni + no:]
        for phase in plan["phases"]:
            phase(ins, outs, sems)

    return pl.pallas_call(
        body, name=name, in_specs=[ANY] * ni, out_specs=[ANY] * no, out_shape=plan["outs"],
        input_output_aliases={a: a for a in range(ni)} if plan["alias"] else {}, scratch_shapes=plan["sems"],
    )(*plan["ins"])


class _Hosted:
    def __init__(self, plan, n_in, n_out):
        self.plan, self.n_in, self.n_out = plan, n_in, n_out
        self.ni, self.no, self.ns = (len(plan["ins"]) if plan else 0, len(plan["outs"]) if plan else 0,
                                     len(plan["sems"]) if plan else 0)

    def split(self, refs):
        a, b = self.n_in, self.n_in + self.ni
        c, d = b + self.n_out, b + self.n_out + self.no
        e = len(refs) - self.ns
        return refs[:a], refs[b:c], refs[d:e], (refs[a:b], refs[c:d], refs[e:])

    def run(self, k, cond, prefs):
        if self.plan is not None:
            @pl.when(cond)
            def _():
                self.plan["phases"][k](*prefs)

    def call_args(self):
        p = self.plan
        if p is None:
            return dict(in_specs=[], out_specs=[], out_shape=[], scratch=[], aliases={}, args=[])
        al = {self.n_in + a: self.n_out + a for a in range(self.ni)} if p["alias"] else {}
        return dict(in_specs=[ANY] * self.ni, out_specs=[ANY] * self.no, out_shape=list(p["outs"]), scratch=list(p["sems"]),
                    aliases=al, args=list(p["ins"]))


def _swap(name, arrs):
    n = len(arrs)

    def body(*refs):
        ins, outs = refs[:n], refs[n:2 * n]
        ssem, rsem = refs[2 * n:]
        x, y, c, me, sib, chips = _place()
        cps = [_rcopy(ins[a], outs[a], ssem.at[a], rsem.at[a], sib) for a in range(n)]
        for cp in cps:
            cp.start()
        for cp in cps:
            cp.wait()

    return pl.pallas_call(
        body, name=name, in_specs=[ANY] * n, out_specs=[ANY] * n,
        out_shape=[jax.ShapeDtypeStruct(a.shape, a.dtype) for a in arrs],
        scratch_shapes=[pltpu.SemaphoreType.DMA((n,))] * 2,
    )(*arrs)


def _scatter4(name, arrs):
    return _run_plan(name, _scatter_plan(arrs))


def _scatter_plan(arrs):
    n = len(arrs)

    def send(ins, outs, sems, a, j, chip, c, me):
        return _rcopy(ins[a].at[2 * chip[0] + chip[1]], outs[a].at[me], sems[0].at[3 * a + j], sems[1].at[3 * a + j], (*chip, c))

    def start(ins, outs, sems):
        x, y, c, me, sib, chips = _place()
        for a in range(n):
            for j, chip in enumerate(chips):
                send(ins, outs, sems, a, j, chip, c, me).start()

    def mid(ins, outs, sems):
        pass

    def end(ins, outs, sems):
        x, y, c, me, sib, chips = _place()
        for a in range(n):
            for j, chip in enumerate(chips):
                blk = outs[a].at[2 * chip[0] + chip[1]]
                _rcopy(blk, blk, sems[0].at[3 * a + j], sems[1].at[3 * a + j], sib).wait_recv()
        for a in range(n):
            for j, chip in enumerate(chips):
                send(ins, outs, sems, a, j, chip, c, me).wait_send()

    return dict(ins=list(arrs), outs=[jax.ShapeDtypeStruct(a.shape, a.dtype) for a in arrs], alias=False,
                sems=[pltpu.SemaphoreType.DMA((3 * n,))] * 2, phases=(start, mid, end))


def _join_halves(name, arrs, split):
    n = len(arrs)

    def body(*refs):
        outs = refs[n:2 * n]
        ssem, rsem = refs[2 * n:]
        x, y, c, me, sib, chips = _place()
        cps = []
        for a in range(n):
            mine = _half(outs[a], 0, split[a], c, arrs[a].shape[split[a]] // 2)
            cp = _rcopy(mine, mine, ssem.at[a], rsem.at[a], sib)
            cp.start()
            cps.append(cp)
        for a in range(n):
            blk = _half(outs[a], 0, split[a], 1 - c, arrs[a].shape[split[a]] // 2)
            _rcopy(blk, blk, ssem.at[a], rsem.at[a], sib).wait_recv()
        for cp in cps:
            cp.wait_send()

    return pl.pallas_call(
        body, name=name, in_specs=[ANY] * n, out_specs=[ANY] * n,
        out_shape=[jax.ShapeDtypeStruct(a.shape, a.dtype) for a in arrs],
        input_output_aliases={a: a for a in range(n)}, scratch_shapes=[pltpu.SemaphoreType.DMA((n,))] * 2,
    )(*arrs)


def _allreduce_small(s):
    r, cdim = s.shape

    def body(s_ref, o_ref, buf, ssem, rsem):
        x, y, c, me, sib, chips = _place()
        me8 = 4 * x + 2 * y + c
        buf[me8] = s_ref[...]
        flips = [(fx, fy, fc) for fx in (0, 1) for fy in (0, 1) for fc in (0, 1)][1:]
        cps = []
        for k, (fx, fy, fc) in enumerate(flips):
            peer = (x ^ fx if fx else x, y ^ fy if fy else y, c ^ fc if fc else c)
            cp = _rcopy(s_ref, buf.at[me8], ssem.at[k], rsem.at[k], peer)
            cp.start()
            cps.append(cp)
        for k, (fx, fy, fc) in enumerate(flips):
            src = 4 * (x ^ fx if fx else x) + 2 * (y ^ fy if fy else y) + (c ^ fc if fc else c)
            _rcopy(s_ref, buf.at[src], ssem.at[k], rsem.at[k], sib).wait_recv()
        for cp in cps:
            cp.wait_send()
        acc = buf[0]
        for k in range(1, 8):
            acc = acc + buf[k]
        o_ref[...] = acc

    vm = pl.BlockSpec(memory_space=pltpu.VMEM)
    return pl.pallas_call(
        body, name="allreduce_small", in_specs=[vm], out_specs=vm, out_shape=jax.ShapeDtypeStruct((r, cdim), F32),
        scratch_shapes=[pltpu.VMEM((8, r, cdim), F32), pltpu.SemaphoreType.DMA((7,)), pltpu.SemaphoreType.DMA((7,))],
    )(s)


def _add_my_half(name, g, recv, c_idx, axis):
    nb, hr, hc = recv.shape
    tr = _pick(hr, (256, 176, 128, 64))
    if axis == 0:
        g4 = g.reshape(nb, 2, hr, hc)
        gspec = pl.BlockSpec((None, None, tr, hc), lambda b, i, c_ref: (b, c_ref[0], i, 0))
    else:
        g4 = g
        gspec = pl.BlockSpec((None, tr, hc), lambda b, i, c_ref: (b, i, c_ref[0]))

    def body(c_ref, g_ref, r_ref, o_ref, ob_ref):
        s = g_ref[...] + r_ref[...].astype(F32)
        o_ref[...] = s
        ob_ref[...] = s.astype(BF16)

    ospec = pl.BlockSpec((None, tr, hc), lambda b, i, c_ref: (b, i, 0))
    return pl.pallas_call(
        body, name=name,
        grid_spec=pltpu.PrefetchScalarGridSpec(
            num_scalar_prefetch=1, grid=(nb, hr // tr), in_specs=[gspec, ospec], out_specs=[ospec, ospec]),
        out_shape=[jax.ShapeDtypeStruct((nb, hr, hc), F32), jax.ShapeDtypeStruct((nb, hr, hc), BF16)],
        compiler_params=_cparams(("parallel", "parallel")),
    )(c_idx, g4, recv)


def _sum4(name, landed, own, place, axis):
    nb, h, cdim = landed.shape
    tr = _pick(h, (256, 176, 128, 64))
    nt = h // tr

    def body(p_ref, a1_ref, a2_ref, a3_ref, own_ref, o_ref):
        o_ref[...] = ((own_ref[...] + a1_ref[...].astype(F32)) + a2_ref[...].astype(F32)) + a3_ref[...].astype(F32)

    def nxt(k):
        return pl.BlockSpec((None, tr, cdim), lambda i, p_ref: ((p_ref[0] + k) % nb, i, 0))

    if axis == 0:
        ospec = pl.BlockSpec((tr, cdim), lambda i, p_ref: (p_ref[1] * nt + i, 0))
        oshape = (2 * h, cdim)
    else:
        ospec = pl.BlockSpec((tr, cdim), lambda i, p_ref: (i, p_ref[1]))
        oshape = (h, 2 * cdim)
    return pl.pallas_call(
        body, name=name,
        grid_spec=pltpu.PrefetchScalarGridSpec(
            num_scalar_prefetch=1, grid=(nt,), in_specs=[nxt(1), nxt(2), nxt(3), nxt(0)], out_specs=ospec),
        out_shape=jax.ShapeDtypeStruct(oshape, F32), compiler_params=_cparams(("parallel",)),
    )(place, landed, landed, landed, own)


def _cast_other_half(name, g, c_idx, axis):
    nb, r, cdim = g.shape
    hr, hc = (r // 2, cdim) if axis == 0 else (r, cdim // 2)
    tr = _pick(hr, (256, 176, 128, 64))
    if axis == 0:
        g4 = g.reshape(nb, 2, hr, hc)
        gspec = pl.BlockSpec((None, None, tr, hc), lambda b, i, c_ref: (b, 1 - c_ref[0], i, 0))
    else:
        g4 = g
        gspec = pl.BlockSpec((None, tr, hc), lambda b, i, c_ref: (b, i, 1 - c_ref[0]))

    def body(c_ref, g_ref, o_ref):
        o_ref[...] = g_ref[...].astype(BF16)

    return pl.pallas_call(
        body, name=name,
        grid_spec=pltpu.PrefetchScalarGridSpec(
            num_scalar_prefetch=1, grid=(nb, hr // tr), in_specs=[gspec],
            out_specs=pl.BlockSpec((None, tr, hc), lambda b, i, c_ref: (b, i, 0))),
        out_shape=jax.ShapeDtypeStruct((nb, hr, hc), BF16), compiler_params=_cparams(("parallel", "parallel")),
    )(c_idx, g4)


def _adamw(name, w, g, m, v):
    c1 = 1.0 - ADAM_B1 ** ADAM_STEP
    c2 = 1.0 - ADAM_B2 ** ADAM_STEP

    def fn(w_t, g_t, m_t, v_t):
        m_n = ADAM_B1 * m_t + (1.0 - ADAM_B1) * g_t
        v_n = ADAM_B2 * v_t + (1.0 - ADAM_B2) * (g_t * g_t)
        delta = -ADAM_LR * ((m_n / c1) / (jnp.sqrt(v_n / c2) + ADAM_EPS) + ADAM_WD * w_t)
        return delta, m_n, v_n

    cdim = w.shape[1]
    return _rowwise(name, fn, [w, g, m, v], [], [(cdim, F32)] * 3, tm=_pick(w.shape[0], (256, 176, 128, 64, 8)))


BIG = ("ffn1_w_gate", "ffn1_w_up", "ffn1_w_down", "w_in", "w_out", "ffn2_w_gate", "ffn2_w_up", "ffn2_w_down",
       "w_ple_gate", "w_ple_proj")
SMALL = ("ffn1_norm", "mix_norm", "b_mlstm_gates", "b_fox_f", "mlstm_out_norm", "fox_out_norm", "ffn2_norm",
         "ple_gate_norm", "ple_proj_norm", "final_norm")
WEIGHTS = ("ffn1_norm", "ffn1_w_gate", "ffn1_w_up", "ffn1_w_down", "mix_norm", "w_in", "conv_qk", "b_mlstm_gates",
           "b_fox_f", "mlstm_out_norm", "fox_out_norm", "w_out", "ffn2_norm", "ffn2_w_gate", "ffn2_w_up", "ffn2_w_down",
           "ple_gate_norm", "w_ple_gate", "w_ple_proj", "ple_proj_norm", "final_norm")
TRANSPOSED = ("ffn1_w_gate", "ffn1_w_up", "w_in", "ffn2_w_gate", "ffn2_w_up")
PACK_W = 1024


def _chip_blocks(a):
    r, c4 = a.shape
    return a.reshape(r, 4, c4 // 4).transpose(1, 0, 2)


def _from_chip_blocks(a):
    nb, r, c = a.shape
    return a.transpose(1, 0, 2).reshape(r, nb * c)


def kernel(x, p, ffn1_norm, ffn1_w_gate, ffn1_w_up, ffn1_w_down, mix_norm, w_in, conv_qk, b_mlstm_gates, b_fox_f, mlstm_out_norm, fox_out_norm, w_out, ffn2_norm, ffn2_w_gate, ffn2_w_up, ffn2_w_down, ple_gate_norm, w_ple_gate, w_ple_proj, ple_proj_norm, final_norm, loss_target, m_ffn1_norm, m_ffn1_w_gate, m_ffn1_w_up, m_ffn1_w_down, m_mix_norm, m_w_in, m_conv_qk, m_b_mlstm_gates, m_b_fox_f, m_mlstm_out_norm, m_fox_out_norm, m_w_out, m_ffn2_norm, m_ffn2_w_gate, m_ffn2_w_up, m_ffn2_w_down, m_ple_gate_norm, m_w_ple_gate, m_w_ple_proj, m_ple_proj_norm, m_final_norm, v_ffn1_norm, v_ffn1_w_gate, v_ffn1_w_up, v_ffn1_w_down, v_mix_norm, v_w_in, v_conv_qk, v_b_mlstm_gates, v_b_fox_f, v_mlstm_out_norm, v_fox_out_norm, v_w_out, v_ffn2_norm, v_ffn2_w_gate, v_ffn2_w_up, v_ffn2_w_down, v_ple_gate_norm, v_w_ple_gate, v_w_ple_proj, v_ple_proj_norm, v_final_norm):
    w = dict(ffn1_norm=ffn1_norm, ffn1_w_gate=ffn1_w_gate, ffn1_w_up=ffn1_w_up, ffn1_w_down=ffn1_w_down, mix_norm=mix_norm,
             w_in=w_in, conv_qk=conv_qk, b_mlstm_gates=b_mlstm_gates, b_fox_f=b_fox_f, mlstm_out_norm=mlstm_out_norm,
             fox_out_norm=fox_out_norm, w_out=w_out, ffn2_norm=ffn2_norm, ffn2_w_gate=ffn2_w_gate, ffn2_w_up=ffn2_w_up,
             ffn2_w_down=ffn2_w_down, ple_gate_norm=ple_gate_norm, w_ple_gate=w_ple_gate, w_ple_proj=w_ple_proj,
             ple_proj_norm=ple_proj_norm, final_norm=final_norm)
    m = dict(ffn1_norm=m_ffn1_norm, ffn1_w_gate=m_ffn1_w_gate, ffn1_w_up=m_ffn1_w_up, ffn1_w_down=m_ffn1_w_down,
             mix_norm=m_mix_norm, w_in=m_w_in, conv_qk=m_conv_qk, b_mlstm_gates=m_b_mlstm_gates, b_fox_f=m_b_fox_f,
             mlstm_out_norm=m_mlstm_out_norm, fox_out_norm=m_fox_out_norm, w_out=m_w_out, ffn2_norm=m_ffn2_norm,
             ffn2_w_gate=m_ffn2_w_gate, ffn2_w_up=m_ffn2_w_up, ffn2_w_down=m_ffn2_w_down, ple_gate_norm=m_ple_gate_norm,
             w_ple_gate=m_w_ple_gate, w_ple_proj=m_w_ple_proj, ple_proj_norm=m_ple_proj_norm, final_norm=m_final_norm)
    v = dict(ffn1_norm=v_ffn1_norm, ffn1_w_gate=v_ffn1_w_gate, ffn1_w_up=v_ffn1_w_up, ffn1_w_down=v_ffn1_w_down,
             mix_norm=v_mix_norm, w_in=v_w_in, conv_qk=v_conv_qk, b_mlstm_gates=v_b_mlstm_gates, b_fox_f=v_b_fox_f,
             mlstm_out_norm=v_mlstm_out_norm, fox_out_norm=v_fox_out_norm, w_out=v_w_out, ffn2_norm=v_ffn2_norm,
             ffn2_w_gate=v_ffn2_w_gate, ffn2_w_up=v_ffn2_w_up, ffn2_w_down=v_ffn2_w_down, ple_gate_norm=v_ple_gate_norm,
             w_ple_gate=v_w_ple_gate, w_ple_proj=v_w_ple_proj, ple_proj_norm=v_ple_proj_norm, final_norm=v_final_norm)
    shapes = {n: w[n].shape for n in WEIGHTS}

    def view(a, n):
        return a[0].T if n in TRANSPOSED else a.reshape(-1, a.shape[-1])

    def unview(a, n):
        return (a.T if n in TRANSPOSED else a).reshape(shapes[n])

    w2, m2, v2 = ({n: view(a, n) for n, a in d.items()} for d in (w, m, v))

    c_idx = lax.axis_index("c").astype(jnp.int32).reshape(1)
    me_idx = (2 * lax.axis_index("x") + lax.axis_index("y")).astype(jnp.int32).reshape(1)
    place = jnp.concatenate([me_idx, c_idx])
    slot = {n: _to_slot("slot_" + n, w2[n], me_idx, BF16) for n in BIG}
    slot["conv_qk"] = _to_slot("slot_conv_qk", w2["conv_qk"], me_idx, F32)
    wg1, wu1, wd1 = _gather4("gather_ffn1", [slot[n] for n in FFN1], [SPLIT[n] for n in FFN1])
    sp = {n: w2[n] for n in SMALL}
    loss_part, grad_x, grads, gs, conv_grad = _local_step(
        x[0], p[0, 0], loss_target[0], sp, wg1, wu1, wd1, [slot[n] for n in REST + ("conv_qk",)], c_idx, place)
    loss = lax.psum(loss_part[0, 0], ("x", "y", "c"))

    small = [gs[n].reshape(1, -1) for n in SMALL] + [conv_grad]
    rows = [jnp.pad(a, ((0, 0), (0, PACK_W - a.shape[1]))) for a in small]
    packed = jnp.concatenate(rows, axis=0)
    packed = jnp.pad(packed, ((0, -packed.shape[0] % 8), (0, 0)))
    red = _allreduce_small(packed)
    for i, n in enumerate(SMALL):
        grads[n] = red[i:i + 1, :gs[n].size]
    dconv = red[len(SMALL):len(SMALL) + CONV_W, :conv_grad.shape[1]]
    cw = conv_qk.shape[-1]
    grads["conv_qk"] = lax.dynamic_slice_in_dim(dconv, (2 * lax.axis_index("x") + lax.axis_index("y")) * cw, cw, axis=1)

    outs = {}
    for n in WEIGHTS:
        g2 = grads[n].reshape(w2[n].shape)
        d, nm, nv = _adamw("adamw_" + n, w2[n], g2, m2[n], v2[n])
        outs[n] = tuple(unview(a, n) for a in (g2, d, nm, nv))
    return (loss, grad_x[None], *[outs[n][0] for n in WEIGHTS], *[outs[n][1] for n in WEIGHTS],
            *[outs[n][2] for n in WEIGHTS], *[outs[n][3] for n in WEIGHTS])
```

```python
import functools
import math

import jax
import jax.numpy as jnp
from jax import lax
from jax.experimental import pallas as pl
from jax.experimental.pallas import tpu as pltpu

F32 = jnp.float32
BF16 = jnp.bfloat16
EPS = 1e-6
NH_M, DK_M, DV_M = 4, 64, 128
NH_F, DH_F = 8, 64
CONV_W = 4
ADAM_LR, ADAM_B1, ADAM_B2, ADAM_EPS, ADAM_WD, ADAM_STEP = 0.001, 0.9, 0.999, 1e-08, 0.01, 10
VMEM_LIMIT = 56 * 1024 * 1024


def _cparams(sem):
    return pltpu.CompilerParams(dimension_semantics=sem, vmem_limit_bytes=VMEM_LIMIT)


def _sigmoid(x):
    return 1.0 / (1.0 + jnp.exp(-x))


def _dot(a, b, ca, cb):
    return lax.dot_general(a.astype(BF16), b.astype(BF16), (((ca,), (cb,)), ((), ())), preferred_element_type=F32)


def _rowwise(name, fn, tiled, full, outs, accs=(), tm=256):
    rows = tiled[0].shape[0]
    tm = min(tm, rows)
    assert rows % tm == 0
    n_t, n_f, n_o, n_a = len(tiled), len(full), len(outs), len(accs)

    def body(*refs):
        ins = [r[...] for r in refs[: n_t + n_f]]
        res = fn(*ins)
        if not isinstance(res, (tuple, list)):
            res = (res,)
        orefs = refs[n_t + n_f:]
        for r, v in zip(orefs[:n_o], res[:n_o]):
            r[...] = v.astype(r.dtype)
        if n_a:
            @pl.when(pl.program_id(0) == 0)
            def _():
                for r in orefs[n_o:]:
                    r[...] = jnp.zeros_like(r)
            for r, v in zip(orefs[n_o:], res[n_o:]):
                r[...] += v.astype(r.dtype)

    in_specs = [pl.BlockSpec((tm, a.shape[1]), lambda i: (i, 0)) for a in tiled]
    in_specs += [pl.BlockSpec(a.shape, lambda i: (0, 0)) for a in full]
    out_specs = [pl.BlockSpec((tm, c), lambda i: (i, 0)) for c, _ in outs]
    out_specs += [pl.BlockSpec(s, lambda i: (0, 0)) for s, _ in accs]
    out_shape = [jax.ShapeDtypeStruct((rows, c), d) for c, d in outs]
    out_shape += [jax.ShapeDtypeStruct(s, d) for s, d in accs]
    res = pl.pallas_call(
        body, name=name, grid=(rows // tm,), in_specs=in_specs, out_specs=out_specs, out_shape=out_shape,
        compiler_params=_cparams(("arbitrary",) if n_a else ("parallel",)),
    )(*tiled, *full)
    return res


def _colsum(v):
    return jnp.sum(v, axis=0, keepdims=True)


def _rms_fwd_val(x, g):
    r = lax.rsqrt(jnp.mean(x * x, axis=-1, keepdims=True) + EPS)
    return x * r * g


def _rms_bwd_val(dy, x, g):
    r = lax.rsqrt(jnp.mean(x * x, axis=-1, keepdims=True) + EPS)
    xh = x * r
    dxh = dy * g
    dx = r * (dxh - xh * jnp.mean(dxh * xh, axis=-1, keepdims=True))
    return dx, _colsum(dy * xh)


def _mm(name, pairs, out_shape, out_block, out_map, grid, kaxis, ta=False, tb=False, scale=None, res=None,
        out_dtype=F32):
    nk = grid[kaxis]
    npairs = len(pairs)
    ca, cb = (0 if ta else 1), (1 if tb else 0)
    acc_shape = tuple(d for d in out_block if d is not None)

    def body(*refs):
        in_refs = refs[: 2 * npairs]
        res_ref = refs[2 * npairs] if res is not None else None
        o_ref = refs[2 * npairs + (1 if res is not None else 0)]
        acc_ref = refs[-1]
        k = pl.program_id(kaxis)

        @pl.when(k == 0)
        def _():
            acc_ref[...] = jnp.zeros_like(acc_ref)

        part = None
        for p in range(npairs):
            d = _dot(in_refs[2 * p][...], in_refs[2 * p + 1][...], ca, cb)
            part = d if part is None else part + d
        acc_ref[...] += part

        @pl.when(k == nk - 1)
        def _():
            v = acc_ref[...]
            if scale is not None:
                v = v * scale
            if res_ref is not None:
                v = v + res_ref[...].astype(F32)
            o_ref[...] = v.astype(o_ref.dtype)

    in_specs, args = [], []
    for a, ab, am, b, bb, bm in pairs:
        in_specs += [pl.BlockSpec(ab, am), pl.BlockSpec(bb, bm)]
        args += [a, b]
    if res is not None:
        in_specs.append(pl.BlockSpec(out_block, out_map))
        args.append(res)
    sem = tuple("arbitrary" if i == kaxis else "parallel" for i in range(len(grid)))
    return pl.pallas_call(
        body, name=name, grid=grid, in_specs=in_specs, out_specs=pl.BlockSpec(out_block, out_map),
        out_shape=jax.ShapeDtypeStruct(out_shape, out_dtype), scratch_shapes=[pltpu.VMEM(acc_shape, F32)],
        compiler_params=_cparams(sem),
    )(*args)


def _pick(n, pref):
    for t in pref:
        if n % t == 0:
            return t
    return n


def _mm_nn(name, a, b, tm=512, tn=512, tk=512, **kw):
    (m, k), n = a.shape, b.shape[1]
    tm, tn, tk = _pick(m, (tm, 256, 128)), _pick(n, (tn, 256, 128)), _pick(k, (tk, 256, 128))
    return _mm(name, [(a, (tm, tk), lambda i, j, kk: (i, kk), b, (tk, tn), lambda i, j, kk: (kk, j))],
               (m, n), (tm, tn), lambda i, j, kk: (i, j), (m // tm, n // tn, k // tk), 2, **kw)


def _mm_nt(name, a, b, tm=512, tn=512, tk=512, **kw):
    (m, k), n = a.shape, b.shape[0]
    tm, tn, tk = _pick(m, (tm, 256, 128)), _pick(n, (tn, 256, 128)), _pick(k, (tk, 256, 128))
    return _mm(name, [(a, (tm, tk), lambda i, j, kk: (i, kk), b, (tn, tk), lambda i, j, kk: (j, kk))],
               (m, n), (tm, tn), lambda i, j, kk: (i, j), (m // tm, n // tn, k // tk), 2, tb=True, **kw)


def _mm_tn(name, a, b, tm=512, tn=512, tk=512, **kw):
    (k, m), n = a.shape, b.shape[1]
    tm, tn, tk = _pick(m, (tm, 256, 128)), _pick(n, (tn, 256, 128)), _pick(k, (tk, 256, 128))
    return _mm(name, [(a, (tk, tm), lambda i, j, kk: (kk, i), b, (tk, tn), lambda i, j, kk: (kk, j))],
               (m, n), (tm, tn), lambda i, j, kk: (i, j), (m // tm, n // tn, k // tk), 2, ta=True, **kw)


def _norm_mm(name, h, gamma, w, w_transposed, out_dtype):
    t, d = h.shape
    n = w.shape[0] if w_transposed else w.shape[1]
    tm, tn = _pick(t, (512, 256)), _pick(n, (1024, 512, 256, 128))

    def body(h_ref, gam_ref, w_ref, xn_ref, o_ref, xn_scr):
        @pl.when(pl.program_id(1) == 0)
        def _():
            xn = _rms_fwd_val(h_ref[...], gam_ref[...]).astype(BF16)
            xn_scr[...] = xn
            xn_ref[...] = xn

        o_ref[...] = _dot(xn_scr[...], w_ref[...], 1, 1 if w_transposed else 0).astype(o_ref.dtype)

    wspec = pl.BlockSpec((tn, d), lambda i, j: (j, 0)) if w_transposed else pl.BlockSpec((d, tn), lambda i, j: (0, j))
    return pl.pallas_call(
        body, name=name, grid=(t // tm, n // tn),
        in_specs=[pl.BlockSpec((tm, d), lambda i, j: (i, 0)), pl.BlockSpec((1, d), lambda i, j: (0, 0)), wspec],
        out_specs=[pl.BlockSpec((tm, d), lambda i, j: (i, 0)), pl.BlockSpec((tm, tn), lambda i, j: (i, j))],
        out_shape=[jax.ShapeDtypeStruct((t, d), BF16), jax.ShapeDtypeStruct((t, n), out_dtype)],
        scratch_shapes=[pltpu.VMEM((tm, d), BF16)], compiler_params=_cparams(("parallel", "arbitrary")),
    )(h, gamma, w)


def _ffn_fwd(pfx, h, gamma, wg, wu, wd, plan=None):
    t, d = h.shape
    nb, f, _ = wg.shape
    tm = _pick(t, (512, 256))
    nt = t // tm
    host = _Hosted(plan, 5, 4)

    def body(*refs):
        (h_ref, gam_ref, wg_ref, wu_ref, wd_ref), (ho_ref, xn_ref, g_ref, u_ref), (xn_scr, acc_ref), prefs = host.split(refs)
        i, j = pl.program_id(0), pl.program_id(1)
        host.run(0, (i == 0) & (j == 0), prefs)
        host.run(1, (i == nt // 2) & (j == 0), prefs)

        @pl.when(j == 0)
        def _():
            xn = _rms_fwd_val(h_ref[...], gam_ref[...]).astype(BF16)
            xn_scr[...] = xn
            xn_ref[...] = xn
            acc_ref[...] = jnp.zeros_like(acc_ref)

        x = xn_scr[...]
        g = _dot(x, wg_ref[...], 1, 1)
        u = _dot(x, wu_ref[...], 1, 1)
        g_ref[...] = g.astype(BF16)
        u_ref[...] = u.astype(BF16)
        acc_ref[...] += _dot(g * _sigmoid(g) * u, wd_ref[...], 1, 0)

        @pl.when(j == nb - 1)
        def _():
            ho_ref[...] = h_ref[...] + 0.5 * acc_ref[...]

        host.run(2, (i == nt - 1) & (j == nb - 1), prefs)

    row = pl.BlockSpec((tm, d), lambda i, j: (i, 0))
    blk = pl.BlockSpec((None, tm, f), lambda i, j: (j, i, 0))
    wspec = pl.BlockSpec((None, f, d), lambda i, j: (j, 0, 0))
    hc = host.call_args()
    res = pl.pallas_call(
        body, name=pfx + "_fwd", grid=(nt, nb),
        in_specs=[row, pl.BlockSpec((1, d), lambda i, j: (0, 0)), wspec, wspec, wspec] + hc["in_specs"],
        out_specs=[row, row, blk, blk] + hc["out_specs"],
        out_shape=[jax.ShapeDtypeStruct((t, d), F32), jax.ShapeDtypeStruct((t, d), BF16),
                   jax.ShapeDtypeStruct((nb, t, f), BF16), jax.ShapeDtypeStruct((nb, t, f), BF16)] + hc["out_shape"],
        scratch_shapes=[pltpu.VMEM((tm, d), BF16), pltpu.VMEM((tm, d), F32)] + hc["scratch"],
        input_output_aliases=hc["aliases"], compiler_params=_cparams(("arbitrary", "arbitrary")),
    )(h, gamma, wg, wu, wd, *hc["args"])
    return res[:4], res[4:]


def _ffn_bwd(pfx, dh_out, h, gamma, xn, g_all, u_all, wg, wu, wd, plan=None):
    t, d = h.shape
    nb, f, _ = wg.shape
    tm = _pick(t, (512, 256))
    tk = _pick(t, (512, 256))

    nt = t // tm
    host = _Hosted(plan, 8, 5)

    def body(*refs):
        ((dy_ref, h_ref, gam_ref, wg_ref, wu_ref, wd_ref, g_ref, u_ref), (dh_ref, dgam_ref, dg_ref, du_ref, a_ref),
         (acc_ref,), prefs) = host.split(refs)
        i, j = pl.program_id(0), pl.program_id(1)
        host.run(0, (i == 0) & (j == 0), prefs)
        host.run(1, (i == nt // 2) & (j == 0), prefs)

        @pl.when((i == 0) & (j == 0))
        def _():
            dgam_ref[...] = jnp.zeros_like(dgam_ref)

        @pl.when(j == 0)
        def _():
            acc_ref[...] = jnp.zeros_like(acc_ref)

        da = _dot(dy_ref[...], wd_ref[...], 1, 1) * 0.5
        g = g_ref[...].astype(F32)
        u = u_ref[...].astype(F32)
        s = _sigmoid(g)
        sl = g * s
        du = (da * sl).astype(BF16)
        dg = (da * u * (s * (1.0 + g * (1.0 - s)))).astype(BF16)
        du_ref[...] = du
        dg_ref[...] = dg
        a_ref[...] = (sl * u).astype(BF16)
        acc_ref[...] += _dot(dg, wg_ref[...], 1, 0) + _dot(du, wu_ref[...], 1, 0)

        @pl.when(j == nb - 1)
        def _():
            dx, dgam = _rms_bwd_val(acc_ref[...], h_ref[...], gam_ref[...])
            dh_ref[...] = dy_ref[...] + dx
            dgam_ref[...] += dgam

        host.run(2, (i == nt - 1) & (j == nb - 1), prefs)

    row = pl.BlockSpec((tm, d), lambda i, j: (i, 0))
    vec = pl.BlockSpec((1, d), lambda i, j: (0, 0))
    blk = pl.BlockSpec((None, tm, f), lambda i, j: (j, i, 0))
    wspec = pl.BlockSpec((None, f, d), lambda i, j: (j, 0, 0))
    hc = host.call_args()
    res = pl.pallas_call(
        body, name=pfx + "_bwd", grid=(nt, nb),
        in_specs=[row, row, vec, wspec, wspec, wspec, blk, blk] + hc["in_specs"],
        out_specs=[row, vec, blk, blk, blk] + hc["out_specs"],
        out_shape=[jax.ShapeDtypeStruct((t, d), F32), jax.ShapeDtypeStruct((1, d), F32)]
        + [jax.ShapeDtypeStruct((nb, t, f), BF16)] * 3 + hc["out_shape"],
        scratch_shapes=[pltpu.VMEM((tm, d), F32)] + hc["scratch"], input_output_aliases=hc["aliases"],
        compiler_params=_cparams(("arbitrary", "arbitrary")),
    )(dh_out, h, gamma, wg, wu, wd, g_all, u_all, *hc["args"])
    dh, dgamma, dg_all, du_all, a_all = res[:5]

    xmap, bmap, omap = (lambda b, k: (k, 0)), (lambda b, k: (b, k, 0)), (lambda b, k: (b, 0, 0))
    dwg = _mm(pfx + "_dwg", [(dg_all, (None, tk, f), bmap, xn, (tk, d), xmap)], (nb, f, d), (None, f, d), omap,
              (nb, t // tk), 1, ta=True)
    dwu = _mm(pfx + "_dwu", [(du_all, (None, tk, f), bmap, xn, (tk, d), xmap)], (nb, f, d), (None, f, d), omap,
              (nb, t // tk), 1, ta=True)
    dwd = _mm(pfx + "_dwd", [(a_all, (None, tk, f), bmap, dh_out, (tk, d), xmap)], (nb, f, d), (None, f, d), omap,
              (nb, t // tk), 1, ta=True, scale=0.5)
    return (dh, dgamma, dwg, dwu, dwd), res[5:]


HALO = 16


def _silu_grad(y):
    s = _sigmoid(y)
    return s * (1.0 + y * (1.0 - s))


def _with_halo(ref, i, n_tiles, tm, before, after):
    t = ref.shape[0]
    r0 = pl.multiple_of(i * tm, tm)
    parts = [ref[pl.ds(r0, tm), :].astype(F32)]
    if before:
        prev = ref[pl.ds(pl.multiple_of(jnp.maximum(r0 - HALO, 0), HALO), HALO), :].astype(F32)
        parts.insert(0, jnp.where(i > 0, prev, 0.0))
    if after:
        nxt = ref[pl.ds(pl.multiple_of(jnp.minimum(r0 + tm, t - HALO), HALO), HALO), :].astype(F32)
        parts.append(jnp.where(i < n_tiles - 1, nxt, 0.0))
    return jnp.concatenate(parts, axis=0)


def _conv_fwd(zbig, w):
    t, c = zbig.shape[0], w.shape[1]
    tm = _pick(t, (512, 256))
    nt = t // tm

    def body(x_ref, w_ref, o_ref):
        xe = _with_halo(x_ref, pl.program_id(0), nt, tm, True, False)
        wv = w_ref[...]
        y = xe * wv[3:4, :]
        for i in range(CONV_W - 1):
            y = y + pltpu.roll(xe, CONV_W - 1 - i, 0) * wv[i:i + 1, :]
        y = y[HALO:, :]
        o_ref[...] = (y * _sigmoid(y)).astype(o_ref.dtype)

    return pl.pallas_call(
        body, name="conv_fwd", grid=(nt,),
        in_specs=[pl.BlockSpec((t, c), lambda i: (0, 0)), pl.BlockSpec(w.shape, lambda i: (0, 0))],
        out_specs=pl.BlockSpec((tm, c), lambda i: (i, 0)), out_shape=jax.ShapeDtypeStruct((t, c), BF16),
        compiler_params=_cparams(("parallel",)),
    )(zbig, w)


def _conv_bwd(zbig, dact, w):
    t, c = dact.shape
    tm = _pick(t, (512, 256))
    nt = t // tm
    n = tm + HALO

    def body(x_ref, d_ref, w_ref, dx_ref, dw_ref):
        xe = _with_halo(x_ref, pl.program_id(0), nt, tm, True, True)
        de = _with_halo(d_ref, pl.program_id(0), nt, tm, False, True)
        wv = w_ref[...]
        sh = [pltpu.roll(xe, CONV_W - 1 - i, 0)[HALO:, :] if i < CONV_W - 1 else xe[HALO:, :] for i in range(CONV_W)]
        y = sh[0] * wv[0:1, :]
        for i in range(1, CONV_W):
            y = y + sh[i] * wv[i:i + 1, :]
        dy = de * _silu_grad(y)
        dx = dy * wv[3:4, :]
        for i in range(CONV_W - 1):
            dx = dx + pltpu.roll(dy, n - (CONV_W - 1 - i), 0) * wv[i:i + 1, :]
        dx_ref[...] = dx[:tm, :].astype(dx_ref.dtype)
        dyc = dy[:tm, :]
        dwp = jnp.concatenate([_colsum(dyc * sh[i][:tm, :]) for i in range(CONV_W)], axis=0)

        @pl.when(pl.program_id(0) == 0)
        def _():
            dw_ref[...] = jnp.zeros_like(dw_ref)
        dw_ref[...] += dwp

    return pl.pallas_call(
        body, name="conv_bwd", grid=(nt,),
        in_specs=[pl.BlockSpec((t, c), lambda i: (0, 0)), pl.BlockSpec((t, c), lambda i: (0, 0)),
                  pl.BlockSpec(w.shape, lambda i: (0, 0))],
        out_specs=[pl.BlockSpec((tm, c), lambda i: (i, 0)), pl.BlockSpec(w.shape, lambda i: (0, 0))],
        out_shape=[jax.ShapeDtypeStruct((t, c), BF16), jax.ShapeDtypeStruct(w.shape, F32)],
        compiler_params=_cparams(("arbitrary",)),
    )(zbig, dact, w)


LM = 256
HI = lax.Precision.HIGHEST


def _logsig(x):
    return jnp.minimum(x, 0.0) - jnp.log(1.0 + jnp.exp(-jnp.abs(x)))


def _tri(n, lower):
    r = lax.broadcasted_iota(jnp.int32, (n, n), 0)
    c = lax.broadcasted_iota(jnp.int32, (n, n), 1)
    return (r >= c) if lower else (r <= c)


def _f32dot(a, b):
    return lax.dot_general(a, b, (((1,), (0,)), ((), ())), precision=HI, preferred_element_type=F32)


def _mlstm_chunk(h, q_ref, k_ref, v_ref, zs_ref, zsr_ref, bc_ref, br_ref, c_prev, m_prev):
    l = LM
    q = q_ref[:, h * DK_M:(h + 1) * DK_M].astype(F32) * (DK_M ** -0.5)
    k = k_ref[:, h * DK_M:(h + 1) * DK_M]
    v = v_ref[:, h * DV_M:(h + 1) * DV_M]
    lane = lax.broadcasted_iota(jnp.int32, (l, DV_M), 1)
    v1 = jnp.concatenate([v, (lane == 0).astype(v.dtype)], axis=1)
    zs, zsr = zs_ref[...], zsr_ref[...]
    li_c = zs[:, h:h + 1] + bc_ref[:, h:h + 1]
    fp_c = zs[:, NH_M + h:NH_M + h + 1] + bc_ref[:, NH_M + h:NH_M + h + 1]
    li_r = zsr[h:h + 1, :] + br_ref[h:h + 1, :]
    fp_r = zsr[NH_M + h:NH_M + h + 1, :] + br_ref[NH_M + h:NH_M + h + 1, :]
    lf_c, lf_r = _logsig(fp_c), _logsig(fp_r)
    low = _tri(l, True)
    b_c = _f32dot(low.astype(F32), lf_c)
    b_r = _f32dot(lf_r, _tri(l, False).astype(F32))
    g = b_r[:, l - 1:l]
    dmat = jnp.where(low, b_c - b_r + li_r, -jnp.inf)
    inter = b_c + m_prev
    m_t = jnp.maximum(inter, jnp.max(dmat, axis=1, keepdims=True))
    w_inter = jnp.exp(inter - m_t)
    amat = jnp.exp(dmat - m_t)
    s = _dot(q, k, 1, 1)
    p = amat * s
    qc = _dot(q, c_prev, 1, 0)
    qc_w = w_inter * qc
    num1 = qc_w + _dot(p, v1, 1, 0)
    den = num1[:, DV_M:DV_M + 1]
    mx = jnp.maximum(jnp.abs(den), jnp.exp(-m_t))
    hh = num1[:, :DV_M] / mx
    a_c = g - b_c + li_c
    return dict(q=q, k=k, v1=v1, fp_c=fp_c, fp_r=fp_r, b_c=b_c, g=g, m_t=m_t, w_inter=w_inter, amat=amat, s=s, p=p,
                qc_w=qc_w, den=den, mx=mx, hh=hh, a_c=a_c)


def _mlstm_fwd(qk, zbig, zs, zsr, bc, br, gm):
    t = zs.shape[0]
    l = LM
    nc = t // l
    dm = NH_M * DV_M

    def body(q_ref, k_ref, v_ref, o_ref, zs_ref, zsr_ref, bc_ref, br_ref, gm_ref, y_ref, cst_ref, mst_ref, c_scr, m_scr):
        @pl.when(pl.program_id(0) == 0)
        def _():
            c_scr[...] = jnp.zeros_like(c_scr)
            m_scr[...] = jnp.zeros_like(m_scr)

        cst_ref[...] = c_scr[...]
        mst_ref[...] = m_scr[...]
        ys = []
        for h in range(NH_M):
            c_prev = c_scr[h]
            m_prev = m_scr[h:h + 1, 0:1]
            r = _mlstm_chunk(h, q_ref, k_ref, v_ref, zs_ref, zsr_ref, bc_ref, br_ref, c_prev, m_prev)
            hh = r["hh"]
            gh = gm_ref[:, h * DV_M:(h + 1) * DV_M]
            hn = hh * lax.rsqrt(jnp.mean(hh * hh, axis=-1, keepdims=True) + EPS) * gh
            og = o_ref[:, h * DV_M:(h + 1) * DV_M].astype(F32)
            ys.append(hn * _sigmoid(og))
            m_new = jnp.maximum(r["g"] + m_prev, jnp.max(r["a_c"], axis=0, keepdims=True))
            decay = jnp.exp(r["g"] + m_prev - m_new)
            wk = r["k"].astype(F32) * jnp.exp(r["a_c"] - m_new)
            c_scr[h] = decay * c_prev + _dot(wk, r["v1"], 0, 0)
            m_scr[h:h + 1, :] = jnp.broadcast_to(m_new, (1, 128))
        y_ref[...] = jnp.concatenate(ys, axis=1).astype(y_ref.dtype)

    return pl.pallas_call(
        body, name="mlstm_fwd", grid=(nc,),
        in_specs=[pl.BlockSpec((l, NH_M * DK_M), lambda i: (i, 0)), pl.BlockSpec((l, NH_M * DK_M), lambda i: (i, 1)),
                  pl.BlockSpec((l, dm), lambda i: (i, 1)), pl.BlockSpec((l, dm), lambda i: (i, 2)),
                  pl.BlockSpec((l, 128), lambda i: (i, 0)), pl.BlockSpec((8, l), lambda i: (0, i)),
                  pl.BlockSpec((1, 8), lambda i: (0, 0)), pl.BlockSpec((8, 1), lambda i: (0, 0)),
                  pl.BlockSpec((1, dm), lambda i: (0, 0))],
        out_specs=[pl.BlockSpec((l, dm), lambda i: (i, 0)), pl.BlockSpec((None, NH_M, DK_M, 2 * DV_M), lambda i: (i, 0, 0, 0)),
                   pl.BlockSpec((None, 8, 128), lambda i: (i, 0, 0))],
        out_shape=[jax.ShapeDtypeStruct((t, dm), BF16), jax.ShapeDtypeStruct((nc, NH_M, DK_M, 2 * DV_M), F32),
                   jax.ShapeDtypeStruct((nc, 8, 128), F32)],
        scratch_shapes=[pltpu.VMEM((NH_M, DK_M, 2 * DV_M), F32), pltpu.VMEM((8, 128), F32)],
        compiler_params=_cparams(("arbitrary",)),
    )(qk, qk, zbig, zbig, zs, zsr, bc, br, gm)


def _mlstm_bwd(qk, zbig, zs, zsr, bc, br, gm, cst, mst, dycat):
    t = zs.shape[0]
    l = LM
    nc = t // l
    dm = NH_M * DV_M

    def body(q_ref, k_ref, v_ref, o_ref, zs_ref, zsr_ref, bc_ref, br_ref, gm_ref, cst_ref, mst_ref, cnx_ref, mnx_ref,
             dy_ref, dqk_ref, dv_ref, do_ref, dzs_ref, dzr_ref, dgm_ref, dc_scr):
        @pl.when(pl.program_id(0) == 0)
        def _():
            dc_scr[...] = jnp.zeros_like(dc_scr)
            dgm_ref[...] = jnp.zeros_like(dgm_ref)

        lane = lax.broadcasted_iota(jnp.int32, (l, 128), 1)
        upper = _tri(l, False).astype(F32)
        lower = _tri(l, True).astype(F32)
        dzr_rows = [None] * 8
        dvs, dos, dgs, dqs, dks = [], [], [], [], []
        dzs = jnp.zeros((l, 128), F32)
        for h in range(NH_M):
            c_prev = cst_ref[h]
            m_prev = mst_ref[h:h + 1, 0:1]
            r = _mlstm_chunk(h, q_ref, k_ref, v_ref, zs_ref, zsr_ref, bc_ref, br_ref, c_prev, m_prev)
            hh, mx, den, m_t, v1, amat = r["hh"], r["mx"], r["den"], r["m_t"], r["v1"], r["amat"]
            gh = gm_ref[:, h * DV_M:(h + 1) * DV_M]
            rs = lax.rsqrt(jnp.mean(hh * hh, axis=-1, keepdims=True) + EPS)
            xh = hh * rs
            sg = _sigmoid(o_ref[:, h * DV_M:(h + 1) * DV_M].astype(F32))
            dyh = dy_ref[:, h * DV_M:(h + 1) * DV_M]
            dos.append(dyh * xh * gh * sg * (1.0 - sg))
            dhn = dyh * sg
            dgs.append(_colsum(dhn * xh))
            dxh = dhn * gh
            dh = rs * (dxh - xh * jnp.mean(dxh * xh, axis=-1, keepdims=True))
            g1 = dh / mx
            hd = jnp.sum(hh * dh, axis=-1, keepdims=True)
            dden = jnp.where(jnp.abs(den) > jnp.exp(-m_t), -hd / mx * jnp.sign(den), 0.0)
            g256 = jnp.concatenate([g1, jnp.where(lane == 0, dden, 0.0)], axis=1)
            dc_h = dc_scr[h]
            ea = jnp.exp(r["a_c"])
            dp = _dot(g256, v1, 1, 1)
            ds = dp * amat
            dqs.append((r["w_inter"] * _dot(g256, c_prev, 1, 1) + _dot(ds, r["k"], 1, 0)) * (DK_M ** -0.5))
            dks.append(_dot(ds, r["q"], 0, 0) + ea * _dot(v1, dc_h, 1, 1))
            dv_st = ea * _dot(r["k"], dc_h, 1, 0)
            dv1 = _dot(r["p"], g256, 0, 0) + dv_st
            dvs.append(dv1[:, :DV_M])
            wmat = dp * r["p"]
            c_in = _colsum(wmat)
            c_st = jnp.sum(v1.astype(F32) * dv_st, axis=-1, keepdims=True)
            r_t = jnp.sum(wmat, axis=1, keepdims=True) + jnp.sum(g256 * r["qc_w"], axis=-1, keepdims=True)
            db = r_t - c_st
            carry = jnp.exp(mnx_ref[h:h + 1, 0:1]) * jnp.sum(
                jnp.sum(dc_h * cnx_ref[h], axis=1, keepdims=True), axis=0, keepdims=True)
            dlf_c = _f32dot(upper, db) + carry
            dlf_r = -_f32dot(c_in, lower)
            dfp = dlf_c * _sigmoid(-r["fp_c"])
            dzs = dzs + jnp.where(lane == h, c_st, 0.0) + jnp.where(lane == NH_M + h, dfp, 0.0)
            dzr_rows[h] = c_in
            dzr_rows[NH_M + h] = dlf_r * _sigmoid(-r["fp_r"])
            wq = r["q"] * jnp.exp(r["b_c"] - m_t)
            dc_scr[h] = jnp.exp(r["g"]) * dc_h + _dot(wq, g256, 0, 0)
        dqk_ref[...] = jnp.concatenate(dqs + dks, axis=1)
        dv_ref[...] = jnp.concatenate(dvs, axis=1).astype(dv_ref.dtype)
        do_ref[...] = jnp.concatenate(dos, axis=1).astype(do_ref.dtype)
        dzs_ref[...] = dzs
        dzr_ref[...] = jnp.concatenate(dzr_rows, axis=0)
        dgm_ref[...] += jnp.concatenate(dgs, axis=1)

    rev = lambda i: nc - 1 - i
    nxt = lambda i: jnp.minimum(nc - i, nc - 1)
    return pl.pallas_call(
        body, name="mlstm_bwd", grid=(nc,),
        in_specs=[pl.BlockSpec((l, NH_M * DK_M), lambda i: (rev(i), 0)), pl.BlockSpec((l, NH_M * DK_M), lambda i: (rev(i), 1)),
                  pl.BlockSpec((l, dm), lambda i: (rev(i), 1)), pl.BlockSpec((l, dm), lambda i: (rev(i), 2)),
                  pl.BlockSpec((l, 128), lambda i: (rev(i), 0)), pl.BlockSpec((8, l), lambda i: (0, rev(i))),
                  pl.BlockSpec((1, 8), lambda i: (0, 0)), pl.BlockSpec((8, 1), lambda i: (0, 0)),
                  pl.BlockSpec((1, dm), lambda i: (0, 0)),
                  pl.BlockSpec((None, NH_M, DK_M, 2 * DV_M), lambda i: (rev(i), 0, 0, 0)),
                  pl.BlockSpec((None, 8, 128), lambda i: (rev(i), 0, 0)),
                  pl.BlockSpec((None, NH_M, DK_M, 2 * DV_M), lambda i: (nxt(i), 0, 0, 0)),
                  pl.BlockSpec((None, 8, 128), lambda i: (nxt(i), 0, 0)),
                  pl.BlockSpec((l, dm), lambda i: (rev(i), 0))],
        out_specs=[pl.BlockSpec((l, dm), lambda i: (rev(i), 0)),
                   pl.BlockSpec((l, dm), lambda i: (rev(i), 0)), pl.BlockSpec((l, dm), lambda i: (rev(i), 0)),
                   pl.BlockSpec((l, 128), lambda i: (rev(i), 0)), pl.BlockSpec((8, l), lambda i: (0, rev(i))),
                   pl.BlockSpec((1, dm), lambda i: (0, 0))],
        out_shape=[jax.ShapeDtypeStruct((t, dm), F32),
                   jax.ShapeDtypeStruct((t, dm), BF16), jax.ShapeDtypeStruct((t, dm), BF16),
                   jax.ShapeDtypeStruct((t, 128), F32), jax.ShapeDtypeStruct((8, t), F32),
                   jax.ShapeDtypeStruct((1, dm), F32)],
        scratch_shapes=[pltpu.VMEM((NH_M, DK_M, 2 * DV_M), F32)],
        compiler_params=_cparams(("arbitrary",)),
    )(qk, qk, zbig, zbig, zs, zsr, bc, br, gm, cst, mst, cst, mst, dycat)


def _fox_cumsum(zsr, bf_r):
    t = zsr.shape[1]
    cw = _pick(t, (512, 256))

    def body(z_ref, b_ref, c_ref):
        up = _tri(cw, False).astype(F32)
        carry = jnp.zeros((NH_F, 1), F32)
        for j in range(t // cw):
            cs = _f32dot(_logsig(z_ref[:, j * cw:(j + 1) * cw] + b_ref[...]), up) + carry
            c_ref[:, j * cw:(j + 1) * cw] = cs
            carry = cs[:, cw - 1:cw]

    return pl.pallas_call(
        body, name="fox_cumsum", grid=(1,),
        in_specs=[pl.BlockSpec((NH_F, t), lambda i: (1, 0)), pl.BlockSpec((NH_F, 1), lambda i: (0, 0))],
        out_specs=pl.BlockSpec((NH_F, t), lambda i: (0, 0)), out_shape=jax.ShapeDtypeStruct((NH_F, t), F32),
        compiler_params=_cparams(("arbitrary",)),
    )(zsr, bf_r)


def _fox_gate_bwd(zsr, bf_r, dc):
    t = zsr.shape[1]
    cw = _pick(t, (512, 256))

    def body(z_ref, b_ref, dc_ref, o_ref):
        low = _tri(cw, True).astype(F32)
        carry = jnp.zeros((NH_F, 1), F32)
        for j in reversed(range(t // cw)):
            sl = slice(j * cw, (j + 1) * cw)
            dlf = _f32dot(dc_ref[:, sl], low) + carry
            o_ref[:, sl] = dlf * _sigmoid(-(z_ref[:, sl] + b_ref[...]))
            carry = dlf[:, 0:1]

    return pl.pallas_call(
        body, name="fox_gate_bwd", grid=(1,),
        in_specs=[pl.BlockSpec((NH_F, t), lambda i: (1, 0)), pl.BlockSpec((NH_F, 1), lambda i: (0, 0)),
                  pl.BlockSpec((NH_F, t), lambda i: (0, 0))],
        out_specs=pl.BlockSpec((NH_F, t), lambda i: (0, 0)), out_shape=jax.ShapeDtypeStruct((NH_F, t), F32),
        compiler_params=_cparams(("arbitrary",)),
    )(zsr, bf_r, dc)


def _causal_mask(n):
    return _tri(n, True)


def _fox_fwd(q, k, v, c_col, c_row, gf):
    nh, t, dh = q.shape
    tq = _pick(t, (512, 256))
    scale = dh ** -0.5

    def body(q_ref, k_ref, v_ref, cc_ref, cr_ref, g_ref, o_ref, lse_ref, y_ref):
        i = pl.program_id(1)
        qv = q_ref[...]
        cq = cc_ref[...]

        def blk(j, carry, masked):
            m, l, acc = carry
            k0 = pl.multiple_of(j * tq, tq)
            kb = k_ref[pl.ds(k0, tq), :]
            vb = v_ref[pl.ds(k0, tq), :]
            s = _dot(qv, kb, 1, 1) * scale + cq - cr_ref[:, pl.ds(k0, tq)]
            if masked:
                s = jnp.where(_causal_mask(tq), s, -jnp.inf)
            m_new = jnp.maximum(m, jnp.max(s, axis=1, keepdims=True))
            alpha = jnp.exp(m - m_new)
            p = jnp.exp(s - m_new)
            return m_new, alpha * l + jnp.sum(p, axis=1, keepdims=True), alpha * acc + _dot(p, vb, 1, 0)

        init = (jnp.full((tq, 1), -jnp.inf, F32), jnp.zeros((tq, 1), F32), jnp.zeros((tq, dh), F32))
        carry = lax.fori_loop(0, i, lambda j, c: blk(j, c, False), init)
        m, l, acc = blk(i, carry, True)
        o = acc / l
        o_ref[...] = o
        lse_ref[...] = m + jnp.log(l)
        y_ref[...] = (o * lax.rsqrt(jnp.mean(o * o, axis=-1, keepdims=True) + EPS) * g_ref[...]).astype(y_ref.dtype)

    full = lambda w: pl.BlockSpec((None, t, w), lambda h, i: (h, 0, 0))
    tile = lambda w: pl.BlockSpec((None, tq, w), lambda h, i: (h, i, 0))
    return pl.pallas_call(
        body, name="fox_fwd", grid=(nh, t // tq),
        in_specs=[tile(dh), full(dh), full(dh), tile(1), pl.BlockSpec((None, 1, t), lambda h, i: (h, 0, 0)),
                  pl.BlockSpec((None, 1, dh), lambda h, i: (h, 0, 0))],
        out_specs=[tile(dh), tile(1), tile(dh)],
        out_shape=[jax.ShapeDtypeStruct((nh, t, dh), F32), jax.ShapeDtypeStruct((nh, t, 1), F32),
                   jax.ShapeDtypeStruct((nh, t, dh), BF16)],
        compiler_params=_cparams(("parallel", "parallel")),
    )(q, k, v, c_col, c_row, gf)


def _fox_norm_bwd(dy, o, gf):
    nh, t, dh = o.shape
    tm = _pick(t, (512, 256))

    def body(dy_ref, o_ref, g_ref, do_ref, dl_ref, dg_ref):
        ov = o_ref[...]
        dx, dg = _rms_bwd_val(dy_ref[...], ov, g_ref[...])
        do_ref[...] = dx
        dl_ref[...] = jnp.sum(dx * ov, axis=-1, keepdims=True)

        @pl.when(pl.program_id(1) == 0)
        def _():
            dg_ref[...] = jnp.zeros_like(dg_ref)
        dg_ref[...] += dg

    tile = lambda w: pl.BlockSpec((None, tm, w), lambda h, i: (h, i, 0))
    gspec = pl.BlockSpec((None, 1, dh), lambda h, i: (h, 0, 0))
    return pl.pallas_call(
        body, name="fox_norm_bwd", grid=(nh, t // tm), in_specs=[tile(dh), tile(dh), gspec],
        out_specs=[tile(dh), tile(1), gspec],
        out_shape=[jax.ShapeDtypeStruct((nh, t, dh), F32), jax.ShapeDtypeStruct((nh, t, 1), F32),
                   jax.ShapeDtypeStruct((nh, 1, dh), F32)],
        compiler_params=_cparams(("parallel", "arbitrary")),
    )(dy, o, gf)


def _fox_bwd(q, k, v, c_col, c_row, do, lse, delta):
    nh, t, dh = q.shape
    tq = _pick(t, (512, 256))
    nq = t // tq
    scale = dh ** -0.5

    def body(q_ref, k_ref, v_ref, cc_ref, cr_ref, do_ref, lse_ref, dl_ref, dq_ref, dk_ref, dv_ref, dc_ref, dcq_ref):
        j = pl.program_id(1)

        @pl.when(j == 0)
        def _():
            dq_ref[...] = jnp.zeros_like(dq_ref)
            dcq_ref[...] = jnp.zeros_like(dcq_ref)

        kb, vb, crb = k_ref[...], v_ref[...], cr_ref[...]

        def blk(i, carry, masked):
            dk, dv, dc = carry
            rows = pl.ds(pl.multiple_of(i * tq, tq), tq)
            qb = q_ref[rows, :]
            dob = do_ref[rows, :].astype(BF16)
            s = _dot(qb, kb, 1, 1) * scale + cc_ref[rows, :] - crb
            if masked:
                s = jnp.where(_causal_mask(tq), s, -jnp.inf)
            p = jnp.exp(s - lse_ref[rows, :])
            dv = dv + _dot(p, dob, 0, 0)
            ds = p * (_dot(dob, vb, 1, 1) - dl_ref[rows, :])
            dc = dc + _colsum(ds)
            dk = dk + _dot(ds, qb, 0, 0) * scale
            dq_ref[rows, :] += _dot(ds, kb, 1, 0) * scale
            dcq_ref[rows, :] += jnp.sum(ds, axis=1, keepdims=True)
            return dk, dv, dc

        init = (jnp.zeros((tq, dh), F32), jnp.zeros((tq, dh), F32), jnp.zeros((1, tq), F32))
        carry = blk(j, init, True)
        dk, dv, dc = lax.fori_loop(j + 1, nq, lambda i, c: blk(i, c, False), carry)
        dk_ref[...] = dk
        dv_ref[...] = dv
        dc_ref[...] = -dc

    full = lambda w: pl.BlockSpec((None, t, w), lambda h, j: (h, 0, 0))
    tile = lambda w: pl.BlockSpec((None, tq, w), lambda h, j: (h, j, 0))
    crow = pl.BlockSpec((None, 1, tq), lambda h, j: (h, 0, j))
    return pl.pallas_call(
        body, name="fox_bwd", grid=(nh, nq),
        in_specs=[full(dh), tile(dh), tile(dh), full(1), crow, full(dh), full(1), full(1)],
        out_specs=[full(dh), tile(dh), tile(dh), crow, full(1)],
        out_shape=[jax.ShapeDtypeStruct((nh, t, dh), F32)] * 3 + [jax.ShapeDtypeStruct((nh, 1, t), F32),
                                                                jax.ShapeDtypeStruct((nh, t, 1), F32)],
        compiler_params=_cparams(("parallel", "arbitrary")),
    )(q, k, v, c_col, c_row, do, lse, delta)


AUG = 64


def _split3(c):
    hi = c.astype(BF16).astype(F32)
    r1 = c - hi
    mid = r1.astype(BF16).astype(F32)
    return hi, mid, r1 - mid


def _fox_prep(zbig, ct):
    t = zbig.shape[0]
    tm = _pick(t, (512, 256))

    def body(q_ref, k_ref, v_ref, c_ref, qo_ref, ko_ref, vo_ref):
        lane = lax.broadcasted_iota(jnp.int32, (tm, AUG), 1)
        qv, kv, vv, cv = q_ref[...], k_ref[...], v_ref[...], c_ref[...]
        one = (lane == 0).astype(BF16)
        for h in range(NH_F):
            hi, mid, lo = _split3(cv[:, h:h + 1])
            aq = jnp.where(lane == 0, hi, jnp.where(lane == 1, mid, jnp.where(lane == 2, lo, jnp.where(lane < 6, 1.0, 0.0))))
            ak = jnp.where(lane < 3, 1.0, jnp.where(lane == 3, -hi, jnp.where(lane == 4, -mid, jnp.where(lane == 5, -lo, 0.0))))
            sl = slice(h * DH_F, (h + 1) * DH_F)
            qo_ref[h] = jnp.concatenate([qv[:, sl] * (DH_F ** -0.5), aq.astype(BF16)], axis=1).astype(BF16)
            ko_ref[h] = jnp.concatenate([kv[:, sl], ak.astype(BF16)], axis=1)
            vo_ref[h] = jnp.concatenate([vv[:, sl], one], axis=1)

    ospec = pl.BlockSpec((NH_F, tm, 128), lambda i: (0, i, 0))
    return pl.pallas_call(
        body, name="fox_prep", grid=(t // tm,),
        in_specs=[pl.BlockSpec((tm, 512), lambda i: (i, 3)), pl.BlockSpec((tm, 512), lambda i: (i, 4)),
                  pl.BlockSpec((tm, 512), lambda i: (i, 5)), pl.BlockSpec((tm, NH_F), lambda i: (i, 0))],
        out_specs=[ospec] * 3, out_shape=[jax.ShapeDtypeStruct((NH_F, t, 128), BF16)] * 3,
        compiler_params=_cparams(("parallel",)),
    )(zbig, zbig, zbig, ct)


def _fox_fwd2(qa, ka, va, gf, plan=None):
    nh, t, _ = qa.shape
    tq = _pick(t, (512, 256))
    nq = t // tq
    group = 2
    host = _Hosted(plan, 4, 3)

    def body(*refs):
        (q_ref, k_ref, v_ref, g_ref), (y_ref, o_ref, lse_ref), _, prefs = host.split(refs)
        i = pl.program_id(0)
        host.run(0, i == 0, prefs)
        host.run(1, i == nq // 2, prefs)
        lane = lax.broadcasted_iota(jnp.int32, (tq, 128), 1)
        ys, os_ = [], []
        lse_all = jnp.zeros((tq, 128), F32)
        for h0 in range(0, nh, group):
            heads = range(h0, h0 + group)
            qvs = [q_ref[h] for h in heads]

            def blk(j, carry, masked, heads=heads, qvs=qvs):
                k0 = pl.multiple_of(j * tq, tq)
                out = []
                for (m, acc), h, qv in zip(carry, heads, qvs):
                    s = lax.dot_general(qv, k_ref[h, pl.ds(k0, tq), :], (((1,), (1,)), ((), ())), preferred_element_type=F32)
                    if masked:
                        s = jnp.where(_causal_mask(tq), s, -jnp.inf)
                    m_new = jnp.maximum(m, jnp.max(s, axis=1, keepdims=True))
                    p = jnp.exp(s - m_new).astype(BF16)
                    pv = lax.dot_general(p, v_ref[h, pl.ds(k0, tq), :], (((1,), (0,)), ((), ())), preferred_element_type=F32)
                    out.append((m_new, jnp.exp(m - m_new) * acc + pv))
                return tuple(out)

            init = tuple((jnp.full((tq, 1), -jnp.inf, F32), jnp.zeros((tq, 128), F32)) for _ in heads)
            carry = lax.fori_loop(0, i, lambda j, c: blk(j, c, False), init)
            for (m, acc), h in zip(blk(i, carry, True), heads):
                l = acc[:, DH_F:DH_F + 1]
                o = acc[:, :DH_F] / l
                os_.append(o)
                gh = g_ref[:, h * DH_F:(h + 1) * DH_F]
                ys.append(o * lax.rsqrt(jnp.mean(o * o, axis=-1, keepdims=True) + EPS) * gh)
                lse_all = lse_all + jnp.where(lane == h, m + jnp.log(l), 0.0)
        y_ref[...] = jnp.concatenate(ys, axis=1).astype(y_ref.dtype)
        o_ref[...] = jnp.concatenate(os_, axis=1)
        lse_ref[...] = lse_all
        host.run(2, i == nq - 1, prefs)

    full = pl.BlockSpec((nh, t, 128), lambda i: (0, 0, 0))
    hc = host.call_args()
    res = pl.pallas_call(
        body, name="fox_fwd", grid=(nq,),
        in_specs=[pl.BlockSpec((nh, tq, 128), lambda i: (0, i, 0)), full, full, pl.BlockSpec((1, nh * DH_F), lambda i: (0, 0))]
        + hc["in_specs"],
        out_specs=[pl.BlockSpec((tq, nh * DH_F), lambda i: (i, 0)), pl.BlockSpec((tq, nh * DH_F), lambda i: (i, 0)),
                   pl.BlockSpec((tq, 128), lambda i: (i, 0))] + hc["out_specs"],
        out_shape=[jax.ShapeDtypeStruct((t, nh * DH_F), BF16), jax.ShapeDtypeStruct((t, nh * DH_F), F32),
                   jax.ShapeDtypeStruct((t, 128), F32)] + hc["out_shape"],
        scratch_shapes=hc["scratch"], input_output_aliases=hc["aliases"], compiler_params=_cparams(("arbitrary",)),
    )(qa, ka, va, gf, *hc["args"])
    return res[:3], res[3:]


def _fox_bwd_prep(dycat, o, gf):
    t = o.shape[0]
    tm = _pick(t, (512, 256))

    def body(dy_ref, o_ref, g_ref, do_ref, dl_ref, dg_ref):
        lane = lax.broadcasted_iota(jnp.int32, (tm, 128), 1)
        dyv, ov, gv = dy_ref[...], o_ref[...], g_ref[...]
        dgs = []
        dl = jnp.zeros((tm, 128), F32)
        pad = jnp.zeros((tm, AUG), BF16)
        for h in range(NH_F):
            sl = slice(h * DH_F, (h + 1) * DH_F)
            dx, dg = _rms_bwd_val(dyv[:, sl], ov[:, sl], gv[:, sl])
            dgs.append(dg)
            do_ref[h] = jnp.concatenate([dx.astype(BF16), pad], axis=1)
            dl = dl + jnp.where(lane == h, jnp.sum(dx * ov[:, sl], axis=-1, keepdims=True), 0.0)
        dl_ref[...] = dl

        @pl.when(pl.program_id(0) == 0)
        def _():
            dg_ref[...] = jnp.zeros_like(dg_ref)
        dg_ref[...] += jnp.concatenate(dgs, axis=1)

    return pl.pallas_call(
        body, name="fox_bwd_prep", grid=(t // tm,),
        in_specs=[pl.BlockSpec((tm, 512), lambda i: (i, 1)), pl.BlockSpec((tm, 512), lambda i: (i, 0)),
                  pl.BlockSpec((1, 512), lambda i: (0, 0))],
        out_specs=[pl.BlockSpec((NH_F, tm, 128), lambda i: (0, i, 0)), pl.BlockSpec((tm, 128), lambda i: (i, 0)),
                   pl.BlockSpec((1, 512), lambda i: (0, 0))],
        out_shape=[jax.ShapeDtypeStruct((NH_F, t, 128), BF16), jax.ShapeDtypeStruct((t, 128), F32),
                   jax.ShapeDtypeStruct((1, 512), F32)],
        compiler_params=_cparams(("arbitrary",)),
    )(dycat, o, gf)


def _fox_bwd2(qa, ka, va, doa, lse, delta):
    nh, t, _ = qa.shape
    tq = _pick(t, (512, 256))
    nq = t // tq

    group = 2

    def body(q_ref, k_ref, v_ref, do_ref, lse_ref, dl_ref, dq_ref, dk_ref, dv_ref):
        hp, j = pl.program_id(0), pl.program_id(1)

        @pl.when(j == 0)
        def _():
            dq_ref[...] = jnp.zeros_like(dq_ref)

        lane = lax.broadcasted_iota(jnp.int32, (tq, 128), 1)

        def blk(i, carry, masked):
            rows = pl.ds(pl.multiple_of(i * tq, tq), tq)
            lse_t, dl_t = lse_ref[rows, :], dl_ref[rows, :]
            out = []
            for g, (dk, dv) in enumerate(carry):
                h = hp * group + g
                kb, vb = k_ref[g], v_ref[g]
                qb, dob = q_ref[g, rows, :], do_ref[g, rows, :]
                lse_h = jnp.sum(jnp.where(lane == h, lse_t, 0.0), axis=1, keepdims=True)
                dl_h = jnp.sum(jnp.where(lane == h, dl_t, 0.0), axis=1, keepdims=True)
                s = lax.dot_general(qb, kb, (((1,), (1,)), ((), ())), preferred_element_type=F32)
                if masked:
                    s = jnp.where(_causal_mask(tq), s, -jnp.inf)
                p = jnp.exp(s - lse_h)
                dp = lax.dot_general(dob, vb, (((1,), (1,)), ((), ())), preferred_element_type=F32)
                ds = (p * (dp - dl_h)).astype(BF16)
                dv = dv + lax.dot_general(p.astype(BF16), dob, (((0,), (0,)), ((), ())), preferred_element_type=F32)
                dk = dk + lax.dot_general(ds, qb, (((0,), (0,)), ((), ())), preferred_element_type=F32)
                dq_ref[g, rows, :] += lax.dot_general(ds, kb, (((1,), (0,)), ((), ())), preferred_element_type=F32)
                out.append((dk, dv))
            return tuple(out)

        init = tuple((jnp.zeros((tq, 128), F32), jnp.zeros((tq, 128), F32)) for _ in range(group))
        carry = blk(j, init, True)
        carry = lax.fori_loop(j + 1, nq, lambda i, c: blk(i, c, False), carry)
        for g, (dk, dv) in enumerate(carry):
            dk_ref[g] = dk
            dv_ref[g] = dv

    full = pl.BlockSpec((group, t, 128), lambda h, j: (h, 0, 0))
    tile = pl.BlockSpec((group, tq, 128), lambda h, j: (h, j, 0))
    cols = pl.BlockSpec((t, 128), lambda h, j: (0, 0))
    return pl.pallas_call(
        body, name="fox_bwd", grid=(nh // group, nq), in_specs=[full, tile, tile, full, cols, cols],
        out_specs=[full, tile, tile], out_shape=[jax.ShapeDtypeStruct((nh, t, 128), F32)] * 3,
        compiler_params=_cparams(("parallel", "arbitrary")),
    )(qa, ka, va, doa, lse, delta)


def _fox_bwd_post(dqa, dka, dva):
    nh, t, _ = dqa.shape
    tm = _pick(t, (512, 256))

    def body(dq_ref, dk_ref, dv_ref, oq_ref, ok_ref, ov_ref, dc_ref):
        lane = lax.broadcasted_iota(jnp.int32, (tm, 128), 1)
        dc = jnp.zeros((tm, 128), F32)
        qs, ks, vs = [], [], []
        for h in range(nh):
            dq, dk = dq_ref[h], dk_ref[h]
            qs.append(dq[:, :DH_F] * (DH_F ** -0.5))
            ks.append(dk[:, :DH_F])
            vs.append(dv_ref[h][:, :DH_F])
            dc = dc + jnp.where(lane == h, dq[:, DH_F:DH_F + 1] - dk[:, DH_F + 3:DH_F + 4], 0.0)
        oq_ref[...] = jnp.concatenate(qs, axis=1).astype(BF16)
        ok_ref[...] = jnp.concatenate(ks, axis=1).astype(BF16)
        ov_ref[...] = jnp.concatenate(vs, axis=1).astype(BF16)
        dc_ref[...] = dc

    ispec = pl.BlockSpec((nh, tm, 128), lambda i: (0, i, 0))
    ospec = pl.BlockSpec((tm, nh * DH_F), lambda i: (i, 0))
    return pl.pallas_call(
        body, name="fox_bwd_post", grid=(t // tm,), in_specs=[ispec] * 3,
        out_specs=[ospec] * 3 + [pl.BlockSpec((tm, 128), lambda i: (i, 0))],
        out_shape=[jax.ShapeDtypeStruct((t, nh * DH_F), BF16)] * 3 + [jax.ShapeDtypeStruct((t, 128), F32)],
        compiler_params=_cparams(("parallel",)),
    )(dqa, dka, dva)


W_BIG = 6 * 512
IN_OFF = (0, 512, 1024, 1544, 2056, 2568)
IN_GATES = (1536, 3080)


def _heads(a, nh):
    t = a.shape[0]
    return a.reshape(t, nh, -1).transpose(1, 0, 2)


def _unheads(a):
    nh, t, dh = a.shape
    return a.transpose(1, 0, 2).reshape(t, nh * dh)


FFN1 = ("ffn1_w_gate", "ffn1_w_up", "ffn1_w_down")
REST = ("w_in", "w_out", "ffn2_w_gate", "ffn2_w_up", "ffn2_w_down", "w_ple_gate", "w_ple_proj")
SPLIT = {n: 1 if n == "w_in" else 0 for n in FFN1 + REST}


def _rs_partials(names, gw, c_idx):
    wire = [_cast_other_half("rs_cast_" + n, gw[n], c_idx, SPLIT[n]) for n in names]
    swapped = _swap("rs_swap_" + names[0], wire)
    return [_add_my_half("rs_add_" + n, gw[n], r, c_idx, SPLIT[n]) for n, r in zip(names, swapped)]


def _local_step(x, p, tgt, sp, wg1, wu1, wd1, rest_slots, c_idx, place):
    t, d = x.shape
    slot = dict(zip(REST + ("conv_qk",), rest_slots))
    (h1, xn1, g1, u1), (w_in, conv_w) = _ffn_fwd(
        "ffn1", x, sp["ffn1_norm"], wg1, wu1, wd1, plan=_gather_plan([slot["w_in"], slot["conv_qk"]], [SPLIT["w_in"], None]))
    w_in, conv_w = w_in.reshape(-1, d), _from_chip_blocks(conv_w)
    w_big = jnp.concatenate([w_in[o:o + 512] for o in IN_OFF], axis=0)
    w_small = jnp.concatenate([w_in[IN_GATES[0]:IN_GATES[0] + 8], w_in[IN_GATES[1]:IN_GATES[1] + 8],
                               jnp.zeros((112, d), w_in.dtype)], axis=0)
    u, zbig = _norm_mm("in_big", h1, sp["mix_norm"], w_big, True, BF16)
    zs = _mm_nt("in_small", u, w_small, tm=1024, tk=1024)
    zsr = zs.T
    qk_act = _conv_fwd(zbig, conv_w)
    bm_c, bf_c = sp["b_mlstm_gates"], sp["b_fox_f"]
    y_m, cst, mst = _mlstm_fwd(qk_act, zbig, zs, zsr, bm_c, bm_c.T, sp["mlstm_out_norm"])
    c = _fox_cumsum(zsr, bf_c.T)
    qa, ka, va = _fox_prep(zbig, c.T)
    (y_ft, o_f, lse), late = _fox_fwd2(qa, ka, va, sp["fox_out_norm"],
                                       plan=_gather_plan([slot[n] for n in REST[1:]], [SPLIT[n] for n in REST[1:]]))
    full = dict(zip(REST[1:], late))
    w_out, w_pg = (full[n].reshape(-1, d) for n in ("w_out", "w_ple_gate"))
    wg2, wu2, wd2 = full["ffn2_w_gate"], full["ffn2_w_up"], full["ffn2_w_down"]
    w_pp = _from_chip_blocks(full["w_ple_proj"])
    tm = _pick(t, (1024, 512, 256))
    h2 = _mm("out_proj", [(y_m, (tm, 512), lambda i, j, k: (i, 0), w_out, (512, d), lambda i, j, k: (0, 0)),
                          (y_ft, (tm, 512), lambda i, j, k: (i, 0), w_out, (512, d), lambda i, j, k: (1, 0))],
             (t, d), (tm, d), lambda i, j, k: (i, 0), (t // tm, 1, 1), 2, res=h1)
    (h3, xn2, g2, u2), _ = _ffn_fwd("ffn2", h2, sp["ffn2_norm"], wg2, wu2, wd2)
    hn3, gate_pre = _norm_mm("ple_gate", h3, sp["ple_gate_norm"], w_pg, False, F32)
    pp = _mm_nn("ple_proj", p, w_pp, tm=1024)

    def head_fn(h3_t, gp_t, pp_t, tgt_t, g_pp, g_fin):
        gate = _sigmoid(gp_t)
        ppn = _rms_fwd_val(pp_t, g_pp)
        h4 = h3_t + gate * ppn
        err = _rms_fwd_val(h4, g_fin) - tgt_t
        loss = 0.5 * jnp.sum(jnp.mean(err * err, axis=-1, keepdims=True), axis=0, keepdims=True)
        dh4, dg_fin = _rms_bwd_val(err * (1.0 / d), h4, g_fin)
        dpp, dg_pp = _rms_bwd_val(dh4 * gate, pp_t, g_pp)
        dgp = dh4 * ppn * gate * (1.0 - gate)
        return dh4, dgp, dpp, jnp.broadcast_to(loss, (1, 128)), dg_fin, dg_pp

    dh4, dgp, dpp, loss_part, dg_fin, dg_pp = _rowwise(
        "loss_head", head_fn, [h3, gate_pre, pp, tgt], [sp["ple_proj_norm"], sp["final_norm"]],
        [(d, F32), (d, BF16), (d, BF16)], [((1, 128), F32), ((1, d), F32), ((1, d), F32)])
    gw, gs = {}, {"final_norm": dg_fin, "ple_proj_norm": dg_pp}
    gw["w_ple_gate"] = _mm_tn("d_w_pg", hn3, dgp, tm=1024, tn=1024)
    gw["w_ple_proj"] = _mm_tn("d_w_pp", p, dpp, tn=1024)
    dhn3 = _mm_nt("d_hn3", dgp, w_pg, tm=1024, tn=1024, tk=1024)

    def res_norm_bwd(dn_t, h_t, dres_t, g):
        dx, dg = _rms_bwd_val(dn_t, h_t, g)
        return dres_t + dx, dg

    dh3, gs["ple_gate_norm"] = _rowwise("ple_norm_bwd", res_norm_bwd, [dhn3, h3, dh4], [sp["ple_gate_norm"]],
                                        [(d, F32)], [((1, d), F32)])
    (dh2, gs["ffn2_norm"], gw["ffn2_w_gate"], gw["ffn2_w_up"], gw["ffn2_w_down"]), _ = _ffn_bwd(
        "ffn2", dh3, h2, sp["ffn2_norm"], xn2, g2, u2, wg2, wu2, wd2)
    dycat = _mm_nt("d_ycat", dh2, w_out, tm=1024, tn=1024, tk=1024)
    gw["w_out"] = jnp.concatenate([_mm_tn("d_w_out_m", y_m, dh2, tn=1024), _mm_tn("d_w_out_f", y_ft, dh2, tn=1024)], axis=0)
    doa, delta, gs["fox_out_norm"] = _fox_bwd_prep(dycat, o_f, sp["fox_out_norm"])
    dq_f, dk_f, dv_f, dct = _fox_bwd_post(*_fox_bwd2(qa, ka, va, doa, lse, delta))
    dfp = _fox_gate_bwd(zsr, bf_c.T, dct[:, :NH_F].T)
    dact, dv_m, do_m, dzs_m, dzr_m, gs["mlstm_out_norm"] = _mlstm_bwd(
        qk_act, zbig, zs, zsr, bm_c, bm_c.T, sp["mlstm_out_norm"], cst, mst, dycat)
    dqk, gw["conv_qk"] = _conv_bwd(zbig, dact, conv_w)
    dz_big = jnp.concatenate([dqk, dv_m, do_m, dq_f, dk_f, dv_f], axis=1)
    dzs = dzs_m + jnp.pad(jnp.concatenate([dzr_m, dfp], axis=0).T, ((0, 0), (0, 112)))
    dw_big = _mm_tn("d_w_big", dz_big, u, tn=1024)
    dw_small = _mm_tn("d_w_small", dzs, u, tn=1024)
    gw["w_in"] = jnp.concatenate([dw_big[0:1536], dw_small[0:8], dw_big[1536:3072], dw_small[8:16]], axis=0)
    du_a = _mm_nn("d_u_big", dz_big, w_big, tm=1024, tn=1024, tk=1024)
    du_b = _mm_nn("d_u_small", dzs, w_small, tm=1024, tn=1024)

    def mix_norm_bwd(da_t, db_t, h_t, dres_t, dzs_t, g):
        dx, dg = _rms_bwd_val(da_t + db_t, h_t, g)
        return dres_t + dx, dg, _colsum(dzs_t)

    dh1, gs["mix_norm"], dbias = _rowwise("mix_norm_bwd", mix_norm_bwd, [du_a, du_b, h1, dh2, dzs], [sp["mix_norm"]],
                                          [(d, F32)], [((1, d), F32), ((1, 128), F32)])
    gs["b_mlstm_gates"], gs["b_fox_f"] = dbias[:, 0:8], dbias[:, 8:16]
    conv_grad = gw.pop("conv_qk")
    gw["w_ple_proj"] = _chip_blocks(gw["w_ple_proj"])
    for n in ("w_in", "w_out", "w_ple_gate"):
        gw[n] = gw[n].reshape(4, -1, gw[n].shape[-1])
    part_rest = _rs_partials(REST, gw, c_idx)
    (grad_x, gs["ffn1_norm"], gw["ffn1_w_gate"], gw["ffn1_w_up"], gw["ffn1_w_down"]), landed_rest = _ffn_bwd(
        "ffn1", dh1, x, sp["ffn1_norm"], xn1, g1, u1, wg1, wu1, wd1, plan=_scatter_plan([pb for _, pb in part_rest]))
    part_ffn1 = _rs_partials(FFN1, gw, c_idx)
    landed_ffn1 = _scatter4("rs_scatter_ffn1", [pb for _, pb in part_ffn1])
    names, parts, landed = REST + FFN1, part_rest + part_ffn1, list(landed_rest) + list(landed_ffn1)
    mine = [_sum4("rs_sum_" + n, a, pf, place, SPLIT[n]) for n, a, (pf, _) in zip(names, landed, parts)]
    grads = dict(zip(names, _join_halves("rs_join", mine, [SPLIT[n] for n in names])))
    return loss_part, grad_x, grads, gs, conv_grad


ANY = pl.BlockSpec(memory_space=pl.ANY)
MESH = pl.DeviceIdType.MESH


def _place():
    x, y, c = lax.axis_index("x"), lax.axis_index("y"), lax.axis_index("c")
    chips = [(1 - x, y), (x, 1 - y), (1 - x, 1 - y)]
    return x, y, c, 2 * x + y, (x, y, 1 - c), chips


def _rcopy(src, dst, ssem, rsem, dev):
    return pltpu.make_async_remote_copy(src_ref=src, dst_ref=dst, send_sem=ssem, recv_sem=rsem, device_id=dev,
                                        device_id_type=MESH)


def _half(ref, lead, axis, idx, half):
    return ref.at[(slice(None),) * (lead + axis) + (pl.ds(idx * half, half),)]


def _to_slot(name, a, me_idx, dtype):
    r, cdim = a.shape
    tr = _pick(r, (256, 176, 128, 64))

    def body(me_ref, a_ref, o_ref):
        o_ref[...] = a_ref[...].astype(o_ref.dtype)

    return pl.pallas_call(
        body, name=name,
        grid_spec=pltpu.PrefetchScalarGridSpec(
            num_scalar_prefetch=1, grid=(r // tr,), in_specs=[pl.BlockSpec((tr, cdim), lambda i, me_ref: (i, 0))],
            out_specs=pl.BlockSpec((None, tr, cdim), lambda i, me_ref: (me_ref[0], i, 0))),
        out_shape=jax.ShapeDtypeStruct((4, r, cdim), dtype), compiler_params=_cparams(("parallel",)),
    )(me_idx, a)


def _gather4(name, bufs, split):
    return _run_plan(name, _gather_plan(bufs, split))


def _gather_plan(bufs, split):
    n = len(bufs)
    shapes = [b.shape[1:] for b in bufs]

    def ctx(outs):
        x, y, c, me, sib, chips = _place()

        def part(ref, a, which):
            if split[a] is None:
                return ref
            return _half(ref, 0, split[a], which, shapes[a][split[a]] // 2)

        return c, me, sib, chips, part

    def ici(outs, sems, a, j, chip, c, me, part):
        mine = part(outs[a].at[me], a, c)
        return _rcopy(mine, mine, sems[0].at[3 * a + j], sems[1].at[3 * a + j], (*chip, c))

    def fwd(outs, sems, a, j, chip, c, sib, part, which):
        blk = part(outs[a].at[2 * chip[0] + chip[1]], a, which)
        return _rcopy(blk, blk, sems[2].at[3 * a + j], sems[3].at[3 * a + j], sib)

    def start(ins, outs, sems):
        c, me, sib, chips, part = ctx(outs)
        for a in range(n):
            for j, chip in enumerate(chips):
                ici(outs, sems, a, j, chip, c, me, part).start()

    def mid(ins, outs, sems):
        c, me, sib, chips, part = ctx(outs)
        for j, chip in enumerate(chips):
            for a in range(n):
                blk = part(outs[a].at[2 * chip[0] + chip[1]], a, c)
                _rcopy(blk, blk, sems[0].at[3 * a + j], sems[1].at[3 * a + j], sib).wait_recv()
                if split[a] is not None:
                    fwd(outs, sems, a, j, chip, c, sib, part, c).start()

    def end(ins, outs, sems):
        c, me, sib, chips, part = ctx(outs)
        for j, chip in enumerate(chips):
            for a in range(n):
                if split[a] is not None:
                    fwd(outs, sems, a, j, chip, c, sib, part, 1 - c).wait_recv()
        for a in range(n):
            for j, chip in enumerate(chips):
                ici(outs, sems, a, j, chip, c, me, part).wait_send()
                if split[a] is not None:
                    fwd(outs, sems, a, j, chip, c, sib, part, c).wait_send()

    return dict(ins=list(bufs), outs=[jax.ShapeDtypeStruct(b.shape, b.dtype) for b in bufs], alias=True,
                sems=[pltpu.SemaphoreType.DMA((3 * n,))] * 4, phases=(start, mid, end))


def _run_plan(name, plan):
    ni, no = len(plan["ins"]), len(plan["outs"])

    def body(*refs):
        ins, outs, sems = refs[:ni], refs[ni:ni + no], refs[ni + no:]
        for phase in plan["phases"]:
            phase(ins, outs, sems)

    return pl.pallas_call(
        body, name=name, in_specs=[ANY] * ni, out_specs=[ANY] * no, out_shape=plan["outs"],
        input_output_aliases={a: a for a in range(ni)} if plan["alias"] else {}, scratch_shapes=plan["sems"],
    )(*plan["ins"])


class _Hosted:
    def __init__(self, plan, n_in, n_out):
        self.plan, self.n_in, self.n_out = plan, n_in, n_out
        self.ni, self.no, self.ns = (len(plan["ins"]) if plan else 0, len(plan["outs"]) if plan else 0,
                                     len(plan["sems"]) if plan else 0)

    def split(self, refs):
        a, b = self.n_in, self.n_in + self.ni
        c, d = b + self.n_out, b + self.n_out + self.no
        e = len(refs) - self.ns
        return refs[:a], refs[b:c], refs[d:e], (refs[a:b], refs[c:d], refs[e:])

    def run(self, k, cond, prefs):
        if self.plan is not None:
            @pl.when(cond)
            def _():
                self.plan["phases"][k](*prefs)

    def call_args(self):
        p = self.plan
        if p is None:
            return dict(in_specs=[], out_specs=[], out_shape=[], scratch=[], aliases={}, args=[])
        al = {self.n_in + a: self.n_out + a for a in range(self.ni)} if p["alias"] else {}
        return dict(in_specs=[ANY] * self.ni, out_specs=[ANY] * self.no, out_shape=list(p["outs"]), scratch=list(p["sems"]),
                    aliases=al, args=list(p["ins"]))


def _swap(name, arrs):
    n = len(arrs)

    def body(*refs):
        ins, outs = refs[:n], refs[n:2 * n]
        ssem, rsem = refs[2 * n:]
        x, y, c, me, sib, chips = _place()
        cps = [_rcopy(ins[a], outs[a], ssem.at[a], rsem.at[a], sib) for a in range(n)]
        for cp in cps:
            cp.start()
        for cp in cps:
            cp.wait()

    return pl.pallas_call(
        body, name=name, in_specs=[ANY] * n, out_specs=[ANY] * n,
        out_shape=[jax.ShapeDtypeStruct(a.shape, a.dtype) for a in arrs],
        scratch_shapes=[pltpu.SemaphoreType.DMA((n,))] * 2,
    )(*arrs)


def _scatter4(name, arrs):
    return _run_plan(name, _scatter_plan(arrs))


def _scatter_plan(arrs):
    n = len(arrs)

    def send(ins, outs, sems, a, j, chip, c, me):
        return _rcopy(ins[a].at[2 * chip[0] + chip[1]], outs[a].at[me], sems[0].at[3 * a + j], sems[1].at[3 * a + j], (*chip, c))

    def start(ins, outs, sems):
        x, y, c, me, sib, chips = _place()
        for a in range(n):
            for j, chip in enumerate(chips):
                send(ins, outs, sems, a, j, chip, c, me).start()

    def mid(ins, outs, sems):
        pass

    def end(ins, outs, sems):
        x, y, c, me, sib, chips = _place()
        for a in range(n):
            for j, chip in enumerate(chips):
                blk = outs[a].at[2 * chip[0] + chip[1]]
                _rcopy(blk, blk, sems[0].at[3 * a + j], sems[1].at[3 * a + j], sib).wait_recv()
        for a in range(n):
            for j, chip in enumerate(chips):
                send(ins, outs, sems, a, j, chip, c, me).wait_send()

    return dict(ins=list(arrs), outs=[jax.ShapeDtypeStruct(a.shape, a.dtype) for a in arrs], alias=False,
                sems=[pltpu.SemaphoreType.DMA((3 * n,))] * 2, phases=(start, mid, end))


def _join_halves(name, arrs, split):
    n = len(arrs)

    def body(*refs):
        outs = refs[n:2 * n]
        ssem, rsem = refs[2 * n:]
        x, y, c, me, sib, chips = _place()
        cps = []
        for a in range(n):
            mine = _half(outs[a], 0, split[a], c, arrs[a].shape[split[a]] // 2)
            cp = _rcopy(mine, mine, ssem.at[a], rsem.at[a], sib)
            cp.start()
            cps.append(cp)
        for a in range(n):
            blk = _half(outs[a], 0, split[a], 1 - c, arrs[a].shape[split[a]] // 2)
            _rcopy(blk, blk, ssem.at[a], rsem.at[a], sib).wait_recv()
        for cp in cps:
            cp.wait_send()

    return pl.pallas_call(
        body, name=name, in_specs=[ANY] * n, out_specs=[ANY] * n,
        out_shape=[jax.ShapeDtypeStruct(a.shape, a.dtype) for a in arrs],
        input_output_aliases={a: a for a in range(n)}, scratch_shapes=[pltpu.SemaphoreType.DMA((n,))] * 2,
    )(*arrs)


def _allreduce_small(s):
    r, cdim = s.shape

    def body(s_ref, o_ref, buf, ssem, rsem):
        x, y, c, me, sib, chips = _place()
        me8 = 4 * x + 2 * y + c
        buf[me8] = s_ref[...]
        flips = [(fx, fy, fc) for fx in (0, 1) for fy in (0, 1) for fc in (0, 1)][1:]
        cps = []
        for k, (fx, fy, fc) in enumerate(flips):
            peer = (x ^ fx if fx else x, y ^ fy if fy else y, c ^ fc if fc else c)
            cp = _rcopy(s_ref, buf.at[me8], ssem.at[k], rsem.at[k], peer)
            cp.start()
            cps.append(cp)
        for k, (fx, fy, fc) in enumerate(flips):
            src = 4 * (x ^ fx if fx else x) + 2 * (y ^ fy if fy else y) + (c ^ fc if fc else c)
            _rcopy(s_ref, buf.at[src], ssem.at[k], rsem.at[k], sib).wait_recv()
        for cp in cps:
            cp.wait_send()
        acc = buf[0]
        for k in range(1, 8):
            acc = acc + buf[k]
        o_ref[...] = acc

    vm = pl.BlockSpec(memory_space=pltpu.VMEM)
    return pl.pallas_call(
        body, name="allreduce_small", in_specs=[vm], out_specs=vm, out_shape=jax.ShapeDtypeStruct((r, cdim), F32),
        scratch_shapes=[pltpu.VMEM((8, r, cdim), F32), pltpu.SemaphoreType.DMA((7,)), pltpu.SemaphoreType.DMA((7,))],
    )(s)


def _add_my_half(name, g, recv, c_idx, axis):
    nb, hr, hc = recv.shape
    tr = _pick(hr, (256, 176, 128, 64))
    if axis == 0:
        g4 = g.reshape(nb, 2, hr, hc)
        gspec = pl.BlockSpec((None, None, tr, hc), lambda b, i, c_ref: (b, c_ref[0], i, 0))
    else:
        g4 = g
        gspec = pl.BlockSpec((None, tr, hc), lambda b, i, c_ref: (b, i, c_ref[0]))

    def body(c_ref, g_ref, r_ref, o_ref, ob_ref):
        s = g_ref[...] + r_ref[...].astype(F32)
        o_ref[...] = s
        ob_ref[...] = s.astype(BF16)

    ospec = pl.BlockSpec((None, tr, hc), lambda b, i, c_ref: (b, i, 0))
    return pl.pallas_call(
        body, name=name,
        grid_spec=pltpu.PrefetchScalarGridSpec(
            num_scalar_prefetch=1, grid=(nb, hr // tr), in_specs=[gspec, ospec], out_specs=[ospec, ospec]),
        out_shape=[jax.ShapeDtypeStruct((nb, hr, hc), F32), jax.ShapeDtypeStruct((nb, hr, hc), BF16)],
        compiler_params=_cparams(("parallel", "parallel")),
    )(c_idx, g4, recv)


def _sum4(name, landed, own, place, axis):
    nb, h, cdim = landed.shape
    tr = _pick(h, (256, 176, 128, 64))
    nt = h // tr

    def body(p_ref, a1_ref, a2_ref, a3_ref, own_ref, o_ref):
        o_ref[...] = ((own_ref[...] + a1_ref[...].astype(F32)) + a2_ref[...].astype(F32)) + a3_ref[...].astype(F32)

    def nxt(k):
        return pl.BlockSpec((None, tr, cdim), lambda i, p_ref: ((p_ref[0] + k) % nb, i, 0))

    if axis == 0:
        ospec = pl.BlockSpec((tr, cdim), lambda i, p_ref: (p_ref[1] * nt + i, 0))
        oshape = (2 * h, cdim)
    else:
        ospec = pl.BlockSpec((tr, cdim), lambda i, p_ref: (i, p_ref[1]))
        oshape = (h, 2 * cdim)
    return pl.pallas_call(
        body, name=name,
        grid_spec=pltpu.PrefetchScalarGridSpec(
            num_scalar_prefetch=1, grid=(nt,), in_specs=[nxt(1), nxt(2), nxt(3), nxt(0)], out_specs=ospec),
        out_shape=jax.ShapeDtypeStruct(oshape, F32), compiler_params=_cparams(("parallel",)),
    )(place, landed, landed, landed, own)


def _cast_other_half(name, g, c_idx, axis):
    nb, r, cdim = g.shape
    hr, hc = (r // 2, cdim) if axis == 0 else (r, cdim // 2)
    tr = _pick(hr, (256, 176, 128, 64))
    if axis == 0:
        g4 = g.reshape(nb, 2, hr, hc)
        gspec = pl.BlockSpec((None, None, tr, hc), lambda b, i, c_ref: (b, 1 - c_ref[0], i, 0))
    else:
        g4 = g
        gspec = pl.BlockSpec((None, tr, hc), lambda b, i, c_ref: (b, i, 1 - c_ref[0]))

    def body(c_ref, g_ref, o_ref):
        o_ref[...] = g_ref[...].astype(BF16)

    return pl.pallas_call(
        body, name=name,
        grid_spec=pltpu.PrefetchScalarGridSpec(
            num_scalar_prefetch=1, grid=(nb, hr // tr), in_specs=[gspec],
            out_specs=pl.BlockSpec((None, tr, hc), lambda b, i, c_ref: (b, i, 0))),
        out_shape=jax.ShapeDtypeStruct((nb, hr, hc), BF16), compiler_params=_cparams(("parallel", "parallel")),
    )(c_idx, g4)


def _adamw(name, w, g, m, v):
    c1 = 1.0 - ADAM_B1 ** ADAM_STEP
    c2 = 1.0 - ADAM_B2 ** ADAM_STEP

    def fn(w_t, g_t, m_t, v_t):
        m_n = ADAM_B1 * m_t + (1.0 - ADAM_B1) * g_t
        v_n = ADAM_B2 * v_t + (1.0 - ADAM_B2) * (g_t * g_t)
        delta = -ADAM_LR * ((m_n / c1) / (jnp.sqrt(v_n / c2) + ADAM_EPS) + ADAM_WD * w_t)
        return delta, m_n, v_n

    cdim = w.shape[1]
    return _rowwise(name, fn, [w, g, m, v], [], [(cdim, F32)] * 3, tm=_pick(w.shape[0], (256, 176, 128, 64, 8)))


BIG = ("ffn1_w_gate", "ffn1_w_up", "ffn1_w_down", "w_in", "w_out", "ffn2_w_gate", "ffn2_w_up", "ffn2_w_down",
       "w_ple_gate", "w_ple_proj")
SMALL = ("ffn1_norm", "mix_norm", "b_mlstm_gates", "b_fox_f", "mlstm_out_norm", "fox_out_norm", "ffn2_norm",
         "ple_gate_norm", "ple_proj_norm", "final_norm")
WEIGHTS = ("ffn1_norm", "ffn1_w_gate", "ffn1_w_up", "ffn1_w_down", "mix_norm", "w_in", "conv_qk", "b_mlstm_gates",
           "b_fox_f", "mlstm_out_norm", "fox_out_norm", "w_out", "ffn2_norm", "ffn2_w_gate", "ffn2_w_up", "ffn2_w_down",
           "ple_gate_norm", "w_ple_gate", "w_ple_proj", "ple_proj_norm", "final_norm")
TRANSPOSED = ("ffn1_w_gate", "ffn1_w_up", "w_in", "ffn2_w_gate", "ffn2_w_up")
PACK_W = 1024


def _chip_blocks(a):
    r, c4 = a.shape
    return a.reshape(r, 4, c4 // 4).transpose(1, 0, 2)


def _from_chip_blocks(a):
    nb, r, c = a.shape
    return a.transpose(1, 0, 2).reshape(r, nb * c)


def kernel(x, p, ffn1_norm, ffn1_w_gate, ffn1_w_up, ffn1_w_down, mix_norm, w_in, conv_qk, b_mlstm_gates, b_fox_f, mlstm_out_norm, fox_out_norm, w_out, ffn2_norm, ffn2_w_gate, ffn2_w_up, ffn2_w_down, ple_gate_norm, w_ple_gate, w_ple_proj, ple_proj_norm, final_norm, loss_target, m_ffn1_norm, m_ffn1_w_gate, m_ffn1_w_up, m_ffn1_w_down, m_mix_norm, m_w_in, m_conv_qk, m_b_mlstm_gates, m_b_fox_f, m_mlstm_out_norm, m_fox_out_norm, m_w_out, m_ffn2_norm, m_ffn2_w_gate, m_ffn2_w_up, m_ffn2_w_down, m_ple_gate_norm, m_w_ple_gate, m_w_ple_proj, m_ple_proj_norm, m_final_norm, v_ffn1_norm, v_ffn1_w_gate, v_ffn1_w_up, v_ffn1_w_down, v_mix_norm, v_w_in, v_conv_qk, v_b_mlstm_gates, v_b_fox_f, v_mlstm_out_norm, v_fox_out_norm, v_w_out, v_ffn2_norm, v_ffn2_w_gate, v_ffn2_w_up, v_ffn2_w_down, v_ple_gate_norm, v_w_ple_gate, v_w_ple_proj, v_ple_proj_norm, v_final_norm):
    w = dict(ffn1_norm=ffn1_norm, ffn1_w_gate=ffn1_w_gate, ffn1_w_up=ffn1_w_up, ffn1_w_down=ffn1_w_down, mix_norm=mix_norm,
             w_in=w_in, conv_qk=conv_qk, b_mlstm_gates=b_mlstm_gates, b_fox_f=b_fox_f, mlstm_out_norm=mlstm_out_norm,
             fox_out_norm=fox_out_norm, w_out=w_out, ffn2_norm=ffn2_norm, ffn2_w_gate=ffn2_w_gate, ffn2_w_up=ffn2_w_up,
             ffn2_w_down=ffn2_w_down, ple_gate_norm=ple_gate_norm, w_ple_gate=w_ple_gate, w_ple_proj=w_ple_proj,
             ple_proj_norm=ple_proj_norm, final_norm=final_norm)
    m = dict(ffn1_norm=m_ffn1_norm, ffn1_w_gate=m_ffn1_w_gate, ffn1_w_up=m_ffn1_w_up, ffn1_w_down=m_ffn1_w_down,
             mix_norm=m_mix_norm, w_in=m_w_in, conv_qk=m_conv_qk, b_mlstm_gates=m_b_mlstm_gates, b_fox_f=m_b_fox_f,
             mlstm_out_norm=m_mlstm_out_norm, fox_out_norm=m_fox_out_norm, w_out=m_w_out, ffn2_norm=m_ffn2_norm,
             ffn2_w_gate=m_ffn2_w_gate, ffn2_w_up=m_ffn2_w_up, ffn2_w_down=m_ffn2_w_down, ple_gate_norm=m_ple_gate_norm,
             w_ple_gate=m_w_ple_gate, w_ple_proj=m_w_ple_proj, ple_proj_norm=m_ple_proj_norm, final_norm=m_final_norm)
    v = dict(ffn1_norm=v_ffn1_norm, ffn1_w_gate=v_ffn1_w_gate, ffn1_w_up=v_ffn1_w_up, ffn1_w_down=v_ffn1_w_down,
             mix_norm=v_mix_norm, w_in=v_w_in, conv_qk=v_conv_qk, b_mlstm_gates=v_b_mlstm_gates, b_fox_f=v_b_fox_f,
             mlstm_out_norm=v_mlstm_out_norm, fox_out_norm=v_fox_out_norm, w_out=v_w_out, ffn2_norm=v_ffn2_norm,
             ffn2_w_gate=v_ffn2_w_gate, ffn2_w_up=v_ffn2_w_up, ffn2_w_down=v_ffn2_w_down, ple_gate_norm=v_ple_gate_norm,
             w_ple_gate=v_w_ple_gate, w_ple_proj=v_w_ple_proj, ple_proj_norm=v_ple_proj_norm, final_norm=v_final_norm)
    shapes = {n: w[n].shape for n in WEIGHTS}

    def view(a, n):
        return a[0].T if n in TRANSPOSED else a.reshape(-1, a.shape[-1])

    def unview(a, n):
        return (a.T if n in TRANSPOSED else a).reshape(shapes[n])

    w2, m2, v2 = ({n: view(a, n) for n, a in d.items()} for d in (w, m, v))

    c_idx = lax.axis_index("c").astype(jnp.int32).reshape(1)
    me_idx = (2 * lax.axis_index("x") + lax.axis_index("y")).astype(jnp.int32).reshape(1)
    place = jnp.concatenate([me_idx, c_idx])
    slot = {n: _to_slot("slot_" + n, w2[n], me_idx, BF16) for n in BIG}
    slot["conv_qk"] = _to_slot("slot_conv_qk", w2["conv_qk"], me_idx, F32)
    wg1, wu1, wd1 = _gather4("gather_ffn1", [slot[n] for n in FFN1], [SPLIT[n] for n in FFN1])
    sp = {n: w2[n] for n in SMALL}
    loss_part, grad_x, grads, gs, conv_grad = _local_step(
        x[0], p[0, 0], loss_target[0], sp, wg1, wu1, wd1, [slot[n] for n in REST + ("conv_qk",)], c_idx, place)
    loss = lax.psum(loss_part[0, 0], ("x", "y", "c"))

    small = [gs[n].reshape(1, -1) for n in SMALL] + [conv_grad]
    rows = [jnp.pad(a, ((0, 0), (0, PACK_W - a.shape[1]))) for a in small]
    packed = jnp.concatenate(rows, axis=0)
    packed = jnp.pad(packed, ((0, -packed.shape[0] % 8), (0, 0)))
    red = _allreduce_small(packed)
    for i, n in enumerate(SMALL):
        grads[n] = red[i:i + 1, :gs[n].size]
    dconv = red[len(SMALL):len(SMALL) + CONV_W, :conv_grad.shape[1]]
    cw = conv_qk.shape[-1]
    grads["conv_qk"] = lax.dynamic_slice_in_dim(dconv, (2 * lax.axis_index("x") + lax.axis_index("y")) * cw, cw, axis=1)

    outs = {}
    for n in WEIGHTS:
        g2 = grads[n].reshape(w2[n].shape)
        d, nm, nv = _adamw("adamw_" + n, w2[n], g2, m2[n], v2[n])
        outs[n] = tuple(unview(a, n) for a in (g2, d, nm, nv))
    return (loss, grad_x[None], *[outs[n][0] for n in WEIGHTS], *[outs[n][1] for n in WEIGHTS],
            *[outs[n][2] for n in WEIGHTS], *[outs[n][3] for n in WEIGHTS])
```

```python
import functools
import math

import jax
import jax.numpy as jnp
from jax import lax
from jax.experimental import pallas as pl
from jax.experimental.pallas import tpu as pltpu

F32 = jnp.float32
BF16 = jnp.bfloat16
EPS = 1e-6
NH_M, DK_M, DV_M = 4, 64, 128
NH_F, DH_F = 8, 64
CONV_W = 4
ADAM_LR, ADAM_B1, ADAM_B2, ADAM_EPS, ADAM_WD, ADAM_STEP = 0.001, 0.9, 0.999, 1e-08, 0.01, 10
VMEM_LIMIT = 56 * 1024 * 1024


def _cparams(sem):
    return pltpu.CompilerParams(dimension_semantics=sem, vmem_limit_bytes=VMEM_LIMIT)


def _sigmoid(x):
    return 1.0 / (1.0 + jnp.exp(-x))


def _dot(a, b, ca, cb):
    return lax.dot_general(a.astype(BF16), b.astype(BF16), (((ca,), (cb,)), ((), ())), preferred_element_type=F32)


def _rowwise(name, fn, tiled, full, outs, accs=(), tm=256):
    rows = tiled[0].shape[0]
    tm = min(tm, rows)
    assert rows % tm == 0
    n_t, n_f, n_o, n_a = len(tiled), len(full), len(outs), len(accs)

    def body(*refs):
        ins = [r[...] for r in refs[: n_t + n_f]]
        res = fn(*ins)
        if not isinstance(res, (tuple, list)):
            res = (res,)
        orefs = refs[n_t + n_f:]
        for r, v in zip(orefs[:n_o], res[:n_o]):
            r[...] = v.astype(r.dtype)
        if n_a:
            @pl.when(pl.program_id(0) == 0)
            def _():
                for r in orefs[n_o:]:
                    r[...] = jnp.zeros_like(r)
            for r, v in zip(orefs[n_o:], res[n_o:]):
                r[...] += v.astype(r.dtype)

    in_specs = [pl.BlockSpec((tm, a.shape[1]), lambda i: (i, 0)) for a in tiled]
    in_specs += [pl.BlockSpec(a.shape, lambda i: (0, 0)) for a in full]
    out_specs = [pl.BlockSpec((tm, c), lambda i: (i, 0)) for c, _ in outs]
    out_specs += [pl.BlockSpec(s, lambda i: (0, 0)) for s, _ in accs]
    out_shape = [jax.ShapeDtypeStruct((rows, c), d) for c, d in outs]
    out_shape += [jax.ShapeDtypeStruct(s, d) for s, d in accs]
    res = pl.pallas_call(
        body, name=name, grid=(rows // tm,), in_specs=in_specs, out_specs=out_specs, out_shape=out_shape,
        compiler_params=_cparams(("arbitrary",) if n_a else ("parallel",)),
    )(*tiled, *full)
    return res


def _colsum(v):
    return jnp.sum(v, axis=0, keepdims=True)


def _rms_fwd_val(x, g):
    r = lax.rsqrt(jnp.mean(x * x, axis=-1, keepdims=True) + EPS)
    return x * r * g


def _rms_bwd_val(dy, x, g):
    r = lax.rsqrt(jnp.mean(x * x, axis=-1, keepdims=True) + EPS)
    xh = x * r
    dxh = dy * g
    dx = r * (dxh - xh * jnp.mean(dxh * xh, axis=-1, keepdims=True))
    return dx, _colsum(dy * xh)


def _mm(name, pairs, out_shape, out_block, out_map, grid, kaxis, ta=False, tb=False, scale=None, res=None,
        out_dtype=F32, plan=None):
    nk = grid[kaxis]
    npairs = len(pairs)
    ca, cb = (0 if ta else 1), (1 if tb else 0)
    acc_shape = tuple(d for d in out_block if d is not None)
    n_in = 2 * npairs + (1 if res is not None else 0)
    host = _Hosted(plan, n_in, 1)

    def body(*refs):
        ins, (o_ref,), (acc_ref,), prefs = host.split(refs)
        in_refs = ins[: 2 * npairs]
        res_ref = ins[2 * npairs] if res is not None else None
        k = pl.program_id(kaxis)
        ids = [pl.program_id(a) for a in range(len(grid))]
        first, last = ids[0] == 0, ids[0] == grid[0] - 1
        for a in range(1, len(grid)):
            first, last = first & (ids[a] == 0), last & (ids[a] == grid[a] - 1)
        host.run(0, first, prefs)
        host.run(1, first, prefs)

        @pl.when(k == 0)
        def _():
            acc_ref[...] = jnp.zeros_like(acc_ref)

        part = None
        for p in range(npairs):
            d = _dot(in_refs[2 * p][...], in_refs[2 * p + 1][...], ca, cb)
            part = d if part is None else part + d
        acc_ref[...] += part

        @pl.when(k == nk - 1)
        def _():
            v = acc_ref[...]
            if scale is not None:
                v = v * scale
            if res_ref is not None:
                v = v + res_ref[...].astype(F32)
            o_ref[...] = v.astype(o_ref.dtype)

        host.run(2, last, prefs)

    in_specs, args = [], []
    for a, ab, am, b, bb, bm in pairs:
        in_specs += [pl.BlockSpec(ab, am), pl.BlockSpec(bb, bm)]
        args += [a, b]
    if res is not None:
        in_specs.append(pl.BlockSpec(out_block, out_map))
        args.append(res)
    sem = tuple("arbitrary" if (i == kaxis or plan is not None) else "parallel" for i in range(len(grid)))
    hc = host.call_args()
    out = pl.pallas_call(
        body, name=name, grid=grid, in_specs=in_specs + hc["in_specs"],
        out_specs=[pl.BlockSpec(out_block, out_map)] + hc["out_specs"],
        out_shape=[jax.ShapeDtypeStruct(out_shape, out_dtype)] + hc["out_shape"],
        scratch_shapes=[pltpu.VMEM(acc_shape, F32)] + hc["scratch"], input_output_aliases=hc["aliases"],
        compiler_params=_cparams(sem),
    )(*args, *hc["args"])
    return out[0] if plan is None else (out[0], out[1:])


def _pick(n, pref):
    for t in pref:
        if n % t == 0:
            return t
    return n


def _mm_nn(name, a, b, tm=512, tn=512, tk=512, **kw):
    (m, k), n = a.shape, b.shape[1]
    tm, tn, tk = _pick(m, (tm, 256, 128)), _pick(n, (tn, 256, 128)), _pick(k, (tk, 256, 128))
    return _mm(name, [(a, (tm, tk), lambda i, j, kk: (i, kk), b, (tk, tn), lambda i, j, kk: (kk, j))],
               (m, n), (tm, tn), lambda i, j, kk: (i, j), (m // tm, n // tn, k // tk), 2, **kw)


def _mm_nt(name, a, b, tm=512, tn=512, tk=512, **kw):
    (m, k), n = a.shape, b.shape[0]
    tm, tn, tk = _pick(m, (tm, 256, 128)), _pick(n, (tn, 256, 128)), _pick(k, (tk, 256, 128))
    return _mm(name, [(a, (tm, tk), lambda i, j, kk: (i, kk), b, (tn, tk), lambda i, j, kk: (j, kk))],
               (m, n), (tm, tn), lambda i, j, kk: (i, j), (m // tm, n // tn, k // tk), 2, tb=True, **kw)


def _mm_tn(name, a, b, tm=512, tn=512, tk=512, **kw):
    (k, m), n = a.shape, b.shape[1]
    tm, tn, tk = _pick(m, (tm, 256, 128)), _pick(n, (tn, 256, 128)), _pick(k, (tk, 256, 128))
    return _mm(name, [(a, (tk, tm), lambda i, j, kk: (kk, i), b, (tk, tn), lambda i, j, kk: (kk, j))],
               (m, n), (tm, tn), lambda i, j, kk: (i, j), (m // tm, n // tn, k // tk), 2, ta=True, **kw)


def _norm_mm(name, h, gamma, w, w_transposed, out_dtype):
    t, d = h.shape
    n = w.shape[0] if w_transposed else w.shape[1]
    tm, tn = _pick(t, (512, 256)), _pick(n, (1024, 512, 256, 128))

    def body(h_ref, gam_ref, w_ref, xn_ref, o_ref, xn_scr):
        @pl.when(pl.program_id(1) == 0)
        def _():
            xn = _rms_fwd_val(h_ref[...], gam_ref[...]).astype(BF16)
            xn_scr[...] = xn
            xn_ref[...] = xn

        o_ref[...] = _dot(xn_scr[...], w_ref[...], 1, 1 if w_transposed else 0).astype(o_ref.dtype)

    wspec = pl.BlockSpec((tn, d), lambda i, j: (j, 0)) if w_transposed else pl.BlockSpec((d, tn), lambda i, j: (0, j))
    return pl.pallas_call(
        body, name=name, grid=(t // tm, n // tn),
        in_specs=[pl.BlockSpec((tm, d), lambda i, j: (i, 0)), pl.BlockSpec((1, d), lambda i, j: (0, 0)), wspec],
        out_specs=[pl.BlockSpec((tm, d), lambda i, j: (i, 0)), pl.BlockSpec((tm, tn), lambda i, j: (i, j))],
        out_shape=[jax.ShapeDtypeStruct((t, d), BF16), jax.ShapeDtypeStruct((t, n), out_dtype)],
        scratch_shapes=[pltpu.VMEM((tm, d), BF16)], compiler_params=_cparams(("parallel", "arbitrary")),
    )(h, gamma, w)


def _ffn_fwd(pfx, h, gamma, wg, wu, wd, plan=None):
    t, d = h.shape
    nb, f, _ = wg.shape
    tm = _pick(t, (1024, 512, 256))
    nt = t // tm
    host = _Hosted(plan, 5, 4)

    def body(*refs):
        (h_ref, gam_ref, wg_ref, wu_ref, wd_ref), (ho_ref, xn_ref, g_ref, u_ref), (xn_scr, acc_ref), prefs = host.split(refs)
        i, j = pl.program_id(0), pl.program_id(1)
        host.run(0, (i == 0) & (j == 0), prefs)
        host.run(1, (i == nt // 2) & (j == 0), prefs)

        @pl.when(j == 0)
        def _():
            xn = _rms_fwd_val(h_ref[...], gam_ref[...]).astype(BF16)
            xn_scr[...] = xn
            xn_ref[...] = xn
            acc_ref[...] = jnp.zeros_like(acc_ref)

        x = xn_scr[...]
        g = _dot(x, wg_ref[...], 1, 1)
        u = _dot(x, wu_ref[...], 1, 1)
        g_ref[...] = g.astype(BF16)
        u_ref[...] = u.astype(BF16)
        acc_ref[...] += _dot(g * _sigmoid(g) * u, wd_ref[...], 1, 0)

        @pl.when(j == nb - 1)
        def _():
            ho_ref[...] = h_ref[...] + 0.5 * acc_ref[...]

        host.run(2, (i == nt - 1) & (j == nb - 1), prefs)

    row = pl.BlockSpec((tm, d), lambda i, j: (i, 0))
    blk = pl.BlockSpec((None, tm, f), lambda i, j: (j, i, 0))
    wspec = pl.BlockSpec((None, f, d), lambda i, j: (j, 0, 0))
    hc = host.call_args()
    res = pl.pallas_call(
        body, name=pfx + "_fwd", grid=(nt, nb),
        in_specs=[row, pl.BlockSpec((1, d), lambda i, j: (0, 0)), wspec, wspec, wspec] + hc["in_specs"],
        out_specs=[row, row, blk, blk] + hc["out_specs"],
        out_shape=[jax.ShapeDtypeStruct((t, d), F32), jax.ShapeDtypeStruct((t, d), BF16),
                   jax.ShapeDtypeStruct((nb, t, f), BF16), jax.ShapeDtypeStruct((nb, t, f), BF16)] + hc["out_shape"],
        scratch_shapes=[pltpu.VMEM((tm, d), BF16), pltpu.VMEM((tm, d), F32)] + hc["scratch"],
        input_output_aliases=hc["aliases"], compiler_params=_cparams(("arbitrary", "arbitrary")),
    )(h, gamma, wg, wu, wd, *hc["args"])
    return res[:4], res[4:]


def _ffn_bwd(pfx, dh_out, h, gamma, xn, g_all, u_all, wg, wu, wd, plan=None):
    t, d = h.shape
    nb, f, _ = wg.shape
    tm = _pick(t, (512, 256))
    tk = _pick(t, (512, 256))

    nt = t // tm
    host = _Hosted(plan, 8, 5)

    def body(*refs):
        ((dy_ref, h_ref, gam_ref, wg_ref, wu_ref, wd_ref, g_ref, u_ref), (dh_ref, dgam_ref, dg_ref, du_ref, a_ref),
         (acc_ref,), prefs) = host.split(refs)
        i, j = pl.program_id(0), pl.program_id(1)
        host.run(0, (i == 0) & (j == 0), prefs)
        host.run(1, (i == nt // 2) & (j == 0), prefs)

        @pl.when((i == 0) & (j == 0))
        def _():
            dgam_ref[...] = jnp.zeros_like(dgam_ref)

        @pl.when(j == 0)
        def _():
            acc_ref[...] = jnp.zeros_like(acc_ref)

        da = _dot(dy_ref[...], wd_ref[...], 1, 1) * 0.5
        g = g_ref[...].astype(F32)
        u = u_ref[...].astype(F32)
        s = _sigmoid(g)
        sl = g * s
        du = (da * sl).astype(BF16)
        dg = (da * u * (s * (1.0 + g * (1.0 - s)))).astype(BF16)
        du_ref[...] = du
        dg_ref[...] = dg
        a_ref[...] = (sl * u).astype(BF16)
        acc_ref[...] += _dot(dg, wg_ref[...], 1, 0) + _dot(du, wu_ref[...], 1, 0)

        @pl.when(j == nb - 1)
        def _():
            dx, dgam = _rms_bwd_val(acc_ref[...], h_ref[...], gam_ref[...])
            dh_ref[...] = dy_ref[...] + dx
            dgam_ref[...] += dgam

        host.run(2, (i == nt - 1) & (j == nb - 1), prefs)

    row = pl.BlockSpec((tm, d), lambda i, j: (i, 0))
    vec = pl.BlockSpec((1, d), lambda i, j: (0, 0))
    blk = pl.BlockSpec((None, tm, f), lambda i, j: (j, i, 0))
    wspec = pl.BlockSpec((None, f, d), lambda i, j: (j, 0, 0))
    hc = host.call_args()
    res = pl.pallas_call(
        body, name=pfx + "_bwd", grid=(nt, nb),
        in_specs=[row, row, vec, wspec, wspec, wspec, blk, blk] + hc["in_specs"],
        out_specs=[row, vec, blk, blk, blk] + hc["out_specs"],
        out_shape=[jax.ShapeDtypeStruct((t, d), F32), jax.ShapeDtypeStruct((1, d), F32)]
        + [jax.ShapeDtypeStruct((nb, t, f), BF16)] * 3 + hc["out_shape"],
        scratch_shapes=[pltpu.VMEM((tm, d), F32)] + hc["scratch"], input_output_aliases=hc["aliases"],
        compiler_params=_cparams(("arbitrary", "arbitrary")),
    )(dh_out, h, gamma, wg, wu, wd, g_all, u_all, *hc["args"])
    dh, dgamma, dg_all, du_all, a_all = res[:5]

    xmap, bmap, omap = (lambda b, k: (k, 0)), (lambda b, k: (b, k, 0)), (lambda b, k: (b, 0, 0))
    dwg = _mm(pfx + "_dwg", [(dg_all, (None, tk, f), bmap, xn, (tk, d), xmap)], (nb, f, d), (None, f, d), omap,
              (nb, t // tk), 1, ta=True)
    dwu = _mm(pfx + "_dwu", [(du_all, (None, tk, f), bmap, xn, (tk, d), xmap)], (nb, f, d), (None, f, d), omap,
              (nb, t // tk), 1, ta=True)
    dwd = _mm(pfx + "_dwd", [(a_all, (None, tk, f), bmap, dh_out, (tk, d), xmap)], (nb, f, d), (None, f, d), omap,
              (nb, t // tk), 1, ta=True, scale=0.5)
    return (dh, dgamma, dwg, dwu, dwd), res[5:]


def _ffn_bwd_late_dx(pfx, dh_out, h, gamma, xn, g_all, u_all, wg, wu, wd, plan_gu, plans_dw, make_plan_dx):
    t, d = h.shape
    nb, f, _ = wg.shape
    tm = _pick(t, (512, 256))
    tk = _pick(t, (512, 256))
    nt = t // tm
    host_a = _Hosted(plan_gu, 4, 3)

    def body_a(*refs):
        (dy_ref, wd_ref, g_ref, u_ref), (dg_ref, du_ref, a_ref), _, prefs = host_a.split(refs)
        i, j = pl.program_id(0), pl.program_id(1)
        host_a.run(0, (i == 0) & (j == 0), prefs)
        host_a.run(1, (i == 0) & (j == 0), prefs)
        da = _dot(dy_ref[...], wd_ref[...], 1, 1) * 0.5
        g = g_ref[...].astype(F32)
        u = u_ref[...].astype(F32)
        s = _sigmoid(g)
        sl = g * s
        du_ref[...] = (da * sl).astype(BF16)
        dg_ref[...] = (da * u * (s * (1.0 + g * (1.0 - s)))).astype(BF16)
        a_ref[...] = (sl * u).astype(BF16)
        host_a.run(2, (i == nt - 1) & (j == nb - 1), prefs)

    row = pl.BlockSpec((tm, d), lambda i, j: (i, 0))
    vec = pl.BlockSpec((1, d), lambda i, j: (0, 0))
    blk = pl.BlockSpec((None, tm, f), lambda i, j: (j, i, 0))
    wspec = pl.BlockSpec((None, f, d), lambda i, j: (j, 0, 0))
    hc = host_a.call_args()
    res_a = pl.pallas_call(
        body_a, name=pfx + "_bwd_gu", grid=(nt, nb), in_specs=[row, wspec, blk, blk] + hc["in_specs"],
        out_specs=[blk] * 3 + hc["out_specs"], out_shape=[jax.ShapeDtypeStruct((nb, t, f), BF16)] * 3 + hc["out_shape"],
        scratch_shapes=hc["scratch"], input_output_aliases=hc["aliases"], compiler_params=_cparams(("arbitrary", "arbitrary")),
    )(dh_out, wd, g_all, u_all, *hc["args"])
    dg_all, du_all, a_all = res_a[:3]

    xmap, bmap, omap = (lambda b, k: (k, 0)), (lambda b, k: (b, k, 0)), (lambda b, k: (b, 0, 0))
    dwd, out_d = _mm(pfx + "_dwd", [(a_all, (None, tk, f), bmap, dh_out, (tk, d), xmap)], (nb, f, d), (None, f, d), omap,
                     (nb, t // tk), 1, ta=True, scale=0.5, plan=plans_dw[0])
    dwg, out_g = _mm(pfx + "_dwg", [(dg_all, (None, tk, f), bmap, xn, (tk, d), xmap)], (nb, f, d), (None, f, d), omap,
                     (nb, t // tk), 1, ta=True, plan=plans_dw[1])
    dwu, out_u = _mm(pfx + "_dwu", [(du_all, (None, tk, f), bmap, xn, (tk, d), xmap)], (nb, f, d), (None, f, d), omap,
                     (nb, t // tk), 1, ta=True, plan=plans_dw[2])

    plan_dx = make_plan_dx(dwg, dwu, dwd)
    host_b = _Hosted(plan_dx, 7, 2)

    def body_b(*refs):
        (dy_ref, h_ref, gam_ref, wg_ref, wu_ref, dg_ref, du_ref), (dh_ref, dgam_ref), (acc_ref,), prefs = host_b.split(refs)
        i, j = pl.program_id(0), pl.program_id(1)
        host_b.run(0, (i == 0) & (j == 0), prefs)
        host_b.run(1, (i == 0) & (j == 0), prefs)

        @pl.when((i == 0) & (j == 0))
        def _():
            dgam_ref[...] = jnp.zeros_like(dgam_ref)

        @pl.when(j == 0)
        def _():
            acc_ref[...] = jnp.zeros_like(acc_ref)

        acc_ref[...] += _dot(dg_ref[...], wg_ref[...], 1, 0) + _dot(du_ref[...], wu_ref[...], 1, 0)

        @pl.when(j == nb - 1)
        def _():
            dx, dgam = _rms_bwd_val(acc_ref[...], h_ref[...], gam_ref[...])
            dh_ref[...] = dy_ref[...] + dx
            dgam_ref[...] += dgam

        host_b.run(2, (i == nt - 1) & (j == nb - 1), prefs)

    hc = host_b.call_args()
    res_b = pl.pallas_call(
        body_b, name=pfx + "_bwd_dx", grid=(nt, nb), in_specs=[row, row, vec, wspec, wspec, blk, blk] + hc["in_specs"],
        out_specs=[row, vec] + hc["out_specs"],
        out_shape=[jax.ShapeDtypeStruct((t, d), F32), jax.ShapeDtypeStruct((1, d), F32)] + hc["out_shape"],
        scratch_shapes=[pltpu.VMEM((tm, d), F32)] + hc["scratch"], input_output_aliases=hc["aliases"],
        compiler_params=_cparams(("arbitrary", "arbitrary")),
    )(dh_out, h, gamma, wg, wu, dg_all, du_all, *hc["args"])
    return (res_b[0], res_b[1], dwg, dwu, dwd), (res_a[3:], out_d, out_g, out_u, res_b[2:])


HALO = 16


def _silu_grad(y):
    s = _sigmoid(y)
    return s * (1.0 + y * (1.0 - s))


def _with_halo(ref, i, n_tiles, tm, before, after):
    t = ref.shape[0]
    r0 = pl.multiple_of(i * tm, tm)
    parts = [ref[pl.ds(r0, tm), :].astype(F32)]
    if before:
        prev = ref[pl.ds(pl.multiple_of(jnp.maximum(r0 - HALO, 0), HALO), HALO), :].astype(F32)
        parts.insert(0, jnp.where(i > 0, prev, 0.0))
    if after:
        nxt = ref[pl.ds(pl.multiple_of(jnp.minimum(r0 + tm, t - HALO), HALO), HALO), :].astype(F32)
        parts.append(jnp.where(i < n_tiles - 1, nxt, 0.0))
    return jnp.concatenate(parts, axis=0)


def _conv_fwd(zbig, w):
    t, c = zbig.shape[0], w.shape[1]
    tm = _pick(t, (512, 256))
    nt = t // tm

    def body(x_ref, w_ref, o_ref):
        xe = _with_halo(x_ref, pl.program_id(0), nt, tm, True, False)
        wv = w_ref[...]
        y = xe * wv[3:4, :]
        for i in range(CONV_W - 1):
            y = y + pltpu.roll(xe, CONV_W - 1 - i, 0) * wv[i:i + 1, :]
        y = y[HALO:, :]
        o_ref[...] = (y * _sigmoid(y)).astype(o_ref.dtype)

    return pl.pallas_call(
        body, name="conv_fwd", grid=(nt,),
        in_specs=[pl.BlockSpec((t, c), lambda i: (0, 0)), pl.BlockSpec(w.shape, lambda i: (0, 0))],
        out_specs=pl.BlockSpec((tm, c), lambda i: (i, 0)), out_shape=jax.ShapeDtypeStruct((t, c), BF16),
        compiler_params=_cparams(("parallel",)),
    )(zbig, w)


def _conv_bwd(zbig, dact, w):
    t, c = dact.shape
    tm = _pick(t, (512, 256))
    nt = t // tm
    n = tm + HALO

    def body(x_ref, d_ref, w_ref, dx_ref, dw_ref):
        xe = _with_halo(x_ref, pl.program_id(0), nt, tm, True, True)
        de = _with_halo(d_ref, pl.program_id(0), nt, tm, False, True)
        wv = w_ref[...]
        sh = [pltpu.roll(xe, CONV_W - 1 - i, 0)[HALO:, :] if i < CONV_W - 1 else xe[HALO:, :] for i in range(CONV_W)]
        y = sh[0] * wv[0:1, :]
        for i in range(1, CONV_W):
            y = y + sh[i] * wv[i:i + 1, :]
        dy = de * _silu_grad(y)
        dx = dy * wv[3:4, :]
        for i in range(CONV_W - 1):
            dx = dx + pltpu.roll(dy, n - (CONV_W - 1 - i), 0) * wv[i:i + 1, :]
        dx_ref[...] = dx[:tm, :].astype(dx_ref.dtype)
        dyc = dy[:tm, :]
        dwp = jnp.concatenate([_colsum(dyc * sh[i][:tm, :]) for i in range(CONV_W)], axis=0)

        @pl.when(pl.program_id(0) == 0)
        def _():
            dw_ref[...] = jnp.zeros_like(dw_ref)
        dw_ref[...] += dwp

    return pl.pallas_call(
        body, name="conv_bwd", grid=(nt,),
        in_specs=[pl.BlockSpec((t, c), lambda i: (0, 0)), pl.BlockSpec((t, c), lambda i: (0, 0)),
                  pl.BlockSpec(w.shape, lambda i: (0, 0))],
        out_specs=[pl.BlockSpec((tm, c), lambda i: (i, 0)), pl.BlockSpec(w.shape, lambda i: (0, 0))],
        out_shape=[jax.ShapeDtypeStruct((t, c), BF16), jax.ShapeDtypeStruct(w.shape, F32)],
        compiler_params=_cparams(("arbitrary",)),
    )(zbig, dact, w)


LM = 256
HI = lax.Precision.HIGHEST


def _logsig(x):
    return jnp.minimum(x, 0.0) - jnp.log(1.0 + jnp.exp(-jnp.abs(x)))


def _tri(n, lower):
    r = lax.broadcasted_iota(jnp.int32, (n, n), 0)
    c = lax.broadcasted_iota(jnp.int32, (n, n), 1)
    return (r >= c) if lower else (r <= c)


def _f32dot(a, b):
    return lax.dot_general(a, b, (((1,), (0,)), ((), ())), precision=HI, preferred_element_type=F32)


def _mlstm_chunk(h, q_ref, k_ref, v_ref, zs_ref, zsr_ref, bc_ref, br_ref, c_prev, m_prev):
    l = LM
    q = q_ref[:, h * DK_M:(h + 1) * DK_M].astype(F32) * (DK_M ** -0.5)
    k = k_ref[:, h * DK_M:(h + 1) * DK_M]
    v = v_ref[:, h * DV_M:(h + 1) * DV_M]
    lane = lax.broadcasted_iota(jnp.int32, (l, DV_M), 1)
    v1 = jnp.concatenate([v, (lane == 0).astype(v.dtype)], axis=1)
    zs, zsr = zs_ref[...], zsr_ref[...]
    li_c = zs[:, h:h + 1] + bc_ref[:, h:h + 1]
    fp_c = zs[:, NH_M + h:NH_M + h + 1] + bc_ref[:, NH_M + h:NH_M + h + 1]
    li_r = zsr[h:h + 1, :] + br_ref[h:h + 1, :]
    fp_r = zsr[NH_M + h:NH_M + h + 1, :] + br_ref[NH_M + h:NH_M + h + 1, :]
    lf_c, lf_r = _logsig(fp_c), _logsig(fp_r)
    low = _tri(l, True)
    b_c = _f32dot(low.astype(F32), lf_c)
    b_r = _f32dot(lf_r, _tri(l, False).astype(F32))
    g = b_r[:, l - 1:l]
    dmat = jnp.where(low, b_c - b_r + li_r, -jnp.inf)
    inter = b_c + m_prev
    m_t = jnp.maximum(inter, jnp.max(dmat, axis=1, keepdims=True))
    w_inter = jnp.exp(inter - m_t)
    amat = jnp.exp(dmat - m_t)
    s = _dot(q, k, 1, 1)
    p = amat * s
    qc = _dot(q, c_prev, 1, 0)
    qc_w = w_inter * qc
    num1 = qc_w + _dot(p, v1, 1, 0)
    den = num1[:, DV_M:DV_M + 1]
    mx = jnp.maximum(jnp.abs(den), jnp.exp(-m_t))
    hh = num1[:, :DV_M] / mx
    a_c = g - b_c + li_c
    return dict(q=q, k=k, v1=v1, fp_c=fp_c, fp_r=fp_r, b_c=b_c, g=g, m_t=m_t, w_inter=w_inter, amat=amat, s=s, p=p,
                qc_w=qc_w, den=den, mx=mx, hh=hh, a_c=a_c)


def _mlstm_fwd(qk, zbig, zs, zsr, bc, br, gm):
    t = zs.shape[0]
    l = LM
    nc = t // l
    dm = NH_M * DV_M

    def body(q_ref, k_ref, v_ref, o_ref, zs_ref, zsr_ref, bc_ref, br_ref, gm_ref, y_ref, cst_ref, mst_ref, c_scr, m_scr):
        @pl.when(pl.program_id(0) == 0)
        def _():
            c_scr[...] = jnp.zeros_like(c_scr)
            m_scr[...] = jnp.zeros_like(m_scr)

        cst_ref[...] = c_scr[...]
        mst_ref[...] = m_scr[...]
        ys = []
        for h in range(NH_M):
            c_prev = c_scr[h]
            m_prev = m_scr[h:h + 1, 0:1]
            r = _mlstm_chunk(h, q_ref, k_ref, v_ref, zs_ref, zsr_ref, bc_ref, br_ref, c_prev, m_prev)
            hh = r["hh"]
            gh = gm_ref[:, h * DV_M:(h + 1) * DV_M]
            hn = hh * lax.rsqrt(jnp.mean(hh * hh, axis=-1, keepdims=True) + EPS) * gh
            og = o_ref[:, h * DV_M:(h + 1) * DV_M].astype(F32)
            ys.append(hn * _sigmoid(og))
            m_new = jnp.maximum(r["g"] + m_prev, jnp.max(r["a_c"], axis=0, keepdims=True))
            decay = jnp.exp(r["g"] + m_prev - m_new)
            wk = r["k"].astype(F32) * jnp.exp(r["a_c"] - m_new)
            c_scr[h] = decay * c_prev + _dot(wk, r["v1"], 0, 0)
            m_scr[h:h + 1, :] = jnp.broadcast_to(m_new, (1, 128))
        y_ref[...] = jnp.concatenate(ys, axis=1).astype(y_ref.dtype)

    return pl.pallas_call(
        body, name="mlstm_fwd", grid=(nc,),
        in_specs=[pl.BlockSpec((l, NH_M * DK_M), lambda i: (i, 0)), pl.BlockSpec((l, NH_M * DK_M), lambda i: (i, 1)),
                  pl.BlockSpec((l, dm), lambda i: (i, 1)), pl.BlockSpec((l, dm), lambda i: (i, 2)),
                  pl.BlockSpec((l, 128), lambda i: (i, 0)), pl.BlockSpec((8, l), lambda i: (0, i)),
                  pl.BlockSpec((1, 8), lambda i: (0, 0)), pl.BlockSpec((8, 1), lambda i: (0, 0)),
                  pl.BlockSpec((1, dm), lambda i: (0, 0))],
        out_specs=[pl.BlockSpec((l, dm), lambda i: (i, 0)), pl.BlockSpec((None, NH_M, DK_M, 2 * DV_M), lambda i: (i, 0, 0, 0)),
                   pl.BlockSpec((None, 8, 128), lambda i: (i, 0, 0))],
        out_shape=[jax.ShapeDtypeStruct((t, dm), BF16), jax.ShapeDtypeStruct((nc, NH_M, DK_M, 2 * DV_M), F32),
                   jax.ShapeDtypeStruct((nc, 8, 128), F32)],
        scratch_shapes=[pltpu.VMEM((NH_M, DK_M, 2 * DV_M), F32), pltpu.VMEM((8, 128), F32)],
        compiler_params=_cparams(("arbitrary",)),
    )(qk, qk, zbig, zbig, zs, zsr, bc, br, gm)


def _mlstm_bwd(qk, zbig, zs, zsr, bc, br, gm, cst, mst, dycat):
    t = zs.shape[0]
    l = LM
    nc = t // l
    dm = NH_M * DV_M

    def body(q_ref, k_ref, v_ref, o_ref, zs_ref, zsr_ref, bc_ref, br_ref, gm_ref, cst_ref, mst_ref, cnx_ref, mnx_ref,
             dy_ref, dqk_ref, dv_ref, do_ref, dzs_ref, dzr_ref, dgm_ref, dc_scr):
        @pl.when(pl.program_id(0) == 0)
        def _():
            dc_scr[...] = jnp.zeros_like(dc_scr)
            dgm_ref[...] = jnp.zeros_like(dgm_ref)

        lane = lax.broadcasted_iota(jnp.int32, (l, 128), 1)
        upper = _tri(l, False).astype(F32)
        lower = _tri(l, True).astype(F32)
        dzr_rows = [None] * 8
        dvs, dos, dgs, dqs, dks = [], [], [], [], []
        dzs = jnp.zeros((l, 128), F32)
        for h in range(NH_M):
            c_prev = cst_ref[h]
            m_prev = mst_ref[h:h + 1, 0:1]
            r = _mlstm_chunk(h, q_ref, k_ref, v_ref, zs_ref, zsr_ref, bc_ref, br_ref, c_prev, m_prev)
            hh, mx, den, m_t, v1, amat = r["hh"], r["mx"], r["den"], r["m_t"], r["v1"], r["amat"]
            gh = gm_ref[:, h * DV_M:(h + 1) * DV_M]
            rs = lax.rsqrt(jnp.mean(hh * hh, axis=-1, keepdims=True) + EPS)
            xh = hh * rs
            sg = _sigmoid(o_ref[:, h * DV_M:(h + 1) * DV_M].astype(F32))
            dyh = dy_ref[:, h * DV_M:(h + 1) * DV_M]
            dos.append(dyh * xh * gh * sg * (1.0 - sg))
            dhn = dyh * sg
            dgs.append(_colsum(dhn * xh))
            dxh = dhn * gh
            dh = rs * (dxh - xh * jnp.mean(dxh * xh, axis=-1, keepdims=True))
            g1 = dh / mx
            hd = jnp.sum(hh * dh, axis=-1, keepdims=True)
            dden = jnp.where(jnp.abs(den) > jnp.exp(-m_t), -hd / mx * jnp.sign(den), 0.0)
            g256 = jnp.concatenate([g1, jnp.where(lane == 0, dden, 0.0)], axis=1)
            dc_h = dc_scr[h]
            ea = jnp.exp(r["a_c"])
            dp = _dot(g256, v1, 1, 1)
            ds = dp * amat
            dqs.append((r["w_inter"] * _dot(g256, c_prev, 1, 1) + _dot(ds, r["k"], 1, 0)) * (DK_M ** -0.5))
            dks.append(_dot(ds, r["q"], 0, 0) + ea * _dot(v1, dc_h, 1, 1))
            dv_st = ea * _dot(r["k"], dc_h, 1, 0)
            dv1 = _dot(r["p"], g256, 0, 0) + dv_st
            dvs.append(dv1[:, :DV_M])
            wmat = dp * r["p"]
            c_in = _colsum(wmat)
            c_st = jnp.sum(v1.astype(F32) * dv_st, axis=-1, keepdims=True)
            r_t = jnp.sum(wmat, axis=1, keepdims=True) + jnp.sum(g256 * r["qc_w"], axis=-1, keepdims=True)
            db = r_t - c_st
            carry = jnp.exp(mnx_ref[h:h + 1, 0:1]) * jnp.sum(
                jnp.sum(dc_h * cnx_ref[h], axis=1, keepdims=True), axis=0, keepdims=True)
            dlf_c = _f32dot(upper, db) + carry
            dlf_r = -_f32dot(c_in, lower)
            dfp = dlf_c * _sigmoid(-r["fp_c"])
            dzs = dzs + jnp.where(lane == h, c_st, 0.0) + jnp.where(lane == NH_M + h, dfp, 0.0)
            dzr_rows[h] = c_in
            dzr_rows[NH_M + h] = dlf_r * _sigmoid(-r["fp_r"])
            wq = r["q"] * jnp.exp(r["b_c"] - m_t)
            dc_scr[h] = jnp.exp(r["g"]) * dc_h + _dot(wq, g256, 0, 0)
        dqk_ref[...] = jnp.concatenate(dqs + dks, axis=1)
        dv_ref[...] = jnp.concatenate(dvs, axis=1).astype(dv_ref.dtype)
        do_ref[...] = jnp.concatenate(dos, axis=1).astype(do_ref.dtype)
        dzs_ref[...] = dzs
        dzr_ref[...] = jnp.concatenate(dzr_rows, axis=0)
        dgm_ref[...] += jnp.concatenate(dgs, axis=1)

    rev = lambda i: nc - 1 - i
    nxt = lambda i: jnp.minimum(nc - i, nc - 1)
    return pl.pallas_call(
        body, name="mlstm_bwd", grid=(nc,),
        in_specs=[pl.BlockSpec((l, NH_M * DK_M), lambda i: (rev(i), 0)), pl.BlockSpec((l, NH_M * DK_M), lambda i: (rev(i), 1)),
                  pl.BlockSpec((l, dm), lambda i: (rev(i), 1)), pl.BlockSpec((l, dm), lambda i: (rev(i), 2)),
                  pl.BlockSpec((l, 128), lambda i: (rev(i), 0)), pl.BlockSpec((8, l), lambda i: (0, rev(i))),
                  pl.BlockSpec((1, 8), lambda i: (0, 0)), pl.BlockSpec((8, 1), lambda i: (0, 0)),
                  pl.BlockSpec((1, dm), lambda i: (0, 0)),
                  pl.BlockSpec((None, NH_M, DK_M, 2 * DV_M), lambda i: (rev(i), 0, 0, 0)),
                  pl.BlockSpec((None, 8, 128), lambda i: (rev(i), 0, 0)),
                  pl.BlockSpec((None, NH_M, DK_M, 2 * DV_M), lambda i: (nxt(i), 0, 0, 0)),
                  pl.BlockSpec((None, 8, 128), lambda i: (nxt(i), 0, 0)),
                  pl.BlockSpec((l, dm), lambda i: (rev(i), 0))],
        out_specs=[pl.BlockSpec((l, dm), lambda i: (rev(i), 0)),
                   pl.BlockSpec((l, dm), lambda i: (rev(i), 0)), pl.BlockSpec((l, dm), lambda i: (rev(i), 0)),
                   pl.BlockSpec((l, 128), lambda i: (rev(i), 0)), pl.BlockSpec((8, l), lambda i: (0, rev(i))),
                   pl.BlockSpec((1, dm), lambda i: (0, 0))],
        out_shape=[jax.ShapeDtypeStruct((t, dm), F32),
                   jax.ShapeDtypeStruct((t, dm), BF16), jax.ShapeDtypeStruct((t, dm), BF16),
                   jax.ShapeDtypeStruct((t, 128), F32), jax.ShapeDtypeStruct((8, t), F32),
                   jax.ShapeDtypeStruct((1, dm), F32)],
        scratch_shapes=[pltpu.VMEM((NH_M, DK_M, 2 * DV_M), F32)],
        compiler_params=_cparams(("arbitrary",)),
    )(qk, qk, zbig, zbig, zs, zsr, bc, br, gm, cst, mst, cst, mst, dycat)


def _fox_cumsum(zsr, bf_r):
    t = zsr.shape[1]
    cw = _pick(t, (512, 256))

    def body(z_ref, b_ref, c_ref):
        up = _tri(cw, False).astype(F32)
        carry = jnp.zeros((NH_F, 1), F32)
        for j in range(t // cw):
            cs = _f32dot(_logsig(z_ref[:, j * cw:(j + 1) * cw] + b_ref[...]), up) + carry
            c_ref[:, j * cw:(j + 1) * cw] = cs
            carry = cs[:, cw - 1:cw]

    return pl.pallas_call(
        body, name="fox_cumsum", grid=(1,),
        in_specs=[pl.BlockSpec((NH_F, t), lambda i: (1, 0)), pl.BlockSpec((NH_F, 1), lambda i: (0, 0))],
        out_specs=pl.BlockSpec((NH_F, t), lambda i: (0, 0)), out_shape=jax.ShapeDtypeStruct((NH_F, t), F32),
        compiler_params=_cparams(("arbitrary",)),
    )(zsr, bf_r)


def _fox_gate_bwd(zsr, bf_r, dc):
    t = zsr.shape[1]
    cw = _pick(t, (512, 256))

    def body(z_ref, b_ref, dc_ref, o_ref):
        low = _tri(cw, True).astype(F32)
        carry = jnp.zeros((NH_F, 1), F32)
        for j in reversed(range(t // cw)):
            sl = slice(j * cw, (j + 1) * cw)
            dlf = _f32dot(dc_ref[:, sl], low) + carry
            o_ref[:, sl] = dlf * _sigmoid(-(z_ref[:, sl] + b_ref[...]))
            carry = dlf[:, 0:1]

    return pl.pallas_call(
        body, name="fox_gate_bwd", grid=(1,),
        in_specs=[pl.BlockSpec((NH_F, t), lambda i: (1, 0)), pl.BlockSpec((NH_F, 1), lambda i: (0, 0)),
                  pl.BlockSpec((NH_F, t), lambda i: (0, 0))],
        out_specs=pl.BlockSpec((NH_F, t), lambda i: (0, 0)), out_shape=jax.ShapeDtypeStruct((NH_F, t), F32),
        compiler_params=_cparams(("arbitrary",)),
    )(zsr, bf_r, dc)


def _causal_mask(n):
    return _tri(n, True)


def _fox_fwd(q, k, v, c_col, c_row, gf):
    nh, t, dh = q.shape
    tq = _pick(t, (512, 256))
    scale = dh ** -0.5

    def body(q_ref, k_ref, v_ref, cc_ref, cr_ref, g_ref, o_ref, lse_ref, y_ref):
        i = pl.program_id(1)
        qv = q_ref[...]
        cq = cc_ref[...]

        def blk(j, carry, masked):
            m, l, acc = carry
            k0 = pl.multiple_of(j * tq, tq)
            kb = k_ref[pl.ds(k0, tq), :]
            vb = v_ref[pl.ds(k0, tq), :]
            s = _dot(qv, kb, 1, 1) * scale + cq - cr_ref[:, pl.ds(k0, tq)]
            if masked:
                s = jnp.where(_causal_mask(tq), s, -jnp.inf)
            m_new = jnp.maximum(m, jnp.max(s, axis=1, keepdims=True))
            alpha = jnp.exp(m - m_new)
            p = jnp.exp(s - m_new)
            return m_new, alpha * l + jnp.sum(p, axis=1, keepdims=True), alpha * acc + _dot(p, vb, 1, 0)

        init = (jnp.full((tq, 1), -jnp.inf, F32), jnp.zeros((tq, 1), F32), jnp.zeros((tq, dh), F32))
        carry = lax.fori_loop(0, i, lambda j, c: blk(j, c, False), init)
        m, l, acc = blk(i, carry, True)
        o = acc / l
        o_ref[...] = o
        lse_ref[...] = m + jnp.log(l)
        y_ref[...] = (o * lax.rsqrt(jnp.mean(o * o, axis=-1, keepdims=True) + EPS) * g_ref[...]).astype(y_ref.dtype)

    full = lambda w: pl.BlockSpec((None, t, w), lambda h, i: (h, 0, 0))
    tile = lambda w: pl.BlockSpec((None, tq, w), lambda h, i: (h, i, 0))
    return pl.pallas_call(
        body, name="fox_fwd", grid=(nh, t // tq),
        in_specs=[tile(dh), full(dh), full(dh), tile(1), pl.BlockSpec((None, 1, t), lambda h, i: (h, 0, 0)),
                  pl.BlockSpec((None, 1, dh), lambda h, i: (h, 0, 0))],
        out_specs=[tile(dh), tile(1), tile(dh)],
        out_shape=[jax.ShapeDtypeStruct((nh, t, dh), F32), jax.ShapeDtypeStruct((nh, t, 1), F32),
                   jax.ShapeDtypeStruct((nh, t, dh), BF16)],
        compiler_params=_cparams(("parallel", "parallel")),
    )(q, k, v, c_col, c_row, gf)


def _fox_norm_bwd(dy, o, gf):
    nh, t, dh = o.shape
    tm = _pick(t, (512, 256))

    def body(dy_ref, o_ref, g_ref, do_ref, dl_ref, dg_ref):
        ov = o_ref[...]
        dx, dg = _rms_bwd_val(dy_ref[...], ov, g_ref[...])
        do_ref[...] = dx
        dl_ref[...] = jnp.sum(dx * ov, axis=-1, keepdims=True)

        @pl.when(pl.program_id(1) == 0)
        def _():
            dg_ref[...] = jnp.zeros_like(dg_ref)
        dg_ref[...] += dg

    tile = lambda w: pl.BlockSpec((None, tm, w), lambda h, i: (h, i, 0))
    gspec = pl.BlockSpec((None, 1, dh), lambda h, i: (h, 0, 0))
    return pl.pallas_call(
        body, name="fox_norm_bwd", grid=(nh, t // tm), in_specs=[tile(dh), tile(dh), gspec],
        out_specs=[tile(dh), tile(1), gspec],
        out_shape=[jax.ShapeDtypeStruct((nh, t, dh), F32), jax.ShapeDtypeStruct((nh, t, 1), F32),
                   jax.ShapeDtypeStruct((nh, 1, dh), F32)],
        compiler_params=_cparams(("parallel", "arbitrary")),
    )(dy, o, gf)


def _fox_bwd(q, k, v, c_col, c_row, do, lse, delta):
    nh, t, dh = q.shape
    tq = _pick(t, (512, 256))
    nq = t // tq
    scale = dh ** -0.5

    def body(q_ref, k_ref, v_ref, cc_ref, cr_ref, do_ref, lse_ref, dl_ref, dq_ref, dk_ref, dv_ref, dc_ref, dcq_ref):
        j = pl.program_id(1)

        @pl.when(j == 0)
        def _():
            dq_ref[...] = jnp.zeros_like(dq_ref)
            dcq_ref[...] = jnp.zeros_like(dcq_ref)

        kb, vb, crb = k_ref[...], v_ref[...], cr_ref[...]

        def blk(i, carry, masked):
            dk, dv, dc = carry
            rows = pl.ds(pl.multiple_of(i * tq, tq), tq)
            qb = q_ref[rows, :]
            dob = do_ref[rows, :].astype(BF16)
            s = _dot(qb, kb, 1, 1) * scale + cc_ref[rows, :] - crb
            if masked:
                s = jnp.where(_causal_mask(tq), s, -jnp.inf)
            p = jnp.exp(s - lse_ref[rows, :])
            dv = dv + _dot(p, dob, 0, 0)
            ds = p * (_dot(dob, vb, 1, 1) - dl_ref[rows, :])
            dc = dc + _colsum(ds)
            dk = dk + _dot(ds, qb, 0, 0) * scale
            dq_ref[rows, :] += _dot(ds, kb, 1, 0) * scale
            dcq_ref[rows, :] += jnp.sum(ds, axis=1, keepdims=True)
            return dk, dv, dc

        init = (jnp.zeros((tq, dh), F32), jnp.zeros((tq, dh), F32), jnp.zeros((1, tq), F32))
        carry = blk(j, init, True)
        dk, dv, dc = lax.fori_loop(j + 1, nq, lambda i, c: blk(i, c, False), carry)
        dk_ref[...] = dk
        dv_ref[...] = dv
        dc_ref[...] = -dc

    full = lambda w: pl.BlockSpec((None, t, w), lambda h, j: (h, 0, 0))
    tile = lambda w: pl.BlockSpec((None, tq, w), lambda h, j: (h, j, 0))
    crow = pl.BlockSpec((None, 1, tq), lambda h, j: (h, 0, j))
    return pl.pallas_call(
        body, name="fox_bwd", grid=(nh, nq),
        in_specs=[full(dh), tile(dh), tile(dh), full(1), crow, full(dh), full(1), full(1)],
        out_specs=[full(dh), tile(dh), tile(dh), crow, full(1)],
        out_shape=[jax.ShapeDtypeStruct((nh, t, dh), F32)] * 3 + [jax.ShapeDtypeStruct((nh, 1, t), F32),
                                                                jax.ShapeDtypeStruct((nh, t, 1), F32)],
        compiler_params=_cparams(("parallel", "arbitrary")),
    )(q, k, v, c_col, c_row, do, lse, delta)


AUG = 64


def _split3(c):
    hi = c.astype(BF16).astype(F32)
    r1 = c - hi
    mid = r1.astype(BF16).astype(F32)
    return hi, mid, r1 - mid


def _fox_prep(zbig, ct):
    t = zbig.shape[0]
    tm = _pick(t, (512, 256))

    def body(q_ref, k_ref, v_ref, c_ref, qo_ref, ko_ref, vo_ref):
        lane = lax.broadcasted_iota(jnp.int32, (tm, AUG), 1)
        qv, kv, vv, cv = q_ref[...], k_ref[...], v_ref[...], c_ref[...]
        one = (lane == 0).astype(BF16)
        for h in range(NH_F):
            hi, mid, lo = _split3(cv[:, h:h + 1])
            aq = jnp.where(lane == 0, hi, jnp.where(lane == 1, mid, jnp.where(lane == 2, lo, jnp.where(lane < 6, 1.0, 0.0))))
            ak = jnp.where(lane < 3, 1.0, jnp.where(lane == 3, -hi, jnp.where(lane == 4, -mid, jnp.where(lane == 5, -lo, 0.0))))
            sl = slice(h * DH_F, (h + 1) * DH_F)
            qo_ref[h] = jnp.concatenate([qv[:, sl] * (DH_F ** -0.5), aq.astype(BF16)], axis=1).astype(BF16)
            ko_ref[h] = jnp.concatenate([kv[:, sl], ak.astype(BF16)], axis=1)
            vo_ref[h] = jnp.concatenate([vv[:, sl], one], axis=1)

    ospec = pl.BlockSpec((NH_F, tm, 128), lambda i: (0, i, 0))
    return pl.pallas_call(
        body, name="fox_prep", grid=(t // tm,),
        in_specs=[pl.BlockSpec((tm, 512), lambda i: (i, 3)), pl.BlockSpec((tm, 512), lambda i: (i, 4)),
                  pl.BlockSpec((tm, 512), lambda i: (i, 5)), pl.BlockSpec((tm, NH_F), lambda i: (i, 0))],
        out_specs=[ospec] * 3, out_shape=[jax.ShapeDtypeStruct((NH_F, t, 128), BF16)] * 3,
        compiler_params=_cparams(("parallel",)),
    )(zbig, zbig, zbig, ct)


def _fox_fwd2(qa, ka, va, gf, plan=None):
    nh, t, _ = qa.shape
    tq = _pick(t, (512, 256))
    nq = t // tq
    group = 2
    host = _Hosted(plan, 4, 3)

    def body(*refs):
        (q_ref, k_ref, v_ref, g_ref), (y_ref, o_ref, lse_ref), _, prefs = host.split(refs)
        i = pl.program_id(0)
        host.run(0, i == 0, prefs)
        host.run(1, i == max(nq - 2, 0), prefs)
        lane = lax.broadcasted_iota(jnp.int32, (tq, 128), 1)
        ys, os_ = [], []
        lse_all = jnp.zeros((tq, 128), F32)
        for h0 in range(0, nh, group):
            heads = range(h0, h0 + group)
            qvs = [q_ref[h] for h in heads]

            def blk(j, carry, masked, heads=heads, qvs=qvs):
                k0 = pl.multiple_of(j * tq, tq)
                out = []
                for (m, acc), h, qv in zip(carry, heads, qvs):
                    s = lax.dot_general(qv, k_ref[h, pl.ds(k0, tq), :], (((1,), (1,)), ((), ())), preferred_element_type=F32)
                    if masked:
                        s = jnp.where(_causal_mask(tq), s, -jnp.inf)
                    m_new = jnp.maximum(m, jnp.max(s, axis=1, keepdims=True))
                    p = jnp.exp(s - m_new).astype(BF16)
                    pv = lax.dot_general(p, v_ref[h, pl.ds(k0, tq), :], (((1,), (0,)), ((), ())), preferred_element_type=F32)
                    out.append((m_new, jnp.exp(m - m_new) * acc + pv))
                return tuple(out)

            init = tuple((jnp.full((tq, 1), -jnp.inf, F32), jnp.zeros((tq, 128), F32)) for _ in heads)
            carry = lax.fori_loop(0, i, lambda j, c: blk(j, c, False), init)
            for (m, acc), h in zip(blk(i, carry, True), heads):
                l = acc[:, DH_F:DH_F + 1]
                o = acc[:, :DH_F] / l
                os_.append(o)
                gh = g_ref[:, h * DH_F:(h + 1) * DH_F]
                ys.append(o * lax.rsqrt(jnp.mean(o * o, axis=-1, keepdims=True) + EPS) * gh)
                lse_all = lse_all + jnp.where(lane == h, m + jnp.log(l), 0.0)
        y_ref[...] = jnp.concatenate(ys, axis=1).astype(y_ref.dtype)
        o_ref[...] = jnp.concatenate(os_, axis=1)
        lse_ref[...] = lse_all
        host.run(2, i == nq - 1, prefs)

    full = pl.BlockSpec((nh, t, 128), lambda i: (0, 0, 0))
    hc = host.call_args()
    res = pl.pallas_call(
        body, name="fox_fwd", grid=(nq,),
        in_specs=[pl.BlockSpec((nh, tq, 128), lambda i: (0, i, 0)), full, full, pl.BlockSpec((1, nh * DH_F), lambda i: (0, 0))]
        + hc["in_specs"],
        out_specs=[pl.BlockSpec((tq, nh * DH_F), lambda i: (i, 0)), pl.BlockSpec((tq, nh * DH_F), lambda i: (i, 0)),
                   pl.BlockSpec((tq, 128), lambda i: (i, 0))] + hc["out_specs"],
        out_shape=[jax.ShapeDtypeStruct((t, nh * DH_F), BF16), jax.ShapeDtypeStruct((t, nh * DH_F), F32),
                   jax.ShapeDtypeStruct((t, 128), F32)] + hc["out_shape"],
        scratch_shapes=hc["scratch"], input_output_aliases=hc["aliases"], compiler_params=_cparams(("arbitrary",)),
    )(qa, ka, va, gf, *hc["args"])
    return res[:3], res[3:]


def _fox_bwd_prep(dycat, o, gf):
    t = o.shape[0]
    tm = _pick(t, (512, 256))

    def body(dy_ref, o_ref, g_ref, do_ref, dl_ref, dg_ref):
        lane = lax.broadcasted_iota(jnp.int32, (tm, 128), 1)
        dyv, ov, gv = dy_ref[...], o_ref[...], g_ref[...]
        dgs = []
        dl = jnp.zeros((tm, 128), F32)
        pad = jnp.zeros((tm, AUG), BF16)
        for h in range(NH_F):
            sl = slice(h * DH_F, (h + 1) * DH_F)
            dx, dg = _rms_bwd_val(dyv[:, sl], ov[:, sl], gv[:, sl])
            dgs.append(dg)
            do_ref[h] = jnp.concatenate([dx.astype(BF16), pad], axis=1)
            dl = dl + jnp.where(lane == h, jnp.sum(dx * ov[:, sl], axis=-1, keepdims=True), 0.0)
        dl_ref[...] = dl

        @pl.when(pl.program_id(0) == 0)
        def _():
            dg_ref[...] = jnp.zeros_like(dg_ref)
        dg_ref[...] += jnp.concatenate(dgs, axis=1)

    return pl.pallas_call(
        body, name="fox_bwd_prep", grid=(t // tm,),
        in_specs=[pl.BlockSpec((tm, 512), lambda i: (i, 1)), pl.BlockSpec((tm, 512), lambda i: (i, 0)),
                  pl.BlockSpec((1, 512), lambda i: (0, 0))],
        out_specs=[pl.BlockSpec((NH_F, tm, 128), lambda i: (0, i, 0)), pl.BlockSpec((tm, 128), lambda i: (i, 0)),
                   pl.BlockSpec((1, 512), lambda i: (0, 0))],
        out_shape=[jax.ShapeDtypeStruct((NH_F, t, 128), BF16), jax.ShapeDtypeStruct((t, 128), F32),
                   jax.ShapeDtypeStruct((1, 512), F32)],
        compiler_params=_cparams(("arbitrary",)),
    )(dycat, o, gf)


def _fox_bwd2(qa, ka, va, doa, lse, delta):
    nh, t, _ = qa.shape
    tq = _pick(t, (512, 256))
    nq = t // tq

    group = 2

    def body(q_ref, k_ref, v_ref, do_ref, lse_ref, dl_ref, dq_ref, dk_ref, dv_ref):
        hp, j = pl.program_id(0), pl.program_id(1)

        @pl.when(j == 0)
        def _():
            dq_ref[...] = jnp.zeros_like(dq_ref)

        lane = lax.broadcasted_iota(jnp.int32, (tq, 128), 1)

        def blk(i, carry, masked):
            rows = pl.ds(pl.multiple_of(i * tq, tq), tq)
            lse_t, dl_t = lse_ref[rows, :], dl_ref[rows, :]
            out = []
            for g, (dk, dv) in enumerate(carry):
                h = hp * group + g
                kb, vb = k_ref[g], v_ref[g]
                qb, dob = q_ref[g, rows, :], do_ref[g, rows, :]
                lse_h = jnp.sum(jnp.where(lane == h, lse_t, 0.0), axis=1, keepdims=True)
                dl_h = jnp.sum(jnp.where(lane == h, dl_t, 0.0), axis=1, keepdims=True)
                s = lax.dot_general(qb, kb, (((1,), (1,)), ((), ())), preferred_element_type=F32)
                if masked:
                    s = jnp.where(_causal_mask(tq), s, -jnp.inf)
                p = jnp.exp(s - lse_h)
                dp = lax.dot_general(dob, vb, (((1,), (1,)), ((), ())), preferred_element_type=F32)
                ds = (p * (dp - dl_h)).astype(BF16)
                dv = dv + lax.dot_general(p.astype(BF16), dob, (((0,), (0,)), ((), ())), preferred_element_type=F32)
                dk = dk + lax.dot_general(ds, qb, (((0,), (0,)), ((), ())), preferred_element_type=F32)
                dq_ref[g, rows, :] += lax.dot_general(ds, kb, (((1,), (0,)), ((), ())), preferred_element_type=F32)
                out.append((dk, dv))
            return tuple(out)

        init = tuple((jnp.zeros((tq, 128), F32), jnp.zeros((tq, 128), F32)) for _ in range(group))
        carry = blk(j, init, True)
        carry = lax.fori_loop(j + 1, nq, lambda i, c: blk(i, c, False), carry)
        for g, (dk, dv) in enumerate(carry):
            dk_ref[g] = dk
            dv_ref[g] = dv

    full = pl.BlockSpec((group, t, 128), lambda h, j: (h, 0, 0))
    tile = pl.BlockSpec((group, tq, 128), lambda h, j: (h, j, 0))
    cols = pl.BlockSpec((t, 128), lambda h, j: (0, 0))
    return pl.pallas_call(
        body, name="fox_bwd", grid=(nh // group, nq), in_specs=[full, tile, tile, full, cols, cols],
        out_specs=[full, tile, tile], out_shape=[jax.ShapeDtypeStruct((nh, t, 128), F32)] * 3,
        compiler_params=_cparams(("parallel", "arbitrary")),
    )(qa, ka, va, doa, lse, delta)


def _fox_bwd_post(dqa, dka, dva):
    nh, t, _ = dqa.shape
    tm = _pick(t, (512, 256))

    def body(dq_ref, dk_ref, dv_ref, oq_ref, ok_ref, ov_ref, dc_ref):
        lane = lax.broadcasted_iota(jnp.int32, (tm, 128), 1)
        dc = jnp.zeros((tm, 128), F32)
        qs, ks, vs = [], [], []
        for h in range(nh):
            dq, dk = dq_ref[h], dk_ref[h]
            qs.append(dq[:, :DH_F] * (DH_F ** -0.5))
            ks.append(dk[:, :DH_F])
            vs.append(dv_ref[h][:, :DH_F])
            dc = dc + jnp.where(lane == h, dq[:, DH_F:DH_F + 1] - dk[:, DH_F + 3:DH_F + 4], 0.0)
        oq_ref[...] = jnp.concatenate(qs, axis=1).astype(BF16)
        ok_ref[...] = jnp.concatenate(ks, axis=1).astype(BF16)
        ov_ref[...] = jnp.concatenate(vs, axis=1).astype(BF16)
        dc_ref[...] = dc

    ispec = pl.BlockSpec((nh, tm, 128), lambda i: (0, i, 0))
    ospec = pl.BlockSpec((tm, nh * DH_F), lambda i: (i, 0))
    return pl.pallas_call(
        body, name="fox_bwd_post", grid=(t // tm,), in_specs=[ispec] * 3,
        out_specs=[ospec] * 3 + [pl.BlockSpec((tm, 128), lambda i: (i, 0))],
        out_shape=[jax.ShapeDtypeStruct((t, nh * DH_F), BF16)] * 3 + [jax.ShapeDtypeStruct((t, 128), F32)],
        compiler_params=_cparams(("parallel",)),
    )(dqa, dka, dva)


W_BIG = 6 * 512
IN_OFF = (0, 512, 1024, 1544, 2056, 2568)
IN_GATES = (1536, 3080)


def _heads(a, nh):
    t = a.shape[0]
    return a.reshape(t, nh, -1).transpose(1, 0, 2)


def _unheads(a):
    nh, t, dh = a.shape
    return a.transpose(1, 0, 2).reshape(t, nh * dh)


FFN1 = ("ffn1_w_gate", "ffn1_w_up", "ffn1_w_down")
REST = ("w_in", "w_out", "ffn2_w_gate", "ffn2_w_up", "ffn2_w_down", "w_ple_gate", "w_ple_proj")
SPLIT = {n: 1 if n == "w_in" else 0 for n in FFN1 + REST}


def _rs_partials(names, gw, c_idx):
    wire = [_cast_other_half("rs_cast_" + n, gw[n], c_idx, SPLIT[n]) for n in names]
    swapped = _swap("rs_swap_" + names[0], wire)
    return [_add_my_half("rs_add_" + n, gw[n], r, c_idx, SPLIT[n]) for n, r in zip(names, swapped)]


def _local_step(x, p, tgt, sp, wg1, wu1, wd1, rest_slots, c_idx, place):
    t, d = x.shape
    slot = dict(zip(REST + ("conv_qk",), rest_slots))
    (h1, xn1, g1, u1), (w_in, conv_w) = _ffn_fwd(
        "ffn1", x, sp["ffn1_norm"], wg1, wu1, wd1, plan=_gather_plan([slot["w_in"], slot["conv_qk"]], [SPLIT["w_in"], None]))
    w_in, conv_w = w_in.reshape(-1, d), _from_chip_blocks(conv_w)
    w_big = jnp.concatenate([w_in[o:o + 512] for o in IN_OFF], axis=0)
    w_small = jnp.concatenate([w_in[IN_GATES[0]:IN_GATES[0] + 8], w_in[IN_GATES[1]:IN_GATES[1] + 8],
                               jnp.zeros((112, d), w_in.dtype)], axis=0)
    u, zbig = _norm_mm("in_big", h1, sp["mix_norm"], w_big, True, BF16)
    zs = _mm_nt("in_small", u, w_small, tm=1024, tk=1024)
    zsr = zs.T
    qk_act = _conv_fwd(zbig, conv_w)
    bm_c, bf_c = sp["b_mlstm_gates"], sp["b_fox_f"]
    y_m, cst, mst = _mlstm_fwd(qk_act, zbig, zs, zsr, bm_c, bm_c.T, sp["mlstm_out_norm"])
    c = _fox_cumsum(zsr, bf_c.T)
    qa, ka, va = _fox_prep(zbig, c.T)
    (y_ft, o_f, lse), late = _fox_fwd2(qa, ka, va, sp["fox_out_norm"],
                                       plan=_gather_plan([slot[n] for n in REST[1:]], [SPLIT[n] for n in REST[1:]]))
    full = dict(zip(REST[1:], late))
    w_out, w_pg = (full[n].reshape(-1, d) for n in ("w_out", "w_ple_gate"))
    wg2, wu2, wd2 = full["ffn2_w_gate"], full["ffn2_w_up"], full["ffn2_w_down"]
    w_pp = _from_chip_blocks(full["w_ple_proj"])
    tm = _pick(t, (1024, 512, 256))
    h2 = _mm("out_proj", [(y_m, (tm, 512), lambda i, j, k: (i, 0), w_out, (512, d), lambda i, j, k: (0, 0)),
                          (y_ft, (tm, 512), lambda i, j, k: (i, 0), w_out, (512, d), lambda i, j, k: (1, 0))],
             (t, d), (tm, d), lambda i, j, k: (i, 0), (t // tm, 1, 1), 2, res=h1)
    (h3, xn2, g2, u2), _ = _ffn_fwd("ffn2", h2, sp["ffn2_norm"], wg2, wu2, wd2)
    hn3, gate_pre = _norm_mm("ple_gate", h3, sp["ple_gate_norm"], w_pg, False, F32)
    pp = _mm_nn("ple_proj", p, w_pp, tm=1024)

    def head_fn(h3_t, gp_t, pp_t, tgt_t, g_pp, g_fin):
        gate = _sigmoid(gp_t)
        ppn = _rms_fwd_val(pp_t, g_pp)
        h4 = h3_t + gate * ppn
        err = _rms_fwd_val(h4, g_fin) - tgt_t
        loss = 0.5 * jnp.sum(jnp.mean(err * err, axis=-1, keepdims=True), axis=0, keepdims=True)
        dh4, dg_fin = _rms_bwd_val(err * (1.0 / d), h4, g_fin)
        dpp, dg_pp = _rms_bwd_val(dh4 * gate, pp_t, g_pp)
        dgp = dh4 * ppn * gate * (1.0 - gate)
        return dh4, dgp, dpp, jnp.broadcast_to(loss, (1, 128)), dg_fin, dg_pp

    dh4, dgp, dpp, loss_part, dg_fin, dg_pp = _rowwise(
        "loss_head", head_fn, [h3, gate_pre, pp, tgt], [sp["ple_proj_norm"], sp["final_norm"]],
        [(d, F32), (d, BF16), (d, BF16)], [((1, 128), F32), ((1, d), F32), ((1, d), F32)])
    gw, gs = {}, {"final_norm": dg_fin, "ple_proj_norm": dg_pp}
    gw["w_ple_gate"] = _mm_tn("d_w_pg", hn3, dgp, tm=1024, tn=1024)
    gw["w_ple_proj"] = _mm_tn("d_w_pp", p, dpp, tn=1024)
    dhn3 = _mm_nt("d_hn3", dgp, w_pg, tm=1024, tn=1024, tk=1024)

    def res_norm_bwd(dn_t, h_t, dres_t, g):
        dx, dg = _rms_bwd_val(dn_t, h_t, g)
        return dres_t + dx, dg

    dh3, gs["ple_gate_norm"] = _rowwise("ple_norm_bwd", res_norm_bwd, [dhn3, h3, dh4], [sp["ple_gate_norm"]],
                                        [(d, F32)], [((1, d), F32)])
    (dh2, gs["ffn2_norm"], gw["ffn2_w_gate"], gw["ffn2_w_up"], gw["ffn2_w_down"]), _ = _ffn_bwd(
        "ffn2", dh3, h2, sp["ffn2_norm"], xn2, g2, u2, wg2, wu2, wd2)
    dycat = _mm_nt("d_ycat", dh2, w_out, tm=1024, tn=1024, tk=1024)
    gw["w_out"] = jnp.concatenate([_mm_tn("d_w_out_m", y_m, dh2, tn=1024), _mm_tn("d_w_out_f", y_ft, dh2, tn=1024)], axis=0)
    doa, delta, gs["fox_out_norm"] = _fox_bwd_prep(dycat, o_f, sp["fox_out_norm"])
    dq_f, dk_f, dv_f, dct = _fox_bwd_post(*_fox_bwd2(qa, ka, va, doa, lse, delta))
    dfp = _fox_gate_bwd(zsr, bf_c.T, dct[:, :NH_F].T)
    dact, dv_m, do_m, dzs_m, dzr_m, gs["mlstm_out_norm"] = _mlstm_bwd(
        qk_act, zbig, zs, zsr, bm_c, bm_c.T, sp["mlstm_out_norm"], cst, mst, dycat)
    dqk, gw["conv_qk"] = _conv_bwd(zbig, dact, conv_w)
    dz_big = jnp.concatenate([dqk, dv_m, do_m, dq_f, dk_f, dv_f], axis=1)
    dzs = dzs_m + jnp.pad(jnp.concatenate([dzr_m, dfp], axis=0).T, ((0, 0), (0, 112)))
    dw_big = _mm_tn("d_w_big", dz_big, u, tn=1024)
    dw_small = _mm_tn("d_w_small", dzs, u, tn=1024)
    gw["w_in"] = jnp.concatenate([dw_big[0:1536], dw_small[0:8], dw_big[1536:3072], dw_small[8:16]], axis=0)
    du_a = _mm_nn("d_u_big", dz_big, w_big, tm=1024, tn=1024, tk=1024)
    du_b = _mm_nn("d_u_small", dzs, w_small, tm=1024, tn=1024)

    def mix_norm_bwd(da_t, db_t, h_t, dres_t, dzs_t, g):
        dx, dg = _rms_bwd_val(da_t + db_t, h_t, g)
        return dres_t + dx, dg, _colsum(dzs_t)

    dh1, gs["mix_norm"], dbias = _rowwise("mix_norm_bwd", mix_norm_bwd, [du_a, du_b, h1, dh2, dzs], [sp["mix_norm"]],
                                          [(d, F32)], [((1, d), F32), ((1, 128), F32)])
    gs["b_mlstm_gates"], gs["b_fox_f"] = dbias[:, 0:8], dbias[:, 8:16]
    conv_grad = gw.pop("conv_qk")
    gw["w_ple_proj"] = _chip_blocks(gw["w_ple_proj"])
    for n in ("w_in", "w_out", "w_ple_gate"):
        gw[n] = gw[n].reshape(4, -1, gw[n].shape[-1])
    part_rest = dict(zip(REST, _rs_partials(REST, gw, c_idx)))
    light = ("w_in", "w_out", "w_ple_gate", "w_ple_proj")
    part_ffn1 = []

    def own_plan(dwg, dwu, dwd):
        part_ffn1.extend(_rs_partials(FFN1, dict(zip(FFN1, (dwg, dwu, dwd))), c_idx))
        return _scatter_plan([pb for _, pb in part_ffn1])

    (grad_x, gs["ffn1_norm"], _, _, _), (l_light, l_down, l_gate, l_up, landed_ffn1) = _ffn_bwd_late_dx(
        "ffn1", dh1, x, sp["ffn1_norm"], xn1, g1, u1, wg1, wu1, wd1,
        _scatter_plan([part_rest[n][1] for n in light]),
        [_scatter_plan([part_rest[n][1]]) for n in ("ffn2_w_down", "ffn2_w_gate", "ffn2_w_up")], own_plan)
    landed_rest = dict(zip(light + ("ffn2_w_down", "ffn2_w_gate", "ffn2_w_up"), list(l_light) + [l_down[0], l_gate[0], l_up[0]]))
    names = REST + FFN1
    parts = [part_rest[n] for n in REST] + part_ffn1
    landed = [landed_rest[n] for n in REST] + list(landed_ffn1)
    mine = [_sum4("rs_sum_" + n, a, pf, place, SPLIT[n]) for n, a, (pf, _) in zip(names, landed, parts)]
    grads = dict(zip(names, _join_halves("rs_join", mine, [SPLIT[n] for n in names])))
    return loss_part, grad_x, grads, gs, conv_grad


ANY = pl.BlockSpec(memory_space=pl.ANY)
MESH = pl.DeviceIdType.MESH


def _place():
    x, y, c = lax.axis_index("x"), lax.axis_index("y"), lax.axis_index("c")
    chips = [(1 - x, y), (x, 1 - y), (1 - x, 1 - y)]
    return x, y, c, 2 * x + y, (x, y, 1 - c), chips


def _rcopy(src, dst, ssem, rsem, dev):
    return pltpu.make_async_remote_copy(src_ref=src, dst_ref=dst, send_sem=ssem, recv_sem=rsem, device_id=dev,
                                        device_id_type=MESH)


def _half(ref, lead, axis, idx, half):
    return ref.at[(slice(None),) * (lead + axis) + (pl.ds(idx * half, half),)]


def _to_slot(name, a, me_idx, dtype):
    r, cdim = a.shape
    tr = _pick(r, (256, 176, 128, 64))

    def body(me_ref, a_ref, o_ref):
        o_ref[...] = a_ref[...].astype(o_ref.dtype)

    return pl.pallas_call(
        body, name=name,
        grid_spec=pltpu.PrefetchScalarGridSpec(
            num_scalar_prefetch=1, grid=(r // tr,), in_specs=[pl.BlockSpec((tr, cdim), lambda i, me_ref: (i, 0))],
            out_specs=pl.BlockSpec((None, tr, cdim), lambda i, me_ref: (me_ref[0], i, 0))),
        out_shape=jax.ShapeDtypeStruct((4, r, cdim), dtype), compiler_params=_cparams(("parallel",)),
    )(me_idx, a)


def _gather4(name, bufs, split):
    return _run_plan(name, _gather_plan(bufs, split))


def _gather_plan(bufs, split):
    n = len(bufs)
    shapes = [b.shape[1:] for b in bufs]

    def ctx(outs):
        x, y, c, me, sib, chips = _place()

        def part(ref, a, which):
            if split[a] is None:
                return ref
            return _half(ref, 0, split[a], which, shapes[a][split[a]] // 2)

        return c, me, sib, chips, part

    def ici(outs, sems, a, j, chip, c, me, part):
        mine = part(outs[a].at[me], a, c)
        return _rcopy(mine, mine, sems[0].at[3 * a + j], sems[1].at[3 * a + j], (*chip, c))

    def fwd(outs, sems, a, j, chip, c, sib, part, which):
        blk = part(outs[a].at[2 * chip[0] + chip[1]], a, which)
        return _rcopy(blk, blk, sems[2].at[3 * a + j], sems[3].at[3 * a + j], sib)

    def start(ins, outs, sems):
        c, me, sib, chips, part = ctx(outs)
        for a in range(n):
            for j, chip in enumerate(chips):
                ici(outs, sems, a, j, chip, c, me, part).start()

    def mid(ins, outs, sems):
        c, me, sib, chips, part = ctx(outs)
        for j, chip in enumerate(chips):
            for a in range(n):
                blk = part(outs[a].at[2 * chip[0] + chip[1]], a, c)
                _rcopy(blk, blk, sems[0].at[3 * a + j], sems[1].at[3 * a + j], sib).wait_recv()
                if split[a] is not None:
                    fwd(outs, sems, a, j, chip, c, sib, part, c).start()

    def end(ins, outs, sems):
        c, me, sib, chips, part = ctx(outs)
        for j, chip in enumerate(chips):
            for a in range(n):
                if split[a] is not None:
                    fwd(outs, sems, a, j, chip, c, sib, part, 1 - c).wait_recv()
        for a in range(n):
            for j, chip in enumerate(chips):
                ici(outs, sems, a, j, chip, c, me, part).wait_send()
                if split[a] is not None:
                    fwd(outs, sems, a, j, chip, c, sib, part, c).wait_send()

    return dict(ins=list(bufs), outs=[jax.ShapeDtypeStruct(b.shape, b.dtype) for b in bufs], alias=True,
                sems=[pltpu.SemaphoreType.DMA((3 * n,))] * 4, phases=(start, mid, end))


def _run_plan(name, plan):
    ni, no = len(plan["ins"]), len(plan["outs"])

    def body(*refs):
        ins, outs, sems = refs[:ni], refs[ni:ni + no], refs[ni + no:]
        for phase in plan["phases"]:
            phase(ins, outs, sems)

    return pl.pallas_call(
        body, name=name, in_specs=[ANY] * ni, out_specs=[ANY] * no, out_shape=plan["outs"],
        input_output_aliases={a: a for a in range(ni)} if plan["alias"] else {}, scratch_shapes=plan["sems"],
    )(*plan["ins"])


class _Hosted:
    def __init__(self, plan, n_in, n_out):
        self.plan, self.n_in, self.n_out = plan, n_in, n_out
        self.ni, self.no, self.ns = (len(plan["ins"]) if plan else 0, len(plan["outs"]) if plan else 0,
                                     len(plan["sems"]) if plan else 0)

    def split(self, refs):
        a, b = self.n_in, self.n_in + self.ni
        c, d = b + self.n_out, b + self.n_out + self.no
        e = len(refs) - self.ns
        return refs[:a], refs[b:c], refs[d:e], (refs[a:b], refs[c:d], refs[e:])

    def run(self, k, cond, prefs):
        if self.plan is not None:
            @pl.when(cond)
            def _():
                self.plan["phases"][k](*prefs)

    def call_args(self):
        p = self.plan
        if p is None:
            return dict(in_specs=[], out_specs=[], out_shape=[], scratch=[], aliases={}, args=[])
        al = {self.n_in + a: self.n_out + a for a in range(self.ni)} if p["alias"] else {}
        return dict(in_specs=[ANY] * self.ni, out_specs=[ANY] * self.no, out_shape=list(p["outs"]), scratch=list(p["sems"]),
                    aliases=al, args=list(p["ins"]))


def _swap(name, arrs):
    n = len(arrs)

    def body(*refs):
        ins, outs = refs[:n], refs[n:2 * n]
        ssem, rsem = refs[2 * n:]
        x, y, c, me, sib, chips = _place()
        cps = [_rcopy(ins[a], outs[a], ssem.at[a], rsem.at[a], sib) for a in range(n)]
        for cp in cps:
            cp.start()
        for cp in cps:
            cp.wait()

    return pl.pallas_call(
        body, name=name, in_specs=[ANY] * n, out_specs=[ANY] * n,
        out_shape=[jax.ShapeDtypeStruct(a.shape, a.dtype) for a in arrs],
        scratch_shapes=[pltpu.SemaphoreType.DMA((n,))] * 2,
    )(*arrs)


def _scatter4(name, arrs):
    return _run_plan(name, _scatter_plan(arrs))


def _scatter_plan(arrs):
    n = len(arrs)

    def send(ins, outs, sems, a, j, chip, c, me):
        return _rcopy(ins[a].at[2 * chip[0] + chip[1]], outs[a].at[me], sems[0].at[3 * a + j], sems[1].at[3 * a + j], (*chip, c))

    def start(ins, outs, sems):
        x, y, c, me, sib, chips = _place()
        for a in range(n):
            for j, chip in enumerate(chips):
                send(ins, outs, sems, a, j, chip, c, me).start()

    def mid(ins, outs, sems):
        pass

    def end(ins, outs, sems):
        x, y, c, me, sib, chips = _place()
        for a in range(n):
            for j, chip in enumerate(chips):
                blk = outs[a].at[2 * chip[0] + chip[1]]
                _rcopy(blk, blk, sems[0].at[3 * a + j], sems[1].at[3 * a + j], sib).wait_recv()
        for a in range(n):
            for j, chip in enumerate(chips):
                send(ins, outs, sems, a, j, chip, c, me).wait_send()

    return dict(ins=list(arrs), outs=[jax.ShapeDtypeStruct(a.shape, a.dtype) for a in arrs], alias=False,
                sems=[pltpu.SemaphoreType.DMA((3 * n,))] * 2, phases=(start, mid, end))


def _join_halves(name, arrs, split):
    n = len(arrs)

    def body(*refs):
        outs = refs[n:2 * n]
        ssem, rsem = refs[2 * n:]
        x, y, c, me, sib, chips = _place()
        cps = []
        for a in range(n):
            mine = _half(outs[a], 0, split[a], c, arrs[a].shape[split[a]] // 2)
            cp = _rcopy(mine, mine, ssem.at[a], rsem.at[a], sib)
            cp.start()
            cps.append(cp)
        for a in range(n):
            blk = _half(outs[a], 0, split[a], 1 - c, arrs[a].shape[split[a]] // 2)
            _rcopy(blk, blk, ssem.at[a], rsem.at[a], sib).wait_recv()
        for cp in cps:
            cp.wait_send()

    return pl.pallas_call(
        body, name=name, in_specs=[ANY] * n, out_specs=[ANY] * n,
        out_shape=[jax.ShapeDtypeStruct(a.shape, a.dtype) for a in arrs],
        input_output_aliases={a: a for a in range(n)}, scratch_shapes=[pltpu.SemaphoreType.DMA((n,))] * 2,
    )(*arrs)


def _allreduce_small(s):
    r, cdim = s.shape

    def body(s_ref, o_ref, buf, ssem, rsem):
        x, y, c, me, sib, chips = _place()
        me8 = 4 * x + 2 * y + c
        buf[me8] = s_ref[...]
        flips = [(fx, fy, fc) for fx in (0, 1) for fy in (0, 1) for fc in (0, 1)][1:]
        cps = []
        for k, (fx, fy, fc) in enumerate(flips):
            peer = (x ^ fx if fx else x, y ^ fy if fy else y, c ^ fc if fc else c)
            cp = _rcopy(s_ref, buf.at[me8], ssem.at[k], rsem.at[k], peer)
            cp.start()
            cps.append(cp)
        for k, (fx, fy, fc) in enumerate(flips):
            src = 4 * (x ^ fx if fx else x) + 2 * (y ^ fy if fy else y) + (c ^ fc if fc else c)
            _rcopy(s_ref, buf.at[src], ssem.at[k], rsem.at[k], sib).wait_recv()
        for cp in cps:
            cp.wait_send()
        acc = buf[0]
        for k in range(1, 8):
            acc = acc + buf[k]
        o_ref[...] = acc

    vm = pl.BlockSpec(memory_space=pltpu.VMEM)
    return pl.pallas_call(
        body, name="allreduce_small", in_specs=[vm], out_specs=vm, out_shape=jax.ShapeDtypeStruct((r, cdim), F32),
        scratch_shapes=[pltpu.VMEM((8, r, cdim), F32), pltpu.SemaphoreType.DMA((7,)), pltpu.SemaphoreType.DMA((7,))],
    )(s)


def _add_my_half(name, g, recv, c_idx, axis):
    nb, hr, hc = recv.shape
    tr = _pick(hr, (256, 176, 128, 64))
    if axis == 0:
        g4 = g.reshape(nb, 2, hr, hc)
        gspec = pl.BlockSpec((None, None, tr, hc), lambda b, i, c_ref: (b, c_ref[0], i, 0))
    else:
        g4 = g
        gspec = pl.BlockSpec((None, tr, hc), lambda b, i, c_ref: (b, i, c_ref[0]))

    def body(c_ref, g_ref, r_ref, o_ref, ob_ref):
        s = g_ref[...] + r_ref[...].astype(F32)
        o_ref[...] = s
        ob_ref[...] = s.astype(BF16)

    ospec = pl.BlockSpec((None, tr, hc), lambda b, i, c_ref: (b, i, 0))
    return pl.pallas_call(
        body, name=name,
        grid_spec=pltpu.PrefetchScalarGridSpec(
            num_scalar_prefetch=1, grid=(nb, hr // tr), in_specs=[gspec, ospec], out_specs=[ospec, ospec]),
        out_shape=[jax.ShapeDtypeStruct((nb, hr, hc), F32), jax.ShapeDtypeStruct((nb, hr, hc), BF16)],
        compiler_params=_cparams(("parallel", "parallel")),
    )(c_idx, g4, recv)


def _sum4(name, landed, own, place, axis):
    nb, h, cdim = landed.shape
    tr = _pick(h, (256, 176, 128, 64))
    nt = h // tr

    def body(p_ref, a1_ref, a2_ref, a3_ref, own_ref, o_ref):
        o_ref[...] = ((own_ref[...] + a1_ref[...].astype(F32)) + a2_ref[...].astype(F32)) + a3_ref[...].astype(F32)

    def nxt(k):
        return pl.BlockSpec((None, tr, cdim), lambda i, p_ref: ((p_ref[0] + k) % nb, i, 0))

    if axis == 0:
        ospec = pl.BlockSpec((tr, cdim), lambda i, p_ref: (p_ref[1] * nt + i, 0))
        oshape = (2 * h, cdim)
    else:
        ospec = pl.BlockSpec((tr, cdim), lambda i, p_ref: (i, p_ref[1]))
        oshape = (h, 2 * cdim)
    return pl.pallas_call(
        body, name=name,
        grid_spec=pltpu.PrefetchScalarGridSpec(
            num_scalar_prefetch=1, grid=(nt,), in_specs=[nxt(1), nxt(2), nxt(3), nxt(0)], out_specs=ospec),
        out_shape=jax.ShapeDtypeStruct(oshape, F32), compiler_params=_cparams(("parallel",)),
    )(place, landed, landed, landed, own)


def _cast_other_half(name, g, c_idx, axis):
    nb, r, cdim = g.shape
    hr, hc = (r // 2, cdim) if axis == 0 else (r, cdim // 2)
    tr = _pick(hr, (256, 176, 128, 64))
    if axis == 0:
        g4 = g.reshape(nb, 2, hr, hc)
        gspec = pl.BlockSpec((None, None, tr, hc), lambda b, i, c_ref: (b, 1 - c_ref[0], i, 0))
    else:
        g4 = g
        gspec = pl.BlockSpec((None, tr, hc), lambda b, i, c_ref: (b, i, 1 - c_ref[0]))

    def body(c_ref, g_ref, o_ref):
        o_ref[...] = g_ref[...].astype(BF16)

    return pl.pallas_call(
        body, name=name,
        grid_spec=pltpu.PrefetchScalarGridSpec(
            num_scalar_prefetch=1, grid=(nb, hr // tr), in_specs=[gspec],
            out_specs=pl.BlockSpec((None, tr, hc), lambda b, i, c_ref: (b, i, 0))),
        out_shape=jax.ShapeDtypeStruct((nb, hr, hc), BF16), compiler_params=_cparams(("parallel", "parallel")),
    )(c_idx, g4)


def _adamw(name, w, g, m, v):
    c1 = 1.0 - ADAM_B1 ** ADAM_STEP
    c2 = 1.0 - ADAM_B2 ** ADAM_STEP

    def fn(w_t, g_t, m_t, v_t):
        m_n = ADAM_B1 * m_t + (1.0 - ADAM_B1) * g_t
        v_n = ADAM_B2 * v_t + (1.0 - ADAM_B2) * (g_t * g_t)
        delta = -ADAM_LR * ((m_n / c1) / (jnp.sqrt(v_n / c2) + ADAM_EPS) + ADAM_WD * w_t)
        return delta, m_n, v_n

    cdim = w.shape[1]
    return _rowwise(name, fn, [w, g, m, v], [], [(cdim, F32)] * 3, tm=_pick(w.shape[0], (256, 176, 128, 64, 8)))


BIG = ("ffn1_w_gate", "ffn1_w_up", "ffn1_w_down", "w_in", "w_out", "ffn2_w_gate", "ffn2_w_up", "ffn2_w_down",
       "w_ple_gate", "w_ple_proj")
SMALL = ("ffn1_norm", "mix_norm", "b_mlstm_gates", "b_fox_f", "mlstm_out_norm", "fox_out_norm", "ffn2_norm",
         "ple_gate_norm", "ple_proj_norm", "final_norm")
WEIGHTS = ("ffn1_norm", "ffn1_w_gate", "ffn1_w_up", "ffn1_w_down", "mix_norm", "w_in", "conv_qk", "b_mlstm_gates",
           "b_fox_f", "mlstm_out_norm", "fox_out_norm", "w_out", "ffn2_norm", "ffn2_w_gate", "ffn2_w_up", "ffn2_w_down",
           "ple_gate_norm", "w_ple_gate", "w_ple_proj", "ple_proj_norm", "final_norm")
TRANSPOSED = ("ffn1_w_gate", "ffn1_w_up", "w_in", "ffn2_w_gate", "ffn2_w_up")
PACK_W = 1024


def _chip_blocks(a):
    r, c4 = a.shape
    return a.reshape(r, 4, c4 // 4).transpose(1, 0, 2)


def _from_chip_blocks(a):
    nb, r, c = a.shape
    return a.transpose(1, 0, 2).reshape(r, nb * c)


def kernel(x, p, ffn1_norm, ffn1_w_gate, ffn1_w_up, ffn1_w_down, mix_norm, w_in, conv_qk, b_mlstm_gates, b_fox_f, mlstm_out_norm, fox_out_norm, w_out, ffn2_norm, ffn2_w_gate, ffn2_w_up, ffn2_w_down, ple_gate_norm, w_ple_gate, w_ple_proj, ple_proj_norm, final_norm, loss_target, m_ffn1_norm, m_ffn1_w_gate, m_ffn1_w_up, m_ffn1_w_down, m_mix_norm, m_w_in, m_conv_qk, m_b_mlstm_gates, m_b_fox_f, m_mlstm_out_norm, m_fox_out_norm, m_w_out, m_ffn2_norm, m_ffn2_w_gate, m_ffn2_w_up, m_ffn2_w_down, m_ple_gate_norm, m_w_ple_gate, m_w_ple_proj, m_ple_proj_norm, m_final_norm, v_ffn1_norm, v_ffn1_w_gate, v_ffn1_w_up, v_ffn1_w_down, v_mix_norm, v_w_in, v_conv_qk, v_b_mlstm_gates, v_b_fox_f, v_mlstm_out_norm, v_fox_out_norm, v_w_out, v_ffn2_norm, v_ffn2_w_gate, v_ffn2_w_up, v_ffn2_w_down, v_ple_gate_norm, v_w_ple_gate, v_w_ple_proj, v_ple_proj_norm, v_final_norm):
    w = dict(ffn1_norm=ffn1_norm, ffn1_w_gate=ffn1_w_gate, ffn1_w_up=ffn1_w_up, ffn1_w_down=ffn1_w_down, mix_norm=mix_norm,
             w_in=w_in, conv_qk=conv_qk, b_mlstm_gates=b_mlstm_gates, b_fox_f=b_fox_f, mlstm_out_norm=mlstm_out_norm,
             fox_out_norm=fox_out_norm, w_out=w_out, ffn2_norm=ffn2_norm, ffn2_w_gate=ffn2_w_gate, ffn2_w_up=ffn2_w_up,
             ffn2_w_down=ffn2_w_down, ple_gate_norm=ple_gate_norm, w_ple_gate=w_ple_gate, w_ple_proj=w_ple_proj,
             ple_proj_norm=ple_proj_norm, final_norm=final_norm)
    m = dict(ffn1_norm=m_ffn1_norm, ffn1_w_gate=m_ffn1_w_gate, ffn1_w_up=m_ffn1_w_up, ffn1_w_down=m_ffn1_w_down,
             mix_norm=m_mix_norm, w_in=m_w_in, conv_qk=m_conv_qk, b_mlstm_gates=m_b_mlstm_gates, b_fox_f=m_b_fox_f,
             mlstm_out_norm=m_mlstm_out_norm, fox_out_norm=m_fox_out_norm, w_out=m_w_out, ffn2_norm=m_ffn2_norm,
             ffn2_w_gate=m_ffn2_w_gate, ffn2_w_up=m_ffn2_w_up, ffn2_w_down=m_ffn2_w_down, ple_gate_norm=m_ple_gate_norm,
             w_ple_gate=m_w_ple_gate, w_ple_proj=m_w_ple_proj, ple_proj_norm=m_ple_proj_norm, final_norm=m_final_norm)
    v = dict(ffn1_norm=v_ffn1_norm, ffn1_w_gate=v_ffn1_w_gate, ffn1_w_up=v_ffn1_w_up, ffn1_w_down=v_ffn1_w_down,
             mix_norm=v_mix_norm, w_in=v_w_in, conv_qk=v_conv_qk, b_mlstm_gates=v_b_mlstm_gates, b_fox_f=v_b_fox_f,
             mlstm_out_norm=v_mlstm_out_norm, fox_out_norm=v_fox_out_norm, w_out=v_w_out, ffn2_norm=v_ffn2_norm,
             ffn2_w_gate=v_ffn2_w_gate, ffn2_w_up=v_ffn2_w_up, ffn2_w_down=v_ffn2_w_down, ple_gate_norm=v_ple_gate_norm,
             w_ple_gate=v_w_ple_gate, w_ple_proj=v_w_ple_proj, ple_proj_norm=v_ple_proj_norm, final_norm=v_final_norm)
    shapes = {n: w[n].shape for n in WEIGHTS}

    def view(a, n):
        return a[0].T if n in TRANSPOSED else a.reshape(-1, a.shape[-1])

    def unview(a, n):
        return (a.T if n in TRANSPOSED else a).reshape(shapes[n])

    w2, m2, v2 = ({n: view(a, n) for n, a in d.items()} for d in (w, m, v))

    c_idx = lax.axis_index("c").astype(jnp.int32).reshape(1)
    me_idx = (2 * lax.axis_index("x") + lax.axis_index("y")).astype(jnp.int32).reshape(1)
    place = jnp.concatenate([me_idx, c_idx])
    slot = {n: _to_slot("slot_" + n, w2[n], me_idx, BF16) for n in BIG}
    slot["conv_qk"] = _to_slot("slot_conv_qk", w2["conv_qk"], me_idx, F32)
    wg1, wu1, wd1 = _gather4("gather_ffn1", [slot[n] for n in FFN1], [SPLIT[n] for n in FFN1])
    sp = {n: w2[n] for n in SMALL}
    loss_part, grad_x, grads, gs, conv_grad = _local_step(
        x[0], p[0, 0], loss_target[0], sp, wg1, wu1, wd1, [slot[n] for n in REST + ("conv_qk",)], c_idx, place)
    loss = lax.psum(loss_part[0, 0], ("x", "y", "c"))

    small = [gs[n].reshape(1, -1) for n in SMALL] + [conv_grad]
    rows = [jnp.pad(a, ((0, 0), (0, PACK_W - a.shape[1]))) for a in small]
    packed = jnp.concatenate(rows, axis=0)
    packed = jnp.pad(packed, ((0, -packed.shape[0] % 8), (0, 0)))
    red = _allreduce_small(packed)
    for i, n in enumerate(SMALL):
        grads[n] = red[i:i + 1, :gs[n].size]
    dconv = red[len(SMALL):len(SMALL) + CONV_W, :conv_grad.shape[1]]
    cw = conv_qk.shape[-1]
    grads["conv_qk"] = lax.dynamic_slice_in_dim(dconv, (2 * lax.axis_index("x") + lax.axis_index("y")) * cw, cw, axis=1)

    outs = {}
    for n in WEIGHTS:
        g2 = grads[n].reshape(w2[n].shape)
        d, nm, nv = _adamw("adamw_" + n, w2[n], g2, m2[n], v2[n])
        outs[n] = tuple(unview(a, n) for a in (g2, d, nm, nv))
    return (loss, grad_x[None], *[outs[n][0] for n in WEIGHTS], *[outs[n][1] for n in WEIGHTS],
            *[outs[n][2] for n in WEIGHTS], *[outs[n][3] for n in WEIGHTS])
```

```python
import functools
import math

import jax
import jax.numpy as jnp
from jax import lax
from jax.experimental import pallas as pl
from jax.experimental.pallas import tpu as pltpu

F32 = jnp.float32
BF16 = jnp.bfloat16
EPS = 1e-6
NH_M, DK_M, DV_M = 4, 64, 128
NH_F, DH_F = 8, 64
CONV_W = 4
ADAM_LR, ADAM_B1, ADAM_B2, ADAM_EPS, ADAM_WD, ADAM_STEP = 0.001, 0.9, 0.999, 1e-08, 0.01, 10
VMEM_LIMIT = 56 * 1024 * 1024


def _cparams(sem):
    return pltpu.CompilerParams(dimension_semantics=sem, vmem_limit_bytes=VMEM_LIMIT)


def _sigmoid(x):
    return 1.0 / (1.0 + jnp.exp(-x))


def _dot(a, b, ca, cb):
    return lax.dot_general(a.astype(BF16), b.astype(BF16), (((ca,), (cb,)), ((), ())), preferred_element_type=F32)


def _rowwise(name, fn, tiled, full, outs, accs=(), tm=256):
    rows = tiled[0].shape[0]
    tm = min(tm, rows)
    assert rows % tm == 0
    n_t, n_f, n_o, n_a = len(tiled), len(full), len(outs), len(accs)

    def body(*refs):
        ins = [r[...] for r in refs[: n_t + n_f]]
        res = fn(*ins)
        if not isinstance(res, (tuple, list)):
            res = (res,)
        orefs = refs[n_t + n_f:]
        for r, v in zip(orefs[:n_o], res[:n_o]):
            r[...] = v.astype(r.dtype)
        if n_a:
            @pl.when(pl.program_id(0) == 0)
            def _():
                for r in orefs[n_o:]:
                    r[...] = jnp.zeros_like(r)
            for r, v in zip(orefs[n_o:], res[n_o:]):
                r[...] += v.astype(r.dtype)

    in_specs = [pl.BlockSpec((tm, a.shape[1]), lambda i: (i, 0)) for a in tiled]
    in_specs += [pl.BlockSpec(a.shape, lambda i: (0, 0)) for a in full]
    out_specs = [pl.BlockSpec((tm, c), lambda i: (i, 0)) for c, _ in outs]
    out_specs += [pl.BlockSpec(s, lambda i: (0, 0)) for s, _ in accs]
    out_shape = [jax.ShapeDtypeStruct((rows, c), d) for c, d in outs]
    out_shape += [jax.ShapeDtypeStruct(s, d) for s, d in accs]
    res = pl.pallas_call(
        body, name=name, grid=(rows // tm,), in_specs=in_specs, out_specs=out_specs, out_shape=out_shape,
        compiler_params=_cparams(("arbitrary",) if n_a else ("parallel",)),
    )(*tiled, *full)
    return res


def _colsum(v):
    return jnp.sum(v, axis=0, keepdims=True)


def _rms_fwd_val(x, g):
    r = lax.rsqrt(jnp.mean(x * x, axis=-1, keepdims=True) + EPS)
    return x * r * g


def _rms_bwd_val(dy, x, g):
    r = lax.rsqrt(jnp.mean(x * x, axis=-1, keepdims=True) + EPS)
    xh = x * r
    dxh = dy * g
    dx = r * (dxh - xh * jnp.mean(dxh * xh, axis=-1, keepdims=True))
    return dx, _colsum(dy * xh)


def _mm(name, pairs, out_shape, out_block, out_map, grid, kaxis, ta=False, tb=False, scale=None, res=None,
        out_dtype=F32, plan=None):
    nk = grid[kaxis]
    npairs = len(pairs)
    ca, cb = (0 if ta else 1), (1 if tb else 0)
    acc_shape = tuple(d for d in out_block if d is not None)
    n_in = 2 * npairs + (1 if res is not None else 0)
    host = _Hosted(plan, n_in, 1)

    def body(*refs):
        ins, (o_ref,), (acc_ref,), prefs = host.split(refs)
        in_refs = ins[: 2 * npairs]
        res_ref = ins[2 * npairs] if res is not None else None
        k = pl.program_id(kaxis)
        ids = [pl.program_id(a) for a in range(len(grid))]
        first, last = ids[0] == 0, ids[0] == grid[0] - 1
        for a in range(1, len(grid)):
            first, last = first & (ids[a] == 0), last & (ids[a] == grid[a] - 1)
        host.run(0, first, prefs)
        host.run(1, first, prefs)

        @pl.when(k == 0)
        def _():
            acc_ref[...] = jnp.zeros_like(acc_ref)

        part = None
        for p in range(npairs):
            d = _dot(in_refs[2 * p][...], in_refs[2 * p + 1][...], ca, cb)
            part = d if part is None else part + d
        acc_ref[...] += part

        @pl.when(k == nk - 1)
        def _():
            v = acc_ref[...]
            if scale is not None:
                v = v * scale
            if res_ref is not None:
                v = v + res_ref[...].astype(F32)
            o_ref[...] = v.astype(o_ref.dtype)

        host.run(2, last, prefs)

    in_specs, args = [], []
    for a, ab, am, b, bb, bm in pairs:
        in_specs += [pl.BlockSpec(ab, am), pl.BlockSpec(bb, bm)]
        args += [a, b]
    if res is not None:
        in_specs.append(pl.BlockSpec(out_block, out_map))
        args.append(res)
    sem = tuple("arbitrary" if (i == kaxis or plan is not None) else "parallel" for i in range(len(grid)))
    hc = host.call_args()
    out = pl.pallas_call(
        body, name=name, grid=grid, in_specs=in_specs + hc["in_specs"],
        out_specs=[pl.BlockSpec(out_block, out_map)] + hc["out_specs"],
        out_shape=[jax.ShapeDtypeStruct(out_shape, out_dtype)] + hc["out_shape"],
        scratch_shapes=[pltpu.VMEM(acc_shape, F32)] + hc["scratch"], input_output_aliases=hc["aliases"],
        compiler_params=_cparams(sem),
    )(*args, *hc["args"])
    return out[0] if plan is None else (out[0], out[1:])


def _pick(n, pref):
    for t in pref:
        if n % t == 0:
            return t
    return n


def _mm_nn(name, a, b, tm=512, tn=512, tk=512, **kw):
    (m, k), n = a.shape, b.shape[1]
    tm, tn, tk = _pick(m, (tm, 256, 128)), _pick(n, (tn, 256, 128)), _pick(k, (tk, 256, 128))
    return _mm(name, [(a, (tm, tk), lambda i, j, kk: (i, kk), b, (tk, tn), lambda i, j, kk: (kk, j))],
               (m, n), (tm, tn), lambda i, j, kk: (i, j), (m // tm, n // tn, k // tk), 2, **kw)


def _mm_nt(name, a, b, tm=512, tn=512, tk=512, **kw):
    (m, k), n = a.shape, b.shape[0]
    tm, tn, tk = _pick(m, (tm, 256, 128)), _pick(n, (tn, 256, 128)), _pick(k, (tk, 256, 128))
    return _mm(name, [(a, (tm, tk), lambda i, j, kk: (i, kk), b, (tn, tk), lambda i, j, kk: (j, kk))],
               (m, n), (tm, tn), lambda i, j, kk: (i, j), (m // tm, n // tn, k // tk), 2, tb=True, **kw)


def _mm_tn(name, a, b, tm=512, tn=512, tk=2048, **kw):
    (k, m), n = a.shape, b.shape[1]
    tm, tn, tk = _pick(m, (tm, 256, 128)), _pick(n, (tn, 256, 128)), _pick(k, (tk, 1024, 512, 256, 128))
    return _mm(name, [(a, (tk, tm), lambda i, j, kk: (kk, i), b, (tk, tn), lambda i, j, kk: (kk, j))],
               (m, n), (tm, tn), lambda i, j, kk: (i, j), (m // tm, n // tn, k // tk), 2, ta=True, **kw)


def _norm_mm(name, h, gamma, w, w_transposed, out_dtype):
    t, d = h.shape
    n = w.shape[0] if w_transposed else w.shape[1]
    tm, tn = _pick(t, (512, 256)), _pick(n, (1024, 512, 256, 128))

    def body(h_ref, gam_ref, w_ref, xn_ref, o_ref, xn_scr):
        @pl.when(pl.program_id(1) == 0)
        def _():
            xn = _rms_fwd_val(h_ref[...], gam_ref[...]).astype(BF16)
            xn_scr[...] = xn
            xn_ref[...] = xn

        o_ref[...] = _dot(xn_scr[...], w_ref[...], 1, 1 if w_transposed else 0).astype(o_ref.dtype)

    wspec = pl.BlockSpec((tn, d), lambda i, j: (j, 0)) if w_transposed else pl.BlockSpec((d, tn), lambda i, j: (0, j))
    return pl.pallas_call(
        body, name=name, grid=(t // tm, n // tn),
        in_specs=[pl.BlockSpec((tm, d), lambda i, j: (i, 0)), pl.BlockSpec((1, d), lambda i, j: (0, 0)), wspec],
        out_specs=[pl.BlockSpec((tm, d), lambda i, j: (i, 0)), pl.BlockSpec((tm, tn), lambda i, j: (i, j))],
        out_shape=[jax.ShapeDtypeStruct((t, d), BF16), jax.ShapeDtypeStruct((t, n), out_dtype)],
        scratch_shapes=[pltpu.VMEM((tm, d), BF16)], compiler_params=_cparams(("parallel", "arbitrary")),
    )(h, gamma, w)


def _ffn_fwd(pfx, h, gamma, wg, wu, wd, plan=None):
    t, d = h.shape
    nb, f, _ = wg.shape
    tm = _pick(t, (1024, 512, 256))
    nt = t // tm
    host = _Hosted(plan, 5, 4)

    def body(*refs):
        (h_ref, gam_ref, wg_ref, wu_ref, wd_ref), (ho_ref, xn_ref, g_ref, u_ref), (xn_scr, acc_ref), prefs = host.split(refs)
        i, j = pl.program_id(0), pl.program_id(1)
        host.run(0, (i == 0) & (j == 0), prefs)
        host.run(1, (i == nt // 2) & (j == 0), prefs)

        @pl.when(j == 0)
        def _():
            xn = _rms_fwd_val(h_ref[...], gam_ref[...]).astype(BF16)
            xn_scr[...] = xn
            xn_ref[...] = xn
            acc_ref[...] = jnp.zeros_like(acc_ref)

        x = xn_scr[...]
        g = _dot(x, wg_ref[...], 1, 1)
        u = _dot(x, wu_ref[...], 1, 1)
        g_ref[...] = g.astype(BF16)
        u_ref[...] = u.astype(BF16)
        acc_ref[...] += _dot(g * _sigmoid(g) * u, wd_ref[...], 1, 0)

        @pl.when(j == nb - 1)
        def _():
            ho_ref[...] = h_ref[...] + 0.5 * acc_ref[...]

        host.run(2, (i == nt - 1) & (j == nb - 1), prefs)

    row = pl.BlockSpec((tm, d), lambda i, j: (i, 0))
    blk = pl.BlockSpec((None, tm, f), lambda i, j: (j, i, 0))
    wspec = pl.BlockSpec((None, f, d), lambda i, j: (j, 0, 0))
    hc = host.call_args()
    res = pl.pallas_call(
        body, name=pfx + "_fwd", grid=(nt, nb),
        in_specs=[row, pl.BlockSpec((1, d), lambda i, j: (0, 0)), wspec, wspec, wspec] + hc["in_specs"],
        out_specs=[row, row, blk, blk] + hc["out_specs"],
        out_shape=[jax.ShapeDtypeStruct((t, d), F32), jax.ShapeDtypeStruct((t, d), BF16),
                   jax.ShapeDtypeStruct((nb, t, f), BF16), jax.ShapeDtypeStruct((nb, t, f), BF16)] + hc["out_shape"],
        scratch_shapes=[pltpu.VMEM((tm, d), BF16), pltpu.VMEM((tm, d), F32)] + hc["scratch"],
        input_output_aliases=hc["aliases"], compiler_params=_cparams(("arbitrary", "arbitrary")),
    )(h, gamma, wg, wu, wd, *hc["args"])
    return res[:4], res[4:]


def _ffn_bwd(pfx, dh_out, h, gamma, xn, g_all, u_all, wg, wu, wd, plan=None):
    t, d = h.shape
    nb, f, _ = wg.shape
    tm = _pick(t, (512, 256))
    tk = _pick(t, (2048, 1024, 512, 256))

    nt = t // tm
    host = _Hosted(plan, 8, 5)

    def body(*refs):
        ((dy_ref, h_ref, gam_ref, wg_ref, wu_ref, wd_ref, g_ref, u_ref), (dh_ref, dgam_ref, dg_ref, du_ref, a_ref),
         (acc_ref,), prefs) = host.split(refs)
        i, j = pl.program_id(0), pl.program_id(1)
        host.run(0, (i == 0) & (j == 0), prefs)
        host.run(1, (i == nt // 2) & (j == 0), prefs)

        @pl.when((i == 0) & (j == 0))
        def _():
            dgam_ref[...] = jnp.zeros_like(dgam_ref)

        @pl.when(j == 0)
        def _():
            acc_ref[...] = jnp.zeros_like(acc_ref)

        da = _dot(dy_ref[...], wd_ref[...], 1, 1) * 0.5
        g = g_ref[...].astype(F32)
        u = u_ref[...].astype(F32)
        s = _sigmoid(g)
        sl = g * s
        du = (da * sl).astype(BF16)
        dg = (da * u * (s * (1.0 + g * (1.0 - s)))).astype(BF16)
        du_ref[...] = du
        dg_ref[...] = dg
        a_ref[...] = (sl * u).astype(BF16)
        acc_ref[...] += _dot(dg, wg_ref[...], 1, 0) + _dot(du, wu_ref[...], 1, 0)

        @pl.when(j == nb - 1)
        def _():
            dx, dgam = _rms_bwd_val(acc_ref[...], h_ref[...], gam_ref[...])
            dh_ref[...] = dy_ref[...] + dx
            dgam_ref[...] += dgam

        host.run(2, (i == nt - 1) & (j == nb - 1), prefs)

    row = pl.BlockSpec((tm, d), lambda i, j: (i, 0))
    vec = pl.BlockSpec((1, d), lambda i, j: (0, 0))
    blk = pl.BlockSpec((None, tm, f), lambda i, j: (j, i, 0))
    wspec = pl.BlockSpec((None, f, d), lambda i, j: (j, 0, 0))
    hc = host.call_args()
    res = pl.pallas_call(
        body, name=pfx + "_bwd", grid=(nt, nb),
        in_specs=[row, row, vec, wspec, wspec, wspec, blk, blk] + hc["in_specs"],
        out_specs=[row, vec, blk, blk, blk] + hc["out_specs"],
        out_shape=[jax.ShapeDtypeStruct((t, d), F32), jax.ShapeDtypeStruct((1, d), F32)]
        + [jax.ShapeDtypeStruct((nb, t, f), BF16)] * 3 + hc["out_shape"],
        scratch_shapes=[pltpu.VMEM((tm, d), F32)] + hc["scratch"], input_output_aliases=hc["aliases"],
        compiler_params=_cparams(("arbitrary", "arbitrary")),
    )(dh_out, h, gamma, wg, wu, wd, g_all, u_all, *hc["args"])
    dh, dgamma, dg_all, du_all, a_all = res[:5]

    xmap, bmap, omap = (lambda b, k: (k, 0)), (lambda b, k: (b, k, 0)), (lambda b, k: (b, 0, 0))
    dwg = _mm(pfx + "_dwg", [(dg_all, (None, tk, f), bmap, xn, (tk, d), xmap)], (nb, f, d), (None, f, d), omap,
              (nb, t // tk), 1, ta=True)
    dwu = _mm(pfx + "_dwu", [(du_all, (None, tk, f), bmap, xn, (tk, d), xmap)], (nb, f, d), (None, f, d), omap,
              (nb, t // tk), 1, ta=True)
    dwd = _mm(pfx + "_dwd", [(a_all, (None, tk, f), bmap, dh_out, (tk, d), xmap)], (nb, f, d), (None, f, d), omap,
              (nb, t // tk), 1, ta=True, scale=0.5)
    return (dh, dgamma, dwg, dwu, dwd), res[5:]


def _ffn_bwd_late_dx(pfx, dh_out, h, gamma, xn, g_all, u_all, wg, wu, wd, plan_gu, plans_dw, make_plan_dx):
    t, d = h.shape
    nb, f, _ = wg.shape
    tm = _pick(t, (512, 256))
    tk = _pick(t, (2048, 1024, 512, 256))
    nt = t // tm
    host_a = _Hosted(plan_gu, 4, 3)

    def body_a(*refs):
        (dy_ref, wd_ref, g_ref, u_ref), (dg_ref, du_ref, a_ref), _, prefs = host_a.split(refs)
        i, j = pl.program_id(0), pl.program_id(1)
        host_a.run(0, (i == 0) & (j == 0), prefs)
        host_a.run(1, (i == 0) & (j == 0), prefs)
        da = _dot(dy_ref[...], wd_ref[...], 1, 1) * 0.5
        g = g_ref[...].astype(F32)
        u = u_ref[...].astype(F32)
        s = _sigmoid(g)
        sl = g * s
        du_ref[...] = (da * sl).astype(BF16)
        dg_ref[...] = (da * u * (s * (1.0 + g * (1.0 - s)))).astype(BF16)
        a_ref[...] = (sl * u).astype(BF16)
        host_a.run(2, (i == nt - 1) & (j == nb - 1), prefs)

    row = pl.BlockSpec((tm, d), lambda i, j: (i, 0))
    vec = pl.BlockSpec((1, d), lambda i, j: (0, 0))
    blk = pl.BlockSpec((None, tm, f), lambda i, j: (j, i, 0))
    wspec = pl.BlockSpec((None, f, d), lambda i, j: (j, 0, 0))
    hc = host_a.call_args()
    res_a = pl.pallas_call(
        body_a, name=pfx + "_bwd_gu", grid=(nt, nb), in_specs=[row, wspec, blk, blk] + hc["in_specs"],
        out_specs=[blk] * 3 + hc["out_specs"], out_shape=[jax.ShapeDtypeStruct((nb, t, f), BF16)] * 3 + hc["out_shape"],
        scratch_shapes=hc["scratch"], input_output_aliases=hc["aliases"], compiler_params=_cparams(("arbitrary", "arbitrary")),
    )(dh_out, wd, g_all, u_all, *hc["args"])
    dg_all, du_all, a_all = res_a[:3]

    xmap, bmap, omap = (lambda b, k: (k, 0)), (lambda b, k: (b, k, 0)), (lambda b, k: (b, 0, 0))
    dwd, out_d = _mm(pfx + "_dwd", [(a_all, (None, tk, f), bmap, dh_out, (tk, d), xmap)], (nb, f, d), (None, f, d), omap,
                     (nb, t // tk), 1, ta=True, scale=0.5, plan=plans_dw[0])
    dwg, out_g = _mm(pfx + "_dwg", [(dg_all, (None, tk, f), bmap, xn, (tk, d), xmap)], (nb, f, d), (None, f, d), omap,
                     (nb, t // tk), 1, ta=True, plan=plans_dw[1])
    dwu, out_u = _mm(pfx + "_dwu", [(du_all, (None, tk, f), bmap, xn, (tk, d), xmap)], (nb, f, d), (None, f, d), omap,
                     (nb, t // tk), 1, ta=True, plan=plans_dw[2])

    plan_dx = make_plan_dx(dwg, dwu, dwd)
    host_b = _Hosted(plan_dx, 7, 2)

    def body_b(*refs):
        (dy_ref, h_ref, gam_ref, wg_ref, wu_ref, dg_ref, du_ref), (dh_ref, dgam_ref), (acc_ref,), prefs = host_b.split(refs)
        i, j = pl.program_id(0), pl.program_id(1)
        host_b.run(0, (i == 0) & (j == 0), prefs)
        host_b.run(1, (i == 0) & (j == 0), prefs)

        @pl.when((i == 0) & (j == 0))
        def _():
            dgam_ref[...] = jnp.zeros_like(dgam_ref)

        @pl.when(j == 0)
        def _():
            acc_ref[...] = jnp.zeros_like(acc_ref)

        acc_ref[...] += _dot(dg_ref[...], wg_ref[...], 1, 0) + _dot(du_ref[...], wu_ref[...], 1, 0)

        @pl.when(j == nb - 1)
        def _():
            dx, dgam = _rms_bwd_val(acc_ref[...], h_ref[...], gam_ref[...])
            dh_ref[...] = dy_ref[...] + dx
            dgam_ref[...] += dgam

        host_b.run(2, (i == nt - 1) & (j == nb - 1), prefs)

    hc = host_b.call_args()
    res_b = pl.pallas_call(
        body_b, name=pfx + "_bwd_dx", grid=(nt, nb), in_specs=[row, row, vec, wspec, wspec, blk, blk] + hc["in_specs"],
        out_specs=[row, vec] + hc["out_specs"],
        out_shape=[jax.ShapeDtypeStruct((t, d), F32), jax.ShapeDtypeStruct((1, d), F32)] + hc["out_shape"],
        scratch_shapes=[pltpu.VMEM((tm, d), F32)] + hc["scratch"], input_output_aliases=hc["aliases"],
        compiler_params=_cparams(("arbitrary", "arbitrary")),
    )(dh_out, h, gamma, wg, wu, dg_all, du_all, *hc["args"])
    return (res_b[0], res_b[1], dwg, dwu, dwd), (res_a[3:], out_d, out_g, out_u, res_b[2:])


HALO = 16


def _silu_grad(y):
    s = _sigmoid(y)
    return s * (1.0 + y * (1.0 - s))


def _with_halo(ref, i, n_tiles, tm, before, after):
    t = ref.shape[0]
    r0 = pl.multiple_of(i * tm, tm)
    parts = [ref[pl.ds(r0, tm), :].astype(F32)]
    if before:
        prev = ref[pl.ds(pl.multiple_of(jnp.maximum(r0 - HALO, 0), HALO), HALO), :].astype(F32)
        parts.insert(0, jnp.where(i > 0, prev, 0.0))
    if after:
        nxt = ref[pl.ds(pl.multiple_of(jnp.minimum(r0 + tm, t - HALO), HALO), HALO), :].astype(F32)
        parts.append(jnp.where(i < n_tiles - 1, nxt, 0.0))
    return jnp.concatenate(parts, axis=0)


def _conv_fwd(zbig, w):
    t, c = zbig.shape[0], w.shape[1]
    tm = _pick(t, (512, 256))
    nt = t // tm

    def body(x_ref, w_ref, o_ref):
        xe = _with_halo(x_ref, pl.program_id(0), nt, tm, True, False)
        wv = w_ref[...]
        y = xe * wv[3:4, :]
        for i in range(CONV_W - 1):
            y = y + pltpu.roll(xe, CONV_W - 1 - i, 0) * wv[i:i + 1, :]
        y = y[HALO:, :]
        o_ref[...] = (y * _sigmoid(y)).astype(o_ref.dtype)

    return pl.pallas_call(
        body, name="conv_fwd", grid=(nt,),
        in_specs=[pl.BlockSpec((t, c), lambda i: (0, 0)), pl.BlockSpec(w.shape, lambda i: (0, 0))],
        out_specs=pl.BlockSpec((tm, c), lambda i: (i, 0)), out_shape=jax.ShapeDtypeStruct((t, c), BF16),
        compiler_params=_cparams(("parallel",)),
    )(zbig, w)


def _conv_bwd(zbig, dact, w):
    t, c = dact.shape
    tm = _pick(t, (512, 256))
    nt = t // tm
    n = tm + HALO

    def body(x_ref, d_ref, w_ref, dx_ref, dw_ref):
        xe = _with_halo(x_ref, pl.program_id(0), nt, tm, True, True)
        de = _with_halo(d_ref, pl.program_id(0), nt, tm, False, True)
        wv = w_ref[...]
        sh = [pltpu.roll(xe, CONV_W - 1 - i, 0)[HALO:, :] if i < CONV_W - 1 else xe[HALO:, :] for i in range(CONV_W)]
        y = sh[0] * wv[0:1, :]
        for i in range(1, CONV_W):
            y = y + sh[i] * wv[i:i + 1, :]
        dy = de * _silu_grad(y)
        dx = dy * wv[3:4, :]
        for i in range(CONV_W - 1):
            dx = dx + pltpu.roll(dy, n - (CONV_W - 1 - i), 0) * wv[i:i + 1, :]
        dx_ref[...] = dx[:tm, :].astype(dx_ref.dtype)
        dyc = dy[:tm, :]
        dwp = jnp.concatenate([_colsum(dyc * sh[i][:tm, :]) for i in range(CONV_W)], axis=0)

        @pl.when(pl.program_id(0) == 0)
        def _():
            dw_ref[...] = jnp.zeros_like(dw_ref)
        dw_ref[...] += dwp

    return pl.pallas_call(
        body, name="conv_bwd", grid=(nt,),
        in_specs=[pl.BlockSpec((t, c), lambda i: (0, 0)), pl.BlockSpec((t, c), lambda i: (0, 0)),
                  pl.BlockSpec(w.shape, lambda i: (0, 0))],
        out_specs=[pl.BlockSpec((tm, c), lambda i: (i, 0)), pl.BlockSpec(w.shape, lambda i: (0, 0))],
        out_shape=[jax.ShapeDtypeStruct((t, c), BF16), jax.ShapeDtypeStruct(w.shape, F32)],
        compiler_params=_cparams(("arbitrary",)),
    )(zbig, dact, w)


LM = 256
HI = lax.Precision.HIGHEST


def _logsig(x):
    return jnp.minimum(x, 0.0) - jnp.log(1.0 + jnp.exp(-jnp.abs(x)))


def _tri(n, lower):
    r = lax.broadcasted_iota(jnp.int32, (n, n), 0)
    c = lax.broadcasted_iota(jnp.int32, (n, n), 1)
    return (r >= c) if lower else (r <= c)


def _f32dot(a, b):
    return lax.dot_general(a, b, (((1,), (0,)), ((), ())), precision=HI, preferred_element_type=F32)


def _mlstm_chunk(h, q_ref, k_ref, v_ref, zs_ref, zsr_ref, bc_ref, br_ref, c_prev, m_prev):
    l = LM
    q = q_ref[:, h * DK_M:(h + 1) * DK_M].astype(F32) * (DK_M ** -0.5)
    k = k_ref[:, h * DK_M:(h + 1) * DK_M]
    v = v_ref[:, h * DV_M:(h + 1) * DV_M]
    lane = lax.broadcasted_iota(jnp.int32, (l, DV_M), 1)
    v1 = jnp.concatenate([v, (lane == 0).astype(v.dtype)], axis=1)
    zs, zsr = zs_ref[...], zsr_ref[...]
    li_c = zs[:, h:h + 1] + bc_ref[:, h:h + 1]
    fp_c = zs[:, NH_M + h:NH_M + h + 1] + bc_ref[:, NH_M + h:NH_M + h + 1]
    li_r = zsr[h:h + 1, :] + br_ref[h:h + 1, :]
    fp_r = zsr[NH_M + h:NH_M + h + 1, :] + br_ref[NH_M + h:NH_M + h + 1, :]
    lf_c, lf_r = _logsig(fp_c), _logsig(fp_r)
    low = _tri(l, True)
    b_c = _f32dot(low.astype(F32), lf_c)
    b_r = _f32dot(lf_r, _tri(l, False).astype(F32))
    g = b_r[:, l - 1:l]
    dmat = jnp.where(low, b_c - b_r + li_r, -jnp.inf)
    inter = b_c + m_prev
    m_t = jnp.maximum(inter, jnp.max(dmat, axis=1, keepdims=True))
    w_inter = jnp.exp(inter - m_t)
    amat = jnp.exp(dmat - m_t)
    s = _dot(q, k, 1, 1)
    p = amat * s
    qc = _dot(q, c_prev, 1, 0)
    qc_w = w_inter * qc
    num1 = qc_w + _dot(p, v1, 1, 0)
    den = num1[:, DV_M:DV_M + 1]
    mx = jnp.maximum(jnp.abs(den), jnp.exp(-m_t))
    hh = num1[:, :DV_M] / mx
    a_c = g - b_c + li_c
    return dict(q=q, k=k, v1=v1, fp_c=fp_c, fp_r=fp_r, b_c=b_c, g=g, m_t=m_t, w_inter=w_inter, amat=amat, s=s, p=p,
                qc_w=qc_w, den=den, mx=mx, hh=hh, a_c=a_c)


def _mlstm_fwd(qk, zbig, zs, zsr, bc, br, gm):
    t = zs.shape[0]
    l = LM
    nc = t // l
    dm = NH_M * DV_M

    def body(q_ref, k_ref, v_ref, o_ref, zs_ref, zsr_ref, bc_ref, br_ref, gm_ref, y_ref, cst_ref, mst_ref, c_scr, m_scr):
        @pl.when(pl.program_id(0) == 0)
        def _():
            c_scr[...] = jnp.zeros_like(c_scr)
            m_scr[...] = jnp.zeros_like(m_scr)

        cst_ref[...] = c_scr[...]
        mst_ref[...] = m_scr[...]
        ys = []
        for h in range(NH_M):
            c_prev = c_scr[h]
            m_prev = m_scr[h:h + 1, 0:1]
            r = _mlstm_chunk(h, q_ref, k_ref, v_ref, zs_ref, zsr_ref, bc_ref, br_ref, c_prev, m_prev)
            hh = r["hh"]
            gh = gm_ref[:, h * DV_M:(h + 1) * DV_M]
            hn = hh * lax.rsqrt(jnp.mean(hh * hh, axis=-1, keepdims=True) + EPS) * gh
            og = o_ref[:, h * DV_M:(h + 1) * DV_M].astype(F32)
            ys.append(hn * _sigmoid(og))
            m_new = jnp.maximum(r["g"] + m_prev, jnp.max(r["a_c"], axis=0, keepdims=True))
            decay = jnp.exp(r["g"] + m_prev - m_new)
            wk = r["k"].astype(F32) * jnp.exp(r["a_c"] - m_new)
            c_scr[h] = decay * c_prev + _dot(wk, r["v1"], 0, 0)
            m_scr[h:h + 1, :] = jnp.broadcast_to(m_new, (1, 128))
        y_ref[...] = jnp.concatenate(ys, axis=1).astype(y_ref.dtype)

    return pl.pallas_call(
        body, name="mlstm_fwd", grid=(nc,),
        in_specs=[pl.BlockSpec((l, NH_M * DK_M), lambda i: (i, 0)), pl.BlockSpec((l, NH_M * DK_M), lambda i: (i, 1)),
                  pl.BlockSpec((l, dm), lambda i: (i, 1)), pl.BlockSpec((l, dm), lambda i: (i, 2)),
                  pl.BlockSpec((l, 128), lambda i: (i, 0)), pl.BlockSpec((8, l), lambda i: (0, i)),
                  pl.BlockSpec((1, 8), lambda i: (0, 0)), pl.BlockSpec((8, 1), lambda i: (0, 0)),
                  pl.BlockSpec((1, dm), lambda i: (0, 0))],
        out_specs=[pl.BlockSpec((l, dm), lambda i: (i, 0)), pl.BlockSpec((None, NH_M, DK_M, 2 * DV_M), lambda i: (i, 0, 0, 0)),
                   pl.BlockSpec((None, 8, 128), lambda i: (i, 0, 0))],
        out_shape=[jax.ShapeDtypeStruct((t, dm), BF16), jax.ShapeDtypeStruct((nc, NH_M, DK_M, 2 * DV_M), F32),
                   jax.ShapeDtypeStruct((nc, 8, 128), F32)],
        scratch_shapes=[pltpu.VMEM((NH_M, DK_M, 2 * DV_M), F32), pltpu.VMEM((8, 128), F32)],
        compiler_params=_cparams(("arbitrary",)),
    )(qk, qk, zbig, zbig, zs, zsr, bc, br, gm)


def _mlstm_bwd(qk, zbig, zs, zsr, bc, br, gm, cst, mst, dycat):
    t = zs.shape[0]
    l = LM
    nc = t // l
    dm = NH_M * DV_M

    def body(q_ref, k_ref, v_ref, o_ref, zs_ref, zsr_ref, bc_ref, br_ref, gm_ref, cst_ref, mst_ref, cnx_ref, mnx_ref,
             dy_ref, dqk_ref, dv_ref, do_ref, dzs_ref, dzr_ref, dgm_ref, dc_scr):
        @pl.when(pl.program_id(0) == 0)
        def _():
            dc_scr[...] = jnp.zeros_like(dc_scr)
            dgm_ref[...] = jnp.zeros_like(dgm_ref)

        lane = lax.broadcasted_iota(jnp.int32, (l, 128), 1)
        upper = _tri(l, False).astype(F32)
        lower = _tri(l, True).astype(F32)
        dzr_rows = [None] * 8
        dvs, dos, dgs, dqs, dks = [], [], [], [], []
        dzs = jnp.zeros((l, 128), F32)
        for h in range(NH_M):
            c_prev = cst_ref[h]
            m_prev = mst_ref[h:h + 1, 0:1]
            r = _mlstm_chunk(h, q_ref, k_ref, v_ref, zs_ref, zsr_ref, bc_ref, br_ref, c_prev, m_prev)
            hh, mx, den, m_t, v1, amat = r["hh"], r["mx"], r["den"], r["m_t"], r["v1"], r["amat"]
            gh = gm_ref[:, h * DV_M:(h + 1) * DV_M]
            rs = lax.rsqrt(jnp.mean(hh * hh, axis=-1, keepdims=True) + EPS)
            xh = hh * rs
            sg = _sigmoid(o_ref[:, h * DV_M:(h + 1) * DV_M].astype(F32))
            dyh = dy_ref[:, h * DV_M:(h + 1) * DV_M]
            dos.append(dyh * xh * gh * sg * (1.0 - sg))
            dhn = dyh * sg
            dgs.append(_colsum(dhn * xh))
            dxh = dhn * gh
            dh = rs * (dxh - xh * jnp.mean(dxh * xh, axis=-1, keepdims=True))
            g1 = dh / mx
            hd = jnp.sum(hh * dh, axis=-1, keepdims=True)
            dden = jnp.where(jnp.abs(den) > jnp.exp(-m_t), -hd / mx * jnp.sign(den), 0.0)
            g256 = jnp.concatenate([g1, jnp.where(lane == 0, dden, 0.0)], axis=1)
            dc_h = dc_scr[h]
            ea = jnp.exp(r["a_c"])
            dp = _dot(g256, v1, 1, 1)
            ds = dp * amat
            dqs.append((r["w_inter"] * _dot(g256, c_prev, 1, 1) + _dot(ds, r["k"], 1, 0)) * (DK_M ** -0.5))
            dks.append(_dot(ds, r["q"], 0, 0) + ea * _dot(v1, dc_h, 1, 1))
            dv_st = ea * _dot(r["k"], dc_h, 1, 0)
            dv1 = _dot(r["p"], g256, 0, 0) + dv_st
            dvs.append(dv1[:, :DV_M])
            wmat = dp * r["p"]
            c_in = _colsum(wmat)
            c_st = jnp.sum(v1.astype(F32) * dv_st, axis=-1, keepdims=True)
            r_t = jnp.sum(wmat, axis=1, keepdims=True) + jnp.sum(g256 * r["qc_w"], axis=-1, keepdims=True)
            db = r_t - c_st
            carry = jnp.exp(mnx_ref[h:h + 1, 0:1]) * jnp.sum(
                jnp.sum(dc_h * cnx_ref[h], axis=1, keepdims=True), axis=0, keepdims=True)
            dlf_c = _f32dot(upper, db) + carry
            dlf_r = -_f32dot(c_in, lower)
            dfp = dlf_c * _sigmoid(-r["fp_c"])
            dzs = dzs + jnp.where(lane == h, c_st, 0.0) + jnp.where(lane == NH_M + h, dfp, 0.0)
            dzr_rows[h] = c_in
            dzr_rows[NH_M + h] = dlf_r * _sigmoid(-r["fp_r"])
            wq = r["q"] * jnp.exp(r["b_c"] - m_t)
            dc_scr[h] = jnp.exp(r["g"]) * dc_h + _dot(wq, g256, 0, 0)
        dqk_ref[...] = jnp.concatenate(dqs + dks, axis=1)
        dv_ref[...] = jnp.concatenate(dvs, axis=1).astype(dv_ref.dtype)
        do_ref[...] = jnp.concatenate(dos, axis=1).astype(do_ref.dtype)
        dzs_ref[...] = dzs
        dzr_ref[...] = jnp.concatenate(dzr_rows, axis=0)
        dgm_ref[...] += jnp.concatenate(dgs, axis=1)

    rev = lambda i: nc - 1 - i
    nxt = lambda i: jnp.minimum(nc - i, nc - 1)
    return pl.pallas_call(
        body, name="mlstm_bwd", grid=(nc,),
        in_specs=[pl.BlockSpec((l, NH_M * DK_M), lambda i: (rev(i), 0)), pl.BlockSpec((l, NH_M * DK_M), lambda i: (rev(i), 1)),
                  pl.BlockSpec((l, dm), lambda i: (rev(i), 1)), pl.BlockSpec((l, dm), lambda i: (rev(i), 2)),
                  pl.BlockSpec((l, 128), lambda i: (rev(i), 0)), pl.BlockSpec((8, l), lambda i: (0, rev(i))),
                  pl.BlockSpec((1, 8), lambda i: (0, 0)), pl.BlockSpec((8, 1), lambda i: (0, 0)),
                  pl.BlockSpec((1, dm), lambda i: (0, 0)),
                  pl.BlockSpec((None, NH_M, DK_M, 2 * DV_M), lambda i: (rev(i), 0, 0, 0)),
                  pl.BlockSpec((None, 8, 128), lambda i: (rev(i), 0, 0)),
                  pl.BlockSpec((None, NH_M, DK_M, 2 * DV_M), lambda i: (nxt(i), 0, 0, 0)),
                  pl.BlockSpec((None, 8, 128), lambda i: (nxt(i), 0, 0)),
                  pl.BlockSpec((l, dm), lambda i: (rev(i), 0))],
        out_specs=[pl.BlockSpec((l, dm), lambda i: (rev(i), 0)),
                   pl.BlockSpec((l, dm), lambda i: (rev(i), 0)), pl.BlockSpec((l, dm), lambda i: (rev(i), 0)),
                   pl.BlockSpec((l, 128), lambda i: (rev(i), 0)), pl.BlockSpec((8, l), lambda i: (0, rev(i))),
                   pl.BlockSpec((1, dm), lambda i: (0, 0))],
        out_shape=[jax.ShapeDtypeStruct((t, dm), F32),
                   jax.ShapeDtypeStruct((t, dm), BF16), jax.ShapeDtypeStruct((t, dm), BF16),
                   jax.ShapeDtypeStruct((t, 128), F32), jax.ShapeDtypeStruct((8, t), F32),
                   jax.ShapeDtypeStruct((1, dm), F32)],
        scratch_shapes=[pltpu.VMEM((NH_M, DK_M, 2 * DV_M), F32)],
        compiler_params=_cparams(("arbitrary",)),
    )(qk, qk, zbig, zbig, zs, zsr, bc, br, gm, cst, mst, cst, mst, dycat)


def _fox_cumsum(zsr, bf_r):
    t = zsr.shape[1]
    cw = _pick(t, (512, 256))

    def body(z_ref, b_ref, c_ref):
        up = _tri(cw, False).astype(F32)
        carry = jnp.zeros((NH_F, 1), F32)
        for j in range(t // cw):
            cs = _f32dot(_logsig(z_ref[:, j * cw:(j + 1) * cw] + b_ref[...]), up) + carry
            c_ref[:, j * cw:(j + 1) * cw] = cs
            carry = cs[:, cw - 1:cw]

    return pl.pallas_call(
        body, name="fox_cumsum", grid=(1,),
        in_specs=[pl.BlockSpec((NH_F, t), lambda i: (1, 0)), pl.BlockSpec((NH_F, 1), lambda i: (0, 0))],
        out_specs=pl.BlockSpec((NH_F, t), lambda i: (0, 0)), out_shape=jax.ShapeDtypeStruct((NH_F, t), F32),
        compiler_params=_cparams(("arbitrary",)),
    )(zsr, bf_r)


def _fox_gate_bwd(zsr, bf_r, dc):
    t = zsr.shape[1]
    cw = _pick(t, (512, 256))

    def body(z_ref, b_ref, dc_ref, o_ref):
        low = _tri(cw, True).astype(F32)
        carry = jnp.zeros((NH_F, 1), F32)
        for j in reversed(range(t // cw)):
            sl = slice(j * cw, (j + 1) * cw)
            dlf = _f32dot(dc_ref[:, sl], low) + carry
            o_ref[:, sl] = dlf * _sigmoid(-(z_ref[:, sl] + b_ref[...]))
            carry = dlf[:, 0:1]

    return pl.pallas_call(
        body, name="fox_gate_bwd", grid=(1,),
        in_specs=[pl.BlockSpec((NH_F, t), lambda i: (1, 0)), pl.BlockSpec((NH_F, 1), lambda i: (0, 0)),
                  pl.BlockSpec((NH_F, t), lambda i: (0, 0))],
        out_specs=pl.BlockSpec((NH_F, t), lambda i: (0, 0)), out_shape=jax.ShapeDtypeStruct((NH_F, t), F32),
        compiler_params=_cparams(("arbitrary",)),
    )(zsr, bf_r, dc)


def _causal_mask(n):
    return _tri(n, True)


def _fox_fwd(q, k, v, c_col, c_row, gf):
    nh, t, dh = q.shape
    tq = _pick(t, (512, 256))
    scale = dh ** -0.5

    def body(q_ref, k_ref, v_ref, cc_ref, cr_ref, g_ref, o_ref, lse_ref, y_ref):
        i = pl.program_id(1)
        qv = q_ref[...]
        cq = cc_ref[...]

        def blk(j, carry, masked):
            m, l, acc = carry
            k0 = pl.multiple_of(j * tq, tq)
            kb = k_ref[pl.ds(k0, tq), :]
            vb = v_ref[pl.ds(k0, tq), :]
            s = _dot(qv, kb, 1, 1) * scale + cq - cr_ref[:, pl.ds(k0, tq)]
            if masked:
                s = jnp.where(_causal_mask(tq), s, -jnp.inf)
            m_new = jnp.maximum(m, jnp.max(s, axis=1, keepdims=True))
            alpha = jnp.exp(m - m_new)
            p = jnp.exp(s - m_new)
            return m_new, alpha * l + jnp.sum(p, axis=1, keepdims=True), alpha * acc + _dot(p, vb, 1, 0)

        init = (jnp.full((tq, 1), -jnp.inf, F32), jnp.zeros((tq, 1), F32), jnp.zeros((tq, dh), F32))
        carry = lax.fori_loop(0, i, lambda j, c: blk(j, c, False), init)
        m, l, acc = blk(i, carry, True)
        o = acc / l
        o_ref[...] = o
        lse_ref[...] = m + jnp.log(l)
        y_ref[...] = (o * lax.rsqrt(jnp.mean(o * o, axis=-1, keepdims=True) + EPS) * g_ref[...]).astype(y_ref.dtype)

    full = lambda w: pl.BlockSpec((None, t, w), lambda h, i: (h, 0, 0))
    tile = lambda w: pl.BlockSpec((None, tq, w), lambda h, i: (h, i, 0))
    return pl.pallas_call(
        body, name="fox_fwd", grid=(nh, t // tq),
        in_specs=[tile(dh), full(dh), full(dh), tile(1), pl.BlockSpec((None, 1, t), lambda h, i: (h, 0, 0)),
                  pl.BlockSpec((None, 1, dh), lambda h, i: (h, 0, 0))],
        out_specs=[tile(dh), tile(1), tile(dh)],
        out_shape=[jax.ShapeDtypeStruct((nh, t, dh), F32), jax.ShapeDtypeStruct((nh, t, 1), F32),
                   jax.ShapeDtypeStruct((nh, t, dh), BF16)],
        compiler_params=_cparams(("parallel", "parallel")),
    )(q, k, v, c_col, c_row, gf)


def _fox_norm_bwd(dy, o, gf):
    nh, t, dh = o.shape
    tm = _pick(t, (512, 256))

    def body(dy_ref, o_ref, g_ref, do_ref, dl_ref, dg_ref):
        ov = o_ref[...]
        dx, dg = _rms_bwd_val(dy_ref[...], ov, g_ref[...])
        do_ref[...] = dx
        dl_ref[...] = jnp.sum(dx * ov, axis=-1, keepdims=True)

        @pl.when(pl.program_id(1) == 0)
        def _():
            dg_ref[...] = jnp.zeros_like(dg_ref)
        dg_ref[...] += dg

    tile = lambda w: pl.BlockSpec((None, tm, w), lambda h, i: (h, i, 0))
    gspec = pl.BlockSpec((None, 1, dh), lambda h, i: (h, 0, 0))
    return pl.pallas_call(
        body, name="fox_norm_bwd", grid=(nh, t // tm), in_specs=[tile(dh), tile(dh), gspec],
        out_specs=[tile(dh), tile(1), gspec],
        out_shape=[jax.ShapeDtypeStruct((nh, t, dh), F32), jax.ShapeDtypeStruct((nh, t, 1), F32),
                   jax.ShapeDtypeStruct((nh, 1, dh), F32)],
        compiler_params=_cparams(("parallel", "arbitrary")),
    )(dy, o, gf)


def _fox_bwd(q, k, v, c_col, c_row, do, lse, delta):
    nh, t, dh = q.shape
    tq = _pick(t, (512, 256))
    nq = t // tq
    scale = dh ** -0.5

    def body(q_ref, k_ref, v_ref, cc_ref, cr_ref, do_ref, lse_ref, dl_ref, dq_ref, dk_ref, dv_ref, dc_ref, dcq_ref):
        j = pl.program_id(1)

        @pl.when(j == 0)
        def _():
            dq_ref[...] = jnp.zeros_like(dq_ref)
            dcq_ref[...] = jnp.zeros_like(dcq_ref)

        kb, vb, crb = k_ref[...], v_ref[...], cr_ref[...]

        def blk(i, carry, masked):
            dk, dv, dc = carry
            rows = pl.ds(pl.multiple_of(i * tq, tq), tq)
            qb = q_ref[rows, :]
            dob = do_ref[rows, :].astype(BF16)
            s = _dot(qb, kb, 1, 1) * scale + cc_ref[rows, :] - crb
            if masked:
                s = jnp.where(_causal_mask(tq), s, -jnp.inf)
            p = jnp.exp(s - lse_ref[rows, :])
            dv = dv + _dot(p, dob, 0, 0)
            ds = p * (_dot(dob, vb, 1, 1) - dl_ref[rows, :])
            dc = dc + _colsum(ds)
            dk = dk + _dot(ds, qb, 0, 0) * scale
            dq_ref[rows, :] += _dot(ds, kb, 1, 0) * scale
            dcq_ref[rows, :] += jnp.sum(ds, axis=1, keepdims=True)
            return dk, dv, dc

        init = (jnp.zeros((tq, dh), F32), jnp.zeros((tq, dh), F32), jnp.zeros((1, tq), F32))
        carry = blk(j, init, True)
        dk, dv, dc = lax.fori_loop(j + 1, nq, lambda i, c: blk(i, c, False), carry)
        dk_ref[...] = dk
        dv_ref[...] = dv
        dc_ref[...] = -dc

    full = lambda w: pl.BlockSpec((None, t, w), lambda h, j: (h, 0, 0))
    tile = lambda w: pl.BlockSpec((None, tq, w), lambda h, j: (h, j, 0))
    crow = pl.BlockSpec((None, 1, tq), lambda h, j: (h, 0, j))
    return pl.pallas_call(
        body, name="fox_bwd", grid=(nh, nq),
        in_specs=[full(dh), tile(dh), tile(dh), full(1), crow, full(dh), full(1), full(1)],
        out_specs=[full(dh), tile(dh), tile(dh), crow, full(1)],
        out_shape=[jax.ShapeDtypeStruct((nh, t, dh), F32)] * 3 + [jax.ShapeDtypeStruct((nh, 1, t), F32),
                                                                jax.ShapeDtypeStruct((nh, t, 1), F32)],
        compiler_params=_cparams(("parallel", "arbitrary")),
    )(q, k, v, c_col, c_row, do, lse, delta)


AUG = 64


def _split3(c):
    hi = c.astype(BF16).astype(F32)
    r1 = c - hi
    mid = r1.astype(BF16).astype(F32)
    return hi, mid, r1 - mid


def _fox_prep(zbig, ct):
    t = zbig.shape[0]
    tm = _pick(t, (512, 256))

    def body(q_ref, k_ref, v_ref, c_ref, qo_ref, ko_ref, vo_ref):
        lane = lax.broadcasted_iota(jnp.int32, (tm, AUG), 1)
        qv, kv, vv, cv = q_ref[...], k_ref[...], v_ref[...], c_ref[...]
        one = (lane == 0).astype(BF16)
        for h in range(NH_F):
            hi, mid, lo = _split3(cv[:, h:h + 1])
            aq = jnp.where(lane == 0, hi, jnp.where(lane == 1, mid, jnp.where(lane == 2, lo, jnp.where(lane < 6, 1.0, 0.0))))
            ak = jnp.where(lane < 3, 1.0, jnp.where(lane == 3, -hi, jnp.where(lane == 4, -mid, jnp.where(lane == 5, -lo, 0.0))))
            sl = slice(h * DH_F, (h + 1) * DH_F)
            qo_ref[h] = jnp.concatenate([qv[:, sl] * (DH_F ** -0.5), aq.astype(BF16)], axis=1).astype(BF16)
            ko_ref[h] = jnp.concatenate([kv[:, sl], ak.astype(BF16)], axis=1)
            vo_ref[h] = jnp.concatenate([vv[:, sl], one], axis=1)

    ospec = pl.BlockSpec((NH_F, tm, 128), lambda i: (0, i, 0))
    return pl.pallas_call(
        body, name="fox_prep", grid=(t // tm,),
        in_specs=[pl.BlockSpec((tm, 512), lambda i: (i, 3)), pl.BlockSpec((tm, 512), lambda i: (i, 4)),
                  pl.BlockSpec((tm, 512), lambda i: (i, 5)), pl.BlockSpec((tm, NH_F), lambda i: (i, 0))],
        out_specs=[ospec] * 3, out_shape=[jax.ShapeDtypeStruct((NH_F, t, 128), BF16)] * 3,
        compiler_params=_cparams(("parallel",)),
    )(zbig, zbig, zbig, ct)


def _fox_fwd2(qa, ka, va, gf, plan=None):
    nh, t, _ = qa.shape
    tq = _pick(t, (512, 256))
    nq = t // tq
    group = 2
    host = _Hosted(plan, 4, 3)

    def body(*refs):
        (q_ref, k_ref, v_ref, g_ref), (y_ref, o_ref, lse_ref), _, prefs = host.split(refs)
        i = pl.program_id(0)
        host.run(0, i == 0, prefs)
        host.run(1, i == max(nq - 2, 0), prefs)
        lane = lax.broadcasted_iota(jnp.int32, (tq, 128), 1)
        ys, os_ = [], []
        lse_all = jnp.zeros((tq, 128), F32)
        for h0 in range(0, nh, group):
            heads = range(h0, h0 + group)
            qvs = [q_ref[h] for h in heads]

            def blk(j, carry, masked, heads=heads, qvs=qvs):
                k0 = pl.multiple_of(j * tq, tq)
                out = []
                for (m, acc), h, qv in zip(carry, heads, qvs):
                    s = lax.dot_general(qv, k_ref[h, pl.ds(k0, tq), :], (((1,), (1,)), ((), ())), preferred_element_type=F32)
                    if masked:
                        s = jnp.where(_causal_mask(tq), s, -jnp.inf)
                    m_new = jnp.maximum(m, jnp.max(s, axis=1, keepdims=True))
                    p = jnp.exp(s - m_new).astype(BF16)
                    pv = lax.dot_general(p, v_ref[h, pl.ds(k0, tq), :], (((1,), (0,)), ((), ())), preferred_element_type=F32)
                    out.append((m_new, jnp.exp(m - m_new) * acc + pv))
                return tuple(out)

            init = tuple((jnp.full((tq, 1), -jnp.inf, F32), jnp.zeros((tq, 128), F32)) for _ in heads)
            carry = lax.fori_loop(0, i, lambda j, c: blk(j, c, False), init)
            for (m, acc), h in zip(blk(i, carry, True), heads):
                l = acc[:, DH_F:DH_F + 1]
                o = acc[:, :DH_F] / l
                os_.append(o)
                gh = g_ref[:, h * DH_F:(h + 1) * DH_F]
                ys.append(o * lax.rsqrt(jnp.mean(o * o, axis=-1, keepdims=True) + EPS) * gh)
                lse_all = lse_all + jnp.where(lane == h, m + jnp.log(l), 0.0)
        y_ref[...] = jnp.concatenate(ys, axis=1).astype(y_ref.dtype)
        o_ref[...] = jnp.concatenate(os_, axis=1)
        lse_ref[...] = lse_all
        host.run(2, i == nq - 1, prefs)

    full = pl.BlockSpec((nh, t, 128), lambda i: (0, 0, 0))
    hc = host.call_args()
    res = pl.pallas_call(
        body, name="fox_fwd", grid=(nq,),
        in_specs=[pl.BlockSpec((nh, tq, 128), lambda i: (0, i, 0)), full, full, pl.BlockSpec((1, nh * DH_F), lambda i: (0, 0))]
        + hc["in_specs"],
        out_specs=[pl.BlockSpec((tq, nh * DH_F), lambda i: (i, 0)), pl.BlockSpec((tq, nh * DH_F), lambda i: (i, 0)),
                   pl.BlockSpec((tq, 128), lambda i: (i, 0))] + hc["out_specs"],
        out_shape=[jax.ShapeDtypeStruct((t, nh * DH_F), BF16), jax.ShapeDtypeStruct((t, nh * DH_F), F32),
                   jax.ShapeDtypeStruct((t, 128), F32)] + hc["out_shape"],
        scratch_shapes=hc["scratch"], input_output_aliases=hc["aliases"], compiler_params=_cparams(("arbitrary",)),
    )(qa, ka, va, gf, *hc["args"])
    return res[:3], res[3:]


def _fox_bwd_prep(dycat, o, gf):
    t = o.shape[0]
    tm = _pick(t, (512, 256))

    def body(dy_ref, o_ref, g_ref, do_ref, dl_ref, dg_ref):
        lane = lax.broadcasted_iota(jnp.int32, (tm, 128), 1)
        dyv, ov, gv = dy_ref[...], o_ref[...], g_ref[...]
        dgs = []
        dl = jnp.zeros((tm, 128), F32)
        pad = jnp.zeros((tm, AUG), BF16)
        for h in range(NH_F):
            sl = slice(h * DH_F, (h + 1) * DH_F)
            dx, dg = _rms_bwd_val(dyv[:, sl], ov[:, sl], gv[:, sl])
            dgs.append(dg)
            do_ref[h] = jnp.concatenate([dx.astype(BF16), pad], axis=1)
            dl = dl + jnp.where(lane == h, jnp.sum(dx * ov[:, sl], axis=-1, keepdims=True), 0.0)
        dl_ref[...] = dl

        @pl.when(pl.program_id(0) == 0)
        def _():
            dg_ref[...] = jnp.zeros_like(dg_ref)
        dg_ref[...] += jnp.concatenate(dgs, axis=1)

    return pl.pallas_call(
        body, name="fox_bwd_prep", grid=(t // tm,),
        in_specs=[pl.BlockSpec((tm, 512), lambda i: (i, 1)), pl.BlockSpec((tm, 512), lambda i: (i, 0)),
                  pl.BlockSpec((1, 512), lambda i: (0, 0))],
        out_specs=[pl.BlockSpec((NH_F, tm, 128), lambda i: (0, i, 0)), pl.BlockSpec((tm, 128), lambda i: (i, 0)),
                   pl.BlockSpec((1, 512), lambda i: (0, 0))],
        out_shape=[jax.ShapeDtypeStruct((NH_F, t, 128), BF16), jax.ShapeDtypeStruct((t, 128), F32),
                   jax.ShapeDtypeStruct((1, 512), F32)],
        compiler_params=_cparams(("arbitrary",)),
    )(dycat, o, gf)


def _fox_bwd2(qa, ka, va, doa, lse, delta):
    nh, t, _ = qa.shape
    tq = _pick(t, (512, 256))
    nq = t // tq

    group = 2

    def body(q_ref, k_ref, v_ref, do_ref, lse_ref, dl_ref, dq_ref, dk_ref, dv_ref):
        hp, j = pl.program_id(0), pl.program_id(1)

        @pl.when(j == 0)
        def _():
            dq_ref[...] = jnp.zeros_like(dq_ref)

        lane = lax.broadcasted_iota(jnp.int32, (tq, 128), 1)

        def blk(i, carry, masked):
            rows = pl.ds(pl.multiple_of(i * tq, tq), tq)
            lse_t, dl_t = lse_ref[rows, :], dl_ref[rows, :]
            out = []
            for g, (dk, dv) in enumerate(carry):
                h = hp * group + g
                kb, vb = k_ref[g], v_ref[g]
                qb, dob = q_ref[g, rows, :], do_ref[g, rows, :]
                lse_h = jnp.sum(jnp.where(lane == h, lse_t, 0.0), axis=1, keepdims=True)
                dl_h = jnp.sum(jnp.where(lane == h, dl_t, 0.0), axis=1, keepdims=True)
                s = lax.dot_general(qb, kb, (((1,), (1,)), ((), ())), preferred_element_type=F32)
                if masked:
                    s = jnp.where(_causal_mask(tq), s, -jnp.inf)
                p = jnp.exp(s - lse_h)
                dp = lax.dot_general(dob, vb, (((1,), (1,)), ((), ())), preferred_element_type=F32)
                ds = (p * (dp - dl_h)).astype(BF16)
                dv = dv + lax.dot_general(p.astype(BF16), dob, (((0,), (0,)), ((), ())), preferred_element_type=F32)
                dk = dk + lax.dot_general(ds, qb, (((0,), (0,)), ((), ())), preferred_element_type=F32)
                dq_ref[g, rows, :] += lax.dot_general(ds, kb, (((1,), (0,)), ((), ())), preferred_element_type=F32)
                out.append((dk, dv))
            return tuple(out)

        init = tuple((jnp.zeros((tq, 128), F32), jnp.zeros((tq, 128), F32)) for _ in range(group))
        carry = blk(j, init, True)
        carry = lax.fori_loop(j + 1, nq, lambda i, c: blk(i, c, False), carry)
        for g, (dk, dv) in enumerate(carry):
            dk_ref[g] = dk
            dv_ref[g] = dv

    full = pl.BlockSpec((group, t, 128), lambda h, j: (h, 0, 0))
    tile = pl.BlockSpec((group, tq, 128), lambda h, j: (h, j, 0))
    cols = pl.BlockSpec((t, 128), lambda h, j: (0, 0))
    return pl.pallas_call(
        body, name="fox_bwd", grid=(nh // group, nq), in_specs=[full, tile, tile, full, cols, cols],
        out_specs=[full, tile, tile], out_shape=[jax.ShapeDtypeStruct((nh, t, 128), F32)] * 3,
        compiler_params=_cparams(("parallel", "arbitrary")),
    )(qa, ka, va, doa, lse, delta)


def _fox_bwd_post(dqa, dka, dva):
    nh, t, _ = dqa.shape
    tm = _pick(t, (512, 256))

    def body(dq_ref, dk_ref, dv_ref, oq_ref, ok_ref, ov_ref, dc_ref):
        lane = lax.broadcasted_iota(jnp.int32, (tm, 128), 1)
        dc = jnp.zeros((tm, 128), F32)
        qs, ks, vs = [], [], []
        for h in range(nh):
            dq, dk = dq_ref[h], dk_ref[h]
            qs.append(dq[:, :DH_F] * (DH_F ** -0.5))
            ks.append(dk[:, :DH_F])
            vs.append(dv_ref[h][:, :DH_F])
            dc = dc + jnp.where(lane == h, dq[:, DH_F:DH_F + 1] - dk[:, DH_F + 3:DH_F + 4], 0.0)
        oq_ref[...] = jnp.concatenate(qs, axis=1).astype(BF16)
        ok_ref[...] = jnp.concatenate(ks, axis=1).astype(BF16)
        ov_ref[...] = jnp.concatenate(vs, axis=1).astype(BF16)
        dc_ref[...] = dc

    ispec = pl.BlockSpec((nh, tm, 128), lambda i: (0, i, 0))
    ospec = pl.BlockSpec((tm, nh * DH_F), lambda i: (i, 0))
    return pl.pallas_call(
        body, name="fox_bwd_post", grid=(t // tm,), in_specs=[ispec] * 3,
        out_specs=[ospec] * 3 + [pl.BlockSpec((tm, 128), lambda i: (i, 0))],
        out_shape=[jax.ShapeDtypeStruct((t, nh * DH_F), BF16)] * 3 + [jax.ShapeDtypeStruct((t, 128), F32)],
        compiler_params=_cparams(("parallel",)),
    )(dqa, dka, dva)


W_BIG = 6 * 512
IN_OFF = (0, 512, 1024, 1544, 2056, 2568)
IN_GATES = (1536, 3080)


def _heads(a, nh):
    t = a.shape[0]
    return a.reshape(t, nh, -1).transpose(1, 0, 2)


def _unheads(a):
    nh, t, dh = a.shape
    return a.transpose(1, 0, 2).reshape(t, nh * dh)


FFN1 = ("ffn1_w_gate", "ffn1_w_up", "ffn1_w_down")
REST = ("w_in", "w_out", "ffn2_w_gate", "ffn2_w_up", "ffn2_w_down", "w_ple_gate", "w_ple_proj")
SPLIT = {n: 1 if n == "w_in" else 0 for n in FFN1 + REST}


def _rs_partials(names, gw, c_idx):
    wire = [_cast_other_half("rs_cast_" + n, gw[n], c_idx, SPLIT[n]) for n in names]
    swapped = _swap("rs_swap_" + names[0], wire)
    return [_add_my_half("rs_add_" + n, gw[n], r, c_idx, SPLIT[n]) for n, r in zip(names, swapped)]


def _local_step(x, p, tgt, sp, wg1, wu1, wd1, rest_slots, c_idx, place):
    t, d = x.shape
    slot = dict(zip(REST + ("conv_qk",), rest_slots))
    (h1, xn1, g1, u1), (w_in, conv_w) = _ffn_fwd(
        "ffn1", x, sp["ffn1_norm"], wg1, wu1, wd1, plan=_gather_plan([slot["w_in"], slot["conv_qk"]], [SPLIT["w_in"], None]))
    w_in, conv_w = w_in.reshape(-1, d), _from_chip_blocks(conv_w)
    w_big = jnp.concatenate([w_in[o:o + 512] for o in IN_OFF], axis=0)
    w_small = jnp.concatenate([w_in[IN_GATES[0]:IN_GATES[0] + 8], w_in[IN_GATES[1]:IN_GATES[1] + 8],
                               jnp.zeros((112, d), w_in.dtype)], axis=0)
    u, zbig = _norm_mm("in_big", h1, sp["mix_norm"], w_big, True, BF16)
    zs = _mm_nt("in_small", u, w_small, tm=1024, tk=1024)
    zsr = zs.T
    qk_act = _conv_fwd(zbig, conv_w)
    bm_c, bf_c = sp["b_mlstm_gates"], sp["b_fox_f"]
    y_m, cst, mst = _mlstm_fwd(qk_act, zbig, zs, zsr, bm_c, bm_c.T, sp["mlstm_out_norm"])
    c = _fox_cumsum(zsr, bf_c.T)
    qa, ka, va = _fox_prep(zbig, c.T)
    (y_ft, o_f, lse), late = _fox_fwd2(qa, ka, va, sp["fox_out_norm"],
                                       plan=_gather_plan([slot[n] for n in REST[1:]], [SPLIT[n] for n in REST[1:]]))
    full = dict(zip(REST[1:], late))
    w_out, w_pg = (full[n].reshape(-1, d) for n in ("w_out", "w_ple_gate"))
    wg2, wu2, wd2 = full["ffn2_w_gate"], full["ffn2_w_up"], full["ffn2_w_down"]
    w_pp = _from_chip_blocks(full["w_ple_proj"])
    tm = _pick(t, (1024, 512, 256))
    h2 = _mm("out_proj", [(y_m, (tm, 512), lambda i, j, k: (i, 0), w_out, (512, d), lambda i, j, k: (0, 0)),
                          (y_ft, (tm, 512), lambda i, j, k: (i, 0), w_out, (512, d), lambda i, j, k: (1, 0))],
             (t, d), (tm, d), lambda i, j, k: (i, 0), (t // tm, 1, 1), 2, res=h1)
    (h3, xn2, g2, u2), _ = _ffn_fwd("ffn2", h2, sp["ffn2_norm"], wg2, wu2, wd2)
    hn3, gate_pre = _norm_mm("ple_gate", h3, sp["ple_gate_norm"], w_pg, False, F32)
    pp = _mm_nn("ple_proj", p, w_pp, tm=1024)

    def head_fn(h3_t, gp_t, pp_t, tgt_t, g_pp, g_fin):
        gate = _sigmoid(gp_t)
        ppn = _rms_fwd_val(pp_t, g_pp)
        h4 = h3_t + gate * ppn
        err = _rms_fwd_val(h4, g_fin) - tgt_t
        loss = 0.5 * jnp.sum(jnp.mean(err * err, axis=-1, keepdims=True), axis=0, keepdims=True)
        dh4, dg_fin = _rms_bwd_val(err * (1.0 / d), h4, g_fin)
        dpp, dg_pp = _rms_bwd_val(dh4 * gate, pp_t, g_pp)
        dgp = dh4 * ppn * gate * (1.0 - gate)
        return dh4, dgp, dpp, jnp.broadcast_to(loss, (1, 128)), dg_fin, dg_pp

    dh4, dgp, dpp, loss_part, dg_fin, dg_pp = _rowwise(
        "loss_head", head_fn, [h3, gate_pre, pp, tgt], [sp["ple_proj_norm"], sp["final_norm"]],
        [(d, F32), (d, BF16), (d, BF16)], [((1, 128), F32), ((1, d), F32), ((1, d), F32)])
    gw, gs = {}, {"final_norm": dg_fin, "ple_proj_norm": dg_pp}
    gw["w_ple_gate"] = _mm_tn("d_w_pg", hn3, dgp, tm=1024, tn=1024)
    gw["w_ple_proj"] = _mm_tn("d_w_pp", p, dpp, tn=1024)
    dhn3 = _mm_nt("d_hn3", dgp, w_pg, tm=1024, tn=1024, tk=1024)

    def res_norm_bwd(dn_t, h_t, dres_t, g):
        dx, dg = _rms_bwd_val(dn_t, h_t, g)
        return dres_t + dx, dg

    dh3, gs["ple_gate_norm"] = _rowwise("ple_norm_bwd", res_norm_bwd, [dhn3, h3, dh4], [sp["ple_gate_norm"]],
                                        [(d, F32)], [((1, d), F32)])
    (dh2, gs["ffn2_norm"], gw["ffn2_w_gate"], gw["ffn2_w_up"], gw["ffn2_w_down"]), _ = _ffn_bwd(
        "ffn2", dh3, h2, sp["ffn2_norm"], xn2, g2, u2, wg2, wu2, wd2)
    dycat = _mm_nt("d_ycat", dh2, w_out, tm=1024, tn=1024, tk=1024)
    gw["w_out"] = jnp.concatenate([_mm_tn("d_w_out_m", y_m, dh2, tn=1024), _mm_tn("d_w_out_f", y_ft, dh2, tn=1024)], axis=0)
    doa, delta, gs["fox_out_norm"] = _fox_bwd_prep(dycat, o_f, sp["fox_out_norm"])
    dq_f, dk_f, dv_f, dct = _fox_bwd_post(*_fox_bwd2(qa, ka, va, doa, lse, delta))
    dfp = _fox_gate_bwd(zsr, bf_c.T, dct[:, :NH_F].T)
    dact, dv_m, do_m, dzs_m, dzr_m, gs["mlstm_out_norm"] = _mlstm_bwd(
        qk_act, zbig, zs, zsr, bm_c, bm_c.T, sp["mlstm_out_norm"], cst, mst, dycat)
    dqk, gw["conv_qk"] = _conv_bwd(zbig, dact, conv_w)
    dz_big = jnp.concatenate([dqk, dv_m, do_m, dq_f, dk_f, dv_f], axis=1)
    dzs = dzs_m + jnp.pad(jnp.concatenate([dzr_m, dfp], axis=0).T, ((0, 0), (0, 112)))
    dw_big = _mm_tn("d_w_big", dz_big, u, tn=1024)
    dw_small = _mm_tn("d_w_small", dzs, u, tn=1024)
    gw["w_in"] = jnp.concatenate([dw_big[0:1536], dw_small[0:8], dw_big[1536:3072], dw_small[8:16]], axis=0)
    du_a = _mm_nn("d_u_big", dz_big, w_big, tm=1024, tn=1024, tk=1024)
    du_b = _mm_nn("d_u_small", dzs, w_small, tm=1024, tn=1024)

    def mix_norm_bwd(da_t, db_t, h_t, dres_t, dzs_t, g):
        dx, dg = _rms_bwd_val(da_t + db_t, h_t, g)
        return dres_t + dx, dg, _colsum(dzs_t)

    dh1, gs["mix_norm"], dbias = _rowwise("mix_norm_bwd", mix_norm_bwd, [du_a, du_b, h1, dh2, dzs], [sp["mix_norm"]],
                                          [(d, F32)], [((1, d), F32), ((1, 128), F32)])
    gs["b_mlstm_gates"], gs["b_fox_f"] = dbias[:, 0:8], dbias[:, 8:16]
    conv_grad = gw.pop("conv_qk")
    gw["w_ple_proj"] = _chip_blocks(gw["w_ple_proj"])
    for n in ("w_in", "w_out", "w_ple_gate"):
        gw[n] = gw[n].reshape(4, -1, gw[n].shape[-1])
    part_rest = dict(zip(REST, _rs_partials(REST, gw, c_idx)))
    light = ("w_in", "w_out", "w_ple_gate", "w_ple_proj")
    part_ffn1 = []

    def own_plan(dwg, dwu, dwd):
        part_ffn1.extend(_rs_partials(FFN1, dict(zip(FFN1, (dwg, dwu, dwd))), c_idx))
        return _scatter_plan([pb for _, pb in part_ffn1])

    (grad_x, gs["ffn1_norm"], _, _, _), (l_light, l_down, l_gate, l_up, landed_ffn1) = _ffn_bwd_late_dx(
        "ffn1", dh1, x, sp["ffn1_norm"], xn1, g1, u1, wg1, wu1, wd1,
        _scatter_plan([part_rest[n][1] for n in light]),
        [_scatter_plan([part_rest[n][1]]) for n in ("ffn2_w_down", "ffn2_w_gate", "ffn2_w_up")], own_plan)
    landed_rest = dict(zip(light + ("ffn2_w_down", "ffn2_w_gate", "ffn2_w_up"), list(l_light) + [l_down[0], l_gate[0], l_up[0]]))
    names = REST + FFN1
    parts = [part_rest[n] for n in REST] + part_ffn1
    landed = [landed_rest[n] for n in REST] + list(landed_ffn1)
    mine = [_sum4("rs_sum_" + n, a, pf, place, SPLIT[n]) for n, a, (pf, _) in zip(names, landed, parts)]
    grads = dict(zip(names, _join_halves("rs_join", mine, [SPLIT[n] for n in names])))
    return loss_part, grad_x, grads, gs, conv_grad


ANY = pl.BlockSpec(memory_space=pl.ANY)
MESH = pl.DeviceIdType.MESH


def _place():
    x, y, c = lax.axis_index("x"), lax.axis_index("y"), lax.axis_index("c")
    chips = [(1 - x, y), (x, 1 - y), (1 - x, 1 - y)]
    return x, y, c, 2 * x + y, (x, y, 1 - c), chips


def _rcopy(src, dst, ssem, rsem, dev):
    return pltpu.make_async_remote_copy(src_ref=src, dst_ref=dst, send_sem=ssem, recv_sem=rsem, device_id=dev,
                                        device_id_type=MESH)


def _half(ref, lead, axis, idx, half):
    return ref.at[(slice(None),) * (lead + axis) + (pl.ds(idx * half, half),)]


def _to_slot(name, a, me_idx, dtype):
    r, cdim = a.shape
    tr = _pick(r, (256, 176, 128, 64))

    def body(me_ref, a_ref, o_ref):
        o_ref[...] = a_ref[...].astype(o_ref.dtype)

    return pl.pallas_call(
        body, name=name,
        grid_spec=pltpu.PrefetchScalarGridSpec(
            num_scalar_prefetch=1, grid=(r // tr,), in_specs=[pl.BlockSpec((tr, cdim), lambda i, me_ref: (i, 0))],
            out_specs=pl.BlockSpec((None, tr, cdim), lambda i, me_ref: (me_ref[0], i, 0))),
        out_shape=jax.ShapeDtypeStruct((4, r, cdim), dtype), compiler_params=_cparams(("parallel",)),
    )(me_idx, a)


def _gather4(name, bufs, split):
    return _run_plan(name, _gather_plan(bufs, split))


def _gather_plan(bufs, split):
    n = len(bufs)
    shapes = [b.shape[1:] for b in bufs]

    def ctx(outs):
        x, y, c, me, sib, chips = _place()

        def part(ref, a, which):
            if split[a] is None:
                return ref
            return _half(ref, 0, split[a], which, shapes[a][split[a]] // 2)

        return c, me, sib, chips, part

    def ici(outs, sems, a, j, chip, c, me, part):
        mine = part(outs[a].at[me], a, c)
        return _rcopy(mine, mine, sems[0].at[3 * a + j], sems[1].at[3 * a + j], (*chip, c))

    def fwd(outs, sems, a, j, chip, c, sib, part, which):
        blk = part(outs[a].at[2 * chip[0] + chip[1]], a, which)
        return _rcopy(blk, blk, sems[2].at[3 * a + j], sems[3].at[3 * a + j], sib)

    def start(ins, outs, sems):
        c, me, sib, chips, part = ctx(outs)
        for a in range(n):
            for j, chip in enumerate(chips):
                ici(outs, sems, a, j, chip, c, me, part).start()

    def mid(ins, outs, sems):
        c, me, sib, chips, part = ctx(outs)
        for j, chip in enumerate(chips):
            for a in range(n):
                blk = part(outs[a].at[2 * chip[0] + chip[1]], a, c)
                _rcopy(blk, blk, sems[0].at[3 * a + j], sems[1].at[3 * a + j], sib).wait_recv()
                if split[a] is not None:
                    fwd(outs, sems, a, j, chip, c, sib, part, c).start()

    def end(ins, outs, sems):
        c, me, sib, chips, part = ctx(outs)
        for j, chip in enumerate(chips):
            for a in range(n):
                if split[a] is not None:
                    fwd(outs, sems, a, j, chip, c, sib, part, 1 - c).wait_recv()
        for a in range(n):
            for j, chip in enumerate(chips):
                ici(outs, sems, a, j, chip, c, me, part).wait_send()
                if split[a] is not None:
                    fwd(outs, sems, a, j, chip, c, sib, part, c).wait_send()

    return dict(ins=list(bufs), outs=[jax.ShapeDtypeStruct(b.shape, b.dtype) for b in bufs], alias=True,
                sems=[pltpu.SemaphoreType.DMA((3 * n,))] * 4, phases=(start, mid, end))


def _run_plan(name, plan):
    ni, no = len(plan["ins"]), len(plan["outs"])

    def body(*refs):
        ins, outs, sems = refs[:ni], refs[ni:ni + no], refs[ni + no:]
        for phase in plan["phases"]:
            phase(ins, outs, sems)

    return pl.pallas_call(
        body, name=name, in_specs=[ANY] * ni, out_specs=[ANY] * no, out_shape=plan["outs"],
        input_output_aliases={a: a for a in range(ni)} if plan["alias"] else {}, scratch_shapes=plan["sems"],
    )(*plan["ins"])


class _Hosted:
    def __init__(self, plan, n_in, n_out):
        self.plan, self.n_in, self.n_out = plan, n_in, n_out
        self.ni, self.no, self.ns = (len(plan["ins"]) if plan else 0, len(plan["outs"]) if plan else 0,
                                     len(plan["sems"]) if plan else 0)

    def split(self, refs):
        a, b = self.n_in, self.n_in + self.ni
        c, d = b + self.n_out, b + self.n_out + self.no
        e = len(refs) - self.ns
        return refs[:a], refs[b:c], refs[d:e], (refs[a:b], refs[c:d], refs[e:])

    def run(self, k, cond, prefs):
        if self.plan is not None:
            @pl.when(cond)
            def _():
                self.plan["phases"][k](*prefs)

    def call_args(self):
        p = self.plan
        if p is None:
            return dict(in_specs=[], out_specs=[], out_shape=[], scratch=[], aliases={}, args=[])
        al = {self.n_in + a: self.n_out + a for a in range(self.ni)} if p["alias"] else {}
        return dict(in_specs=[ANY] * self.ni, out_specs=[ANY] * self.no, out_shape=list(p["outs"]), scratch=list(p["sems"]),
                    aliases=al, args=list(p["ins"]))


def _swap(name, arrs):
    n = len(arrs)

    def body(*refs):
        ins, outs = refs[:n], refs[n:2 * n]
        ssem, rsem = refs[2 * n:]
        x, y, c, me, sib, chips = _place()
        cps = [_rcopy(ins[a], outs[a], ssem.at[a], rsem.at[a], sib) for a in range(n)]
        for cp in cps:
            cp.start()
        for cp in cps:
            cp.wait()

    return pl.pallas_call(
        body, name=name, in_specs=[ANY] * n, out_specs=[ANY] * n,
        out_shape=[jax.ShapeDtypeStruct(a.shape, a.dtype) for a in arrs],
        scratch_shapes=[pltpu.SemaphoreType.DMA((n,))] * 2,
    )(*arrs)


def _scatter4(name, arrs):
    return _run_plan(name, _scatter_plan(arrs))


def _scatter_plan(arrs):
    n = len(arrs)

    def send(ins, outs, sems, a, j, chip, c, me):
        return _rcopy(ins[a].at[2 * chip[0] + chip[1]], outs[a].at[me], sems[0].at[3 * a + j], sems[1].at[3 * a + j], (*chip, c))

    def start(ins, outs, sems):
        x, y, c, me, sib, chips = _place()
        for a in range(n):
            for j, chip in enumerate(chips):
                send(ins, outs, sems, a, j, chip, c, me).start()

    def mid(ins, outs, sems):
        pass

    def end(ins, outs, sems):
        x, y, c, me, sib, chips = _place()
        for a in range(n):
            for j, chip in enumerate(chips):
                blk = outs[a].at[2 * chip[0] + chip[1]]
                _rcopy(blk, blk, sems[0].at[3 * a + j], sems[1].at[3 * a + j], sib).wait_recv()
        for a in range(n):
            for j, chip in enumerate(chips):
                send(ins, outs, sems, a, j, chip, c, me).wait_send()

    return dict(ins=list(arrs), outs=[jax.ShapeDtypeStruct(a.shape, a.dtype) for a in arrs], alias=False,
                sems=[pltpu.SemaphoreType.DMA((3 * n,))] * 2, phases=(start, mid, end))


def _join_halves(name, arrs, split):
    n = len(arrs)

    def body(*refs):
        outs = refs[n:2 * n]
        ssem, rsem = refs[2 * n:]
        x, y, c, me, sib, chips = _place()
        cps = []
        for a in range(n):
            mine = _half(outs[a], 0, split[a], c, arrs[a].shape[split[a]] // 2)
            cp = _rcopy(mine, mine, ssem.at[a], rsem.at[a], sib)
            cp.start()
            cps.append(cp)
        for a in range(n):
            blk = _half(outs[a], 0, split[a], 1 - c, arrs[a].shape[split[a]] // 2)
            _rcopy(blk, blk, ssem.at[a], rsem.at[a], sib).wait_recv()
        for cp in cps:
            cp.wait_send()

    return pl.pallas_call(
        body, name=name, in_specs=[ANY] * n, out_specs=[ANY] * n,
        out_shape=[jax.ShapeDtypeStruct(a.shape, a.dtype) for a in arrs],
        input_output_aliases={a: a for a in range(n)}, scratch_shapes=[pltpu.SemaphoreType.DMA((n,))] * 2,
    )(*arrs)


def _allreduce_small(s):
    r, cdim = s.shape

    def body(s_ref, o_ref, buf, ssem, rsem):
        x, y, c, me, sib, chips = _place()
        me8 = 4 * x + 2 * y + c
        buf[me8] = s_ref[...]
        flips = [(fx, fy, fc) for fx in (0, 1) for fy in (0, 1) for fc in (0, 1)][1:]
        cps = []
        for k, (fx, fy, fc) in enumerate(flips):
            peer = (x ^ fx if fx else x, y ^ fy if fy else y, c ^ fc if fc else c)
            cp = _rcopy(s_ref, buf.at[me8], ssem.at[k], rsem.at[k], peer)
            cp.start()
            cps.append(cp)
        for k, (fx, fy, fc) in enumerate(flips):
            src = 4 * (x ^ fx if fx else x) + 2 * (y ^ fy if fy else y) + (c ^ fc if fc else c)
            _rcopy(s_ref, buf.at[src], ssem.at[k], rsem.at[k], sib).wait_recv()
        for cp in cps:
            cp.wait_send()
        acc = buf[0]
        for k in range(1, 8):
            acc = acc + buf[k]
        o_ref[...] = acc

    vm = pl.BlockSpec(memory_space=pltpu.VMEM)
    return pl.pallas_call(
        body, name="allreduce_small", in_specs=[vm], out_specs=vm, out_shape=jax.ShapeDtypeStruct((r, cdim), F32),
        scratch_shapes=[pltpu.VMEM((8, r, cdim), F32), pltpu.SemaphoreType.DMA((7,)), pltpu.SemaphoreType.DMA((7,))],
    )(s)


def _add_my_half(name, g, recv, c_idx, axis):
    nb, hr, hc = recv.shape
    tr = _pick(hr, (256, 176, 128, 64))
    if axis == 0:
        g4 = g.reshape(nb, 2, hr, hc)
        gspec = pl.BlockSpec((None, None, tr, hc), lambda b, i, c_ref: (b, c_ref[0], i, 0))
    else:
        g4 = g
        gspec = pl.BlockSpec((None, tr, hc), lambda b, i, c_ref: (b, i, c_ref[0]))

    def body(c_ref, g_ref, r_ref, o_ref, ob_ref):
        s = g_ref[...] + r_ref[...].astype(F32)
        o_ref[...] = s
        ob_ref[...] = s.astype(BF16)

    ospec = pl.BlockSpec((None, tr, hc), lambda b, i, c_ref: (b, i, 0))
    return pl.pallas_call(
        body, name=name,
        grid_spec=pltpu.PrefetchScalarGridSpec(
            num_scalar_prefetch=1, grid=(nb, hr // tr), in_specs=[gspec, ospec], out_specs=[ospec, ospec]),
        out_shape=[jax.ShapeDtypeStruct((nb, hr, hc), F32), jax.ShapeDtypeStruct((nb, hr, hc), BF16)],
        compiler_params=_cparams(("parallel", "parallel")),
    )(c_idx, g4, recv)


def _sum4(name, landed, own, place, axis):
    nb, h, cdim = landed.shape
    tr = _pick(h, (256, 176, 128, 64))
    nt = h // tr

    def body(p_ref, a1_ref, a2_ref, a3_ref, own_ref, o_ref):
        o_ref[...] = ((own_ref[...] + a1_ref[...].astype(F32)) + a2_ref[...].astype(F32)) + a3_ref[...].astype(F32)

    def nxt(k):
        return pl.BlockSpec((None, tr, cdim), lambda i, p_ref: ((p_ref[0] + k) % nb, i, 0))

    if axis == 0:
        ospec = pl.BlockSpec((tr, cdim), lambda i, p_ref: (p_ref[1] * nt + i, 0))
        oshape = (2 * h, cdim)
    else:
        ospec = pl.BlockSpec((tr, cdim), lambda i, p_ref: (i, p_ref[1]))
        oshape = (h, 2 * cdim)
    return pl.pallas_call(
        body, name=name,
        grid_spec=pltpu.PrefetchScalarGridSpec(
            num_scalar_prefetch=1, grid=(nt,), in_specs=[nxt(1), nxt(2), nxt(3), nxt(0)], out_specs=ospec),
        out_shape=jax.ShapeDtypeStruct(oshape, F32), compiler_params=_cparams(("parallel",)),
    )(place, landed, landed, landed, own)


def _cast_other_half(name, g, c_idx, axis):
    nb, r, cdim = g.shape
    hr, hc = (r // 2, cdim) if axis == 0 else (r, cdim // 2)
    tr = _pick(hr, (256, 176, 128, 64))
    if axis == 0:
        g4 = g.reshape(nb, 2, hr, hc)
        gspec = pl.BlockSpec((None, None, tr, hc), lambda b, i, c_ref: (b, 1 - c_ref[0], i, 0))
    else:
        g4 = g
        gspec = pl.BlockSpec((None, tr, hc), lambda b, i, c_ref: (b, i, 1 - c_ref[0]))

    def body(c_ref, g_ref, o_ref):
        o_ref[...] = g_ref[...].astype(BF16)

    return pl.pallas_call(
        body, name=name,
        grid_spec=pltpu.PrefetchScalarGridSpec(
            num_scalar_prefetch=1, grid=(nb, hr // tr), in_specs=[gspec],
            out_specs=pl.BlockSpec((None, tr, hc), lambda b, i, c_ref: (b, i, 0))),
        out_shape=jax.ShapeDtypeStruct((nb, hr, hc), BF16), compiler_params=_cparams(("parallel", "parallel")),
    )(c_idx, g4)


def _adamw(name, w, g, m, v):
    c1 = 1.0 - ADAM_B1 ** ADAM_STEP
    c2 = 1.0 - ADAM_B2 ** ADAM_STEP

    def fn(w_t, g_t, m_t, v_t):
        m_n = ADAM_B1 * m_t + (1.0 - ADAM_B1) * g_t
        v_n = ADAM_B2 * v_t + (1.0 - ADAM_B2) * (g_t * g_t)
        delta = -ADAM_LR * ((m_n / c1) / (jnp.sqrt(v_n / c2) + ADAM_EPS) + ADAM_WD * w_t)
        return delta, m_n, v_n

    cdim = w.shape[1]
    return _rowwise(name, fn, [w, g, m, v], [], [(cdim, F32)] * 3, tm=_pick(w.shape[0], (256, 176, 128, 64, 8)))


BIG = ("ffn1_w_gate", "ffn1_w_up", "ffn1_w_down", "w_in", "w_out", "ffn2_w_gate", "ffn2_w_up", "ffn2_w_down",
       "w_ple_gate", "w_ple_proj")
SMALL = ("ffn1_norm", "mix_norm", "b_mlstm_gates", "b_fox_f", "mlstm_out_norm", "fox_out_norm", "ffn2_norm",
         "ple_gate_norm", "ple_proj_norm", "final_norm")
WEIGHTS = ("ffn1_norm", "ffn1_w_gate", "ffn1_w_up", "ffn1_w_down", "mix_norm", "w_in", "conv_qk", "b_mlstm_gates",
           "b_fox_f", "mlstm_out_norm", "fox_out_norm", "w_out", "ffn2_norm", "ffn2_w_gate", "ffn2_w_up", "ffn2_w_down",
           "ple_gate_norm", "w_ple_gate", "w_ple_proj", "ple_proj_norm", "final_norm")
TRANSPOSED = ("ffn1_w_gate", "ffn1_w_up", "w_in", "ffn2_w_gate", "ffn2_w_up")
PACK_W = 1024


def _chip_blocks(a):
    r, c4 = a.shape
    return a.reshape(r, 4, c4 // 4).transpose(1, 0, 2)


def _from_chip_blocks(a):
    nb, r, c = a.shape
    return a.transpose(1, 0, 2).reshape(r, nb * c)


def kernel(x, p, ffn1_norm, ffn1_w_gate, ffn1_w_up, ffn1_w_down, mix_norm, w_in, conv_qk, b_mlstm_gates, b_fox_f, mlstm_out_norm, fox_out_norm, w_out, ffn2_norm, ffn2_w_gate, ffn2_w_up, ffn2_w_down, ple_gate_norm, w_ple_gate, w_ple_proj, ple_proj_norm, final_norm, loss_target, m_ffn1_norm, m_ffn1_w_gate, m_ffn1_w_up, m_ffn1_w_down, m_mix_norm, m_w_in, m_conv_qk, m_b_mlstm_gates, m_b_fox_f, m_mlstm_out_norm, m_fox_out_norm, m_w_out, m_ffn2_norm, m_ffn2_w_gate, m_ffn2_w_up, m_ffn2_w_down, m_ple_gate_norm, m_w_ple_gate, m_w_ple_proj, m_ple_proj_norm, m_final_norm, v_ffn1_norm, v_ffn1_w_gate, v_ffn1_w_up, v_ffn1_w_down, v_mix_norm, v_w_in, v_conv_qk, v_b_mlstm_gates, v_b_fox_f, v_mlstm_out_norm, v_fox_out_norm, v_w_out, v_ffn2_norm, v_ffn2_w_gate, v_ffn2_w_up, v_ffn2_w_down, v_ple_gate_norm, v_w_ple_gate, v_w_ple_proj, v_ple_proj_norm, v_final_norm):
    w = dict(ffn1_norm=ffn1_norm, ffn1_w_gate=ffn1_w_gate, ffn1_w_up=ffn1_w_up, ffn1_w_down=ffn1_w_down, mix_norm=mix_norm,
             w_in=w_in, conv_qk=conv_qk, b_mlstm_gates=b_mlstm_gates, b_fox_f=b_fox_f, mlstm_out_norm=mlstm_out_norm,
             fox_out_norm=fox_out_norm, w_out=w_out, ffn2_norm=ffn2_norm, ffn2_w_gate=ffn2_w_gate, ffn2_w_up=ffn2_w_up,
             ffn2_w_down=ffn2_w_down, ple_gate_norm=ple_gate_norm, w_ple_gate=w_ple_gate, w_ple_proj=w_ple_proj,
             ple_proj_norm=ple_proj_norm, final_norm=final_norm)
    m = dict(ffn1_norm=m_ffn1_norm, ffn1_w_gate=m_ffn1_w_gate, ffn1_w_up=m_ffn1_w_up, ffn1_w_down=m_ffn1_w_down,
             mix_norm=m_mix_norm, w_in=m_w_in, conv_qk=m_conv_qk, b_mlstm_gates=m_b_mlstm_gates, b_fox_f=m_b_fox_f,
             mlstm_out_norm=m_mlstm_out_norm, fox_out_norm=m_fox_out_norm, w_out=m_w_out, ffn2_norm=m_ffn2_norm,
             ffn2_w_gate=m_ffn2_w_gate, ffn2_w_up=m_ffn2_w_up, ffn2_w_down=m_ffn2_w_down, ple_gate_norm=m_ple_gate_norm,
             w_ple_gate=m_w_ple_gate, w_ple_proj=m_w_ple_proj, ple_proj_norm=m_ple_proj_norm, final_norm=m_final_norm)
    v = dict(ffn1_norm=v_ffn1_norm, ffn1_w_gate=v_ffn1_w_gate, ffn1_w_up=v_ffn1_w_up, ffn1_w_down=v_ffn1_w_down,
             mix_norm=v_mix_norm, w_in=v_w_in, conv_qk=v_conv_qk, b_mlstm_gates=v_b_mlstm_gates, b_fox_f=v_b_fox_f,
             mlstm_out_norm=v_mlstm_out_norm, fox_out_norm=v_fox_out_norm, w_out=v_w_out, ffn2_norm=v_ffn2_norm,
             ffn2_w_gate=v_ffn2_w_gate, ffn2_w_up=v_ffn2_w_up, ffn2_w_down=v_ffn2_w_down, ple_gate_norm=v_ple_gate_norm,
             w_ple_gate=v_w_ple_gate, w_ple_proj=v_w_ple_proj, ple_proj_norm=v_ple_proj_norm, final_norm=v_final_norm)
    shapes = {n: w[n].shape for n in WEIGHTS}

    def view(a, n):
        return a[0].T if n in TRANSPOSED else a.reshape(-1, a.shape[-1])

    def unview(a, n):
        return (a.T if n in TRANSPOSED else a).reshape(shapes[n])

    w2, m2, v2 = ({n: view(a, n) for n, a in d.items()} for d in (w, m, v))

    c_idx = lax.axis_index("c").astype(jnp.int32).reshape(1)
    me_idx = (2 * lax.axis_index("x") + lax.axis_index("y")).astype(jnp.int32).reshape(1)
    place = jnp.concatenate([me_idx, c_idx])
    slot = {n: _to_slot("slot_" + n, w2[n], me_idx, BF16) for n in BIG}
    slot["conv_qk"] = _to_slot("slot_conv_qk", w2["conv_qk"], me_idx, F32)
    wg1, wu1, wd1 = _gather4("gather_ffn1", [slot[n] for n in FFN1], [SPLIT[n] for n in FFN1])
    sp = {n: w2[n] for n in SMALL}
    loss_part, grad_x, grads, gs, conv_grad = _local_step(
        x[0], p[0, 0], loss_target[0], sp, wg1, wu1, wd1, [slot[n] for n in REST + ("conv_qk",)], c_idx, place)
    loss = lax.psum(loss_part[0, 0], ("x", "y", "c"))

    small = [gs[n].reshape(1, -1) for n in SMALL] + [conv_grad]
    rows = [jnp.pad(a, ((0, 0), (0, PACK_W - a.shape[1]))) for a in small]
    packed = jnp.concatenate(rows, axis=0)
    packed = jnp.pad(packed, ((0, -packed.shape[0] % 8), (0, 0)))
    red = _allreduce_small(packed)
    for i, n in enumerate(SMALL):
        grads[n] = red[i:i + 1, :gs[n].size]
    dconv = red[len(SMALL):len(SMALL) + CONV_W, :conv_grad.shape[1]]
    cw = conv_qk.shape[-1]
    grads["conv_qk"] = lax.dynamic_slice_in_dim(dconv, (2 * lax.axis_index("x") + lax.axis_index("y")) * cw, cw, axis=1)

    outs = {}
    for n in WEIGHTS:
        g2 = grads[n].reshape(w2[n].shape)
        d, nm, nv = _adamw("adamw_" + n, w2[n], g2, m2[n], v2[n])
        outs[n] = tuple(unview(a, n) for a in (g2, d, nm, nv))
    return (loss, grad_x[None], *[outs[n][0] for n in WEIGHTS], *[outs[n][1] for n in WEIGHTS],
            *[outs[n][2] for n in WEIGHTS], *[outs[n][3] for n in WEIGHTS])
```

```python
import functools
import math

import jax
import jax.numpy as jnp
from jax import lax
from jax.experimental import pallas as pl
from jax.experimental.pallas import tpu as pltpu

F32 = jnp.float32
BF16 = jnp.bfloat16
EPS = 1e-6
NH_M, DK_M, DV_M = 4, 64, 128
NH_F, DH_F = 8, 64
CONV_W = 4
ADAM_LR, ADAM_B1, ADAM_B2, ADAM_EPS, ADAM_WD, ADAM_STEP = 0.001, 0.9, 0.999, 1e-08, 0.01, 10
VMEM_LIMIT = 56 * 1024 * 1024


def _cparams(sem):
    return pltpu.CompilerParams(dimension_semantics=sem, vmem_limit_bytes=VMEM_LIMIT)


def _sigmoid(x):
    return 1.0 / (1.0 + jnp.exp(-x))


def _dot(a, b, ca, cb):
    return lax.dot_general(a.astype(BF16), b.astype(BF16), (((ca,), (cb,)), ((), ())), preferred_element_type=F32)


def _rowwise(name, fn, tiled, full, outs, accs=(), tm=256):
    rows = tiled[0].shape[0]
    tm = min(tm, rows)
    assert rows % tm == 0
    n_t, n_f, n_o, n_a = len(tiled), len(full), len(outs), len(accs)

    def body(*refs):
        ins = [r[...] for r in refs[: n_t + n_f]]
        res = fn(*ins)
        if not isinstance(res, (tuple, list)):
            res = (res,)
        orefs = refs[n_t + n_f:]
        for r, v in zip(orefs[:n_o], res[:n_o]):
            r[...] = v.astype(r.dtype)
        if n_a:
            @pl.when(pl.program_id(0) == 0)
            def _():
                for r in orefs[n_o:]:
                    r[...] = jnp.zeros_like(r)
            for r, v in zip(orefs[n_o:], res[n_o:]):
                r[...] += v.astype(r.dtype)

    in_specs = [pl.BlockSpec((tm, a.shape[1]), lambda i: (i, 0)) for a in tiled]
    in_specs += [pl.BlockSpec(a.shape, lambda i: (0, 0)) for a in full]
    out_specs = [pl.BlockSpec((tm, c), lambda i: (i, 0)) for c, _ in outs]
    out_specs += [pl.BlockSpec(s, lambda i: (0, 0)) for s, _ in accs]
    out_shape = [jax.ShapeDtypeStruct((rows, c), d) for c, d in outs]
    out_shape += [jax.ShapeDtypeStruct(s, d) for s, d in accs]
    res = pl.pallas_call(
        body, name=name, grid=(rows // tm,), in_specs=in_specs, out_specs=out_specs, out_shape=out_shape,
        compiler_params=_cparams(("arbitrary",) if n_a else ("parallel",)),
    )(*tiled, *full)
    return res


def _colsum(v):
    return jnp.sum(v, axis=0, keepdims=True)


def _rms_fwd_val(x, g):
    r = lax.rsqrt(jnp.mean(x * x, axis=-1, keepdims=True) + EPS)
    return x * r * g


def _rms_bwd_val(dy, x, g):
    r = lax.rsqrt(jnp.mean(x * x, axis=-1, keepdims=True) + EPS)
    xh = x * r
    dxh = dy * g
    dx = r * (dxh - xh * jnp.mean(dxh * xh, axis=-1, keepdims=True))
    return dx, _colsum(dy * xh)


def _mm(name, pairs, out_shape, out_block, out_map, grid, kaxis, ta=False, tb=False, scale=None, res=None,
        out_dtype=F32, plan=None):
    nk = grid[kaxis]
    npairs = len(pairs)
    ca, cb = (0 if ta else 1), (1 if tb else 0)
    acc_shape = tuple(d for d in out_block if d is not None)
    n_in = 2 * npairs + (1 if res is not None else 0)
    host = _Hosted(plan, n_in, 1)

    def body(*refs):
        ins, (o_ref,), (acc_ref,), prefs = host.split(refs)
        in_refs = ins[: 2 * npairs]
        res_ref = ins[2 * npairs] if res is not None else None
        k = pl.program_id(kaxis)
        ids = [pl.program_id(a) for a in range(len(grid))]
        first, last = ids[0] == 0, ids[0] == grid[0] - 1
        for a in range(1, len(grid)):
            first, last = first & (ids[a] == 0), last & (ids[a] == grid[a] - 1)
        host.run(0, first, prefs)
        host.run(1, first, prefs)

        @pl.when(k == 0)
        def _():
            acc_ref[...] = jnp.zeros_like(acc_ref)

        part = None
        for p in range(npairs):
            d = _dot(in_refs[2 * p][...], in_refs[2 * p + 1][...], ca, cb)
            part = d if part is None else part + d
        acc_ref[...] += part

        @pl.when(k == nk - 1)
        def _():
            v = acc_ref[...]
            if scale is not None:
                v = v * scale
            if res_ref is not None:
                v = v + res_ref[...].astype(F32)
            o_ref[...] = v.astype(o_ref.dtype)

        host.run(2, last, prefs)

    in_specs, args = [], []
    for a, ab, am, b, bb, bm in pairs:
        in_specs += [pl.BlockSpec(ab, am), pl.BlockSpec(bb, bm)]
        args += [a, b]
    if res is not None:
        in_specs.append(pl.BlockSpec(out_block, out_map))
        args.append(res)
    sem = tuple("arbitrary" if (i == kaxis or plan is not None) else "parallel" for i in range(len(grid)))
    hc = host.call_args()
    out = pl.pallas_call(
        body, name=name, grid=grid, in_specs=in_specs + hc["in_specs"],
        out_specs=[pl.BlockSpec(out_block, out_map)] + hc["out_specs"],
        out_shape=[jax.ShapeDtypeStruct(out_shape, out_dtype)] + hc["out_shape"],
        scratch_shapes=[pltpu.VMEM(acc_shape, F32)] + hc["scratch"], input_output_aliases=hc["aliases"],
        compiler_params=_cparams(sem),
    )(*args, *hc["args"])
    return out[0] if plan is None else (out[0], out[1:])


def _pick(n, pref):
    for t in pref:
        if n % t == 0:
            return t
    return n


def _mm_nn(name, a, b, tm=512, tn=512, tk=512, **kw):
    (m, k), n = a.shape, b.shape[1]
    tm, tn, tk = _pick(m, (tm, 256, 128)), _pick(n, (tn, 256, 128)), _pick(k, (tk, 256, 128))
    return _mm(name, [(a, (tm, tk), lambda i, j, kk: (i, kk), b, (tk, tn), lambda i, j, kk: (kk, j))],
               (m, n), (tm, tn), lambda i, j, kk: (i, j), (m // tm, n // tn, k // tk), 2, **kw)


def _mm_nt(name, a, b, tm=512, tn=512, tk=512, **kw):
    (m, k), n = a.shape, b.shape[0]
    tm, tn, tk = _pick(m, (tm, 256, 128)), _pick(n, (tn, 256, 128)), _pick(k, (tk, 256, 128))
    return _mm(name, [(a, (tm, tk), lambda i, j, kk: (i, kk), b, (tn, tk), lambda i, j, kk: (j, kk))],
               (m, n), (tm, tn), lambda i, j, kk: (i, j), (m // tm, n // tn, k // tk), 2, tb=True, **kw)


def _mm_tn(name, a, b, tm=512, tn=512, tk=2048, **kw):
    (k, m), n = a.shape, b.shape[1]
    tm, tn, tk = _pick(m, (tm, 256, 128)), _pick(n, (tn, 256, 128)), _pick(k, (tk, 1024, 512, 256, 128))
    return _mm(name, [(a, (tk, tm), lambda i, j, kk: (kk, i), b, (tk, tn), lambda i, j, kk: (kk, j))],
               (m, n), (tm, tn), lambda i, j, kk: (i, j), (m // tm, n // tn, k // tk), 2, ta=True, **kw)


def _norm_mm(name, h, gamma, w, w_transposed, out_dtype):
    t, d = h.shape
    n = w.shape[0] if w_transposed else w.shape[1]
    tm, tn = _pick(t, (512, 256)), _pick(n, (1024, 512, 256, 128))

    def body(h_ref, gam_ref, w_ref, xn_ref, o_ref, xn_scr):
        @pl.when(pl.program_id(1) == 0)
        def _():
            xn = _rms_fwd_val(h_ref[...], gam_ref[...]).astype(BF16)
            xn_scr[...] = xn
            xn_ref[...] = xn

        o_ref[...] = _dot(xn_scr[...], w_ref[...], 1, 1 if w_transposed else 0).astype(o_ref.dtype)

    wspec = pl.BlockSpec((tn, d), lambda i, j: (j, 0)) if w_transposed else pl.BlockSpec((d, tn), lambda i, j: (0, j))
    return pl.pallas_call(
        body, name=name, grid=(t // tm, n // tn),
        in_specs=[pl.BlockSpec((tm, d), lambda i, j: (i, 0)), pl.BlockSpec((1, d), lambda i, j: (0, 0)), wspec],
        out_specs=[pl.BlockSpec((tm, d), lambda i, j: (i, 0)), pl.BlockSpec((tm, tn), lambda i, j: (i, j))],
        out_shape=[jax.ShapeDtypeStruct((t, d), BF16), jax.ShapeDtypeStruct((t, n), out_dtype)],
        scratch_shapes=[pltpu.VMEM((tm, d), BF16)], compiler_params=_cparams(("parallel", "arbitrary")),
    )(h, gamma, w)


CHAIN_ROWS = 256


def _row_chains(tm):
    n = max(tm // CHAIN_ROWS, 1)
    return [slice(r * (tm // n), (r + 1) * (tm // n)) for r in range(n)]


def _ffn_fwd(pfx, h, gamma, wg, wu, wd, plan=None):
    t, d = h.shape
    nb, f, _ = wg.shape
    tm = _pick(t, (1024, 512, 256))
    nt = t // tm
    host = _Hosted(plan, 5, 4)

    def body(*refs):
        (h_ref, gam_ref, wg_ref, wu_ref, wd_ref), (ho_ref, xn_ref, g_ref, u_ref), (xn_scr, acc_ref), prefs = host.split(refs)
        i, j = pl.program_id(0), pl.program_id(1)
        host.run(0, (i == 0) & (j == 0), prefs)
        host.run(1, (i == nt // 2) & (j == 0), prefs)

        @pl.when(j == 0)
        def _():
            xn = _rms_fwd_val(h_ref[...], gam_ref[...]).astype(BF16)
            xn_scr[...] = xn
            xn_ref[...] = xn
            acc_ref[...] = jnp.zeros_like(acc_ref)

        for rows in _row_chains(tm):
            x = xn_scr[rows, :]
            g = _dot(x, wg_ref[...], 1, 1)
            u = _dot(x, wu_ref[...], 1, 1)
            g_ref[rows, :] = g.astype(BF16)
            u_ref[rows, :] = u.astype(BF16)
            acc_ref[rows, :] += _dot(g * _sigmoid(g) * u, wd_ref[...], 1, 0)

        @pl.when(j == nb - 1)
        def _():
            ho_ref[...] = h_ref[...] + 0.5 * acc_ref[...]

        host.run(2, (i == nt - 1) & (j == nb - 1), prefs)

    row = pl.BlockSpec((tm, d), lambda i, j: (i, 0))
    blk = pl.BlockSpec((None, tm, f), lambda i, j: (j, i, 0))
    wspec = pl.BlockSpec((None, f, d), lambda i, j: (j, 0, 0))
    hc = host.call_args()
    res = pl.pallas_call(
        body, name=pfx + "_fwd", grid=(nt, nb),
        in_specs=[row, pl.BlockSpec((1, d), lambda i, j: (0, 0)), wspec, wspec, wspec] + hc["in_specs"],
        out_specs=[row, row, blk, blk] + hc["out_specs"],
        out_shape=[jax.ShapeDtypeStruct((t, d), F32), jax.ShapeDtypeStruct((t, d), BF16),
                   jax.ShapeDtypeStruct((nb, t, f), BF16), jax.ShapeDtypeStruct((nb, t, f), BF16)] + hc["out_shape"],
        scratch_shapes=[pltpu.VMEM((tm, d), BF16), pltpu.VMEM((tm, d), F32)] + hc["scratch"],
        input_output_aliases=hc["aliases"], compiler_params=_cparams(("arbitrary", "arbitrary")),
    )(h, gamma, wg, wu, wd, *hc["args"])
    return res[:4], res[4:]


def _ffn_bwd(pfx, dh_out, h, gamma, xn, g_all, u_all, wg, wu, wd, plan=None):
    t, d = h.shape
    nb, f, _ = wg.shape
    tm = _pick(t, (512, 256))
    tk = _pick(t, (2048, 1024, 512, 256))

    nt = t // tm
    host = _Hosted(plan, 8, 5)

    def body(*refs):
        ((dy_ref, h_ref, gam_ref, wg_ref, wu_ref, wd_ref, g_ref, u_ref), (dh_ref, dgam_ref, dg_ref, du_ref, a_ref),
         (acc_ref,), prefs) = host.split(refs)
        i, j = pl.program_id(0), pl.program_id(1)
        host.run(0, (i == 0) & (j == 0), prefs)
        host.run(1, (i == nt // 2) & (j == 0), prefs)

        @pl.when((i == 0) & (j == 0))
        def _():
            dgam_ref[...] = jnp.zeros_like(dgam_ref)

        @pl.when(j == 0)
        def _():
            acc_ref[...] = jnp.zeros_like(acc_ref)

        for rows in _row_chains(tm):
            da = _dot(dy_ref[rows, :], wd_ref[...], 1, 1) * 0.5
            g = g_ref[rows, :].astype(F32)
            u = u_ref[rows, :].astype(F32)
            s = _sigmoid(g)
            sl = g * s
            du = (da * sl).astype(BF16)
            dg = (da * u * (s + sl * (1.0 - s))).astype(BF16)
            du_ref[rows, :] = du
            dg_ref[rows, :] = dg
            a_ref[rows, :] = (sl * u).astype(BF16)
            acc_ref[rows, :] += _dot(dg, wg_ref[...], 1, 0) + _dot(du, wu_ref[...], 1, 0)

        @pl.when(j == nb - 1)
        def _():
            dx, dgam = _rms_bwd_val(acc_ref[...], h_ref[...], gam_ref[...])
            dh_ref[...] = dy_ref[...] + dx
            dgam_ref[...] += dgam

        host.run(2, (i == nt - 1) & (j == nb - 1), prefs)

    row = pl.BlockSpec((tm, d), lambda i, j: (i, 0))
    vec = pl.BlockSpec((1, d), lambda i, j: (0, 0))
    blk = pl.BlockSpec((None, tm, f), lambda i, j: (j, i, 0))
    wspec = pl.BlockSpec((None, f, d), lambda i, j: (j, 0, 0))
    hc = host.call_args()
    res = pl.pallas_call(
        body, name=pfx + "_bwd", grid=(nt, nb),
        in_specs=[row, row, vec, wspec, wspec, wspec, blk, blk] + hc["in_specs"],
        out_specs=[row, vec, blk, blk, blk] + hc["out_specs"],
        out_shape=[jax.ShapeDtypeStruct((t, d), F32), jax.ShapeDtypeStruct((1, d), F32)]
        + [jax.ShapeDtypeStruct((nb, t, f), BF16)] * 3 + hc["out_shape"],
        scratch_shapes=[pltpu.VMEM((tm, d), F32)] + hc["scratch"], input_output_aliases=hc["aliases"],
        compiler_params=_cparams(("arbitrary", "arbitrary")),
    )(dh_out, h, gamma, wg, wu, wd, g_all, u_all, *hc["args"])
    dh, dgamma, dg_all, du_all, a_all = res[:5]

    xmap, bmap, omap = (lambda b, k: (k, 0)), (lambda b, k: (b, k, 0)), (lambda b, k: (b, 0, 0))
    dwg = _mm(pfx + "_dwg", [(dg_all, (None, tk, f), bmap, xn, (tk, d), xmap)], (nb, f, d), (None, f, d), omap,
              (nb, t // tk), 1, ta=True)
    dwu = _mm(pfx + "_dwu", [(du_all, (None, tk, f), bmap, xn, (tk, d), xmap)], (nb, f, d), (None, f, d), omap,
              (nb, t // tk), 1, ta=True)
    dwd = _mm(pfx + "_dwd", [(a_all, (None, tk, f), bmap, dh_out, (tk, d), xmap)], (nb, f, d), (None, f, d), omap,
              (nb, t // tk), 1, ta=True, scale=0.5)
    return (dh, dgamma, dwg, dwu, dwd), res[5:]


def _ffn_bwd_late_dx(pfx, dh_out, h, gamma, xn, g_all, u_all, wg, wu, wd, plan_gu, plans_dw, make_plan_dx):
    t, d = h.shape
    nb, f, _ = wg.shape
    tm = _pick(t, (512, 256))
    tk = _pick(t, (2048, 1024, 512, 256))
    nt = t // tm
    host_a = _Hosted(plan_gu, 4, 3)

    def body_a(*refs):
        (dy_ref, wd_ref, g_ref, u_ref), (dg_ref, du_ref, a_ref), _, prefs = host_a.split(refs)
        i, j = pl.program_id(0), pl.program_id(1)
        host_a.run(0, (i == 0) & (j == 0), prefs)
        host_a.run(1, (i == 0) & (j == 0), prefs)
        for rows in _row_chains(tm):
            da = _dot(dy_ref[rows, :], wd_ref[...], 1, 1) * 0.5
            g = g_ref[rows, :].astype(F32)
            u = u_ref[rows, :].astype(F32)
            s = _sigmoid(g)
            sl = g * s
            du_ref[rows, :] = (da * sl).astype(BF16)
            dg_ref[rows, :] = (da * u * (s + sl * (1.0 - s))).astype(BF16)
            a_ref[rows, :] = (sl * u).astype(BF16)
        host_a.run(2, (i == nt - 1) & (j == nb - 1), prefs)

    row = pl.BlockSpec((tm, d), lambda i, j: (i, 0))
    vec = pl.BlockSpec((1, d), lambda i, j: (0, 0))
    blk = pl.BlockSpec((None, tm, f), lambda i, j: (j, i, 0))
    wspec = pl.BlockSpec((None, f, d), lambda i, j: (j, 0, 0))
    hc = host_a.call_args()
    res_a = pl.pallas_call(
        body_a, name=pfx + "_bwd_gu", grid=(nt, nb), in_specs=[row, wspec, blk, blk] + hc["in_specs"],
        out_specs=[blk] * 3 + hc["out_specs"], out_shape=[jax.ShapeDtypeStruct((nb, t, f), BF16)] * 3 + hc["out_shape"],
        scratch_shapes=hc["scratch"], input_output_aliases=hc["aliases"], compiler_params=_cparams(("arbitrary", "arbitrary")),
    )(dh_out, wd, g_all, u_all, *hc["args"])
    dg_all, du_all, a_all = res_a[:3]

    xmap, bmap, omap = (lambda b, k: (k, 0)), (lambda b, k: (b, k, 0)), (lambda b, k: (b, 0, 0))
    dwd, out_d = _mm(pfx + "_dwd", [(a_all, (None, tk, f), bmap, dh_out, (tk, d), xmap)], (nb, f, d), (None, f, d), omap,
                     (nb, t // tk), 1, ta=True, scale=0.5, plan=plans_dw[0])
    dwg, out_g = _mm(pfx + "_dwg", [(dg_all, (None, tk, f), bmap, xn, (tk, d), xmap)], (nb, f, d), (None, f, d), omap,
                     (nb, t // tk), 1, ta=True, plan=plans_dw[1])
    dwu, out_u = _mm(pfx + "_dwu", [(du_all, (None, tk, f), bmap, xn, (tk, d), xmap)], (nb, f, d), (None, f, d), omap,
                     (nb, t // tk), 1, ta=True, plan=plans_dw[2])

    plan_dx = make_plan_dx(dwg, dwu, dwd)
    host_b = _Hosted(plan_dx, 7, 2)

    def body_b(*refs):
        (dy_ref, h_ref, gam_ref, wg_ref, wu_ref, dg_ref, du_ref), (dh_ref, dgam_ref), (acc_ref,), prefs = host_b.split(refs)
        i, j = pl.program_id(0), pl.program_id(1)
        host_b.run(0, (i == 0) & (j == 0), prefs)
        host_b.run(1, (i == 0) & (j == 0), prefs)

        @pl.when((i == 0) & (j == 0))
        def _():
            dgam_ref[...] = jnp.zeros_like(dgam_ref)

        @pl.when(j == 0)
        def _():
            acc_ref[...] = jnp.zeros_like(acc_ref)

        acc_ref[...] += _dot(dg_ref[...], wg_ref[...], 1, 0) + _dot(du_ref[...], wu_ref[...], 1, 0)

        @pl.when(j == nb - 1)
        def _():
            dx, dgam = _rms_bwd_val(acc_ref[...], h_ref[...], gam_ref[...])
            dh_ref[...] = dy_ref[...] + dx
            dgam_ref[...] += dgam

        host_b.run(2, (i == nt - 1) & (j == nb - 1), prefs)

    hc = host_b.call_args()
    res_b = pl.pallas_call(
        body_b, name=pfx + "_bwd_dx", grid=(nt, nb), in_specs=[row, row, vec, wspec, wspec, blk, blk] + hc["in_specs"],
        out_specs=[row, vec] + hc["out_specs"],
        out_shape=[jax.ShapeDtypeStruct((t, d), F32), jax.ShapeDtypeStruct((1, d), F32)] + hc["out_shape"],
        scratch_shapes=[pltpu.VMEM((tm, d), F32)] + hc["scratch"], input_output_aliases=hc["aliases"],
        compiler_params=_cparams(("arbitrary", "arbitrary")),
    )(dh_out, h, gamma, wg, wu, dg_all, du_all, *hc["args"])
    return (res_b[0], res_b[1], dwg, dwu, dwd), (res_a[3:], out_d, out_g, out_u, res_b[2:])


HALO = 16


def _silu_grad(y):
    s = _sigmoid(y)
    return s * (1.0 + y * (1.0 - s))


def _with_halo(ref, i, n_tiles, tm, before, after):
    t = ref.shape[0]
    r0 = pl.multiple_of(i * tm, tm)
    parts = [ref[pl.ds(r0, tm), :].astype(F32)]
    if before:
        prev = ref[pl.ds(pl.multiple_of(jnp.maximum(r0 - HALO, 0), HALO), HALO), :].astype(F32)
        parts.insert(0, jnp.where(i > 0, prev, 0.0))
    if after:
        nxt = ref[pl.ds(pl.multiple_of(jnp.minimum(r0 + tm, t - HALO), HALO), HALO), :].astype(F32)
        parts.append(jnp.where(i < n_tiles - 1, nxt, 0.0))
    return jnp.concatenate(parts, axis=0)


def _conv_fwd(zbig, w):
    t, c = zbig.shape[0], w.shape[1]
    tm = _pick(t, (512, 256))
    nt = t // tm

    def body(x_ref, w_ref, o_ref):
        xe = _with_halo(x_ref, pl.program_id(0), nt, tm, True, False)
        wv = w_ref[...]
        y = xe * wv[3:4, :]
        for i in range(CONV_W - 1):
            y = y + pltpu.roll(xe, CONV_W - 1 - i, 0) * wv[i:i + 1, :]
        y = y[HALO:, :]
        o_ref[...] = (y * _sigmoid(y)).astype(o_ref.dtype)

    return pl.pallas_call(
        body, name="conv_fwd", grid=(nt,),
        in_specs=[pl.BlockSpec((t, c), lambda i: (0, 0)), pl.BlockSpec(w.shape, lambda i: (0, 0))],
        out_specs=pl.BlockSpec((tm, c), lambda i: (i, 0)), out_shape=jax.ShapeDtypeStruct((t, c), BF16),
        compiler_params=_cparams(("parallel",)),
    )(zbig, w)


def _conv_bwd(zbig, dact, w):
    t, c = dact.shape
    tm = _pick(t, (512, 256))
    nt = t // tm
    n = tm + HALO

    def body(x_ref, d_ref, w_ref, dx_ref, dw_ref):
        xe = _with_halo(x_ref, pl.program_id(0), nt, tm, True, True)
        de = _with_halo(d_ref, pl.program_id(0), nt, tm, False, True)
        wv = w_ref[...]
        sh = [pltpu.roll(xe, CONV_W - 1 - i, 0)[HALO:, :] if i < CONV_W - 1 else xe[HALO:, :] for i in range(CONV_W)]
        y = sh[0] * wv[0:1, :]
        for i in range(1, CONV_W):
            y = y + sh[i] * wv[i:i + 1, :]
        dy = de * _silu_grad(y)
        dx = dy * wv[3:4, :]
        for i in range(CONV_W - 1):
            dx = dx + pltpu.roll(dy, n - (CONV_W - 1 - i), 0) * wv[i:i + 1, :]
        dx_ref[...] = dx[:tm, :].astype(dx_ref.dtype)
        dyc = dy[:tm, :]
        dwp = jnp.concatenate([_colsum(dyc * sh[i][:tm, :]) for i in range(CONV_W)], axis=0)

        @pl.when(pl.program_id(0) == 0)
        def _():
            dw_ref[...] = jnp.zeros_like(dw_ref)
        dw_ref[...] += dwp

    return pl.pallas_call(
        body, name="conv_bwd", grid=(nt,),
        in_specs=[pl.BlockSpec((t, c), lambda i: (0, 0)), pl.BlockSpec((t, c), lambda i: (0, 0)),
                  pl.BlockSpec(w.shape, lambda i: (0, 0))],
        out_specs=[pl.BlockSpec((tm, c), lambda i: (i, 0)), pl.BlockSpec(w.shape, lambda i: (0, 0))],
        out_shape=[jax.ShapeDtypeStruct((t, c), BF16), jax.ShapeDtypeStruct(w.shape, F32)],
        compiler_params=_cparams(("arbitrary",)),
    )(zbig, dact, w)


LM = 256
HI = lax.Precision.HIGHEST


def _logsig(x):
    return jnp.minimum(x, 0.0) - jnp.log(1.0 + jnp.exp(-jnp.abs(x)))


def _tri(n, lower):
    r = lax.broadcasted_iota(jnp.int32, (n, n), 0)
    c = lax.broadcasted_iota(jnp.int32, (n, n), 1)
    return (r >= c) if lower else (r <= c)


def _f32dot(a, b):
    return lax.dot_general(a, b, (((1,), (0,)), ((), ())), precision=HI, preferred_element_type=F32)


def _tri_dot(a, b, a_is_tri):
    tri = (a if a_is_tri else b).astype(BF16)
    parts = _split3(b if a_is_tri else a)
    outs = [_dot(tri, p, 1, 0) if a_is_tri else _dot(p, tri, 1, 0) for p in parts]
    return (outs[0] + outs[1]) + outs[2]


def _mlstm_decays(zs_ref, zsr_ref, bc_ref, br_ref):
    l = LM
    lf_c = _logsig(zs_ref[:, 0:2 * NH_M] + bc_ref[...])
    lf_r = _logsig(zsr_ref[...] + br_ref[...])
    return _tri_dot(_tri(l, True), lf_c, True), _tri_dot(lf_r, _tri(l, False), False)


def _mlstm_chunk(h, q_ref, k_ref, v_ref, zs_ref, zsr_ref, bc_ref, br_ref, c_prev, m_prev, decays):
    l = LM
    q = q_ref[:, h * DK_M:(h + 1) * DK_M].astype(F32) * (DK_M ** -0.5)
    k = k_ref[:, h * DK_M:(h + 1) * DK_M]
    v = v_ref[:, h * DV_M:(h + 1) * DV_M]
    lane = lax.broadcasted_iota(jnp.int32, (l, DV_M), 1)
    v1 = jnp.concatenate([v, (lane == 0).astype(v.dtype)], axis=1)
    zs, zsr = zs_ref[...], zsr_ref[...]
    li_c = zs[:, h:h + 1] + bc_ref[:, h:h + 1]
    fp_c = zs[:, NH_M + h:NH_M + h + 1] + bc_ref[:, NH_M + h:NH_M + h + 1]
    li_r = zsr[h:h + 1, :] + br_ref[h:h + 1, :]
    fp_r = zsr[NH_M + h:NH_M + h + 1, :] + br_ref[NH_M + h:NH_M + h + 1, :]
    low = _tri(l, True)
    b_c = decays[0][:, NH_M + h:NH_M + h + 1]
    b_r = decays[1][NH_M + h:NH_M + h + 1, :]
    g = b_r[:, l - 1:l]
    dmat = jnp.where(low, b_c - b_r + li_r, -jnp.inf)
    inter = b_c + m_prev
    m_t = jnp.maximum(inter, jnp.max(dmat, axis=1, keepdims=True))
    w_inter = jnp.exp(inter - m_t)
    amat = jnp.exp(dmat - m_t)
    s = _dot(q, k, 1, 1)
    p = amat * s
    qc = _dot(q, c_prev, 1, 0)
    qc_w = w_inter * qc
    num1 = qc_w + _dot(p, v1, 1, 0)
    den = num1[:, DV_M:DV_M + 1]
    mx = jnp.maximum(jnp.abs(den), jnp.exp(-m_t))
    hh = num1[:, :DV_M] / mx
    a_c = g - b_c + li_c
    return dict(q=q, k=k, v1=v1, fp_c=fp_c, fp_r=fp_r, b_c=b_c, g=g, m_t=m_t, w_inter=w_inter, amat=amat, s=s, p=p,
                qc_w=qc_w, den=den, mx=mx, hh=hh, a_c=a_c)


def _mlstm_fwd(qk, zbig, zs, zsr, bc, br, gm):
    t = zs.shape[0]
    l = LM
    nc = t // l
    dm = NH_M * DV_M

    def body(q_ref, k_ref, v_ref, o_ref, zs_ref, zsr_ref, bc_ref, br_ref, gm_ref, y_ref, cst_ref, mst_ref, c_scr, m_scr):
        @pl.when(pl.program_id(0) == 0)
        def _():
            c_scr[...] = jnp.zeros_like(c_scr)
            m_scr[...] = jnp.zeros_like(m_scr)

        cst_ref[...] = c_scr[...]
        mst_ref[...] = m_scr[...]
        ys = []
        decays = _mlstm_decays(zs_ref, zsr_ref, bc_ref, br_ref)
        for h in range(NH_M):
            c_prev = c_scr[h]
            m_prev = m_scr[h:h + 1, 0:1]
            r = _mlstm_chunk(h, q_ref, k_ref, v_ref, zs_ref, zsr_ref, bc_ref, br_ref, c_prev, m_prev, decays)
            hh = r["hh"]
            gh = gm_ref[:, h * DV_M:(h + 1) * DV_M]
            hn = hh * lax.rsqrt(jnp.mean(hh * hh, axis=-1, keepdims=True) + EPS) * gh
            og = o_ref[:, h * DV_M:(h + 1) * DV_M].astype(F32)
            ys.append(hn * _sigmoid(og))
            m_new = jnp.maximum(r["g"] + m_prev, jnp.max(r["a_c"], axis=0, keepdims=True))
            decay = jnp.exp(r["g"] + m_prev - m_new)
            wk = r["k"].astype(F32) * jnp.exp(r["a_c"] - m_new)
            c_scr[h] = decay * c_prev + _dot(wk, r["v1"], 0, 0)
            m_scr[h:h + 1, :] = jnp.broadcast_to(m_new, (1, 128))
        y_ref[...] = jnp.concatenate(ys, axis=1).astype(y_ref.dtype)

    return pl.pallas_call(
        body, name="mlstm_fwd", grid=(nc,),
        in_specs=[pl.BlockSpec((l, NH_M * DK_M), lambda i: (i, 0)), pl.BlockSpec((l, NH_M * DK_M), lambda i: (i, 1)),
                  pl.BlockSpec((l, dm), lambda i: (i, 1)), pl.BlockSpec((l, dm), lambda i: (i, 2)),
                  pl.BlockSpec((l, 128), lambda i: (i, 0)), pl.BlockSpec((8, l), lambda i: (0, i)),
                  pl.BlockSpec((1, 8), lambda i: (0, 0)), pl.BlockSpec((8, 1), lambda i: (0, 0)),
                  pl.BlockSpec((1, dm), lambda i: (0, 0))],
        out_specs=[pl.BlockSpec((l, dm), lambda i: (i, 0)), pl.BlockSpec((None, NH_M, DK_M, 2 * DV_M), lambda i: (i, 0, 0, 0)),
                   pl.BlockSpec((None, 8, 128), lambda i: (i, 0, 0))],
        out_shape=[jax.ShapeDtypeStruct((t, dm), BF16), jax.ShapeDtypeStruct((nc, NH_M, DK_M, 2 * DV_M), F32),
                   jax.ShapeDtypeStruct((nc, 8, 128), F32)],
        scratch_shapes=[pltpu.VMEM((NH_M, DK_M, 2 * DV_M), F32), pltpu.VMEM((8, 128), F32)],
        compiler_params=_cparams(("arbitrary",)),
    )(qk, qk, zbig, zbig, zs, zsr, bc, br, gm)


def _mlstm_bwd(qk, zbig, zs, zsr, bc, br, gm, cst, mst, dycat):
    t = zs.shape[0]
    l = LM
    nc = t // l
    dm = NH_M * DV_M

    def body(q_ref, k_ref, v_ref, o_ref, zs_ref, zsr_ref, bc_ref, br_ref, gm_ref, cst_ref, mst_ref, cnx_ref, mnx_ref,
             dy_ref, dqk_ref, dv_ref, do_ref, dzs_ref, dzr_ref, dgm_ref, dc_scr):
        @pl.when(pl.program_id(0) == 0)
        def _():
            dc_scr[...] = jnp.zeros_like(dc_scr)
            dgm_ref[...] = jnp.zeros_like(dgm_ref)

        lane = lax.broadcasted_iota(jnp.int32, (l, 128), 1)
        upper, lower = _tri(l, False), _tri(l, True)
        db_all, sig_c, carries = jnp.zeros((l, 128), F32), jnp.zeros((l, 128), F32), jnp.zeros((1, 128), F32)
        decays = _mlstm_decays(zs_ref, zsr_ref, bc_ref, br_ref)
        dzr_rows = [None] * 8
        dvs, dos, dgs, dqs, dks = [], [], [], [], []
        dzs = jnp.zeros((l, 128), F32)
        for h in range(NH_M):
            c_prev = cst_ref[h]
            m_prev = mst_ref[h:h + 1, 0:1]
            r = _mlstm_chunk(h, q_ref, k_ref, v_ref, zs_ref, zsr_ref, bc_ref, br_ref, c_prev, m_prev, decays)
            hh, mx, den, m_t, v1, amat = r["hh"], r["mx"], r["den"], r["m_t"], r["v1"], r["amat"]
            gh = gm_ref[:, h * DV_M:(h + 1) * DV_M]
            rs = lax.rsqrt(jnp.mean(hh * hh, axis=-1, keepdims=True) + EPS)
            xh = hh * rs
            sg = _sigmoid(o_ref[:, h * DV_M:(h + 1) * DV_M].astype(F32))
            dyh = dy_ref[:, h * DV_M:(h + 1) * DV_M]
            dos.append(dyh * xh * gh * sg * (1.0 - sg))
            dhn = dyh * sg
            dgs.append(_colsum(dhn * xh))
            dxh = dhn * gh
            dh = rs * (dxh - xh * jnp.mean(dxh * xh, axis=-1, keepdims=True))
            g1 = dh / mx
            hd = jnp.sum(hh * dh, axis=-1, keepdims=True)
            dden = jnp.where(jnp.abs(den) > jnp.exp(-m_t), -hd / mx * jnp.sign(den), 0.0)
            g256 = jnp.concatenate([g1, jnp.where(lane == 0, dden, 0.0)], axis=1)
            dc_h = dc_scr[h]
            ea = jnp.exp(r["a_c"])
            dp = _dot(g256, v1, 1, 1)
            ds = dp * amat
            dqs.append((r["w_inter"] * _dot(g256, c_prev, 1, 1) + _dot(ds, r["k"], 1, 0)) * (DK_M ** -0.5))
            dks.append(_dot(ds, r["q"], 0, 0) + ea * _dot(v1, dc_h, 1, 1))
            dv_st = ea * _dot(r["k"], dc_h, 1, 0)
            dv1 = _dot(r["p"], g256, 0, 0) + dv_st
            dvs.append(dv1[:, :DV_M])
            wmat = dp * r["p"]
            c_in = _colsum(wmat)
            c_st = jnp.sum(v1.astype(F32) * dv_st, axis=-1, keepdims=True)
            r_t = jnp.sum(wmat, axis=1, keepdims=True) + jnp.sum(g256 * r["qc_w"], axis=-1, keepdims=True)
            db = r_t - c_st
            carry = jnp.exp(mnx_ref[h:h + 1, 0:1]) * jnp.sum(
                jnp.sum(dc_h * cnx_ref[h], axis=1, keepdims=True), axis=0, keepdims=True)
            db_all = db_all + jnp.where(lane == NH_M + h, db, 0.0)
            sig_c = sig_c + jnp.where(lane == NH_M + h, _sigmoid(-r["fp_c"]), 0.0)
            carries = carries + jnp.where(lane[0:1, :] == NH_M + h, carry, 0.0)
            dzs = dzs + jnp.where(lane == h, c_st, 0.0)
            dzr_rows[h] = c_in
            dzr_rows[NH_M + h] = _sigmoid(-r["fp_r"])
            wq = r["q"] * jnp.exp(r["b_c"] - m_t)
            dc_scr[h] = jnp.exp(r["g"]) * dc_h + _dot(wq, g256, 0, 0)
        dzs = dzs + (_tri_dot(upper, db_all, True) + carries) * sig_c
        c_in4 = jnp.concatenate(dzr_rows[:NH_M], axis=0)
        dlf_r4 = -_tri_dot(c_in4, lower, False)
        dzr_rows = dzr_rows[:NH_M] + [dlf_r4[h:h + 1, :] * dzr_rows[NH_M + h] for h in range(NH_M)]
        dqk_ref[...] = jnp.concatenate(dqs + dks, axis=1)
        dv_ref[...] = jnp.concatenate(dvs, axis=1).astype(dv_ref.dtype)
        do_ref[...] = jnp.concatenate(dos, axis=1).astype(do_ref.dtype)
        dzs_ref[...] = dzs
        dzr_ref[...] = jnp.concatenate(dzr_rows, axis=0)
        dgm_ref[...] += jnp.concatenate(dgs, axis=1)

    rev = lambda i: nc - 1 - i
    nxt = lambda i: jnp.minimum(nc - i, nc - 1)
    return pl.pallas_call(
        body, name="mlstm_bwd", grid=(nc,),
        in_specs=[pl.BlockSpec((l, NH_M * DK_M), lambda i: (rev(i), 0)), pl.BlockSpec((l, NH_M * DK_M), lambda i: (rev(i), 1)),
                  pl.BlockSpec((l, dm), lambda i: (rev(i), 1)), pl.BlockSpec((l, dm), lambda i: (rev(i), 2)),
                  pl.BlockSpec((l, 128), lambda i: (rev(i), 0)), pl.BlockSpec((8, l), lambda i: (0, rev(i))),
                  pl.BlockSpec((1, 8), lambda i: (0, 0)), pl.BlockSpec((8, 1), lambda i: (0, 0)),
                  pl.BlockSpec((1, dm), lambda i: (0, 0)),
                  pl.BlockSpec((None, NH_M, DK_M, 2 * DV_M), lambda i: (rev(i), 0, 0, 0)),
                  pl.BlockSpec((None, 8, 128), lambda i: (rev(i), 0, 0)),
                  pl.BlockSpec((None, NH_M, DK_M, 2 * DV_M), lambda i: (nxt(i), 0, 0, 0)),
                  pl.BlockSpec((None, 8, 128), lambda i: (nxt(i), 0, 0)),
                  pl.BlockSpec((l, dm), lambda i: (rev(i), 0))],
        out_specs=[pl.BlockSpec((l, dm), lambda i: (rev(i), 0)),
                   pl.BlockSpec((l, dm), lambda i: (rev(i), 0)), pl.BlockSpec((l, dm), lambda i: (rev(i), 0)),
                   pl.BlockSpec((l, 128), lambda i: (rev(i), 0)), pl.BlockSpec((8, l), lambda i: (0, rev(i))),
                   pl.BlockSpec((1, dm), lambda i: (0, 0))],
        out_shape=[jax.ShapeDtypeStruct((t, dm), F32),
                   jax.ShapeDtypeStruct((t, dm), BF16), jax.ShapeDtypeStruct((t, dm), BF16),
                   jax.ShapeDtypeStruct((t, 128), F32), jax.ShapeDtypeStruct((8, t), F32),
                   jax.ShapeDtypeStruct((1, dm), F32)],
        scratch_shapes=[pltpu.VMEM((NH_M, DK_M, 2 * DV_M), F32)],
        compiler_params=_cparams(("arbitrary",)),
    )(qk, qk, zbig, zbig, zs, zsr, bc, br, gm, cst, mst, cst, mst, dycat)


def _fox_cumsum(zsr, bf_r):
    t = zsr.shape[1]
    cw = _pick(t, (512, 256))

    def body(z_ref, b_ref, c_ref):
        up = _tri(cw, False).astype(F32)
        carry = jnp.zeros((NH_F, 1), F32)
        for j in range(t // cw):
            cs = _f32dot(_logsig(z_ref[:, j * cw:(j + 1) * cw] + b_ref[...]), up) + carry
            c_ref[:, j * cw:(j + 1) * cw] = cs
            carry = cs[:, cw - 1:cw]

    return pl.pallas_call(
        body, name="fox_cumsum", grid=(1,),
        in_specs=[pl.BlockSpec((NH_F, t), lambda i: (1, 0)), pl.BlockSpec((NH_F, 1), lambda i: (0, 0))],
        out_specs=pl.BlockSpec((NH_F, t), lambda i: (0, 0)), out_shape=jax.ShapeDtypeStruct((NH_F, t), F32),
        compiler_params=_cparams(("arbitrary",)),
    )(zsr, bf_r)


def _fox_gate_bwd(zsr, bf_r, dc):
    t = zsr.shape[1]
    cw = _pick(t, (512, 256))

    def body(z_ref, b_ref, dc_ref, o_ref):
        low = _tri(cw, True).astype(F32)
        carry = jnp.zeros((NH_F, 1), F32)
        for j in reversed(range(t // cw)):
            sl = slice(j * cw, (j + 1) * cw)
            dlf = _f32dot(dc_ref[:, sl], low) + carry
            o_ref[:, sl] = dlf * _sigmoid(-(z_ref[:, sl] + b_ref[...]))
            carry = dlf[:, 0:1]

    return pl.pallas_call(
        body, name="fox_gate_bwd", grid=(1,),
        in_specs=[pl.BlockSpec((NH_F, t), lambda i: (1, 0)), pl.BlockSpec((NH_F, 1), lambda i: (0, 0)),
                  pl.BlockSpec((NH_F, t), lambda i: (0, 0))],
        out_specs=pl.BlockSpec((NH_F, t), lambda i: (0, 0)), out_shape=jax.ShapeDtypeStruct((NH_F, t), F32),
        compiler_params=_cparams(("arbitrary",)),
    )(zsr, bf_r, dc)


def _causal_mask(n):
    return _tri(n, True)


def _fox_fwd(q, k, v, c_col, c_row, gf):
    nh, t, dh = q.shape
    tq = _pick(t, (512, 256))
    scale = dh ** -0.5

    def body(q_ref, k_ref, v_ref, cc_ref, cr_ref, g_ref, o_ref, lse_ref, y_ref):
        i = pl.program_id(1)
        qv = q_ref[...]
        cq = cc_ref[...]

        def blk(j, carry, masked):
            m, l, acc = carry
            k0 = pl.multiple_of(j * tq, tq)
            kb = k_ref[pl.ds(k0, tq), :]
            vb = v_ref[pl.ds(k0, tq), :]
            s = _dot(qv, kb, 1, 1) * scale + cq - cr_ref[:, pl.ds(k0, tq)]
            if masked:
                s = jnp.where(_causal_mask(tq), s, -jnp.inf)
            m_new = jnp.maximum(m, jnp.max(s, axis=1, keepdims=True))
            alpha = jnp.exp(m - m_new)
            p = jnp.exp(s - m_new)
            return m_new, alpha * l + jnp.sum(p, axis=1, keepdims=True), alpha * acc + _dot(p, vb, 1, 0)

        init = (jnp.full((tq, 1), -jnp.inf, F32), jnp.zeros((tq, 1), F32), jnp.zeros((tq, dh), F32))
        carry = lax.fori_loop(0, i, lambda j, c: blk(j, c, False), init)
        m, l, acc = blk(i, carry, True)
        o = acc / l
        o_ref[...] = o
        lse_ref[...] = m + jnp.log(l)
        y_ref[...] = (o * lax.rsqrt(jnp.mean(o * o, axis=-1, keepdims=True) + EPS) * g_ref[...]).astype(y_ref.dtype)

    full = lambda w: pl.BlockSpec((None, t, w), lambda h, i: (h, 0, 0))
    tile = lambda w: pl.BlockSpec((None, tq, w), lambda h, i: (h, i, 0))
    return pl.pallas_call(
        body, name="fox_fwd", grid=(nh, t // tq),
        in_specs=[tile(dh), full(dh), full(dh), tile(1), pl.BlockSpec((None, 1, t), lambda h, i: (h, 0, 0)),
                  pl.BlockSpec((None, 1, dh), lambda h, i: (h, 0, 0))],
        out_specs=[tile(dh), tile(1), tile(dh)],
        out_shape=[jax.ShapeDtypeStruct((nh, t, dh), F32), jax.ShapeDtypeStruct((nh, t, 1), F32),
                   jax.ShapeDtypeStruct((nh, t, dh), BF16)],
        compiler_params=_cparams(("parallel", "parallel")),
    )(q, k, v, c_col, c_row, gf)


def _fox_norm_bwd(dy, o, gf):
    nh, t, dh = o.shape
    tm = _pick(t, (512, 256))

    def body(dy_ref, o_ref, g_ref, do_ref, dl_ref, dg_ref):
        ov = o_ref[...]
        dx, dg = _rms_bwd_val(dy_ref[...], ov, g_ref[...])
        do_ref[...] = dx
        dl_ref[...] = jnp.sum(dx * ov, axis=-1, keepdims=True)

        @pl.when(pl.program_id(1) == 0)
        def _():
            dg_ref[...] = jnp.zeros_like(dg_ref)
        dg_ref[...] += dg

    tile = lambda w: pl.BlockSpec((None, tm, w), lambda h, i: (h, i, 0))
    gspec = pl.BlockSpec((None, 1, dh), lambda h, i: (h, 0, 0))
    return pl.pallas_call(
        body, name="fox_norm_bwd", grid=(nh, t // tm), in_specs=[tile(dh), tile(dh), gspec],
        out_specs=[tile(dh), tile(1), gspec],
        out_shape=[jax.ShapeDtypeStruct((nh, t, dh), F32), jax.ShapeDtypeStruct((nh, t, 1), F32),
                   jax.ShapeDtypeStruct((nh, 1, dh), F32)],
        compiler_params=_cparams(("parallel", "arbitrary")),
    )(dy, o, gf)


def _fox_bwd(q, k, v, c_col, c_row, do, lse, delta):
    nh, t, dh = q.shape
    tq = _pick(t, (512, 256))
    nq = t // tq
    scale = dh ** -0.5

    def body(q_ref, k_ref, v_ref, cc_ref, cr_ref, do_ref, lse_ref, dl_ref, dq_ref, dk_ref, dv_ref, dc_ref, dcq_ref):
        j = pl.program_id(1)

        @pl.when(j == 0)
        def _():
            dq_ref[...] = jnp.zeros_like(dq_ref)
            dcq_ref[...] = jnp.zeros_like(dcq_ref)

        kb, vb, crb = k_ref[...], v_ref[...], cr_ref[...]

        def blk(i, carry, masked):
            dk, dv, dc = carry
            rows = pl.ds(pl.multiple_of(i * tq, tq), tq)
            qb = q_ref[rows, :]
            dob = do_ref[rows, :].astype(BF16)
            s = _dot(qb, kb, 1, 1) * scale + cc_ref[rows, :] - crb
            if masked:
                s = jnp.where(_causal_mask(tq), s, -jnp.inf)
            p = jnp.exp(s - lse_ref[rows, :])
            dv = dv + _dot(p, dob, 0, 0)
            ds = p * (_dot(dob, vb, 1, 1) - dl_ref[rows, :])
            dc = dc + _colsum(ds)
            dk = dk + _dot(ds, qb, 0, 0) * scale
            dq_ref[rows, :] += _dot(ds, kb, 1, 0) * scale
            dcq_ref[rows, :] += jnp.sum(ds, axis=1, keepdims=True)
            return dk, dv, dc

        init = (jnp.zeros((tq, dh), F32), jnp.zeros((tq, dh), F32), jnp.zeros((1, tq), F32))
        carry = blk(j, init, True)
        dk, dv, dc = lax.fori_loop(j + 1, nq, lambda i, c: blk(i, c, False), carry)
        dk_ref[...] = dk
        dv_ref[...] = dv
        dc_ref[...] = -dc

    full = lambda w: pl.BlockSpec((None, t, w), lambda h, j: (h, 0, 0))
    tile = lambda w: pl.BlockSpec((None, tq, w), lambda h, j: (h, j, 0))
    crow = pl.BlockSpec((None, 1, tq), lambda h, j: (h, 0, j))
    return pl.pallas_call(
        body, name="fox_bwd", grid=(nh, nq),
        in_specs=[full(dh), tile(dh), tile(dh), full(1), crow, full(dh), full(1), full(1)],
        out_specs=[full(dh), tile(dh), tile(dh), crow, full(1)],
        out_shape=[jax.ShapeDtypeStruct((nh, t, dh), F32)] * 3 + [jax.ShapeDtypeStruct((nh, 1, t), F32),
                                                                jax.ShapeDtypeStruct((nh, t, 1), F32)],
        compiler_params=_cparams(("parallel", "arbitrary")),
    )(q, k, v, c_col, c_row, do, lse, delta)


AUG = 64


def _split3(c):
    hi = c.astype(BF16).astype(F32)
    r1 = c - hi
    mid = r1.astype(BF16).astype(F32)
    return hi, mid, r1 - mid


def _fox_prep(zbig, ct):
    t = zbig.shape[0]
    tm = _pick(t, (512, 256))

    def body(q_ref, k_ref, v_ref, c_ref, qo_ref, ko_ref, vo_ref):
        lane = lax.broadcasted_iota(jnp.int32, (tm, AUG), 1)
        qv, kv, vv, cv = q_ref[...], k_ref[...], v_ref[...], c_ref[...]
        one = (lane == 0).astype(BF16)
        for h in range(NH_F):
            hi, mid, lo = _split3(cv[:, h:h + 1])
            aq = jnp.where(lane == 0, hi, jnp.where(lane == 1, mid, jnp.where(lane == 2, lo, jnp.where(lane < 6, 1.0, 0.0))))
            ak = jnp.where(lane < 3, 1.0, jnp.where(lane == 3, -hi, jnp.where(lane == 4, -mid, jnp.where(lane == 5, -lo, 0.0))))
            sl = slice(h * DH_F, (h + 1) * DH_F)
            qo_ref[h] = jnp.concatenate([qv[:, sl] * (DH_F ** -0.5), aq.astype(BF16)], axis=1).astype(BF16)
            ko_ref[h] = jnp.concatenate([kv[:, sl], ak.astype(BF16)], axis=1)
            vo_ref[h] = jnp.concatenate([vv[:, sl], one], axis=1)

    ospec = pl.BlockSpec((NH_F, tm, 128), lambda i: (0, i, 0))
    return pl.pallas_call(
        body, name="fox_prep", grid=(t // tm,),
        in_specs=[pl.BlockSpec((tm, 512), lambda i: (i, 3)), pl.BlockSpec((tm, 512), lambda i: (i, 4)),
                  pl.BlockSpec((tm, 512), lambda i: (i, 5)), pl.BlockSpec((tm, NH_F), lambda i: (i, 0))],
        out_specs=[ospec] * 3, out_shape=[jax.ShapeDtypeStruct((NH_F, t, 128), BF16)] * 3,
        compiler_params=_cparams(("parallel",)),
    )(zbig, zbig, zbig, ct)


def _fox_fwd2(qa, ka, va, gf, plan=None):
    nh, t, _ = qa.shape
    tq = _pick(t, (512, 256))
    nq = t // tq
    group = 2
    host = _Hosted(plan, 4, 3)

    def body(*refs):
        (q_ref, k_ref, v_ref, g_ref), (y_ref, o_ref, lse_ref), _, prefs = host.split(refs)
        i = pl.program_id(0)
        host.run(0, i == 0, prefs)
        host.run(1, i == max(nq - 2, 0), prefs)
        lane = lax.broadcasted_iota(jnp.int32, (tq, 128), 1)
        ys, os_ = [], []
        lse_all = jnp.zeros((tq, 128), F32)
        for h0 in range(0, nh, group):
            heads = range(h0, h0 + group)
            qvs = [q_ref[h] for h in heads]

            def blk(j, carry, masked, heads=heads, qvs=qvs):
                k0 = pl.multiple_of(j * tq, tq)
                out = []
                for (m, acc), h, qv in zip(carry, heads, qvs):
                    s = lax.dot_general(qv, k_ref[h, pl.ds(k0, tq), :], (((1,), (1,)), ((), ())), preferred_element_type=F32)
                    if masked:
                        s = jnp.where(_causal_mask(tq), s, -jnp.inf)
                    m_new = jnp.maximum(m, jnp.max(s, axis=1, keepdims=True))
                    p = jnp.exp(s - m_new).astype(BF16)
                    pv = lax.dot_general(p, v_ref[h, pl.ds(k0, tq), :], (((1,), (0,)), ((), ())), preferred_element_type=F32)
                    out.append((m_new, jnp.exp(m - m_new) * acc + pv))
                return tuple(out)

            init = tuple((jnp.full((tq, 1), -jnp.inf, F32), jnp.zeros((tq, 128), F32)) for _ in heads)
            carry = lax.fori_loop(0, i, lambda j, c: blk(j, c, False), init)
            for (m, acc), h in zip(blk(i, carry, True), heads):
                l = acc[:, DH_F:DH_F + 1]
                o = acc[:, :DH_F] / l
                os_.append(o)
                gh = g_ref[:, h * DH_F:(h + 1) * DH_F]
                ys.append(o * lax.rsqrt(jnp.mean(o * o, axis=-1, keepdims=True) + EPS) * gh)
                lse_all = lse_all + jnp.where(lane == h, m + jnp.log(l), 0.0)
        y_ref[...] = jnp.concatenate(ys, axis=1).astype(y_ref.dtype)
        o_ref[...] = jnp.concatenate(os_, axis=1)
        lse_ref[...] = lse_all
        host.run(2, i == nq - 1, prefs)

    full = pl.BlockSpec((nh, t, 128), lambda i: (0, 0, 0))
    hc = host.call_args()
    res = pl.pallas_call(
        body, name="fox_fwd", grid=(nq,),
        in_specs=[pl.BlockSpec((nh, tq, 128), lambda i: (0, i, 0)), full, full, pl.BlockSpec((1, nh * DH_F), lambda i: (0, 0))]
        + hc["in_specs"],
        out_specs=[pl.BlockSpec((tq, nh * DH_F), lambda i: (i, 0)), pl.BlockSpec((tq, nh * DH_F), lambda i: (i, 0)),
                   pl.BlockSpec((tq, 128), lambda i: (i, 0))] + hc["out_specs"],
        out_shape=[jax.ShapeDtypeStruct((t, nh * DH_F), BF16), jax.ShapeDtypeStruct((t, nh * DH_F), F32),
                   jax.ShapeDtypeStruct((t, 128), F32)] + hc["out_shape"],
        scratch_shapes=hc["scratch"], input_output_aliases=hc["aliases"], compiler_params=_cparams(("arbitrary",)),
    )(qa, ka, va, gf, *hc["args"])
    return res[:3], res[3:]


def _fox_bwd_prep(dycat, o, gf):
    t = o.shape[0]
    tm = _pick(t, (512, 256))

    def body(dy_ref, o_ref, g_ref, do_ref, dl_ref, dg_ref):
        lane = lax.broadcasted_iota(jnp.int32, (tm, 128), 1)
        dyv, ov, gv = dy_ref[...], o_ref[...], g_ref[...]
        dgs = []
        dl = jnp.zeros((tm, 128), F32)
        pad = jnp.zeros((tm, AUG), BF16)
        for h in range(NH_F):
            sl = slice(h * DH_F, (h + 1) * DH_F)
            dx, dg = _rms_bwd_val(dyv[:, sl], ov[:, sl], gv[:, sl])
            dgs.append(dg)
            do_ref[h] = jnp.concatenate([dx.astype(BF16), pad], axis=1)
            dl = dl + jnp.where(lane == h, jnp.sum(dx * ov[:, sl], axis=-1, keepdims=True), 0.0)
        dl_ref[...] = dl

        @pl.when(pl.program_id(0) == 0)
        def _():
            dg_ref[...] = jnp.zeros_like(dg_ref)
        dg_ref[...] += jnp.concatenate(dgs, axis=1)

    return pl.pallas_call(
        body, name="fox_bwd_prep", grid=(t // tm,),
        in_specs=[pl.BlockSpec((tm, 512), lambda i: (i, 1)), pl.BlockSpec((tm, 512), lambda i: (i, 0)),
                  pl.BlockSpec((1, 512), lambda i: (0, 0))],
        out_specs=[pl.BlockSpec((NH_F, tm, 128), lambda i: (0, i, 0)), pl.BlockSpec((tm, 128), lambda i: (i, 0)),
                   pl.BlockSpec((1, 512), lambda i: (0, 0))],
        out_shape=[jax.ShapeDtypeStruct((NH_F, t, 128), BF16), jax.ShapeDtypeStruct((t, 128), F32),
                   jax.ShapeDtypeStruct((1, 512), F32)],
        compiler_params=_cparams(("arbitrary",)),
    )(dycat, o, gf)


def _fox_bwd2(qa, ka, va, doa, lse, delta):
    nh, t, _ = qa.shape
    tq = _pick(t, (512, 256))
    nq = t // tq

    group = 2

    def body(q_ref, k_ref, v_ref, do_ref, lse_ref, dl_ref, dq_ref, dk_ref, dv_ref):
        hp, j = pl.program_id(0), pl.program_id(1)

        @pl.when(j == 0)
        def _():
            dq_ref[...] = jnp.zeros_like(dq_ref)

        lane = lax.broadcasted_iota(jnp.int32, (tq, 128), 1)

        def blk(i, carry, masked):
            rows = pl.ds(pl.multiple_of(i * tq, tq), tq)
            lse_t, dl_t = lse_ref[rows, :], dl_ref[rows, :]
            out = []
            for g, (dk, dv) in enumerate(carry):
                h = hp * group + g
                kb, vb = k_ref[g], v_ref[g]
                qb, dob = q_ref[g, rows, :], do_ref[g, rows, :]
                lse_h = jnp.sum(jnp.where(lane == h, lse_t, 0.0), axis=1, keepdims=True)
                dl_h = jnp.sum(jnp.where(lane == h, dl_t, 0.0), axis=1, keepdims=True)
                s = lax.dot_general(qb, kb, (((1,), (1,)), ((), ())), preferred_element_type=F32)
                if masked:
                    s = jnp.where(_causal_mask(tq), s, -jnp.inf)
                p = jnp.exp(s - lse_h)
                dp = lax.dot_general(dob, vb, (((1,), (1,)), ((), ())), preferred_element_type=F32)
                ds = (p * (dp - dl_h)).astype(BF16)
                dv = dv + lax.dot_general(p.astype(BF16), dob, (((0,), (0,)), ((), ())), preferred_element_type=F32)
                dk = dk + lax.dot_general(ds, qb, (((0,), (0,)), ((), ())), preferred_element_type=F32)
                dq_ref[g, rows, :] += lax.dot_general(ds, kb, (((1,), (0,)), ((), ())), preferred_element_type=F32)
                out.append((dk, dv))
            return tuple(out)

        init = tuple((jnp.zeros((tq, 128), F32), jnp.zeros((tq, 128), F32)) for _ in range(group))
        carry = blk(j, init, True)
        carry = lax.fori_loop(j + 1, nq, lambda i, c: blk(i, c, False), carry)
        for g, (dk, dv) in enumerate(carry):
            dk_ref[g] = dk
            dv_ref[g] = dv

    full = pl.BlockSpec((group, t, 128), lambda h, j: (h, 0, 0))
    tile = pl.BlockSpec((group, tq, 128), lambda h, j: (h, j, 0))
    cols = pl.BlockSpec((t, 128), lambda h, j: (0, 0))
    return pl.pallas_call(
        body, name="fox_bwd", grid=(nh // group, nq), in_specs=[full, tile, tile, full, cols, cols],
        out_specs=[full, tile, tile], out_shape=[jax.ShapeDtypeStruct((nh, t, 128), F32)] * 3,
        compiler_params=_cparams(("parallel", "arbitrary")),
    )(qa, ka, va, doa, lse, delta)


def _fox_bwd_post(dqa, dka, dva):
    nh, t, _ = dqa.shape
    tm = _pick(t, (512, 256))

    def body(dq_ref, dk_ref, dv_ref, oq_ref, ok_ref, ov_ref, dc_ref):
        lane = lax.broadcasted_iota(jnp.int32, (tm, 128), 1)
        dc = jnp.zeros((tm, 128), F32)
        qs, ks, vs = [], [], []
        for h in range(nh):
            dq, dk = dq_ref[h], dk_ref[h]
            qs.append(dq[:, :DH_F] * (DH_F ** -0.5))
            ks.append(dk[:, :DH_F])
            vs.append(dv_ref[h][:, :DH_F])
            dc = dc + jnp.where(lane == h, dq[:, DH_F:DH_F + 1] - dk[:, DH_F + 3:DH_F + 4], 0.0)
        oq_ref[...] = jnp.concatenate(qs, axis=1).astype(BF16)
        ok_ref[...] = jnp.concatenate(ks, axis=1).astype(BF16)
        ov_ref[...] = jnp.concatenate(vs, axis=1).astype(BF16)
        dc_ref[...] = dc

    ispec = pl.BlockSpec((nh, tm, 128), lambda i: (0, i, 0))
    ospec = pl.BlockSpec((tm, nh * DH_F), lambda i: (i, 0))
    return pl.pallas_call(
        body, name="fox_bwd_post", grid=(t // tm,), in_specs=[ispec] * 3,
        out_specs=[ospec] * 3 + [pl.BlockSpec((tm, 128), lambda i: (i, 0))],
        out_shape=[jax.ShapeDtypeStruct((t, nh * DH_F), BF16)] * 3 + [jax.ShapeDtypeStruct((t, 128), F32)],
        compiler_params=_cparams(("parallel",)),
    )(dqa, dka, dva)


W_BIG = 6 * 512
IN_OFF = (0, 512, 1024, 1544, 2056, 2568)
IN_GATES = (1536, 3080)


def _heads(a, nh):
    t = a.shape[0]
    return a.reshape(t, nh, -1).transpose(1, 0, 2)


def _unheads(a):
    nh, t, dh = a.shape
    return a.transpose(1, 0, 2).reshape(t, nh * dh)


FFN1 = ("ffn1_w_gate", "ffn1_w_up", "ffn1_w_down")
REST = ("w_in", "w_out", "ffn2_w_gate", "ffn2_w_up", "ffn2_w_down", "w_ple_gate", "w_ple_proj")
SPLIT = {n: 1 if n == "w_in" else 0 for n in FFN1 + REST}


def _rs_partials(names, gw, c_idx):
    wire = [_cast_other_half("rs_cast_" + n, gw[n], c_idx, SPLIT[n]) for n in names]
    swapped = _swap("rs_swap_" + names[0], wire)
    return [_add_my_half("rs_add_" + n, gw[n], r, c_idx, SPLIT[n]) for n, r in zip(names, swapped)]


def _local_step(x, p, tgt, sp, wg1, wu1, wd1, rest_slots, c_idx, place):
    t, d = x.shape
    slot = dict(zip(REST + ("conv_qk",), rest_slots))
    (h1, xn1, g1, u1), (w_in, conv_w) = _ffn_fwd(
        "ffn1", x, sp["ffn1_norm"], wg1, wu1, wd1, plan=_gather_plan([slot["w_in"], slot["conv_qk"]], [SPLIT["w_in"], None]))
    w_in, conv_w = w_in.reshape(-1, d), _from_chip_blocks(conv_w)
    w_big = jnp.concatenate([w_in[o:o + 512] for o in IN_OFF], axis=0)
    w_small = jnp.concatenate([w_in[IN_GATES[0]:IN_GATES[0] + 8], w_in[IN_GATES[1]:IN_GATES[1] + 8],
                               jnp.zeros((112, d), w_in.dtype)], axis=0)
    u, zbig = _norm_mm("in_big", h1, sp["mix_norm"], w_big, True, BF16)
    zs = _mm_nt("in_small", u, w_small, tm=1024, tk=1024)
    zsr = zs.T
    qk_act = _conv_fwd(zbig, conv_w)
    bm_c, bf_c = sp["b_mlstm_gates"], sp["b_fox_f"]
    y_m, cst, mst = _mlstm_fwd(qk_act, zbig, zs, zsr, bm_c, bm_c.T, sp["mlstm_out_norm"])
    c = _fox_cumsum(zsr, bf_c.T)
    qa, ka, va = _fox_prep(zbig, c.T)
    (y_ft, o_f, lse), late = _fox_fwd2(qa, ka, va, sp["fox_out_norm"],
                                       plan=_gather_plan([slot[n] for n in REST[1:]], [SPLIT[n] for n in REST[1:]]))
    full = dict(zip(REST[1:], late))
    w_out, w_pg = (full[n].reshape(-1, d) for n in ("w_out", "w_ple_gate"))
    wg2, wu2, wd2 = full["ffn2_w_gate"], full["ffn2_w_up"], full["ffn2_w_down"]
    w_pp = _from_chip_blocks(full["w_ple_proj"])
    tm = _pick(t, (1024, 512, 256))
    h2 = _mm("out_proj", [(y_m, (tm, 512), lambda i, j, k: (i, 0), w_out, (512, d), lambda i, j, k: (0, 0)),
                          (y_ft, (tm, 512), lambda i, j, k: (i, 0), w_out, (512, d), lambda i, j, k: (1, 0))],
             (t, d), (tm, d), lambda i, j, k: (i, 0), (t // tm, 1, 1), 2, res=h1)
    (h3, xn2, g2, u2), _ = _ffn_fwd("ffn2", h2, sp["ffn2_norm"], wg2, wu2, wd2)
    hn3, gate_pre = _norm_mm("ple_gate", h3, sp["ple_gate_norm"], w_pg, False, F32)
    pp = _mm_nn("ple_proj", p, w_pp, tm=1024)

    def head_fn(h3_t, gp_t, pp_t, tgt_t, g_pp, g_fin):
        gate = _sigmoid(gp_t)
        ppn = _rms_fwd_val(pp_t, g_pp)
        h4 = h3_t + gate * ppn
        err = _rms_fwd_val(h4, g_fin) - tgt_t
        loss = 0.5 * jnp.sum(jnp.mean(err * err, axis=-1, keepdims=True), axis=0, keepdims=True)
        dh4, dg_fin = _rms_bwd_val(err * (1.0 / d), h4, g_fin)
        dpp, dg_pp = _rms_bwd_val(dh4 * gate, pp_t, g_pp)
        dgp = dh4 * ppn * gate * (1.0 - gate)
        return dh4, dgp, dpp, jnp.broadcast_to(loss, (1, 128)), dg_fin, dg_pp

    dh4, dgp, dpp, loss_part, dg_fin, dg_pp = _rowwise(
        "loss_head", head_fn, [h3, gate_pre, pp, tgt], [sp["ple_proj_norm"], sp["final_norm"]],
        [(d, F32), (d, BF16), (d, BF16)], [((1, 128), F32), ((1, d), F32), ((1, d), F32)])
    gw, gs = {}, {"final_norm": dg_fin, "ple_proj_norm": dg_pp}
    gw["w_ple_gate"] = _mm_tn("d_w_pg", hn3, dgp, tm=1024, tn=1024)
    gw["w_ple_proj"] = _mm_tn("d_w_pp", p, dpp, tn=1024)
    dhn3 = _mm_nt("d_hn3", dgp, w_pg, tm=1024, tn=1024, tk=1024)

    def res_norm_bwd(dn_t, h_t, dres_t, g):
        dx, dg = _rms_bwd_val(dn_t, h_t, g)
        return dres_t + dx, dg

    dh3, gs["ple_gate_norm"] = _rowwise("ple_norm_bwd", res_norm_bwd, [dhn3, h3, dh4], [sp["ple_gate_norm"]],
                                        [(d, F32)], [((1, d), F32)])
    (dh2, gs["ffn2_norm"], gw["ffn2_w_gate"], gw["ffn2_w_up"], gw["ffn2_w_down"]), _ = _ffn_bwd(
        "ffn2", dh3, h2, sp["ffn2_norm"], xn2, g2, u2, wg2, wu2, wd2)
    dycat = _mm_nt("d_ycat", dh2, w_out, tm=1024, tn=1024, tk=1024)
    gw["w_out"] = jnp.concatenate([_mm_tn("d_w_out_m", y_m, dh2, tn=1024), _mm_tn("d_w_out_f", y_ft, dh2, tn=1024)], axis=0)
    doa, delta, gs["fox_out_norm"] = _fox_bwd_prep(dycat, o_f, sp["fox_out_norm"])
    dq_f, dk_f, dv_f, dct = _fox_bwd_post(*_fox_bwd2(qa, ka, va, doa, lse, delta))
    dfp = _fox_gate_bwd(zsr, bf_c.T, dct[:, :NH_F].T)
    dact, dv_m, do_m, dzs_m, dzr_m, gs["mlstm_out_norm"] = _mlstm_bwd(
        qk_act, zbig, zs, zsr, bm_c, bm_c.T, sp["mlstm_out_norm"], cst, mst, dycat)
    dqk, gw["conv_qk"] = _conv_bwd(zbig, dact, conv_w)
    dz_big = jnp.concatenate([dqk, dv_m, do_m, dq_f, dk_f, dv_f], axis=1)
    dzs = dzs_m + jnp.pad(jnp.concatenate([dzr_m, dfp], axis=0).T, ((0, 0), (0, 112)))
    dw_big = _mm_tn("d_w_big", dz_big, u, tn=1024)
    dw_small = _mm_tn("d_w_small", dzs, u, tn=1024)
    gw["w_in"] = jnp.concatenate([dw_big[0:1536], dw_small[0:8], dw_big[1536:3072], dw_small[8:16]], axis=0)
    du_a = _mm_nn("d_u_big", dz_big, w_big, tm=1024, tn=1024, tk=1024)
    du_b = _mm_nn("d_u_small", dzs, w_small, tm=1024, tn=1024)

    def mix_norm_bwd(da_t, db_t, h_t, dres_t, dzs_t, g):
        dx, dg = _rms_bwd_val(da_t + db_t, h_t, g)
        return dres_t + dx, dg, _colsum(dzs_t)

    dh1, gs["mix_norm"], dbias = _rowwise("mix_norm_bwd", mix_norm_bwd, [du_a, du_b, h1, dh2, dzs], [sp["mix_norm"]],
                                          [(d, F32)], [((1, d), F32), ((1, 128), F32)])
    gs["b_mlstm_gates"], gs["b_fox_f"] = dbias[:, 0:8], dbias[:, 8:16]
    conv_grad = gw.pop("conv_qk")
    gw["w_ple_proj"] = _chip_blocks(gw["w_ple_proj"])
    for n in ("w_in", "w_out", "w_ple_gate"):
        gw[n] = gw[n].reshape(4, -1, gw[n].shape[-1])
    part_rest = dict(zip(REST, _rs_partials(REST, gw, c_idx)))
    light = ("w_in", "w_out", "w_ple_gate", "w_ple_proj")
    part_ffn1 = []

    def own_plan(dwg, dwu, dwd):
        part_ffn1.extend(_rs_partials(FFN1, dict(zip(FFN1, (dwg, dwu, dwd))), c_idx))
        return _scatter_plan([pb for _, pb in part_ffn1])

    (grad_x, gs["ffn1_norm"], _, _, _), (l_light, l_down, l_gate, l_up, landed_ffn1) = _ffn_bwd_late_dx(
        "ffn1", dh1, x, sp["ffn1_norm"], xn1, g1, u1, wg1, wu1, wd1,
        _scatter_plan([part_rest[n][1] for n in light]),
        [_scatter_plan([part_rest[n][1]]) for n in ("ffn2_w_down", "ffn2_w_gate", "ffn2_w_up")], own_plan)
    landed_rest = dict(zip(light + ("ffn2_w_down", "ffn2_w_gate", "ffn2_w_up"), list(l_light) + [l_down[0], l_gate[0], l_up[0]]))
    names = REST + FFN1
    parts = [part_rest[n] for n in REST] + part_ffn1
    landed = [landed_rest[n] for n in REST] + list(landed_ffn1)
    mine = [_sum4("rs_sum_" + n, a, pf, place, SPLIT[n]) for n, a, (pf, _) in zip(names, landed, parts)]
    grads = dict(zip(names, _join_halves("rs_join", mine, [SPLIT[n] for n in names])))
    return loss_part, grad_x, grads, gs, conv_grad


ANY = pl.BlockSpec(memory_space=pl.ANY)
MESH = pl.DeviceIdType.MESH


def _place():
    x, y, c = lax.axis_index("x"), lax.axis_index("y"), lax.axis_index("c")
    chips = [(1 - x, y), (x, 1 - y), (1 - x, 1 - y)]
    return x, y, c, 2 * x + y, (x, y, 1 - c), chips


def _rcopy(src, dst, ssem, rsem, dev):
    return pltpu.make_async_remote_copy(src_ref=src, dst_ref=dst, send_sem=ssem, recv_sem=rsem, device_id=dev,
                                        device_id_type=MESH)


def _half(ref, lead, axis, idx, half):
    return ref.at[(slice(None),) * (lead + axis) + (pl.ds(idx * half, half),)]


def _to_slot(name, a, me_idx, dtype):
    r, cdim = a.shape
    tr = _pick(r, (256, 176, 128, 64))

    def body(me_ref, a_ref, o_ref):
        o_ref[...] = a_ref[...].astype(o_ref.dtype)

    return pl.pallas_call(
        body, name=name,
        grid_spec=pltpu.PrefetchScalarGridSpec(
            num_scalar_prefetch=1, grid=(r // tr,), in_specs=[pl.BlockSpec((tr, cdim), lambda i, me_ref: (i, 0))],
            out_specs=pl.BlockSpec((None, tr, cdim), lambda i, me_ref: (me_ref[0], i, 0))),
        out_shape=jax.ShapeDtypeStruct((4, r, cdim), dtype), compiler_params=_cparams(("parallel",)),
    )(me_idx, a)


def _gather4(name, bufs, split):
    return _run_plan(name, _gather_plan(bufs, split))


def _gather_plan(bufs, split):
    n = len(bufs)
    shapes = [b.shape[1:] for b in bufs]

    def ctx(outs):
        x, y, c, me, sib, chips = _place()

        def part(ref, a, which):
            if split[a] is None:
                return ref
            return _half(ref, 0, split[a], which, shapes[a][split[a]] // 2)

        return c, me, sib, chips, part

    def ici(outs, sems, a, j, chip, c, me, part):
        mine = part(outs[a].at[me], a, c)
        return _rcopy(mine, mine, sems[0].at[3 * a + j], sems[1].at[3 * a + j], (*chip, c))

    def fwd(outs, sems, a, j, chip, c, sib, part, which):
        blk = part(outs[a].at[2 * chip[0] + chip[1]], a, which)
        return _rcopy(blk, blk, sems[2].at[3 * a + j], sems[3].at[3 * a + j], sib)

    def start(ins, outs, sems):
        c, me, sib, chips, part = ctx(outs)
        for a in range(n):
            for j, chip in enumerate(chips):
                ici(outs, sems, a, j, chip, c, me, part).start()

    def mid(ins, outs, sems):
        c, me, sib, chips, part = ctx(outs)
        for j, chip in enumerate(chips):
            for a in range(n):
                blk = part(outs[a].at[2 * chip[0] + chip[1]], a, c)
                _rcopy(blk, blk, sems[0].at[3 * a + j], sems[1].at[3 * a + j], sib).wait_recv()
                if split[a] is not None:
                    fwd(outs, sems, a, j, chip, c, sib, part, c).start()

    def end(ins, outs, sems):
        c, me, sib, chips, part = ctx(outs)
        for j, chip in enumerate(chips):
            for a in range(n):
                if split[a] is not None:
                    fwd(outs, sems, a, j, chip, c, sib, part, 1 - c).wait_recv()
        for a in range(n):
            for j, chip in enumerate(chips):
                ici(outs, sems, a, j, chip, c, me, part).wait_send()
                if split[a] is not None:
                    fwd(outs, sems, a, j, chip, c, sib, part, c).wait_send()

    return dict(ins=list(bufs), outs=[jax.ShapeDtypeStruct(b.shape, b.dtype) for b in bufs], alias=True,
                sems=[pltpu.SemaphoreType.DMA((3 * n,))] * 4, phases=(start, mid, end))


def _run_plan(name, plan):
    ni, no = len(plan["ins"]), len(plan["outs"])

    def body(*refs):
        ins, outs, sems = refs[:ni], refs[ni:ni + no], refs[ni + no:]
        for phase in plan["phases"]:
            phase(ins, outs, sems)

    return pl.pallas_call(
        body, name=name, in_specs=[ANY] * ni, out_specs=[ANY] * no, out_shape=plan["outs"],
        input_output_aliases={a: a for a in range(ni)} if plan["alias"] else {}, scratch_shapes=plan["sems"],
    )(*plan["ins"])


class _Hosted:
    def __init__(self, plan, n_in, n_out):
        self.plan, self.n_in, self.n_out = plan, n_in, n_out
        self.ni, self.no, self.ns = (len(plan["ins"]) if plan else 0, len(plan["outs"]) if plan else 0,
                                     len(plan["sems"]) if plan else 0)

    def split(self, refs):
        a, b = self.n_in, self.n_in + self.ni
        c, d = b + self.n_out, b + self.n_out + self.no
        e = len(refs) - self.ns
        return refs[:a], refs[b:c], refs[d:e], (refs[a:b], refs[c:d], refs[e:])

    def run(self, k, cond, prefs):
        if self.plan is not None:
            @pl.when(cond)
            def _():
                self.plan["phases"][k](*prefs)

    def call_args(self):
        p = self.plan
        if p is None:
            return dict(in_specs=[], out_specs=[], out_shape=[], scratch=[], aliases={}, args=[])
        al = {self.n_in + a: self.n_out + a for a in range(self.ni)} if p["alias"] else {}
        return dict(in_specs=[ANY] * self.ni, out_specs=[ANY] * self.no, out_shape=list(p["outs"]), scratch=list(p["sems"]),
                    aliases=al, args=list(p["ins"]))


def _swap(name, arrs):
    n = len(arrs)

    def body(*refs):
        ins, outs = refs[:n], refs[n:2 * n]
        ssem, rsem = refs[2 * n:]
        x, y, c, me, sib, chips = _place()
        cps = [_rcopy(ins[a], outs[a], ssem.at[a], rsem.at[a], sib) for a in range(n)]
        for cp in cps:
            cp.start()
        for cp in cps:
            cp.wait()

    return pl.pallas_call(
        body, name=name, in_specs=[ANY] * n, out_specs=[ANY] * n,
        out_shape=[jax.ShapeDtypeStruct(a.shape, a.dtype) for a in arrs],
        scratch_shapes=[pltpu.SemaphoreType.DMA((n,))] * 2,
    )(*arrs)


def _scatter4(name, arrs):
    return _run_plan(name, _scatter_plan(arrs))


def _scatter_plan(arrs):
    n = len(arrs)

    def send(ins, outs, sems, a, j, chip, c, me):
        return _rcopy(ins[a].at[2 * chip[0] + chip[1]], outs[a].at[me], sems[0].at[3 * a + j], sems[1].at[3 * a + j], (*chip, c))

    def start(ins, outs, sems):
        x, y, c, me, sib, chips = _place()
        for a in range(n):
            for j, chip in enumerate(chips):
                send(ins, outs, sems, a, j, chip, c, me).start()

    def mid(ins, outs, sems):
        pass

    def end(ins, outs, sems):
        x, y, c, me, sib, chips = _place()
        for a in range(n):
            for j, chip in enumerate(chips):
                blk = outs[a].at[2 * chip[0] + chip[1]]
                _rcopy(blk, blk, sems[0].at[3 * a + j], sems[1].at[3 * a + j], sib).wait_recv()
        for a in range(n):
            for j, chip in enumerate(chips):
                send(ins, outs, sems, a, j, chip, c, me).wait_send()

    return dict(ins=list(arrs), outs=[jax.ShapeDtypeStruct(a.shape, a.dtype) for a in arrs], alias=False,
                sems=[pltpu.SemaphoreType.DMA((3 * n,))] * 2, phases=(start, mid, end))


def _join_halves(name, arrs, split):
    n = len(arrs)

    def body(*refs):
        outs = refs[n:2 * n]
        ssem, rsem = refs[2 * n:]
        x, y, c, me, sib, chips = _place()
        cps = []
        for a in range(n):
            mine = _half(outs[a], 0, split[a], c, arrs[a].shape[split[a]] // 2)
            cp = _rcopy(mine, mine, ssem.at[a], rsem.at[a], sib)
            cp.start()
            cps.append(cp)
        for a in range(n):
            blk = _half(outs[a], 0, split[a], 1 - c, arrs[a].shape[split[a]] // 2)
            _rcopy(blk, blk, ssem.at[a], rsem.at[a], sib).wait_recv()
        for cp in cps:
            cp.wait_send()

    return pl.pallas_call(
        body, name=name, in_specs=[ANY] * n, out_specs=[ANY] * n,
        out_shape=[jax.ShapeDtypeStruct(a.shape, a.dtype) for a in arrs],
        input_output_aliases={a: a for a in range(n)}, scratch_shapes=[pltpu.SemaphoreType.DMA((n,))] * 2,
    )(*arrs)


def _allreduce_small(s):
    r, cdim = s.shape

    def body(s_ref, o_ref, buf, ssem, rsem):
        x, y, c, me, sib, chips = _place()
        me8 = 4 * x + 2 * y + c
        buf[me8] = s_ref[...]
        flips = [(fx, fy, fc) for fx in (0, 1) for fy in (0, 1) for fc in (0, 1)][1:]
        cps = []
        for k, (fx, fy, fc) in enumerate(flips):
            peer = (x ^ fx if fx else x, y ^ fy if fy else y, c ^ fc if fc else c)
            cp = _rcopy(s_ref, buf.at[me8], ssem.at[k], rsem.at[k], peer)
            cp.start()
            cps.append(cp)
        for k, (fx, fy, fc) in enumerate(flips):
            src = 4 * (x ^ fx if fx else x) + 2 * (y ^ fy if fy else y) + (c ^ fc if fc else c)
            _rcopy(s_ref, buf.at[src], ssem.at[k], rsem.at[k], sib).wait_recv()
        for cp in cps:
            cp.wait_send()
        acc = buf[0]
        for k in range(1, 8):
            acc = acc + buf[k]
        o_ref[...] = acc

    vm = pl.BlockSpec(memory_space=pltpu.VMEM)
    return pl.pallas_call(
        body, name="allreduce_small", in_specs=[vm], out_specs=vm, out_shape=jax.ShapeDtypeStruct((r, cdim), F32),
        scratch_shapes=[pltpu.VMEM((8, r, cdim), F32), pltpu.SemaphoreType.DMA((7,)), pltpu.SemaphoreType.DMA((7,))],
    )(s)


def _add_my_half(name, g, recv, c_idx, axis):
    nb, hr, hc = recv.shape
    tr = _pick(hr, (256, 176, 128, 64))
    if axis == 0:
        g4 = g.reshape(nb, 2, hr, hc)
        gspec = pl.BlockSpec((None, None, tr, hc), lambda b, i, c_ref: (b, c_ref[0], i, 0))
    else:
        g4 = g
        gspec = pl.BlockSpec((None, tr, hc), lambda b, i, c_ref: (b, i, c_ref[0]))

    def body(c_ref, g_ref, r_ref, o_ref, ob_ref):
        s = g_ref[...] + r_ref[...].astype(F32)
        o_ref[...] = s
        ob_ref[...] = s.astype(BF16)

    ospec = pl.BlockSpec((None, tr, hc), lambda b, i, c_ref: (b, i, 0))
    return pl.pallas_call(
        body, name=name,
        grid_spec=pltpu.PrefetchScalarGridSpec(
            num_scalar_prefetch=1, grid=(nb, hr // tr), in_specs=[gspec, ospec], out_specs=[ospec, ospec]),
        out_shape=[jax.ShapeDtypeStruct((nb, hr, hc), F32), jax.ShapeDtypeStruct((nb, hr, hc), BF16)],
        compiler_params=_cparams(("parallel", "parallel")),
    )(c_idx, g4, recv)


def _sum4(name, landed, own, place, axis):
    nb, h, cdim = landed.shape
    tr = _pick(h, (256, 176, 128, 64))
    nt = h // tr

    def body(p_ref, a1_ref, a2_ref, a3_ref, own_ref, o_ref):
        o_ref[...] = ((own_ref[...] + a1_ref[...].astype(F32)) + a2_ref[...].astype(F32)) + a3_ref[...].astype(F32)

    def nxt(k):
        return pl.BlockSpec((None, tr, cdim), lambda i, p_ref: ((p_ref[0] + k) % nb, i, 0))

    if axis == 0:
        ospec = pl.BlockSpec((tr, cdim), lambda i, p_ref: (p_ref[1] * nt + i, 0))
        oshape = (2 * h, cdim)
    else:
        ospec = pl.BlockSpec((tr, cdim), lambda i, p_ref: (i, p_ref[1]))
        oshape = (h, 2 * cdim)
    return pl.pallas_call(
        body, name=name,
        grid_spec=pltpu.PrefetchScalarGridSpec(
            num_scalar_prefetch=1, grid=(nt,), in_specs=[nxt(1), nxt(2), nxt(3), nxt(0)], out_specs=ospec),
        out_shape=jax.ShapeDtypeStruct(oshape, F32), compiler_params=_cparams(("parallel",)),
    )(place, landed, landed, landed, own)


def _cast_other_half(name, g, c_idx, axis):
    nb, r, cdim = g.shape
    hr, hc = (r // 2, cdim) if axis == 0 else (r, cdim // 2)
    tr = _pick(hr, (256, 176, 128, 64))
    if axis == 0:
        g4 = g.reshape(nb, 2, hr, hc)
        gspec = pl.BlockSpec((None, None, tr, hc), lambda b, i, c_ref: (b, 1 - c_ref[0], i, 0))
    else:
        g4 = g
        gspec = pl.BlockSpec((None, tr, hc), lambda b, i, c_ref: (b, i, 1 - c_ref[0]))

    def body(c_ref, g_ref, o_ref):
        o_ref[...] = g_ref[...].astype(BF16)

    return pl.pallas_call(
        body, name=name,
        grid_spec=pltpu.PrefetchScalarGridSpec(
            num_scalar_prefetch=1, grid=(nb, hr // tr), in_specs=[gspec],
            out_specs=pl.BlockSpec((None, tr, hc), lambda b, i, c_ref: (b, i, 0))),
        out_shape=jax.ShapeDtypeStruct((nb, hr, hc), BF16), compiler_params=_cparams(("parallel", "parallel")),
    )(c_idx, g4)


def _adamw(name, w, g, m, v):
    c1 = 1.0 - ADAM_B1 ** ADAM_STEP
    c2 = 1.0 - ADAM_B2 ** ADAM_STEP

    def fn(w_t, g_t, m_t, v_t):
        m_n = ADAM_B1 * m_t + (1.0 - ADAM_B1) * g_t
        v_n = ADAM_B2 * v_t + (1.0 - ADAM_B2) * (g_t * g_t)
        delta = -ADAM_LR * ((m_n / c1) / (jnp.sqrt(v_n / c2) + ADAM_EPS) + ADAM_WD * w_t)
        return delta, m_n, v_n

    cdim = w.shape[1]
    return _rowwise(name, fn, [w, g, m, v], [], [(cdim, F32)] * 3, tm=_pick(w.shape[0], (256, 176, 128, 64, 8)))


BIG = ("ffn1_w_gate", "ffn1_w_up", "ffn1_w_down", "w_in", "w_out", "ffn2_w_gate", "ffn2_w_up", "ffn2_w_down",
       "w_ple_gate", "w_ple_proj")
SMALL = ("ffn1_norm", "mix_norm", "b_mlstm_gates", "b_fox_f", "mlstm_out_norm", "fox_out_norm", "ffn2_norm",
         "ple_gate_norm", "ple_proj_norm", "final_norm")
WEIGHTS = ("ffn1_norm", "ffn1_w_gate", "ffn1_w_up", "ffn1_w_down", "mix_norm", "w_in", "conv_qk", "b_mlstm_gates",
           "b_fox_f", "mlstm_out_norm", "fox_out_norm", "w_out", "ffn2_norm", "ffn2_w_gate", "ffn2_w_up", "ffn2_w_down",
           "ple_gate_norm", "w_ple_gate", "w_ple_proj", "ple_proj_norm", "final_norm")
TRANSPOSED = ("ffn1_w_gate", "ffn1_w_up", "w_in", "ffn2_w_gate", "ffn2_w_up")
PACK_W = 1024


def _chip_blocks(a):
    r, c4 = a.shape
    return a.reshape(r, 4, c4 // 4).transpose(1, 0, 2)


def _from_chip_blocks(a):
    nb, r, c = a.shape
    return a.transpose(1, 0, 2).reshape(r, nb * c)


def kernel(x, p, ffn1_norm, ffn1_w_gate, ffn1_w_up, ffn1_w_down, mix_norm, w_in, conv_qk, b_mlstm_gates, b_fox_f, mlstm_out_norm, fox_out_norm, w_out, ffn2_norm, ffn2_w_gate, ffn2_w_up, ffn2_w_down, ple_gate_norm, w_ple_gate, w_ple_proj, ple_proj_norm, final_norm, loss_target, m_ffn1_norm, m_ffn1_w_gate, m_ffn1_w_up, m_ffn1_w_down, m_mix_norm, m_w_in, m_conv_qk, m_b_mlstm_gates, m_b_fox_f, m_mlstm_out_norm, m_fox_out_norm, m_w_out, m_ffn2_norm, m_ffn2_w_gate, m_ffn2_w_up, m_ffn2_w_down, m_ple_gate_norm, m_w_ple_gate, m_w_ple_proj, m_ple_proj_norm, m_final_norm, v_ffn1_norm, v_ffn1_w_gate, v_ffn1_w_up, v_ffn1_w_down, v_mix_norm, v_w_in, v_conv_qk, v_b_mlstm_gates, v_b_fox_f, v_mlstm_out_norm, v_fox_out_norm, v_w_out, v_ffn2_norm, v_ffn2_w_gate, v_ffn2_w_up, v_ffn2_w_down, v_ple_gate_norm, v_w_ple_gate, v_w_ple_proj, v_ple_proj_norm, v_final_norm):
    w = dict(ffn1_norm=ffn1_norm, ffn1_w_gate=ffn1_w_gate, ffn1_w_up=ffn1_w_up, ffn1_w_down=ffn1_w_down, mix_norm=mix_norm,
             w_in=w_in, conv_qk=conv_qk, b_mlstm_gates=b_mlstm_gates, b_fox_f=b_fox_f, mlstm_out_norm=mlstm_out_norm,
             fox_out_norm=fox_out_norm, w_out=w_out, ffn2_norm=ffn2_norm, ffn2_w_gate=ffn2_w_gate, ffn2_w_up=ffn2_w_up,
             ffn2_w_down=ffn2_w_down, ple_gate_norm=ple_gate_norm, w_ple_gate=w_ple_gate, w_ple_proj=w_ple_proj,
             ple_proj_norm=ple_proj_norm, final_norm=final_norm)
    m = dict(ffn1_norm=m_ffn1_norm, ffn1_w_gate=m_ffn1_w_gate, ffn1_w_up=m_ffn1_w_up, ffn1_w_down=m_ffn1_w_down,
             mix_norm=m_mix_norm, w_in=m_w_in, conv_qk=m_conv_qk, b_mlstm_gates=m_b_mlstm_gates, b_fox_f=m_b_fox_f,
             mlstm_out_norm=m_mlstm_out_norm, fox_out_norm=m_fox_out_norm, w_out=m_w_out, ffn2_norm=m_ffn2_norm,
             ffn2_w_gate=m_ffn2_w_gate, ffn2_w_up=m_ffn2_w_up, ffn2_w_down=m_ffn2_w_down, ple_gate_norm=m_ple_gate_norm,
             w_ple_gate=m_w_ple_gate, w_ple_proj=m_w_ple_proj, ple_proj_norm=m_ple_proj_norm, final_norm=m_final_norm)
    v = dict(ffn1_norm=v_ffn1_norm, ffn1_w_gate=v_ffn1_w_gate, ffn1_w_up=v_ffn1_w_up, ffn1_w_down=v_ffn1_w_down,
             mix_norm=v_mix_norm, w_in=v_w_in, conv_qk=v_conv_qk, b_mlstm_gates=v_b_mlstm_gates, b_fox_f=v_b_fox_f,
             mlstm_out_norm=v_mlstm_out_norm, fox_out_norm=v_fox_out_norm, w_out=v_w_out, ffn2_norm=v_ffn2_norm,
             ffn2_w_gate=v_ffn2_w_gate, ffn2_w_up=v_ffn2_w_up, ffn2_w_down=v_ffn2_w_down, ple_gate_norm=v_ple_gate_norm,
             w_ple_gate=v_w_ple_gate, w_ple_proj=v_w_ple_proj, ple_proj_norm=v_ple_proj_norm, final_norm=v_final_norm)
    shapes = {n: w[n].shape for n in WEIGHTS}

    def view(a, n):
        return a[0].T if n in TRANSPOSED else a.reshape(-1, a.shape[-1])

    def unview(a, n):
        return (a.T if n in TRANSPOSED else a).reshape(shapes[n])

    w2, m2, v2 = ({n: view(a, n) for n, a in d.items()} for d in (w, m, v))

    c_idx = lax.axis_index("c").astype(jnp.int32).reshape(1)
    me_idx = (2 * lax.axis_index("x") + lax.axis_index("y")).astype(jnp.int32).reshape(1)
    place = jnp.concatenate([me_idx, c_idx])
    slot = {n: _to_slot("slot_" + n, w2[n], me_idx, BF16) for n in BIG}
    slot["conv_qk"] = _to_slot("slot_conv_qk", w2["conv_qk"], me_idx, F32)
    wg1, wu1, wd1 = _gather4("gather_ffn1", [slot[n] for n in FFN1], [SPLIT[n] for n in FFN1])
    sp = {n: w2[n] for n in SMALL}
    loss_part, grad_x, grads, gs, conv_grad = _local_step(
        x[0], p[0, 0], loss_target[0], sp, wg1, wu1, wd1, [slot[n] for n in REST + ("conv_qk",)], c_idx, place)
    loss = lax.psum(loss_part[0, 0], ("x", "y", "c"))

    small = [gs[n].reshape(1, -1) for n in SMALL] + [conv_grad]
    rows = [jnp.pad(a, ((0, 0), (0, PACK_W - a.shape[1]))) for a in small]
    packed = jnp.concatenate(rows, axis=0)
    packed = jnp.pad(packed, ((0, -packed.shape[0] % 8), (0, 0)))
    red = _allreduce_small(packed)
    for i, n in enumerate(SMALL):
        grads[n] = red[i:i + 1, :gs[n].size]
    dconv = red[len(SMALL):len(SMALL) + CONV_W, :conv_grad.shape[1]]
    cw = conv_qk.shape[-1]
    grads["conv_qk"] = lax.dynamic_slice_in_dim(dconv, (2 * lax.axis_index("x") + lax.axis_index("y")) * cw, cw, axis=1)

    outs = {}
    for n in WEIGHTS:
        g2 = grads[n].reshape(w2[n].shape)
        d, nm, nv = _adamw("adamw_" + n, w2[n], g2, m2[n], v2[n])
        outs[n] = tuple(unview(a, n) for a in (g2, d, nm, nv))
    return (loss, grad_x[None], *[outs[n][0] for n in WEIGHTS], *[outs[n][1] for n in WEIGHTS],
            *[outs[n][2] for n in WEIGHTS], *[outs[n][3] for n in WEIGHTS])
```

```python
import functools
import math

import jax
import jax.numpy as jnp
from jax import lax
from jax.experimental import pallas as pl
from jax.experimental.pallas import tpu as pltpu

F32 = jnp.float32
BF16 = jnp.bfloat16
EPS = 1e-6
NH_M, DK_M, DV_M = 4, 64, 128
NH_F, DH_F = 8, 64
CONV_W = 4
ADAM_LR, ADAM_B1, ADAM_B2, ADAM_EPS, ADAM_WD, ADAM_STEP = 0.001, 0.9, 0.999, 1e-08, 0.01, 10
VMEM_LIMIT = 56 * 1024 * 1024


def _cparams(sem):
    return pltpu.CompilerParams(dimension_semantics=sem, vmem_limit_bytes=VMEM_LIMIT)


def _sigmoid(x):
    return 1.0 / (1.0 + jnp.exp(-x))


def _dot(a, b, ca, cb):
    return lax.dot_general(a.astype(BF16), b.astype(BF16), (((ca,), (cb,)), ((), ())), preferred_element_type=F32)


def _rowwise(name, fn, tiled, full, outs, accs=(), tm=256):
    rows = tiled[0].shape[0]
    tm = min(tm, rows)
    assert rows % tm == 0
    n_t, n_f, n_o, n_a = len(tiled), len(full), len(outs), len(accs)

    def body(*refs):
        ins = [r[...] for r in refs[: n_t + n_f]]
        res = fn(*ins)
        if not isinstance(res, (tuple, list)):
            res = (res,)
        orefs = refs[n_t + n_f:]
        for r, v in zip(orefs[:n_o], res[:n_o]):
            r[...] = v.astype(r.dtype)
        if n_a:
            @pl.when(pl.program_id(0) == 0)
            def _():
                for r in orefs[n_o:]:
                    r[...] = jnp.zeros_like(r)
            for r, v in zip(orefs[n_o:], res[n_o:]):
                r[...] += v.astype(r.dtype)

    in_specs = [pl.BlockSpec((tm, a.shape[1]), lambda i: (i, 0)) for a in tiled]
    in_specs += [pl.BlockSpec(a.shape, lambda i: (0, 0)) for a in full]
    out_specs = [pl.BlockSpec((tm, c), lambda i: (i, 0)) for c, _ in outs]
    out_specs += [pl.BlockSpec(s, lambda i: (0, 0)) for s, _ in accs]
    out_shape = [jax.ShapeDtypeStruct((rows, c), d) for c, d in outs]
    out_shape += [jax.ShapeDtypeStruct(s, d) for s, d in accs]
    res = pl.pallas_call(
        body, name=name, grid=(rows // tm,), in_specs=in_specs, out_specs=out_specs, out_shape=out_shape,
        compiler_params=_cparams(("arbitrary",) if n_a else ("parallel",)),
    )(*tiled, *full)
    return res


def _colsum(v):
    return jnp.sum(v, axis=0, keepdims=True)


def _rms_fwd_val(x, g):
    r = lax.rsqrt(jnp.mean(x * x, axis=-1, keepdims=True) + EPS)
    return x * r * g


def _rms_bwd_val(dy, x, g):
    r = lax.rsqrt(jnp.mean(x * x, axis=-1, keepdims=True) + EPS)
    xh = x * r
    dxh = dy * g
    dx = r * (dxh - xh * jnp.mean(dxh * xh, axis=-1, keepdims=True))
    return dx, _colsum(dy * xh)


def _mm(name, pairs, out_shape, out_block, out_map, grid, kaxis, ta=False, tb=False, scale=None, res=None,
        out_dtype=F32, plan=None):
    nk = grid[kaxis]
    npairs = len(pairs)
    ca, cb = (0 if ta else 1), (1 if tb else 0)
    acc_shape = tuple(d for d in out_block if d is not None)
    n_in = 2 * npairs + (1 if res is not None else 0)
    host = _Hosted(plan, n_in, 1)

    def body(*refs):
        ins, (o_ref,), (acc_ref,), prefs = host.split(refs)
        in_refs = ins[: 2 * npairs]
        res_ref = ins[2 * npairs] if res is not None else None
        k = pl.program_id(kaxis)
        ids = [pl.program_id(a) for a in range(len(grid))]
        first, last = ids[0] == 0, ids[0] == grid[0] - 1
        for a in range(1, len(grid)):
            first, last = first & (ids[a] == 0), last & (ids[a] == grid[a] - 1)
        host.run(0, first, prefs)
        host.run(1, first, prefs)

        @pl.when(k == 0)
        def _():
            acc_ref[...] = jnp.zeros_like(acc_ref)

        part = None
        for p in range(npairs):
            d = _dot(in_refs[2 * p][...], in_refs[2 * p + 1][...], ca, cb)
            part = d if part is None else part + d
        acc_ref[...] += part

        @pl.when(k == nk - 1)
        def _():
            v = acc_ref[...]
            if scale is not None:
                v = v * scale
            if res_ref is not None:
                v = v + res_ref[...].astype(F32)
            o_ref[...] = v.astype(o_ref.dtype)

        host.run(2, last, prefs)

    in_specs, args = [], []
    for a, ab, am, b, bb, bm in pairs:
        in_specs += [pl.BlockSpec(ab, am), pl.BlockSpec(bb, bm)]
        args += [a, b]
    if res is not None:
        in_specs.append(pl.BlockSpec(out_block, out_map))
        args.append(res)
    sem = tuple("arbitrary" if (i == kaxis or plan is not None) else "parallel" for i in range(len(grid)))
    hc = host.call_args()
    out = pl.pallas_call(
        body, name=name, grid=grid, in_specs=in_specs + hc["in_specs"],
        out_specs=[pl.BlockSpec(out_block, out_map)] + hc["out_specs"],
        out_shape=[jax.ShapeDtypeStruct(out_shape, out_dtype)] + hc["out_shape"],
        scratch_shapes=[pltpu.VMEM(acc_shape, F32)] + hc["scratch"], input_output_aliases=hc["aliases"],
        compiler_params=_cparams(sem),
    )(*args, *hc["args"])
    return out[0] if plan is None else (out[0], out[1:])


def _pick(n, pref):
    for t in pref:
        if n % t == 0:
            return t
    return n


def _mm_nn(name, a, b, tm=512, tn=512, tk=512, **kw):
    (m, k), n = a.shape, b.shape[1]
    tm, tn, tk = _pick(m, (tm, 256, 128)), _pick(n, (tn, 256, 128)), _pick(k, (tk, 256, 128))
    return _mm(name, [(a, (tm, tk), lambda i, j, kk: (i, kk), b, (tk, tn), lambda i, j, kk: (kk, j))],
               (m, n), (tm, tn), lambda i, j, kk: (i, j), (m // tm, n // tn, k // tk), 2, **kw)


def _mm_nt(name, a, b, tm=512, tn=512, tk=512, **kw):
    (m, k), n = a.shape, b.shape[0]
    tm, tn, tk = _pick(m, (tm, 256, 128)), _pick(n, (tn, 256, 128)), _pick(k, (tk, 256, 128))
    return _mm(name, [(a, (tm, tk), lambda i, j, kk: (i, kk), b, (tn, tk), lambda i, j, kk: (j, kk))],
               (m, n), (tm, tn), lambda i, j, kk: (i, j), (m // tm, n // tn, k // tk), 2, tb=True, **kw)


def _mm_tn(name, a, b, tm=512, tn=512, tk=2048, **kw):
    (k, m), n = a.shape, b.shape[1]
    tm, tn, tk = _pick(m, (tm, 256, 128)), _pick(n, (tn, 256, 128)), _pick(k, (tk, 1024, 512, 256, 128))
    return _mm(name, [(a, (tk, tm), lambda i, j, kk: (kk, i), b, (tk, tn), lambda i, j, kk: (kk, j))],
               (m, n), (tm, tn), lambda i, j, kk: (i, j), (m // tm, n // tn, k // tk), 2, ta=True, **kw)


def _norm_mm(name, h, gamma, w, w_transposed, out_dtype):
    t, d = h.shape
    n = w.shape[0] if w_transposed else w.shape[1]
    tm, tn = _pick(t, (512, 256)), _pick(n, (1024, 512, 256, 128))

    def body(h_ref, gam_ref, w_ref, xn_ref, o_ref, xn_scr):
        @pl.when(pl.program_id(1) == 0)
        def _():
            xn = _rms_fwd_val(h_ref[...], gam_ref[...]).astype(BF16)
            xn_scr[...] = xn
            xn_ref[...] = xn

        o_ref[...] = _dot(xn_scr[...], w_ref[...], 1, 1 if w_transposed else 0).astype(o_ref.dtype)

    wspec = pl.BlockSpec((tn, d), lambda i, j: (j, 0)) if w_transposed else pl.BlockSpec((d, tn), lambda i, j: (0, j))
    return pl.pallas_call(
        body, name=name, grid=(t // tm, n // tn),
        in_specs=[pl.BlockSpec((tm, d), lambda i, j: (i, 0)), pl.BlockSpec((1, d), lambda i, j: (0, 0)), wspec],
        out_specs=[pl.BlockSpec((tm, d), lambda i, j: (i, 0)), pl.BlockSpec((tm, tn), lambda i, j: (i, j))],
        out_shape=[jax.ShapeDtypeStruct((t, d), BF16), jax.ShapeDtypeStruct((t, n), out_dtype)],
        scratch_shapes=[pltpu.VMEM((tm, d), BF16)], compiler_params=_cparams(("parallel", "arbitrary")),
    )(h, gamma, w)


CHAIN_ROWS = 256


def _row_chains(tm):
    n = max(tm // CHAIN_ROWS, 1)
    return [slice(r * (tm // n), (r + 1) * (tm // n)) for r in range(n)]


def _ffn_fwd(pfx, h, gamma, wg, wu, wd, plan=None):
    t, d = h.shape
    nb, f, _ = wg.shape
    tm = _pick(t, (1024, 512, 256))
    nt = t // tm
    host = _Hosted(plan, 5, 4)

    def body(*refs):
        (h_ref, gam_ref, wg_ref, wu_ref, wd_ref), (ho_ref, xn_ref, g_ref, u_ref), (xn_scr, acc_ref), prefs = host.split(refs)
        i, j = pl.program_id(0), pl.program_id(1)
        host.run(0, (i == 0) & (j == 0), prefs)
        host.run(1, (i == nt // 2) & (j == 0), prefs)

        @pl.when(j == 0)
        def _():
            xn = _rms_fwd_val(h_ref[...], gam_ref[...]).astype(BF16)
            xn_scr[...] = xn
            xn_ref[...] = xn
            acc_ref[...] = jnp.zeros_like(acc_ref)

        for rows in _row_chains(tm):
            x = xn_scr[rows, :]
            g = _dot(x, wg_ref[...], 1, 1)
            u = _dot(x, wu_ref[...], 1, 1)
            g_ref[rows, :] = g.astype(BF16)
            u_ref[rows, :] = u.astype(BF16)
            acc_ref[rows, :] += _dot(g * _sigmoid(g) * u, wd_ref[...], 1, 0)

        @pl.when(j == nb - 1)
        def _():
            ho_ref[...] = h_ref[...] + 0.5 * acc_ref[...]

        host.run(2, (i == nt - 1) & (j == nb - 1), prefs)

    row = pl.BlockSpec((tm, d), lambda i, j: (i, 0))
    blk = pl.BlockSpec((None, tm, f), lambda i, j: (j, i, 0))
    wspec = pl.BlockSpec((None, f, d), lambda i, j: (j, 0, 0))
    hc = host.call_args()
    res = pl.pallas_call(
        body, name=pfx + "_fwd", grid=(nt, nb),
        in_specs=[row, pl.BlockSpec((1, d), lambda i, j: (0, 0)), wspec, wspec, wspec] + hc["in_specs"],
        out_specs=[row, row, blk, blk] + hc["out_specs"],
        out_shape=[jax.ShapeDtypeStruct((t, d), F32), jax.ShapeDtypeStruct((t, d), BF16),
                   jax.ShapeDtypeStruct((nb, t, f), BF16), jax.ShapeDtypeStruct((nb, t, f), BF16)] + hc["out_shape"],
        scratch_shapes=[pltpu.VMEM((tm, d), BF16), pltpu.VMEM((tm, d), F32)] + hc["scratch"],
        input_output_aliases=hc["aliases"], compiler_params=_cparams(("arbitrary", "arbitrary")),
    )(h, gamma, wg, wu, wd, *hc["args"])
    return res[:4], res[4:]


def _ffn_bwd(pfx, dh_out, h, gamma, xn, g_all, u_all, wg, wu, wd, plan=None):
    t, d = h.shape
    nb, f, _ = wg.shape
    tm = _pick(t, (512, 256))
    tk = _pick(t, (2048, 1024, 512, 256))

    nt = t // tm
    host = _Hosted(plan, 8, 5)

    def body(*refs):
        ((dy_ref, h_ref, gam_ref, wg_ref, wu_ref, wd_ref, g_ref, u_ref), (dh_ref, dgam_ref, dg_ref, du_ref, a_ref),
         (acc_ref,), prefs) = host.split(refs)
        i, j = pl.program_id(0), pl.program_id(1)
        host.run(0, (i == 0) & (j == 0), prefs)
        host.run(1, (i == nt // 2) & (j == 0), prefs)

        @pl.when((i == 0) & (j == 0))
        def _():
            dgam_ref[...] = jnp.zeros_like(dgam_ref)

        @pl.when(j == 0)
        def _():
            acc_ref[...] = jnp.zeros_like(acc_ref)

        for rows in _row_chains(tm):
            da = _dot(dy_ref[rows, :], wd_ref[...], 1, 1) * 0.5
            g = g_ref[rows, :].astype(F32)
            u = u_ref[rows, :].astype(F32)
            s = _sigmoid(g)
            sl = g * s
            du = (da * sl).astype(BF16)
            dg = (da * u * (s + sl * (1.0 - s))).astype(BF16)
            du_ref[rows, :] = du
            dg_ref[rows, :] = dg
            a_ref[rows, :] = (sl * u).astype(BF16)
            acc_ref[rows, :] += _dot(dg, wg_ref[...], 1, 0) + _dot(du, wu_ref[...], 1, 0)

        @pl.when(j == nb - 1)
        def _():
            dx, dgam = _rms_bwd_val(acc_ref[...], h_ref[...], gam_ref[...])
            dh_ref[...] = dy_ref[...] + dx
            dgam_ref[...] += dgam

        host.run(2, (i == nt - 1) & (j == nb - 1), prefs)

    row = pl.BlockSpec((tm, d), lambda i, j: (i, 0))
    vec = pl.BlockSpec((1, d), lambda i, j: (0, 0))
    blk = pl.BlockSpec((None, tm, f), lambda i, j: (j, i, 0))
    wspec = pl.BlockSpec((None, f, d), lambda i, j: (j, 0, 0))
    hc = host.call_args()
    res = pl.pallas_call(
        body, name=pfx + "_bwd", grid=(nt, nb),
        in_specs=[row, row, vec, wspec, wspec, wspec, blk, blk] + hc["in_specs"],
        out_specs=[row, vec, blk, blk, blk] + hc["out_specs"],
        out_shape=[jax.ShapeDtypeStruct((t, d), F32), jax.ShapeDtypeStruct((1, d), F32)]
        + [jax.ShapeDtypeStruct((nb, t, f), BF16)] * 3 + hc["out_shape"],
        scratch_shapes=[pltpu.VMEM((tm, d), F32)] + hc["scratch"], input_output_aliases=hc["aliases"],
        compiler_params=_cparams(("arbitrary", "arbitrary")),
    )(dh_out, h, gamma, wg, wu, wd, g_all, u_all, *hc["args"])
    dh, dgamma, dg_all, du_all, a_all = res[:5]

    xmap, bmap, omap = (lambda b, k: (k, 0)), (lambda b, k: (b, k, 0)), (lambda b, k: (b, 0, 0))
    dwg = _mm(pfx + "_dwg", [(dg_all, (None, tk, f), bmap, xn, (tk, d), xmap)], (nb, f, d), (None, f, d), omap,
              (nb, t // tk), 1, ta=True)
    dwu = _mm(pfx + "_dwu", [(du_all, (None, tk, f), bmap, xn, (tk, d), xmap)], (nb, f, d), (None, f, d), omap,
              (nb, t // tk), 1, ta=True)
    dwd = _mm(pfx + "_dwd", [(a_all, (None, tk, f), bmap, dh_out, (tk, d), xmap)], (nb, f, d), (None, f, d), omap,
              (nb, t // tk), 1, ta=True, scale=0.5)
    return (dh, dgamma, dwg, dwu, dwd), res[5:]


def _ffn_bwd_late_dx(pfx, dh_out, h, gamma, xn, g_all, u_all, wg, wu, wd, plan_gu, plans_dw, make_plan_dx):
    t, d = h.shape
    nb, f, _ = wg.shape
    tm = _pick(t, (512, 256))
    tk = _pick(t, (2048, 1024, 512, 256))
    nt = t // tm
    host_a = _Hosted(plan_gu, 4, 3)

    def body_a(*refs):
        (dy_ref, wd_ref, g_ref, u_ref), (dg_ref, du_ref, a_ref), _, prefs = host_a.split(refs)
        i, j = pl.program_id(0), pl.program_id(1)
        host_a.run(0, (i == 0) & (j == 0), prefs)
        host_a.run(1, (i == 0) & (j == 0), prefs)
        for rows in _row_chains(tm):
            da = _dot(dy_ref[rows, :], wd_ref[...], 1, 1) * 0.5
            g = g_ref[rows, :].astype(F32)
            u = u_ref[rows, :].astype(F32)
            s = _sigmoid(g)
            sl = g * s
            du_ref[rows, :] = (da * sl).astype(BF16)
            dg_ref[rows, :] = (da * u * (s + sl * (1.0 - s))).astype(BF16)
            a_ref[rows, :] = (sl * u).astype(BF16)
        host_a.run(2, (i == nt - 1) & (j == nb - 1), prefs)

    row = pl.BlockSpec((tm, d), lambda i, j: (i, 0))
    vec = pl.BlockSpec((1, d), lambda i, j: (0, 0))
    blk = pl.BlockSpec((None, tm, f), lambda i, j: (j, i, 0))
    wspec = pl.BlockSpec((None, f, d), lambda i, j: (j, 0, 0))
    hc = host_a.call_args()
    res_a = pl.pallas_call(
        body_a, name=pfx + "_bwd_gu", grid=(nt, nb), in_specs=[row, wspec, blk, blk] + hc["in_specs"],
        out_specs=[blk] * 3 + hc["out_specs"], out_shape=[jax.ShapeDtypeStruct((nb, t, f), BF16)] * 3 + hc["out_shape"],
        scratch_shapes=hc["scratch"], input_output_aliases=hc["aliases"], compiler_params=_cparams(("arbitrary", "arbitrary")),
    )(dh_out, wd, g_all, u_all, *hc["args"])
    dg_all, du_all, a_all = res_a[:3]

    xmap, bmap, omap = (lambda b, k: (k, 0)), (lambda b, k: (b, k, 0)), (lambda b, k: (b, 0, 0))
    dwd, out_d = _mm(pfx + "_dwd", [(a_all, (None, tk, f), bmap, dh_out, (tk, d), xmap)], (nb, f, d), (None, f, d), omap,
                     (nb, t // tk), 1, ta=True, scale=0.5, plan=plans_dw[0])
    dwg, out_g = _mm(pfx + "_dwg", [(dg_all, (None, tk, f), bmap, xn, (tk, d), xmap)], (nb, f, d), (None, f, d), omap,
                     (nb, t // tk), 1, ta=True, plan=plans_dw[1])
    dwu, out_u = _mm(pfx + "_dwu", [(du_all, (None, tk, f), bmap, xn, (tk, d), xmap)], (nb, f, d), (None, f, d), omap,
                     (nb, t // tk), 1, ta=True, plan=plans_dw[2])

    plan_dx = make_plan_dx(dwg, dwu, dwd)
    host_b = _Hosted(plan_dx, 7, 2)

    def body_b(*refs):
        (dy_ref, h_ref, gam_ref, wg_ref, wu_ref, dg_ref, du_ref), (dh_ref, dgam_ref), (acc_ref,), prefs = host_b.split(refs)
        i, j = pl.program_id(0), pl.program_id(1)
        host_b.run(0, (i == 0) & (j == 0), prefs)
        host_b.run(1, (i == 0) & (j == 0), prefs)

        @pl.when((i == 0) & (j == 0))
        def _():
            dgam_ref[...] = jnp.zeros_like(dgam_ref)

        @pl.when(j == 0)
        def _():
            acc_ref[...] = jnp.zeros_like(acc_ref)

        acc_ref[...] += _dot(dg_ref[...], wg_ref[...], 1, 0) + _dot(du_ref[...], wu_ref[...], 1, 0)

        @pl.when(j == nb - 1)
        def _():
            dx, dgam = _rms_bwd_val(acc_ref[...], h_ref[...], gam_ref[...])
            dh_ref[...] = dy_ref[...] + dx
            dgam_ref[...] += dgam

        host_b.run(2, (i == nt - 1) & (j == nb - 1), prefs)

    hc = host_b.call_args()
    res_b = pl.pallas_call(
        body_b, name=pfx + "_bwd_dx", grid=(nt, nb), in_specs=[row, row, vec, wspec, wspec, blk, blk] + hc["in_specs"],
        out_specs=[row, vec] + hc["out_specs"],
        out_shape=[jax.ShapeDtypeStruct((t, d), F32), jax.ShapeDtypeStruct((1, d), F32)] + hc["out_shape"],
        scratch_shapes=[pltpu.VMEM((tm, d), F32)] + hc["scratch"], input_output_aliases=hc["aliases"],
        compiler_params=_cparams(("arbitrary", "arbitrary")),
    )(dh_out, h, gamma, wg, wu, dg_all, du_all, *hc["args"])
    return (res_b[0], res_b[1], dwg, dwu, dwd), (res_a[3:], out_d, out_g, out_u, res_b[2:])


HALO = 16


def _silu_grad(y):
    s = _sigmoid(y)
    return s * (1.0 + y * (1.0 - s))


def _with_halo(ref, i, n_tiles, tm, before, after):
    t = ref.shape[0]
    r0 = pl.multiple_of(i * tm, tm)
    parts = [ref[pl.ds(r0, tm), :].astype(F32)]
    if before:
        prev = ref[pl.ds(pl.multiple_of(jnp.maximum(r0 - HALO, 0), HALO), HALO), :].astype(F32)
        parts.insert(0, jnp.where(i > 0, prev, 0.0))
    if after:
        nxt = ref[pl.ds(pl.multiple_of(jnp.minimum(r0 + tm, t - HALO), HALO), HALO), :].astype(F32)
        parts.append(jnp.where(i < n_tiles - 1, nxt, 0.0))
    return jnp.concatenate(parts, axis=0)


def _conv_fwd(zbig, w):
    t, c = zbig.shape[0], w.shape[1]
    tm = _pick(t, (512, 256))
    nt = t // tm

    def body(x_ref, w_ref, o_ref):
        xe = _with_halo(x_ref, pl.program_id(0), nt, tm, True, False)
        wv = w_ref[...]
        y = xe * wv[3:4, :]
        for i in range(CONV_W - 1):
            y = y + pltpu.roll(xe, CONV_W - 1 - i, 0) * wv[i:i + 1, :]
        y = y[HALO:, :]
        o_ref[...] = (y * _sigmoid(y)).astype(o_ref.dtype)

    return pl.pallas_call(
        body, name="conv_fwd", grid=(nt,),
        in_specs=[pl.BlockSpec((t, c), lambda i: (0, 0)), pl.BlockSpec(w.shape, lambda i: (0, 0))],
        out_specs=pl.BlockSpec((tm, c), lambda i: (i, 0)), out_shape=jax.ShapeDtypeStruct((t, c), BF16),
        compiler_params=_cparams(("parallel",)),
    )(zbig, w)


def _conv_bwd(zbig, dact, w):
    t, c = dact.shape
    tm = _pick(t, (512, 256))
    nt = t // tm
    n = tm + HALO

    def body(x_ref, d_ref, w_ref, dx_ref, dw_ref):
        xe = _with_halo(x_ref, pl.program_id(0), nt, tm, True, True)
        de = _with_halo(d_ref, pl.program_id(0), nt, tm, False, True)
        wv = w_ref[...]
        sh = [pltpu.roll(xe, CONV_W - 1 - i, 0)[HALO:, :] if i < CONV_W - 1 else xe[HALO:, :] for i in range(CONV_W)]
        y = sh[0] * wv[0:1, :]
        for i in range(1, CONV_W):
            y = y + sh[i] * wv[i:i + 1, :]
        dy = de * _silu_grad(y)
        dx = dy * wv[3:4, :]
        for i in range(CONV_W - 1):
            dx = dx + pltpu.roll(dy, n - (CONV_W - 1 - i), 0) * wv[i:i + 1, :]
        dx_ref[...] = dx[:tm, :].astype(dx_ref.dtype)
        dyc = dy[:tm, :]
        dwp = jnp.concatenate([_colsum(dyc * sh[i][:tm, :]) for i in range(CONV_W)], axis=0)

        @pl.when(pl.program_id(0) == 0)
        def _():
            dw_ref[...] = jnp.zeros_like(dw_ref)
        dw_ref[...] += dwp

    return pl.pallas_call(
        body, name="conv_bwd", grid=(nt,),
        in_specs=[pl.BlockSpec((t, c), lambda i: (0, 0)), pl.BlockSpec((t, c), lambda i: (0, 0)),
                  pl.BlockSpec(w.shape, lambda i: (0, 0))],
        out_specs=[pl.BlockSpec((tm, c), lambda i: (i, 0)), pl.BlockSpec(w.shape, lambda i: (0, 0))],
        out_shape=[jax.ShapeDtypeStruct((t, c), BF16), jax.ShapeDtypeStruct(w.shape, F32)],
        compiler_params=_cparams(("arbitrary",)),
    )(zbig, dact, w)


LM = 256
HI = lax.Precision.HIGHEST


def _logsig(x):
    return jnp.minimum(x, 0.0) - jnp.log(1.0 + jnp.exp(-jnp.abs(x)))


def _tri(n, lower):
    r = lax.broadcasted_iota(jnp.int32, (n, n), 0)
    c = lax.broadcasted_iota(jnp.int32, (n, n), 1)
    return (r >= c) if lower else (r <= c)


def _f32dot(a, b):
    return lax.dot_general(a, b, (((1,), (0,)), ((), ())), precision=HI, preferred_element_type=F32)


def _tri_dot(a, b, a_is_tri):
    tri = (a if a_is_tri else b).astype(BF16)
    parts = _split3(b if a_is_tri else a)
    outs = [_dot(tri, p, 1, 0) if a_is_tri else _dot(p, tri, 1, 0) for p in parts]
    return (outs[0] + outs[1]) + outs[2]


def _mlstm_decays(zs_ref, zsr_ref, bc_ref, br_ref):
    l = LM
    lf_c = _logsig(zs_ref[:, 0:2 * NH_M] + bc_ref[...])
    lf_r = _logsig(zsr_ref[...] + br_ref[...])
    return _tri_dot(_tri(l, True), lf_c, True), _tri_dot(lf_r, _tri(l, False), False)


def _mlstm_chunk(h, q_ref, k_ref, v_ref, zs_ref, zsr_ref, bc_ref, br_ref, c_prev, m_prev, decays):
    l = LM
    q = q_ref[:, h * DK_M:(h + 1) * DK_M].astype(F32) * (DK_M ** -0.5)
    k = k_ref[:, h * DK_M:(h + 1) * DK_M]
    v = v_ref[:, h * DV_M:(h + 1) * DV_M]
    lane = lax.broadcasted_iota(jnp.int32, (l, DV_M), 1)
    v1 = jnp.concatenate([v, (lane == 0).astype(v.dtype)], axis=1)
    zs, zsr = zs_ref[...], zsr_ref[...]
    li_c = zs[:, h:h + 1] + bc_ref[:, h:h + 1]
    fp_c = zs[:, NH_M + h:NH_M + h + 1] + bc_ref[:, NH_M + h:NH_M + h + 1]
    li_r = zsr[h:h + 1, :] + br_ref[h:h + 1, :]
    fp_r = zsr[NH_M + h:NH_M + h + 1, :] + br_ref[NH_M + h:NH_M + h + 1, :]
    low = _tri(l, True)
    b_c = decays[0][:, NH_M + h:NH_M + h + 1]
    b_r = decays[1][NH_M + h:NH_M + h + 1, :]
    g = b_r[:, l - 1:l]
    dmat = jnp.where(low, b_c - b_r + li_r, -jnp.inf)
    inter = b_c + m_prev
    m_t = jnp.maximum(inter, jnp.max(dmat, axis=1, keepdims=True))
    w_inter = jnp.exp(inter - m_t)
    amat = jnp.exp(dmat - m_t)
    s = _dot(q, k, 1, 1)
    p = amat * s
    qc = _dot(q, c_prev, 1, 0)
    qc_w = w_inter * qc
    num1 = qc_w + _dot(p, v1, 1, 0)
    den = num1[:, DV_M:DV_M + 1]
    mx = jnp.maximum(jnp.abs(den), jnp.exp(-m_t))
    hh = num1[:, :DV_M] / mx
    a_c = g - b_c + li_c
    return dict(q=q, k=k, v1=v1, fp_c=fp_c, fp_r=fp_r, b_c=b_c, g=g, m_t=m_t, w_inter=w_inter, amat=amat, s=s, p=p,
                qc_w=qc_w, den=den, mx=mx, hh=hh, a_c=a_c)


def _mlstm_fwd(qk, zbig, zs, zsr, bc, br, gm):
    t = zs.shape[0]
    l = LM
    nc = t // l
    dm = NH_M * DV_M

    def body(q_ref, k_ref, v_ref, o_ref, zs_ref, zsr_ref, bc_ref, br_ref, gm_ref, y_ref, cst_ref, mst_ref, c_scr, m_scr):
        @pl.when(pl.program_id(0) == 0)
        def _():
            c_scr[...] = jnp.zeros_like(c_scr)
            m_scr[...] = jnp.zeros_like(m_scr)

        cst_ref[...] = c_scr[...]
        mst_ref[...] = m_scr[...]
        ys = []
        decays = _mlstm_decays(zs_ref, zsr_ref, bc_ref, br_ref)
        for h in range(NH_M):
            c_prev = c_scr[h]
            m_prev = m_scr[h:h + 1, 0:1]
            r = _mlstm_chunk(h, q_ref, k_ref, v_ref, zs_ref, zsr_ref, bc_ref, br_ref, c_prev, m_prev, decays)
            hh = r["hh"]
            gh = gm_ref[:, h * DV_M:(h + 1) * DV_M]
            hn = hh * lax.rsqrt(jnp.mean(hh * hh, axis=-1, keepdims=True) + EPS) * gh
            og = o_ref[:, h * DV_M:(h + 1) * DV_M].astype(F32)
            ys.append(hn * _sigmoid(og))
            m_new = jnp.maximum(r["g"] + m_prev, jnp.max(r["a_c"], axis=0, keepdims=True))
            decay = jnp.exp(r["g"] + m_prev - m_new)
            wk = r["k"].astype(F32) * jnp.exp(r["a_c"] - m_new)
            c_scr[h] = decay * c_prev + _dot(wk, r["v1"], 0, 0)
            m_scr[h:h + 1, :] = jnp.broadcast_to(m_new, (1, 128))
        y_ref[...] = jnp.concatenate(ys, axis=1).astype(y_ref.dtype)

    return pl.pallas_call(
        body, name="mlstm_fwd", grid=(nc,),
        in_specs=[pl.BlockSpec((l, NH_M * DK_M), lambda i: (i, 0)), pl.BlockSpec((l, NH_M * DK_M), lambda i: (i, 1)),
                  pl.BlockSpec((l, dm), lambda i: (i, 1)), pl.BlockSpec((l, dm), lambda i: (i, 2)),
                  pl.BlockSpec((l, 128), lambda i: (i, 0)), pl.BlockSpec((8, l), lambda i: (0, i)),
                  pl.BlockSpec((1, 8), lambda i: (0, 0)), pl.BlockSpec((8, 1), lambda i: (0, 0)),
                  pl.BlockSpec((1, dm), lambda i: (0, 0))],
        out_specs=[pl.BlockSpec((l, dm), lambda i: (i, 0)), pl.BlockSpec((None, NH_M, DK_M, 2 * DV_M), lambda i: (i, 0, 0, 0)),
                   pl.BlockSpec((None, 8, 128), lambda i: (i, 0, 0))],
        out_shape=[jax.ShapeDtypeStruct((t, dm), BF16), jax.ShapeDtypeStruct((nc, NH_M, DK_M, 2 * DV_M), F32),
                   jax.ShapeDtypeStruct((nc, 8, 128), F32)],
        scratch_shapes=[pltpu.VMEM((NH_M, DK_M, 2 * DV_M), F32), pltpu.VMEM((8, 128), F32)],
        compiler_params=_cparams(("arbitrary",)),
    )(qk, qk, zbig, zbig, zs, zsr, bc, br, gm)


def _mlstm_bwd(qk, zbig, zs, zsr, bc, br, gm, cst, mst, dycat):
    t = zs.shape[0]
    l = LM
    nc = t // l
    dm = NH_M * DV_M

    def body(q_ref, k_ref, v_ref, o_ref, zs_ref, zsr_ref, bc_ref, br_ref, gm_ref, cst_ref, mst_ref, cnx_ref, mnx_ref,
             dy_ref, dqk_ref, dv_ref, do_ref, dzs_ref, dzr_ref, dgm_ref, dc_scr):
        @pl.when(pl.program_id(0) == 0)
        def _():
            dc_scr[...] = jnp.zeros_like(dc_scr)
            dgm_ref[...] = jnp.zeros_like(dgm_ref)

        lane = lax.broadcasted_iota(jnp.int32, (l, 128), 1)
        upper, lower = _tri(l, False), _tri(l, True)
        db_all, sig_c, carries = jnp.zeros((l, 128), F32), jnp.zeros((l, 128), F32), jnp.zeros((1, 128), F32)
        decays = _mlstm_decays(zs_ref, zsr_ref, bc_ref, br_ref)
        dzr_rows = [None] * 8
        dvs, dos, dgs, dqs, dks = [], [], [], [], []
        dzs = jnp.zeros((l, 128), F32)
        for h in range(NH_M):
            c_prev = cst_ref[h]
            m_prev = mst_ref[h:h + 1, 0:1]
            r = _mlstm_chunk(h, q_ref, k_ref, v_ref, zs_ref, zsr_ref, bc_ref, br_ref, c_prev, m_prev, decays)
            hh, mx, den, m_t, v1, amat = r["hh"], r["mx"], r["den"], r["m_t"], r["v1"], r["amat"]
            gh = gm_ref[:, h * DV_M:(h + 1) * DV_M]
            rs = lax.rsqrt(jnp.mean(hh * hh, axis=-1, keepdims=True) + EPS)
            xh = hh * rs
            sg = _sigmoid(o_ref[:, h * DV_M:(h + 1) * DV_M].astype(F32))
            dyh = dy_ref[:, h * DV_M:(h + 1) * DV_M]
            dos.append(dyh * xh * gh * sg * (1.0 - sg))
            dhn = dyh * sg
            dgs.append(_colsum(dhn * xh))
            dxh = dhn * gh
            dh = rs * (dxh - xh * jnp.mean(dxh * xh, axis=-1, keepdims=True))
            g1 = dh / mx
            hd = jnp.sum(hh * dh, axis=-1, keepdims=True)
            dden = jnp.where(jnp.abs(den) > jnp.exp(-m_t), -hd / mx * jnp.sign(den), 0.0)
            g256 = jnp.concatenate([g1, jnp.where(lane == 0, dden, 0.0)], axis=1)
            dc_h = dc_scr[h]
            ea = jnp.exp(r["a_c"])
            dp = _dot(g256, v1, 1, 1)
            ds = dp * amat
            dqs.append((r["w_inter"] * _dot(g256, c_prev, 1, 1) + _dot(ds, r["k"], 1, 0)) * (DK_M ** -0.5))
            dks.append(_dot(ds, r["q"], 0, 0) + ea * _dot(v1, dc_h, 1, 1))
            dv_st = ea * _dot(r["k"], dc_h, 1, 0)
            dv1 = _dot(r["p"], g256, 0, 0) + dv_st
            dvs.append(dv1[:, :DV_M])
            wmat = dp * r["p"]
            c_in = _colsum(wmat)
            c_st = jnp.sum(v1.astype(F32) * dv_st, axis=-1, keepdims=True)
            r_t = jnp.sum(wmat, axis=1, keepdims=True) + jnp.sum(g256 * r["qc_w"], axis=-1, keepdims=True)
            db = r_t - c_st
            carry = jnp.exp(mnx_ref[h:h + 1, 0:1]) * jnp.sum(
                jnp.sum(dc_h * cnx_ref[h], axis=1, keepdims=True), axis=0, keepdims=True)
            db_all = db_all + jnp.where(lane == NH_M + h, db, 0.0)
            sig_c = sig_c + jnp.where(lane == NH_M + h, _sigmoid(-r["fp_c"]), 0.0)
            carries = carries + jnp.where(lane[0:1, :] == NH_M + h, carry, 0.0)
            dzs = dzs + jnp.where(lane == h, c_st, 0.0)
            dzr_rows[h] = c_in
            dzr_rows[NH_M + h] = _sigmoid(-r["fp_r"])
            wq = r["q"] * jnp.exp(r["b_c"] - m_t)
            dc_scr[h] = jnp.exp(r["g"]) * dc_h + _dot(wq, g256, 0, 0)
        dzs = dzs + (_tri_dot(upper, db_all, True) + carries) * sig_c
        c_in4 = jnp.concatenate(dzr_rows[:NH_M], axis=0)
        dlf_r4 = -_tri_dot(c_in4, lower, False)
        dzr_rows = dzr_rows[:NH_M] + [dlf_r4[h:h + 1, :] * dzr_rows[NH_M + h] for h in range(NH_M)]
        dqk_ref[...] = jnp.concatenate(dqs + dks, axis=1)
        dv_ref[...] = jnp.concatenate(dvs, axis=1).astype(dv_ref.dtype)
        do_ref[...] = jnp.concatenate(dos, axis=1).astype(do_ref.dtype)
        dzs_ref[...] = dzs
        dzr_ref[...] = jnp.concatenate(dzr_rows, axis=0)
        dgm_ref[...] += jnp.concatenate(dgs, axis=1)

    rev = lambda i: nc - 1 - i
    nxt = lambda i: jnp.minimum(nc - i, nc - 1)
    return pl.pallas_call(
        body, name="mlstm_bwd", grid=(nc,),
        in_specs=[pl.BlockSpec((l, NH_M * DK_M), lambda i: (rev(i), 0)), pl.BlockSpec((l, NH_M * DK_M), lambda i: (rev(i), 1)),
                  pl.BlockSpec((l, dm), lambda i: (rev(i), 1)), pl.BlockSpec((l, dm), lambda i: (rev(i), 2)),
                  pl.BlockSpec((l, 128), lambda i: (rev(i), 0)), pl.BlockSpec((8, l), lambda i: (0, rev(i))),
                  pl.BlockSpec((1, 8), lambda i: (0, 0)), pl.BlockSpec((8, 1), lambda i: (0, 0)),
                  pl.BlockSpec((1, dm), lambda i: (0, 0)),
                  pl.BlockSpec((None, NH_M, DK_M, 2 * DV_M), lambda i: (rev(i), 0, 0, 0)),
                  pl.BlockSpec((None, 8, 128), lambda i: (rev(i), 0, 0)),
                  pl.BlockSpec((None, NH_M, DK_M, 2 * DV_M), lambda i: (nxt(i), 0, 0, 0)),
                  pl.BlockSpec((None, 8, 128), lambda i: (nxt(i), 0, 0)),
                  pl.BlockSpec((l, dm), lambda i: (rev(i), 0))],
        out_specs=[pl.BlockSpec((l, dm), lambda i: (rev(i), 0)),
                   pl.BlockSpec((l, dm), lambda i: (rev(i), 0)), pl.BlockSpec((l, dm), lambda i: (rev(i), 0)),
                   pl.BlockSpec((l, 128), lambda i: (rev(i), 0)), pl.BlockSpec((8, l), lambda i: (0, rev(i))),
                   pl.BlockSpec((1, dm), lambda i: (0, 0))],
        out_shape=[jax.ShapeDtypeStruct((t, dm), F32),
                   jax.ShapeDtypeStruct((t, dm), BF16), jax.ShapeDtypeStruct((t, dm), BF16),
                   jax.ShapeDtypeStruct((t, 128), F32), jax.ShapeDtypeStruct((8, t), F32),
                   jax.ShapeDtypeStruct((1, dm), F32)],
        scratch_shapes=[pltpu.VMEM((NH_M, DK_M, 2 * DV_M), F32)],
        compiler_params=_cparams(("arbitrary",)),
    )(qk, qk, zbig, zbig, zs, zsr, bc, br, gm, cst, mst, cst, mst, dycat)


def _fox_cumsum(zsr, bf_r):
    t = zsr.shape[1]
    cw = _pick(t, (512, 256))

    def body(z_ref, b_ref, c_ref):
        up = _tri(cw, False).astype(F32)
        carry = jnp.zeros((NH_F, 1), F32)
        for j in range(t // cw):
            cs = _f32dot(_logsig(z_ref[:, j * cw:(j + 1) * cw] + b_ref[...]), up) + carry
            c_ref[:, j * cw:(j + 1) * cw] = cs
            carry = cs[:, cw - 1:cw]

    return pl.pallas_call(
        body, name="fox_cumsum", grid=(1,),
        in_specs=[pl.BlockSpec((NH_F, t), lambda i: (1, 0)), pl.BlockSpec((NH_F, 1), lambda i: (0, 0))],
        out_specs=pl.BlockSpec((NH_F, t), lambda i: (0, 0)), out_shape=jax.ShapeDtypeStruct((NH_F, t), F32),
        compiler_params=_cparams(("arbitrary",)),
    )(zsr, bf_r)


def _fox_gate_bwd(zsr, bf_r, dc):
    t = zsr.shape[1]
    cw = _pick(t, (512, 256))

    def body(z_ref, b_ref, dc_ref, o_ref):
        low = _tri(cw, True).astype(F32)
        carry = jnp.zeros((NH_F, 1), F32)
        for j in reversed(range(t // cw)):
            sl = slice(j * cw, (j + 1) * cw)
            dlf = _f32dot(dc_ref[:, sl], low) + carry
            o_ref[:, sl] = dlf * _sigmoid(-(z_ref[:, sl] + b_ref[...]))
            carry = dlf[:, 0:1]

    return pl.pallas_call(
        body, name="fox_gate_bwd", grid=(1,),
        in_specs=[pl.BlockSpec((NH_F, t), lambda i: (1, 0)), pl.BlockSpec((NH_F, 1), lambda i: (0, 0)),
                  pl.BlockSpec((NH_F, t), lambda i: (0, 0))],
        out_specs=pl.BlockSpec((NH_F, t), lambda i: (0, 0)), out_shape=jax.ShapeDtypeStruct((NH_F, t), F32),
        compiler_params=_cparams(("arbitrary",)),
    )(zsr, bf_r, dc)


def _causal_mask(n):
    return _tri(n, True)


def _fox_fwd(q, k, v, c_col, c_row, gf):
    nh, t, dh = q.shape
    tq = _pick(t, (512, 256))
    scale = dh ** -0.5

    def body(q_ref, k_ref, v_ref, cc_ref, cr_ref, g_ref, o_ref, lse_ref, y_ref):
        i = pl.program_id(1)
        qv = q_ref[...]
        cq = cc_ref[...]

        def blk(j, carry, masked):
            m, l, acc = carry
            k0 = pl.multiple_of(j * tq, tq)
            kb = k_ref[pl.ds(k0, tq), :]
            vb = v_ref[pl.ds(k0, tq), :]
            s = _dot(qv, kb, 1, 1) * scale + cq - cr_ref[:, pl.ds(k0, tq)]
            if masked:
                s = jnp.where(_causal_mask(tq), s, -jnp.inf)
            m_new = jnp.maximum(m, jnp.max(s, axis=1, keepdims=True))
            alpha = jnp.exp(m - m_new)
            p = jnp.exp(s - m_new)
            return m_new, alpha * l + jnp.sum(p, axis=1, keepdims=True), alpha * acc + _dot(p, vb, 1, 0)

        init = (jnp.full((tq, 1), -jnp.inf, F32), jnp.zeros((tq, 1), F32), jnp.zeros((tq, dh), F32))
        carry = lax.fori_loop(0, i, lambda j, c: blk(j, c, False), init)
        m, l, acc = blk(i, carry, True)
        o = acc / l
        o_ref[...] = o
        lse_ref[...] = m + jnp.log(l)
        y_ref[...] = (o * lax.rsqrt(jnp.mean(o * o, axis=-1, keepdims=True) + EPS) * g_ref[...]).astype(y_ref.dtype)

    full = lambda w: pl.BlockSpec((None, t, w), lambda h, i: (h, 0, 0))
    tile = lambda w: pl.BlockSpec((None, tq, w), lambda h, i: (h, i, 0))
    return pl.pallas_call(
        body, name="fox_fwd", grid=(nh, t // tq),
        in_specs=[tile(dh), full(dh), full(dh), tile(1), pl.BlockSpec((None, 1, t), lambda h, i: (h, 0, 0)),
                  pl.BlockSpec((None, 1, dh), lambda h, i: (h, 0, 0))],
        out_specs=[tile(dh), tile(1), tile(dh)],
        out_shape=[jax.ShapeDtypeStruct((nh, t, dh), F32), jax.ShapeDtypeStruct((nh, t, 1), F32),
                   jax.ShapeDtypeStruct((nh, t, dh), BF16)],
        compiler_params=_cparams(("parallel", "parallel")),
    )(q, k, v, c_col, c_row, gf)


def _fox_norm_bwd(dy, o, gf):
    nh, t, dh = o.shape
    tm = _pick(t, (512, 256))

    def body(dy_ref, o_ref, g_ref, do_ref, dl_ref, dg_ref):
        ov = o_ref[...]
        dx, dg = _rms_bwd_val(dy_ref[...], ov, g_ref[...])
        do_ref[...] = dx
        dl_ref[...] = jnp.sum(dx * ov, axis=-1, keepdims=True)

        @pl.when(pl.program_id(1) == 0)
        def _():
            dg_ref[...] = jnp.zeros_like(dg_ref)
        dg_ref[...] += dg

    tile = lambda w: pl.BlockSpec((None, tm, w), lambda h, i: (h, i, 0))
    gspec = pl.BlockSpec((None, 1, dh), lambda h, i: (h, 0, 0))
    return pl.pallas_call(
        body, name="fox_norm_bwd", grid=(nh, t // tm), in_specs=[tile(dh), tile(dh), gspec],
        out_specs=[tile(dh), tile(1), gspec],
        out_shape=[jax.ShapeDtypeStruct((nh, t, dh), F32), jax.ShapeDtypeStruct((nh, t, 1), F32),
                   jax.ShapeDtypeStruct((nh, 1, dh), F32)],
        compiler_params=_cparams(("parallel", "arbitrary")),
    )(dy, o, gf)


def _fox_bwd(q, k, v, c_col, c_row, do, lse, delta):
    nh, t, dh = q.shape
    tq = _pick(t, (512, 256))
    nq = t // tq
    scale = dh ** -0.5

    def body(q_ref, k_ref, v_ref, cc_ref, cr_ref, do_ref, lse_ref, dl_ref, dq_ref, dk_ref, dv_ref, dc_ref, dcq_ref):
        j = pl.program_id(1)

        @pl.when(j == 0)
        def _():
            dq_ref[...] = jnp.zeros_like(dq_ref)
            dcq_ref[...] = jnp.zeros_like(dcq_ref)

        kb, vb, crb = k_ref[...], v_ref[...], cr_ref[...]

        def blk(i, carry, masked):
            dk, dv, dc = carry
            rows = pl.ds(pl.multiple_of(i * tq, tq), tq)
            qb = q_ref[rows, :]
            dob = do_ref[rows, :].astype(BF16)
            s = _dot(qb, kb, 1, 1) * scale + cc_ref[rows, :] - crb
            if masked:
                s = jnp.where(_causal_mask(tq), s, -jnp.inf)
            p = jnp.exp(s - lse_ref[rows, :])
            dv = dv + _dot(p, dob, 0, 0)
            ds = p * (_dot(dob, vb, 1, 1) - dl_ref[rows, :])
            dc = dc + _colsum(ds)
            dk = dk + _dot(ds, qb, 0, 0) * scale
            dq_ref[rows, :] += _dot(ds, kb, 1, 0) * scale
            dcq_ref[rows, :] += jnp.sum(ds, axis=1, keepdims=True)
            return dk, dv, dc

        init = (jnp.zeros((tq, dh), F32), jnp.zeros((tq, dh), F32), jnp.zeros((1, tq), F32))
        carry = blk(j, init, True)
        dk, dv, dc = lax.fori_loop(j + 1, nq, lambda i, c: blk(i, c, False), carry)
        dk_ref[...] = dk
        dv_ref[...] = dv
        dc_ref[...] = -dc

    full = lambda w: pl.BlockSpec((None, t, w), lambda h, j: (h, 0, 0))
    tile = lambda w: pl.BlockSpec((None, tq, w), lambda h, j: (h, j, 0))
    crow = pl.BlockSpec((None, 1, tq), lambda h, j: (h, 0, j))
    return pl.pallas_call(
        body, name="fox_bwd", grid=(nh, nq),
        in_specs=[full(dh), tile(dh), tile(dh), full(1), crow, full(dh), full(1), full(1)],
        out_specs=[full(dh), tile(dh), tile(dh), crow, full(1)],
        out_shape=[jax.ShapeDtypeStruct((nh, t, dh), F32)] * 3 + [jax.ShapeDtypeStruct((nh, 1, t), F32),
                                                                jax.ShapeDtypeStruct((nh, t, 1), F32)],
        compiler_params=_cparams(("parallel", "arbitrary")),
    )(q, k, v, c_col, c_row, do, lse, delta)


AUG = 64


def _split3(c):
    hi = c.astype(BF16).astype(F32)
    r1 = c - hi
    mid = r1.astype(BF16).astype(F32)
    return hi, mid, r1 - mid


def _fox_prep(zbig, ct):
    t = zbig.shape[0]
    tm = _pick(t, (512, 256))

    def body(q_ref, k_ref, v_ref, c_ref, qo_ref, ko_ref, vo_ref):
        lane = lax.broadcasted_iota(jnp.int32, (tm, AUG), 1)
        qv, kv, vv, cv = q_ref[...], k_ref[...], v_ref[...], c_ref[...]
        one = (lane == 0).astype(BF16)
        for h in range(NH_F):
            hi, mid, lo = _split3(cv[:, h:h + 1])
            aq = jnp.where(lane == 0, hi, jnp.where(lane == 1, mid, jnp.where(lane == 2, lo, jnp.where(lane < 6, 1.0, 0.0))))
            ak = jnp.where(lane < 3, 1.0, jnp.where(lane == 3, -hi, jnp.where(lane == 4, -mid, jnp.where(lane == 5, -lo, 0.0))))
            sl = slice(h * DH_F, (h + 1) * DH_F)
            qo_ref[h] = jnp.concatenate([qv[:, sl] * (DH_F ** -0.5), aq.astype(BF16)], axis=1).astype(BF16)
            ko_ref[h] = jnp.concatenate([kv[:, sl], ak.astype(BF16)], axis=1)
            vo_ref[h] = jnp.concatenate([vv[:, sl], one], axis=1)

    ospec = pl.BlockSpec((NH_F, tm, 128), lambda i: (0, i, 0))
    return pl.pallas_call(
        body, name="fox_prep", grid=(t // tm,),
        in_specs=[pl.BlockSpec((tm, 512), lambda i: (i, 3)), pl.BlockSpec((tm, 512), lambda i: (i, 4)),
                  pl.BlockSpec((tm, 512), lambda i: (i, 5)), pl.BlockSpec((tm, NH_F), lambda i: (i, 0))],
        out_specs=[ospec] * 3, out_shape=[jax.ShapeDtypeStruct((NH_F, t, 128), BF16)] * 3,
        compiler_params=_cparams(("parallel",)),
    )(zbig, zbig, zbig, ct)


def _fox_fwd2(qa, ka, va, gf, plan=None):
    nh, t, _ = qa.shape
    tq = _pick(t, (512, 256))
    nq = t // tq
    group = 4
    host = _Hosted(plan, 4, 3)

    def body(*refs):
        (q_ref, k_ref, v_ref, g_ref), (y_ref, o_ref, lse_ref), _, prefs = host.split(refs)
        i = pl.program_id(0)
        host.run(0, i == 0, prefs)
        host.run(1, i == max(nq - 2, 0), prefs)
        lane = lax.broadcasted_iota(jnp.int32, (tq, 128), 1)
        ys, os_ = [], []
        lse_all = jnp.zeros((tq, 128), F32)
        for h0 in range(0, nh, group):
            heads = range(h0, h0 + group)
            qvs = [q_ref[h] for h in heads]

            def blk(j, carry, masked, heads=heads, qvs=qvs):
                k0 = pl.multiple_of(j * tq, tq)
                out = []
                for (m, acc), h, qv in zip(carry, heads, qvs):
                    s = lax.dot_general(qv, k_ref[h, pl.ds(k0, tq), :], (((1,), (1,)), ((), ())), preferred_element_type=F32)
                    if masked:
                        s = jnp.where(_causal_mask(tq), s, -jnp.inf)
                    m_new = jnp.maximum(m, jnp.max(s, axis=1, keepdims=True))
                    p = jnp.exp(s - m_new).astype(BF16)
                    pv = lax.dot_general(p, v_ref[h, pl.ds(k0, tq), :], (((1,), (0,)), ((), ())), preferred_element_type=F32)
                    out.append((m_new, jnp.exp(m - m_new) * acc + pv))
                return tuple(out)

            init = tuple((jnp.full((tq, 1), -jnp.inf, F32), jnp.zeros((tq, 128), F32)) for _ in heads)
            carry = lax.fori_loop(0, i, lambda j, c: blk(j, c, False), init)
            for (m, acc), h in zip(blk(i, carry, True), heads):
                l = acc[:, DH_F:DH_F + 1]
                o = acc[:, :DH_F] / l
                os_.append(o)
                gh = g_ref[:, h * DH_F:(h + 1) * DH_F]
                ys.append(o * lax.rsqrt(jnp.mean(o * o, axis=-1, keepdims=True) + EPS) * gh)
                lse_all = lse_all + jnp.where(lane == h, m + jnp.log(l), 0.0)
        y_ref[...] = jnp.concatenate(ys, axis=1).astype(y_ref.dtype)
        o_ref[...] = jnp.concatenate(os_, axis=1)
        lse_ref[...] = lse_all
        host.run(2, i == nq - 1, prefs)

    full = pl.BlockSpec((nh, t, 128), lambda i: (0, 0, 0))
    hc = host.call_args()
    res = pl.pallas_call(
        body, name="fox_fwd", grid=(nq,),
        in_specs=[pl.BlockSpec((nh, tq, 128), lambda i: (0, i, 0)), full, full, pl.BlockSpec((1, nh * DH_F), lambda i: (0, 0))]
        + hc["in_specs"],
        out_specs=[pl.BlockSpec((tq, nh * DH_F), lambda i: (i, 0)), pl.BlockSpec((tq, nh * DH_F), lambda i: (i, 0)),
                   pl.BlockSpec((tq, 128), lambda i: (i, 0))] + hc["out_specs"],
        out_shape=[jax.ShapeDtypeStruct((t, nh * DH_F), BF16), jax.ShapeDtypeStruct((t, nh * DH_F), F32),
                   jax.ShapeDtypeStruct((t, 128), F32)] + hc["out_shape"],
        scratch_shapes=hc["scratch"], input_output_aliases=hc["aliases"], compiler_params=_cparams(("arbitrary",)),
    )(qa, ka, va, gf, *hc["args"])
    return res[:3], res[3:]


def _fox_bwd_prep(dycat, o, gf):
    t = o.shape[0]
    tm = _pick(t, (512, 256))

    def body(dy_ref, o_ref, g_ref, do_ref, dl_ref, dg_ref):
        lane = lax.broadcasted_iota(jnp.int32, (tm, 128), 1)
        dyv, ov, gv = dy_ref[...], o_ref[...], g_ref[...]
        dgs = []
        dl = jnp.zeros((tm, 128), F32)
        pad = jnp.zeros((tm, AUG), BF16)
        for h in range(NH_F):
            sl = slice(h * DH_F, (h + 1) * DH_F)
            dx, dg = _rms_bwd_val(dyv[:, sl], ov[:, sl], gv[:, sl])
            dgs.append(dg)
            do_ref[h] = jnp.concatenate([dx.astype(BF16), pad], axis=1)
            dl = dl + jnp.where(lane == h, jnp.sum(dx * ov[:, sl], axis=-1, keepdims=True), 0.0)
        dl_ref[...] = dl

        @pl.when(pl.program_id(0) == 0)
        def _():
            dg_ref[...] = jnp.zeros_like(dg_ref)
        dg_ref[...] += jnp.concatenate(dgs, axis=1)

    return pl.pallas_call(
        body, name="fox_bwd_prep", grid=(t // tm,),
        in_specs=[pl.BlockSpec((tm, 512), lambda i: (i, 1)), pl.BlockSpec((tm, 512), lambda i: (i, 0)),
                  pl.BlockSpec((1, 512), lambda i: (0, 0))],
        out_specs=[pl.BlockSpec((NH_F, tm, 128), lambda i: (0, i, 0)), pl.BlockSpec((tm, 128), lambda i: (i, 0)),
                   pl.BlockSpec((1, 512), lambda i: (0, 0))],
        out_shape=[jax.ShapeDtypeStruct((NH_F, t, 128), BF16), jax.ShapeDtypeStruct((t, 128), F32),
                   jax.ShapeDtypeStruct((1, 512), F32)],
        compiler_params=_cparams(("arbitrary",)),
    )(dycat, o, gf)


def _fox_bwd2(qa, ka, va, doa, lse, delta):
    nh, t, _ = qa.shape
    tq = _pick(t, (512, 256))
    nq = t // tq

    group = 2

    def tdot(a, b, cb):
        return lax.dot_general(a, b, (((0,), (cb,)), ((), ())), preferred_element_type=F32)

    def body(q_ref, k_ref, v_ref, do_ref, lse_ref, dl_ref, dq_ref, dk_ref, dv_ref):
        hp, j = pl.program_id(0), pl.program_id(1)

        @pl.when(j == 0)
        def _():
            dq_ref[...] = jnp.zeros_like(dq_ref)

        lane = lax.broadcasted_iota(jnp.int32, (tq, 128), 1)

        def blk(i, carry, masked):
            rows = pl.ds(pl.multiple_of(i * tq, tq), tq)
            lse_t, dl_t = lse_ref[rows, :], dl_ref[rows, :]
            out = []
            for g, (dk, dv) in enumerate(carry):
                h = hp * group + g
                kb, vb = k_ref[g], v_ref[g]
                qb, dob = q_ref[g, rows, :], do_ref[g, rows, :]
                lse_h = jnp.sum(jnp.where(lane == h, lse_t, 0.0), axis=1, keepdims=True)
                dl_h = jnp.sum(jnp.where(lane == h, dl_t, 0.0), axis=1, keepdims=True)
                s = lax.dot_general(qb, kb, (((1,), (1,)), ((), ())), preferred_element_type=F32)
                if masked:
                    s = jnp.where(_causal_mask(tq), s, -jnp.inf)
                p = jnp.exp(s - lse_h)
                dp = lax.dot_general(dob, vb, (((1,), (1,)), ((), ())), preferred_element_type=F32)
                ds = (p * (dp - dl_h)).astype(BF16)
                dv = dv + tdot(dob, p.astype(BF16), 0)
                dk = dk + tdot(qb, ds, 0)
                dq_ref[g, :, rows] += tdot(kb, ds, 1)
                out.append((dk, dv))
            return tuple(out)

        init = tuple((jnp.zeros((128, tq), F32), jnp.zeros((128, tq), F32)) for _ in range(group))
        carry = blk(j, init, True)
        carry = lax.fori_loop(j + 1, nq, lambda i, c: blk(i, c, False), carry)
        for g, (dk, dv) in enumerate(carry):
            dk_ref[g] = dk
            dv_ref[g] = dv

    full = pl.BlockSpec((group, t, 128), lambda h, j: (h, 0, 0))
    tile = pl.BlockSpec((group, tq, 128), lambda h, j: (h, j, 0))
    cols = pl.BlockSpec((t, 128), lambda h, j: (0, 0))
    full_t = pl.BlockSpec((group, 128, t), lambda h, j: (h, 0, 0))
    tile_t = pl.BlockSpec((group, 128, tq), lambda h, j: (h, 0, j))
    return pl.pallas_call(
        body, name="fox_bwd", grid=(nh // group, nq), in_specs=[full, tile, tile, full, cols, cols],
        out_specs=[full_t, tile_t, tile_t], out_shape=[jax.ShapeDtypeStruct((nh, 128, t), F32)] * 3,
        compiler_params=_cparams(("parallel", "arbitrary")),
    )(qa, ka, va, doa, lse, delta)


def _fox_bwd_post(dqa, dka, dva):
    nh, _, t = dqa.shape
    tm = _pick(t, (512, 256))

    def body(dq_ref, dk_ref, dv_ref, oq_ref, ok_ref, ov_ref, dc_ref):
        qs, ks, vs, dcs = [], [], [], []
        for h in range(nh):
            dq, dk = dq_ref[h], dk_ref[h]
            qs.append(dq.T[:, :DH_F] * (DH_F ** -0.5))
            ks.append(dk.T[:, :DH_F])
            vs.append(dv_ref[h].T[:, :DH_F])
            dcs.append(dq[DH_F:DH_F + 1, :] - dk[DH_F + 3:DH_F + 4, :])
        oq_ref[...] = jnp.concatenate(qs, axis=1).astype(BF16)
        ok_ref[...] = jnp.concatenate(ks, axis=1).astype(BF16)
        ov_ref[...] = jnp.concatenate(vs, axis=1).astype(BF16)
        dc_ref[...] = jnp.concatenate(dcs, axis=0)

    ispec = pl.BlockSpec((nh, 128, tm), lambda i: (0, 0, i))
    ospec = pl.BlockSpec((tm, nh * DH_F), lambda i: (i, 0))
    return pl.pallas_call(
        body, name="fox_bwd_post", grid=(t // tm,), in_specs=[ispec] * 3,
        out_specs=[ospec] * 3 + [pl.BlockSpec((nh, tm), lambda i: (0, i))],
        out_shape=[jax.ShapeDtypeStruct((t, nh * DH_F), BF16)] * 3 + [jax.ShapeDtypeStruct((nh, t), F32)],
        compiler_params=_cparams(("parallel",)),
    )(dqa, dka, dva)


W_BIG = 6 * 512
IN_OFF = (0, 512, 1024, 1544, 2056, 2568)
IN_GATES = (1536, 3080)


def _heads(a, nh):
    t = a.shape[0]
    return a.reshape(t, nh, -1).transpose(1, 0, 2)


def _unheads(a):
    nh, t, dh = a.shape
    return a.transpose(1, 0, 2).reshape(t, nh * dh)


FFN1 = ("ffn1_w_gate", "ffn1_w_up", "ffn1_w_down")
REST = ("w_in", "w_out", "ffn2_w_gate", "ffn2_w_up", "ffn2_w_down", "w_ple_gate", "w_ple_proj")
SPLIT = {n: 1 if n == "w_in" else 0 for n in FFN1 + REST}


def _rs_partials(names, gw, c_idx):
    wire = [_cast_other_half("rs_cast_" + n, gw[n], c_idx, SPLIT[n]) for n in names]
    swapped = _swap("rs_swap_" + names[0], wire)
    return [_add_my_half("rs_add_" + n, gw[n], r, c_idx, SPLIT[n]) for n, r in zip(names, swapped)]


def _local_step(x, p, tgt, sp, wg1, wu1, wd1, rest_slots, c_idx, place):
    t, d = x.shape
    slot = dict(zip(REST + ("conv_qk",), rest_slots))
    (h1, xn1, g1, u1), (w_in, conv_w) = _ffn_fwd(
        "ffn1", x, sp["ffn1_norm"], wg1, wu1, wd1, plan=_gather_plan([slot["w_in"], slot["conv_qk"]], [SPLIT["w_in"], None]))
    w_in, conv_w = w_in.reshape(-1, d), _from_chip_blocks(conv_w)
    w_big = jnp.concatenate([w_in[o:o + 512] for o in IN_OFF], axis=0)
    w_small = jnp.concatenate([w_in[IN_GATES[0]:IN_GATES[0] + 8], w_in[IN_GATES[1]:IN_GATES[1] + 8],
                               jnp.zeros((112, d), w_in.dtype)], axis=0)
    u, zbig = _norm_mm("in_big", h1, sp["mix_norm"], w_big, True, BF16)
    zs = _mm_nt("in_small", u, w_small, tm=1024, tk=1024)
    zsr = zs.T
    qk_act = _conv_fwd(zbig, conv_w)
    bm_c, bf_c = sp["b_mlstm_gates"], sp["b_fox_f"]
    y_m, cst, mst = _mlstm_fwd(qk_act, zbig, zs, zsr, bm_c, bm_c.T, sp["mlstm_out_norm"])
    c = _fox_cumsum(zsr, bf_c.T)
    qa, ka, va = _fox_prep(zbig, c.T)
    (y_ft, o_f, lse), late = _fox_fwd2(qa, ka, va, sp["fox_out_norm"],
                                       plan=_gather_plan([slot[n] for n in REST[1:]], [SPLIT[n] for n in REST[1:]]))
    full = dict(zip(REST[1:], late))
    w_out, w_pg = (full[n].reshape(-1, d) for n in ("w_out", "w_ple_gate"))
    wg2, wu2, wd2 = full["ffn2_w_gate"], full["ffn2_w_up"], full["ffn2_w_down"]
    w_pp = _from_chip_blocks(full["w_ple_proj"])
    tm = _pick(t, (1024, 512, 256))
    h2 = _mm("out_proj", [(y_m, (tm, 512), lambda i, j, k: (i, 0), w_out, (512, d), lambda i, j, k: (0, 0)),
                          (y_ft, (tm, 512), lambda i, j, k: (i, 0), w_out, (512, d), lambda i, j, k: (1, 0))],
             (t, d), (tm, d), lambda i, j, k: (i, 0), (t // tm, 1, 1), 2, res=h1)
    (h3, xn2, g2, u2), _ = _ffn_fwd("ffn2", h2, sp["ffn2_norm"], wg2, wu2, wd2)
    hn3, gate_pre = _norm_mm("ple_gate", h3, sp["ple_gate_norm"], w_pg, False, F32)
    pp = _mm_nn("ple_proj", p, w_pp, tm=1024)

    def head_fn(h3_t, gp_t, pp_t, tgt_t, g_pp, g_fin):
        gate = _sigmoid(gp_t)
        ppn = _rms_fwd_val(pp_t, g_pp)
        h4 = h3_t + gate * ppn
        err = _rms_fwd_val(h4, g_fin) - tgt_t
        loss = 0.5 * jnp.sum(jnp.mean(err * err, axis=-1, keepdims=True), axis=0, keepdims=True)
        dh4, dg_fin = _rms_bwd_val(err * (1.0 / d), h4, g_fin)
        dpp, dg_pp = _rms_bwd_val(dh4 * gate, pp_t, g_pp)
        dgp = dh4 * ppn * gate * (1.0 - gate)
        return dh4, dgp, dpp, jnp.broadcast_to(loss, (1, 128)), dg_fin, dg_pp

    dh4, dgp, dpp, loss_part, dg_fin, dg_pp = _rowwise(
        "loss_head", head_fn, [h3, gate_pre, pp, tgt], [sp["ple_proj_norm"], sp["final_norm"]],
        [(d, F32), (d, BF16), (d, BF16)], [((1, 128), F32), ((1, d), F32), ((1, d), F32)])
    gw, gs = {}, {"final_norm": dg_fin, "ple_proj_norm": dg_pp}
    gw["w_ple_gate"] = _mm_tn("d_w_pg", hn3, dgp, tm=1024, tn=1024)
    gw["w_ple_proj"] = _mm_tn("d_w_pp", p, dpp, tn=1024)
    dhn3 = _mm_nt("d_hn3", dgp, w_pg, tm=1024, tn=1024, tk=1024)

    def res_norm_bwd(dn_t, h_t, dres_t, g):
        dx, dg = _rms_bwd_val(dn_t, h_t, g)
        return dres_t + dx, dg

    dh3, gs["ple_gate_norm"] = _rowwise("ple_norm_bwd", res_norm_bwd, [dhn3, h3, dh4], [sp["ple_gate_norm"]],
                                        [(d, F32)], [((1, d), F32)])
    (dh2, gs["ffn2_norm"], gw["ffn2_w_gate"], gw["ffn2_w_up"], gw["ffn2_w_down"]), _ = _ffn_bwd(
        "ffn2", dh3, h2, sp["ffn2_norm"], xn2, g2, u2, wg2, wu2, wd2)
    dycat = _mm_nt("d_ycat", dh2, w_out, tm=1024, tn=1024, tk=1024)
    gw["w_out"] = jnp.concatenate([_mm_tn("d_w_out_m", y_m, dh2, tn=1024), _mm_tn("d_w_out_f", y_ft, dh2, tn=1024)], axis=0)
    doa, delta, gs["fox_out_norm"] = _fox_bwd_prep(dycat, o_f, sp["fox_out_norm"])
    dq_f, dk_f, dv_f, dct = _fox_bwd_post(*_fox_bwd2(qa, ka, va, doa, lse, delta))
    dfp = _fox_gate_bwd(zsr, bf_c.T, dct)
    dact, dv_m, do_m, dzs_m, dzr_m, gs["mlstm_out_norm"] = _mlstm_bwd(
        qk_act, zbig, zs, zsr, bm_c, bm_c.T, sp["mlstm_out_norm"], cst, mst, dycat)
    dqk, gw["conv_qk"] = _conv_bwd(zbig, dact, conv_w)
    dz_big = jnp.concatenate([dqk, dv_m, do_m, dq_f, dk_f, dv_f], axis=1)
    dzs = dzs_m + jnp.pad(jnp.concatenate([dzr_m, dfp], axis=0).T, ((0, 0), (0, 112)))
    dw_big = _mm_tn("d_w_big", dz_big, u, tn=1024)
    dw_small = _mm_tn("d_w_small", dzs, u, tn=1024)
    gw["w_in"] = jnp.concatenate([dw_big[0:1536], dw_small[0:8], dw_big[1536:3072], dw_small[8:16]], axis=0)
    du_a = _mm_nn("d_u_big", dz_big, w_big, tm=1024, tn=1024, tk=1024)
    du_b = _mm_nn("d_u_small", dzs, w_small, tm=1024, tn=1024)

    def mix_norm_bwd(da_t, db_t, h_t, dres_t, dzs_t, g):
        dx, dg = _rms_bwd_val(da_t + db_t, h_t, g)
        return dres_t + dx, dg, _colsum(dzs_t)

    dh1, gs["mix_norm"], dbias = _rowwise("mix_norm_bwd", mix_norm_bwd, [du_a, du_b, h1, dh2, dzs], [sp["mix_norm"]],
                                          [(d, F32)], [((1, d), F32), ((1, 128), F32)])
    gs["b_mlstm_gates"], gs["b_fox_f"] = dbias[:, 0:8], dbias[:, 8:16]
    conv_grad = gw.pop("conv_qk")
    gw["w_ple_proj"] = _chip_blocks(gw["w_ple_proj"])
    for n in ("w_in", "w_out", "w_ple_gate"):
        gw[n] = gw[n].reshape(4, -1, gw[n].shape[-1])
    part_rest = dict(zip(REST, _rs_partials(REST, gw, c_idx)))
    light = ("w_in", "w_out", "w_ple_gate", "w_ple_proj")
    part_ffn1 = []

    def own_plan(dwg, dwu, dwd):
        part_ffn1.extend(_rs_partials(FFN1, dict(zip(FFN1, (dwg, dwu, dwd))), c_idx))
        return _scatter_plan([pb for _, pb in part_ffn1])

    (grad_x, gs["ffn1_norm"], _, _, _), (l_light, l_down, l_gate, l_up, landed_ffn1) = _ffn_bwd_late_dx(
        "ffn1", dh1, x, sp["ffn1_norm"], xn1, g1, u1, wg1, wu1, wd1,
        _scatter_plan([part_rest[n][1] for n in light]),
        [_scatter_plan([part_rest[n][1]]) for n in ("ffn2_w_down", "ffn2_w_gate", "ffn2_w_up")], own_plan)
    landed_rest = dict(zip(light + ("ffn2_w_down", "ffn2_w_gate", "ffn2_w_up"), list(l_light) + [l_down[0], l_gate[0], l_up[0]]))
    names = REST + FFN1
    parts = [part_rest[n] for n in REST] + part_ffn1
    landed = [landed_rest[n] for n in REST] + list(landed_ffn1)
    mine = [_sum4("rs_sum_" + n, a, pf, place, SPLIT[n]) for n, a, (pf, _) in zip(names, landed, parts)]
    grads = dict(zip(names, _join_halves("rs_join", mine, [SPLIT[n] for n in names])))
    return loss_part, grad_x, grads, gs, conv_grad


ANY = pl.BlockSpec(memory_space=pl.ANY)
MESH = pl.DeviceIdType.MESH


def _place():
    x, y, c = lax.axis_index("x"), lax.axis_index("y"), lax.axis_index("c")
    chips = [(1 - x, y), (x, 1 - y), (1 - x, 1 - y)]
    return x, y, c, 2 * x + y, (x, y, 1 - c), chips


def _rcopy(src, dst, ssem, rsem, dev):
    return pltpu.make_async_remote_copy(src_ref=src, dst_ref=dst, send_sem=ssem, recv_sem=rsem, device_id=dev,
                                        device_id_type=MESH)


def _half(ref, lead, axis, idx, half):
    return ref.at[(slice(None),) * (lead + axis) + (pl.ds(idx * half, half),)]


def _to_slot(name, a, me_idx, dtype):
    r, cdim = a.shape
    tr = _pick(r, (256, 176, 128, 64))

    def body(me_ref, a_ref, o_ref):
        o_ref[...] = a_ref[...].astype(o_ref.dtype)

    return pl.pallas_call(
        body, name=name,
        grid_spec=pltpu.PrefetchScalarGridSpec(
            num_scalar_prefetch=1, grid=(r // tr,), in_specs=[pl.BlockSpec((tr, cdim), lambda i, me_ref: (i, 0))],
            out_specs=pl.BlockSpec((None, tr, cdim), lambda i, me_ref: (me_ref[0], i, 0))),
        out_shape=jax.ShapeDtypeStruct((4, r, cdim), dtype), compiler_params=_cparams(("parallel",)),
    )(me_idx, a)


def _gather4(name, bufs, split):
    return _run_plan(name, _gather_plan(bufs, split))


def _gather_plan(bufs, split):
    n = len(bufs)
    shapes = [b.shape[1:] for b in bufs]

    def ctx(outs):
        x, y, c, me, sib, chips = _place()

        def part(ref, a, which):
            if split[a] is None:
                return ref
            return _half(ref, 0, split[a], which, shapes[a][split[a]] // 2)

        return c, me, sib, chips, part

    def ici(outs, sems, a, j, chip, c, me, part):
        mine = part(outs[a].at[me], a, c)
        return _rcopy(mine, mine, sems[0].at[3 * a + j], sems[1].at[3 * a + j], (*chip, c))

    def fwd(outs, sems, a, j, chip, c, sib, part, which):
        blk = part(outs[a].at[2 * chip[0] + chip[1]], a, which)
        return _rcopy(blk, blk, sems[2].at[3 * a + j], sems[3].at[3 * a + j], sib)

    def start(ins, outs, sems):
        c, me, sib, chips, part = ctx(outs)
        for a in range(n):
            for j, chip in enumerate(chips):
                ici(outs, sems, a, j, chip, c, me, part).start()

    def mid(ins, outs, sems):
        c, me, sib, chips, part = ctx(outs)
        for j, chip in enumerate(chips):
            for a in range(n):
                blk = part(outs[a].at[2 * chip[0] + chip[1]], a, c)
                _rcopy(blk, blk, sems[0].at[3 * a + j], sems[1].at[3 * a + j], sib).wait_recv()
                if split[a] is not None:
                    fwd(outs, sems, a, j, chip, c, sib, part, c).start()

    def end(ins, outs, sems):
        c, me, sib, chips, part = ctx(outs)
        for j, chip in enumerate(chips):
            for a in range(n):
                if split[a] is not None:
                    fwd(outs, sems, a, j, chip, c, sib, part, 1 - c).wait_recv()
        for a in range(n):
            for j, chip in enumerate(chips):
                ici(outs, sems, a, j, chip, c, me, part).wait_send()
                if split[a] is not None:
                    fwd(outs, sems, a, j, chip, c, sib, part, c).wait_send()

    return dict(ins=list(bufs), outs=[jax.ShapeDtypeStruct(b.shape, b.dtype) for b in bufs], alias=True,
                sems=[pltpu.SemaphoreType.DMA((3 * n,))] * 4, phases=(start, mid, end))


def _run_plan(name, plan):
    ni, no = len(plan["ins"]), len(plan["outs"])

    def body(*refs):
        ins, outs, sems = refs[:ni], refs[ni:ni + no], refs[ni + no:]
        for phase in plan["phases"]:
            phase(ins, outs, sems)

    return pl.pallas_call(
        body, name=name, in_specs=[ANY] * ni, out_specs=[ANY] * no, out_shape=plan["outs"],
        input_output_aliases={a: a for a in range(ni)} if plan["alias"] else {}, scratch_shapes=plan["sems"],
    )(*plan["ins"])


class _Hosted:
    def __init__(self, plan, n_in, n_out):
        self.plan, self.n_in, self.n_out = plan, n_in, n_out
        self.ni, self.no, self.ns = (len(plan["ins"]) if plan else 0, len(plan["outs"]) if plan else 0,
                                     len(plan["sems"]) if plan else 0)

    def split(self, refs):
        a, b = self.n_in, self.n_in + self.ni
        c, d = b + self.n_out, b + self.n_out + self.no
        e = len(refs) - self.ns
        return refs[:a], refs[b:c], refs[d:e], (refs[a:b], refs[c:d], refs[e:])

    def run(self, k, cond, prefs):
        if self.plan is not None:
            @pl.when(cond)
            def _():
                self.plan["phases"][k](*prefs)

    def call_args(self):
        p = self.plan
        if p is None:
            return dict(in_specs=[], out_specs=[], out_shape=[], scratch=[], aliases={}, args=[])
        al = {self.n_in + a: self.n_out + a for a in range(self.ni)} if p["alias"] else {}
        return dict(in_specs=[ANY] * self.ni, out_specs=[ANY] * self.no, out_shape=list(p["outs"]), scratch=list(p["sems"]),
                    aliases=al, args=list(p["ins"]))


def _swap(name, arrs):
    n = len(arrs)

    def body(*refs):
        ins, outs = refs[:n], refs[n:2 * n]
        ssem, rsem = refs[2 * n:]
        x, y, c, me, sib, chips = _place()
        cps = [_rcopy(ins[a], outs[a], ssem.at[a], rsem.at[a], sib) for a in range(n)]
        for cp in cps:
            cp.start()
        for cp in cps:
            cp.wait()

    return pl.pallas_call(
        body, name=name, in_specs=[ANY] * n, out_specs=[ANY] * n,
        out_shape=[jax.ShapeDtypeStruct(a.shape, a.dtype) for a in arrs],
        scratch_shapes=[pltpu.SemaphoreType.DMA((n,))] * 2,
    )(*arrs)


def _scatter4(name, arrs):
    return _run_plan(name, _scatter_plan(arrs))


def _scatter_plan(arrs):
    n = len(arrs)

    def send(ins, outs, sems, a, j, chip, c, me):
        return _rcopy(ins[a].at[2 * chip[0] + chip[1]], outs[a].at[me], sems[0].at[3 * a + j], sems[1].at[3 * a + j], (*chip, c))

    def start(ins, outs, sems):
        x, y, c, me, sib, chips = _place()
        for a in range(n):
            for j, chip in enumerate(chips):
                send(ins, outs, sems, a, j, chip, c, me).start()

    def mid(ins, outs, sems):
        pass

    def end(ins, outs, sems):
        x, y, c, me, sib, chips = _place()
        for a in range(n):
            for j, chip in enumerate(chips):
                blk = outs[a].at[2 * chip[0] + chip[1]]
                _rcopy(blk, blk, sems[0].at[3 * a + j], sems[1].at[3 * a + j], sib).wait_recv()
        for a in range(n):
            for j, chip in enumerate(chips):
                send(ins, outs, sems, a, j, chip, c, me).wait_send()

    return dict(ins=list(arrs), outs=[jax.ShapeDtypeStruct(a.shape, a.dtype) for a in arrs], alias=False,
                sems=[pltpu.SemaphoreType.DMA((3 * n,))] * 2, phases=(start, mid, end))


def _join_halves(name, arrs, split):
    n = len(arrs)

    def body(*refs):
        outs = refs[n:2 * n]
        ssem, rsem = refs[2 * n:]
        x, y, c, me, sib, chips = _place()
        cps = []
        for a in range(n):
            mine = _half(outs[a], 0, split[a], c, arrs[a].shape[split[a]] // 2)
            cp = _rcopy(mine, mine, ssem.at[a], rsem.at[a], sib)
            cp.start()
            cps.append(cp)
        for a in range(n):
            blk = _half(outs[a], 0, split[a], 1 - c, arrs[a].shape[split[a]] // 2)
            _rcopy(blk, blk, ssem.at[a], rsem.at[a], sib).wait_recv()
        for cp in cps:
            cp.wait_send()

    return pl.pallas_call(
        body, name=name, in_specs=[ANY] * n, out_specs=[ANY] * n,
        out_shape=[jax.ShapeDtypeStruct(a.shape, a.dtype) for a in arrs],
        input_output_aliases={a: a for a in range(n)}, scratch_shapes=[pltpu.SemaphoreType.DMA((n,))] * 2,
    )(*arrs)


def _allreduce_small(s):
    r, cdim = s.shape

    def body(s_ref, o_ref, buf, ssem, rsem):
        x, y, c, me, sib, chips = _place()
        me8 = 4 * x + 2 * y + c
        buf[me8] = s_ref[...]
        flips = [(fx, fy, fc) for fx in (0, 1) for fy in (0, 1) for fc in (0, 1)][1:]
        cps = []
        for k, (fx, fy, fc) in enumerate(flips):
            peer = (x ^ fx if fx else x, y ^ fy if fy else y, c ^ fc if fc else c)
            cp = _rcopy(s_ref, buf.at[me8], ssem.at[k], rsem.at[k], peer)
            cp.start()
            cps.append(cp)
        for k, (fx, fy, fc) in enumerate(flips):
            src = 4 * (x ^ fx if fx else x) + 2 * (y ^ fy if fy else y) + (c ^ fc if fc else c)
            _rcopy(s_ref, buf.at[src], ssem.at[k], rsem.at[k], sib).wait_recv()
        for cp in cps:
            cp.wait_send()
        acc = buf[0]
        for k in range(1, 8):
            acc = acc + buf[k]
        o_ref[...] = acc

    vm = pl.BlockSpec(memory_space=pltpu.VMEM)
    return pl.pallas_call(
        body, name="allreduce_small", in_specs=[vm], out_specs=vm, out_shape=jax.ShapeDtypeStruct((r, cdim), F32),
        scratch_shapes=[pltpu.VMEM((8, r, cdim), F32), pltpu.SemaphoreType.DMA((7,)), pltpu.SemaphoreType.DMA((7,))],
    )(s)


def _add_my_half(name, g, recv, c_idx, axis):
    nb, hr, hc = recv.shape
    tr = _pick(hr, (256, 176, 128, 64))
    if axis == 0:
        g4 = g.reshape(nb, 2, hr, hc)
        gspec = pl.BlockSpec((None, None, tr, hc), lambda b, i, c_ref: (b, c_ref[0], i, 0))
    else:
        g4 = g
        gspec = pl.BlockSpec((None, tr, hc), lambda b, i, c_ref: (b, i, c_ref[0]))

    def body(c_ref, g_ref, r_ref, o_ref, ob_ref):
        s = g_ref[...] + r_ref[...].astype(F32)
        o_ref[...] = s
        ob_ref[...] = s.astype(BF16)

    ospec = pl.BlockSpec((None, tr, hc), lambda b, i, c_ref: (b, i, 0))
    return pl.pallas_call(
        body, name=name,
        grid_spec=pltpu.PrefetchScalarGridSpec(
            num_scalar_prefetch=1, grid=(nb, hr // tr), in_specs=[gspec, ospec], out_specs=[ospec, ospec]),
        out_shape=[jax.ShapeDtypeStruct((nb, hr, hc), F32), jax.ShapeDtypeStruct((nb, hr, hc), BF16)],
        compiler_params=_cparams(("parallel", "parallel")),
    )(c_idx, g4, recv)


def _sum4(name, landed, own, place, axis):
    nb, h, cdim = landed.shape
    tr = _pick(h, (256, 176, 128, 64))
    nt = h // tr

    def body(p_ref, a1_ref, a2_ref, a3_ref, own_ref, o_ref):
        o_ref[...] = ((own_ref[...] + a1_ref[...].astype(F32)) + a2_ref[...].astype(F32)) + a3_ref[...].astype(F32)

    def nxt(k):
        return pl.BlockSpec((None, tr, cdim), lambda i, p_ref: ((p_ref[0] + k) % nb, i, 0))

    if axis == 0:
        ospec = pl.BlockSpec((tr, cdim), lambda i, p_ref: (p_ref[1] * nt + i, 0))
        oshape = (2 * h, cdim)
    else:
        ospec = pl.BlockSpec((tr, cdim), lambda i, p_ref: (i, p_ref[1]))
        oshape = (h, 2 * cdim)
    return pl.pallas_call(
        body, name=name,
        grid_spec=pltpu.PrefetchScalarGridSpec(
            num_scalar_prefetch=1, grid=(nt,), in_specs=[nxt(1), nxt(2), nxt(3), nxt(0)], out_specs=ospec),
        out_shape=jax.ShapeDtypeStruct(oshape, F32), compiler_params=_cparams(("parallel",)),
    )(place, landed, landed, landed, own)


def _cast_other_half(name, g, c_idx, axis):
    nb, r, cdim = g.shape
    hr, hc = (r // 2, cdim) if axis == 0 else (r, cdim // 2)
    tr = _pick(hr, (256, 176, 128, 64))
    if axis == 0:
        g4 = g.reshape(nb, 2, hr, hc)
        gspec = pl.BlockSpec((None, None, tr, hc), lambda b, i, c_ref: (b, 1 - c_ref[0], i, 0))
    else:
        g4 = g
        gspec = pl.BlockSpec((None, tr, hc), lambda b, i, c_ref: (b, i, 1 - c_ref[0]))

    def body(c_ref, g_ref, o_ref):
        o_ref[...] = g_ref[...].astype(BF16)

    return pl.pallas_call(
        body, name=name,
        grid_spec=pltpu.PrefetchScalarGridSpec(
            num_scalar_prefetch=1, grid=(nb, hr // tr), in_specs=[gspec],
            out_specs=pl.BlockSpec((None, tr, hc), lambda b, i, c_ref: (b, i, 0))),
        out_shape=jax.ShapeDtypeStruct((nb, hr, hc), BF16), compiler_params=_cparams(("parallel", "parallel")),
    )(c_idx, g4)


def _adamw(name, w, g, m, v):
    c1 = 1.0 - ADAM_B1 ** ADAM_STEP
    c2 = 1.0 - ADAM_B2 ** ADAM_STEP

    def fn(w_t, g_t, m_t, v_t):
        m_n = ADAM_B1 * m_t + (1.0 - ADAM_B1) * g_t
        v_n = ADAM_B2 * v_t + (1.0 - ADAM_B2) * (g_t * g_t)
        delta = -ADAM_LR * ((m_n / c1) / (jnp.sqrt(v_n / c2) + ADAM_EPS) + ADAM_WD * w_t)
        return delta, m_n, v_n

    cdim = w.shape[1]
    return _rowwise(name, fn, [w, g, m, v], [], [(cdim, F32)] * 3, tm=_pick(w.shape[0], (256, 176, 128, 64, 8)))


BIG = ("ffn1_w_gate", "ffn1_w_up", "ffn1_w_down", "w_in", "w_out", "ffn2_w_gate", "ffn2_w_up", "ffn2_w_down",
       "w_ple_gate", "w_ple_proj")
SMALL = ("ffn1_norm", "mix_norm", "b_mlstm_gates", "b_fox_f", "mlstm_out_norm", "fox_out_norm", "ffn2_norm",
         "ple_gate_norm", "ple_proj_norm", "final_norm")
WEIGHTS = ("ffn1_norm", "ffn1_w_gate", "ffn1_w_up", "ffn1_w_down", "mix_norm", "w_in", "conv_qk", "b_mlstm_gates",
           "b_fox_f", "mlstm_out_norm", "fox_out_norm", "w_out", "ffn2_norm", "ffn2_w_gate", "ffn2_w_up", "ffn2_w_down",
           "ple_gate_norm", "w_ple_gate", "w_ple_proj", "ple_proj_norm", "final_norm")
TRANSPOSED = ("ffn1_w_gate", "ffn1_w_up", "w_in", "ffn2_w_gate", "ffn2_w_up")
PACK_W = 1024


def _chip_blocks(a):
    r, c4 = a.shape
    return a.reshape(r, 4, c4 // 4).transpose(1, 0, 2)


def _from_chip_blocks(a):
    nb, r, c = a.shape
    return a.transpose(1, 0, 2).reshape(r, nb * c)


def kernel(x, p, ffn1_norm, ffn1_w_gate, ffn1_w_up, ffn1_w_down, mix_norm, w_in, conv_qk, b_mlstm_gates, b_fox_f, mlstm_out_norm, fox_out_norm, w_out, ffn2_norm, ffn2_w_gate, ffn2_w_up, ffn2_w_down, ple_gate_norm, w_ple_gate, w_ple_proj, ple_proj_norm, final_norm, loss_target, m_ffn1_norm, m_ffn1_w_gate, m_ffn1_w_up, m_ffn1_w_down, m_mix_norm, m_w_in, m_conv_qk, m_b_mlstm_gates, m_b_fox_f, m_mlstm_out_norm, m_fox_out_norm, m_w_out, m_ffn2_norm, m_ffn2_w_gate, m_ffn2_w_up, m_ffn2_w_down, m_ple_gate_norm, m_w_ple_gate, m_w_ple_proj, m_ple_proj_norm, m_final_norm, v_ffn1_norm, v_ffn1_w_gate, v_ffn1_w_up, v_ffn1_w_down, v_mix_norm, v_w_in, v_conv_qk, v_b_mlstm_gates, v_b_fox_f, v_mlstm_out_norm, v_fox_out_norm, v_w_out, v_ffn2_norm, v_ffn2_w_gate, v_ffn2_w_up, v_ffn2_w_down, v_ple_gate_norm, v_w_ple_gate, v_w_ple_proj, v_ple_proj_norm, v_final_norm):
    w = dict(ffn1_norm=ffn1_norm, ffn1_w_gate=ffn1_w_gate, ffn1_w_up=ffn1_w_up, ffn1_w_down=ffn1_w_down, mix_norm=mix_norm,
             w_in=w_in, conv_qk=conv_qk, b_mlstm_gates=b_mlstm_gates, b_fox_f=b_fox_f, mlstm_out_norm=mlstm_out_norm,
             fox_out_norm=fox_out_norm, w_out=w_out, ffn2_norm=ffn2_norm, ffn2_w_gate=ffn2_w_gate, ffn2_w_up=ffn2_w_up,
             ffn2_w_down=ffn2_w_down, ple_gate_norm=ple_gate_norm, w_ple_gate=w_ple_gate, w_ple_proj=w_ple_proj,
             ple_proj_norm=ple_proj_norm, final_norm=final_norm)
    m = dict(ffn1_norm=m_ffn1_norm, ffn1_w_gate=m_ffn1_w_gate, ffn1_w_up=m_ffn1_w_up, ffn1_w_down=m_ffn1_w_down,
             mix_norm=m_mix_norm, w_in=m_w_in, conv_qk=m_conv_qk, b_mlstm_gates=m_b_mlstm_gates, b_fox_f=m_b_fox_f,
             mlstm_out_norm=m_mlstm_out_norm, fox_out_norm=m_fox_out_norm, w_out=m_w_out, ffn2_norm=m_ffn2_norm,
             ffn2_w_gate=m_ffn2_w_gate, ffn2_w_up=m_ffn2_w_up, ffn2_w_down=m_ffn2_w_down, ple_gate_norm=m_ple_gate_norm,
             w_ple_gate=m_w_ple_gate, w_ple_proj=m_w_ple_proj, ple_proj_norm=m_ple_proj_norm, final_norm=m_final_norm)
    v = dict(ffn1_norm=v_ffn1_norm, ffn1_w_gate=v_ffn1_w_gate, ffn1_w_up=v_ffn1_w_up, ffn1_w_down=v_ffn1_w_down,
             mix_norm=v_mix_norm, w_in=v_w_in, conv_qk=v_conv_qk, b_mlstm_gates=v_b_mlstm_gates, b_fox_f=v_b_fox_f,
             mlstm_out_norm=v_mlstm_out_norm, fox_out_norm=v_fox_out_norm, w_out=v_w_out, ffn2_norm=v_ffn2_norm,
             ffn2_w_gate=v_ffn2_w_gate, ffn2_w_up=v_ffn2_w_up, ffn2_w_down=v_ffn2_w_down, ple_gate_norm=v_ple_gate_norm,
             w_ple_gate=v_w_ple_gate, w_ple_proj=v_w_ple_proj, ple_proj_norm=v_ple_proj_norm, final_norm=v_final_norm)
    shapes = {n: w[n].shape for n in WEIGHTS}

    def view(a, n):
        return a[0].T if n in TRANSPOSED else a.reshape(-1, a.shape[-1])

    def unview(a, n):
        return (a.T if n in TRANSPOSED else a).reshape(shapes[n])

    w2, m2, v2 = ({n: view(a, n) for n, a in d.items()} for d in (w, m, v))

    c_idx = lax.axis_index("c").astype(jnp.int32).reshape(1)
    me_idx = (2 * lax.axis_index("x") + lax.axis_index("y")).astype(jnp.int32).reshape(1)
    place = jnp.concatenate([me_idx, c_idx])
    slot = {n: _to_slot("slot_" + n, w2[n], me_idx, BF16) for n in BIG}
    slot["conv_qk"] = _to_slot("slot_conv_qk", w2["conv_qk"], me_idx, F32)
    wg1, wu1, wd1 = _gather4("gather_ffn1", [slot[n] for n in FFN1], [SPLIT[n] for n in FFN1])
    sp = {n: w2[n] for n in SMALL}
    loss_part, grad_x, grads, gs, conv_grad = _local_step(
        x[0], p[0, 0], loss_target[0], sp, wg1, wu1, wd1, [slot[n] for n in REST + ("conv_qk",)], c_idx, place)
    loss = lax.psum(loss_part[0, 0], ("x", "y", "c"))

    small = [gs[n].reshape(1, -1) for n in SMALL] + [conv_grad]
    rows = [jnp.pad(a, ((0, 0), (0, PACK_W - a.shape[1]))) for a in small]
    packed = jnp.concatenate(rows, axis=0)
    packed = jnp.pad(packed, ((0, -packed.shape[0] % 8), (0, 0)))
    red = _allreduce_small(packed)
    for i, n in enumerate(SMALL):
        grads[n] = red[i:i + 1, :gs[n].size]
    dconv = red[len(SMALL):len(SMALL) + CONV_W, :conv_grad.shape[1]]
    cw = conv_qk.shape[-1]
    grads["conv_qk"] = lax.dynamic_slice_in_dim(dconv, (2 * lax.axis_index("x") + lax.axis_index("y")) * cw, cw, axis=1)

    outs = {}
    for n in WEIGHTS:
        g2 = grads[n].reshape(w2[n].shape)
        d, nm, nv = _adamw("adamw_" + n, w2[n], g2, m2[n], v2[n])
        outs[n] = tuple(unview(a, n) for a in (g2, d, nm, nv))
    return (loss, grad_x[None], *[outs[n][0] for n in WEIGHTS], *[outs[n][1] for n in WEIGHTS],
            *[outs[n][2] for n in WEIGHTS], *[outs[n][3] for n in WEIGHTS])
```

```python
import functools
import math

import jax
import jax.numpy as jnp
from jax import lax
from jax.experimental import pallas as pl
from jax.experimental.pallas import tpu as pltpu

F32 = jnp.float32
BF16 = jnp.bfloat16
EPS = 1e-6
NH_M, DK_M, DV_M = 4, 64, 128
NH_F, DH_F = 8, 64
CONV_W = 4
ADAM_LR, ADAM_B1, ADAM_B2, ADAM_EPS, ADAM_WD, ADAM_STEP = 0.001, 0.9, 0.999, 1e-08, 0.01, 10
VMEM_LIMIT = 56 * 1024 * 1024


def _cparams(sem):
    return pltpu.CompilerParams(dimension_semantics=sem, vmem_limit_bytes=VMEM_LIMIT)


def _sigmoid(x):
    return 1.0 / (1.0 + jnp.exp(-x))


def _dot(a, b, ca, cb):
    return lax.dot_general(a.astype(BF16), b.astype(BF16), (((ca,), (cb,)), ((), ())), preferred_element_type=F32)


def _rowwise(name, fn, tiled, full, outs, accs=(), tm=512):
    rows = tiled[0].shape[0]
    tm = min(tm, rows)
    assert rows % tm == 0
    n_t, n_f, n_o, n_a = len(tiled), len(full), len(outs), len(accs)

    def body(*refs):
        ins = [r[...] for r in refs[: n_t + n_f]]
        res = fn(*ins)
        if not isinstance(res, (tuple, list)):
            res = (res,)
        orefs = refs[n_t + n_f:]
        for r, v in zip(orefs[:n_o], res[:n_o]):
            r[...] = v.astype(r.dtype)
        if n_a:
            @pl.when(pl.program_id(0) == 0)
            def _():
                for r in orefs[n_o:]:
                    r[...] = jnp.zeros_like(r)
            for r, v in zip(orefs[n_o:], res[n_o:]):
                r[...] += v.astype(r.dtype)

    in_specs = [pl.BlockSpec((tm, a.shape[1]), lambda i: (i, 0)) for a in tiled]
    in_specs += [pl.BlockSpec(a.shape, lambda i: (0, 0)) for a in full]
    out_specs = [pl.BlockSpec((tm, c), lambda i: (i, 0)) for c, _ in outs]
    out_specs += [pl.BlockSpec(s, lambda i: (0, 0)) for s, _ in accs]
    out_shape = [jax.ShapeDtypeStruct((rows, c), d) for c, d in outs]
    out_shape += [jax.ShapeDtypeStruct(s, d) for s, d in accs]
    res = pl.pallas_call(
        body, name=name, grid=(rows // tm,), in_specs=in_specs, out_specs=out_specs, out_shape=out_shape,
        compiler_params=_cparams(("arbitrary",) if n_a else ("parallel",)),
    )(*tiled, *full)
    return res


def _colsum(v):
    return jnp.sum(v, axis=0, keepdims=True)


def _rms_fwd_val(x, g):
    r = lax.rsqrt(jnp.mean(x * x, axis=-1, keepdims=True) + EPS)
    return x * r * g


def _rms_bwd_val(dy, x, g):
    r = lax.rsqrt(jnp.mean(x * x, axis=-1, keepdims=True) + EPS)
    xh = x * r
    dxh = dy * g
    dx = r * (dxh - xh * jnp.mean(dxh * xh, axis=-1, keepdims=True))
    return dx, _colsum(dy * xh)


def _mm(name, pairs, out_shape, out_block, out_map, grid, kaxis, ta=False, tb=False, scale=None, res=None,
        out_dtype=F32, plan=None, twin=False):
    n_o = 2 if twin else 1
    nk = grid[kaxis]
    npairs = len(pairs)
    ca, cb = (0 if ta else 1), (1 if tb else 0)
    acc_shape = tuple(d for d in out_block if d is not None)
    n_in = 2 * npairs + (1 if res is not None else 0)
    host = _Hosted(plan, n_in, n_o)

    def body(*refs):
        ins, o_refs, (acc_ref,), prefs = host.split(refs)
        o_ref = o_refs[0]
        in_refs = ins[: 2 * npairs]
        res_ref = ins[2 * npairs] if res is not None else None
        k = pl.program_id(kaxis)
        ids = [pl.program_id(a) for a in range(len(grid))]
        first, last = ids[0] == 0, ids[0] == grid[0] - 1
        for a in range(1, len(grid)):
            first, last = first & (ids[a] == 0), last & (ids[a] == grid[a] - 1)
        host.run(0, first, prefs)
        host.run(1, first, prefs)

        @pl.when(k == 0)
        def _():
            acc_ref[...] = jnp.zeros_like(acc_ref)

        part = None
        for p in range(npairs):
            d = _dot(in_refs[2 * p][...], in_refs[2 * p + 1][...], ca, cb)
            part = d if part is None else part + d
        acc_ref[...] += part

        @pl.when(k == nk - 1)
        def _():
            v = acc_ref[...]
            if scale is not None:
                v = v * scale
            if res_ref is not None:
                v = v + res_ref[...].astype(F32)
            o_ref[...] = v.astype(o_ref.dtype)
            if twin:
                o_refs[1][...] = v.astype(BF16)

        host.run(2, last, prefs)

    in_specs, args = [], []
    for a, ab, am, b, bb, bm in pairs:
        in_specs += [pl.BlockSpec(ab, am), pl.BlockSpec(bb, bm)]
        args += [a, b]
    if res is not None:
        in_specs.append(pl.BlockSpec(out_block, out_map))
        args.append(res)
    sem = tuple("arbitrary" if (i == kaxis or plan is not None) else "parallel" for i in range(len(grid)))
    hc = host.call_args()
    out = pl.pallas_call(
        body, name=name, grid=grid, in_specs=in_specs + hc["in_specs"],
        out_specs=[pl.BlockSpec(out_block, out_map)] * n_o + hc["out_specs"],
        out_shape=[jax.ShapeDtypeStruct(out_shape, out_dtype)] + [jax.ShapeDtypeStruct(out_shape, BF16)] * (n_o - 1)
        + hc["out_shape"],
        scratch_shapes=[pltpu.VMEM(acc_shape, F32)] + hc["scratch"], input_output_aliases=hc["aliases"],
        compiler_params=_cparams(sem),
    )(*args, *hc["args"])
    res_out = tuple(out[:2]) if twin else out[0]
    return res_out if plan is None else (res_out, out[n_o:])


def _pick(n, pref):
    for t in pref:
        if n % t == 0:
            return t
    return n


def _mm_nn(name, a, b, tm=512, tn=512, tk=512, **kw):
    (m, k), n = a.shape, b.shape[1]
    tm, tn, tk = _pick(m, (tm, 256, 128)), _pick(n, (tn, 256, 128)), _pick(k, (tk, 256, 128))
    return _mm(name, [(a, (tm, tk), lambda i, j, kk: (i, kk), b, (tk, tn), lambda i, j, kk: (kk, j))],
               (m, n), (tm, tn), lambda i, j, kk: (i, j), (m // tm, n // tn, k // tk), 2, **kw)


def _mm_nt(name, a, b, tm=512, tn=512, tk=512, **kw):
    (m, k), n = a.shape, b.shape[0]
    tm, tn, tk = _pick(m, (tm, 256, 128)), _pick(n, (tn, 256, 128)), _pick(k, (tk, 256, 128))
    return _mm(name, [(a, (tm, tk), lambda i, j, kk: (i, kk), b, (tn, tk), lambda i, j, kk: (j, kk))],
               (m, n), (tm, tn), lambda i, j, kk: (i, j), (m // tm, n // tn, k // tk), 2, tb=True, **kw)


def _mm_tn(name, a, b, tm=512, tn=512, tk=2048, **kw):
    (k, m), n = a.shape, b.shape[1]
    tm, tn, tk = _pick(m, (tm, 256, 128)), _pick(n, (tn, 256, 128)), _pick(k, (tk, 1024, 512, 256, 128))
    return _mm(name, [(a, (tk, tm), lambda i, j, kk: (kk, i), b, (tk, tn), lambda i, j, kk: (kk, j))],
               (m, n), (tm, tn), lambda i, j, kk: (i, j), (m // tm, n // tn, k // tk), 2, ta=True, **kw)


def _norm_mm(name, h, gamma, w, w_transposed, out_dtype):
    t, d = h.shape
    n = w.shape[0] if w_transposed else w.shape[1]
    tm, tn = _pick(t, (512, 256)), _pick(n, (1024, 512, 256, 128))

    def body(h_ref, gam_ref, w_ref, xn_ref, o_ref, xn_scr):
        @pl.when(pl.program_id(1) == 0)
        def _():
            xn = _rms_fwd_val(h_ref[...], gam_ref[...]).astype(BF16)
            xn_scr[...] = xn
            xn_ref[...] = xn

        o_ref[...] = _dot(xn_scr[...], w_ref[...], 1, 1 if w_transposed else 0).astype(o_ref.dtype)

    wspec = pl.BlockSpec((tn, d), lambda i, j: (j, 0)) if w_transposed else pl.BlockSpec((d, tn), lambda i, j: (0, j))
    return pl.pallas_call(
        body, name=name, grid=(t // tm, n // tn),
        in_specs=[pl.BlockSpec((tm, d), lambda i, j: (i, 0)), pl.BlockSpec((1, d), lambda i, j: (0, 0)), wspec],
        out_specs=[pl.BlockSpec((tm, d), lambda i, j: (i, 0)), pl.BlockSpec((tm, tn), lambda i, j: (i, j))],
        out_shape=[jax.ShapeDtypeStruct((t, d), BF16), jax.ShapeDtypeStruct((t, n), out_dtype)],
        scratch_shapes=[pltpu.VMEM((tm, d), BF16)], compiler_params=_cparams(("parallel", "arbitrary")),
    )(h, gamma, w)


CHAIN_ROWS = 256


def _row_chains(tm):
    n = max(tm // CHAIN_ROWS, 1)
    return [slice(r * (tm // n), (r + 1) * (tm // n)) for r in range(n)]


def _ffn_fwd(pfx, h, gamma, wg, wu, wd, plan=None):
    t, d = h.shape
    nb, f, _ = wg.shape
    tm = _pick(t, (1024, 512, 256))
    nt = t // tm
    host = _Hosted(plan, 5, 4)

    def body(*refs):
        (h_ref, gam_ref, wg_ref, wu_ref, wd_ref), (ho_ref, xn_ref, g_ref, u_ref), (xn_scr, acc_ref), prefs = host.split(refs)
        i, j = pl.program_id(0), pl.program_id(1)
        host.run(0, (i == 0) & (j == 0), prefs)
        host.run(1, (i == nt // 2) & (j == 0), prefs)

        @pl.when(j == 0)
        def _():
            xn = _rms_fwd_val(h_ref[...], gam_ref[...]).astype(BF16)
            xn_scr[...] = xn
            xn_ref[...] = xn
            acc_ref[...] = jnp.zeros_like(acc_ref)

        for rows in _row_chains(tm):
            x = xn_scr[rows, :]
            g = _dot(x, wg_ref[...], 1, 1)
            u = _dot(x, wu_ref[...], 1, 1)
            g_ref[rows, :] = g.astype(BF16)
            u_ref[rows, :] = u.astype(BF16)
            acc_ref[rows, :] += _dot(g * _sigmoid(g) * u, wd_ref[...], 1, 0)

        @pl.when(j == nb - 1)
        def _():
            ho_ref[...] = h_ref[...] + 0.5 * acc_ref[...]

        host.run(2, (i == nt - 1) & (j == nb - 1), prefs)

    row = pl.BlockSpec((tm, d), lambda i, j: (i, 0))
    blk = pl.BlockSpec((None, tm, f), lambda i, j: (j, i, 0))
    wspec = pl.BlockSpec((None, f, d), lambda i, j: (j, 0, 0))
    hc = host.call_args()
    res = pl.pallas_call(
        body, name=pfx + "_fwd", grid=(nt, nb),
        in_specs=[row, pl.BlockSpec((1, d), lambda i, j: (0, 0)), wspec, wspec, wspec] + hc["in_specs"],
        out_specs=[row, row, blk, blk] + hc["out_specs"],
        out_shape=[jax.ShapeDtypeStruct((t, d), F32), jax.ShapeDtypeStruct((t, d), BF16),
                   jax.ShapeDtypeStruct((nb, t, f), BF16), jax.ShapeDtypeStruct((nb, t, f), BF16)] + hc["out_shape"],
        scratch_shapes=[pltpu.VMEM((tm, d), BF16), pltpu.VMEM((tm, d), F32)] + hc["scratch"],
        input_output_aliases=hc["aliases"], compiler_params=_cparams(("arbitrary", "arbitrary")),
    )(h, gamma, wg, wu, wd, *hc["args"])
    return res[:4], res[4:]


def _ffn_bwd(pfx, dh_out, h, gamma, xn, g_all, u_all, wg, wu, wd, plan=None):
    t, d = h.shape
    nb, f, _ = wg.shape
    tm = _pick(t, (512, 256))
    tk = _pick(t, (2048, 1024, 512, 256))

    nt = t // tm
    host = _Hosted(plan, 8, 5)

    def body(*refs):
        ((dy_ref, h_ref, gam_ref, wg_ref, wu_ref, wd_ref, g_ref, u_ref), (dh_ref, dgam_ref, dg_ref, du_ref, a_ref),
         (acc_ref,), prefs) = host.split(refs)
        i, j = pl.program_id(0), pl.program_id(1)
        host.run(0, (i == 0) & (j == 0), prefs)
        host.run(1, (i == nt // 2) & (j == 0), prefs)

        @pl.when((i == 0) & (j == 0))
        def _():
            dgam_ref[...] = jnp.zeros_like(dgam_ref)

        @pl.when(j == 0)
        def _():
            acc_ref[...] = jnp.zeros_like(acc_ref)

        for rows in _row_chains(tm):
            da = _dot(dy_ref[rows, :], wd_ref[...], 1, 1) * 0.5
            g = g_ref[rows, :].astype(F32)
            u = u_ref[rows, :].astype(F32)
            s = _sigmoid(g)
            sl = g * s
            du = (da * sl).astype(BF16)
            dg = (da * u * (s + sl * (1.0 - s))).astype(BF16)
            du_ref[rows, :] = du
            dg_ref[rows, :] = dg
            a_ref[rows, :] = (sl * u).astype(BF16)
            acc_ref[rows, :] += _dot(dg, wg_ref[...], 1, 0) + _dot(du, wu_ref[...], 1, 0)

        @pl.when(j == nb - 1)
        def _():
            dx, dgam = _rms_bwd_val(acc_ref[...], h_ref[...], gam_ref[...])
            dh_ref[...] = dy_ref[...] + dx
            dgam_ref[...] += dgam

        host.run(2, (i == nt - 1) & (j == nb - 1), prefs)

    row = pl.BlockSpec((tm, d), lambda i, j: (i, 0))
    vec = pl.BlockSpec((1, d), lambda i, j: (0, 0))
    blk = pl.BlockSpec((None, tm, f), lambda i, j: (j, i, 0))
    wspec = pl.BlockSpec((None, f, d), lambda i, j: (j, 0, 0))
    hc = host.call_args()
    res = pl.pallas_call(
        body, name=pfx + "_bwd", grid=(nt, nb),
        in_specs=[row, row, vec, wspec, wspec, wspec, blk, blk] + hc["in_specs"],
        out_specs=[row, vec, blk, blk, blk] + hc["out_specs"],
        out_shape=[jax.ShapeDtypeStruct((t, d), F32), jax.ShapeDtypeStruct((1, d), F32)]
        + [jax.ShapeDtypeStruct((nb, t, f), BF16)] * 3 + hc["out_shape"],
        scratch_shapes=[pltpu.VMEM((tm, d), F32)] + hc["scratch"], input_output_aliases=hc["aliases"],
        compiler_params=_cparams(("arbitrary", "arbitrary")),
    )(dh_out, h, gamma, wg, wu, wd, g_all, u_all, *hc["args"])
    dh, dgamma, dg_all, du_all, a_all = res[:5]

    xmap, bmap, omap = (lambda b, k: (k, 0)), (lambda b, k: (b, k, 0)), (lambda b, k: (b, 0, 0))
    dwg, tg = _mm(pfx + "_dwg", [(dg_all, (None, tk, f), bmap, xn, (tk, d), xmap)], (nb, f, d), (None, f, d), omap,
                  (nb, t // tk), 1, ta=True, twin=True)
    dwu, tu = _mm(pfx + "_dwu", [(du_all, (None, tk, f), bmap, xn, (tk, d), xmap)], (nb, f, d), (None, f, d), omap,
                  (nb, t // tk), 1, ta=True, twin=True)
    dwd, td = _mm(pfx + "_dwd", [(a_all, (None, tk, f), bmap, dh_out, (tk, d), xmap)], (nb, f, d), (None, f, d), omap,
                  (nb, t // tk), 1, ta=True, scale=0.5, twin=True)
    return (dh, dgamma, dwg, dwu, dwd), res[5:], (tg, tu, td)


def _ffn_bwd_late_dx(pfx, dh_out, h, gamma, xn, g_all, u_all, wg, wu, wd, plan_gu, plans_dw, make_plan_dx):
    t, d = h.shape
    nb, f, _ = wg.shape
    tm = _pick(t, (512, 256))
    tk = _pick(t, (2048, 1024, 512, 256))
    nt = t // tm
    host_a = _Hosted(plan_gu, 4, 3)

    def body_a(*refs):
        (dy_ref, wd_ref, g_ref, u_ref), (dg_ref, du_ref, a_ref), _, prefs = host_a.split(refs)
        i, j = pl.program_id(0), pl.program_id(1)
        host_a.run(0, (i == 0) & (j == 0), prefs)
        host_a.run(1, (i == 0) & (j == 0), prefs)
        for rows in _row_chains(tm):
            da = _dot(dy_ref[rows, :], wd_ref[...], 1, 1) * 0.5
            g = g_ref[rows, :].astype(F32)
            u = u_ref[rows, :].astype(F32)
            s = _sigmoid(g)
            sl = g * s
            du_ref[rows, :] = (da * sl).astype(BF16)
            dg_ref[rows, :] = (da * u * (s + sl * (1.0 - s))).astype(BF16)
            a_ref[rows, :] = (sl * u).astype(BF16)
        host_a.run(2, (i == nt - 1) & (j == nb - 1), prefs)

    row = pl.BlockSpec((tm, d), lambda i, j: (i, 0))
    vec = pl.BlockSpec((1, d), lambda i, j: (0, 0))
    blk = pl.BlockSpec((None, tm, f), lambda i, j: (j, i, 0))
    wspec = pl.BlockSpec((None, f, d), lambda i, j: (j, 0, 0))
    hc = host_a.call_args()
    res_a = pl.pallas_call(
        body_a, name=pfx + "_bwd_gu", grid=(nt, nb), in_specs=[row, wspec, blk, blk] + hc["in_specs"],
        out_specs=[blk] * 3 + hc["out_specs"], out_shape=[jax.ShapeDtypeStruct((nb, t, f), BF16)] * 3 + hc["out_shape"],
        scratch_shapes=hc["scratch"], input_output_aliases=hc["aliases"], compiler_params=_cparams(("arbitrary", "arbitrary")),
    )(dh_out, wd, g_all, u_all, *hc["args"])
    dg_all, du_all, a_all = res_a[:3]

    xmap, bmap, omap = (lambda b, k: (k, 0)), (lambda b, k: (b, k, 0)), (lambda b, k: (b, 0, 0))
    (dwd, td), out_d = _mm(pfx + "_dwd", [(a_all, (None, tk, f), bmap, dh_out, (tk, d), xmap)], (nb, f, d), (None, f, d),
                           omap, (nb, t // tk), 1, ta=True, scale=0.5, plan=plans_dw[0], twin=True)
    (dwg, tg), out_g = _mm(pfx + "_dwg", [(dg_all, (None, tk, f), bmap, xn, (tk, d), xmap)], (nb, f, d), (None, f, d),
                           omap, (nb, t // tk), 1, ta=True, plan=plans_dw[1], twin=True)
    (dwu, tu), out_u = _mm(pfx + "_dwu", [(du_all, (None, tk, f), bmap, xn, (tk, d), xmap)], (nb, f, d), (None, f, d),
                           omap, (nb, t // tk), 1, ta=True, plan=plans_dw[2], twin=True)

    plan_dx = make_plan_dx((dwg, dwu, dwd), (tg, tu, td))
    host_b = _Hosted(plan_dx, 7, 2)

    def body_b(*refs):
        (dy_ref, h_ref, gam_ref, wg_ref, wu_ref, dg_ref, du_ref), (dh_ref, dgam_ref), (acc_ref,), prefs = host_b.split(refs)
        i, j = pl.program_id(0), pl.program_id(1)
        host_b.run(0, (i == 0) & (j == 0), prefs)
        host_b.run(1, (i == 0) & (j == 0), prefs)

        @pl.when((i == 0) & (j == 0))
        def _():
            dgam_ref[...] = jnp.zeros_like(dgam_ref)

        @pl.when(j == 0)
        def _():
            acc_ref[...] = jnp.zeros_like(acc_ref)

        acc_ref[...] += _dot(dg_ref[...], wg_ref[...], 1, 0) + _dot(du_ref[...], wu_ref[...], 1, 0)

        @pl.when(j == nb - 1)
        def _():
            dx, dgam = _rms_bwd_val(acc_ref[...], h_ref[...], gam_ref[...])
            dh_ref[...] = dy_ref[...] + dx
            dgam_ref[...] += dgam

        host_b.run(2, (i == nt - 1) & (j == nb - 1), prefs)

    hc = host_b.call_args()
    res_b = pl.pallas_call(
        body_b, name=pfx + "_bwd_dx", grid=(nt, nb), in_specs=[row, row, vec, wspec, wspec, blk, blk] + hc["in_specs"],
        out_specs=[row, vec] + hc["out_specs"],
        out_shape=[jax.ShapeDtypeStruct((t, d), F32), jax.ShapeDtypeStruct((1, d), F32)] + hc["out_shape"],
        scratch_shapes=[pltpu.VMEM((tm, d), F32)] + hc["scratch"], input_output_aliases=hc["aliases"],
        compiler_params=_cparams(("arbitrary", "arbitrary")),
    )(dh_out, h, gamma, wg, wu, dg_all, du_all, *hc["args"])
    return (res_b[0], res_b[1], dwg, dwu, dwd), (res_a[3:], out_d, out_g, out_u, res_b[2:])


HALO = 16


def _silu_grad(y):
    s = _sigmoid(y)
    return s * (1.0 + y * (1.0 - s))


def _with_halo(ref, i, n_tiles, tm, before, after):
    t = ref.shape[0]
    r0 = pl.multiple_of(i * tm, tm)
    parts = [ref[pl.ds(r0, tm), :].astype(F32)]
    if before:
        prev = ref[pl.ds(pl.multiple_of(jnp.maximum(r0 - HALO, 0), HALO), HALO), :].astype(F32)
        parts.insert(0, jnp.where(i > 0, prev, 0.0))
    if after:
        nxt = ref[pl.ds(pl.multiple_of(jnp.minimum(r0 + tm, t - HALO), HALO), HALO), :].astype(F32)
        parts.append(jnp.where(i < n_tiles - 1, nxt, 0.0))
    return jnp.concatenate(parts, axis=0)


def _conv_fwd(zbig, w):
    t, c = zbig.shape[0], w.shape[1]
    tm = _pick(t, (512, 256))
    nt = t // tm

    def body(x_ref, w_ref, o_ref):
        xe = _with_halo(x_ref, pl.program_id(0), nt, tm, True, False)
        wv = w_ref[...]
        y = xe * wv[3:4, :]
        for i in range(CONV_W - 1):
            y = y + pltpu.roll(xe, CONV_W - 1 - i, 0) * wv[i:i + 1, :]
        y = y[HALO:, :]
        o_ref[...] = (y * _sigmoid(y)).astype(o_ref.dtype)

    return pl.pallas_call(
        body, name="conv_fwd", grid=(nt,),
        in_specs=[pl.BlockSpec((t, c), lambda i: (0, 0)), pl.BlockSpec(w.shape, lambda i: (0, 0))],
        out_specs=pl.BlockSpec((tm, c), lambda i: (i, 0)), out_shape=jax.ShapeDtypeStruct((t, c), BF16),
        compiler_params=_cparams(("parallel",)),
    )(zbig, w)


def _conv_bwd(zbig, dact, w):
    t, c = dact.shape
    tm = _pick(t, (512, 256))
    nt = t // tm
    n = tm + HALO

    def body(x_ref, d_ref, w_ref, dx_ref, dw_ref):
        xe = _with_halo(x_ref, pl.program_id(0), nt, tm, True, True)
        de = _with_halo(d_ref, pl.program_id(0), nt, tm, False, True)
        wv = w_ref[...]
        sh = [pltpu.roll(xe, CONV_W - 1 - i, 0)[HALO:, :] if i < CONV_W - 1 else xe[HALO:, :] for i in range(CONV_W)]
        y = sh[0] * wv[0:1, :]
        for i in range(1, CONV_W):
            y = y + sh[i] * wv[i:i + 1, :]
        dy = de * _silu_grad(y)
        dx = dy * wv[3:4, :]
        for i in range(CONV_W - 1):
            dx = dx + pltpu.roll(dy, n - (CONV_W - 1 - i), 0) * wv[i:i + 1, :]
        dx_ref[...] = dx[:tm, :].astype(dx_ref.dtype)
        dyc = dy[:tm, :]
        dwp = jnp.concatenate([_colsum(dyc * sh[i][:tm, :]) for i in range(CONV_W)], axis=0)

        @pl.when(pl.program_id(0) == 0)
        def _():
            dw_ref[...] = jnp.zeros_like(dw_ref)
        dw_ref[...] += dwp

    return pl.pallas_call(
        body, name="conv_bwd", grid=(nt,),
        in_specs=[pl.BlockSpec((t, c), lambda i: (0, 0)), pl.BlockSpec((t, c), lambda i: (0, 0)),
                  pl.BlockSpec(w.shape, lambda i: (0, 0))],
        out_specs=[pl.BlockSpec((tm, c), lambda i: (i, 0)), pl.BlockSpec(w.shape, lambda i: (0, 0))],
        out_shape=[jax.ShapeDtypeStruct((t, c), BF16), jax.ShapeDtypeStruct(w.shape, F32)],
        compiler_params=_cparams(("arbitrary",)),
    )(zbig, dact, w)


LM = 256
HI = lax.Precision.HIGHEST


def _logsig(x):
    return jnp.minimum(x, 0.0) - jnp.log(1.0 + jnp.exp(-jnp.abs(x)))


def _tri(n, lower):
    r = lax.broadcasted_iota(jnp.int32, (n, n), 0)
    c = lax.broadcasted_iota(jnp.int32, (n, n), 1)
    return (r >= c) if lower else (r <= c)


def _f32dot(a, b):
    return lax.dot_general(a, b, (((1,), (0,)), ((), ())), precision=HI, preferred_element_type=F32)


def _tri_dot(a, b, a_is_tri):
    tri = (a if a_is_tri else b).astype(BF16)
    parts = _split3(b if a_is_tri else a)
    outs = [_dot(tri, p, 1, 0) if a_is_tri else _dot(p, tri, 1, 0) for p in parts]
    return (outs[0] + outs[1]) + outs[2]


def _mlstm_decays(zs_ref, zsr_ref, bc_ref, br_ref):
    l = LM
    lf_c = _logsig(zs_ref[:, 0:2 * NH_M] + bc_ref[...])
    lf_r = _logsig(zsr_ref[...] + br_ref[...])
    return _tri_dot(_tri(l, True), lf_c, True), _tri_dot(lf_r, _tri(l, False), False)


def _mlstm_chunk(h, q_ref, k_ref, v_ref, zs_ref, zsr_ref, bc_ref, br_ref, c_prev, m_prev, decays):
    l = LM
    q = q_ref[:, h * DK_M:(h + 1) * DK_M].astype(F32) * (DK_M ** -0.5)
    k = k_ref[:, h * DK_M:(h + 1) * DK_M]
    v = v_ref[:, h * DV_M:(h + 1) * DV_M]
    lane = lax.broadcasted_iota(jnp.int32, (l, DV_M), 1)
    v1 = jnp.concatenate([v, (lane == 0).astype(v.dtype)], axis=1)
    zs, zsr = zs_ref[...], zsr_ref[...]
    li_c = zs[:, h:h + 1] + bc_ref[:, h:h + 1]
    fp_c = zs[:, NH_M + h:NH_M + h + 1] + bc_ref[:, NH_M + h:NH_M + h + 1]
    li_r = zsr[h:h + 1, :] + br_ref[h:h + 1, :]
    fp_r = zsr[NH_M + h:NH_M + h + 1, :] + br_ref[NH_M + h:NH_M + h + 1, :]
    low = _tri(l, True)
    b_c = decays[0][:, NH_M + h:NH_M + h + 1]
    b_r = decays[1][NH_M + h:NH_M + h + 1, :]
    g = b_r[:, l - 1:l]
    dmat = jnp.where(low, b_c - b_r + li_r, -jnp.inf)
    inter = b_c + m_prev
    m_t = jnp.maximum(inter, jnp.max(dmat, axis=1, keepdims=True))
    w_inter = jnp.exp(inter - m_t)
    amat = jnp.exp(dmat - m_t)
    s = _dot(q, k, 1, 1)
    p = amat * s
    qc = _dot(q, c_prev, 1, 0)
    qc_w = w_inter * qc
    num1 = qc_w + _dot(p, v1, 1, 0)
    den = num1[:, DV_M:DV_M + 1]
    mx = jnp.maximum(jnp.abs(den), jnp.exp(-m_t))
    hh = num1[:, :DV_M] / mx
    a_c = g - b_c + li_c
    return dict(q=q, k=k, v1=v1, fp_c=fp_c, fp_r=fp_r, b_c=b_c, g=g, m_t=m_t, w_inter=w_inter, amat=amat, s=s, p=p,
                qc_w=qc_w, den=den, mx=mx, hh=hh, a_c=a_c)


def _mlstm_fwd(qk, zbig, zs, zsr, bc, br, gm):
    t = zs.shape[0]
    l = LM
    nc = t // l
    dm = NH_M * DV_M

    def body(q_ref, k_ref, v_ref, o_ref, zs_ref, zsr_ref, bc_ref, br_ref, gm_ref, y_ref, cst_ref, mst_ref, c_scr, m_scr):
        @pl.when(pl.program_id(0) == 0)
        def _():
            c_scr[...] = jnp.zeros_like(c_scr)
            m_scr[...] = jnp.zeros_like(m_scr)

        cst_ref[...] = c_scr[...]
        mst_ref[...] = m_scr[...]
        ys = []
        decays = _mlstm_decays(zs_ref, zsr_ref, bc_ref, br_ref)
        for h in range(NH_M):
            c_prev = c_scr[h]
            m_prev = m_scr[h:h + 1, 0:1]
            r = _mlstm_chunk(h, q_ref, k_ref, v_ref, zs_ref, zsr_ref, bc_ref, br_ref, c_prev, m_prev, decays)
            hh = r["hh"]
            gh = gm_ref[:, h * DV_M:(h + 1) * DV_M]
            hn = hh * lax.rsqrt(jnp.mean(hh * hh, axis=-1, keepdims=True) + EPS) * gh
            og = o_ref[:, h * DV_M:(h + 1) * DV_M].astype(F32)
            ys.append(hn * _sigmoid(og))
            m_new = jnp.maximum(r["g"] + m_prev, jnp.max(r["a_c"], axis=0, keepdims=True))
            decay = jnp.exp(r["g"] + m_prev - m_new)
            wk = r["k"].astype(F32) * jnp.exp(r["a_c"] - m_new)
            c_scr[h] = decay * c_prev + _dot(wk, r["v1"], 0, 0)
            m_scr[h:h + 1, :] = jnp.broadcast_to(m_new, (1, 128))
        y_ref[...] = jnp.concatenate(ys, axis=1).astype(y_ref.dtype)

    return pl.pallas_call(
        body, name="mlstm_fwd", grid=(nc,),
        in_specs=[pl.BlockSpec((l, NH_M * DK_M), lambda i: (i, 0)), pl.BlockSpec((l, NH_M * DK_M), lambda i: (i, 1)),
                  pl.BlockSpec((l, dm), lambda i: (i, 1)), pl.BlockSpec((l, dm), lambda i: (i, 2)),
                  pl.BlockSpec((l, 128), lambda i: (i, 0)), pl.BlockSpec((8, l), lambda i: (0, i)),
                  pl.BlockSpec((1, 8), lambda i: (0, 0)), pl.BlockSpec((8, 1), lambda i: (0, 0)),
                  pl.BlockSpec((1, dm), lambda i: (0, 0))],
        out_specs=[pl.BlockSpec((l, dm), lambda i: (i, 0)), pl.BlockSpec((None, NH_M, DK_M, 2 * DV_M), lambda i: (i, 0, 0, 0)),
                   pl.BlockSpec((None, 8, 128), lambda i: (i, 0, 0))],
        out_shape=[jax.ShapeDtypeStruct((t, dm), BF16), jax.ShapeDtypeStruct((nc, NH_M, DK_M, 2 * DV_M), F32),
                   jax.ShapeDtypeStruct((nc, 8, 128), F32)],
        scratch_shapes=[pltpu.VMEM((NH_M, DK_M, 2 * DV_M), F32), pltpu.VMEM((8, 128), F32)],
        compiler_params=_cparams(("arbitrary",)),
    )(qk, qk, zbig, zbig, zs, zsr, bc, br, gm)


def _mlstm_bwd(qk, zbig, zs, zsr, bc, br, gm, cst, mst, dycat):
    t = zs.shape[0]
    l = LM
    nc = t // l
    dm = NH_M * DV_M

    def body(q_ref, k_ref, v_ref, o_ref, zs_ref, zsr_ref, bc_ref, br_ref, gm_ref, cst_ref, mst_ref, cnx_ref, mnx_ref,
             dy_ref, dqk_ref, dv_ref, do_ref, dzs_ref, dzr_ref, dgm_ref, dc_scr):
        @pl.when(pl.program_id(0) == 0)
        def _():
            dc_scr[...] = jnp.zeros_like(dc_scr)
            dgm_ref[...] = jnp.zeros_like(dgm_ref)

        lane = lax.broadcasted_iota(jnp.int32, (l, 128), 1)
        upper, lower = _tri(l, False), _tri(l, True)
        db_all, sig_c, carries = jnp.zeros((l, 128), F32), jnp.zeros((l, 128), F32), jnp.zeros((1, 128), F32)
        decays = _mlstm_decays(zs_ref, zsr_ref, bc_ref, br_ref)
        dzr_rows = [None] * 8
        dvs, dos, dgs, dqs, dks = [], [], [], [], []
        dzs = jnp.zeros((l, 128), F32)
        for h in range(NH_M):
            c_prev = cst_ref[h]
            m_prev = mst_ref[h:h + 1, 0:1]
            r = _mlstm_chunk(h, q_ref, k_ref, v_ref, zs_ref, zsr_ref, bc_ref, br_ref, c_prev, m_prev, decays)
            hh, mx, den, m_t, v1, amat = r["hh"], r["mx"], r["den"], r["m_t"], r["v1"], r["amat"]
            gh = gm_ref[:, h * DV_M:(h + 1) * DV_M]
            rs = lax.rsqrt(jnp.mean(hh * hh, axis=-1, keepdims=True) + EPS)
            xh = hh * rs
            sg = _sigmoid(o_ref[:, h * DV_M:(h + 1) * DV_M].astype(F32))
            dyh = dy_ref[:, h * DV_M:(h + 1) * DV_M]
            dos.append(dyh * xh * gh * sg * (1.0 - sg))
            dhn = dyh * sg
            dgs.append(_colsum(dhn * xh))
            dxh = dhn * gh
            dh = rs * (dxh - xh * jnp.mean(dxh * xh, axis=-1, keepdims=True))
            g1 = dh / mx
            hd = jnp.sum(hh * dh, axis=-1, keepdims=True)
            dden = jnp.where(jnp.abs(den) > jnp.exp(-m_t), -hd / mx * jnp.sign(den), 0.0)
            g256 = jnp.concatenate([g1, jnp.where(lane == 0, dden, 0.0)], axis=1)
            dc_h = dc_scr[h]
            ea = jnp.exp(r["a_c"])
            dp = _dot(g256, v1, 1, 1)
            ds = dp * amat
            dqs.append((r["w_inter"] * _dot(g256, c_prev, 1, 1) + _dot(ds, r["k"], 1, 0)) * (DK_M ** -0.5))
            dks.append(_dot(ds, r["q"], 0, 0) + ea * _dot(v1, dc_h, 1, 1))
            dv_st = ea * _dot(r["k"], dc_h, 1, 0)
            dv1 = _dot(r["p"], g256, 0, 0) + dv_st
            dvs.append(dv1[:, :DV_M])
            wmat = dp * r["p"]
            c_in = _colsum(wmat)
            c_st = jnp.sum(v1.astype(F32) * dv_st, axis=-1, keepdims=True)
            r_t = jnp.sum(wmat, axis=1, keepdims=True) + jnp.sum(g256 * r["qc_w"], axis=-1, keepdims=True)
            db = r_t - c_st
            carry = jnp.exp(mnx_ref[h:h + 1, 0:1]) * jnp.sum(
                jnp.sum(dc_h * cnx_ref[h], axis=1, keepdims=True), axis=0, keepdims=True)
            db_all = db_all + jnp.where(lane == NH_M + h, db, 0.0)
            sig_c = sig_c + jnp.where(lane == NH_M + h, _sigmoid(-r["fp_c"]), 0.0)
            carries = carries + jnp.where(lane[0:1, :] == NH_M + h, carry, 0.0)
            dzs = dzs + jnp.where(lane == h, c_st, 0.0)
            dzr_rows[h] = c_in
            dzr_rows[NH_M + h] = _sigmoid(-r["fp_r"])
            wq = r["q"] * jnp.exp(r["b_c"] - m_t)
            dc_scr[h] = jnp.exp(r["g"]) * dc_h + _dot(wq, g256, 0, 0)
        dzs = dzs + (_tri_dot(upper, db_all, True) + carries) * sig_c
        c_in4 = jnp.concatenate(dzr_rows[:NH_M], axis=0)
        dlf_r4 = -_tri_dot(c_in4, lower, False)
        dzr_rows = dzr_rows[:NH_M] + [dlf_r4[h:h + 1, :] * dzr_rows[NH_M + h] for h in range(NH_M)]
        dqk_ref[...] = jnp.concatenate(dqs + dks, axis=1)
        dv_ref[...] = jnp.concatenate(dvs, axis=1).astype(dv_ref.dtype)
        do_ref[...] = jnp.concatenate(dos, axis=1).astype(do_ref.dtype)
        dzs_ref[...] = dzs
        dzr_ref[...] = jnp.concatenate(dzr_rows, axis=0)
        dgm_ref[...] += jnp.concatenate(dgs, axis=1)

    rev = lambda i: nc - 1 - i
    nxt = lambda i: jnp.minimum(nc - i, nc - 1)
    return pl.pallas_call(
        body, name="mlstm_bwd", grid=(nc,),
        in_specs=[pl.BlockSpec((l, NH_M * DK_M), lambda i: (rev(i), 0)), pl.BlockSpec((l, NH_M * DK_M), lambda i: (rev(i), 1)),
                  pl.BlockSpec((l, dm), lambda i: (rev(i), 1)), pl.BlockSpec((l, dm), lambda i: (rev(i), 2)),
                  pl.BlockSpec((l, 128), lambda i: (rev(i), 0)), pl.BlockSpec((8, l), lambda i: (0, rev(i))),
                  pl.BlockSpec((1, 8), lambda i: (0, 0)), pl.BlockSpec((8, 1), lambda i: (0, 0)),
                  pl.BlockSpec((1, dm), lambda i: (0, 0)),
                  pl.BlockSpec((None, NH_M, DK_M, 2 * DV_M), lambda i: (rev(i), 0, 0, 0)),
                  pl.BlockSpec((None, 8, 128), lambda i: (rev(i), 0, 0)),
                  pl.BlockSpec((None, NH_M, DK_M, 2 * DV_M), lambda i: (nxt(i), 0, 0, 0)),
                  pl.BlockSpec((None, 8, 128), lambda i: (nxt(i), 0, 0)),
                  pl.BlockSpec((l, dm), lambda i: (rev(i), 0))],
        out_specs=[pl.BlockSpec((l, dm), lambda i: (rev(i), 0)),
                   pl.BlockSpec((l, dm), lambda i: (rev(i), 0)), pl.BlockSpec((l, dm), lambda i: (rev(i), 0)),
                   pl.BlockSpec((l, 128), lambda i: (rev(i), 0)), pl.BlockSpec((8, l), lambda i: (0, rev(i))),
                   pl.BlockSpec((1, dm), lambda i: (0, 0))],
        out_shape=[jax.ShapeDtypeStruct((t, dm), F32),
                   jax.ShapeDtypeStruct((t, dm), BF16), jax.ShapeDtypeStruct((t, dm), BF16),
                   jax.ShapeDtypeStruct((t, 128), F32), jax.ShapeDtypeStruct((8, t), F32),
                   jax.ShapeDtypeStruct((1, dm), F32)],
        scratch_shapes=[pltpu.VMEM((NH_M, DK_M, 2 * DV_M), F32)],
        compiler_params=_cparams(("arbitrary",)),
    )(qk, qk, zbig, zbig, zs, zsr, bc, br, gm, cst, mst, cst, mst, dycat)


def _fox_cumsum(zsr, bf_r):
    t = zsr.shape[1]
    cw = _pick(t, (512, 256))

    def body(z_ref, b_ref, c_ref):
        up = _tri(cw, False).astype(F32)
        carry = jnp.zeros((NH_F, 1), F32)
        for j in range(t // cw):
            cs = _f32dot(_logsig(z_ref[:, j * cw:(j + 1) * cw] + b_ref[...]), up) + carry
            c_ref[:, j * cw:(j + 1) * cw] = cs
            carry = cs[:, cw - 1:cw]

    return pl.pallas_call(
        body, name="fox_cumsum", grid=(1,),
        in_specs=[pl.BlockSpec((NH_F, t), lambda i: (1, 0)), pl.BlockSpec((NH_F, 1), lambda i: (0, 0))],
        out_specs=pl.BlockSpec((NH_F, t), lambda i: (0, 0)), out_shape=jax.ShapeDtypeStruct((NH_F, t), F32),
        compiler_params=_cparams(("arbitrary",)),
    )(zsr, bf_r)


def _fox_gate_bwd(zsr, bf_r, dc):
    t = zsr.shape[1]
    cw = _pick(t, (512, 256))

    def body(z_ref, b_ref, dc_ref, o_ref):
        low = _tri(cw, True).astype(F32)
        carry = jnp.zeros((NH_F, 1), F32)
        for j in reversed(range(t // cw)):
            sl = slice(j * cw, (j + 1) * cw)
            dlf = _f32dot(dc_ref[:, sl], low) + carry
            o_ref[:, sl] = dlf * _sigmoid(-(z_ref[:, sl] + b_ref[...]))
            carry = dlf[:, 0:1]

    return pl.pallas_call(
        body, name="fox_gate_bwd", grid=(1,),
        in_specs=[pl.BlockSpec((NH_F, t), lambda i: (1, 0)), pl.BlockSpec((NH_F, 1), lambda i: (0, 0)),
                  pl.BlockSpec((NH_F, t), lambda i: (0, 0))],
        out_specs=pl.BlockSpec((NH_F, t), lambda i: (0, 0)), out_shape=jax.ShapeDtypeStruct((NH_F, t), F32),
        compiler_params=_cparams(("arbitrary",)),
    )(zsr, bf_r, dc)


def _causal_mask(n):
    return _tri(n, True)


def _fox_fwd(q, k, v, c_col, c_row, gf):
    nh, t, dh = q.shape
    tq = _pick(t, (512, 256))
    scale = dh ** -0.5

    def body(q_ref, k_ref, v_ref, cc_ref, cr_ref, g_ref, o_ref, lse_ref, y_ref):
        i = pl.program_id(1)
        qv = q_ref[...]
        cq = cc_ref[...]

        def blk(j, carry, masked):
            m, l, acc = carry
            k0 = pl.multiple_of(j * tq, tq)
            kb = k_ref[pl.ds(k0, tq), :]
            vb = v_ref[pl.ds(k0, tq), :]
            s = _dot(qv, kb, 1, 1) * scale + cq - cr_ref[:, pl.ds(k0, tq)]
            if masked:
                s = jnp.where(_causal_mask(tq), s, -jnp.inf)
            m_new = jnp.maximum(m, jnp.max(s, axis=1, keepdims=True))
            alpha = jnp.exp(m - m_new)
            p = jnp.exp(s - m_new)
            return m_new, alpha * l + jnp.sum(p, axis=1, keepdims=True), alpha * acc + _dot(p, vb, 1, 0)

        init = (jnp.full((tq, 1), -jnp.inf, F32), jnp.zeros((tq, 1), F32), jnp.zeros((tq, dh), F32))
        carry = lax.fori_loop(0, i, lambda j, c: blk(j, c, False), init)
        m, l, acc = blk(i, carry, True)
        o = acc / l
        o_ref[...] = o
        lse_ref[...] = m + jnp.log(l)
        y_ref[...] = (o * lax.rsqrt(jnp.mean(o * o, axis=-1, keepdims=True) + EPS) * g_ref[...]).astype(y_ref.dtype)

    full = lambda w: pl.BlockSpec((None, t, w), lambda h, i: (h, 0, 0))
    tile = lambda w: pl.BlockSpec((None, tq, w), lambda h, i: (h, i, 0))
    return pl.pallas_call(
        body, name="fox_fwd", grid=(nh, t // tq),
        in_specs=[tile(dh), full(dh), full(dh), tile(1), pl.BlockSpec((None, 1, t), lambda h, i: (h, 0, 0)),
                  pl.BlockSpec((None, 1, dh), lambda h, i: (h, 0, 0))],
        out_specs=[tile(dh), tile(1), tile(dh)],
        out_shape=[jax.ShapeDtypeStruct((nh, t, dh), F32), jax.ShapeDtypeStruct((nh, t, 1), F32),
                   jax.ShapeDtypeStruct((nh, t, dh), BF16)],
        compiler_params=_cparams(("parallel", "parallel")),
    )(q, k, v, c_col, c_row, gf)


def _fox_norm_bwd(dy, o, gf):
    nh, t, dh = o.shape
    tm = _pick(t, (512, 256))

    def body(dy_ref, o_ref, g_ref, do_ref, dl_ref, dg_ref):
        ov = o_ref[...]
        dx, dg = _rms_bwd_val(dy_ref[...], ov, g_ref[...])
        do_ref[...] = dx
        dl_ref[...] = jnp.sum(dx * ov, axis=-1, keepdims=True)

        @pl.when(pl.program_id(1) == 0)
        def _():
            dg_ref[...] = jnp.zeros_like(dg_ref)
        dg_ref[...] += dg

    tile = lambda w: pl.BlockSpec((None, tm, w), lambda h, i: (h, i, 0))
    gspec = pl.BlockSpec((None, 1, dh), lambda h, i: (h, 0, 0))
    return pl.pallas_call(
        body, name="fox_norm_bwd", grid=(nh, t // tm), in_specs=[tile(dh), tile(dh), gspec],
        out_specs=[tile(dh), tile(1), gspec],
        out_shape=[jax.ShapeDtypeStruct((nh, t, dh), F32), jax.ShapeDtypeStruct((nh, t, 1), F32),
                   jax.ShapeDtypeStruct((nh, 1, dh), F32)],
        compiler_params=_cparams(("parallel", "arbitrary")),
    )(dy, o, gf)


def _fox_bwd(q, k, v, c_col, c_row, do, lse, delta):
    nh, t, dh = q.shape
    tq = _pick(t, (512, 256))
    nq = t // tq
    scale = dh ** -0.5

    def body(q_ref, k_ref, v_ref, cc_ref, cr_ref, do_ref, lse_ref, dl_ref, dq_ref, dk_ref, dv_ref, dc_ref, dcq_ref):
        j = pl.program_id(1)

        @pl.when(j == 0)
        def _():
            dq_ref[...] = jnp.zeros_like(dq_ref)
            dcq_ref[...] = jnp.zeros_like(dcq_ref)

        kb, vb, crb = k_ref[...], v_ref[...], cr_ref[...]

        def blk(i, carry, masked):
            dk, dv, dc = carry
            rows = pl.ds(pl.multiple_of(i * tq, tq), tq)
            qb = q_ref[rows, :]
            dob = do_ref[rows, :].astype(BF16)
            s = _dot(qb, kb, 1, 1) * scale + cc_ref[rows, :] - crb
            if masked:
                s = jnp.where(_causal_mask(tq), s, -jnp.inf)
            p = jnp.exp(s - lse_ref[rows, :])
            dv = dv + _dot(p, dob, 0, 0)
            ds = p * (_dot(dob, vb, 1, 1) - dl_ref[rows, :])
            dc = dc + _colsum(ds)
            dk = dk + _dot(ds, qb, 0, 0) * scale
            dq_ref[rows, :] += _dot(ds, kb, 1, 0) * scale
            dcq_ref[rows, :] += jnp.sum(ds, axis=1, keepdims=True)
            return dk, dv, dc

        init = (jnp.zeros((tq, dh), F32), jnp.zeros((tq, dh), F32), jnp.zeros((1, tq), F32))
        carry = blk(j, init, True)
        dk, dv, dc = lax.fori_loop(j + 1, nq, lambda i, c: blk(i, c, False), carry)
        dk_ref[...] = dk
        dv_ref[...] = dv
        dc_ref[...] = -dc

    full = lambda w: pl.BlockSpec((None, t, w), lambda h, j: (h, 0, 0))
    tile = lambda w: pl.BlockSpec((None, tq, w), lambda h, j: (h, j, 0))
    crow = pl.BlockSpec((None, 1, tq), lambda h, j: (h, 0, j))
    return pl.pallas_call(
        body, name="fox_bwd", grid=(nh, nq),
        in_specs=[full(dh), tile(dh), tile(dh), full(1), crow, full(dh), full(1), full(1)],
        out_specs=[full(dh), tile(dh), tile(dh), crow, full(1)],
        out_shape=[jax.ShapeDtypeStruct((nh, t, dh), F32)] * 3 + [jax.ShapeDtypeStruct((nh, 1, t), F32),
                                                                jax.ShapeDtypeStruct((nh, t, 1), F32)],
        compiler_params=_cparams(("parallel", "arbitrary")),
    )(q, k, v, c_col, c_row, do, lse, delta)


AUG = 64


def _split3(c):
    hi = c.astype(BF16).astype(F32)
    r1 = c - hi
    mid = r1.astype(BF16).astype(F32)
    return hi, mid, r1 - mid


def _fox_prep(zbig, ct):
    t = zbig.shape[0]
    tm = _pick(t, (512, 256))

    def body(q_ref, k_ref, v_ref, c_ref, qo_ref, ko_ref, vo_ref):
        lane = lax.broadcasted_iota(jnp.int32, (tm, AUG), 1)
        qv, kv, vv, cv = q_ref[...], k_ref[...], v_ref[...], c_ref[...]
        one = (lane == 0).astype(BF16)
        for h in range(NH_F):
            hi, mid, lo = _split3(cv[:, h:h + 1])
            aq = jnp.where(lane == 0, hi, jnp.where(lane == 1, mid, jnp.where(lane == 2, lo, jnp.where(lane < 6, 1.0, 0.0))))
            ak = jnp.where(lane < 3, 1.0, jnp.where(lane == 3, -hi, jnp.where(lane == 4, -mid, jnp.where(lane == 5, -lo, 0.0))))
            sl = slice(h * DH_F, (h + 1) * DH_F)
            qo_ref[h] = jnp.concatenate([qv[:, sl] * (DH_F ** -0.5), aq.astype(BF16)], axis=1).astype(BF16)
            ko_ref[h] = jnp.concatenate([kv[:, sl], ak.astype(BF16)], axis=1)
            vo_ref[h] = jnp.concatenate([vv[:, sl], one], axis=1)

    ospec = pl.BlockSpec((NH_F, tm, 128), lambda i: (0, i, 0))
    return pl.pallas_call(
        body, name="fox_prep", grid=(t // tm,),
        in_specs=[pl.BlockSpec((tm, 512), lambda i: (i, 3)), pl.BlockSpec((tm, 512), lambda i: (i, 4)),
                  pl.BlockSpec((tm, 512), lambda i: (i, 5)), pl.BlockSpec((tm, NH_F), lambda i: (i, 0))],
        out_specs=[ospec] * 3, out_shape=[jax.ShapeDtypeStruct((NH_F, t, 128), BF16)] * 3,
        compiler_params=_cparams(("parallel",)),
    )(zbig, zbig, zbig, ct)


def _fox_fwd2(qa, ka, va, gf, plan=None):
    nh, t, _ = qa.shape
    tq = _pick(t, (512, 256))
    nq = t // tq
    group = 4
    host = _Hosted(plan, 4, 3)

    def body(*refs):
        (q_ref, k_ref, v_ref, g_ref), (y_ref, o_ref, lse_ref), _, prefs = host.split(refs)
        i = pl.program_id(0)
        host.run(0, i == 0, prefs)
        host.run(1, i == max(nq - 2, 0), prefs)
        lane = lax.broadcasted_iota(jnp.int32, (tq, 128), 1)
        ys, os_ = [], []
        lse_all = jnp.zeros((tq, 128), F32)
        for h0 in range(0, nh, group):
            heads = range(h0, h0 + group)
            qvs = [q_ref[h] for h in heads]

            def blk(j, carry, masked, heads=heads, qvs=qvs):
                k0 = pl.multiple_of(j * tq, tq)
                out = []
                for (m, acc), h, qv in zip(carry, heads, qvs):
                    s = lax.dot_general(qv, k_ref[h, pl.ds(k0, tq), :], (((1,), (1,)), ((), ())), preferred_element_type=F32)
                    if masked:
                        s = jnp.where(_causal_mask(tq), s, -jnp.inf)
                    m_new = jnp.maximum(m, jnp.max(s, axis=1, keepdims=True))
                    p = jnp.exp(s - m_new).astype(BF16)
                    pv = lax.dot_general(p, v_ref[h, pl.ds(k0, tq), :], (((1,), (0,)), ((), ())), preferred_element_type=F32)
                    out.append((m_new, jnp.exp(m - m_new) * acc + pv))
                return tuple(out)

            init = tuple((jnp.full((tq, 1), -jnp.inf, F32), jnp.zeros((tq, 128), F32)) for _ in heads)
            carry = lax.fori_loop(0, i, lambda j, c: blk(j, c, False), init)
            for (m, acc), h in zip(blk(i, carry, True), heads):
                l = acc[:, DH_F:DH_F + 1]
                o = acc[:, :DH_F] / l
                os_.append(o)
                gh = g_ref[:, h * DH_F:(h + 1) * DH_F]
                ys.append(o * lax.rsqrt(jnp.mean(o * o, axis=-1, keepdims=True) + EPS) * gh)
                lse_all = lse_all + jnp.where(lane == h, m + jnp.log(l), 0.0)
        y_ref[...] = jnp.concatenate(ys, axis=1).astype(y_ref.dtype)
        o_ref[...] = jnp.concatenate(os_, axis=1)
        lse_ref[...] = lse_all
        host.run(2, i == nq - 1, prefs)

    full = pl.BlockSpec((nh, t, 128), lambda i: (0, 0, 0))
    hc = host.call_args()
    res = pl.pallas_call(
        body, name="fox_fwd", grid=(nq,),
        in_specs=[pl.BlockSpec((nh, tq, 128), lambda i: (0, i, 0)), full, full, pl.BlockSpec((1, nh * DH_F), lambda i: (0, 0))]
        + hc["in_specs"],
        out_specs=[pl.BlockSpec((tq, nh * DH_F), lambda i: (i, 0)), pl.BlockSpec((tq, nh * DH_F), lambda i: (i, 0)),
                   pl.BlockSpec((tq, 128), lambda i: (i, 0))] + hc["out_specs"],
        out_shape=[jax.ShapeDtypeStruct((t, nh * DH_F), BF16), jax.ShapeDtypeStruct((t, nh * DH_F), F32),
                   jax.ShapeDtypeStruct((t, 128), F32)] + hc["out_shape"],
        scratch_shapes=hc["scratch"], input_output_aliases=hc["aliases"], compiler_params=_cparams(("arbitrary",)),
    )(qa, ka, va, gf, *hc["args"])
    return res[:3], res[3:]


def _fox_bwd_prep(dycat, o, gf):
    t = o.shape[0]
    tm = _pick(t, (512, 256))

    def body(dy_ref, o_ref, g_ref, do_ref, dl_ref, dg_ref):
        lane = lax.broadcasted_iota(jnp.int32, (tm, 128), 1)
        dyv, ov, gv = dy_ref[...], o_ref[...], g_ref[...]
        dgs = []
        dl = jnp.zeros((tm, 128), F32)
        pad = jnp.zeros((tm, AUG), BF16)
        for h in range(NH_F):
            sl = slice(h * DH_F, (h + 1) * DH_F)
            dx, dg = _rms_bwd_val(dyv[:, sl], ov[:, sl], gv[:, sl])
            dgs.append(dg)
            do_ref[h] = jnp.concatenate([dx.astype(BF16), pad], axis=1)
            dl = dl + jnp.where(lane == h, jnp.sum(dx * ov[:, sl], axis=-1, keepdims=True), 0.0)
        dl_ref[...] = dl

        @pl.when(pl.program_id(0) == 0)
        def _():
            dg_ref[...] = jnp.zeros_like(dg_ref)
        dg_ref[...] += jnp.concatenate(dgs, axis=1)

    return pl.pallas_call(
        body, name="fox_bwd_prep", grid=(t // tm,),
        in_specs=[pl.BlockSpec((tm, 512), lambda i: (i, 1)), pl.BlockSpec((tm, 512), lambda i: (i, 0)),
                  pl.BlockSpec((1, 512), lambda i: (0, 0))],
        out_specs=[pl.BlockSpec((NH_F, tm, 128), lambda i: (0, i, 0)), pl.BlockSpec((tm, 128), lambda i: (i, 0)),
                   pl.BlockSpec((1, 512), lambda i: (0, 0))],
        out_shape=[jax.ShapeDtypeStruct((NH_F, t, 128), BF16), jax.ShapeDtypeStruct((t, 128), F32),
                   jax.ShapeDtypeStruct((1, 512), F32)],
        compiler_params=_cparams(("arbitrary",)),
    )(dycat, o, gf)


def _fox_bwd2(qa, ka, va, doa, lse, delta):
    nh, t, _ = qa.shape
    tq = _pick(t, (512, 256))
    nq = t // tq

    group = 2

    def tdot(a, b, cb):
        return lax.dot_general(a, b, (((0,), (cb,)), ((), ())), preferred_element_type=F32)

    def body(q_ref, k_ref, v_ref, do_ref, lse_ref, dl_ref, dq_ref, dk_ref, dv_ref):
        hp, j = pl.program_id(0), pl.program_id(1)

        @pl.when(j == 0)
        def _():
            dq_ref[...] = jnp.zeros_like(dq_ref)

        lane = lax.broadcasted_iota(jnp.int32, (tq, 128), 1)

        def blk(i, carry, masked):
            rows = pl.ds(pl.multiple_of(i * tq, tq), tq)
            lse_t, dl_t = lse_ref[rows, :], dl_ref[rows, :]
            out = []
            for g, (dk, dv) in enumerate(carry):
                h = hp * group + g
                kb, vb = k_ref[g], v_ref[g]
                qb, dob = q_ref[g, rows, :], do_ref[g, rows, :]
                lse_h = jnp.sum(jnp.where(lane == h, lse_t, 0.0), axis=1, keepdims=True)
                dl_h = jnp.sum(jnp.where(lane == h, dl_t, 0.0), axis=1, keepdims=True)
                s = lax.dot_general(qb, kb, (((1,), (1,)), ((), ())), preferred_element_type=F32)
                if masked:
                    s = jnp.where(_causal_mask(tq), s, -jnp.inf)
                p = jnp.exp(s - lse_h)
                dp = lax.dot_general(dob, vb, (((1,), (1,)), ((), ())), preferred_element_type=F32)
                ds = (p * (dp - dl_h)).astype(BF16)
                dv = dv + tdot(dob, p.astype(BF16), 0)
                dk = dk + tdot(qb, ds, 0)
                dq_ref[g, :, rows] += tdot(kb, ds, 1)
                out.append((dk, dv))
            return tuple(out)

        init = tuple((jnp.zeros((128, tq), F32), jnp.zeros((128, tq), F32)) for _ in range(group))
        carry = blk(j, init, True)
        carry = lax.fori_loop(j + 1, nq, lambda i, c: blk(i, c, False), carry)
        for g, (dk, dv) in enumerate(carry):
            dk_ref[g] = dk
            dv_ref[g] = dv

    full = pl.BlockSpec((group, t, 128), lambda h, j: (h, 0, 0))
    tile = pl.BlockSpec((group, tq, 128), lambda h, j: (h, j, 0))
    cols = pl.BlockSpec((t, 128), lambda h, j: (0, 0))
    full_t = pl.BlockSpec((group, 128, t), lambda h, j: (h, 0, 0))
    tile_t = pl.BlockSpec((group, 128, tq), lambda h, j: (h, 0, j))
    return pl.pallas_call(
        body, name="fox_bwd", grid=(nh // group, nq), in_specs=[full, tile, tile, full, cols, cols],
        out_specs=[full_t, tile_t, tile_t], out_shape=[jax.ShapeDtypeStruct((nh, 128, t), F32)] * 3,
        compiler_params=_cparams(("parallel", "arbitrary")),
    )(qa, ka, va, doa, lse, delta)


def _fox_bwd_post(dqa, dka, dva):
    nh, _, t = dqa.shape
    tm = _pick(t, (512, 256))

    def body(dq_ref, dk_ref, dv_ref, oq_ref, ok_ref, ov_ref, dc_ref):
        qs, ks, vs, dcs = [], [], [], []
        for h in range(nh):
            dq, dk = dq_ref[h], dk_ref[h]
            qs.append(dq.T[:, :DH_F] * (DH_F ** -0.5))
            ks.append(dk.T[:, :DH_F])
            vs.append(dv_ref[h].T[:, :DH_F])
            dcs.append(dq[DH_F:DH_F + 1, :] - dk[DH_F + 3:DH_F + 4, :])
        oq_ref[...] = jnp.concatenate(qs, axis=1).astype(BF16)
        ok_ref[...] = jnp.concatenate(ks, axis=1).astype(BF16)
        ov_ref[...] = jnp.concatenate(vs, axis=1).astype(BF16)
        dc_ref[...] = jnp.concatenate(dcs, axis=0)

    ispec = pl.BlockSpec((nh, 128, tm), lambda i: (0, 0, i))
    ospec = pl.BlockSpec((tm, nh * DH_F), lambda i: (i, 0))
    return pl.pallas_call(
        body, name="fox_bwd_post", grid=(t // tm,), in_specs=[ispec] * 3,
        out_specs=[ospec] * 3 + [pl.BlockSpec((nh, tm), lambda i: (0, i))],
        out_shape=[jax.ShapeDtypeStruct((t, nh * DH_F), BF16)] * 3 + [jax.ShapeDtypeStruct((nh, t), F32)],
        compiler_params=_cparams(("parallel",)),
    )(dqa, dka, dva)


W_BIG = 6 * 512
IN_OFF = (0, 512, 1024, 1544, 2056, 2568)
IN_GATES = (1536, 3080)


def _heads(a, nh):
    t = a.shape[0]
    return a.reshape(t, nh, -1).transpose(1, 0, 2)


def _unheads(a):
    nh, t, dh = a.shape
    return a.transpose(1, 0, 2).reshape(t, nh * dh)


FFN1 = ("ffn1_w_gate", "ffn1_w_up", "ffn1_w_down")
REST = ("w_in", "w_out", "ffn2_w_gate", "ffn2_w_up", "ffn2_w_down", "w_ple_gate", "w_ple_proj")
SPLIT = {n: 1 if n == "w_in" else 0 for n in FFN1 + REST}


def _rs_partials(names, gw, c_idx, twins):
    wire = [twins[n] if n in twins else _cast_other_half("rs_cast_" + n, gw[n], c_idx, SPLIT[n]) for n in names]
    swapped = _swap("rs_swap_" + names[0], wire, [SPLIT[n] if n in twins else None for n in names])
    return [_add_my_half("rs_add_" + n, gw[n], r, c_idx, SPLIT[n]) for n, r in zip(names, swapped)]


def _local_step(x, p, tgt, sp, wg1, wu1, wd1, rest_slots, c_idx, place):
    t, d = x.shape
    slot = dict(zip(REST + ("conv_qk",), rest_slots))
    (h1, xn1, g1, u1), (w_in, conv_w) = _ffn_fwd(
        "ffn1", x, sp["ffn1_norm"], wg1, wu1, wd1, plan=_gather_plan([slot["w_in"], slot["conv_qk"]], [SPLIT["w_in"], None]))
    w_in, conv_w = w_in.reshape(-1, d), _from_chip_blocks(conv_w)
    w_big = jnp.concatenate([w_in[o:o + 512] for o in IN_OFF], axis=0)
    w_small = jnp.concatenate([w_in[IN_GATES[0]:IN_GATES[0] + 8], w_in[IN_GATES[1]:IN_GATES[1] + 8],
                               jnp.zeros((112, d), w_in.dtype)], axis=0)
    u, zbig = _norm_mm("in_big", h1, sp["mix_norm"], w_big, True, BF16)
    zs = _mm_nt("in_small", u, w_small, tm=1024, tk=1024)
    zsr = zs.T
    qk_act = _conv_fwd(zbig, conv_w)
    bm_c, bf_c = sp["b_mlstm_gates"], sp["b_fox_f"]
    y_m, cst, mst = _mlstm_fwd(qk_act, zbig, zs, zsr, bm_c, bm_c.T, sp["mlstm_out_norm"])
    c = _fox_cumsum(zsr, bf_c.T)
    qa, ka, va = _fox_prep(zbig, c.T)
    (y_ft, o_f, lse), late = _fox_fwd2(qa, ka, va, sp["fox_out_norm"],
                                       plan=_gather_plan([slot[n] for n in REST[1:]], [SPLIT[n] for n in REST[1:]]))
    full = dict(zip(REST[1:], late))
    w_out, w_pg = (full[n].reshape(-1, d) for n in ("w_out", "w_ple_gate"))
    wg2, wu2, wd2 = full["ffn2_w_gate"], full["ffn2_w_up"], full["ffn2_w_down"]
    w_pp = _from_chip_blocks(full["w_ple_proj"])
    tm = _pick(t, (1024, 512, 256))
    h2 = _mm("out_proj", [(y_m, (tm, 512), lambda i, j, k: (i, 0), w_out, (512, d), lambda i, j, k: (0, 0)),
                          (y_ft, (tm, 512), lambda i, j, k: (i, 0), w_out, (512, d), lambda i, j, k: (1, 0))],
             (t, d), (tm, d), lambda i, j, k: (i, 0), (t // tm, 1, 1), 2, res=h1)
    (h3, xn2, g2, u2), _ = _ffn_fwd("ffn2", h2, sp["ffn2_norm"], wg2, wu2, wd2)
    hn3, gate_pre = _norm_mm("ple_gate", h3, sp["ple_gate_norm"], w_pg, False, F32)
    pp = _mm_nn("ple_proj", p, w_pp, tm=1024)

    def head_fn(h3_t, gp_t, pp_t, tgt_t, g_pp, g_fin):
        gate = _sigmoid(gp_t)
        ppn = _rms_fwd_val(pp_t, g_pp)
        h4 = h3_t + gate * ppn
        err = _rms_fwd_val(h4, g_fin) - tgt_t
        loss = 0.5 * jnp.sum(jnp.mean(err * err, axis=-1, keepdims=True), axis=0, keepdims=True)
        dh4, dg_fin = _rms_bwd_val(err * (1.0 / d), h4, g_fin)
        dpp, dg_pp = _rms_bwd_val(dh4 * gate, pp_t, g_pp)
        dgp = dh4 * ppn * gate * (1.0 - gate)
        return dh4, dgp, dpp, jnp.broadcast_to(loss, (1, 128)), dg_fin, dg_pp

    dh4, dgp, dpp, loss_part, dg_fin, dg_pp = _rowwise(
        "loss_head", head_fn, [h3, gate_pre, pp, tgt], [sp["ple_proj_norm"], sp["final_norm"]],
        [(d, F32), (d, BF16), (d, BF16)], [((1, 128), F32), ((1, d), F32), ((1, d), F32)])
    gw, gs = {}, {"final_norm": dg_fin, "ple_proj_norm": dg_pp}
    gw["w_ple_gate"] = _mm_tn("d_w_pg", hn3, dgp, tm=1024, tn=1024)
    gw["w_ple_proj"] = _mm_tn("d_w_pp", p, dpp, tn=1024)
    dhn3 = _mm_nt("d_hn3", dgp, w_pg, tm=1024, tn=1024, tk=1024)

    def res_norm_bwd(dn_t, h_t, dres_t, g):
        dx, dg = _rms_bwd_val(dn_t, h_t, g)
        return dres_t + dx, dg

    dh3, gs["ple_gate_norm"] = _rowwise("ple_norm_bwd", res_norm_bwd, [dhn3, h3, dh4], [sp["ple_gate_norm"]],
                                        [(d, F32)], [((1, d), F32)])
    (dh2, gs["ffn2_norm"], gw["ffn2_w_gate"], gw["ffn2_w_up"], gw["ffn2_w_down"]), _, twins2 = _ffn_bwd(
        "ffn2", dh3, h2, sp["ffn2_norm"], xn2, g2, u2, wg2, wu2, wd2)
    dycat = _mm_nt("d_ycat", dh2, w_out, tm=1024, tn=1024, tk=1024)
    gw["w_out"] = jnp.concatenate([_mm_tn("d_w_out_m", y_m, dh2, tn=1024), _mm_tn("d_w_out_f", y_ft, dh2, tn=1024)], axis=0)
    doa, delta, gs["fox_out_norm"] = _fox_bwd_prep(dycat, o_f, sp["fox_out_norm"])
    dq_f, dk_f, dv_f, dct = _fox_bwd_post(*_fox_bwd2(qa, ka, va, doa, lse, delta))
    dfp = _fox_gate_bwd(zsr, bf_c.T, dct)
    dact, dv_m, do_m, dzs_m, dzr_m, gs["mlstm_out_norm"] = _mlstm_bwd(
        qk_act, zbig, zs, zsr, bm_c, bm_c.T, sp["mlstm_out_norm"], cst, mst, dycat)
    dqk, gw["conv_qk"] = _conv_bwd(zbig, dact, conv_w)
    dz_big = jnp.concatenate([dqk, dv_m, do_m, dq_f, dk_f, dv_f], axis=1)
    dzs = dzs_m + jnp.pad(jnp.concatenate([dzr_m, dfp], axis=0).T, ((0, 0), (0, 112)))
    dw_big = _mm_tn("d_w_big", dz_big, u, tn=1024)
    dw_small = _mm_tn("d_w_small", dzs, u, tn=1024)
    gw["w_in"] = jnp.concatenate([dw_big[0:1536], dw_small[0:8], dw_big[1536:3072], dw_small[8:16]], axis=0)
    du_a = _mm_nn("d_u_big", dz_big, w_big, tm=1024, tn=1024, tk=1024)
    du_b = _mm_nn("d_u_small", dzs, w_small, tm=1024, tn=1024)

    def mix_norm_bwd(da_t, db_t, h_t, dres_t, dzs_t, g):
        dx, dg = _rms_bwd_val(da_t + db_t, h_t, g)
        return dres_t + dx, dg, _colsum(dzs_t)

    dh1, gs["mix_norm"], dbias = _rowwise("mix_norm_bwd", mix_norm_bwd, [du_a, du_b, h1, dh2, dzs], [sp["mix_norm"]],
                                          [(d, F32)], [((1, d), F32), ((1, 128), F32)])
    gs["b_mlstm_gates"], gs["b_fox_f"] = dbias[:, 0:8], dbias[:, 8:16]
    conv_grad = gw.pop("conv_qk")
    gw["w_ple_proj"] = _chip_blocks(gw["w_ple_proj"])
    for n in ("w_in", "w_out", "w_ple_gate"):
        gw[n] = gw[n].reshape(4, -1, gw[n].shape[-1])
    twins = dict(zip(("ffn2_w_gate", "ffn2_w_up", "ffn2_w_down"), twins2))
    part_rest = dict(zip(REST, _rs_partials(REST, gw, c_idx, twins)))
    light = ("w_in", "w_out", "w_ple_gate", "w_ple_proj")
    part_ffn1 = []

    def own_plan(dws, dw_twins):
        part_ffn1.extend(_rs_partials(FFN1, dict(zip(FFN1, dws)), c_idx, dict(zip(FFN1, dw_twins))))
        return _scatter_plan([pb for _, pb in part_ffn1])

    (grad_x, gs["ffn1_norm"], _, _, _), (l_light, l_down, l_gate, l_up, landed_ffn1) = _ffn_bwd_late_dx(
        "ffn1", dh1, x, sp["ffn1_norm"], xn1, g1, u1, wg1, wu1, wd1,
        _scatter_plan([part_rest[n][1] for n in light]),
        [_scatter_plan([part_rest[n][1]]) for n in ("ffn2_w_down", "ffn2_w_gate", "ffn2_w_up")], own_plan)
    landed_rest = dict(zip(light + ("ffn2_w_down", "ffn2_w_gate", "ffn2_w_up"), list(l_light) + [l_down[0], l_gate[0], l_up[0]]))
    names = REST + FFN1
    parts = [part_rest[n] for n in REST] + part_ffn1
    landed = [landed_rest[n] for n in REST] + list(landed_ffn1)
    mine = [_sum4("rs_sum_" + n, a, pf, place, SPLIT[n]) for n, a, (pf, _) in zip(names, landed, parts)]
    grads = dict(zip(names, _join_halves("rs_join", mine, [SPLIT[n] for n in names])))
    return loss_part, grad_x, grads, gs, conv_grad


ANY = pl.BlockSpec(memory_space=pl.ANY)
MESH = pl.DeviceIdType.MESH


def _place():
    x, y, c = lax.axis_index("x"), lax.axis_index("y"), lax.axis_index("c")
    chips = [(1 - x, y), (x, 1 - y), (1 - x, 1 - y)]
    return x, y, c, 2 * x + y, (x, y, 1 - c), chips


def _rcopy(src, dst, ssem, rsem, dev):
    return pltpu.make_async_remote_copy(src_ref=src, dst_ref=dst, send_sem=ssem, recv_sem=rsem, device_id=dev,
                                        device_id_type=MESH)


def _half(ref, lead, axis, idx, half):
    return ref.at[(slice(None),) * (lead + axis) + (pl.ds(idx * half, half),)]


def _to_slot(name, a, me_idx, dtype):
    r, cdim = a.shape
    tr = _pick(r, (256, 176, 128, 64))

    def body(me_ref, a_ref, o_ref):
        o_ref[...] = a_ref[...].astype(o_ref.dtype)

    return pl.pallas_call(
        body, name=name,
        grid_spec=pltpu.PrefetchScalarGridSpec(
            num_scalar_prefetch=1, grid=(r // tr,), in_specs=[pl.BlockSpec((tr, cdim), lambda i, me_ref: (i, 0))],
            out_specs=pl.BlockSpec((None, tr, cdim), lambda i, me_ref: (me_ref[0], i, 0))),
        out_shape=jax.ShapeDtypeStruct((4, r, cdim), dtype), compiler_params=_cparams(("parallel",)),
    )(me_idx, a)


def _gather4(name, bufs, split):
    return _run_plan(name, _gather_plan(bufs, split))


def _gather_plan(bufs, split):
    n = len(bufs)
    shapes = [b.shape[1:] for b in bufs]

    def ctx(outs):
        x, y, c, me, sib, chips = _place()

        def part(ref, a, which):
            if split[a] is None:
                return ref
            return _half(ref, 0, split[a], which, shapes[a][split[a]] // 2)

        return c, me, sib, chips, part

    def ici(outs, sems, a, j, chip, c, me, part):
        mine = part(outs[a].at[me], a, c)
        return _rcopy(mine, mine, sems[0].at[3 * a + j], sems[1].at[3 * a + j], (*chip, c))

    def fwd(outs, sems, a, j, chip, c, sib, part, which):
        blk = part(outs[a].at[2 * chip[0] + chip[1]], a, which)
        return _rcopy(blk, blk, sems[2].at[3 * a + j], sems[3].at[3 * a + j], sib)

    def start(ins, outs, sems):
        c, me, sib, chips, part = ctx(outs)
        for a in range(n):
            for j, chip in enumerate(chips):
                ici(outs, sems, a, j, chip, c, me, part).start()

    def mid(ins, outs, sems):
        c, me, sib, chips, part = ctx(outs)
        for j, chip in enumerate(chips):
            for a in range(n):
                blk = part(outs[a].at[2 * chip[0] + chip[1]], a, c)
                _rcopy(blk, blk, sems[0].at[3 * a + j], sems[1].at[3 * a + j], sib).wait_recv()
                if split[a] is not None:
                    fwd(outs, sems, a, j, chip, c, sib, part, c).start()

    def end(ins, outs, sems):
        c, me, sib, chips, part = ctx(outs)
        for j, chip in enumerate(chips):
            for a in range(n):
                if split[a] is not None:
                    fwd(outs, sems, a, j, chip, c, sib, part, 1 - c).wait_recv()
        for a in range(n):
            for j, chip in enumerate(chips):
                ici(outs, sems, a, j, chip, c, me, part).wait_send()
                if split[a] is not None:
                    fwd(outs, sems, a, j, chip, c, sib, part, c).wait_send()

    return dict(ins=list(bufs), outs=[jax.ShapeDtypeStruct(b.shape, b.dtype) for b in bufs], alias=True,
                sems=[pltpu.SemaphoreType.DMA((3 * n,))] * 4, phases=(start, mid, end))


def _run_plan(name, plan):
    ni, no = len(plan["ins"]), len(plan["outs"])

    def body(*refs):
        ins, outs, sems = refs[:ni], refs[ni:ni + no], refs[ni + no:]
        for phase in plan["phases"]:
            phase(ins, outs, sems)

    return pl.pallas_call(
        body, name=name, in_specs=[ANY] * ni, out_specs=[ANY] * no, out_shape=plan["outs"],
        input_output_aliases={a: a for a in range(ni)} if plan["alias"] else {}, scratch_shapes=plan["sems"],
    )(*plan["ins"])


class _Hosted:
    def __init__(self, plan, n_in, n_out):
        self.plan, self.n_in, self.n_out = plan, n_in, n_out
        self.ni, self.no, self.ns = (len(plan["ins"]) if plan else 0, len(plan["outs"]) if plan else 0,
                                     len(plan["sems"]) if plan else 0)

    def split(self, refs):
        a, b = self.n_in, self.n_in + self.ni
        c, d = b + self.n_out, b + self.n_out + self.no
        e = len(refs) - self.ns
        return refs[:a], refs[b:c], refs[d:e], (refs[a:b], refs[c:d], refs[e:])

    def run(self, k, cond, prefs):
        if self.plan is not None:
            @pl.when(cond)
            def _():
                self.plan["phases"][k](*prefs)

    def call_args(self):
        p = self.plan
        if p is None:
            return dict(in_specs=[], out_specs=[], out_shape=[], scratch=[], aliases={}, args=[])
        al = {self.n_in + a: self.n_out + a for a in range(self.ni)} if p["alias"] else {}
        return dict(in_specs=[ANY] * self.ni, out_specs=[ANY] * self.no, out_shape=list(p["outs"]), scratch=list(p["sems"]),
                    aliases=al, args=list(p["ins"]))


def _swap(name, arrs, halve):
    n = len(arrs)

    def half_shape(a, ax):
        return a.shape if ax is None else (a.shape[0],) + tuple(d // 2 if i == ax else d for i, d in enumerate(a.shape[1:]))

    def body(*refs):
        ins, outs = refs[:n], refs[n:2 * n]
        ssem, rsem = refs[2 * n:]
        x, y, c, me, sib, chips = _place()
        cps = []
        for a in range(n):
            src = ins[a] if halve[a] is None else _half(ins[a], 1, halve[a], 1 - c, arrs[a].shape[1 + halve[a]] // 2)
            cps.append(_rcopy(src, outs[a], ssem.at[a], rsem.at[a], sib))
        for cp in cps:
            cp.start()
        for cp in cps:
            cp.wait()

    return pl.pallas_call(
        body, name=name, in_specs=[ANY] * n, out_specs=[ANY] * n,
        out_shape=[jax.ShapeDtypeStruct(half_shape(a, ax), a.dtype) for a, ax in zip(arrs, halve)],
        scratch_shapes=[pltpu.SemaphoreType.DMA((n,))] * 2,
    )(*arrs)


def _scatter4(name, arrs):
    return _run_plan(name, _scatter_plan(arrs))


def _scatter_plan(arrs):
    n = len(arrs)

    def send(ins, outs, sems, a, j, chip, c, me):
        return _rcopy(ins[a].at[2 * chip[0] + chip[1]], outs[a].at[me], sems[0].at[3 * a + j], sems[1].at[3 * a + j], (*chip, c))

    def start(ins, outs, sems):
        x, y, c, me, sib, chips = _place()
        for a in range(n):
            for j, chip in enumerate(chips):
                send(ins, outs, sems, a, j, chip, c, me).start()

    def mid(ins, outs, sems):
        pass

    def end(ins, outs, sems):
        x, y, c, me, sib, chips = _place()
        for a in range(n):
            for j, chip in enumerate(chips):
                blk = outs[a].at[2 * chip[0] + chip[1]]
                _rcopy(blk, blk, sems[0].at[3 * a + j], sems[1].at[3 * a + j], sib).wait_recv()
        for a in range(n):
            for j, chip in enumerate(chips):
                send(ins, outs, sems, a, j, chip, c, me).wait_send()

    return dict(ins=list(arrs), outs=[jax.ShapeDtypeStruct(a.shape, a.dtype) for a in arrs], alias=False,
                sems=[pltpu.SemaphoreType.DMA((3 * n,))] * 2, phases=(start, mid, end))


def _join_halves(name, arrs, split):
    n = len(arrs)

    def body(*refs):
        outs = refs[n:2 * n]
        ssem, rsem = refs[2 * n:]
        x, y, c, me, sib, chips = _place()
        cps = []
        for a in range(n):
            mine = _half(outs[a], 0, split[a], c, arrs[a].shape[split[a]] // 2)
            cp = _rcopy(mine, mine, ssem.at[a], rsem.at[a], sib)
            cp.start()
            cps.append(cp)
        for a in range(n):
            blk = _half(outs[a], 0, split[a], 1 - c, arrs[a].shape[split[a]] // 2)
            _rcopy(blk, blk, ssem.at[a], rsem.at[a], sib).wait_recv()
        for cp in cps:
            cp.wait_send()

    return pl.pallas_call(
        body, name=name, in_specs=[ANY] * n, out_specs=[ANY] * n,
        out_shape=[jax.ShapeDtypeStruct(a.shape, a.dtype) for a in arrs],
        input_output_aliases={a: a for a in range(n)}, scratch_shapes=[pltpu.SemaphoreType.DMA((n,))] * 2,
    )(*arrs)


def _allreduce_small(s):
    r, cdim = s.shape

    def body(s_ref, o_ref, buf, ssem, rsem):
        x, y, c, me, sib, chips = _place()
        me8 = 4 * x + 2 * y + c
        buf[me8] = s_ref[...]
        flips = [(fx, fy, fc) for fx in (0, 1) for fy in (0, 1) for fc in (0, 1)][1:]
        cps = []
        for k, (fx, fy, fc) in enumerate(flips):
            peer = (x ^ fx if fx else x, y ^ fy if fy else y, c ^ fc if fc else c)
            cp = _rcopy(s_ref, buf.at[me8], ssem.at[k], rsem.at[k], peer)
            cp.start()
            cps.append(cp)
        for k, (fx, fy, fc) in enumerate(flips):
            src = 4 * (x ^ fx if fx else x) + 2 * (y ^ fy if fy else y) + (c ^ fc if fc else c)
            _rcopy(s_ref, buf.at[src], ssem.at[k], rsem.at[k], sib).wait_recv()
        for cp in cps:
            cp.wait_send()
        acc = buf[0]
        for k in range(1, 8):
            acc = acc + buf[k]
        o_ref[...] = acc

    vm = pl.BlockSpec(memory_space=pltpu.VMEM)
    return pl.pallas_call(
        body, name="allreduce_small", in_specs=[vm], out_specs=vm, out_shape=jax.ShapeDtypeStruct((r, cdim), F32),
        scratch_shapes=[pltpu.VMEM((8, r, cdim), F32), pltpu.SemaphoreType.DMA((7,)), pltpu.SemaphoreType.DMA((7,))],
    )(s)


def _add_my_half(name, g, recv, c_idx, axis):
    nb, hr, hc = recv.shape
    tr = _pick(hr, (256, 176, 128, 64))
    if axis == 0:
        g4 = g.reshape(nb, 2, hr, hc)
        gspec = pl.BlockSpec((None, None, tr, hc), lambda b, i, c_ref: (b, c_ref[0], i, 0))
    else:
        g4 = g
        gspec = pl.BlockSpec((None, tr, hc), lambda b, i, c_ref: (b, i, c_ref[0]))

    def body(c_ref, g_ref, r_ref, o_ref, ob_ref):
        s = g_ref[...] + r_ref[...].astype(F32)
        o_ref[...] = s
        ob_ref[...] = s.astype(BF16)

    ospec = pl.BlockSpec((None, tr, hc), lambda b, i, c_ref: (b, i, 0))
    return pl.pallas_call(
        body, name=name,
        grid_spec=pltpu.PrefetchScalarGridSpec(
            num_scalar_prefetch=1, grid=(nb, hr // tr), in_specs=[gspec, ospec], out_specs=[ospec, ospec]),
        out_shape=[jax.ShapeDtypeStruct((nb, hr, hc), F32), jax.ShapeDtypeStruct((nb, hr, hc), BF16)],
        compiler_params=_cparams(("parallel", "parallel")),
    )(c_idx, g4, recv)


def _sum4(name, landed, own, place, axis):
    nb, h, cdim = landed.shape
    tr = _pick(h, (256, 176, 128, 64))
    nt = h // tr

    def body(p_ref, a1_ref, a2_ref, a3_ref, own_ref, o_ref):
        o_ref[...] = ((own_ref[...] + a1_ref[...].astype(F32)) + a2_ref[...].astype(F32)) + a3_ref[...].astype(F32)

    def nxt(k):
        return pl.BlockSpec((None, tr, cdim), lambda i, p_ref: ((p_ref[0] + k) % nb, i, 0))

    if axis == 0:
        ospec = pl.BlockSpec((tr, cdim), lambda i, p_ref: (p_ref[1] * nt + i, 0))
        oshape = (2 * h, cdim)
    else:
        ospec = pl.BlockSpec((tr, cdim), lambda i, p_ref: (i, p_ref[1]))
        oshape = (h, 2 * cdim)
    return pl.pallas_call(
        body, name=name,
        grid_spec=pltpu.PrefetchScalarGridSpec(
            num_scalar_prefetch=1, grid=(nt,), in_specs=[nxt(1), nxt(2), nxt(3), nxt(0)], out_specs=ospec),
        out_shape=jax.ShapeDtypeStruct(oshape, F32), compiler_params=_cparams(("parallel",)),
    )(place, landed, landed, landed, own)


def _cast_other_half(name, g, c_idx, axis):
    nb, r, cdim = g.shape
    hr, hc = (r // 2, cdim) if axis == 0 else (r, cdim // 2)
    tr = _pick(hr, (256, 176, 128, 64))
    if axis == 0:
        g4 = g.reshape(nb, 2, hr, hc)
        gspec = pl.BlockSpec((None, None, tr, hc), lambda b, i, c_ref: (b, 1 - c_ref[0], i, 0))
    else:
        g4 = g
        gspec = pl.BlockSpec((None, tr, hc), lambda b, i, c_ref: (b, i, 1 - c_ref[0]))

    def body(c_ref, g_ref, o_ref):
        o_ref[...] = g_ref[...].astype(BF16)

    return pl.pallas_call(
        body, name=name,
        grid_spec=pltpu.PrefetchScalarGridSpec(
            num_scalar_prefetch=1, grid=(nb, hr // tr), in_specs=[gspec],
            out_specs=pl.BlockSpec((None, tr, hc), lambda b, i, c_ref: (b, i, 0))),
        out_shape=jax.ShapeDtypeStruct((nb, hr, hc), BF16), compiler_params=_cparams(("parallel", "parallel")),
    )(c_idx, g4)


def _adamw(name, w, g, m, v):
    c1 = 1.0 - ADAM_B1 ** ADAM_STEP
    c2 = 1.0 - ADAM_B2 ** ADAM_STEP

    def fn(w_t, g_t, m_t, v_t):
        m_n = ADAM_B1 * m_t + (1.0 - ADAM_B1) * g_t
        v_n = ADAM_B2 * v_t + (1.0 - ADAM_B2) * (g_t * g_t)
        delta = -ADAM_LR * ((m_n / c1) / (jnp.sqrt(v_n / c2) + ADAM_EPS) + ADAM_WD * w_t)
        return delta, m_n, v_n

    cdim = w.shape[1]
    return _rowwise(name, fn, [w, g, m, v], [], [(cdim, F32)] * 3, tm=_pick(w.shape[0], (512, 352, 256, 128, 64, 8)))


BIG = ("ffn1_w_gate", "ffn1_w_up", "ffn1_w_down", "w_in", "w_out", "ffn2_w_gate", "ffn2_w_up", "ffn2_w_down",
       "w_ple_gate", "w_ple_proj")
SMALL = ("ffn1_norm", "mix_norm", "b_mlstm_gates", "b_fox_f", "mlstm_out_norm", "fox_out_norm", "ffn2_norm",
         "ple_gate_norm", "ple_proj_norm", "final_norm")
WEIGHTS = ("ffn1_norm", "ffn1_w_gate", "ffn1_w_up", "ffn1_w_down", "mix_norm", "w_in", "conv_qk", "b_mlstm_gates",
           "b_fox_f", "mlstm_out_norm", "fox_out_norm", "w_out", "ffn2_norm", "ffn2_w_gate", "ffn2_w_up", "ffn2_w_down",
           "ple_gate_norm", "w_ple_gate", "w_ple_proj", "ple_proj_norm", "final_norm")
TRANSPOSED = ("ffn1_w_gate", "ffn1_w_up", "w_in", "ffn2_w_gate", "ffn2_w_up")
PACK_W = 1024


def _chip_blocks(a):
    r, c4 = a.shape
    return a.reshape(r, 4, c4 // 4).transpose(1, 0, 2)


def _from_chip_blocks(a):
    nb, r, c = a.shape
    return a.transpose(1, 0, 2).reshape(r, nb * c)


def kernel(x, p, ffn1_norm, ffn1_w_gate, ffn1_w_up, ffn1_w_down, mix_norm, w_in, conv_qk, b_mlstm_gates, b_fox_f, mlstm_out_norm, fox_out_norm, w_out, ffn2_norm, ffn2_w_gate, ffn2_w_up, ffn2_w_down, ple_gate_norm, w_ple_gate, w_ple_proj, ple_proj_norm, final_norm, loss_target, m_ffn1_norm, m_ffn1_w_gate, m_ffn1_w_up, m_ffn1_w_down, m_mix_norm, m_w_in, m_conv_qk, m_b_mlstm_gates, m_b_fox_f, m_mlstm_out_norm, m_fox_out_norm, m_w_out, m_ffn2_norm, m_ffn2_w_gate, m_ffn2_w_up, m_ffn2_w_down, m_ple_gate_norm, m_w_ple_gate, m_w_ple_proj, m_ple_proj_norm, m_final_norm, v_ffn1_norm, v_ffn1_w_gate, v_ffn1_w_up, v_ffn1_w_down, v_mix_norm, v_w_in, v_conv_qk, v_b_mlstm_gates, v_b_fox_f, v_mlstm_out_norm, v_fox_out_norm, v_w_out, v_ffn2_norm, v_ffn2_w_gate, v_ffn2_w_up, v_ffn2_w_down, v_ple_gate_norm, v_w_ple_gate, v_w_ple_proj, v_ple_proj_norm, v_final_norm):
    w = dict(ffn1_norm=ffn1_norm, ffn1_w_gate=ffn1_w_gate, ffn1_w_up=ffn1_w_up, ffn1_w_down=ffn1_w_down, mix_norm=mix_norm,
             w_in=w_in, conv_qk=conv_qk, b_mlstm_gates=b_mlstm_gates, b_fox_f=b_fox_f, mlstm_out_norm=mlstm_out_norm,
             fox_out_norm=fox_out_norm, w_out=w_out, ffn2_norm=ffn2_norm, ffn2_w_gate=ffn2_w_gate, ffn2_w_up=ffn2_w_up,
             ffn2_w_down=ffn2_w_down, ple_gate_norm=ple_gate_norm, w_ple_gate=w_ple_gate, w_ple_proj=w_ple_proj,
             ple_proj_norm=ple_proj_norm, final_norm=final_norm)
    m = dict(ffn1_norm=m_ffn1_norm, ffn1_w_gate=m_ffn1_w_gate, ffn1_w_up=m_ffn1_w_up, ffn1_w_down=m_ffn1_w_down,
             mix_norm=m_mix_norm, w_in=m_w_in, conv_qk=m_conv_qk, b_mlstm_gates=m_b_mlstm_gates, b_fox_f=m_b_fox_f,
             mlstm_out_norm=m_mlstm_out_norm, fox_out_norm=m_fox_out_norm, w_out=m_w_out, ffn2_norm=m_ffn2_norm,
             ffn2_w_gate=m_ffn2_w_gate, ffn2_w_up=m_ffn2_w_up, ffn2_w_down=m_ffn2_w_down, ple_gate_norm=m_ple_gate_norm,
             w_ple_gate=m_w_ple_gate, w_ple_proj=m_w_ple_proj, ple_proj_norm=m_ple_proj_norm, final_norm=m_final_norm)
    v = dict(ffn1_norm=v_ffn1_norm, ffn1_w_gate=v_ffn1_w_gate, ffn1_w_up=v_ffn1_w_up, ffn1_w_down=v_ffn1_w_down,
             mix_norm=v_mix_norm, w_in=v_w_in, conv_qk=v_conv_qk, b_mlstm_gates=v_b_mlstm_gates, b_fox_f=v_b_fox_f,
             mlstm_out_norm=v_mlstm_out_norm, fox_out_norm=v_fox_out_norm, w_out=v_w_out, ffn2_norm=v_ffn2_norm,
             ffn2_w_gate=v_ffn2_w_gate, ffn2_w_up=v_ffn2_w_up, ffn2_w_down=v_ffn2_w_down, ple_gate_norm=v_ple_gate_norm,
             w_ple_gate=v_w_ple_gate, w_ple_proj=v_w_ple_proj, ple_proj_norm=v_ple_proj_norm, final_norm=v_final_norm)
    shapes = {n: w[n].shape for n in WEIGHTS}

    def view(a, n):
        return a[0].T if n in TRANSPOSED else a.reshape(-1, a.shape[-1])

    def unview(a, n):
        return (a.T if n in TRANSPOSED else a).reshape(shapes[n])

    w2, m2, v2 = ({n: view(a, n) for n, a in d.items()} for d in (w, m, v))

    c_idx = lax.axis_index("c").astype(jnp.int32).reshape(1)
    me_idx = (2 * lax.axis_index("x") + lax.axis_index("y")).astype(jnp.int32).reshape(1)
    place = jnp.concatenate([me_idx, c_idx])
    slot = {n: _to_slot("slot_" + n, w2[n], me_idx, BF16) for n in BIG}
    slot["conv_qk"] = _to_slot("slot_conv_qk", w2["conv_qk"], me_idx, F32)
    wg1, wu1, wd1 = _gather4("gather_ffn1", [slot[n] for n in FFN1], [SPLIT[n] for n in FFN1])
    sp = {n: w2[n] for n in SMALL}
    loss_part, grad_x, grads, gs, conv_grad = _local_step(
        x[0], p[0, 0], loss_target[0], sp, wg1, wu1, wd1, [slot[n] for n in REST + ("conv_qk",)], c_idx, place)
    loss = lax.psum(loss_part[0, 0], ("x", "y", "c"))

    small = [gs[n].reshape(1, -1) for n in SMALL] + [conv_grad]
    rows = [jnp.pad(a, ((0, 0), (0, PACK_W - a.shape[1]))) for a in small]
    packed = jnp.concatenate(rows, axis=0)
    packed = jnp.pad(packed, ((0, -packed.shape[0] % 8), (0, 0)))
    red = _allreduce_small(packed)
    for i, n in enumerate(SMALL):
        grads[n] = red[i:i + 1, :gs[n].size]
    dconv = red[len(SMALL):len(SMALL) + CONV_W, :conv_grad.shape[1]]
    cw = conv_qk.shape[-1]
    grads["conv_qk"] = lax.dynamic_slice_in_dim(dconv, (2 * lax.axis_index("x") + lax.axis_index("y")) * cw, cw, axis=1)

    outs = {}
    for n in WEIGHTS:
        g2 = grads[n].reshape(w2[n].shape)
        d, nm, nv = _adamw("adamw_" + n, w2[n], g2, m2[n], v2[n])
        outs[n] = tuple(unview(a, n) for a in (g2, d, nm, nv))
    return (loss, grad_x[None], *[outs[n][0] for n in WEIGHTS], *[outs[n][1] for n in WEIGHTS],
            *[outs[n][2] for n in WEIGHTS], *[outs[n][3] for n in WEIGHTS])
```

```python
import jax
import jax.numpy as jnp
from jax import lax
from jax.experimental import pallas as pl
from jax.experimental.pallas import tpu as pltpu

F32 = jnp.float32
BF16 = jnp.bfloat16
EPS = 1e-6
NH_M, DK_M, DV_M = 4, 64, 128
NH_F, DH_F = 8, 64
CONV_W = 4
ADAM_LR, ADAM_B1, ADAM_B2, ADAM_EPS, ADAM_WD, ADAM_STEP = 0.001, 0.9, 0.999, 1e-08, 0.01, 10
VMEM_LIMIT = 56 * 1024 * 1024


def _cparams(sem):
    return pltpu.CompilerParams(dimension_semantics=sem, vmem_limit_bytes=VMEM_LIMIT)


def _sigmoid(x):
    return 1.0 / (1.0 + jnp.exp(-x))


def _dot(a, b, ca, cb):
    return lax.dot_general(a.astype(BF16), b.astype(BF16), (((ca,), (cb,)), ((), ())), preferred_element_type=F32)


def _rowwise(name, fn, tiled, full, outs, accs=(), tm=512):
    rows = tiled[0].shape[0]
    tm = min(tm, rows)
    assert rows % tm == 0
    n_t, n_f, n_o, n_a = len(tiled), len(full), len(outs), len(accs)

    def body(*refs):
        ins = [r[...] for r in refs[: n_t + n_f]]
        res = fn(*ins)
        if not isinstance(res, (tuple, list)):
            res = (res,)
        orefs = refs[n_t + n_f:]
        for r, v in zip(orefs[:n_o], res[:n_o]):
            r[...] = v.astype(r.dtype)
        if n_a:
            @pl.when(pl.program_id(0) == 0)
            def _():
                for r in orefs[n_o:]:
                    r[...] = jnp.zeros_like(r)
            for r, v in zip(orefs[n_o:], res[n_o:]):
                r[...] += v.astype(r.dtype)

    in_specs = [pl.BlockSpec((tm, a.shape[1]), lambda i: (i, 0)) for a in tiled]
    in_specs += [pl.BlockSpec(a.shape, lambda i: (0, 0)) for a in full]
    out_specs = [pl.BlockSpec((tm, c), lambda i: (i, 0)) for c, _ in outs]
    out_specs += [pl.BlockSpec(s, lambda i: (0, 0)) for s, _ in accs]
    out_shape = [jax.ShapeDtypeStruct((rows, c), d) for c, d in outs]
    out_shape += [jax.ShapeDtypeStruct(s, d) for s, d in accs]
    res = pl.pallas_call(
        body, name=name, grid=(rows // tm,), in_specs=in_specs, out_specs=out_specs, out_shape=out_shape,
        compiler_params=_cparams(("arbitrary",) if n_a else ("parallel",)),
    )(*tiled, *full)
    return res


def _colsum(v):
    return jnp.sum(v, axis=0, keepdims=True)


def _rms_fwd_val(x, g):
    r = lax.rsqrt(jnp.mean(x * x, axis=-1, keepdims=True) + EPS)
    return x * r * g


def _rms_bwd_val(dy, x, g):
    r = lax.rsqrt(jnp.mean(x * x, axis=-1, keepdims=True) + EPS)
    xh = x * r
    dxh = dy * g
    dx = r * (dxh - xh * jnp.mean(dxh * xh, axis=-1, keepdims=True))
    return dx, _colsum(dy * xh)


def _mm(name, pairs, out_shape, out_block, out_map, grid, kaxis, ta=False, tb=False, scale=None, res=None,
        out_dtype=F32, plan=None, twin=False):
    n_o = 2 if twin else 1
    nk = grid[kaxis]
    npairs = len(pairs)
    ca, cb = (0 if ta else 1), (1 if tb else 0)
    acc_shape = tuple(d for d in out_block if d is not None)
    n_in = 2 * npairs + (1 if res is not None else 0)
    host = _Hosted(plan, n_in, n_o)

    def body(*refs):
        ins, o_refs, (acc_ref,), prefs = host.split(refs)
        o_ref = o_refs[0]
        in_refs = ins[: 2 * npairs]
        res_ref = ins[2 * npairs] if res is not None else None
        k = pl.program_id(kaxis)
        ids = [pl.program_id(a) for a in range(len(grid))]
        first, last = ids[0] == 0, ids[0] == grid[0] - 1
        for a in range(1, len(grid)):
            first, last = first & (ids[a] == 0), last & (ids[a] == grid[a] - 1)
        host.run(0, first, prefs)
        host.run(1, first, prefs)

        @pl.when(k == 0)
        def _():
            acc_ref[...] = jnp.zeros_like(acc_ref)

        part = None
        for p in range(npairs):
            d = _dot(in_refs[2 * p][...], in_refs[2 * p + 1][...], ca, cb)
            part = d if part is None else part + d
        acc_ref[...] += part

        @pl.when(k == nk - 1)
        def _():
            v = acc_ref[...]
            if scale is not None:
                v = v * scale
            if res_ref is not None:
                v = v + res_ref[...].astype(F32)
            o_ref[...] = v.astype(o_ref.dtype)
            if twin:
                o_refs[1][...] = v.astype(BF16)

        host.run(2, last, prefs)

    in_specs, args = [], []
    for a, ab, am, b, bb, bm in pairs:
        in_specs += [pl.BlockSpec(ab, am), pl.BlockSpec(bb, bm)]
        args += [a, b]
    if res is not None:
        in_specs.append(pl.BlockSpec(out_block, out_map))
        args.append(res)
    sem = tuple("arbitrary" if (i == kaxis or plan is not None) else "parallel" for i in range(len(grid)))
    hc = host.call_args()
    out = pl.pallas_call(
        body, name=name, grid=grid, in_specs=in_specs + hc["in_specs"],
        out_specs=[pl.BlockSpec(out_block, out_map)] * n_o + hc["out_specs"],
        out_shape=[jax.ShapeDtypeStruct(out_shape, out_dtype)] + [jax.ShapeDtypeStruct(out_shape, BF16)] * (n_o - 1)
        + hc["out_shape"],
        scratch_shapes=[pltpu.VMEM(acc_shape, F32)] + hc["scratch"], input_output_aliases=hc["aliases"],
        compiler_params=_cparams(sem),
    )(*args, *hc["args"])
    res_out = tuple(out[:2]) if twin else out[0]
    return res_out if plan is None else (res_out, out[n_o:])


def _pick(n, pref):
    for t in pref:
        if n % t == 0:
            return t
    return n


def _mm_nn(name, a, b, tm=512, tn=512, tk=512, **kw):
    (m, k), n = a.shape, b.shape[1]
    tm, tn, tk = _pick(m, (tm, 256, 128)), _pick(n, (tn, 256, 128)), _pick(k, (tk, 256, 128))
    return _mm(name, [(a, (tm, tk), lambda i, j, kk: (i, kk), b, (tk, tn), lambda i, j, kk: (kk, j))],
               (m, n), (tm, tn), lambda i, j, kk: (i, j), (m // tm, n // tn, k // tk), 2, **kw)


def _mm_nt(name, a, b, tm=512, tn=512, tk=512, **kw):
    (m, k), n = a.shape, b.shape[0]
    tm, tn, tk = _pick(m, (tm, 256, 128)), _pick(n, (tn, 256, 128)), _pick(k, (tk, 256, 128))
    return _mm(name, [(a, (tm, tk), lambda i, j, kk: (i, kk), b, (tn, tk), lambda i, j, kk: (j, kk))],
               (m, n), (tm, tn), lambda i, j, kk: (i, j), (m // tm, n // tn, k // tk), 2, tb=True, **kw)


def _mm_tn(name, a, b, tm=512, tn=512, tk=2048, **kw):
    (k, m), n = a.shape, b.shape[1]
    tm, tn, tk = _pick(m, (tm, 256, 128)), _pick(n, (tn, 256, 128)), _pick(k, (tk, 1024, 512, 256, 128))
    return _mm(name, [(a, (tk, tm), lambda i, j, kk: (kk, i), b, (tk, tn), lambda i, j, kk: (kk, j))],
               (m, n), (tm, tn), lambda i, j, kk: (i, j), (m // tm, n // tn, k // tk), 2, ta=True, **kw)


def _norm_mm(name, h, gamma, w, w_transposed, out_dtype):
    t, d = h.shape
    n = w.shape[0] if w_transposed else w.shape[1]
    tm, tn = _pick(t, (512, 256)), _pick(n, (1024, 512, 256, 128))

    def body(h_ref, gam_ref, w_ref, xn_ref, o_ref, xn_scr):
        @pl.when(pl.program_id(1) == 0)
        def _():
            xn = _rms_fwd_val(h_ref[...], gam_ref[...]).astype(BF16)
            xn_scr[...] = xn
            xn_ref[...] = xn

        o_ref[...] = _dot(xn_scr[...], w_ref[...], 1, 1 if w_transposed else 0).astype(o_ref.dtype)

    wspec = pl.BlockSpec((tn, d), lambda i, j: (j, 0)) if w_transposed else pl.BlockSpec((d, tn), lambda i, j: (0, j))
    return pl.pallas_call(
        body, name=name, grid=(t // tm, n // tn),
        in_specs=[pl.BlockSpec((tm, d), lambda i, j: (i, 0)), pl.BlockSpec((1, d), lambda i, j: (0, 0)), wspec],
        out_specs=[pl.BlockSpec((tm, d), lambda i, j: (i, 0)), pl.BlockSpec((tm, tn), lambda i, j: (i, j))],
        out_shape=[jax.ShapeDtypeStruct((t, d), BF16), jax.ShapeDtypeStruct((t, n), out_dtype)],
        scratch_shapes=[pltpu.VMEM((tm, d), BF16)], compiler_params=_cparams(("parallel", "arbitrary")),
    )(h, gamma, w)


CHAIN_ROWS = 256


def _row_chains(tm):
    n = max(tm // CHAIN_ROWS, 1)
    return [slice(r * (tm // n), (r + 1) * (tm // n)) for r in range(n)]


def _ffn_fwd(pfx, h, gamma, wg, wu, wd, plan=None):
    t, d = h.shape
    nb, f, _ = wg.shape
    tm = _pick(t, (1024, 512, 256))
    nt = t // tm
    host = _Hosted(plan, 5, 4)

    def body(*refs):
        (h_ref, gam_ref, wg_ref, wu_ref, wd_ref), (ho_ref, xn_ref, g_ref, u_ref), (xn_scr, acc_ref), prefs = host.split(refs)
        i, j = pl.program_id(0), pl.program_id(1)
        host.run(0, (i == 0) & (j == 0), prefs)
        host.run(1, (i == nt // 2) & (j == 0), prefs)

        @pl.when(j == 0)
        def _():
            xn = _rms_fwd_val(h_ref[...], gam_ref[...]).astype(BF16)
            xn_scr[...] = xn
            xn_ref[...] = xn
            acc_ref[...] = jnp.zeros_like(acc_ref)

        for rows in _row_chains(tm):
            x = xn_scr[rows, :]
            g = _dot(x, wg_ref[...], 1, 1)
            u = _dot(x, wu_ref[...], 1, 1)
            g_ref[rows, :] = g.astype(BF16)
            u_ref[rows, :] = u.astype(BF16)
            acc_ref[rows, :] += _dot(g * _sigmoid(g) * u, wd_ref[...], 1, 0)

        @pl.when(j == nb - 1)
        def _():
            ho_ref[...] = h_ref[...] + 0.5 * acc_ref[...]

        host.run(2, (i == nt - 1) & (j == nb - 1), prefs)

    row = pl.BlockSpec((tm, d), lambda i, j: (i, 0))
    blk = pl.BlockSpec((None, tm, f), lambda i, j: (j, i, 0))
    wspec = pl.BlockSpec((None, f, d), lambda i, j: (j, 0, 0))
    hc = host.call_args()
    res = pl.pallas_call(
        body, name=pfx + "_fwd", grid=(nt, nb),
        in_specs=[row, pl.BlockSpec((1, d), lambda i, j: (0, 0)), wspec, wspec, wspec] + hc["in_specs"],
        out_specs=[row, row, blk, blk] + hc["out_specs"],
        out_shape=[jax.ShapeDtypeStruct((t, d), F32), jax.ShapeDtypeStruct((t, d), BF16),
                   jax.ShapeDtypeStruct((nb, t, f), BF16), jax.ShapeDtypeStruct((nb, t, f), BF16)] + hc["out_shape"],
        scratch_shapes=[pltpu.VMEM((tm, d), BF16), pltpu.VMEM((tm, d), F32)] + hc["scratch"],
        input_output_aliases=hc["aliases"], compiler_params=_cparams(("arbitrary", "arbitrary")),
    )(h, gamma, wg, wu, wd, *hc["args"])
    return res[:4], res[4:]


def _ffn_bwd(pfx, dh_out, h, gamma, xn, g_all, u_all, wg, wu, wd, plan=None):
    t, d = h.shape
    nb, f, _ = wg.shape
    tm = _pick(t, (512, 256))
    tk = _pick(t, (2048, 1024, 512, 256))

    nt = t // tm
    host = _Hosted(plan, 8, 5)

    def body(*refs):
        ((dy_ref, h_ref, gam_ref, wg_ref, wu_ref, wd_ref, g_ref, u_ref), (dh_ref, dgam_ref, dg_ref, du_ref, a_ref),
         (acc_ref,), prefs) = host.split(refs)
        i, j = pl.program_id(0), pl.program_id(1)
        host.run(0, (i == 0) & (j == 0), prefs)
        host.run(1, (i == nt // 2) & (j == 0), prefs)

        @pl.when((i == 0) & (j == 0))
        def _():
            dgam_ref[...] = jnp.zeros_like(dgam_ref)

        @pl.when(j == 0)
        def _():
            acc_ref[...] = jnp.zeros_like(acc_ref)

        for rows in _row_chains(tm):
            da = _dot(dy_ref[rows, :], wd_ref[...], 1, 1) * 0.5
            g = g_ref[rows, :].astype(F32)
            u = u_ref[rows, :].astype(F32)
            s = _sigmoid(g)
            sl = g * s
            du = (da * sl).astype(BF16)
            dg = (da * u * (s + sl * (1.0 - s))).astype(BF16)
            du_ref[rows, :] = du
            dg_ref[rows, :] = dg
            a_ref[rows, :] = (sl * u).astype(BF16)
            acc_ref[rows, :] += _dot(dg, wg_ref[...], 1, 0) + _dot(du, wu_ref[...], 1, 0)

        @pl.when(j == nb - 1)
        def _():
            dx, dgam = _rms_bwd_val(acc_ref[...], h_ref[...], gam_ref[...])
            dh_ref[...] = dy_ref[...] + dx
            dgam_ref[...] += dgam

        host.run(2, (i == nt - 1) & (j == nb - 1), prefs)

    row = pl.BlockSpec((tm, d), lambda i, j: (i, 0))
    vec = pl.BlockSpec((1, d), lambda i, j: (0, 0))
    blk = pl.BlockSpec((None, tm, f), lambda i, j: (j, i, 0))
    wspec = pl.BlockSpec((None, f, d), lambda i, j: (j, 0, 0))
    hc = host.call_args()
    res = pl.pallas_call(
        body, name=pfx + "_bwd", grid=(nt, nb),
        in_specs=[row, row, vec, wspec, wspec, wspec, blk, blk] + hc["in_specs"],
        out_specs=[row, vec, blk, blk, blk] + hc["out_specs"],
        out_shape=[jax.ShapeDtypeStruct((t, d), F32), jax.ShapeDtypeStruct((1, d), F32)]
        + [jax.ShapeDtypeStruct((nb, t, f), BF16)] * 3 + hc["out_shape"],
        scratch_shapes=[pltpu.VMEM((tm, d), F32)] + hc["scratch"], input_output_aliases=hc["aliases"],
        compiler_params=_cparams(("arbitrary", "arbitrary")),
    )(dh_out, h, gamma, wg, wu, wd, g_all, u_all, *hc["args"])
    dh, dgamma, dg_all, du_all, a_all = res[:5]

    xmap, bmap, omap = (lambda b, k: (k, 0)), (lambda b, k: (b, k, 0)), (lambda b, k: (b, 0, 0))
    dwg, tg = _mm(pfx + "_dwg", [(dg_all, (None, tk, f), bmap, xn, (tk, d), xmap)], (nb, f, d), (None, f, d), omap,
                  (nb, t // tk), 1, ta=True, twin=True)
    dwu, tu = _mm(pfx + "_dwu", [(du_all, (None, tk, f), bmap, xn, (tk, d), xmap)], (nb, f, d), (None, f, d), omap,
                  (nb, t // tk), 1, ta=True, twin=True)
    dwd, td = _mm(pfx + "_dwd", [(a_all, (None, tk, f), bmap, dh_out, (tk, d), xmap)], (nb, f, d), (None, f, d), omap,
                  (nb, t // tk), 1, ta=True, scale=0.5, twin=True)
    return (dh, dgamma, dwg, dwu, dwd), res[5:], (tg, tu, td)


def _ffn_bwd_late_dx(pfx, dh_out, h, gamma, xn, g_all, u_all, wg, wu, wd, plan_gu, plans_dw, make_plan_dx):
    t, d = h.shape
    nb, f, _ = wg.shape
    tm = _pick(t, (512, 256))
    tk = _pick(t, (2048, 1024, 512, 256))
    nt = t // tm
    host_a = _Hosted(plan_gu, 4, 3)

    def body_a(*refs):
        (dy_ref, wd_ref, g_ref, u_ref), (dg_ref, du_ref, a_ref), _, prefs = host_a.split(refs)
        i, j = pl.program_id(0), pl.program_id(1)
        host_a.run(0, (i == 0) & (j == 0), prefs)
        host_a.run(1, (i == 0) & (j == 0), prefs)
        for rows in _row_chains(tm):
            da = _dot(dy_ref[rows, :], wd_ref[...], 1, 1) * 0.5
            g = g_ref[rows, :].astype(F32)
            u = u_ref[rows, :].astype(F32)
            s = _sigmoid(g)
            sl = g * s
            du_ref[rows, :] = (da * sl).astype(BF16)
            dg_ref[rows, :] = (da * u * (s + sl * (1.0 - s))).astype(BF16)
            a_ref[rows, :] = (sl * u).astype(BF16)
        host_a.run(2, (i == nt - 1) & (j == nb - 1), prefs)

    row = pl.BlockSpec((tm, d), lambda i, j: (i, 0))
    vec = pl.BlockSpec((1, d), lambda i, j: (0, 0))
    blk = pl.BlockSpec((None, tm, f), lambda i, j: (j, i, 0))
    wspec = pl.BlockSpec((None, f, d), lambda i, j: (j, 0, 0))
    hc = host_a.call_args()
    res_a = pl.pallas_call(
        body_a, name=pfx + "_bwd_gu", grid=(nt, nb), in_specs=[row, wspec, blk, blk] + hc["in_specs"],
        out_specs=[blk] * 3 + hc["out_specs"], out_shape=[jax.ShapeDtypeStruct((nb, t, f), BF16)] * 3 + hc["out_shape"],
        scratch_shapes=hc["scratch"], input_output_aliases=hc["aliases"], compiler_params=_cparams(("arbitrary", "arbitrary")),
    )(dh_out, wd, g_all, u_all, *hc["args"])
    dg_all, du_all, a_all = res_a[:3]

    xmap, bmap, omap = (lambda b, k: (k, 0)), (lambda b, k: (b, k, 0)), (lambda b, k: (b, 0, 0))
    (dwd, td), out_d = _mm(pfx + "_dwd", [(a_all, (None, tk, f), bmap, dh_out, (tk, d), xmap)], (nb, f, d), (None, f, d),
                           omap, (nb, t // tk), 1, ta=True, scale=0.5, plan=plans_dw[0], twin=True)
    (dwg, tg), out_g = _mm(pfx + "_dwg", [(dg_all, (None, tk, f), bmap, xn, (tk, d), xmap)], (nb, f, d), (None, f, d),
                           omap, (nb, t // tk), 1, ta=True, plan=plans_dw[1], twin=True)
    (dwu, tu), out_u = _mm(pfx + "_dwu", [(du_all, (None, tk, f), bmap, xn, (tk, d), xmap)], (nb, f, d), (None, f, d),
                           omap, (nb, t // tk), 1, ta=True, plan=plans_dw[2], twin=True)

    plan_dx = make_plan_dx((dwg, dwu, dwd), (tg, tu, td))
    host_b = _Hosted(plan_dx, 7, 2)

    def body_b(*refs):
        (dy_ref, h_ref, gam_ref, wg_ref, wu_ref, dg_ref, du_ref), (dh_ref, dgam_ref), (acc_ref,), prefs = host_b.split(refs)
        i, j = pl.program_id(0), pl.program_id(1)
        host_b.run(0, (i == 0) & (j == 0), prefs)
        host_b.run(1, (i == 0) & (j == 0), prefs)

        @pl.when((i == 0) & (j == 0))
        def _():
            dgam_ref[...] = jnp.zeros_like(dgam_ref)

        @pl.when(j == 0)
        def _():
            acc_ref[...] = jnp.zeros_like(acc_ref)

        acc_ref[...] += _dot(dg_ref[...], wg_ref[...], 1, 0) + _dot(du_ref[...], wu_ref[...], 1, 0)

        @pl.when(j == nb - 1)
        def _():
            dx, dgam = _rms_bwd_val(acc_ref[...], h_ref[...], gam_ref[...])
            dh_ref[...] = dy_ref[...] + dx
            dgam_ref[...] += dgam

        host_b.run(2, (i == nt - 1) & (j == nb - 1), prefs)

    hc = host_b.call_args()
    res_b = pl.pallas_call(
        body_b, name=pfx + "_bwd_dx", grid=(nt, nb), in_specs=[row, row, vec, wspec, wspec, blk, blk] + hc["in_specs"],
        out_specs=[row, vec] + hc["out_specs"],
        out_shape=[jax.ShapeDtypeStruct((t, d), F32), jax.ShapeDtypeStruct((1, d), F32)] + hc["out_shape"],
        scratch_shapes=[pltpu.VMEM((tm, d), F32)] + hc["scratch"], input_output_aliases=hc["aliases"],
        compiler_params=_cparams(("arbitrary", "arbitrary")),
    )(dh_out, h, gamma, wg, wu, dg_all, du_all, *hc["args"])
    return (res_b[0], res_b[1], dwg, dwu, dwd), (res_a[3:], out_d, out_g, out_u, res_b[2:])


HALO = 16


def _silu_grad(y):
    s = _sigmoid(y)
    return s * (1.0 + y * (1.0 - s))


def _with_halo(ref, i, n_tiles, tm, before, after):
    t = ref.shape[0]
    r0 = pl.multiple_of(i * tm, tm)
    parts = [ref[pl.ds(r0, tm), :].astype(F32)]
    if before:
        prev = ref[pl.ds(pl.multiple_of(jnp.maximum(r0 - HALO, 0), HALO), HALO), :].astype(F32)
        parts.insert(0, jnp.where(i > 0, prev, 0.0))
    if after:
        nxt = ref[pl.ds(pl.multiple_of(jnp.minimum(r0 + tm, t - HALO), HALO), HALO), :].astype(F32)
        parts.append(jnp.where(i < n_tiles - 1, nxt, 0.0))
    return jnp.concatenate(parts, axis=0)


def _conv_fwd(zbig, w):
    t, c = zbig.shape[0], w.shape[1]
    tm = _pick(t, (512, 256))
    nt = t // tm

    def body(x_ref, w_ref, o_ref):
        xe = _with_halo(x_ref, pl.program_id(0), nt, tm, True, False)
        wv = w_ref[...]
        y = xe * wv[3:4, :]
        for i in range(CONV_W - 1):
            y = y + pltpu.roll(xe, CONV_W - 1 - i, 0) * wv[i:i + 1, :]
        y = y[HALO:, :]
        o_ref[...] = (y * _sigmoid(y)).astype(o_ref.dtype)

    return pl.pallas_call(
        body, name="conv_fwd", grid=(nt,),
        in_specs=[pl.BlockSpec((t, c), lambda i: (0, 0)), pl.BlockSpec(w.shape, lambda i: (0, 0))],
        out_specs=pl.BlockSpec((tm, c), lambda i: (i, 0)), out_shape=jax.ShapeDtypeStruct((t, c), BF16),
        compiler_params=_cparams(("parallel",)),
    )(zbig, w)


def _conv_bwd(zbig, dact, w):
    t, c = dact.shape
    tm = _pick(t, (512, 256))
    nt = t // tm
    n = tm + HALO

    def body(x_ref, d_ref, w_ref, dx_ref, dw_ref):
        xe = _with_halo(x_ref, pl.program_id(0), nt, tm, True, True)
        de = _with_halo(d_ref, pl.program_id(0), nt, tm, False, True)
        wv = w_ref[...]
        sh = [pltpu.roll(xe, CONV_W - 1 - i, 0)[HALO:, :] if i < CONV_W - 1 else xe[HALO:, :] for i in range(CONV_W)]
        y = sh[0] * wv[0:1, :]
        for i in range(1, CONV_W):
            y = y + sh[i] * wv[i:i + 1, :]
        dy = de * _silu_grad(y)
        dx = dy * wv[3:4, :]
        for i in range(CONV_W - 1):
            dx = dx + pltpu.roll(dy, n - (CONV_W - 1 - i), 0) * wv[i:i + 1, :]
        dx_ref[...] = dx[:tm, :].astype(dx_ref.dtype)
        dyc = dy[:tm, :]
        dwp = jnp.concatenate([_colsum(dyc * sh[i][:tm, :]) for i in range(CONV_W)], axis=0)

        @pl.when(pl.program_id(0) == 0)
        def _():
            dw_ref[...] = jnp.zeros_like(dw_ref)
        dw_ref[...] += dwp

    return pl.pallas_call(
        body, name="conv_bwd", grid=(nt,),
        in_specs=[pl.BlockSpec((t, c), lambda i: (0, 0)), pl.BlockSpec((t, c), lambda i: (0, 0)),
                  pl.BlockSpec(w.shape, lambda i: (0, 0))],
        out_specs=[pl.BlockSpec((tm, c), lambda i: (i, 0)), pl.BlockSpec(w.shape, lambda i: (0, 0))],
        out_shape=[jax.ShapeDtypeStruct((t, c), BF16), jax.ShapeDtypeStruct(w.shape, F32)],
        compiler_params=_cparams(("arbitrary",)),
    )(zbig, dact, w)


LM = 256
HI = lax.Precision.HIGHEST


def _logsig(x):
    return jnp.minimum(x, 0.0) - jnp.log(1.0 + jnp.exp(-jnp.abs(x)))


def _tri(n, lower):
    r = lax.broadcasted_iota(jnp.int32, (n, n), 0)
    c = lax.broadcasted_iota(jnp.int32, (n, n), 1)
    return (r >= c) if lower else (r <= c)


def _f32dot(a, b):
    return lax.dot_general(a, b, (((1,), (0,)), ((), ())), precision=HI, preferred_element_type=F32)


def _tri_dot(a, b, a_is_tri):
    tri = (a if a_is_tri else b).astype(BF16)
    parts = _split3(b if a_is_tri else a)
    outs = [_dot(tri, p, 1, 0) if a_is_tri else _dot(p, tri, 1, 0) for p in parts]
    return (outs[0] + outs[1]) + outs[2]


def _mlstm_decays(zs_ref, zsr_ref, bc_ref, br_ref):
    l = LM
    lf_c = _logsig(zs_ref[:, 0:2 * NH_M] + bc_ref[...])
    lf_r = _logsig(zsr_ref[...] + br_ref[...])
    return _tri_dot(_tri(l, True), lf_c, True), _tri_dot(lf_r, _tri(l, False), False)


def _mlstm_chunk(h, q_ref, k_ref, v_ref, zs_ref, zsr_ref, bc_ref, br_ref, c_prev, m_prev, decays):
    l = LM
    q = q_ref[:, h * DK_M:(h + 1) * DK_M].astype(F32) * (DK_M ** -0.5)
    k = k_ref[:, h * DK_M:(h + 1) * DK_M]
    v = v_ref[:, h * DV_M:(h + 1) * DV_M]
    lane = lax.broadcasted_iota(jnp.int32, (l, DV_M), 1)
    v1 = jnp.concatenate([v, (lane == 0).astype(v.dtype)], axis=1)
    zs, zsr = zs_ref[...], zsr_ref[...]
    li_c = zs[:, h:h + 1] + bc_ref[:, h:h + 1]
    fp_c = zs[:, NH_M + h:NH_M + h + 1] + bc_ref[:, NH_M + h:NH_M + h + 1]
    li_r = zsr[h:h + 1, :] + br_ref[h:h + 1, :]
    fp_r = zsr[NH_M + h:NH_M + h + 1, :] + br_ref[NH_M + h:NH_M + h + 1, :]
    low = _tri(l, True)
    b_c = decays[0][:, NH_M + h:NH_M + h + 1]
    b_r = decays[1][NH_M + h:NH_M + h + 1, :]
    g = b_r[:, l - 1:l]
    dmat = jnp.where(low, b_c - b_r + li_r, -jnp.inf)
    inter = b_c + m_prev
    m_t = jnp.maximum(inter, jnp.max(dmat, axis=1, keepdims=True))
    w_inter = jnp.exp(inter - m_t)
    amat = jnp.exp(dmat - m_t)
    s = _dot(q, k, 1, 1)
    p = amat * s
    qc = _dot(q, c_prev, 1, 0)
    qc_w = w_inter * qc
    num1 = qc_w + _dot(p, v1, 1, 0)
    den = num1[:, DV_M:DV_M + 1]
    mx = jnp.maximum(jnp.abs(den), jnp.exp(-m_t))
    hh = num1[:, :DV_M] / mx
    a_c = g - b_c + li_c
    return dict(q=q, k=k, v1=v1, fp_c=fp_c, fp_r=fp_r, b_c=b_c, g=g, m_t=m_t, w_inter=w_inter, amat=amat, s=s, p=p,
                qc_w=qc_w, den=den, mx=mx, hh=hh, a_c=a_c)


def _mlstm_fwd(qk, zbig, zs, zsr, bc, br, gm):
    t = zs.shape[0]
    l = LM
    nc = t // l
    dm = NH_M * DV_M

    def body(q_ref, k_ref, v_ref, o_ref, zs_ref, zsr_ref, bc_ref, br_ref, gm_ref, y_ref, cst_ref, mst_ref, c_scr, m_scr):
        @pl.when(pl.program_id(0) == 0)
        def _():
            c_scr[...] = jnp.zeros_like(c_scr)
            m_scr[...] = jnp.zeros_like(m_scr)

        cst_ref[...] = c_scr[...]
        mst_ref[...] = m_scr[...]
        ys = []
        decays = _mlstm_decays(zs_ref, zsr_ref, bc_ref, br_ref)
        for h in range(NH_M):
            c_prev = c_scr[h]
            m_prev = m_scr[h:h + 1, 0:1]
            r = _mlstm_chunk(h, q_ref, k_ref, v_ref, zs_ref, zsr_ref, bc_ref, br_ref, c_prev, m_prev, decays)
            hh = r["hh"]
            gh = gm_ref[:, h * DV_M:(h + 1) * DV_M]
            hn = hh * lax.rsqrt(jnp.mean(hh * hh, axis=-1, keepdims=True) + EPS) * gh
            og = o_ref[:, h * DV_M:(h + 1) * DV_M].astype(F32)
            ys.append(hn * _sigmoid(og))
            m_new = jnp.maximum(r["g"] + m_prev, jnp.max(r["a_c"], axis=0, keepdims=True))
            decay = jnp.exp(r["g"] + m_prev - m_new)
            wk = r["k"].astype(F32) * jnp.exp(r["a_c"] - m_new)
            c_scr[h] = decay * c_prev + _dot(wk, r["v1"], 0, 0)
            m_scr[h:h + 1, :] = jnp.broadcast_to(m_new, (1, 128))
        y_ref[...] = jnp.concatenate(ys, axis=1).astype(y_ref.dtype)

    return pl.pallas_call(
        body, name="mlstm_fwd", grid=(nc,),
        in_specs=[pl.BlockSpec((l, NH_M * DK_M), lambda i: (i, 0)), pl.BlockSpec((l, NH_M * DK_M), lambda i: (i, 1)),
                  pl.BlockSpec((l, dm), lambda i: (i, 1)), pl.BlockSpec((l, dm), lambda i: (i, 2)),
                  pl.BlockSpec((l, 128), lambda i: (i, 0)), pl.BlockSpec((8, l), lambda i: (0, i)),
                  pl.BlockSpec((1, 8), lambda i: (0, 0)), pl.BlockSpec((8, 1), lambda i: (0, 0)),
                  pl.BlockSpec((1, dm), lambda i: (0, 0))],
        out_specs=[pl.BlockSpec((l, dm), lambda i: (i, 0)), pl.BlockSpec((None, NH_M, DK_M, 2 * DV_M), lambda i: (i, 0, 0, 0)),
                   pl.BlockSpec((None, 8, 128), lambda i: (i, 0, 0))],
        out_shape=[jax.ShapeDtypeStruct((t, dm), BF16), jax.ShapeDtypeStruct((nc, NH_M, DK_M, 2 * DV_M), F32),
                   jax.ShapeDtypeStruct((nc, 8, 128), F32)],
        scratch_shapes=[pltpu.VMEM((NH_M, DK_M, 2 * DV_M), F32), pltpu.VMEM((8, 128), F32)],
        compiler_params=_cparams(("arbitrary",)),
    )(qk, qk, zbig, zbig, zs, zsr, bc, br, gm)


def _mlstm_bwd(qk, zbig, zs, zsr, bc, br, gm, cst, mst, dycat):
    t = zs.shape[0]
    l = LM
    nc = t // l
    dm = NH_M * DV_M

    def body(q_ref, k_ref, v_ref, o_ref, zs_ref, zsr_ref, bc_ref, br_ref, gm_ref, cst_ref, mst_ref, cnx_ref, mnx_ref,
             dy_ref, dqk_ref, dv_ref, do_ref, dzs_ref, dzr_ref, dgm_ref, dc_scr):
        @pl.when(pl.program_id(0) == 0)
        def _():
            dc_scr[...] = jnp.zeros_like(dc_scr)
            dgm_ref[...] = jnp.zeros_like(dgm_ref)

        lane = lax.broadcasted_iota(jnp.int32, (l, 128), 1)
        upper, lower = _tri(l, False), _tri(l, True)
        db_all, sig_c, carries = jnp.zeros((l, 128), F32), jnp.zeros((l, 128), F32), jnp.zeros((1, 128), F32)
        decays = _mlstm_decays(zs_ref, zsr_ref, bc_ref, br_ref)
        dzr_rows = [None] * 8
        dvs, dos, dgs, dqs, dks = [], [], [], [], []
        dzs = jnp.zeros((l, 128), F32)
        for h in range(NH_M):
            c_prev = cst_ref[h]
            m_prev = mst_ref[h:h + 1, 0:1]
            r = _mlstm_chunk(h, q_ref, k_ref, v_ref, zs_ref, zsr_ref, bc_ref, br_ref, c_prev, m_prev, decays)
            hh, mx, den, m_t, v1, amat = r["hh"], r["mx"], r["den"], r["m_t"], r["v1"], r["amat"]
            gh = gm_ref[:, h * DV_M:(h + 1) * DV_M]
            rs = lax.rsqrt(jnp.mean(hh * hh, axis=-1, keepdims=True) + EPS)
            xh = hh * rs
            sg = _sigmoid(o_ref[:, h * DV_M:(h + 1) * DV_M].astype(F32))
            dyh = dy_ref[:, h * DV_M:(h + 1) * DV_M]
            dos.append(dyh * xh * gh * sg * (1.0 - sg))
            dhn = dyh * sg
            dgs.append(_colsum(dhn * xh))
            dxh = dhn * gh
            dh = rs * (dxh - xh * jnp.mean(dxh * xh, axis=-1, keepdims=True))
            g1 = dh / mx
            hd = jnp.sum(hh * dh, axis=-1, keepdims=True)
            dden = jnp.where(jnp.abs(den) > jnp.exp(-m_t), -hd / mx * jnp.sign(den), 0.0)
            g256 = jnp.concatenate([g1, jnp.where(lane == 0, dden, 0.0)], axis=1)
            dc_h = dc_scr[h]
            ea = jnp.exp(r["a_c"])
            dp = _dot(g256, v1, 1, 1)
            ds = dp * amat
            dqs.append((r["w_inter"] * _dot(g256, c_prev, 1, 1) + _dot(ds, r["k"], 1, 0)) * (DK_M ** -0.5))
            dks.append(_dot(ds, r["q"], 0, 0) + ea * _dot(v1, dc_h, 1, 1))
            dv_st = ea * _dot(r["k"], dc_h, 1, 0)
            dv1 = _dot(r["p"], g256, 0, 0) + dv_st
            dvs.append(dv1[:, :DV_M])
            wmat = dp * r["p"]
            c_in = _colsum(wmat)
            c_st = jnp.sum(v1.astype(F32) * dv_st, axis=-1, keepdims=True)
            r_t = jnp.sum(wmat, axis=1, keepdims=True) + jnp.sum(g256 * r["qc_w"], axis=-1, keepdims=True)
            db = r_t - c_st
            carry = jnp.exp(mnx_ref[h:h + 1, 0:1]) * jnp.sum(
                jnp.sum(dc_h * cnx_ref[h], axis=1, keepdims=True), axis=0, keepdims=True)
            db_all = db_all + jnp.where(lane == NH_M + h, db, 0.0)
            sig_c = sig_c + jnp.where(lane == NH_M + h, _sigmoid(-r["fp_c"]), 0.0)
            carries = carries + jnp.where(lane[0:1, :] == NH_M + h, carry, 0.0)
            dzs = dzs + jnp.where(lane == h, c_st, 0.0)
            dzr_rows[h] = c_in
            dzr_rows[NH_M + h] = _sigmoid(-r["fp_r"])
            wq = r["q"] * jnp.exp(r["b_c"] - m_t)
            dc_scr[h] = jnp.exp(r["g"]) * dc_h + _dot(wq, g256, 0, 0)
        dzs = dzs + (_tri_dot(upper, db_all, True) + carries) * sig_c
        c_in4 = jnp.concatenate(dzr_rows[:NH_M], axis=0)
        dlf_r4 = -_tri_dot(c_in4, lower, False)
        dzr_rows = dzr_rows[:NH_M] + [dlf_r4[h:h + 1, :] * dzr_rows[NH_M + h] for h in range(NH_M)]
        dqk_ref[...] = jnp.concatenate(dqs + dks, axis=1)
        dv_ref[...] = jnp.concatenate(dvs, axis=1).astype(dv_ref.dtype)
        do_ref[...] = jnp.concatenate(dos, axis=1).astype(do_ref.dtype)
        dzs_ref[...] = dzs
        dzr_ref[...] = jnp.concatenate(dzr_rows, axis=0)
        dgm_ref[...] += jnp.concatenate(dgs, axis=1)

    rev = lambda i: nc - 1 - i
    nxt = lambda i: jnp.minimum(nc - i, nc - 1)
    return pl.pallas_call(
        body, name="mlstm_bwd", grid=(nc,),
        in_specs=[pl.BlockSpec((l, NH_M * DK_M), lambda i: (rev(i), 0)), pl.BlockSpec((l, NH_M * DK_M), lambda i: (rev(i), 1)),
                  pl.BlockSpec((l, dm), lambda i: (rev(i), 1)), pl.BlockSpec((l, dm), lambda i: (rev(i), 2)),
                  pl.BlockSpec((l, 128), lambda i: (rev(i), 0)), pl.BlockSpec((8, l), lambda i: (0, rev(i))),
                  pl.BlockSpec((1, 8), lambda i: (0, 0)), pl.BlockSpec((8, 1), lambda i: (0, 0)),
                  pl.BlockSpec((1, dm), lambda i: (0, 0)),
                  pl.BlockSpec((None, NH_M, DK_M, 2 * DV_M), lambda i: (rev(i), 0, 0, 0)),
                  pl.BlockSpec((None, 8, 128), lambda i: (rev(i), 0, 0)),
                  pl.BlockSpec((None, NH_M, DK_M, 2 * DV_M), lambda i: (nxt(i), 0, 0, 0)),
                  pl.BlockSpec((None, 8, 128), lambda i: (nxt(i), 0, 0)),
                  pl.BlockSpec((l, dm), lambda i: (rev(i), 0))],
        out_specs=[pl.BlockSpec((l, dm), lambda i: (rev(i), 0)),
                   pl.BlockSpec((l, dm), lambda i: (rev(i), 0)), pl.BlockSpec((l, dm), lambda i: (rev(i), 0)),
                   pl.BlockSpec((l, 128), lambda i: (rev(i), 0)), pl.BlockSpec((8, l), lambda i: (0, rev(i))),
                   pl.BlockSpec((1, dm), lambda i: (0, 0))],
        out_shape=[jax.ShapeDtypeStruct((t, dm), F32),
                   jax.ShapeDtypeStruct((t, dm), BF16), jax.ShapeDtypeStruct((t, dm), BF16),
                   jax.ShapeDtypeStruct((t, 128), F32), jax.ShapeDtypeStruct((8, t), F32),
                   jax.ShapeDtypeStruct((1, dm), F32)],
        scratch_shapes=[pltpu.VMEM((NH_M, DK_M, 2 * DV_M), F32)],
        compiler_params=_cparams(("arbitrary",)),
    )(qk, qk, zbig, zbig, zs, zsr, bc, br, gm, cst, mst, cst, mst, dycat)


def _fox_cumsum(zsr, bf_r):
    t = zsr.shape[1]
    cw = _pick(t, (512, 256))

    def body(z_ref, b_ref, c_ref):
        up = _tri(cw, False).astype(F32)
        carry = jnp.zeros((NH_F, 1), F32)
        for j in range(t // cw):
            cs = _f32dot(_logsig(z_ref[:, j * cw:(j + 1) * cw] + b_ref[...]), up) + carry
            c_ref[:, j * cw:(j + 1) * cw] = cs
            carry = cs[:, cw - 1:cw]

    return pl.pallas_call(
        body, name="fox_cumsum", grid=(1,),
        in_specs=[pl.BlockSpec((NH_F, t), lambda i: (1, 0)), pl.BlockSpec((NH_F, 1), lambda i: (0, 0))],
        out_specs=pl.BlockSpec((NH_F, t), lambda i: (0, 0)), out_shape=jax.ShapeDtypeStruct((NH_F, t), F32),
        compiler_params=_cparams(("arbitrary",)),
    )(zsr, bf_r)


def _fox_gate_bwd(zsr, bf_r, dc):
    t = zsr.shape[1]
    cw = _pick(t, (512, 256))

    def body(z_ref, b_ref, dc_ref, o_ref):
        low = _tri(cw, True).astype(F32)
        carry = jnp.zeros((NH_F, 1), F32)
        for j in reversed(range(t // cw)):
            sl = slice(j * cw, (j + 1) * cw)
            dlf = _f32dot(dc_ref[:, sl], low) + carry
            o_ref[:, sl] = dlf * _sigmoid(-(z_ref[:, sl] + b_ref[...]))
            carry = dlf[:, 0:1]

    return pl.pallas_call(
        body, name="fox_gate_bwd", grid=(1,),
        in_specs=[pl.BlockSpec((NH_F, t), lambda i: (1, 0)), pl.BlockSpec((NH_F, 1), lambda i: (0, 0)),
                  pl.BlockSpec((NH_F, t), lambda i: (0, 0))],
        out_specs=pl.BlockSpec((NH_F, t), lambda i: (0, 0)), out_shape=jax.ShapeDtypeStruct((NH_F, t), F32),
        compiler_params=_cparams(("arbitrary",)),
    )(zsr, bf_r, dc)


def _causal_mask(n):
    return _tri(n, True)


AUG = 64


def _split3(c):
    hi = c.astype(BF16).astype(F32)
    r1 = c - hi
    mid = r1.astype(BF16).astype(F32)
    return hi, mid, r1 - mid


def _fox_prep(zbig, ct):
    t = zbig.shape[0]
    tm = _pick(t, (512, 256))

    def body(q_ref, k_ref, v_ref, c_ref, qo_ref, ko_ref, vo_ref):
        lane = lax.broadcasted_iota(jnp.int32, (tm, AUG), 1)
        qv, kv, vv, cv = q_ref[...], k_ref[...], v_ref[...], c_ref[...]
        one = (lane == 0).astype(BF16)
        for h in range(NH_F):
            hi, mid, lo = _split3(cv[:, h:h + 1])
            aq = jnp.where(lane == 0, hi, jnp.where(lane == 1, mid, jnp.where(lane == 2, lo, jnp.where(lane < 6, 1.0, 0.0))))
            ak = jnp.where(lane < 3, 1.0, jnp.where(lane == 3, -hi, jnp.where(lane == 4, -mid, jnp.where(lane == 5, -lo, 0.0))))
            sl = slice(h * DH_F, (h + 1) * DH_F)
            qo_ref[h] = jnp.concatenate([qv[:, sl] * (DH_F ** -0.5), aq.astype(BF16)], axis=1).astype(BF16)
            ko_ref[h] = jnp.concatenate([kv[:, sl], ak.astype(BF16)], axis=1)
            vo_ref[h] = jnp.concatenate([vv[:, sl], one], axis=1)

    ospec = pl.BlockSpec((NH_F, tm, 128), lambda i: (0, i, 0))
    return pl.pallas_call(
        body, name="fox_prep", grid=(t // tm,),
        in_specs=[pl.BlockSpec((tm, 512), lambda i: (i, 3)), pl.BlockSpec((tm, 512), lambda i: (i, 4)),
                  pl.BlockSpec((tm, 512), lambda i: (i, 5)), pl.BlockSpec((tm, NH_F), lambda i: (i, 0))],
        out_specs=[ospec] * 3, out_shape=[jax.ShapeDtypeStruct((NH_F, t, 128), BF16)] * 3,
        compiler_params=_cparams(("parallel",)),
    )(zbig, zbig, zbig, ct)


def _fox_fwd2(qa, ka, va, gf, plan=None):
    nh, t, _ = qa.shape
    tq = _pick(t, (512, 256))
    nq = t // tq
    group = 4
    host = _Hosted(plan, 4, 3)

    def body(*refs):
        (q_ref, k_ref, v_ref, g_ref), (y_ref, o_ref, lse_ref), _, prefs = host.split(refs)
        i = pl.program_id(0)
        host.run(0, i == 0, prefs)
        host.run(1, i == max(nq - 2, 0), prefs)
        lane = lax.broadcasted_iota(jnp.int32, (tq, 128), 1)
        ys, os_ = [], []
        lse_all = jnp.zeros((tq, 128), F32)
        for h0 in range(0, nh, group):
            heads = range(h0, h0 + group)
            qvs = [q_ref[h] for h in heads]

            def blk(j, carry, masked, heads=heads, qvs=qvs):
                k0 = pl.multiple_of(j * tq, tq)
                out = []
                for (m, acc), h, qv in zip(carry, heads, qvs):
                    s = lax.dot_general(qv, k_ref[h, pl.ds(k0, tq), :], (((1,), (1,)), ((), ())), preferred_element_type=F32)
                    if masked:
                        s = jnp.where(_causal_mask(tq), s, -jnp.inf)
                    m_new = jnp.maximum(m, jnp.max(s, axis=1, keepdims=True))
                    p = jnp.exp(s - m_new).astype(BF16)
                    pv = lax.dot_general(p, v_ref[h, pl.ds(k0, tq), :], (((1,), (0,)), ((), ())), preferred_element_type=F32)
                    out.append((m_new, jnp.exp(m - m_new) * acc + pv))
                return tuple(out)

            init = tuple((jnp.full((tq, 1), -jnp.inf, F32), jnp.zeros((tq, 128), F32)) for _ in heads)
            carry = lax.fori_loop(0, i, lambda j, c: blk(j, c, False), init)
            for (m, acc), h in zip(blk(i, carry, True), heads):
                l = acc[:, DH_F:DH_F + 1]
                o = acc[:, :DH_F] / l
                os_.append(o)
                gh = g_ref[:, h * DH_F:(h + 1) * DH_F]
                ys.append(o * lax.rsqrt(jnp.mean(o * o, axis=-1, keepdims=True) + EPS) * gh)
                lse_all = lse_all + jnp.where(lane == h, m + jnp.log(l), 0.0)
        y_ref[...] = jnp.concatenate(ys, axis=1).astype(y_ref.dtype)
        o_ref[...] = jnp.concatenate(os_, axis=1)
        lse_ref[...] = lse_all
        host.run(2, i == nq - 1, prefs)

    full = pl.BlockSpec((nh, t, 128), lambda i: (0, 0, 0))
    hc = host.call_args()
    res = pl.pallas_call(
        body, name="fox_fwd", grid=(nq,),
        in_specs=[pl.BlockSpec((nh, tq, 128), lambda i: (0, i, 0)), full, full, pl.BlockSpec((1, nh * DH_F), lambda i: (0, 0))]
        + hc["in_specs"],
        out_specs=[pl.BlockSpec((tq, nh * DH_F), lambda i: (i, 0)), pl.BlockSpec((tq, nh * DH_F), lambda i: (i, 0)),
                   pl.BlockSpec((tq, 128), lambda i: (i, 0))] + hc["out_specs"],
        out_shape=[jax.ShapeDtypeStruct((t, nh * DH_F), BF16), jax.ShapeDtypeStruct((t, nh * DH_F), F32),
                   jax.ShapeDtypeStruct((t, 128), F32)] + hc["out_shape"],
        scratch_shapes=hc["scratch"], input_output_aliases=hc["aliases"], compiler_params=_cparams(("arbitrary",)),
    )(qa, ka, va, gf, *hc["args"])
    return res[:3], res[3:]


def _fox_bwd_prep(dycat, o, gf):
    t = o.shape[0]
    tm = _pick(t, (512, 256))

    def body(dy_ref, o_ref, g_ref, do_ref, dl_ref, dg_ref):
        lane = lax.broadcasted_iota(jnp.int32, (tm, 128), 1)
        dyv, ov, gv = dy_ref[...], o_ref[...], g_ref[...]
        dgs = []
        dl = jnp.zeros((tm, 128), F32)
        pad = jnp.zeros((tm, AUG), BF16)
        for h in range(NH_F):
            sl = slice(h * DH_F, (h + 1) * DH_F)
            dx, dg = _rms_bwd_val(dyv[:, sl], ov[:, sl], gv[:, sl])
            dgs.append(dg)
            do_ref[h] = jnp.concatenate([dx.astype(BF16), pad], axis=1)
            dl = dl + jnp.where(lane == h, jnp.sum(dx * ov[:, sl], axis=-1, keepdims=True), 0.0)
        dl_ref[...] = dl

        @pl.when(pl.program_id(0) == 0)
        def _():
            dg_ref[...] = jnp.zeros_like(dg_ref)
        dg_ref[...] += jnp.concatenate(dgs, axis=1)

    return pl.pallas_call(
        body, name="fox_bwd_prep", grid=(t // tm,),
        in_specs=[pl.BlockSpec((tm, 512), lambda i: (i, 1)), pl.BlockSpec((tm, 512), lambda i: (i, 0)),
                  pl.BlockSpec((1, 512), lambda i: (0, 0))],
        out_specs=[pl.BlockSpec((NH_F, tm, 128), lambda i: (0, i, 0)), pl.BlockSpec((tm, 128), lambda i: (i, 0)),
                   pl.BlockSpec((1, 512), lambda i: (0, 0))],
        out_shape=[jax.ShapeDtypeStruct((NH_F, t, 128), BF16), jax.ShapeDtypeStruct((t, 128), F32),
                   jax.ShapeDtypeStruct((1, 512), F32)],
        compiler_params=_cparams(("arbitrary",)),
    )(dycat, o, gf)


def _fox_bwd2(qa, ka, va, doa, lse, delta):
    nh, t, _ = qa.shape
    tq = _pick(t, (512, 256))
    nq = t // tq

    group = 2

    def tdot(a, b, cb):
        return lax.dot_general(a, b, (((0,), (cb,)), ((), ())), preferred_element_type=F32)

    def body(q_ref, k_ref, v_ref, do_ref, lse_ref, dl_ref, dq_ref, dk_ref, dv_ref):
        hp, j = pl.program_id(0), pl.program_id(1)

        @pl.when(j == 0)
        def _():
            dq_ref[...] = jnp.zeros_like(dq_ref)

        lane = lax.broadcasted_iota(jnp.int32, (tq, 128), 1)

        def blk(i, carry, masked):
            rows = pl.ds(pl.multiple_of(i * tq, tq), tq)
            lse_t, dl_t = lse_ref[rows, :], dl_ref[rows, :]
            out = []
            for g, (dk, dv) in enumerate(carry):
                h = hp * group + g
                kb, vb = k_ref[g], v_ref[g]
                qb, dob = q_ref[g, rows, :], do_ref[g, rows, :]
                lse_h = jnp.sum(jnp.where(lane == h, lse_t, 0.0), axis=1, keepdims=True)
                dl_h = jnp.sum(jnp.where(lane == h, dl_t, 0.0), axis=1, keepdims=True)
                s = lax.dot_general(qb, kb, (((1,), (1,)), ((), ())), preferred_element_type=F32)
                if masked:
                    s = jnp.where(_causal_mask(tq), s, -jnp.inf)
                p = jnp.exp(s - lse_h)
                dp = lax.dot_general(dob, vb, (((1,), (1,)), ((), ())), preferred_element_type=F32)
                ds = (p * (dp - dl_h)).astype(BF16)
                dv = dv + tdot(dob, p.astype(BF16), 0)
                dk = dk + tdot(qb, ds, 0)
                dq_ref[g, :, rows] += tdot(kb, ds, 1)
                out.append((dk, dv))
            return tuple(out)

        init = tuple((jnp.zeros((128, tq), F32), jnp.zeros((128, tq), F32)) for _ in range(group))
        carry = blk(j, init, True)
        carry = lax.fori_loop(j + 1, nq, lambda i, c: blk(i, c, False), carry)
        for g, (dk, dv) in enumerate(carry):
            dk_ref[g] = dk
            dv_ref[g] = dv

    full = pl.BlockSpec((group, t, 128), lambda h, j: (h, 0, 0))
    tile = pl.BlockSpec((group, tq, 128), lambda h, j: (h, j, 0))
    cols = pl.BlockSpec((t, 128), lambda h, j: (0, 0))
    full_t = pl.BlockSpec((group, 128, t), lambda h, j: (h, 0, 0))
    tile_t = pl.BlockSpec((group, 128, tq), lambda h, j: (h, 0, j))
    return pl.pallas_call(
        body, name="fox_bwd", grid=(nh // group, nq), in_specs=[full, tile, tile, full, cols, cols],
        out_specs=[full_t, tile_t, tile_t], out_shape=[jax.ShapeDtypeStruct((nh, 128, t), F32)] * 3,
        compiler_params=_cparams(("parallel", "arbitrary")),
    )(qa, ka, va, doa, lse, delta)


def _fox_bwd_post(dqa, dka, dva):
    nh, _, t = dqa.shape
    tm = _pick(t, (512, 256))

    def body(dq_ref, dk_ref, dv_ref, oq_ref, ok_ref, ov_ref, dc_ref):
        qs, ks, vs, dcs = [], [], [], []
        for h in range(nh):
            dq, dk = dq_ref[h], dk_ref[h]
            qs.append(dq.T[:, :DH_F] * (DH_F ** -0.5))
            ks.append(dk.T[:, :DH_F])
            vs.append(dv_ref[h].T[:, :DH_F])
            dcs.append(dq[DH_F:DH_F + 1, :] - dk[DH_F + 3:DH_F + 4, :])
        oq_ref[...] = jnp.concatenate(qs, axis=1).astype(BF16)
        ok_ref[...] = jnp.concatenate(ks, axis=1).astype(BF16)
        ov_ref[...] = jnp.concatenate(vs, axis=1).astype(BF16)
        dc_ref[...] = jnp.concatenate(dcs, axis=0)

    ispec = pl.BlockSpec((nh, 128, tm), lambda i: (0, 0, i))
    ospec = pl.BlockSpec((tm, nh * DH_F), lambda i: (i, 0))
    return pl.pallas_call(
        body, name="fox_bwd_post", grid=(t // tm,), in_specs=[ispec] * 3,
        out_specs=[ospec] * 3 + [pl.BlockSpec((nh, tm), lambda i: (0, i))],
        out_shape=[jax.ShapeDtypeStruct((t, nh * DH_F), BF16)] * 3 + [jax.ShapeDtypeStruct((nh, t), F32)],
        compiler_params=_cparams(("parallel",)),
    )(dqa, dka, dva)


IN_OFF = (0, 512, 1024, 1544, 2056, 2568)
IN_GATES = (1536, 3080)


FFN1 = ("ffn1_w_gate", "ffn1_w_up", "ffn1_w_down")
REST = ("w_in", "w_out", "ffn2_w_gate", "ffn2_w_up", "ffn2_w_down", "w_ple_gate", "w_ple_proj")
SPLIT = {n: 1 if n == "w_in" else 0 for n in FFN1 + REST}
SAME_SHAPE = (FFN1, ("ffn2_w_gate", "ffn2_w_up", "ffn2_w_down"), ("w_out", "w_ple_gate"), ("w_in",), ("w_ple_proj",))


def _grouped(names):
    return [tuple(n for n in grp if n in names) for grp in SAME_SHAPE if any(n in names for n in grp)]


def _rs_partials(names, gw, c_idx, twins):
    wire = [twins[n] if n in twins else _cast_other_half("rs_cast_" + n, gw[n], c_idx, SPLIT[n]) for n in names]
    swapped = dict(zip(names, _swap("rs_swap_" + names[0], wire, [SPLIT[n] if n in twins else None for n in names])))
    out = {}
    for grp in _grouped(names):
        res = _add_my_half("rs_add_" + grp[0], [gw[n] for n in grp], [swapped[n] for n in grp], c_idx, SPLIT[grp[0]])
        out.update(zip(grp, res))
    return [out[n] for n in names]


def _local_step(x, p, tgt, sp, wg1, wu1, wd1, rest_slots, c_idx, place):
    t, d = x.shape
    slot = dict(zip(REST + ("conv_qk",), rest_slots))
    (h1, xn1, g1, u1), (w_in, conv_w) = _ffn_fwd(
        "ffn1", x, sp["ffn1_norm"], wg1, wu1, wd1, plan=_gather_plan([slot["w_in"], slot["conv_qk"]], [SPLIT["w_in"], None]))
    w_in, conv_w = w_in.reshape(-1, d), _from_chip_blocks(conv_w)
    w_big = jnp.concatenate([w_in[o:o + 512] for o in IN_OFF], axis=0)
    w_small = jnp.concatenate([w_in[IN_GATES[0]:IN_GATES[0] + 8], w_in[IN_GATES[1]:IN_GATES[1] + 8],
                               jnp.zeros((112, d), w_in.dtype)], axis=0)
    u, zbig = _norm_mm("in_big", h1, sp["mix_norm"], w_big, True, BF16)
    zs = _mm_nt("in_small", u, w_small, tm=1024, tk=1024)
    zsr = zs.T
    qk_act = _conv_fwd(zbig, conv_w)
    bm_c, bf_c = sp["b_mlstm_gates"], sp["b_fox_f"]
    y_m, cst, mst = _mlstm_fwd(qk_act, zbig, zs, zsr, bm_c, bm_c.T, sp["mlstm_out_norm"])
    c = _fox_cumsum(zsr, bf_c.T)
    qa, ka, va = _fox_prep(zbig, c.T)
    (y_ft, o_f, lse), late = _fox_fwd2(qa, ka, va, sp["fox_out_norm"],
                                       plan=_gather_plan([slot[n] for n in REST[1:]], [SPLIT[n] for n in REST[1:]]))
    full = dict(zip(REST[1:], late))
    w_out, w_pg = (full[n].reshape(-1, d) for n in ("w_out", "w_ple_gate"))
    wg2, wu2, wd2 = full["ffn2_w_gate"], full["ffn2_w_up"], full["ffn2_w_down"]
    w_pp = _from_chip_blocks(full["w_ple_proj"])
    tm = _pick(t, (1024, 512, 256))
    h2 = _mm("out_proj", [(y_m, (tm, 512), lambda i, j, k: (i, 0), w_out, (512, d), lambda i, j, k: (0, 0)),
                          (y_ft, (tm, 512), lambda i, j, k: (i, 0), w_out, (512, d), lambda i, j, k: (1, 0))],
             (t, d), (tm, d), lambda i, j, k: (i, 0), (t // tm, 1, 1), 2, res=h1)
    (h3, xn2, g2, u2), _ = _ffn_fwd("ffn2", h2, sp["ffn2_norm"], wg2, wu2, wd2)
    hn3, gate_pre = _norm_mm("ple_gate", h3, sp["ple_gate_norm"], w_pg, False, F32)
    pp = _mm_nn("ple_proj", p, w_pp, tm=1024)

    def head_fn(h3_t, gp_t, pp_t, tgt_t, g_pp, g_fin):
        gate = _sigmoid(gp_t)
        ppn = _rms_fwd_val(pp_t, g_pp)
        h4 = h3_t + gate * ppn
        err = _rms_fwd_val(h4, g_fin) - tgt_t
        loss = 0.5 * jnp.sum(jnp.mean(err * err, axis=-1, keepdims=True), axis=0, keepdims=True)
        dh4, dg_fin = _rms_bwd_val(err * (1.0 / d), h4, g_fin)
        dpp, dg_pp = _rms_bwd_val(dh4 * gate, pp_t, g_pp)
        dgp = dh4 * ppn * gate * (1.0 - gate)
        return dh4, dgp, dpp, jnp.broadcast_to(loss, (1, 128)), dg_fin, dg_pp

    dh4, dgp, dpp, loss_part, dg_fin, dg_pp = _rowwise(
        "loss_head", head_fn, [h3, gate_pre, pp, tgt], [sp["ple_proj_norm"], sp["final_norm"]],
        [(d, F32), (d, BF16), (d, BF16)], [((1, 128), F32), ((1, d), F32), ((1, d), F32)])
    gw, gs = {}, {"final_norm": dg_fin, "ple_proj_norm": dg_pp}
    gw["w_ple_gate"] = _mm_tn("d_w_pg", hn3, dgp, tm=1024, tn=1024)
    gw["w_ple_proj"] = _mm_tn("d_w_pp", p, dpp, tn=1024)
    dhn3 = _mm_nt("d_hn3", dgp, w_pg, tm=1024, tn=1024, tk=1024)

    def res_norm_bwd(dn_t, h_t, dres_t, g):
        dx, dg = _rms_bwd_val(dn_t, h_t, g)
        return dres_t + dx, dg

    dh3, gs["ple_gate_norm"] = _rowwise("ple_norm_bwd", res_norm_bwd, [dhn3, h3, dh4], [sp["ple_gate_norm"]],
                                        [(d, F32)], [((1, d), F32)])
    (dh2, gs["ffn2_norm"], gw["ffn2_w_gate"], gw["ffn2_w_up"], gw["ffn2_w_down"]), _, twins2 = _ffn_bwd(
        "ffn2", dh3, h2, sp["ffn2_norm"], xn2, g2, u2, wg2, wu2, wd2)
    dycat = _mm_nt("d_ycat", dh2, w_out, tm=1024, tn=1024, tk=1024)
    gw["w_out"] = jnp.concatenate([_mm_tn("d_w_out_m", y_m, dh2, tn=1024), _mm_tn("d_w_out_f", y_ft, dh2, tn=1024)], axis=0)
    doa, delta, gs["fox_out_norm"] = _fox_bwd_prep(dycat, o_f, sp["fox_out_norm"])
    dq_f, dk_f, dv_f, dct = _fox_bwd_post(*_fox_bwd2(qa, ka, va, doa, lse, delta))
    dfp = _fox_gate_bwd(zsr, bf_c.T, dct)
    dact, dv_m, do_m, dzs_m, dzr_m, gs["mlstm_out_norm"] = _mlstm_bwd(
        qk_act, zbig, zs, zsr, bm_c, bm_c.T, sp["mlstm_out_norm"], cst, mst, dycat)
    dqk, gw["conv_qk"] = _conv_bwd(zbig, dact, conv_w)
    dz_big = jnp.concatenate([dqk, dv_m, do_m, dq_f, dk_f, dv_f], axis=1)
    dzs = dzs_m + jnp.pad(jnp.concatenate([dzr_m, dfp], axis=0).T, ((0, 0), (0, 112)))
    dw_big = _mm_tn("d_w_big", dz_big, u, tn=1024)
    dw_small = _mm_tn("d_w_small", dzs, u, tn=1024)
    gw["w_in"] = jnp.concatenate([dw_big[0:1536], dw_small[0:8], dw_big[1536:3072], dw_small[8:16]], axis=0)
    du_a = _mm_nn("d_u_big", dz_big, w_big, tm=1024, tn=1024, tk=1024)
    du_b = _mm_nn("d_u_small", dzs, w_small, tm=1024, tn=1024)

    def mix_norm_bwd(da_t, db_t, h_t, dres_t, dzs_t, g):
        dx, dg = _rms_bwd_val(da_t + db_t, h_t, g)
        return dres_t + dx, dg, _colsum(dzs_t)

    dh1, gs["mix_norm"], dbias = _rowwise("mix_norm_bwd", mix_norm_bwd, [du_a, du_b, h1, dh2, dzs], [sp["mix_norm"]],
                                          [(d, F32)], [((1, d), F32), ((1, 128), F32)])
    gs["b_mlstm_gates"], gs["b_fox_f"] = dbias[:, 0:8], dbias[:, 8:16]
    conv_grad = gw.pop("conv_qk")
    gw["w_ple_proj"] = _chip_blocks(gw["w_ple_proj"])
    for n in ("w_in", "w_out", "w_ple_gate"):
        gw[n] = gw[n].reshape(4, -1, gw[n].shape[-1])
    twins = dict(zip(("ffn2_w_gate", "ffn2_w_up", "ffn2_w_down"), twins2))
    part_rest = dict(zip(REST, _rs_partials(REST, gw, c_idx, twins)))
    light = ("w_in", "w_out", "w_ple_gate", "w_ple_proj")
    part_ffn1 = []

    def own_plan(dws, dw_twins):
        part_ffn1.extend(_rs_partials(FFN1, dict(zip(FFN1, dws)), c_idx, dict(zip(FFN1, dw_twins))))
        return _scatter_plan([pb for _, pb in part_ffn1])

    (grad_x, gs["ffn1_norm"], _, _, _), (l_light, l_down, l_gate, l_up, landed_ffn1) = _ffn_bwd_late_dx(
        "ffn1", dh1, x, sp["ffn1_norm"], xn1, g1, u1, wg1, wu1, wd1,
        _scatter_plan([part_rest[n][1] for n in light]),
        [_scatter_plan([part_rest[n][1]]) for n in ("ffn2_w_down", "ffn2_w_gate", "ffn2_w_up")], own_plan)
    landed_rest = dict(zip(light + ("ffn2_w_down", "ffn2_w_gate", "ffn2_w_up"), list(l_light) + [l_down[0], l_gate[0], l_up[0]]))
    names = REST + FFN1
    parts = {**part_rest, **dict(zip(FFN1, part_ffn1))}
    landed = {**landed_rest, **dict(zip(FFN1, landed_ffn1))}
    mine = {}
    for grp in _grouped(names):
        res = _sum4("rs_sum_" + grp[0], [landed[n] for n in grp], [parts[n][0] for n in grp], place, SPLIT[grp[0]])
        mine.update(zip(grp, res))
    grads = dict(zip(names, _join_halves("rs_join", [mine[n] for n in names], [SPLIT[n] for n in names])))
    return loss_part, grad_x, grads, gs, conv_grad


ANY = pl.BlockSpec(memory_space=pl.ANY)
MESH = pl.DeviceIdType.MESH


def _place():
    x, y, c = lax.axis_index("x"), lax.axis_index("y"), lax.axis_index("c")
    chips = [(1 - x, y), (x, 1 - y), (1 - x, 1 - y)]
    return x, y, c, 2 * x + y, (x, y, 1 - c), chips


def _rcopy(src, dst, ssem, rsem, dev):
    return pltpu.make_async_remote_copy(src_ref=src, dst_ref=dst, send_sem=ssem, recv_sem=rsem, device_id=dev,
                                        device_id_type=MESH)


def _half(ref, lead, axis, idx, half):
    return ref.at[(slice(None),) * (lead + axis) + (pl.ds(idx * half, half),)]


def _to_slot(name, arrs, me_idx, dtype):
    n = len(arrs)
    r, cdim = arrs[0].shape
    tr = _pick(r, (352, 256, 176, 128, 64))

    def body(me_ref, *refs):
        for k in range(n):
            refs[n + k][...] = refs[k][...].astype(dtype)

    return pl.pallas_call(
        body, name=name,
        grid_spec=pltpu.PrefetchScalarGridSpec(
            num_scalar_prefetch=1, grid=(r // tr,), in_specs=[pl.BlockSpec((tr, cdim), lambda i, me_ref: (i, 0))] * n,
            out_specs=[pl.BlockSpec((None, tr, cdim), lambda i, me_ref: (me_ref[0], i, 0))] * n),
        out_shape=[jax.ShapeDtypeStruct((4, r, cdim), dtype)] * n, compiler_params=_cparams(("parallel",)),
    )(me_idx, *arrs)


def _gather4(name, bufs, split):
    return _run_plan(name, _gather_plan(bufs, split))


def _gather_plan(bufs, split):
    n = len(bufs)
    shapes = [b.shape[1:] for b in bufs]

    def ctx(outs):
        x, y, c, me, sib, chips = _place()

        def part(ref, a, which):
            if split[a] is None:
                return ref
            return _half(ref, 0, split[a], which, shapes[a][split[a]] // 2)

        return c, me, sib, chips, part

    def ici(outs, sems, a, j, chip, c, me, part):
        mine = part(outs[a].at[me], a, c)
        return _rcopy(mine, mine, sems[0].at[3 * a + j], sems[1].at[3 * a + j], (*chip, c))

    def fwd(outs, sems, a, j, chip, c, sib, part, which):
        blk = part(outs[a].at[2 * chip[0] + chip[1]], a, which)
        return _rcopy(blk, blk, sems[2].at[3 * a + j], sems[3].at[3 * a + j], sib)

    def start(ins, outs, sems):
        c, me, sib, chips, part = ctx(outs)
        for a in range(n):
            for j, chip in enumerate(chips):
                ici(outs, sems, a, j, chip, c, me, part).start()

    def mid(ins, outs, sems):
        c, me, sib, chips, part = ctx(outs)
        for j, chip in enumerate(chips):
            for a in range(n):
                blk = part(outs[a].at[2 * chip[0] + chip[1]], a, c)
                _rcopy(blk, blk, sems[0].at[3 * a + j], sems[1].at[3 * a + j], sib).wait_recv()
                if split[a] is not None:
                    fwd(outs, sems, a, j, chip, c, sib, part, c).start()

    def end(ins, outs, sems):
        c, me, sib, chips, part = ctx(outs)
        for j, chip in enumerate(chips):
            for a in range(n):
                if split[a] is not None:
                    fwd(outs, sems, a, j, chip, c, sib, part, 1 - c).wait_recv()
        for a in range(n):
            for j, chip in enumerate(chips):
                ici(outs, sems, a, j, chip, c, me, part).wait_send()
                if split[a] is not None:
                    fwd(outs, sems, a, j, chip, c, sib, part, c).wait_send()

    return dict(ins=list(bufs), outs=[jax.ShapeDtypeStruct(b.shape, b.dtype) for b in bufs], alias=True,
                sems=[pltpu.SemaphoreType.DMA((3 * n,))] * 4, phases=(start, mid, end))


def _run_plan(name, plan):
    ni, no = len(plan["ins"]), len(plan["outs"])

    def body(*refs):
        ins, outs, sems = refs[:ni], refs[ni:ni + no], refs[ni + no:]
        for phase in plan["phases"]:
            phase(ins, outs, sems)

    return pl.pallas_call(
        body, name=name, in_specs=[ANY] * ni, out_specs=[ANY] * no, out_shape=plan["outs"],
        input_output_aliases={a: a for a in range(ni)} if plan["alias"] else {}, scratch_shapes=plan["sems"],
    )(*plan["ins"])


class _Hosted:
    def __init__(self, plan, n_in, n_out):
        self.plan, self.n_in, self.n_out = plan, n_in, n_out
        self.ni, self.no, self.ns = (len(plan["ins"]) if plan else 0, len(plan["outs"]) if plan else 0,
                                     len(plan["sems"]) if plan else 0)

    def split(self, refs):
        a, b = self.n_in, self.n_in + self.ni
        c, d = b + self.n_out, b + self.n_out + self.no
        e = len(refs) - self.ns
        return refs[:a], refs[b:c], refs[d:e], (refs[a:b], refs[c:d], refs[e:])

    def run(self, k, cond, prefs):
        if self.plan is not None:
            @pl.when(cond)
            def _():
                self.plan["phases"][k](*prefs)

    def call_args(self):
        p = self.plan
        if p is None:
            return dict(in_specs=[], out_specs=[], out_shape=[], scratch=[], aliases={}, args=[])
        al = {self.n_in + a: self.n_out + a for a in range(self.ni)} if p["alias"] else {}
        return dict(in_specs=[ANY] * self.ni, out_specs=[ANY] * self.no, out_shape=list(p["outs"]), scratch=list(p["sems"]),
                    aliases=al, args=list(p["ins"]))


def _swap(name, arrs, halve):
    n = len(arrs)

    def half_shape(a, ax):
        return a.shape if ax is None else (a.shape[0],) + tuple(d // 2 if i == ax else d for i, d in enumerate(a.shape[1:]))

    def body(*refs):
        ins, outs = refs[:n], refs[n:2 * n]
        ssem, rsem = refs[2 * n:]
        x, y, c, me, sib, chips = _place()
        cps = []
        for a in range(n):
            src = ins[a] if halve[a] is None else _half(ins[a], 1, halve[a], 1 - c, arrs[a].shape[1 + halve[a]] // 2)
            cps.append(_rcopy(src, outs[a], ssem.at[a], rsem.at[a], sib))
        for cp in cps:
            cp.start()
        for cp in cps:
            cp.wait()

    return pl.pallas_call(
        body, name=name, in_specs=[ANY] * n, out_specs=[ANY] * n,
        out_shape=[jax.ShapeDtypeStruct(half_shape(a, ax), a.dtype) for a, ax in zip(arrs, halve)],
        scratch_shapes=[pltpu.SemaphoreType.DMA((n,))] * 2,
    )(*arrs)


def _scatter4(name, arrs):
    return _run_plan(name, _scatter_plan(arrs))


def _scatter_plan(arrs):
    n = len(arrs)

    def send(ins, outs, sems, a, j, chip, c, me):
        return _rcopy(ins[a].at[2 * chip[0] + chip[1]], outs[a].at[me], sems[0].at[3 * a + j], sems[1].at[3 * a + j], (*chip, c))

    def start(ins, outs, sems):
        x, y, c, me, sib, chips = _place()
        for a in range(n):
            for j, chip in enumerate(chips):
                send(ins, outs, sems, a, j, chip, c, me).start()

    def mid(ins, outs, sems):
        pass

    def end(ins, outs, sems):
        x, y, c, me, sib, chips = _place()
        for a in range(n):
            for j, chip in enumerate(chips):
                blk = outs[a].at[2 * chip[0] + chip[1]]
                _rcopy(blk, blk, sems[0].at[3 * a + j], sems[1].at[3 * a + j], sib).wait_recv()
        for a in range(n):
            for j, chip in enumerate(chips):
                send(ins, outs, sems, a, j, chip, c, me).wait_send()

    return dict(ins=list(arrs), outs=[jax.ShapeDtypeStruct(a.shape, a.dtype) for a in arrs], alias=False,
                sems=[pltpu.SemaphoreType.DMA((3 * n,))] * 2, phases=(start, mid, end))


def _join_halves(name, arrs, split):
    n = len(arrs)

    def body(*refs):
        outs = refs[n:2 * n]
        ssem, rsem = refs[2 * n:]
        x, y, c, me, sib, chips = _place()
        cps = []
        for a in range(n):
            mine = _half(outs[a], 0, split[a], c, arrs[a].shape[split[a]] // 2)
            cp = _rcopy(mine, mine, ssem.at[a], rsem.at[a], sib)
            cp.start()
            cps.append(cp)
        for a in range(n):
            blk = _half(outs[a], 0, split[a], 1 - c, arrs[a].shape[split[a]] // 2)
            _rcopy(blk, blk, ssem.at[a], rsem.at[a], sib).wait_recv()
        for cp in cps:
            cp.wait_send()

    return pl.pallas_call(
        body, name=name, in_specs=[ANY] * n, out_specs=[ANY] * n,
        out_shape=[jax.ShapeDtypeStruct(a.shape, a.dtype) for a in arrs],
        input_output_aliases={a: a for a in range(n)}, scratch_shapes=[pltpu.SemaphoreType.DMA((n,))] * 2,
    )(*arrs)


def _allreduce_small(s):
    r, cdim = s.shape

    def body(s_ref, o_ref, buf, ssem, rsem):
        x, y, c, me, sib, chips = _place()
        me8 = 4 * x + 2 * y + c
        buf[me8] = s_ref[...]
        flips = [(fx, fy, fc) for fx in (0, 1) for fy in (0, 1) for fc in (0, 1)][1:]
        cps = []
        for k, (fx, fy, fc) in enumerate(flips):
            peer = (x ^ fx if fx else x, y ^ fy if fy else y, c ^ fc if fc else c)
            cp = _rcopy(s_ref, buf.at[me8], ssem.at[k], rsem.at[k], peer)
            cp.start()
            cps.append(cp)
        for k, (fx, fy, fc) in enumerate(flips):
            src = 4 * (x ^ fx if fx else x) + 2 * (y ^ fy if fy else y) + (c ^ fc if fc else c)
            _rcopy(s_ref, buf.at[src], ssem.at[k], rsem.at[k], sib).wait_recv()
        for cp in cps:
            cp.wait_send()
        acc = buf[0]
        for k in range(1, 8):
            acc = acc + buf[k]
        o_ref[...] = acc

    vm = pl.BlockSpec(memory_space=pltpu.VMEM)
    return pl.pallas_call(
        body, name="allreduce_small", in_specs=[vm], out_specs=vm, out_shape=jax.ShapeDtypeStruct((r, cdim), F32),
        scratch_shapes=[pltpu.VMEM((8, r, cdim), F32), pltpu.SemaphoreType.DMA((7,)), pltpu.SemaphoreType.DMA((7,))],
    )(s)


def _add_my_half(name, gs, recvs, c_idx, axis):
    n = len(gs)
    nb, hr, hc = recvs[0].shape
    tr = _pick(hr, (256, 176, 128, 64))
    if axis == 0:
        g4s = [g.reshape(nb, 2, hr, hc) for g in gs]
        gspec = pl.BlockSpec((None, None, tr, hc), lambda b, i, c_ref: (b, c_ref[0], i, 0))
    else:
        g4s = list(gs)
        gspec = pl.BlockSpec((None, tr, hc), lambda b, i, c_ref: (b, i, c_ref[0]))

    def body(c_ref, *refs):
        for k in range(n):
            s = refs[k][...] + refs[n + k][...].astype(F32)
            refs[2 * n + 2 * k][...] = s
            refs[2 * n + 2 * k + 1][...] = s.astype(BF16)

    ospec = pl.BlockSpec((None, tr, hc), lambda b, i, c_ref: (b, i, 0))
    res = pl.pallas_call(
        body, name=name,
        grid_spec=pltpu.PrefetchScalarGridSpec(
            num_scalar_prefetch=1, grid=(nb, hr // tr), in_specs=[gspec] * n + [ospec] * n, out_specs=[ospec] * (2 * n)),
        out_shape=[jax.ShapeDtypeStruct((nb, hr, hc), F32), jax.ShapeDtypeStruct((nb, hr, hc), BF16)] * n,
        compiler_params=_cparams(("parallel", "parallel")),
    )(c_idx, *g4s, *recvs)
    return [(res[2 * k], res[2 * k + 1]) for k in range(n)]


def _sum4(name, landeds, owns, place, axis):
    n = len(landeds)
    nb, h, cdim = landeds[0].shape
    tr = _pick(h, (256, 176, 128, 64))
    nt = h // tr

    def body(p_ref, *refs):
        for k in range(n):
            a1, a2, a3, own = refs[4 * k:4 * k + 4]
            refs[4 * n + k][...] = ((own[...] + a1[...].astype(F32)) + a2[...].astype(F32)) + a3[...].astype(F32)

    def nxt(k):
        return pl.BlockSpec((None, tr, cdim), lambda i, p_ref: ((p_ref[0] + k) % nb, i, 0))

    if axis == 0:
        ospec = pl.BlockSpec((tr, cdim), lambda i, p_ref: (p_ref[1] * nt + i, 0))
        oshape = (2 * h, cdim)
    else:
        ospec = pl.BlockSpec((tr, cdim), lambda i, p_ref: (i, p_ref[1]))
        oshape = (h, 2 * cdim)
    args = []
    for landed, own in zip(landeds, owns):
        args += [landed, landed, landed, own]
    return pl.pallas_call(
        body, name=name,
        grid_spec=pltpu.PrefetchScalarGridSpec(
            num_scalar_prefetch=1, grid=(nt,), in_specs=[nxt(1), nxt(2), nxt(3), nxt(0)] * n, out_specs=[ospec] * n),
        out_shape=[jax.ShapeDtypeStruct(oshape, F32)] * n, compiler_params=_cparams(("parallel",)),
    )(place, *args)


def _cast_other_half(name, g, c_idx, axis):
    nb, r, cdim = g.shape
    hr, hc = (r // 2, cdim) if axis == 0 else (r, cdim // 2)
    tr = _pick(hr, (256, 176, 128, 64))
    if axis == 0:
        g4 = g.reshape(nb, 2, hr, hc)
        gspec = pl.BlockSpec((None, None, tr, hc), lambda b, i, c_ref: (b, 1 - c_ref[0], i, 0))
    else:
        g4 = g
        gspec = pl.BlockSpec((None, tr, hc), lambda b, i, c_ref: (b, i, 1 - c_ref[0]))

    def body(c_ref, g_ref, o_ref):
        o_ref[...] = g_ref[...].astype(BF16)

    return pl.pallas_call(
        body, name=name,
        grid_spec=pltpu.PrefetchScalarGridSpec(
            num_scalar_prefetch=1, grid=(nb, hr // tr), in_specs=[gspec],
            out_specs=pl.BlockSpec((None, tr, hc), lambda b, i, c_ref: (b, i, 0))),
        out_shape=jax.ShapeDtypeStruct((nb, hr, hc), BF16), compiler_params=_cparams(("parallel", "parallel")),
    )(c_idx, g4)


def _adamw(name, ws, gs, ms, vs):
    n = len(ws)
    c1 = 1.0 - ADAM_B1 ** ADAM_STEP
    c2 = 1.0 - ADAM_B2 ** ADAM_STEP

    def fn(*tiles):
        out = []
        for k in range(n):
            w_t, g_t, m_t, v_t = tiles[4 * k:4 * k + 4]
            m_n = ADAM_B1 * m_t + (1.0 - ADAM_B1) * g_t
            v_n = ADAM_B2 * v_t + (1.0 - ADAM_B2) * (g_t * g_t)
            out += [-ADAM_LR * ((m_n / c1) / (jnp.sqrt(v_n / c2) + ADAM_EPS) + ADAM_WD * w_t), m_n, v_n]
        return out

    rows, cdim = ws[0].shape
    tiled = [a for quad in zip(ws, gs, ms, vs) for a in quad]
    pref = (512, 352, 256, 128, 64, 8) if n == 1 else (176, 128, 64, 8)
    res = _rowwise(name, fn, tiled, [], [(cdim, F32)] * (3 * n), tm=_pick(rows, pref))
    return [tuple(res[3 * k:3 * k + 3]) for k in range(n)]


BIG = ("ffn1_w_gate", "ffn1_w_up", "ffn1_w_down", "w_in", "w_out", "ffn2_w_gate", "ffn2_w_up", "ffn2_w_down",
       "w_ple_gate", "w_ple_proj")
SMALL = ("ffn1_norm", "mix_norm", "b_mlstm_gates", "b_fox_f", "mlstm_out_norm", "fox_out_norm", "ffn2_norm",
         "ple_gate_norm", "ple_proj_norm", "final_norm")
WEIGHTS = ("ffn1_norm", "ffn1_w_gate", "ffn1_w_up", "ffn1_w_down", "mix_norm", "w_in", "conv_qk", "b_mlstm_gates",
           "b_fox_f", "mlstm_out_norm", "fox_out_norm", "w_out", "ffn2_norm", "ffn2_w_gate", "ffn2_w_up", "ffn2_w_down",
           "ple_gate_norm", "w_ple_gate", "w_ple_proj", "ple_proj_norm", "final_norm")
TRANSPOSED = ("ffn1_w_gate", "ffn1_w_up", "w_in", "ffn2_w_gate", "ffn2_w_up")
PACK_W = 1024


def _chip_blocks(a):
    r, c4 = a.shape
    return a.reshape(r, 4, c4 // 4).transpose(1, 0, 2)


def _from_chip_blocks(a):
    nb, r, c = a.shape
    return a.transpose(1, 0, 2).reshape(r, nb * c)


def kernel(x, p, ffn1_norm, ffn1_w_gate, ffn1_w_up, ffn1_w_down, mix_norm, w_in, conv_qk, b_mlstm_gates, b_fox_f, mlstm_out_norm, fox_out_norm, w_out, ffn2_norm, ffn2_w_gate, ffn2_w_up, ffn2_w_down, ple_gate_norm, w_ple_gate, w_ple_proj, ple_proj_norm, final_norm, loss_target, m_ffn1_norm, m_ffn1_w_gate, m_ffn1_w_up, m_ffn1_w_down, m_mix_norm, m_w_in, m_conv_qk, m_b_mlstm_gates, m_b_fox_f, m_mlstm_out_norm, m_fox_out_norm, m_w_out, m_ffn2_norm, m_ffn2_w_gate, m_ffn2_w_up, m_ffn2_w_down, m_ple_gate_norm, m_w_ple_gate, m_w_ple_proj, m_ple_proj_norm, m_final_norm, v_ffn1_norm, v_ffn1_w_gate, v_ffn1_w_up, v_ffn1_w_down, v_mix_norm, v_w_in, v_conv_qk, v_b_mlstm_gates, v_b_fox_f, v_mlstm_out_norm, v_fox_out_norm, v_w_out, v_ffn2_norm, v_ffn2_w_gate, v_ffn2_w_up, v_ffn2_w_down, v_ple_gate_norm, v_w_ple_gate, v_w_ple_proj, v_ple_proj_norm, v_final_norm):
    w = dict(ffn1_norm=ffn1_norm, ffn1_w_gate=ffn1_w_gate, ffn1_w_up=ffn1_w_up, ffn1_w_down=ffn1_w_down, mix_norm=mix_norm,
             w_in=w_in, conv_qk=conv_qk, b_mlstm_gates=b_mlstm_gates, b_fox_f=b_fox_f, mlstm_out_norm=mlstm_out_norm,
             fox_out_norm=fox_out_norm, w_out=w_out, ffn2_norm=ffn2_norm, ffn2_w_gate=ffn2_w_gate, ffn2_w_up=ffn2_w_up,
             ffn2_w_down=ffn2_w_down, ple_gate_norm=ple_gate_norm, w_ple_gate=w_ple_gate, w_ple_proj=w_ple_proj,
             ple_proj_norm=ple_proj_norm, final_norm=final_norm)
    m = dict(ffn1_norm=m_ffn1_norm, ffn1_w_gate=m_ffn1_w_gate, ffn1_w_up=m_ffn1_w_up, ffn1_w_down=m_ffn1_w_down,
             mix_norm=m_mix_norm, w_in=m_w_in, conv_qk=m_conv_qk, b_mlstm_gates=m_b_mlstm_gates, b_fox_f=m_b_fox_f,
             mlstm_out_norm=m_mlstm_out_norm, fox_out_norm=m_fox_out_norm, w_out=m_w_out, ffn2_norm=m_ffn2_norm,
             ffn2_w_gate=m_ffn2_w_gate, ffn2_w_up=m_ffn2_w_up, ffn2_w_down=m_ffn2_w_down, ple_gate_norm=m_ple_gate_norm,
             w_ple_gate=m_w_ple_gate, w_ple_proj=m_w_ple_proj, ple_proj_norm=m_ple_proj_norm, final_norm=m_final_norm)
    v = dict(ffn1_norm=v_ffn1_norm, ffn1_w_gate=v_ffn1_w_gate, ffn1_w_up=v_ffn1_w_up, ffn1_w_down=v_ffn1_w_down,
             mix_norm=v_mix_norm, w_in=v_w_in, conv_qk=v_conv_qk, b_mlstm_gates=v_b_mlstm_gates, b_fox_f=v_b_fox_f,
             mlstm_out_norm=v_mlstm_out_norm, fox_out_norm=v_fox_out_norm, w_out=v_w_out, ffn2_norm=v_ffn2_norm,
             ffn2_w_gate=v_ffn2_w_gate, ffn2_w_up=v_ffn2_w_up, ffn2_w_down=v_ffn2_w_down, ple_gate_norm=v_ple_gate_norm,
             w_ple_gate=v_w_ple_gate, w_ple_proj=v_w_ple_proj, ple_proj_norm=v_ple_proj_norm, final_norm=v_final_norm)
    shapes = {n: w[n].shape for n in WEIGHTS}

    def view(a, n):
        return a[0].T if n in TRANSPOSED else a.reshape(-1, a.shape[-1])

    def unview(a, n):
        return (a.T if n in TRANSPOSED else a).reshape(shapes[n])

    w2, m2, v2 = ({n: view(a, n) for n, a in d.items()} for d in (w, m, v))

    c_idx = lax.axis_index("c").astype(jnp.int32).reshape(1)
    me_idx = (2 * lax.axis_index("x") + lax.axis_index("y")).astype(jnp.int32).reshape(1)
    place = jnp.concatenate([me_idx, c_idx])
    slot = {}
    for grp in SAME_SHAPE:
        slot.update(zip(grp, _to_slot("slot_" + grp[0], [w2[n] for n in grp], me_idx, BF16)))
    slot["conv_qk"] = _to_slot("slot_conv_qk", [w2["conv_qk"]], me_idx, F32)[0]
    wg1, wu1, wd1 = _gather4("gather_ffn1", [slot[n] for n in FFN1], [SPLIT[n] for n in FFN1])
    sp = {n: w2[n] for n in SMALL}
    loss_part, grad_x, grads, gs, conv_grad = _local_step(
        x[0], p[0, 0], loss_target[0], sp, wg1, wu1, wd1, [slot[n] for n in REST + ("conv_qk",)], c_idx, place)
    loss = lax.psum(loss_part[0, 0], ("x", "y", "c"))

    small = [gs[n].reshape(1, -1) for n in SMALL] + [conv_grad]
    rows = [jnp.pad(a, ((0, 0), (0, PACK_W - a.shape[1]))) for a in small]
    packed = jnp.concatenate(rows, axis=0)
    packed = jnp.pad(packed, ((0, -packed.shape[0] % 8), (0, 0)))
    red = _allreduce_small(packed)
    for i, n in enumerate(SMALL):
        grads[n] = red[i:i + 1, :gs[n].size]
    dconv = red[len(SMALL):len(SMALL) + CONV_W, :conv_grad.shape[1]]
    cw = conv_qk.shape[-1]
    grads["conv_qk"] = lax.dynamic_slice_in_dim(dconv, (2 * lax.axis_index("x") + lax.axis_index("y")) * cw, cw, axis=1)

    outs = {}
    for grp in SAME_SHAPE + tuple((n,) for n in WEIGHTS if n not in BIG):
        g2s = [grads[n].reshape(w2[n].shape) for n in grp]
        res = _adamw("adamw_" + grp[0], [w2[n] for n in grp], g2s, [m2[n] for n in grp], [v2[n] for n in grp])
        for n, g2, (d, nm, nv) in zip(grp, g2s, res):
            outs[n] = tuple(unview(a, n) for a in (g2, d, nm, nv))
    return (loss, grad_x[None], *[outs[n][0] for n in WEIGHTS], *[outs[n][1] for n in WEIGHTS],
            *[outs[n][2] for n in WEIGHTS], *[outs[n][3] for n in WEIGHTS])
```

```python
import jax
import jax.numpy as jnp
from jax import lax
from jax.experimental import pallas as pl
from jax.experimental.pallas import tpu as pltpu

F32 = jnp.float32
BF16 = jnp.bfloat16
EPS = 1e-6
NH_M, DK_M, DV_M = 4, 64, 128
NH_F, DH_F = 8, 64
CONV_W = 4
ADAM_LR, ADAM_B1, ADAM_B2, ADAM_EPS, ADAM_WD, ADAM_STEP = 0.001, 0.9, 0.999, 1e-08, 0.01, 10
VMEM_LIMIT = 56 * 1024 * 1024


def _cparams(sem):
    return pltpu.CompilerParams(dimension_semantics=sem, vmem_limit_bytes=VMEM_LIMIT)


def _sigmoid(x):
    return 1.0 / (1.0 + jnp.exp(-x))


def _dot(a, b, ca, cb):
    return lax.dot_general(a.astype(BF16), b.astype(BF16), (((ca,), (cb,)), ((), ())), preferred_element_type=F32)


def _rowwise(name, fn, tiled, full, outs, accs=(), tm=512, plan=None):
    rows = tiled[0].shape[0]
    tm = min(tm, rows)
    assert rows % tm == 0
    n_t, n_f, n_o, n_a = len(tiled), len(full), len(outs), len(accs)
    nt = rows // tm
    host = _Hosted(plan, n_t + n_f, n_o + n_a)

    def body(*refs):
        in_refs, orefs, _, prefs = host.split(refs)
        host.run(0, pl.program_id(0) == 0, prefs)
        host.run(1, pl.program_id(0) == 0, prefs)
        ins = [r[...] for r in in_refs]
        res = fn(*ins)
        if not isinstance(res, (tuple, list)):
            res = (res,)
        for r, v in zip(orefs[:n_o], res[:n_o]):
            r[...] = v.astype(r.dtype)
        if n_a:
            @pl.when(pl.program_id(0) == 0)
            def _():
                for r in orefs[n_o:]:
                    r[...] = jnp.zeros_like(r)
            for r, v in zip(orefs[n_o:], res[n_o:]):
                r[...] += v.astype(r.dtype)
        host.run(2, pl.program_id(0) == nt - 1, prefs)

    in_specs = [pl.BlockSpec((tm, a.shape[1]), lambda i: (i, 0)) for a in tiled]
    in_specs += [pl.BlockSpec(a.shape, lambda i: (0, 0)) for a in full]
    out_specs = [pl.BlockSpec((tm, c), lambda i: (i, 0)) for c, _ in outs]
    out_specs += [pl.BlockSpec(s, lambda i: (0, 0)) for s, _ in accs]
    out_shape = [jax.ShapeDtypeStruct((rows, c), d) for c, d in outs]
    out_shape += [jax.ShapeDtypeStruct(s, d) for s, d in accs]
    hc = host.call_args()
    res = pl.pallas_call(
        body, name=name, grid=(nt,), in_specs=in_specs + hc["in_specs"], out_specs=out_specs + hc["out_specs"],
        out_shape=out_shape + hc["out_shape"], scratch_shapes=hc["scratch"], input_output_aliases=hc["aliases"],
        compiler_params=_cparams(("arbitrary",) if (n_a or plan is not None) else ("parallel",)),
    )(*tiled, *full, *hc["args"])
    return res if plan is None else (res[:n_o + n_a], res[n_o + n_a:])


def _colsum(v):
    return jnp.sum(v, axis=0, keepdims=True)


def _rms_fwd_val(x, g):
    r = lax.rsqrt(jnp.mean(x * x, axis=-1, keepdims=True) + EPS)
    return x * r * g


def _rms_bwd_val(dy, x, g):
    r = lax.rsqrt(jnp.mean(x * x, axis=-1, keepdims=True) + EPS)
    xh = x * r
    dxh = dy * g
    dx = r * (dxh - xh * jnp.mean(dxh * xh, axis=-1, keepdims=True))
    return dx, _colsum(dy * xh)


def _mm(name, pairs, out_shape, out_block, out_map, grid, kaxis, ta=False, tb=False, scale=None, res=None,
        out_dtype=F32, plan=None, twin=False):
    n_o = 2 if twin else 1
    nk = grid[kaxis]
    npairs = len(pairs)
    ca, cb = (0 if ta else 1), (1 if tb else 0)
    acc_shape = tuple(d for d in out_block if d is not None)
    n_in = 2 * npairs + (1 if res is not None else 0)
    host = _Hosted(plan, n_in, n_o)

    def body(*refs):
        ins, o_refs, (acc_ref,), prefs = host.split(refs)
        o_ref = o_refs[0]
        in_refs = ins[: 2 * npairs]
        res_ref = ins[2 * npairs] if res is not None else None
        k = pl.program_id(kaxis)
        ids = [pl.program_id(a) for a in range(len(grid))]
        first, last = ids[0] == 0, ids[0] == grid[0] - 1
        for a in range(1, len(grid)):
            first, last = first & (ids[a] == 0), last & (ids[a] == grid[a] - 1)
        host.run(0, first, prefs)
        host.run(1, first, prefs)

        @pl.when(k == 0)
        def _():
            acc_ref[...] = jnp.zeros_like(acc_ref)

        part = None
        for p in range(npairs):
            d = _dot(in_refs[2 * p][...], in_refs[2 * p + 1][...], ca, cb)
            part = d if part is None else part + d
        acc_ref[...] += part

        @pl.when(k == nk - 1)
        def _():
            v = acc_ref[...]
            if scale is not None:
                v = v * scale
            if res_ref is not None:
                v = v + res_ref[...].astype(F32)
            o_ref[...] = v.astype(o_ref.dtype)
            if twin:
                o_refs[1][...] = v.astype(BF16)

        host.run(2, last, prefs)

    in_specs, args = [], []
    for a, ab, am, b, bb, bm in pairs:
        in_specs += [pl.BlockSpec(ab, am), pl.BlockSpec(bb, bm)]
        args += [a, b]
    if res is not None:
        in_specs.append(pl.BlockSpec(out_block, out_map))
        args.append(res)
    sem = tuple("arbitrary" if (i == kaxis or plan is not None) else "parallel" for i in range(len(grid)))
    hc = host.call_args()
    out = pl.pallas_call(
        body, name=name, grid=grid, in_specs=in_specs + hc["in_specs"],
        out_specs=[pl.BlockSpec(out_block, out_map)] * n_o + hc["out_specs"],
        out_shape=[jax.ShapeDtypeStruct(out_shape, out_dtype)] + [jax.ShapeDtypeStruct(out_shape, BF16)] * (n_o - 1)
        + hc["out_shape"],
        scratch_shapes=[pltpu.VMEM(acc_shape, F32)] + hc["scratch"], input_output_aliases=hc["aliases"],
        compiler_params=_cparams(sem),
    )(*args, *hc["args"])
    res_out = tuple(out[:2]) if twin else out[0]
    return res_out if plan is None else (res_out, out[n_o:])


def _pick(n, pref):
    for t in pref:
        if n % t == 0:
            return t
    return n


def _mm_nn(name, a, b, tm=512, tn=512, tk=512, **kw):
    (m, k), n = a.shape, b.shape[1]
    tm, tn, tk = _pick(m, (tm, 256, 128)), _pick(n, (tn, 256, 128)), _pick(k, (tk, 256, 128))
    return _mm(name, [(a, (tm, tk), lambda i, j, kk: (i, kk), b, (tk, tn), lambda i, j, kk: (kk, j))],
               (m, n), (tm, tn), lambda i, j, kk: (i, j), (m // tm, n // tn, k // tk), 2, **kw)


def _mm_nt(name, a, b, tm=512, tn=512, tk=512, **kw):
    (m, k), n = a.shape, b.shape[0]
    tm, tn, tk = _pick(m, (tm, 256, 128)), _pick(n, (tn, 256, 128)), _pick(k, (tk, 256, 128))
    return _mm(name, [(a, (tm, tk), lambda i, j, kk: (i, kk), b, (tn, tk), lambda i, j, kk: (j, kk))],
               (m, n), (tm, tn), lambda i, j, kk: (i, j), (m // tm, n // tn, k // tk), 2, tb=True, **kw)


def _mm_tn(name, a, b, tm=512, tn=512, tk=2048, **kw):
    (k, m), n = a.shape, b.shape[1]
    tm, tn, tk = _pick(m, (tm, 256, 128)), _pick(n, (tn, 256, 128)), _pick(k, (tk, 1024, 512, 256, 128))
    return _mm(name, [(a, (tk, tm), lambda i, j, kk: (kk, i), b, (tk, tn), lambda i, j, kk: (kk, j))],
               (m, n), (tm, tn), lambda i, j, kk: (i, j), (m // tm, n // tn, k // tk), 2, ta=True, **kw)


def _norm_mm(name, h, gamma, w, w_transposed, out_dtype):
    t, d = h.shape
    n = w.shape[0] if w_transposed else w.shape[1]
    tm, tn = _pick(t, (512, 256)), _pick(n, (1024, 512, 256, 128))

    def body(h_ref, gam_ref, w_ref, xn_ref, o_ref, xn_scr):
        @pl.when(pl.program_id(1) == 0)
        def _():
            xn = _rms_fwd_val(h_ref[...], gam_ref[...]).astype(BF16)
            xn_scr[...] = xn
            xn_ref[...] = xn

        o_ref[...] = _dot(xn_scr[...], w_ref[...], 1, 1 if w_transposed else 0).astype(o_ref.dtype)

    wspec = pl.BlockSpec((tn, d), lambda i, j: (j, 0)) if w_transposed else pl.BlockSpec((d, tn), lambda i, j: (0, j))
    return pl.pallas_call(
        body, name=name, grid=(t // tm, n // tn),
        in_specs=[pl.BlockSpec((tm, d), lambda i, j: (i, 0)), pl.BlockSpec((1, d), lambda i, j: (0, 0)), wspec],
        out_specs=[pl.BlockSpec((tm, d), lambda i, j: (i, 0)), pl.BlockSpec((tm, tn), lambda i, j: (i, j))],
        out_shape=[jax.ShapeDtypeStruct((t, d), BF16), jax.ShapeDtypeStruct((t, n), out_dtype)],
        scratch_shapes=[pltpu.VMEM((tm, d), BF16)], compiler_params=_cparams(("parallel", "arbitrary")),
    )(h, gamma, w)


CHAIN_ROWS = 256


def _row_chains(tm):
    n = max(tm // CHAIN_ROWS, 1)
    return [slice(r * (tm // n), (r + 1) * (tm // n)) for r in range(n)]


def _ffn_fwd(pfx, h, gamma, wg, wu, wd, plan=None):
    t, d = h.shape
    nb, f, _ = wg.shape
    tm = _pick(t, (1024, 512, 256))
    nt = t // tm
    host = _Hosted(plan, 5, 4)

    def body(*refs):
        (h_ref, gam_ref, wg_ref, wu_ref, wd_ref), (ho_ref, xn_ref, g_ref, u_ref), (xn_scr, acc_ref), prefs = host.split(refs)
        i, j = pl.program_id(0), pl.program_id(1)
        host.run(0, (i == 0) & (j == 0), prefs)
        host.run(1, (i == nt // 2) & (j == 0), prefs)

        @pl.when(j == 0)
        def _():
            xn = _rms_fwd_val(h_ref[...], gam_ref[...]).astype(BF16)
            xn_scr[...] = xn
            xn_ref[...] = xn
            acc_ref[...] = jnp.zeros_like(acc_ref)

        for rows in _row_chains(tm):
            x = xn_scr[rows, :]
            g = _dot(x, wg_ref[...], 1, 1)
            u = _dot(x, wu_ref[...], 1, 1)
            g_ref[rows, :] = g.astype(BF16)
            u_ref[rows, :] = u.astype(BF16)
            acc_ref[rows, :] += _dot(g * _sigmoid(g) * u, wd_ref[...], 1, 0)

        @pl.when(j == nb - 1)
        def _():
            ho_ref[...] = h_ref[...] + 0.5 * acc_ref[...]

        host.run(2, (i == nt - 1) & (j == nb - 1), prefs)

    row = pl.BlockSpec((tm, d), lambda i, j: (i, 0))
    blk = pl.BlockSpec((None, tm, f), lambda i, j: (j, i, 0))
    wspec = pl.BlockSpec((None, f, d), lambda i, j: (j, 0, 0))
    hc = host.call_args()
    res = pl.pallas_call(
        body, name=pfx + "_fwd", grid=(nt, nb),
        in_specs=[row, pl.BlockSpec((1, d), lambda i, j: (0, 0)), wspec, wspec, wspec] + hc["in_specs"],
        out_specs=[row, row, blk, blk] + hc["out_specs"],
        out_shape=[jax.ShapeDtypeStruct((t, d), F32), jax.ShapeDtypeStruct((t, d), BF16),
                   jax.ShapeDtypeStruct((nb, t, f), BF16), jax.ShapeDtypeStruct((nb, t, f), BF16)] + hc["out_shape"],
        scratch_shapes=[pltpu.VMEM((tm, d), BF16), pltpu.VMEM((tm, d), F32)] + hc["scratch"],
        input_output_aliases=hc["aliases"], compiler_params=_cparams(("arbitrary", "arbitrary")),
    )(h, gamma, wg, wu, wd, *hc["args"])
    return res[:4], res[4:]


def _ffn_bwd(pfx, dh_out, h, gamma, xn, g_all, u_all, wg, wu, wd, plan=None):
    t, d = h.shape
    nb, f, _ = wg.shape
    tm = _pick(t, (512, 256))
    tk = _pick(t, (2048, 1024, 512, 256))

    nt = t // tm
    host = _Hosted(plan, 8, 5)

    def body(*refs):
        ((dy_ref, h_ref, gam_ref, wg_ref, wu_ref, wd_ref, g_ref, u_ref), (dh_ref, dgam_ref, dg_ref, du_ref, a_ref),
         (acc_ref,), prefs) = host.split(refs)
        i, j = pl.program_id(0), pl.program_id(1)
        host.run(0, (i == 0) & (j == 0), prefs)
        host.run(1, (i == nt // 2) & (j == 0), prefs)

        @pl.when((i == 0) & (j == 0))
        def _():
            dgam_ref[...] = jnp.zeros_like(dgam_ref)

        @pl.when(j == 0)
        def _():
            acc_ref[...] = jnp.zeros_like(acc_ref)

        for rows in _row_chains(tm):
            da = _dot(dy_ref[rows, :], wd_ref[...], 1, 1) * 0.5
            g = g_ref[rows, :].astype(F32)
            u = u_ref[rows, :].astype(F32)
            s = _sigmoid(g)
            sl = g * s
            du = (da * sl).astype(BF16)
            dg = (da * u * (s + sl * (1.0 - s))).astype(BF16)
            du_ref[rows, :] = du
            dg_ref[rows, :] = dg
            a_ref[rows, :] = (sl * u).astype(BF16)
            acc_ref[rows, :] += _dot(dg, wg_ref[...], 1, 0) + _dot(du, wu_ref[...], 1, 0)

        @pl.when(j == nb - 1)
        def _():
            dx, dgam = _rms_bwd_val(acc_ref[...], h_ref[...], gam_ref[...])
            dh_ref[...] = dy_ref[...] + dx
            dgam_ref[...] += dgam

        host.run(2, (i == nt - 1) & (j == nb - 1), prefs)

    row = pl.BlockSpec((tm, d), lambda i, j: (i, 0))
    vec = pl.BlockSpec((1, d), lambda i, j: (0, 0))
    blk = pl.BlockSpec((None, tm, f), lambda i, j: (j, i, 0))
    wspec = pl.BlockSpec((None, f, d), lambda i, j: (j, 0, 0))
    hc = host.call_args()
    res = pl.pallas_call(
        body, name=pfx + "_bwd", grid=(nt, nb),
        in_specs=[row, row, vec, wspec, wspec, wspec, blk, blk] + hc["in_specs"],
        out_specs=[row, vec, blk, blk, blk] + hc["out_specs"],
        out_shape=[jax.ShapeDtypeStruct((t, d), F32), jax.ShapeDtypeStruct((1, d), F32)]
        + [jax.ShapeDtypeStruct((nb, t, f), BF16)] * 3 + hc["out_shape"],
        scratch_shapes=[pltpu.VMEM((tm, d), F32)] + hc["scratch"], input_output_aliases=hc["aliases"],
        compiler_params=_cparams(("arbitrary", "arbitrary")),
    )(dh_out, h, gamma, wg, wu, wd, g_all, u_all, *hc["args"])
    dh, dgamma, dg_all, du_all, a_all = res[:5]

    xmap, bmap, omap = (lambda b, k: (k, 0)), (lambda b, k: (b, k, 0)), (lambda b, k: (b, 0, 0))
    dwg, tg = _mm(pfx + "_dwg", [(dg_all, (None, tk, f), bmap, xn, (tk, d), xmap)], (nb, f, d), (None, f, d), omap,
                  (nb, t // tk), 1, ta=True, twin=True)
    dwu, tu = _mm(pfx + "_dwu", [(du_all, (None, tk, f), bmap, xn, (tk, d), xmap)], (nb, f, d), (None, f, d), omap,
                  (nb, t // tk), 1, ta=True, twin=True)
    dwd, td = _mm(pfx + "_dwd", [(a_all, (None, tk, f), bmap, dh_out, (tk, d), xmap)], (nb, f, d), (None, f, d), omap,
                  (nb, t // tk), 1, ta=True, scale=0.5, twin=True)
    return (dh, dgamma, dwg, dwu, dwd), res[5:], (tg, tu, td)


def _ffn_bwd_late_dx(pfx, dh_out, h, gamma, xn, g_all, u_all, wg, wu, wd, plan_gu, plans_dw, make_plan_dx):
    t, d = h.shape
    nb, f, _ = wg.shape
    tm = _pick(t, (512, 256))
    tk = _pick(t, (2048, 1024, 512, 256))
    nt = t // tm
    host_a = _Hosted(plan_gu, 4, 3)

    def body_a(*refs):
        (dy_ref, wd_ref, g_ref, u_ref), (dg_ref, du_ref, a_ref), _, prefs = host_a.split(refs)
        i, j = pl.program_id(0), pl.program_id(1)
        host_a.run(0, (i == 0) & (j == 0), prefs)
        host_a.run(1, (i == 0) & (j == 0), prefs)
        for rows in _row_chains(tm):
            da = _dot(dy_ref[rows, :], wd_ref[...], 1, 1) * 0.5
            g = g_ref[rows, :].astype(F32)
            u = u_ref[rows, :].astype(F32)
            s = _sigmoid(g)
            sl = g * s
            du_ref[rows, :] = (da * sl).astype(BF16)
            dg_ref[rows, :] = (da * u * (s + sl * (1.0 - s))).astype(BF16)
            a_ref[rows, :] = (sl * u).astype(BF16)
        host_a.run(2, (i == nt - 1) & (j == nb - 1), prefs)

    row = pl.BlockSpec((tm, d), lambda i, j: (i, 0))
    vec = pl.BlockSpec((1, d), lambda i, j: (0, 0))
    blk = pl.BlockSpec((None, tm, f), lambda i, j: (j, i, 0))
    wspec = pl.BlockSpec((None, f, d), lambda i, j: (j, 0, 0))
    hc = host_a.call_args()
    res_a = pl.pallas_call(
        body_a, name=pfx + "_bwd_gu", grid=(nt, nb), in_specs=[row, wspec, blk, blk] + hc["in_specs"],
        out_specs=[blk] * 3 + hc["out_specs"], out_shape=[jax.ShapeDtypeStruct((nb, t, f), BF16)] * 3 + hc["out_shape"],
        scratch_shapes=hc["scratch"], input_output_aliases=hc["aliases"], compiler_params=_cparams(("arbitrary", "arbitrary")),
    )(dh_out, wd, g_all, u_all, *hc["args"])
    dg_all, du_all, a_all = res_a[:3]

    xmap, bmap, omap = (lambda b, k: (k, 0)), (lambda b, k: (b, k, 0)), (lambda b, k: (b, 0, 0))
    (dwd, td), out_d = _mm(pfx + "_dwd", [(a_all, (None, tk, f), bmap, dh_out, (tk, d), xmap)], (nb, f, d), (None, f, d),
                           omap, (nb, t // tk), 1, ta=True, scale=0.5, plan=plans_dw[0], twin=True)
    (dwg, tg), out_g = _mm(pfx + "_dwg", [(dg_all, (None, tk, f), bmap, xn, (tk, d), xmap)], (nb, f, d), (None, f, d),
                           omap, (nb, t // tk), 1, ta=True, plan=plans_dw[1], twin=True)
    (dwu, tu), out_u = _mm(pfx + "_dwu", [(du_all, (None, tk, f), bmap, xn, (tk, d), xmap)], (nb, f, d), (None, f, d),
                           omap, (nb, t // tk), 1, ta=True, plan=plans_dw[2], twin=True)

    plan_dx = make_plan_dx((dwg, dwu, dwd), (tg, tu, td))
    host_b = _Hosted(plan_dx, 7, 2)

    def body_b(*refs):
        (dy_ref, h_ref, gam_ref, wg_ref, wu_ref, dg_ref, du_ref), (dh_ref, dgam_ref), (acc_ref,), prefs = host_b.split(refs)
        i, j = pl.program_id(0), pl.program_id(1)
        host_b.run(0, (i == 0) & (j == 0), prefs)
        host_b.run(1, (i == 0) & (j == 0), prefs)

        @pl.when((i == 0) & (j == 0))
        def _():
            dgam_ref[...] = jnp.zeros_like(dgam_ref)

        @pl.when(j == 0)
        def _():
            acc_ref[...] = jnp.zeros_like(acc_ref)

        acc_ref[...] += _dot(dg_ref[...], wg_ref[...], 1, 0) + _dot(du_ref[...], wu_ref[...], 1, 0)

        @pl.when(j == nb - 1)
        def _():
            dx, dgam = _rms_bwd_val(acc_ref[...], h_ref[...], gam_ref[...])
            dh_ref[...] = dy_ref[...] + dx
            dgam_ref[...] += dgam

        host_b.run(2, (i == nt - 1) & (j == nb - 1), prefs)

    hc = host_b.call_args()
    res_b = pl.pallas_call(
        body_b, name=pfx + "_bwd_dx", grid=(nt, nb), in_specs=[row, row, vec, wspec, wspec, blk, blk] + hc["in_specs"],
        out_specs=[row, vec] + hc["out_specs"],
        out_shape=[jax.ShapeDtypeStruct((t, d), F32), jax.ShapeDtypeStruct((1, d), F32)] + hc["out_shape"],
        scratch_shapes=[pltpu.VMEM((tm, d), F32)] + hc["scratch"], input_output_aliases=hc["aliases"],
        compiler_params=_cparams(("arbitrary", "arbitrary")),
    )(dh_out, h, gamma, wg, wu, dg_all, du_all, *hc["args"])
    return (res_b[0], res_b[1], dwg, dwu, dwd), (res_a[3:], out_d, out_g, out_u, res_b[2:])


HALO = 16


def _silu_grad(y):
    s = _sigmoid(y)
    return s * (1.0 + y * (1.0 - s))


def _with_halo(ref, i, n_tiles, tm, before, after):
    t = ref.shape[0]
    r0 = pl.multiple_of(i * tm, tm)
    parts = [ref[pl.ds(r0, tm), :].astype(F32)]
    if before:
        prev = ref[pl.ds(pl.multiple_of(jnp.maximum(r0 - HALO, 0), HALO), HALO), :].astype(F32)
        parts.insert(0, jnp.where(i > 0, prev, 0.0))
    if after:
        nxt = ref[pl.ds(pl.multiple_of(jnp.minimum(r0 + tm, t - HALO), HALO), HALO), :].astype(F32)
        parts.append(jnp.where(i < n_tiles - 1, nxt, 0.0))
    return jnp.concatenate(parts, axis=0)


def _conv_fwd(zbig, w):
    t, c = zbig.shape[0], w.shape[1]
    tm = _pick(t, (512, 256))
    nt = t // tm

    def body(x_ref, w_ref, o_ref):
        xe = _with_halo(x_ref, pl.program_id(0), nt, tm, True, False)
        wv = w_ref[...]
        y = xe * wv[3:4, :]
        for i in range(CONV_W - 1):
            y = y + pltpu.roll(xe, CONV_W - 1 - i, 0) * wv[i:i + 1, :]
        y = y[HALO:, :]
        o_ref[...] = (y * _sigmoid(y)).astype(o_ref.dtype)

    return pl.pallas_call(
        body, name="conv_fwd", grid=(nt,),
        in_specs=[pl.BlockSpec((t, c), lambda i: (0, 0)), pl.BlockSpec(w.shape, lambda i: (0, 0))],
        out_specs=pl.BlockSpec((tm, c), lambda i: (i, 0)), out_shape=jax.ShapeDtypeStruct((t, c), BF16),
        compiler_params=_cparams(("parallel",)),
    )(zbig, w)


def _conv_bwd(zbig, dact, w):
    t, c = dact.shape
    tm = _pick(t, (512, 256))
    nt = t // tm
    n = tm + HALO

    def body(x_ref, d_ref, w_ref, dx_ref, dw_ref):
        xe = _with_halo(x_ref, pl.program_id(0), nt, tm, True, True)
        de = _with_halo(d_ref, pl.program_id(0), nt, tm, False, True)
        wv = w_ref[...]
        sh = [pltpu.roll(xe, CONV_W - 1 - i, 0)[HALO:, :] if i < CONV_W - 1 else xe[HALO:, :] for i in range(CONV_W)]
        y = sh[0] * wv[0:1, :]
        for i in range(1, CONV_W):
            y = y + sh[i] * wv[i:i + 1, :]
        dy = de * _silu_grad(y)
        dx = dy * wv[3:4, :]
        for i in range(CONV_W - 1):
            dx = dx + pltpu.roll(dy, n - (CONV_W - 1 - i), 0) * wv[i:i + 1, :]
        dx_ref[...] = dx[:tm, :].astype(dx_ref.dtype)
        dyc = dy[:tm, :]
        dwp = jnp.concatenate([_colsum(dyc * sh[i][:tm, :]) for i in range(CONV_W)], axis=0)

        @pl.when(pl.program_id(0) == 0)
        def _():
            dw_ref[...] = jnp.zeros_like(dw_ref)
        dw_ref[...] += dwp

    return pl.pallas_call(
        body, name="conv_bwd", grid=(nt,),
        in_specs=[pl.BlockSpec((t, c), lambda i: (0, 0)), pl.BlockSpec((t, c), lambda i: (0, 0)),
                  pl.BlockSpec(w.shape, lambda i: (0, 0))],
        out_specs=[pl.BlockSpec((tm, c), lambda i: (i, 0)), pl.BlockSpec(w.shape, lambda i: (0, 0))],
        out_shape=[jax.ShapeDtypeStruct((t, c), BF16), jax.ShapeDtypeStruct(w.shape, F32)],
        compiler_params=_cparams(("arbitrary",)),
    )(zbig, dact, w)


LM = 256
HI = lax.Precision.HIGHEST


def _logsig(x):
    return jnp.minimum(x, 0.0) - jnp.log(1.0 + jnp.exp(-jnp.abs(x)))


def _tri(n, lower):
    r = lax.broadcasted_iota(jnp.int32, (n, n), 0)
    c = lax.broadcasted_iota(jnp.int32, (n, n), 1)
    return (r >= c) if lower else (r <= c)


def _f32dot(a, b):
    return lax.dot_general(a, b, (((1,), (0,)), ((), ())), precision=HI, preferred_element_type=F32)


def _tri_dot(a, b, a_is_tri):
    tri = (a if a_is_tri else b).astype(BF16)
    parts = _split3(b if a_is_tri else a)
    outs = [_dot(tri, p, 1, 0) if a_is_tri else _dot(p, tri, 1, 0) for p in parts]
    return (outs[0] + outs[1]) + outs[2]


def _mlstm_decays(zs_ref, zsr_ref, bc_ref, br_ref):
    l = LM
    lf_c = _logsig(zs_ref[:, 0:2 * NH_M] + bc_ref[...])
    lf_r = _logsig(zsr_ref[...] + br_ref[...])
    return _tri_dot(_tri(l, True), lf_c, True), _tri_dot(lf_r, _tri(l, False), False)


def _mlstm_chunk(h, q_ref, k_ref, v_ref, zs_ref, zsr_ref, bc_ref, br_ref, c_prev, m_prev, decays):
    l = LM
    q = q_ref[:, h * DK_M:(h + 1) * DK_M].astype(F32) * (DK_M ** -0.5)
    k = k_ref[:, h * DK_M:(h + 1) * DK_M]
    v = v_ref[:, h * DV_M:(h + 1) * DV_M]
    lane = lax.broadcasted_iota(jnp.int32, (l, DV_M), 1)
    v1 = jnp.concatenate([v, (lane == 0).astype(v.dtype)], axis=1)
    zs, zsr = zs_ref[...], zsr_ref[...]
    li_c = zs[:, h:h + 1] + bc_ref[:, h:h + 1]
    fp_c = zs[:, NH_M + h:NH_M + h + 1] + bc_ref[:, NH_M + h:NH_M + h + 1]
    li_r = zsr[h:h + 1, :] + br_ref[h:h + 1, :]
    fp_r = zsr[NH_M + h:NH_M + h + 1, :] + br_ref[NH_M + h:NH_M + h + 1, :]
    low = _tri(l, True)
    b_c = decays[0][:, NH_M + h:NH_M + h + 1]
    b_r = decays[1][NH_M + h:NH_M + h + 1, :]
    g = b_r[:, l - 1:l]
    dmat = jnp.where(low, b_c - b_r + li_r, -jnp.inf)
    inter = b_c + m_prev
    m_t = jnp.maximum(inter, jnp.max(dmat, axis=1, keepdims=True))
    w_inter = jnp.exp(inter - m_t)
    amat = jnp.exp(dmat - m_t)
    s = _dot(q, k, 1, 1)
    p = amat * s
    qc = _dot(q, c_prev, 1, 0)
    qc_w = w_inter * qc
    num1 = qc_w + _dot(p, v1, 1, 0)
    den = num1[:, DV_M:DV_M + 1]
    mx = jnp.maximum(jnp.abs(den), jnp.exp(-m_t))
    hh = num1[:, :DV_M] / mx
    a_c = g - b_c + li_c
    return dict(q=q, k=k, v1=v1, fp_c=fp_c, fp_r=fp_r, b_c=b_c, g=g, m_t=m_t, w_inter=w_inter, amat=amat, s=s, p=p,
                qc_w=qc_w, den=den, mx=mx, hh=hh, a_c=a_c)


def _mlstm_fwd(qk, zbig, zs, zsr, bc, br, gm):
    t = zs.shape[0]
    l = LM
    nc = t // l
    dm = NH_M * DV_M

    def body(q_ref, k_ref, v_ref, o_ref, zs_ref, zsr_ref, bc_ref, br_ref, gm_ref, y_ref, cst_ref, mst_ref, c_scr, m_scr):
        @pl.when(pl.program_id(0) == 0)
        def _():
            c_scr[...] = jnp.zeros_like(c_scr)
            m_scr[...] = jnp.zeros_like(m_scr)

        cst_ref[...] = c_scr[...]
        mst_ref[...] = m_scr[...]
        ys = []
        decays = _mlstm_decays(zs_ref, zsr_ref, bc_ref, br_ref)
        for h in range(NH_M):
            c_prev = c_scr[h]
            m_prev = m_scr[h:h + 1, 0:1]
            r = _mlstm_chunk(h, q_ref, k_ref, v_ref, zs_ref, zsr_ref, bc_ref, br_ref, c_prev, m_prev, decays)
            hh = r["hh"]
            gh = gm_ref[:, h * DV_M:(h + 1) * DV_M]
            hn = hh * lax.rsqrt(jnp.mean(hh * hh, axis=-1, keepdims=True) + EPS) * gh
            og = o_ref[:, h * DV_M:(h + 1) * DV_M].astype(F32)
            ys.append(hn * _sigmoid(og))
            m_new = jnp.maximum(r["g"] + m_prev, jnp.max(r["a_c"], axis=0, keepdims=True))
            decay = jnp.exp(r["g"] + m_prev - m_new)
            wk = r["k"].astype(F32) * jnp.exp(r["a_c"] - m_new)
            c_scr[h] = decay * c_prev + _dot(wk, r["v1"], 0, 0)
            m_scr[h:h + 1, :] = jnp.broadcast_to(m_new, (1, 128))
        y_ref[...] = jnp.concatenate(ys, axis=1).astype(y_ref.dtype)

    return pl.pallas_call(
        body, name="mlstm_fwd", grid=(nc,),
        in_specs=[pl.BlockSpec((l, NH_M * DK_M), lambda i: (i, 0)), pl.BlockSpec((l, NH_M * DK_M), lambda i: (i, 1)),
                  pl.BlockSpec((l, dm), lambda i: (i, 1)), pl.BlockSpec((l, dm), lambda i: (i, 2)),
                  pl.BlockSpec((l, 128), lambda i: (i, 0)), pl.BlockSpec((8, l), lambda i: (0, i)),
                  pl.BlockSpec((1, 8), lambda i: (0, 0)), pl.BlockSpec((8, 1), lambda i: (0, 0)),
                  pl.BlockSpec((1, dm), lambda i: (0, 0))],
        out_specs=[pl.BlockSpec((l, dm), lambda i: (i, 0)), pl.BlockSpec((None, NH_M, DK_M, 2 * DV_M), lambda i: (i, 0, 0, 0)),
                   pl.BlockSpec((None, 8, 128), lambda i: (i, 0, 0))],
        out_shape=[jax.ShapeDtypeStruct((t, dm), BF16), jax.ShapeDtypeStruct((nc, NH_M, DK_M, 2 * DV_M), F32),
                   jax.ShapeDtypeStruct((nc, 8, 128), F32)],
        scratch_shapes=[pltpu.VMEM((NH_M, DK_M, 2 * DV_M), F32), pltpu.VMEM((8, 128), F32)],
        compiler_params=_cparams(("arbitrary",)),
    )(qk, qk, zbig, zbig, zs, zsr, bc, br, gm)


def _mlstm_bwd(qk, zbig, zs, zsr, bc, br, gm, cst, mst, dycat):
    t = zs.shape[0]
    l = LM
    nc = t // l
    dm = NH_M * DV_M

    def body(q_ref, k_ref, v_ref, o_ref, zs_ref, zsr_ref, bc_ref, br_ref, gm_ref, cst_ref, mst_ref, cnx_ref, mnx_ref,
             dy_ref, dqk_ref, dv_ref, do_ref, dzs_ref, dzr_ref, dgm_ref, dc_scr):
        @pl.when(pl.program_id(0) == 0)
        def _():
            dc_scr[...] = jnp.zeros_like(dc_scr)
            dgm_ref[...] = jnp.zeros_like(dgm_ref)

        lane = lax.broadcasted_iota(jnp.int32, (l, 128), 1)
        upper, lower = _tri(l, False), _tri(l, True)
        db_all, sig_c, carries = jnp.zeros((l, 128), F32), jnp.zeros((l, 128), F32), jnp.zeros((1, 128), F32)
        decays = _mlstm_decays(zs_ref, zsr_ref, bc_ref, br_ref)
        dzr_rows = [None] * 8
        dvs, dos, dgs, dqs, dks = [], [], [], [], []
        dzs = jnp.zeros((l, 128), F32)
        for h in range(NH_M):
            c_prev = cst_ref[h]
            m_prev = mst_ref[h:h + 1, 0:1]
            r = _mlstm_chunk(h, q_ref, k_ref, v_ref, zs_ref, zsr_ref, bc_ref, br_ref, c_prev, m_prev, decays)
            hh, mx, den, m_t, v1, amat = r["hh"], r["mx"], r["den"], r["m_t"], r["v1"], r["amat"]
            gh = gm_ref[:, h * DV_M:(h + 1) * DV_M]
            rs = lax.rsqrt(jnp.mean(hh * hh, axis=-1, keepdims=True) + EPS)
            xh = hh * rs
            sg = _sigmoid(o_ref[:, h * DV_M:(h + 1) * DV_M].astype(F32))
            dyh = dy_ref[:, h * DV_M:(h + 1) * DV_M]
            dos.append(dyh * xh * gh * sg * (1.0 - sg))
            dhn = dyh * sg
            dgs.append(_colsum(dhn * xh))
            dxh = dhn * gh
            dh = rs * (dxh - xh * jnp.mean(dxh * xh, axis=-1, keepdims=True))
            g1 = dh / mx
            hd = jnp.sum(hh * dh, axis=-1, keepdims=True)
            dden = jnp.where(jnp.abs(den) > jnp.exp(-m_t), -hd / mx * jnp.sign(den), 0.0)
            g256 = jnp.concatenate([g1, jnp.where(lane == 0, dden, 0.0)], axis=1)
            dc_h = dc_scr[h]
            ea = jnp.exp(r["a_c"])
            dp = _dot(g256, v1, 1, 1)
            ds = dp * amat
            dqs.append((r["w_inter"] * _dot(g256, c_prev, 1, 1) + _dot(ds, r["k"], 1, 0)) * (DK_M ** -0.5))
            dks.append(_dot(ds, r["q"], 0, 0) + ea * _dot(v1, dc_h, 1, 1))
            dv_st = ea * _dot(r["k"], dc_h, 1, 0)
            dv1 = _dot(r["p"], g256, 0, 0) + dv_st
            dvs.append(dv1[:, :DV_M])
            wmat = dp * r["p"]
            c_in = _colsum(wmat)
            c_st = jnp.sum(v1.astype(F32) * dv_st, axis=-1, keepdims=True)
            r_t = jnp.sum(wmat, axis=1, keepdims=True) + jnp.sum(g256 * r["qc_w"], axis=-1, keepdims=True)
            db = r_t - c_st
            carry = jnp.exp(mnx_ref[h:h + 1, 0:1]) * jnp.sum(
                jnp.sum(dc_h * cnx_ref[h], axis=1, keepdims=True), axis=0, keepdims=True)
            db_all = db_all + jnp.where(lane == NH_M + h, db, 0.0)
            sig_c = sig_c + jnp.where(lane == NH_M + h, _sigmoid(-r["fp_c"]), 0.0)
            carries = carries + jnp.where(lane[0:1, :] == NH_M + h, carry, 0.0)
            dzs = dzs + jnp.where(lane == h, c_st, 0.0)
            dzr_rows[h] = c_in
            dzr_rows[NH_M + h] = _sigmoid(-r["fp_r"])
            wq = r["q"] * jnp.exp(r["b_c"] - m_t)
            dc_scr[h] = jnp.exp(r["g"]) * dc_h + _dot(wq, g256, 0, 0)
        dzs = dzs + (_tri_dot(upper, db_all, True) + carries) * sig_c
        c_in4 = jnp.concatenate(dzr_rows[:NH_M], axis=0)
        dlf_r4 = -_tri_dot(c_in4, lower, False)
        dzr_rows = dzr_rows[:NH_M] + [dlf_r4[h:h + 1, :] * dzr_rows[NH_M + h] for h in range(NH_M)]
        dqk_ref[...] = jnp.concatenate(dqs + dks, axis=1)
        dv_ref[...] = jnp.concatenate(dvs, axis=1).astype(dv_ref.dtype)
        do_ref[...] = jnp.concatenate(dos, axis=1).astype(do_ref.dtype)
        dzs_ref[...] = dzs
        dzr_ref[...] = jnp.concatenate(dzr_rows, axis=0)
        dgm_ref[...] += jnp.concatenate(dgs, axis=1)

    rev = lambda i: nc - 1 - i
    nxt = lambda i: jnp.minimum(nc - i, nc - 1)
    return pl.pallas_call(
        body, name="mlstm_bwd", grid=(nc,),
        in_specs=[pl.BlockSpec((l, NH_M * DK_M), lambda i: (rev(i), 0)), pl.BlockSpec((l, NH_M * DK_M), lambda i: (rev(i), 1)),
                  pl.BlockSpec((l, dm), lambda i: (rev(i), 1)), pl.BlockSpec((l, dm), lambda i: (rev(i), 2)),
                  pl.BlockSpec((l, 128), lambda i: (rev(i), 0)), pl.BlockSpec((8, l), lambda i: (0, rev(i))),
                  pl.BlockSpec((1, 8), lambda i: (0, 0)), pl.BlockSpec((8, 1), lambda i: (0, 0)),
                  pl.BlockSpec((1, dm), lambda i: (0, 0)),
                  pl.BlockSpec((None, NH_M, DK_M, 2 * DV_M), lambda i: (rev(i), 0, 0, 0)),
                  pl.BlockSpec((None, 8, 128), lambda i: (rev(i), 0, 0)),
                  pl.BlockSpec((None, NH_M, DK_M, 2 * DV_M), lambda i: (nxt(i), 0, 0, 0)),
                  pl.BlockSpec((None, 8, 128), lambda i: (nxt(i), 0, 0)),
                  pl.BlockSpec((l, dm), lambda i: (rev(i), 0))],
        out_specs=[pl.BlockSpec((l, dm), lambda i: (rev(i), 0)),
                   pl.BlockSpec((l, dm), lambda i: (rev(i), 0)), pl.BlockSpec((l, dm), lambda i: (rev(i), 0)),
                   pl.BlockSpec((l, 128), lambda i: (rev(i), 0)), pl.BlockSpec((8, l), lambda i: (0, rev(i))),
                   pl.BlockSpec((1, dm), lambda i: (0, 0))],
        out_shape=[jax.ShapeDtypeStruct((t, dm), F32),
                   jax.ShapeDtypeStruct((t, dm), BF16), jax.ShapeDtypeStruct((t, dm), BF16),
                   jax.ShapeDtypeStruct((t, 128), F32), jax.ShapeDtypeStruct((8, t), F32),
                   jax.ShapeDtypeStruct((1, dm), F32)],
        scratch_shapes=[pltpu.VMEM((NH_M, DK_M, 2 * DV_M), F32)],
        compiler_params=_cparams(("arbitrary",)),
    )(qk, qk, zbig, zbig, zs, zsr, bc, br, gm, cst, mst, cst, mst, dycat)


def _fox_cumsum(zsr, bf_r):
    t = zsr.shape[1]
    cw = _pick(t, (512, 256))

    def body(z_ref, b_ref, c_ref):
        up = _tri(cw, False).astype(F32)
        carry = jnp.zeros((NH_F, 1), F32)
        for j in range(t // cw):
            cs = _f32dot(_logsig(z_ref[:, j * cw:(j + 1) * cw] + b_ref[...]), up) + carry
            c_ref[:, j * cw:(j + 1) * cw] = cs
            carry = cs[:, cw - 1:cw]

    return pl.pallas_call(
        body, name="fox_cumsum", grid=(1,),
        in_specs=[pl.BlockSpec((NH_F, t), lambda i: (1, 0)), pl.BlockSpec((NH_F, 1), lambda i: (0, 0))],
        out_specs=pl.BlockSpec((NH_F, t), lambda i: (0, 0)), out_shape=jax.ShapeDtypeStruct((NH_F, t), F32),
        compiler_params=_cparams(("arbitrary",)),
    )(zsr, bf_r)


def _fox_gate_bwd(zsr, bf_r, dc):
    t = zsr.shape[1]
    cw = _pick(t, (512, 256))

    def body(z_ref, b_ref, dc_ref, o_ref):
        low = _tri(cw, True).astype(F32)
        carry = jnp.zeros((NH_F, 1), F32)
        for j in reversed(range(t // cw)):
            sl = slice(j * cw, (j + 1) * cw)
            dlf = _f32dot(dc_ref[:, sl], low) + carry
            o_ref[:, sl] = dlf * _sigmoid(-(z_ref[:, sl] + b_ref[...]))
            carry = dlf[:, 0:1]

    return pl.pallas_call(
        body, name="fox_gate_bwd", grid=(1,),
        in_specs=[pl.BlockSpec((NH_F, t), lambda i: (1, 0)), pl.BlockSpec((NH_F, 1), lambda i: (0, 0)),
                  pl.BlockSpec((NH_F, t), lambda i: (0, 0))],
        out_specs=pl.BlockSpec((NH_F, t), lambda i: (0, 0)), out_shape=jax.ShapeDtypeStruct((NH_F, t), F32),
        compiler_params=_cparams(("arbitrary",)),
    )(zsr, bf_r, dc)


def _causal_mask(n):
    return _tri(n, True)


AUG = 64


def _split3(c):
    hi = c.astype(BF16).astype(F32)
    r1 = c - hi
    mid = r1.astype(BF16).astype(F32)
    return hi, mid, r1 - mid


def _fox_prep(zbig, ct):
    t = zbig.shape[0]
    tm = _pick(t, (512, 256))

    def body(q_ref, k_ref, v_ref, c_ref, qo_ref, ko_ref, vo_ref):
        lane = lax.broadcasted_iota(jnp.int32, (tm, AUG), 1)
        qv, kv, vv, cv = q_ref[...], k_ref[...], v_ref[...], c_ref[...]
        one = (lane == 0).astype(BF16)
        for h in range(NH_F):
            hi, mid, lo = _split3(cv[:, h:h + 1])
            aq = jnp.where(lane == 0, hi, jnp.where(lane == 1, mid, jnp.where(lane == 2, lo, jnp.where(lane < 6, 1.0, 0.0))))
            ak = jnp.where(lane < 3, 1.0, jnp.where(lane == 3, -hi, jnp.where(lane == 4, -mid, jnp.where(lane == 5, -lo, 0.0))))
            sl = slice(h * DH_F, (h + 1) * DH_F)
            qo_ref[h] = jnp.concatenate([qv[:, sl] * (DH_F ** -0.5), aq.astype(BF16)], axis=1).astype(BF16)
            ko_ref[h] = jnp.concatenate([kv[:, sl], ak.astype(BF16)], axis=1)
            vo_ref[h] = jnp.concatenate([vv[:, sl], one], axis=1)

    ospec = pl.BlockSpec((NH_F, tm, 128), lambda i: (0, i, 0))
    return pl.pallas_call(
        body, name="fox_prep", grid=(t // tm,),
        in_specs=[pl.BlockSpec((tm, 512), lambda i: (i, 3)), pl.BlockSpec((tm, 512), lambda i: (i, 4)),
                  pl.BlockSpec((tm, 512), lambda i: (i, 5)), pl.BlockSpec((tm, NH_F), lambda i: (i, 0))],
        out_specs=[ospec] * 3, out_shape=[jax.ShapeDtypeStruct((NH_F, t, 128), BF16)] * 3,
        compiler_params=_cparams(("parallel",)),
    )(zbig, zbig, zbig, ct)


def _fox_fwd2(qa, ka, va, gf, plan=None):
    nh, t, _ = qa.shape
    tq = _pick(t, (512, 256))
    nq = t // tq
    group = 4
    host = _Hosted(plan, 4, 3)

    def body(*refs):
        (q_ref, k_ref, v_ref, g_ref), (y_ref, o_ref, lse_ref), _, prefs = host.split(refs)
        i = pl.program_id(0)
        host.run(0, i == 0, prefs)
        host.run(1, i == max(nq - 2, 0), prefs)
        lane = lax.broadcasted_iota(jnp.int32, (tq, 128), 1)
        ys, os_ = [], []
        lse_all = jnp.zeros((tq, 128), F32)
        for h0 in range(0, nh, group):
            heads = range(h0, h0 + group)
            qvs = [q_ref[h] for h in heads]

            def blk(j, carry, masked, heads=heads, qvs=qvs):
                k0 = pl.multiple_of(j * tq, tq)
                out = []
                for (m, acc), h, qv in zip(carry, heads, qvs):
                    s = lax.dot_general(qv, k_ref[h, pl.ds(k0, tq), :], (((1,), (1,)), ((), ())), preferred_element_type=F32)
                    if masked:
                        s = jnp.where(_causal_mask(tq), s, -jnp.inf)
                    m_new = jnp.maximum(m, jnp.max(s, axis=1, keepdims=True))
                    p = jnp.exp(s - m_new).astype(BF16)
                    pv = lax.dot_general(p, v_ref[h, pl.ds(k0, tq), :], (((1,), (0,)), ((), ())), preferred_element_type=F32)
                    out.append((m_new, jnp.exp(m - m_new) * acc + pv))
                return tuple(out)

            init = tuple((jnp.full((tq, 1), -jnp.inf, F32), jnp.zeros((tq, 128), F32)) for _ in heads)
            carry = lax.fori_loop(0, i, lambda j, c: blk(j, c, False), init)
            for (m, acc), h in zip(blk(i, carry, True), heads):
                l = acc[:, DH_F:DH_F + 1]
                o = acc[:, :DH_F] / l
                os_.append(o)
                gh = g_ref[:, h * DH_F:(h + 1) * DH_F]
                ys.append(o * lax.rsqrt(jnp.mean(o * o, axis=-1, keepdims=True) + EPS) * gh)
                lse_all = lse_all + jnp.where(lane == h, m + jnp.log(l), 0.0)
        y_ref[...] = jnp.concatenate(ys, axis=1).astype(y_ref.dtype)
        o_ref[...] = jnp.concatenate(os_, axis=1)
        lse_ref[...] = lse_all
        host.run(2, i == nq - 1, prefs)

    full = pl.BlockSpec((nh, t, 128), lambda i: (0, 0, 0))
    hc = host.call_args()
    res = pl.pallas_call(
        body, name="fox_fwd", grid=(nq,),
        in_specs=[pl.BlockSpec((nh, tq, 128), lambda i: (0, i, 0)), full, full, pl.BlockSpec((1, nh * DH_F), lambda i: (0, 0))]
        + hc["in_specs"],
        out_specs=[pl.BlockSpec((tq, nh * DH_F), lambda i: (i, 0)), pl.BlockSpec((tq, nh * DH_F), lambda i: (i, 0)),
                   pl.BlockSpec((tq, 128), lambda i: (i, 0))] + hc["out_specs"],
        out_shape=[jax.ShapeDtypeStruct((t, nh * DH_F), BF16), jax.ShapeDtypeStruct((t, nh * DH_F), F32),
                   jax.ShapeDtypeStruct((t, 128), F32)] + hc["out_shape"],
        scratch_shapes=hc["scratch"], input_output_aliases=hc["aliases"], compiler_params=_cparams(("arbitrary",)),
    )(qa, ka, va, gf, *hc["args"])
    return res[:3], res[3:]


def _fox_bwd_prep(dycat, o, gf):
    t = o.shape[0]
    tm = _pick(t, (512, 256))

    def body(dy_ref, o_ref, g_ref, do_ref, dl_ref, dg_ref):
        lane = lax.broadcasted_iota(jnp.int32, (tm, 128), 1)
        dyv, ov, gv = dy_ref[...], o_ref[...], g_ref[...]
        dgs = []
        dl = jnp.zeros((tm, 128), F32)
        pad = jnp.zeros((tm, AUG), BF16)
        for h in range(NH_F):
            sl = slice(h * DH_F, (h + 1) * DH_F)
            dx, dg = _rms_bwd_val(dyv[:, sl], ov[:, sl], gv[:, sl])
            dgs.append(dg)
            do_ref[h] = jnp.concatenate([dx.astype(BF16), pad], axis=1)
            dl = dl + jnp.where(lane == h, jnp.sum(dx * ov[:, sl], axis=-1, keepdims=True), 0.0)
        dl_ref[...] = dl

        @pl.when(pl.program_id(0) == 0)
        def _():
            dg_ref[...] = jnp.zeros_like(dg_ref)
        dg_ref[...] += jnp.concatenate(dgs, axis=1)

    return pl.pallas_call(
        body, name="fox_bwd_prep", grid=(t // tm,),
        in_specs=[pl.BlockSpec((tm, 512), lambda i: (i, 1)), pl.BlockSpec((tm, 512), lambda i: (i, 0)),
                  pl.BlockSpec((1, 512), lambda i: (0, 0))],
        out_specs=[pl.BlockSpec((NH_F, tm, 128), lambda i: (0, i, 0)), pl.BlockSpec((tm, 128), lambda i: (i, 0)),
                   pl.BlockSpec((1, 512), lambda i: (0, 0))],
        out_shape=[jax.ShapeDtypeStruct((NH_F, t, 128), BF16), jax.ShapeDtypeStruct((t, 128), F32),
                   jax.ShapeDtypeStruct((1, 512), F32)],
        compiler_params=_cparams(("arbitrary",)),
    )(dycat, o, gf)


def _fox_bwd2(qa, ka, va, doa, lse, delta):
    nh, t, _ = qa.shape
    tq = _pick(t, (512, 256))
    nq = t // tq

    group = 2

    def tdot(a, b, cb):
        return lax.dot_general(a, b, (((0,), (cb,)), ((), ())), preferred_element_type=F32)

    def body(q_ref, k_ref, v_ref, do_ref, lse_ref, dl_ref, dq_ref, dk_ref, dv_ref):
        hp, j = pl.program_id(0), pl.program_id(1)

        @pl.when(j == 0)
        def _():
            dq_ref[...] = jnp.zeros_like(dq_ref)

        lane = lax.broadcasted_iota(jnp.int32, (tq, 128), 1)

        def blk(i, carry, masked):
            rows = pl.ds(pl.multiple_of(i * tq, tq), tq)
            lse_t, dl_t = lse_ref[rows, :], dl_ref[rows, :]
            out = []
            for g, (dk, dv) in enumerate(carry):
                h = hp * group + g
                kb, vb = k_ref[g], v_ref[g]
                qb, dob = q_ref[g, rows, :], do_ref[g, rows, :]
                lse_h = jnp.sum(jnp.where(lane == h, lse_t, 0.0), axis=1, keepdims=True)
                dl_h = jnp.sum(jnp.where(lane == h, dl_t, 0.0), axis=1, keepdims=True)
                s = lax.dot_general(qb, kb, (((1,), (1,)), ((), ())), preferred_element_type=F32)
                if masked:
                    s = jnp.where(_causal_mask(tq), s, -jnp.inf)
                p = jnp.exp(s - lse_h)
                dp = lax.dot_general(dob, vb, (((1,), (1,)), ((), ())), preferred_element_type=F32)
                ds = (p * (dp - dl_h)).astype(BF16)
                dv = dv + tdot(dob, p.astype(BF16), 0)
                dk = dk + tdot(qb, ds, 0)
                dq_ref[g, :, rows] += tdot(kb, ds, 1)
                out.append((dk, dv))
            return tuple(out)

        init = tuple((jnp.zeros((128, tq), F32), jnp.zeros((128, tq), F32)) for _ in range(group))
        carry = blk(j, init, True)
        carry = lax.fori_loop(j + 1, nq, lambda i, c: blk(i, c, False), carry)
        for g, (dk, dv) in enumerate(carry):
            dk_ref[g] = dk
            dv_ref[g] = dv

    full = pl.BlockSpec((group, t, 128), lambda h, j: (h, 0, 0))
    tile = pl.BlockSpec((group, tq, 128), lambda h, j: (h, j, 0))
    cols = pl.BlockSpec((t, 128), lambda h, j: (0, 0))
    full_t = pl.BlockSpec((group, 128, t), lambda h, j: (h, 0, 0))
    tile_t = pl.BlockSpec((group, 128, tq), lambda h, j: (h, 0, j))
    return pl.pallas_call(
        body, name="fox_bwd", grid=(nh // group, nq), in_specs=[full, tile, tile, full, cols, cols],
        out_specs=[full_t, tile_t, tile_t], out_shape=[jax.ShapeDtypeStruct((nh, 128, t), F32)] * 3,
        compiler_params=_cparams(("parallel", "arbitrary")),
    )(qa, ka, va, doa, lse, delta)


def _fox_bwd_post(dqa, dka, dva):
    nh, _, t = dqa.shape
    tm = _pick(t, (512, 256))

    def body(dq_ref, dk_ref, dv_ref, oq_ref, ok_ref, ov_ref, dc_ref):
        qs, ks, vs, dcs = [], [], [], []
        for h in range(nh):
            dq, dk = dq_ref[h], dk_ref[h]
            qs.append(dq.T[:, :DH_F] * (DH_F ** -0.5))
            ks.append(dk.T[:, :DH_F])
            vs.append(dv_ref[h].T[:, :DH_F])
            dcs.append(dq[DH_F:DH_F + 1, :] - dk[DH_F + 3:DH_F + 4, :])
        oq_ref[...] = jnp.concatenate(qs, axis=1).astype(BF16)
        ok_ref[...] = jnp.concatenate(ks, axis=1).astype(BF16)
        ov_ref[...] = jnp.concatenate(vs, axis=1).astype(BF16)
        dc_ref[...] = jnp.concatenate(dcs, axis=0)

    ispec = pl.BlockSpec((nh, 128, tm), lambda i: (0, 0, i))
    ospec = pl.BlockSpec((tm, nh * DH_F), lambda i: (i, 0))
    return pl.pallas_call(
        body, name="fox_bwd_post", grid=(t // tm,), in_specs=[ispec] * 3,
        out_specs=[ospec] * 3 + [pl.BlockSpec((nh, tm), lambda i: (0, i))],
        out_shape=[jax.ShapeDtypeStruct((t, nh * DH_F), BF16)] * 3 + [jax.ShapeDtypeStruct((nh, t), F32)],
        compiler_params=_cparams(("parallel",)),
    )(dqa, dka, dva)


IN_OFF = (0, 512, 1024, 1544, 2056, 2568)
IN_GATES = (1536, 3080)


FFN1 = ("ffn1_w_gate", "ffn1_w_up", "ffn1_w_down")
REST = ("w_in", "w_out", "ffn2_w_gate", "ffn2_w_up", "ffn2_w_down", "w_ple_gate", "w_ple_proj")
SPLIT = {n: 1 if n == "w_in" else 0 for n in FFN1 + REST}
SAME_SHAPE = (FFN1, ("ffn2_w_gate", "ffn2_w_up", "ffn2_w_down"), ("w_out", "w_ple_gate"), ("w_in",), ("w_ple_proj",))


def _grouped(names):
    return [tuple(n for n in grp if n in names) for grp in SAME_SHAPE if any(n in names for n in grp)]


def _rs_partials(names, gw, c_idx, twins, run_swap=None):
    wire = [twins[n] if n in twins else _cast_other_half("rs_cast_" + n, gw[n], c_idx, SPLIT[n]) for n in names]
    plan = _swap_plan(wire, [SPLIT[n] if n in twins else None for n in names])
    swapped = dict(zip(names, run_swap(plan) if run_swap else _run_plan("rs_swap_" + names[0], plan)))
    out = {}
    for grp in _grouped(names):
        res = _add_my_half("rs_add_" + grp[0], [gw[n] for n in grp], [swapped[n] for n in grp], c_idx, SPLIT[grp[0]])
        out.update(zip(grp, res))
    return [out[n] for n in names]


def _local_step(x, p, tgt, sp, wg1, wu1, wd1, rest_slots, c_idx, place):
    t, d = x.shape
    slot = dict(zip(REST + ("conv_qk",), rest_slots))
    (h1, xn1, g1, u1), (w_in, conv_w) = _ffn_fwd(
        "ffn1", x, sp["ffn1_norm"], wg1, wu1, wd1, plan=_gather_plan([slot["w_in"], slot["conv_qk"]], [SPLIT["w_in"], None]))
    w_in, conv_w = w_in.reshape(-1, d), _from_chip_blocks(conv_w)
    w_big = jnp.concatenate([w_in[o:o + 512] for o in IN_OFF], axis=0)
    w_small = jnp.concatenate([w_in[IN_GATES[0]:IN_GATES[0] + 8], w_in[IN_GATES[1]:IN_GATES[1] + 8],
                               jnp.zeros((112, d), w_in.dtype)], axis=0)
    u, zbig = _norm_mm("in_big", h1, sp["mix_norm"], w_big, True, BF16)
    zs = _mm_nt("in_small", u, w_small, tm=1024, tk=1024)
    zsr = zs.T
    qk_act = _conv_fwd(zbig, conv_w)
    bm_c, bf_c = sp["b_mlstm_gates"], sp["b_fox_f"]
    y_m, cst, mst = _mlstm_fwd(qk_act, zbig, zs, zsr, bm_c, bm_c.T, sp["mlstm_out_norm"])
    c = _fox_cumsum(zsr, bf_c.T)
    qa, ka, va = _fox_prep(zbig, c.T)
    (y_ft, o_f, lse), late = _fox_fwd2(qa, ka, va, sp["fox_out_norm"],
                                       plan=_gather_plan([slot[n] for n in REST[1:]], [SPLIT[n] for n in REST[1:]]))
    full = dict(zip(REST[1:], late))
    w_out, w_pg = (full[n].reshape(-1, d) for n in ("w_out", "w_ple_gate"))
    wg2, wu2, wd2 = full["ffn2_w_gate"], full["ffn2_w_up"], full["ffn2_w_down"]
    w_pp = _from_chip_blocks(full["w_ple_proj"])
    tm = _pick(t, (1024, 512, 256))
    h2 = _mm("out_proj", [(y_m, (tm, 512), lambda i, j, k: (i, 0), w_out, (512, d), lambda i, j, k: (0, 0)),
                          (y_ft, (tm, 512), lambda i, j, k: (i, 0), w_out, (512, d), lambda i, j, k: (1, 0))],
             (t, d), (tm, d), lambda i, j, k: (i, 0), (t // tm, 1, 1), 2, res=h1)
    (h3, xn2, g2, u2), _ = _ffn_fwd("ffn2", h2, sp["ffn2_norm"], wg2, wu2, wd2)
    hn3, gate_pre = _norm_mm("ple_gate", h3, sp["ple_gate_norm"], w_pg, False, F32)
    pp = _mm_nn("ple_proj", p, w_pp, tm=1024)

    def head_fn(h3_t, gp_t, pp_t, tgt_t, g_pp, g_fin):
        gate = _sigmoid(gp_t)
        ppn = _rms_fwd_val(pp_t, g_pp)
        h4 = h3_t + gate * ppn
        err = _rms_fwd_val(h4, g_fin) - tgt_t
        loss = 0.5 * jnp.sum(jnp.mean(err * err, axis=-1, keepdims=True), axis=0, keepdims=True)
        dh4, dg_fin = _rms_bwd_val(err * (1.0 / d), h4, g_fin)
        dpp, dg_pp = _rms_bwd_val(dh4 * gate, pp_t, g_pp)
        dgp = dh4 * ppn * gate * (1.0 - gate)
        return dh4, dgp, dpp, jnp.broadcast_to(loss, (1, 128)), dg_fin, dg_pp

    dh4, dgp, dpp, loss_part, dg_fin, dg_pp = _rowwise(
        "loss_head", head_fn, [h3, gate_pre, pp, tgt], [sp["ple_proj_norm"], sp["final_norm"]],
        [(d, F32), (d, BF16), (d, BF16)], [((1, 128), F32), ((1, d), F32), ((1, d), F32)])
    gw, gs = {}, {"final_norm": dg_fin, "ple_proj_norm": dg_pp}
    gw["w_ple_gate"] = _mm_tn("d_w_pg", hn3, dgp, tm=1024, tn=1024)
    gw["w_ple_proj"] = _mm_tn("d_w_pp", p, dpp, tn=1024)
    dhn3 = _mm_nt("d_hn3", dgp, w_pg, tm=1024, tn=1024, tk=1024)

    def res_norm_bwd(dn_t, h_t, dres_t, g):
        dx, dg = _rms_bwd_val(dn_t, h_t, g)
        return dres_t + dx, dg

    dh3, gs["ple_gate_norm"] = _rowwise("ple_norm_bwd", res_norm_bwd, [dhn3, h3, dh4], [sp["ple_gate_norm"]],
                                        [(d, F32)], [((1, d), F32)])
    (dh2, gs["ffn2_norm"], gw["ffn2_w_gate"], gw["ffn2_w_up"], gw["ffn2_w_down"]), _, twins2 = _ffn_bwd(
        "ffn2", dh3, h2, sp["ffn2_norm"], xn2, g2, u2, wg2, wu2, wd2)
    dycat = _mm_nt("d_ycat", dh2, w_out, tm=1024, tn=1024, tk=1024)
    gw["w_out"] = jnp.concatenate([_mm_tn("d_w_out_m", y_m, dh2, tn=1024), _mm_tn("d_w_out_f", y_ft, dh2, tn=1024)], axis=0)
    doa, delta, gs["fox_out_norm"] = _fox_bwd_prep(dycat, o_f, sp["fox_out_norm"])
    dq_f, dk_f, dv_f, dct = _fox_bwd_post(*_fox_bwd2(qa, ka, va, doa, lse, delta))
    dfp = _fox_gate_bwd(zsr, bf_c.T, dct)
    dact, dv_m, do_m, dzs_m, dzr_m, gs["mlstm_out_norm"] = _mlstm_bwd(
        qk_act, zbig, zs, zsr, bm_c, bm_c.T, sp["mlstm_out_norm"], cst, mst, dycat)
    dqk, gw["conv_qk"] = _conv_bwd(zbig, dact, conv_w)
    dz_big = jnp.concatenate([dqk, dv_m, do_m, dq_f, dk_f, dv_f], axis=1)
    dzs = dzs_m + jnp.pad(jnp.concatenate([dzr_m, dfp], axis=0).T, ((0, 0), (0, 112)))
    dw_big = _mm_tn("d_w_big", dz_big, u, tn=1024)
    dw_small = _mm_tn("d_w_small", dzs, u, tn=1024)
    gw["w_in"] = jnp.concatenate([dw_big[0:1536], dw_small[0:8], dw_big[1536:3072], dw_small[8:16]], axis=0)
    du_a = _mm_nn("d_u_big", dz_big, w_big, tm=1024, tn=1024, tk=1024)
    du_b = _mm_nn("d_u_small", dzs, w_small, tm=1024, tn=1024)

    def mix_norm_bwd(da_t, db_t, h_t, dres_t, dzs_t, g):
        dx, dg = _rms_bwd_val(da_t + db_t, h_t, g)
        return dres_t + dx, dg, _colsum(dzs_t)

    conv_grad = gw.pop("conv_qk")
    gw["w_ple_proj"] = _chip_blocks(gw["w_ple_proj"])
    for n in ("w_in", "w_out", "w_ple_gate"):
        gw[n] = gw[n].reshape(4, -1, gw[n].shape[-1])
    twins = dict(zip(("ffn2_w_gate", "ffn2_w_up", "ffn2_w_down"), twins2))
    mix = []

    def swap_in_mix_norm_bwd(plan):
        res, swapped = _rowwise("mix_norm_bwd", mix_norm_bwd, [du_a, du_b, h1, dh2, dzs], [sp["mix_norm"]],
                                [(d, F32)], [((1, d), F32), ((1, 128), F32)], plan=plan)
        mix.extend(res)
        return swapped

    part_rest = dict(zip(REST, _rs_partials(REST, gw, c_idx, twins, swap_in_mix_norm_bwd)))
    dh1, gs["mix_norm"], dbias = mix
    gs["b_mlstm_gates"], gs["b_fox_f"] = dbias[:, 0:8], dbias[:, 8:16]
    light = ("w_in", "w_out", "w_ple_gate", "w_ple_proj")
    part_ffn1 = []

    def own_plan(dws, dw_twins):
        part_ffn1.extend(_rs_partials(FFN1, dict(zip(FFN1, dws)), c_idx, dict(zip(FFN1, dw_twins))))
        return _scatter_plan([pb for _, pb in part_ffn1])

    (grad_x, gs["ffn1_norm"], _, _, _), (l_light, l_down, l_gate, l_up, landed_ffn1) = _ffn_bwd_late_dx(
        "ffn1", dh1, x, sp["ffn1_norm"], xn1, g1, u1, wg1, wu1, wd1,
        _scatter_plan([part_rest[n][1] for n in light]),
        [_scatter_plan([part_rest[n][1]]) for n in ("ffn2_w_down", "ffn2_w_gate", "ffn2_w_up")], own_plan)
    landed_rest = dict(zip(light + ("ffn2_w_down", "ffn2_w_gate", "ffn2_w_up"), list(l_light) + [l_down[0], l_gate[0], l_up[0]]))
    names = REST + FFN1
    parts = {**part_rest, **dict(zip(FFN1, part_ffn1))}
    landed = {**landed_rest, **dict(zip(FFN1, landed_ffn1))}
    mine = {}
    for grp in _grouped(names):
        res = _sum4("rs_sum_" + grp[0], [landed[n] for n in grp], [parts[n][0] for n in grp], place, SPLIT[grp[0]])
        mine.update(zip(grp, res))
    grads = dict(zip(names, _join_halves("rs_join", [mine[n] for n in names], [SPLIT[n] for n in names])))
    return loss_part, grad_x, grads, gs, conv_grad


ANY = pl.BlockSpec(memory_space=pl.ANY)
MESH = pl.DeviceIdType.MESH


def _place():
    x, y, c = lax.axis_index("x"), lax.axis_index("y"), lax.axis_index("c")
    chips = [(1 - x, y), (x, 1 - y), (1 - x, 1 - y)]
    return x, y, c, 2 * x + y, (x, y, 1 - c), chips


def _rcopy(src, dst, ssem, rsem, dev):
    return pltpu.make_async_remote_copy(src_ref=src, dst_ref=dst, send_sem=ssem, recv_sem=rsem, device_id=dev,
                                        device_id_type=MESH)


def _half(ref, lead, axis, idx, half):
    return ref.at[(slice(None),) * (lead + axis) + (pl.ds(idx * half, half),)]


def _to_slot(name, arrs, me_idx, dtype):
    n = len(arrs)
    r, cdim = arrs[0].shape
    tr = _pick(r, (352, 256, 176, 128, 64))

    def body(me_ref, *refs):
        for k in range(n):
            refs[n + k][...] = refs[k][...].astype(dtype)

    return pl.pallas_call(
        body, name=name,
        grid_spec=pltpu.PrefetchScalarGridSpec(
            num_scalar_prefetch=1, grid=(r // tr,), in_specs=[pl.BlockSpec((tr, cdim), lambda i, me_ref: (i, 0))] * n,
            out_specs=[pl.BlockSpec((None, tr, cdim), lambda i, me_ref: (me_ref[0], i, 0))] * n),
        out_shape=[jax.ShapeDtypeStruct((4, r, cdim), dtype)] * n, compiler_params=_cparams(("parallel",)),
    )(me_idx, *arrs)


def _gather4(name, bufs, split):
    return _run_plan(name, _gather_plan(bufs, split))


def _gather_plan(bufs, split):
    n = len(bufs)
    shapes = [b.shape[1:] for b in bufs]

    def ctx(outs):
        x, y, c, me, sib, chips = _place()

        def part(ref, a, which):
            if split[a] is None:
                return ref
            return _half(ref, 0, split[a], which, shapes[a][split[a]] // 2)

        return c, me, sib, chips, part

    def ici(outs, sems, a, j, chip, c, me, part):
        mine = part(outs[a].at[me], a, c)
        return _rcopy(mine, mine, sems[0].at[3 * a + j], sems[1].at[3 * a + j], (*chip, c))

    def fwd(outs, sems, a, j, chip, c, sib, part, which):
        blk = part(outs[a].at[2 * chip[0] + chip[1]], a, which)
        return _rcopy(blk, blk, sems[2].at[3 * a + j], sems[3].at[3 * a + j], sib)

    def start(ins, outs, sems):
        c, me, sib, chips, part = ctx(outs)
        for a in range(n):
            for j, chip in enumerate(chips):
                ici(outs, sems, a, j, chip, c, me, part).start()

    def mid(ins, outs, sems):
        c, me, sib, chips, part = ctx(outs)
        for j, chip in enumerate(chips):
            for a in range(n):
                blk = part(outs[a].at[2 * chip[0] + chip[1]], a, c)
                _rcopy(blk, blk, sems[0].at[3 * a + j], sems[1].at[3 * a + j], sib).wait_recv()
                if split[a] is not None:
                    fwd(outs, sems, a, j, chip, c, sib, part, c).start()

    def end(ins, outs, sems):
        c, me, sib, chips, part = ctx(outs)
        for j, chip in enumerate(chips):
            for a in range(n):
                if split[a] is not None:
                    fwd(outs, sems, a, j, chip, c, sib, part, 1 - c).wait_recv()
        for a in range(n):
            for j, chip in enumerate(chips):
                ici(outs, sems, a, j, chip, c, me, part).wait_send()
                if split[a] is not None:
                    fwd(outs, sems, a, j, chip, c, sib, part, c).wait_send()

    return dict(ins=list(bufs), outs=[jax.ShapeDtypeStruct(b.shape, b.dtype) for b in bufs], alias=True,
                sems=[pltpu.SemaphoreType.DMA((3 * n,))] * 4, phases=(start, mid, end))


def _run_plan(name, plan):
    ni, no = len(plan["ins"]), len(plan["outs"])

    def body(*refs):
        ins, outs, sems = refs[:ni], refs[ni:ni + no], refs[ni + no:]
        for phase in plan["phases"]:
            phase(ins, outs, sems)

    return pl.pallas_call(
        body, name=name, in_specs=[ANY] * ni, out_specs=[ANY] * no, out_shape=plan["outs"],
        input_output_aliases={a: a for a in range(ni)} if plan["alias"] else {}, scratch_shapes=plan["sems"],
    )(*plan["ins"])


class _Hosted:
    def __init__(self, plan, n_in, n_out):
        self.plan, self.n_in, self.n_out = plan, n_in, n_out
        self.ni, self.no, self.ns = (len(plan["ins"]) if plan else 0, len(plan["outs"]) if plan else 0,
                                     len(plan["sems"]) if plan else 0)

    def split(self, refs):
        a, b = self.n_in, self.n_in + self.ni
        c, d = b + self.n_out, b + self.n_out + self.no
        e = len(refs) - self.ns
        return refs[:a], refs[b:c], refs[d:e], (refs[a:b], refs[c:d], refs[e:])

    def run(self, k, cond, prefs):
        if self.plan is not None:
            @pl.when(cond)
            def _():
                self.plan["phases"][k](*prefs)

    def call_args(self):
        p = self.plan
        if p is None:
            return dict(in_specs=[], out_specs=[], out_shape=[], scratch=[], aliases={}, args=[])
        al = {self.n_in + a: self.n_out + a for a in range(self.ni)} if p["alias"] else {}
        return dict(in_specs=[ANY] * self.ni, out_specs=[ANY] * self.no, out_shape=list(p["outs"]), scratch=list(p["sems"]),
                    aliases=al, args=list(p["ins"]))


def _swap(name, arrs, halve):
    return _run_plan(name, _swap_plan(arrs, halve))


def _swap_plan(arrs, halve):
    n = len(arrs)

    def half_shape(a, ax):
        return a.shape if ax is None else (a.shape[0],) + tuple(d // 2 if i == ax else d for i, d in enumerate(a.shape[1:]))

    def copies(ins, outs, sems):
        x, y, c, me, sib, chips = _place()
        cps = []
        for a in range(n):
            src = ins[a] if halve[a] is None else _half(ins[a], 1, halve[a], 1 - c, arrs[a].shape[1 + halve[a]] // 2)
            cps.append(_rcopy(src, outs[a], sems[0].at[a], sems[1].at[a], sib))
        return cps

    def start(ins, outs, sems):
        for cp in copies(ins, outs, sems):
            cp.start()

    def mid(ins, outs, sems):
        pass

    def end(ins, outs, sems):
        for cp in copies(ins, outs, sems):
            cp.wait()

    return dict(ins=list(arrs), outs=[jax.ShapeDtypeStruct(half_shape(a, ax), a.dtype) for a, ax in zip(arrs, halve)],
                alias=False, sems=[pltpu.SemaphoreType.DMA((n,))] * 2, phases=(start, mid, end))


def _scatter4(name, arrs):
    return _run_plan(name, _scatter_plan(arrs))


def _scatter_plan(arrs):
    n = len(arrs)

    def send(ins, outs, sems, a, j, chip, c, me):
        return _rcopy(ins[a].at[2 * chip[0] + chip[1]], outs[a].at[me], sems[0].at[3 * a + j], sems[1].at[3 * a + j], (*chip, c))

    def start(ins, outs, sems):
        x, y, c, me, sib, chips = _place()
        for a in range(n):
            for j, chip in enumerate(chips):
                send(ins, outs, sems, a, j, chip, c, me).start()

    def mid(ins, outs, sems):
        pass

    def end(ins, outs, sems):
        x, y, c, me, sib, chips = _place()
        for a in range(n):
            for j, chip in enumerate(chips):
                blk = outs[a].at[2 * chip[0] + chip[1]]
                _rcopy(blk, blk, sems[0].at[3 * a + j], sems[1].at[3 * a + j], sib).wait_recv()
        for a in range(n):
            for j, chip in enumerate(chips):
                send(ins, outs, sems, a, j, chip, c, me).wait_send()

    return dict(ins=list(arrs), outs=[jax.ShapeDtypeStruct(a.shape, a.dtype) for a in arrs], alias=False,
                sems=[pltpu.SemaphoreType.DMA((3 * n,))] * 2, phases=(start, mid, end))


def _join_halves(name, arrs, split):
    n = len(arrs)

    def body(*refs):
        outs = refs[n:2 * n]
        ssem, rsem = refs[2 * n:]
        x, y, c, me, sib, chips = _place()
        cps = []
        for a in range(n):
            mine = _half(outs[a], 0, split[a], c, arrs[a].shape[split[a]] // 2)
            cp = _rcopy(mine, mine, ssem.at[a], rsem.at[a], sib)
            cp.start()
            cps.append(cp)
        for a in range(n):
            blk = _half(outs[a], 0, split[a], 1 - c, arrs[a].shape[split[a]] // 2)
            _rcopy(blk, blk, ssem.at[a], rsem.at[a], sib).wait_recv()
        for cp in cps:
            cp.wait_send()

    return pl.pallas_call(
        body, name=name, in_specs=[ANY] * n, out_specs=[ANY] * n,
        out_shape=[jax.ShapeDtypeStruct(a.shape, a.dtype) for a in arrs],
        input_output_aliases={a: a for a in range(n)}, scratch_shapes=[pltpu.SemaphoreType.DMA((n,))] * 2,
    )(*arrs)


def _allreduce_small(s):
    r, cdim = s.shape

    def body(s_ref, o_ref, buf, ssem, rsem):
        x, y, c, me, sib, chips = _place()
        me8 = 4 * x + 2 * y + c
        buf[me8] = s_ref[...]
        flips = [(fx, fy, fc) for fx in (0, 1) for fy in (0, 1) for fc in (0, 1)][1:]
        cps = []
        for k, (fx, fy, fc) in enumerate(flips):
            peer = (x ^ fx if fx else x, y ^ fy if fy else y, c ^ fc if fc else c)
            cp = _rcopy(s_ref, buf.at[me8], ssem.at[k], rsem.at[k], peer)
            cp.start()
            cps.append(cp)
        for k, (fx, fy, fc) in enumerate(flips):
            src = 4 * (x ^ fx if fx else x) + 2 * (y ^ fy if fy else y) + (c ^ fc if fc else c)
            _rcopy(s_ref, buf.at[src], ssem.at[k], rsem.at[k], sib).wait_recv()
        for cp in cps:
            cp.wait_send()
        acc = buf[0]
        for k in range(1, 8):
            acc = acc + buf[k]
        o_ref[...] = acc

    vm = pl.BlockSpec(memory_space=pltpu.VMEM)
    return pl.pallas_call(
        body, name="allreduce_small", in_specs=[vm], out_specs=vm, out_shape=jax.ShapeDtypeStruct((r, cdim), F32),
        scratch_shapes=[pltpu.VMEM((8, r, cdim), F32), pltpu.SemaphoreType.DMA((7,)), pltpu.SemaphoreType.DMA((7,))],
    )(s)


def _add_my_half(name, gs, recvs, c_idx, axis):
    n = len(gs)
    nb, hr, hc = recvs[0].shape
    tr = _pick(hr, (256, 176, 128, 64))
    if axis == 0:
        g4s = [g.reshape(nb, 2, hr, hc) for g in gs]
        gspec = pl.BlockSpec((None, None, tr, hc), lambda b, i, c_ref: (b, c_ref[0], i, 0))
    else:
        g4s = list(gs)
        gspec = pl.BlockSpec((None, tr, hc), lambda b, i, c_ref: (b, i, c_ref[0]))

    def body(c_ref, *refs):
        for k in range(n):
            s = refs[k][...] + refs[n + k][...].astype(F32)
            refs[2 * n + 2 * k][...] = s
            refs[2 * n + 2 * k + 1][...] = s.astype(BF16)

    ospec = pl.BlockSpec((None, tr, hc), lambda b, i, c_ref: (b, i, 0))
    res = pl.pallas_call(
        body, name=name,
        grid_spec=pltpu.PrefetchScalarGridSpec(
            num_scalar_prefetch=1, grid=(nb, hr // tr), in_specs=[gspec] * n + [ospec] * n, out_specs=[ospec] * (2 * n)),
        out_shape=[jax.ShapeDtypeStruct((nb, hr, hc), F32), jax.ShapeDtypeStruct((nb, hr, hc), BF16)] * n,
        compiler_params=_cparams(("parallel", "parallel")),
    )(c_idx, *g4s, *recvs)
    return [(res[2 * k], res[2 * k + 1]) for k in range(n)]


def _sum4(name, landeds, owns, place, axis):
    n = len(landeds)
    nb, h, cdim = landeds[0].shape
    tr = _pick(h, (256, 176, 128, 64))
    nt = h // tr

    def body(p_ref, *refs):
        for k in range(n):
            a1, a2, a3, own = refs[4 * k:4 * k + 4]
            refs[4 * n + k][...] = ((own[...] + a1[...].astype(F32)) + a2[...].astype(F32)) + a3[...].astype(F32)

    def nxt(k):
        return pl.BlockSpec((None, tr, cdim), lambda i, p_ref: ((p_ref[0] + k) % nb, i, 0))

    if axis == 0:
        ospec = pl.BlockSpec((tr, cdim), lambda i, p_ref: (p_ref[1] * nt + i, 0))
        oshape = (2 * h, cdim)
    else:
        ospec = pl.BlockSpec((tr, cdim), lambda i, p_ref: (i, p_ref[1]))
        oshape = (h, 2 * cdim)
    args = []
    for landed, own in zip(landeds, owns):
        args += [landed, landed, landed, own]
    return pl.pallas_call(
        body, name=name,
        grid_spec=pltpu.PrefetchScalarGridSpec(
            num_scalar_prefetch=1, grid=(nt,), in_specs=[nxt(1), nxt(2), nxt(3), nxt(0)] * n, out_specs=[ospec] * n),
        out_shape=[jax.ShapeDtypeStruct(oshape, F32)] * n, compiler_params=_cparams(("parallel",)),
    )(place, *args)


def _cast_other_half(name, g, c_idx, axis):
    nb, r, cdim = g.shape
    hr, hc = (r // 2, cdim) if axis == 0 else (r, cdim // 2)
    tr = _pick(hr, (256, 176, 128, 64))
    if axis == 0:
        g4 = g.reshape(nb, 2, hr, hc)
        gspec = pl.BlockSpec((None, None, tr, hc), lambda b, i, c_ref: (b, 1 - c_ref[0], i, 0))
    else:
        g4 = g
        gspec = pl.BlockSpec((None, tr, hc), lambda b, i, c_ref: (b, i, 1 - c_ref[0]))

    def body(c_ref, g_ref, o_ref):
        o_ref[...] = g_ref[...].astype(BF16)

    return pl.pallas_call(
        body, name=name,
        grid_spec=pltpu.PrefetchScalarGridSpec(
            num_scalar_prefetch=1, grid=(nb, hr // tr), in_specs=[gspec],
            out_specs=pl.BlockSpec((None, tr, hc), lambda b, i, c_ref: (b, i, 0))),
        out_shape=jax.ShapeDtypeStruct((nb, hr, hc), BF16), compiler_params=_cparams(("parallel", "parallel")),
    )(c_idx, g4)


def _adamw(name, ws, gs, ms, vs):
    n = len(ws)
    c1 = 1.0 - ADAM_B1 ** ADAM_STEP
    c2 = 1.0 - ADAM_B2 ** ADAM_STEP

    def fn(*tiles):
        out = []
        for k in range(n):
            w_t, g_t, m_t, v_t = tiles[4 * k:4 * k + 4]
            m_n = ADAM_B1 * m_t + (1.0 - ADAM_B1) * g_t
            v_n = ADAM_B2 * v_t + (1.0 - ADAM_B2) * (g_t * g_t)
            out += [-ADAM_LR * ((m_n / c1) / (jnp.sqrt(v_n / c2) + ADAM_EPS) + ADAM_WD * w_t), m_n, v_n]
        return out

    rows, cdim = ws[0].shape
    tiled = [a for quad in zip(ws, gs, ms, vs) for a in quad]
    pref = (512, 352, 256, 128, 64, 8) if n == 1 else (176, 128, 64, 8)
    res = _rowwise(name, fn, tiled, [], [(cdim, F32)] * (3 * n), tm=_pick(rows, pref))
    return [tuple(res[3 * k:3 * k + 3]) for k in range(n)]


BIG = ("ffn1_w_gate", "ffn1_w_up", "ffn1_w_down", "w_in", "w_out", "ffn2_w_gate", "ffn2_w_up", "ffn2_w_down",
       "w_ple_gate", "w_ple_proj")
SMALL = ("ffn1_norm", "mix_norm", "b_mlstm_gates", "b_fox_f", "mlstm_out_norm", "fox_out_norm", "ffn2_norm",
         "ple_gate_norm", "ple_proj_norm", "final_norm")
WEIGHTS = ("ffn1_norm", "ffn1_w_gate", "ffn1_w_up", "ffn1_w_down", "mix_norm", "w_in", "conv_qk", "b_mlstm_gates",
           "b_fox_f", "mlstm_out_norm", "fox_out_norm", "w_out", "ffn2_norm", "ffn2_w_gate", "ffn2_w_up", "ffn2_w_down",
           "ple_gate_norm", "w_ple_gate", "w_ple_proj", "ple_proj_norm", "final_norm")
TRANSPOSED = ("ffn1_w_gate", "ffn1_w_up", "w_in", "ffn2_w_gate", "ffn2_w_up")
PACK_W = 1024


def _chip_blocks(a):
    r, c4 = a.shape
    return a.reshape(r, 4, c4 // 4).transpose(1, 0, 2)


def _from_chip_blocks(a):
    nb, r, c = a.shape
    return a.transpose(1, 0, 2).reshape(r, nb * c)


def kernel(x, p, ffn1_norm, ffn1_w_gate, ffn1_w_up, ffn1_w_down, mix_norm, w_in, conv_qk, b_mlstm_gates, b_fox_f, mlstm_out_norm, fox_out_norm, w_out, ffn2_norm, ffn2_w_gate, ffn2_w_up, ffn2_w_down, ple_gate_norm, w_ple_gate, w_ple_proj, ple_proj_norm, final_norm, loss_target, m_ffn1_norm, m_ffn1_w_gate, m_ffn1_w_up, m_ffn1_w_down, m_mix_norm, m_w_in, m_conv_qk, m_b_mlstm_gates, m_b_fox_f, m_mlstm_out_norm, m_fox_out_norm, m_w_out, m_ffn2_norm, m_ffn2_w_gate, m_ffn2_w_up, m_ffn2_w_down, m_ple_gate_norm, m_w_ple_gate, m_w_ple_proj, m_ple_proj_norm, m_final_norm, v_ffn1_norm, v_ffn1_w_gate, v_ffn1_w_up, v_ffn1_w_down, v_mix_norm, v_w_in, v_conv_qk, v_b_mlstm_gates, v_b_fox_f, v_mlstm_out_norm, v_fox_out_norm, v_w_out, v_ffn2_norm, v_ffn2_w_gate, v_ffn2_w_up, v_ffn2_w_down, v_ple_gate_norm, v_w_ple_gate, v_w_ple_proj, v_ple_proj_norm, v_final_norm):
    w = dict(ffn1_norm=ffn1_norm, ffn1_w_gate=ffn1_w_gate, ffn1_w_up=ffn1_w_up, ffn1_w_down=ffn1_w_down, mix_norm=mix_norm,
             w_in=w_in, conv_qk=conv_qk, b_mlstm_gates=b_mlstm_gates, b_fox_f=b_fox_f, mlstm_out_norm=mlstm_out_norm,
             fox_out_norm=fox_out_norm, w_out=w_out, ffn2_norm=ffn2_norm, ffn2_w_gate=ffn2_w_gate, ffn2_w_up=ffn2_w_up,
             ffn2_w_down=ffn2_w_down, ple_gate_norm=ple_gate_norm, w_ple_gate=w_ple_gate, w_ple_proj=w_ple_proj,
             ple_proj_norm=ple_proj_norm, final_norm=final_norm)
    m = dict(ffn1_norm=m_ffn1_norm, ffn1_w_gate=m_ffn1_w_gate, ffn1_w_up=m_ffn1_w_up, ffn1_w_down=m_ffn1_w_down,
             mix_norm=m_mix_norm, w_in=m_w_in, conv_qk=m_conv_qk, b_mlstm_gates=m_b_mlstm_gates, b_fox_f=m_b_fox_f,
             mlstm_out_norm=m_mlstm_out_norm, fox_out_norm=m_fox_out_norm, w_out=m_w_out, ffn2_norm=m_ffn2_norm,
             ffn2_w_gate=m_ffn2_w_gate, ffn2_w_up=m_ffn2_w_up, ffn2_w_down=m_ffn2_w_down, ple_gate_norm=m_ple_gate_norm,
             w_ple_gate=m_w_ple_gate, w_ple_proj=m_w_ple_proj, ple_proj_norm=m_ple_proj_norm, final_norm=m_final_norm)
    v = dict(ffn1_norm=v_ffn1_norm, ffn1_w_gate=v_ffn1_w_gate, ffn1_w_up=v_ffn1_w_up, ffn1_w_down=v_ffn1_w_down,
             mix_norm=v_mix_norm, w_in=v_w_in, conv_qk=v_conv_qk, b_mlstm_gates=v_b_mlstm_gates, b_fox_f=v_b_fox_f,
             mlstm_out_norm=v_mlstm_out_norm, fox_out_norm=v_fox_out_norm, w_out=v_w_out, ffn2_norm=v_ffn2_norm,
             ffn2_w_gate=v_ffn2_w_gate, ffn2_w_up=v_ffn2_w_up, ffn2_w_down=v_ffn2_w_down, ple_gate_norm=v_ple_gate_norm,
             w_ple_gate=v_w_ple_gate, w_ple_proj=v_w_ple_proj, ple_proj_norm=v_ple_proj_norm, final_norm=v_final_norm)
    shapes = {n: w[n].shape for n in WEIGHTS}

    def view(a, n):
        return a[0].T if n in TRANSPOSED else a.reshape(-1, a.shape[-1])

    def unview(a, n):
        return (a.T if n in TRANSPOSED else a).reshape(shapes[n])

    w2, m2, v2 = ({n: view(a, n) for n, a in d.items()} for d in (w, m, v))

    c_idx = lax.axis_index("c").astype(jnp.int32).reshape(1)
    me_idx = (2 * lax.axis_index("x") + lax.axis_index("y")).astype(jnp.int32).reshape(1)
    place = jnp.concatenate([me_idx, c_idx])
    slot = {}
    for grp in SAME_SHAPE:
        slot.update(zip(grp, _to_slot("slot_" + grp[0], [w2[n] for n in grp], me_idx, BF16)))
    slot["conv_qk"] = _to_slot("slot_conv_qk", [w2["conv_qk"]], me_idx, F32)[0]
    wg1, wu1, wd1 = _gather4("gather_ffn1", [slot[n] for n in FFN1], [SPLIT[n] for n in FFN1])
    sp = {n: w2[n] for n in SMALL}
    loss_part, grad_x, grads, gs, conv_grad = _local_step(
        x[0], p[0, 0], loss_target[0], sp, wg1, wu1, wd1, [slot[n] for n in REST + ("conv_qk",)], c_idx, place)

    small = [gs[n].reshape(1, -1) for n in SMALL] + [conv_grad, loss_part]
    rows = [jnp.pad(a, ((0, 0), (0, PACK_W - a.shape[1]))) for a in small]
    packed = jnp.concatenate(rows, axis=0)
    packed = jnp.pad(packed, ((0, -packed.shape[0] % 8), (0, 0)))
    red = _allreduce_small(packed)
    loss = red[len(SMALL) + CONV_W, 0]
    for i, n in enumerate(SMALL):
        grads[n] = red[i:i + 1, :gs[n].size]
    dconv = red[len(SMALL):len(SMALL) + CONV_W, :conv_grad.shape[1]]
    cw = conv_qk.shape[-1]
    grads["conv_qk"] = lax.dynamic_slice_in_dim(dconv, (2 * lax.axis_index("x") + lax.axis_index("y")) * cw, cw, axis=1)

    outs = {}
    for grp in SAME_SHAPE + tuple((n,) for n in WEIGHTS if n not in BIG):
        g2s = [grads[n].reshape(w2[n].shape) for n in grp]
        res = _adamw("adamw_" + grp[0], [w2[n] for n in grp], g2s, [m2[n] for n in grp], [v2[n] for n in grp])
        for n, g2, (d, nm, nv) in zip(grp, g2s, res):
            outs[n] = tuple(unview(a, n) for a in (g2, d, nm, nv))
    return (loss, grad_x[None], *[outs[n][0] for n in WEIGHTS], *[outs[n][1] for n in WEIGHTS],
            *[outs[n][2] for n in WEIGHTS], *[outs[n][3] for n in WEIGHTS])
```

```python
import jax
import jax.numpy as jnp
from jax import lax
from jax.experimental import pallas as pl
from jax.experimental.pallas import tpu as pltpu

F32 = jnp.float32
BF16 = jnp.bfloat16
EPS = 1e-6
NH_M, DK_M, DV_M = 4, 64, 128
NH_F, DH_F = 8, 64
CONV_W = 4
ADAM_LR, ADAM_B1, ADAM_B2, ADAM_EPS, ADAM_WD, ADAM_STEP = 0.001, 0.9, 0.999, 1e-08, 0.01, 10
VMEM_LIMIT = 56 * 1024 * 1024


def _cparams(sem):
    return pltpu.CompilerParams(dimension_semantics=sem, vmem_limit_bytes=VMEM_LIMIT)


def _sigmoid(x):
    return 1.0 / (1.0 + jnp.exp(-x))


def _dot(a, b, ca, cb):
    return lax.dot_general(a.astype(BF16), b.astype(BF16), (((ca,), (cb,)), ((), ())), preferred_element_type=F32)


def _rowwise(name, fn, tiled, full, outs, accs=(), tm=512, plan=None):
    rows = tiled[0].shape[0]
    tm = min(tm, rows)
    assert rows % tm == 0
    n_t, n_f, n_o, n_a = len(tiled), len(full), len(outs), len(accs)
    nt = rows // tm
    host = _Hosted(plan, n_t + n_f, n_o + n_a)

    def body(*refs):
        in_refs, orefs, _, prefs = host.split(refs)
        host.run(0, pl.program_id(0) == 0, prefs)
        host.run(1, pl.program_id(0) == 0, prefs)
        ins = [r[...] for r in in_refs]
        res = fn(*ins)
        if not isinstance(res, (tuple, list)):
            res = (res,)
        for r, v in zip(orefs[:n_o], res[:n_o]):
            r[...] = v.astype(r.dtype)
        if n_a:
            @pl.when(pl.program_id(0) == 0)
            def _():
                for r in orefs[n_o:]:
                    r[...] = jnp.zeros_like(r)
            for r, v in zip(orefs[n_o:], res[n_o:]):
                r[...] += v.astype(r.dtype)
        host.run(2, pl.program_id(0) == nt - 1, prefs)

    in_specs = [pl.BlockSpec((tm, a.shape[1]), lambda i: (i, 0)) for a in tiled]
    in_specs += [pl.BlockSpec(a.shape, lambda i: (0, 0)) for a in full]
    out_specs = [pl.BlockSpec((tm, c), lambda i: (i, 0)) for c, _ in outs]
    out_specs += [pl.BlockSpec(s, lambda i: (0, 0)) for s, _ in accs]
    out_shape = [jax.ShapeDtypeStruct((rows, c), d) for c, d in outs]
    out_shape += [jax.ShapeDtypeStruct(s, d) for s, d in accs]
    hc = host.call_args()
    res = pl.pallas_call(
        body, name=name, grid=(nt,), in_specs=in_specs + hc["in_specs"], out_specs=out_specs + hc["out_specs"],
        out_shape=out_shape + hc["out_shape"], scratch_shapes=hc["scratch"], input_output_aliases=hc["aliases"],
        compiler_params=_cparams(("arbitrary",) if (n_a or plan is not None) else ("parallel",)),
    )(*tiled, *full, *hc["args"])
    return res if plan is None else (res[:n_o + n_a], res[n_o + n_a:])


def _colsum(v):
    return jnp.sum(v, axis=0, keepdims=True)


def _rms_fwd_val(x, g):
    r = lax.rsqrt(jnp.mean(x * x, axis=-1, keepdims=True) + EPS)
    return x * r * g


def _rms_bwd_val(dy, x, g):
    r = lax.rsqrt(jnp.mean(x * x, axis=-1, keepdims=True) + EPS)
    xh = x * r
    dxh = dy * g
    dx = r * (dxh - xh * jnp.mean(dxh * xh, axis=-1, keepdims=True))
    return dx, _colsum(dy * xh)


def _mm(name, pairs, out_shape, out_block, out_map, grid, kaxis, ta=False, tb=False, scale=None, res=None,
        out_dtype=F32, plan=None, twin=False):
    n_o = 2 if twin else 1
    nk = grid[kaxis]
    npairs = len(pairs)
    ca, cb = (0 if ta else 1), (1 if tb else 0)
    acc_shape = tuple(d for d in out_block if d is not None)
    n_in = 2 * npairs + (1 if res is not None else 0)
    host = _Hosted(plan, n_in, n_o)

    def body(*refs):
        ins, o_refs, (acc_ref,), prefs = host.split(refs)
        o_ref = o_refs[0]
        in_refs = ins[: 2 * npairs]
        res_ref = ins[2 * npairs] if res is not None else None
        k = pl.program_id(kaxis)
        ids = [pl.program_id(a) for a in range(len(grid))]
        first, last = ids[0] == 0, ids[0] == grid[0] - 1
        for a in range(1, len(grid)):
            first, last = first & (ids[a] == 0), last & (ids[a] == grid[a] - 1)
        host.run(0, first, prefs)
        host.run(1, first, prefs)

        @pl.when(k == 0)
        def _():
            acc_ref[...] = jnp.zeros_like(acc_ref)

        part = None
        for p in range(npairs):
            d = _dot(in_refs[2 * p][...], in_refs[2 * p + 1][...], ca, cb)
            part = d if part is None else part + d
        acc_ref[...] += part

        @pl.when(k == nk - 1)
        def _():
            v = acc_ref[...]
            if scale is not None:
                v = v * scale
            if res_ref is not None:
                v = v + res_ref[...].astype(F32)
            o_ref[...] = v.astype(o_ref.dtype)
            if twin:
                o_refs[1][...] = v.astype(BF16)

        host.run(2, last, prefs)

    in_specs, args = [], []
    for a, ab, am, b, bb, bm in pairs:
        in_specs += [pl.BlockSpec(ab, am), pl.BlockSpec(bb, bm)]
        args += [a, b]
    if res is not None:
        in_specs.append(pl.BlockSpec(out_block, out_map))
        args.append(res)
    sem = tuple("arbitrary" if (i == kaxis or plan is not None) else "parallel" for i in range(len(grid)))
    hc = host.call_args()
    out = pl.pallas_call(
        body, name=name, grid=grid, in_specs=in_specs + hc["in_specs"],
        out_specs=[pl.BlockSpec(out_block, out_map)] * n_o + hc["out_specs"],
        out_shape=[jax.ShapeDtypeStruct(out_shape, out_dtype)] + [jax.ShapeDtypeStruct(out_shape, BF16)] * (n_o - 1)
        + hc["out_shape"],
        scratch_shapes=[pltpu.VMEM(acc_shape, F32)] + hc["scratch"], input_output_aliases=hc["aliases"],
        compiler_params=_cparams(sem),
    )(*args, *hc["args"])
    res_out = tuple(out[:2]) if twin else out[0]
    return res_out if plan is None else (res_out, out[n_o:])


def _pick(n, pref):
    for t in pref:
        if n % t == 0:
            return t
    return n


def _mm_nn(name, a, b, tm=512, tn=512, tk=512, **kw):
    (m, k), n = a.shape, b.shape[1]
    tm, tn, tk = _pick(m, (tm, 256, 128)), _pick(n, (tn, 256, 128)), _pick(k, (tk, 256, 128))
    return _mm(name, [(a, (tm, tk), lambda i, j, kk: (i, kk), b, (tk, tn), lambda i, j, kk: (kk, j))],
               (m, n), (tm, tn), lambda i, j, kk: (i, j), (m // tm, n // tn, k // tk), 2, **kw)


def _mm_nt(name, a, b, tm=512, tn=512, tk=512, **kw):
    (m, k), n = a.shape, b.shape[0]
    tm, tn, tk = _pick(m, (tm, 256, 128)), _pick(n, (tn, 256, 128)), _pick(k, (tk, 256, 128))
    return _mm(name, [(a, (tm, tk), lambda i, j, kk: (i, kk), b, (tn, tk), lambda i, j, kk: (j, kk))],
               (m, n), (tm, tn), lambda i, j, kk: (i, j), (m // tm, n // tn, k // tk), 2, tb=True, **kw)


def _mm_tn(name, a, b, tm=512, tn=512, tk=4096, **kw):
    (k, m), n = a.shape, b.shape[1]
    tm, tn, tk = _pick(m, (tm, 256, 128)), _pick(n, (tn, 256, 128)), _pick(k, (tk, 2048, 1024, 512, 256, 128))
    return _mm(name, [(a, (tk, tm), lambda i, j, kk: (kk, i), b, (tk, tn), lambda i, j, kk: (kk, j))],
               (m, n), (tm, tn), lambda i, j, kk: (i, j), (m // tm, n // tn, k // tk), 2, ta=True, **kw)


def _norm_mm(name, h, gamma, w, w_transposed, out_dtype):
    t, d = h.shape
    n = w.shape[0] if w_transposed else w.shape[1]
    tm, tn = _pick(t, (512, 256)), _pick(n, (1024, 512, 256, 128))

    def body(h_ref, gam_ref, w_ref, xn_ref, o_ref, xn_scr):
        @pl.when(pl.program_id(1) == 0)
        def _():
            xn = _rms_fwd_val(h_ref[...], gam_ref[...]).astype(BF16)
            xn_scr[...] = xn
            xn_ref[...] = xn

        o_ref[...] = _dot(xn_scr[...], w_ref[...], 1, 1 if w_transposed else 0).astype(o_ref.dtype)

    wspec = pl.BlockSpec((tn, d), lambda i, j: (j, 0)) if w_transposed else pl.BlockSpec((d, tn), lambda i, j: (0, j))
    return pl.pallas_call(
        body, name=name, grid=(t // tm, n // tn),
        in_specs=[pl.BlockSpec((tm, d), lambda i, j: (i, 0)), pl.BlockSpec((1, d), lambda i, j: (0, 0)), wspec],
        out_specs=[pl.BlockSpec((tm, d), lambda i, j: (i, 0)), pl.BlockSpec((tm, tn), lambda i, j: (i, j))],
        out_shape=[jax.ShapeDtypeStruct((t, d), BF16), jax.ShapeDtypeStruct((t, n), out_dtype)],
        scratch_shapes=[pltpu.VMEM((tm, d), BF16)], compiler_params=_cparams(("parallel", "arbitrary")),
    )(h, gamma, w)


CHAIN_ROWS = 256


def _row_chains(tm):
    n = max(tm // CHAIN_ROWS, 1)
    return [slice(r * (tm // n), (r + 1) * (tm // n)) for r in range(n)]


def _ffn_fwd(pfx, h, gamma, wg, wu, wd, plan=None):
    t, d = h.shape
    nb, f, _ = wg.shape
    tm = _pick(t, (1024, 512, 256))
    nt = t // tm
    host = _Hosted(plan, 5, 4)

    def body(*refs):
        (h_ref, gam_ref, wg_ref, wu_ref, wd_ref), (ho_ref, xn_ref, g_ref, u_ref), (xn_scr, acc_ref), prefs = host.split(refs)
        i, j = pl.program_id(0), pl.program_id(1)
        host.run(0, (i == 0) & (j == 0), prefs)
        host.run(1, (i == nt // 2) & (j == 0), prefs)

        @pl.when(j == 0)
        def _():
            xn = _rms_fwd_val(h_ref[...], gam_ref[...]).astype(BF16)
            xn_scr[...] = xn
            xn_ref[...] = xn
            acc_ref[...] = jnp.zeros_like(acc_ref)

        for rows in _row_chains(tm):
            x = xn_scr[rows, :]
            g = _dot(x, wg_ref[...], 1, 1)
            u = _dot(x, wu_ref[...], 1, 1)
            g_ref[rows, :] = g.astype(BF16)
            u_ref[rows, :] = u.astype(BF16)
            acc_ref[rows, :] += _dot(g * _sigmoid(g) * u, wd_ref[...], 1, 0)

        @pl.when(j == nb - 1)
        def _():
            ho_ref[...] = h_ref[...] + 0.5 * acc_ref[...]

        host.run(2, (i == nt - 1) & (j == nb - 1), prefs)

    row = pl.BlockSpec((tm, d), lambda i, j: (i, 0))
    blk = pl.BlockSpec((None, tm, f), lambda i, j: (j, i, 0))
    wspec = pl.BlockSpec((None, f, d), lambda i, j: (j, 0, 0))
    hc = host.call_args()
    res = pl.pallas_call(
        body, name=pfx + "_fwd", grid=(nt, nb),
        in_specs=[row, pl.BlockSpec((1, d), lambda i, j: (0, 0)), wspec, wspec, wspec] + hc["in_specs"],
        out_specs=[row, row, blk, blk] + hc["out_specs"],
        out_shape=[jax.ShapeDtypeStruct((t, d), F32), jax.ShapeDtypeStruct((t, d), BF16),
                   jax.ShapeDtypeStruct((nb, t, f), BF16), jax.ShapeDtypeStruct((nb, t, f), BF16)] + hc["out_shape"],
        scratch_shapes=[pltpu.VMEM((tm, d), BF16), pltpu.VMEM((tm, d), F32)] + hc["scratch"],
        input_output_aliases=hc["aliases"], compiler_params=_cparams(("arbitrary", "arbitrary")),
    )(h, gamma, wg, wu, wd, *hc["args"])
    return res[:4], res[4:]


def _ffn_bwd(pfx, dh_out, h, gamma, xn, g_all, u_all, wg, wu, wd, plan=None):
    t, d = h.shape
    nb, f, _ = wg.shape
    tm = _pick(t, (512, 256))
    tk = _pick(t, (4096, 2048, 1024, 512, 256))

    nt = t // tm
    host = _Hosted(plan, 8, 6)

    def body(*refs):
        ((dy_ref, h_ref, gam_ref, wg_ref, wu_ref, wd_ref, g_ref, u_ref),
         (dh_ref, dgam_ref, dg_ref, du_ref, a_ref, dyb_ref), (acc_ref,), prefs) = host.split(refs)
        i, j = pl.program_id(0), pl.program_id(1)
        host.run(0, (i == 0) & (j == 0), prefs)
        host.run(1, (i == nt // 2) & (j == 0), prefs)

        @pl.when((i == 0) & (j == 0))
        def _():
            dgam_ref[...] = jnp.zeros_like(dgam_ref)

        @pl.when(j == 0)
        def _():
            acc_ref[...] = jnp.zeros_like(acc_ref)
            dyb_ref[...] = dy_ref[...].astype(BF16)

        for rows in _row_chains(tm):
            da = _dot(dy_ref[rows, :], wd_ref[...], 1, 1) * 0.5
            g = g_ref[rows, :].astype(F32)
            u = u_ref[rows, :].astype(F32)
            s = _sigmoid(g)
            sl = g * s
            du = (da * sl).astype(BF16)
            dg = (da * u * (s + sl * (1.0 - s))).astype(BF16)
            du_ref[rows, :] = du
            dg_ref[rows, :] = dg
            a_ref[rows, :] = (sl * u).astype(BF16)
            acc_ref[rows, :] += _dot(dg, wg_ref[...], 1, 0) + _dot(du, wu_ref[...], 1, 0)

        @pl.when(j == nb - 1)
        def _():
            dx, dgam = _rms_bwd_val(acc_ref[...], h_ref[...], gam_ref[...])
            dh_ref[...] = dy_ref[...] + dx
            dgam_ref[...] += dgam

        host.run(2, (i == nt - 1) & (j == nb - 1), prefs)

    row = pl.BlockSpec((tm, d), lambda i, j: (i, 0))
    vec = pl.BlockSpec((1, d), lambda i, j: (0, 0))
    blk = pl.BlockSpec((None, tm, f), lambda i, j: (j, i, 0))
    wspec = pl.BlockSpec((None, f, d), lambda i, j: (j, 0, 0))
    hc = host.call_args()
    res = pl.pallas_call(
        body, name=pfx + "_bwd", grid=(nt, nb),
        in_specs=[row, row, vec, wspec, wspec, wspec, blk, blk] + hc["in_specs"],
        out_specs=[row, vec, blk, blk, blk, row] + hc["out_specs"],
        out_shape=[jax.ShapeDtypeStruct((t, d), F32), jax.ShapeDtypeStruct((1, d), F32)]
        + [jax.ShapeDtypeStruct((nb, t, f), BF16)] * 3 + [jax.ShapeDtypeStruct((t, d), BF16)] + hc["out_shape"],
        scratch_shapes=[pltpu.VMEM((tm, d), F32)] + hc["scratch"], input_output_aliases=hc["aliases"],
        compiler_params=_cparams(("arbitrary", "arbitrary")),
    )(dh_out, h, gamma, wg, wu, wd, g_all, u_all, *hc["args"])
    dh, dgamma, dg_all, du_all, a_all, dyb = res[:6]

    xmap, bmap, omap = (lambda b, k: (k, 0)), (lambda b, k: (b, k, 0)), (lambda b, k: (b, 0, 0))
    dwg, tg = _mm(pfx + "_dwg", [(dg_all, (None, tk, f), bmap, xn, (tk, d), xmap)], (nb, f, d), (None, f, d), omap,
                  (nb, t // tk), 1, ta=True, twin=True)
    dwu, tu = _mm(pfx + "_dwu", [(du_all, (None, tk, f), bmap, xn, (tk, d), xmap)], (nb, f, d), (None, f, d), omap,
                  (nb, t // tk), 1, ta=True, twin=True)
    dwd, td = _mm(pfx + "_dwd", [(a_all, (None, tk, f), bmap, dyb, (tk, d), xmap)], (nb, f, d), (None, f, d), omap,
                  (nb, t // tk), 1, ta=True, scale=0.5, twin=True)
    return (dh, dgamma, dwg, dwu, dwd), res[6:], (tg, tu, td)


def _ffn_bwd_late_dx(pfx, dh_out, h, gamma, xn, g_all, u_all, wg, wu, wd, plan_gu, plans_dw, make_plan_dx):
    t, d = h.shape
    nb, f, _ = wg.shape
    tm = _pick(t, (512, 256))
    tk = _pick(t, (4096, 2048, 1024, 512, 256))
    nt = t // tm
    host_a = _Hosted(plan_gu, 4, 4)

    def body_a(*refs):
        (dy_ref, wd_ref, g_ref, u_ref), (dg_ref, du_ref, a_ref, dyb_ref), _, prefs = host_a.split(refs)
        i, j = pl.program_id(0), pl.program_id(1)
        host_a.run(0, (i == 0) & (j == 0), prefs)
        host_a.run(1, (i == 0) & (j == 0), prefs)

        @pl.when(j == 0)
        def _():
            dyb_ref[...] = dy_ref[...].astype(BF16)

        for rows in _row_chains(tm):
            da = _dot(dy_ref[rows, :], wd_ref[...], 1, 1) * 0.5
            g = g_ref[rows, :].astype(F32)
            u = u_ref[rows, :].astype(F32)
            s = _sigmoid(g)
            sl = g * s
            du_ref[rows, :] = (da * sl).astype(BF16)
            dg_ref[rows, :] = (da * u * (s + sl * (1.0 - s))).astype(BF16)
            a_ref[rows, :] = (sl * u).astype(BF16)
        host_a.run(2, (i == nt - 1) & (j == nb - 1), prefs)

    row = pl.BlockSpec((tm, d), lambda i, j: (i, 0))
    vec = pl.BlockSpec((1, d), lambda i, j: (0, 0))
    blk = pl.BlockSpec((None, tm, f), lambda i, j: (j, i, 0))
    wspec = pl.BlockSpec((None, f, d), lambda i, j: (j, 0, 0))
    hc = host_a.call_args()
    res_a = pl.pallas_call(
        body_a, name=pfx + "_bwd_gu", grid=(nt, nb), in_specs=[row, wspec, blk, blk] + hc["in_specs"],
        out_specs=[blk] * 3 + [row] + hc["out_specs"],
        out_shape=[jax.ShapeDtypeStruct((nb, t, f), BF16)] * 3 + [jax.ShapeDtypeStruct((t, d), BF16)] + hc["out_shape"],
        scratch_shapes=hc["scratch"], input_output_aliases=hc["aliases"], compiler_params=_cparams(("arbitrary", "arbitrary")),
    )(dh_out, wd, g_all, u_all, *hc["args"])
    dg_all, du_all, a_all, dyb = res_a[:4]

    xmap, bmap, omap = (lambda b, k: (k, 0)), (lambda b, k: (b, k, 0)), (lambda b, k: (b, 0, 0))
    (dwd, td), out_d = _mm(pfx + "_dwd", [(a_all, (None, tk, f), bmap, dyb, (tk, d), xmap)], (nb, f, d), (None, f, d),
                           omap, (nb, t // tk), 1, ta=True, scale=0.5, plan=plans_dw[0], twin=True)
    (dwg, tg), out_g = _mm(pfx + "_dwg", [(dg_all, (None, tk, f), bmap, xn, (tk, d), xmap)], (nb, f, d), (None, f, d),
                           omap, (nb, t // tk), 1, ta=True, plan=plans_dw[1], twin=True)
    (dwu, tu), out_u = _mm(pfx + "_dwu", [(du_all, (None, tk, f), bmap, xn, (tk, d), xmap)], (nb, f, d), (None, f, d),
                           omap, (nb, t // tk), 1, ta=True, plan=plans_dw[2], twin=True)

    plan_dx = make_plan_dx((dwg, dwu, dwd), (tg, tu, td))
    host_b = _Hosted(plan_dx, 7, 2)

    def body_b(*refs):
        (dy_ref, h_ref, gam_ref, wg_ref, wu_ref, dg_ref, du_ref), (dh_ref, dgam_ref), (acc_ref,), prefs = host_b.split(refs)
        i, j = pl.program_id(0), pl.program_id(1)
        host_b.run(0, (i == 0) & (j == 0), prefs)
        host_b.run(1, (i == 0) & (j == 0), prefs)

        @pl.when((i == 0) & (j == 0))
        def _():
            dgam_ref[...] = jnp.zeros_like(dgam_ref)

        @pl.when(j == 0)
        def _():
            acc_ref[...] = jnp.zeros_like(acc_ref)

        acc_ref[...] += _dot(dg_ref[...], wg_ref[...], 1, 0) + _dot(du_ref[...], wu_ref[...], 1, 0)

        @pl.when(j == nb - 1)
        def _():
            dx, dgam = _rms_bwd_val(acc_ref[...], h_ref[...], gam_ref[...])
            dh_ref[...] = dy_ref[...] + dx
            dgam_ref[...] += dgam

        host_b.run(2, (i == nt - 1) & (j == nb - 1), prefs)

    hc = host_b.call_args()
    res_b = pl.pallas_call(
        body_b, name=pfx + "_bwd_dx", grid=(nt, nb), in_specs=[row, row, vec, wspec, wspec, blk, blk] + hc["in_specs"],
        out_specs=[row, vec] + hc["out_specs"],
        out_shape=[jax.ShapeDtypeStruct((t, d), F32), jax.ShapeDtypeStruct((1, d), F32)] + hc["out_shape"],
        scratch_shapes=[pltpu.VMEM((tm, d), F32)] + hc["scratch"], input_output_aliases=hc["aliases"],
        compiler_params=_cparams(("arbitrary", "arbitrary")),
    )(dh_out, h, gamma, wg, wu, dg_all, du_all, *hc["args"])
    return (res_b[0], res_b[1], dwg, dwu, dwd), (res_a[4:], out_d, out_g, out_u, res_b[2:])


HALO = 16


def _silu_grad(y):
    s = _sigmoid(y)
    return s * (1.0 + y * (1.0 - s))


def _with_halo(ref, i, n_tiles, tm, before, after):
    t = ref.shape[0]
    r0 = pl.multiple_of(i * tm, tm)
    parts = [ref[pl.ds(r0, tm), :].astype(F32)]
    if before:
        prev = ref[pl.ds(pl.multiple_of(jnp.maximum(r0 - HALO, 0), HALO), HALO), :].astype(F32)
        parts.insert(0, jnp.where(i > 0, prev, 0.0))
    if after:
        nxt = ref[pl.ds(pl.multiple_of(jnp.minimum(r0 + tm, t - HALO), HALO), HALO), :].astype(F32)
        parts.append(jnp.where(i < n_tiles - 1, nxt, 0.0))
    return jnp.concatenate(parts, axis=0)


def _conv_fwd(zbig, w):
    t, c = zbig.shape[0], w.shape[1]
    tm = _pick(t, (512, 256))
    nt = t // tm

    def body(x_ref, w_ref, o_ref):
        xe = _with_halo(x_ref, pl.program_id(0), nt, tm, True, False)
        wv = w_ref[...]
        y = xe * wv[3:4, :]
        for i in range(CONV_W - 1):
            y = y + pltpu.roll(xe, CONV_W - 1 - i, 0) * wv[i:i + 1, :]
        y = y[HALO:, :]
        o_ref[...] = (y * _sigmoid(y)).astype(o_ref.dtype)

    return pl.pallas_call(
        body, name="conv_fwd", grid=(nt,),
        in_specs=[pl.BlockSpec((t, c), lambda i: (0, 0)), pl.BlockSpec(w.shape, lambda i: (0, 0))],
        out_specs=pl.BlockSpec((tm, c), lambda i: (i, 0)), out_shape=jax.ShapeDtypeStruct((t, c), BF16),
        compiler_params=_cparams(("parallel",)),
    )(zbig, w)


def _conv_bwd(zbig, dact, w):
    t, c = dact.shape
    tm = _pick(t, (512, 256))
    nt = t // tm
    n = tm + HALO

    def body(x_ref, d_ref, w_ref, dx_ref, dw_ref):
        xe = _with_halo(x_ref, pl.program_id(0), nt, tm, True, True)
        de = _with_halo(d_ref, pl.program_id(0), nt, tm, False, True)
        wv = w_ref[...]
        sh = [pltpu.roll(xe, CONV_W - 1 - i, 0)[HALO:, :] if i < CONV_W - 1 else xe[HALO:, :] for i in range(CONV_W)]
        y = sh[0] * wv[0:1, :]
        for i in range(1, CONV_W):
            y = y + sh[i] * wv[i:i + 1, :]
        dy = de * _silu_grad(y)
        dx = dy * wv[3:4, :]
        for i in range(CONV_W - 1):
            dx = dx + pltpu.roll(dy, n - (CONV_W - 1 - i), 0) * wv[i:i + 1, :]
        dx_ref[...] = dx[:tm, :].astype(dx_ref.dtype)
        dyc = dy[:tm, :]
        dwp = jnp.concatenate([_colsum(dyc * sh[i][:tm, :]) for i in range(CONV_W)], axis=0)

        @pl.when(pl.program_id(0) == 0)
        def _():
            dw_ref[...] = jnp.zeros_like(dw_ref)
        dw_ref[...] += dwp

    return pl.pallas_call(
        body, name="conv_bwd", grid=(nt,),
        in_specs=[pl.BlockSpec((t, c), lambda i: (0, 0)), pl.BlockSpec((t, c), lambda i: (0, 0)),
                  pl.BlockSpec(w.shape, lambda i: (0, 0))],
        out_specs=[pl.BlockSpec((tm, c), lambda i: (i, 0)), pl.BlockSpec(w.shape, lambda i: (0, 0))],
        out_shape=[jax.ShapeDtypeStruct((t, c), BF16), jax.ShapeDtypeStruct(w.shape, F32)],
        compiler_params=_cparams(("arbitrary",)),
    )(zbig, dact, w)


LM = 256
HI = lax.Precision.HIGHEST


def _logsig(x):
    return jnp.minimum(x, 0.0) - jnp.log(1.0 + jnp.exp(-jnp.abs(x)))


def _tri(n, lower):
    r = lax.broadcasted_iota(jnp.int32, (n, n), 0)
    c = lax.broadcasted_iota(jnp.int32, (n, n), 1)
    return (r >= c) if lower else (r <= c)


def _f32dot(a, b):
    return lax.dot_general(a, b, (((1,), (0,)), ((), ())), precision=HI, preferred_element_type=F32)


def _tri_dot(a, b, a_is_tri):
    tri = (a if a_is_tri else b).astype(BF16)
    parts = _split3(b if a_is_tri else a)
    outs = [_dot(tri, p, 1, 0) if a_is_tri else _dot(p, tri, 1, 0) for p in parts]
    return (outs[0] + outs[1]) + outs[2]


def _mlstm_decays(zs_ref, zsr_ref, bc_ref, br_ref):
    l = LM
    lf_c = _logsig(zs_ref[:, 0:2 * NH_M] + bc_ref[...])
    lf_r = _logsig(zsr_ref[...] + br_ref[...])
    low, up = _tri(l, True), _tri(l, False)
    return _tri_dot(low, lf_c, True), _tri_dot(lf_r, up, False), low, up


def _mlstm_chunk(h, q_ref, k_ref, v_ref, zs_ref, zsr_ref, bc_ref, br_ref, c_prev, m_prev, decays):
    l = LM
    q = q_ref[:, h * DK_M:(h + 1) * DK_M].astype(F32) * (DK_M ** -0.5)
    k = k_ref[:, h * DK_M:(h + 1) * DK_M]
    v = v_ref[:, h * DV_M:(h + 1) * DV_M]
    lane = lax.broadcasted_iota(jnp.int32, (l, DV_M), 1)
    v1 = jnp.concatenate([v, (lane == 0).astype(v.dtype)], axis=1)
    zs, zsr = zs_ref[...], zsr_ref[...]
    li_c = zs[:, h:h + 1] + bc_ref[:, h:h + 1]
    fp_c = zs[:, NH_M + h:NH_M + h + 1] + bc_ref[:, NH_M + h:NH_M + h + 1]
    li_r = zsr[h:h + 1, :] + br_ref[h:h + 1, :]
    fp_r = zsr[NH_M + h:NH_M + h + 1, :] + br_ref[NH_M + h:NH_M + h + 1, :]
    low = decays[2]
    b_c = decays[0][:, NH_M + h:NH_M + h + 1]
    b_r = decays[1][NH_M + h:NH_M + h + 1, :]
    g = b_r[:, l - 1:l]
    dmat = jnp.where(low, b_c - b_r + li_r, -jnp.inf)
    inter = b_c + m_prev
    m_t = jnp.maximum(inter, jnp.max(dmat, axis=1, keepdims=True))
    w_inter = jnp.exp(inter - m_t)
    amat = jnp.exp(dmat - m_t)
    s = _dot(q, k, 1, 1)
    p = amat * s
    qc = _dot(q, c_prev, 1, 0)
    qc_w = w_inter * qc
    num1 = qc_w + _dot(p, v1, 1, 0)
    den = num1[:, DV_M:DV_M + 1]
    mx = jnp.maximum(jnp.abs(den), jnp.exp(-m_t))
    hh = num1[:, :DV_M] / mx
    a_c = g - b_c + li_c
    return dict(q=q, k=k, v1=v1, fp_c=fp_c, fp_r=fp_r, b_c=b_c, g=g, m_t=m_t, w_inter=w_inter, amat=amat, s=s, p=p,
                qc_w=qc_w, den=den, mx=mx, hh=hh, a_c=a_c)


def _mlstm_fwd(qk, zbig, zs, zsr, bc, br, gm):
    t = zs.shape[0]
    l = LM
    nc = t // l
    dm = NH_M * DV_M

    def body(q_ref, k_ref, v_ref, o_ref, zs_ref, zsr_ref, bc_ref, br_ref, gm_ref, y_ref, cst_ref, mst_ref, c_scr, m_scr):
        @pl.when(pl.program_id(0) == 0)
        def _():
            c_scr[...] = jnp.zeros_like(c_scr)
            m_scr[...] = jnp.zeros_like(m_scr)

        cst_ref[...] = c_scr[...]
        mst_ref[...] = m_scr[...]
        ys = []
        decays = _mlstm_decays(zs_ref, zsr_ref, bc_ref, br_ref)
        for h in range(NH_M):
            c_prev = c_scr[h]
            m_prev = m_scr[h:h + 1, 0:1]
            r = _mlstm_chunk(h, q_ref, k_ref, v_ref, zs_ref, zsr_ref, bc_ref, br_ref, c_prev, m_prev, decays)
            hh = r["hh"]
            gh = gm_ref[:, h * DV_M:(h + 1) * DV_M]
            hn = hh * lax.rsqrt(jnp.mean(hh * hh, axis=-1, keepdims=True) + EPS) * gh
            og = o_ref[:, h * DV_M:(h + 1) * DV_M].astype(F32)
            ys.append(hn * _sigmoid(og))
            m_new = jnp.maximum(r["g"] + m_prev, jnp.max(r["a_c"], axis=0, keepdims=True))
            decay = jnp.exp(r["g"] + m_prev - m_new)
            wk = r["k"].astype(F32) * jnp.exp(r["a_c"] - m_new)
            c_scr[h] = decay * c_prev + _dot(wk, r["v1"], 0, 0)
            m_scr[h:h + 1, :] = jnp.broadcast_to(m_new, (1, 128))
        y_ref[...] = jnp.concatenate(ys, axis=1).astype(y_ref.dtype)

    return pl.pallas_call(
        body, name="mlstm_fwd", grid=(nc,),
        in_specs=[pl.BlockSpec((l, NH_M * DK_M), lambda i: (i, 0)), pl.BlockSpec((l, NH_M * DK_M), lambda i: (i, 1)),
                  pl.BlockSpec((l, dm), lambda i: (i, 1)), pl.BlockSpec((l, dm), lambda i: (i, 2)),
                  pl.BlockSpec((l, 128), lambda i: (i, 0)), pl.BlockSpec((8, l), lambda i: (0, i)),
                  pl.BlockSpec((1, 8), lambda i: (0, 0)), pl.BlockSpec((8, 1), lambda i: (0, 0)),
                  pl.BlockSpec((1, dm), lambda i: (0, 0))],
        out_specs=[pl.BlockSpec((l, dm), lambda i: (i, 0)), pl.BlockSpec((None, NH_M, DK_M, 2 * DV_M), lambda i: (i, 0, 0, 0)),
                   pl.BlockSpec((None, 8, 128), lambda i: (i, 0, 0))],
        out_shape=[jax.ShapeDtypeStruct((t, dm), BF16), jax.ShapeDtypeStruct((nc, NH_M, DK_M, 2 * DV_M), F32),
                   jax.ShapeDtypeStruct((nc, 8, 128), F32)],
        scratch_shapes=[pltpu.VMEM((NH_M, DK_M, 2 * DV_M), F32), pltpu.VMEM((8, 128), F32)],
        compiler_params=_cparams(("arbitrary",)),
    )(qk, qk, zbig, zbig, zs, zsr, bc, br, gm)


def _mlstm_bwd(qk, zbig, zs, zsr, bc, br, gm, cst, mst, dycat):
    t = zs.shape[0]
    l = LM
    nc = t // l
    dm = NH_M * DV_M

    def body(q_ref, k_ref, v_ref, o_ref, zs_ref, zsr_ref, bc_ref, br_ref, gm_ref, cst_ref, mst_ref, cnx_ref, mnx_ref,
             dy_ref, dqk_ref, dv_ref, do_ref, dzs_ref, dzr_ref, dgm_ref, dc_scr):
        @pl.when(pl.program_id(0) == 0)
        def _():
            dc_scr[...] = jnp.zeros_like(dc_scr)
            dgm_ref[...] = jnp.zeros_like(dgm_ref)

        lane = lax.broadcasted_iota(jnp.int32, (l, 128), 1)
        db_all, sig_c, carries = jnp.zeros((l, 128), F32), jnp.zeros((l, 128), F32), jnp.zeros((1, 128), F32)
        decays = _mlstm_decays(zs_ref, zsr_ref, bc_ref, br_ref)
        lower, upper = decays[2], decays[3]
        dzr_rows = [None] * 8
        dvs, dos, dgs, dqs, dks = [], [], [], [], []
        dzs = jnp.zeros((l, 128), F32)
        for h in range(NH_M):
            c_prev = cst_ref[h]
            m_prev = mst_ref[h:h + 1, 0:1]
            r = _mlstm_chunk(h, q_ref, k_ref, v_ref, zs_ref, zsr_ref, bc_ref, br_ref, c_prev, m_prev, decays)
            hh, mx, den, m_t, v1, amat = r["hh"], r["mx"], r["den"], r["m_t"], r["v1"], r["amat"]
            gh = gm_ref[:, h * DV_M:(h + 1) * DV_M]
            rs = lax.rsqrt(jnp.mean(hh * hh, axis=-1, keepdims=True) + EPS)
            xh = hh * rs
            sg = _sigmoid(o_ref[:, h * DV_M:(h + 1) * DV_M].astype(F32))
            dyh = dy_ref[:, h * DV_M:(h + 1) * DV_M]
            dos.append(dyh * xh * gh * sg * (1.0 - sg))
            dhn = dyh * sg
            dgs.append(_colsum(dhn * xh))
            dxh = dhn * gh
            dh = rs * (dxh - xh * jnp.mean(dxh * xh, axis=-1, keepdims=True))
            g1 = dh / mx
            hd = jnp.sum(hh * dh, axis=-1, keepdims=True)
            dden = jnp.where(jnp.abs(den) > jnp.exp(-m_t), -hd / mx * jnp.sign(den), 0.0)
            g256 = jnp.concatenate([g1, jnp.where(lane == 0, dden, 0.0)], axis=1)
            dc_h = dc_scr[h]
            ea = jnp.exp(r["a_c"])
            dp = _dot(g256, v1, 1, 1)
            ds = dp * amat
            dqs.append((r["w_inter"] * _dot(g256, c_prev, 1, 1) + _dot(ds, r["k"], 1, 0)) * (DK_M ** -0.5))
            dks.append(_dot(ds, r["q"], 0, 0) + ea * _dot(v1, dc_h, 1, 1))
            dv_st = ea * _dot(r["k"], dc_h, 1, 0)
            dv1 = _dot(r["p"], g256, 0, 0) + dv_st
            dvs.append(dv1[:, :DV_M])
            wmat = dp * r["p"]
            c_in = _colsum(wmat)
            c_st = jnp.sum(v1.astype(F32) * dv_st, axis=-1, keepdims=True)
            r_t = jnp.sum(wmat, axis=1, keepdims=True) + jnp.sum(g256 * r["qc_w"], axis=-1, keepdims=True)
            db = r_t - c_st
            carry = jnp.exp(mnx_ref[h:h + 1, 0:1]) * jnp.sum(
                jnp.sum(dc_h * cnx_ref[h], axis=1, keepdims=True), axis=0, keepdims=True)
            db_all = db_all + jnp.where(lane == NH_M + h, db, 0.0)
            sig_c = sig_c + jnp.where(lane == NH_M + h, _sigmoid(-r["fp_c"]), 0.0)
            carries = carries + jnp.where(lane[0:1, :] == NH_M + h, carry, 0.0)
            dzs = dzs + jnp.where(lane == h, c_st, 0.0)
            dzr_rows[h] = c_in
            dzr_rows[NH_M + h] = _sigmoid(-r["fp_r"])
            wq = r["q"] * jnp.exp(r["b_c"] - m_t)
            dc_scr[h] = jnp.exp(r["g"]) * dc_h + _dot(wq, g256, 0, 0)
        dzs = dzs + (_tri_dot(upper, db_all, True) + carries) * sig_c
        c_in4 = jnp.concatenate(dzr_rows[:NH_M], axis=0)
        dlf_r4 = -_tri_dot(c_in4, lower, False)
        dzr_rows = dzr_rows[:NH_M] + [dlf_r4[h:h + 1, :] * dzr_rows[NH_M + h] for h in range(NH_M)]
        dqk_ref[...] = jnp.concatenate(dqs + dks, axis=1)
        dv_ref[...] = jnp.concatenate(dvs, axis=1).astype(dv_ref.dtype)
        do_ref[...] = jnp.concatenate(dos, axis=1).astype(do_ref.dtype)
        dzs_ref[...] = dzs
        dzr_ref[...] = jnp.concatenate(dzr_rows, axis=0)
        dgm_ref[...] += jnp.concatenate(dgs, axis=1)

    rev = lambda i: nc - 1 - i
    nxt = lambda i: jnp.minimum(nc - i, nc - 1)
    return pl.pallas_call(
        body, name="mlstm_bwd", grid=(nc,),
        in_specs=[pl.BlockSpec((l, NH_M * DK_M), lambda i: (rev(i), 0)), pl.BlockSpec((l, NH_M * DK_M), lambda i: (rev(i), 1)),
                  pl.BlockSpec((l, dm), lambda i: (rev(i), 1)), pl.BlockSpec((l, dm), lambda i: (rev(i), 2)),
                  pl.BlockSpec((l, 128), lambda i: (rev(i), 0)), pl.BlockSpec((8, l), lambda i: (0, rev(i))),
                  pl.BlockSpec((1, 8), lambda i: (0, 0)), pl.BlockSpec((8, 1), lambda i: (0, 0)),
                  pl.BlockSpec((1, dm), lambda i: (0, 0)),
                  pl.BlockSpec((None, NH_M, DK_M, 2 * DV_M), lambda i: (rev(i), 0, 0, 0)),
                  pl.BlockSpec((None, 8, 128), lambda i: (rev(i), 0, 0)),
                  pl.BlockSpec((None, NH_M, DK_M, 2 * DV_M), lambda i: (nxt(i), 0, 0, 0)),
                  pl.BlockSpec((None, 8, 128), lambda i: (nxt(i), 0, 0)),
                  pl.BlockSpec((l, dm), lambda i: (rev(i), 0))],
        out_specs=[pl.BlockSpec((l, dm), lambda i: (rev(i), 0)),
                   pl.BlockSpec((l, dm), lambda i: (rev(i), 0)), pl.BlockSpec((l, dm), lambda i: (rev(i), 0)),
                   pl.BlockSpec((l, 128), lambda i: (rev(i), 0)), pl.BlockSpec((8, l), lambda i: (0, rev(i))),
                   pl.BlockSpec((1, dm), lambda i: (0, 0))],
        out_shape=[jax.ShapeDtypeStruct((t, dm), F32),
                   jax.ShapeDtypeStruct((t, dm), BF16), jax.ShapeDtypeStruct((t, dm), BF16),
                   jax.ShapeDtypeStruct((t, 128), F32), jax.ShapeDtypeStruct((8, t), F32),
                   jax.ShapeDtypeStruct((1, dm), F32)],
        scratch_shapes=[pltpu.VMEM((NH_M, DK_M, 2 * DV_M), F32)],
        compiler_params=_cparams(("arbitrary",)),
    )(qk, qk, zbig, zbig, zs, zsr, bc, br, gm, cst, mst, cst, mst, dycat)


def _fox_cumsum(zsr, bf_r):
    t = zsr.shape[1]
    cw = _pick(t, (512, 256))

    def body(z_ref, b_ref, c_ref):
        up = _tri(cw, False).astype(F32)
        carry = jnp.zeros((NH_F, 1), F32)
        for j in range(t // cw):
            cs = _f32dot(_logsig(z_ref[:, j * cw:(j + 1) * cw] + b_ref[...]), up) + carry
            c_ref[:, j * cw:(j + 1) * cw] = cs
            carry = cs[:, cw - 1:cw]

    return pl.pallas_call(
        body, name="fox_cumsum", grid=(1,),
        in_specs=[pl.BlockSpec((NH_F, t), lambda i: (1, 0)), pl.BlockSpec((NH_F, 1), lambda i: (0, 0))],
        out_specs=pl.BlockSpec((NH_F, t), lambda i: (0, 0)), out_shape=jax.ShapeDtypeStruct((NH_F, t), F32),
        compiler_params=_cparams(("arbitrary",)),
    )(zsr, bf_r)


def _fox_gate_bwd(zsr, bf_r, dc):
    t = zsr.shape[1]
    cw = _pick(t, (512, 256))

    def body(z_ref, b_ref, dc_ref, o_ref):
        low = _tri(cw, True).astype(F32)
        carry = jnp.zeros((NH_F, 1), F32)
        for j in reversed(range(t // cw)):
            sl = slice(j * cw, (j + 1) * cw)
            dlf = _f32dot(dc_ref[:, sl], low) + carry
            o_ref[:, sl] = dlf * _sigmoid(-(z_ref[:, sl] + b_ref[...]))
            carry = dlf[:, 0:1]

    return pl.pallas_call(
        body, name="fox_gate_bwd", grid=(1,),
        in_specs=[pl.BlockSpec((NH_F, t), lambda i: (1, 0)), pl.BlockSpec((NH_F, 1), lambda i: (0, 0)),
                  pl.BlockSpec((NH_F, t), lambda i: (0, 0))],
        out_specs=pl.BlockSpec((NH_F, t), lambda i: (0, 0)), out_shape=jax.ShapeDtypeStruct((NH_F, t), F32),
        compiler_params=_cparams(("arbitrary",)),
    )(zsr, bf_r, dc)


def _causal_mask(n):
    return _tri(n, True)


AUG = 64


def _split3(c):
    hi = c.astype(BF16).astype(F32)
    r1 = c - hi
    mid = r1.astype(BF16).astype(F32)
    return hi, mid, r1 - mid


def _fox_prep(zbig, ct):
    t = zbig.shape[0]
    tm = _pick(t, (512, 256))

    def body(q_ref, k_ref, v_ref, c_ref, qo_ref, ko_ref, vo_ref):
        lane = lax.broadcasted_iota(jnp.int32, (tm, AUG), 1)
        qv, kv, vv, cv = q_ref[...], k_ref[...], v_ref[...], c_ref[...]
        one = (lane == 0).astype(BF16)
        for h in range(NH_F):
            hi, mid, lo = _split3(cv[:, h:h + 1])
            aq = jnp.where(lane == 0, hi, jnp.where(lane == 1, mid, jnp.where(lane == 2, lo, jnp.where(lane < 6, 1.0, 0.0))))
            ak = jnp.where(lane < 3, 1.0, jnp.where(lane == 3, -hi, jnp.where(lane == 4, -mid, jnp.where(lane == 5, -lo, 0.0))))
            sl = slice(h * DH_F, (h + 1) * DH_F)
            qo_ref[h] = jnp.concatenate([qv[:, sl] * (DH_F ** -0.5), aq.astype(BF16)], axis=1).astype(BF16)
            ko_ref[h] = jnp.concatenate([kv[:, sl], ak.astype(BF16)], axis=1)
            vo_ref[h] = jnp.concatenate([vv[:, sl], one], axis=1)

    ospec = pl.BlockSpec((NH_F, tm, 128), lambda i: (0, i, 0))
    return pl.pallas_call(
        body, name="fox_prep", grid=(t // tm,),
        in_specs=[pl.BlockSpec((tm, 512), lambda i: (i, 3)), pl.BlockSpec((tm, 512), lambda i: (i, 4)),
                  pl.BlockSpec((tm, 512), lambda i: (i, 5)), pl.BlockSpec((tm, NH_F), lambda i: (i, 0))],
        out_specs=[ospec] * 3, out_shape=[jax.ShapeDtypeStruct((NH_F, t, 128), BF16)] * 3,
        compiler_params=_cparams(("parallel",)),
    )(zbig, zbig, zbig, ct)


def _fox_fwd2(qa, ka, va, gf, plan=None):
    nh, t, _ = qa.shape
    tq = _pick(t, (512, 256))
    nq = t // tq
    group = 4
    host = _Hosted(plan, 4, 3)

    def body(*refs):
        (q_ref, k_ref, v_ref, g_ref), (y_ref, o_ref, lse_ref), _, prefs = host.split(refs)
        i = pl.program_id(0)
        host.run(0, i == 0, prefs)
        host.run(1, i == max(nq - 2, 0), prefs)
        lane = lax.broadcasted_iota(jnp.int32, (tq, 128), 1)
        causal = _causal_mask(tq)
        ys, os_ = [], []
        lse_all = jnp.zeros((tq, 128), F32)
        for h0 in range(0, nh, group):
            heads = range(h0, h0 + group)
            qvs = [q_ref[h] for h in heads]

            def blk(j, carry, masked, heads=heads, qvs=qvs):
                k0 = pl.multiple_of(j * tq, tq)
                out = []
                for (m, acc), h, qv in zip(carry, heads, qvs):
                    s = lax.dot_general(qv, k_ref[h, pl.ds(k0, tq), :], (((1,), (1,)), ((), ())), preferred_element_type=F32)
                    if masked:
                        s = jnp.where(causal, s, -jnp.inf)
                    m_new = jnp.maximum(m, jnp.max(s, axis=1, keepdims=True))
                    p = jnp.exp(s - m_new).astype(BF16)
                    pv = lax.dot_general(p, v_ref[h, pl.ds(k0, tq), :], (((1,), (0,)), ((), ())), preferred_element_type=F32)
                    out.append((m_new, jnp.exp(m - m_new) * acc + pv))
                return tuple(out)

            init = tuple((jnp.full((tq, 1), -jnp.inf, F32), jnp.zeros((tq, 128), F32)) for _ in heads)
            carry = lax.fori_loop(0, i, lambda j, c: blk(j, c, False), init)
            for (m, acc), h in zip(blk(i, carry, True), heads):
                l = acc[:, DH_F:DH_F + 1]
                o = acc[:, :DH_F] / l
                os_.append(o)
                gh = g_ref[:, h * DH_F:(h + 1) * DH_F]
                ys.append(o * lax.rsqrt(jnp.mean(o * o, axis=-1, keepdims=True) + EPS) * gh)
                lse_all = lse_all + jnp.where(lane == h, m + jnp.log(l), 0.0)
        y_ref[...] = jnp.concatenate(ys, axis=1).astype(y_ref.dtype)
        o_ref[...] = jnp.concatenate(os_, axis=1)
        lse_ref[...] = lse_all
        host.run(2, i == nq - 1, prefs)

    full = pl.BlockSpec((nh, t, 128), lambda i: (0, 0, 0))
    hc = host.call_args()
    res = pl.pallas_call(
        body, name="fox_fwd", grid=(nq,),
        in_specs=[pl.BlockSpec((nh, tq, 128), lambda i: (0, i, 0)), full, full, pl.BlockSpec((1, nh * DH_F), lambda i: (0, 0))]
        + hc["in_specs"],
        out_specs=[pl.BlockSpec((tq, nh * DH_F), lambda i: (i, 0)), pl.BlockSpec((tq, nh * DH_F), lambda i: (i, 0)),
                   pl.BlockSpec((tq, 128), lambda i: (i, 0))] + hc["out_specs"],
        out_shape=[jax.ShapeDtypeStruct((t, nh * DH_F), BF16), jax.ShapeDtypeStruct((t, nh * DH_F), F32),
                   jax.ShapeDtypeStruct((t, 128), F32)] + hc["out_shape"],
        scratch_shapes=hc["scratch"], input_output_aliases=hc["aliases"], compiler_params=_cparams(("arbitrary",)),
    )(qa, ka, va, gf, *hc["args"])
    return res[:3], res[3:]


def _fox_bwd_prep(dycat, o, gf):
    t = o.shape[0]
    tm = _pick(t, (512, 256))

    def body(dy_ref, o_ref, g_ref, do_ref, dl_ref, dg_ref):
        lane = lax.broadcasted_iota(jnp.int32, (tm, 128), 1)
        dyv, ov, gv = dy_ref[...], o_ref[...], g_ref[...]
        dgs = []
        dl = jnp.zeros((tm, 128), F32)
        pad = jnp.zeros((tm, AUG), BF16)
        for h in range(NH_F):
            sl = slice(h * DH_F, (h + 1) * DH_F)
            dx, dg = _rms_bwd_val(dyv[:, sl], ov[:, sl], gv[:, sl])
            dgs.append(dg)
            do_ref[h] = jnp.concatenate([dx.astype(BF16), pad], axis=1)
            dl = dl + jnp.where(lane == h, jnp.sum(dx * ov[:, sl], axis=-1, keepdims=True), 0.0)
        dl_ref[...] = dl

        @pl.when(pl.program_id(0) == 0)
        def _():
            dg_ref[...] = jnp.zeros_like(dg_ref)
        dg_ref[...] += jnp.concatenate(dgs, axis=1)

    return pl.pallas_call(
        body, name="fox_bwd_prep", grid=(t // tm,),
        in_specs=[pl.BlockSpec((tm, 512), lambda i: (i, 1)), pl.BlockSpec((tm, 512), lambda i: (i, 0)),
                  pl.BlockSpec((1, 512), lambda i: (0, 0))],
        out_specs=[pl.BlockSpec((NH_F, tm, 128), lambda i: (0, i, 0)), pl.BlockSpec((tm, 128), lambda i: (i, 0)),
                   pl.BlockSpec((1, 512), lambda i: (0, 0))],
        out_shape=[jax.ShapeDtypeStruct((NH_F, t, 128), BF16), jax.ShapeDtypeStruct((t, 128), F32),
                   jax.ShapeDtypeStruct((1, 512), F32)],
        compiler_params=_cparams(("arbitrary",)),
    )(dycat, o, gf)


def _fox_bwd2(qa, ka, va, doa, lse, delta):
    nh, t, _ = qa.shape
    tq = _pick(t, (512, 256))
    nq = t // tq

    group = 2

    def tdot(a, b, cb):
        return lax.dot_general(a, b, (((0,), (cb,)), ((), ())), preferred_element_type=F32)

    def body(q_ref, k_ref, v_ref, do_ref, lse_ref, dl_ref, dq_ref, dk_ref, dv_ref):
        hp, j = pl.program_id(0), pl.program_id(1)

        @pl.when(j == 0)
        def _():
            dq_ref[...] = jnp.zeros_like(dq_ref)

        lane = lax.broadcasted_iota(jnp.int32, (tq, 128), 1)
        causal = _causal_mask(tq)

        def blk(i, carry, masked):
            rows = pl.ds(pl.multiple_of(i * tq, tq), tq)
            lse_t, dl_t = lse_ref[rows, :], dl_ref[rows, :]
            out = []
            for g, (dk, dv) in enumerate(carry):
                h = hp * group + g
                kb, vb = k_ref[g], v_ref[g]
                qb, dob = q_ref[g, rows, :], do_ref[g, rows, :]
                lse_h = jnp.sum(jnp.where(lane == h, lse_t, 0.0), axis=1, keepdims=True)
                dl_h = jnp.sum(jnp.where(lane == h, dl_t, 0.0), axis=1, keepdims=True)
                s = lax.dot_general(qb, kb, (((1,), (1,)), ((), ())), preferred_element_type=F32)
                if masked:
                    s = jnp.where(causal, s, -jnp.inf)
                p = jnp.exp(s - lse_h)
                dp = lax.dot_general(dob, vb, (((1,), (1,)), ((), ())), preferred_element_type=F32)
                ds = (p * (dp - dl_h)).astype(BF16)
                dv = dv + tdot(dob, p.astype(BF16), 0)
                dk = dk + tdot(qb, ds, 0)
                dq_ref[g, :, rows] += tdot(kb, ds, 1)
                out.append((dk, dv))
            return tuple(out)

        init = tuple((jnp.zeros((128, tq), F32), jnp.zeros((128, tq), F32)) for _ in range(group))
        carry = blk(j, init, True)
        carry = lax.fori_loop(j + 1, nq, lambda i, c: blk(i, c, False), carry)
        for g, (dk, dv) in enumerate(carry):
            dk_ref[g] = dk
            dv_ref[g] = dv

    full = pl.BlockSpec((group, t, 128), lambda h, j: (h, 0, 0))
    tile = pl.BlockSpec((group, tq, 128), lambda h, j: (h, j, 0))
    cols = pl.BlockSpec((t, 128), lambda h, j: (0, 0))
    full_t = pl.BlockSpec((group, 128, t), lambda h, j: (h, 0, 0))
    tile_t = pl.BlockSpec((group, 128, tq), lambda h, j: (h, 0, j))
    return pl.pallas_call(
        body, name="fox_bwd", grid=(nh // group, nq), in_specs=[full, tile, tile, full, cols, cols],
        out_specs=[full_t, tile_t, tile_t], out_shape=[jax.ShapeDtypeStruct((nh, 128, t), F32)] * 3,
        compiler_params=_cparams(("parallel", "arbitrary")),
    )(qa, ka, va, doa, lse, delta)


def _fox_bwd_post(dqa, dka, dva):
    nh, _, t = dqa.shape
    tm = _pick(t, (512, 256))

    def body(dq_ref, dk_ref, dv_ref, oq_ref, ok_ref, ov_ref, dc_ref):
        qs, ks, vs, dcs = [], [], [], []
        for h in range(nh):
            dq, dk = dq_ref[h], dk_ref[h]
            qs.append(dq.T[:, :DH_F] * (DH_F ** -0.5))
            ks.append(dk.T[:, :DH_F])
            vs.append(dv_ref[h].T[:, :DH_F])
            dcs.append(dq[DH_F:DH_F + 1, :] - dk[DH_F + 3:DH_F + 4, :])
        oq_ref[...] = jnp.concatenate(qs, axis=1).astype(BF16)
        ok_ref[...] = jnp.concatenate(ks, axis=1).astype(BF16)
        ov_ref[...] = jnp.concatenate(vs, axis=1).astype(BF16)
        dc_ref[...] = jnp.concatenate(dcs, axis=0)

    ispec = pl.BlockSpec((nh, 128, tm), lambda i: (0, 0, i))
    ospec = pl.BlockSpec((tm, nh * DH_F), lambda i: (i, 0))
    return pl.pallas_call(
        body, name="fox_bwd_post", grid=(t // tm,), in_specs=[ispec] * 3,
        out_specs=[ospec] * 3 + [pl.BlockSpec((nh, tm), lambda i: (0, i))],
        out_shape=[jax.ShapeDtypeStruct((t, nh * DH_F), BF16)] * 3 + [jax.ShapeDtypeStruct((nh, t), F32)],
        compiler_params=_cparams(("parallel",)),
    )(dqa, dka, dva)


IN_OFF = (0, 512, 1024, 1544, 2056, 2568)
IN_GATES = (1536, 3080)


FFN1 = ("ffn1_w_gate", "ffn1_w_up", "ffn1_w_down")
REST = ("w_in", "w_out", "ffn2_w_gate", "ffn2_w_up", "ffn2_w_down", "w_ple_gate", "w_ple_proj")
SPLIT = {n: 1 if n == "w_in" else 0 for n in FFN1 + REST}
SAME_SHAPE = (FFN1, ("ffn2_w_gate", "ffn2_w_up", "ffn2_w_down"), ("w_out", "w_ple_gate"), ("w_in",), ("w_ple_proj",))


def _grouped(names):
    return [tuple(n for n in grp if n in names) for grp in SAME_SHAPE if any(n in names for n in grp)]


def _rs_partials(names, gw, c_idx, twins, run_swap=None):
    wire = [twins[n] if n in twins else _cast_other_half("rs_cast_" + n, gw[n], c_idx, SPLIT[n]) for n in names]
    plan = _swap_plan(wire, [SPLIT[n] if n in twins else None for n in names])
    swapped = dict(zip(names, run_swap(plan) if run_swap else _run_plan("rs_swap_" + names[0], plan)))
    out = {}
    for grp in _grouped(names):
        res = _add_my_half("rs_add_" + grp[0], [gw[n] for n in grp], [swapped[n] for n in grp], c_idx, SPLIT[grp[0]])
        out.update(zip(grp, res))
    return [out[n] for n in names]


def _local_step(x, p, tgt, sp, wg1, wu1, wd1, rest_slots, c_idx, place):
    t, d = x.shape
    slot = dict(zip(REST + ("conv_qk",), rest_slots))
    (h1, xn1, g1, u1), (w_in, conv_w) = _ffn_fwd(
        "ffn1", x, sp["ffn1_norm"], wg1, wu1, wd1, plan=_gather_plan([slot["w_in"], slot["conv_qk"]], [SPLIT["w_in"], None]))
    w_in, conv_w = w_in.reshape(-1, d), _from_chip_blocks(conv_w)
    w_big = jnp.concatenate([w_in[o:o + 512] for o in IN_OFF], axis=0)
    w_small = jnp.concatenate([w_in[IN_GATES[0]:IN_GATES[0] + 8], w_in[IN_GATES[1]:IN_GATES[1] + 8],
                               jnp.zeros((112, d), w_in.dtype)], axis=0)
    u, zbig = _norm_mm("in_big", h1, sp["mix_norm"], w_big, True, BF16)
    zs = _mm_nt("in_small", u, w_small, tm=1024, tk=1024)
    zsr = zs.T
    qk_act = _conv_fwd(zbig, conv_w)
    bm_c, bf_c = sp["b_mlstm_gates"], sp["b_fox_f"]
    y_m, cst, mst = _mlstm_fwd(qk_act, zbig, zs, zsr, bm_c, bm_c.T, sp["mlstm_out_norm"])
    c = _fox_cumsum(zsr, bf_c.T)
    qa, ka, va = _fox_prep(zbig, c.T)
    (y_ft, o_f, lse), late = _fox_fwd2(qa, ka, va, sp["fox_out_norm"],
                                       plan=_gather_plan([slot[n] for n in REST[1:]], [SPLIT[n] for n in REST[1:]]))
    full = dict(zip(REST[1:], late))
    w_out, w_pg = (full[n].reshape(-1, d) for n in ("w_out", "w_ple_gate"))
    wg2, wu2, wd2 = full["ffn2_w_gate"], full["ffn2_w_up"], full["ffn2_w_down"]
    w_pp = _from_chip_blocks(full["w_ple_proj"])
    tm = _pick(t, (1024, 512, 256))
    h2 = _mm("out_proj", [(y_m, (tm, 512), lambda i, j, k: (i, 0), w_out, (512, d), lambda i, j, k: (0, 0)),
                          (y_ft, (tm, 512), lambda i, j, k: (i, 0), w_out, (512, d), lambda i, j, k: (1, 0))],
             (t, d), (tm, d), lambda i, j, k: (i, 0), (t // tm, 1, 1), 2, res=h1)
    (h3, xn2, g2, u2), _ = _ffn_fwd("ffn2", h2, sp["ffn2_norm"], wg2, wu2, wd2)
    hn3, gate_pre = _norm_mm("ple_gate", h3, sp["ple_gate_norm"], w_pg, False, F32)
    pp = _mm_nn("ple_proj", p, w_pp, tm=1024)

    def head_fn(h3_t, gp_t, pp_t, tgt_t, g_pp, g_fin):
        gate = _sigmoid(gp_t)
        ppn = _rms_fwd_val(pp_t, g_pp)
        h4 = h3_t + gate * ppn
        err = _rms_fwd_val(h4, g_fin) - tgt_t
        loss = 0.5 * jnp.sum(jnp.mean(err * err, axis=-1, keepdims=True), axis=0, keepdims=True)
        dh4, dg_fin = _rms_bwd_val(err * (1.0 / d), h4, g_fin)
        dpp, dg_pp = _rms_bwd_val(dh4 * gate, pp_t, g_pp)
        dgp = dh4 * ppn * gate * (1.0 - gate)
        return dh4, dgp, dpp, jnp.broadcast_to(loss, (1, 128)), dg_fin, dg_pp

    dh4, dgp, dpp, loss_part, dg_fin, dg_pp = _rowwise(
        "loss_head", head_fn, [h3, gate_pre, pp, tgt], [sp["ple_proj_norm"], sp["final_norm"]],
        [(d, F32), (d, BF16), (d, BF16)], [((1, 128), F32), ((1, d), F32), ((1, d), F32)])
    gw, gs = {}, {"final_norm": dg_fin, "ple_proj_norm": dg_pp}
    gw["w_ple_gate"] = _mm_tn("d_w_pg", hn3, dgp, tm=1024, tn=1024)
    gw["w_ple_proj"] = _mm_tn("d_w_pp", p, dpp, tn=1024)
    dhn3 = _mm_nt("d_hn3", dgp, w_pg, tm=1024, tn=1024, tk=1024)

    def res_norm_bwd(dn_t, h_t, dres_t, g):
        dx, dg = _rms_bwd_val(dn_t, h_t, g)
        return dres_t + dx, dg

    dh3, gs["ple_gate_norm"] = _rowwise("ple_norm_bwd", res_norm_bwd, [dhn3, h3, dh4], [sp["ple_gate_norm"]],
                                        [(d, F32)], [((1, d), F32)])
    (dh2, gs["ffn2_norm"], gw["ffn2_w_gate"], gw["ffn2_w_up"], gw["ffn2_w_down"]), _, twins2 = _ffn_bwd(
        "ffn2", dh3, h2, sp["ffn2_norm"], xn2, g2, u2, wg2, wu2, wd2)
    dycat = _mm_nt("d_ycat", dh2, w_out, tm=1024, tn=1024, tk=1024)
    gw["w_out"] = jnp.concatenate([_mm_tn("d_w_out_m", y_m, dh2, tn=1024, tk=2048),
                                   _mm_tn("d_w_out_f", y_ft, dh2, tn=1024, tk=2048)], axis=0)
    doa, delta, gs["fox_out_norm"] = _fox_bwd_prep(dycat, o_f, sp["fox_out_norm"])
    dq_f, dk_f, dv_f, dct = _fox_bwd_post(*_fox_bwd2(qa, ka, va, doa, lse, delta))
    dfp = _fox_gate_bwd(zsr, bf_c.T, dct)
    dact, dv_m, do_m, dzs_m, dzr_m, gs["mlstm_out_norm"] = _mlstm_bwd(
        qk_act, zbig, zs, zsr, bm_c, bm_c.T, sp["mlstm_out_norm"], cst, mst, dycat)
    dqk, gw["conv_qk"] = _conv_bwd(zbig, dact, conv_w)
    dz_big = jnp.concatenate([dqk, dv_m, do_m, dq_f, dk_f, dv_f], axis=1)
    dzs = dzs_m + jnp.pad(jnp.concatenate([dzr_m, dfp], axis=0).T, ((0, 0), (0, 112)))
    dw_big = _mm_tn("d_w_big", dz_big, u, tn=1024)
    dw_small = _mm_tn("d_w_small", dzs, u, tn=1024)
    gw["w_in"] = jnp.concatenate([dw_big[0:1536], dw_small[0:8], dw_big[1536:3072], dw_small[8:16]], axis=0)
    du_a = _mm_nn("d_u_big", dz_big, w_big, tm=1024, tn=1024, tk=1024)
    du_b = _mm_nn("d_u_small", dzs, w_small, tm=1024, tn=1024)

    def mix_norm_bwd(da_t, db_t, h_t, dres_t, dzs_t, g):
        dx, dg = _rms_bwd_val(da_t + db_t, h_t, g)
        return dres_t + dx, dg, _colsum(dzs_t)

    conv_grad = gw.pop("conv_qk")
    gw["w_ple_proj"] = _chip_blocks(gw["w_ple_proj"])
    for n in ("w_in", "w_out", "w_ple_gate"):
        gw[n] = gw[n].reshape(4, -1, gw[n].shape[-1])
    twins = dict(zip(("ffn2_w_gate", "ffn2_w_up", "ffn2_w_down"), twins2))
    mix = []

    def swap_in_mix_norm_bwd(plan):
        res, swapped = _rowwise("mix_norm_bwd", mix_norm_bwd, [du_a, du_b, h1, dh2, dzs], [sp["mix_norm"]],
                                [(d, F32)], [((1, d), F32), ((1, 128), F32)], plan=plan)
        mix.extend(res)
        return swapped

    part_rest = dict(zip(REST, _rs_partials(REST, gw, c_idx, twins, swap_in_mix_norm_bwd)))
    dh1, gs["mix_norm"], dbias = mix
    gs["b_mlstm_gates"], gs["b_fox_f"] = dbias[:, 0:8], dbias[:, 8:16]
    light = ("w_in", "w_out", "w_ple_gate", "w_ple_proj")
    part_ffn1 = []

    def own_plan(dws, dw_twins):
        part_ffn1.extend(_rs_partials(FFN1, dict(zip(FFN1, dws)), c_idx, dict(zip(FFN1, dw_twins))))
        return _scatter_plan([pb for _, pb in part_ffn1])

    (grad_x, gs["ffn1_norm"], _, _, _), (l_light, l_down, l_gate, l_up, landed_ffn1) = _ffn_bwd_late_dx(
        "ffn1", dh1, x, sp["ffn1_norm"], xn1, g1, u1, wg1, wu1, wd1,
        _scatter_plan([part_rest[n][1] for n in light]),
        [_scatter_plan([part_rest[n][1]]) for n in ("ffn2_w_down", "ffn2_w_gate", "ffn2_w_up")], own_plan)
    landed_rest = dict(zip(light + ("ffn2_w_down", "ffn2_w_gate", "ffn2_w_up"), list(l_light) + [l_down[0], l_gate[0], l_up[0]]))
    names = REST + FFN1
    parts = {**part_rest, **dict(zip(FFN1, part_ffn1))}
    landed = {**landed_rest, **dict(zip(FFN1, landed_ffn1))}
    mine = {}
    for grp in _grouped(names):
        res = _sum4("rs_sum_" + grp[0], [landed[n] for n in grp], [parts[n][0] for n in grp], place, SPLIT[grp[0]])
        mine.update(zip(grp, res))
    grads = dict(zip(names, _join_halves("rs_join", [mine[n] for n in names], [SPLIT[n] for n in names])))
    return loss_part, grad_x, grads, gs, conv_grad


ANY = pl.BlockSpec(memory_space=pl.ANY)
MESH = pl.DeviceIdType.MESH


def _place():
    x, y, c = lax.axis_index("x"), lax.axis_index("y"), lax.axis_index("c")
    chips = [(1 - x, y), (x, 1 - y), (1 - x, 1 - y)]
    return x, y, c, 2 * x + y, (x, y, 1 - c), chips


def _rcopy(src, dst, ssem, rsem, dev):
    return pltpu.make_async_remote_copy(src_ref=src, dst_ref=dst, send_sem=ssem, recv_sem=rsem, device_id=dev,
                                        device_id_type=MESH)


def _half(ref, lead, axis, idx, half):
    return ref.at[(slice(None),) * (lead + axis) + (pl.ds(idx * half, half),)]


def _to_slot(name, arrs, me_idx, dtype):
    n = len(arrs)
    r, cdim = arrs[0].shape
    tr = _pick(r, (352, 256, 176, 128, 64))

    def body(me_ref, *refs):
        for k in range(n):
            refs[n + k][...] = refs[k][...].astype(dtype)

    return pl.pallas_call(
        body, name=name,
        grid_spec=pltpu.PrefetchScalarGridSpec(
            num_scalar_prefetch=1, grid=(r // tr,), in_specs=[pl.BlockSpec((tr, cdim), lambda i, me_ref: (i, 0))] * n,
            out_specs=[pl.BlockSpec((None, tr, cdim), lambda i, me_ref: (me_ref[0], i, 0))] * n),
        out_shape=[jax.ShapeDtypeStruct((4, r, cdim), dtype)] * n, compiler_params=_cparams(("parallel",)),
    )(me_idx, *arrs)


def _gather4(name, bufs, split):
    return _run_plan(name, _gather_plan(bufs, split))


def _gather_plan(bufs, split):
    n = len(bufs)
    shapes = [b.shape[1:] for b in bufs]

    def ctx(outs):
        x, y, c, me, sib, chips = _place()

        def part(ref, a, which):
            if split[a] is None:
                return ref
            return _half(ref, 0, split[a], which, shapes[a][split[a]] // 2)

        return c, me, sib, chips, part

    def ici(outs, sems, a, j, chip, c, me, part):
        mine = part(outs[a].at[me], a, c)
        return _rcopy(mine, mine, sems[0].at[3 * a + j], sems[1].at[3 * a + j], (*chip, c))

    def fwd(outs, sems, a, j, chip, c, sib, part, which):
        blk = part(outs[a].at[2 * chip[0] + chip[1]], a, which)
        return _rcopy(blk, blk, sems[2].at[3 * a + j], sems[3].at[3 * a + j], sib)

    def start(ins, outs, sems):
        c, me, sib, chips, part = ctx(outs)
        for a in range(n):
            for j, chip in enumerate(chips):
                ici(outs, sems, a, j, chip, c, me, part).start()

    def mid(ins, outs, sems):
        c, me, sib, chips, part = ctx(outs)
        for j, chip in enumerate(chips):
            for a in range(n):
                blk = part(outs[a].at[2 * chip[0] + chip[1]], a, c)
                _rcopy(blk, blk, sems[0].at[3 * a + j], sems[1].at[3 * a + j], sib).wait_recv()
                if split[a] is not None:
                    fwd(outs, sems, a, j, chip, c, sib, part, c).start()

    def end(ins, outs, sems):
        c, me, sib, chips, part = ctx(outs)
        for j, chip in enumerate(chips):
            for a in range(n):
                if split[a] is not None:
                    fwd(outs, sems, a, j, chip, c, sib, part, 1 - c).wait_recv()
        for a in range(n):
            for j, chip in enumerate(chips):
                ici(outs, sems, a, j, chip, c, me, part).wait_send()
                if split[a] is not None:
                    fwd(outs, sems, a, j, chip, c, sib, part, c).wait_send()

    return dict(ins=list(bufs), outs=[jax.ShapeDtypeStruct(b.shape, b.dtype) for b in bufs], alias=True,
                sems=[pltpu.SemaphoreType.DMA((3 * n,))] * 4, phases=(start, mid, end))


def _run_plan(name, plan):
    ni, no = len(plan["ins"]), len(plan["outs"])

    def body(*refs):
        ins, outs, sems = refs[:ni], refs[ni:ni + no], refs[ni + no:]
        for phase in plan["phases"]:
            phase(ins, outs, sems)

    return pl.pallas_call(
        body, name=name, in_specs=[ANY] * ni, out_specs=[ANY] * no, out_shape=plan["outs"],
        input_output_aliases={a: a for a in range(ni)} if plan["alias"] else {}, scratch_shapes=plan["sems"],
    )(*plan["ins"])


class _Hosted:
    def __init__(self, plan, n_in, n_out):
        self.plan, self.n_in, self.n_out = plan, n_in, n_out
        self.ni, self.no, self.ns = (len(plan["ins"]) if plan else 0, len(plan["outs"]) if plan else 0,
                                     len(plan["sems"]) if plan else 0)

    def split(self, refs):
        a, b = self.n_in, self.n_in + self.ni
        c, d = b + self.n_out, b + self.n_out + self.no
        e = len(refs) - self.ns
        return refs[:a], refs[b:c], refs[d:e], (refs[a:b], refs[c:d], refs[e:])

    def run(self, k, cond, prefs):
        if self.plan is not None:
            @pl.when(cond)
            def _():
                self.plan["phases"][k](*prefs)

    def call_args(self):
        p = self.plan
        if p is None:
            return dict(in_specs=[], out_specs=[], out_shape=[], scratch=[], aliases={}, args=[])
        al = {self.n_in + a: self.n_out + a for a in range(self.ni)} if p["alias"] else {}
        return dict(in_specs=[ANY] * self.ni, out_specs=[ANY] * self.no, out_shape=list(p["outs"]), scratch=list(p["sems"]),
                    aliases=al, args=list(p["ins"]))


def _swap(name, arrs, halve):
    return _run_plan(name, _swap_plan(arrs, halve))


def _swap_plan(arrs, halve):
    n = len(arrs)

    def half_shape(a, ax):
        return a.shape if ax is None else (a.shape[0],) + tuple(d // 2 if i == ax else d for i, d in enumerate(a.shape[1:]))

    def copies(ins, outs, sems):
        x, y, c, me, sib, chips = _place()
        cps = []
        for a in range(n):
            src = ins[a] if halve[a] is None else _half(ins[a], 1, halve[a], 1 - c, arrs[a].shape[1 + halve[a]] // 2)
            cps.append(_rcopy(src, outs[a], sems[0].at[a], sems[1].at[a], sib))
        return cps

    def start(ins, outs, sems):
        for cp in copies(ins, outs, sems):
            cp.start()

    def mid(ins, outs, sems):
        pass

    def end(ins, outs, sems):
        for cp in copies(ins, outs, sems):
            cp.wait()

    return dict(ins=list(arrs), outs=[jax.ShapeDtypeStruct(half_shape(a, ax), a.dtype) for a, ax in zip(arrs, halve)],
                alias=False, sems=[pltpu.SemaphoreType.DMA((n,))] * 2, phases=(start, mid, end))


def _scatter4(name, arrs):
    return _run_plan(name, _scatter_plan(arrs))


def _scatter_plan(arrs):
    n = len(arrs)

    def send(ins, outs, sems, a, j, chip, c, me):
        return _rcopy(ins[a].at[2 * chip[0] + chip[1]], outs[a].at[me], sems[0].at[3 * a + j], sems[1].at[3 * a + j], (*chip, c))

    def start(ins, outs, sems):
        x, y, c, me, sib, chips = _place()
        for a in range(n):
            for j, chip in enumerate(chips):
                send(ins, outs, sems, a, j, chip, c, me).start()

    def mid(ins, outs, sems):
        pass

    def end(ins, outs, sems):
        x, y, c, me, sib, chips = _place()
        for a in range(n):
            for j, chip in enumerate(chips):
                blk = outs[a].at[2 * chip[0] + chip[1]]
                _rcopy(blk, blk, sems[0].at[3 * a + j], sems[1].at[3 * a + j], sib).wait_recv()
        for a in range(n):
            for j, chip in enumerate(chips):
                send(ins, outs, sems, a, j, chip, c, me).wait_send()

    return dict(ins=list(arrs), outs=[jax.ShapeDtypeStruct(a.shape, a.dtype) for a in arrs], alias=False,
                sems=[pltpu.SemaphoreType.DMA((3 * n,))] * 2, phases=(start, mid, end))


def _join_halves(name, arrs, split):
    n = len(arrs)

    def body(*refs):
        outs = refs[n:2 * n]
        ssem, rsem = refs[2 * n:]
        x, y, c, me, sib, chips = _place()
        cps = []
        for a in range(n):
            mine = _half(outs[a], 0, split[a], c, arrs[a].shape[split[a]] // 2)
            cp = _rcopy(mine, mine, ssem.at[a], rsem.at[a], sib)
            cp.start()
            cps.append(cp)
        for a in range(n):
            blk = _half(outs[a], 0, split[a], 1 - c, arrs[a].shape[split[a]] // 2)
            _rcopy(blk, blk, ssem.at[a], rsem.at[a], sib).wait_recv()
        for cp in cps:
            cp.wait_send()

    return pl.pallas_call(
        body, name=name, in_specs=[ANY] * n, out_specs=[ANY] * n,
        out_shape=[jax.ShapeDtypeStruct(a.shape, a.dtype) for a in arrs],
        input_output_aliases={a: a for a in range(n)}, scratch_shapes=[pltpu.SemaphoreType.DMA((n,))] * 2,
    )(*arrs)


def _allreduce_small(s):
    r, cdim = s.shape

    def body(s_ref, o_ref, buf, ssem, rsem):
        x, y, c, me, sib, chips = _place()
        me8 = 4 * x + 2 * y + c
        buf[me8] = s_ref[...]
        flips = [(fx, fy, fc) for fx in (0, 1) for fy in (0, 1) for fc in (0, 1)][1:]
        cps = []
        for k, (fx, fy, fc) in enumerate(flips):
            peer = (x ^ fx if fx else x, y ^ fy if fy else y, c ^ fc if fc else c)
            cp = _rcopy(s_ref, buf.at[me8], ssem.at[k], rsem.at[k], peer)
            cp.start()
            cps.append(cp)
        for k, (fx, fy, fc) in enumerate(flips):
            src = 4 * (x ^ fx if fx else x) + 2 * (y ^ fy if fy else y) + (c ^ fc if fc else c)
            _rcopy(s_ref, buf.at[src], ssem.at[k], rsem.at[k], sib).wait_recv()
        for cp in cps:
            cp.wait_send()
        acc = buf[0]
        for k in range(1, 8):
            acc = acc + buf[k]
        o_ref[...] = acc

    vm = pl.BlockSpec(memory_space=pltpu.VMEM)
    return pl.pallas_call(
        body, name="allreduce_small", in_specs=[vm], out_specs=vm, out_shape=jax.ShapeDtypeStruct((r, cdim), F32),
        scratch_shapes=[pltpu.VMEM((8, r, cdim), F32), pltpu.SemaphoreType.DMA((7,)), pltpu.SemaphoreType.DMA((7,))],
    )(s)


def _add_my_half(name, gs, recvs, c_idx, axis):
    n = len(gs)
    nb, hr, hc = recvs[0].shape
    tr = _pick(hr, (256, 176, 128, 64))
    if axis == 0:
        g4s = [g.reshape(nb, 2, hr, hc) for g in gs]
        gspec = pl.BlockSpec((None, None, tr, hc), lambda b, i, c_ref: (b, c_ref[0], i, 0))
    else:
        g4s = list(gs)
        gspec = pl.BlockSpec((None, tr, hc), lambda b, i, c_ref: (b, i, c_ref[0]))

    def body(c_ref, *refs):
        for k in range(n):
            s = refs[k][...] + refs[n + k][...].astype(F32)
            refs[2 * n + 2 * k][...] = s
            refs[2 * n + 2 * k + 1][...] = s.astype(BF16)

    ospec = pl.BlockSpec((None, tr, hc), lambda b, i, c_ref: (b, i, 0))
    res = pl.pallas_call(
        body, name=name,
        grid_spec=pltpu.PrefetchScalarGridSpec(
            num_scalar_prefetch=1, grid=(nb, hr // tr), in_specs=[gspec] * n + [ospec] * n, out_specs=[ospec] * (2 * n)),
        out_shape=[jax.ShapeDtypeStruct((nb, hr, hc), F32), jax.ShapeDtypeStruct((nb, hr, hc), BF16)] * n,
        compiler_params=_cparams(("parallel", "parallel")),
    )(c_idx, *g4s, *recvs)
    return [(res[2 * k], res[2 * k + 1]) for k in range(n)]


def _sum4(name, landeds, owns, place, axis):
    n = len(landeds)
    nb, h, cdim = landeds[0].shape
    tr = _pick(h, (256, 176, 128, 64))
    nt = h // tr

    def body(p_ref, *refs):
        for k in range(n):
            a1, a2, a3, own = refs[4 * k:4 * k + 4]
            refs[4 * n + k][...] = ((own[...] + a1[...].astype(F32)) + a2[...].astype(F32)) + a3[...].astype(F32)

    def nxt(k):
        return pl.BlockSpec((None, tr, cdim), lambda i, p_ref: ((p_ref[0] + k) % nb, i, 0))

    if axis == 0:
        ospec = pl.BlockSpec((tr, cdim), lambda i, p_ref: (p_ref[1] * nt + i, 0))
        oshape = (2 * h, cdim)
    else:
        ospec = pl.BlockSpec((tr, cdim), lambda i, p_ref: (i, p_ref[1]))
        oshape = (h, 2 * cdim)
    args = []
    for landed, own in zip(landeds, owns):
        args += [landed, landed, landed, own]
    return pl.pallas_call(
        body, name=name,
        grid_spec=pltpu.PrefetchScalarGridSpec(
            num_scalar_prefetch=1, grid=(nt,), in_specs=[nxt(1), nxt(2), nxt(3), nxt(0)] * n, out_specs=[ospec] * n),
        out_shape=[jax.ShapeDtypeStruct(oshape, F32)] * n, compiler_params=_cparams(("parallel",)),
    )(place, *args)


def _cast_other_half(name, g, c_idx, axis):
    nb, r, cdim = g.shape
    hr, hc = (r // 2, cdim) if axis == 0 else (r, cdim // 2)
    tr = _pick(hr, (256, 176, 128, 64))
    if axis == 0:
        g4 = g.reshape(nb, 2, hr, hc)
        gspec = pl.BlockSpec((None, None, tr, hc), lambda b, i, c_ref: (b, 1 - c_ref[0], i, 0))
    else:
        g4 = g
        gspec = pl.BlockSpec((None, tr, hc), lambda b, i, c_ref: (b, i, 1 - c_ref[0]))

    def body(c_ref, g_ref, o_ref):
        o_ref[...] = g_ref[...].astype(BF16)

    return pl.pallas_call(
        body, name=name,
        grid_spec=pltpu.PrefetchScalarGridSpec(
            num_scalar_prefetch=1, grid=(nb, hr // tr), in_specs=[gspec],
            out_specs=pl.BlockSpec((None, tr, hc), lambda b, i, c_ref: (b, i, 0))),
        out_shape=jax.ShapeDtypeStruct((nb, hr, hc), BF16), compiler_params=_cparams(("parallel", "parallel")),
    )(c_idx, g4)


def _adamw(name, ws, gs, ms, vs):
    n = len(ws)
    c1 = 1.0 - ADAM_B1 ** ADAM_STEP
    c2 = 1.0 - ADAM_B2 ** ADAM_STEP

    def fn(*tiles):
        out = []
        for k in range(n):
            w_t, g_t, m_t, v_t = tiles[4 * k:4 * k + 4]
            m_n = ADAM_B1 * m_t + (1.0 - ADAM_B1) * g_t
            v_n = ADAM_B2 * v_t + (1.0 - ADAM_B2) * (g_t * g_t)
            out += [-ADAM_LR * ((m_n / c1) / (jnp.sqrt(v_n / c2) + ADAM_EPS) + ADAM_WD * w_t), m_n, v_n]
        return out

    rows, cdim = ws[0].shape
    tiled = [a for quad in zip(ws, gs, ms, vs) for a in quad]
    pref = (512, 352, 256, 128, 64, 8) if n == 1 else (176, 128, 64, 8)
    res = _rowwise(name, fn, tiled, [], [(cdim, F32)] * (3 * n), tm=_pick(rows, pref))
    return [tuple(res[3 * k:3 * k + 3]) for k in range(n)]


BIG = ("ffn1_w_gate", "ffn1_w_up", "ffn1_w_down", "w_in", "w_out", "ffn2_w_gate", "ffn2_w_up", "ffn2_w_down",
       "w_ple_gate", "w_ple_proj")
SMALL = ("ffn1_norm", "mix_norm", "b_mlstm_gates", "b_fox_f", "mlstm_out_norm", "fox_out_norm", "ffn2_norm",
         "ple_gate_norm", "ple_proj_norm", "final_norm")
WEIGHTS = ("ffn1_norm", "ffn1_w_gate", "ffn1_w_up", "ffn1_w_down", "mix_norm", "w_in", "conv_qk", "b_mlstm_gates",
           "b_fox_f", "mlstm_out_norm", "fox_out_norm", "w_out", "ffn2_norm", "ffn2_w_gate", "ffn2_w_up", "ffn2_w_down",
           "ple_gate_norm", "w_ple_gate", "w_ple_proj", "ple_proj_norm", "final_norm")
TRANSPOSED = ("ffn1_w_gate", "ffn1_w_up", "w_in", "ffn2_w_gate", "ffn2_w_up")
PACK_W = 1024


def _chip_blocks(a):
    r, c4 = a.shape
    return a.reshape(r, 4, c4 // 4).transpose(1, 0, 2)


def _from_chip_blocks(a):
    nb, r, c = a.shape
    return a.transpose(1, 0, 2).reshape(r, nb * c)


def kernel(x, p, ffn1_norm, ffn1_w_gate, ffn1_w_up, ffn1_w_down, mix_norm, w_in, conv_qk, b_mlstm_gates, b_fox_f, mlstm_out_norm, fox_out_norm, w_out, ffn2_norm, ffn2_w_gate, ffn2_w_up, ffn2_w_down, ple_gate_norm, w_ple_gate, w_ple_proj, ple_proj_norm, final_norm, loss_target, m_ffn1_norm, m_ffn1_w_gate, m_ffn1_w_up, m_ffn1_w_down, m_mix_norm, m_w_in, m_conv_qk, m_b_mlstm_gates, m_b_fox_f, m_mlstm_out_norm, m_fox_out_norm, m_w_out, m_ffn2_norm, m_ffn2_w_gate, m_ffn2_w_up, m_ffn2_w_down, m_ple_gate_norm, m_w_ple_gate, m_w_ple_proj, m_ple_proj_norm, m_final_norm, v_ffn1_norm, v_ffn1_w_gate, v_ffn1_w_up, v_ffn1_w_down, v_mix_norm, v_w_in, v_conv_qk, v_b_mlstm_gates, v_b_fox_f, v_mlstm_out_norm, v_fox_out_norm, v_w_out, v_ffn2_norm, v_ffn2_w_gate, v_ffn2_w_up, v_ffn2_w_down, v_ple_gate_norm, v_w_ple_gate, v_w_ple_proj, v_ple_proj_norm, v_final_norm):
    w = dict(ffn1_norm=ffn1_norm, ffn1_w_gate=ffn1_w_gate, ffn1_w_up=ffn1_w_up, ffn1_w_down=ffn1_w_down, mix_norm=mix_norm,
             w_in=w_in, conv_qk=conv_qk, b_mlstm_gates=b_mlstm_gates, b_fox_f=b_fox_f, mlstm_out_norm=mlstm_out_norm,
             fox_out_norm=fox_out_norm, w_out=w_out, ffn2_norm=ffn2_norm, ffn2_w_gate=ffn2_w_gate, ffn2_w_up=ffn2_w_up,
             ffn2_w_down=ffn2_w_down, ple_gate_norm=ple_gate_norm, w_ple_gate=w_ple_gate, w_ple_proj=w_ple_proj,
             ple_proj_norm=ple_proj_norm, final_norm=final_norm)
    m = dict(ffn1_norm=m_ffn1_norm, ffn1_w_gate=m_ffn1_w_gate, ffn1_w_up=m_ffn1_w_up, ffn1_w_down=m_ffn1_w_down,
             mix_norm=m_mix_norm, w_in=m_w_in, conv_qk=m_conv_qk, b_mlstm_gates=m_b_mlstm_gates, b_fox_f=m_b_fox_f,
             mlstm_out_norm=m_mlstm_out_norm, fox_out_norm=m_fox_out_norm, w_out=m_w_out, ffn2_norm=m_ffn2_norm,
             ffn2_w_gate=m_ffn2_w_gate, ffn2_w_up=m_ffn2_w_up, ffn2_w_down=m_ffn2_w_down, ple_gate_norm=m_ple_gate_norm,
             w_ple_gate=m_w_ple_gate, w_ple_proj=m_w_ple_proj, ple_proj_norm=m_ple_proj_norm, final_norm=m_final_norm)
    v = dict(ffn1_norm=v_ffn1_norm, ffn1_w_gate=v_ffn1_w_gate, ffn1_w_up=v_ffn1_w_up, ffn1_w_down=v_ffn1_w_down,
             mix_norm=v_mix_norm, w_in=v_w_in, conv_qk=v_conv_qk, b_mlstm_gates=v_b_mlstm_gates, b_fox_f=v_b_fox_f,
             mlstm_out_norm=v_mlstm_out_norm, fox_out_norm=v_fox_out_norm, w_out=v_w_out, ffn2_norm=v_ffn2_norm,
             ffn2_w_gate=v_ffn2_w_gate, ffn2_w_up=v_ffn2_w_up, ffn2_w_down=v_ffn2_w_down, ple_gate_norm=v_ple_gate_norm,
             w_ple_gate=v_w_ple_gate, w_ple_proj=v_w_ple_proj, ple_proj_norm=v_ple_proj_norm, final_norm=v_final_norm)
    shapes = {n: w[n].shape for n in WEIGHTS}

    def view(a, n):
        return a[0].T if n in TRANSPOSED else a.reshape(-1, a.shape[-1])

    def unview(a, n):
        return (a.T if n in TRANSPOSED else a).reshape(shapes[n])

    w2, m2, v2 = ({n: view(a, n) for n, a in d.items()} for d in (w, m, v))

    c_idx = lax.axis_index("c").astype(jnp.int32).reshape(1)
    me_idx = (2 * lax.axis_index("x") + lax.axis_index("y")).astype(jnp.int32).reshape(1)
    place = jnp.concatenate([me_idx, c_idx])
    slot = {}
    for grp in SAME_SHAPE:
        slot.update(zip(grp, _to_slot("slot_" + grp[0], [w2[n] for n in grp], me_idx, BF16)))
    slot["conv_qk"] = _to_slot("slot_conv_qk", [w2["conv_qk"]], me_idx, F32)[0]
    wg1, wu1, wd1 = _gather4("gather_ffn1", [slot[n] for n in FFN1], [SPLIT[n] for n in FFN1])
    sp = {n: w2[n] for n in SMALL}
    loss_part, grad_x, grads, gs, conv_grad = _local_step(
        x[0], p[0, 0], loss_target[0], sp, wg1, wu1, wd1, [slot[n] for n in REST + ("conv_qk",)], c_idx, place)

    small = [gs[n].reshape(1, -1) for n in SMALL] + [conv_grad, loss_part]
    rows = [jnp.pad(a, ((0, 0), (0, PACK_W - a.shape[1]))) for a in small]
    packed = jnp.concatenate(rows, axis=0)
    packed = jnp.pad(packed, ((0, -packed.shape[0] % 8), (0, 0)))
    red = _allreduce_small(packed)
    loss = red[len(SMALL) + CONV_W, 0]
    for i, n in enumerate(SMALL):
        grads[n] = red[i:i + 1, :gs[n].size]
    dconv = red[len(SMALL):len(SMALL) + CONV_W, :conv_grad.shape[1]]
    cw = conv_qk.shape[-1]
    grads["conv_qk"] = lax.dynamic_slice_in_dim(dconv, (2 * lax.axis_index("x") + lax.axis_index("y")) * cw, cw, axis=1)

    outs = {}
    for grp in SAME_SHAPE + tuple((n,) for n in WEIGHTS if n not in BIG):
        g2s = [grads[n].reshape(w2[n].shape) for n in grp]
        res = _adamw("adamw_" + grp[0], [w2[n] for n in grp], g2s, [m2[n] for n in grp], [v2[n] for n in grp])
        for n, g2, (d, nm, nv) in zip(grp, g2s, res):
            outs[n] = tuple(unview(a, n) for a in (g2, d, nm, nv))
    return (loss, grad_x[None], *[outs[n][0] for n in WEIGHTS], *[outs[n][1] for n in WEIGHTS],
            *[outs[n][2] for n in WEIGHTS], *[outs[n][3] for n in WEIGHTS])
```

```python
import jax
import jax.numpy as jnp
from jax import lax
from jax.experimental import pallas as pl
from jax.experimental.pallas import tpu as pltpu

F32 = jnp.float32
BF16 = jnp.bfloat16
EPS = 1e-6
NH_M, DK_M, DV_M = 4, 64, 128
NH_F, DH_F = 8, 64
CONV_W = 4
ADAM_LR, ADAM_B1, ADAM_B2, ADAM_EPS, ADAM_WD, ADAM_STEP = 0.001, 0.9, 0.999, 1e-08, 0.01, 10
VMEM_LIMIT = 56 * 1024 * 1024


def _cparams(sem):
    return pltpu.CompilerParams(dimension_semantics=sem, vmem_limit_bytes=VMEM_LIMIT)


def _sigmoid(x):
    return 1.0 / (1.0 + jnp.exp(-x))


def _dot(a, b, ca, cb):
    return lax.dot_general(a.astype(BF16), b.astype(BF16), (((ca,), (cb,)), ((), ())), preferred_element_type=F32)


def _rowwise(name, fn, tiled, full, outs, accs=(), tm=512, plan=None):
    rows = tiled[0].shape[0]
    tm = min(tm, rows)
    assert rows % tm == 0
    n_t, n_f, n_o, n_a = len(tiled), len(full), len(outs), len(accs)
    nt = rows // tm
    host = _Hosted(plan, n_t + n_f, n_o + n_a)

    def body(*refs):
        in_refs, orefs, _, prefs = host.split(refs)
        host.run(0, pl.program_id(0) == 0, prefs)
        host.run(1, pl.program_id(0) == 0, prefs)
        ins = [r[...] for r in in_refs]
        res = fn(*ins)
        if not isinstance(res, (tuple, list)):
            res = (res,)
        for r, v in zip(orefs[:n_o], res[:n_o]):
            r[...] = v.astype(r.dtype)
        if n_a:
            @pl.when(pl.program_id(0) == 0)
            def _():
                for r in orefs[n_o:]:
                    r[...] = jnp.zeros_like(r)
            for r, v in zip(orefs[n_o:], res[n_o:]):
                r[...] += v.astype(r.dtype)
        host.run(2, pl.program_id(0) == nt - 1, prefs)

    in_specs = [pl.BlockSpec((tm, a.shape[1]), lambda i: (i, 0)) for a in tiled]
    in_specs += [pl.BlockSpec(a.shape, lambda i: (0, 0)) for a in full]
    out_specs = [pl.BlockSpec((tm, c), lambda i: (i, 0)) for c, _ in outs]
    out_specs += [pl.BlockSpec(s, lambda i: (0, 0)) for s, _ in accs]
    out_shape = [jax.ShapeDtypeStruct((rows, c), d) for c, d in outs]
    out_shape += [jax.ShapeDtypeStruct(s, d) for s, d in accs]
    hc = host.call_args()
    res = pl.pallas_call(
        body, name=name, grid=(nt,), in_specs=in_specs + hc["in_specs"], out_specs=out_specs + hc["out_specs"],
        out_shape=out_shape + hc["out_shape"], scratch_shapes=hc["scratch"], input_output_aliases=hc["aliases"],
        compiler_params=_cparams(("arbitrary",) if (n_a or plan is not None) else ("parallel",)),
    )(*tiled, *full, *hc["args"])
    return res if plan is None else (res[:n_o + n_a], res[n_o + n_a:])


def _colsum(v):
    return jnp.sum(v, axis=0, keepdims=True)


def _rms_fwd_val(x, g):
    r = lax.rsqrt(jnp.mean(x * x, axis=-1, keepdims=True) + EPS)
    return x * r * g


def _rms_bwd_val(dy, x, g):
    r = lax.rsqrt(jnp.mean(x * x, axis=-1, keepdims=True) + EPS)
    xh = x * r
    dxh = dy * g
    dx = r * (dxh - xh * jnp.mean(dxh * xh, axis=-1, keepdims=True))
    return dx, _colsum(dy * xh)


def _mm(name, pairs, out_shape, out_block, out_map, grid, kaxis, ta=False, tb=False, scale=None, res=None,
        out_dtype=F32, plan=None, twin=False):
    n_o = 2 if twin else 1
    nk = grid[kaxis]
    npairs = len(pairs)
    ca, cb = (0 if ta else 1), (1 if tb else 0)
    acc_shape = tuple(d for d in out_block if d is not None)
    n_in = 2 * npairs + (1 if res is not None else 0)
    host = _Hosted(plan, n_in, n_o)

    def body(*refs):
        ins, o_refs, (acc_ref,), prefs = host.split(refs)
        o_ref = o_refs[0]
        in_refs = ins[: 2 * npairs]
        res_ref = ins[2 * npairs] if res is not None else None
        k = pl.program_id(kaxis)
        ids = [pl.program_id(a) for a in range(len(grid))]
        first, last = ids[0] == 0, ids[0] == grid[0] - 1
        for a in range(1, len(grid)):
            first, last = first & (ids[a] == 0), last & (ids[a] == grid[a] - 1)
        host.run(0, first, prefs)
        host.run(1, first, prefs)

        @pl.when(k == 0)
        def _():
            acc_ref[...] = jnp.zeros_like(acc_ref)

        part = None
        for p in range(npairs):
            d = _dot(in_refs[2 * p][...], in_refs[2 * p + 1][...], ca, cb)
            part = d if part is None else part + d
        acc_ref[...] += part

        @pl.when(k == nk - 1)
        def _():
            v = acc_ref[...]
            if scale is not None:
                v = v * scale
            if res_ref is not None:
                v = v + res_ref[...].astype(F32)
            o_ref[...] = v.astype(o_ref.dtype)
            if twin:
                o_refs[1][...] = v.astype(BF16)

        host.run(2, last, prefs)

    in_specs, args = [], []
    for a, ab, am, b, bb, bm in pairs:
        in_specs += [pl.BlockSpec(ab, am), pl.BlockSpec(bb, bm)]
        args += [a, b]
    if res is not None:
        in_specs.append(pl.BlockSpec(out_block, out_map))
        args.append(res)
    sem = tuple("arbitrary" if (i == kaxis or plan is not None) else "parallel" for i in range(len(grid)))
    hc = host.call_args()
    out = pl.pallas_call(
        body, name=name, grid=grid, in_specs=in_specs + hc["in_specs"],
        out_specs=[pl.BlockSpec(out_block, out_map)] * n_o + hc["out_specs"],
        out_shape=[jax.ShapeDtypeStruct(out_shape, out_dtype)] + [jax.ShapeDtypeStruct(out_shape, BF16)] * (n_o - 1)
        + hc["out_shape"],
        scratch_shapes=[pltpu.VMEM(acc_shape, F32)] + hc["scratch"], input_output_aliases=hc["aliases"],
        compiler_params=_cparams(sem),
    )(*args, *hc["args"])
    res_out = tuple(out[:2]) if twin else out[0]
    return res_out if plan is None else (res_out, out[n_o:])


def _pick(n, pref):
    for t in pref:
        if n % t == 0:
            return t
    return n


def _mm_nn(name, a, b, tm=512, tn=512, tk=512, **kw):
    (m, k), n = a.shape, b.shape[1]
    tm, tn, tk = _pick(m, (tm, 256, 128)), _pick(n, (tn, 256, 128)), _pick(k, (tk, 256, 128))
    return _mm(name, [(a, (tm, tk), lambda i, j, kk: (i, kk), b, (tk, tn), lambda i, j, kk: (kk, j))],
               (m, n), (tm, tn), lambda i, j, kk: (i, j), (m // tm, n // tn, k // tk), 2, **kw)


def _mm_nt(name, a, b, tm=512, tn=512, tk=512, **kw):
    (m, k), n = a.shape, b.shape[0]
    tm, tn, tk = _pick(m, (tm, 256, 128)), _pick(n, (tn, 256, 128)), _pick(k, (tk, 256, 128))
    return _mm(name, [(a, (tm, tk), lambda i, j, kk: (i, kk), b, (tn, tk), lambda i, j, kk: (j, kk))],
               (m, n), (tm, tn), lambda i, j, kk: (i, j), (m // tm, n // tn, k // tk), 2, tb=True, **kw)


def _mm_tn(name, a, b, tm=512, tn=512, tk=4096, **kw):
    (k, m), n = a.shape, b.shape[1]
    tm, tn, tk = _pick(m, (tm, 256, 128)), _pick(n, (tn, 256, 128)), _pick(k, (tk, 2048, 1024, 512, 256, 128))
    return _mm(name, [(a, (tk, tm), lambda i, j, kk: (kk, i), b, (tk, tn), lambda i, j, kk: (kk, j))],
               (m, n), (tm, tn), lambda i, j, kk: (i, j), (m // tm, n // tn, k // tk), 2, ta=True, **kw)


def _norm_mm(name, h, gamma, w, w_transposed, out_dtype):
    t, d = h.shape
    n = w.shape[0] if w_transposed else w.shape[1]
    tm, tn = _pick(t, (512, 256)), _pick(n, (1024, 512, 256, 128))

    def body(h_ref, gam_ref, w_ref, xn_ref, o_ref, xn_scr):
        @pl.when(pl.program_id(1) == 0)
        def _():
            xn = _rms_fwd_val(h_ref[...], gam_ref[...]).astype(BF16)
            xn_scr[...] = xn
            xn_ref[...] = xn

        o_ref[...] = _dot(xn_scr[...], w_ref[...], 1, 1 if w_transposed else 0).astype(o_ref.dtype)

    wspec = pl.BlockSpec((tn, d), lambda i, j: (j, 0)) if w_transposed else pl.BlockSpec((d, tn), lambda i, j: (0, j))
    return pl.pallas_call(
        body, name=name, grid=(t // tm, n // tn),
        in_specs=[pl.BlockSpec((tm, d), lambda i, j: (i, 0)), pl.BlockSpec((1, d), lambda i, j: (0, 0)), wspec],
        out_specs=[pl.BlockSpec((tm, d), lambda i, j: (i, 0)), pl.BlockSpec((tm, tn), lambda i, j: (i, j))],
        out_shape=[jax.ShapeDtypeStruct((t, d), BF16), jax.ShapeDtypeStruct((t, n), out_dtype)],
        scratch_shapes=[pltpu.VMEM((tm, d), BF16)], compiler_params=_cparams(("parallel", "arbitrary")),
    )(h, gamma, w)


CHAIN_ROWS = 256


def _row_chains(tm):
    n = max(tm // CHAIN_ROWS, 1)
    return [slice(r * (tm // n), (r + 1) * (tm // n)) for r in range(n)]


def _ffn_fwd(pfx, h, gamma, wg, wu, wd, plan=None):
    t, d = h.shape
    nb, f, _ = wg.shape
    tm = _pick(t, (1024, 512, 256))
    nt = t // tm
    host = _Hosted(plan, 5, 4)

    def body(*refs):
        (h_ref, gam_ref, wg_ref, wu_ref, wd_ref), (ho_ref, xn_ref, g_ref, u_ref), (xn_scr, acc_ref), prefs = host.split(refs)
        i, j = pl.program_id(0), pl.program_id(1)
        host.run(0, (i == 0) & (j == 0), prefs)
        host.run(1, (i == nt // 2) & (j == 0), prefs)

        @pl.when(j == 0)
        def _():
            xn = _rms_fwd_val(h_ref[...], gam_ref[...]).astype(BF16)
            xn_scr[...] = xn
            xn_ref[...] = xn
            acc_ref[...] = jnp.zeros_like(acc_ref)

        for rows in _row_chains(tm):
            x = xn_scr[rows, :]
            g = _dot(x, wg_ref[...], 1, 1)
            u = _dot(x, wu_ref[...], 1, 1)
            g_ref[rows, :] = g.astype(BF16)
            u_ref[rows, :] = u.astype(BF16)
            acc_ref[rows, :] += _dot(g * _sigmoid(g) * u, wd_ref[...], 1, 0)

        @pl.when(j == nb - 1)
        def _():
            ho_ref[...] = h_ref[...] + 0.5 * acc_ref[...]

        host.run(2, (i == nt - 1) & (j == nb - 1), prefs)

    row = pl.BlockSpec((tm, d), lambda i, j: (i, 0))
    blk = pl.BlockSpec((None, tm, f), lambda i, j: (j, i, 0))
    wspec = pl.BlockSpec((None, f, d), lambda i, j: (j, 0, 0))
    hc = host.call_args()
    res = pl.pallas_call(
        body, name=pfx + "_fwd", grid=(nt, nb),
        in_specs=[row, pl.BlockSpec((1, d), lambda i, j: (0, 0)), wspec, wspec, wspec] + hc["in_specs"],
        out_specs=[row, row, blk, blk] + hc["out_specs"],
        out_shape=[jax.ShapeDtypeStruct((t, d), F32), jax.ShapeDtypeStruct((t, d), BF16),
                   jax.ShapeDtypeStruct((nb, t, f), BF16), jax.ShapeDtypeStruct((nb, t, f), BF16)] + hc["out_shape"],
        scratch_shapes=[pltpu.VMEM((tm, d), BF16), pltpu.VMEM((tm, d), F32)] + hc["scratch"],
        input_output_aliases=hc["aliases"], compiler_params=_cparams(("arbitrary", "arbitrary")),
    )(h, gamma, wg, wu, wd, *hc["args"])
    return res[:4], res[4:]


def _ffn_bwd(pfx, dh_out, h, gamma, xn, g_all, u_all, wg, wu, wd, plan=None):
    t, d = h.shape
    nb, f, _ = wg.shape
    tm = _pick(t, (512, 256))
    tk = _pick(t, (4096, 2048, 1024, 512, 256))

    nt = t // tm
    host = _Hosted(plan, 8, 6)

    def body(*refs):
        ((dy_ref, h_ref, gam_ref, wg_ref, wu_ref, wd_ref, g_ref, u_ref),
         (dh_ref, dgam_ref, dg_ref, du_ref, a_ref, dyb_ref), (acc_ref,), prefs) = host.split(refs)
        i, j = pl.program_id(0), pl.program_id(1)
        host.run(0, (i == 0) & (j == 0), prefs)
        host.run(1, (i == nt // 2) & (j == 0), prefs)

        @pl.when((i == 0) & (j == 0))
        def _():
            dgam_ref[...] = jnp.zeros_like(dgam_ref)

        @pl.when(j == 0)
        def _():
            acc_ref[...] = jnp.zeros_like(acc_ref)
            dyb_ref[...] = dy_ref[...].astype(BF16)

        for rows in _row_chains(tm):
            da = _dot(dy_ref[rows, :], wd_ref[...], 1, 1) * 0.5
            g = g_ref[rows, :].astype(F32)
            u = u_ref[rows, :].astype(F32)
            s = _sigmoid(g)
            sl = g * s
            du = (da * sl).astype(BF16)
            dg = (da * u * (s + sl * (1.0 - s))).astype(BF16)
            du_ref[rows, :] = du
            dg_ref[rows, :] = dg
            a_ref[rows, :] = (sl * u).astype(BF16)
            acc_ref[rows, :] += _dot(dg, wg_ref[...], 1, 0) + _dot(du, wu_ref[...], 1, 0)

        @pl.when(j == nb - 1)
        def _():
            dx, dgam = _rms_bwd_val(acc_ref[...], h_ref[...], gam_ref[...])
            dh_ref[...] = dy_ref[...] + dx
            dgam_ref[...] += dgam

        host.run(2, (i == nt - 1) & (j == nb - 1), prefs)

    row = pl.BlockSpec((tm, d), lambda i, j: (i, 0))
    vec = pl.BlockSpec((1, d), lambda i, j: (0, 0))
    blk = pl.BlockSpec((None, tm, f), lambda i, j: (j, i, 0))
    wspec = pl.BlockSpec((None, f, d), lambda i, j: (j, 0, 0))
    hc = host.call_args()
    res = pl.pallas_call(
        body, name=pfx + "_bwd", grid=(nt, nb),
        in_specs=[row, row, vec, wspec, wspec, wspec, blk, blk] + hc["in_specs"],
        out_specs=[row, vec, blk, blk, blk, row] + hc["out_specs"],
        out_shape=[jax.ShapeDtypeStruct((t, d), F32), jax.ShapeDtypeStruct((1, d), F32)]
        + [jax.ShapeDtypeStruct((nb, t, f), BF16)] * 3 + [jax.ShapeDtypeStruct((t, d), BF16)] + hc["out_shape"],
        scratch_shapes=[pltpu.VMEM((tm, d), F32)] + hc["scratch"], input_output_aliases=hc["aliases"],
        compiler_params=_cparams(("arbitrary", "arbitrary")),
    )(dh_out, h, gamma, wg, wu, wd, g_all, u_all, *hc["args"])
    dh, dgamma, dg_all, du_all, a_all, dyb = res[:6]

    xmap, bmap, omap = (lambda b, k: (k, 0)), (lambda b, k: (b, k, 0)), (lambda b, k: (b, 0, 0))
    dwg, tg = _mm(pfx + "_dwg", [(dg_all, (None, tk, f), bmap, xn, (tk, d), xmap)], (nb, f, d), (None, f, d), omap,
                  (nb, t // tk), 1, ta=True, twin=True)
    dwu, tu = _mm(pfx + "_dwu", [(du_all, (None, tk, f), bmap, xn, (tk, d), xmap)], (nb, f, d), (None, f, d), omap,
                  (nb, t // tk), 1, ta=True, twin=True)
    dwd, td = _mm(pfx + "_dwd", [(a_all, (None, tk, f), bmap, dyb, (tk, d), xmap)], (nb, f, d), (None, f, d), omap,
                  (nb, t // tk), 1, ta=True, scale=0.5, twin=True)
    return (dh, dgamma, dwg, dwu, dwd), res[6:], (tg, tu, td)


def _ffn_bwd_late_dx(pfx, dh_out, h, gamma, xn, g_all, u_all, wg, wu, wd, plan_gu, plans_dw, make_plan_dx):
    t, d = h.shape
    nb, f, _ = wg.shape
    tm = _pick(t, (512, 256))
    tk = _pick(t, (4096, 2048, 1024, 512, 256))
    nt = t // tm
    host_a = _Hosted(plan_gu, 4, 4)

    def body_a(*refs):
        (dy_ref, wd_ref, g_ref, u_ref), (dg_ref, du_ref, a_ref, dyb_ref), _, prefs = host_a.split(refs)
        i, j = pl.program_id(0), pl.program_id(1)
        host_a.run(0, (i == 0) & (j == 0), prefs)
        host_a.run(1, (i == 0) & (j == 0), prefs)

        @pl.when(j == 0)
        def _():
            dyb_ref[...] = dy_ref[...].astype(BF16)

        for rows in _row_chains(tm):
            da = _dot(dy_ref[rows, :], wd_ref[...], 1, 1) * 0.5
            g = g_ref[rows, :].astype(F32)
            u = u_ref[rows, :].astype(F32)
            s = _sigmoid(g)
            sl = g * s
            du_ref[rows, :] = (da * sl).astype(BF16)
            dg_ref[rows, :] = (da * u * (s + sl * (1.0 - s))).astype(BF16)
            a_ref[rows, :] = (sl * u).astype(BF16)
        host_a.run(2, (i == nt - 1) & (j == nb - 1), prefs)

    row = pl.BlockSpec((tm, d), lambda i, j: (i, 0))
    vec = pl.BlockSpec((1, d), lambda i, j: (0, 0))
    blk = pl.BlockSpec((None, tm, f), lambda i, j: (j, i, 0))
    wspec = pl.BlockSpec((None, f, d), lambda i, j: (j, 0, 0))
    hc = host_a.call_args()
    res_a = pl.pallas_call(
        body_a, name=pfx + "_bwd_gu", grid=(nt, nb), in_specs=[row, wspec, blk, blk] + hc["in_specs"],
        out_specs=[blk] * 3 + [row] + hc["out_specs"],
        out_shape=[jax.ShapeDtypeStruct((nb, t, f), BF16)] * 3 + [jax.ShapeDtypeStruct((t, d), BF16)] + hc["out_shape"],
        scratch_shapes=hc["scratch"], input_output_aliases=hc["aliases"], compiler_params=_cparams(("arbitrary", "arbitrary")),
    )(dh_out, wd, g_all, u_all, *hc["args"])
    dg_all, du_all, a_all, dyb = res_a[:4]

    xmap, bmap, omap = (lambda b, k: (k, 0)), (lambda b, k: (b, k, 0)), (lambda b, k: (b, 0, 0))
    def dw(name, a, b, plan, scale=None):
        r = _mm(pfx + name, [(a, (None, tk, f), bmap, b, (tk, d), xmap)], (nb, f, d), (None, f, d), omap, (nb, t // tk), 1,
                ta=True, scale=scale, plan=plan, twin=True)
        return r if plan is not None else (r, ())

    (dwd, td), out_d = dw("_dwd", a_all, dyb, plans_dw[0], 0.5)
    (dwg, tg), out_g = dw("_dwg", dg_all, xn, plans_dw[1])
    (dwu, tu), out_u = dw("_dwu", du_all, xn, plans_dw[2])

    plan_dx = make_plan_dx((dwg, dwu, dwd), (tg, tu, td))
    host_b = _Hosted(plan_dx, 7, 2)

    def body_b(*refs):
        (dy_ref, h_ref, gam_ref, wg_ref, wu_ref, dg_ref, du_ref), (dh_ref, dgam_ref), (acc_ref,), prefs = host_b.split(refs)
        i, j = pl.program_id(0), pl.program_id(1)
        host_b.run(0, (i == 0) & (j == 0), prefs)
        host_b.run(1, (i == 0) & (j == 0), prefs)

        @pl.when((i == 0) & (j == 0))
        def _():
            dgam_ref[...] = jnp.zeros_like(dgam_ref)

        @pl.when(j == 0)
        def _():
            acc_ref[...] = jnp.zeros_like(acc_ref)

        acc_ref[...] += _dot(dg_ref[...], wg_ref[...], 1, 0) + _dot(du_ref[...], wu_ref[...], 1, 0)

        @pl.when(j == nb - 1)
        def _():
            dx, dgam = _rms_bwd_val(acc_ref[...], h_ref[...], gam_ref[...])
            dh_ref[...] = dy_ref[...] + dx
            dgam_ref[...] += dgam

        host_b.run(2, (i == nt - 1) & (j == nb - 1), prefs)

    hc = host_b.call_args()
    res_b = pl.pallas_call(
        body_b, name=pfx + "_bwd_dx", grid=(nt, nb), in_specs=[row, row, vec, wspec, wspec, blk, blk] + hc["in_specs"],
        out_specs=[row, vec] + hc["out_specs"],
        out_shape=[jax.ShapeDtypeStruct((t, d), F32), jax.ShapeDtypeStruct((1, d), F32)] + hc["out_shape"],
        scratch_shapes=[pltpu.VMEM((tm, d), F32)] + hc["scratch"], input_output_aliases=hc["aliases"],
        compiler_params=_cparams(("arbitrary", "arbitrary")),
    )(dh_out, h, gamma, wg, wu, dg_all, du_all, *hc["args"])
    return (res_b[0], res_b[1], dwg, dwu, dwd), (res_a[4:], out_d, out_g, out_u, res_b[2:])


HALO = 16


def _silu_grad(y):
    s = _sigmoid(y)
    return s * (1.0 + y * (1.0 - s))


def _with_halo(ref, i, n_tiles, tm, before, after):
    t = ref.shape[0]
    r0 = pl.multiple_of(i * tm, tm)
    parts = [ref[pl.ds(r0, tm), :].astype(F32)]
    if before:
        prev = ref[pl.ds(pl.multiple_of(jnp.maximum(r0 - HALO, 0), HALO), HALO), :].astype(F32)
        parts.insert(0, jnp.where(i > 0, prev, 0.0))
    if after:
        nxt = ref[pl.ds(pl.multiple_of(jnp.minimum(r0 + tm, t - HALO), HALO), HALO), :].astype(F32)
        parts.append(jnp.where(i < n_tiles - 1, nxt, 0.0))
    return jnp.concatenate(parts, axis=0)


def _conv_fwd(zbig, w):
    t, c = zbig.shape[0], w.shape[1]
    tm = _pick(t, (512, 256))
    nt = t // tm

    def body(x_ref, w_ref, o_ref):
        xe = _with_halo(x_ref, pl.program_id(0), nt, tm, True, False)
        wv = w_ref[...]
        y = xe * wv[3:4, :]
        for i in range(CONV_W - 1):
            y = y + pltpu.roll(xe, CONV_W - 1 - i, 0) * wv[i:i + 1, :]
        y = y[HALO:, :]
        o_ref[...] = (y * _sigmoid(y)).astype(o_ref.dtype)

    return pl.pallas_call(
        body, name="conv_fwd", grid=(nt,),
        in_specs=[pl.BlockSpec((t, c), lambda i: (0, 0)), pl.BlockSpec(w.shape, lambda i: (0, 0))],
        out_specs=pl.BlockSpec((tm, c), lambda i: (i, 0)), out_shape=jax.ShapeDtypeStruct((t, c), BF16),
        compiler_params=_cparams(("parallel",)),
    )(zbig, w)


def _conv_bwd(zbig, dact, w):
    t, c = dact.shape
    tm = _pick(t, (512, 256))
    nt = t // tm
    n = tm + HALO

    def body(x_ref, d_ref, w_ref, dx_ref, dw_ref):
        xe = _with_halo(x_ref, pl.program_id(0), nt, tm, True, True)
        de = _with_halo(d_ref, pl.program_id(0), nt, tm, False, True)
        wv = w_ref[...]
        sh = [pltpu.roll(xe, CONV_W - 1 - i, 0)[HALO:, :] if i < CONV_W - 1 else xe[HALO:, :] for i in range(CONV_W)]
        y = sh[0] * wv[0:1, :]
        for i in range(1, CONV_W):
            y = y + sh[i] * wv[i:i + 1, :]
        dy = de * _silu_grad(y)
        dx = dy * wv[3:4, :]
        for i in range(CONV_W - 1):
            dx = dx + pltpu.roll(dy, n - (CONV_W - 1 - i), 0) * wv[i:i + 1, :]
        dx_ref[...] = dx[:tm, :].astype(dx_ref.dtype)
        dyc = dy[:tm, :]
        dwp = jnp.concatenate([_colsum(dyc * sh[i][:tm, :]) for i in range(CONV_W)], axis=0)

        @pl.when(pl.program_id(0) == 0)
        def _():
            dw_ref[...] = jnp.zeros_like(dw_ref)
        dw_ref[...] += dwp

    return pl.pallas_call(
        body, name="conv_bwd", grid=(nt,),
        in_specs=[pl.BlockSpec((t, c), lambda i: (0, 0)), pl.BlockSpec((t, c), lambda i: (0, 0)),
                  pl.BlockSpec(w.shape, lambda i: (0, 0))],
        out_specs=[pl.BlockSpec((tm, c), lambda i: (i, 0)), pl.BlockSpec(w.shape, lambda i: (0, 0))],
        out_shape=[jax.ShapeDtypeStruct((t, c), BF16), jax.ShapeDtypeStruct(w.shape, F32)],
        compiler_params=_cparams(("arbitrary",)),
    )(zbig, dact, w)


LM = 256
HI = lax.Precision.HIGHEST


def _logsig(x):
    return jnp.minimum(x, 0.0) - jnp.log(1.0 + jnp.exp(-jnp.abs(x)))


def _tri(n, lower):
    r = lax.broadcasted_iota(jnp.int32, (n, n), 0)
    c = lax.broadcasted_iota(jnp.int32, (n, n), 1)
    return (r >= c) if lower else (r <= c)


def _f32dot(a, b):
    return lax.dot_general(a, b, (((1,), (0,)), ((), ())), precision=HI, preferred_element_type=F32)


def _tri_dot(a, b, a_is_tri):
    tri = (a if a_is_tri else b).astype(BF16)
    parts = _split3(b if a_is_tri else a)
    outs = [_dot(tri, p, 1, 0) if a_is_tri else _dot(p, tri, 1, 0) for p in parts]
    return (outs[0] + outs[1]) + outs[2]


def _mlstm_decays(zs_ref, zsr_ref, bc_ref, br_ref):
    l = LM
    lf_c = _logsig(zs_ref[:, 0:2 * NH_M] + bc_ref[...])
    lf_r = _logsig(zsr_ref[...] + br_ref[...])
    low, up = _tri(l, True), _tri(l, False)
    return _tri_dot(low, lf_c, True), _tri_dot(lf_r, up, False), low, up


def _mlstm_chunk(h, q_ref, k_ref, v_ref, zs_ref, zsr_ref, bc_ref, br_ref, c_prev, m_prev, decays):
    l = LM
    q = q_ref[:, h * DK_M:(h + 1) * DK_M].astype(F32) * (DK_M ** -0.5)
    k = k_ref[:, h * DK_M:(h + 1) * DK_M]
    v = v_ref[:, h * DV_M:(h + 1) * DV_M]
    lane = lax.broadcasted_iota(jnp.int32, (l, DV_M), 1)
    v1 = jnp.concatenate([v, (lane == 0).astype(v.dtype)], axis=1)
    zs, zsr = zs_ref[...], zsr_ref[...]
    li_c = zs[:, h:h + 1] + bc_ref[:, h:h + 1]
    fp_c = zs[:, NH_M + h:NH_M + h + 1] + bc_ref[:, NH_M + h:NH_M + h + 1]
    li_r = zsr[h:h + 1, :] + br_ref[h:h + 1, :]
    fp_r = zsr[NH_M + h:NH_M + h + 1, :] + br_ref[NH_M + h:NH_M + h + 1, :]
    low = decays[2]
    b_c = decays[0][:, NH_M + h:NH_M + h + 1]
    b_r = decays[1][NH_M + h:NH_M + h + 1, :]
    g = b_r[:, l - 1:l]
    dmat = jnp.where(low, b_c - b_r + li_r, -jnp.inf)
    inter = b_c + m_prev
    m_t = jnp.maximum(inter, jnp.max(dmat, axis=1, keepdims=True))
    w_inter = jnp.exp(inter - m_t)
    amat = jnp.exp(dmat - m_t)
    s = _dot(q, k, 1, 1)
    p = amat * s
    qc = _dot(q, c_prev, 1, 0)
    qc_w = w_inter * qc
    num1 = qc_w + _dot(p, v1, 1, 0)
    den = num1[:, DV_M:DV_M + 1]
    mx = jnp.maximum(jnp.abs(den), jnp.exp(-m_t))
    hh = num1[:, :DV_M] / mx
    a_c = g - b_c + li_c
    return dict(q=q, k=k, v1=v1, fp_c=fp_c, fp_r=fp_r, b_c=b_c, g=g, m_t=m_t, w_inter=w_inter, amat=amat, s=s, p=p,
                qc_w=qc_w, den=den, mx=mx, hh=hh, a_c=a_c)


def _mlstm_fwd(qk, zbig, zs, zsr, bc, br, gm):
    t = zs.shape[0]
    l = LM
    nc = t // l
    dm = NH_M * DV_M

    def body(q_ref, k_ref, v_ref, o_ref, zs_ref, zsr_ref, bc_ref, br_ref, gm_ref, y_ref, cst_ref, mst_ref, c_scr, m_scr):
        @pl.when(pl.program_id(0) == 0)
        def _():
            c_scr[...] = jnp.zeros_like(c_scr)
            m_scr[...] = jnp.zeros_like(m_scr)

        cst_ref[...] = c_scr[...]
        mst_ref[...] = m_scr[...]
        ys = []
        decays = _mlstm_decays(zs_ref, zsr_ref, bc_ref, br_ref)
        for h in range(NH_M):
            c_prev = c_scr[h]
            m_prev = m_scr[h:h + 1, 0:1]
            r = _mlstm_chunk(h, q_ref, k_ref, v_ref, zs_ref, zsr_ref, bc_ref, br_ref, c_prev, m_prev, decays)
            hh = r["hh"]
            gh = gm_ref[:, h * DV_M:(h + 1) * DV_M]
            hn = hh * lax.rsqrt(jnp.mean(hh * hh, axis=-1, keepdims=True) + EPS) * gh
            og = o_ref[:, h * DV_M:(h + 1) * DV_M].astype(F32)
            ys.append(hn * _sigmoid(og))
            m_new = jnp.maximum(r["g"] + m_prev, jnp.max(r["a_c"], axis=0, keepdims=True))
            decay = jnp.exp(r["g"] + m_prev - m_new)
            wk = r["k"].astype(F32) * jnp.exp(r["a_c"] - m_new)
            c_scr[h] = decay * c_prev + _dot(wk, r["v1"], 0, 0)
            m_scr[h:h + 1, :] = jnp.broadcast_to(m_new, (1, 128))
        y_ref[...] = jnp.concatenate(ys, axis=1).astype(y_ref.dtype)

    return pl.pallas_call(
        body, name="mlstm_fwd", grid=(nc,),
        in_specs=[pl.BlockSpec((l, NH_M * DK_M), lambda i: (i, 0)), pl.BlockSpec((l, NH_M * DK_M), lambda i: (i, 1)),
                  pl.BlockSpec((l, dm), lambda i: (i, 1)), pl.BlockSpec((l, dm), lambda i: (i, 2)),
                  pl.BlockSpec((l, 128), lambda i: (i, 0)), pl.BlockSpec((8, l), lambda i: (0, i)),
                  pl.BlockSpec((1, 8), lambda i: (0, 0)), pl.BlockSpec((8, 1), lambda i: (0, 0)),
                  pl.BlockSpec((1, dm), lambda i: (0, 0))],
        out_specs=[pl.BlockSpec((l, dm), lambda i: (i, 0)), pl.BlockSpec((None, NH_M, DK_M, 2 * DV_M), lambda i: (i, 0, 0, 0)),
                   pl.BlockSpec((None, 8, 128), lambda i: (i, 0, 0))],
        out_shape=[jax.ShapeDtypeStruct((t, dm), BF16), jax.ShapeDtypeStruct((nc, NH_M, DK_M, 2 * DV_M), F32),
                   jax.ShapeDtypeStruct((nc, 8, 128), F32)],
        scratch_shapes=[pltpu.VMEM((NH_M, DK_M, 2 * DV_M), F32), pltpu.VMEM((8, 128), F32)],
        compiler_params=_cparams(("arbitrary",)),
    )(qk, qk, zbig, zbig, zs, zsr, bc, br, gm)


def _mlstm_bwd(qk, zbig, zs, zsr, bc, br, gm, cst, mst, dycat):
    t = zs.shape[0]
    l = LM
    nc = t // l
    dm = NH_M * DV_M

    def body(q_ref, k_ref, v_ref, o_ref, zs_ref, zsr_ref, bc_ref, br_ref, gm_ref, cst_ref, mst_ref, cnx_ref, mnx_ref,
             dy_ref, dqk_ref, dv_ref, do_ref, dzs_ref, dzr_ref, dgm_ref, dc_scr):
        @pl.when(pl.program_id(0) == 0)
        def _():
            dc_scr[...] = jnp.zeros_like(dc_scr)
            dgm_ref[...] = jnp.zeros_like(dgm_ref)

        lane = lax.broadcasted_iota(jnp.int32, (l, 128), 1)
        db_all, sig_c, carries = jnp.zeros((l, 128), F32), jnp.zeros((l, 128), F32), jnp.zeros((1, 128), F32)
        decays = _mlstm_decays(zs_ref, zsr_ref, bc_ref, br_ref)
        lower, upper = decays[2], decays[3]
        dzr_rows = [None] * 8
        dvs, dos, dgs, dqs, dks = [], [], [], [], []
        dzs = jnp.zeros((l, 128), F32)
        for h in range(NH_M):
            c_prev = cst_ref[h]
            m_prev = mst_ref[h:h + 1, 0:1]
            r = _mlstm_chunk(h, q_ref, k_ref, v_ref, zs_ref, zsr_ref, bc_ref, br_ref, c_prev, m_prev, decays)
            hh, mx, den, m_t, v1, amat = r["hh"], r["mx"], r["den"], r["m_t"], r["v1"], r["amat"]
            gh = gm_ref[:, h * DV_M:(h + 1) * DV_M]
            rs = lax.rsqrt(jnp.mean(hh * hh, axis=-1, keepdims=True) + EPS)
            xh = hh * rs
            sg = _sigmoid(o_ref[:, h * DV_M:(h + 1) * DV_M].astype(F32))
            dyh = dy_ref[:, h * DV_M:(h + 1) * DV_M]
            dos.append(dyh * xh * gh * sg * (1.0 - sg))
            dhn = dyh * sg
            dgs.append(_colsum(dhn * xh))
            dxh = dhn * gh
            dh = rs * (dxh - xh * jnp.mean(dxh * xh, axis=-1, keepdims=True))
            g1 = dh / mx
            hd = jnp.sum(hh * dh, axis=-1, keepdims=True)
            dden = jnp.where(jnp.abs(den) > jnp.exp(-m_t), -hd / mx * jnp.sign(den), 0.0)
            g256 = jnp.concatenate([g1, jnp.where(lane == 0, dden, 0.0)], axis=1)
            dc_h = dc_scr[h]
            ea = jnp.exp(r["a_c"])
            dp = _dot(g256, v1, 1, 1)
            ds = dp * amat
            dqs.append((r["w_inter"] * _dot(g256, c_prev, 1, 1) + _dot(ds, r["k"], 1, 0)) * (DK_M ** -0.5))
            dks.append(_dot(ds, r["q"], 0, 0) + ea * _dot(v1, dc_h, 1, 1))
            dv_st = ea * _dot(r["k"], dc_h, 1, 0)
            dv1 = _dot(r["p"], g256, 0, 0) + dv_st
            dvs.append(dv1[:, :DV_M])
            wmat = dp * r["p"]
            c_in = _colsum(wmat)
            c_st = jnp.sum(v1.astype(F32) * dv_st, axis=-1, keepdims=True)
            r_t = jnp.sum(wmat, axis=1, keepdims=True) + jnp.sum(g256 * r["qc_w"], axis=-1, keepdims=True)
            db = r_t - c_st
            carry = jnp.exp(mnx_ref[h:h + 1, 0:1]) * jnp.sum(
                jnp.sum(dc_h * cnx_ref[h], axis=1, keepdims=True), axis=0, keepdims=True)
            db_all = db_all + jnp.where(lane == NH_M + h, db, 0.0)
            sig_c = sig_c + jnp.where(lane == NH_M + h, _sigmoid(-r["fp_c"]), 0.0)
            carries = carries + jnp.where(lane[0:1, :] == NH_M + h, carry, 0.0)
            dzs = dzs + jnp.where(lane == h, c_st, 0.0)
            dzr_rows[h] = c_in
            dzr_rows[NH_M + h] = _sigmoid(-r["fp_r"])
            wq = r["q"] * jnp.exp(r["b_c"] - m_t)
            dc_scr[h] = jnp.exp(r["g"]) * dc_h + _dot(wq, g256, 0, 0)
        dzs = dzs + (_tri_dot(upper, db_all, True) + carries) * sig_c
        c_in4 = jnp.concatenate(dzr_rows[:NH_M], axis=0)
        dlf_r4 = -_tri_dot(c_in4, lower, False)
        dzr_rows = dzr_rows[:NH_M] + [dlf_r4[h:h + 1, :] * dzr_rows[NH_M + h] for h in range(NH_M)]
        dqk_ref[...] = jnp.concatenate(dqs + dks, axis=1)
        dv_ref[...] = jnp.concatenate(dvs, axis=1).astype(dv_ref.dtype)
        do_ref[...] = jnp.concatenate(dos, axis=1).astype(do_ref.dtype)
        dzs_ref[...] = dzs
        dzr_ref[...] = jnp.concatenate(dzr_rows, axis=0)
        dgm_ref[...] += jnp.concatenate(dgs, axis=1)

    rev = lambda i: nc - 1 - i
    nxt = lambda i: jnp.minimum(nc - i, nc - 1)
    return pl.pallas_call(
        body, name="mlstm_bwd", grid=(nc,),
        in_specs=[pl.BlockSpec((l, NH_M * DK_M), lambda i: (rev(i), 0)), pl.BlockSpec((l, NH_M * DK_M), lambda i: (rev(i), 1)),
                  pl.BlockSpec((l, dm), lambda i: (rev(i), 1)), pl.BlockSpec((l, dm), lambda i: (rev(i), 2)),
                  pl.BlockSpec((l, 128), lambda i: (rev(i), 0)), pl.BlockSpec((8, l), lambda i: (0, rev(i))),
                  pl.BlockSpec((1, 8), lambda i: (0, 0)), pl.BlockSpec((8, 1), lambda i: (0, 0)),
                  pl.BlockSpec((1, dm), lambda i: (0, 0)),
                  pl.BlockSpec((None, NH_M, DK_M, 2 * DV_M), lambda i: (rev(i), 0, 0, 0)),
                  pl.BlockSpec((None, 8, 128), lambda i: (rev(i), 0, 0)),
                  pl.BlockSpec((None, NH_M, DK_M, 2 * DV_M), lambda i: (nxt(i), 0, 0, 0)),
                  pl.BlockSpec((None, 8, 128), lambda i: (nxt(i), 0, 0)),
                  pl.BlockSpec((l, dm), lambda i: (rev(i), 0))],
        out_specs=[pl.BlockSpec((l, dm), lambda i: (rev(i), 0)),
                   pl.BlockSpec((l, dm), lambda i: (rev(i), 0)), pl.BlockSpec((l, dm), lambda i: (rev(i), 0)),
                   pl.BlockSpec((l, 128), lambda i: (rev(i), 0)), pl.BlockSpec((8, l), lambda i: (0, rev(i))),
                   pl.BlockSpec((1, dm), lambda i: (0, 0))],
        out_shape=[jax.ShapeDtypeStruct((t, dm), F32),
                   jax.ShapeDtypeStruct((t, dm), BF16), jax.ShapeDtypeStruct((t, dm), BF16),
                   jax.ShapeDtypeStruct((t, 128), F32), jax.ShapeDtypeStruct((8, t), F32),
                   jax.ShapeDtypeStruct((1, dm), F32)],
        scratch_shapes=[pltpu.VMEM((NH_M, DK_M, 2 * DV_M), F32)],
        compiler_params=_cparams(("arbitrary",)),
    )(qk, qk, zbig, zbig, zs, zsr, bc, br, gm, cst, mst, cst, mst, dycat)


def _fox_cumsum(zsr, bf_r):
    t = zsr.shape[1]
    cw = _pick(t, (512, 256))

    def body(z_ref, b_ref, c_ref):
        up = _tri(cw, False).astype(F32)
        carry = jnp.zeros((NH_F, 1), F32)
        for j in range(t // cw):
            cs = _f32dot(_logsig(z_ref[:, j * cw:(j + 1) * cw] + b_ref[...]), up) + carry
            c_ref[:, j * cw:(j + 1) * cw] = cs
            carry = cs[:, cw - 1:cw]

    return pl.pallas_call(
        body, name="fox_cumsum", grid=(1,),
        in_specs=[pl.BlockSpec((NH_F, t), lambda i: (1, 0)), pl.BlockSpec((NH_F, 1), lambda i: (0, 0))],
        out_specs=pl.BlockSpec((NH_F, t), lambda i: (0, 0)), out_shape=jax.ShapeDtypeStruct((NH_F, t), F32),
        compiler_params=_cparams(("arbitrary",)),
    )(zsr, bf_r)


def _fox_gate_bwd(zsr, bf_r, dc):
    t = zsr.shape[1]
    cw = _pick(t, (512, 256))

    def body(z_ref, b_ref, dc_ref, o_ref):
        low = _tri(cw, True).astype(F32)
        carry = jnp.zeros((NH_F, 1), F32)
        for j in reversed(range(t // cw)):
            sl = slice(j * cw, (j + 1) * cw)
            dlf = _f32dot(dc_ref[:, sl], low) + carry
            o_ref[:, sl] = dlf * _sigmoid(-(z_ref[:, sl] + b_ref[...]))
            carry = dlf[:, 0:1]

    return pl.pallas_call(
        body, name="fox_gate_bwd", grid=(1,),
        in_specs=[pl.BlockSpec((NH_F, t), lambda i: (1, 0)), pl.BlockSpec((NH_F, 1), lambda i: (0, 0)),
                  pl.BlockSpec((NH_F, t), lambda i: (0, 0))],
        out_specs=pl.BlockSpec((NH_F, t), lambda i: (0, 0)), out_shape=jax.ShapeDtypeStruct((NH_F, t), F32),
        compiler_params=_cparams(("arbitrary",)),
    )(zsr, bf_r, dc)


def _causal_mask(n):
    return _tri(n, True)


AUG = 64


def _split3(c):
    hi = c.astype(BF16).astype(F32)
    r1 = c - hi
    mid = r1.astype(BF16).astype(F32)
    return hi, mid, r1 - mid


def _fox_prep(zbig, ct):
    t = zbig.shape[0]
    tm = _pick(t, (512, 256))

    def body(q_ref, k_ref, v_ref, c_ref, qo_ref, ko_ref, vo_ref):
        lane = lax.broadcasted_iota(jnp.int32, (tm, AUG), 1)
        qv, kv, vv, cv = q_ref[...], k_ref[...], v_ref[...], c_ref[...]
        one = (lane == 0).astype(BF16)
        for h in range(NH_F):
            hi, mid, lo = _split3(cv[:, h:h + 1])
            aq = jnp.where(lane == 0, hi, jnp.where(lane == 1, mid, jnp.where(lane == 2, lo, jnp.where(lane < 6, 1.0, 0.0))))
            ak = jnp.where(lane < 3, 1.0, jnp.where(lane == 3, -hi, jnp.where(lane == 4, -mid, jnp.where(lane == 5, -lo, 0.0))))
            sl = slice(h * DH_F, (h + 1) * DH_F)
            qo_ref[h] = jnp.concatenate([qv[:, sl] * (DH_F ** -0.5), aq.astype(BF16)], axis=1).astype(BF16)
            ko_ref[h] = jnp.concatenate([kv[:, sl], ak.astype(BF16)], axis=1)
            vo_ref[h] = jnp.concatenate([vv[:, sl], one], axis=1)

    ospec = pl.BlockSpec((NH_F, tm, 128), lambda i: (0, i, 0))
    return pl.pallas_call(
        body, name="fox_prep", grid=(t // tm,),
        in_specs=[pl.BlockSpec((tm, 512), lambda i: (i, 3)), pl.BlockSpec((tm, 512), lambda i: (i, 4)),
                  pl.BlockSpec((tm, 512), lambda i: (i, 5)), pl.BlockSpec((tm, NH_F), lambda i: (i, 0))],
        out_specs=[ospec] * 3, out_shape=[jax.ShapeDtypeStruct((NH_F, t, 128), BF16)] * 3,
        compiler_params=_cparams(("parallel",)),
    )(zbig, zbig, zbig, ct)


def _fox_fwd2(qa, ka, va, gf, plan=None):
    nh, t, _ = qa.shape
    tq = _pick(t, (512, 256))
    nq = t // tq
    group = 4
    host = _Hosted(plan, 4, 3)

    def body(*refs):
        (q_ref, k_ref, v_ref, g_ref), (y_ref, o_ref, lse_ref), _, prefs = host.split(refs)
        i = pl.program_id(0)
        host.run(0, i == 0, prefs)
        host.run(1, i == max(nq - 2, 0), prefs)
        lane = lax.broadcasted_iota(jnp.int32, (tq, 128), 1)
        causal = _causal_mask(tq)
        ys, os_ = [], []
        lse_all = jnp.zeros((tq, 128), F32)
        for h0 in range(0, nh, group):
            heads = range(h0, h0 + group)
            qvs = [q_ref[h] for h in heads]

            def blk(j, carry, masked, heads=heads, qvs=qvs):
                k0 = pl.multiple_of(j * tq, tq)
                out = []
                for (m, acc), h, qv in zip(carry, heads, qvs):
                    s = lax.dot_general(qv, k_ref[h, pl.ds(k0, tq), :], (((1,), (1,)), ((), ())), preferred_element_type=F32)
                    if masked:
                        s = jnp.where(causal, s, -jnp.inf)
                    m_new = jnp.maximum(m, jnp.max(s, axis=1, keepdims=True))
                    p = jnp.exp(s - m_new).astype(BF16)
                    pv = lax.dot_general(p, v_ref[h, pl.ds(k0, tq), :], (((1,), (0,)), ((), ())), preferred_element_type=F32)
                    out.append((m_new, jnp.exp(m - m_new) * acc + pv))
                return tuple(out)

            init = tuple((jnp.full((tq, 1), -jnp.inf, F32), jnp.zeros((tq, 128), F32)) for _ in heads)
            carry = lax.fori_loop(0, i, lambda j, c: blk(j, c, False), init)
            for (m, acc), h in zip(blk(i, carry, True), heads):
                l = acc[:, DH_F:DH_F + 1]
                o = acc[:, :DH_F] / l
                os_.append(o)
                gh = g_ref[:, h * DH_F:(h + 1) * DH_F]
                ys.append(o * lax.rsqrt(jnp.mean(o * o, axis=-1, keepdims=True) + EPS) * gh)
                lse_all = lse_all + jnp.where(lane == h, m + jnp.log(l), 0.0)
        y_ref[...] = jnp.concatenate(ys, axis=1).astype(y_ref.dtype)
        o_ref[...] = jnp.concatenate(os_, axis=1)
        lse_ref[...] = lse_all
        host.run(2, i == nq - 1, prefs)

    full = pl.BlockSpec((nh, t, 128), lambda i: (0, 0, 0))
    hc = host.call_args()
    res = pl.pallas_call(
        body, name="fox_fwd", grid=(nq,),
        in_specs=[pl.BlockSpec((nh, tq, 128), lambda i: (0, i, 0)), full, full, pl.BlockSpec((1, nh * DH_F), lambda i: (0, 0))]
        + hc["in_specs"],
        out_specs=[pl.BlockSpec((tq, nh * DH_F), lambda i: (i, 0)), pl.BlockSpec((tq, nh * DH_F), lambda i: (i, 0)),
                   pl.BlockSpec((tq, 128), lambda i: (i, 0))] + hc["out_specs"],
        out_shape=[jax.ShapeDtypeStruct((t, nh * DH_F), BF16), jax.ShapeDtypeStruct((t, nh * DH_F), F32),
                   jax.ShapeDtypeStruct((t, 128), F32)] + hc["out_shape"],
        scratch_shapes=hc["scratch"], input_output_aliases=hc["aliases"], compiler_params=_cparams(("arbitrary",)),
    )(qa, ka, va, gf, *hc["args"])
    return res[:3], res[3:]


def _fox_bwd_prep(dycat, o, gf):
    t = o.shape[0]
    tm = _pick(t, (512, 256))

    def body(dy_ref, o_ref, g_ref, do_ref, dl_ref, dg_ref):
        lane = lax.broadcasted_iota(jnp.int32, (tm, 128), 1)
        dyv, ov, gv = dy_ref[...], o_ref[...], g_ref[...]
        dgs = []
        dl = jnp.zeros((tm, 128), F32)
        pad = jnp.zeros((tm, AUG), BF16)
        for h in range(NH_F):
            sl = slice(h * DH_F, (h + 1) * DH_F)
            dx, dg = _rms_bwd_val(dyv[:, sl], ov[:, sl], gv[:, sl])
            dgs.append(dg)
            do_ref[h] = jnp.concatenate([dx.astype(BF16), pad], axis=1)
            dl = dl + jnp.where(lane == h, jnp.sum(dx * ov[:, sl], axis=-1, keepdims=True), 0.0)
        dl_ref[...] = dl

        @pl.when(pl.program_id(0) == 0)
        def _():
            dg_ref[...] = jnp.zeros_like(dg_ref)
        dg_ref[...] += jnp.concatenate(dgs, axis=1)

    return pl.pallas_call(
        body, name="fox_bwd_prep", grid=(t // tm,),
        in_specs=[pl.BlockSpec((tm, 512), lambda i: (i, 1)), pl.BlockSpec((tm, 512), lambda i: (i, 0)),
                  pl.BlockSpec((1, 512), lambda i: (0, 0))],
        out_specs=[pl.BlockSpec((NH_F, tm, 128), lambda i: (0, i, 0)), pl.BlockSpec((tm, 128), lambda i: (i, 0)),
                   pl.BlockSpec((1, 512), lambda i: (0, 0))],
        out_shape=[jax.ShapeDtypeStruct((NH_F, t, 128), BF16), jax.ShapeDtypeStruct((t, 128), F32),
                   jax.ShapeDtypeStruct((1, 512), F32)],
        compiler_params=_cparams(("arbitrary",)),
    )(dycat, o, gf)


def _fox_bwd2(qa, ka, va, doa, lse, delta, plan=None):
    nh, t, _ = qa.shape
    tq = _pick(t, (512, 256))
    nq = t // tq

    group = 2

    def tdot(a, b, cb):
        return lax.dot_general(a, b, (((0,), (cb,)), ((), ())), preferred_element_type=F32)

    host = _Hosted(plan, 6, 3)
    ng = nh // group

    def body(*refs):
        (q_ref, k_ref, v_ref, do_ref, lse_ref, dl_ref), (dq_ref, dk_ref, dv_ref), _, prefs = host.split(refs)
        hp, j = pl.program_id(0), pl.program_id(1)
        host.run(0, (hp == 0) & (j == 0), prefs)
        host.run(1, (hp == 0) & (j == 0), prefs)

        @pl.when(j == 0)
        def _():
            dq_ref[...] = jnp.zeros_like(dq_ref)

        lane = lax.broadcasted_iota(jnp.int32, (tq, 128), 1)
        causal = _causal_mask(tq)

        def blk(i, carry, masked):
            rows = pl.ds(pl.multiple_of(i * tq, tq), tq)
            lse_t, dl_t = lse_ref[rows, :], dl_ref[rows, :]
            out = []
            for g, (dk, dv) in enumerate(carry):
                h = hp * group + g
                kb, vb = k_ref[g], v_ref[g]
                qb, dob = q_ref[g, rows, :], do_ref[g, rows, :]
                lse_h = jnp.sum(jnp.where(lane == h, lse_t, 0.0), axis=1, keepdims=True)
                dl_h = jnp.sum(jnp.where(lane == h, dl_t, 0.0), axis=1, keepdims=True)
                s = lax.dot_general(qb, kb, (((1,), (1,)), ((), ())), preferred_element_type=F32)
                if masked:
                    s = jnp.where(causal, s, -jnp.inf)
                p = jnp.exp(s - lse_h)
                dp = lax.dot_general(dob, vb, (((1,), (1,)), ((), ())), preferred_element_type=F32)
                ds = (p * (dp - dl_h)).astype(BF16)
                dv = dv + tdot(dob, p.astype(BF16), 0)
                dk = dk + tdot(qb, ds, 0)
                dq_ref[g, :, rows] += tdot(kb, ds, 1)
                out.append((dk, dv))
            return tuple(out)

        init = tuple((jnp.zeros((128, tq), F32), jnp.zeros((128, tq), F32)) for _ in range(group))
        carry = blk(j, init, True)
        carry = lax.fori_loop(j + 1, nq, lambda i, c: blk(i, c, False), carry)
        for g, (dk, dv) in enumerate(carry):
            dk_ref[g] = dk
            dv_ref[g] = dv
        host.run(2, (hp == ng - 1) & (j == nq - 1), prefs)

    full = pl.BlockSpec((group, t, 128), lambda h, j: (h, 0, 0))
    tile = pl.BlockSpec((group, tq, 128), lambda h, j: (h, j, 0))
    cols = pl.BlockSpec((t, 128), lambda h, j: (0, 0))
    full_t = pl.BlockSpec((group, 128, t), lambda h, j: (h, 0, 0))
    tile_t = pl.BlockSpec((group, 128, tq), lambda h, j: (h, 0, j))
    hc = host.call_args()
    res = pl.pallas_call(
        body, name="fox_bwd", grid=(ng, nq), in_specs=[full, tile, tile, full, cols, cols] + hc["in_specs"],
        out_specs=[full_t, tile_t, tile_t] + hc["out_specs"],
        out_shape=[jax.ShapeDtypeStruct((nh, 128, t), F32)] * 3 + hc["out_shape"], scratch_shapes=hc["scratch"],
        input_output_aliases=hc["aliases"], compiler_params=_cparams(("arbitrary", "arbitrary")),
    )(qa, ka, va, doa, lse, delta, *hc["args"])
    return res[:3], res[3:]


def _fox_bwd_post(dqa, dka, dva):
    nh, _, t = dqa.shape
    tm = _pick(t, (512, 256))

    def body(dq_ref, dk_ref, dv_ref, oq_ref, ok_ref, ov_ref, dc_ref):
        qs, ks, vs, dcs = [], [], [], []
        for h in range(nh):
            dq, dk = dq_ref[h], dk_ref[h]
            qs.append(dq.T[:, :DH_F] * (DH_F ** -0.5))
            ks.append(dk.T[:, :DH_F])
            vs.append(dv_ref[h].T[:, :DH_F])
            dcs.append(dq[DH_F:DH_F + 1, :] - dk[DH_F + 3:DH_F + 4, :])
        oq_ref[...] = jnp.concatenate(qs, axis=1).astype(BF16)
        ok_ref[...] = jnp.concatenate(ks, axis=1).astype(BF16)
        ov_ref[...] = jnp.concatenate(vs, axis=1).astype(BF16)
        dc_ref[...] = jnp.concatenate(dcs, axis=0)

    ispec = pl.BlockSpec((nh, 128, tm), lambda i: (0, 0, i))
    ospec = pl.BlockSpec((tm, nh * DH_F), lambda i: (i, 0))
    return pl.pallas_call(
        body, name="fox_bwd_post", grid=(t // tm,), in_specs=[ispec] * 3,
        out_specs=[ospec] * 3 + [pl.BlockSpec((nh, tm), lambda i: (0, i))],
        out_shape=[jax.ShapeDtypeStruct((t, nh * DH_F), BF16)] * 3 + [jax.ShapeDtypeStruct((nh, t), F32)],
        compiler_params=_cparams(("parallel",)),
    )(dqa, dka, dva)


IN_OFF = (0, 512, 1024, 1544, 2056, 2568)
IN_GATES = (1536, 3080)


FFN1 = ("ffn1_w_gate", "ffn1_w_up", "ffn1_w_down")
REST = ("w_in", "w_out", "ffn2_w_gate", "ffn2_w_up", "ffn2_w_down", "w_ple_gate", "w_ple_proj")
SPLIT = {n: 1 if n == "w_in" else 0 for n in FFN1 + REST}
SAME_SHAPE = (FFN1, ("ffn2_w_gate", "ffn2_w_up", "ffn2_w_down"), ("w_out", "w_ple_gate"), ("w_in",), ("w_ple_proj",))


def _grouped(names):
    return [tuple(n for n in grp if n in names) for grp in SAME_SHAPE if any(n in names for n in grp)]


def _rs_partials(names, gw, c_idx, twins, run_swap=None):
    wire = [twins[n] if n in twins else _cast_other_half("rs_cast_" + n, gw[n], c_idx, SPLIT[n]) for n in names]
    plan = _swap_plan(wire, [SPLIT[n] if n in twins else None for n in names])
    swapped = dict(zip(names, run_swap(plan) if run_swap else _run_plan("rs_swap_" + names[0], plan)))
    out = {}
    for grp in _grouped(names):
        res = _add_my_half("rs_add_" + grp[0], [gw[n] for n in grp], [swapped[n] for n in grp], c_idx, SPLIT[grp[0]])
        out.update(zip(grp, res))
    return [out[n] for n in names]


def _local_step(x, p, tgt, sp, wg1, wu1, wd1, rest_slots, c_idx, place):
    t, d = x.shape
    slot = dict(zip(REST + ("conv_qk",), rest_slots))
    (h1, xn1, g1, u1), (w_in, conv_w) = _ffn_fwd(
        "ffn1", x, sp["ffn1_norm"], wg1, wu1, wd1, plan=_gather_plan([slot["w_in"], slot["conv_qk"]], [SPLIT["w_in"], None]))
    w_in, conv_w = w_in.reshape(-1, d), _from_chip_blocks(conv_w)
    w_big = jnp.concatenate([w_in[o:o + 512] for o in IN_OFF], axis=0)
    w_small = jnp.concatenate([w_in[IN_GATES[0]:IN_GATES[0] + 8], w_in[IN_GATES[1]:IN_GATES[1] + 8],
                               jnp.zeros((112, d), w_in.dtype)], axis=0)
    u, zbig = _norm_mm("in_big", h1, sp["mix_norm"], w_big, True, BF16)
    zs = _mm_nt("in_small", u, w_small, tm=1024, tk=1024)
    zsr = zs.T
    qk_act = _conv_fwd(zbig, conv_w)
    bm_c, bf_c = sp["b_mlstm_gates"], sp["b_fox_f"]
    y_m, cst, mst = _mlstm_fwd(qk_act, zbig, zs, zsr, bm_c, bm_c.T, sp["mlstm_out_norm"])
    c = _fox_cumsum(zsr, bf_c.T)
    qa, ka, va = _fox_prep(zbig, c.T)
    (y_ft, o_f, lse), late = _fox_fwd2(qa, ka, va, sp["fox_out_norm"],
                                       plan=_gather_plan([slot[n] for n in REST[1:]], [SPLIT[n] for n in REST[1:]]))
    full = dict(zip(REST[1:], late))
    w_out, w_pg = (full[n].reshape(-1, d) for n in ("w_out", "w_ple_gate"))
    wg2, wu2, wd2 = full["ffn2_w_gate"], full["ffn2_w_up"], full["ffn2_w_down"]
    w_pp = _from_chip_blocks(full["w_ple_proj"])
    tm = _pick(t, (1024, 512, 256))
    h2 = _mm("out_proj", [(y_m, (tm, 512), lambda i, j, k: (i, 0), w_out, (512, d), lambda i, j, k: (0, 0)),
                          (y_ft, (tm, 512), lambda i, j, k: (i, 0), w_out, (512, d), lambda i, j, k: (1, 0))],
             (t, d), (tm, d), lambda i, j, k: (i, 0), (t // tm, 1, 1), 2, res=h1)
    (h3, xn2, g2, u2), _ = _ffn_fwd("ffn2", h2, sp["ffn2_norm"], wg2, wu2, wd2)
    hn3, gate_pre = _norm_mm("ple_gate", h3, sp["ple_gate_norm"], w_pg, False, F32)
    pp = _mm_nn("ple_proj", p, w_pp, tm=1024)

    def head_fn(h3_t, gp_t, pp_t, tgt_t, g_pp, g_fin):
        gate = _sigmoid(gp_t)
        ppn = _rms_fwd_val(pp_t, g_pp)
        h4 = h3_t + gate * ppn
        err = _rms_fwd_val(h4, g_fin) - tgt_t
        loss = 0.5 * jnp.sum(jnp.mean(err * err, axis=-1, keepdims=True), axis=0, keepdims=True)
        dh4, dg_fin = _rms_bwd_val(err * (1.0 / d), h4, g_fin)
        dpp, dg_pp = _rms_bwd_val(dh4 * gate, pp_t, g_pp)
        dgp = dh4 * ppn * gate * (1.0 - gate)
        return dh4, dgp, dpp, jnp.broadcast_to(loss, (1, 128)), dg_fin, dg_pp

    dh4, dgp, dpp, loss_part, dg_fin, dg_pp = _rowwise(
        "loss_head", head_fn, [h3, gate_pre, pp, tgt], [sp["ple_proj_norm"], sp["final_norm"]],
        [(d, F32), (d, BF16), (d, BF16)], [((1, 128), F32), ((1, d), F32), ((1, d), F32)])
    gw, gs = {}, {"final_norm": dg_fin, "ple_proj_norm": dg_pp}
    gw["w_ple_gate"] = _mm_tn("d_w_pg", hn3, dgp, tm=1024, tn=1024)
    gw["w_ple_proj"] = _mm_tn("d_w_pp", p, dpp, tn=1024)
    dhn3 = _mm_nt("d_hn3", dgp, w_pg, tm=1024, tn=1024, tk=1024)

    def res_norm_bwd(dn_t, h_t, dres_t, g):
        dx, dg = _rms_bwd_val(dn_t, h_t, g)
        return dres_t + dx, dg

    dh3, gs["ple_gate_norm"] = _rowwise("ple_norm_bwd", res_norm_bwd, [dhn3, h3, dh4], [sp["ple_gate_norm"]],
                                        [(d, F32)], [((1, d), F32)])
    (dh2, gs["ffn2_norm"], gw["ffn2_w_gate"], gw["ffn2_w_up"], gw["ffn2_w_down"]), _, twins2 = _ffn_bwd(
        "ffn2", dh3, h2, sp["ffn2_norm"], xn2, g2, u2, wg2, wu2, wd2)
    ffn2_names = ("ffn2_w_gate", "ffn2_w_up", "ffn2_w_down")
    early = []

    def swap_in_d_ycat(plan):
        dyc, swapped = _mm_nt("d_ycat", dh2, w_out, tm=1024, tn=1024, tk=1024, plan=plan)
        early.append(dyc)
        return swapped

    part_ffn2 = dict(zip(ffn2_names, _rs_partials(ffn2_names, gw, c_idx, dict(zip(ffn2_names, twins2)), swap_in_d_ycat)))
    dycat = early[0]
    gw["w_out"] = jnp.concatenate([_mm_tn("d_w_out_m", y_m, dh2, tn=1024, tk=2048),
                                   _mm_tn("d_w_out_f", y_ft, dh2, tn=1024, tk=2048)], axis=0)
    doa, delta, gs["fox_out_norm"] = _fox_bwd_prep(dycat, o_f, sp["fox_out_norm"])
    dqkv_t, landed_ffn2 = _fox_bwd2(qa, ka, va, doa, lse, delta, plan=_scatter_plan([part_ffn2[n][1] for n in ffn2_names]))
    dq_f, dk_f, dv_f, dct = _fox_bwd_post(*dqkv_t)
    dfp = _fox_gate_bwd(zsr, bf_c.T, dct)
    dact, dv_m, do_m, dzs_m, dzr_m, gs["mlstm_out_norm"] = _mlstm_bwd(
        qk_act, zbig, zs, zsr, bm_c, bm_c.T, sp["mlstm_out_norm"], cst, mst, dycat)
    dqk, gw["conv_qk"] = _conv_bwd(zbig, dact, conv_w)
    dz_big = jnp.concatenate([dqk, dv_m, do_m, dq_f, dk_f, dv_f], axis=1)
    dzs = dzs_m + jnp.pad(jnp.concatenate([dzr_m, dfp], axis=0).T, ((0, 0), (0, 112)))
    dw_big = _mm_tn("d_w_big", dz_big, u, tn=1024)
    dw_small = _mm_tn("d_w_small", dzs, u, tn=1024)
    gw["w_in"] = jnp.concatenate([dw_big[0:1536], dw_small[0:8], dw_big[1536:3072], dw_small[8:16]], axis=0)
    du_a = _mm_nn("d_u_big", dz_big, w_big, tm=1024, tn=1024, tk=1024)
    du_b = _mm_nn("d_u_small", dzs, w_small, tm=1024, tn=1024)

    def mix_norm_bwd(da_t, db_t, h_t, dres_t, dzs_t, g):
        dx, dg = _rms_bwd_val(da_t + db_t, h_t, g)
        return dres_t + dx, dg, _colsum(dzs_t)

    conv_grad = gw.pop("conv_qk")
    gw["w_ple_proj"] = _chip_blocks(gw["w_ple_proj"])
    for n in ("w_in", "w_out", "w_ple_gate"):
        gw[n] = gw[n].reshape(4, -1, gw[n].shape[-1])
    mix = []

    def swap_in_mix_norm_bwd(plan):
        res, swapped = _rowwise("mix_norm_bwd", mix_norm_bwd, [du_a, du_b, h1, dh2, dzs], [sp["mix_norm"]],
                                [(d, F32)], [((1, d), F32), ((1, 128), F32)], plan=plan)
        mix.extend(res)
        return swapped

    light = ("w_in", "w_out", "w_ple_gate", "w_ple_proj")
    part_light = dict(zip(light, _rs_partials(light, gw, c_idx, {}, swap_in_mix_norm_bwd)))
    dh1, gs["mix_norm"], dbias = mix
    gs["b_mlstm_gates"], gs["b_fox_f"] = dbias[:, 0:8], dbias[:, 8:16]
    part_ffn1 = []

    def own_plan(dws, dw_twins):
        part_ffn1.extend(_rs_partials(FFN1, dict(zip(FFN1, dws)), c_idx, dict(zip(FFN1, dw_twins))))
        return _scatter_plan([pb for _, pb in part_ffn1])

    (grad_x, gs["ffn1_norm"], _, _, _), (l_light, _, _, _, landed_ffn1) = _ffn_bwd_late_dx(
        "ffn1", dh1, x, sp["ffn1_norm"], xn1, g1, u1, wg1, wu1, wd1,
        _scatter_plan([part_light[n][1] for n in light]), [None] * 3, own_plan)
    names = REST + FFN1
    parts = {**part_light, **part_ffn2, **dict(zip(FFN1, part_ffn1))}
    landed = {**dict(zip(light, l_light)), **dict(zip(ffn2_names, landed_ffn2)), **dict(zip(FFN1, landed_ffn1))}
    mine = {}
    for grp in _grouped(names):
        res = _sum4("rs_sum_" + grp[0], [landed[n] for n in grp], [parts[n][0] for n in grp], place, SPLIT[grp[0]])
        mine.update(zip(grp, res))
    grads = dict(zip(names, _join_halves("rs_join", [mine[n] for n in names], [SPLIT[n] for n in names])))
    return loss_part, grad_x, grads, gs, conv_grad


ANY = pl.BlockSpec(memory_space=pl.ANY)
MESH = pl.DeviceIdType.MESH


def _place():
    x, y, c = lax.axis_index("x"), lax.axis_index("y"), lax.axis_index("c")
    chips = [(1 - x, y), (x, 1 - y), (1 - x, 1 - y)]
    return x, y, c, 2 * x + y, (x, y, 1 - c), chips


def _rcopy(src, dst, ssem, rsem, dev):
    return pltpu.make_async_remote_copy(src_ref=src, dst_ref=dst, send_sem=ssem, recv_sem=rsem, device_id=dev,
                                        device_id_type=MESH)


def _half(ref, lead, axis, idx, half):
    return ref.at[(slice(None),) * (lead + axis) + (pl.ds(idx * half, half),)]


def _to_slot(name, arrs, me_idx, dtype):
    n = len(arrs)
    r, cdim = arrs[0].shape
    tr = _pick(r, (352, 256, 176, 128, 64))

    def body(me_ref, *refs):
        for k in range(n):
            refs[n + k][...] = refs[k][...].astype(dtype)

    return pl.pallas_call(
        body, name=name,
        grid_spec=pltpu.PrefetchScalarGridSpec(
            num_scalar_prefetch=1, grid=(r // tr,), in_specs=[pl.BlockSpec((tr, cdim), lambda i, me_ref: (i, 0))] * n,
            out_specs=[pl.BlockSpec((None, tr, cdim), lambda i, me_ref: (me_ref[0], i, 0))] * n),
        out_shape=[jax.ShapeDtypeStruct((4, r, cdim), dtype)] * n, compiler_params=_cparams(("parallel",)),
    )(me_idx, *arrs)


def _gather4(name, bufs, split):
    return _run_plan(name, _gather_plan(bufs, split))


def _gather_plan(bufs, split):
    n = len(bufs)
    shapes = [b.shape[1:] for b in bufs]

    def ctx(outs):
        x, y, c, me, sib, chips = _place()

        def part(ref, a, which):
            if split[a] is None:
                return ref
            return _half(ref, 0, split[a], which, shapes[a][split[a]] // 2)

        return c, me, sib, chips, part

    def ici(outs, sems, a, j, chip, c, me, part):
        mine = part(outs[a].at[me], a, c)
        return _rcopy(mine, mine, sems[0].at[3 * a + j], sems[1].at[3 * a + j], (*chip, c))

    def fwd(outs, sems, a, j, chip, c, sib, part, which):
        blk = part(outs[a].at[2 * chip[0] + chip[1]], a, which)
        return _rcopy(blk, blk, sems[2].at[3 * a + j], sems[3].at[3 * a + j], sib)

    def start(ins, outs, sems):
        c, me, sib, chips, part = ctx(outs)
        for a in range(n):
            for j, chip in enumerate(chips):
                ici(outs, sems, a, j, chip, c, me, part).start()

    def mid(ins, outs, sems):
        c, me, sib, chips, part = ctx(outs)
        for j, chip in enumerate(chips):
            for a in range(n):
                blk = part(outs[a].at[2 * chip[0] + chip[1]], a, c)
                _rcopy(blk, blk, sems[0].at[3 * a + j], sems[1].at[3 * a + j], sib).wait_recv()
                if split[a] is not None:
                    fwd(outs, sems, a, j, chip, c, sib, part, c).start()

    def end(ins, outs, sems):
        c, me, sib, chips, part = ctx(outs)
        for j, chip in enumerate(chips):
            for a in range(n):
                if split[a] is not None:
                    fwd(outs, sems, a, j, chip, c, sib, part, 1 - c).wait_recv()
        for a in range(n):
            for j, chip in enumerate(chips):
                ici(outs, sems, a, j, chip, c, me, part).wait_send()
                if split[a] is not None:
                    fwd(outs, sems, a, j, chip, c, sib, part, c).wait_send()

    return dict(ins=list(bufs), outs=[jax.ShapeDtypeStruct(b.shape, b.dtype) for b in bufs], alias=True,
                sems=[pltpu.SemaphoreType.DMA((3 * n,))] * 4, phases=(start, mid, end))


def _run_plan(name, plan):
    ni, no = len(plan["ins"]), len(plan["outs"])

    def body(*refs):
        ins, outs, sems = refs[:ni], refs[ni:ni + no], refs[ni + no:]
        for phase in plan["phases"]:
            phase(ins, outs, sems)

    return pl.pallas_call(
        body, name=name, in_specs=[ANY] * ni, out_specs=[ANY] * no, out_shape=plan["outs"],
        input_output_aliases={a: a for a in range(ni)} if plan["alias"] else {}, scratch_shapes=plan["sems"],
    )(*plan["ins"])


class _Hosted:
    def __init__(self, plan, n_in, n_out):
        self.plan, self.n_in, self.n_out = plan, n_in, n_out
        self.ni, self.no, self.ns = (len(plan["ins"]) if plan else 0, len(plan["outs"]) if plan else 0,
                                     len(plan["sems"]) if plan else 0)

    def split(self, refs):
        a, b = self.n_in, self.n_in + self.ni
        c, d = b + self.n_out, b + self.n_out + self.no
        e = len(refs) - self.ns
        return refs[:a], refs[b:c], refs[d:e], (refs[a:b], refs[c:d], refs[e:])

    def run(self, k, cond, prefs):
        if self.plan is not None:
            @pl.when(cond)
            def _():
                self.plan["phases"][k](*prefs)

    def call_args(self):
        p = self.plan
        if p is None:
            return dict(in_specs=[], out_specs=[], out_shape=[], scratch=[], aliases={}, args=[])
        al = {self.n_in + a: self.n_out + a for a in range(self.ni)} if p["alias"] else {}
        return dict(in_specs=[ANY] * self.ni, out_specs=[ANY] * self.no, out_shape=list(p["outs"]), scratch=list(p["sems"]),
                    aliases=al, args=list(p["ins"]))


def _swap(name, arrs, halve):
    return _run_plan(name, _swap_plan(arrs, halve))


def _swap_plan(arrs, halve):
    n = len(arrs)

    def half_shape(a, ax):
        return a.shape if ax is None else (a.shape[0],) + tuple(d // 2 if i == ax else d for i, d in enumerate(a.shape[1:]))

    def copies(ins, outs, sems):
        x, y, c, me, sib, chips = _place()
        cps = []
        for a in range(n):
            src = ins[a] if halve[a] is None else _half(ins[a], 1, halve[a], 1 - c, arrs[a].shape[1 + halve[a]] // 2)
            cps.append(_rcopy(src, outs[a], sems[0].at[a], sems[1].at[a], sib))
        return cps

    def start(ins, outs, sems):
        for cp in copies(ins, outs, sems):
            cp.start()

    def mid(ins, outs, sems):
        pass

    def end(ins, outs, sems):
        for cp in copies(ins, outs, sems):
            cp.wait()

    return dict(ins=list(arrs), outs=[jax.ShapeDtypeStruct(half_shape(a, ax), a.dtype) for a, ax in zip(arrs, halve)],
                alias=False, sems=[pltpu.SemaphoreType.DMA((n,))] * 2, phases=(start, mid, end))


def _scatter4(name, arrs):
    return _run_plan(name, _scatter_plan(arrs))


def _scatter_plan(arrs):
    n = len(arrs)

    def send(ins, outs, sems, a, j, chip, c, me):
        return _rcopy(ins[a].at[2 * chip[0] + chip[1]], outs[a].at[me], sems[0].at[3 * a + j], sems[1].at[3 * a + j], (*chip, c))

    def start(ins, outs, sems):
        x, y, c, me, sib, chips = _place()
        for a in range(n):
            for j, chip in enumerate(chips):
                send(ins, outs, sems, a, j, chip, c, me).start()

    def mid(ins, outs, sems):
        pass

    def end(ins, outs, sems):
        x, y, c, me, sib, chips = _place()
        for a in range(n):
            for j, chip in enumerate(chips):
                blk = outs[a].at[2 * chip[0] + chip[1]]
                _rcopy(blk, blk, sems[0].at[3 * a + j], sems[1].at[3 * a + j], sib).wait_recv()
        for a in range(n):
            for j, chip in enumerate(chips):
                send(ins, outs, sems, a, j, chip, c, me).wait_send()

    return dict(ins=list(arrs), outs=[jax.ShapeDtypeStruct(a.shape, a.dtype) for a in arrs], alias=False,
                sems=[pltpu.SemaphoreType.DMA((3 * n,))] * 2, phases=(start, mid, end))


def _join_halves(name, arrs, split):
    n = len(arrs)

    def body(*refs):
        outs = refs[n:2 * n]
        ssem, rsem = refs[2 * n:]
        x, y, c, me, sib, chips = _place()
        cps = []
        for a in range(n):
            mine = _half(outs[a], 0, split[a], c, arrs[a].shape[split[a]] // 2)
            cp = _rcopy(mine, mine, ssem.at[a], rsem.at[a], sib)
            cp.start()
            cps.append(cp)
        for a in range(n):
            blk = _half(outs[a], 0, split[a], 1 - c, arrs[a].shape[split[a]] // 2)
            _rcopy(blk, blk, ssem.at[a], rsem.at[a], sib).wait_recv()
        for cp in cps:
            cp.wait_send()

    return pl.pallas_call(
        body, name=name, in_specs=[ANY] * n, out_specs=[ANY] * n,
        out_shape=[jax.ShapeDtypeStruct(a.shape, a.dtype) for a in arrs],
        input_output_aliases={a: a for a in range(n)}, scratch_shapes=[pltpu.SemaphoreType.DMA((n,))] * 2,
    )(*arrs)


def _allreduce_small(s):
    r, cdim = s.shape

    def body(s_ref, o_ref, buf, ssem, rsem):
        x, y, c, me, sib, chips = _place()
        me8 = 4 * x + 2 * y + c
        buf[me8] = s_ref[...]
        flips = [(fx, fy, fc) for fx in (0, 1) for fy in (0, 1) for fc in (0, 1)][1:]
        cps = []
        for k, (fx, fy, fc) in enumerate(flips):
            peer = (x ^ fx if fx else x, y ^ fy if fy else y, c ^ fc if fc else c)
            cp = _rcopy(s_ref, buf.at[me8], ssem.at[k], rsem.at[k], peer)
            cp.start()
            cps.append(cp)
        for k, (fx, fy, fc) in enumerate(flips):
            src = 4 * (x ^ fx if fx else x) + 2 * (y ^ fy if fy else y) + (c ^ fc if fc else c)
            _rcopy(s_ref, buf.at[src], ssem.at[k], rsem.at[k], sib).wait_recv()
        for cp in cps:
            cp.wait_send()
        acc = buf[0]
        for k in range(1, 8):
            acc = acc + buf[k]
        o_ref[...] = acc

    vm = pl.BlockSpec(memory_space=pltpu.VMEM)
    return pl.pallas_call(
        body, name="allreduce_small", in_specs=[vm], out_specs=vm, out_shape=jax.ShapeDtypeStruct((r, cdim), F32),
        scratch_shapes=[pltpu.VMEM((8, r, cdim), F32), pltpu.SemaphoreType.DMA((7,)), pltpu.SemaphoreType.DMA((7,))],
    )(s)


def _add_my_half(name, gs, recvs, c_idx, axis):
    n = len(gs)
    nb, hr, hc = recvs[0].shape
    tr = _pick(hr, (256, 176, 128, 64))
    if axis == 0:
        g4s = [g.reshape(nb, 2, hr, hc) for g in gs]
        gspec = pl.BlockSpec((None, None, tr, hc), lambda b, i, c_ref: (b, c_ref[0], i, 0))
    else:
        g4s = list(gs)
        gspec = pl.BlockSpec((None, tr, hc), lambda b, i, c_ref: (b, i, c_ref[0]))

    def body(c_ref, *refs):
        for k in range(n):
            s = refs[k][...] + refs[n + k][...].astype(F32)
            refs[2 * n + 2 * k][...] = s
            refs[2 * n + 2 * k + 1][...] = s.astype(BF16)

    ospec = pl.BlockSpec((None, tr, hc), lambda b, i, c_ref: (b, i, 0))
    res = pl.pallas_call(
        body, name=name,
        grid_spec=pltpu.PrefetchScalarGridSpec(
            num_scalar_prefetch=1, grid=(nb, hr // tr), in_specs=[gspec] * n + [ospec] * n, out_specs=[ospec] * (2 * n)),
        out_shape=[jax.ShapeDtypeStruct((nb, hr, hc), F32), jax.ShapeDtypeStruct((nb, hr, hc), BF16)] * n,
        compiler_params=_cparams(("parallel", "parallel")),
    )(c_idx, *g4s, *recvs)
    return [(res[2 * k], res[2 * k + 1]) for k in range(n)]


def _sum4(name, landeds, owns, place, axis):
    n = len(landeds)
    nb, h, cdim = landeds[0].shape
    tr = _pick(h, (256, 176, 128, 64))
    nt = h // tr

    def body(p_ref, *refs):
        for k in range(n):
            a1, a2, a3, own = refs[4 * k:4 * k + 4]
            refs[4 * n + k][...] = ((own[...] + a1[...].astype(F32)) + a2[...].astype(F32)) + a3[...].astype(F32)

    def nxt(k):
        return pl.BlockSpec((None, tr, cdim), lambda i, p_ref: ((p_ref[0] + k) % nb, i, 0))

    if axis == 0:
        ospec = pl.BlockSpec((tr, cdim), lambda i, p_ref: (p_ref[1] * nt + i, 0))
        oshape = (2 * h, cdim)
    else:
        ospec = pl.BlockSpec((tr, cdim), lambda i, p_ref: (i, p_ref[1]))
        oshape = (h, 2 * cdim)
    args = []
    for landed, own in zip(landeds, owns):
        args += [landed, landed, landed, own]
    return pl.pallas_call(
        body, name=name,
        grid_spec=pltpu.PrefetchScalarGridSpec(
            num_scalar_prefetch=1, grid=(nt,), in_specs=[nxt(1), nxt(2), nxt(3), nxt(0)] * n, out_specs=[ospec] * n),
        out_shape=[jax.ShapeDtypeStruct(oshape, F32)] * n, compiler_params=_cparams(("parallel",)),
    )(place, *args)


def _cast_other_half(name, g, c_idx, axis):
    nb, r, cdim = g.shape
    hr, hc = (r // 2, cdim) if axis == 0 else (r, cdim // 2)
    tr = _pick(hr, (256, 176, 128, 64))
    if axis == 0:
        g4 = g.reshape(nb, 2, hr, hc)
        gspec = pl.BlockSpec((None, None, tr, hc), lambda b, i, c_ref: (b, 1 - c_ref[0], i, 0))
    else:
        g4 = g
        gspec = pl.BlockSpec((None, tr, hc), lambda b, i, c_ref: (b, i, 1 - c_ref[0]))

    def body(c_ref, g_ref, o_ref):
        o_ref[...] = g_ref[...].astype(BF16)

    return pl.pallas_call(
        body, name=name,
        grid_spec=pltpu.PrefetchScalarGridSpec(
            num_scalar_prefetch=1, grid=(nb, hr // tr), in_specs=[gspec],
            out_specs=pl.BlockSpec((None, tr, hc), lambda b, i, c_ref: (b, i, 0))),
        out_shape=jax.ShapeDtypeStruct((nb, hr, hc), BF16), compiler_params=_cparams(("parallel", "parallel")),
    )(c_idx, g4)


def _adamw(name, ws, gs, ms, vs):
    n = len(ws)
    c1 = 1.0 - ADAM_B1 ** ADAM_STEP
    c2 = 1.0 - ADAM_B2 ** ADAM_STEP

    def fn(*tiles):
        out = []
        for k in range(n):
            w_t, g_t, m_t, v_t = tiles[4 * k:4 * k + 4]
            m_n = ADAM_B1 * m_t + (1.0 - ADAM_B1) * g_t
            v_n = ADAM_B2 * v_t + (1.0 - ADAM_B2) * (g_t * g_t)
            out += [-ADAM_LR * ((m_n / c1) / (jnp.sqrt(v_n / c2) + ADAM_EPS) + ADAM_WD * w_t), m_n, v_n]
        return out

    rows, cdim = ws[0].shape
    tiled = [a for quad in zip(ws, gs, ms, vs) for a in quad]
    pref = (512, 352, 256, 128, 64, 8) if n == 1 else (176, 128, 64, 8)
    res = _rowwise(name, fn, tiled, [], [(cdim, F32)] * (3 * n), tm=_pick(rows, pref))
    return [tuple(res[3 * k:3 * k + 3]) for k in range(n)]


BIG = ("ffn1_w_gate", "ffn1_w_up", "ffn1_w_down", "w_in", "w_out", "ffn2_w_gate", "ffn2_w_up", "ffn2_w_down",
       "w_ple_gate", "w_ple_proj")
SMALL = ("ffn1_norm", "mix_norm", "b_mlstm_gates", "b_fox_f", "mlstm_out_norm", "fox_out_norm", "ffn2_norm",
         "ple_gate_norm", "ple_proj_norm", "final_norm")
WEIGHTS = ("ffn1_norm", "ffn1_w_gate", "ffn1_w_up", "ffn1_w_down", "mix_norm", "w_in", "conv_qk", "b_mlstm_gates",
           "b_fox_f", "mlstm_out_norm", "fox_out_norm", "w_out", "ffn2_norm", "ffn2_w_gate", "ffn2_w_up", "ffn2_w_down",
           "ple_gate_norm", "w_ple_gate", "w_ple_proj", "ple_proj_norm", "final_norm")
TRANSPOSED = ("ffn1_w_gate", "ffn1_w_up", "w_in", "ffn2_w_gate", "ffn2_w_up")
PACK_W = 1024


def _chip_blocks(a):
    r, c4 = a.shape
    return a.reshape(r, 4, c4 // 4).transpose(1, 0, 2)


def _from_chip_blocks(a):
    nb, r, c = a.shape
    return a.transpose(1, 0, 2).reshape(r, nb * c)


def kernel(x, p, ffn1_norm, ffn1_w_gate, ffn1_w_up, ffn1_w_down, mix_norm, w_in, conv_qk, b_mlstm_gates, b_fox_f, mlstm_out_norm, fox_out_norm, w_out, ffn2_norm, ffn2_w_gate, ffn2_w_up, ffn2_w_down, ple_gate_norm, w_ple_gate, w_ple_proj, ple_proj_norm, final_norm, loss_target, m_ffn1_norm, m_ffn1_w_gate, m_ffn1_w_up, m_ffn1_w_down, m_mix_norm, m_w_in, m_conv_qk, m_b_mlstm_gates, m_b_fox_f, m_mlstm_out_norm, m_fox_out_norm, m_w_out, m_ffn2_norm, m_ffn2_w_gate, m_ffn2_w_up, m_ffn2_w_down, m_ple_gate_norm, m_w_ple_gate, m_w_ple_proj, m_ple_proj_norm, m_final_norm, v_ffn1_norm, v_ffn1_w_gate, v_ffn1_w_up, v_ffn1_w_down, v_mix_norm, v_w_in, v_conv_qk, v_b_mlstm_gates, v_b_fox_f, v_mlstm_out_norm, v_fox_out_norm, v_w_out, v_ffn2_norm, v_ffn2_w_gate, v_ffn2_w_up, v_ffn2_w_down, v_ple_gate_norm, v_w_ple_gate, v_w_ple_proj, v_ple_proj_norm, v_final_norm):
    w = dict(ffn1_norm=ffn1_norm, ffn1_w_gate=ffn1_w_gate, ffn1_w_up=ffn1_w_up, ffn1_w_down=ffn1_w_down, mix_norm=mix_norm,
             w_in=w_in, conv_qk=conv_qk, b_mlstm_gates=b_mlstm_gates, b_fox_f=b_fox_f, mlstm_out_norm=mlstm_out_norm,
             fox_out_norm=fox_out_norm, w_out=w_out, ffn2_norm=ffn2_norm, ffn2_w_gate=ffn2_w_gate, ffn2_w_up=ffn2_w_up,
             ffn2_w_down=ffn2_w_down, ple_gate_norm=ple_gate_norm, w_ple_gate=w_ple_gate, w_ple_proj=w_ple_proj,
             ple_proj_norm=ple_proj_norm, final_norm=final_norm)
    m = dict(ffn1_norm=m_ffn1_norm, ffn1_w_gate=m_ffn1_w_gate, ffn1_w_up=m_ffn1_w_up, ffn1_w_down=m_ffn1_w_down,
             mix_norm=m_mix_norm, w_in=m_w_in, conv_qk=m_conv_qk, b_mlstm_gates=m_b_mlstm_gates, b_fox_f=m_b_fox_f,
             mlstm_out_norm=m_mlstm_out_norm, fox_out_norm=m_fox_out_norm, w_out=m_w_out, ffn2_norm=m_ffn2_norm,
             ffn2_w_gate=m_ffn2_w_gate, ffn2_w_up=m_ffn2_w_up, ffn2_w_down=m_ffn2_w_down, ple_gate_norm=m_ple_gate_norm,
             w_ple_gate=m_w_ple_gate, w_ple_proj=m_w_ple_proj, ple_proj_norm=m_ple_proj_norm, final_norm=m_final_norm)
    v = dict(ffn1_norm=v_ffn1_norm, ffn1_w_gate=v_ffn1_w_gate, ffn1_w_up=v_ffn1_w_up, ffn1_w_down=v_ffn1_w_down,
             mix_norm=v_mix_norm, w_in=v_w_in, conv_qk=v_conv_qk, b_mlstm_gates=v_b_mlstm_gates, b_fox_f=v_b_fox_f,
             mlstm_out_norm=v_mlstm_out_norm, fox_out_norm=v_fox_out_norm, w_out=v_w_out, ffn2_norm=v_ffn2_norm,
             ffn2_w_gate=v_ffn2_w_gate, ffn2_w_up=v_ffn2_w_up, ffn2_w_down=v_ffn2_w_down, ple_gate_norm=v_ple_gate_norm,
             w_ple_gate=v_w_ple_gate, w_ple_proj=v_w_ple_proj, ple_proj_norm=v_ple_proj_norm, final_norm=v_final_norm)
    shapes = {n: w[n].shape for n in WEIGHTS}

    def view(a, n):
        return a[0].T if n in TRANSPOSED else a.reshape(-1, a.shape[-1])

    def unview(a, n):
        return (a.T if n in TRANSPOSED else a).reshape(shapes[n])

    w2, m2, v2 = ({n: view(a, n) for n, a in d.items()} for d in (w, m, v))

    c_idx = lax.axis_index("c").astype(jnp.int32).reshape(1)
    me_idx = (2 * lax.axis_index("x") + lax.axis_index("y")).astype(jnp.int32).reshape(1)
    place = jnp.concatenate([me_idx, c_idx])
    slot = {}
    for grp in SAME_SHAPE:
        slot.update(zip(grp, _to_slot("slot_" + grp[0], [w2[n] for n in grp], me_idx, BF16)))
    slot["conv_qk"] = _to_slot("slot_conv_qk", [w2["conv_qk"]], me_idx, F32)[0]
    wg1, wu1, wd1 = _gather4("gather_ffn1", [slot[n] for n in FFN1], [SPLIT[n] for n in FFN1])
    sp = {n: w2[n] for n in SMALL}
    loss_part, grad_x, grads, gs, conv_grad = _local_step(
        x[0], p[0, 0], loss_target[0], sp, wg1, wu1, wd1, [slot[n] for n in REST + ("conv_qk",)], c_idx, place)

    small = [gs[n].reshape(1, -1) for n in SMALL] + [conv_grad, loss_part]
    rows = [jnp.pad(a, ((0, 0), (0, PACK_W - a.shape[1]))) for a in small]
    packed = jnp.concatenate(rows, axis=0)
    packed = jnp.pad(packed, ((0, -packed.shape[0] % 8), (0, 0)))
    red = _allreduce_small(packed)
    loss = red[len(SMALL) + CONV_W, 0]
    for i, n in enumerate(SMALL):
        grads[n] = red[i:i + 1, :gs[n].size]
    dconv = red[len(SMALL):len(SMALL) + CONV_W, :conv_grad.shape[1]]
    cw = conv_qk.shape[-1]
    grads["conv_qk"] = lax.dynamic_slice_in_dim(dconv, (2 * lax.axis_index("x") + lax.axis_index("y")) * cw, cw, axis=1)

    outs = {}
    for grp in SAME_SHAPE + tuple((n,) for n in WEIGHTS if n not in BIG):
        g2s = [grads[n].reshape(w2[n].shape) for n in grp]
        res = _adamw("adamw_" + grp[0], [w2[n] for n in grp], g2s, [m2[n] for n in grp], [v2[n] for n in grp])
        for n, g2, (d, nm, nv) in zip(grp, g2s, res):
            outs[n] = tuple(unview(a, n) for a in (g2, d, nm, nv))
    return (loss, grad_x[None], *[outs[n][0] for n in WEIGHTS], *[outs[n][1] for n in WEIGHTS],
            *[outs[n][2] for n in WEIGHTS], *[outs[n][3] for n in WEIGHTS])
```

```python
import jax
import jax.numpy as jnp
from jax import lax
from jax.experimental import pallas as pl
from jax.experimental.pallas import tpu as pltpu

F32 = jnp.float32
BF16 = jnp.bfloat16
EPS = 1e-6
NH_M, DK_M, DV_M = 4, 64, 128
NH_F, DH_F = 8, 64
CONV_W = 4
ADAM_LR, ADAM_B1, ADAM_B2, ADAM_EPS, ADAM_WD, ADAM_STEP = 0.001, 0.9, 0.999, 1e-08, 0.01, 10
VMEM_LIMIT = 56 * 1024 * 1024


def _cparams(sem):
    return pltpu.CompilerParams(dimension_semantics=sem, vmem_limit_bytes=VMEM_LIMIT)


def _sigmoid(x):
    return 1.0 / (1.0 + jnp.exp(-x))


def _dot(a, b, ca, cb):
    return lax.dot_general(a.astype(BF16), b.astype(BF16), (((ca,), (cb,)), ((), ())), preferred_element_type=F32)


def _rowwise(name, fn, tiled, full, outs, accs=(), tm=512, plan=None):
    rows = tiled[0].shape[0]
    tm = min(tm, rows)
    assert rows % tm == 0
    n_t, n_f, n_o, n_a = len(tiled), len(full), len(outs), len(accs)
    nt = rows // tm
    host = _Hosted(plan, n_t + n_f, n_o + n_a)

    def body(*refs):
        in_refs, orefs, _, prefs = host.split(refs)
        host.run(0, pl.program_id(0) == 0, prefs)
        host.run(1, pl.program_id(0) == 0, prefs)
        ins = [r[...] for r in in_refs]
        res = fn(*ins)
        if not isinstance(res, (tuple, list)):
            res = (res,)
        for r, v in zip(orefs[:n_o], res[:n_o]):
            r[...] = v.astype(r.dtype)
        if n_a:
            @pl.when(pl.program_id(0) == 0)
            def _():
                for r in orefs[n_o:]:
                    r[...] = jnp.zeros_like(r)
            for r, v in zip(orefs[n_o:], res[n_o:]):
                r[...] += v.astype(r.dtype)
        host.run(2, pl.program_id(0) == nt - 1, prefs)

    in_specs = [pl.BlockSpec((tm, a.shape[1]), lambda i: (i, 0)) for a in tiled]
    in_specs += [pl.BlockSpec(a.shape, lambda i: (0, 0)) for a in full]
    out_specs = [pl.BlockSpec((tm, c), lambda i: (i, 0)) for c, _ in outs]
    out_specs += [pl.BlockSpec(s, lambda i: (0, 0)) for s, _ in accs]
    out_shape = [jax.ShapeDtypeStruct((rows, c), d) for c, d in outs]
    out_shape += [jax.ShapeDtypeStruct(s, d) for s, d in accs]
    hc = host.call_args()
    res = pl.pallas_call(
        body, name=name, grid=(nt,), in_specs=in_specs + hc["in_specs"], out_specs=out_specs + hc["out_specs"],
        out_shape=out_shape + hc["out_shape"], scratch_shapes=hc["scratch"], input_output_aliases=hc["aliases"],
        compiler_params=_cparams(("arbitrary",) if (n_a or plan is not None) else ("parallel",)),
    )(*tiled, *full, *hc["args"])
    return res if plan is None else (res[:n_o + n_a], res[n_o + n_a:])


def _colsum(v):
    return jnp.sum(v, axis=0, keepdims=True)


def _rms_fwd_val(x, g):
    r = lax.rsqrt(jnp.mean(x * x, axis=-1, keepdims=True) + EPS)
    return x * r * g


def _rms_bwd_val(dy, x, g):
    r = lax.rsqrt(jnp.mean(x * x, axis=-1, keepdims=True) + EPS)
    xh = x * r
    dxh = dy * g
    dx = r * (dxh - xh * jnp.mean(dxh * xh, axis=-1, keepdims=True))
    return dx, _colsum(dy * xh)


def _mm(name, pairs, out_shape, out_block, out_map, grid, kaxis, ta=False, tb=False, scale=None, res=None,
        out_dtype=F32, plan=None, twin=False):
    n_o = 2 if twin else 1
    nk = grid[kaxis]
    npairs = len(pairs)
    ca, cb = (0 if ta else 1), (1 if tb else 0)
    acc_shape = tuple(d for d in out_block if d is not None)
    n_in = 2 * npairs + (1 if res is not None else 0)
    host = _Hosted(plan, n_in, n_o)

    def body(*refs):
        ins, o_refs, (acc_ref,), prefs = host.split(refs)
        o_ref = o_refs[0]
        in_refs = ins[: 2 * npairs]
        res_ref = ins[2 * npairs] if res is not None else None
        k = pl.program_id(kaxis)
        ids = [pl.program_id(a) for a in range(len(grid))]
        first, last = ids[0] == 0, ids[0] == grid[0] - 1
        for a in range(1, len(grid)):
            first, last = first & (ids[a] == 0), last & (ids[a] == grid[a] - 1)
        host.run(0, first, prefs)
        host.run(1, first, prefs)

        @pl.when(k == 0)
        def _():
            acc_ref[...] = jnp.zeros_like(acc_ref)

        part = None
        for p in range(npairs):
            d = _dot(in_refs[2 * p][...], in_refs[2 * p + 1][...], ca, cb)
            part = d if part is None else part + d
        acc_ref[...] += part

        @pl.when(k == nk - 1)
        def _():
            v = acc_ref[...]
            if scale is not None:
                v = v * scale
            if res_ref is not None:
                v = v + res_ref[...].astype(F32)
            o_ref[...] = v.astype(o_ref.dtype)
            if twin:
                o_refs[1][...] = v.astype(BF16)

        host.run(2, last, prefs)

    in_specs, args = [], []
    for a, ab, am, b, bb, bm in pairs:
        in_specs += [pl.BlockSpec(ab, am), pl.BlockSpec(bb, bm)]
        args += [a, b]
    if res is not None:
        in_specs.append(pl.BlockSpec(out_block, out_map))
        args.append(res)
    sem = tuple("arbitrary" if (i == kaxis or plan is not None) else "parallel" for i in range(len(grid)))
    hc = host.call_args()
    out = pl.pallas_call(
        body, name=name, grid=grid, in_specs=in_specs + hc["in_specs"],
        out_specs=[pl.BlockSpec(out_block, out_map)] * n_o + hc["out_specs"],
        out_shape=[jax.ShapeDtypeStruct(out_shape, out_dtype)] + [jax.ShapeDtypeStruct(out_shape, BF16)] * (n_o - 1)
        + hc["out_shape"],
        scratch_shapes=[pltpu.VMEM(acc_shape, F32)] + hc["scratch"], input_output_aliases=hc["aliases"],
        compiler_params=_cparams(sem),
    )(*args, *hc["args"])
    res_out = tuple(out[:2]) if twin else out[0]
    return res_out if plan is None else (res_out, out[n_o:])


def _pick(n, pref):
    for t in pref:
        if n % t == 0:
            return t
    return n


def _mm_nn(name, a, b, tm=512, tn=512, tk=512, **kw):
    (m, k), n = a.shape, b.shape[1]
    tm, tn, tk = _pick(m, (tm, 256, 128)), _pick(n, (tn, 256, 128)), _pick(k, (tk, 256, 128))
    return _mm(name, [(a, (tm, tk), lambda i, j, kk: (i, kk), b, (tk, tn), lambda i, j, kk: (kk, j))],
               (m, n), (tm, tn), lambda i, j, kk: (i, j), (m // tm, n // tn, k // tk), 2, **kw)


def _mm_nt(name, a, b, tm=512, tn=512, tk=512, **kw):
    (m, k), n = a.shape, b.shape[0]
    tm, tn, tk = _pick(m, (tm, 256, 128)), _pick(n, (tn, 256, 128)), _pick(k, (tk, 256, 128))
    return _mm(name, [(a, (tm, tk), lambda i, j, kk: (i, kk), b, (tn, tk), lambda i, j, kk: (j, kk))],
               (m, n), (tm, tn), lambda i, j, kk: (i, j), (m // tm, n // tn, k // tk), 2, tb=True, **kw)


def _mm_tn(name, a, b, tm=512, tn=512, tk=4096, **kw):
    (k, m), n = a.shape, b.shape[1]
    tm, tn, tk = _pick(m, (tm, 256, 128)), _pick(n, (tn, 256, 128)), _pick(k, (tk, 2048, 1024, 512, 256, 128))
    return _mm(name, [(a, (tk, tm), lambda i, j, kk: (kk, i), b, (tk, tn), lambda i, j, kk: (kk, j))],
               (m, n), (tm, tn), lambda i, j, kk: (i, j), (m // tm, n // tn, k // tk), 2, ta=True, **kw)


def _norm_mm(name, h, gamma, w, w_transposed, out_dtype):
    t, d = h.shape
    n = w.shape[0] if w_transposed else w.shape[1]
    tm, tn = _pick(t, (512, 256)), _pick(n, (1024, 512, 256, 128))

    def body(h_ref, gam_ref, w_ref, xn_ref, o_ref, xn_scr):
        @pl.when(pl.program_id(1) == 0)
        def _():
            xn = _rms_fwd_val(h_ref[...], gam_ref[...]).astype(BF16)
            xn_scr[...] = xn
            xn_ref[...] = xn

        o_ref[...] = _dot(xn_scr[...], w_ref[...], 1, 1 if w_transposed else 0).astype(o_ref.dtype)

    wspec = pl.BlockSpec((tn, d), lambda i, j: (j, 0)) if w_transposed else pl.BlockSpec((d, tn), lambda i, j: (0, j))
    return pl.pallas_call(
        body, name=name, grid=(t // tm, n // tn),
        in_specs=[pl.BlockSpec((tm, d), lambda i, j: (i, 0)), pl.BlockSpec((1, d), lambda i, j: (0, 0)), wspec],
        out_specs=[pl.BlockSpec((tm, d), lambda i, j: (i, 0)), pl.BlockSpec((tm, tn), lambda i, j: (i, j))],
        out_shape=[jax.ShapeDtypeStruct((t, d), BF16), jax.ShapeDtypeStruct((t, n), out_dtype)],
        scratch_shapes=[pltpu.VMEM((tm, d), BF16)], compiler_params=_cparams(("parallel", "arbitrary")),
    )(h, gamma, w)


CHAIN_ROWS = 256


def _row_chains(tm):
    n = max(tm // CHAIN_ROWS, 1)
    return [slice(r * (tm // n), (r + 1) * (tm // n)) for r in range(n)]


def _ffn_fwd(pfx, h, gamma, wg, wu, wd, plan=None):
    t, d = h.shape
    nb, f, _ = wg.shape
    tm = _pick(t, (1024, 512, 256))
    nt = t // tm
    host = _Hosted(plan, 5, 4)

    def body(*refs):
        (h_ref, gam_ref, wg_ref, wu_ref, wd_ref), (ho_ref, xn_ref, g_ref, u_ref), (xn_scr, acc_ref), prefs = host.split(refs)
        i, j = pl.program_id(0), pl.program_id(1)
        host.run(0, (i == 0) & (j == 0), prefs)
        host.run(1, (i == nt // 2) & (j == 0), prefs)

        @pl.when(j == 0)
        def _():
            xn = _rms_fwd_val(h_ref[...], gam_ref[...]).astype(BF16)
            xn_scr[...] = xn
            xn_ref[...] = xn
            acc_ref[...] = jnp.zeros_like(acc_ref)

        for rows in _row_chains(tm):
            x = xn_scr[rows, :]
            g = _dot(x, wg_ref[...], 1, 1)
            u = _dot(x, wu_ref[...], 1, 1)
            g_ref[rows, :] = g.astype(BF16)
            u_ref[rows, :] = u.astype(BF16)
            acc_ref[rows, :] += _dot(g * _sigmoid(g) * u, wd_ref[...], 1, 0)

        @pl.when(j == nb - 1)
        def _():
            ho_ref[...] = h_ref[...] + 0.5 * acc_ref[...]

        host.run(2, (i == nt - 1) & (j == nb - 1), prefs)

    row = pl.BlockSpec((tm, d), lambda i, j: (i, 0))
    blk = pl.BlockSpec((None, tm, f), lambda i, j: (j, i, 0))
    wspec = pl.BlockSpec((None, f, d), lambda i, j: (j, 0, 0))
    hc = host.call_args()
    res = pl.pallas_call(
        body, name=pfx + "_fwd", grid=(nt, nb),
        in_specs=[row, pl.BlockSpec((1, d), lambda i, j: (0, 0)), wspec, wspec, wspec] + hc["in_specs"],
        out_specs=[row, row, blk, blk] + hc["out_specs"],
        out_shape=[jax.ShapeDtypeStruct((t, d), F32), jax.ShapeDtypeStruct((t, d), BF16),
                   jax.ShapeDtypeStruct((nb, t, f), BF16), jax.ShapeDtypeStruct((nb, t, f), BF16)] + hc["out_shape"],
        scratch_shapes=[pltpu.VMEM((tm, d), BF16), pltpu.VMEM((tm, d), F32)] + hc["scratch"],
        input_output_aliases=hc["aliases"], compiler_params=_cparams(("arbitrary", "arbitrary")),
    )(h, gamma, wg, wu, wd, *hc["args"])
    return res[:4], res[4:]


def _ffn_bwd(pfx, dh_out, h, gamma, xn, g_all, u_all, wg, wu, wd, plan=None):
    t, d = h.shape
    nb, f, _ = wg.shape
    tm = _pick(t, (512, 256))
    tk = _pick(t, (4096, 2048, 1024, 512, 256))

    nt = t // tm
    host = _Hosted(plan, 8, 6)

    def body(*refs):
        ((dy_ref, h_ref, gam_ref, wg_ref, wu_ref, wd_ref, g_ref, u_ref),
         (dh_ref, dgam_ref, dg_ref, du_ref, a_ref, dyb_ref), (acc_ref,), prefs) = host.split(refs)
        i, j = pl.program_id(0), pl.program_id(1)
        host.run(0, (i == 0) & (j == 0), prefs)
        host.run(1, (i == nt // 2) & (j == 0), prefs)

        @pl.when((i == 0) & (j == 0))
        def _():
            dgam_ref[...] = jnp.zeros_like(dgam_ref)

        @pl.when(j == 0)
        def _():
            acc_ref[...] = jnp.zeros_like(acc_ref)
            dyb_ref[...] = dy_ref[...].astype(BF16)

        for rows in _row_chains(tm):
            da = _dot(dy_ref[rows, :], wd_ref[...], 1, 1) * 0.5
            g = g_ref[rows, :].astype(F32)
            u = u_ref[rows, :].astype(F32)
            s = _sigmoid(g)
            sl = g * s
            du = (da * sl).astype(BF16)
            dg = (da * u * (s + sl * (1.0 - s))).astype(BF16)
            du_ref[rows, :] = du
            dg_ref[rows, :] = dg
            a_ref[rows, :] = (sl * u).astype(BF16)
            acc_ref[rows, :] += _dot(dg, wg_ref[...], 1, 0) + _dot(du, wu_ref[...], 1, 0)

        @pl.when(j == nb - 1)
        def _():
            dx, dgam = _rms_bwd_val(acc_ref[...], h_ref[...], gam_ref[...])
            dh_ref[...] = dy_ref[...] + dx
            dgam_ref[...] += dgam

        host.run(2, (i == nt - 1) & (j == nb - 1), prefs)

    row = pl.BlockSpec((tm, d), lambda i, j: (i, 0))
    vec = pl.BlockSpec((1, d), lambda i, j: (0, 0))
    blk = pl.BlockSpec((None, tm, f), lambda i, j: (j, i, 0))
    wspec = pl.BlockSpec((None, f, d), lambda i, j: (j, 0, 0))
    hc = host.call_args()
    res = pl.pallas_call(
        body, name=pfx + "_bwd", grid=(nt, nb),
        in_specs=[row, row, vec, wspec, wspec, wspec, blk, blk] + hc["in_specs"],
        out_specs=[row, vec, blk, blk, blk, row] + hc["out_specs"],
        out_shape=[jax.ShapeDtypeStruct((t, d), F32), jax.ShapeDtypeStruct((1, d), F32)]
        + [jax.ShapeDtypeStruct((nb, t, f), BF16)] * 3 + [jax.ShapeDtypeStruct((t, d), BF16)] + hc["out_shape"],
        scratch_shapes=[pltpu.VMEM((tm, d), F32)] + hc["scratch"], input_output_aliases=hc["aliases"],
        compiler_params=_cparams(("arbitrary", "arbitrary")),
    )(dh_out, h, gamma, wg, wu, wd, g_all, u_all, *hc["args"])
    dh, dgamma, dg_all, du_all, a_all, dyb = res[:6]

    xmap, bmap, omap = (lambda b, k: (k, 0)), (lambda b, k: (b, k, 0)), (lambda b, k: (b, 0, 0))
    dwg, tg = _mm(pfx + "_dwg", [(dg_all, (None, tk, f), bmap, xn, (tk, d), xmap)], (nb, f, d), (None, f, d), omap,
                  (nb, t // tk), 1, ta=True, twin=True)
    dwu, tu = _mm(pfx + "_dwu", [(du_all, (None, tk, f), bmap, xn, (tk, d), xmap)], (nb, f, d), (None, f, d), omap,
                  (nb, t // tk), 1, ta=True, twin=True)
    dwd, td = _mm(pfx + "_dwd", [(a_all, (None, tk, f), bmap, dyb, (tk, d), xmap)], (nb, f, d), (None, f, d), omap,
                  (nb, t // tk), 1, ta=True, scale=0.5, twin=True)
    return (dh, dgamma, dwg, dwu, dwd), res[6:], (tg, tu, td)


def _ffn_bwd_late_dx(pfx, dh_out, h, gamma, xn, g_all, u_all, wg, wu, wd, plan_gu, plans_dw, make_plan_dx):
    t, d = h.shape
    nb, f, _ = wg.shape
    tm = _pick(t, (512, 256))
    tk = _pick(t, (4096, 2048, 1024, 512, 256))
    nt = t // tm
    host_a = _Hosted(plan_gu, 4, 4)

    def body_a(*refs):
        (dy_ref, wd_ref, g_ref, u_ref), (dg_ref, du_ref, a_ref, dyb_ref), _, prefs = host_a.split(refs)
        i, j = pl.program_id(0), pl.program_id(1)
        host_a.run(0, (i == 0) & (j == 0), prefs)
        host_a.run(1, (i == 0) & (j == 0), prefs)

        @pl.when(j == 0)
        def _():
            dyb_ref[...] = dy_ref[...].astype(BF16)

        for rows in _row_chains(tm):
            da = _dot(dy_ref[rows, :], wd_ref[...], 1, 1) * 0.5
            g = g_ref[rows, :].astype(F32)
            u = u_ref[rows, :].astype(F32)
            s = _sigmoid(g)
            sl = g * s
            du_ref[rows, :] = (da * sl).astype(BF16)
            dg_ref[rows, :] = (da * u * (s + sl * (1.0 - s))).astype(BF16)
            a_ref[rows, :] = (sl * u).astype(BF16)
        host_a.run(2, (i == nt - 1) & (j == nb - 1), prefs)

    row = pl.BlockSpec((tm, d), lambda i, j: (i, 0))
    vec = pl.BlockSpec((1, d), lambda i, j: (0, 0))
    blk = pl.BlockSpec((None, tm, f), lambda i, j: (j, i, 0))
    wspec = pl.BlockSpec((None, f, d), lambda i, j: (j, 0, 0))
    hc = host_a.call_args()
    res_a = pl.pallas_call(
        body_a, name=pfx + "_bwd_gu", grid=(nt, nb), in_specs=[row, wspec, blk, blk] + hc["in_specs"],
        out_specs=[blk] * 3 + [row] + hc["out_specs"],
        out_shape=[jax.ShapeDtypeStruct((nb, t, f), BF16)] * 3 + [jax.ShapeDtypeStruct((t, d), BF16)] + hc["out_shape"],
        scratch_shapes=hc["scratch"], input_output_aliases=hc["aliases"], compiler_params=_cparams(("arbitrary", "arbitrary")),
    )(dh_out, wd, g_all, u_all, *hc["args"])
    dg_all, du_all, a_all, dyb = res_a[:4]

    xmap, bmap, omap = (lambda b, k: (k, 0)), (lambda b, k: (b, k, 0)), (lambda b, k: (b, 0, 0))
    def dw(name, a, b, plan, scale=None):
        r = _mm(pfx + name, [(a, (None, tk, f), bmap, b, (tk, d), xmap)], (nb, f, d), (None, f, d), omap, (nb, t // tk), 1,
                ta=True, scale=scale, plan=plan, twin=True)
        return r if plan is not None else (r, ())

    (dwd, td), out_d = dw("_dwd", a_all, dyb, plans_dw[0], 0.5)
    (dwg, tg), out_g = dw("_dwg", dg_all, xn, plans_dw[1])
    (dwu, tu), out_u = dw("_dwu", du_all, xn, plans_dw[2])

    plan_dx = make_plan_dx((dwg, dwu, dwd), (tg, tu, td))
    host_b = _Hosted(plan_dx, 7, 2)

    def body_b(*refs):
        (dy_ref, h_ref, gam_ref, wg_ref, wu_ref, dg_ref, du_ref), (dh_ref, dgam_ref), (acc_ref,), prefs = host_b.split(refs)
        i, j = pl.program_id(0), pl.program_id(1)
        host_b.run(0, (i == 0) & (j == 0), prefs)
        host_b.run(1, (i == 0) & (j == 0), prefs)

        @pl.when((i == 0) & (j == 0))
        def _():
            dgam_ref[...] = jnp.zeros_like(dgam_ref)

        @pl.when(j == 0)
        def _():
            acc_ref[...] = jnp.zeros_like(acc_ref)

        acc_ref[...] += _dot(dg_ref[...], wg_ref[...], 1, 0) + _dot(du_ref[...], wu_ref[...], 1, 0)

        @pl.when(j == nb - 1)
        def _():
            dx, dgam = _rms_bwd_val(acc_ref[...], h_ref[...], gam_ref[...])
            dh_ref[...] = dy_ref[...] + dx
            dgam_ref[...] += dgam

        host_b.run(2, (i == nt - 1) & (j == nb - 1), prefs)

    hc = host_b.call_args()
    res_b = pl.pallas_call(
        body_b, name=pfx + "_bwd_dx", grid=(nt, nb), in_specs=[row, row, vec, wspec, wspec, blk, blk] + hc["in_specs"],
        out_specs=[row, vec] + hc["out_specs"],
        out_shape=[jax.ShapeDtypeStruct((t, d), F32), jax.ShapeDtypeStruct((1, d), F32)] + hc["out_shape"],
        scratch_shapes=[pltpu.VMEM((tm, d), F32)] + hc["scratch"], input_output_aliases=hc["aliases"],
        compiler_params=_cparams(("arbitrary", "arbitrary")),
    )(dh_out, h, gamma, wg, wu, dg_all, du_all, *hc["args"])
    return (res_b[0], res_b[1], dwg, dwu, dwd), (res_a[4:], out_d, out_g, out_u, res_b[2:])


HALO = 16


def _silu_grad(y):
    s = _sigmoid(y)
    return s * (1.0 + y * (1.0 - s))


def _with_halo(ref, i, n_tiles, tm, before, after):
    t = ref.shape[0]
    r0 = pl.multiple_of(i * tm, tm)
    parts = [ref[pl.ds(r0, tm), :].astype(F32)]
    if before:
        prev = ref[pl.ds(pl.multiple_of(jnp.maximum(r0 - HALO, 0), HALO), HALO), :].astype(F32)
        parts.insert(0, jnp.where(i > 0, prev, 0.0))
    if after:
        nxt = ref[pl.ds(pl.multiple_of(jnp.minimum(r0 + tm, t - HALO), HALO), HALO), :].astype(F32)
        parts.append(jnp.where(i < n_tiles - 1, nxt, 0.0))
    return jnp.concatenate(parts, axis=0)


def _conv_fwd(zbig, w):
    t, c = zbig.shape[0], w.shape[1]
    tm = _pick(t, (512, 256))
    nt = t // tm

    def body(x_ref, w_ref, o_ref):
        xe = _with_halo(x_ref, pl.program_id(0), nt, tm, True, False)
        wv = w_ref[...]
        y = xe * wv[3:4, :]
        for i in range(CONV_W - 1):
            y = y + pltpu.roll(xe, CONV_W - 1 - i, 0) * wv[i:i + 1, :]
        y = y[HALO:, :]
        o_ref[...] = (y * _sigmoid(y)).astype(o_ref.dtype)

    return pl.pallas_call(
        body, name="conv_fwd", grid=(nt,),
        in_specs=[pl.BlockSpec((t, c), lambda i: (0, 0)), pl.BlockSpec(w.shape, lambda i: (0, 0))],
        out_specs=pl.BlockSpec((tm, c), lambda i: (i, 0)), out_shape=jax.ShapeDtypeStruct((t, c), BF16),
        compiler_params=_cparams(("parallel",)),
    )(zbig, w)


def _conv_bwd(zbig, dact, w):
    t, c = dact.shape
    tm = _pick(t, (512, 256))
    nt = t // tm
    n = tm + HALO

    def body(x_ref, d_ref, w_ref, dx_ref, dw_ref):
        xe = _with_halo(x_ref, pl.program_id(0), nt, tm, True, True)
        de = _with_halo(d_ref, pl.program_id(0), nt, tm, False, True)
        wv = w_ref[...]
        sh = [pltpu.roll(xe, CONV_W - 1 - i, 0)[HALO:, :] if i < CONV_W - 1 else xe[HALO:, :] for i in range(CONV_W)]
        y = sh[0] * wv[0:1, :]
        for i in range(1, CONV_W):
            y = y + sh[i] * wv[i:i + 1, :]
        dy = de * _silu_grad(y)
        dx = dy * wv[3:4, :]
        for i in range(CONV_W - 1):
            dx = dx + pltpu.roll(dy, n - (CONV_W - 1 - i), 0) * wv[i:i + 1, :]
        dx_ref[...] = dx[:tm, :].astype(dx_ref.dtype)
        dyc = dy[:tm, :]
        dwp = jnp.concatenate([_colsum(dyc * sh[i][:tm, :]) for i in range(CONV_W)], axis=0)

        @pl.when(pl.program_id(0) == 0)
        def _():
            dw_ref[...] = jnp.zeros_like(dw_ref)
        dw_ref[...] += dwp

    return pl.pallas_call(
        body, name="conv_bwd", grid=(nt,),
        in_specs=[pl.BlockSpec((t, c), lambda i: (0, 0)), pl.BlockSpec((t, c), lambda i: (0, 0)),
                  pl.BlockSpec(w.shape, lambda i: (0, 0))],
        out_specs=[pl.BlockSpec((tm, c), lambda i: (i, 0)), pl.BlockSpec(w.shape, lambda i: (0, 0))],
        out_shape=[jax.ShapeDtypeStruct((t, c), BF16), jax.ShapeDtypeStruct(w.shape, F32)],
        compiler_params=_cparams(("arbitrary",)),
    )(zbig, dact, w)


LM = 256
HI = lax.Precision.HIGHEST


def _logsig(x):
    return jnp.minimum(x, 0.0) - jnp.log(1.0 + jnp.exp(-jnp.abs(x)))


def _tri(n, lower):
    r = lax.broadcasted_iota(jnp.int32, (n, n), 0)
    c = lax.broadcasted_iota(jnp.int32, (n, n), 1)
    return (r >= c) if lower else (r <= c)


def _f32dot(a, b):
    return lax.dot_general(a, b, (((1,), (0,)), ((), ())), precision=HI, preferred_element_type=F32)


def _tri_dot(a, b, a_is_tri):
    tri = (a if a_is_tri else b).astype(BF16)
    parts = _split3(b if a_is_tri else a)
    outs = [_dot(tri, p, 1, 0) if a_is_tri else _dot(p, tri, 1, 0) for p in parts]
    return (outs[0] + outs[1]) + outs[2]


def _mlstm_decays(zs_ref, zsr_ref, bc_ref, br_ref):
    l = LM
    lf_c = _logsig(zs_ref[:, 0:2 * NH_M] + bc_ref[...])
    lf_r = _logsig(zsr_ref[...] + br_ref[...])
    low, up = _tri(l, True), _tri(l, False)
    return _tri_dot(low, lf_c, True), _tri_dot(lf_r, up, False), low, up


def _mlstm_chunk(h, q_ref, k_ref, v_ref, zs_ref, zsr_ref, bc_ref, br_ref, c_prev, m_prev, decays):
    l = LM
    q = q_ref[:, h * DK_M:(h + 1) * DK_M].astype(F32) * (DK_M ** -0.5)
    k = k_ref[:, h * DK_M:(h + 1) * DK_M]
    v = v_ref[:, h * DV_M:(h + 1) * DV_M]
    lane = lax.broadcasted_iota(jnp.int32, (l, DV_M), 1)
    v1 = jnp.concatenate([v, (lane == 0).astype(v.dtype)], axis=1)
    zs, zsr = zs_ref[...], zsr_ref[...]
    li_c = zs[:, h:h + 1] + bc_ref[:, h:h + 1]
    fp_c = zs[:, NH_M + h:NH_M + h + 1] + bc_ref[:, NH_M + h:NH_M + h + 1]
    li_r = zsr[h:h + 1, :] + br_ref[h:h + 1, :]
    fp_r = zsr[NH_M + h:NH_M + h + 1, :] + br_ref[NH_M + h:NH_M + h + 1, :]
    low = decays[2]
    b_c = decays[0][:, NH_M + h:NH_M + h + 1]
    b_r = decays[1][NH_M + h:NH_M + h + 1, :]
    g = b_r[:, l - 1:l]
    dmat = jnp.where(low, b_c - b_r + li_r, -jnp.inf)
    inter = b_c + m_prev
    m_t = jnp.maximum(inter, jnp.max(dmat, axis=1, keepdims=True))
    w_inter = jnp.exp(inter - m_t)
    amat = jnp.exp(dmat - m_t)
    s = _dot(q, k, 1, 1)
    p = amat * s
    qc = _dot(q, c_prev, 1, 0)
    qc_w = w_inter * qc
    num1 = qc_w + _dot(p, v1, 1, 0)
    den = num1[:, DV_M:DV_M + 1]
    mx = jnp.maximum(jnp.abs(den), jnp.exp(-m_t))
    hh = num1[:, :DV_M] / mx
    a_c = g - b_c + li_c
    return dict(q=q, k=k, v1=v1, fp_c=fp_c, fp_r=fp_r, b_c=b_c, g=g, m_t=m_t, w_inter=w_inter, amat=amat, s=s, p=p,
                qc_w=qc_w, den=den, mx=mx, hh=hh, a_c=a_c)


def _mlstm_fwd(qk, zbig, zs, zsr, bc, br, gm):
    t = zs.shape[0]
    l = LM
    nc = t // l
    dm = NH_M * DV_M

    def body(q_ref, k_ref, v_ref, o_ref, zs_ref, zsr_ref, bc_ref, br_ref, gm_ref, y_ref, cst_ref, mst_ref, c_scr, m_scr):
        @pl.when(pl.program_id(0) == 0)
        def _():
            c_scr[...] = jnp.zeros_like(c_scr)
            m_scr[...] = jnp.zeros_like(m_scr)

        cst_ref[...] = c_scr[...]
        mst_ref[...] = m_scr[...]
        ys = []
        decays = _mlstm_decays(zs_ref, zsr_ref, bc_ref, br_ref)
        for h in range(NH_M):
            c_prev = c_scr[h]
            m_prev = m_scr[h:h + 1, 0:1]
            r = _mlstm_chunk(h, q_ref, k_ref, v_ref, zs_ref, zsr_ref, bc_ref, br_ref, c_prev, m_prev, decays)
            hh = r["hh"]
            gh = gm_ref[:, h * DV_M:(h + 1) * DV_M]
            hn = hh * lax.rsqrt(jnp.mean(hh * hh, axis=-1, keepdims=True) + EPS) * gh
            og = o_ref[:, h * DV_M:(h + 1) * DV_M].astype(F32)
            ys.append(hn * _sigmoid(og))
            m_new = jnp.maximum(r["g"] + m_prev, jnp.max(r["a_c"], axis=0, keepdims=True))
            decay = jnp.exp(r["g"] + m_prev - m_new)
            wk = r["k"].astype(F32) * jnp.exp(r["a_c"] - m_new)
            c_scr[h] = decay * c_prev + _dot(wk, r["v1"], 0, 0)
            m_scr[h:h + 1, :] = jnp.broadcast_to(m_new, (1, 128))
        y_ref[...] = jnp.concatenate(ys, axis=1).astype(y_ref.dtype)

    return pl.pallas_call(
        body, name="mlstm_fwd", grid=(nc,),
        in_specs=[pl.BlockSpec((l, NH_M * DK_M), lambda i: (i, 0)), pl.BlockSpec((l, NH_M * DK_M), lambda i: (i, 1)),
                  pl.BlockSpec((l, dm), lambda i: (i, 1)), pl.BlockSpec((l, dm), lambda i: (i, 2)),
                  pl.BlockSpec((l, 128), lambda i: (i, 0)), pl.BlockSpec((8, l), lambda i: (0, i)),
                  pl.BlockSpec((1, 8), lambda i: (0, 0)), pl.BlockSpec((8, 1), lambda i: (0, 0)),
                  pl.BlockSpec((1, dm), lambda i: (0, 0))],
        out_specs=[pl.BlockSpec((l, dm), lambda i: (i, 0)), pl.BlockSpec((None, NH_M, DK_M, 2 * DV_M), lambda i: (i, 0, 0, 0)),
                   pl.BlockSpec((None, 8, 128), lambda i: (i, 0, 0))],
        out_shape=[jax.ShapeDtypeStruct((t, dm), BF16), jax.ShapeDtypeStruct((nc, NH_M, DK_M, 2 * DV_M), F32),
                   jax.ShapeDtypeStruct((nc, 8, 128), F32)],
        scratch_shapes=[pltpu.VMEM((NH_M, DK_M, 2 * DV_M), F32), pltpu.VMEM((8, 128), F32)],
        compiler_params=_cparams(("arbitrary",)),
    )(qk, qk, zbig, zbig, zs, zsr, bc, br, gm)


def _mlstm_bwd(qk, zbig, zs, zsr, bc, br, gm, cst, mst, dycat):
    t = zs.shape[0]
    l = LM
    nc = t // l
    dm = NH_M * DV_M

    def body(q_ref, k_ref, v_ref, o_ref, zs_ref, zsr_ref, bc_ref, br_ref, gm_ref, cst_ref, mst_ref, cnx_ref, mnx_ref,
             dy_ref, dqk_ref, dv_ref, do_ref, dzs_ref, dzr_ref, dgm_ref, dc_scr):
        @pl.when(pl.program_id(0) == 0)
        def _():
            dc_scr[...] = jnp.zeros_like(dc_scr)
            dgm_ref[...] = jnp.zeros_like(dgm_ref)

        lane = lax.broadcasted_iota(jnp.int32, (l, 128), 1)
        db_all, sig_c, carries = jnp.zeros((l, 128), F32), jnp.zeros((l, 128), F32), jnp.zeros((1, 128), F32)
        decays = _mlstm_decays(zs_ref, zsr_ref, bc_ref, br_ref)
        lower, upper = decays[2], decays[3]
        dzr_rows = [None] * 8
        dvs, dos, dgs, dqs, dks = [], [], [], [], []
        dzs = jnp.zeros((l, 128), F32)
        for h in range(NH_M):
            c_prev = cst_ref[h]
            m_prev = mst_ref[h:h + 1, 0:1]
            r = _mlstm_chunk(h, q_ref, k_ref, v_ref, zs_ref, zsr_ref, bc_ref, br_ref, c_prev, m_prev, decays)
            hh, mx, den, m_t, v1, amat = r["hh"], r["mx"], r["den"], r["m_t"], r["v1"], r["amat"]
            gh = gm_ref[:, h * DV_M:(h + 1) * DV_M]
            rs = lax.rsqrt(jnp.mean(hh * hh, axis=-1, keepdims=True) + EPS)
            xh = hh * rs
            sg = _sigmoid(o_ref[:, h * DV_M:(h + 1) * DV_M].astype(F32))
            dyh = dy_ref[:, h * DV_M:(h + 1) * DV_M]
            dos.append(dyh * xh * gh * sg * (1.0 - sg))
            dhn = dyh * sg
            dgs.append(_colsum(dhn * xh))
            dxh = dhn * gh
            dh = rs * (dxh - xh * jnp.mean(dxh * xh, axis=-1, keepdims=True))
            g1 = dh / mx
            hd = jnp.sum(hh * dh, axis=-1, keepdims=True)
            dden = jnp.where(jnp.abs(den) > jnp.exp(-m_t), -hd / mx * jnp.sign(den), 0.0)
            g256 = jnp.concatenate([g1, jnp.where(lane == 0, dden, 0.0)], axis=1)
            dc_h = dc_scr[h]
            ea = jnp.exp(r["a_c"])
            dp = _dot(g256, v1, 1, 1)
            ds = dp * amat
            dqs.append((r["w_inter"] * _dot(g256, c_prev, 1, 1) + _dot(ds, r["k"], 1, 0)) * (DK_M ** -0.5))
            dks.append(_dot(ds, r["q"], 0, 0) + ea * _dot(v1, dc_h, 1, 1))
            dv_st = ea * _dot(r["k"], dc_h, 1, 0)
            dv1 = _dot(r["p"], g256, 0, 0) + dv_st
            dvs.append(dv1[:, :DV_M])
            wmat = dp * r["p"]
            c_in = _colsum(wmat)
            c_st = jnp.sum(v1.astype(F32) * dv_st, axis=-1, keepdims=True)
            r_t = jnp.sum(wmat, axis=1, keepdims=True) + jnp.sum(g256 * r["qc_w"], axis=-1, keepdims=True)
            db = r_t - c_st
            carry = jnp.exp(mnx_ref[h:h + 1, 0:1]) * jnp.sum(
                jnp.sum(dc_h * cnx_ref[h], axis=1, keepdims=True), axis=0, keepdims=True)
            db_all = db_all + jnp.where(lane == NH_M + h, db, 0.0)
            sig_c = sig_c + jnp.where(lane == NH_M + h, _sigmoid(-r["fp_c"]), 0.0)
            carries = carries + jnp.where(lane[0:1, :] == NH_M + h, carry, 0.0)
            dzs = dzs + jnp.where(lane == h, c_st, 0.0)
            dzr_rows[h] = c_in
            dzr_rows[NH_M + h] = _sigmoid(-r["fp_r"])
            wq = r["q"] * jnp.exp(r["b_c"] - m_t)
            dc_scr[h] = jnp.exp(r["g"]) * dc_h + _dot(wq, g256, 0, 0)
        dzs = dzs + (_tri_dot(upper, db_all, True) + carries) * sig_c
        c_in4 = jnp.concatenate(dzr_rows[:NH_M], axis=0)
        dlf_r4 = -_tri_dot(c_in4, lower, False)
        dzr_rows = dzr_rows[:NH_M] + [dlf_r4[h:h + 1, :] * dzr_rows[NH_M + h] for h in range(NH_M)]
        dqk_ref[...] = jnp.concatenate(dqs + dks, axis=1)
        dv_ref[...] = jnp.concatenate(dvs, axis=1).astype(dv_ref.dtype)
        do_ref[...] = jnp.concatenate(dos, axis=1).astype(do_ref.dtype)
        dzs_ref[...] = dzs
        dzr_ref[...] = jnp.concatenate(dzr_rows, axis=0)
        dgm_ref[...] += jnp.concatenate(dgs, axis=1)

    rev = lambda i: nc - 1 - i
    nxt = lambda i: jnp.minimum(nc - i, nc - 1)
    return pl.pallas_call(
        body, name="mlstm_bwd", grid=(nc,),
        in_specs=[pl.BlockSpec((l, NH_M * DK_M), lambda i: (rev(i), 0)), pl.BlockSpec((l, NH_M * DK_M), lambda i: (rev(i), 1)),
                  pl.BlockSpec((l, dm), lambda i: (rev(i), 1)), pl.BlockSpec((l, dm), lambda i: (rev(i), 2)),
                  pl.BlockSpec((l, 128), lambda i: (rev(i), 0)), pl.BlockSpec((8, l), lambda i: (0, rev(i))),
                  pl.BlockSpec((1, 8), lambda i: (0, 0)), pl.BlockSpec((8, 1), lambda i: (0, 0)),
                  pl.BlockSpec((1, dm), lambda i: (0, 0)),
                  pl.BlockSpec((None, NH_M, DK_M, 2 * DV_M), lambda i: (rev(i), 0, 0, 0)),
                  pl.BlockSpec((None, 8, 128), lambda i: (rev(i), 0, 0)),
                  pl.BlockSpec((None, NH_M, DK_M, 2 * DV_M), lambda i: (nxt(i), 0, 0, 0)),
                  pl.BlockSpec((None, 8, 128), lambda i: (nxt(i), 0, 0)),
                  pl.BlockSpec((l, dm), lambda i: (rev(i), 0))],
        out_specs=[pl.BlockSpec((l, dm), lambda i: (rev(i), 0)),
                   pl.BlockSpec((l, dm), lambda i: (rev(i), 0)), pl.BlockSpec((l, dm), lambda i: (rev(i), 0)),
                   pl.BlockSpec((l, 128), lambda i: (rev(i), 0)), pl.BlockSpec((8, l), lambda i: (0, rev(i))),
                   pl.BlockSpec((1, dm), lambda i: (0, 0))],
        out_shape=[jax.ShapeDtypeStruct((t, dm), F32),
                   jax.ShapeDtypeStruct((t, dm), BF16), jax.ShapeDtypeStruct((t, dm), BF16),
                   jax.ShapeDtypeStruct((t, 128), F32), jax.ShapeDtypeStruct((8, t), F32),
                   jax.ShapeDtypeStruct((1, dm), F32)],
        scratch_shapes=[pltpu.VMEM((NH_M, DK_M, 2 * DV_M), F32)],
        compiler_params=_cparams(("arbitrary",)),
    )(qk, qk, zbig, zbig, zs, zsr, bc, br, gm, cst, mst, cst, mst, dycat)


def _fox_cumsum(zsr, bf_r):
    t = zsr.shape[1]
    cw = _pick(t, (512, 256))

    def body(z_ref, b_ref, c_ref):
        up = _tri(cw, False).astype(F32)
        carry = jnp.zeros((NH_F, 1), F32)
        for j in range(t // cw):
            cs = _f32dot(_logsig(z_ref[:, j * cw:(j + 1) * cw] + b_ref[...]), up) + carry
            c_ref[:, j * cw:(j + 1) * cw] = cs
            carry = cs[:, cw - 1:cw]

    return pl.pallas_call(
        body, name="fox_cumsum", grid=(1,),
        in_specs=[pl.BlockSpec((NH_F, t), lambda i: (1, 0)), pl.BlockSpec((NH_F, 1), lambda i: (0, 0))],
        out_specs=pl.BlockSpec((NH_F, t), lambda i: (0, 0)), out_shape=jax.ShapeDtypeStruct((NH_F, t), F32),
        compiler_params=_cparams(("arbitrary",)),
    )(zsr, bf_r)


def _fox_gate_bwd(zsr, bf_r, dc):
    t = zsr.shape[1]
    cw = _pick(t, (512, 256))

    def body(z_ref, b_ref, dc_ref, o_ref):
        low = _tri(cw, True).astype(F32)
        carry = jnp.zeros((NH_F, 1), F32)
        for j in reversed(range(t // cw)):
            sl = slice(j * cw, (j + 1) * cw)
            dlf = _f32dot(dc_ref[:, sl], low) + carry
            o_ref[:, sl] = dlf * _sigmoid(-(z_ref[:, sl] + b_ref[...]))
            carry = dlf[:, 0:1]

    return pl.pallas_call(
        body, name="fox_gate_bwd", grid=(1,),
        in_specs=[pl.BlockSpec((NH_F, t), lambda i: (1, 0)), pl.BlockSpec((NH_F, 1), lambda i: (0, 0)),
                  pl.BlockSpec((NH_F, t), lambda i: (0, 0))],
        out_specs=pl.BlockSpec((NH_F, t), lambda i: (0, 0)), out_shape=jax.ShapeDtypeStruct((NH_F, t), F32),
        compiler_params=_cparams(("arbitrary",)),
    )(zsr, bf_r, dc)


def _causal_mask(n):
    return _tri(n, True)


AUG = 64


def _split3(c):
    hi = c.astype(BF16).astype(F32)
    r1 = c - hi
    mid = r1.astype(BF16).astype(F32)
    return hi, mid, r1 - mid


def _fox_prep(zbig, ct):
    t = zbig.shape[0]
    tm = _pick(t, (512, 256))

    def body(q_ref, k_ref, v_ref, c_ref, qo_ref, ko_ref, vo_ref):
        lane = lax.broadcasted_iota(jnp.int32, (tm, AUG), 1)
        qv, kv, vv, cv = q_ref[...], k_ref[...], v_ref[...], c_ref[...]
        one = (lane == 0).astype(BF16)
        for h in range(NH_F):
            hi, mid, lo = _split3(cv[:, h:h + 1])
            aq = jnp.where(lane == 0, hi, jnp.where(lane == 1, mid, jnp.where(lane == 2, lo, jnp.where(lane < 6, 1.0, 0.0))))
            ak = jnp.where(lane < 3, 1.0, jnp.where(lane == 3, -hi, jnp.where(lane == 4, -mid, jnp.where(lane == 5, -lo, 0.0))))
            sl = slice(h * DH_F, (h + 1) * DH_F)
            qo_ref[h] = jnp.concatenate([qv[:, sl] * (DH_F ** -0.5), aq.astype(BF16)], axis=1).astype(BF16)
            ko_ref[h] = jnp.concatenate([kv[:, sl], ak.astype(BF16)], axis=1)
            vo_ref[h] = jnp.concatenate([vv[:, sl], one], axis=1)

    ospec = pl.BlockSpec((NH_F, tm, 128), lambda i: (0, i, 0))
    return pl.pallas_call(
        body, name="fox_prep", grid=(t // tm,),
        in_specs=[pl.BlockSpec((tm, 512), lambda i: (i, 3)), pl.BlockSpec((tm, 512), lambda i: (i, 4)),
                  pl.BlockSpec((tm, 512), lambda i: (i, 5)), pl.BlockSpec((tm, NH_F), lambda i: (i, 0))],
        out_specs=[ospec] * 3, out_shape=[jax.ShapeDtypeStruct((NH_F, t, 128), BF16)] * 3,
        compiler_params=_cparams(("parallel",)),
    )(zbig, zbig, zbig, ct)


def _fox_fwd2(qa, ka, va, gf, plan=None):
    nh, t, _ = qa.shape
    tq = _pick(t, (512, 256))
    nq = t // tq
    group = 4
    host = _Hosted(plan, 4, 3)

    def body(*refs):
        (q_ref, k_ref, v_ref, g_ref), (y_ref, o_ref, lse_ref), _, prefs = host.split(refs)
        i = pl.program_id(0)
        host.run(0, i == 0, prefs)
        host.run(1, i == max(nq - 2, 0), prefs)
        lane = lax.broadcasted_iota(jnp.int32, (tq, 128), 1)
        causal = _causal_mask(tq)
        ys, os_ = [], []
        lse_all = jnp.zeros((tq, 128), F32)
        for h0 in range(0, nh, group):
            heads = range(h0, h0 + group)
            qvs = [q_ref[h] for h in heads]

            def blk(j, carry, masked, heads=heads, qvs=qvs):
                k0 = pl.multiple_of(j * tq, tq)
                out = []
                for (m, acc), h, qv in zip(carry, heads, qvs):
                    s = lax.dot_general(qv, k_ref[h, pl.ds(k0, tq), :], (((1,), (1,)), ((), ())), preferred_element_type=F32)
                    if masked:
                        s = jnp.where(causal, s, -jnp.inf)
                    m_new = jnp.maximum(m, jnp.max(s, axis=1, keepdims=True))
                    p = jnp.exp(s - m_new).astype(BF16)
                    pv = lax.dot_general(p, v_ref[h, pl.ds(k0, tq), :], (((1,), (0,)), ((), ())), preferred_element_type=F32)
                    out.append((m_new, jnp.exp(m - m_new) * acc + pv))
                return tuple(out)

            init = tuple((jnp.full((tq, 1), -jnp.inf, F32), jnp.zeros((tq, 128), F32)) for _ in heads)
            carry = lax.fori_loop(0, i, lambda j, c: blk(j, c, False), init)
            for (m, acc), h in zip(blk(i, carry, True), heads):
                l = acc[:, DH_F:DH_F + 1]
                o = acc[:, :DH_F] / l
                os_.append(o)
                gh = g_ref[:, h * DH_F:(h + 1) * DH_F]
                ys.append(o * lax.rsqrt(jnp.mean(o * o, axis=-1, keepdims=True) + EPS) * gh)
                lse_all = lse_all + jnp.where(lane == h, m + jnp.log(l), 0.0)
        y_ref[...] = jnp.concatenate(ys, axis=1).astype(y_ref.dtype)
        o_ref[...] = jnp.concatenate(os_, axis=1)
        lse_ref[...] = lse_all
        host.run(2, i == nq - 1, prefs)

    full = pl.BlockSpec((nh, t, 128), lambda i: (0, 0, 0))
    hc = host.call_args()
    res = pl.pallas_call(
        body, name="fox_fwd", grid=(nq,),
        in_specs=[pl.BlockSpec((nh, tq, 128), lambda i: (0, i, 0)), full, full, pl.BlockSpec((1, nh * DH_F), lambda i: (0, 0))]
        + hc["in_specs"],
        out_specs=[pl.BlockSpec((tq, nh * DH_F), lambda i: (i, 0)), pl.BlockSpec((tq, nh * DH_F), lambda i: (i, 0)),
                   pl.BlockSpec((tq, 128), lambda i: (i, 0))] + hc["out_specs"],
        out_shape=[jax.ShapeDtypeStruct((t, nh * DH_F), BF16), jax.ShapeDtypeStruct((t, nh * DH_F), F32),
                   jax.ShapeDtypeStruct((t, 128), F32)] + hc["out_shape"],
        scratch_shapes=hc["scratch"], input_output_aliases=hc["aliases"], compiler_params=_cparams(("arbitrary",)),
    )(qa, ka, va, gf, *hc["args"])
    return res[:3], res[3:]


def _fox_bwd_prep(dycat, o, gf):
    t = o.shape[0]
    tm = _pick(t, (512, 256))

    def body(dy_ref, o_ref, g_ref, do_ref, dl_ref, dg_ref):
        lane = lax.broadcasted_iota(jnp.int32, (tm, 128), 1)
        dyv, ov, gv = dy_ref[...], o_ref[...], g_ref[...]
        dgs = []
        dl = jnp.zeros((tm, 128), F32)
        pad = jnp.zeros((tm, AUG), BF16)
        for h in range(NH_F):
            sl = slice(h * DH_F, (h + 1) * DH_F)
            dx, dg = _rms_bwd_val(dyv[:, sl], ov[:, sl], gv[:, sl])
            dgs.append(dg)
            do_ref[h] = jnp.concatenate([dx.astype(BF16), pad], axis=1)
            dl = dl + jnp.where(lane == h, jnp.sum(dx * ov[:, sl], axis=-1, keepdims=True), 0.0)
        dl_ref[...] = dl

        @pl.when(pl.program_id(0) == 0)
        def _():
            dg_ref[...] = jnp.zeros_like(dg_ref)
        dg_ref[...] += jnp.concatenate(dgs, axis=1)

    return pl.pallas_call(
        body, name="fox_bwd_prep", grid=(t // tm,),
        in_specs=[pl.BlockSpec((tm, 512), lambda i: (i, 1)), pl.BlockSpec((tm, 512), lambda i: (i, 0)),
                  pl.BlockSpec((1, 512), lambda i: (0, 0))],
        out_specs=[pl.BlockSpec((NH_F, tm, 128), lambda i: (0, i, 0)), pl.BlockSpec((tm, 128), lambda i: (i, 0)),
                   pl.BlockSpec((1, 512), lambda i: (0, 0))],
        out_shape=[jax.ShapeDtypeStruct((NH_F, t, 128), BF16), jax.ShapeDtypeStruct((t, 128), F32),
                   jax.ShapeDtypeStruct((1, 512), F32)],
        compiler_params=_cparams(("arbitrary",)),
    )(dycat, o, gf)


def _fox_bwd2(qa, ka, va, doa, lse, delta, plan=None):
    nh, t, _ = qa.shape
    tq = _pick(t, (512, 256))
    nq = t // tq

    group = 2

    def tdot(a, b, cb):
        return lax.dot_general(a, b, (((0,), (cb,)), ((), ())), preferred_element_type=F32)

    host = _Hosted(plan, 6, 3)
    ng = nh // group

    def body(*refs):
        (q_ref, k_ref, v_ref, do_ref, lse_ref, dl_ref), (dq_ref, dk_ref, dv_ref), _, prefs = host.split(refs)
        hp, j = pl.program_id(0), pl.program_id(1)
        host.run(0, (hp == 0) & (j == 0), prefs)
        host.run(1, (hp == 0) & (j == 0), prefs)

        @pl.when(j == 0)
        def _():
            dq_ref[...] = jnp.zeros_like(dq_ref)

        lane = lax.broadcasted_iota(jnp.int32, (tq, 128), 1)
        causal = _causal_mask(tq)

        def blk(i, carry, masked):
            rows = pl.ds(pl.multiple_of(i * tq, tq), tq)
            lse_t, dl_t = lse_ref[rows, :], dl_ref[rows, :]
            out = []
            for g, (dk, dv) in enumerate(carry):
                h = hp * group + g
                kb, vb = k_ref[g], v_ref[g]
                qb, dob = q_ref[g, rows, :], do_ref[g, rows, :]
                lse_h = jnp.sum(jnp.where(lane == h, lse_t, 0.0), axis=1, keepdims=True)
                dl_h = jnp.sum(jnp.where(lane == h, dl_t, 0.0), axis=1, keepdims=True)
                s = lax.dot_general(qb, kb, (((1,), (1,)), ((), ())), preferred_element_type=F32)
                if masked:
                    s = jnp.where(causal, s, -jnp.inf)
                p = jnp.exp(s - lse_h)
                dp = lax.dot_general(dob, vb, (((1,), (1,)), ((), ())), preferred_element_type=F32)
                ds = (p * (dp - dl_h)).astype(BF16)
                dv = dv + tdot(dob, p.astype(BF16), 0)
                dk = dk + tdot(qb, ds, 0)
                dq_ref[g, :, rows] += tdot(kb, ds, 1)
                out.append((dk, dv))
            return tuple(out)

        init = tuple((jnp.zeros((128, tq), F32), jnp.zeros((128, tq), F32)) for _ in range(group))
        carry = blk(j, init, True)
        carry = lax.fori_loop(j + 1, nq, lambda i, c: blk(i, c, False), carry)
        for g, (dk, dv) in enumerate(carry):
            dk_ref[g] = dk
            dv_ref[g] = dv
        host.run(2, (hp == ng - 1) & (j == nq - 1), prefs)

    full = pl.BlockSpec((group, t, 128), lambda h, j: (h, 0, 0))
    tile = pl.BlockSpec((group, tq, 128), lambda h, j: (h, j, 0))
    cols = pl.BlockSpec((t, 128), lambda h, j: (0, 0))
    full_t = pl.BlockSpec((group, 128, t), lambda h, j: (h, 0, 0))
    tile_t = pl.BlockSpec((group, 128, tq), lambda h, j: (h, 0, j))
    hc = host.call_args()
    res = pl.pallas_call(
        body, name="fox_bwd", grid=(ng, nq), in_specs=[full, tile, tile, full, cols, cols] + hc["in_specs"],
        out_specs=[full_t, tile_t, tile_t] + hc["out_specs"],
        out_shape=[jax.ShapeDtypeStruct((nh, 128, t), F32)] * 3 + hc["out_shape"], scratch_shapes=hc["scratch"],
        input_output_aliases=hc["aliases"], compiler_params=_cparams(("arbitrary", "arbitrary")),
    )(qa, ka, va, doa, lse, delta, *hc["args"])
    return res[:3], res[3:]


def _fox_bwd_post(dqa, dka, dva):
    nh, _, t = dqa.shape
    tm = _pick(t, (512, 256))

    def body(dq_ref, dk_ref, dv_ref, oq_ref, ok_ref, ov_ref, dc_ref):
        qs, ks, vs, dcs = [], [], [], []
        for h in range(nh):
            dq, dk = dq_ref[h], dk_ref[h]
            qs.append(dq.T[:, :DH_F] * (DH_F ** -0.5))
            ks.append(dk.T[:, :DH_F])
            vs.append(dv_ref[h].T[:, :DH_F])
            dcs.append(dq[DH_F:DH_F + 1, :] - dk[DH_F + 3:DH_F + 4, :])
        oq_ref[...] = jnp.concatenate(qs, axis=1).astype(BF16)
        ok_ref[...] = jnp.concatenate(ks, axis=1).astype(BF16)
        ov_ref[...] = jnp.concatenate(vs, axis=1).astype(BF16)
        dc_ref[...] = jnp.concatenate(dcs, axis=0)

    ispec = pl.BlockSpec((nh, 128, tm), lambda i: (0, 0, i))
    ospec = pl.BlockSpec((tm, nh * DH_F), lambda i: (i, 0))
    return pl.pallas_call(
        body, name="fox_bwd_post", grid=(t // tm,), in_specs=[ispec] * 3,
        out_specs=[ospec] * 3 + [pl.BlockSpec((nh, tm), lambda i: (0, i))],
        out_shape=[jax.ShapeDtypeStruct((t, nh * DH_F), BF16)] * 3 + [jax.ShapeDtypeStruct((nh, t), F32)],
        compiler_params=_cparams(("parallel",)),
    )(dqa, dka, dva)


IN_OFF = (0, 512, 1024, 1544, 2056, 2568)
IN_GATES = (1536, 3080)


FFN1 = ("ffn1_w_gate", "ffn1_w_up", "ffn1_w_down")
REST = ("w_in", "w_out", "ffn2_w_gate", "ffn2_w_up", "ffn2_w_down", "w_ple_gate", "w_ple_proj")
SPLIT = {n: 1 if n == "w_in" else 0 for n in FFN1 + REST}
SAME_SHAPE = (FFN1, ("ffn2_w_gate", "ffn2_w_up", "ffn2_w_down"), ("w_out", "w_ple_gate"), ("w_in",), ("w_ple_proj",))


def _grouped(names):
    return [tuple(n for n in grp if n in names) for grp in SAME_SHAPE if any(n in names for n in grp)]


def _rs_partials(names, gw, c_idx, twins, run_swap=None):
    wire = [twins[n] if n in twins else _cast_other_half("rs_cast_" + n, gw[n], c_idx, SPLIT[n]) for n in names]
    plan = _swap_plan(wire, [SPLIT[n] if n in twins else None for n in names])
    swapped = dict(zip(names, run_swap(plan) if run_swap else _run_plan("rs_swap_" + names[0], plan)))
    out = {}
    for grp in _grouped(names):
        res = _add_my_half("rs_add_" + grp[0], [gw[n] for n in grp], [swapped[n] for n in grp], c_idx, SPLIT[grp[0]])
        out.update(zip(grp, res))
    return [out[n] for n in names]


def _local_step(x, p, tgt, sp, wg1, wu1, wd1, rest_slots, c_idx, place):
    t, d = x.shape
    slot = dict(zip(REST + ("conv_qk",), rest_slots))
    (h1, xn1, g1, u1), (w_in, conv_w) = _ffn_fwd(
        "ffn1", x, sp["ffn1_norm"], wg1, wu1, wd1, plan=_gather_plan([slot["w_in"], slot["conv_qk"]], [SPLIT["w_in"], None]))
    w_in, conv_w = w_in.reshape(-1, d), _from_chip_blocks(conv_w)
    w_big = jnp.concatenate([w_in[o:o + 512] for o in IN_OFF], axis=0)
    w_small = jnp.concatenate([w_in[IN_GATES[0]:IN_GATES[0] + 8], w_in[IN_GATES[1]:IN_GATES[1] + 8],
                               jnp.zeros((112, d), w_in.dtype)], axis=0)
    u, zbig = _norm_mm("in_big", h1, sp["mix_norm"], w_big, True, BF16)
    zs = _mm_nt("in_small", u, w_small, tm=1024, tk=1024)
    zsr = zs.T
    qk_act = _conv_fwd(zbig, conv_w)
    bm_c, bf_c = sp["b_mlstm_gates"], sp["b_fox_f"]
    y_m, cst, mst = _mlstm_fwd(qk_act, zbig, zs, zsr, bm_c, bm_c.T, sp["mlstm_out_norm"])
    c = _fox_cumsum(zsr, bf_c.T)
    qa, ka, va = _fox_prep(zbig, c.T)
    (y_ft, o_f, lse), late = _fox_fwd2(qa, ka, va, sp["fox_out_norm"],
                                       plan=_gather_plan([slot[n] for n in REST[1:]], [SPLIT[n] for n in REST[1:]]))
    full = dict(zip(REST[1:], late))
    w_out, w_pg = (full[n].reshape(-1, d) for n in ("w_out", "w_ple_gate"))
    wg2, wu2, wd2 = full["ffn2_w_gate"], full["ffn2_w_up"], full["ffn2_w_down"]
    w_pp = _from_chip_blocks(full["w_ple_proj"])
    tm = _pick(t, (1024, 512, 256))
    h2 = _mm("out_proj", [(y_m, (tm, 512), lambda i, j, k: (i, 0), w_out, (512, d), lambda i, j, k: (0, 0)),
                          (y_ft, (tm, 512), lambda i, j, k: (i, 0), w_out, (512, d), lambda i, j, k: (1, 0))],
             (t, d), (tm, d), lambda i, j, k: (i, 0), (t // tm, 1, 1), 2, res=h1)
    (h3, xn2, g2, u2), _ = _ffn_fwd("ffn2", h2, sp["ffn2_norm"], wg2, wu2, wd2)
    hn3, gate_pre = _norm_mm("ple_gate", h3, sp["ple_gate_norm"], w_pg, False, F32)
    pp = _mm_nn("ple_proj", p, w_pp, tm=1024)

    def head_fn(h3_t, gp_t, pp_t, tgt_t, g_pp, g_fin):
        gate = _sigmoid(gp_t)
        ppn = _rms_fwd_val(pp_t, g_pp)
        h4 = h3_t + gate * ppn
        err = _rms_fwd_val(h4, g_fin) - tgt_t
        loss = 0.5 * jnp.sum(jnp.mean(err * err, axis=-1, keepdims=True), axis=0, keepdims=True)
        dh4, dg_fin = _rms_bwd_val(err * (1.0 / d), h4, g_fin)
        dpp, dg_pp = _rms_bwd_val(dh4 * gate, pp_t, g_pp)
        dgp = dh4 * ppn * gate * (1.0 - gate)
        return dh4, dgp, dpp, jnp.broadcast_to(loss, (1, 128)), dg_fin, dg_pp

    dh4, dgp, dpp, loss_part, dg_fin, dg_pp = _rowwise(
        "loss_head", head_fn, [h3, gate_pre, pp, tgt], [sp["ple_proj_norm"], sp["final_norm"]],
        [(d, F32), (d, BF16), (d, BF16)], [((1, 128), F32), ((1, d), F32), ((1, d), F32)])
    gw, gs = {}, {"final_norm": dg_fin, "ple_proj_norm": dg_pp}
    gw["w_ple_gate"] = _mm_tn("d_w_pg", hn3, dgp, tm=1024, tn=1024)
    gw["w_ple_proj"] = _mm_tn("d_w_pp", p, dpp, tn=1024)
    dhn3 = _mm_nt("d_hn3", dgp, w_pg, tm=1024, tn=1024, tk=1024)

    def res_norm_bwd(dn_t, h_t, dres_t, g):
        dx, dg = _rms_bwd_val(dn_t, h_t, g)
        return dres_t + dx, dg

    dh3, gs["ple_gate_norm"] = _rowwise("ple_norm_bwd", res_norm_bwd, [dhn3, h3, dh4], [sp["ple_gate_norm"]],
                                        [(d, F32)], [((1, d), F32)])
    (dh2, gs["ffn2_norm"], gw["ffn2_w_gate"], gw["ffn2_w_up"], gw["ffn2_w_down"]), _, twins2 = _ffn_bwd(
        "ffn2", dh3, h2, sp["ffn2_norm"], xn2, g2, u2, wg2, wu2, wd2)
    ffn2_names = ("ffn2_w_gate", "ffn2_w_up", "ffn2_w_down")
    early = []

    def swap_in_d_ycat(plan):
        dyc, swapped = _mm_nt("d_ycat", dh2, w_out, tm=1024, tn=1024, tk=1024, plan=plan)
        early.append(dyc)
        return swapped

    part_ffn2 = dict(zip(ffn2_names, _rs_partials(ffn2_names, gw, c_idx, dict(zip(ffn2_names, twins2)), swap_in_d_ycat)))
    dycat = early[0]
    gw["w_out"] = jnp.concatenate([_mm_tn("d_w_out_m", y_m, dh2, tn=1024, tk=2048),
                                   _mm_tn("d_w_out_f", y_ft, dh2, tn=1024, tk=2048)], axis=0)
    doa, delta, gs["fox_out_norm"] = _fox_bwd_prep(dycat, o_f, sp["fox_out_norm"])
    dqkv_t, landed_ffn2 = _fox_bwd2(qa, ka, va, doa, lse, delta, plan=_scatter_plan([part_ffn2[n][1] for n in ffn2_names]))
    dq_f, dk_f, dv_f, dct = _fox_bwd_post(*dqkv_t)
    dfp = _fox_gate_bwd(zsr, bf_c.T, dct)
    dact, dv_m, do_m, dzs_m, dzr_m, gs["mlstm_out_norm"] = _mlstm_bwd(
        qk_act, zbig, zs, zsr, bm_c, bm_c.T, sp["mlstm_out_norm"], cst, mst, dycat)
    dqk, gw["conv_qk"] = _conv_bwd(zbig, dact, conv_w)
    dz_big = jnp.concatenate([dqk, dv_m, do_m, dq_f, dk_f, dv_f], axis=1)
    dzs = dzs_m + jnp.pad(jnp.concatenate([dzr_m, dfp], axis=0).T, ((0, 0), (0, 112)))
    dw_big = _mm_tn("d_w_big", dz_big, u, tn=1024)
    dw_small = _mm_tn("d_w_small", dzs, u, tn=1024)
    gw["w_in"] = jnp.concatenate([dw_big[0:1536], dw_small[0:8], dw_big[1536:3072], dw_small[8:16]], axis=0)
    du_a = _mm_nn("d_u_big", dz_big, w_big, tm=1024, tn=1024, tk=1024)
    du_b = _mm_nn("d_u_small", dzs, w_small, tm=1024, tn=1024)

    def mix_norm_bwd(da_t, db_t, h_t, dres_t, dzs_t, g):
        dx, dg = _rms_bwd_val(da_t + db_t, h_t, g)
        return dres_t + dx, dg, _colsum(dzs_t)

    conv_grad = gw.pop("conv_qk")
    gw["w_ple_proj"] = _chip_blocks(gw["w_ple_proj"])
    for n in ("w_in", "w_out", "w_ple_gate"):
        gw[n] = gw[n].reshape(4, -1, gw[n].shape[-1])
    mix = []

    def swap_in_mix_norm_bwd(plan):
        res, swapped = _rowwise("mix_norm_bwd", mix_norm_bwd, [du_a, du_b, h1, dh2, dzs], [sp["mix_norm"]],
                                [(d, F32)], [((1, d), F32), ((1, 128), F32)], plan=plan)
        mix.extend(res)
        return swapped

    light = ("w_in", "w_out", "w_ple_gate", "w_ple_proj")
    part_light = dict(zip(light, _rs_partials(light, gw, c_idx, {}, swap_in_mix_norm_bwd)))
    dh1, gs["mix_norm"], dbias = mix
    gs["b_mlstm_gates"], gs["b_fox_f"] = dbias[:, 0:8], dbias[:, 8:16]
    part_ffn1 = []

    def own_plan(dws, dw_twins):
        part_ffn1.extend(_rs_partials(FFN1, dict(zip(FFN1, dws)), c_idx, dict(zip(FFN1, dw_twins))))
        return _scatter_plan([pb for _, pb in part_ffn1])

    (grad_x, gs["ffn1_norm"], _, _, _), (l_small, l_in, _, _, landed_ffn1) = _ffn_bwd_late_dx(
        "ffn1", dh1, x, sp["ffn1_norm"], xn1, g1, u1, wg1, wu1, wd1,
        _scatter_plan([part_light[n][1] for n in light[1:]]), [_scatter_plan([part_light["w_in"][1]]), None, None], own_plan)
    names = REST + FFN1
    parts = {**part_light, **part_ffn2, **dict(zip(FFN1, part_ffn1))}
    landed = {"w_in": l_in[0], **dict(zip(light[1:], l_small)), **dict(zip(ffn2_names, landed_ffn2)),
              **dict(zip(FFN1, landed_ffn1))}
    mine = {}
    for grp in _grouped(names):
        res = _sum4("rs_sum_" + grp[0], [landed[n] for n in grp], [parts[n][0] for n in grp], place, SPLIT[grp[0]])
        mine.update(zip(grp, res))
    grads = dict(zip(names, _join_halves("rs_join", [mine[n] for n in names], [SPLIT[n] for n in names])))
    return loss_part, grad_x, grads, gs, conv_grad


ANY = pl.BlockSpec(memory_space=pl.ANY)
MESH = pl.DeviceIdType.MESH


def _place():
    x, y, c = lax.axis_index("x"), lax.axis_index("y"), lax.axis_index("c")
    chips = [(1 - x, y), (x, 1 - y), (1 - x, 1 - y)]
    return x, y, c, 2 * x + y, (x, y, 1 - c), chips


def _rcopy(src, dst, ssem, rsem, dev):
    return pltpu.make_async_remote_copy(src_ref=src, dst_ref=dst, send_sem=ssem, recv_sem=rsem, device_id=dev,
                                        device_id_type=MESH)


def _half(ref, lead, axis, idx, half):
    return ref.at[(slice(None),) * (lead + axis) + (pl.ds(idx * half, half),)]


def _to_slot(name, arrs, me_idx, dtype):
    n = len(arrs)
    r, cdim = arrs[0].shape
    tr = _pick(r, (352, 256, 176, 128, 64))

    def body(me_ref, *refs):
        for k in range(n):
            refs[n + k][...] = refs[k][...].astype(dtype)

    return pl.pallas_call(
        body, name=name,
        grid_spec=pltpu.PrefetchScalarGridSpec(
            num_scalar_prefetch=1, grid=(r // tr,), in_specs=[pl.BlockSpec((tr, cdim), lambda i, me_ref: (i, 0))] * n,
            out_specs=[pl.BlockSpec((None, tr, cdim), lambda i, me_ref: (me_ref[0], i, 0))] * n),
        out_shape=[jax.ShapeDtypeStruct((4, r, cdim), dtype)] * n, compiler_params=_cparams(("parallel",)),
    )(me_idx, *arrs)


def _gather4(name, bufs, split):
    return _run_plan(name, _gather_plan(bufs, split))


def _gather_plan(bufs, split):
    n = len(bufs)
    shapes = [b.shape[1:] for b in bufs]

    def ctx(outs):
        x, y, c, me, sib, chips = _place()

        def part(ref, a, which):
            if split[a] is None:
                return ref
            return _half(ref, 0, split[a], which, shapes[a][split[a]] // 2)

        return c, me, sib, chips, part

    def ici(outs, sems, a, j, chip, c, me, part):
        mine = part(outs[a].at[me], a, c)
        return _rcopy(mine, mine, sems[0].at[3 * a + j], sems[1].at[3 * a + j], (*chip, c))

    def fwd(outs, sems, a, j, chip, c, sib, part, which):
        blk = part(outs[a].at[2 * chip[0] + chip[1]], a, which)
        return _rcopy(blk, blk, sems[2].at[3 * a + j], sems[3].at[3 * a + j], sib)

    def start(ins, outs, sems):
        c, me, sib, chips, part = ctx(outs)
        for a in range(n):
            for j, chip in enumerate(chips):
                ici(outs, sems, a, j, chip, c, me, part).start()

    def mid(ins, outs, sems):
        c, me, sib, chips, part = ctx(outs)
        for j, chip in enumerate(chips):
            for a in range(n):
                blk = part(outs[a].at[2 * chip[0] + chip[1]], a, c)
                _rcopy(blk, blk, sems[0].at[3 * a + j], sems[1].at[3 * a + j], sib).wait_recv()
                if split[a] is not None:
                    fwd(outs, sems, a, j, chip, c, sib, part, c).start()

    def end(ins, outs, sems):
        c, me, sib, chips, part = ctx(outs)
        for j, chip in enumerate(chips):
            for a in range(n):
                if split[a] is not None:
                    fwd(outs, sems, a, j, chip, c, sib, part, 1 - c).wait_recv()
        for a in range(n):
            for j, chip in enumerate(chips):
                ici(outs, sems, a, j, chip, c, me, part).wait_send()
                if split[a] is not None:
                    fwd(outs, sems, a, j, chip, c, sib, part, c).wait_send()

    return dict(ins=list(bufs), outs=[jax.ShapeDtypeStruct(b.shape, b.dtype) for b in bufs], alias=True,
                sems=[pltpu.SemaphoreType.DMA((3 * n,))] * 4, phases=(start, mid, end))


def _run_plan(name, plan):
    ni, no = len(plan["ins"]), len(plan["outs"])

    def body(*refs):
        ins, outs, sems = refs[:ni], refs[ni:ni + no], refs[ni + no:]
        for phase in plan["phases"]:
            phase(ins, outs, sems)

    return pl.pallas_call(
        body, name=name, in_specs=[ANY] * ni, out_specs=[ANY] * no, out_shape=plan["outs"],
        input_output_aliases={a: a for a in range(ni)} if plan["alias"] else {}, scratch_shapes=plan["sems"],
    )(*plan["ins"])


class _Hosted:
    def __init__(self, plan, n_in, n_out):
        self.plan, self.n_in, self.n_out = plan, n_in, n_out
        self.ni, self.no, self.ns = (len(plan["ins"]) if plan else 0, len(plan["outs"]) if plan else 0,
                                     len(plan["sems"]) if plan else 0)

    def split(self, refs):
        a, b = self.n_in, self.n_in + self.ni
        c, d = b + self.n_out, b + self.n_out + self.no
        e = len(refs) - self.ns
        return refs[:a], refs[b:c], refs[d:e], (refs[a:b], refs[c:d], refs[e:])

    def run(self, k, cond, prefs):
        if self.plan is not None:
            @pl.when(cond)
            def _():
                self.plan["phases"][k](*prefs)

    def call_args(self):
        p = self.plan
        if p is None:
            return dict(in_specs=[], out_specs=[], out_shape=[], scratch=[], aliases={}, args=[])
        al = {self.n_in + a: self.n_out + a for a in range(self.ni)} if p["alias"] else {}
        return dict(in_specs=[ANY] * self.ni, out_specs=[ANY] * self.no, out_shape=list(p["outs"]), scratch=list(p["sems"]),
                    aliases=al, args=list(p["ins"]))


def _swap(name, arrs, halve):
    return _run_plan(name, _swap_plan(arrs, halve))


def _swap_plan(arrs, halve):
    n = len(arrs)

    def half_shape(a, ax):
        return a.shape if ax is None else (a.shape[0],) + tuple(d // 2 if i == ax else d for i, d in enumerate(a.shape[1:]))

    def copies(ins, outs, sems):
        x, y, c, me, sib, chips = _place()
        cps = []
        for a in range(n):
            src = ins[a] if halve[a] is None else _half(ins[a], 1, halve[a], 1 - c, arrs[a].shape[1 + halve[a]] // 2)
            cps.append(_rcopy(src, outs[a], sems[0].at[a], sems[1].at[a], sib))
        return cps

    def start(ins, outs, sems):
        for cp in copies(ins, outs, sems):
            cp.start()

    def mid(ins, outs, sems):
        pass

    def end(ins, outs, sems):
        for cp in copies(ins, outs, sems):
            cp.wait()

    return dict(ins=list(arrs), outs=[jax.ShapeDtypeStruct(half_shape(a, ax), a.dtype) for a, ax in zip(arrs, halve)],
                alias=False, sems=[pltpu.SemaphoreType.DMA((n,))] * 2, phases=(start, mid, end))


def _scatter4(name, arrs):
    return _run_plan(name, _scatter_plan(arrs))


def _scatter_plan(arrs):
    n = len(arrs)

    def send(ins, outs, sems, a, j, chip, c, me):
        return _rcopy(ins[a].at[2 * chip[0] + chip[1]], outs[a].at[me], sems[0].at[3 * a + j], sems[1].at[3 * a + j], (*chip, c))

    def start(ins, outs, sems):
        x, y, c, me, sib, chips = _place()
        for a in range(n):
            for j, chip in enumerate(chips):
                send(ins, outs, sems, a, j, chip, c, me).start()

    def mid(ins, outs, sems):
        pass

    def end(ins, outs, sems):
        x, y, c, me, sib, chips = _place()
        for a in range(n):
            for j, chip in enumerate(chips):
                blk = outs[a].at[2 * chip[0] + chip[1]]
                _rcopy(blk, blk, sems[0].at[3 * a + j], sems[1].at[3 * a + j], sib).wait_recv()
        for a in range(n):
            for j, chip in enumerate(chips):
                send(ins, outs, sems, a, j, chip, c, me).wait_send()

    return dict(ins=list(arrs), outs=[jax.ShapeDtypeStruct(a.shape, a.dtype) for a in arrs], alias=False,
                sems=[pltpu.SemaphoreType.DMA((3 * n,))] * 2, phases=(start, mid, end))


def _join_halves(name, arrs, split):
    n = len(arrs)

    def body(*refs):
        outs = refs[n:2 * n]
        ssem, rsem = refs[2 * n:]
        x, y, c, me, sib, chips = _place()
        cps = []
        for a in range(n):
            mine = _half(outs[a], 0, split[a], c, arrs[a].shape[split[a]] // 2)
            cp = _rcopy(mine, mine, ssem.at[a], rsem.at[a], sib)
            cp.start()
            cps.append(cp)
        for a in range(n):
            blk = _half(outs[a], 0, split[a], 1 - c, arrs[a].shape[split[a]] // 2)
            _rcopy(blk, blk, ssem.at[a], rsem.at[a], sib).wait_recv()
        for cp in cps:
            cp.wait_send()

    return pl.pallas_call(
        body, name=name, in_specs=[ANY] * n, out_specs=[ANY] * n,
        out_shape=[jax.ShapeDtypeStruct(a.shape, a.dtype) for a in arrs],
        input_output_aliases={a: a for a in range(n)}, scratch_shapes=[pltpu.SemaphoreType.DMA((n,))] * 2,
    )(*arrs)


def _allreduce_small(s):
    r, cdim = s.shape

    def body(s_ref, o_ref, buf, ssem, rsem):
        x, y, c, me, sib, chips = _place()
        me8 = 4 * x + 2 * y + c
        buf[me8] = s_ref[...]
        flips = [(fx, fy, fc) for fx in (0, 1) for fy in (0, 1) for fc in (0, 1)][1:]
        cps = []
        for k, (fx, fy, fc) in enumerate(flips):
            peer = (x ^ fx if fx else x, y ^ fy if fy else y, c ^ fc if fc else c)
            cp = _rcopy(s_ref, buf.at[me8], ssem.at[k], rsem.at[k], peer)
            cp.start()
            cps.append(cp)
        for k, (fx, fy, fc) in enumerate(flips):
            src = 4 * (x ^ fx if fx else x) + 2 * (y ^ fy if fy else y) + (c ^ fc if fc else c)
            _rcopy(s_ref, buf.at[src], ssem.at[k], rsem.at[k], sib).wait_recv()
        for cp in cps:
            cp.wait_send()
        acc = buf[0]
        for k in range(1, 8):
            acc = acc + buf[k]
        o_ref[...] = acc

    vm = pl.BlockSpec(memory_space=pltpu.VMEM)
    return pl.pallas_call(
        body, name="allreduce_small", in_specs=[vm], out_specs=vm, out_shape=jax.ShapeDtypeStruct((r, cdim), F32),
        scratch_shapes=[pltpu.VMEM((8, r, cdim), F32), pltpu.SemaphoreType.DMA((7,)), pltpu.SemaphoreType.DMA((7,))],
    )(s)


def _add_my_half(name, gs, recvs, c_idx, axis):
    n = len(gs)
    nb, hr, hc = recvs[0].shape
    tr = _pick(hr, (256, 176, 128, 64))
    if axis == 0:
        g4s = [g.reshape(nb, 2, hr, hc) for g in gs]
        gspec = pl.BlockSpec((None, None, tr, hc), lambda b, i, c_ref: (b, c_ref[0], i, 0))
    else:
        g4s = list(gs)
        gspec = pl.BlockSpec((None, tr, hc), lambda b, i, c_ref: (b, i, c_ref[0]))

    def body(c_ref, *refs):
        for k in range(n):
            s = refs[k][...] + refs[n + k][...].astype(F32)
            refs[2 * n + 2 * k][...] = s
            refs[2 * n + 2 * k + 1][...] = s.astype(BF16)

    ospec = pl.BlockSpec((None, tr, hc), lambda b, i, c_ref: (b, i, 0))
    res = pl.pallas_call(
        body, name=name,
        grid_spec=pltpu.PrefetchScalarGridSpec(
            num_scalar_prefetch=1, grid=(nb, hr // tr), in_specs=[gspec] * n + [ospec] * n, out_specs=[ospec] * (2 * n)),
        out_shape=[jax.ShapeDtypeStruct((nb, hr, hc), F32), jax.ShapeDtypeStruct((nb, hr, hc), BF16)] * n,
        compiler_params=_cparams(("parallel", "parallel")),
    )(c_idx, *g4s, *recvs)
    return [(res[2 * k], res[2 * k + 1]) for k in range(n)]


def _sum4(name, landeds, owns, place, axis):
    n = len(landeds)
    nb, h, cdim = landeds[0].shape
    tr = _pick(h, (256, 176, 128, 64))
    nt = h // tr

    def body(p_ref, *refs):
        for k in range(n):
            a1, a2, a3, own = refs[4 * k:4 * k + 4]
            refs[4 * n + k][...] = ((own[...] + a1[...].astype(F32)) + a2[...].astype(F32)) + a3[...].astype(F32)

    def nxt(k):
        return pl.BlockSpec((None, tr, cdim), lambda i, p_ref: ((p_ref[0] + k) % nb, i, 0))

    if axis == 0:
        ospec = pl.BlockSpec((tr, cdim), lambda i, p_ref: (p_ref[1] * nt + i, 0))
        oshape = (2 * h, cdim)
    else:
        ospec = pl.BlockSpec((tr, cdim), lambda i, p_ref: (i, p_ref[1]))
        oshape = (h, 2 * cdim)
    args = []
    for landed, own in zip(landeds, owns):
        args += [landed, landed, landed, own]
    return pl.pallas_call(
        body, name=name,
        grid_spec=pltpu.PrefetchScalarGridSpec(
            num_scalar_prefetch=1, grid=(nt,), in_specs=[nxt(1), nxt(2), nxt(3), nxt(0)] * n, out_specs=[ospec] * n),
        out_shape=[jax.ShapeDtypeStruct(oshape, F32)] * n, compiler_params=_cparams(("parallel",)),
    )(place, *args)


def _cast_other_half(name, g, c_idx, axis):
    nb, r, cdim = g.shape
    hr, hc = (r // 2, cdim) if axis == 0 else (r, cdim // 2)
    tr = _pick(hr, (256, 176, 128, 64))
    if axis == 0:
        g4 = g.reshape(nb, 2, hr, hc)
        gspec = pl.BlockSpec((None, None, tr, hc), lambda b, i, c_ref: (b, 1 - c_ref[0], i, 0))
    else:
        g4 = g
        gspec = pl.BlockSpec((None, tr, hc), lambda b, i, c_ref: (b, i, 1 - c_ref[0]))

    def body(c_ref, g_ref, o_ref):
        o_ref[...] = g_ref[...].astype(BF16)

    return pl.pallas_call(
        body, name=name,
        grid_spec=pltpu.PrefetchScalarGridSpec(
            num_scalar_prefetch=1, grid=(nb, hr // tr), in_specs=[gspec],
            out_specs=pl.BlockSpec((None, tr, hc), lambda b, i, c_ref: (b, i, 0))),
        out_shape=jax.ShapeDtypeStruct((nb, hr, hc), BF16), compiler_params=_cparams(("parallel", "parallel")),
    )(c_idx, g4)


def _adamw(name, ws, gs, ms, vs):
    n = len(ws)
    c1 = 1.0 - ADAM_B1 ** ADAM_STEP
    c2 = 1.0 - ADAM_B2 ** ADAM_STEP

    def fn(*tiles):
        out = []
        for k in range(n):
            w_t, g_t, m_t, v_t = tiles[4 * k:4 * k + 4]
            m_n = ADAM_B1 * m_t + (1.0 - ADAM_B1) * g_t
            v_n = ADAM_B2 * v_t + (1.0 - ADAM_B2) * (g_t * g_t)
            out += [-ADAM_LR * ((m_n / c1) / (jnp.sqrt(v_n / c2) + ADAM_EPS) + ADAM_WD * w_t), m_n, v_n]
        return out

    rows, cdim = ws[0].shape
    tiled = [a for quad in zip(ws, gs, ms, vs) for a in quad]
    pref = (512, 352, 256, 128, 64, 8) if n == 1 else (176, 128, 64, 8)
    res = _rowwise(name, fn, tiled, [], [(cdim, F32)] * (3 * n), tm=_pick(rows, pref))
    return [tuple(res[3 * k:3 * k + 3]) for k in range(n)]


BIG = ("ffn1_w_gate", "ffn1_w_up", "ffn1_w_down", "w_in", "w_out", "ffn2_w_gate", "ffn2_w_up", "ffn2_w_down",
       "w_ple_gate", "w_ple_proj")
SMALL = ("ffn1_norm", "mix_norm", "b_mlstm_gates", "b_fox_f", "mlstm_out_norm", "fox_out_norm", "ffn2_norm",
         "ple_gate_norm", "ple_proj_norm", "final_norm")
WEIGHTS = ("ffn1_norm", "ffn1_w_gate", "ffn1_w_up", "ffn1_w_down", "mix_norm", "w_in", "conv_qk", "b_mlstm_gates",
           "b_fox_f", "mlstm_out_norm", "fox_out_norm", "w_out", "ffn2_norm", "ffn2_w_gate", "ffn2_w_up", "ffn2_w_down",
           "ple_gate_norm", "w_ple_gate", "w_ple_proj", "ple_proj_norm", "final_norm")
TRANSPOSED = ("ffn1_w_gate", "ffn1_w_up", "w_in", "ffn2_w_gate", "ffn2_w_up")
PACK_W = 1024


def _chip_blocks(a):
    r, c4 = a.shape
    return a.reshape(r, 4, c4 // 4).transpose(1, 0, 2)


def _from_chip_blocks(a):
    nb, r, c = a.shape
    return a.transpose(1, 0, 2).reshape(r, nb * c)


def kernel(x, p, ffn1_norm, ffn1_w_gate, ffn1_w_up, ffn1_w_down, mix_norm, w_in, conv_qk, b_mlstm_gates, b_fox_f, mlstm_out_norm, fox_out_norm, w_out, ffn2_norm, ffn2_w_gate, ffn2_w_up, ffn2_w_down, ple_gate_norm, w_ple_gate, w_ple_proj, ple_proj_norm, final_norm, loss_target, m_ffn1_norm, m_ffn1_w_gate, m_ffn1_w_up, m_ffn1_w_down, m_mix_norm, m_w_in, m_conv_qk, m_b_mlstm_gates, m_b_fox_f, m_mlstm_out_norm, m_fox_out_norm, m_w_out, m_ffn2_norm, m_ffn2_w_gate, m_ffn2_w_up, m_ffn2_w_down, m_ple_gate_norm, m_w_ple_gate, m_w_ple_proj, m_ple_proj_norm, m_final_norm, v_ffn1_norm, v_ffn1_w_gate, v_ffn1_w_up, v_ffn1_w_down, v_mix_norm, v_w_in, v_conv_qk, v_b_mlstm_gates, v_b_fox_f, v_mlstm_out_norm, v_fox_out_norm, v_w_out, v_ffn2_norm, v_ffn2_w_gate, v_ffn2_w_up, v_ffn2_w_down, v_ple_gate_norm, v_w_ple_gate, v_w_ple_proj, v_ple_proj_norm, v_final_norm):
    w = dict(ffn1_norm=ffn1_norm, ffn1_w_gate=ffn1_w_gate, ffn1_w_up=ffn1_w_up, ffn1_w_down=ffn1_w_down, mix_norm=mix_norm,
             w_in=w_in, conv_qk=conv_qk, b_mlstm_gates=b_mlstm_gates, b_fox_f=b_fox_f, mlstm_out_norm=mlstm_out_norm,
             fox_out_norm=fox_out_norm, w_out=w_out, ffn2_norm=ffn2_norm, ffn2_w_gate=ffn2_w_gate, ffn2_w_up=ffn2_w_up,
             ffn2_w_down=ffn2_w_down, ple_gate_norm=ple_gate_norm, w_ple_gate=w_ple_gate, w_ple_proj=w_ple_proj,
             ple_proj_norm=ple_proj_norm, final_norm=final_norm)
    m = dict(ffn1_norm=m_ffn1_norm, ffn1_w_gate=m_ffn1_w_gate, ffn1_w_up=m_ffn1_w_up, ffn1_w_down=m_ffn1_w_down,
             mix_norm=m_mix_norm, w_in=m_w_in, conv_qk=m_conv_qk, b_mlstm_gates=m_b_mlstm_gates, b_fox_f=m_b_fox_f,
             mlstm_out_norm=m_mlstm_out_norm, fox_out_norm=m_fox_out_norm, w_out=m_w_out, ffn2_norm=m_ffn2_norm,
             ffn2_w_gate=m_ffn2_w_gate, ffn2_w_up=m_ffn2_w_up, ffn2_w_down=m_ffn2_w_down, ple_gate_norm=m_ple_gate_norm,
             w_ple_gate=m_w_ple_gate, w_ple_proj=m_w_ple_proj, ple_proj_norm=m_ple_proj_norm, final_norm=m_final_norm)
    v = dict(ffn1_norm=v_ffn1_norm, ffn1_w_gate=v_ffn1_w_gate, ffn1_w_up=v_ffn1_w_up, ffn1_w_down=v_ffn1_w_down,
             mix_norm=v_mix_norm, w_in=v_w_in, conv_qk=v_conv_qk, b_mlstm_gates=v_b_mlstm_gates, b_fox_f=v_b_fox_f,
             mlstm_out_norm=v_mlstm_out_norm, fox_out_norm=v_fox_out_norm, w_out=v_w_out, ffn2_norm=v_ffn2_norm,
             ffn2_w_gate=v_ffn2_w_gate, ffn2_w_up=v_ffn2_w_up, ffn2_w_down=v_ffn2_w_down, ple_gate_norm=v_ple_gate_norm,
             w_ple_gate=v_w_ple_gate, w_ple_proj=v_w_ple_proj, ple_proj_norm=v_ple_proj_norm, final_norm=v_final_norm)
    shapes = {n: w[n].shape for n in WEIGHTS}

    def view(a, n):
        return a[0].T if n in TRANSPOSED else a.reshape(-1, a.shape[-1])

    def unview(a, n):
        return (a.T if n in TRANSPOSED else a).reshape(shapes[n])

    w2, m2, v2 = ({n: view(a, n) for n, a in d.items()} for d in (w, m, v))

    c_idx = lax.axis_index("c").astype(jnp.int32).reshape(1)
    me_idx = (2 * lax.axis_index("x") + lax.axis_index("y")).astype(jnp.int32).reshape(1)
    place = jnp.concatenate([me_idx, c_idx])
    slot = {}
    for grp in SAME_SHAPE:
        slot.update(zip(grp, _to_slot("slot_" + grp[0], [w2[n] for n in grp], me_idx, BF16)))
    slot["conv_qk"] = _to_slot("slot_conv_qk", [w2["conv_qk"]], me_idx, F32)[0]
    wg1, wu1, wd1 = _gather4("gather_ffn1", [slot[n] for n in FFN1], [SPLIT[n] for n in FFN1])
    sp = {n: w2[n] for n in SMALL}
    loss_part, grad_x, grads, gs, conv_grad = _local_step(
        x[0], p[0, 0], loss_target[0], sp, wg1, wu1, wd1, [slot[n] for n in REST + ("conv_qk",)], c_idx, place)

    small = [gs[n].reshape(1, -1) for n in SMALL] + [conv_grad, loss_part]
    rows = [jnp.pad(a, ((0, 0), (0, PACK_W - a.shape[1]))) for a in small]
    packed = jnp.concatenate(rows, axis=0)
    packed = jnp.pad(packed, ((0, -packed.shape[0] % 8), (0, 0)))
    red = _allreduce_small(packed)
    loss = red[len(SMALL) + CONV_W, 0]
    for i, n in enumerate(SMALL):
        grads[n] = red[i:i + 1, :gs[n].size]
    dconv = red[len(SMALL):len(SMALL) + CONV_W, :conv_grad.shape[1]]
    cw = conv_qk.shape[-1]
    grads["conv_qk"] = lax.dynamic_slice_in_dim(dconv, (2 * lax.axis_index("x") + lax.axis_index("y")) * cw, cw, axis=1)

    outs = {}
    for grp in SAME_SHAPE + tuple((n,) for n in WEIGHTS if n not in BIG):
        g2s = [grads[n].reshape(w2[n].shape) for n in grp]
        res = _adamw("adamw_" + grp[0], [w2[n] for n in grp], g2s, [m2[n] for n in grp], [v2[n] for n in grp])
        for n, g2, (d, nm, nv) in zip(grp, g2s, res):
            outs[n] = tuple(unview(a, n) for a in (g2, d, nm, nv))
    return (loss, grad_x[None], *[outs[n][0] for n in WEIGHTS], *[outs[n][1] for n in WEIGHTS],
            *[outs[n][2] for n in WEIGHTS], *[outs[n][3] for n in WEIGHTS])
```

```python
import jax
import jax.numpy as jnp
from jax import lax
from jax.experimental import pallas as pl
from jax.experimental.pallas import tpu as pltpu

F32 = jnp.float32
BF16 = jnp.bfloat16
EPS = 1e-6
NH_M, DK_M, DV_M = 4, 64, 128
NH_F, DH_F = 8, 64
CONV_W = 4
ADAM_LR, ADAM_B1, ADAM_B2, ADAM_EPS, ADAM_WD, ADAM_STEP = 0.001, 0.9, 0.999, 1e-08, 0.01, 10
VMEM_LIMIT = 56 * 1024 * 1024


def _cparams(sem):
    return pltpu.CompilerParams(dimension_semantics=sem, vmem_limit_bytes=VMEM_LIMIT)


def _sigmoid(x):
    return 1.0 / (1.0 + jnp.exp(-x))


def _dot(a, b, ca, cb):
    return lax.dot_general(a.astype(BF16), b.astype(BF16), (((ca,), (cb,)), ((), ())), preferred_element_type=F32)


def _rowwise(name, fn, tiled, full, outs, accs=(), tm=512, plan=None):
    rows = tiled[0].shape[0]
    tm = min(tm, rows)
    assert rows % tm == 0
    n_t, n_f, n_o, n_a = len(tiled), len(full), len(outs), len(accs)
    nt = rows // tm
    host = _Hosted(plan, n_t + n_f, n_o + n_a)

    def body(*refs):
        in_refs, orefs, _, prefs = host.split(refs)
        host.run(0, pl.program_id(0) == 0, prefs)
        host.run(1, pl.program_id(0) == 0, prefs)
        ins = [r[...] for r in in_refs]
        res = fn(*ins)
        if not isinstance(res, (tuple, list)):
            res = (res,)
        for r, v in zip(orefs[:n_o], res[:n_o]):
            r[...] = v.astype(r.dtype)
        if n_a:
            @pl.when(pl.program_id(0) == 0)
            def _():
                for r in orefs[n_o:]:
                    r[...] = jnp.zeros_like(r)
            for r, v in zip(orefs[n_o:], res[n_o:]):
                r[...] += v.astype(r.dtype)
        host.run(2, pl.program_id(0) == nt - 1, prefs)

    in_specs = [pl.BlockSpec((tm, a.shape[1]), lambda i: (i, 0)) for a in tiled]
    in_specs += [pl.BlockSpec(a.shape, lambda i: (0, 0)) for a in full]
    out_specs = [pl.BlockSpec((tm, c), lambda i: (i, 0)) for c, _ in outs]
    out_specs += [pl.BlockSpec(s, lambda i: (0, 0)) for s, _ in accs]
    out_shape = [jax.ShapeDtypeStruct((rows, c), d) for c, d in outs]
    out_shape += [jax.ShapeDtypeStruct(s, d) for s, d in accs]
    hc = host.call_args()
    res = pl.pallas_call(
        body, name=name, grid=(nt,), in_specs=in_specs + hc["in_specs"], out_specs=out_specs + hc["out_specs"],
        out_shape=out_shape + hc["out_shape"], scratch_shapes=hc["scratch"], input_output_aliases=hc["aliases"],
        compiler_params=_cparams(("arbitrary",) if (n_a or plan is not None) else ("parallel",)),
    )(*tiled, *full, *hc["args"])
    return res if plan is None else (res[:n_o + n_a], res[n_o + n_a:])


def _colsum(v):
    return jnp.sum(v, axis=0, keepdims=True)


def _rms_fwd_val(x, g):
    r = lax.rsqrt(jnp.mean(x * x, axis=-1, keepdims=True) + EPS)
    return x * r * g


def _rms_bwd_val(dy, x, g):
    r = lax.rsqrt(jnp.mean(x * x, axis=-1, keepdims=True) + EPS)
    xh = x * r
    dxh = dy * g
    dx = r * (dxh - xh * jnp.mean(dxh * xh, axis=-1, keepdims=True))
    return dx, _colsum(dy * xh)


def _mm(name, pairs, out_shape, out_block, out_map, grid, kaxis, ta=False, tb=False, scale=None, res=None,
        out_dtype=F32, plan=None, twin=False):
    n_o = 2 if twin else 1
    nk = grid[kaxis]
    npairs = len(pairs)
    ca, cb = (0 if ta else 1), (1 if tb else 0)
    acc_shape = tuple(d for d in out_block if d is not None)
    n_in = 2 * npairs + (1 if res is not None else 0)
    host = _Hosted(plan, n_in, n_o)

    def body(*refs):
        ins, o_refs, (acc_ref,), prefs = host.split(refs)
        o_ref = o_refs[0]
        in_refs = ins[: 2 * npairs]
        res_ref = ins[2 * npairs] if res is not None else None
        k = pl.program_id(kaxis)
        ids = [pl.program_id(a) for a in range(len(grid))]
        first, last = ids[0] == 0, ids[0] == grid[0] - 1
        for a in range(1, len(grid)):
            first, last = first & (ids[a] == 0), last & (ids[a] == grid[a] - 1)
        host.run(0, first, prefs)
        host.run(1, first, prefs)

        @pl.when(k == 0)
        def _():
            acc_ref[...] = jnp.zeros_like(acc_ref)

        part = None
        for p in range(npairs):
            d = _dot(in_refs[2 * p][...], in_refs[2 * p + 1][...], ca, cb)
            part = d if part is None else part + d
        acc_ref[...] += part

        @pl.when(k == nk - 1)
        def _():
            v = acc_ref[...]
            if scale is not None:
                v = v * scale
            if res_ref is not None:
                v = v + res_ref[...].astype(F32)
            o_ref[...] = v.astype(o_ref.dtype)
            if twin:
                o_refs[1][...] = v.astype(BF16)

        host.run(2, last, prefs)

    in_specs, args = [], []
    for a, ab, am, b, bb, bm in pairs:
        in_specs += [pl.BlockSpec(ab, am), pl.BlockSpec(bb, bm)]
        args += [a, b]
    if res is not None:
        in_specs.append(pl.BlockSpec(out_block, out_map))
        args.append(res)
    sem = tuple("arbitrary" if (i == kaxis or plan is not None) else "parallel" for i in range(len(grid)))
    hc = host.call_args()
    out = pl.pallas_call(
        body, name=name, grid=grid, in_specs=in_specs + hc["in_specs"],
        out_specs=[pl.BlockSpec(out_block, out_map)] * n_o + hc["out_specs"],
        out_shape=[jax.ShapeDtypeStruct(out_shape, out_dtype)] + [jax.ShapeDtypeStruct(out_shape, BF16)] * (n_o - 1)
        + hc["out_shape"],
        scratch_shapes=[pltpu.VMEM(acc_shape, F32)] + hc["scratch"], input_output_aliases=hc["aliases"],
        compiler_params=_cparams(sem),
    )(*args, *hc["args"])
    res_out = tuple(out[:2]) if twin else out[0]
    return res_out if plan is None else (res_out, out[n_o:])


def _pick(n, pref):
    for t in pref:
        if n % t == 0:
            return t
    return n


def _mm_nn(name, a, b, tm=512, tn=512, tk=512, **kw):
    (m, k), n = a.shape, b.shape[1]
    tm, tn, tk = _pick(m, (tm, 256, 128)), _pick(n, (tn, 256, 128)), _pick(k, (tk, 256, 128))
    return _mm(name, [(a, (tm, tk), lambda i, j, kk: (i, kk), b, (tk, tn), lambda i, j, kk: (kk, j))],
               (m, n), (tm, tn), lambda i, j, kk: (i, j), (m // tm, n // tn, k // tk), 2, **kw)


def _mm_nt(name, a, b, tm=512, tn=512, tk=512, **kw):
    (m, k), n = a.shape, b.shape[0]
    tm, tn, tk = _pick(m, (tm, 256, 128)), _pick(n, (tn, 256, 128)), _pick(k, (tk, 256, 128))
    return _mm(name, [(a, (tm, tk), lambda i, j, kk: (i, kk), b, (tn, tk), lambda i, j, kk: (j, kk))],
               (m, n), (tm, tn), lambda i, j, kk: (i, j), (m // tm, n // tn, k // tk), 2, tb=True, **kw)


def _mm_tn(name, a, b, tm=512, tn=512, tk=4096, **kw):
    (k, m), n = a.shape, b.shape[1]
    tm, tn, tk = _pick(m, (tm, 256, 128)), _pick(n, (tn, 256, 128)), _pick(k, (tk, 2048, 1024, 512, 256, 128))
    return _mm(name, [(a, (tk, tm), lambda i, j, kk: (kk, i), b, (tk, tn), lambda i, j, kk: (kk, j))],
               (m, n), (tm, tn), lambda i, j, kk: (i, j), (m // tm, n // tn, k // tk), 2, ta=True, **kw)


def _norm_mm(name, h, gamma, w, w_transposed, out_dtype):
    t, d = h.shape
    n = w.shape[0] if w_transposed else w.shape[1]
    tm, tn = _pick(t, (512, 256)), _pick(n, (1024, 512, 256, 128))

    def body(h_ref, gam_ref, w_ref, xn_ref, o_ref, xn_scr):
        @pl.when(pl.program_id(1) == 0)
        def _():
            xn = _rms_fwd_val(h_ref[...], gam_ref[...]).astype(BF16)
            xn_scr[...] = xn
            xn_ref[...] = xn

        o_ref[...] = _dot(xn_scr[...], w_ref[...], 1, 1 if w_transposed else 0).astype(o_ref.dtype)

    wspec = pl.BlockSpec((tn, d), lambda i, j: (j, 0)) if w_transposed else pl.BlockSpec((d, tn), lambda i, j: (0, j))
    return pl.pallas_call(
        body, name=name, grid=(t // tm, n // tn),
        in_specs=[pl.BlockSpec((tm, d), lambda i, j: (i, 0)), pl.BlockSpec((1, d), lambda i, j: (0, 0)), wspec],
        out_specs=[pl.BlockSpec((tm, d), lambda i, j: (i, 0)), pl.BlockSpec((tm, tn), lambda i, j: (i, j))],
        out_shape=[jax.ShapeDtypeStruct((t, d), BF16), jax.ShapeDtypeStruct((t, n), out_dtype)],
        scratch_shapes=[pltpu.VMEM((tm, d), BF16)], compiler_params=_cparams(("parallel", "arbitrary")),
    )(h, gamma, w)


CHAIN_ROWS = 256


def _row_chains(tm):
    n = max(tm // CHAIN_ROWS, 1)
    return [slice(r * (tm // n), (r + 1) * (tm // n)) for r in range(n)]


def _ffn_fwd(pfx, h, gamma, wg, wu, wd, plan=None):
    t, d = h.shape
    nb, f, _ = wg.shape
    tm = _pick(t, (1024, 512, 256))
    nt = t // tm
    host = _Hosted(plan, 5, 4)

    def body(*refs):
        (h_ref, gam_ref, wg_ref, wu_ref, wd_ref), (ho_ref, xn_ref, g_ref, u_ref), (xn_scr, acc_ref), prefs = host.split(refs)
        i, j = pl.program_id(0), pl.program_id(1)
        host.run(0, (i == 0) & (j == 0), prefs)
        host.run(1, (i == nt // 2) & (j == 0), prefs)

        @pl.when(j == 0)
        def _():
            xn = _rms_fwd_val(h_ref[...], gam_ref[...]).astype(BF16)
            xn_scr[...] = xn
            xn_ref[...] = xn
            acc_ref[...] = jnp.zeros_like(acc_ref)

        for rows in _row_chains(tm):
            x = xn_scr[rows, :]
            g = _dot(x, wg_ref[...], 1, 1)
            u = _dot(x, wu_ref[...], 1, 1)
            g_ref[rows, :] = g.astype(BF16)
            u_ref[rows, :] = u.astype(BF16)
            acc_ref[rows, :] += _dot(g * _sigmoid(g) * u, wd_ref[...], 1, 0)

        @pl.when(j == nb - 1)
        def _():
            ho_ref[...] = h_ref[...] + 0.5 * acc_ref[...]

        host.run(2, (i == nt - 1) & (j == nb - 1), prefs)

    row = pl.BlockSpec((tm, d), lambda i, j: (i, 0))
    blk = pl.BlockSpec((None, tm, f), lambda i, j: (j, i, 0))
    wspec = pl.BlockSpec((None, f, d), lambda i, j: (j, 0, 0))
    hc = host.call_args()
    res = pl.pallas_call(
        body, name=pfx + "_fwd", grid=(nt, nb),
        in_specs=[row, pl.BlockSpec((1, d), lambda i, j: (0, 0)), wspec, wspec, wspec] + hc["in_specs"],
        out_specs=[row, row, blk, blk] + hc["out_specs"],
        out_shape=[jax.ShapeDtypeStruct((t, d), F32), jax.ShapeDtypeStruct((t, d), BF16),
                   jax.ShapeDtypeStruct((nb, t, f), BF16), jax.ShapeDtypeStruct((nb, t, f), BF16)] + hc["out_shape"],
        scratch_shapes=[pltpu.VMEM((tm, d), BF16), pltpu.VMEM((tm, d), F32)] + hc["scratch"],
        input_output_aliases=hc["aliases"], compiler_params=_cparams(("arbitrary", "arbitrary")),
    )(h, gamma, wg, wu, wd, *hc["args"])
    return res[:4], res[4:]


def _ffn_bwd(pfx, dh_out, h, gamma, xn, g_all, u_all, wg, wu, wd, plan=None):
    t, d = h.shape
    nb, f, _ = wg.shape
    tm = _pick(t, (512, 256))
    tk = _pick(t, (4096, 2048, 1024, 512, 256))

    nt = t // tm
    host = _Hosted(plan, 8, 6)

    def body(*refs):
        ((dy_ref, h_ref, gam_ref, wg_ref, wu_ref, wd_ref, g_ref, u_ref),
         (dh_ref, dgam_ref, dg_ref, du_ref, a_ref, dyb_ref), (acc_ref,), prefs) = host.split(refs)
        i, j = pl.program_id(0), pl.program_id(1)
        host.run(0, (i == 0) & (j == 0), prefs)
        host.run(1, (i == nt // 2) & (j == 0), prefs)

        @pl.when((i == 0) & (j == 0))
        def _():
            dgam_ref[...] = jnp.zeros_like(dgam_ref)

        @pl.when(j == 0)
        def _():
            acc_ref[...] = jnp.zeros_like(acc_ref)
            dyb_ref[...] = dy_ref[...].astype(BF16)

        for rows in _row_chains(tm):
            da = _dot(dy_ref[rows, :], wd_ref[...], 1, 1) * 0.5
            g = g_ref[rows, :].astype(F32)
            u = u_ref[rows, :].astype(F32)
            s = _sigmoid(g)
            sl = g * s
            du = (da * sl).astype(BF16)
            dg = (da * u * (s + sl * (1.0 - s))).astype(BF16)
            du_ref[rows, :] = du
            dg_ref[rows, :] = dg
            a_ref[rows, :] = (sl * u).astype(BF16)
            acc_ref[rows, :] += _dot(dg, wg_ref[...], 1, 0) + _dot(du, wu_ref[...], 1, 0)

        @pl.when(j == nb - 1)
        def _():
            dx, dgam = _rms_bwd_val(acc_ref[...], h_ref[...], gam_ref[...])
            dh_ref[...] = dy_ref[...] + dx
            dgam_ref[...] += dgam

        host.run(2, (i == nt - 1) & (j == nb - 1), prefs)

    row = pl.BlockSpec((tm, d), lambda i, j: (i, 0))
    vec = pl.BlockSpec((1, d), lambda i, j: (0, 0))
    blk = pl.BlockSpec((None, tm, f), lambda i, j: (j, i, 0))
    wspec = pl.BlockSpec((None, f, d), lambda i, j: (j, 0, 0))
    hc = host.call_args()
    res = pl.pallas_call(
        body, name=pfx + "_bwd", grid=(nt, nb),
        in_specs=[row, row, vec, wspec, wspec, wspec, blk, blk] + hc["in_specs"],
        out_specs=[row, vec, blk, blk, blk, row] + hc["out_specs"],
        out_shape=[jax.ShapeDtypeStruct((t, d), F32), jax.ShapeDtypeStruct((1, d), F32)]
        + [jax.ShapeDtypeStruct((nb, t, f), BF16)] * 3 + [jax.ShapeDtypeStruct((t, d), BF16)] + hc["out_shape"],
        scratch_shapes=[pltpu.VMEM((tm, d), F32)] + hc["scratch"], input_output_aliases=hc["aliases"],
        compiler_params=_cparams(("arbitrary", "arbitrary")),
    )(dh_out, h, gamma, wg, wu, wd, g_all, u_all, *hc["args"])
    dh, dgamma, dg_all, du_all, a_all, dyb = res[:6]

    xmap, bmap, omap = (lambda b, k: (k, 0)), (lambda b, k: (b, k, 0)), (lambda b, k: (b, 0, 0))
    dwg, tg = _mm(pfx + "_dwg", [(dg_all, (None, tk, f), bmap, xn, (tk, d), xmap)], (nb, f, d), (None, f, d), omap,
                  (nb, t // tk), 1, ta=True, twin=True)
    dwu, tu = _mm(pfx + "_dwu", [(du_all, (None, tk, f), bmap, xn, (tk, d), xmap)], (nb, f, d), (None, f, d), omap,
                  (nb, t // tk), 1, ta=True, twin=True)
    dwd, td = _mm(pfx + "_dwd", [(a_all, (None, tk, f), bmap, dyb, (tk, d), xmap)], (nb, f, d), (None, f, d), omap,
                  (nb, t // tk), 1, ta=True, scale=0.5, twin=True)
    return (dh, dgamma, dwg, dwu, dwd), res[6:], (tg, tu, td)


def _ffn_bwd_late_dx(pfx, dh_out, h, gamma, xn, g_all, u_all, wg, wu, wd, plan_gu, plans_dw, make_plan_dx):
    t, d = h.shape
    nb, f, _ = wg.shape
    tm = _pick(t, (512, 256))
    tk = _pick(t, (4096, 2048, 1024, 512, 256))
    nt = t // tm
    host_a = _Hosted(plan_gu, 4, 4)

    def body_a(*refs):
        (dy_ref, wd_ref, g_ref, u_ref), (dg_ref, du_ref, a_ref, dyb_ref), _, prefs = host_a.split(refs)
        i, j = pl.program_id(0), pl.program_id(1)
        host_a.run(0, (i == 0) & (j == 0), prefs)
        host_a.run(1, (i == 0) & (j == 0), prefs)

        @pl.when(j == 0)
        def _():
            dyb_ref[...] = dy_ref[...].astype(BF16)

        for rows in _row_chains(tm):
            da = _dot(dy_ref[rows, :], wd_ref[...], 1, 1) * 0.5
            g = g_ref[rows, :].astype(F32)
            u = u_ref[rows, :].astype(F32)
            s = _sigmoid(g)
            sl = g * s
            du_ref[rows, :] = (da * sl).astype(BF16)
            dg_ref[rows, :] = (da * u * (s + sl * (1.0 - s))).astype(BF16)
            a_ref[rows, :] = (sl * u).astype(BF16)
        host_a.run(2, (i == nt - 1) & (j == nb - 1), prefs)

    row = pl.BlockSpec((tm, d), lambda i, j: (i, 0))
    vec = pl.BlockSpec((1, d), lambda i, j: (0, 0))
    blk = pl.BlockSpec((None, tm, f), lambda i, j: (j, i, 0))
    wspec = pl.BlockSpec((None, f, d), lambda i, j: (j, 0, 0))
    hc = host_a.call_args()
    res_a = pl.pallas_call(
        body_a, name=pfx + "_bwd_gu", grid=(nt, nb), in_specs=[row, wspec, blk, blk] + hc["in_specs"],
        out_specs=[blk] * 3 + [row] + hc["out_specs"],
        out_shape=[jax.ShapeDtypeStruct((nb, t, f), BF16)] * 3 + [jax.ShapeDtypeStruct((t, d), BF16)] + hc["out_shape"],
        scratch_shapes=hc["scratch"], input_output_aliases=hc["aliases"], compiler_params=_cparams(("arbitrary", "arbitrary")),
    )(dh_out, wd, g_all, u_all, *hc["args"])
    dg_all, du_all, a_all, dyb = res_a[:4]

    xmap, bmap, omap = (lambda b, k: (k, 0)), (lambda b, k: (b, k, 0)), (lambda b, k: (b, 0, 0))
    def dw(name, a, b, plan, scale=None):
        r = _mm(pfx + name, [(a, (None, tk, f), bmap, b, (tk, d), xmap)], (nb, f, d), (None, f, d), omap, (nb, t // tk), 1,
                ta=True, scale=scale, plan=plan, twin=True)
        return r if plan is not None else (r, ())

    (dwd, td), out_d = dw("_dwd", a_all, dyb, plans_dw, 0.5)
    (dwg, tg), out_g = dw("_dwg", dg_all, xn, make_plan_dx[0](dwd, td))
    (dwu, tu), out_u = dw("_dwu", du_all, xn, make_plan_dx[1](out_g))
    plan_dx = make_plan_dx[2](out_u, (dwg, dwu), (tg, tu))
    host_b = _Hosted(plan_dx, 7, 2)

    def body_b(*refs):
        (dy_ref, h_ref, gam_ref, wg_ref, wu_ref, dg_ref, du_ref), (dh_ref, dgam_ref), (acc_ref,), prefs = host_b.split(refs)
        i, j = pl.program_id(0), pl.program_id(1)
        host_b.run(0, (i == 0) & (j == 0), prefs)
        host_b.run(1, (i == 0) & (j == 0), prefs)

        @pl.when((i == 0) & (j == 0))
        def _():
            dgam_ref[...] = jnp.zeros_like(dgam_ref)

        @pl.when(j == 0)
        def _():
            acc_ref[...] = jnp.zeros_like(acc_ref)

        acc_ref[...] += _dot(dg_ref[...], wg_ref[...], 1, 0) + _dot(du_ref[...], wu_ref[...], 1, 0)

        @pl.when(j == nb - 1)
        def _():
            dx, dgam = _rms_bwd_val(acc_ref[...], h_ref[...], gam_ref[...])
            dh_ref[...] = dy_ref[...] + dx
            dgam_ref[...] += dgam

        host_b.run(2, (i == nt - 1) & (j == nb - 1), prefs)

    hc = host_b.call_args()
    res_b = pl.pallas_call(
        body_b, name=pfx + "_bwd_dx", grid=(nt, nb), in_specs=[row, row, vec, wspec, wspec, blk, blk] + hc["in_specs"],
        out_specs=[row, vec] + hc["out_specs"],
        out_shape=[jax.ShapeDtypeStruct((t, d), F32), jax.ShapeDtypeStruct((1, d), F32)] + hc["out_shape"],
        scratch_shapes=[pltpu.VMEM((tm, d), F32)] + hc["scratch"], input_output_aliases=hc["aliases"],
        compiler_params=_cparams(("arbitrary", "arbitrary")),
    )(dh_out, h, gamma, wg, wu, dg_all, du_all, *hc["args"])
    return (res_b[0], res_b[1], dwg, dwu, dwd), (res_a[4:], out_d, out_g, out_u, res_b[2:])


HALO = 16


def _silu_grad(y):
    s = _sigmoid(y)
    return s * (1.0 + y * (1.0 - s))


def _with_halo(ref, i, n_tiles, tm, before, after):
    t = ref.shape[0]
    r0 = pl.multiple_of(i * tm, tm)
    parts = [ref[pl.ds(r0, tm), :].astype(F32)]
    if before:
        prev = ref[pl.ds(pl.multiple_of(jnp.maximum(r0 - HALO, 0), HALO), HALO), :].astype(F32)
        parts.insert(0, jnp.where(i > 0, prev, 0.0))
    if after:
        nxt = ref[pl.ds(pl.multiple_of(jnp.minimum(r0 + tm, t - HALO), HALO), HALO), :].astype(F32)
        parts.append(jnp.where(i < n_tiles - 1, nxt, 0.0))
    return jnp.concatenate(parts, axis=0)


def _conv_fwd(zbig, w):
    t, c = zbig.shape[0], w.shape[1]
    tm = _pick(t, (512, 256))
    nt = t // tm

    def body(x_ref, w_ref, o_ref):
        xe = _with_halo(x_ref, pl.program_id(0), nt, tm, True, False)
        wv = w_ref[...]
        y = xe * wv[3:4, :]
        for i in range(CONV_W - 1):
            y = y + pltpu.roll(xe, CONV_W - 1 - i, 0) * wv[i:i + 1, :]
        y = y[HALO:, :]
        o_ref[...] = (y * _sigmoid(y)).astype(o_ref.dtype)

    return pl.pallas_call(
        body, name="conv_fwd", grid=(nt,),
        in_specs=[pl.BlockSpec((t, c), lambda i: (0, 0)), pl.BlockSpec(w.shape, lambda i: (0, 0))],
        out_specs=pl.BlockSpec((tm, c), lambda i: (i, 0)), out_shape=jax.ShapeDtypeStruct((t, c), BF16),
        compiler_params=_cparams(("parallel",)),
    )(zbig, w)


def _conv_bwd(zbig, dact, w):
    t, c = dact.shape
    tm = _pick(t, (512, 256))
    nt = t // tm
    n = tm + HALO

    def body(x_ref, d_ref, w_ref, dx_ref, dw_ref):
        xe = _with_halo(x_ref, pl.program_id(0), nt, tm, True, True)
        de = _with_halo(d_ref, pl.program_id(0), nt, tm, False, True)
        wv = w_ref[...]
        sh = [pltpu.roll(xe, CONV_W - 1 - i, 0)[HALO:, :] if i < CONV_W - 1 else xe[HALO:, :] for i in range(CONV_W)]
        y = sh[0] * wv[0:1, :]
        for i in range(1, CONV_W):
            y = y + sh[i] * wv[i:i + 1, :]
        dy = de * _silu_grad(y)
        dx = dy * wv[3:4, :]
        for i in range(CONV_W - 1):
            dx = dx + pltpu.roll(dy, n - (CONV_W - 1 - i), 0) * wv[i:i + 1, :]
        dx_ref[...] = dx[:tm, :].astype(dx_ref.dtype)
        dyc = dy[:tm, :]
        dwp = jnp.concatenate([_colsum(dyc * sh[i][:tm, :]) for i in range(CONV_W)], axis=0)

        @pl.when(pl.program_id(0) == 0)
        def _():
            dw_ref[...] = jnp.zeros_like(dw_ref)
        dw_ref[...] += dwp

    return pl.pallas_call(
        body, name="conv_bwd", grid=(nt,),
        in_specs=[pl.BlockSpec((t, c), lambda i: (0, 0)), pl.BlockSpec((t, c), lambda i: (0, 0)),
                  pl.BlockSpec(w.shape, lambda i: (0, 0))],
        out_specs=[pl.BlockSpec((tm, c), lambda i: (i, 0)), pl.BlockSpec(w.shape, lambda i: (0, 0))],
        out_shape=[jax.ShapeDtypeStruct((t, c), BF16), jax.ShapeDtypeStruct(w.shape, F32)],
        compiler_params=_cparams(("arbitrary",)),
    )(zbig, dact, w)


LM = 256
HI = lax.Precision.HIGHEST


def _logsig(x):
    return jnp.minimum(x, 0.0) - jnp.log(1.0 + jnp.exp(-jnp.abs(x)))


def _tri(n, lower):
    r = lax.broadcasted_iota(jnp.int32, (n, n), 0)
    c = lax.broadcasted_iota(jnp.int32, (n, n), 1)
    return (r >= c) if lower else (r <= c)


def _f32dot(a, b):
    return lax.dot_general(a, b, (((1,), (0,)), ((), ())), precision=HI, preferred_element_type=F32)


def _tri_dot(a, b, a_is_tri):
    tri = (a if a_is_tri else b).astype(BF16)
    parts = _split3(b if a_is_tri else a)
    outs = [_dot(tri, p, 1, 0) if a_is_tri else _dot(p, tri, 1, 0) for p in parts]
    return (outs[0] + outs[1]) + outs[2]


def _mlstm_decays(zs_ref, zsr_ref, bc_ref, br_ref):
    l = LM
    lf_c = _logsig(zs_ref[:, 0:2 * NH_M] + bc_ref[...])
    lf_r = _logsig(zsr_ref[...] + br_ref[...])
    low, up = _tri(l, True), _tri(l, False)
    return _tri_dot(low, lf_c, True), _tri_dot(lf_r, up, False), low, up


def _mlstm_chunk(h, q_ref, k_ref, v_ref, zs_ref, zsr_ref, bc_ref, br_ref, c_prev, m_prev, decays):
    l = LM
    q = q_ref[:, h * DK_M:(h + 1) * DK_M].astype(F32) * (DK_M ** -0.5)
    k = k_ref[:, h * DK_M:(h + 1) * DK_M]
    v = v_ref[:, h * DV_M:(h + 1) * DV_M]
    lane = lax.broadcasted_iota(jnp.int32, (l, DV_M), 1)
    v1 = jnp.concatenate([v, (lane == 0).astype(v.dtype)], axis=1)
    zs, zsr = zs_ref[...], zsr_ref[...]
    li_c = zs[:, h:h + 1] + bc_ref[:, h:h + 1]
    fp_c = zs[:, NH_M + h:NH_M + h + 1] + bc_ref[:, NH_M + h:NH_M + h + 1]
    li_r = zsr[h:h + 1, :] + br_ref[h:h + 1, :]
    fp_r = zsr[NH_M + h:NH_M + h + 1, :] + br_ref[NH_M + h:NH_M + h + 1, :]
    low = decays[2]
    b_c = decays[0][:, NH_M + h:NH_M + h + 1]
    b_r = decays[1][NH_M + h:NH_M + h + 1, :]
    g = b_r[:, l - 1:l]
    dmat = jnp.where(low, b_c - b_r + li_r, -jnp.inf)
    inter = b_c + m_prev
    m_t = jnp.maximum(inter, jnp.max(dmat, axis=1, keepdims=True))
    w_inter = jnp.exp(inter - m_t)
    amat = jnp.exp(dmat - m_t)
    s = _dot(q, k, 1, 1)
    p = amat * s
    qc = _dot(q, c_prev, 1, 0)
    qc_w = w_inter * qc
    num1 = qc_w + _dot(p, v1, 1, 0)
    den = num1[:, DV_M:DV_M + 1]
    mx = jnp.maximum(jnp.abs(den), jnp.exp(-m_t))
    hh = num1[:, :DV_M] / mx
    a_c = g - b_c + li_c
    return dict(q=q, k=k, v1=v1, fp_c=fp_c, fp_r=fp_r, b_c=b_c, g=g, m_t=m_t, w_inter=w_inter, amat=amat, s=s, p=p,
                qc_w=qc_w, den=den, mx=mx, hh=hh, a_c=a_c)


def _mlstm_fwd(qk, zbig, zs, zsr, bc, br, gm):
    t = zs.shape[0]
    l = LM
    nc = t // l
    dm = NH_M * DV_M

    def body(q_ref, k_ref, v_ref, o_ref, zs_ref, zsr_ref, bc_ref, br_ref, gm_ref, y_ref, cst_ref, mst_ref, c_scr, m_scr):
        @pl.when(pl.program_id(0) == 0)
        def _():
            c_scr[...] = jnp.zeros_like(c_scr)
            m_scr[...] = jnp.zeros_like(m_scr)

        cst_ref[...] = c_scr[...]
        mst_ref[...] = m_scr[...]
        ys = []
        decays = _mlstm_decays(zs_ref, zsr_ref, bc_ref, br_ref)
        for h in range(NH_M):
            c_prev = c_scr[h]
            m_prev = m_scr[h:h + 1, 0:1]
            r = _mlstm_chunk(h, q_ref, k_ref, v_ref, zs_ref, zsr_ref, bc_ref, br_ref, c_prev, m_prev, decays)
            hh = r["hh"]
            gh = gm_ref[:, h * DV_M:(h + 1) * DV_M]
            hn = hh * lax.rsqrt(jnp.mean(hh * hh, axis=-1, keepdims=True) + EPS) * gh
            og = o_ref[:, h * DV_M:(h + 1) * DV_M].astype(F32)
            ys.append(hn * _sigmoid(og))
            m_new = jnp.maximum(r["g"] + m_prev, jnp.max(r["a_c"], axis=0, keepdims=True))
            decay = jnp.exp(r["g"] + m_prev - m_new)
            wk = r["k"].astype(F32) * jnp.exp(r["a_c"] - m_new)
            c_scr[h] = decay * c_prev + _dot(wk, r["v1"], 0, 0)
            m_scr[h:h + 1, :] = jnp.broadcast_to(m_new, (1, 128))
        y_ref[...] = jnp.concatenate(ys, axis=1).astype(y_ref.dtype)

    return pl.pallas_call(
        body, name="mlstm_fwd", grid=(nc,),
        in_specs=[pl.BlockSpec((l, NH_M * DK_M), lambda i: (i, 0)), pl.BlockSpec((l, NH_M * DK_M), lambda i: (i, 1)),
                  pl.BlockSpec((l, dm), lambda i: (i, 1)), pl.BlockSpec((l, dm), lambda i: (i, 2)),
                  pl.BlockSpec((l, 128), lambda i: (i, 0)), pl.BlockSpec((8, l), lambda i: (0, i)),
                  pl.BlockSpec((1, 8), lambda i: (0, 0)), pl.BlockSpec((8, 1), lambda i: (0, 0)),
                  pl.BlockSpec((1, dm), lambda i: (0, 0))],
        out_specs=[pl.BlockSpec((l, dm), lambda i: (i, 0)), pl.BlockSpec((None, NH_M, DK_M, 2 * DV_M), lambda i: (i, 0, 0, 0)),
                   pl.BlockSpec((None, 8, 128), lambda i: (i, 0, 0))],
        out_shape=[jax.ShapeDtypeStruct((t, dm), BF16), jax.ShapeDtypeStruct((nc, NH_M, DK_M, 2 * DV_M), F32),
                   jax.ShapeDtypeStruct((nc, 8, 128), F32)],
        scratch_shapes=[pltpu.VMEM((NH_M, DK_M, 2 * DV_M), F32), pltpu.VMEM((8, 128), F32)],
        compiler_params=_cparams(("arbitrary",)),
    )(qk, qk, zbig, zbig, zs, zsr, bc, br, gm)


def _mlstm_bwd(qk, zbig, zs, zsr, bc, br, gm, cst, mst, dycat):
    t = zs.shape[0]
    l = LM
    nc = t // l
    dm = NH_M * DV_M

    def body(q_ref, k_ref, v_ref, o_ref, zs_ref, zsr_ref, bc_ref, br_ref, gm_ref, cst_ref, mst_ref, cnx_ref, mnx_ref,
             dy_ref, dqk_ref, dv_ref, do_ref, dzs_ref, dzr_ref, dgm_ref, dc_scr):
        @pl.when(pl.program_id(0) == 0)
        def _():
            dc_scr[...] = jnp.zeros_like(dc_scr)
            dgm_ref[...] = jnp.zeros_like(dgm_ref)

        lane = lax.broadcasted_iota(jnp.int32, (l, 128), 1)
        db_all, sig_c, carries = jnp.zeros((l, 128), F32), jnp.zeros((l, 128), F32), jnp.zeros((1, 128), F32)
        decays = _mlstm_decays(zs_ref, zsr_ref, bc_ref, br_ref)
        lower, upper = decays[2], decays[3]
        dzr_rows = [None] * 8
        dvs, dos, dgs, dqs, dks = [], [], [], [], []
        dzs = jnp.zeros((l, 128), F32)
        for h in range(NH_M):
            c_prev = cst_ref[h]
            m_prev = mst_ref[h:h + 1, 0:1]
            r = _mlstm_chunk(h, q_ref, k_ref, v_ref, zs_ref, zsr_ref, bc_ref, br_ref, c_prev, m_prev, decays)
            hh, mx, den, m_t, v1, amat = r["hh"], r["mx"], r["den"], r["m_t"], r["v1"], r["amat"]
            gh = gm_ref[:, h * DV_M:(h + 1) * DV_M]
            rs = lax.rsqrt(jnp.mean(hh * hh, axis=-1, keepdims=True) + EPS)
            xh = hh * rs
            sg = _sigmoid(o_ref[:, h * DV_M:(h + 1) * DV_M].astype(F32))
            dyh = dy_ref[:, h * DV_M:(h + 1) * DV_M]
            dos.append(dyh * xh * gh * sg * (1.0 - sg))
            dhn = dyh * sg
            dgs.append(_colsum(dhn * xh))
            dxh = dhn * gh
            dh = rs * (dxh - xh * jnp.mean(dxh * xh, axis=-1, keepdims=True))
            g1 = dh / mx
            hd = jnp.sum(hh * dh, axis=-1, keepdims=True)
            dden = jnp.where(jnp.abs(den) > jnp.exp(-m_t), -hd / mx * jnp.sign(den), 0.0)
            g256 = jnp.concatenate([g1, jnp.where(lane == 0, dden, 0.0)], axis=1)
            dc_h = dc_scr[h]
            ea = jnp.exp(r["a_c"])
            dp = _dot(g256, v1, 1, 1)
            ds = dp * amat
            dqs.append((r["w_inter"] * _dot(g256, c_prev, 1, 1) + _dot(ds, r["k"], 1, 0)) * (DK_M ** -0.5))
            dks.append(_dot(ds, r["q"], 0, 0) + ea * _dot(v1, dc_h, 1, 1))
            dv_st = ea * _dot(r["k"], dc_h, 1, 0)
            dv1 = _dot(r["p"], g256, 0, 0) + dv_st
            dvs.append(dv1[:, :DV_M])
            wmat = dp * r["p"]
            c_in = _colsum(wmat)
            c_st = jnp.sum(v1.astype(F32) * dv_st, axis=-1, keepdims=True)
            r_t = jnp.sum(wmat, axis=1, keepdims=True) + jnp.sum(g256 * r["qc_w"], axis=-1, keepdims=True)
            db = r_t - c_st
            carry = jnp.exp(mnx_ref[h:h + 1, 0:1]) * jnp.sum(
                jnp.sum(dc_h * cnx_ref[h], axis=1, keepdims=True), axis=0, keepdims=True)
            db_all = db_all + jnp.where(lane == NH_M + h, db, 0.0)
            sig_c = sig_c + jnp.where(lane == NH_M + h, _sigmoid(-r["fp_c"]), 0.0)
            carries = carries + jnp.where(lane[0:1, :] == NH_M + h, carry, 0.0)
            dzs = dzs + jnp.where(lane == h, c_st, 0.0)
            dzr_rows[h] = c_in
            dzr_rows[NH_M + h] = _sigmoid(-r["fp_r"])
            wq = r["q"] * jnp.exp(r["b_c"] - m_t)
            dc_scr[h] = jnp.exp(r["g"]) * dc_h + _dot(wq, g256, 0, 0)
        dzs = dzs + (_tri_dot(upper, db_all, True) + carries) * sig_c
        c_in4 = jnp.concatenate(dzr_rows[:NH_M], axis=0)
        dlf_r4 = -_tri_dot(c_in4, lower, False)
        dzr_rows = dzr_rows[:NH_M] + [dlf_r4[h:h + 1, :] * dzr_rows[NH_M + h] for h in range(NH_M)]
        dqk_ref[...] = jnp.concatenate(dqs + dks, axis=1)
        dv_ref[...] = jnp.concatenate(dvs, axis=1).astype(dv_ref.dtype)
        do_ref[...] = jnp.concatenate(dos, axis=1).astype(do_ref.dtype)
        dzs_ref[...] = dzs
        dzr_ref[...] = jnp.concatenate(dzr_rows, axis=0)
        dgm_ref[...] += jnp.concatenate(dgs, axis=1)

    rev = lambda i: nc - 1 - i
    nxt = lambda i: jnp.minimum(nc - i, nc - 1)
    return pl.pallas_call(
        body, name="mlstm_bwd", grid=(nc,),
        in_specs=[pl.BlockSpec((l, NH_M * DK_M), lambda i: (rev(i), 0)), pl.BlockSpec((l, NH_M * DK_M), lambda i: (rev(i), 1)),
                  pl.BlockSpec((l, dm), lambda i: (rev(i), 1)), pl.BlockSpec((l, dm), lambda i: (rev(i), 2)),
                  pl.BlockSpec((l, 128), lambda i: (rev(i), 0)), pl.BlockSpec((8, l), lambda i: (0, rev(i))),
                  pl.BlockSpec((1, 8), lambda i: (0, 0)), pl.BlockSpec((8, 1), lambda i: (0, 0)),
                  pl.BlockSpec((1, dm), lambda i: (0, 0)),
                  pl.BlockSpec((None, NH_M, DK_M, 2 * DV_M), lambda i: (rev(i), 0, 0, 0)),
                  pl.BlockSpec((None, 8, 128), lambda i: (rev(i), 0, 0)),
                  pl.BlockSpec((None, NH_M, DK_M, 2 * DV_M), lambda i: (nxt(i), 0, 0, 0)),
                  pl.BlockSpec((None, 8, 128), lambda i: (nxt(i), 0, 0)),
                  pl.BlockSpec((l, dm), lambda i: (rev(i), 0))],
        out_specs=[pl.BlockSpec((l, dm), lambda i: (rev(i), 0)),
                   pl.BlockSpec((l, dm), lambda i: (rev(i), 0)), pl.BlockSpec((l, dm), lambda i: (rev(i), 0)),
                   pl.BlockSpec((l, 128), lambda i: (rev(i), 0)), pl.BlockSpec((8, l), lambda i: (0, rev(i))),
                   pl.BlockSpec((1, dm), lambda i: (0, 0))],
        out_shape=[jax.ShapeDtypeStruct((t, dm), F32),
                   jax.ShapeDtypeStruct((t, dm), BF16), jax.ShapeDtypeStruct((t, dm), BF16),
                   jax.ShapeDtypeStruct((t, 128), F32), jax.ShapeDtypeStruct((8, t), F32),
                   jax.ShapeDtypeStruct((1, dm), F32)],
        scratch_shapes=[pltpu.VMEM((NH_M, DK_M, 2 * DV_M), F32)],
        compiler_params=_cparams(("arbitrary",)),
    )(qk, qk, zbig, zbig, zs, zsr, bc, br, gm, cst, mst, cst, mst, dycat)


def _fox_cumsum(zsr, bf_r):
    t = zsr.shape[1]
    cw = _pick(t, (512, 256))

    def body(z_ref, b_ref, c_ref):
        up = _tri(cw, False).astype(F32)
        carry = jnp.zeros((NH_F, 1), F32)
        for j in range(t // cw):
            cs = _f32dot(_logsig(z_ref[:, j * cw:(j + 1) * cw] + b_ref[...]), up) + carry
            c_ref[:, j * cw:(j + 1) * cw] = cs
            carry = cs[:, cw - 1:cw]

    return pl.pallas_call(
        body, name="fox_cumsum", grid=(1,),
        in_specs=[pl.BlockSpec((NH_F, t), lambda i: (1, 0)), pl.BlockSpec((NH_F, 1), lambda i: (0, 0))],
        out_specs=pl.BlockSpec((NH_F, t), lambda i: (0, 0)), out_shape=jax.ShapeDtypeStruct((NH_F, t), F32),
        compiler_params=_cparams(("arbitrary",)),
    )(zsr, bf_r)


def _fox_gate_bwd(zsr, bf_r, dc):
    t = zsr.shape[1]
    cw = _pick(t, (512, 256))

    def body(z_ref, b_ref, dc_ref, o_ref):
        low = _tri(cw, True).astype(F32)
        carry = jnp.zeros((NH_F, 1), F32)
        for j in reversed(range(t // cw)):
            sl = slice(j * cw, (j + 1) * cw)
            dlf = _f32dot(dc_ref[:, sl], low) + carry
            o_ref[:, sl] = dlf * _sigmoid(-(z_ref[:, sl] + b_ref[...]))
            carry = dlf[:, 0:1]

    return pl.pallas_call(
        body, name="fox_gate_bwd", grid=(1,),
        in_specs=[pl.BlockSpec((NH_F, t), lambda i: (1, 0)), pl.BlockSpec((NH_F, 1), lambda i: (0, 0)),
                  pl.BlockSpec((NH_F, t), lambda i: (0, 0))],
        out_specs=pl.BlockSpec((NH_F, t), lambda i: (0, 0)), out_shape=jax.ShapeDtypeStruct((NH_F, t), F32),
        compiler_params=_cparams(("arbitrary",)),
    )(zsr, bf_r, dc)


def _causal_mask(n):
    return _tri(n, True)


AUG = 64


def _split3(c):
    hi = c.astype(BF16).astype(F32)
    r1 = c - hi
    mid = r1.astype(BF16).astype(F32)
    return hi, mid, r1 - mid


def _fox_prep(zbig, ct):
    t = zbig.shape[0]
    tm = _pick(t, (512, 256))

    def body(q_ref, k_ref, v_ref, c_ref, qo_ref, ko_ref, vo_ref):
        lane = lax.broadcasted_iota(jnp.int32, (tm, AUG), 1)
        qv, kv, vv, cv = q_ref[...], k_ref[...], v_ref[...], c_ref[...]
        one = (lane == 0).astype(BF16)
        for h in range(NH_F):
            hi, mid, lo = _split3(cv[:, h:h + 1])
            aq = jnp.where(lane == 0, hi, jnp.where(lane == 1, mid, jnp.where(lane == 2, lo, jnp.where(lane < 6, 1.0, 0.0))))
            ak = jnp.where(lane < 3, 1.0, jnp.where(lane == 3, -hi, jnp.where(lane == 4, -mid, jnp.where(lane == 5, -lo, 0.0))))
            sl = slice(h * DH_F, (h + 1) * DH_F)
            qo_ref[h] = jnp.concatenate([qv[:, sl] * (DH_F ** -0.5), aq.astype(BF16)], axis=1).astype(BF16)
            ko_ref[h] = jnp.concatenate([kv[:, sl], ak.astype(BF16)], axis=1)
            vo_ref[h] = jnp.concatenate([vv[:, sl], one], axis=1)

    ospec = pl.BlockSpec((NH_F, tm, 128), lambda i: (0, i, 0))
    return pl.pallas_call(
        body, name="fox_prep", grid=(t // tm,),
        in_specs=[pl.BlockSpec((tm, 512), lambda i: (i, 3)), pl.BlockSpec((tm, 512), lambda i: (i, 4)),
                  pl.BlockSpec((tm, 512), lambda i: (i, 5)), pl.BlockSpec((tm, NH_F), lambda i: (i, 0))],
        out_specs=[ospec] * 3, out_shape=[jax.ShapeDtypeStruct((NH_F, t, 128), BF16)] * 3,
        compiler_params=_cparams(("parallel",)),
    )(zbig, zbig, zbig, ct)


def _fox_fwd2(qa, ka, va, gf, plan=None):
    nh, t, _ = qa.shape
    tq = _pick(t, (512, 256))
    nq = t // tq
    group = 4
    host = _Hosted(plan, 4, 3)

    def body(*refs):
        (q_ref, k_ref, v_ref, g_ref), (y_ref, o_ref, lse_ref), _, prefs = host.split(refs)
        i = pl.program_id(0)
        host.run(0, i == 0, prefs)
        host.run(1, i == max(nq - 2, 0), prefs)
        lane = lax.broadcasted_iota(jnp.int32, (tq, 128), 1)
        causal = _causal_mask(tq)
        ys, os_ = [], []
        lse_all = jnp.zeros((tq, 128), F32)
        for h0 in range(0, nh, group):
            heads = range(h0, h0 + group)
            qvs = [q_ref[h] for h in heads]

            def blk(j, carry, masked, heads=heads, qvs=qvs):
                k0 = pl.multiple_of(j * tq, tq)
                out = []
                for (m, acc), h, qv in zip(carry, heads, qvs):
                    s = lax.dot_general(qv, k_ref[h, pl.ds(k0, tq), :], (((1,), (1,)), ((), ())), preferred_element_type=F32)
                    if masked:
                        s = jnp.where(causal, s, -jnp.inf)
                    m_new = jnp.maximum(m, jnp.max(s, axis=1, keepdims=True))
                    p = jnp.exp(s - m_new).astype(BF16)
                    pv = lax.dot_general(p, v_ref[h, pl.ds(k0, tq), :], (((1,), (0,)), ((), ())), preferred_element_type=F32)
                    out.append((m_new, jnp.exp(m - m_new) * acc + pv))
                return tuple(out)

            init = tuple((jnp.full((tq, 1), -jnp.inf, F32), jnp.zeros((tq, 128), F32)) for _ in heads)
            carry = lax.fori_loop(0, i, lambda j, c: blk(j, c, False), init)
            for (m, acc), h in zip(blk(i, carry, True), heads):
                l = acc[:, DH_F:DH_F + 1]
                o = acc[:, :DH_F] / l
                os_.append(o)
                gh = g_ref[:, h * DH_F:(h + 1) * DH_F]
                ys.append(o * lax.rsqrt(jnp.mean(o * o, axis=-1, keepdims=True) + EPS) * gh)
                lse_all = lse_all + jnp.where(lane == h, m + jnp.log(l), 0.0)
        y_ref[...] = jnp.concatenate(ys, axis=1).astype(y_ref.dtype)
        o_ref[...] = jnp.concatenate(os_, axis=1)
        lse_ref[...] = lse_all
        host.run(2, i == nq - 1, prefs)

    full = pl.BlockSpec((nh, t, 128), lambda i: (0, 0, 0))
    hc = host.call_args()
    res = pl.pallas_call(
        body, name="fox_fwd", grid=(nq,),
        in_specs=[pl.BlockSpec((nh, tq, 128), lambda i: (0, i, 0)), full, full, pl.BlockSpec((1, nh * DH_F), lambda i: (0, 0))]
        + hc["in_specs"],
        out_specs=[pl.BlockSpec((tq, nh * DH_F), lambda i: (i, 0)), pl.BlockSpec((tq, nh * DH_F), lambda i: (i, 0)),
                   pl.BlockSpec((tq, 128), lambda i: (i, 0))] + hc["out_specs"],
        out_shape=[jax.ShapeDtypeStruct((t, nh * DH_F), BF16), jax.ShapeDtypeStruct((t, nh * DH_F), F32),
                   jax.ShapeDtypeStruct((t, 128), F32)] + hc["out_shape"],
        scratch_shapes=hc["scratch"], input_output_aliases=hc["aliases"], compiler_params=_cparams(("arbitrary",)),
    )(qa, ka, va, gf, *hc["args"])
    return res[:3], res[3:]


def _fox_bwd_prep(dycat, o, gf):
    t = o.shape[0]
    tm = _pick(t, (512, 256))

    def body(dy_ref, o_ref, g_ref, do_ref, dl_ref, dg_ref):
        lane = lax.broadcasted_iota(jnp.int32, (tm, 128), 1)
        dyv, ov, gv = dy_ref[...], o_ref[...], g_ref[...]
        dgs = []
        dl = jnp.zeros((tm, 128), F32)
        pad = jnp.zeros((tm, AUG), BF16)
        for h in range(NH_F):
            sl = slice(h * DH_F, (h + 1) * DH_F)
            dx, dg = _rms_bwd_val(dyv[:, sl], ov[:, sl], gv[:, sl])
            dgs.append(dg)
            do_ref[h] = jnp.concatenate([dx.astype(BF16), pad], axis=1)
            dl = dl + jnp.where(lane == h, jnp.sum(dx * ov[:, sl], axis=-1, keepdims=True), 0.0)
        dl_ref[...] = dl

        @pl.when(pl.program_id(0) == 0)
        def _():
            dg_ref[...] = jnp.zeros_like(dg_ref)
        dg_ref[...] += jnp.concatenate(dgs, axis=1)

    return pl.pallas_call(
        body, name="fox_bwd_prep", grid=(t // tm,),
        in_specs=[pl.BlockSpec((tm, 512), lambda i: (i, 1)), pl.BlockSpec((tm, 512), lambda i: (i, 0)),
                  pl.BlockSpec((1, 512), lambda i: (0, 0))],
        out_specs=[pl.BlockSpec((NH_F, tm, 128), lambda i: (0, i, 0)), pl.BlockSpec((tm, 128), lambda i: (i, 0)),
                   pl.BlockSpec((1, 512), lambda i: (0, 0))],
        out_shape=[jax.ShapeDtypeStruct((NH_F, t, 128), BF16), jax.ShapeDtypeStruct((t, 128), F32),
                   jax.ShapeDtypeStruct((1, 512), F32)],
        compiler_params=_cparams(("arbitrary",)),
    )(dycat, o, gf)


def _fox_bwd2(qa, ka, va, doa, lse, delta, plan=None):
    nh, t, _ = qa.shape
    tq = _pick(t, (512, 256))
    nq = t // tq

    group = 2

    def tdot(a, b, cb):
        return lax.dot_general(a, b, (((0,), (cb,)), ((), ())), preferred_element_type=F32)

    host = _Hosted(plan, 6, 3)
    ng = nh // group

    def body(*refs):
        (q_ref, k_ref, v_ref, do_ref, lse_ref, dl_ref), (dq_ref, dk_ref, dv_ref), _, prefs = host.split(refs)
        hp, j = pl.program_id(0), pl.program_id(1)
        host.run(0, (hp == 0) & (j == 0), prefs)
        host.run(1, (hp == 0) & (j == 0), prefs)

        @pl.when(j == 0)
        def _():
            dq_ref[...] = jnp.zeros_like(dq_ref)

        lane = lax.broadcasted_iota(jnp.int32, (tq, 128), 1)
        causal = _causal_mask(tq)

        def blk(i, carry, masked):
            rows = pl.ds(pl.multiple_of(i * tq, tq), tq)
            lse_t, dl_t = lse_ref[rows, :], dl_ref[rows, :]
            out = []
            for g, (dk, dv) in enumerate(carry):
                h = hp * group + g
                kb, vb = k_ref[g], v_ref[g]
                qb, dob = q_ref[g, rows, :], do_ref[g, rows, :]
                lse_h = jnp.sum(jnp.where(lane == h, lse_t, 0.0), axis=1, keepdims=True)
                dl_h = jnp.sum(jnp.where(lane == h, dl_t, 0.0), axis=1, keepdims=True)
                s = lax.dot_general(qb, kb, (((1,), (1,)), ((), ())), preferred_element_type=F32)
                if masked:
                    s = jnp.where(causal, s, -jnp.inf)
                p = jnp.exp(s - lse_h)
                dp = lax.dot_general(dob, vb, (((1,), (1,)), ((), ())), preferred_element_type=F32)
                ds = (p * (dp - dl_h)).astype(BF16)
                dv = dv + tdot(dob, p.astype(BF16), 0)
                dk = dk + tdot(qb, ds, 0)
                dq_ref[g, :, rows] += tdot(kb, ds, 1)
                out.append((dk, dv))
            return tuple(out)

        init = tuple((jnp.zeros((128, tq), F32), jnp.zeros((128, tq), F32)) for _ in range(group))
        carry = blk(j, init, True)
        carry = lax.fori_loop(j + 1, nq, lambda i, c: blk(i, c, False), carry)
        for g, (dk, dv) in enumerate(carry):
            dk_ref[g] = dk
            dv_ref[g] = dv
        host.run(2, (hp == ng - 1) & (j == nq - 1), prefs)

    full = pl.BlockSpec((group, t, 128), lambda h, j: (h, 0, 0))
    tile = pl.BlockSpec((group, tq, 128), lambda h, j: (h, j, 0))
    cols = pl.BlockSpec((t, 128), lambda h, j: (0, 0))
    full_t = pl.BlockSpec((group, 128, t), lambda h, j: (h, 0, 0))
    tile_t = pl.BlockSpec((group, 128, tq), lambda h, j: (h, 0, j))
    hc = host.call_args()
    res = pl.pallas_call(
        body, name="fox_bwd", grid=(ng, nq), in_specs=[full, tile, tile, full, cols, cols] + hc["in_specs"],
        out_specs=[full_t, tile_t, tile_t] + hc["out_specs"],
        out_shape=[jax.ShapeDtypeStruct((nh, 128, t), F32)] * 3 + hc["out_shape"], scratch_shapes=hc["scratch"],
        input_output_aliases=hc["aliases"], compiler_params=_cparams(("arbitrary", "arbitrary")),
    )(qa, ka, va, doa, lse, delta, *hc["args"])
    return res[:3], res[3:]


def _fox_bwd_post(dqa, dka, dva):
    nh, _, t = dqa.shape
    tm = _pick(t, (512, 256))

    def body(dq_ref, dk_ref, dv_ref, oq_ref, ok_ref, ov_ref, dc_ref):
        qs, ks, vs, dcs = [], [], [], []
        for h in range(nh):
            dq, dk = dq_ref[h], dk_ref[h]
            qs.append(dq.T[:, :DH_F] * (DH_F ** -0.5))
            ks.append(dk.T[:, :DH_F])
            vs.append(dv_ref[h].T[:, :DH_F])
            dcs.append(dq[DH_F:DH_F + 1, :] - dk[DH_F + 3:DH_F + 4, :])
        oq_ref[...] = jnp.concatenate(qs, axis=1).astype(BF16)
        ok_ref[...] = jnp.concatenate(ks, axis=1).astype(BF16)
        ov_ref[...] = jnp.concatenate(vs, axis=1).astype(BF16)
        dc_ref[...] = jnp.concatenate(dcs, axis=0)

    ispec = pl.BlockSpec((nh, 128, tm), lambda i: (0, 0, i))
    ospec = pl.BlockSpec((tm, nh * DH_F), lambda i: (i, 0))
    return pl.pallas_call(
        body, name="fox_bwd_post", grid=(t // tm,), in_specs=[ispec] * 3,
        out_specs=[ospec] * 3 + [pl.BlockSpec((nh, tm), lambda i: (0, i))],
        out_shape=[jax.ShapeDtypeStruct((t, nh * DH_F), BF16)] * 3 + [jax.ShapeDtypeStruct((nh, t), F32)],
        compiler_params=_cparams(("parallel",)),
    )(dqa, dka, dva)


IN_OFF = (0, 512, 1024, 1544, 2056, 2568)
IN_GATES = (1536, 3080)


FFN1 = ("ffn1_w_gate", "ffn1_w_up", "ffn1_w_down")
REST = ("w_in", "w_out", "ffn2_w_gate", "ffn2_w_up", "ffn2_w_down", "w_ple_gate", "w_ple_proj")
SPLIT = {n: 1 if n == "w_in" else 0 for n in FFN1 + REST}
SAME_SHAPE = (FFN1, ("ffn2_w_gate", "ffn2_w_up", "ffn2_w_down"), ("w_out", "w_ple_gate"), ("w_in",), ("w_ple_proj",))


def _grouped(names):
    return [tuple(n for n in grp if n in names) for grp in SAME_SHAPE if any(n in names for n in grp)]


def _rs_partials(names, gw, c_idx, twins, run_swap=None):
    wire = [twins[n] if n in twins else _cast_other_half("rs_cast_" + n, gw[n], c_idx, SPLIT[n]) for n in names]
    plan = _swap_plan(wire, [SPLIT[n] if n in twins else None for n in names])
    swapped = dict(zip(names, run_swap(plan) if run_swap else _run_plan("rs_swap_" + names[0], plan)))
    out = {}
    for grp in _grouped(names):
        res = _add_my_half("rs_add_" + grp[0], [gw[n] for n in grp], [swapped[n] for n in grp], c_idx, SPLIT[grp[0]])
        out.update(zip(grp, res))
    return [out[n] for n in names]


def _local_step(x, p, tgt, sp, wg1, wu1, wd1, rest_slots, c_idx, place):
    t, d = x.shape
    slot = dict(zip(REST + ("conv_qk",), rest_slots))
    (h1, xn1, g1, u1), (w_in, conv_w) = _ffn_fwd(
        "ffn1", x, sp["ffn1_norm"], wg1, wu1, wd1, plan=_gather_plan([slot["w_in"], slot["conv_qk"]], [SPLIT["w_in"], None]))
    w_in, conv_w = w_in.reshape(-1, d), _from_chip_blocks(conv_w)
    w_big = jnp.concatenate([w_in[o:o + 512] for o in IN_OFF], axis=0)
    w_small = jnp.concatenate([w_in[IN_GATES[0]:IN_GATES[0] + 8], w_in[IN_GATES[1]:IN_GATES[1] + 8],
                               jnp.zeros((112, d), w_in.dtype)], axis=0)
    u, zbig = _norm_mm("in_big", h1, sp["mix_norm"], w_big, True, BF16)
    zs = _mm_nt("in_small", u, w_small, tm=1024, tk=1024)
    zsr = zs.T
    qk_act = _conv_fwd(zbig, conv_w)
    bm_c, bf_c = sp["b_mlstm_gates"], sp["b_fox_f"]
    y_m, cst, mst = _mlstm_fwd(qk_act, zbig, zs, zsr, bm_c, bm_c.T, sp["mlstm_out_norm"])
    c = _fox_cumsum(zsr, bf_c.T)
    qa, ka, va = _fox_prep(zbig, c.T)
    (y_ft, o_f, lse), late = _fox_fwd2(qa, ka, va, sp["fox_out_norm"],
                                       plan=_gather_plan([slot[n] for n in REST[1:]], [SPLIT[n] for n in REST[1:]]))
    full = dict(zip(REST[1:], late))
    w_out, w_pg = (full[n].reshape(-1, d) for n in ("w_out", "w_ple_gate"))
    wg2, wu2, wd2 = full["ffn2_w_gate"], full["ffn2_w_up"], full["ffn2_w_down"]
    w_pp = _from_chip_blocks(full["w_ple_proj"])
    tm = _pick(t, (1024, 512, 256))
    h2 = _mm("out_proj", [(y_m, (tm, 512), lambda i, j, k: (i, 0), w_out, (512, d), lambda i, j, k: (0, 0)),
                          (y_ft, (tm, 512), lambda i, j, k: (i, 0), w_out, (512, d), lambda i, j, k: (1, 0))],
             (t, d), (tm, d), lambda i, j, k: (i, 0), (t // tm, 1, 1), 2, res=h1)
    (h3, xn2, g2, u2), _ = _ffn_fwd("ffn2", h2, sp["ffn2_norm"], wg2, wu2, wd2)
    hn3, gate_pre = _norm_mm("ple_gate", h3, sp["ple_gate_norm"], w_pg, False, F32)
    pp = _mm_nn("ple_proj", p, w_pp, tm=1024)

    def head_fn(h3_t, gp_t, pp_t, tgt_t, g_pp, g_fin):
        gate = _sigmoid(gp_t)
        ppn = _rms_fwd_val(pp_t, g_pp)
        h4 = h3_t + gate * ppn
        err = _rms_fwd_val(h4, g_fin) - tgt_t
        loss = 0.5 * jnp.sum(jnp.mean(err * err, axis=-1, keepdims=True), axis=0, keepdims=True)
        dh4, dg_fin = _rms_bwd_val(err * (1.0 / d), h4, g_fin)
        dpp, dg_pp = _rms_bwd_val(dh4 * gate, pp_t, g_pp)
        dgp = dh4 * ppn * gate * (1.0 - gate)
        return dh4, dgp, dpp, jnp.broadcast_to(loss, (1, 128)), dg_fin, dg_pp

    dh4, dgp, dpp, loss_part, dg_fin, dg_pp = _rowwise(
        "loss_head", head_fn, [h3, gate_pre, pp, tgt], [sp["ple_proj_norm"], sp["final_norm"]],
        [(d, F32), (d, BF16), (d, BF16)], [((1, 128), F32), ((1, d), F32), ((1, d), F32)])
    gw, gs = {}, {"final_norm": dg_fin, "ple_proj_norm": dg_pp}
    gw["w_ple_gate"] = _mm_tn("d_w_pg", hn3, dgp, tm=1024, tn=1024)
    gw["w_ple_proj"] = _mm_tn("d_w_pp", p, dpp, tn=1024)
    dhn3 = _mm_nt("d_hn3", dgp, w_pg, tm=1024, tn=1024, tk=1024)

    def res_norm_bwd(dn_t, h_t, dres_t, g):
        dx, dg = _rms_bwd_val(dn_t, h_t, g)
        return dres_t + dx, dg

    dh3, gs["ple_gate_norm"] = _rowwise("ple_norm_bwd", res_norm_bwd, [dhn3, h3, dh4], [sp["ple_gate_norm"]],
                                        [(d, F32)], [((1, d), F32)])
    (dh2, gs["ffn2_norm"], gw["ffn2_w_gate"], gw["ffn2_w_up"], gw["ffn2_w_down"]), _, twins2 = _ffn_bwd(
        "ffn2", dh3, h2, sp["ffn2_norm"], xn2, g2, u2, wg2, wu2, wd2)
    ffn2_names = ("ffn2_w_gate", "ffn2_w_up", "ffn2_w_down")
    early = []

    def swap_in_d_ycat(plan):
        dyc, swapped = _mm_nt("d_ycat", dh2, w_out, tm=1024, tn=1024, tk=1024, plan=plan)
        early.append(dyc)
        return swapped

    part_ffn2 = dict(zip(ffn2_names, _rs_partials(ffn2_names, gw, c_idx, dict(zip(ffn2_names, twins2)), swap_in_d_ycat)))
    dycat = early[0]
    gw["w_out"] = jnp.concatenate([_mm_tn("d_w_out_m", y_m, dh2, tn=1024, tk=2048),
                                   _mm_tn("d_w_out_f", y_ft, dh2, tn=1024, tk=2048)], axis=0)
    doa, delta, gs["fox_out_norm"] = _fox_bwd_prep(dycat, o_f, sp["fox_out_norm"])
    dqkv_t, landed_ffn2 = _fox_bwd2(qa, ka, va, doa, lse, delta, plan=_scatter_plan([part_ffn2[n][1] for n in ffn2_names]))
    dq_f, dk_f, dv_f, dct = _fox_bwd_post(*dqkv_t)
    dfp = _fox_gate_bwd(zsr, bf_c.T, dct)
    dact, dv_m, do_m, dzs_m, dzr_m, gs["mlstm_out_norm"] = _mlstm_bwd(
        qk_act, zbig, zs, zsr, bm_c, bm_c.T, sp["mlstm_out_norm"], cst, mst, dycat)
    dqk, gw["conv_qk"] = _conv_bwd(zbig, dact, conv_w)
    dz_big = jnp.concatenate([dqk, dv_m, do_m, dq_f, dk_f, dv_f], axis=1)
    dzs = dzs_m + jnp.pad(jnp.concatenate([dzr_m, dfp], axis=0).T, ((0, 0), (0, 112)))
    dw_big = _mm_tn("d_w_big", dz_big, u, tn=1024)
    dw_small = _mm_tn("d_w_small", dzs, u, tn=1024)
    gw["w_in"] = jnp.concatenate([dw_big[0:1536], dw_small[0:8], dw_big[1536:3072], dw_small[8:16]], axis=0)
    du_a = _mm_nn("d_u_big", dz_big, w_big, tm=1024, tn=1024, tk=1024)
    du_b = _mm_nn("d_u_small", dzs, w_small, tm=1024, tn=1024)

    def mix_norm_bwd(da_t, db_t, h_t, dres_t, dzs_t, g):
        dx, dg = _rms_bwd_val(da_t + db_t, h_t, g)
        return dres_t + dx, dg, _colsum(dzs_t)

    conv_grad = gw.pop("conv_qk")
    gw["w_ple_proj"] = _chip_blocks(gw["w_ple_proj"])
    for n in ("w_in", "w_out", "w_ple_gate"):
        gw[n] = gw[n].reshape(4, -1, gw[n].shape[-1])
    mix = []

    def swap_in_mix_norm_bwd(plan):
        res, swapped = _rowwise("mix_norm_bwd", mix_norm_bwd, [du_a, du_b, h1, dh2, dzs], [sp["mix_norm"]],
                                [(d, F32)], [((1, d), F32), ((1, 128), F32)], plan=plan)
        mix.extend(res)
        return swapped

    light = ("w_in", "w_out", "w_ple_gate", "w_ple_proj")
    part_light = dict(zip(light, _rs_partials(light, gw, c_idx, {}, swap_in_mix_norm_bwd)))
    dh1, gs["mix_norm"], dbias = mix
    gs["b_mlstm_gates"], gs["b_fox_f"] = dbias[:, 0:8], dbias[:, 8:16]
    own = {}

    def swap_down(dwd, twin):
        own["dwd"] = dwd
        return _swap_plan([twin], [SPLIT["ffn1_w_down"]])

    def scatter_down(swapped):
        own["part_down"] = _add_my_half("rs_add_ffn1_w_down", [own["dwd"]], [swapped[0]], c_idx, SPLIT["ffn1_w_down"])[0]
        return _scatter_plan([own["part_down"][1]])

    def scatter_gate_up(landed_down, dws, dw_twins):
        own["landed_down"] = landed_down[0]
        own["part_gu"] = _rs_partials(FFN1[:2], dict(zip(FFN1[:2], dws)), c_idx, dict(zip(FFN1[:2], dw_twins)))
        return _scatter_plan([pb for _, pb in own["part_gu"]])

    (grad_x, gs["ffn1_norm"], _, _, _), (l_light, _, _, _, landed_gu) = _ffn_bwd_late_dx(
        "ffn1", dh1, x, sp["ffn1_norm"], xn1, g1, u1, wg1, wu1, wd1,
        _scatter_plan([part_light[n][1] for n in light]), None, (swap_down, scatter_down, scatter_gate_up))
    names = REST + FFN1
    parts = {**part_light, **part_ffn2, **dict(zip(FFN1[:2], own["part_gu"])), "ffn1_w_down": own["part_down"]}
    landed = {**dict(zip(light, l_light)), **dict(zip(ffn2_names, landed_ffn2)), **dict(zip(FFN1[:2], landed_gu)),
              "ffn1_w_down": own["landed_down"]}
    mine = {}
    for grp in _grouped(names):
        res = _sum4("rs_sum_" + grp[0], [landed[n] for n in grp], [parts[n][0] for n in grp], place, SPLIT[grp[0]])
        mine.update(zip(grp, res))
    grads = dict(zip(names, _join_halves("rs_join", [mine[n] for n in names], [SPLIT[n] for n in names])))
    return loss_part, grad_x, grads, gs, conv_grad


ANY = pl.BlockSpec(memory_space=pl.ANY)
MESH = pl.DeviceIdType.MESH


def _place():
    x, y, c = lax.axis_index("x"), lax.axis_index("y"), lax.axis_index("c")
    chips = [(1 - x, y), (x, 1 - y), (1 - x, 1 - y)]
    return x, y, c, 2 * x + y, (x, y, 1 - c), chips


def _rcopy(src, dst, ssem, rsem, dev):
    return pltpu.make_async_remote_copy(src_ref=src, dst_ref=dst, send_sem=ssem, recv_sem=rsem, device_id=dev,
                                        device_id_type=MESH)


def _half(ref, lead, axis, idx, half):
    return ref.at[(slice(None),) * (lead + axis) + (pl.ds(idx * half, half),)]


def _to_slot(name, arrs, me_idx, dtype):
    n = len(arrs)
    r, cdim = arrs[0].shape
    tr = _pick(r, (352, 256, 176, 128, 64))

    def body(me_ref, *refs):
        for k in range(n):
            refs[n + k][...] = refs[k][...].astype(dtype)

    return pl.pallas_call(
        body, name=name,
        grid_spec=pltpu.PrefetchScalarGridSpec(
            num_scalar_prefetch=1, grid=(r // tr,), in_specs=[pl.BlockSpec((tr, cdim), lambda i, me_ref: (i, 0))] * n,
            out_specs=[pl.BlockSpec((None, tr, cdim), lambda i, me_ref: (me_ref[0], i, 0))] * n),
        out_shape=[jax.ShapeDtypeStruct((4, r, cdim), dtype)] * n, compiler_params=_cparams(("parallel",)),
    )(me_idx, *arrs)


def _gather4(name, bufs, split):
    return _run_plan(name, _gather_plan(bufs, split))


def _gather_plan(bufs, split):
    n = len(bufs)
    shapes = [b.shape[1:] for b in bufs]

    def ctx(outs):
        x, y, c, me, sib, chips = _place()

        def part(ref, a, which):
            if split[a] is None:
                return ref
            return _half(ref, 0, split[a], which, shapes[a][split[a]] // 2)

        return c, me, sib, chips, part

    def ici(outs, sems, a, j, chip, c, me, part):
        mine = part(outs[a].at[me], a, c)
        return _rcopy(mine, mine, sems[0].at[3 * a + j], sems[1].at[3 * a + j], (*chip, c))

    def fwd(outs, sems, a, j, chip, c, sib, part, which):
        blk = part(outs[a].at[2 * chip[0] + chip[1]], a, which)
        return _rcopy(blk, blk, sems[2].at[3 * a + j], sems[3].at[3 * a + j], sib)

    def start(ins, outs, sems):
        c, me, sib, chips, part = ctx(outs)
        for a in range(n):
            for j, chip in enumerate(chips):
                ici(outs, sems, a, j, chip, c, me, part).start()

    def mid(ins, outs, sems):
        c, me, sib, chips, part = ctx(outs)
        for j, chip in enumerate(chips):
            for a in range(n):
                blk = part(outs[a].at[2 * chip[0] + chip[1]], a, c)
                _rcopy(blk, blk, sems[0].at[3 * a + j], sems[1].at[3 * a + j], sib).wait_recv()
                if split[a] is not None:
                    fwd(outs, sems, a, j, chip, c, sib, part, c).start()

    def end(ins, outs, sems):
        c, me, sib, chips, part = ctx(outs)
        for j, chip in enumerate(chips):
            for a in range(n):
                if split[a] is not None:
                    fwd(outs, sems, a, j, chip, c, sib, part, 1 - c).wait_recv()
        for a in range(n):
            for j, chip in enumerate(chips):
                ici(outs, sems, a, j, chip, c, me, part).wait_send()
                if split[a] is not None:
                    fwd(outs, sems, a, j, chip, c, sib, part, c).wait_send()

    return dict(ins=list(bufs), outs=[jax.ShapeDtypeStruct(b.shape, b.dtype) for b in bufs], alias=True,
                sems=[pltpu.SemaphoreType.DMA((3 * n,))] * 4, phases=(start, mid, end))


def _run_plan(name, plan):
    ni, no = len(plan["ins"]), len(plan["outs"])

    def body(*refs):
        ins, outs, sems = refs[:ni], refs[ni:ni + no], refs[ni + no:]
        for phase in plan["phases"]:
            phase(ins, outs, sems)

    return pl.pallas_call(
        body, name=name, in_specs=[ANY] * ni, out_specs=[ANY] * no, out_shape=plan["outs"],
        input_output_aliases={a: a for a in range(ni)} if plan["alias"] else {}, scratch_shapes=plan["sems"],
    )(*plan["ins"])


class _Hosted:
    def __init__(self, plan, n_in, n_out):
        self.plan, self.n_in, self.n_out = plan, n_in, n_out
        self.ni, self.no, self.ns = (len(plan["ins"]) if plan else 0, len(plan["outs"]) if plan else 0,
                                     len(plan["sems"]) if plan else 0)

    def split(self, refs):
        a, b = self.n_in, self.n_in + self.ni
        c, d = b + self.n_out, b + self.n_out + self.no
        e = len(refs) - self.ns
        return refs[:a], refs[b:c], refs[d:e], (refs[a:b], refs[c:d], refs[e:])

    def run(self, k, cond, prefs):
        if self.plan is not None:
            @pl.when(cond)
            def _():
                self.plan["phases"][k](*prefs)

    def call_args(self):
        p = self.plan
        if p is None:
            return dict(in_specs=[], out_specs=[], out_shape=[], scratch=[], aliases={}, args=[])
        al = {self.n_in + a: self.n_out + a for a in range(self.ni)} if p["alias"] else {}
        return dict(in_specs=[ANY] * self.ni, out_specs=[ANY] * self.no, out_shape=list(p["outs"]), scratch=list(p["sems"]),
                    aliases=al, args=list(p["ins"]))


def _swap(name, arrs, halve):
    return _run_plan(name, _swap_plan(arrs, halve))


def _swap_plan(arrs, halve):
    n = len(arrs)

    def half_shape(a, ax):
        return a.shape if ax is None else (a.shape[0],) + tuple(d // 2 if i == ax else d for i, d in enumerate(a.shape[1:]))

    def copies(ins, outs, sems):
        x, y, c, me, sib, chips = _place()
        cps = []
        for a in range(n):
            src = ins[a] if halve[a] is None else _half(ins[a], 1, halve[a], 1 - c, arrs[a].shape[1 + halve[a]] // 2)
            cps.append(_rcopy(src, outs[a], sems[0].at[a], sems[1].at[a], sib))
        return cps

    def start(ins, outs, sems):
        for cp in copies(ins, outs, sems):
            cp.start()

    def mid(ins, outs, sems):
        pass

    def end(ins, outs, sems):
        for cp in copies(ins, outs, sems):
            cp.wait()

    return dict(ins=list(arrs), outs=[jax.ShapeDtypeStruct(half_shape(a, ax), a.dtype) for a, ax in zip(arrs, halve)],
                alias=False, sems=[pltpu.SemaphoreType.DMA((n,))] * 2, phases=(start, mid, end))


def _scatter4(name, arrs):
    return _run_plan(name, _scatter_plan(arrs))


def _scatter_plan(arrs):
    n = len(arrs)

    def send(ins, outs, sems, a, j, chip, c, me):
        return _rcopy(ins[a].at[2 * chip[0] + chip[1]], outs[a].at[me], sems[0].at[3 * a + j], sems[1].at[3 * a + j], (*chip, c))

    def start(ins, outs, sems):
        x, y, c, me, sib, chips = _place()
        for a in range(n):
            for j, chip in enumerate(chips):
                send(ins, outs, sems, a, j, chip, c, me).start()

    def mid(ins, outs, sems):
        pass

    def end(ins, outs, sems):
        x, y, c, me, sib, chips = _place()
        for a in range(n):
            for j, chip in enumerate(chips):
                blk = outs[a].at[2 * chip[0] + chip[1]]
                _rcopy(blk, blk, sems[0].at[3 * a + j], sems[1].at[3 * a + j], sib).wait_recv()
        for a in range(n):
            for j, chip in enumerate(chips):
                send(ins, outs, sems, a, j, chip, c, me).wait_send()

    return dict(ins=list(arrs), outs=[jax.ShapeDtypeStruct(a.shape, a.dtype) for a in arrs], alias=False,
                sems=[pltpu.SemaphoreType.DMA((3 * n,))] * 2, phases=(start, mid, end))


def _join_halves(name, arrs, split):
    n = len(arrs)

    def body(*refs):
        outs = refs[n:2 * n]
        ssem, rsem = refs[2 * n:]
        x, y, c, me, sib, chips = _place()
        cps = []
        for a in range(n):
            mine = _half(outs[a], 0, split[a], c, arrs[a].shape[split[a]] // 2)
            cp = _rcopy(mine, mine, ssem.at[a], rsem.at[a], sib)
            cp.start()
            cps.append(cp)
        for a in range(n):
            blk = _half(outs[a], 0, split[a], 1 - c, arrs[a].shape[split[a]] // 2)
            _rcopy(blk, blk, ssem.at[a], rsem.at[a], sib).wait_recv()
        for cp in cps:
            cp.wait_send()

    return pl.pallas_call(
        body, name=name, in_specs=[ANY] * n, out_specs=[ANY] * n,
        out_shape=[jax.ShapeDtypeStruct(a.shape, a.dtype) for a in arrs],
        input_output_aliases={a: a for a in range(n)}, scratch_shapes=[pltpu.SemaphoreType.DMA((n,))] * 2,
    )(*arrs)


def _allreduce_small(s):
    r, cdim = s.shape

    def body(s_ref, o_ref, buf, ssem, rsem):
        x, y, c, me, sib, chips = _place()
        me8 = 4 * x + 2 * y + c
        buf[me8] = s_ref[...]
        flips = [(fx, fy, fc) for fx in (0, 1) for fy in (0, 1) for fc in (0, 1)][1:]
        cps = []
        for k, (fx, fy, fc) in enumerate(flips):
            peer = (x ^ fx if fx else x, y ^ fy if fy else y, c ^ fc if fc else c)
            cp = _rcopy(s_ref, buf.at[me8], ssem.at[k], rsem.at[k], peer)
            cp.start()
            cps.append(cp)
        for k, (fx, fy, fc) in enumerate(flips):
            src = 4 * (x ^ fx if fx else x) + 2 * (y ^ fy if fy else y) + (c ^ fc if fc else c)
            _rcopy(s_ref, buf.at[src], ssem.at[k], rsem.at[k], sib).wait_recv()
        for cp in cps:
            cp.wait_send()
        acc = buf[0]
        for k in range(1, 8):
            acc = acc + buf[k]
        o_ref[...] = acc

    vm = pl.BlockSpec(memory_space=pltpu.VMEM)
    return pl.pallas_call(
        body, name="allreduce_small", in_specs=[vm], out_specs=vm, out_shape=jax.ShapeDtypeStruct((r, cdim), F32),
        scratch_shapes=[pltpu.VMEM((8, r, cdim), F32), pltpu.SemaphoreType.DMA((7,)), pltpu.SemaphoreType.DMA((7,))],
    )(s)


def _add_my_half(name, gs, recvs, c_idx, axis):
    n = len(gs)
    nb, hr, hc = recvs[0].shape
    tr = _pick(hr, (256, 176, 128, 64))
    if axis == 0:
        g4s = [g.reshape(nb, 2, hr, hc) for g in gs]
        gspec = pl.BlockSpec((None, None, tr, hc), lambda b, i, c_ref: (b, c_ref[0], i, 0))
    else:
        g4s = list(gs)
        gspec = pl.BlockSpec((None, tr, hc), lambda b, i, c_ref: (b, i, c_ref[0]))

    def body(c_ref, *refs):
        for k in range(n):
            s = refs[k][...] + refs[n + k][...].astype(F32)
            refs[2 * n + 2 * k][...] = s
            refs[2 * n + 2 * k + 1][...] = s.astype(BF16)

    ospec = pl.BlockSpec((None, tr, hc), lambda b, i, c_ref: (b, i, 0))
    res = pl.pallas_call(
        body, name=name,
        grid_spec=pltpu.PrefetchScalarGridSpec(
            num_scalar_prefetch=1, grid=(nb, hr // tr), in_specs=[gspec] * n + [ospec] * n, out_specs=[ospec] * (2 * n)),
        out_shape=[jax.ShapeDtypeStruct((nb, hr, hc), F32), jax.ShapeDtypeStruct((nb, hr, hc), BF16)] * n,
        compiler_params=_cparams(("parallel", "parallel")),
    )(c_idx, *g4s, *recvs)
    return [(res[2 * k], res[2 * k + 1]) for k in range(n)]


def _sum4(name, landeds, owns, place, axis):
    n = len(landeds)
    nb, h, cdim = landeds[0].shape
    tr = _pick(h, (256, 176, 128, 64))
    nt = h // tr

    def body(p_ref, *refs):
        for k in range(n):
            a1, a2, a3, own = refs[4 * k:4 * k + 4]
            refs[4 * n + k][...] = ((own[...] + a1[...].astype(F32)) + a2[...].astype(F32)) + a3[...].astype(F32)

    def nxt(k):
        return pl.BlockSpec((None, tr, cdim), lambda i, p_ref: ((p_ref[0] + k) % nb, i, 0))

    if axis == 0:
        ospec = pl.BlockSpec((tr, cdim), lambda i, p_ref: (p_ref[1] * nt + i, 0))
        oshape = (2 * h, cdim)
    else:
        ospec = pl.BlockSpec((tr, cdim), lambda i, p_ref: (i, p_ref[1]))
        oshape = (h, 2 * cdim)
    args = []
    for landed, own in zip(landeds, owns):
        args += [landed, landed, landed, own]
    return pl.pallas_call(
        body, name=name,
        grid_spec=pltpu.PrefetchScalarGridSpec(
            num_scalar_prefetch=1, grid=(nt,), in_specs=[nxt(1), nxt(2), nxt(3), nxt(0)] * n, out_specs=[ospec] * n),
        out_shape=[jax.ShapeDtypeStruct(oshape, F32)] * n, compiler_params=_cparams(("parallel",)),
    )(place, *args)


def _cast_other_half(name, g, c_idx, axis):
    nb, r, cdim = g.shape
    hr, hc = (r // 2, cdim) if axis == 0 else (r, cdim // 2)
    tr = _pick(hr, (256, 176, 128, 64))
    if axis == 0:
        g4 = g.reshape(nb, 2, hr, hc)
        gspec = pl.BlockSpec((None, None, tr, hc), lambda b, i, c_ref: (b, 1 - c_ref[0], i, 0))
    else:
        g4 = g
        gspec = pl.BlockSpec((None, tr, hc), lambda b, i, c_ref: (b, i, 1 - c_ref[0]))

    def body(c_ref, g_ref, o_ref):
        o_ref[...] = g_ref[...].astype(BF16)

    return pl.pallas_call(
        body, name=name,
        grid_spec=pltpu.PrefetchScalarGridSpec(
            num_scalar_prefetch=1, grid=(nb, hr // tr), in_specs=[gspec],
            out_specs=pl.BlockSpec((None, tr, hc), lambda b, i, c_ref: (b, i, 0))),
        out_shape=jax.ShapeDtypeStruct((nb, hr, hc), BF16), compiler_params=_cparams(("parallel", "parallel")),
    )(c_idx, g4)


def _adamw(name, ws, gs, ms, vs):
    n = len(ws)
    c1 = 1.0 - ADAM_B1 ** ADAM_STEP
    c2 = 1.0 - ADAM_B2 ** ADAM_STEP

    def fn(*tiles):
        out = []
        for k in range(n):
            w_t, g_t, m_t, v_t = tiles[4 * k:4 * k + 4]
            m_n = ADAM_B1 * m_t + (1.0 - ADAM_B1) * g_t
            v_n = ADAM_B2 * v_t + (1.0 - ADAM_B2) * (g_t * g_t)
            out += [-ADAM_LR * ((m_n / c1) / (jnp.sqrt(v_n / c2) + ADAM_EPS) + ADAM_WD * w_t), m_n, v_n]
        return out

    rows, cdim = ws[0].shape
    tiled = [a for quad in zip(ws, gs, ms, vs) for a in quad]
    pref = (512, 352, 256, 128, 64, 8) if n == 1 else (176, 128, 64, 8)
    res = _rowwise(name, fn, tiled, [], [(cdim, F32)] * (3 * n), tm=_pick(rows, pref))
    return [tuple(res[3 * k:3 * k + 3]) for k in range(n)]


BIG = ("ffn1_w_gate", "ffn1_w_up", "ffn1_w_down", "w_in", "w_out", "ffn2_w_gate", "ffn2_w_up", "ffn2_w_down",
       "w_ple_gate", "w_ple_proj")
SMALL = ("ffn1_norm", "mix_norm", "b_mlstm_gates", "b_fox_f", "mlstm_out_norm", "fox_out_norm", "ffn2_norm",
         "ple_gate_norm", "ple_proj_norm", "final_norm")
WEIGHTS = ("ffn1_norm", "ffn1_w_gate", "ffn1_w_up", "ffn1_w_down", "mix_norm", "w_in", "conv_qk", "b_mlstm_gates",
           "b_fox_f", "mlstm_out_norm", "fox_out_norm", "w_out", "ffn2_norm", "ffn2_w_gate", "ffn2_w_up", "ffn2_w_down",
           "ple_gate_norm", "w_ple_gate", "w_ple_proj", "ple_proj_norm", "final_norm")
TRANSPOSED = ("ffn1_w_gate", "ffn1_w_up", "w_in", "ffn2_w_gate", "ffn2_w_up")
PACK_W = 1024


def _chip_blocks(a):
    r, c4 = a.shape
    return a.reshape(r, 4, c4 // 4).transpose(1, 0, 2)


def _from_chip_blocks(a):
    nb, r, c = a.shape
    return a.transpose(1, 0, 2).reshape(r, nb * c)


def kernel(x, p, ffn1_norm, ffn1_w_gate, ffn1_w_up, ffn1_w_down, mix_norm, w_in, conv_qk, b_mlstm_gates, b_fox_f, mlstm_out_norm, fox_out_norm, w_out, ffn2_norm, ffn2_w_gate, ffn2_w_up, ffn2_w_down, ple_gate_norm, w_ple_gate, w_ple_proj, ple_proj_norm, final_norm, loss_target, m_ffn1_norm, m_ffn1_w_gate, m_ffn1_w_up, m_ffn1_w_down, m_mix_norm, m_w_in, m_conv_qk, m_b_mlstm_gates, m_b_fox_f, m_mlstm_out_norm, m_fox_out_norm, m_w_out, m_ffn2_norm, m_ffn2_w_gate, m_ffn2_w_up, m_ffn2_w_down, m_ple_gate_norm, m_w_ple_gate, m_w_ple_proj, m_ple_proj_norm, m_final_norm, v_ffn1_norm, v_ffn1_w_gate, v_ffn1_w_up, v_ffn1_w_down, v_mix_norm, v_w_in, v_conv_qk, v_b_mlstm_gates, v_b_fox_f, v_mlstm_out_norm, v_fox_out_norm, v_w_out, v_ffn2_norm, v_ffn2_w_gate, v_ffn2_w_up, v_ffn2_w_down, v_ple_gate_norm, v_w_ple_gate, v_w_ple_proj, v_ple_proj_norm, v_final_norm):
    w = dict(ffn1_norm=ffn1_norm, ffn1_w_gate=ffn1_w_gate, ffn1_w_up=ffn1_w_up, ffn1_w_down=ffn1_w_down, mix_norm=mix_norm,
             w_in=w_in, conv_qk=conv_qk, b_mlstm_gates=b_mlstm_gates, b_fox_f=b_fox_f, mlstm_out_norm=mlstm_out_norm,
             fox_out_norm=fox_out_norm, w_out=w_out, ffn2_norm=ffn2_norm, ffn2_w_gate=ffn2_w_gate, ffn2_w_up=ffn2_w_up,
             ffn2_w_down=ffn2_w_down, ple_gate_norm=ple_gate_norm, w_ple_gate=w_ple_gate, w_ple_proj=w_ple_proj,
             ple_proj_norm=ple_proj_norm, final_norm=final_norm)
    m = dict(ffn1_norm=m_ffn1_norm, ffn1_w_gate=m_ffn1_w_gate, ffn1_w_up=m_ffn1_w_up, ffn1_w_down=m_ffn1_w_down,
             mix_norm=m_mix_norm, w_in=m_w_in, conv_qk=m_conv_qk, b_mlstm_gates=m_b_mlstm_gates, b_fox_f=m_b_fox_f,
             mlstm_out_norm=m_mlstm_out_norm, fox_out_norm=m_fox_out_norm, w_out=m_w_out, ffn2_norm=m_ffn2_norm,
             ffn2_w_gate=m_ffn2_w_gate, ffn2_w_up=m_ffn2_w_up, ffn2_w_down=m_ffn2_w_down, ple_gate_norm=m_ple_gate_norm,
             w_ple_gate=m_w_ple_gate, w_ple_proj=m_w_ple_proj, ple_proj_norm=m_ple_proj_norm, final_norm=m_final_norm)
    v = dict(ffn1_norm=v_ffn1_norm, ffn1_w_gate=v_ffn1_w_gate, ffn1_w_up=v_ffn1_w_up, ffn1_w_down=v_ffn1_w_down,
             mix_norm=v_mix_norm, w_in=v_w_in, conv_qk=v_conv_qk, b_mlstm_gates=v_b_mlstm_gates, b_fox_f=v_b_fox_f,
             mlstm_out_norm=v_mlstm_out_norm, fox_out_norm=v_fox_out_norm, w_out=v_w_out, ffn2_norm=v_ffn2_norm,
             ffn2_w_gate=v_ffn2_w_gate, ffn2_w_up=v_ffn2_w_up, ffn2_w_down=v_ffn2_w_down, ple_gate_norm=v_ple_gate_norm,
             w_ple_gate=v_w_ple_gate, w_ple_proj=v_w_ple_proj, ple_proj_norm=v_ple_proj_norm, final_norm=v_final_norm)
    shapes = {n: w[n].shape for n in WEIGHTS}

    def view(a, n):
        return a[0].T if n in TRANSPOSED else a.reshape(-1, a.shape[-1])

    def unview(a, n):
        return (a.T if n in TRANSPOSED else a).reshape(shapes[n])

    w2, m2, v2 = ({n: view(a, n) for n, a in d.items()} for d in (w, m, v))

    c_idx = lax.axis_index("c").astype(jnp.int32).reshape(1)
    me_idx = (2 * lax.axis_index("x") + lax.axis_index("y")).astype(jnp.int32).reshape(1)
    place = jnp.concatenate([me_idx, c_idx])
    slot = {}
    for grp in SAME_SHAPE:
        slot.update(zip(grp, _to_slot("slot_" + grp[0], [w2[n] for n in grp], me_idx, BF16)))
    slot["conv_qk"] = _to_slot("slot_conv_qk", [w2["conv_qk"]], me_idx, F32)[0]
    wg1, wu1, wd1 = _gather4("gather_ffn1", [slot[n] for n in FFN1], [SPLIT[n] for n in FFN1])
    sp = {n: w2[n] for n in SMALL}
    loss_part, grad_x, grads, gs, conv_grad = _local_step(
        x[0], p[0, 0], loss_target[0], sp, wg1, wu1, wd1, [slot[n] for n in REST + ("conv_qk",)], c_idx, place)

    small = [gs[n].reshape(1, -1) for n in SMALL] + [conv_grad, loss_part]
    rows = [jnp.pad(a, ((0, 0), (0, PACK_W - a.shape[1]))) for a in small]
    packed = jnp.concatenate(rows, axis=0)
    packed = jnp.pad(packed, ((0, -packed.shape[0] % 8), (0, 0)))
    red = _allreduce_small(packed)
    loss = red[len(SMALL) + CONV_W, 0]
    for i, n in enumerate(SMALL):
        grads[n] = red[i:i + 1, :gs[n].size]
    dconv = red[len(SMALL):len(SMALL) + CONV_W, :conv_grad.shape[1]]
    cw = conv_qk.shape[-1]
    grads["conv_qk"] = lax.dynamic_slice_in_dim(dconv, (2 * lax.axis_index("x") + lax.axis_index("y")) * cw, cw, axis=1)

    outs = {}
    for grp in SAME_SHAPE + tuple((n,) for n in WEIGHTS if n not in BIG):
        g2s = [grads[n].reshape(w2[n].shape) for n in grp]
        res = _adamw("adamw_" + grp[0], [w2[n] for n in grp], g2s, [m2[n] for n in grp], [v2[n] for n in grp])
        for n, g2, (d, nm, nv) in zip(grp, g2s, res):
            outs[n] = tuple(unview(a, n) for a in (g2, d, nm, nv))
    return (loss, grad_x[None], *[outs[n][0] for n in WEIGHTS], *[outs[n][1] for n in WEIGHTS],
            *[outs[n][2] for n in WEIGHTS], *[outs[n][3] for n in WEIGHTS])
```

```python
import jax
import jax.numpy as jnp
from jax import lax
from jax.experimental import pallas as pl
from jax.experimental.pallas import tpu as pltpu

F32 = jnp.float32
BF16 = jnp.bfloat16
EPS = 1e-6
NH_M, DK_M, DV_M = 4, 64, 128
NH_F, DH_F = 8, 64
CONV_W = 4
ADAM_LR, ADAM_B1, ADAM_B2, ADAM_EPS, ADAM_WD, ADAM_STEP = 0.001, 0.9, 0.999, 1e-08, 0.01, 10
VMEM_LIMIT = 56 * 1024 * 1024


def _cparams(sem):
    return pltpu.CompilerParams(dimension_semantics=sem, vmem_limit_bytes=VMEM_LIMIT)


def _sigmoid(x):
    return 1.0 / (1.0 + jnp.exp(-x))


def _dot(a, b, ca, cb):
    return lax.dot_general(a.astype(BF16), b.astype(BF16), (((ca,), (cb,)), ((), ())), preferred_element_type=F32)


def _rowwise(name, fn, tiled, full, outs, accs=(), tm=512, plan=None):
    rows = tiled[0].shape[0]
    tm = min(tm, rows)
    assert rows % tm == 0
    n_t, n_f, n_o, n_a = len(tiled), len(full), len(outs), len(accs)
    nt = rows // tm
    host = _Hosted(plan, n_t + n_f, n_o + n_a)

    def body(*refs):
        in_refs, orefs, _, prefs = host.split(refs)
        host.run(0, pl.program_id(0) == 0, prefs)
        host.run(1, pl.program_id(0) == 0, prefs)
        ins = [r[...] for r in in_refs]
        res = fn(*ins)
        if not isinstance(res, (tuple, list)):
            res = (res,)
        for r, v in zip(orefs[:n_o], res[:n_o]):
            r[...] = v.astype(r.dtype)
        if n_a:
            @pl.when(pl.program_id(0) == 0)
            def _():
                for r in orefs[n_o:]:
                    r[...] = jnp.zeros_like(r)
            for r, v in zip(orefs[n_o:], res[n_o:]):
                r[...] += v.astype(r.dtype)
        host.run(2, pl.program_id(0) == nt - 1, prefs)

    in_specs = [pl.BlockSpec((tm, a.shape[1]), lambda i: (i, 0)) for a in tiled]
    in_specs += [pl.BlockSpec(a.shape, lambda i: (0, 0)) for a in full]
    out_specs = [pl.BlockSpec((tm, c), lambda i: (i, 0)) for c, _ in outs]
    out_specs += [pl.BlockSpec(s, lambda i: (0, 0)) for s, _ in accs]
    out_shape = [jax.ShapeDtypeStruct((rows, c), d) for c, d in outs]
    out_shape += [jax.ShapeDtypeStruct(s, d) for s, d in accs]
    hc = host.call_args()
    res = pl.pallas_call(
        body, name=name, grid=(nt,), in_specs=in_specs + hc["in_specs"], out_specs=out_specs + hc["out_specs"],
        out_shape=out_shape + hc["out_shape"], scratch_shapes=hc["scratch"], input_output_aliases=hc["aliases"],
        compiler_params=_cparams(("arbitrary",) if (n_a or plan is not None) else ("parallel",)),
    )(*tiled, *full, *hc["args"])
    return res if plan is None else (res[:n_o + n_a], res[n_o + n_a:])


def _colsum(v):
    return jnp.sum(v, axis=0, keepdims=True)


def _rms_fwd_val(x, g):
    r = lax.rsqrt(jnp.mean(x * x, axis=-1, keepdims=True) + EPS)
    return x * r * g


def _rms_bwd_val(dy, x, g):
    r = lax.rsqrt(jnp.mean(x * x, axis=-1, keepdims=True) + EPS)
    xh = x * r
    dxh = dy * g
    dx = r * (dxh - xh * jnp.mean(dxh * xh, axis=-1, keepdims=True))
    return dx, _colsum(dy * xh)


def _mm(name, pairs, out_shape, out_block, out_map, grid, kaxis, ta=False, tb=False, scale=None, res=None,
        out_dtype=F32, plan=None, twin=False):
    n_o = 2 if twin else 1
    nk = grid[kaxis]
    npairs = len(pairs)
    ca, cb = (0 if ta else 1), (1 if tb else 0)
    acc_shape = tuple(d for d in out_block if d is not None)
    n_in = 2 * npairs + (1 if res is not None else 0)
    host = _Hosted(plan, n_in, n_o)

    def body(*refs):
        ins, o_refs, (acc_ref,), prefs = host.split(refs)
        o_ref = o_refs[0]
        in_refs = ins[: 2 * npairs]
        res_ref = ins[2 * npairs] if res is not None else None
        k = pl.program_id(kaxis)
        ids = [pl.program_id(a) for a in range(len(grid))]
        first, last = ids[0] == 0, ids[0] == grid[0] - 1
        for a in range(1, len(grid)):
            first, last = first & (ids[a] == 0), last & (ids[a] == grid[a] - 1)
        host.run(0, first, prefs)
        host.run(1, first, prefs)

        @pl.when(k == 0)
        def _():
            acc_ref[...] = jnp.zeros_like(acc_ref)

        part = None
        for p in range(npairs):
            d = _dot(in_refs[2 * p][...], in_refs[2 * p + 1][...], ca, cb)
            part = d if part is None else part + d
        acc_ref[...] += part

        @pl.when(k == nk - 1)
        def _():
            v = acc_ref[...]
            if scale is not None:
                v = v * scale
            if res_ref is not None:
                v = v + res_ref[...].astype(F32)
            o_ref[...] = v.astype(o_ref.dtype)
            if twin:
                o_refs[1][...] = v.astype(BF16)

        host.run(2, last, prefs)

    in_specs, args = [], []
    for a, ab, am, b, bb, bm in pairs:
        in_specs += [pl.BlockSpec(ab, am), pl.BlockSpec(bb, bm)]
        args += [a, b]
    if res is not None:
        in_specs.append(pl.BlockSpec(out_block, out_map))
        args.append(res)
    sem = tuple("arbitrary" if (i == kaxis or plan is not None) else "parallel" for i in range(len(grid)))
    hc = host.call_args()
    out = pl.pallas_call(
        body, name=name, grid=grid, in_specs=in_specs + hc["in_specs"],
        out_specs=[pl.BlockSpec(out_block, out_map)] * n_o + hc["out_specs"],
        out_shape=[jax.ShapeDtypeStruct(out_shape, out_dtype)] + [jax.ShapeDtypeStruct(out_shape, BF16)] * (n_o - 1)
        + hc["out_shape"],
        scratch_shapes=[pltpu.VMEM(acc_shape, F32)] + hc["scratch"], input_output_aliases=hc["aliases"],
        compiler_params=_cparams(sem),
    )(*args, *hc["args"])
    res_out = tuple(out[:2]) if twin else out[0]
    return res_out if plan is None else (res_out, out[n_o:])


def _pick(n, pref):
    for t in pref:
        if n % t == 0:
            return t
    return n


def _mm_nn(name, a, b, tm=512, tn=512, tk=512, **kw):
    (m, k), n = a.shape, b.shape[1]
    tm, tn, tk = _pick(m, (tm, 256, 128)), _pick(n, (tn, 256, 128)), _pick(k, (tk, 256, 128))
    return _mm(name, [(a, (tm, tk), lambda i, j, kk: (i, kk), b, (tk, tn), lambda i, j, kk: (kk, j))],
               (m, n), (tm, tn), lambda i, j, kk: (i, j), (m // tm, n // tn, k // tk), 2, **kw)


def _mm_nt(name, a, b, tm=512, tn=512, tk=512, **kw):
    (m, k), n = a.shape, b.shape[0]
    tm, tn, tk = _pick(m, (tm, 256, 128)), _pick(n, (tn, 256, 128)), _pick(k, (tk, 256, 128))
    return _mm(name, [(a, (tm, tk), lambda i, j, kk: (i, kk), b, (tn, tk), lambda i, j, kk: (j, kk))],
               (m, n), (tm, tn), lambda i, j, kk: (i, j), (m // tm, n // tn, k // tk), 2, tb=True, **kw)


def _mm_tn(name, a, b, tm=512, tn=512, tk=4096, **kw):
    (k, m), n = a.shape, b.shape[1]
    tm, tn, tk = _pick(m, (tm, 256, 128)), _pick(n, (tn, 256, 128)), _pick(k, (tk, 2048, 1024, 512, 256, 128))
    return _mm(name, [(a, (tk, tm), lambda i, j, kk: (kk, i), b, (tk, tn), lambda i, j, kk: (kk, j))],
               (m, n), (tm, tn), lambda i, j, kk: (i, j), (m // tm, n // tn, k // tk), 2, ta=True, **kw)


def _norm_mm(name, h, gamma, w, w_transposed, out_dtype):
    t, d = h.shape
    n = w.shape[0] if w_transposed else w.shape[1]
    tm, tn = _pick(t, (1024, 512, 256)), _pick(n, (1024, 512, 256, 128))

    def body(h_ref, gam_ref, w_ref, xn_ref, o_ref, xn_scr):
        @pl.when(pl.program_id(1) == 0)
        def _():
            xn = _rms_fwd_val(h_ref[...], gam_ref[...]).astype(BF16)
            xn_scr[...] = xn
            xn_ref[...] = xn

        o_ref[...] = _dot(xn_scr[...], w_ref[...], 1, 1 if w_transposed else 0).astype(o_ref.dtype)

    wspec = pl.BlockSpec((tn, d), lambda i, j: (j, 0)) if w_transposed else pl.BlockSpec((d, tn), lambda i, j: (0, j))
    return pl.pallas_call(
        body, name=name, grid=(t // tm, n // tn),
        in_specs=[pl.BlockSpec((tm, d), lambda i, j: (i, 0)), pl.BlockSpec((1, d), lambda i, j: (0, 0)), wspec],
        out_specs=[pl.BlockSpec((tm, d), lambda i, j: (i, 0)), pl.BlockSpec((tm, tn), lambda i, j: (i, j))],
        out_shape=[jax.ShapeDtypeStruct((t, d), BF16), jax.ShapeDtypeStruct((t, n), out_dtype)],
        scratch_shapes=[pltpu.VMEM((tm, d), BF16)], compiler_params=_cparams(("parallel", "arbitrary")),
    )(h, gamma, w)


CHAIN_ROWS = 256


def _row_chains(tm):
    n = max(tm // CHAIN_ROWS, 1)
    return [slice(r * (tm // n), (r + 1) * (tm // n)) for r in range(n)]


def _ffn_fwd(pfx, h, gamma, wg, wu, wd, plan=None):
    t, d = h.shape
    nb, f, _ = wg.shape
    tm = _pick(t, (1024, 512, 256))
    nt = t // tm
    host = _Hosted(plan, 5, 4)

    def body(*refs):
        (h_ref, gam_ref, wg_ref, wu_ref, wd_ref), (ho_ref, xn_ref, g_ref, u_ref), (xn_scr, acc_ref), prefs = host.split(refs)
        i, j = pl.program_id(0), pl.program_id(1)
        host.run(0, (i == 0) & (j == 0), prefs)
        host.run(1, (i == nt // 2) & (j == 0), prefs)

        @pl.when(j == 0)
        def _():
            xn = _rms_fwd_val(h_ref[...], gam_ref[...]).astype(BF16)
            xn_scr[...] = xn
            xn_ref[...] = xn
            acc_ref[...] = jnp.zeros_like(acc_ref)

        for rows in _row_chains(tm):
            x = xn_scr[rows, :]
            g = _dot(x, wg_ref[...], 1, 1)
            u = _dot(x, wu_ref[...], 1, 1)
            g_ref[rows, :] = g.astype(BF16)
            u_ref[rows, :] = u.astype(BF16)
            acc_ref[rows, :] += _dot(g * _sigmoid(g) * u, wd_ref[...], 1, 0)

        @pl.when(j == nb - 1)
        def _():
            ho_ref[...] = h_ref[...] + 0.5 * acc_ref[...]

        host.run(2, (i == nt - 1) & (j == nb - 1), prefs)

    row = pl.BlockSpec((tm, d), lambda i, j: (i, 0))
    blk = pl.BlockSpec((None, tm, f), lambda i, j: (j, i, 0))
    wspec = pl.BlockSpec((None, f, d), lambda i, j: (j, 0, 0))
    hc = host.call_args()
    res = pl.pallas_call(
        body, name=pfx + "_fwd", grid=(nt, nb),
        in_specs=[row, pl.BlockSpec((1, d), lambda i, j: (0, 0)), wspec, wspec, wspec] + hc["in_specs"],
        out_specs=[row, row, blk, blk] + hc["out_specs"],
        out_shape=[jax.ShapeDtypeStruct((t, d), F32), jax.ShapeDtypeStruct((t, d), BF16),
                   jax.ShapeDtypeStruct((nb, t, f), BF16), jax.ShapeDtypeStruct((nb, t, f), BF16)] + hc["out_shape"],
        scratch_shapes=[pltpu.VMEM((tm, d), BF16), pltpu.VMEM((tm, d), F32)] + hc["scratch"],
        input_output_aliases=hc["aliases"], compiler_params=_cparams(("arbitrary", "arbitrary")),
    )(h, gamma, wg, wu, wd, *hc["args"])
    return res[:4], res[4:]


def _ffn_bwd(pfx, dh_out, h, gamma, xn, g_all, u_all, wg, wu, wd, plan=None):
    t, d = h.shape
    nb, f, _ = wg.shape
    tm = _pick(t, (512, 256))
    tk = _pick(t, (4096, 2048, 1024, 512, 256))

    nt = t // tm
    host = _Hosted(plan, 8, 6)

    def body(*refs):
        ((dy_ref, h_ref, gam_ref, wg_ref, wu_ref, wd_ref, g_ref, u_ref),
         (dh_ref, dgam_ref, dg_ref, du_ref, a_ref, dyb_ref), (acc_ref,), prefs) = host.split(refs)
        i, j = pl.program_id(0), pl.program_id(1)
        host.run(0, (i == 0) & (j == 0), prefs)
        host.run(1, (i == nt // 2) & (j == 0), prefs)

        @pl.when((i == 0) & (j == 0))
        def _():
            dgam_ref[...] = jnp.zeros_like(dgam_ref)

        @pl.when(j == 0)
        def _():
            acc_ref[...] = jnp.zeros_like(acc_ref)
            dyb_ref[...] = dy_ref[...].astype(BF16)

        for rows in _row_chains(tm):
            da = _dot(dy_ref[rows, :], wd_ref[...], 1, 1) * 0.5
            g = g_ref[rows, :].astype(F32)
            u = u_ref[rows, :].astype(F32)
            s = _sigmoid(g)
            sl = g * s
            du = (da * sl).astype(BF16)
            dg = (da * u * (s + sl * (1.0 - s))).astype(BF16)
            du_ref[rows, :] = du
            dg_ref[rows, :] = dg
            a_ref[rows, :] = (sl * u).astype(BF16)
            acc_ref[rows, :] += _dot(dg, wg_ref[...], 1, 0) + _dot(du, wu_ref[...], 1, 0)

        @pl.when(j == nb - 1)
        def _():
            dx, dgam = _rms_bwd_val(acc_ref[...], h_ref[...], gam_ref[...])
            dh_ref[...] = dy_ref[...] + dx
            dgam_ref[...] += dgam

        host.run(2, (i == nt - 1) & (j == nb - 1), prefs)

    row = pl.BlockSpec((tm, d), lambda i, j: (i, 0))
    vec = pl.BlockSpec((1, d), lambda i, j: (0, 0))
    blk = pl.BlockSpec((None, tm, f), lambda i, j: (j, i, 0))
    wspec = pl.BlockSpec((None, f, d), lambda i, j: (j, 0, 0))
    hc = host.call_args()
    res = pl.pallas_call(
        body, name=pfx + "_bwd", grid=(nt, nb),
        in_specs=[row, row, vec, wspec, wspec, wspec, blk, blk] + hc["in_specs"],
        out_specs=[row, vec, blk, blk, blk, row] + hc["out_specs"],
        out_shape=[jax.ShapeDtypeStruct((t, d), F32), jax.ShapeDtypeStruct((1, d), F32)]
        + [jax.ShapeDtypeStruct((nb, t, f), BF16)] * 3 + [jax.ShapeDtypeStruct((t, d), BF16)] + hc["out_shape"],
        scratch_shapes=[pltpu.VMEM((tm, d), F32)] + hc["scratch"], input_output_aliases=hc["aliases"],
        compiler_params=_cparams(("arbitrary", "arbitrary")),
    )(dh_out, h, gamma, wg, wu, wd, g_all, u_all, *hc["args"])
    dh, dgamma, dg_all, du_all, a_all, dyb = res[:6]

    xmap, bmap, omap = (lambda b, k: (k, 0)), (lambda b, k: (b, k, 0)), (lambda b, k: (b, 0, 0))
    dwg, tg = _mm(pfx + "_dwg", [(dg_all, (None, tk, f), bmap, xn, (tk, d), xmap)], (nb, f, d), (None, f, d), omap,
                  (nb, t // tk), 1, ta=True, twin=True)
    dwu, tu = _mm(pfx + "_dwu", [(du_all, (None, tk, f), bmap, xn, (tk, d), xmap)], (nb, f, d), (None, f, d), omap,
                  (nb, t // tk), 1, ta=True, twin=True)
    dwd, td = _mm(pfx + "_dwd", [(a_all, (None, tk, f), bmap, dyb, (tk, d), xmap)], (nb, f, d), (None, f, d), omap,
                  (nb, t // tk), 1, ta=True, scale=0.5, twin=True)
    return (dh, dgamma, dwg, dwu, dwd), res[6:], (tg, tu, td)


def _ffn_bwd_late_dx(pfx, dh_out, h, gamma, xn, g_all, u_all, wg, wu, wd, plan_gu, plans_dw, make_plan_dx):
    t, d = h.shape
    nb, f, _ = wg.shape
    tm = _pick(t, (512, 256))
    tk = _pick(t, (4096, 2048, 1024, 512, 256))
    nt = t // tm
    host_a = _Hosted(plan_gu, 4, 4)

    def body_a(*refs):
        (dy_ref, wd_ref, g_ref, u_ref), (dg_ref, du_ref, a_ref, dyb_ref), _, prefs = host_a.split(refs)
        i, j = pl.program_id(0), pl.program_id(1)
        host_a.run(0, (i == 0) & (j == 0), prefs)
        host_a.run(1, (i == 0) & (j == 0), prefs)

        @pl.when(j == 0)
        def _():
            dyb_ref[...] = dy_ref[...].astype(BF16)

        for rows in _row_chains(tm):
            da = _dot(dy_ref[rows, :], wd_ref[...], 1, 1) * 0.5
            g = g_ref[rows, :].astype(F32)
            u = u_ref[rows, :].astype(F32)
            s = _sigmoid(g)
            sl = g * s
            du_ref[rows, :] = (da * sl).astype(BF16)
            dg_ref[rows, :] = (da * u * (s + sl * (1.0 - s))).astype(BF16)
            a_ref[rows, :] = (sl * u).astype(BF16)
        host_a.run(2, (i == nt - 1) & (j == nb - 1), prefs)

    row = pl.BlockSpec((tm, d), lambda i, j: (i, 0))
    vec = pl.BlockSpec((1, d), lambda i, j: (0, 0))
    blk = pl.BlockSpec((None, tm, f), lambda i, j: (j, i, 0))
    wspec = pl.BlockSpec((None, f, d), lambda i, j: (j, 0, 0))
    hc = host_a.call_args()
    res_a = pl.pallas_call(
        body_a, name=pfx + "_bwd_gu", grid=(nt, nb), in_specs=[row, wspec, blk, blk] + hc["in_specs"],
        out_specs=[blk] * 3 + [row] + hc["out_specs"],
        out_shape=[jax.ShapeDtypeStruct((nb, t, f), BF16)] * 3 + [jax.ShapeDtypeStruct((t, d), BF16)] + hc["out_shape"],
        scratch_shapes=hc["scratch"], input_output_aliases=hc["aliases"], compiler_params=_cparams(("arbitrary", "arbitrary")),
    )(dh_out, wd, g_all, u_all, *hc["args"])
    dg_all, du_all, a_all, dyb = res_a[:4]

    xmap, bmap, omap = (lambda b, k: (k, 0)), (lambda b, k: (b, k, 0)), (lambda b, k: (b, 0, 0))
    def dw(name, a, b, plan, scale=None):
        r = _mm(pfx + name, [(a, (None, tk, f), bmap, b, (tk, d), xmap)], (nb, f, d), (None, f, d), omap, (nb, t // tk), 1,
                ta=True, scale=scale, plan=plan, twin=True)
        return r if plan is not None else (r, ())

    (dwd, td), out_d = dw("_dwd", a_all, dyb, plans_dw[0], 0.5)
    (dwg, tg), out_g = dw("_dwg", dg_all, xn, plans_dw[1])
    (dwu, tu), out_u = dw("_dwu", du_all, xn, plans_dw[2])

    plan_dx = make_plan_dx((dwg, dwu, dwd), (tg, tu, td))
    host_b = _Hosted(plan_dx, 7, 2)

    def body_b(*refs):
        (dy_ref, h_ref, gam_ref, wg_ref, wu_ref, dg_ref, du_ref), (dh_ref, dgam_ref), (acc_ref,), prefs = host_b.split(refs)
        i, j = pl.program_id(0), pl.program_id(1)
        host_b.run(0, (i == 0) & (j == 0), prefs)
        host_b.run(1, (i == 0) & (j == 0), prefs)

        @pl.when((i == 0) & (j == 0))
        def _():
            dgam_ref[...] = jnp.zeros_like(dgam_ref)

        @pl.when(j == 0)
        def _():
            acc_ref[...] = jnp.zeros_like(acc_ref)

        acc_ref[...] += _dot(dg_ref[...], wg_ref[...], 1, 0) + _dot(du_ref[...], wu_ref[...], 1, 0)

        @pl.when(j == nb - 1)
        def _():
            dx, dgam = _rms_bwd_val(acc_ref[...], h_ref[...], gam_ref[...])
            dh_ref[...] = dy_ref[...] + dx
            dgam_ref[...] += dgam

        host_b.run(2, (i == nt - 1) & (j == nb - 1), prefs)

    hc = host_b.call_args()
    res_b = pl.pallas_call(
        body_b, name=pfx + "_bwd_dx", grid=(nt, nb), in_specs=[row, row, vec, wspec, wspec, blk, blk] + hc["in_specs"],
        out_specs=[row, vec] + hc["out_specs"],
        out_shape=[jax.ShapeDtypeStruct((t, d), F32), jax.ShapeDtypeStruct((1, d), F32)] + hc["out_shape"],
        scratch_shapes=[pltpu.VMEM((tm, d), F32)] + hc["scratch"], input_output_aliases=hc["aliases"],
        compiler_params=_cparams(("arbitrary", "arbitrary")),
    )(dh_out, h, gamma, wg, wu, dg_all, du_all, *hc["args"])
    return (res_b[0], res_b[1], dwg, dwu, dwd), (res_a[4:], out_d, out_g, out_u, res_b[2:])


HALO = 16


def _silu_grad(y):
    s = _sigmoid(y)
    return s * (1.0 + y * (1.0 - s))


def _with_halo(ref, i, n_tiles, tm, before, after):
    t = ref.shape[0]
    r0 = pl.multiple_of(i * tm, tm)
    parts = [ref[pl.ds(r0, tm), :].astype(F32)]
    if before:
        prev = ref[pl.ds(pl.multiple_of(jnp.maximum(r0 - HALO, 0), HALO), HALO), :].astype(F32)
        parts.insert(0, jnp.where(i > 0, prev, 0.0))
    if after:
        nxt = ref[pl.ds(pl.multiple_of(jnp.minimum(r0 + tm, t - HALO), HALO), HALO), :].astype(F32)
        parts.append(jnp.where(i < n_tiles - 1, nxt, 0.0))
    return jnp.concatenate(parts, axis=0)


def _conv_fwd(zbig, w):
    t, c = zbig.shape[0], w.shape[1]
    tm = _pick(t, (512, 256))
    nt = t // tm

    def body(x_ref, w_ref, o_ref):
        xe = _with_halo(x_ref, pl.program_id(0), nt, tm, True, False)
        wv = w_ref[...]
        y = xe * wv[3:4, :]
        for i in range(CONV_W - 1):
            y = y + pltpu.roll(xe, CONV_W - 1 - i, 0) * wv[i:i + 1, :]
        y = y[HALO:, :]
        o_ref[...] = (y * _sigmoid(y)).astype(o_ref.dtype)

    return pl.pallas_call(
        body, name="conv_fwd", grid=(nt,),
        in_specs=[pl.BlockSpec((t, c), lambda i: (0, 0)), pl.BlockSpec(w.shape, lambda i: (0, 0))],
        out_specs=pl.BlockSpec((tm, c), lambda i: (i, 0)), out_shape=jax.ShapeDtypeStruct((t, c), BF16),
        compiler_params=_cparams(("parallel",)),
    )(zbig, w)


def _conv_bwd(zbig, dact, w):
    t, c = dact.shape
    tm = _pick(t, (512, 256))
    nt = t // tm
    n = tm + HALO

    def body(x_ref, d_ref, w_ref, dx_ref, dw_ref):
        xe = _with_halo(x_ref, pl.program_id(0), nt, tm, True, True)
        de = _with_halo(d_ref, pl.program_id(0), nt, tm, False, True)
        wv = w_ref[...]
        sh = [pltpu.roll(xe, CONV_W - 1 - i, 0)[HALO:, :] if i < CONV_W - 1 else xe[HALO:, :] for i in range(CONV_W)]
        y = sh[0] * wv[0:1, :]
        for i in range(1, CONV_W):
            y = y + sh[i] * wv[i:i + 1, :]
        dy = de * _silu_grad(y)
        dx = dy * wv[3:4, :]
        for i in range(CONV_W - 1):
            dx = dx + pltpu.roll(dy, n - (CONV_W - 1 - i), 0) * wv[i:i + 1, :]
        dx_ref[...] = dx[:tm, :].astype(dx_ref.dtype)
        dyc = dy[:tm, :]
        dwp = jnp.concatenate([_colsum(dyc * sh[i][:tm, :]) for i in range(CONV_W)], axis=0)

        @pl.when(pl.program_id(0) == 0)
        def _():
            dw_ref[...] = jnp.zeros_like(dw_ref)
        dw_ref[...] += dwp

    return pl.pallas_call(
        body, name="conv_bwd", grid=(nt,),
        in_specs=[pl.BlockSpec((t, c), lambda i: (0, 0)), pl.BlockSpec((t, c), lambda i: (0, 0)),
                  pl.BlockSpec(w.shape, lambda i: (0, 0))],
        out_specs=[pl.BlockSpec((tm, c), lambda i: (i, 0)), pl.BlockSpec(w.shape, lambda i: (0, 0))],
        out_shape=[jax.ShapeDtypeStruct((t, c), BF16), jax.ShapeDtypeStruct(w.shape, F32)],
        compiler_params=_cparams(("arbitrary",)),
    )(zbig, dact, w)


LM = 256
HI = lax.Precision.HIGHEST


def _logsig(x):
    return jnp.minimum(x, 0.0) - jnp.log(1.0 + jnp.exp(-jnp.abs(x)))


def _tri(n, lower):
    r = lax.broadcasted_iota(jnp.int32, (n, n), 0)
    c = lax.broadcasted_iota(jnp.int32, (n, n), 1)
    return (r >= c) if lower else (r <= c)


def _f32dot(a, b):
    return lax.dot_general(a, b, (((1,), (0,)), ((), ())), precision=HI, preferred_element_type=F32)


def _tri_dot(a, b, a_is_tri):
    tri = (a if a_is_tri else b).astype(BF16)
    parts = _split3(b if a_is_tri else a)
    outs = [_dot(tri, p, 1, 0) if a_is_tri else _dot(p, tri, 1, 0) for p in parts]
    return (outs[0] + outs[1]) + outs[2]


def _mlstm_decays(zs_ref, zsr_ref, bc_ref, br_ref):
    l = LM
    lf_c = _logsig(zs_ref[:, 0:2 * NH_M] + bc_ref[...])
    lf_r = _logsig(zsr_ref[...] + br_ref[...])
    low, up = _tri(l, True), _tri(l, False)
    return _tri_dot(low, lf_c, True), _tri_dot(lf_r, up, False), low, up


def _mlstm_chunk(h, q_ref, k_ref, v_ref, zs_ref, zsr_ref, bc_ref, br_ref, c_prev, m_prev, decays):
    l = LM
    q = q_ref[:, h * DK_M:(h + 1) * DK_M].astype(F32) * (DK_M ** -0.5)
    k = k_ref[:, h * DK_M:(h + 1) * DK_M]
    v = v_ref[:, h * DV_M:(h + 1) * DV_M]
    lane = lax.broadcasted_iota(jnp.int32, (l, DV_M), 1)
    v1 = jnp.concatenate([v, (lane == 0).astype(v.dtype)], axis=1)
    zs, zsr = zs_ref[...], zsr_ref[...]
    li_c = zs[:, h:h + 1] + bc_ref[:, h:h + 1]
    fp_c = zs[:, NH_M + h:NH_M + h + 1] + bc_ref[:, NH_M + h:NH_M + h + 1]
    li_r = zsr[h:h + 1, :] + br_ref[h:h + 1, :]
    fp_r = zsr[NH_M + h:NH_M + h + 1, :] + br_ref[NH_M + h:NH_M + h + 1, :]
    low = decays[2]
    b_c = decays[0][:, NH_M + h:NH_M + h + 1]
    b_r = decays[1][NH_M + h:NH_M + h + 1, :]
    g = b_r[:, l - 1:l]
    dmat = jnp.where(low, b_c - b_r + li_r, -jnp.inf)
    inter = b_c + m_prev
    m_t = jnp.maximum(inter, jnp.max(dmat, axis=1, keepdims=True))
    w_inter = jnp.exp(inter - m_t)
    amat = jnp.exp(dmat - m_t)
    s = _dot(q, k, 1, 1)
    p = amat * s
    qc = _dot(q, c_prev, 1, 0)
    qc_w = w_inter * qc
    num1 = qc_w + _dot(p, v1, 1, 0)
    den = num1[:, DV_M:DV_M + 1]
    mx = jnp.maximum(jnp.abs(den), jnp.exp(-m_t))
    hh = num1[:, :DV_M] / mx
    a_c = g - b_c + li_c
    return dict(q=q, k=k, v1=v1, fp_c=fp_c, fp_r=fp_r, b_c=b_c, g=g, m_t=m_t, w_inter=w_inter, amat=amat, s=s, p=p,
                qc_w=qc_w, den=den, mx=mx, hh=hh, a_c=a_c)


def _mlstm_fwd(qk, zbig, zs, zsr, bc, br, gm):
    t = zs.shape[0]
    l = LM
    nc = t // l
    dm = NH_M * DV_M

    def body(q_ref, k_ref, v_ref, o_ref, zs_ref, zsr_ref, bc_ref, br_ref, gm_ref, y_ref, cst_ref, mst_ref, c_scr, m_scr):
        @pl.when(pl.program_id(0) == 0)
        def _():
            c_scr[...] = jnp.zeros_like(c_scr)
            m_scr[...] = jnp.zeros_like(m_scr)

        cst_ref[...] = c_scr[...]
        mst_ref[...] = m_scr[...]
        ys = []
        decays = _mlstm_decays(zs_ref, zsr_ref, bc_ref, br_ref)
        for h in range(NH_M):
            c_prev = c_scr[h]
            m_prev = m_scr[h:h + 1, 0:1]
            r = _mlstm_chunk(h, q_ref, k_ref, v_ref, zs_ref, zsr_ref, bc_ref, br_ref, c_prev, m_prev, decays)
            hh = r["hh"]
            gh = gm_ref[:, h * DV_M:(h + 1) * DV_M]
            hn = hh * lax.rsqrt(jnp.mean(hh * hh, axis=-1, keepdims=True) + EPS) * gh
            og = o_ref[:, h * DV_M:(h + 1) * DV_M].astype(F32)
            ys.append(hn * _sigmoid(og))
            m_new = jnp.maximum(r["g"] + m_prev, jnp.max(r["a_c"], axis=0, keepdims=True))
            decay = jnp.exp(r["g"] + m_prev - m_new)
            wk = r["k"].astype(F32) * jnp.exp(r["a_c"] - m_new)
            c_scr[h] = decay * c_prev + _dot(wk, r["v1"], 0, 0)
            m_scr[h:h + 1, :] = jnp.broadcast_to(m_new, (1, 128))
        y_ref[...] = jnp.concatenate(ys, axis=1).astype(y_ref.dtype)

    return pl.pallas_call(
        body, name="mlstm_fwd", grid=(nc,),
        in_specs=[pl.BlockSpec((l, NH_M * DK_M), lambda i: (i, 0)), pl.BlockSpec((l, NH_M * DK_M), lambda i: (i, 1)),
                  pl.BlockSpec((l, dm), lambda i: (i, 1)), pl.BlockSpec((l, dm), lambda i: (i, 2)),
                  pl.BlockSpec((l, 128), lambda i: (i, 0)), pl.BlockSpec((8, l), lambda i: (0, i)),
                  pl.BlockSpec((1, 8), lambda i: (0, 0)), pl.BlockSpec((8, 1), lambda i: (0, 0)),
                  pl.BlockSpec((1, dm), lambda i: (0, 0))],
        out_specs=[pl.BlockSpec((l, dm), lambda i: (i, 0)), pl.BlockSpec((None, NH_M, DK_M, 2 * DV_M), lambda i: (i, 0, 0, 0)),
                   pl.BlockSpec((None, 8, 128), lambda i: (i, 0, 0))],
        out_shape=[jax.ShapeDtypeStruct((t, dm), BF16), jax.ShapeDtypeStruct((nc, NH_M, DK_M, 2 * DV_M), F32),
                   jax.ShapeDtypeStruct((nc, 8, 128), F32)],
        scratch_shapes=[pltpu.VMEM((NH_M, DK_M, 2 * DV_M), F32), pltpu.VMEM((8, 128), F32)],
        compiler_params=_cparams(("arbitrary",)),
    )(qk, qk, zbig, zbig, zs, zsr, bc, br, gm)


def _mlstm_bwd(qk, zbig, zs, zsr, bc, br, gm, cst, mst, dycat):
    t = zs.shape[0]
    l = LM
    nc = t // l
    dm = NH_M * DV_M

    def body(q_ref, k_ref, v_ref, o_ref, zs_ref, zsr_ref, bc_ref, br_ref, gm_ref, cst_ref, mst_ref, cnx_ref, mnx_ref,
             dy_ref, dqk_ref, dv_ref, do_ref, dzs_ref, dzr_ref, dgm_ref, dc_scr):
        @pl.when(pl.program_id(0) == 0)
        def _():
            dc_scr[...] = jnp.zeros_like(dc_scr)
            dgm_ref[...] = jnp.zeros_like(dgm_ref)

        lane = lax.broadcasted_iota(jnp.int32, (l, 128), 1)
        db_all, sig_c, carries = jnp.zeros((l, 128), F32), jnp.zeros((l, 128), F32), jnp.zeros((1, 128), F32)
        decays = _mlstm_decays(zs_ref, zsr_ref, bc_ref, br_ref)
        lower, upper = decays[2], decays[3]
        dzr_rows = [None] * 8
        dvs, dos, dgs, dqs, dks = [], [], [], [], []
        dzs = jnp.zeros((l, 128), F32)
        for h in range(NH_M):
            c_prev = cst_ref[h]
            m_prev = mst_ref[h:h + 1, 0:1]
            r = _mlstm_chunk(h, q_ref, k_ref, v_ref, zs_ref, zsr_ref, bc_ref, br_ref, c_prev, m_prev, decays)
            hh, mx, den, m_t, v1, amat = r["hh"], r["mx"], r["den"], r["m_t"], r["v1"], r["amat"]
            gh = gm_ref[:, h * DV_M:(h + 1) * DV_M]
            rs = lax.rsqrt(jnp.mean(hh * hh, axis=-1, keepdims=True) + EPS)
            xh = hh * rs
            sg = _sigmoid(o_ref[:, h * DV_M:(h + 1) * DV_M].astype(F32))
            dyh = dy_ref[:, h * DV_M:(h + 1) * DV_M]
            dos.append(dyh * xh * gh * sg * (1.0 - sg))
            dhn = dyh * sg
            dgs.append(_colsum(dhn * xh))
            dxh = dhn * gh
            dh = rs * (dxh - xh * jnp.mean(dxh * xh, axis=-1, keepdims=True))
            g1 = dh / mx
            hd = jnp.sum(hh * dh, axis=-1, keepdims=True)
            dden = jnp.where(jnp.abs(den) > jnp.exp(-m_t), -hd / mx * jnp.sign(den), 0.0)
            g256 = jnp.concatenate([g1, jnp.where(lane == 0, dden, 0.0)], axis=1)
            dc_h = dc_scr[h]
            ea = jnp.exp(r["a_c"])
            dp = _dot(g256, v1, 1, 1)
            ds = dp * amat
            dqs.append((r["w_inter"] * _dot(g256, c_prev, 1, 1) + _dot(ds, r["k"], 1, 0)) * (DK_M ** -0.5))
            dks.append(_dot(ds, r["q"], 0, 0) + ea * _dot(v1, dc_h, 1, 1))
            dv_st = ea * _dot(r["k"], dc_h, 1, 0)
            dv1 = _dot(r["p"], g256, 0, 0) + dv_st
            dvs.append(dv1[:, :DV_M])
            wmat = dp * r["p"]
            c_in = _colsum(wmat)
            c_st = jnp.sum(v1.astype(F32) * dv_st, axis=-1, keepdims=True)
            r_t = jnp.sum(wmat, axis=1, keepdims=True) + jnp.sum(g256 * r["qc_w"], axis=-1, keepdims=True)
            db = r_t - c_st
            carry = jnp.exp(mnx_ref[h:h + 1, 0:1]) * jnp.sum(
                jnp.sum(dc_h * cnx_ref[h], axis=1, keepdims=True), axis=0, keepdims=True)
            db_all = db_all + jnp.where(lane == NH_M + h, db, 0.0)
            sig_c = sig_c + jnp.where(lane == NH_M + h, _sigmoid(-r["fp_c"]), 0.0)
            carries = carries + jnp.where(lane[0:1, :] == NH_M + h, carry, 0.0)
            dzs = dzs + jnp.where(lane == h, c_st, 0.0)
            dzr_rows[h] = c_in
            dzr_rows[NH_M + h] = _sigmoid(-r["fp_r"])
            wq = r["q"] * jnp.exp(r["b_c"] - m_t)
            dc_scr[h] = jnp.exp(r["g"]) * dc_h + _dot(wq, g256, 0, 0)
        dzs = dzs + (_tri_dot(upper, db_all, True) + carries) * sig_c
        c_in4 = jnp.concatenate(dzr_rows[:NH_M], axis=0)
        dlf_r4 = -_tri_dot(c_in4, lower, False)
        dzr_rows = dzr_rows[:NH_M] + [dlf_r4[h:h + 1, :] * dzr_rows[NH_M + h] for h in range(NH_M)]
        dqk_ref[...] = jnp.concatenate(dqs + dks, axis=1)
        dv_ref[...] = jnp.concatenate(dvs, axis=1).astype(dv_ref.dtype)
        do_ref[...] = jnp.concatenate(dos, axis=1).astype(do_ref.dtype)
        dzs_ref[...] = dzs
        dzr_ref[...] = jnp.concatenate(dzr_rows, axis=0)
        dgm_ref[...] += jnp.concatenate(dgs, axis=1)

    rev = lambda i: nc - 1 - i
    nxt = lambda i: jnp.minimum(nc - i, nc - 1)
    return pl.pallas_call(
        body, name="mlstm_bwd", grid=(nc,),
        in_specs=[pl.BlockSpec((l, NH_M * DK_M), lambda i: (rev(i), 0)), pl.BlockSpec((l, NH_M * DK_M), lambda i: (rev(i), 1)),
                  pl.BlockSpec((l, dm), lambda i: (rev(i), 1)), pl.BlockSpec((l, dm), lambda i: (rev(i), 2)),
                  pl.BlockSpec((l, 128), lambda i: (rev(i), 0)), pl.BlockSpec((8, l), lambda i: (0, rev(i))),
                  pl.BlockSpec((1, 8), lambda i: (0, 0)), pl.BlockSpec((8, 1), lambda i: (0, 0)),
                  pl.BlockSpec((1, dm), lambda i: (0, 0)),
                  pl.BlockSpec((None, NH_M, DK_M, 2 * DV_M), lambda i: (rev(i), 0, 0, 0)),
                  pl.BlockSpec((None, 8, 128), lambda i: (rev(i), 0, 0)),
                  pl.BlockSpec((None, NH_M, DK_M, 2 * DV_M), lambda i: (nxt(i), 0, 0, 0)),
                  pl.BlockSpec((None, 8, 128), lambda i: (nxt(i), 0, 0)),
                  pl.BlockSpec((l, dm), lambda i: (rev(i), 0))],
        out_specs=[pl.BlockSpec((l, dm), lambda i: (rev(i), 0)),
                   pl.BlockSpec((l, dm), lambda i: (rev(i), 0)), pl.BlockSpec((l, dm), lambda i: (rev(i), 0)),
                   pl.BlockSpec((l, 128), lambda i: (rev(i), 0)), pl.BlockSpec((8, l), lambda i: (0, rev(i))),
                   pl.BlockSpec((1, dm), lambda i: (0, 0))],
        out_shape=[jax.ShapeDtypeStruct((t, dm), F32),
                   jax.ShapeDtypeStruct((t, dm), BF16), jax.ShapeDtypeStruct((t, dm), BF16),
                   jax.ShapeDtypeStruct((t, 128), F32), jax.ShapeDtypeStruct((8, t), F32),
                   jax.ShapeDtypeStruct((1, dm), F32)],
        scratch_shapes=[pltpu.VMEM((NH_M, DK_M, 2 * DV_M), F32)],
        compiler_params=_cparams(("arbitrary",)),
    )(qk, qk, zbig, zbig, zs, zsr, bc, br, gm, cst, mst, cst, mst, dycat)


def _fox_cumsum(zsr, bf_r):
    t = zsr.shape[1]
    cw = _pick(t, (512, 256))

    def body(z_ref, b_ref, c_ref):
        up = _tri(cw, False).astype(F32)
        carry = jnp.zeros((NH_F, 1), F32)
        for j in range(t // cw):
            cs = _f32dot(_logsig(z_ref[:, j * cw:(j + 1) * cw] + b_ref[...]), up) + carry
            c_ref[:, j * cw:(j + 1) * cw] = cs
            carry = cs[:, cw - 1:cw]

    return pl.pallas_call(
        body, name="fox_cumsum", grid=(1,),
        in_specs=[pl.BlockSpec((NH_F, t), lambda i: (1, 0)), pl.BlockSpec((NH_F, 1), lambda i: (0, 0))],
        out_specs=pl.BlockSpec((NH_F, t), lambda i: (0, 0)), out_shape=jax.ShapeDtypeStruct((NH_F, t), F32),
        compiler_params=_cparams(("arbitrary",)),
    )(zsr, bf_r)


def _fox_gate_bwd(zsr, bf_r, dc):
    t = zsr.shape[1]
    cw = _pick(t, (512, 256))

    def body(z_ref, b_ref, dc_ref, o_ref):
        low = _tri(cw, True).astype(F32)
        carry = jnp.zeros((NH_F, 1), F32)
        for j in reversed(range(t // cw)):
            sl = slice(j * cw, (j + 1) * cw)
            dlf = _f32dot(dc_ref[:, sl], low) + carry
            o_ref[:, sl] = dlf * _sigmoid(-(z_ref[:, sl] + b_ref[...]))
            carry = dlf[:, 0:1]

    return pl.pallas_call(
        body, name="fox_gate_bwd", grid=(1,),
        in_specs=[pl.BlockSpec((NH_F, t), lambda i: (1, 0)), pl.BlockSpec((NH_F, 1), lambda i: (0, 0)),
                  pl.BlockSpec((NH_F, t), lambda i: (0, 0))],
        out_specs=pl.BlockSpec((NH_F, t), lambda i: (0, 0)), out_shape=jax.ShapeDtypeStruct((NH_F, t), F32),
        compiler_params=_cparams(("arbitrary",)),
    )(zsr, bf_r, dc)


def _causal_mask(n):
    return _tri(n, True)


AUG = 64


def _split3(c):
    hi = c.astype(BF16).astype(F32)
    r1 = c - hi
    mid = r1.astype(BF16).astype(F32)
    return hi, mid, r1 - mid


def _fox_prep(zbig, ct):
    t = zbig.shape[0]
    tm = _pick(t, (512, 256))

    def body(q_ref, k_ref, v_ref, c_ref, qo_ref, ko_ref, vo_ref):
        lane = lax.broadcasted_iota(jnp.int32, (tm, AUG), 1)
        qv, kv, vv, cv = q_ref[...], k_ref[...], v_ref[...], c_ref[...]
        one = (lane == 0).astype(BF16)
        for h in range(NH_F):
            hi, mid, lo = _split3(cv[:, h:h + 1])
            aq = jnp.where(lane == 0, hi, jnp.where(lane == 1, mid, jnp.where(lane == 2, lo, jnp.where(lane < 6, 1.0, 0.0))))
            ak = jnp.where(lane < 3, 1.0, jnp.where(lane == 3, -hi, jnp.where(lane == 4, -mid, jnp.where(lane == 5, -lo, 0.0))))
            sl = slice(h * DH_F, (h + 1) * DH_F)
            qo_ref[h] = jnp.concatenate([qv[:, sl] * (DH_F ** -0.5), aq.astype(BF16)], axis=1).astype(BF16)
            ko_ref[h] = jnp.concatenate([kv[:, sl], ak.astype(BF16)], axis=1)
            vo_ref[h] = jnp.concatenate([vv[:, sl], one], axis=1)

    ospec = pl.BlockSpec((NH_F, tm, 128), lambda i: (0, i, 0))
    return pl.pallas_call(
        body, name="fox_prep", grid=(t // tm,),
        in_specs=[pl.BlockSpec((tm, 512), lambda i: (i, 3)), pl.BlockSpec((tm, 512), lambda i: (i, 4)),
                  pl.BlockSpec((tm, 512), lambda i: (i, 5)), pl.BlockSpec((tm, NH_F), lambda i: (i, 0))],
        out_specs=[ospec] * 3, out_shape=[jax.ShapeDtypeStruct((NH_F, t, 128), BF16)] * 3,
        compiler_params=_cparams(("parallel",)),
    )(zbig, zbig, zbig, ct)


def _fox_fwd2(qa, ka, va, gf, plan=None):
    nh, t, _ = qa.shape
    tq = _pick(t, (512, 256))
    nq = t // tq
    group = 4
    host = _Hosted(plan, 4, 3)

    def body(*refs):
        (q_ref, k_ref, v_ref, g_ref), (y_ref, o_ref, lse_ref), _, prefs = host.split(refs)
        i = pl.program_id(0)
        host.run(0, i == 0, prefs)
        host.run(1, i == max(nq - 2, 0), prefs)
        lane = lax.broadcasted_iota(jnp.int32, (tq, 128), 1)
        causal = _causal_mask(tq)
        ys, os_ = [], []
        lse_all = jnp.zeros((tq, 128), F32)
        for h0 in range(0, nh, group):
            heads = range(h0, h0 + group)
            qvs = [q_ref[h] for h in heads]

            def blk(j, carry, masked, heads=heads, qvs=qvs):
                k0 = pl.multiple_of(j * tq, tq)
                out = []
                for (m, acc), h, qv in zip(carry, heads, qvs):
                    s = lax.dot_general(qv, k_ref[h, pl.ds(k0, tq), :], (((1,), (1,)), ((), ())), preferred_element_type=F32)
                    if masked:
                        s = jnp.where(causal, s, -jnp.inf)
                    m_new = jnp.maximum(m, jnp.max(s, axis=1, keepdims=True))
                    p = jnp.exp(s - m_new).astype(BF16)
                    pv = lax.dot_general(p, v_ref[h, pl.ds(k0, tq), :], (((1,), (0,)), ((), ())), preferred_element_type=F32)
                    out.append((m_new, jnp.exp(m - m_new) * acc + pv))
                return tuple(out)

            init = tuple((jnp.full((tq, 1), -jnp.inf, F32), jnp.zeros((tq, 128), F32)) for _ in heads)
            carry = lax.fori_loop(0, i, lambda j, c: blk(j, c, False), init)
            for (m, acc), h in zip(blk(i, carry, True), heads):
                l = acc[:, DH_F:DH_F + 1]
                o = acc[:, :DH_F] / l
                os_.append(o)
                gh = g_ref[:, h * DH_F:(h + 1) * DH_F]
                ys.append(o * lax.rsqrt(jnp.mean(o * o, axis=-1, keepdims=True) + EPS) * gh)
                lse_all = lse_all + jnp.where(lane == h, m + jnp.log(l), 0.0)
        y_ref[...] = jnp.concatenate(ys, axis=1).astype(y_ref.dtype)
        o_ref[...] = jnp.concatenate(os_, axis=1)
        lse_ref[...] = lse_all
        host.run(2, i == nq - 1, prefs)

    full = pl.BlockSpec((nh, t, 128), lambda i: (0, 0, 0))
    hc = host.call_args()
    res = pl.pallas_call(
        body, name="fox_fwd", grid=(nq,),
        in_specs=[pl.BlockSpec((nh, tq, 128), lambda i: (0, i, 0)), full, full, pl.BlockSpec((1, nh * DH_F), lambda i: (0, 0))]
        + hc["in_specs"],
        out_specs=[pl.BlockSpec((tq, nh * DH_F), lambda i: (i, 0)), pl.BlockSpec((tq, nh * DH_F), lambda i: (i, 0)),
                   pl.BlockSpec((tq, 128), lambda i: (i, 0))] + hc["out_specs"],
        out_shape=[jax.ShapeDtypeStruct((t, nh * DH_F), BF16), jax.ShapeDtypeStruct((t, nh * DH_F), F32),
                   jax.ShapeDtypeStruct((t, 128), F32)] + hc["out_shape"],
        scratch_shapes=hc["scratch"], input_output_aliases=hc["aliases"], compiler_params=_cparams(("arbitrary",)),
    )(qa, ka, va, gf, *hc["args"])
    return res[:3], res[3:]


def _fox_bwd_prep(dycat, o, gf):
    t = o.shape[0]
    tm = _pick(t, (512, 256))

    def body(dy_ref, o_ref, g_ref, do_ref, dl_ref, dg_ref):
        lane = lax.broadcasted_iota(jnp.int32, (tm, 128), 1)
        dyv, ov, gv = dy_ref[...], o_ref[...], g_ref[...]
        dgs = []
        dl = jnp.zeros((tm, 128), F32)
        pad = jnp.zeros((tm, AUG), BF16)
        for h in range(NH_F):
            sl = slice(h * DH_F, (h + 1) * DH_F)
            dx, dg = _rms_bwd_val(dyv[:, sl], ov[:, sl], gv[:, sl])
            dgs.append(dg)
            do_ref[h] = jnp.concatenate([dx.astype(BF16), pad], axis=1)
            dl = dl + jnp.where(lane == h, jnp.sum(dx * ov[:, sl], axis=-1, keepdims=True), 0.0)
        dl_ref[...] = dl

        @pl.when(pl.program_id(0) == 0)
        def _():
            dg_ref[...] = jnp.zeros_like(dg_ref)
        dg_ref[...] += jnp.concatenate(dgs, axis=1)

    return pl.pallas_call(
        body, name="fox_bwd_prep", grid=(t // tm,),
        in_specs=[pl.BlockSpec((tm, 512), lambda i: (i, 1)), pl.BlockSpec((tm, 512), lambda i: (i, 0)),
                  pl.BlockSpec((1, 512), lambda i: (0, 0))],
        out_specs=[pl.BlockSpec((NH_F, tm, 128), lambda i: (0, i, 0)), pl.BlockSpec((tm, 128), lambda i: (i, 0)),
                   pl.BlockSpec((1, 512), lambda i: (0, 0))],
        out_shape=[jax.ShapeDtypeStruct((NH_F, t, 128), BF16), jax.ShapeDtypeStruct((t, 128), F32),
                   jax.ShapeDtypeStruct((1, 512), F32)],
        compiler_params=_cparams(("arbitrary",)),
    )(dycat, o, gf)


def _fox_bwd2(qa, ka, va, doa, lse, delta, plan=None):
    nh, t, _ = qa.shape
    tq = _pick(t, (512, 256))
    nq = t // tq

    group = 2

    def tdot(a, b, cb):
        return lax.dot_general(a, b, (((0,), (cb,)), ((), ())), preferred_element_type=F32)

    host = _Hosted(plan, 6, 3)
    ng = nh // group

    def body(*refs):
        (q_ref, k_ref, v_ref, do_ref, lse_ref, dl_ref), (dq_ref, dk_ref, dv_ref), _, prefs = host.split(refs)
        hp, j = pl.program_id(0), pl.program_id(1)
        host.run(0, (hp == 0) & (j == 0), prefs)
        host.run(1, (hp == 0) & (j == 0), prefs)

        @pl.when(j == 0)
        def _():
            dq_ref[...] = jnp.zeros_like(dq_ref)

        lane = lax.broadcasted_iota(jnp.int32, (tq, 128), 1)
        causal = _causal_mask(tq)

        def blk(i, carry, masked):
            rows = pl.ds(pl.multiple_of(i * tq, tq), tq)
            lse_t, dl_t = lse_ref[rows, :], dl_ref[rows, :]
            out = []
            for g, (dk, dv) in enumerate(carry):
                h = hp * group + g
                kb, vb = k_ref[g], v_ref[g]
                qb, dob = q_ref[g, rows, :], do_ref[g, rows, :]
                lse_h = jnp.sum(jnp.where(lane == h, lse_t, 0.0), axis=1, keepdims=True)
                dl_h = jnp.sum(jnp.where(lane == h, dl_t, 0.0), axis=1, keepdims=True)
                s = lax.dot_general(qb, kb, (((1,), (1,)), ((), ())), preferred_element_type=F32)
                if masked:
                    s = jnp.where(causal, s, -jnp.inf)
                p = jnp.exp(s - lse_h)
                dp = lax.dot_general(dob, vb, (((1,), (1,)), ((), ())), preferred_element_type=F32)
                ds = (p * (dp - dl_h)).astype(BF16)
                dv = dv + tdot(dob, p.astype(BF16), 0)
                dk = dk + tdot(qb, ds, 0)
                dq_ref[g, :, rows] += tdot(kb, ds, 1)
                out.append((dk, dv))
            return tuple(out)

        init = tuple((jnp.zeros((128, tq), F32), jnp.zeros((128, tq), F32)) for _ in range(group))
        carry = blk(j, init, True)
        carry = lax.fori_loop(j + 1, nq, lambda i, c: blk(i, c, False), carry)
        for g, (dk, dv) in enumerate(carry):
            dk_ref[g] = dk
            dv_ref[g] = dv
        host.run(2, (hp == ng - 1) & (j == nq - 1), prefs)

    full = pl.BlockSpec((group, t, 128), lambda h, j: (h, 0, 0))
    tile = pl.BlockSpec((group, tq, 128), lambda h, j: (h, j, 0))
    cols = pl.BlockSpec((t, 128), lambda h, j: (0, 0))
    full_t = pl.BlockSpec((group, 128, t), lambda h, j: (h, 0, 0))
    tile_t = pl.BlockSpec((group, 128, tq), lambda h, j: (h, 0, j))
    hc = host.call_args()
    res = pl.pallas_call(
        body, name="fox_bwd", grid=(ng, nq), in_specs=[full, tile, tile, full, cols, cols] + hc["in_specs"],
        out_specs=[full_t, tile_t, tile_t] + hc["out_specs"],
        out_shape=[jax.ShapeDtypeStruct((nh, 128, t), F32)] * 3 + hc["out_shape"], scratch_shapes=hc["scratch"],
        input_output_aliases=hc["aliases"], compiler_params=_cparams(("arbitrary", "arbitrary")),
    )(qa, ka, va, doa, lse, delta, *hc["args"])
    return res[:3], res[3:]


def _fox_bwd_post(dqa, dka, dva):
    nh, _, t = dqa.shape
    tm = _pick(t, (512, 256))

    def body(dq_ref, dk_ref, dv_ref, oq_ref, ok_ref, ov_ref, dc_ref):
        qs, ks, vs, dcs = [], [], [], []
        for h in range(nh):
            dq, dk = dq_ref[h], dk_ref[h]
            qs.append(dq.T[:, :DH_F] * (DH_F ** -0.5))
            ks.append(dk.T[:, :DH_F])
            vs.append(dv_ref[h].T[:, :DH_F])
            dcs.append(dq[DH_F:DH_F + 1, :] - dk[DH_F + 3:DH_F + 4, :])
        oq_ref[...] = jnp.concatenate(qs, axis=1).astype(BF16)
        ok_ref[...] = jnp.concatenate(ks, axis=1).astype(BF16)
        ov_ref[...] = jnp.concatenate(vs, axis=1).astype(BF16)
        dc_ref[...] = jnp.concatenate(dcs, axis=0)

    ispec = pl.BlockSpec((nh, 128, tm), lambda i: (0, 0, i))
    ospec = pl.BlockSpec((tm, nh * DH_F), lambda i: (i, 0))
    return pl.pallas_call(
        body, name="fox_bwd_post", grid=(t // tm,), in_specs=[ispec] * 3,
        out_specs=[ospec] * 3 + [pl.BlockSpec((nh, tm), lambda i: (0, i))],
        out_shape=[jax.ShapeDtypeStruct((t, nh * DH_F), BF16)] * 3 + [jax.ShapeDtypeStruct((nh, t), F32)],
        compiler_params=_cparams(("parallel",)),
    )(dqa, dka, dva)


IN_OFF = (0, 512, 1024, 1544, 2056, 2568)
IN_GATES = (1536, 3080)


FFN1 = ("ffn1_w_gate", "ffn1_w_up", "ffn1_w_down")
REST = ("w_in", "w_out", "ffn2_w_gate", "ffn2_w_up", "ffn2_w_down", "w_ple_gate", "w_ple_proj")
SPLIT = {n: 1 if n == "w_in" else 0 for n in FFN1 + REST}
SAME_SHAPE = (FFN1, ("ffn2_w_gate", "ffn2_w_up", "ffn2_w_down"), ("w_out", "w_ple_gate"), ("w_in",), ("w_ple_proj",))


def _grouped(names):
    return [tuple(n for n in grp if n in names) for grp in SAME_SHAPE if any(n in names for n in grp)]


def _rs_partials(names, gw, c_idx, twins, run_swap=None):
    wire = [twins[n] if n in twins else _cast_other_half("rs_cast_" + n, gw[n], c_idx, SPLIT[n]) for n in names]
    plan = _swap_plan(wire, [SPLIT[n] if n in twins else None for n in names])
    swapped = dict(zip(names, run_swap(plan) if run_swap else _run_plan("rs_swap_" + names[0], plan)))
    out = {}
    for grp in _grouped(names):
        res = _add_my_half("rs_add_" + grp[0], [gw[n] for n in grp], [swapped[n] for n in grp], c_idx, SPLIT[grp[0]])
        out.update(zip(grp, res))
    return [out[n] for n in names]


def _local_step(x, p, tgt, sp, wg1, wu1, wd1, rest_slots, c_idx, place):
    t, d = x.shape
    slot = dict(zip(REST + ("conv_qk",), rest_slots))
    (h1, xn1, g1, u1), (w_in, conv_w) = _ffn_fwd(
        "ffn1", x, sp["ffn1_norm"], wg1, wu1, wd1, plan=_gather_plan([slot["w_in"], slot["conv_qk"]], [SPLIT["w_in"], None]))
    w_in, conv_w = w_in.reshape(-1, d), _from_chip_blocks(conv_w)
    w_big = jnp.concatenate([w_in[o:o + 512] for o in IN_OFF], axis=0)
    w_small = jnp.concatenate([w_in[IN_GATES[0]:IN_GATES[0] + 8], w_in[IN_GATES[1]:IN_GATES[1] + 8],
                               jnp.zeros((112, d), w_in.dtype)], axis=0)
    u, zbig = _norm_mm("in_big", h1, sp["mix_norm"], w_big, True, BF16)
    zs = _mm_nt("in_small", u, w_small, tm=1024, tk=1024)
    zsr = zs.T
    qk_act = _conv_fwd(zbig, conv_w)
    bm_c, bf_c = sp["b_mlstm_gates"], sp["b_fox_f"]
    y_m, cst, mst = _mlstm_fwd(qk_act, zbig, zs, zsr, bm_c, bm_c.T, sp["mlstm_out_norm"])
    c = _fox_cumsum(zsr, bf_c.T)
    qa, ka, va = _fox_prep(zbig, c.T)
    (y_ft, o_f, lse), late = _fox_fwd2(qa, ka, va, sp["fox_out_norm"],
                                       plan=_gather_plan([slot[n] for n in REST[1:]], [SPLIT[n] for n in REST[1:]]))
    full = dict(zip(REST[1:], late))
    w_out, w_pg = (full[n].reshape(-1, d) for n in ("w_out", "w_ple_gate"))
    wg2, wu2, wd2 = full["ffn2_w_gate"], full["ffn2_w_up"], full["ffn2_w_down"]
    w_pp = _from_chip_blocks(full["w_ple_proj"])
    tm = _pick(t, (1024, 512, 256))
    h2 = _mm("out_proj", [(y_m, (tm, 512), lambda i, j, k: (i, 0), w_out, (512, d), lambda i, j, k: (0, 0)),
                          (y_ft, (tm, 512), lambda i, j, k: (i, 0), w_out, (512, d), lambda i, j, k: (1, 0))],
             (t, d), (tm, d), lambda i, j, k: (i, 0), (t // tm, 1, 1), 2, res=h1)
    (h3, xn2, g2, u2), _ = _ffn_fwd("ffn2", h2, sp["ffn2_norm"], wg2, wu2, wd2)
    hn3, gate_pre = _norm_mm("ple_gate", h3, sp["ple_gate_norm"], w_pg, False, F32)
    pp = _mm_nn("ple_proj", p, w_pp, tm=1024)

    def head_fn(h3_t, gp_t, pp_t, tgt_t, g_pp, g_fin):
        gate = _sigmoid(gp_t)
        ppn = _rms_fwd_val(pp_t, g_pp)
        h4 = h3_t + gate * ppn
        err = _rms_fwd_val(h4, g_fin) - tgt_t
        loss = 0.5 * jnp.sum(jnp.mean(err * err, axis=-1, keepdims=True), axis=0, keepdims=True)
        dh4, dg_fin = _rms_bwd_val(err * (1.0 / d), h4, g_fin)
        dpp, dg_pp = _rms_bwd_val(dh4 * gate, pp_t, g_pp)
        dgp = dh4 * ppn * gate * (1.0 - gate)
        return dh4, dgp, dpp, jnp.broadcast_to(loss, (1, 128)), dg_fin, dg_pp

    dh4, dgp, dpp, loss_part, dg_fin, dg_pp = _rowwise(
        "loss_head", head_fn, [h3, gate_pre, pp, tgt], [sp["ple_proj_norm"], sp["final_norm"]],
        [(d, F32), (d, BF16), (d, BF16)], [((1, 128), F32), ((1, d), F32), ((1, d), F32)])
    gw, gs = {}, {"final_norm": dg_fin, "ple_proj_norm": dg_pp}
    gw["w_ple_gate"] = _mm_tn("d_w_pg", hn3, dgp, tm=1024, tn=1024)
    gw["w_ple_proj"] = _mm_tn("d_w_pp", p, dpp, tn=1024)
    dhn3 = _mm_nt("d_hn3", dgp, w_pg, tm=1024, tn=1024, tk=1024)

    def res_norm_bwd(dn_t, h_t, dres_t, g):
        dx, dg = _rms_bwd_val(dn_t, h_t, g)
        return dres_t + dx, dg

    dh3, gs["ple_gate_norm"] = _rowwise("ple_norm_bwd", res_norm_bwd, [dhn3, h3, dh4], [sp["ple_gate_norm"]],
                                        [(d, F32)], [((1, d), F32)])
    (dh2, gs["ffn2_norm"], gw["ffn2_w_gate"], gw["ffn2_w_up"], gw["ffn2_w_down"]), _, twins2 = _ffn_bwd(
        "ffn2", dh3, h2, sp["ffn2_norm"], xn2, g2, u2, wg2, wu2, wd2)
    ffn2_names = ("ffn2_w_gate", "ffn2_w_up", "ffn2_w_down")
    early = []

    def swap_in_d_ycat(plan):
        dyc, swapped = _mm_nt("d_ycat", dh2, w_out, tm=1024, tn=1024, tk=1024, plan=plan)
        early.append(dyc)
        return swapped

    part_ffn2 = dict(zip(ffn2_names, _rs_partials(ffn2_names, gw, c_idx, dict(zip(ffn2_names, twins2)), swap_in_d_ycat)))
    dycat = early[0]
    gw["w_out"] = jnp.concatenate([_mm_tn("d_w_out_m", y_m, dh2, tn=1024, tk=2048),
                                   _mm_tn("d_w_out_f", y_ft, dh2, tn=1024, tk=2048)], axis=0)
    doa, delta, gs["fox_out_norm"] = _fox_bwd_prep(dycat, o_f, sp["fox_out_norm"])
    dqkv_t, landed_ffn2 = _fox_bwd2(qa, ka, va, doa, lse, delta, plan=_scatter_plan([part_ffn2[n][1] for n in ffn2_names]))
    dq_f, dk_f, dv_f, dct = _fox_bwd_post(*dqkv_t)
    dfp = _fox_gate_bwd(zsr, bf_c.T, dct)
    dact, dv_m, do_m, dzs_m, dzr_m, gs["mlstm_out_norm"] = _mlstm_bwd(
        qk_act, zbig, zs, zsr, bm_c, bm_c.T, sp["mlstm_out_norm"], cst, mst, dycat)
    dqk, gw["conv_qk"] = _conv_bwd(zbig, dact, conv_w)
    dz_big = jnp.concatenate([dqk, dv_m, do_m, dq_f, dk_f, dv_f], axis=1)
    dzs = dzs_m + jnp.pad(jnp.concatenate([dzr_m, dfp], axis=0).T, ((0, 0), (0, 112)))
    dw_big = _mm_tn("d_w_big", dz_big, u, tn=1024)
    dw_small = _mm_tn("d_w_small", dzs, u, tn=1024)
    gw["w_in"] = jnp.concatenate([dw_big[0:1536], dw_small[0:8], dw_big[1536:3072], dw_small[8:16]], axis=0)
    du_a = _mm_nn("d_u_big", dz_big, w_big, tm=1024, tn=1024, tk=3072)
    du_b = _mm_nn("d_u_small", dzs, w_small, tm=1024, tn=1024)

    def mix_norm_bwd(da_t, db_t, h_t, dres_t, dzs_t, g):
        dx, dg = _rms_bwd_val(da_t + db_t, h_t, g)
        return dres_t + dx, dg, _colsum(dzs_t)

    conv_grad = gw.pop("conv_qk")
    gw["w_ple_proj"] = _chip_blocks(gw["w_ple_proj"])
    for n in ("w_in", "w_out", "w_ple_gate"):
        gw[n] = gw[n].reshape(4, -1, gw[n].shape[-1])
    mix = []

    def swap_in_mix_norm_bwd(plan):
        res, swapped = _rowwise("mix_norm_bwd", mix_norm_bwd, [du_a, du_b, h1, dh2, dzs], [sp["mix_norm"]],
                                [(d, F32)], [((1, d), F32), ((1, 128), F32)], plan=plan)
        mix.extend(res)
        return swapped

    light = ("w_in", "w_out", "w_ple_gate", "w_ple_proj")
    part_light = dict(zip(light, _rs_partials(light, gw, c_idx, {}, swap_in_mix_norm_bwd)))
    dh1, gs["mix_norm"], dbias = mix
    gs["b_mlstm_gates"], gs["b_fox_f"] = dbias[:, 0:8], dbias[:, 8:16]
    part_ffn1 = []

    def own_plan(dws, dw_twins):
        part_ffn1.extend(_rs_partials(FFN1, dict(zip(FFN1, dws)), c_idx, dict(zip(FFN1, dw_twins))))
        return _scatter_plan([pb for _, pb in part_ffn1])

    (grad_x, gs["ffn1_norm"], _, _, _), (l_light, _, _, _, landed_ffn1) = _ffn_bwd_late_dx(
        "ffn1", dh1, x, sp["ffn1_norm"], xn1, g1, u1, wg1, wu1, wd1,
        _scatter_plan([part_light[n][1] for n in light]), [None] * 3, own_plan)
    names = REST + FFN1
    parts = {**part_light, **part_ffn2, **dict(zip(FFN1, part_ffn1))}
    landed = {**dict(zip(light, l_light)), **dict(zip(ffn2_names, landed_ffn2)), **dict(zip(FFN1, landed_ffn1))}
    mine = {}
    for grp in _grouped(names):
        res = _sum4("rs_sum_" + grp[0], [landed[n] for n in grp], [parts[n][0] for n in grp], place, SPLIT[grp[0]])
        mine.update(zip(grp, res))
    grads = dict(zip(names, _join_halves("rs_join", [mine[n] for n in names], [SPLIT[n] for n in names])))
    return loss_part, grad_x, grads, gs, conv_grad


ANY = pl.BlockSpec(memory_space=pl.ANY)
MESH = pl.DeviceIdType.MESH


def _place():
    x, y, c = lax.axis_index("x"), lax.axis_index("y"), lax.axis_index("c")
    chips = [(1 - x, y), (x, 1 - y), (1 - x, 1 - y)]
    return x, y, c, 2 * x + y, (x, y, 1 - c), chips


def _rcopy(src, dst, ssem, rsem, dev):
    return pltpu.make_async_remote_copy(src_ref=src, dst_ref=dst, send_sem=ssem, recv_sem=rsem, device_id=dev,
                                        device_id_type=MESH)


def _half(ref, lead, axis, idx, half):
    return ref.at[(slice(None),) * (lead + axis) + (pl.ds(idx * half, half),)]


def _to_slot(name, arrs, me_idx, dtype):
    n = len(arrs)
    r, cdim = arrs[0].shape
    tr = _pick(r, (352, 256, 176, 128, 64))

    def body(me_ref, *refs):
        for k in range(n):
            refs[n + k][...] = refs[k][...].astype(dtype)

    return pl.pallas_call(
        body, name=name,
        grid_spec=pltpu.PrefetchScalarGridSpec(
            num_scalar_prefetch=1, grid=(r // tr,), in_specs=[pl.BlockSpec((tr, cdim), lambda i, me_ref: (i, 0))] * n,
            out_specs=[pl.BlockSpec((None, tr, cdim), lambda i, me_ref: (me_ref[0], i, 0))] * n),
        out_shape=[jax.ShapeDtypeStruct((4, r, cdim), dtype)] * n, compiler_params=_cparams(("parallel",)),
    )(me_idx, *arrs)


def _gather4(name, bufs, split):
    return _run_plan(name, _gather_plan(bufs, split))


def _gather_plan(bufs, split):
    n = len(bufs)
    shapes = [b.shape[1:] for b in bufs]

    def ctx(outs):
        x, y, c, me, sib, chips = _place()

        def part(ref, a, which):
            if split[a] is None:
                return ref
            return _half(ref, 0, split[a], which, shapes[a][split[a]] // 2)

        return c, me, sib, chips, part

    def ici(outs, sems, a, j, chip, c, me, part):
        mine = part(outs[a].at[me], a, c)
        return _rcopy(mine, mine, sems[0].at[3 * a + j], sems[1].at[3 * a + j], (*chip, c))

    def fwd(outs, sems, a, j, chip, c, sib, part, which):
        blk = part(outs[a].at[2 * chip[0] + chip[1]], a, which)
        return _rcopy(blk, blk, sems[2].at[3 * a + j], sems[3].at[3 * a + j], sib)

    def start(ins, outs, sems):
        c, me, sib, chips, part = ctx(outs)
        for a in range(n):
            for j, chip in enumerate(chips):
                ici(outs, sems, a, j, chip, c, me, part).start()

    def mid(ins, outs, sems):
        c, me, sib, chips, part = ctx(outs)
        for j, chip in enumerate(chips):
            for a in range(n):
                blk = part(outs[a].at[2 * chip[0] + chip[1]], a, c)
                _rcopy(blk, blk, sems[0].at[3 * a + j], sems[1].at[3 * a + j], sib).wait_recv()
                if split[a] is not None:
                    fwd(outs, sems, a, j, chip, c, sib, part, c).start()

    def end(ins, outs, sems):
        c, me, sib, chips, part = ctx(outs)
        for j, chip in enumerate(chips):
            for a in range(n):
                if split[a] is not None:
                    fwd(outs, sems, a, j, chip, c, sib, part, 1 - c).wait_recv()
        for a in range(n):
            for j, chip in enumerate(chips):
                ici(outs, sems, a, j, chip, c, me, part).wait_send()
                if split[a] is not None:
                    fwd(outs, sems, a, j, chip, c, sib, part, c).wait_send()

    return dict(ins=list(bufs), outs=[jax.ShapeDtypeStruct(b.shape, b.dtype) for b in bufs], alias=True,
                sems=[pltpu.SemaphoreType.DMA((3 * n,))] * 4, phases=(start, mid, end))


def _run_plan(name, plan):
    ni, no = len(plan["ins"]), len(plan["outs"])

    def body(*refs):
        ins, outs, sems = refs[:ni], refs[ni:ni + no], refs[ni + no:]
        for phase in plan["phases"]:
            phase(ins, outs, sems)

    return pl.pallas_call(
        body, name=name, in_specs=[ANY] * ni, out_specs=[ANY] * no, out_shape=plan["outs"],
        input_output_aliases={a: a for a in range(ni)} if plan["alias"] else {}, scratch_shapes=plan["sems"],
    )(*plan["ins"])


class _Hosted:
    def __init__(self, plan, n_in, n_out):
        self.plan, self.n_in, self.n_out = plan, n_in, n_out
        self.ni, self.no, self.ns = (len(plan["ins"]) if plan else 0, len(plan["outs"]) if plan else 0,
                                     len(plan["sems"]) if plan else 0)

    def split(self, refs):
        a, b = self.n_in, self.n_in + self.ni
        c, d = b + self.n_out, b + self.n_out + self.no
        e = len(refs) - self.ns
        return refs[:a], refs[b:c], refs[d:e], (refs[a:b], refs[c:d], refs[e:])

    def run(self, k, cond, prefs):
        if self.plan is not None:
            @pl.when(cond)
            def _():
                self.plan["phases"][k](*prefs)

    def call_args(self):
        p = self.plan
        if p is None:
            return dict(in_specs=[], out_specs=[], out_shape=[], scratch=[], aliases={}, args=[])
        al = {self.n_in + a: self.n_out + a for a in range(self.ni)} if p["alias"] else {}
        return dict(in_specs=[ANY] * self.ni, out_specs=[ANY] * self.no, out_shape=list(p["outs"]), scratch=list(p["sems"]),
                    aliases=al, args=list(p["ins"]))


def _swap(name, arrs, halve):
    return _run_plan(name, _swap_plan(arrs, halve))


def _swap_plan(arrs, halve):
    n = len(arrs)

    def half_shape(a, ax):
        return a.shape if ax is None else (a.shape[0],) + tuple(d // 2 if i == ax else d for i, d in enumerate(a.shape[1:]))

    def copies(ins, outs, sems):
        x, y, c, me, sib, chips = _place()
        cps = []
        for a in range(n):
            src = ins[a] if halve[a] is None else _half(ins[a], 1, halve[a], 1 - c, arrs[a].shape[1 + halve[a]] // 2)
            cps.append(_rcopy(src, outs[a], sems[0].at[a], sems[1].at[a], sib))
        return cps

    def start(ins, outs, sems):
        for cp in copies(ins, outs, sems):
            cp.start()

    def mid(ins, outs, sems):
        pass

    def end(ins, outs, sems):
        for cp in copies(ins, outs, sems):
            cp.wait()

    return dict(ins=list(arrs), outs=[jax.ShapeDtypeStruct(half_shape(a, ax), a.dtype) for a, ax in zip(arrs, halve)],
                alias=False, sems=[pltpu.SemaphoreType.DMA((n,))] * 2, phases=(start, mid, end))


def _scatter4(name, arrs):
    return _run_plan(name, _scatter_plan(arrs))


def _scatter_plan(arrs):
    n = len(arrs)

    def send(ins, outs, sems, a, j, chip, c, me):
        return _rcopy(ins[a].at[2 * chip[0] + chip[1]], outs[a].at[me], sems[0].at[3 * a + j], sems[1].at[3 * a + j], (*chip, c))

    def start(ins, outs, sems):
        x, y, c, me, sib, chips = _place()
        for a in range(n):
            for j, chip in enumerate(chips):
                send(ins, outs, sems, a, j, chip, c, me).start()

    def mid(ins, outs, sems):
        pass

    def end(ins, outs, sems):
        x, y, c, me, sib, chips = _place()
        for a in range(n):
            for j, chip in enumerate(chips):
                blk = outs[a].at[2 * chip[0] + chip[1]]
                _rcopy(blk, blk, sems[0].at[3 * a + j], sems[1].at[3 * a + j], sib).wait_recv()
        for a in range(n):
            for j, chip in enumerate(chips):
                send(ins, outs, sems, a, j, chip, c, me).wait_send()

    return dict(ins=list(arrs), outs=[jax.ShapeDtypeStruct(a.shape, a.dtype) for a in arrs], alias=False,
                sems=[pltpu.SemaphoreType.DMA((3 * n,))] * 2, phases=(start, mid, end))


def _join_halves(name, arrs, split):
    n = len(arrs)

    def body(*refs):
        outs = refs[n:2 * n]
        ssem, rsem = refs[2 * n:]
        x, y, c, me, sib, chips = _place()
        cps = []
        for a in range(n):
            mine = _half(outs[a], 0, split[a], c, arrs[a].shape[split[a]] // 2)
            cp = _rcopy(mine, mine, ssem.at[a], rsem.at[a], sib)
            cp.start()
            cps.append(cp)
        for a in range(n):
            blk = _half(outs[a], 0, split[a], 1 - c, arrs[a].shape[split[a]] // 2)
            _rcopy(blk, blk, ssem.at[a], rsem.at[a], sib).wait_recv()
        for cp in cps:
            cp.wait_send()

    return pl.pallas_call(
        body, name=name, in_specs=[ANY] * n, out_specs=[ANY] * n,
        out_shape=[jax.ShapeDtypeStruct(a.shape, a.dtype) for a in arrs],
        input_output_aliases={a: a for a in range(n)}, scratch_shapes=[pltpu.SemaphoreType.DMA((n,))] * 2,
    )(*arrs)


def _allreduce_small(s):
    r, cdim = s.shape

    def body(s_ref, o_ref, buf, ssem, rsem):
        x, y, c, me, sib, chips = _place()
        me8 = 4 * x + 2 * y + c
        buf[me8] = s_ref[...]
        flips = [(fx, fy, fc) for fx in (0, 1) for fy in (0, 1) for fc in (0, 1)][1:]
        cps = []
        for k, (fx, fy, fc) in enumerate(flips):
            peer = (x ^ fx if fx else x, y ^ fy if fy else y, c ^ fc if fc else c)
            cp = _rcopy(s_ref, buf.at[me8], ssem.at[k], rsem.at[k], peer)
            cp.start()
            cps.append(cp)
        for k, (fx, fy, fc) in enumerate(flips):
            src = 4 * (x ^ fx if fx else x) + 2 * (y ^ fy if fy else y) + (c ^ fc if fc else c)
            _rcopy(s_ref, buf.at[src], ssem.at[k], rsem.at[k], sib).wait_recv()
        for cp in cps:
            cp.wait_send()
        acc = buf[0]
        for k in range(1, 8):
            acc = acc + buf[k]
        o_ref[...] = acc

    vm = pl.BlockSpec(memory_space=pltpu.VMEM)
    return pl.pallas_call(
        body, name="allreduce_small", in_specs=[vm], out_specs=vm, out_shape=jax.ShapeDtypeStruct((r, cdim), F32),
        scratch_shapes=[pltpu.VMEM((8, r, cdim), F32), pltpu.SemaphoreType.DMA((7,)), pltpu.SemaphoreType.DMA((7,))],
    )(s)


def _add_my_half(name, gs, recvs, c_idx, axis):
    n = len(gs)
    nb, hr, hc = recvs[0].shape
    tr = _pick(hr, (256, 176, 128, 64))
    if axis == 0:
        g4s = [g.reshape(nb, 2, hr, hc) for g in gs]
        gspec = pl.BlockSpec((None, None, tr, hc), lambda b, i, c_ref: (b, c_ref[0], i, 0))
    else:
        g4s = list(gs)
        gspec = pl.BlockSpec((None, tr, hc), lambda b, i, c_ref: (b, i, c_ref[0]))

    def body(c_ref, *refs):
        for k in range(n):
            s = refs[k][...] + refs[n + k][...].astype(F32)
            refs[2 * n + 2 * k][...] = s
            refs[2 * n + 2 * k + 1][...] = s.astype(BF16)

    ospec = pl.BlockSpec((None, tr, hc), lambda b, i, c_ref: (b, i, 0))
    res = pl.pallas_call(
        body, name=name,
        grid_spec=pltpu.PrefetchScalarGridSpec(
            num_scalar_prefetch=1, grid=(nb, hr // tr), in_specs=[gspec] * n + [ospec] * n, out_specs=[ospec] * (2 * n)),
        out_shape=[jax.ShapeDtypeStruct((nb, hr, hc), F32), jax.ShapeDtypeStruct((nb, hr, hc), BF16)] * n,
        compiler_params=_cparams(("parallel", "parallel")),
    )(c_idx, *g4s, *recvs)
    return [(res[2 * k], res[2 * k + 1]) for k in range(n)]


def _sum4(name, landeds, owns, place, axis):
    n = len(landeds)
    nb, h, cdim = landeds[0].shape
    tr = _pick(h, (256, 176, 128, 64))
    nt = h // tr

    def body(p_ref, *refs):
        for k in range(n):
            a1, a2, a3, own = refs[4 * k:4 * k + 4]
            refs[4 * n + k][...] = ((own[...] + a1[...].astype(F32)) + a2[...].astype(F32)) + a3[...].astype(F32)

    def nxt(k):
        return pl.BlockSpec((None, tr, cdim), lambda i, p_ref: ((p_ref[0] + k) % nb, i, 0))

    if axis == 0:
        ospec = pl.BlockSpec((tr, cdim), lambda i, p_ref: (p_ref[1] * nt + i, 0))
        oshape = (2 * h, cdim)
    else:
        ospec = pl.BlockSpec((tr, cdim), lambda i, p_ref: (i, p_ref[1]))
        oshape = (h, 2 * cdim)
    args = []
    for landed, own in zip(landeds, owns):
        args += [landed, landed, landed, own]
    return pl.pallas_call(
        body, name=name,
        grid_spec=pltpu.PrefetchScalarGridSpec(
            num_scalar_prefetch=1, grid=(nt,), in_specs=[nxt(1), nxt(2), nxt(3), nxt(0)] * n, out_specs=[ospec] * n),
        out_shape=[jax.ShapeDtypeStruct(oshape, F32)] * n, compiler_params=_cparams(("parallel",)),
    )(place, *args)


def _cast_other_half(name, g, c_idx, axis):
    nb, r, cdim = g.shape
    hr, hc = (r // 2, cdim) if axis == 0 else (r, cdim // 2)
    tr = _pick(hr, (256, 176, 128, 64))
    if axis == 0:
        g4 = g.reshape(nb, 2, hr, hc)
        gspec = pl.BlockSpec((None, None, tr, hc), lambda b, i, c_ref: (b, 1 - c_ref[0], i, 0))
    else:
        g4 = g
        gspec = pl.BlockSpec((None, tr, hc), lambda b, i, c_ref: (b, i, 1 - c_ref[0]))

    def body(c_ref, g_ref, o_ref):
        o_ref[...] = g_ref[...].astype(BF16)

    return pl.pallas_call(
        body, name=name,
        grid_spec=pltpu.PrefetchScalarGridSpec(
            num_scalar_prefetch=1, grid=(nb, hr // tr), in_specs=[gspec],
            out_specs=pl.BlockSpec((None, tr, hc), lambda b, i, c_ref: (b, i, 0))),
        out_shape=jax.ShapeDtypeStruct((nb, hr, hc), BF16), compiler_params=_cparams(("parallel", "parallel")),
    )(c_idx, g4)


def _adamw(name, ws, gs, ms, vs):
    n = len(ws)
    c1 = 1.0 - ADAM_B1 ** ADAM_STEP
    c2 = 1.0 - ADAM_B2 ** ADAM_STEP

    def fn(*tiles):
        out = []
        for k in range(n):
            w_t, g_t, m_t, v_t = tiles[4 * k:4 * k + 4]
            m_n = ADAM_B1 * m_t + (1.0 - ADAM_B1) * g_t
            v_n = ADAM_B2 * v_t + (1.0 - ADAM_B2) * (g_t * g_t)
            out += [-ADAM_LR * ((m_n / c1) / (jnp.sqrt(v_n / c2) + ADAM_EPS) + ADAM_WD * w_t), m_n, v_n]
        return out

    rows, cdim = ws[0].shape
    tiled = [a for quad in zip(ws, gs, ms, vs) for a in quad]
    pref = (512, 352, 256, 128, 64, 8) if n == 1 else (176, 128, 64, 8)
    res = _rowwise(name, fn, tiled, [], [(cdim, F32)] * (3 * n), tm=_pick(rows, pref))
    return [tuple(res[3 * k:3 * k + 3]) for k in range(n)]


BIG = ("ffn1_w_gate", "ffn1_w_up", "ffn1_w_down", "w_in", "w_out", "ffn2_w_gate", "ffn2_w_up", "ffn2_w_down",
       "w_ple_gate", "w_ple_proj")
SMALL = ("ffn1_norm", "mix_norm", "b_mlstm_gates", "b_fox_f", "mlstm_out_norm", "fox_out_norm", "ffn2_norm",
         "ple_gate_norm", "ple_proj_norm", "final_norm")
WEIGHTS = ("ffn1_norm", "ffn1_w_gate", "ffn1_w_up", "ffn1_w_down", "mix_norm", "w_in", "conv_qk", "b_mlstm_gates",
           "b_fox_f", "mlstm_out_norm", "fox_out_norm", "w_out", "ffn2_norm", "ffn2_w_gate", "ffn2_w_up", "ffn2_w_down",
           "ple_gate_norm", "w_ple_gate", "w_ple_proj", "ple_proj_norm", "final_norm")
TRANSPOSED = ("ffn1_w_gate", "ffn1_w_up", "w_in", "ffn2_w_gate", "ffn2_w_up")
PACK_W = 1024


def _chip_blocks(a):
    r, c4 = a.shape
    return a.reshape(r, 4, c4 // 4).transpose(1, 0, 2)


def _from_chip_blocks(a):
    nb, r, c = a.shape
    return a.transpose(1, 0, 2).reshape(r, nb * c)


def kernel(x, p, ffn1_norm, ffn1_w_gate, ffn1_w_up, ffn1_w_down, mix_norm, w_in, conv_qk, b_mlstm_gates, b_fox_f, mlstm_out_norm, fox_out_norm, w_out, ffn2_norm, ffn2_w_gate, ffn2_w_up, ffn2_w_down, ple_gate_norm, w_ple_gate, w_ple_proj, ple_proj_norm, final_norm, loss_target, m_ffn1_norm, m_ffn1_w_gate, m_ffn1_w_up, m_ffn1_w_down, m_mix_norm, m_w_in, m_conv_qk, m_b_mlstm_gates, m_b_fox_f, m_mlstm_out_norm, m_fox_out_norm, m_w_out, m_ffn2_norm, m_ffn2_w_gate, m_ffn2_w_up, m_ffn2_w_down, m_ple_gate_norm, m_w_ple_gate, m_w_ple_proj, m_ple_proj_norm, m_final_norm, v_ffn1_norm, v_ffn1_w_gate, v_ffn1_w_up, v_ffn1_w_down, v_mix_norm, v_w_in, v_conv_qk, v_b_mlstm_gates, v_b_fox_f, v_mlstm_out_norm, v_fox_out_norm, v_w_out, v_ffn2_norm, v_ffn2_w_gate, v_ffn2_w_up, v_ffn2_w_down, v_ple_gate_norm, v_w_ple_gate, v_w_ple_proj, v_ple_proj_norm, v_final_norm):
    w = dict(ffn1_norm=ffn1_norm, ffn1_w_gate=ffn1_w_gate, ffn1_w_up=ffn1_w_up, ffn1_w_down=ffn1_w_down, mix_norm=mix_norm,
             w_in=w_in, conv_qk=conv_qk, b_mlstm_gates=b_mlstm_gates, b_fox_f=b_fox_f, mlstm_out_norm=mlstm_out_norm,
             fox_out_norm=fox_out_norm, w_out=w_out, ffn2_norm=ffn2_norm, ffn2_w_gate=ffn2_w_gate, ffn2_w_up=ffn2_w_up,
             ffn2_w_down=ffn2_w_down, ple_gate_norm=ple_gate_norm, w_ple_gate=w_ple_gate, w_ple_proj=w_ple_proj,
             ple_proj_norm=ple_proj_norm, final_norm=final_norm)
    m = dict(ffn1_norm=m_ffn1_norm, ffn1_w_gate=m_ffn1_w_gate, ffn1_w_up=m_ffn1_w_up, ffn1_w_down=m_ffn1_w_down,
             mix_norm=m_mix_norm, w_in=m_w_in, conv_qk=m_conv_qk, b_mlstm_gates=m_b_mlstm_gates, b_fox_f=m_b_fox_f,
             mlstm_out_norm=m_mlstm_out_norm, fox_out_norm=m_fox_out_norm, w_out=m_w_out, ffn2_norm=m_ffn2_norm,
             ffn2_w_gate=m_ffn2_w_gate, ffn2_w_up=m_ffn2_w_up, ffn2_w_down=m_ffn2_w_down, ple_gate_norm=m_ple_gate_norm,
             w_ple_gate=m_w_ple_gate, w_ple_proj=m_w_ple_proj, ple_proj_norm=m_ple_proj_norm, final_norm=m_final_norm)
    v = dict(ffn1_norm=v_ffn1_norm, ffn1_w_gate=v_ffn1_w_gate, ffn1_w_up=v_ffn1_w_up, ffn1_w_down=v_ffn1_w_down,
             mix_norm=v_mix_norm, w_in=v_w_in, conv_qk=v_conv_qk, b_mlstm_gates=v_b_mlstm_gates, b_fox_f=v_b_fox_f,
             mlstm_out_norm=v_mlstm_out_norm, fox_out_norm=v_fox_out_norm, w_out=v_w_out, ffn2_norm=v_ffn2_norm,
             ffn2_w_gate=v_ffn2_w_gate, ffn2_w_up=v_ffn2_w_up, ffn2_w_down=v_ffn2_w_down, ple_gate_norm=v_ple_gate_norm,
             w_ple_gate=v_w_ple_gate, w_ple_proj=v_w_ple_proj, ple_proj_norm=v_ple_proj_norm, final_norm=v_final_norm)
    shapes = {n: w[n].shape for n in WEIGHTS}

    def view(a, n):
        return a[0].T if n in TRANSPOSED else a.reshape(-1, a.shape[-1])

    def unview(a, n):
        return (a.T if n in TRANSPOSED else a).reshape(shapes[n])

    w2, m2, v2 = ({n: view(a, n) for n, a in d.items()} for d in (w, m, v))

    c_idx = lax.axis_index("c").astype(jnp.int32).reshape(1)
    me_idx = (2 * lax.axis_index("x") + lax.axis_index("y")).astype(jnp.int32).reshape(1)
    place = jnp.concatenate([me_idx, c_idx])
    slot = {}
    for grp in SAME_SHAPE:
        slot.update(zip(grp, _to_slot("slot_" + grp[0], [w2[n] for n in grp], me_idx, BF16)))
    slot["conv_qk"] = _to_slot("slot_conv_qk", [w2["conv_qk"]], me_idx, F32)[0]
    wg1, wu1, wd1 = _gather4("gather_ffn1", [slot[n] for n in FFN1], [SPLIT[n] for n in FFN1])
    sp = {n: w2[n] for n in SMALL}
    loss_part, grad_x, grads, gs, conv_grad = _local_step(
        x[0], p[0, 0], loss_target[0], sp, wg1, wu1, wd1, [slot[n] for n in REST + ("conv_qk",)], c_idx, place)

    small = [gs[n].reshape(1, -1) for n in SMALL] + [conv_grad, loss_part]
    rows = [jnp.pad(a, ((0, 0), (0, PACK_W - a.shape[1]))) for a in small]
    packed = jnp.concatenate(rows, axis=0)
    packed = jnp.pad(packed, ((0, -packed.shape[0] % 8), (0, 0)))
    red = _allreduce_small(packed)
    loss = red[len(SMALL) + CONV_W, 0]
    for i, n in enumerate(SMALL):
        grads[n] = red[i:i + 1, :gs[n].size]
    dconv = red[len(SMALL):len(SMALL) + CONV_W, :conv_grad.shape[1]]
    cw = conv_qk.shape[-1]
    grads["conv_qk"] = lax.dynamic_slice_in_dim(dconv, (2 * lax.axis_index("x") + lax.axis_index("y")) * cw, cw, axis=1)

    outs = {}
    for grp in SAME_SHAPE + tuple((n,) for n in WEIGHTS if n not in BIG):
        g2s = [grads[n].reshape(w2[n].shape) for n in grp]
        res = _adamw("adamw_" + grp[0], [w2[n] for n in grp], g2s, [m2[n] for n in grp], [v2[n] for n in grp])
        for n, g2, (d, nm, nv) in zip(grp, g2s, res):
            outs[n] = tuple(unview(a, n) for a in (g2, d, nm, nv))
    return (loss, grad_x[None], *[outs[n][0] for n in WEIGHTS], *[outs[n][1] for n in WEIGHTS],
            *[outs[n][2] for n in WEIGHTS], *[outs[n][3] for n in WEIGHTS])
```

```python
import jax
import jax.numpy as jnp
from jax import lax
from jax.experimental import pallas as pl
from jax.experimental.pallas import tpu as pltpu

F32 = jnp.float32
BF16 = jnp.bfloat16
EPS = 1e-6
NH_M, DK_M, DV_M = 4, 64, 128
NH_F, DH_F = 8, 64
CONV_W = 4
ADAM_LR, ADAM_B1, ADAM_B2, ADAM_EPS, ADAM_WD, ADAM_STEP = 0.001, 0.9, 0.999, 1e-08, 0.01, 10
VMEM_LIMIT = 56 * 1024 * 1024


def _cparams(sem):
    return pltpu.CompilerParams(dimension_semantics=sem, vmem_limit_bytes=VMEM_LIMIT)


def _sigmoid(x):
    return 1.0 / (1.0 + jnp.exp(-x))


def _dot(a, b, ca, cb):
    return lax.dot_general(a.astype(BF16), b.astype(BF16), (((ca,), (cb,)), ((), ())), preferred_element_type=F32)


def _rowwise(name, fn, tiled, full, outs, accs=(), tm=512, plan=None):
    rows = tiled[0].shape[0]
    tm = min(tm, rows)
    assert rows % tm == 0
    n_t, n_f, n_o, n_a = len(tiled), len(full), len(outs), len(accs)
    nt = rows // tm
    host = _Hosted(plan, n_t + n_f, n_o + n_a)

    def body(*refs):
        in_refs, orefs, _, prefs = host.split(refs)
        host.run(0, pl.program_id(0) == 0, prefs)
        host.run(1, pl.program_id(0) == 0, prefs)
        ins = [r[...] for r in in_refs]
        res = fn(*ins)
        if not isinstance(res, (tuple, list)):
            res = (res,)
        for r, v in zip(orefs[:n_o], res[:n_o]):
            r[...] = v.astype(r.dtype)
        if n_a:
            @pl.when(pl.program_id(0) == 0)
            def _():
                for r in orefs[n_o:]:
                    r[...] = jnp.zeros_like(r)
            for r, v in zip(orefs[n_o:], res[n_o:]):
                r[...] += v.astype(r.dtype)
        host.run(2, pl.program_id(0) == nt - 1, prefs)

    in_specs = [pl.BlockSpec((tm, a.shape[1]), lambda i: (i, 0)) for a in tiled]
    in_specs += [pl.BlockSpec(a.shape, lambda i: (0, 0)) for a in full]
    out_specs = [pl.BlockSpec((tm, c), lambda i: (i, 0)) for c, _ in outs]
    out_specs += [pl.BlockSpec(s, lambda i: (0, 0)) for s, _ in accs]
    out_shape = [jax.ShapeDtypeStruct((rows, c), d) for c, d in outs]
    out_shape += [jax.ShapeDtypeStruct(s, d) for s, d in accs]
    hc = host.call_args()
    res = pl.pallas_call(
        body, name=name, grid=(nt,), in_specs=in_specs + hc["in_specs"], out_specs=out_specs + hc["out_specs"],
        out_shape=out_shape + hc["out_shape"], scratch_shapes=hc["scratch"], input_output_aliases=hc["aliases"],
        compiler_params=_cparams(("arbitrary",) if (n_a or plan is not None) else ("parallel",)),
    )(*tiled, *full, *hc["args"])
    return res if plan is None else (res[:n_o + n_a], res[n_o + n_a:])


def _colsum(v):
    return jnp.sum(v, axis=0, keepdims=True)


def _rms_fwd_val(x, g):
    r = lax.rsqrt(jnp.mean(x * x, axis=-1, keepdims=True) + EPS)
    return x * r * g


def _rms_bwd_val(dy, x, g):
    r = lax.rsqrt(jnp.mean(x * x, axis=-1, keepdims=True) + EPS)
    xh = x * r
    dxh = dy * g
    dx = r * (dxh - xh * jnp.mean(dxh * xh, axis=-1, keepdims=True))
    return dx, _colsum(dy * xh)


def _mm(name, pairs, out_shape, out_block, out_map, grid, kaxis, ta=False, tb=False, scale=None, res=None,
        out_dtype=F32, plan=None, twin=False):
    n_o = 2 if twin else 1
    nk = grid[kaxis]
    npairs = len(pairs)
    ca, cb = (0 if ta else 1), (1 if tb else 0)
    acc_shape = tuple(d for d in out_block if d is not None)
    n_in = 2 * npairs + (1 if res is not None else 0)
    host = _Hosted(plan, n_in, n_o)

    def body(*refs):
        ins, o_refs, (acc_ref,), prefs = host.split(refs)
        o_ref = o_refs[0]
        in_refs = ins[: 2 * npairs]
        res_ref = ins[2 * npairs] if res is not None else None
        k = pl.program_id(kaxis)
        ids = [pl.program_id(a) for a in range(len(grid))]
        first, last = ids[0] == 0, ids[0] == grid[0] - 1
        for a in range(1, len(grid)):
            first, last = first & (ids[a] == 0), last & (ids[a] == grid[a] - 1)
        host.run(0, first, prefs)
        host.run(1, first, prefs)

        @pl.when(k == 0)
        def _():
            acc_ref[...] = jnp.zeros_like(acc_ref)

        part = None
        for p in range(npairs):
            d = _dot(in_refs[2 * p][...], in_refs[2 * p + 1][...], ca, cb)
            part = d if part is None else part + d
        acc_ref[...] += part

        @pl.when(k == nk - 1)
        def _():
            v = acc_ref[...]
            if scale is not None:
                v = v * scale
            if res_ref is not None:
                v = v + res_ref[...].astype(F32)
            o_ref[...] = v.astype(o_ref.dtype)
            if twin:
                o_refs[1][...] = v.astype(BF16)

        host.run(2, last, prefs)

    in_specs, args = [], []
    for a, ab, am, b, bb, bm in pairs:
        in_specs += [pl.BlockSpec(ab, am), pl.BlockSpec(bb, bm)]
        args += [a, b]
    if res is not None:
        in_specs.append(pl.BlockSpec(out_block, out_map))
        args.append(res)
    sem = tuple("arbitrary" if (i == kaxis or plan is not None) else "parallel" for i in range(len(grid)))
    hc = host.call_args()
    out = pl.pallas_call(
        body, name=name, grid=grid, in_specs=in_specs + hc["in_specs"],
        out_specs=[pl.BlockSpec(out_block, out_map)] * n_o + hc["out_specs"],
        out_shape=[jax.ShapeDtypeStruct(out_shape, out_dtype)] + [jax.ShapeDtypeStruct(out_shape, BF16)] * (n_o - 1)
        + hc["out_shape"],
        scratch_shapes=[pltpu.VMEM(acc_shape, F32)] + hc["scratch"], input_output_aliases=hc["aliases"],
        compiler_params=_cparams(sem),
    )(*args, *hc["args"])
    res_out = tuple(out[:2]) if twin else out[0]
    return res_out if plan is None else (res_out, out[n_o:])


def _pick(n, pref):
    for t in pref:
        if n % t == 0:
            return t
    return n


def _mm_nn(name, a, b, tm=512, tn=512, tk=512, **kw):
    (m, k), n = a.shape, b.shape[1]
    tm, tn, tk = _pick(m, (tm, 256, 128)), _pick(n, (tn, 256, 128)), _pick(k, (tk, 256, 128))
    return _mm(name, [(a, (tm, tk), lambda i, j, kk: (i, kk), b, (tk, tn), lambda i, j, kk: (kk, j))],
               (m, n), (tm, tn), lambda i, j, kk: (i, j), (m // tm, n // tn, k // tk), 2, **kw)


def _mm_nt(name, a, b, tm=512, tn=512, tk=512, **kw):
    (m, k), n = a.shape, b.shape[0]
    tm, tn, tk = _pick(m, (tm, 256, 128)), _pick(n, (tn, 256, 128)), _pick(k, (tk, 256, 128))
    return _mm(name, [(a, (tm, tk), lambda i, j, kk: (i, kk), b, (tn, tk), lambda i, j, kk: (j, kk))],
               (m, n), (tm, tn), lambda i, j, kk: (i, j), (m // tm, n // tn, k // tk), 2, tb=True, **kw)


def _mm_tn(name, a, b, tm=512, tn=512, tk=4096, **kw):
    (k, m), n = a.shape, b.shape[1]
    tm, tn, tk = _pick(m, (tm, 256, 128)), _pick(n, (tn, 256, 128)), _pick(k, (tk, 2048, 1024, 512, 256, 128))
    return _mm(name, [(a, (tk, tm), lambda i, j, kk: (kk, i), b, (tk, tn), lambda i, j, kk: (kk, j))],
               (m, n), (tm, tn), lambda i, j, kk: (i, j), (m // tm, n // tn, k // tk), 2, ta=True, **kw)


def _norm_mm(name, h, gamma, w, w_transposed, out_dtype):
    t, d = h.shape
    n = w.shape[0] if w_transposed else w.shape[1]
    tm, tn = _pick(t, (1024, 512, 256)), _pick(n, (1024, 512, 256, 128))

    def body(h_ref, gam_ref, w_ref, xn_ref, o_ref, xn_scr):
        @pl.when(pl.program_id(1) == 0)
        def _():
            xn = _rms_fwd_val(h_ref[...], gam_ref[...]).astype(BF16)
            xn_scr[...] = xn
            xn_ref[...] = xn

        o_ref[...] = _dot(xn_scr[...], w_ref[...], 1, 1 if w_transposed else 0).astype(o_ref.dtype)

    wspec = pl.BlockSpec((tn, d), lambda i, j: (j, 0)) if w_transposed else pl.BlockSpec((d, tn), lambda i, j: (0, j))
    return pl.pallas_call(
        body, name=name, grid=(t // tm, n // tn),
        in_specs=[pl.BlockSpec((tm, d), lambda i, j: (i, 0)), pl.BlockSpec((1, d), lambda i, j: (0, 0)), wspec],
        out_specs=[pl.BlockSpec((tm, d), lambda i, j: (i, 0)), pl.BlockSpec((tm, tn), lambda i, j: (i, j))],
        out_shape=[jax.ShapeDtypeStruct((t, d), BF16), jax.ShapeDtypeStruct((t, n), out_dtype)],
        scratch_shapes=[pltpu.VMEM((tm, d), BF16)], compiler_params=_cparams(("parallel", "arbitrary")),
    )(h, gamma, w)


CHAIN_ROWS = 256


def _row_chains(tm):
    n = max(tm // CHAIN_ROWS, 1)
    return [slice(r * (tm // n), (r + 1) * (tm // n)) for r in range(n)]


def _ffn_fwd(pfx, h, gamma, wg, wu, wd, plan=None):
    t, d = h.shape
    nb, f, _ = wg.shape
    tm = _pick(t, (1024, 512, 256))
    nt = t // tm
    host = _Hosted(plan, 5, 4)

    def body(*refs):
        (h_ref, gam_ref, wg_ref, wu_ref, wd_ref), (ho_ref, xn_ref, g_ref, u_ref), (xn_scr, acc_ref), prefs = host.split(refs)
        i, j = pl.program_id(0), pl.program_id(1)
        host.run(0, (i == 0) & (j == 0), prefs)
        host.run(1, (i == nt // 2) & (j == 0), prefs)

        @pl.when(j == 0)
        def _():
            xn = _rms_fwd_val(h_ref[...], gam_ref[...]).astype(BF16)
            xn_scr[...] = xn
            xn_ref[...] = xn
            acc_ref[...] = jnp.zeros_like(acc_ref)

        for rows in _row_chains(tm):
            x = xn_scr[rows, :]
            g = _dot(x, wg_ref[...], 1, 1)
            u = _dot(x, wu_ref[...], 1, 1)
            g_ref[rows, :] = g.astype(BF16)
            u_ref[rows, :] = u.astype(BF16)
            acc_ref[rows, :] += _dot(g * _sigmoid(g) * u, wd_ref[...], 1, 0)

        @pl.when(j == nb - 1)
        def _():
            ho_ref[...] = h_ref[...] + 0.5 * acc_ref[...]

        host.run(2, (i == nt - 1) & (j == nb - 1), prefs)

    row = pl.BlockSpec((tm, d), lambda i, j: (i, 0))
    blk = pl.BlockSpec((None, tm, f), lambda i, j: (j, i, 0))
    wspec = pl.BlockSpec((None, f, d), lambda i, j: (j, 0, 0))
    hc = host.call_args()
    res = pl.pallas_call(
        body, name=pfx + "_fwd", grid=(nt, nb),
        in_specs=[row, pl.BlockSpec((1, d), lambda i, j: (0, 0)), wspec, wspec, wspec] + hc["in_specs"],
        out_specs=[row, row, blk, blk] + hc["out_specs"],
        out_shape=[jax.ShapeDtypeStruct((t, d), F32), jax.ShapeDtypeStruct((t, d), BF16),
                   jax.ShapeDtypeStruct((nb, t, f), BF16), jax.ShapeDtypeStruct((nb, t, f), BF16)] + hc["out_shape"],
        scratch_shapes=[pltpu.VMEM((tm, d), BF16), pltpu.VMEM((tm, d), F32)] + hc["scratch"],
        input_output_aliases=hc["aliases"], compiler_params=_cparams(("arbitrary", "arbitrary")),
    )(h, gamma, wg, wu, wd, *hc["args"])
    return res[:4], res[4:]


def _ffn_bwd(pfx, dh_out, h, gamma, xn, g_all, u_all, wg, wu, wd, plan=None):
    t, d = h.shape
    nb, f, _ = wg.shape
    tm = _pick(t, (512, 256))
    tk = _pick(t, (4096, 2048, 1024, 512, 256))

    nt = t // tm
    host = _Hosted(plan, 8, 6)

    def body(*refs):
        ((dy_ref, h_ref, gam_ref, wg_ref, wu_ref, wd_ref, g_ref, u_ref),
         (dh_ref, dgam_ref, dg_ref, du_ref, a_ref, dyb_ref), (acc_ref,), prefs) = host.split(refs)
        i, j = pl.program_id(0), pl.program_id(1)
        host.run(0, (i == 0) & (j == 0), prefs)
        host.run(1, (i == nt // 2) & (j == 0), prefs)

        @pl.when((i == 0) & (j == 0))
        def _():
            dgam_ref[...] = jnp.zeros_like(dgam_ref)

        @pl.when(j == 0)
        def _():
            acc_ref[...] = jnp.zeros_like(acc_ref)
            dyb_ref[...] = dy_ref[...].astype(BF16)

        for rows in _row_chains(tm):
            da = _dot(dy_ref[rows, :], wd_ref[...], 1, 1) * 0.5
            g = g_ref[rows, :].astype(F32)
            u = u_ref[rows, :].astype(F32)
            s = _sigmoid(g)
            sl = g * s
            du = (da * sl).astype(BF16)
            dg = (da * u * (s + sl * (1.0 - s))).astype(BF16)
            du_ref[rows, :] = du
            dg_ref[rows, :] = dg
            a_ref[rows, :] = (sl * u).astype(BF16)
            acc_ref[rows, :] += _dot(dg, wg_ref[...], 1, 0) + _dot(du, wu_ref[...], 1, 0)

        @pl.when(j == nb - 1)
        def _():
            dx, dgam = _rms_bwd_val(acc_ref[...], h_ref[...], gam_ref[...])
            dh_ref[...] = dy_ref[...] + dx
            dgam_ref[...] += dgam

        host.run(2, (i == nt - 1) & (j == nb - 1), prefs)

    row = pl.BlockSpec((tm, d), lambda i, j: (i, 0))
    vec = pl.BlockSpec((1, d), lambda i, j: (0, 0))
    blk = pl.BlockSpec((None, tm, f), lambda i, j: (j, i, 0))
    wspec = pl.BlockSpec((None, f, d), lambda i, j: (j, 0, 0))
    hc = host.call_args()
    res = pl.pallas_call(
        body, name=pfx + "_bwd", grid=(nt, nb),
        in_specs=[row, row, vec, wspec, wspec, wspec, blk, blk] + hc["in_specs"],
        out_specs=[row, vec, blk, blk, blk, row] + hc["out_specs"],
        out_shape=[jax.ShapeDtypeStruct((t, d), F32), jax.ShapeDtypeStruct((1, d), F32)]
        + [jax.ShapeDtypeStruct((nb, t, f), BF16)] * 3 + [jax.ShapeDtypeStruct((t, d), BF16)] + hc["out_shape"],
        scratch_shapes=[pltpu.VMEM((tm, d), F32)] + hc["scratch"], input_output_aliases=hc["aliases"],
        compiler_params=_cparams(("arbitrary", "arbitrary")),
    )(dh_out, h, gamma, wg, wu, wd, g_all, u_all, *hc["args"])
    dh, dgamma, dg_all, du_all, a_all, dyb = res[:6]

    xmap, bmap, omap = (lambda b, k: (k, 0)), (lambda b, k: (b, k, 0)), (lambda b, k: (b, 0, 0))
    dwg, tg = _mm(pfx + "_dwg", [(dg_all, (None, tk, f), bmap, xn, (tk, d), xmap)], (nb, f, d), (None, f, d), omap,
                  (nb, t // tk), 1, ta=True, twin=True)
    dwu, tu = _mm(pfx + "_dwu", [(du_all, (None, tk, f), bmap, xn, (tk, d), xmap)], (nb, f, d), (None, f, d), omap,
                  (nb, t // tk), 1, ta=True, twin=True)
    dwd, td = _mm(pfx + "_dwd", [(a_all, (None, tk, f), bmap, dyb, (tk, d), xmap)], (nb, f, d), (None, f, d), omap,
                  (nb, t // tk), 1, ta=True, scale=0.5, twin=True)
    return (dh, dgamma, dwg, dwu, dwd), res[6:], (tg, tu, td)


def _ffn_bwd_late_dx(pfx, dh_out, h, gamma, xn, g_all, u_all, wg, wu, wd, plan_gu, plans_dw, make_plan_dx):
    t, d = h.shape
    nb, f, _ = wg.shape
    tm = _pick(t, (1024, 512, 256))
    tk = _pick(t, (4096, 2048, 1024, 512, 256))
    nt = t // tm
    host_a = _Hosted(plan_gu, 4, 4)

    def body_a(*refs):
        (dy_ref, wd_ref, g_ref, u_ref), (dg_ref, du_ref, a_ref, dyb_ref), _, prefs = host_a.split(refs)
        i, j = pl.program_id(0), pl.program_id(1)
        host_a.run(0, (i == 0) & (j == 0), prefs)
        host_a.run(1, (i == 0) & (j == 0), prefs)

        @pl.when(j == 0)
        def _():
            dyb_ref[...] = dy_ref[...].astype(BF16)

        for rows in _row_chains(tm):
            da = _dot(dy_ref[rows, :], wd_ref[...], 1, 1) * 0.5
            g = g_ref[rows, :].astype(F32)
            u = u_ref[rows, :].astype(F32)
            s = _sigmoid(g)
            sl = g * s
            du_ref[rows, :] = (da * sl).astype(BF16)
            dg_ref[rows, :] = (da * u * (s + sl * (1.0 - s))).astype(BF16)
            a_ref[rows, :] = (sl * u).astype(BF16)
        host_a.run(2, (i == nt - 1) & (j == nb - 1), prefs)

    row = pl.BlockSpec((tm, d), lambda i, j: (i, 0))
    vec = pl.BlockSpec((1, d), lambda i, j: (0, 0))
    blk = pl.BlockSpec((None, tm, f), lambda i, j: (j, i, 0))
    wspec = pl.BlockSpec((None, f, d), lambda i, j: (j, 0, 0))
    hc = host_a.call_args()
    res_a = pl.pallas_call(
        body_a, name=pfx + "_bwd_gu", grid=(nt, nb), in_specs=[row, wspec, blk, blk] + hc["in_specs"],
        out_specs=[blk] * 3 + [row] + hc["out_specs"],
        out_shape=[jax.ShapeDtypeStruct((nb, t, f), BF16)] * 3 + [jax.ShapeDtypeStruct((t, d), BF16)] + hc["out_shape"],
        scratch_shapes=hc["scratch"], input_output_aliases=hc["aliases"], compiler_params=_cparams(("arbitrary", "arbitrary")),
    )(dh_out, wd, g_all, u_all, *hc["args"])
    dg_all, du_all, a_all, dyb = res_a[:4]

    xmap, bmap, omap = (lambda b, k: (k, 0)), (lambda b, k: (b, k, 0)), (lambda b, k: (b, 0, 0))
    def dw(name, a, b, plan, scale=None):
        r = _mm(pfx + name, [(a, (None, tk, f), bmap, b, (tk, d), xmap)], (nb, f, d), (None, f, d), omap, (nb, t // tk), 1,
                ta=True, scale=scale, plan=plan, twin=True)
        return r if plan is not None else (r, ())

    (dwd, td), out_d = dw("_dwd", a_all, dyb, plans_dw[0], 0.5)
    (dwg, tg), out_g = dw("_dwg", dg_all, xn, plans_dw[1])
    (dwu, tu), out_u = dw("_dwu", du_all, xn, plans_dw[2])

    plan_dx = make_plan_dx((dwg, dwu, dwd), (tg, tu, td))
    host_b = _Hosted(plan_dx, 7, 2)

    def body_b(*refs):
        (dy_ref, h_ref, gam_ref, wg_ref, wu_ref, dg_ref, du_ref), (dh_ref, dgam_ref), (acc_ref,), prefs = host_b.split(refs)
        i, j = pl.program_id(0), pl.program_id(1)
        host_b.run(0, (i == 0) & (j == 0), prefs)
        host_b.run(1, (i == 0) & (j == 0), prefs)

        @pl.when((i == 0) & (j == 0))
        def _():
            dgam_ref[...] = jnp.zeros_like(dgam_ref)

        @pl.when(j == 0)
        def _():
            acc_ref[...] = jnp.zeros_like(acc_ref)

        acc_ref[...] += _dot(dg_ref[...], wg_ref[...], 1, 0) + _dot(du_ref[...], wu_ref[...], 1, 0)

        @pl.when(j == nb - 1)
        def _():
            dx, dgam = _rms_bwd_val(acc_ref[...], h_ref[...], gam_ref[...])
            dh_ref[...] = dy_ref[...] + dx
            dgam_ref[...] += dgam

        host_b.run(2, (i == nt - 1) & (j == nb - 1), prefs)

    hc = host_b.call_args()
    res_b = pl.pallas_call(
        body_b, name=pfx + "_bwd_dx", grid=(nt, nb), in_specs=[row, row, vec, wspec, wspec, blk, blk] + hc["in_specs"],
        out_specs=[row, vec] + hc["out_specs"],
        out_shape=[jax.ShapeDtypeStruct((t, d), F32), jax.ShapeDtypeStruct((1, d), F32)] + hc["out_shape"],
        scratch_shapes=[pltpu.VMEM((tm, d), F32)] + hc["scratch"], input_output_aliases=hc["aliases"],
        compiler_params=_cparams(("arbitrary", "arbitrary")),
    )(dh_out, h, gamma, wg, wu, dg_all, du_all, *hc["args"])
    return (res_b[0], res_b[1], dwg, dwu, dwd), (res_a[4:], out_d, out_g, out_u, res_b[2:])


HALO = 16


def _silu_grad(y):
    s = _sigmoid(y)
    return s * (1.0 + y * (1.0 - s))


def _with_halo(ref, i, n_tiles, tm, before, after):
    t = ref.shape[0]
    r0 = pl.multiple_of(i * tm, tm)
    parts = [ref[pl.ds(r0, tm), :].astype(F32)]
    if before:
        prev = ref[pl.ds(pl.multiple_of(jnp.maximum(r0 - HALO, 0), HALO), HALO), :].astype(F32)
        parts.insert(0, jnp.where(i > 0, prev, 0.0))
    if after:
        nxt = ref[pl.ds(pl.multiple_of(jnp.minimum(r0 + tm, t - HALO), HALO), HALO), :].astype(F32)
        parts.append(jnp.where(i < n_tiles - 1, nxt, 0.0))
    return jnp.concatenate(parts, axis=0)


def _conv_fwd(zbig, w):
    t, c = zbig.shape[0], w.shape[1]
    tm = _pick(t, (512, 256))
    nt = t // tm

    def body(x_ref, w_ref, o_ref):
        xe = _with_halo(x_ref, pl.program_id(0), nt, tm, True, False)
        wv = w_ref[...]
        y = xe * wv[3:4, :]
        for i in range(CONV_W - 1):
            y = y + pltpu.roll(xe, CONV_W - 1 - i, 0) * wv[i:i + 1, :]
        y = y[HALO:, :]
        o_ref[...] = (y * _sigmoid(y)).astype(o_ref.dtype)

    return pl.pallas_call(
        body, name="conv_fwd", grid=(nt,),
        in_specs=[pl.BlockSpec((t, c), lambda i: (0, 0)), pl.BlockSpec(w.shape, lambda i: (0, 0))],
        out_specs=pl.BlockSpec((tm, c), lambda i: (i, 0)), out_shape=jax.ShapeDtypeStruct((t, c), BF16),
        compiler_params=_cparams(("parallel",)),
    )(zbig, w)


def _conv_bwd(zbig, dact, w):
    t, c = dact.shape
    tm = _pick(t, (512, 256))
    nt = t // tm
    n = tm + HALO

    def body(x_ref, d_ref, w_ref, dx_ref, dw_ref):
        xe = _with_halo(x_ref, pl.program_id(0), nt, tm, True, True)
        de = _with_halo(d_ref, pl.program_id(0), nt, tm, False, True)
        wv = w_ref[...]
        sh = [pltpu.roll(xe, CONV_W - 1 - i, 0)[HALO:, :] if i < CONV_W - 1 else xe[HALO:, :] for i in range(CONV_W)]
        y = sh[0] * wv[0:1, :]
        for i in range(1, CONV_W):
            y = y + sh[i] * wv[i:i + 1, :]
        dy = de * _silu_grad(y)
        dx = dy * wv[3:4, :]
        for i in range(CONV_W - 1):
            dx = dx + pltpu.roll(dy, n - (CONV_W - 1 - i), 0) * wv[i:i + 1, :]
        dx_ref[...] = dx[:tm, :].astype(dx_ref.dtype)
        dyc = dy[:tm, :]
        dwp = jnp.concatenate([_colsum(dyc * sh[i][:tm, :]) for i in range(CONV_W)], axis=0)

        @pl.when(pl.program_id(0) == 0)
        def _():
            dw_ref[...] = jnp.zeros_like(dw_ref)
        dw_ref[...] += dwp

    return pl.pallas_call(
        body, name="conv_bwd", grid=(nt,),
        in_specs=[pl.BlockSpec((t, c), lambda i: (0, 0)), pl.BlockSpec((t, c), lambda i: (0, 0)),
                  pl.BlockSpec(w.shape, lambda i: (0, 0))],
        out_specs=[pl.BlockSpec((tm, c), lambda i: (i, 0)), pl.BlockSpec(w.shape, lambda i: (0, 0))],
        out_shape=[jax.ShapeDtypeStruct((t, c), BF16), jax.ShapeDtypeStruct(w.shape, F32)],
        compiler_params=_cparams(("arbitrary",)),
    )(zbig, dact, w)


LM = 256
HI = lax.Precision.HIGHEST


def _logsig(x):
    return jnp.minimum(x, 0.0) - jnp.log(1.0 + jnp.exp(-jnp.abs(x)))


def _tri(n, lower):
    r = lax.broadcasted_iota(jnp.int32, (n, n), 0)
    c = lax.broadcasted_iota(jnp.int32, (n, n), 1)
    return (r >= c) if lower else (r <= c)


def _f32dot(a, b):
    return lax.dot_general(a, b, (((1,), (0,)), ((), ())), precision=HI, preferred_element_type=F32)


def _tri_dot(a, b, a_is_tri):
    tri = (a if a_is_tri else b).astype(BF16)
    parts = _split3(b if a_is_tri else a)
    outs = [_dot(tri, p, 1, 0) if a_is_tri else _dot(p, tri, 1, 0) for p in parts]
    return (outs[0] + outs[1]) + outs[2]


def _mlstm_decays(zs_ref, zsr_ref, bc_ref, br_ref):
    l = LM
    lf_c = _logsig(zs_ref[:, 0:2 * NH_M] + bc_ref[...])
    lf_r = _logsig(zsr_ref[...] + br_ref[...])
    low, up = _tri(l, True), _tri(l, False)
    return _tri_dot(low, lf_c, True), _tri_dot(lf_r, up, False), low, up


def _mlstm_chunk(h, q_ref, k_ref, v_ref, zs_ref, zsr_ref, bc_ref, br_ref, c_prev, m_prev, decays):
    l = LM
    q = q_ref[:, h * DK_M:(h + 1) * DK_M].astype(F32) * (DK_M ** -0.5)
    k = k_ref[:, h * DK_M:(h + 1) * DK_M]
    v = v_ref[:, h * DV_M:(h + 1) * DV_M]
    lane = lax.broadcasted_iota(jnp.int32, (l, DV_M), 1)
    v1 = jnp.concatenate([v, (lane == 0).astype(v.dtype)], axis=1)
    zs, zsr = zs_ref[...], zsr_ref[...]
    li_c = zs[:, h:h + 1] + bc_ref[:, h:h + 1]
    fp_c = zs[:, NH_M + h:NH_M + h + 1] + bc_ref[:, NH_M + h:NH_M + h + 1]
    li_r = zsr[h:h + 1, :] + br_ref[h:h + 1, :]
    fp_r = zsr[NH_M + h:NH_M + h + 1, :] + br_ref[NH_M + h:NH_M + h + 1, :]
    low = decays[2]
    b_c = decays[0][:, NH_M + h:NH_M + h + 1]
    b_r = decays[1][NH_M + h:NH_M + h + 1, :]
    g = b_r[:, l - 1:l]
    dmat = jnp.where(low, b_c - b_r + li_r, -jnp.inf)
    inter = b_c + m_prev
    m_t = jnp.maximum(inter, jnp.max(dmat, axis=1, keepdims=True))
    w_inter = jnp.exp(inter - m_t)
    amat = jnp.exp(dmat - m_t)
    s = _dot(q, k, 1, 1)
    p = amat * s
    qc = _dot(q, c_prev, 1, 0)
    qc_w = w_inter * qc
    num1 = qc_w + _dot(p, v1, 1, 0)
    den = num1[:, DV_M:DV_M + 1]
    mx = jnp.maximum(jnp.abs(den), jnp.exp(-m_t))
    hh = num1[:, :DV_M] / mx
    a_c = g - b_c + li_c
    return dict(q=q, k=k, v1=v1, fp_c=fp_c, fp_r=fp_r, b_c=b_c, g=g, m_t=m_t, w_inter=w_inter, amat=amat, s=s, p=p,
                qc_w=qc_w, den=den, mx=mx, hh=hh, a_c=a_c)


def _mlstm_fwd(qk, zbig, zs, zsr, bc, br, gm):
    t = zs.shape[0]
    l = LM
    nc = t // l
    dm = NH_M * DV_M

    def body(q_ref, k_ref, v_ref, o_ref, zs_ref, zsr_ref, bc_ref, br_ref, gm_ref, y_ref, cst_ref, mst_ref, c_scr, m_scr):
        @pl.when(pl.program_id(0) == 0)
        def _():
            c_scr[...] = jnp.zeros_like(c_scr)
            m_scr[...] = jnp.zeros_like(m_scr)

        cst_ref[...] = c_scr[...]
        mst_ref[...] = m_scr[...]
        ys = []
        decays = _mlstm_decays(zs_ref, zsr_ref, bc_ref, br_ref)
        for h in range(NH_M):
            c_prev = c_scr[h]
            m_prev = m_scr[h:h + 1, 0:1]
            r = _mlstm_chunk(h, q_ref, k_ref, v_ref, zs_ref, zsr_ref, bc_ref, br_ref, c_prev, m_prev, decays)
            hh = r["hh"]
            gh = gm_ref[:, h * DV_M:(h + 1) * DV_M]
            hn = hh * lax.rsqrt(jnp.mean(hh * hh, axis=-1, keepdims=True) + EPS) * gh
            og = o_ref[:, h * DV_M:(h + 1) * DV_M].astype(F32)
            ys.append(hn * _sigmoid(og))
            m_new = jnp.maximum(r["g"] + m_prev, jnp.max(r["a_c"], axis=0, keepdims=True))
            decay = jnp.exp(r["g"] + m_prev - m_new)
            wk = r["k"].astype(F32) * jnp.exp(r["a_c"] - m_new)
            c_scr[h] = decay * c_prev + _dot(wk, r["v1"], 0, 0)
            m_scr[h:h + 1, :] = jnp.broadcast_to(m_new, (1, 128))
        y_ref[...] = jnp.concatenate(ys, axis=1).astype(y_ref.dtype)

    return pl.pallas_call(
        body, name="mlstm_fwd", grid=(nc,),
        in_specs=[pl.BlockSpec((l, NH_M * DK_M), lambda i: (i, 0)), pl.BlockSpec((l, NH_M * DK_M), lambda i: (i, 1)),
                  pl.BlockSpec((l, dm), lambda i: (i, 1)), pl.BlockSpec((l, dm), lambda i: (i, 2)),
                  pl.BlockSpec((l, 128), lambda i: (i, 0)), pl.BlockSpec((8, l), lambda i: (0, i)),
                  pl.BlockSpec((1, 8), lambda i: (0, 0)), pl.BlockSpec((8, 1), lambda i: (0, 0)),
                  pl.BlockSpec((1, dm), lambda i: (0, 0))],
        out_specs=[pl.BlockSpec((l, dm), lambda i: (i, 0)), pl.BlockSpec((None, NH_M, DK_M, 2 * DV_M), lambda i: (i, 0, 0, 0)),
                   pl.BlockSpec((None, 8, 128), lambda i: (i, 0, 0))],
        out_shape=[jax.ShapeDtypeStruct((t, dm), BF16), jax.ShapeDtypeStruct((nc, NH_M, DK_M, 2 * DV_M), F32),
                   jax.ShapeDtypeStruct((nc, 8, 128), F32)],
        scratch_shapes=[pltpu.VMEM((NH_M, DK_M, 2 * DV_M), F32), pltpu.VMEM((8, 128), F32)],
        compiler_params=_cparams(("arbitrary",)),
    )(qk, qk, zbig, zbig, zs, zsr, bc, br, gm)


def _mlstm_bwd(qk, zbig, zs, zsr, bc, br, gm, cst, mst, dycat):
    t = zs.shape[0]
    l = LM
    nc = t // l
    dm = NH_M * DV_M

    def body(q_ref, k_ref, v_ref, o_ref, zs_ref, zsr_ref, bc_ref, br_ref, gm_ref, cst_ref, mst_ref, cnx_ref, mnx_ref,
             dy_ref, dqk_ref, dv_ref, do_ref, dzs_ref, dzr_ref, dgm_ref, dc_scr):
        @pl.when(pl.program_id(0) == 0)
        def _():
            dc_scr[...] = jnp.zeros_like(dc_scr)
            dgm_ref[...] = jnp.zeros_like(dgm_ref)

        lane = lax.broadcasted_iota(jnp.int32, (l, 128), 1)
        db_all, sig_c, carries = jnp.zeros((l, 128), F32), jnp.zeros((l, 128), F32), jnp.zeros((1, 128), F32)
        decays = _mlstm_decays(zs_ref, zsr_ref, bc_ref, br_ref)
        lower, upper = decays[2], decays[3]
        dzr_rows = [None] * 8
        dvs, dos, dgs, dqs, dks = [], [], [], [], []
        dzs = jnp.zeros((l, 128), F32)
        for h in range(NH_M):
            c_prev = cst_ref[h]
            m_prev = mst_ref[h:h + 1, 0:1]
            r = _mlstm_chunk(h, q_ref, k_ref, v_ref, zs_ref, zsr_ref, bc_ref, br_ref, c_prev, m_prev, decays)
            hh, mx, den, m_t, v1, amat = r["hh"], r["mx"], r["den"], r["m_t"], r["v1"], r["amat"]
            gh = gm_ref[:, h * DV_M:(h + 1) * DV_M]
            rs = lax.rsqrt(jnp.mean(hh * hh, axis=-1, keepdims=True) + EPS)
            xh = hh * rs
            sg = _sigmoid(o_ref[:, h * DV_M:(h + 1) * DV_M].astype(F32))
            dyh = dy_ref[:, h * DV_M:(h + 1) * DV_M]
            dos.append(dyh * xh * gh * sg * (1.0 - sg))
            dhn = dyh * sg
            dgs.append(_colsum(dhn * xh))
            dxh = dhn * gh
            dh = rs * (dxh - xh * jnp.mean(dxh * xh, axis=-1, keepdims=True))
            g1 = dh / mx
            hd = jnp.sum(hh * dh, axis=-1, keepdims=True)
            dden = jnp.where(jnp.abs(den) > jnp.exp(-m_t), -hd / mx * jnp.sign(den), 0.0)
            g256 = jnp.concatenate([g1, jnp.where(lane == 0, dden, 0.0)], axis=1)
            dc_h = dc_scr[h]
            ea = jnp.exp(r["a_c"])
            dp = _dot(g256, v1, 1, 1)
            ds = dp * amat
            dqs.append((r["w_inter"] * _dot(g256, c_prev, 1, 1) + _dot(ds, r["k"], 1, 0)) * (DK_M ** -0.5))
            dks.append(_dot(ds, r["q"], 0, 0) + ea * _dot(v1, dc_h, 1, 1))
            dv_st = ea * _dot(r["k"], dc_h, 1, 0)
            dv1 = _dot(r["p"], g256, 0, 0) + dv_st
            dvs.append(dv1[:, :DV_M])
            wmat = dp * r["p"]
            c_in = _colsum(wmat)
            c_st = jnp.sum(v1.astype(F32) * dv_st, axis=-1, keepdims=True)
            r_t = jnp.sum(wmat, axis=1, keepdims=True) + jnp.sum(g256 * r["qc_w"], axis=-1, keepdims=True)
            db = r_t - c_st
            carry = jnp.exp(mnx_ref[h:h + 1, 0:1]) * jnp.sum(
                jnp.sum(dc_h * cnx_ref[h], axis=1, keepdims=True), axis=0, keepdims=True)
            db_all = db_all + jnp.where(lane == NH_M + h, db, 0.0)
            sig_c = sig_c + jnp.where(lane == NH_M + h, _sigmoid(-r["fp_c"]), 0.0)
            carries = carries + jnp.where(lane[0:1, :] == NH_M + h, carry, 0.0)
            dzs = dzs + jnp.where(lane == h, c_st, 0.0)
            dzr_rows[h] = c_in
            dzr_rows[NH_M + h] = _sigmoid(-r["fp_r"])
            wq = r["q"] * jnp.exp(r["b_c"] - m_t)
            dc_scr[h] = jnp.exp(r["g"]) * dc_h + _dot(wq, g256, 0, 0)
        dzs = dzs + (_tri_dot(upper, db_all, True) + carries) * sig_c
        c_in4 = jnp.concatenate(dzr_rows[:NH_M], axis=0)
        dlf_r4 = -_tri_dot(c_in4, lower, False)
        dzr_rows = dzr_rows[:NH_M] + [dlf_r4[h:h + 1, :] * dzr_rows[NH_M + h] for h in range(NH_M)]
        dqk_ref[...] = jnp.concatenate(dqs + dks, axis=1)
        dv_ref[...] = jnp.concatenate(dvs, axis=1).astype(dv_ref.dtype)
        do_ref[...] = jnp.concatenate(dos, axis=1).astype(do_ref.dtype)
        dzs_ref[...] = dzs
        dzr_ref[...] = jnp.concatenate(dzr_rows, axis=0)
        dgm_ref[...] += jnp.concatenate(dgs, axis=1)

    rev = lambda i: nc - 1 - i
    nxt = lambda i: jnp.minimum(nc - i, nc - 1)
    return pl.pallas_call(
        body, name="mlstm_bwd", grid=(nc,),
        in_specs=[pl.BlockSpec((l, NH_M * DK_M), lambda i: (rev(i), 0)), pl.BlockSpec((l, NH_M * DK_M), lambda i: (rev(i), 1)),
                  pl.BlockSpec((l, dm), lambda i: (rev(i), 1)), pl.BlockSpec((l, dm), lambda i: (rev(i), 2)),
                  pl.BlockSpec((l, 128), lambda i: (rev(i), 0)), pl.BlockSpec((8, l), lambda i: (0, rev(i))),
                  pl.BlockSpec((1, 8), lambda i: (0, 0)), pl.BlockSpec((8, 1), lambda i: (0, 0)),
                  pl.BlockSpec((1, dm), lambda i: (0, 0)),
                  pl.BlockSpec((None, NH_M, DK_M, 2 * DV_M), lambda i: (rev(i), 0, 0, 0)),
                  pl.BlockSpec((None, 8, 128), lambda i: (rev(i), 0, 0)),
                  pl.BlockSpec((None, NH_M, DK_M, 2 * DV_M), lambda i: (nxt(i), 0, 0, 0)),
                  pl.BlockSpec((None, 8, 128), lambda i: (nxt(i), 0, 0)),
                  pl.BlockSpec((l, dm), lambda i: (rev(i), 0))],
        out_specs=[pl.BlockSpec((l, dm), lambda i: (rev(i), 0)),
                   pl.BlockSpec((l, dm), lambda i: (rev(i), 0)), pl.BlockSpec((l, dm), lambda i: (rev(i), 0)),
                   pl.BlockSpec((l, 128), lambda i: (rev(i), 0)), pl.BlockSpec((8, l), lambda i: (0, rev(i))),
                   pl.BlockSpec((1, dm), lambda i: (0, 0))],
        out_shape=[jax.ShapeDtypeStruct((t, dm), F32),
                   jax.ShapeDtypeStruct((t, dm), BF16), jax.ShapeDtypeStruct((t, dm), BF16),
                   jax.ShapeDtypeStruct((t, 128), F32), jax.ShapeDtypeStruct((8, t), F32),
                   jax.ShapeDtypeStruct((1, dm), F32)],
        scratch_shapes=[pltpu.VMEM((NH_M, DK_M, 2 * DV_M), F32)],
        compiler_params=_cparams(("arbitrary",)),
    )(qk, qk, zbig, zbig, zs, zsr, bc, br, gm, cst, mst, cst, mst, dycat)


def _fox_cumsum(zsr, bf_r):
    t = zsr.shape[1]
    cw = _pick(t, (512, 256))

    def body(z_ref, b_ref, c_ref):
        up = _tri(cw, False).astype(F32)
        carry = jnp.zeros((NH_F, 1), F32)
        for j in range(t // cw):
            cs = _f32dot(_logsig(z_ref[:, j * cw:(j + 1) * cw] + b_ref[...]), up) + carry
            c_ref[:, j * cw:(j + 1) * cw] = cs
            carry = cs[:, cw - 1:cw]

    return pl.pallas_call(
        body, name="fox_cumsum", grid=(1,),
        in_specs=[pl.BlockSpec((NH_F, t), lambda i: (1, 0)), pl.BlockSpec((NH_F, 1), lambda i: (0, 0))],
        out_specs=pl.BlockSpec((NH_F, t), lambda i: (0, 0)), out_shape=jax.ShapeDtypeStruct((NH_F, t), F32),
        compiler_params=_cparams(("arbitrary",)),
    )(zsr, bf_r)


def _fox_gate_bwd(zsr, bf_r, dc):
    t = zsr.shape[1]
    cw = _pick(t, (512, 256))

    def body(z_ref, b_ref, dc_ref, o_ref):
        low = _tri(cw, True).astype(F32)
        carry = jnp.zeros((NH_F, 1), F32)
        for j in reversed(range(t // cw)):
            sl = slice(j * cw, (j + 1) * cw)
            dlf = _f32dot(dc_ref[:, sl], low) + carry
            o_ref[:, sl] = dlf * _sigmoid(-(z_ref[:, sl] + b_ref[...]))
            carry = dlf[:, 0:1]

    return pl.pallas_call(
        body, name="fox_gate_bwd", grid=(1,),
        in_specs=[pl.BlockSpec((NH_F, t), lambda i: (1, 0)), pl.BlockSpec((NH_F, 1), lambda i: (0, 0)),
                  pl.BlockSpec((NH_F, t), lambda i: (0, 0))],
        out_specs=pl.BlockSpec((NH_F, t), lambda i: (0, 0)), out_shape=jax.ShapeDtypeStruct((NH_F, t), F32),
        compiler_params=_cparams(("arbitrary",)),
    )(zsr, bf_r, dc)


def _causal_mask(n):
    return _tri(n, True)


AUG = 64


def _split3(c):
    hi = c.astype(BF16).astype(F32)
    r1 = c - hi
    mid = r1.astype(BF16).astype(F32)
    return hi, mid, r1 - mid


def _fox_prep(zbig, ct):
    t = zbig.shape[0]
    tm = _pick(t, (1024, 512, 256))

    def body(q_ref, k_ref, v_ref, c_ref, qo_ref, ko_ref, vo_ref):
        lane = lax.broadcasted_iota(jnp.int32, (tm, AUG), 1)
        qv, kv, vv, cv = q_ref[...], k_ref[...], v_ref[...], c_ref[...]
        one = (lane == 0).astype(BF16)
        for h in range(NH_F):
            hi, mid, lo = _split3(cv[:, h:h + 1])
            aq = jnp.where(lane == 0, hi, jnp.where(lane == 1, mid, jnp.where(lane == 2, lo, jnp.where(lane < 6, 1.0, 0.0))))
            ak = jnp.where(lane < 3, 1.0, jnp.where(lane == 3, -hi, jnp.where(lane == 4, -mid, jnp.where(lane == 5, -lo, 0.0))))
            sl = slice(h * DH_F, (h + 1) * DH_F)
            qo_ref[h] = jnp.concatenate([qv[:, sl] * (DH_F ** -0.5), aq.astype(BF16)], axis=1).astype(BF16)
            ko_ref[h] = jnp.concatenate([kv[:, sl], ak.astype(BF16)], axis=1)
            vo_ref[h] = jnp.concatenate([vv[:, sl], one], axis=1)

    ospec = pl.BlockSpec((NH_F, tm, 128), lambda i: (0, i, 0))
    return pl.pallas_call(
        body, name="fox_prep", grid=(t // tm,),
        in_specs=[pl.BlockSpec((tm, 512), lambda i: (i, 3)), pl.BlockSpec((tm, 512), lambda i: (i, 4)),
                  pl.BlockSpec((tm, 512), lambda i: (i, 5)), pl.BlockSpec((tm, NH_F), lambda i: (i, 0))],
        out_specs=[ospec] * 3, out_shape=[jax.ShapeDtypeStruct((NH_F, t, 128), BF16)] * 3,
        compiler_params=_cparams(("parallel",)),
    )(zbig, zbig, zbig, ct)


def _fox_fwd2(qa, ka, va, gf, plan=None):
    nh, t, _ = qa.shape
    tq = _pick(t, (512, 256))
    nq = t // tq
    group = 4
    host = _Hosted(plan, 4, 3)

    def body(*refs):
        (q_ref, k_ref, v_ref, g_ref), (y_ref, o_ref, lse_ref), _, prefs = host.split(refs)
        i = pl.program_id(0)
        host.run(0, i == 0, prefs)
        host.run(1, i == max(nq - 2, 0), prefs)
        lane = lax.broadcasted_iota(jnp.int32, (tq, 128), 1)
        causal = _causal_mask(tq)
        ys, os_ = [], []
        lse_all = jnp.zeros((tq, 128), F32)
        for h0 in range(0, nh, group):
            heads = range(h0, h0 + group)
            qvs = [q_ref[h] for h in heads]

            def blk(j, carry, masked, heads=heads, qvs=qvs):
                k0 = pl.multiple_of(j * tq, tq)
                out = []
                for (m, acc), h, qv in zip(carry, heads, qvs):
                    s = lax.dot_general(qv, k_ref[h, pl.ds(k0, tq), :], (((1,), (1,)), ((), ())), preferred_element_type=F32)
                    if masked:
                        s = jnp.where(causal, s, -jnp.inf)
                    m_new = jnp.maximum(m, jnp.max(s, axis=1, keepdims=True))
                    p = jnp.exp(s - m_new).astype(BF16)
                    pv = lax.dot_general(p, v_ref[h, pl.ds(k0, tq), :], (((1,), (0,)), ((), ())), preferred_element_type=F32)
                    out.append((m_new, jnp.exp(m - m_new) * acc + pv))
                return tuple(out)

            init = tuple((jnp.full((tq, 1), -jnp.inf, F32), jnp.zeros((tq, 128), F32)) for _ in heads)
            carry = lax.fori_loop(0, i, lambda j, c: blk(j, c, False), init)
            for (m, acc), h in zip(blk(i, carry, True), heads):
                l = acc[:, DH_F:DH_F + 1]
                o = acc[:, :DH_F] / l
                os_.append(o)
                gh = g_ref[:, h * DH_F:(h + 1) * DH_F]
                ys.append(o * lax.rsqrt(jnp.mean(o * o, axis=-1, keepdims=True) + EPS) * gh)
                lse_all = lse_all + jnp.where(lane == h, m + jnp.log(l), 0.0)
        y_ref[...] = jnp.concatenate(ys, axis=1).astype(y_ref.dtype)
        o_ref[...] = jnp.concatenate(os_, axis=1)
        lse_ref[...] = lse_all
        host.run(2, i == nq - 1, prefs)

    full = pl.BlockSpec((nh, t, 128), lambda i: (0, 0, 0))
    hc = host.call_args()
    res = pl.pallas_call(
        body, name="fox_fwd", grid=(nq,),
        in_specs=[pl.BlockSpec((nh, tq, 128), lambda i: (0, i, 0)), full, full, pl.BlockSpec((1, nh * DH_F), lambda i: (0, 0))]
        + hc["in_specs"],
        out_specs=[pl.BlockSpec((tq, nh * DH_F), lambda i: (i, 0)), pl.BlockSpec((tq, nh * DH_F), lambda i: (i, 0)),
                   pl.BlockSpec((tq, 128), lambda i: (i, 0))] + hc["out_specs"],
        out_shape=[jax.ShapeDtypeStruct((t, nh * DH_F), BF16), jax.ShapeDtypeStruct((t, nh * DH_F), F32),
                   jax.ShapeDtypeStruct((t, 128), F32)] + hc["out_shape"],
        scratch_shapes=hc["scratch"], input_output_aliases=hc["aliases"], compiler_params=_cparams(("arbitrary",)),
    )(qa, ka, va, gf, *hc["args"])
    return res[:3], res[3:]


def _fox_bwd_prep(dycat, o, gf):
    t = o.shape[0]
    tm = _pick(t, (1024, 512, 256))

    def body(dy_ref, o_ref, g_ref, do_ref, dl_ref, dg_ref):
        lane = lax.broadcasted_iota(jnp.int32, (tm, 128), 1)
        dyv, ov, gv = dy_ref[...], o_ref[...], g_ref[...]
        dgs = []
        dl = jnp.zeros((tm, 128), F32)
        pad = jnp.zeros((tm, AUG), BF16)
        for h in range(NH_F):
            sl = slice(h * DH_F, (h + 1) * DH_F)
            dx, dg = _rms_bwd_val(dyv[:, sl], ov[:, sl], gv[:, sl])
            dgs.append(dg)
            do_ref[h] = jnp.concatenate([dx.astype(BF16), pad], axis=1)
            dl = dl + jnp.where(lane == h, jnp.sum(dx * ov[:, sl], axis=-1, keepdims=True), 0.0)
        dl_ref[...] = dl

        @pl.when(pl.program_id(0) == 0)
        def _():
            dg_ref[...] = jnp.zeros_like(dg_ref)
        dg_ref[...] += jnp.concatenate(dgs, axis=1)

    return pl.pallas_call(
        body, name="fox_bwd_prep", grid=(t // tm,),
        in_specs=[pl.BlockSpec((tm, 512), lambda i: (i, 1)), pl.BlockSpec((tm, 512), lambda i: (i, 0)),
                  pl.BlockSpec((1, 512), lambda i: (0, 0))],
        out_specs=[pl.BlockSpec((NH_F, tm, 128), lambda i: (0, i, 0)), pl.BlockSpec((tm, 128), lambda i: (i, 0)),
                   pl.BlockSpec((1, 512), lambda i: (0, 0))],
        out_shape=[jax.ShapeDtypeStruct((NH_F, t, 128), BF16), jax.ShapeDtypeStruct((t, 128), F32),
                   jax.ShapeDtypeStruct((1, 512), F32)],
        compiler_params=_cparams(("arbitrary",)),
    )(dycat, o, gf)


def _fox_bwd2(qa, ka, va, doa, lse, delta, plan=None):
    nh, t, _ = qa.shape
    tq = _pick(t, (512, 256))
    nq = t // tq

    group = 2

    def tdot(a, b, cb):
        return lax.dot_general(a, b, (((0,), (cb,)), ((), ())), preferred_element_type=F32)

    host = _Hosted(plan, 6, 3)
    ng = nh // group

    def body(*refs):
        (q_ref, k_ref, v_ref, do_ref, lse_ref, dl_ref), (dq_ref, dk_ref, dv_ref), _, prefs = host.split(refs)
        hp, j = pl.program_id(0), pl.program_id(1)
        host.run(0, (hp == 0) & (j == 0), prefs)
        host.run(1, (hp == 0) & (j == 0), prefs)

        @pl.when(j == 0)
        def _():
            dq_ref[...] = jnp.zeros_like(dq_ref)

        lane = lax.broadcasted_iota(jnp.int32, (tq, 128), 1)
        causal = _causal_mask(tq)

        def blk(i, carry, masked):
            rows = pl.ds(pl.multiple_of(i * tq, tq), tq)
            lse_t, dl_t = lse_ref[rows, :], dl_ref[rows, :]
            out = []
            for g, (dk, dv) in enumerate(carry):
                h = hp * group + g
                kb, vb = k_ref[g], v_ref[g]
                qb, dob = q_ref[g, rows, :], do_ref[g, rows, :]
                lse_h = jnp.sum(jnp.where(lane == h, lse_t, 0.0), axis=1, keepdims=True)
                dl_h = jnp.sum(jnp.where(lane == h, dl_t, 0.0), axis=1, keepdims=True)
                s = lax.dot_general(qb, kb, (((1,), (1,)), ((), ())), preferred_element_type=F32)
                if masked:
                    s = jnp.where(causal, s, -jnp.inf)
                p = jnp.exp(s - lse_h)
                dp = lax.dot_general(dob, vb, (((1,), (1,)), ((), ())), preferred_element_type=F32)
                ds = (p * (dp - dl_h)).astype(BF16)
                dv = dv + tdot(dob, p.astype(BF16), 0)
                dk = dk + tdot(qb, ds, 0)
                dq_ref[g, :, rows] += tdot(kb, ds, 1)
                out.append((dk, dv))
            return tuple(out)

        init = tuple((jnp.zeros((128, tq), F32), jnp.zeros((128, tq), F32)) for _ in range(group))
        carry = blk(j, init, True)
        carry = lax.fori_loop(j + 1, nq, lambda i, c: blk(i, c, False), carry)
        for g, (dk, dv) in enumerate(carry):
            dk_ref[g] = dk
            dv_ref[g] = dv
        host.run(2, (hp == ng - 1) & (j == nq - 1), prefs)

    full = pl.BlockSpec((group, t, 128), lambda h, j: (h, 0, 0))
    tile = pl.BlockSpec((group, tq, 128), lambda h, j: (h, j, 0))
    cols = pl.BlockSpec((t, 128), lambda h, j: (0, 0))
    full_t = pl.BlockSpec((group, 128, t), lambda h, j: (h, 0, 0))
    tile_t = pl.BlockSpec((group, 128, tq), lambda h, j: (h, 0, j))
    hc = host.call_args()
    res = pl.pallas_call(
        body, name="fox_bwd", grid=(ng, nq), in_specs=[full, tile, tile, full, cols, cols] + hc["in_specs"],
        out_specs=[full_t, tile_t, tile_t] + hc["out_specs"],
        out_shape=[jax.ShapeDtypeStruct((nh, 128, t), F32)] * 3 + hc["out_shape"], scratch_shapes=hc["scratch"],
        input_output_aliases=hc["aliases"], compiler_params=_cparams(("arbitrary", "arbitrary")),
    )(qa, ka, va, doa, lse, delta, *hc["args"])
    return res[:3], res[3:]


def _fox_bwd_post(dqa, dka, dva):
    nh, _, t = dqa.shape
    tm = _pick(t, (1024, 512, 256))

    def body(dq_ref, dk_ref, dv_ref, oq_ref, ok_ref, ov_ref, dc_ref):
        qs, ks, vs, dcs = [], [], [], []
        for h in range(nh):
            dq, dk = dq_ref[h], dk_ref[h]
            qs.append(dq.T[:, :DH_F] * (DH_F ** -0.5))
            ks.append(dk.T[:, :DH_F])
            vs.append(dv_ref[h].T[:, :DH_F])
            dcs.append(dq[DH_F:DH_F + 1, :] - dk[DH_F + 3:DH_F + 4, :])
        oq_ref[...] = jnp.concatenate(qs, axis=1).astype(BF16)
        ok_ref[...] = jnp.concatenate(ks, axis=1).astype(BF16)
        ov_ref[...] = jnp.concatenate(vs, axis=1).astype(BF16)
        dc_ref[...] = jnp.concatenate(dcs, axis=0)

    ispec = pl.BlockSpec((nh, 128, tm), lambda i: (0, 0, i))
    ospec = pl.BlockSpec((tm, nh * DH_F), lambda i: (i, 0))
    return pl.pallas_call(
        body, name="fox_bwd_post", grid=(t // tm,), in_specs=[ispec] * 3,
        out_specs=[ospec] * 3 + [pl.BlockSpec((nh, tm), lambda i: (0, i))],
        out_shape=[jax.ShapeDtypeStruct((t, nh * DH_F), BF16)] * 3 + [jax.ShapeDtypeStruct((nh, t), F32)],
        compiler_params=_cparams(("parallel",)),
    )(dqa, dka, dva)


IN_OFF = (0, 512, 1024, 1544, 2056, 2568)
IN_GATES = (1536, 3080)


FFN1 = ("ffn1_w_gate", "ffn1_w_up", "ffn1_w_down")
REST = ("w_in", "w_out", "ffn2_w_gate", "ffn2_w_up", "ffn2_w_down", "w_ple_gate", "w_ple_proj")
SPLIT = {n: 1 if n == "w_in" else 0 for n in FFN1 + REST}
SAME_SHAPE = (FFN1, ("ffn2_w_gate", "ffn2_w_up", "ffn2_w_down"), ("w_out", "w_ple_gate"), ("w_in",), ("w_ple_proj",))


def _grouped(names):
    return [tuple(n for n in grp if n in names) for grp in SAME_SHAPE if any(n in names for n in grp)]


def _rs_partials(names, gw, c_idx, twins, run_swap=None):
    wire = [twins[n] if n in twins else _cast_other_half("rs_cast_" + n, gw[n], c_idx, SPLIT[n]) for n in names]
    plan = _swap_plan(wire, [SPLIT[n] if n in twins else None for n in names])
    swapped = dict(zip(names, run_swap(plan) if run_swap else _run_plan("rs_swap_" + names[0], plan)))
    out = {}
    for grp in _grouped(names):
        res = _add_my_half("rs_add_" + grp[0], [gw[n] for n in grp], [swapped[n] for n in grp], c_idx, SPLIT[grp[0]])
        out.update(zip(grp, res))
    return [out[n] for n in names]


def _local_step(x, p, tgt, sp, wg1, wu1, wd1, rest_slots, c_idx, place):
    t, d = x.shape
    slot = dict(zip(REST + ("conv_qk",), rest_slots))
    (h1, xn1, g1, u1), (w_in, conv_w) = _ffn_fwd(
        "ffn1", x, sp["ffn1_norm"], wg1, wu1, wd1, plan=_gather_plan([slot["w_in"], slot["conv_qk"]], [SPLIT["w_in"], None]))
    w_in, conv_w = w_in.reshape(-1, d), _from_chip_blocks(conv_w)
    w_big = jnp.concatenate([w_in[o:o + 512] for o in IN_OFF], axis=0)
    w_small = jnp.concatenate([w_in[IN_GATES[0]:IN_GATES[0] + 8], w_in[IN_GATES[1]:IN_GATES[1] + 8],
                               jnp.zeros((112, d), w_in.dtype)], axis=0)
    u, zbig = _norm_mm("in_big", h1, sp["mix_norm"], w_big, True, BF16)
    zs = _mm_nt("in_small", u, w_small, tm=1024, tk=1024)
    zsr = zs.T
    qk_act = _conv_fwd(zbig, conv_w)
    bm_c, bf_c = sp["b_mlstm_gates"], sp["b_fox_f"]
    y_m, cst, mst = _mlstm_fwd(qk_act, zbig, zs, zsr, bm_c, bm_c.T, sp["mlstm_out_norm"])
    c = _fox_cumsum(zsr, bf_c.T)
    qa, ka, va = _fox_prep(zbig, c.T)
    (y_ft, o_f, lse), late = _fox_fwd2(qa, ka, va, sp["fox_out_norm"],
                                       plan=_gather_plan([slot[n] for n in REST[1:]], [SPLIT[n] for n in REST[1:]]))
    full = dict(zip(REST[1:], late))
    w_out, w_pg = (full[n].reshape(-1, d) for n in ("w_out", "w_ple_gate"))
    wg2, wu2, wd2 = full["ffn2_w_gate"], full["ffn2_w_up"], full["ffn2_w_down"]
    w_pp = _from_chip_blocks(full["w_ple_proj"])
    tm = _pick(t, (1024, 512, 256))
    h2 = _mm("out_proj", [(y_m, (tm, 512), lambda i, j, k: (i, 0), w_out, (512, d), lambda i, j, k: (0, 0)),
                          (y_ft, (tm, 512), lambda i, j, k: (i, 0), w_out, (512, d), lambda i, j, k: (1, 0))],
             (t, d), (tm, d), lambda i, j, k: (i, 0), (t // tm, 1, 1), 2, res=h1)
    (h3, xn2, g2, u2), _ = _ffn_fwd("ffn2", h2, sp["ffn2_norm"], wg2, wu2, wd2)
    hn3, gate_pre = _norm_mm("ple_gate", h3, sp["ple_gate_norm"], w_pg, False, F32)
    pp = _mm_nn("ple_proj", p, w_pp, tm=1024)

    def head_fn(h3_t, gp_t, pp_t, tgt_t, g_pp, g_fin):
        gate = _sigmoid(gp_t)
        ppn = _rms_fwd_val(pp_t, g_pp)
        h4 = h3_t + gate * ppn
        err = _rms_fwd_val(h4, g_fin) - tgt_t
        loss = 0.5 * jnp.sum(jnp.mean(err * err, axis=-1, keepdims=True), axis=0, keepdims=True)
        dh4, dg_fin = _rms_bwd_val(err * (1.0 / d), h4, g_fin)
        dpp, dg_pp = _rms_bwd_val(dh4 * gate, pp_t, g_pp)
        dgp = dh4 * ppn * gate * (1.0 - gate)
        return dh4, dgp, dpp, jnp.broadcast_to(loss, (1, 128)), dg_fin, dg_pp

    dh4, dgp, dpp, loss_part, dg_fin, dg_pp = _rowwise(
        "loss_head", head_fn, [h3, gate_pre, pp, tgt], [sp["ple_proj_norm"], sp["final_norm"]],
        [(d, F32), (d, BF16), (d, BF16)], [((1, 128), F32), ((1, d), F32), ((1, d), F32)])
    gw, gs = {}, {"final_norm": dg_fin, "ple_proj_norm": dg_pp}
    gw["w_ple_gate"] = _mm_tn("d_w_pg", hn3, dgp, tm=1024, tn=1024)
    gw["w_ple_proj"] = _mm_tn("d_w_pp", p, dpp, tn=1024)
    dhn3 = _mm_nt("d_hn3", dgp, w_pg, tm=1024, tn=1024, tk=1024)

    def res_norm_bwd(dn_t, h_t, dres_t, g):
        dx, dg = _rms_bwd_val(dn_t, h_t, g)
        return dres_t + dx, dg

    dh3, gs["ple_gate_norm"] = _rowwise("ple_norm_bwd", res_norm_bwd, [dhn3, h3, dh4], [sp["ple_gate_norm"]],
                                        [(d, F32)], [((1, d), F32)])
    (dh2, gs["ffn2_norm"], gw["ffn2_w_gate"], gw["ffn2_w_up"], gw["ffn2_w_down"]), _, twins2 = _ffn_bwd(
        "ffn2", dh3, h2, sp["ffn2_norm"], xn2, g2, u2, wg2, wu2, wd2)
    ffn2_names = ("ffn2_w_gate", "ffn2_w_up", "ffn2_w_down")
    early = []

    def swap_in_d_ycat(plan):
        dyc, swapped = _mm_nt("d_ycat", dh2, w_out, tm=1024, tn=1024, tk=1024, plan=plan)
        early.append(dyc)
        return swapped

    part_ffn2 = dict(zip(ffn2_names, _rs_partials(ffn2_names, gw, c_idx, dict(zip(ffn2_names, twins2)), swap_in_d_ycat)))
    dycat = early[0]
    gw["w_out"] = jnp.concatenate([_mm_tn("d_w_out_m", y_m, dh2, tn=1024, tk=2048),
                                   _mm_tn("d_w_out_f", y_ft, dh2, tn=1024, tk=2048)], axis=0)
    doa, delta, gs["fox_out_norm"] = _fox_bwd_prep(dycat, o_f, sp["fox_out_norm"])
    dqkv_t, landed_ffn2 = _fox_bwd2(qa, ka, va, doa, lse, delta, plan=_scatter_plan([part_ffn2[n][1] for n in ffn2_names]))
    dq_f, dk_f, dv_f, dct = _fox_bwd_post(*dqkv_t)
    dfp = _fox_gate_bwd(zsr, bf_c.T, dct)
    dact, dv_m, do_m, dzs_m, dzr_m, gs["mlstm_out_norm"] = _mlstm_bwd(
        qk_act, zbig, zs, zsr, bm_c, bm_c.T, sp["mlstm_out_norm"], cst, mst, dycat)
    dqk, gw["conv_qk"] = _conv_bwd(zbig, dact, conv_w)
    dz_big = jnp.concatenate([dqk, dv_m, do_m, dq_f, dk_f, dv_f], axis=1)
    dzs = dzs_m + jnp.pad(jnp.concatenate([dzr_m, dfp], axis=0).T, ((0, 0), (0, 112)))
    dw_big = _mm_tn("d_w_big", dz_big, u, tn=1024)
    dw_small = _mm_tn("d_w_small", dzs, u, tn=1024)
    gw["w_in"] = jnp.concatenate([dw_big[0:1536], dw_small[0:8], dw_big[1536:3072], dw_small[8:16]], axis=0)
    du_a = _mm_nn("d_u_big", dz_big, w_big, tm=1024, tn=1024, tk=3072)
    du_b = _mm_nn("d_u_small", dzs, w_small, tm=1024, tn=1024)

    def mix_norm_bwd(da_t, db_t, h_t, dres_t, dzs_t, g):
        dx, dg = _rms_bwd_val(da_t + db_t, h_t, g)
        return dres_t + dx, dg, _colsum(dzs_t)

    conv_grad = gw.pop("conv_qk")
    gw["w_ple_proj"] = _chip_blocks(gw["w_ple_proj"])
    for n in ("w_in", "w_out", "w_ple_gate"):
        gw[n] = gw[n].reshape(4, -1, gw[n].shape[-1])
    mix = []

    def swap_in_mix_norm_bwd(plan):
        res, swapped = _rowwise("mix_norm_bwd", mix_norm_bwd, [du_a, du_b, h1, dh2, dzs], [sp["mix_norm"]],
                                [(d, F32)], [((1, d), F32), ((1, 128), F32)], plan=plan)
        mix.extend(res)
        return swapped

    light = ("w_in", "w_out", "w_ple_gate", "w_ple_proj")
    part_light = dict(zip(light, _rs_partials(light, gw, c_idx, {}, swap_in_mix_norm_bwd)))
    dh1, gs["mix_norm"], dbias = mix
    gs["b_mlstm_gates"], gs["b_fox_f"] = dbias[:, 0:8], dbias[:, 8:16]
    part_ffn1 = []

    def own_plan(dws, dw_twins):
        part_ffn1.extend(_rs_partials(FFN1, dict(zip(FFN1, dws)), c_idx, dict(zip(FFN1, dw_twins))))
        return _scatter_plan([pb for _, pb in part_ffn1])

    (grad_x, gs["ffn1_norm"], _, _, _), (l_light, _, _, _, landed_ffn1) = _ffn_bwd_late_dx(
        "ffn1", dh1, x, sp["ffn1_norm"], xn1, g1, u1, wg1, wu1, wd1,
        _scatter_plan([part_light[n][1] for n in light]), [None] * 3, own_plan)
    names = REST + FFN1
    parts = {**part_light, **part_ffn2, **dict(zip(FFN1, part_ffn1))}
    landed = {**dict(zip(light, l_light)), **dict(zip(ffn2_names, landed_ffn2)), **dict(zip(FFN1, landed_ffn1))}
    mine = {}
    for grp in _grouped(names):
        res = _sum4("rs_sum_" + grp[0], [landed[n] for n in grp], [parts[n][0] for n in grp], place, SPLIT[grp[0]])
        mine.update(zip(grp, res))
    grads = dict(zip(names, _join_halves("rs_join", [mine[n] for n in names], [SPLIT[n] for n in names])))
    return loss_part, grad_x, grads, gs, conv_grad


ANY = pl.BlockSpec(memory_space=pl.ANY)
MESH = pl.DeviceIdType.MESH


def _place():
    x, y, c = lax.axis_index("x"), lax.axis_index("y"), lax.axis_index("c")
    chips = [(1 - x, y), (x, 1 - y), (1 - x, 1 - y)]
    return x, y, c, 2 * x + y, (x, y, 1 - c), chips


def _rcopy(src, dst, ssem, rsem, dev):
    return pltpu.make_async_remote_copy(src_ref=src, dst_ref=dst, send_sem=ssem, recv_sem=rsem, device_id=dev,
                                        device_id_type=MESH)


def _half(ref, lead, axis, idx, half):
    return ref.at[(slice(None),) * (lead + axis) + (pl.ds(idx * half, half),)]


def _to_slot(name, arrs, me_idx, dtype):
    n = len(arrs)
    r, cdim = arrs[0].shape
    tr = _pick(r, (352, 256, 176, 128, 64))

    def body(me_ref, *refs):
        for k in range(n):
            refs[n + k][...] = refs[k][...].astype(dtype)

    return pl.pallas_call(
        body, name=name,
        grid_spec=pltpu.PrefetchScalarGridSpec(
            num_scalar_prefetch=1, grid=(r // tr,), in_specs=[pl.BlockSpec((tr, cdim), lambda i, me_ref: (i, 0))] * n,
            out_specs=[pl.BlockSpec((None, tr, cdim), lambda i, me_ref: (me_ref[0], i, 0))] * n),
        out_shape=[jax.ShapeDtypeStruct((4, r, cdim), dtype)] * n, compiler_params=_cparams(("parallel",)),
    )(me_idx, *arrs)


def _gather4(name, bufs, split):
    return _run_plan(name, _gather_plan(bufs, split))


def _gather_plan(bufs, split):
    n = len(bufs)
    shapes = [b.shape[1:] for b in bufs]

    def ctx(outs):
        x, y, c, me, sib, chips = _place()

        def part(ref, a, which):
            if split[a] is None:
                return ref
            return _half(ref, 0, split[a], which, shapes[a][split[a]] // 2)

        return c, me, sib, chips, part

    def ici(outs, sems, a, j, chip, c, me, part):
        mine = part(outs[a].at[me], a, c)
        return _rcopy(mine, mine, sems[0].at[3 * a + j], sems[1].at[3 * a + j], (*chip, c))

    def fwd(outs, sems, a, j, chip, c, sib, part, which):
        blk = part(outs[a].at[2 * chip[0] + chip[1]], a, which)
        return _rcopy(blk, blk, sems[2].at[3 * a + j], sems[3].at[3 * a + j], sib)

    def start(ins, outs, sems):
        c, me, sib, chips, part = ctx(outs)
        for a in range(n):
            for j, chip in enumerate(chips):
                ici(outs, sems, a, j, chip, c, me, part).start()

    def mid(ins, outs, sems):
        c, me, sib, chips, part = ctx(outs)
        for j, chip in enumerate(chips):
            for a in range(n):
                blk = part(outs[a].at[2 * chip[0] + chip[1]], a, c)
                _rcopy(blk, blk, sems[0].at[3 * a + j], sems[1].at[3 * a + j], sib).wait_recv()
                if split[a] is not None:
                    fwd(outs, sems, a, j, chip, c, sib, part, c).start()

    def end(ins, outs, sems):
        c, me, sib, chips, part = ctx(outs)
        for j, chip in enumerate(chips):
            for a in range(n):
                if split[a] is not None:
                    fwd(outs, sems, a, j, chip, c, sib, part, 1 - c).wait_recv()
        for a in range(n):
            for j, chip in enumerate(chips):
                ici(outs, sems, a, j, chip, c, me, part).wait_send()
                if split[a] is not None:
                    fwd(outs, sems, a, j, chip, c, sib, part, c).wait_send()

    return dict(ins=list(bufs), outs=[jax.ShapeDtypeStruct(b.shape, b.dtype) for b in bufs], alias=True,
                sems=[pltpu.SemaphoreType.DMA((3 * n,))] * 4, phases=(start, mid, end))


def _run_plan(name, plan):
    ni, no = len(plan["ins"]), len(plan["outs"])

    def body(*refs):
        ins, outs, sems = refs[:ni], refs[ni:ni + no], refs[ni + no:]
        for phase in plan["phases"]:
            phase(ins, outs, sems)

    return pl.pallas_call(
        body, name=name, in_specs=[ANY] * ni, out_specs=[ANY] * no, out_shape=plan["outs"],
        input_output_aliases={a: a for a in range(ni)} if plan["alias"] else {}, scratch_shapes=plan["sems"],
    )(*plan["ins"])


class _Hosted:
    def __init__(self, plan, n_in, n_out):
        self.plan, self.n_in, self.n_out = plan, n_in, n_out
        self.ni, self.no, self.ns = (len(plan["ins"]) if plan else 0, len(plan["outs"]) if plan else 0,
                                     len(plan["sems"]) if plan else 0)

    def split(self, refs):
        a, b = self.n_in, self.n_in + self.ni
        c, d = b + self.n_out, b + self.n_out + self.no
        e = len(refs) - self.ns
        return refs[:a], refs[b:c], refs[d:e], (refs[a:b], refs[c:d], refs[e:])

    def run(self, k, cond, prefs):
        if self.plan is not None:
            @pl.when(cond)
            def _():
                self.plan["phases"][k](*prefs)

    def call_args(self):
        p = self.plan
        if p is None:
            return dict(in_specs=[], out_specs=[], out_shape=[], scratch=[], aliases={}, args=[])
        al = {self.n_in + a: self.n_out + a for a in range(self.ni)} if p["alias"] else {}
        return dict(in_specs=[ANY] * self.ni, out_specs=[ANY] * self.no, out_shape=list(p["outs"]), scratch=list(p["sems"]),
                    aliases=al, args=list(p["ins"]))


def _swap(name, arrs, halve):
    return _run_plan(name, _swap_plan(arrs, halve))


def _swap_plan(arrs, halve):
    n = len(arrs)

    def half_shape(a, ax):
        return a.shape if ax is None else (a.shape[0],) + tuple(d // 2 if i == ax else d for i, d in enumerate(a.shape[1:]))

    def copies(ins, outs, sems):
        x, y, c, me, sib, chips = _place()
        cps = []
        for a in range(n):
            src = ins[a] if halve[a] is None else _half(ins[a], 1, halve[a], 1 - c, arrs[a].shape[1 + halve[a]] // 2)
            cps.append(_rcopy(src, outs[a], sems[0].at[a], sems[1].at[a], sib))
        return cps

    def start(ins, outs, sems):
        for cp in copies(ins, outs, sems):
            cp.start()

    def mid(ins, outs, sems):
        pass

    def end(ins, outs, sems):
        for cp in copies(ins, outs, sems):
            cp.wait()

    return dict(ins=list(arrs), outs=[jax.ShapeDtypeStruct(half_shape(a, ax), a.dtype) for a, ax in zip(arrs, halve)],
                alias=False, sems=[pltpu.SemaphoreType.DMA((n,))] * 2, phases=(start, mid, end))


def _scatter4(name, arrs):
    return _run_plan(name, _scatter_plan(arrs))


def _scatter_plan(arrs):
    n = len(arrs)

    def send(ins, outs, sems, a, j, chip, c, me):
        return _rcopy(ins[a].at[2 * chip[0] + chip[1]], outs[a].at[me], sems[0].at[3 * a + j], sems[1].at[3 * a + j], (*chip, c))

    def start(ins, outs, sems):
        x, y, c, me, sib, chips = _place()
        for a in range(n):
            for j, chip in enumerate(chips):
                send(ins, outs, sems, a, j, chip, c, me).start()

    def mid(ins, outs, sems):
        pass

    def end(ins, outs, sems):
        x, y, c, me, sib, chips = _place()
        for a in range(n):
            for j, chip in enumerate(chips):
                blk = outs[a].at[2 * chip[0] + chip[1]]
                _rcopy(blk, blk, sems[0].at[3 * a + j], sems[1].at[3 * a + j], sib).wait_recv()
        for a in range(n):
            for j, chip in enumerate(chips):
                send(ins, outs, sems, a, j, chip, c, me).wait_send()

    return dict(ins=list(arrs), outs=[jax.ShapeDtypeStruct(a.shape, a.dtype) for a in arrs], alias=False,
                sems=[pltpu.SemaphoreType.DMA((3 * n,))] * 2, phases=(start, mid, end))


def _join_halves(name, arrs, split):
    n = len(arrs)

    def body(*refs):
        outs = refs[n:2 * n]
        ssem, rsem = refs[2 * n:]
        x, y, c, me, sib, chips = _place()
        cps = []
        for a in range(n):
            mine = _half(outs[a], 0, split[a], c, arrs[a].shape[split[a]] // 2)
            cp = _rcopy(mine, mine, ssem.at[a], rsem.at[a], sib)
            cp.start()
            cps.append(cp)
        for a in range(n):
            blk = _half(outs[a], 0, split[a], 1 - c, arrs[a].shape[split[a]] // 2)
            _rcopy(blk, blk, ssem.at[a], rsem.at[a], sib).wait_recv()
        for cp in cps:
            cp.wait_send()

    return pl.pallas_call(
        body, name=name, in_specs=[ANY] * n, out_specs=[ANY] * n,
        out_shape=[jax.ShapeDtypeStruct(a.shape, a.dtype) for a in arrs],
        input_output_aliases={a: a for a in range(n)}, scratch_shapes=[pltpu.SemaphoreType.DMA((n,))] * 2,
    )(*arrs)


def _allreduce_small(s):
    r, cdim = s.shape

    def body(s_ref, o_ref, buf, ssem, rsem):
        x, y, c, me, sib, chips = _place()
        me8 = 4 * x + 2 * y + c
        buf[me8] = s_ref[...]
        flips = [(fx, fy, fc) for fx in (0, 1) for fy in (0, 1) for fc in (0, 1)][1:]
        cps = []
        for k, (fx, fy, fc) in enumerate(flips):
            peer = (x ^ fx if fx else x, y ^ fy if fy else y, c ^ fc if fc else c)
            cp = _rcopy(s_ref, buf.at[me8], ssem.at[k], rsem.at[k], peer)
            cp.start()
            cps.append(cp)
        for k, (fx, fy, fc) in enumerate(flips):
            src = 4 * (x ^ fx if fx else x) + 2 * (y ^ fy if fy else y) + (c ^ fc if fc else c)
            _rcopy(s_ref, buf.at[src], ssem.at[k], rsem.at[k], sib).wait_recv()
        for cp in cps:
            cp.wait_send()
        acc = buf[0]
        for k in range(1, 8):
            acc = acc + buf[k]
        o_ref[...] = acc

    vm = pl.BlockSpec(memory_space=pltpu.VMEM)
    return pl.pallas_call(
        body, name="allreduce_small", in_specs=[vm], out_specs=vm, out_shape=jax.ShapeDtypeStruct((r, cdim), F32),
        scratch_shapes=[pltpu.VMEM((8, r, cdim), F32), pltpu.SemaphoreType.DMA((7,)), pltpu.SemaphoreType.DMA((7,))],
    )(s)


def _add_my_half(name, gs, recvs, c_idx, axis):
    n = len(gs)
    nb, hr, hc = recvs[0].shape
    tr = _pick(hr, (256, 176, 128, 64))
    if axis == 0:
        g4s = [g.reshape(nb, 2, hr, hc) for g in gs]
        gspec = pl.BlockSpec((None, None, tr, hc), lambda b, i, c_ref: (b, c_ref[0], i, 0))
    else:
        g4s = list(gs)
        gspec = pl.BlockSpec((None, tr, hc), lambda b, i, c_ref: (b, i, c_ref[0]))

    def body(c_ref, *refs):
        for k in range(n):
            s = refs[k][...] + refs[n + k][...].astype(F32)
            refs[2 * n + 2 * k][...] = s
            refs[2 * n + 2 * k + 1][...] = s.astype(BF16)

    ospec = pl.BlockSpec((None, tr, hc), lambda b, i, c_ref: (b, i, 0))
    res = pl.pallas_call(
        body, name=name,
        grid_spec=pltpu.PrefetchScalarGridSpec(
            num_scalar_prefetch=1, grid=(nb, hr // tr), in_specs=[gspec] * n + [ospec] * n, out_specs=[ospec] * (2 * n)),
        out_shape=[jax.ShapeDtypeStruct((nb, hr, hc), F32), jax.ShapeDtypeStruct((nb, hr, hc), BF16)] * n,
        compiler_params=_cparams(("parallel", "parallel")),
    )(c_idx, *g4s, *recvs)
    return [(res[2 * k], res[2 * k + 1]) for k in range(n)]


def _sum4(name, landeds, owns, place, axis):
    n = len(landeds)
    nb, h, cdim = landeds[0].shape
    tr = _pick(h, (256, 176, 128, 64))
    nt = h // tr

    def body(p_ref, *refs):
        for k in range(n):
            a1, a2, a3, own = refs[4 * k:4 * k + 4]
            refs[4 * n + k][...] = ((own[...] + a1[...].astype(F32)) + a2[...].astype(F32)) + a3[...].astype(F32)

    def nxt(k):
        return pl.BlockSpec((None, tr, cdim), lambda i, p_ref: ((p_ref[0] + k) % nb, i, 0))

    if axis == 0:
        ospec = pl.BlockSpec((tr, cdim), lambda i, p_ref: (p_ref[1] * nt + i, 0))
        oshape = (2 * h, cdim)
    else:
        ospec = pl.BlockSpec((tr, cdim), lambda i, p_ref: (i, p_ref[1]))
        oshape = (h, 2 * cdim)
    args = []
    for landed, own in zip(landeds, owns):
        args += [landed, landed, landed, own]
    return pl.pallas_call(
        body, name=name,
        grid_spec=pltpu.PrefetchScalarGridSpec(
            num_scalar_prefetch=1, grid=(nt,), in_specs=[nxt(1), nxt(2), nxt(3), nxt(0)] * n, out_specs=[ospec] * n),
        out_shape=[jax.ShapeDtypeStruct(oshape, F32)] * n, compiler_params=_cparams(("parallel",)),
    )(place, *args)


def _cast_other_half(name, g, c_idx, axis):
    nb, r, cdim = g.shape
    hr, hc = (r // 2, cdim) if axis == 0 else (r, cdim // 2)
    tr = _pick(hr, (256, 176, 128, 64))
    if axis == 0:
        g4 = g.reshape(nb, 2, hr, hc)
        gspec = pl.BlockSpec((None, None, tr, hc), lambda b, i, c_ref: (b, 1 - c_ref[0], i, 0))
    else:
        g4 = g
        gspec = pl.BlockSpec((None, tr, hc), lambda b, i, c_ref: (b, i, 1 - c_ref[0]))

    def body(c_ref, g_ref, o_ref):
        o_ref[...] = g_ref[...].astype(BF16)

    return pl.pallas_call(
        body, name=name,
        grid_spec=pltpu.PrefetchScalarGridSpec(
            num_scalar_prefetch=1, grid=(nb, hr // tr), in_specs=[gspec],
            out_specs=pl.BlockSpec((None, tr, hc), lambda b, i, c_ref: (b, i, 0))),
        out_shape=jax.ShapeDtypeStruct((nb, hr, hc), BF16), compiler_params=_cparams(("parallel", "parallel")),
    )(c_idx, g4)


def _adamw(name, ws, gs, ms, vs):
    n = len(ws)
    c1 = 1.0 - ADAM_B1 ** ADAM_STEP
    c2 = 1.0 - ADAM_B2 ** ADAM_STEP

    def fn(*tiles):
        out = []
        for k in range(n):
            w_t, g_t, m_t, v_t = tiles[4 * k:4 * k + 4]
            m_n = ADAM_B1 * m_t + (1.0 - ADAM_B1) * g_t
            v_n = ADAM_B2 * v_t + (1.0 - ADAM_B2) * (g_t * g_t)
            out += [-ADAM_LR * ((m_n / c1) / (jnp.sqrt(v_n / c2) + ADAM_EPS) + ADAM_WD * w_t), m_n, v_n]
        return out

    rows, cdim = ws[0].shape
    tiled = [a for quad in zip(ws, gs, ms, vs) for a in quad]
    pref = (512, 352, 256, 128, 64, 8) if n == 1 else (176, 128, 64, 8)
    res = _rowwise(name, fn, tiled, [], [(cdim, F32)] * (3 * n), tm=_pick(rows, pref))
    return [tuple(res[3 * k:3 * k + 3]) for k in range(n)]


BIG = ("ffn1_w_gate", "ffn1_w_up", "ffn1_w_down", "w_in", "w_out", "ffn2_w_gate", "ffn2_w_up", "ffn2_w_down",
       "w_ple_gate", "w_ple_proj")
SMALL = ("ffn1_norm", "mix_norm", "b_mlstm_gates", "b_fox_f", "mlstm_out_norm", "fox_out_norm", "ffn2_norm",
         "ple_gate_norm", "ple_proj_norm", "final_norm")
WEIGHTS = ("ffn1_norm", "ffn1_w_gate", "ffn1_w_up", "ffn1_w_down", "mix_norm", "w_in", "conv_qk", "b_mlstm_gates",
           "b_fox_f", "mlstm_out_norm", "fox_out_norm", "w_out", "ffn2_norm", "ffn2_w_gate", "ffn2_w_up", "ffn2_w_down",
           "ple_gate_norm", "w_ple_gate", "w_ple_proj", "ple_proj_norm", "final_norm")
TRANSPOSED = ("ffn1_w_gate", "ffn1_w_up", "w_in", "ffn2_w_gate", "ffn2_w_up")
PACK_W = 1024


def _chip_blocks(a):
    r, c4 = a.shape
    return a.reshape(r, 4, c4 // 4).transpose(1, 0, 2)


def _from_chip_blocks(a):
    nb, r, c = a.shape
    return a.transpose(1, 0, 2).reshape(r, nb * c)


def kernel(x, p, ffn1_norm, ffn1_w_gate, ffn1_w_up, ffn1_w_down, mix_norm, w_in, conv_qk, b_mlstm_gates, b_fox_f, mlstm_out_norm, fox_out_norm, w_out, ffn2_norm, ffn2_w_gate, ffn2_w_up, ffn2_w_down, ple_gate_norm, w_ple_gate, w_ple_proj, ple_proj_norm, final_norm, loss_target, m_ffn1_norm, m_ffn1_w_gate, m_ffn1_w_up, m_ffn1_w_down, m_mix_norm, m_w_in, m_conv_qk, m_b_mlstm_gates, m_b_fox_f, m_mlstm_out_norm, m_fox_out_norm, m_w_out, m_ffn2_norm, m_ffn2_w_gate, m_ffn2_w_up, m_ffn2_w_down, m_ple_gate_norm, m_w_ple_gate, m_w_ple_proj, m_ple_proj_norm, m_final_norm, v_ffn1_norm, v_ffn1_w_gate, v_ffn1_w_up, v_ffn1_w_down, v_mix_norm, v_w_in, v_conv_qk, v_b_mlstm_gates, v_b_fox_f, v_mlstm_out_norm, v_fox_out_norm, v_w_out, v_ffn2_norm, v_ffn2_w_gate, v_ffn2_w_up, v_ffn2_w_down, v_ple_gate_norm, v_w_ple_gate, v_w_ple_proj, v_ple_proj_norm, v_final_norm):
    w = dict(ffn1_norm=ffn1_norm, ffn1_w_gate=ffn1_w_gate, ffn1_w_up=ffn1_w_up, ffn1_w_down=ffn1_w_down, mix_norm=mix_norm,
             w_in=w_in, conv_qk=conv_qk, b_mlstm_gates=b_mlstm_gates, b_fox_f=b_fox_f, mlstm_out_norm=mlstm_out_norm,
             fox_out_norm=fox_out_norm, w_out=w_out, ffn2_norm=ffn2_norm, ffn2_w_gate=ffn2_w_gate, ffn2_w_up=ffn2_w_up,
             ffn2_w_down=ffn2_w_down, ple_gate_norm=ple_gate_norm, w_ple_gate=w_ple_gate, w_ple_proj=w_ple_proj,
             ple_proj_norm=ple_proj_norm, final_norm=final_norm)
    m = dict(ffn1_norm=m_ffn1_norm, ffn1_w_gate=m_ffn1_w_gate, ffn1_w_up=m_ffn1_w_up, ffn1_w_down=m_ffn1_w_down,
             mix_norm=m_mix_norm, w_in=m_w_in, conv_qk=m_conv_qk, b_mlstm_gates=m_b_mlstm_gates, b_fox_f=m_b_fox_f,
             mlstm_out_norm=m_mlstm_out_norm, fox_out_norm=m_fox_out_norm, w_out=m_w_out, ffn2_norm=m_ffn2_norm,
             ffn2_w_gate=m_ffn2_w_gate, ffn2_w_up=m_ffn2_w_up, ffn2_w_down=m_ffn2_w_down, ple_gate_norm=m_ple_gate_norm,
             w_ple_gate=m_w_ple_gate, w_ple_proj=m_w_ple_proj, ple_proj_norm=m_ple_proj_norm, final_norm=m_final_norm)
    v = dict(ffn1_norm=v_ffn1_norm, ffn1_w_gate=v_ffn1_w_gate, ffn1_w_up=v_ffn1_w_up, ffn1_w_down=v_ffn1_w_down,
             mix_norm=v_mix_norm, w_in=v_w_in, conv_qk=v_conv_qk, b_mlstm_gates=v_b_mlstm_gates, b_fox_f=v_b_fox_f,
             mlstm_out_norm=v_mlstm_out_norm, fox_out_norm=v_fox_out_norm, w_out=v_w_out, ffn2_norm=v_ffn2_norm,
             ffn2_w_gate=v_ffn2_w_gate, ffn2_w_up=v_ffn2_w_up, ffn2_w_down=v_ffn2_w_down, ple_gate_norm=v_ple_gate_norm,
             w_ple_gate=v_w_ple_gate, w_ple_proj=v_w_ple_proj, ple_proj_norm=v_ple_proj_norm, final_norm=v_final_norm)
    shapes = {n: w[n].shape for n in WEIGHTS}

    def view(a, n):
        return a[0].T if n in TRANSPOSED else a.reshape(-1, a.shape[-1])

    def unview(a, n):
        return (a.T if n in TRANSPOSED else a).reshape(shapes[n])

    w2, m2, v2 = ({n: view(a, n) for n, a in d.items()} for d in (w, m, v))

    c_idx = lax.axis_index("c").astype(jnp.int32).reshape(1)
    me_idx = (2 * lax.axis_index("x") + lax.axis_index("y")).astype(jnp.int32).reshape(1)
    place = jnp.concatenate([me_idx, c_idx])
    slot = {}
    for grp in SAME_SHAPE:
        slot.update(zip(grp, _to_slot("slot_" + grp[0], [w2[n] for n in grp], me_idx, BF16)))
    slot["conv_qk"] = _to_slot("slot_conv_qk", [w2["conv_qk"]], me_idx, F32)[0]
    wg1, wu1, wd1 = _gather4("gather_ffn1", [slot[n] for n in FFN1], [SPLIT[n] for n in FFN1])
    sp = {n: w2[n] for n in SMALL}
    loss_part, grad_x, grads, gs, conv_grad = _local_step(
        x[0], p[0, 0], loss_target[0], sp, wg1, wu1, wd1, [slot[n] for n in REST + ("conv_qk",)], c_idx, place)

    small = [gs[n].reshape(1, -1) for n in SMALL] + [conv_grad, loss_part]
    rows = [jnp.pad(a, ((0, 0), (0, PACK_W - a.shape[1]))) for a in small]
    packed = jnp.concatenate(rows, axis=0)
    packed = jnp.pad(packed, ((0, -packed.shape[0] % 8), (0, 0)))
    red = _allreduce_small(packed)
    loss = red[len(SMALL) + CONV_W, 0]
    for i, n in enumerate(SMALL):
        grads[n] = red[i:i + 1, :gs[n].size]
    dconv = red[len(SMALL):len(SMALL) + CONV_W, :conv_grad.shape[1]]
    cw = conv_qk.shape[-1]
    grads["conv_qk"] = lax.dynamic_slice_in_dim(dconv, (2 * lax.axis_index("x") + lax.axis_index("y")) * cw, cw, axis=1)

    outs = {}
    for grp in SAME_SHAPE + tuple((n,) for n in WEIGHTS if n not in BIG):
        g2s = [grads[n].reshape(w2[n].shape) for n in grp]
        res = _adamw("adamw_" + grp[0], [w2[n] for n in grp], g2s, [m2[n] for n in grp], [v2[n] for n in grp])
        for n, g2, (d, nm, nv) in zip(grp, g2s, res):
            outs[n] = tuple(unview(a, n) for a in (g2, d, nm, nv))
    return (loss, grad_x[None], *[outs[n][0] for n in WEIGHTS], *[outs[n][1] for n in WEIGHTS],
            *[outs[n][2] for n in WEIGHTS], *[outs[n][3] for n in WEIGHTS])
```

```python
import jax
import jax.numpy as jnp
from jax import lax
from jax.experimental import pallas as pl
from jax.experimental.pallas import tpu as pltpu

F32 = jnp.float32
BF16 = jnp.bfloat16
EPS = 1e-6
NH_M, DK_M, DV_M = 4, 64, 128
NH_F, DH_F = 8, 64
CONV_W = 4
ADAM_LR, ADAM_B1, ADAM_B2, ADAM_EPS, ADAM_WD, ADAM_STEP = 0.001, 0.9, 0.999, 1e-08, 0.01, 10
VMEM_LIMIT = 56 * 1024 * 1024


def _cparams(sem):
    return pltpu.CompilerParams(dimension_semantics=sem, vmem_limit_bytes=VMEM_LIMIT)


def _sigmoid(x):
    return 1.0 / (1.0 + jnp.exp(-x))


def _dot(a, b, ca, cb):
    return lax.dot_general(a.astype(BF16), b.astype(BF16), (((ca,), (cb,)), ((), ())), preferred_element_type=F32)


def _rowwise(name, fn, tiled, full, outs, accs=(), tm=512, plan=None):
    rows = tiled[0].shape[0]
    tm = min(tm, rows)
    assert rows % tm == 0
    n_t, n_f, n_o, n_a = len(tiled), len(full), len(outs), len(accs)
    nt = rows // tm
    host = _Hosted(plan, n_t + n_f, n_o + n_a)

    def body(*refs):
        in_refs, orefs, _, prefs = host.split(refs)
        host.run(0, pl.program_id(0) == 0, prefs)
        host.run(1, pl.program_id(0) == 0, prefs)
        ins = [r[...] for r in in_refs]
        res = fn(*ins)
        if not isinstance(res, (tuple, list)):
            res = (res,)
        for r, v in zip(orefs[:n_o], res[:n_o]):
            r[...] = v.astype(r.dtype)
        if n_a:
            @pl.when(pl.program_id(0) == 0)
            def _():
                for r in orefs[n_o:]:
                    r[...] = jnp.zeros_like(r)
            for r, v in zip(orefs[n_o:], res[n_o:]):
                r[...] += v.astype(r.dtype)
        host.run(2, pl.program_id(0) == nt - 1, prefs)

    in_specs = [pl.BlockSpec((tm, a.shape[1]), lambda i: (i, 0)) for a in tiled]
    in_specs += [pl.BlockSpec(a.shape, lambda i: (0, 0)) for a in full]
    out_specs = [pl.BlockSpec((tm, c), lambda i: (i, 0)) for c, _ in outs]
    out_specs += [pl.BlockSpec(s, lambda i: (0, 0)) for s, _ in accs]
    out_shape = [jax.ShapeDtypeStruct((rows, c), d) for c, d in outs]
    out_shape += [jax.ShapeDtypeStruct(s, d) for s, d in accs]
    hc = host.call_args()
    res = pl.pallas_call(
        body, name=name, grid=(nt,), in_specs=in_specs + hc["in_specs"], out_specs=out_specs + hc["out_specs"],
        out_shape=out_shape + hc["out_shape"], scratch_shapes=hc["scratch"], input_output_aliases=hc["aliases"],
        compiler_params=_cparams(("arbitrary",) if (n_a or plan is not None) else ("parallel",)),
    )(*tiled, *full, *hc["args"])
    return res if plan is None else (res[:n_o + n_a], res[n_o + n_a:])


def _colsum(v):
    return jnp.sum(v, axis=0, keepdims=True)


def _rms_fwd_val(x, g):
    r = lax.rsqrt(jnp.mean(x * x, axis=-1, keepdims=True) + EPS)
    return x * r * g


def _rms_bwd_val(dy, x, g):
    r = lax.rsqrt(jnp.mean(x * x, axis=-1, keepdims=True) + EPS)
    xh = x * r
    dxh = dy * g
    dx = r * (dxh - xh * jnp.mean(dxh * xh, axis=-1, keepdims=True))
    return dx, _colsum(dy * xh)


def _mm(name, pairs, out_shape, out_block, out_map, grid, kaxis, ta=False, tb=False, scale=None, res=None,
        out_dtype=F32, plan=None, twin=False):
    n_o = 2 if twin else 1
    nk = grid[kaxis]
    npairs = len(pairs)
    ca, cb = (0 if ta else 1), (1 if tb else 0)
    acc_shape = tuple(d for d in out_block if d is not None)
    n_in = 2 * npairs + (1 if res is not None else 0)
    host = _Hosted(plan, n_in, n_o)

    def body(*refs):
        ins, o_refs, (acc_ref,), prefs = host.split(refs)
        o_ref = o_refs[0]
        in_refs = ins[: 2 * npairs]
        res_ref = ins[2 * npairs] if res is not None else None
        k = pl.program_id(kaxis)
        ids = [pl.program_id(a) for a in range(len(grid))]
        first, last = ids[0] == 0, ids[0] == grid[0] - 1
        for a in range(1, len(grid)):
            first, last = first & (ids[a] == 0), last & (ids[a] == grid[a] - 1)
        host.run(0, first, prefs)
        host.run(1, first, prefs)

        @pl.when(k == 0)
        def _():
            acc_ref[...] = jnp.zeros_like(acc_ref)

        part = None
        for p in range(npairs):
            d = _dot(in_refs[2 * p][...], in_refs[2 * p + 1][...], ca, cb)
            part = d if part is None else part + d
        acc_ref[...] += part

        @pl.when(k == nk - 1)
        def _():
            v = acc_ref[...]
            if scale is not None:
                v = v * scale
            if res_ref is not None:
                v = v + res_ref[...].astype(F32)
            o_ref[...] = v.astype(o_ref.dtype)
            if twin:
                o_refs[1][...] = v.astype(BF16)

        host.run(2, last, prefs)

    in_specs, args = [], []
    for a, ab, am, b, bb, bm in pairs:
        in_specs += [pl.BlockSpec(ab, am), pl.BlockSpec(bb, bm)]
        args += [a, b]
    if res is not None:
        in_specs.append(pl.BlockSpec(out_block, out_map))
        args.append(res)
    sem = tuple("arbitrary" if (i == kaxis or plan is not None) else "parallel" for i in range(len(grid)))
    hc = host.call_args()
    out = pl.pallas_call(
        body, name=name, grid=grid, in_specs=in_specs + hc["in_specs"],
        out_specs=[pl.BlockSpec(out_block, out_map)] * n_o + hc["out_specs"],
        out_shape=[jax.ShapeDtypeStruct(out_shape, out_dtype)] + [jax.ShapeDtypeStruct(out_shape, BF16)] * (n_o - 1)
        + hc["out_shape"],
        scratch_shapes=[pltpu.VMEM(acc_shape, F32)] + hc["scratch"], input_output_aliases=hc["aliases"],
        compiler_params=_cparams(sem),
    )(*args, *hc["args"])
    res_out = tuple(out[:2]) if twin else out[0]
    return res_out if plan is None else (res_out, out[n_o:])


def _pick(n, pref):
    for t in pref:
        if n % t == 0:
            return t
    return n


def _mm_nn(name, a, b, tm=512, tn=512, tk=512, **kw):
    (m, k), n = a.shape, b.shape[1]
    tm, tn, tk = _pick(m, (tm, 256, 128)), _pick(n, (tn, 256, 128)), _pick(k, (tk, 256, 128))
    return _mm(name, [(a, (tm, tk), lambda i, j, kk: (i, kk), b, (tk, tn), lambda i, j, kk: (kk, j))],
               (m, n), (tm, tn), lambda i, j, kk: (i, j), (m // tm, n // tn, k // tk), 2, **kw)


def _mm_nt(name, a, b, tm=512, tn=512, tk=512, **kw):
    (m, k), n = a.shape, b.shape[0]
    tm, tn, tk = _pick(m, (tm, 256, 128)), _pick(n, (tn, 256, 128)), _pick(k, (tk, 256, 128))
    return _mm(name, [(a, (tm, tk), lambda i, j, kk: (i, kk), b, (tn, tk), lambda i, j, kk: (j, kk))],
               (m, n), (tm, tn), lambda i, j, kk: (i, j), (m // tm, n // tn, k // tk), 2, tb=True, **kw)


def _mm_tn(name, a, b, tm=512, tn=512, tk=4096, **kw):
    (k, m), n = a.shape, b.shape[1]
    tm, tn, tk = _pick(m, (tm, 256, 128)), _pick(n, (tn, 256, 128)), _pick(k, (tk, 2048, 1024, 512, 256, 128))
    return _mm(name, [(a, (tk, tm), lambda i, j, kk: (kk, i), b, (tk, tn), lambda i, j, kk: (kk, j))],
               (m, n), (tm, tn), lambda i, j, kk: (i, j), (m // tm, n // tn, k // tk), 2, ta=True, **kw)


def _norm_mm(name, h, gamma, w, w_transposed, out_dtype, w_side=None):
    t, d = h.shape
    n = w.shape[0] if w_transposed else w.shape[1]
    tm, tn = _pick(t, (1024, 512, 256)), _pick(n, (1024, 512, 256, 128))
    ns = 0 if w_side is None else w_side.shape[0]

    def body(*refs):
        h_ref, gam_ref, w_ref = refs[:3]
        xn_ref, o_ref = refs[3 + (ns > 0)], refs[4 + (ns > 0)]
        xn_scr = refs[-1]

        @pl.when(pl.program_id(1) == 0)
        def _():
            xn = _rms_fwd_val(h_ref[...], gam_ref[...]).astype(BF16)
            xn_scr[...] = xn
            xn_ref[...] = xn
            if ns:
                refs[5 + 1][...] = _dot(xn, refs[3][...], 1, 1)

        o_ref[...] = _dot(xn_scr[...], w_ref[...], 1, 1 if w_transposed else 0).astype(o_ref.dtype)

    wspec = pl.BlockSpec((tn, d), lambda i, j: (j, 0)) if w_transposed else pl.BlockSpec((d, tn), lambda i, j: (0, j))
    side_in = [pl.BlockSpec((ns, d), lambda i, j: (0, 0))] if ns else []
    side_out = [pl.BlockSpec((tm, ns), lambda i, j: (i, 0))] if ns else []
    return pl.pallas_call(
        body, name=name, grid=(t // tm, n // tn),
        in_specs=[pl.BlockSpec((tm, d), lambda i, j: (i, 0)), pl.BlockSpec((1, d), lambda i, j: (0, 0)), wspec] + side_in,
        out_specs=[pl.BlockSpec((tm, d), lambda i, j: (i, 0)), pl.BlockSpec((tm, tn), lambda i, j: (i, j))] + side_out,
        out_shape=[jax.ShapeDtypeStruct((t, d), BF16), jax.ShapeDtypeStruct((t, n), out_dtype)]
        + ([jax.ShapeDtypeStruct((t, ns), F32)] if ns else []),
        scratch_shapes=[pltpu.VMEM((tm, d), BF16)], compiler_params=_cparams(("parallel", "arbitrary")),
    )(h, gamma, w, *([w_side] if ns else []))


CHAIN_ROWS = 256


def _row_chains(tm):
    n = max(tm // CHAIN_ROWS, 1)
    return [slice(r * (tm // n), (r + 1) * (tm // n)) for r in range(n)]


def _ffn_fwd(pfx, h, gamma, wg, wu, wd, plan=None):
    t, d = h.shape
    nb, f, _ = wg.shape
    tm = _pick(t, (1024, 512, 256))
    nt = t // tm
    host = _Hosted(plan, 5, 4)

    def body(*refs):
        (h_ref, gam_ref, wg_ref, wu_ref, wd_ref), (ho_ref, xn_ref, g_ref, u_ref), (xn_scr, acc_ref), prefs = host.split(refs)
        i, j = pl.program_id(0), pl.program_id(1)
        host.run(0, (i == 0) & (j == 0), prefs)
        host.run(1, (i == nt // 2) & (j == 0), prefs)

        @pl.when(j == 0)
        def _():
            xn = _rms_fwd_val(h_ref[...], gam_ref[...]).astype(BF16)
            xn_scr[...] = xn
            xn_ref[...] = xn
            acc_ref[...] = jnp.zeros_like(acc_ref)

        for rows in _row_chains(tm):
            x = xn_scr[rows, :]
            g = _dot(x, wg_ref[...], 1, 1)
            u = _dot(x, wu_ref[...], 1, 1)
            g_ref[rows, :] = g.astype(BF16)
            u_ref[rows, :] = u.astype(BF16)
            acc_ref[rows, :] += _dot(g * _sigmoid(g) * u, wd_ref[...], 1, 0)

        @pl.when(j == nb - 1)
        def _():
            ho_ref[...] = h_ref[...] + 0.5 * acc_ref[...]

        host.run(2, (i == nt - 1) & (j == nb - 1), prefs)

    row = pl.BlockSpec((tm, d), lambda i, j: (i, 0))
    blk = pl.BlockSpec((None, tm, f), lambda i, j: (j, i, 0))
    wspec = pl.BlockSpec((None, f, d), lambda i, j: (j, 0, 0))
    hc = host.call_args()
    res = pl.pallas_call(
        body, name=pfx + "_fwd", grid=(nt, nb),
        in_specs=[row, pl.BlockSpec((1, d), lambda i, j: (0, 0)), wspec, wspec, wspec] + hc["in_specs"],
        out_specs=[row, row, blk, blk] + hc["out_specs"],
        out_shape=[jax.ShapeDtypeStruct((t, d), F32), jax.ShapeDtypeStruct((t, d), BF16),
                   jax.ShapeDtypeStruct((nb, t, f), BF16), jax.ShapeDtypeStruct((nb, t, f), BF16)] + hc["out_shape"],
        scratch_shapes=[pltpu.VMEM((tm, d), BF16), pltpu.VMEM((tm, d), F32)] + hc["scratch"],
        input_output_aliases=hc["aliases"], compiler_params=_cparams(("arbitrary", "arbitrary")),
    )(h, gamma, wg, wu, wd, *hc["args"])
    return res[:4], res[4:]


def _ffn_bwd(pfx, dh_out, h, gamma, xn, g_all, u_all, wg, wu, wd, plan=None):
    t, d = h.shape
    nb, f, _ = wg.shape
    tm = _pick(t, (512, 256))
    tk = _pick(t, (4096, 2048, 1024, 512, 256))

    nt = t // tm
    host = _Hosted(plan, 8, 6)

    def body(*refs):
        ((dy_ref, h_ref, gam_ref, wg_ref, wu_ref, wd_ref, g_ref, u_ref),
         (dh_ref, dgam_ref, dg_ref, du_ref, a_ref, dyb_ref), (acc_ref,), prefs) = host.split(refs)
        i, j = pl.program_id(0), pl.program_id(1)
        host.run(0, (i == 0) & (j == 0), prefs)
        host.run(1, (i == nt // 2) & (j == 0), prefs)

        @pl.when((i == 0) & (j == 0))
        def _():
            dgam_ref[...] = jnp.zeros_like(dgam_ref)

        @pl.when(j == 0)
        def _():
            acc_ref[...] = jnp.zeros_like(acc_ref)
            dyb_ref[...] = dy_ref[...].astype(BF16)

        for rows in _row_chains(tm):
            da = _dot(dy_ref[rows, :], wd_ref[...], 1, 1) * 0.5
            g = g_ref[rows, :].astype(F32)
            u = u_ref[rows, :].astype(F32)
            s = _sigmoid(g)
            sl = g * s
            du = (da * sl).astype(BF16)
            dg = (da * u * (s + sl * (1.0 - s))).astype(BF16)
            du_ref[rows, :] = du
            dg_ref[rows, :] = dg
            a_ref[rows, :] = (sl * u).astype(BF16)
            acc_ref[rows, :] += _dot(dg, wg_ref[...], 1, 0) + _dot(du, wu_ref[...], 1, 0)

        @pl.when(j == nb - 1)
        def _():
            dx, dgam = _rms_bwd_val(acc_ref[...], h_ref[...], gam_ref[...])
            dh_ref[...] = dy_ref[...] + dx
            dgam_ref[...] += dgam

        host.run(2, (i == nt - 1) & (j == nb - 1), prefs)

    row = pl.BlockSpec((tm, d), lambda i, j: (i, 0))
    vec = pl.BlockSpec((1, d), lambda i, j: (0, 0))
    blk = pl.BlockSpec((None, tm, f), lambda i, j: (j, i, 0))
    wspec = pl.BlockSpec((None, f, d), lambda i, j: (j, 0, 0))
    hc = host.call_args()
    res = pl.pallas_call(
        body, name=pfx + "_bwd", grid=(nt, nb),
        in_specs=[row, row, vec, wspec, wspec, wspec, blk, blk] + hc["in_specs"],
        out_specs=[row, vec, blk, blk, blk, row] + hc["out_specs"],
        out_shape=[jax.ShapeDtypeStruct((t, d), F32), jax.ShapeDtypeStruct((1, d), F32)]
        + [jax.ShapeDtypeStruct((nb, t, f), BF16)] * 3 + [jax.ShapeDtypeStruct((t, d), BF16)] + hc["out_shape"],
        scratch_shapes=[pltpu.VMEM((tm, d), F32)] + hc["scratch"], input_output_aliases=hc["aliases"],
        compiler_params=_cparams(("arbitrary", "arbitrary")),
    )(dh_out, h, gamma, wg, wu, wd, g_all, u_all, *hc["args"])
    dh, dgamma, dg_all, du_all, a_all, dyb = res[:6]

    xmap, bmap, omap = (lambda b, k: (k, 0)), (lambda b, k: (b, k, 0)), (lambda b, k: (b, 0, 0))
    dwg, tg = _mm(pfx + "_dwg", [(dg_all, (None, tk, f), bmap, xn, (tk, d), xmap)], (nb, f, d), (None, f, d), omap,
                  (nb, t // tk), 1, ta=True, twin=True)
    dwu, tu = _mm(pfx + "_dwu", [(du_all, (None, tk, f), bmap, xn, (tk, d), xmap)], (nb, f, d), (None, f, d), omap,
                  (nb, t // tk), 1, ta=True, twin=True)
    dwd, td = _mm(pfx + "_dwd", [(a_all, (None, tk, f), bmap, dyb, (tk, d), xmap)], (nb, f, d), (None, f, d), omap,
                  (nb, t // tk), 1, ta=True, scale=0.5, twin=True)
    return (dh, dgamma, dwg, dwu, dwd), res[6:], (tg, tu, td)


def _ffn_bwd_late_dx(pfx, dh_out, h, gamma, xn, g_all, u_all, wg, wu, wd, plan_gu, plans_dw, make_plan_dx):
    t, d = h.shape
    nb, f, _ = wg.shape
    tm = _pick(t, (1024, 512, 256))
    tk = _pick(t, (4096, 2048, 1024, 512, 256))
    nt = t // tm
    host_a = _Hosted(plan_gu, 4, 4)

    def body_a(*refs):
        (dy_ref, wd_ref, g_ref, u_ref), (dg_ref, du_ref, a_ref, dyb_ref), _, prefs = host_a.split(refs)
        i, j = pl.program_id(0), pl.program_id(1)
        host_a.run(0, (i == 0) & (j == 0), prefs)
        host_a.run(1, (i == 0) & (j == 0), prefs)

        @pl.when(j == 0)
        def _():
            dyb_ref[...] = dy_ref[...].astype(BF16)

        for rows in _row_chains(tm):
            da = _dot(dy_ref[rows, :], wd_ref[...], 1, 1) * 0.5
            g = g_ref[rows, :].astype(F32)
            u = u_ref[rows, :].astype(F32)
            s = _sigmoid(g)
            sl = g * s
            du_ref[rows, :] = (da * sl).astype(BF16)
            dg_ref[rows, :] = (da * u * (s + sl * (1.0 - s))).astype(BF16)
            a_ref[rows, :] = (sl * u).astype(BF16)
        host_a.run(2, (i == nt - 1) & (j == nb - 1), prefs)

    row = pl.BlockSpec((tm, d), lambda i, j: (i, 0))
    vec = pl.BlockSpec((1, d), lambda i, j: (0, 0))
    blk = pl.BlockSpec((None, tm, f), lambda i, j: (j, i, 0))
    wspec = pl.BlockSpec((None, f, d), lambda i, j: (j, 0, 0))
    hc = host_a.call_args()
    res_a = pl.pallas_call(
        body_a, name=pfx + "_bwd_gu", grid=(nt, nb), in_specs=[row, wspec, blk, blk] + hc["in_specs"],
        out_specs=[blk] * 3 + [row] + hc["out_specs"],
        out_shape=[jax.ShapeDtypeStruct((nb, t, f), BF16)] * 3 + [jax.ShapeDtypeStruct((t, d), BF16)] + hc["out_shape"],
        scratch_shapes=hc["scratch"], input_output_aliases=hc["aliases"], compiler_params=_cparams(("arbitrary", "arbitrary")),
    )(dh_out, wd, g_all, u_all, *hc["args"])
    dg_all, du_all, a_all, dyb = res_a[:4]

    xmap, bmap, omap = (lambda b, k: (k, 0)), (lambda b, k: (b, k, 0)), (lambda b, k: (b, 0, 0))
    def dw(name, a, b, plan, scale=None):
        r = _mm(pfx + name, [(a, (None, tk, f), bmap, b, (tk, d), xmap)], (nb, f, d), (None, f, d), omap, (nb, t // tk), 1,
                ta=True, scale=scale, plan=plan, twin=True)
        return r if plan is not None else (r, ())

    (dwd, td), out_d = dw("_dwd", a_all, dyb, plans_dw[0], 0.5)
    (dwg, tg), out_g = dw("_dwg", dg_all, xn, plans_dw[1])
    (dwu, tu), out_u = dw("_dwu", du_all, xn, plans_dw[2])

    plan_dx = make_plan_dx((dwg, dwu, dwd), (tg, tu, td))
    host_b = _Hosted(plan_dx, 7, 2)

    def body_b(*refs):
        (dy_ref, h_ref, gam_ref, wg_ref, wu_ref, dg_ref, du_ref), (dh_ref, dgam_ref), (acc_ref,), prefs = host_b.split(refs)
        i, j = pl.program_id(0), pl.program_id(1)
        host_b.run(0, (i == 0) & (j == 0), prefs)
        host_b.run(1, (i == 0) & (j == 0), prefs)

        @pl.when((i == 0) & (j == 0))
        def _():
            dgam_ref[...] = jnp.zeros_like(dgam_ref)

        @pl.when(j == 0)
        def _():
            acc_ref[...] = jnp.zeros_like(acc_ref)

        acc_ref[...] += _dot(dg_ref[...], wg_ref[...], 1, 0) + _dot(du_ref[...], wu_ref[...], 1, 0)

        @pl.when(j == nb - 1)
        def _():
            dx, dgam = _rms_bwd_val(acc_ref[...], h_ref[...], gam_ref[...])
            dh_ref[...] = dy_ref[...] + dx
            dgam_ref[...] += dgam

        host_b.run(2, (i == nt - 1) & (j == nb - 1), prefs)

    hc = host_b.call_args()
    res_b = pl.pallas_call(
        body_b, name=pfx + "_bwd_dx", grid=(nt, nb), in_specs=[row, row, vec, wspec, wspec, blk, blk] + hc["in_specs"],
        out_specs=[row, vec] + hc["out_specs"],
        out_shape=[jax.ShapeDtypeStruct((t, d), F32), jax.ShapeDtypeStruct((1, d), F32)] + hc["out_shape"],
        scratch_shapes=[pltpu.VMEM((tm, d), F32)] + hc["scratch"], input_output_aliases=hc["aliases"],
        compiler_params=_cparams(("arbitrary", "arbitrary")),
    )(dh_out, h, gamma, wg, wu, dg_all, du_all, *hc["args"])
    return (res_b[0], res_b[1], dwg, dwu, dwd), (res_a[4:], out_d, out_g, out_u, res_b[2:])


HALO = 16


def _silu_grad(y):
    s = _sigmoid(y)
    return s * (1.0 + y * (1.0 - s))


def _with_halo(ref, i, n_tiles, tm, before, after):
    t = ref.shape[0]
    r0 = pl.multiple_of(i * tm, tm)
    parts = [ref[pl.ds(r0, tm), :].astype(F32)]
    if before:
        prev = ref[pl.ds(pl.multiple_of(jnp.maximum(r0 - HALO, 0), HALO), HALO), :].astype(F32)
        parts.insert(0, jnp.where(i > 0, prev, 0.0))
    if after:
        nxt = ref[pl.ds(pl.multiple_of(jnp.minimum(r0 + tm, t - HALO), HALO), HALO), :].astype(F32)
        parts.append(jnp.where(i < n_tiles - 1, nxt, 0.0))
    return jnp.concatenate(parts, axis=0)


def _conv_fwd(zbig, w):
    t, c = zbig.shape[0], w.shape[1]
    tm = _pick(t, (512, 256))
    nt = t // tm

    def body(x_ref, w_ref, o_ref):
        xe = _with_halo(x_ref, pl.program_id(0), nt, tm, True, False)
        wv = w_ref[...]
        y = xe * wv[3:4, :]
        for i in range(CONV_W - 1):
            y = y + pltpu.roll(xe, CONV_W - 1 - i, 0) * wv[i:i + 1, :]
        y = y[HALO:, :]
        o_ref[...] = (y * _sigmoid(y)).astype(o_ref.dtype)

    return pl.pallas_call(
        body, name="conv_fwd", grid=(nt,),
        in_specs=[pl.BlockSpec((t, c), lambda i: (0, 0)), pl.BlockSpec(w.shape, lambda i: (0, 0))],
        out_specs=pl.BlockSpec((tm, c), lambda i: (i, 0)), out_shape=jax.ShapeDtypeStruct((t, c), BF16),
        compiler_params=_cparams(("parallel",)),
    )(zbig, w)


def _conv_bwd(zbig, dact, w):
    t, c = dact.shape
    tm = _pick(t, (512, 256))
    nt = t // tm
    n = tm + HALO

    def body(x_ref, d_ref, w_ref, dx_ref, dw_ref):
        xe = _with_halo(x_ref, pl.program_id(0), nt, tm, True, True)
        de = _with_halo(d_ref, pl.program_id(0), nt, tm, False, True)
        wv = w_ref[...]
        sh = [pltpu.roll(xe, CONV_W - 1 - i, 0)[HALO:, :] if i < CONV_W - 1 else xe[HALO:, :] for i in range(CONV_W)]
        y = sh[0] * wv[0:1, :]
        for i in range(1, CONV_W):
            y = y + sh[i] * wv[i:i + 1, :]
        dy = de * _silu_grad(y)
        dx = dy * wv[3:4, :]
        for i in range(CONV_W - 1):
            dx = dx + pltpu.roll(dy, n - (CONV_W - 1 - i), 0) * wv[i:i + 1, :]
        dx_ref[...] = dx[:tm, :].astype(dx_ref.dtype)
        dyc = dy[:tm, :]
        dwp = jnp.concatenate([_colsum(dyc * sh[i][:tm, :]) for i in range(CONV_W)], axis=0)

        @pl.when(pl.program_id(0) == 0)
        def _():
            dw_ref[...] = jnp.zeros_like(dw_ref)
        dw_ref[...] += dwp

    return pl.pallas_call(
        body, name="conv_bwd", grid=(nt,),
        in_specs=[pl.BlockSpec((t, c), lambda i: (0, 0)), pl.BlockSpec((t, c), lambda i: (0, 0)),
                  pl.BlockSpec(w.shape, lambda i: (0, 0))],
        out_specs=[pl.BlockSpec((tm, c), lambda i: (i, 0)), pl.BlockSpec(w.shape, lambda i: (0, 0))],
        out_shape=[jax.ShapeDtypeStruct((t, c), BF16), jax.ShapeDtypeStruct(w.shape, F32)],
        compiler_params=_cparams(("arbitrary",)),
    )(zbig, dact, w)


LM = 256
HI = lax.Precision.HIGHEST


def _logsig(x):
    return jnp.minimum(x, 0.0) - jnp.log(1.0 + jnp.exp(-jnp.abs(x)))


def _tri(n, lower):
    r = lax.broadcasted_iota(jnp.int32, (n, n), 0)
    c = lax.broadcasted_iota(jnp.int32, (n, n), 1)
    return (r >= c) if lower else (r <= c)


def _f32dot(a, b):
    return lax.dot_general(a, b, (((1,), (0,)), ((), ())), precision=HI, preferred_element_type=F32)


def _tri_dot(a, b, a_is_tri):
    tri = (a if a_is_tri else b).astype(BF16)
    parts = _split3(b if a_is_tri else a)
    outs = [_dot(tri, p, 1, 0) if a_is_tri else _dot(p, tri, 1, 0) for p in parts]
    return (outs[0] + outs[1]) + outs[2]


def _mlstm_decays(zs_ref, zsr_ref, bc_ref, br_ref):
    l = LM
    lf_c = _logsig(zs_ref[:, 0:2 * NH_M] + bc_ref[...])
    lf_r = _logsig(zsr_ref[...] + br_ref[...])
    low, up = _tri(l, True), _tri(l, False)
    return _tri_dot(low, lf_c, True), _tri_dot(lf_r, up, False), low, up


def _mlstm_chunk(h, q_ref, k_ref, v_ref, zs_ref, zsr_ref, bc_ref, br_ref, c_prev, m_prev, decays):
    l = LM
    q = q_ref[:, h * DK_M:(h + 1) * DK_M].astype(F32) * (DK_M ** -0.5)
    k = k_ref[:, h * DK_M:(h + 1) * DK_M]
    v = v_ref[:, h * DV_M:(h + 1) * DV_M]
    lane = lax.broadcasted_iota(jnp.int32, (l, DV_M), 1)
    v1 = jnp.concatenate([v, (lane == 0).astype(v.dtype)], axis=1)
    zs, zsr = zs_ref[...], zsr_ref[...]
    li_c = zs[:, h:h + 1] + bc_ref[:, h:h + 1]
    fp_c = zs[:, NH_M + h:NH_M + h + 1] + bc_ref[:, NH_M + h:NH_M + h + 1]
    li_r = zsr[h:h + 1, :] + br_ref[h:h + 1, :]
    fp_r = zsr[NH_M + h:NH_M + h + 1, :] + br_ref[NH_M + h:NH_M + h + 1, :]
    low = decays[2]
    b_c = decays[0][:, NH_M + h:NH_M + h + 1]
    b_r = decays[1][NH_M + h:NH_M + h + 1, :]
    g = b_r[:, l - 1:l]
    dmat = jnp.where(low, b_c - b_r + li_r, -jnp.inf)
    inter = b_c + m_prev
    m_t = jnp.maximum(inter, jnp.max(dmat, axis=1, keepdims=True))
    w_inter = jnp.exp(inter - m_t)
    amat = jnp.exp(dmat - m_t)
    s = _dot(q, k, 1, 1)
    p = amat * s
    qc = _dot(q, c_prev, 1, 0)
    qc_w = w_inter * qc
    num1 = qc_w + _dot(p, v1, 1, 0)
    den = num1[:, DV_M:DV_M + 1]
    mx = jnp.maximum(jnp.abs(den), jnp.exp(-m_t))
    hh = num1[:, :DV_M] / mx
    a_c = g - b_c + li_c
    return dict(q=q, k=k, v1=v1, fp_c=fp_c, fp_r=fp_r, b_c=b_c, g=g, m_t=m_t, w_inter=w_inter, amat=amat, s=s, p=p,
                qc_w=qc_w, den=den, mx=mx, hh=hh, a_c=a_c)


def _mlstm_fwd(qk, zbig, zs, zsr, bc, br, gm):
    t = zs.shape[0]
    l = LM
    nc = t // l
    dm = NH_M * DV_M

    def body(q_ref, k_ref, v_ref, o_ref, zs_ref, zsr_ref, bc_ref, br_ref, gm_ref, y_ref, cst_ref, mst_ref, c_scr, m_scr):
        @pl.when(pl.program_id(0) == 0)
        def _():
            c_scr[...] = jnp.zeros_like(c_scr)
            m_scr[...] = jnp.zeros_like(m_scr)

        cst_ref[...] = c_scr[...]
        mst_ref[...] = m_scr[...]
        ys = []
        decays = _mlstm_decays(zs_ref, zsr_ref, bc_ref, br_ref)
        for h in range(NH_M):
            c_prev = c_scr[h]
            m_prev = m_scr[h:h + 1, 0:1]
            r = _mlstm_chunk(h, q_ref, k_ref, v_ref, zs_ref, zsr_ref, bc_ref, br_ref, c_prev, m_prev, decays)
            hh = r["hh"]
            gh = gm_ref[:, h * DV_M:(h + 1) * DV_M]
            hn = hh * lax.rsqrt(jnp.mean(hh * hh, axis=-1, keepdims=True) + EPS) * gh
            og = o_ref[:, h * DV_M:(h + 1) * DV_M].astype(F32)
            ys.append(hn * _sigmoid(og))
            m_new = jnp.maximum(r["g"] + m_prev, jnp.max(r["a_c"], axis=0, keepdims=True))
            decay = jnp.exp(r["g"] + m_prev - m_new)
            wk = r["k"].astype(F32) * jnp.exp(r["a_c"] - m_new)
            c_scr[h] = decay * c_prev + _dot(wk, r["v1"], 0, 0)
            m_scr[h:h + 1, :] = jnp.broadcast_to(m_new, (1, 128))
        y_ref[...] = jnp.concatenate(ys, axis=1).astype(y_ref.dtype)

    return pl.pallas_call(
        body, name="mlstm_fwd", grid=(nc,),
        in_specs=[pl.BlockSpec((l, NH_M * DK_M), lambda i: (i, 0)), pl.BlockSpec((l, NH_M * DK_M), lambda i: (i, 1)),
                  pl.BlockSpec((l, dm), lambda i: (i, 1)), pl.BlockSpec((l, dm), lambda i: (i, 2)),
                  pl.BlockSpec((l, 128), lambda i: (i, 0)), pl.BlockSpec((8, l), lambda i: (0, i)),
                  pl.BlockSpec((1, 8), lambda i: (0, 0)), pl.BlockSpec((8, 1), lambda i: (0, 0)),
                  pl.BlockSpec((1, dm), lambda i: (0, 0))],
        out_specs=[pl.BlockSpec((l, dm), lambda i: (i, 0)), pl.BlockSpec((None, NH_M, DK_M, 2 * DV_M), lambda i: (i, 0, 0, 0)),
                   pl.BlockSpec((None, 8, 128), lambda i: (i, 0, 0))],
        out_shape=[jax.ShapeDtypeStruct((t, dm), BF16), jax.ShapeDtypeStruct((nc, NH_M, DK_M, 2 * DV_M), F32),
                   jax.ShapeDtypeStruct((nc, 8, 128), F32)],
        scratch_shapes=[pltpu.VMEM((NH_M, DK_M, 2 * DV_M), F32), pltpu.VMEM((8, 128), F32)],
        compiler_params=_cparams(("arbitrary",)),
    )(qk, qk, zbig, zbig, zs, zsr, bc, br, gm)


def _mlstm_bwd(qk, zbig, zs, zsr, bc, br, gm, cst, mst, dycat):
    t = zs.shape[0]
    l = LM
    nc = t // l
    dm = NH_M * DV_M

    def body(q_ref, k_ref, v_ref, o_ref, zs_ref, zsr_ref, bc_ref, br_ref, gm_ref, cst_ref, mst_ref, cnx_ref, mnx_ref,
             dy_ref, dqk_ref, dv_ref, do_ref, dzs_ref, dzr_ref, dgm_ref, dc_scr):
        @pl.when(pl.program_id(0) == 0)
        def _():
            dc_scr[...] = jnp.zeros_like(dc_scr)
            dgm_ref[...] = jnp.zeros_like(dgm_ref)

        lane = lax.broadcasted_iota(jnp.int32, (l, 128), 1)
        db_all, sig_c, carries = jnp.zeros((l, 128), F32), jnp.zeros((l, 128), F32), jnp.zeros((1, 128), F32)
        decays = _mlstm_decays(zs_ref, zsr_ref, bc_ref, br_ref)
        lower, upper = decays[2], decays[3]
        dzr_rows = [None] * 8
        dvs, dos, dgs, dqs, dks = [], [], [], [], []
        dzs = jnp.zeros((l, 128), F32)
        for h in range(NH_M):
            c_prev = cst_ref[h]
            m_prev = mst_ref[h:h + 1, 0:1]
            r = _mlstm_chunk(h, q_ref, k_ref, v_ref, zs_ref, zsr_ref, bc_ref, br_ref, c_prev, m_prev, decays)
            hh, mx, den, m_t, v1, amat = r["hh"], r["mx"], r["den"], r["m_t"], r["v1"], r["amat"]
            gh = gm_ref[:, h * DV_M:(h + 1) * DV_M]
            rs = lax.rsqrt(jnp.mean(hh * hh, axis=-1, keepdims=True) + EPS)
            xh = hh * rs
            sg = _sigmoid(o_ref[:, h * DV_M:(h + 1) * DV_M].astype(F32))
            dyh = dy_ref[:, h * DV_M:(h + 1) * DV_M]
            dos.append(dyh * xh * gh * sg * (1.0 - sg))
            dhn = dyh * sg
            dgs.append(_colsum(dhn * xh))
            dxh = dhn * gh
            dh = rs * (dxh - xh * jnp.mean(dxh * xh, axis=-1, keepdims=True))
            g1 = dh / mx
            hd = jnp.sum(hh * dh, axis=-1, keepdims=True)
            dden = jnp.where(jnp.abs(den) > jnp.exp(-m_t), -hd / mx * jnp.sign(den), 0.0)
            g256 = jnp.concatenate([g1, jnp.where(lane == 0, dden, 0.0)], axis=1)
            dc_h = dc_scr[h]
            ea = jnp.exp(r["a_c"])
            dp = _dot(g256, v1, 1, 1)
            ds = dp * amat
            dqs.append((r["w_inter"] * _dot(g256, c_prev, 1, 1) + _dot(ds, r["k"], 1, 0)) * (DK_M ** -0.5))
            dks.append(_dot(ds, r["q"], 0, 0) + ea * _dot(v1, dc_h, 1, 1))
            dv_st = ea * _dot(r["k"], dc_h, 1, 0)
            dv1 = _dot(r["p"], g256, 0, 0) + dv_st
            dvs.append(dv1[:, :DV_M])
            wmat = dp * r["p"]
            c_in = _colsum(wmat)
            c_st = jnp.sum(v1.astype(F32) * dv_st, axis=-1, keepdims=True)
            r_t = jnp.sum(wmat, axis=1, keepdims=True) + jnp.sum(g256 * r["qc_w"], axis=-1, keepdims=True)
            db = r_t - c_st
            carry = jnp.exp(mnx_ref[h:h + 1, 0:1]) * jnp.sum(
                jnp.sum(dc_h * cnx_ref[h], axis=1, keepdims=True), axis=0, keepdims=True)
            db_all = db_all + jnp.where(lane == NH_M + h, db, 0.0)
            sig_c = sig_c + jnp.where(lane == NH_M + h, _sigmoid(-r["fp_c"]), 0.0)
            carries = carries + jnp.where(lane[0:1, :] == NH_M + h, carry, 0.0)
            dzs = dzs + jnp.where(lane == h, c_st, 0.0)
            dzr_rows[h] = c_in
            dzr_rows[NH_M + h] = _sigmoid(-r["fp_r"])
            wq = r["q"] * jnp.exp(r["b_c"] - m_t)
            dc_scr[h] = jnp.exp(r["g"]) * dc_h + _dot(wq, g256, 0, 0)
        dzs = dzs + (_tri_dot(upper, db_all, True) + carries) * sig_c
        c_in4 = jnp.concatenate(dzr_rows[:NH_M], axis=0)
        dlf_r4 = -_tri_dot(c_in4, lower, False)
        dzr_rows = dzr_rows[:NH_M] + [dlf_r4[h:h + 1, :] * dzr_rows[NH_M + h] for h in range(NH_M)]
        dqk_ref[...] = jnp.concatenate(dqs + dks, axis=1)
        dv_ref[...] = jnp.concatenate(dvs, axis=1).astype(dv_ref.dtype)
        do_ref[...] = jnp.concatenate(dos, axis=1).astype(do_ref.dtype)
        dzs_ref[...] = dzs
        dzr_ref[...] = jnp.concatenate(dzr_rows, axis=0)
        dgm_ref[...] += jnp.concatenate(dgs, axis=1)

    rev = lambda i: nc - 1 - i
    nxt = lambda i: jnp.minimum(nc - i, nc - 1)
    return pl.pallas_call(
        body, name="mlstm_bwd", grid=(nc,),
        in_specs=[pl.BlockSpec((l, NH_M * DK_M), lambda i: (rev(i), 0)), pl.BlockSpec((l, NH_M * DK_M), lambda i: (rev(i), 1)),
                  pl.BlockSpec((l, dm), lambda i: (rev(i), 1)), pl.BlockSpec((l, dm), lambda i: (rev(i), 2)),
                  pl.BlockSpec((l, 128), lambda i: (rev(i), 0)), pl.BlockSpec((8, l), lambda i: (0, rev(i))),
                  pl.BlockSpec((1, 8), lambda i: (0, 0)), pl.BlockSpec((8, 1), lambda i: (0, 0)),
                  pl.BlockSpec((1, dm), lambda i: (0, 0)),
                  pl.BlockSpec((None, NH_M, DK_M, 2 * DV_M), lambda i: (rev(i), 0, 0, 0)),
                  pl.BlockSpec((None, 8, 128), lambda i: (rev(i), 0, 0)),
                  pl.BlockSpec((None, NH_M, DK_M, 2 * DV_M), lambda i: (nxt(i), 0, 0, 0)),
                  pl.BlockSpec((None, 8, 128), lambda i: (nxt(i), 0, 0)),
                  pl.BlockSpec((l, dm), lambda i: (rev(i), 0))],
        out_specs=[pl.BlockSpec((l, dm), lambda i: (rev(i), 0)),
                   pl.BlockSpec((l, dm), lambda i: (rev(i), 0)), pl.BlockSpec((l, dm), lambda i: (rev(i), 0)),
                   pl.BlockSpec((l, 128), lambda i: (rev(i), 0)), pl.BlockSpec((8, l), lambda i: (0, rev(i))),
                   pl.BlockSpec((1, dm), lambda i: (0, 0))],
        out_shape=[jax.ShapeDtypeStruct((t, dm), F32),
                   jax.ShapeDtypeStruct((t, dm), BF16), jax.ShapeDtypeStruct((t, dm), BF16),
                   jax.ShapeDtypeStruct((t, 128), F32), jax.ShapeDtypeStruct((8, t), F32),
                   jax.ShapeDtypeStruct((1, dm), F32)],
        scratch_shapes=[pltpu.VMEM((NH_M, DK_M, 2 * DV_M), F32)],
        compiler_params=_cparams(("arbitrary",)),
    )(qk, qk, zbig, zbig, zs, zsr, bc, br, gm, cst, mst, cst, mst, dycat)


def _fox_cumsum(zsr, bf_r):
    t = zsr.shape[1]
    cw = _pick(t, (512, 256))

    def body(z_ref, b_ref, c_ref):
        up = _tri(cw, False).astype(F32)
        carry = jnp.zeros((NH_F, 1), F32)
        for j in range(t // cw):
            cs = _f32dot(_logsig(z_ref[:, j * cw:(j + 1) * cw] + b_ref[...]), up) + carry
            c_ref[:, j * cw:(j + 1) * cw] = cs
            carry = cs[:, cw - 1:cw]

    return pl.pallas_call(
        body, name="fox_cumsum", grid=(1,),
        in_specs=[pl.BlockSpec((NH_F, t), lambda i: (1, 0)), pl.BlockSpec((NH_F, 1), lambda i: (0, 0))],
        out_specs=pl.BlockSpec((NH_F, t), lambda i: (0, 0)), out_shape=jax.ShapeDtypeStruct((NH_F, t), F32),
        compiler_params=_cparams(("arbitrary",)),
    )(zsr, bf_r)


def _fox_gate_bwd(zsr, bf_r, dc):
    t = zsr.shape[1]
    cw = _pick(t, (512, 256))

    def body(z_ref, b_ref, dc_ref, o_ref):
        low = _tri(cw, True).astype(F32)
        carry = jnp.zeros((NH_F, 1), F32)
        for j in reversed(range(t // cw)):
            sl = slice(j * cw, (j + 1) * cw)
            dlf = _f32dot(dc_ref[:, sl], low) + carry
            o_ref[:, sl] = dlf * _sigmoid(-(z_ref[:, sl] + b_ref[...]))
            carry = dlf[:, 0:1]

    return pl.pallas_call(
        body, name="fox_gate_bwd", grid=(1,),
        in_specs=[pl.BlockSpec((NH_F, t), lambda i: (1, 0)), pl.BlockSpec((NH_F, 1), lambda i: (0, 0)),
                  pl.BlockSpec((NH_F, t), lambda i: (0, 0))],
        out_specs=pl.BlockSpec((NH_F, t), lambda i: (0, 0)), out_shape=jax.ShapeDtypeStruct((NH_F, t), F32),
        compiler_params=_cparams(("arbitrary",)),
    )(zsr, bf_r, dc)


def _causal_mask(n):
    return _tri(n, True)


AUG = 64


def _split3(c):
    hi = c.astype(BF16).astype(F32)
    r1 = c - hi
    mid = r1.astype(BF16).astype(F32)
    return hi, mid, r1 - mid


def _fox_prep(zbig, ct):
    t = zbig.shape[0]
    tm = _pick(t, (1024, 512, 256))

    def body(q_ref, k_ref, v_ref, c_ref, qo_ref, ko_ref, vo_ref):
        lane = lax.broadcasted_iota(jnp.int32, (tm, AUG), 1)
        qv, kv, vv, cv = q_ref[...], k_ref[...], v_ref[...], c_ref[...]
        one = (lane == 0).astype(BF16)
        for h in range(NH_F):
            hi, mid, lo = _split3(cv[:, h:h + 1])
            aq = jnp.where(lane == 0, hi, jnp.where(lane == 1, mid, jnp.where(lane == 2, lo, jnp.where(lane < 6, 1.0, 0.0))))
            ak = jnp.where(lane < 3, 1.0, jnp.where(lane == 3, -hi, jnp.where(lane == 4, -mid, jnp.where(lane == 5, -lo, 0.0))))
            sl = slice(h * DH_F, (h + 1) * DH_F)
            qo_ref[h] = jnp.concatenate([qv[:, sl] * (DH_F ** -0.5), aq.astype(BF16)], axis=1).astype(BF16)
            ko_ref[h] = jnp.concatenate([kv[:, sl], ak.astype(BF16)], axis=1)
            vo_ref[h] = jnp.concatenate([vv[:, sl], one], axis=1)

    ospec = pl.BlockSpec((NH_F, tm, 128), lambda i: (0, i, 0))
    return pl.pallas_call(
        body, name="fox_prep", grid=(t // tm,),
        in_specs=[pl.BlockSpec((tm, 512), lambda i: (i, 3)), pl.BlockSpec((tm, 512), lambda i: (i, 4)),
                  pl.BlockSpec((tm, 512), lambda i: (i, 5)), pl.BlockSpec((tm, NH_F), lambda i: (i, 0))],
        out_specs=[ospec] * 3, out_shape=[jax.ShapeDtypeStruct((NH_F, t, 128), BF16)] * 3,
        compiler_params=_cparams(("parallel",)),
    )(zbig, zbig, zbig, ct)


def _fox_fwd2(qa, ka, va, gf, plan=None):
    nh, t, _ = qa.shape
    tq = _pick(t, (512, 256))
    nq = t // tq
    group = 4
    host = _Hosted(plan, 4, 3)

    def body(*refs):
        (q_ref, k_ref, v_ref, g_ref), (y_ref, o_ref, lse_ref), _, prefs = host.split(refs)
        i = pl.program_id(0)
        host.run(0, i == 0, prefs)
        host.run(1, i == max(nq - 2, 0), prefs)
        lane = lax.broadcasted_iota(jnp.int32, (tq, 128), 1)
        causal = _causal_mask(tq)
        ys, os_ = [], []
        lse_all = jnp.zeros((tq, 128), F32)
        for h0 in range(0, nh, group):
            heads = range(h0, h0 + group)
            qvs = [q_ref[h] for h in heads]

            def blk(j, carry, masked, heads=heads, qvs=qvs):
                k0 = pl.multiple_of(j * tq, tq)
                out = []
                for (m, acc), h, qv in zip(carry, heads, qvs):
                    s = lax.dot_general(qv, k_ref[h, pl.ds(k0, tq), :], (((1,), (1,)), ((), ())), preferred_element_type=F32)
                    if masked:
                        s = jnp.where(causal, s, -jnp.inf)
                    m_new = jnp.maximum(m, jnp.max(s, axis=1, keepdims=True))
                    p = jnp.exp(s - m_new).astype(BF16)
                    pv = lax.dot_general(p, v_ref[h, pl.ds(k0, tq), :], (((1,), (0,)), ((), ())), preferred_element_type=F32)
                    out.append((m_new, jnp.exp(m - m_new) * acc + pv))
                return tuple(out)

            init = tuple((jnp.full((tq, 1), -jnp.inf, F32), jnp.zeros((tq, 128), F32)) for _ in heads)
            carry = lax.fori_loop(0, i, lambda j, c: blk(j, c, False), init)
            for (m, acc), h in zip(blk(i, carry, True), heads):
                l = acc[:, DH_F:DH_F + 1]
                o = acc[:, :DH_F] / l
                os_.append(o)
                gh = g_ref[:, h * DH_F:(h + 1) * DH_F]
                ys.append(o * lax.rsqrt(jnp.mean(o * o, axis=-1, keepdims=True) + EPS) * gh)
                lse_all = lse_all + jnp.where(lane == h, m + jnp.log(l), 0.0)
        y_ref[...] = jnp.concatenate(ys, axis=1).astype(y_ref.dtype)
        o_ref[...] = jnp.concatenate(os_, axis=1)
        lse_ref[...] = lse_all
        host.run(2, i == nq - 1, prefs)

    full = pl.BlockSpec((nh, t, 128), lambda i: (0, 0, 0))
    hc = host.call_args()
    res = pl.pallas_call(
        body, name="fox_fwd", grid=(nq,),
        in_specs=[pl.BlockSpec((nh, tq, 128), lambda i: (0, i, 0)), full, full, pl.BlockSpec((1, nh * DH_F), lambda i: (0, 0))]
        + hc["in_specs"],
        out_specs=[pl.BlockSpec((tq, nh * DH_F), lambda i: (i, 0)), pl.BlockSpec((tq, nh * DH_F), lambda i: (i, 0)),
                   pl.BlockSpec((tq, 128), lambda i: (i, 0))] + hc["out_specs"],
        out_shape=[jax.ShapeDtypeStruct((t, nh * DH_F), BF16), jax.ShapeDtypeStruct((t, nh * DH_F), F32),
                   jax.ShapeDtypeStruct((t, 128), F32)] + hc["out_shape"],
        scratch_shapes=hc["scratch"], input_output_aliases=hc["aliases"], compiler_params=_cparams(("arbitrary",)),
    )(qa, ka, va, gf, *hc["args"])
    return res[:3], res[3:]


def _fox_bwd_prep(dycat, o, gf):
    t = o.shape[0]
    tm = _pick(t, (1024, 512, 256))

    def body(dy_ref, o_ref, g_ref, do_ref, dl_ref, dg_ref):
        lane = lax.broadcasted_iota(jnp.int32, (tm, 128), 1)
        dyv, ov, gv = dy_ref[...], o_ref[...], g_ref[...]
        dgs = []
        dl = jnp.zeros((tm, 128), F32)
        pad = jnp.zeros((tm, AUG), BF16)
        for h in range(NH_F):
            sl = slice(h * DH_F, (h + 1) * DH_F)
            dx, dg = _rms_bwd_val(dyv[:, sl], ov[:, sl], gv[:, sl])
            dgs.append(dg)
            do_ref[h] = jnp.concatenate([dx.astype(BF16), pad], axis=1)
            dl = dl + jnp.where(lane == h, jnp.sum(dx * ov[:, sl], axis=-1, keepdims=True), 0.0)
        dl_ref[...] = dl

        @pl.when(pl.program_id(0) == 0)
        def _():
            dg_ref[...] = jnp.zeros_like(dg_ref)
        dg_ref[...] += jnp.concatenate(dgs, axis=1)

    return pl.pallas_call(
        body, name="fox_bwd_prep", grid=(t // tm,),
        in_specs=[pl.BlockSpec((tm, 512), lambda i: (i, 1)), pl.BlockSpec((tm, 512), lambda i: (i, 0)),
                  pl.BlockSpec((1, 512), lambda i: (0, 0))],
        out_specs=[pl.BlockSpec((NH_F, tm, 128), lambda i: (0, i, 0)), pl.BlockSpec((tm, 128), lambda i: (i, 0)),
                   pl.BlockSpec((1, 512), lambda i: (0, 0))],
        out_shape=[jax.ShapeDtypeStruct((NH_F, t, 128), BF16), jax.ShapeDtypeStruct((t, 128), F32),
                   jax.ShapeDtypeStruct((1, 512), F32)],
        compiler_params=_cparams(("arbitrary",)),
    )(dycat, o, gf)


def _fox_bwd2(qa, ka, va, doa, lse, delta, plan=None):
    nh, t, _ = qa.shape
    tq = _pick(t, (512, 256))
    nq = t // tq

    group = 2

    def tdot(a, b, cb):
        return lax.dot_general(a, b, (((0,), (cb,)), ((), ())), preferred_element_type=F32)

    host = _Hosted(plan, 6, 3)
    ng = nh // group

    def body(*refs):
        (q_ref, k_ref, v_ref, do_ref, lse_ref, dl_ref), (dq_ref, dk_ref, dv_ref), _, prefs = host.split(refs)
        hp, j = pl.program_id(0), pl.program_id(1)
        host.run(0, (hp == 0) & (j == 0), prefs)
        host.run(1, (hp == 0) & (j == 0), prefs)

        @pl.when(j == 0)
        def _():
            dq_ref[...] = jnp.zeros_like(dq_ref)

        lane = lax.broadcasted_iota(jnp.int32, (tq, 128), 1)
        causal = _causal_mask(tq)

        def blk(i, carry, masked):
            rows = pl.ds(pl.multiple_of(i * tq, tq), tq)
            lse_t, dl_t = lse_ref[rows, :], dl_ref[rows, :]
            out = []
            for g, (dk, dv) in enumerate(carry):
                h = hp * group + g
                kb, vb = k_ref[g], v_ref[g]
                qb, dob = q_ref[g, rows, :], do_ref[g, rows, :]
                lse_h = jnp.sum(jnp.where(lane == h, lse_t, 0.0), axis=1, keepdims=True)
                dl_h = jnp.sum(jnp.where(lane == h, dl_t, 0.0), axis=1, keepdims=True)
                s = lax.dot_general(qb, kb, (((1,), (1,)), ((), ())), preferred_element_type=F32)
                if masked:
                    s = jnp.where(causal, s, -jnp.inf)
                p = jnp.exp(s - lse_h)
                dp = lax.dot_general(dob, vb, (((1,), (1,)), ((), ())), preferred_element_type=F32)
                ds = (p * (dp - dl_h)).astype(BF16)
                dv = dv + tdot(dob, p.astype(BF16), 0)
                dk = dk + tdot(qb, ds, 0)
                dq_ref[g, :, rows] += tdot(kb, ds, 1)
                out.append((dk, dv))
            return tuple(out)

        init = tuple((jnp.zeros((128, tq), F32), jnp.zeros((128, tq), F32)) for _ in range(group))
        carry = blk(j, init, True)
        carry = lax.fori_loop(j + 1, nq, lambda i, c: blk(i, c, False), carry)
        for g, (dk, dv) in enumerate(carry):
            dk_ref[g] = dk
            dv_ref[g] = dv
        host.run(2, (hp == ng - 1) & (j == nq - 1), prefs)

    full = pl.BlockSpec((group, t, 128), lambda h, j: (h, 0, 0))
    tile = pl.BlockSpec((group, tq, 128), lambda h, j: (h, j, 0))
    cols = pl.BlockSpec((t, 128), lambda h, j: (0, 0))
    full_t = pl.BlockSpec((group, 128, t), lambda h, j: (h, 0, 0))
    tile_t = pl.BlockSpec((group, 128, tq), lambda h, j: (h, 0, j))
    hc = host.call_args()
    res = pl.pallas_call(
        body, name="fox_bwd", grid=(ng, nq), in_specs=[full, tile, tile, full, cols, cols] + hc["in_specs"],
        out_specs=[full_t, tile_t, tile_t] + hc["out_specs"],
        out_shape=[jax.ShapeDtypeStruct((nh, 128, t), F32)] * 3 + hc["out_shape"], scratch_shapes=hc["scratch"],
        input_output_aliases=hc["aliases"], compiler_params=_cparams(("arbitrary", "arbitrary")),
    )(qa, ka, va, doa, lse, delta, *hc["args"])
    return res[:3], res[3:]


def _fox_bwd_post(dqa, dka, dva):
    nh, _, t = dqa.shape
    tm = _pick(t, (1024, 512, 256))

    def body(dq_ref, dk_ref, dv_ref, oq_ref, ok_ref, ov_ref, dc_ref):
        qs, ks, vs, dcs = [], [], [], []
        for h in range(nh):
            dq, dk = dq_ref[h], dk_ref[h]
            qs.append(dq.T[:, :DH_F] * (DH_F ** -0.5))
            ks.append(dk.T[:, :DH_F])
            vs.append(dv_ref[h].T[:, :DH_F])
            dcs.append(dq[DH_F:DH_F + 1, :] - dk[DH_F + 3:DH_F + 4, :])
        oq_ref[...] = jnp.concatenate(qs, axis=1).astype(BF16)
        ok_ref[...] = jnp.concatenate(ks, axis=1).astype(BF16)
        ov_ref[...] = jnp.concatenate(vs, axis=1).astype(BF16)
        dc_ref[...] = jnp.concatenate(dcs, axis=0)

    ispec = pl.BlockSpec((nh, 128, tm), lambda i: (0, 0, i))
    ospec = pl.BlockSpec((tm, nh * DH_F), lambda i: (i, 0))
    return pl.pallas_call(
        body, name="fox_bwd_post", grid=(t // tm,), in_specs=[ispec] * 3,
        out_specs=[ospec] * 3 + [pl.BlockSpec((nh, tm), lambda i: (0, i))],
        out_shape=[jax.ShapeDtypeStruct((t, nh * DH_F), BF16)] * 3 + [jax.ShapeDtypeStruct((nh, t), F32)],
        compiler_params=_cparams(("parallel",)),
    )(dqa, dka, dva)


IN_OFF = (0, 512, 1024, 1544, 2056, 2568)
IN_GATES = (1536, 3080)


FFN1 = ("ffn1_w_gate", "ffn1_w_up", "ffn1_w_down")
REST = ("w_in", "w_out", "ffn2_w_gate", "ffn2_w_up", "ffn2_w_down", "w_ple_gate", "w_ple_proj")
SPLIT = {n: 1 if n == "w_in" else 0 for n in FFN1 + REST}
SAME_SHAPE = (FFN1, ("ffn2_w_gate", "ffn2_w_up", "ffn2_w_down"), ("w_out", "w_ple_gate"), ("w_in",), ("w_ple_proj",))


def _grouped(names):
    return [tuple(n for n in grp if n in names) for grp in SAME_SHAPE if any(n in names for n in grp)]


def _rs_partials(names, gw, c_idx, twins, run_swap=None):
    wire = [twins[n] if n in twins else _cast_other_half("rs_cast_" + n, gw[n], c_idx, SPLIT[n]) for n in names]
    plan = _swap_plan(wire, [SPLIT[n] if n in twins else None for n in names])
    swapped = dict(zip(names, run_swap(plan) if run_swap else _run_plan("rs_swap_" + names[0], plan)))
    out = {}
    for grp in _grouped(names):
        res = _add_my_half("rs_add_" + grp[0], [gw[n] for n in grp], [swapped[n] for n in grp], c_idx, SPLIT[grp[0]])
        out.update(zip(grp, res))
    return [out[n] for n in names]


def _local_step(x, p, tgt, sp, wg1, wu1, wd1, rest_slots, c_idx, place):
    t, d = x.shape
    slot = dict(zip(REST + ("conv_qk",), rest_slots))
    (h1, xn1, g1, u1), (w_in, conv_w) = _ffn_fwd(
        "ffn1", x, sp["ffn1_norm"], wg1, wu1, wd1, plan=_gather_plan([slot["w_in"], slot["conv_qk"]], [SPLIT["w_in"], None]))
    w_in, conv_w = w_in.reshape(-1, d), _from_chip_blocks(conv_w)
    w_big = jnp.concatenate([w_in[o:o + 512] for o in IN_OFF], axis=0)
    w_small = jnp.concatenate([w_in[IN_GATES[0]:IN_GATES[0] + 8], w_in[IN_GATES[1]:IN_GATES[1] + 8],
                               jnp.zeros((112, d), w_in.dtype)], axis=0)
    u, zbig, zs = _norm_mm("in_big", h1, sp["mix_norm"], w_big, True, BF16, w_side=w_small)
    zsr = zs.T
    qk_act = _conv_fwd(zbig, conv_w)
    bm_c, bf_c = sp["b_mlstm_gates"], sp["b_fox_f"]
    y_m, cst, mst = _mlstm_fwd(qk_act, zbig, zs, zsr, bm_c, bm_c.T, sp["mlstm_out_norm"])
    c = _fox_cumsum(zsr, bf_c.T)
    qa, ka, va = _fox_prep(zbig, c.T)
    (y_ft, o_f, lse), late = _fox_fwd2(qa, ka, va, sp["fox_out_norm"],
                                       plan=_gather_plan([slot[n] for n in REST[1:]], [SPLIT[n] for n in REST[1:]]))
    full = dict(zip(REST[1:], late))
    w_out, w_pg = (full[n].reshape(-1, d) for n in ("w_out", "w_ple_gate"))
    wg2, wu2, wd2 = full["ffn2_w_gate"], full["ffn2_w_up"], full["ffn2_w_down"]
    w_pp = _from_chip_blocks(full["w_ple_proj"])
    tm = _pick(t, (1024, 512, 256))
    h2 = _mm("out_proj", [(y_m, (tm, 512), lambda i, j, k: (i, 0), w_out, (512, d), lambda i, j, k: (0, 0)),
                          (y_ft, (tm, 512), lambda i, j, k: (i, 0), w_out, (512, d), lambda i, j, k: (1, 0))],
             (t, d), (tm, d), lambda i, j, k: (i, 0), (t // tm, 1, 1), 2, res=h1)
    (h3, xn2, g2, u2), _ = _ffn_fwd("ffn2", h2, sp["ffn2_norm"], wg2, wu2, wd2)
    hn3, gate_pre = _norm_mm("ple_gate", h3, sp["ple_gate_norm"], w_pg, False, F32)
    pp = _mm_nn("ple_proj", p, w_pp, tm=1024)

    def head_fn(h3_t, gp_t, pp_t, tgt_t, g_pp, g_fin):
        gate = _sigmoid(gp_t)
        ppn = _rms_fwd_val(pp_t, g_pp)
        h4 = h3_t + gate * ppn
        err = _rms_fwd_val(h4, g_fin) - tgt_t
        loss = 0.5 * jnp.sum(jnp.mean(err * err, axis=-1, keepdims=True), axis=0, keepdims=True)
        dh4, dg_fin = _rms_bwd_val(err * (1.0 / d), h4, g_fin)
        dpp, dg_pp = _rms_bwd_val(dh4 * gate, pp_t, g_pp)
        dgp = dh4 * ppn * gate * (1.0 - gate)
        return dh4, dgp, dpp, jnp.broadcast_to(loss, (1, 128)), dg_fin, dg_pp

    dh4, dgp, dpp, loss_part, dg_fin, dg_pp = _rowwise(
        "loss_head", head_fn, [h3, gate_pre, pp, tgt], [sp["ple_proj_norm"], sp["final_norm"]],
        [(d, F32), (d, BF16), (d, BF16)], [((1, 128), F32), ((1, d), F32), ((1, d), F32)])
    gw, gs = {}, {"final_norm": dg_fin, "ple_proj_norm": dg_pp}
    gw["w_ple_gate"] = _mm_tn("d_w_pg", hn3, dgp, tm=1024, tn=1024)
    gw["w_ple_proj"] = _mm_tn("d_w_pp", p, dpp, tn=1024)
    dhn3 = _mm_nt("d_hn3", dgp, w_pg, tm=1024, tn=1024, tk=1024)

    def res_norm_bwd(dn_t, h_t, dres_t, g):
        dx, dg = _rms_bwd_val(dn_t, h_t, g)
        return dres_t + dx, dg

    dh3, gs["ple_gate_norm"] = _rowwise("ple_norm_bwd", res_norm_bwd, [dhn3, h3, dh4], [sp["ple_gate_norm"]],
                                        [(d, F32)], [((1, d), F32)])
    (dh2, gs["ffn2_norm"], gw["ffn2_w_gate"], gw["ffn2_w_up"], gw["ffn2_w_down"]), _, twins2 = _ffn_bwd(
        "ffn2", dh3, h2, sp["ffn2_norm"], xn2, g2, u2, wg2, wu2, wd2)
    ffn2_names = ("ffn2_w_gate", "ffn2_w_up", "ffn2_w_down")
    early = []

    def swap_in_d_ycat(plan):
        dyc, swapped = _mm_nt("d_ycat", dh2, w_out, tm=1024, tn=1024, tk=1024, plan=plan)
        early.append(dyc)
        return swapped

    part_ffn2 = dict(zip(ffn2_names, _rs_partials(ffn2_names, gw, c_idx, dict(zip(ffn2_names, twins2)), swap_in_d_ycat)))
    dycat = early[0]
    gw["w_out"] = jnp.concatenate([_mm_tn("d_w_out_m", y_m, dh2, tn=1024, tk=2048),
                                   _mm_tn("d_w_out_f", y_ft, dh2, tn=1024, tk=2048)], axis=0)
    doa, delta, gs["fox_out_norm"] = _fox_bwd_prep(dycat, o_f, sp["fox_out_norm"])
    dqkv_t, landed_ffn2 = _fox_bwd2(qa, ka, va, doa, lse, delta, plan=_scatter_plan([part_ffn2[n][1] for n in ffn2_names]))
    dq_f, dk_f, dv_f, dct = _fox_bwd_post(*dqkv_t)
    dfp = _fox_gate_bwd(zsr, bf_c.T, dct)
    dact, dv_m, do_m, dzs_m, dzr_m, gs["mlstm_out_norm"] = _mlstm_bwd(
        qk_act, zbig, zs, zsr, bm_c, bm_c.T, sp["mlstm_out_norm"], cst, mst, dycat)
    dqk, gw["conv_qk"] = _conv_bwd(zbig, dact, conv_w)
    dz_big = jnp.concatenate([dqk, dv_m, do_m, dq_f, dk_f, dv_f], axis=1)
    dzs = dzs_m + jnp.pad(jnp.concatenate([dzr_m, dfp], axis=0).T, ((0, 0), (0, 112)))
    dw_big = _mm_tn("d_w_big", dz_big, u, tn=1024)
    dw_small = _mm_tn("d_w_small", dzs, u, tn=1024)
    gw["w_in"] = jnp.concatenate([dw_big[0:1536], dw_small[0:8], dw_big[1536:3072], dw_small[8:16]], axis=0)
    du_a = _mm_nn("d_u_big", dz_big, w_big, tm=1024, tn=1024, tk=3072)
    du_b = _mm_nn("d_u_small", dzs, w_small, tm=1024, tn=1024)

    def mix_norm_bwd(da_t, db_t, h_t, dres_t, dzs_t, g):
        dx, dg = _rms_bwd_val(da_t + db_t, h_t, g)
        return dres_t + dx, dg, _colsum(dzs_t)

    conv_grad = gw.pop("conv_qk")
    gw["w_ple_proj"] = _chip_blocks(gw["w_ple_proj"])
    for n in ("w_in", "w_out", "w_ple_gate"):
        gw[n] = gw[n].reshape(4, -1, gw[n].shape[-1])
    mix = []

    def swap_in_mix_norm_bwd(plan):
        res, swapped = _rowwise("mix_norm_bwd", mix_norm_bwd, [du_a, du_b, h1, dh2, dzs], [sp["mix_norm"]],
                                [(d, F32)], [((1, d), F32), ((1, 128), F32)], plan=plan)
        mix.extend(res)
        return swapped

    light = ("w_in", "w_out", "w_ple_gate", "w_ple_proj")
    part_light = dict(zip(light, _rs_partials(light, gw, c_idx, {}, swap_in_mix_norm_bwd)))
    dh1, gs["mix_norm"], dbias = mix
    gs["b_mlstm_gates"], gs["b_fox_f"] = dbias[:, 0:8], dbias[:, 8:16]
    part_ffn1 = []

    def own_plan(dws, dw_twins):
        part_ffn1.extend(_rs_partials(FFN1, dict(zip(FFN1, dws)), c_idx, dict(zip(FFN1, dw_twins))))
        return _scatter_plan([pb for _, pb in part_ffn1])

    (grad_x, gs["ffn1_norm"], _, _, _), (l_light, _, _, _, landed_ffn1) = _ffn_bwd_late_dx(
        "ffn1", dh1, x, sp["ffn1_norm"], xn1, g1, u1, wg1, wu1, wd1,
        _scatter_plan([part_light[n][1] for n in light]), [None] * 3, own_plan)
    names = REST + FFN1
    parts = {**part_light, **part_ffn2, **dict(zip(FFN1, part_ffn1))}
    landed = {**dict(zip(light, l_light)), **dict(zip(ffn2_names, landed_ffn2)), **dict(zip(FFN1, landed_ffn1))}
    mine = {}
    for grp in _grouped(names):
        res = _sum4("rs_sum_" + grp[0], [landed[n] for n in grp], [parts[n][0] for n in grp], place, SPLIT[grp[0]])
        mine.update(zip(grp, res))
    grads = dict(zip(names, _join_halves("rs_join", [mine[n] for n in names], [SPLIT[n] for n in names])))
    return loss_part, grad_x, grads, gs, conv_grad


ANY = pl.BlockSpec(memory_space=pl.ANY)
MESH = pl.DeviceIdType.MESH


def _place():
    x, y, c = lax.axis_index("x"), lax.axis_index("y"), lax.axis_index("c")
    chips = [(1 - x, y), (x, 1 - y), (1 - x, 1 - y)]
    return x, y, c, 2 * x + y, (x, y, 1 - c), chips


def _rcopy(src, dst, ssem, rsem, dev):
    return pltpu.make_async_remote_copy(src_ref=src, dst_ref=dst, send_sem=ssem, recv_sem=rsem, device_id=dev,
                                        device_id_type=MESH)


def _half(ref, lead, axis, idx, half):
    return ref.at[(slice(None),) * (lead + axis) + (pl.ds(idx * half, half),)]


def _to_slot(name, arrs, me_idx, dtype):
    n = len(arrs)
    r, cdim = arrs[0].shape
    tr = _pick(r, (352, 256, 176, 128, 64))

    def body(me_ref, *refs):
        for k in range(n):
            refs[n + k][...] = refs[k][...].astype(dtype)

    return pl.pallas_call(
        body, name=name,
        grid_spec=pltpu.PrefetchScalarGridSpec(
            num_scalar_prefetch=1, grid=(r // tr,), in_specs=[pl.BlockSpec((tr, cdim), lambda i, me_ref: (i, 0))] * n,
            out_specs=[pl.BlockSpec((None, tr, cdim), lambda i, me_ref: (me_ref[0], i, 0))] * n),
        out_shape=[jax.ShapeDtypeStruct((4, r, cdim), dtype)] * n, compiler_params=_cparams(("parallel",)),
    )(me_idx, *arrs)


def _gather4(name, bufs, split):
    return _run_plan(name, _gather_plan(bufs, split))


def _gather_plan(bufs, split):
    n = len(bufs)
    shapes = [b.shape[1:] for b in bufs]

    def ctx(outs):
        x, y, c, me, sib, chips = _place()

        def part(ref, a, which):
            if split[a] is None:
                return ref
            return _half(ref, 0, split[a], which, shapes[a][split[a]] // 2)

        return c, me, sib, chips, part

    def ici(outs, sems, a, j, chip, c, me, part):
        mine = part(outs[a].at[me], a, c)
        return _rcopy(mine, mine, sems[0].at[3 * a + j], sems[1].at[3 * a + j], (*chip, c))

    def fwd(outs, sems, a, j, chip, c, sib, part, which):
        blk = part(outs[a].at[2 * chip[0] + chip[1]], a, which)
        return _rcopy(blk, blk, sems[2].at[3 * a + j], sems[3].at[3 * a + j], sib)

    def start(ins, outs, sems):
        c, me, sib, chips, part = ctx(outs)
        for a in range(n):
            for j, chip in enumerate(chips):
                ici(outs, sems, a, j, chip, c, me, part).start()

    def mid(ins, outs, sems):
        c, me, sib, chips, part = ctx(outs)
        for j, chip in enumerate(chips):
            for a in range(n):
                blk = part(outs[a].at[2 * chip[0] + chip[1]], a, c)
                _rcopy(blk, blk, sems[0].at[3 * a + j], sems[1].at[3 * a + j], sib).wait_recv()
                if split[a] is not None:
                    fwd(outs, sems, a, j, chip, c, sib, part, c).start()

    def end(ins, outs, sems):
        c, me, sib, chips, part = ctx(outs)
        for j, chip in enumerate(chips):
            for a in range(n):
                if split[a] is not None:
                    fwd(outs, sems, a, j, chip, c, sib, part, 1 - c).wait_recv()
        for a in range(n):
            for j, chip in enumerate(chips):
                ici(outs, sems, a, j, chip, c, me, part).wait_send()
                if split[a] is not None:
                    fwd(outs, sems, a, j, chip, c, sib, part, c).wait_send()

    return dict(ins=list(bufs), outs=[jax.ShapeDtypeStruct(b.shape, b.dtype) for b in bufs], alias=True,
                sems=[pltpu.SemaphoreType.DMA((3 * n,))] * 4, phases=(start, mid, end))


def _run_plan(name, plan):
    ni, no = len(plan["ins"]), len(plan["outs"])

    def body(*refs):
        ins, outs, sems = refs[:ni], refs[ni:ni + no], refs[ni + no:]
        for phase in plan["phases"]:
            phase(ins, outs, sems)

    return pl.pallas_call(
        body, name=name, in_specs=[ANY] * ni, out_specs=[ANY] * no, out_shape=plan["outs"],
        input_output_aliases={a: a for a in range(ni)} if plan["alias"] else {}, scratch_shapes=plan["sems"],
    )(*plan["ins"])


class _Hosted:
    def __init__(self, plan, n_in, n_out):
        self.plan, self.n_in, self.n_out = plan, n_in, n_out
        self.ni, self.no, self.ns = (len(plan["ins"]) if plan else 0, len(plan["outs"]) if plan else 0,
                                     len(plan["sems"]) if plan else 0)

    def split(self, refs):
        a, b = self.n_in, self.n_in + self.ni
        c, d = b + self.n_out, b + self.n_out + self.no
        e = len(refs) - self.ns
        return refs[:a], refs[b:c], refs[d:e], (refs[a:b], refs[c:d], refs[e:])

    def run(self, k, cond, prefs):
        if self.plan is not None:
            @pl.when(cond)
            def _():
                self.plan["phases"][k](*prefs)

    def call_args(self):
        p = self.plan
        if p is None:
            return dict(in_specs=[], out_specs=[], out_shape=[], scratch=[], aliases={}, args=[])
        al = {self.n_in + a: self.n_out + a for a in range(self.ni)} if p["alias"] else {}
        return dict(in_specs=[ANY] * self.ni, out_specs=[ANY] * self.no, out_shape=list(p["outs"]), scratch=list(p["sems"]),
                    aliases=al, args=list(p["ins"]))


def _swap(name, arrs, halve):
    return _run_plan(name, _swap_plan(arrs, halve))


def _swap_plan(arrs, halve):
    n = len(arrs)

    def half_shape(a, ax):
        return a.shape if ax is None else (a.shape[0],) + tuple(d // 2 if i == ax else d for i, d in enumerate(a.shape[1:]))

    def copies(ins, outs, sems):
        x, y, c, me, sib, chips = _place()
        cps = []
        for a in range(n):
            src = ins[a] if halve[a] is None else _half(ins[a], 1, halve[a], 1 - c, arrs[a].shape[1 + halve[a]] // 2)
            cps.append(_rcopy(src, outs[a], sems[0].at[a], sems[1].at[a], sib))
        return cps

    def start(ins, outs, sems):
        for cp in copies(ins, outs, sems):
            cp.start()

    def mid(ins, outs, sems):
        pass

    def end(ins, outs, sems):
        for cp in copies(ins, outs, sems):
            cp.wait()

    return dict(ins=list(arrs), outs=[jax.ShapeDtypeStruct(half_shape(a, ax), a.dtype) for a, ax in zip(arrs, halve)],
                alias=False, sems=[pltpu.SemaphoreType.DMA((n,))] * 2, phases=(start, mid, end))


def _scatter4(name, arrs):
    return _run_plan(name, _scatter_plan(arrs))


def _scatter_plan(arrs):
    n = len(arrs)

    def send(ins, outs, sems, a, j, chip, c, me):
        return _rcopy(ins[a].at[2 * chip[0] + chip[1]], outs[a].at[me], sems[0].at[3 * a + j], sems[1].at[3 * a + j], (*chip, c))

    def start(ins, outs, sems):
        x, y, c, me, sib, chips = _place()
        for a in range(n):
            for j, chip in enumerate(chips):
                send(ins, outs, sems, a, j, chip, c, me).start()

    def mid(ins, outs, sems):
        pass

    def end(ins, outs, sems):
        x, y, c, me, sib, chips = _place()
        for a in range(n):
            for j, chip in enumerate(chips):
                blk = outs[a].at[2 * chip[0] + chip[1]]
                _rcopy(blk, blk, sems[0].at[3 * a + j], sems[1].at[3 * a + j], sib).wait_recv()
        for a in range(n):
            for j, chip in enumerate(chips):
                send(ins, outs, sems, a, j, chip, c, me).wait_send()

    return dict(ins=list(arrs), outs=[jax.ShapeDtypeStruct(a.shape, a.dtype) for a in arrs], alias=False,
                sems=[pltpu.SemaphoreType.DMA((3 * n,))] * 2, phases=(start, mid, end))


def _join_halves(name, arrs, split):
    n = len(arrs)

    def body(*refs):
        outs = refs[n:2 * n]
        ssem, rsem = refs[2 * n:]
        x, y, c, me, sib, chips = _place()
        cps = []
        for a in range(n):
            mine = _half(outs[a], 0, split[a], c, arrs[a].shape[split[a]] // 2)
            cp = _rcopy(mine, mine, ssem.at[a], rsem.at[a], sib)
            cp.start()
            cps.append(cp)
        for a in range(n):
            blk = _half(outs[a], 0, split[a], 1 - c, arrs[a].shape[split[a]] // 2)
            _rcopy(blk, blk, ssem.at[a], rsem.at[a], sib).wait_recv()
        for cp in cps:
            cp.wait_send()

    return pl.pallas_call(
        body, name=name, in_specs=[ANY] * n, out_specs=[ANY] * n,
        out_shape=[jax.ShapeDtypeStruct(a.shape, a.dtype) for a in arrs],
        input_output_aliases={a: a for a in range(n)}, scratch_shapes=[pltpu.SemaphoreType.DMA((n,))] * 2,
    )(*arrs)


def _allreduce_small(s):
    r, cdim = s.shape

    def body(s_ref, o_ref, buf, ssem, rsem):
        x, y, c, me, sib, chips = _place()
        me8 = 4 * x + 2 * y + c
        buf[me8] = s_ref[...]
        flips = [(fx, fy, fc) for fx in (0, 1) for fy in (0, 1) for fc in (0, 1)][1:]
        cps = []
        for k, (fx, fy, fc) in enumerate(flips):
            peer = (x ^ fx if fx else x, y ^ fy if fy else y, c ^ fc if fc else c)
            cp = _rcopy(s_ref, buf.at[me8], ssem.at[k], rsem.at[k], peer)
            cp.start()
            cps.append(cp)
        for k, (fx, fy, fc) in enumerate(flips):
            src = 4 * (x ^ fx if fx else x) + 2 * (y ^ fy if fy else y) + (c ^ fc if fc else c)
            _rcopy(s_ref, buf.at[src], ssem.at[k], rsem.at[k], sib).wait_recv()
        for cp in cps:
            cp.wait_send()
        acc = buf[0]
        for k in range(1, 8):
            acc = acc + buf[k]
        o_ref[...] = acc

    vm = pl.BlockSpec(memory_space=pltpu.VMEM)
    return pl.pallas_call(
        body, name="allreduce_small", in_specs=[vm], out_specs=vm, out_shape=jax.ShapeDtypeStruct((r, cdim), F32),
        scratch_shapes=[pltpu.VMEM((8, r, cdim), F32), pltpu.SemaphoreType.DMA((7,)), pltpu.SemaphoreType.DMA((7,))],
    )(s)


def _add_my_half(name, gs, recvs, c_idx, axis):
    n = len(gs)
    nb, hr, hc = recvs[0].shape
    tr = _pick(hr, (256, 176, 128, 64))
    if axis == 0:
        g4s = [g.reshape(nb, 2, hr, hc) for g in gs]
        gspec = pl.BlockSpec((None, None, tr, hc), lambda b, i, c_ref: (b, c_ref[0], i, 0))
    else:
        g4s = list(gs)
        gspec = pl.BlockSpec((None, tr, hc), lambda b, i, c_ref: (b, i, c_ref[0]))

    def body(c_ref, *refs):
        for k in range(n):
            s = refs[k][...] + refs[n + k][...].astype(F32)
            refs[2 * n + 2 * k][...] = s
            refs[2 * n + 2 * k + 1][...] = s.astype(BF16)

    ospec = pl.BlockSpec((None, tr, hc), lambda b, i, c_ref: (b, i, 0))
    res = pl.pallas_call(
        body, name=name,
        grid_spec=pltpu.PrefetchScalarGridSpec(
            num_scalar_prefetch=1, grid=(nb, hr // tr), in_specs=[gspec] * n + [ospec] * n, out_specs=[ospec] * (2 * n)),
        out_shape=[jax.ShapeDtypeStruct((nb, hr, hc), F32), jax.ShapeDtypeStruct((nb, hr, hc), BF16)] * n,
        compiler_params=_cparams(("parallel", "parallel")),
    )(c_idx, *g4s, *recvs)
    return [(res[2 * k], res[2 * k + 1]) for k in range(n)]


def _sum4(name, landeds, owns, place, axis):
    n = len(landeds)
    nb, h, cdim = landeds[0].shape
    tr = _pick(h, (256, 176, 128, 64))
    nt = h // tr

    def body(p_ref, *refs):
        for k in range(n):
            a1, a2, a3, own = refs[4 * k:4 * k + 4]
            refs[4 * n + k][...] = ((own[...] + a1[...].astype(F32)) + a2[...].astype(F32)) + a3[...].astype(F32)

    def nxt(k):
        return pl.BlockSpec((None, tr, cdim), lambda i, p_ref: ((p_ref[0] + k) % nb, i, 0))

    if axis == 0:
        ospec = pl.BlockSpec((tr, cdim), lambda i, p_ref: (p_ref[1] * nt + i, 0))
        oshape = (2 * h, cdim)
    else:
        ospec = pl.BlockSpec((tr, cdim), lambda i, p_ref: (i, p_ref[1]))
        oshape = (h, 2 * cdim)
    args = []
    for landed, own in zip(landeds, owns):
        args += [landed, landed, landed, own]
    return pl.pallas_call(
        body, name=name,
        grid_spec=pltpu.PrefetchScalarGridSpec(
            num_scalar_prefetch=1, grid=(nt,), in_specs=[nxt(1), nxt(2), nxt(3), nxt(0)] * n, out_specs=[ospec] * n),
        out_shape=[jax.ShapeDtypeStruct(oshape, F32)] * n, compiler_params=_cparams(("parallel",)),
    )(place, *args)


def _cast_other_half(name, g, c_idx, axis):
    nb, r, cdim = g.shape
    hr, hc = (r // 2, cdim) if axis == 0 else (r, cdim // 2)
    tr = _pick(hr, (256, 176, 128, 64))
    if axis == 0:
        g4 = g.reshape(nb, 2, hr, hc)
        gspec = pl.BlockSpec((None, None, tr, hc), lambda b, i, c_ref: (b, 1 - c_ref[0], i, 0))
    else:
        g4 = g
        gspec = pl.BlockSpec((None, tr, hc), lambda b, i, c_ref: (b, i, 1 - c_ref[0]))

    def body(c_ref, g_ref, o_ref):
        o_ref[...] = g_ref[...].astype(BF16)

    return pl.pallas_call(
        body, name=name,
        grid_spec=pltpu.PrefetchScalarGridSpec(
            num_scalar_prefetch=1, grid=(nb, hr // tr), in_specs=[gspec],
            out_specs=pl.BlockSpec((None, tr, hc), lambda b, i, c_ref: (b, i, 0))),
        out_shape=jax.ShapeDtypeStruct((nb, hr, hc), BF16), compiler_params=_cparams(("parallel", "parallel")),
    )(c_idx, g4)


def _adamw(name, ws, gs, ms, vs):
    n = len(ws)
    c1 = 1.0 - ADAM_B1 ** ADAM_STEP
    c2 = 1.0 - ADAM_B2 ** ADAM_STEP

    def fn(*tiles):
        out = []
        for k in range(n):
            w_t, g_t, m_t, v_t = tiles[4 * k:4 * k + 4]
            m_n = ADAM_B1 * m_t + (1.0 - ADAM_B1) * g_t
            v_n = ADAM_B2 * v_t + (1.0 - ADAM_B2) * (g_t * g_t)
            out += [-ADAM_LR * ((m_n / c1) / (jnp.sqrt(v_n / c2) + ADAM_EPS) + ADAM_WD * w_t), m_n, v_n]
        return out

    rows, cdim = ws[0].shape
    tiled = [a for quad in zip(ws, gs, ms, vs) for a in quad]
    pref = (512, 352, 256, 128, 64, 8) if n == 1 else (176, 128, 64, 8)
    res = _rowwise(name, fn, tiled, [], [(cdim, F32)] * (3 * n), tm=_pick(rows, pref))
    return [tuple(res[3 * k:3 * k + 3]) for k in range(n)]


BIG = ("ffn1_w_gate", "ffn1_w_up", "ffn1_w_down", "w_in", "w_out", "ffn2_w_gate", "ffn2_w_up", "ffn2_w_down",
       "w_ple_gate", "w_ple_proj")
SMALL = ("ffn1_norm", "mix_norm", "b_mlstm_gates", "b_fox_f", "mlstm_out_norm", "fox_out_norm", "ffn2_norm",
         "ple_gate_norm", "ple_proj_norm", "final_norm")
WEIGHTS = ("ffn1_norm", "ffn1_w_gate", "ffn1_w_up", "ffn1_w_down", "mix_norm", "w_in", "conv_qk", "b_mlstm_gates",
           "b_fox_f", "mlstm_out_norm", "fox_out_norm", "w_out", "ffn2_norm", "ffn2_w_gate", "ffn2_w_up", "ffn2_w_down",
           "ple_gate_norm", "w_ple_gate", "w_ple_proj", "ple_proj_norm", "final_norm")
TRANSPOSED = ("ffn1_w_gate", "ffn1_w_up", "w_in", "ffn2_w_gate", "ffn2_w_up")
PACK_W = 1024


def _chip_blocks(a):
    r, c4 = a.shape
    return a.reshape(r, 4, c4 // 4).transpose(1, 0, 2)


def _from_chip_blocks(a):
    nb, r, c = a.shape
    return a.transpose(1, 0, 2).reshape(r, nb * c)


def kernel(x, p, ffn1_norm, ffn1_w_gate, ffn1_w_up, ffn1_w_down, mix_norm, w_in, conv_qk, b_mlstm_gates, b_fox_f, mlstm_out_norm, fox_out_norm, w_out, ffn2_norm, ffn2_w_gate, ffn2_w_up, ffn2_w_down, ple_gate_norm, w_ple_gate, w_ple_proj, ple_proj_norm, final_norm, loss_target, m_ffn1_norm, m_ffn1_w_gate, m_ffn1_w_up, m_ffn1_w_down, m_mix_norm, m_w_in, m_conv_qk, m_b_mlstm_gates, m_b_fox_f, m_mlstm_out_norm, m_fox_out_norm, m_w_out, m_ffn2_norm, m_ffn2_w_gate, m_ffn2_w_up, m_ffn2_w_down, m_ple_gate_norm, m_w_ple_gate, m_w_ple_proj, m_ple_proj_norm, m_final_norm, v_ffn1_norm, v_ffn1_w_gate, v_ffn1_w_up, v_ffn1_w_down, v_mix_norm, v_w_in, v_conv_qk, v_b_mlstm_gates, v_b_fox_f, v_mlstm_out_norm, v_fox_out_norm, v_w_out, v_ffn2_norm, v_ffn2_w_gate, v_ffn2_w_up, v_ffn2_w_down, v_ple_gate_norm, v_w_ple_gate, v_w_ple_proj, v_ple_proj_norm, v_final_norm):
    w = dict(ffn1_norm=ffn1_norm, ffn1_w_gate=ffn1_w_gate, ffn1_w_up=ffn1_w_up, ffn1_w_down=ffn1_w_down, mix_norm=mix_norm,
             w_in=w_in, conv_qk=conv_qk, b_mlstm_gates=b_mlstm_gates, b_fox_f=b_fox_f, mlstm_out_norm=mlstm_out_norm,
             fox_out_norm=fox_out_norm, w_out=w_out, ffn2_norm=ffn2_norm, ffn2_w_gate=ffn2_w_gate, ffn2_w_up=ffn2_w_up,
             ffn2_w_down=ffn2_w_down, ple_gate_norm=ple_gate_norm, w_ple_gate=w_ple_gate, w_ple_proj=w_ple_proj,
             ple_proj_norm=ple_proj_norm, final_norm=final_norm)
    m = dict(ffn1_norm=m_ffn1_norm, ffn1_w_gate=m_ffn1_w_gate, ffn1_w_up=m_ffn1_w_up, ffn1_w_down=m_ffn1_w_down,
             mix_norm=m_mix_norm, w_in=m_w_in, conv_qk=m_conv_qk, b_mlstm_gates=m_b_mlstm_gates, b_fox_f=m_b_fox_f,
             mlstm_out_norm=m_mlstm_out_norm, fox_out_norm=m_fox_out_norm, w_out=m_w_out, ffn2_norm=m_ffn2_norm,
             ffn2_w_gate=m_ffn2_w_gate, ffn2_w_up=m_ffn2_w_up, ffn2_w_down=m_ffn2_w_down, ple_gate_norm=m_ple_gate_norm,
             w_ple_gate=m_w_ple_gate, w_ple_proj=m_w_ple_proj, ple_proj_norm=m_ple_proj_norm, final_norm=m_final_norm)
    v = dict(ffn1_norm=v_ffn1_norm, ffn1_w_gate=v_ffn1_w_gate, ffn1_w_up=v_ffn1_w_up, ffn1_w_down=v_ffn1_w_down,
             mix_norm=v_mix_norm, w_in=v_w_in, conv_qk=v_conv_qk, b_mlstm_gates=v_b_mlstm_gates, b_fox_f=v_b_fox_f,
             mlstm_out_norm=v_mlstm_out_norm, fox_out_norm=v_fox_out_norm, w_out=v_w_out, ffn2_norm=v_ffn2_norm,
             ffn2_w_gate=v_ffn2_w_gate, ffn2_w_up=v_ffn2_w_up, ffn2_w_down=v_ffn2_w_down, ple_gate_norm=v_ple_gate_norm,
             w_ple_gate=v_w_ple_gate, w_ple_proj=v_w_ple_proj, ple_proj_norm=v_ple_proj_norm, final_norm=v_final_norm)
    shapes = {n: w[n].shape for n in WEIGHTS}

    def view(a, n):
        return a[0].T if n in TRANSPOSED else a.reshape(-1, a.shape[-1])

    def unview(a, n):
        return (a.T if n in TRANSPOSED else a).reshape(shapes[n])

    w2, m2, v2 = ({n: view(a, n) for n, a in d.items()} for d in (w, m, v))

    c_idx = lax.axis_index("c").astype(jnp.int32).reshape(1)
    me_idx = (2 * lax.axis_index("x") + lax.axis_index("y")).astype(jnp.int32).reshape(1)
    place = jnp.concatenate([me_idx, c_idx])
    slot = {}
    for grp in SAME_SHAPE:
        slot.update(zip(grp, _to_slot("slot_" + grp[0], [w2[n] for n in grp], me_idx, BF16)))
    slot["conv_qk"] = _to_slot("slot_conv_qk", [w2["conv_qk"]], me_idx, F32)[0]
    wg1, wu1, wd1 = _gather4("gather_ffn1", [slot[n] for n in FFN1], [SPLIT[n] for n in FFN1])
    sp = {n: w2[n] for n in SMALL}
    loss_part, grad_x, grads, gs, conv_grad = _local_step(
        x[0], p[0, 0], loss_target[0], sp, wg1, wu1, wd1, [slot[n] for n in REST + ("conv_qk",)], c_idx, place)

    small = [gs[n].reshape(1, -1) for n in SMALL] + [conv_grad, loss_part]
    rows = [jnp.pad(a, ((0, 0), (0, PACK_W - a.shape[1]))) for a in small]
    packed = jnp.concatenate(rows, axis=0)
    packed = jnp.pad(packed, ((0, -packed.shape[0] % 8), (0, 0)))
    red = _allreduce_small(packed)
    loss = red[len(SMALL) + CONV_W, 0]
    for i, n in enumerate(SMALL):
        grads[n] = red[i:i + 1, :gs[n].size]
    dconv = red[len(SMALL):len(SMALL) + CONV_W, :conv_grad.shape[1]]
    cw = conv_qk.shape[-1]
    grads["conv_qk"] = lax.dynamic_slice_in_dim(dconv, (2 * lax.axis_index("x") + lax.axis_index("y")) * cw, cw, axis=1)

    outs = {}
    for grp in SAME_SHAPE + tuple((n,) for n in WEIGHTS if n not in BIG):
        g2s = [grads[n].reshape(w2[n].shape) for n in grp]
        res = _adamw("adamw_" + grp[0], [w2[n] for n in grp], g2s, [m2[n] for n in grp], [v2[n] for n in grp])
        for n, g2, (d, nm, nv) in zip(grp, g2s, res):
            outs[n] = tuple(unview(a, n) for a in (g2, d, nm, nv))
    return (loss, grad_x[None], *[outs[n][0] for n in WEIGHTS], *[outs[n][1] for n in WEIGHTS],
            *[outs[n][2] for n in WEIGHTS], *[outs[n][3] for n in WEIGHTS])
```

```python
import jax
import jax.numpy as jnp
from jax import lax
from jax.experimental import pallas as pl
from jax.experimental.pallas import tpu as pltpu

F32 = jnp.float32
BF16 = jnp.bfloat16
EPS = 1e-6
NH_M, DK_M, DV_M = 4, 64, 128
NH_F, DH_F = 8, 64
CONV_W = 4
ADAM_LR, ADAM_B1, ADAM_B2, ADAM_EPS, ADAM_WD, ADAM_STEP = 0.001, 0.9, 0.999, 1e-08, 0.01, 10
VMEM_LIMIT = 56 * 1024 * 1024


def _cparams(sem):
    return pltpu.CompilerParams(dimension_semantics=sem, vmem_limit_bytes=VMEM_LIMIT)


def _sigmoid(x):
    return 1.0 / (1.0 + jnp.exp(-x))


def _dot(a, b, ca, cb):
    return lax.dot_general(a.astype(BF16), b.astype(BF16), (((ca,), (cb,)), ((), ())), preferred_element_type=F32)


def _rowwise(name, fn, tiled, full, outs, accs=(), tm=512, plan=None):
    rows = tiled[0].shape[0]
    tm = min(tm, rows)
    assert rows % tm == 0
    n_t, n_f, n_o, n_a = len(tiled), len(full), len(outs), len(accs)
    nt = rows // tm
    host = _Hosted(plan, n_t + n_f, n_o + n_a)

    def body(*refs):
        in_refs, orefs, _, prefs = host.split(refs)
        host.run(0, pl.program_id(0) == 0, prefs)
        host.run(1, pl.program_id(0) == 0, prefs)
        ins = [r[...] for r in in_refs]
        res = fn(*ins)
        if not isinstance(res, (tuple, list)):
            res = (res,)
        for r, v in zip(orefs[:n_o], res[:n_o]):
            r[...] = v.astype(r.dtype)
        if n_a:
            @pl.when(pl.program_id(0) == 0)
            def _():
                for r in orefs[n_o:]:
                    r[...] = jnp.zeros_like(r)
            for r, v in zip(orefs[n_o:], res[n_o:]):
                r[...] += v.astype(r.dtype)
        host.run(2, pl.program_id(0) == nt - 1, prefs)

    in_specs = [pl.BlockSpec((tm, a.shape[1]), lambda i: (i, 0)) for a in tiled]
    in_specs += [pl.BlockSpec(a.shape, lambda i: (0, 0)) for a in full]
    out_specs = [pl.BlockSpec((tm, c), lambda i: (i, 0)) for c, _ in outs]
    out_specs += [pl.BlockSpec(s, lambda i: (0, 0)) for s, _ in accs]
    out_shape = [jax.ShapeDtypeStruct((rows, c), d) for c, d in outs]
    out_shape += [jax.ShapeDtypeStruct(s, d) for s, d in accs]
    hc = host.call_args()
    res = pl.pallas_call(
        body, name=name, grid=(nt,), in_specs=in_specs + hc["in_specs"], out_specs=out_specs + hc["out_specs"],
        out_shape=out_shape + hc["out_shape"], scratch_shapes=hc["scratch"], input_output_aliases=hc["aliases"],
        compiler_params=_cparams(("arbitrary",) if (n_a or plan is not None) else ("parallel",)),
    )(*tiled, *full, *hc["args"])
    return res if plan is None else (res[:n_o + n_a], res[n_o + n_a:])


def _colsum(v):
    return jnp.sum(v, axis=0, keepdims=True)


def _rms_fwd_val(x, g):
    r = lax.rsqrt(jnp.mean(x * x, axis=-1, keepdims=True) + EPS)
    return x * r * g


def _rms_bwd_val(dy, x, g):
    r = lax.rsqrt(jnp.mean(x * x, axis=-1, keepdims=True) + EPS)
    xh = x * r
    dxh = dy * g
    dx = r * (dxh - xh * jnp.mean(dxh * xh, axis=-1, keepdims=True))
    return dx, _colsum(dy * xh)


def _mm(name, pairs, out_shape, out_block, out_map, grid, kaxis, ta=False, tb=False, scale=None, res=None,
        out_dtype=F32, plan=None, twin=False):
    n_o = 2 if twin else 1
    nk = grid[kaxis]
    npairs = len(pairs)
    ca, cb = (0 if ta else 1), (1 if tb else 0)
    acc_shape = tuple(d for d in out_block if d is not None)
    n_in = 2 * npairs + (1 if res is not None else 0)
    host = _Hosted(plan, n_in, n_o)

    def body(*refs):
        ins, o_refs, (acc_ref,), prefs = host.split(refs)
        o_ref = o_refs[0]
        in_refs = ins[: 2 * npairs]
        res_ref = ins[2 * npairs] if res is not None else None
        k = pl.program_id(kaxis)
        ids = [pl.program_id(a) for a in range(len(grid))]
        first, last = ids[0] == 0, ids[0] == grid[0] - 1
        for a in range(1, len(grid)):
            first, last = first & (ids[a] == 0), last & (ids[a] == grid[a] - 1)
        host.run(0, first, prefs)
        host.run(1, first, prefs)

        @pl.when(k == 0)
        def _():
            acc_ref[...] = jnp.zeros_like(acc_ref)

        part = None
        for p in range(npairs):
            d = _dot(in_refs[2 * p][...], in_refs[2 * p + 1][...], ca, cb)
            part = d if part is None else part + d
        acc_ref[...] += part

        @pl.when(k == nk - 1)
        def _():
            v = acc_ref[...]
            if scale is not None:
                v = v * scale
            if res_ref is not None:
                v = v + res_ref[...].astype(F32)
            o_ref[...] = v.astype(o_ref.dtype)
            if twin:
                o_refs[1][...] = v.astype(BF16)

        host.run(2, last, prefs)

    in_specs, args = [], []
    for a, ab, am, b, bb, bm in pairs:
        in_specs += [pl.BlockSpec(ab, am), pl.BlockSpec(bb, bm)]
        args += [a, b]
    if res is not None:
        in_specs.append(pl.BlockSpec(out_block, out_map))
        args.append(res)
    sem = tuple("arbitrary" if (i == kaxis or plan is not None) else "parallel" for i in range(len(grid)))
    hc = host.call_args()
    out = pl.pallas_call(
        body, name=name, grid=grid, in_specs=in_specs + hc["in_specs"],
        out_specs=[pl.BlockSpec(out_block, out_map)] * n_o + hc["out_specs"],
        out_shape=[jax.ShapeDtypeStruct(out_shape, out_dtype)] + [jax.ShapeDtypeStruct(out_shape, BF16)] * (n_o - 1)
        + hc["out_shape"],
        scratch_shapes=[pltpu.VMEM(acc_shape, F32)] + hc["scratch"], input_output_aliases=hc["aliases"],
        compiler_params=_cparams(sem),
    )(*args, *hc["args"])
    res_out = tuple(out[:2]) if twin else out[0]
    return res_out if plan is None else (res_out, out[n_o:])


def _pick(n, pref):
    for t in pref:
        if n % t == 0:
            return t
    return n


def _mm_nn(name, a, b, tm=512, tn=512, tk=512, **kw):
    (m, k), n = a.shape, b.shape[1]
    tm, tn, tk = _pick(m, (tm, 256, 128)), _pick(n, (tn, 256, 128)), _pick(k, (tk, 256, 128))
    return _mm(name, [(a, (tm, tk), lambda i, j, kk: (i, kk), b, (tk, tn), lambda i, j, kk: (kk, j))],
               (m, n), (tm, tn), lambda i, j, kk: (i, j), (m // tm, n // tn, k // tk), 2, **kw)


def _mm_nt(name, a, b, tm=512, tn=512, tk=512, **kw):
    (m, k), n = a.shape, b.shape[0]
    tm, tn, tk = _pick(m, (tm, 256, 128)), _pick(n, (tn, 256, 128)), _pick(k, (tk, 256, 128))
    return _mm(name, [(a, (tm, tk), lambda i, j, kk: (i, kk), b, (tn, tk), lambda i, j, kk: (j, kk))],
               (m, n), (tm, tn), lambda i, j, kk: (i, j), (m // tm, n // tn, k // tk), 2, tb=True, **kw)


def _mm_tn(name, a, b, tm=512, tn=512, tk=4096, **kw):
    (k, m), n = a.shape, b.shape[1]
    tm, tn, tk = _pick(m, (tm, 256, 128)), _pick(n, (tn, 256, 128)), _pick(k, (tk, 2048, 1024, 512, 256, 128))
    return _mm(name, [(a, (tk, tm), lambda i, j, kk: (kk, i), b, (tk, tn), lambda i, j, kk: (kk, j))],
               (m, n), (tm, tn), lambda i, j, kk: (i, j), (m // tm, n // tn, k // tk), 2, ta=True, **kw)


def _norm_mm(name, h, gamma, w, w_transposed, out_dtype, w_side=None):
    t, d = h.shape
    n = w.shape[0] if w_transposed else w.shape[1]
    tm, tn = _pick(t, (1024, 512, 256)), _pick(n, (1024, 512, 256, 128))
    ns = 0 if w_side is None else w_side.shape[0]

    def body(*refs):
        h_ref, gam_ref, w_ref = refs[:3]
        xn_ref, o_ref = refs[3 + (ns > 0)], refs[4 + (ns > 0)]
        xn_scr = refs[-1]

        @pl.when(pl.program_id(1) == 0)
        def _():
            xn = _rms_fwd_val(h_ref[...], gam_ref[...]).astype(BF16)
            xn_scr[...] = xn
            xn_ref[...] = xn
            if ns:
                refs[5 + 1][...] = _dot(xn, refs[3][...], 1, 1)

        o_ref[...] = _dot(xn_scr[...], w_ref[...], 1, 1 if w_transposed else 0).astype(o_ref.dtype)

    wspec = pl.BlockSpec((tn, d), lambda i, j: (j, 0)) if w_transposed else pl.BlockSpec((d, tn), lambda i, j: (0, j))
    side_in = [pl.BlockSpec((ns, d), lambda i, j: (0, 0))] if ns else []
    side_out = [pl.BlockSpec((tm, ns), lambda i, j: (i, 0))] if ns else []
    return pl.pallas_call(
        body, name=name, grid=(t // tm, n // tn),
        in_specs=[pl.BlockSpec((tm, d), lambda i, j: (i, 0)), pl.BlockSpec((1, d), lambda i, j: (0, 0)), wspec] + side_in,
        out_specs=[pl.BlockSpec((tm, d), lambda i, j: (i, 0)), pl.BlockSpec((tm, tn), lambda i, j: (i, j))] + side_out,
        out_shape=[jax.ShapeDtypeStruct((t, d), BF16), jax.ShapeDtypeStruct((t, n), out_dtype)]
        + ([jax.ShapeDtypeStruct((t, ns), F32)] if ns else []),
        scratch_shapes=[pltpu.VMEM((tm, d), BF16)], compiler_params=_cparams(("parallel", "arbitrary")),
    )(h, gamma, w, *([w_side] if ns else []))


CHAIN_ROWS = 256


def _row_chains(tm):
    n = max(tm // CHAIN_ROWS, 1)
    return [slice(r * (tm // n), (r + 1) * (tm // n)) for r in range(n)]


def _ffn_fwd(pfx, h, gamma, wg, wu, wd, plan=None):
    t, d = h.shape
    nb, f, _ = wg.shape
    tm = _pick(t, (1024, 512, 256))
    nt = t // tm
    host = _Hosted(plan, 5, 4)

    def body(*refs):
        (h_ref, gam_ref, wg_ref, wu_ref, wd_ref), (ho_ref, xn_ref, g_ref, u_ref), (xn_scr, acc_ref), prefs = host.split(refs)
        i, j = pl.program_id(0), pl.program_id(1)
        host.run(0, (i == 0) & (j == 0), prefs)
        host.run(1, (i == nt // 2) & (j == 0), prefs)

        @pl.when(j == 0)
        def _():
            xn = _rms_fwd_val(h_ref[...], gam_ref[...]).astype(BF16)
            xn_scr[...] = xn
            xn_ref[...] = xn
            acc_ref[...] = jnp.zeros_like(acc_ref)

        for rows in _row_chains(tm):
            x = xn_scr[rows, :]
            g = _dot(x, wg_ref[...], 1, 1)
            u = _dot(x, wu_ref[...], 1, 1)
            g_ref[rows, :] = g.astype(BF16)
            u_ref[rows, :] = u.astype(BF16)
            acc_ref[rows, :] += _dot(g * _sigmoid(g) * u, wd_ref[...], 1, 0)

        @pl.when(j == nb - 1)
        def _():
            ho_ref[...] = h_ref[...] + 0.5 * acc_ref[...]

        host.run(2, (i == nt - 1) & (j == nb - 1), prefs)

    row = pl.BlockSpec((tm, d), lambda i, j: (i, 0))
    blk = pl.BlockSpec((None, tm, f), lambda i, j: (j, i, 0))
    wspec = pl.BlockSpec((None, f, d), lambda i, j: (j, 0, 0))
    hc = host.call_args()
    res = pl.pallas_call(
        body, name=pfx + "_fwd", grid=(nt, nb),
        in_specs=[row, pl.BlockSpec((1, d), lambda i, j: (0, 0)), wspec, wspec, wspec] + hc["in_specs"],
        out_specs=[row, row, blk, blk] + hc["out_specs"],
        out_shape=[jax.ShapeDtypeStruct((t, d), F32), jax.ShapeDtypeStruct((t, d), BF16),
                   jax.ShapeDtypeStruct((nb, t, f), BF16), jax.ShapeDtypeStruct((nb, t, f), BF16)] + hc["out_shape"],
        scratch_shapes=[pltpu.VMEM((tm, d), BF16), pltpu.VMEM((tm, d), F32)] + hc["scratch"],
        input_output_aliases=hc["aliases"], compiler_params=_cparams(("arbitrary", "arbitrary")),
    )(h, gamma, wg, wu, wd, *hc["args"])
    return res[:4], res[4:]


def _ffn_bwd(pfx, dh_out, h, gamma, xn, g_all, u_all, wg, wu, wd, plan=None):
    t, d = h.shape
    nb, f, _ = wg.shape
    tm = _pick(t, (512, 256))
    tk = _pick(t, (4096, 2048, 1024, 512, 256))

    nt = t // tm
    host = _Hosted(plan, 8, 6)

    def body(*refs):
        ((dy_ref, h_ref, gam_ref, wg_ref, wu_ref, wd_ref, g_ref, u_ref),
         (dh_ref, dgam_ref, dg_ref, du_ref, a_ref, dyb_ref), (acc_ref,), prefs) = host.split(refs)
        i, j = pl.program_id(0), pl.program_id(1)
        host.run(0, (i == 0) & (j == 0), prefs)
        host.run(1, (i == nt // 2) & (j == 0), prefs)

        @pl.when((i == 0) & (j == 0))
        def _():
            dgam_ref[...] = jnp.zeros_like(dgam_ref)

        @pl.when(j == 0)
        def _():
            acc_ref[...] = jnp.zeros_like(acc_ref)
            dyb_ref[...] = dy_ref[...].astype(BF16)

        for rows in _row_chains(tm):
            da = _dot(dy_ref[rows, :], wd_ref[...], 1, 1) * 0.5
            g = g_ref[rows, :].astype(F32)
            u = u_ref[rows, :].astype(F32)
            s = _sigmoid(g)
            sl = g * s
            du = (da * sl).astype(BF16)
            dg = (da * u * (s + sl * (1.0 - s))).astype(BF16)
            du_ref[rows, :] = du
            dg_ref[rows, :] = dg
            a_ref[rows, :] = (sl * u).astype(BF16)
            acc_ref[rows, :] += _dot(dg, wg_ref[...], 1, 0) + _dot(du, wu_ref[...], 1, 0)

        @pl.when(j == nb - 1)
        def _():
            dx, dgam = _rms_bwd_val(acc_ref[...], h_ref[...], gam_ref[...])
            dh_ref[...] = dy_ref[...] + dx
            dgam_ref[...] += dgam

        host.run(2, (i == nt - 1) & (j == nb - 1), prefs)

    row = pl.BlockSpec((tm, d), lambda i, j: (i, 0))
    vec = pl.BlockSpec((1, d), lambda i, j: (0, 0))
    blk = pl.BlockSpec((None, tm, f), lambda i, j: (j, i, 0))
    wspec = pl.BlockSpec((None, f, d), lambda i, j: (j, 0, 0))
    hc = host.call_args()
    res = pl.pallas_call(
        body, name=pfx + "_bwd", grid=(nt, nb),
        in_specs=[row, row, vec, wspec, wspec, wspec, blk, blk] + hc["in_specs"],
        out_specs=[row, vec, blk, blk, blk, row] + hc["out_specs"],
        out_shape=[jax.ShapeDtypeStruct((t, d), F32), jax.ShapeDtypeStruct((1, d), F32)]
        + [jax.ShapeDtypeStruct((nb, t, f), BF16)] * 3 + [jax.ShapeDtypeStruct((t, d), BF16)] + hc["out_shape"],
        scratch_shapes=[pltpu.VMEM((tm, d), F32)] + hc["scratch"], input_output_aliases=hc["aliases"],
        compiler_params=_cparams(("arbitrary", "arbitrary")),
    )(dh_out, h, gamma, wg, wu, wd, g_all, u_all, *hc["args"])
    dh, dgamma, dg_all, du_all, a_all, dyb = res[:6]

    xmap, bmap, omap = (lambda b, k: (k, 0)), (lambda b, k: (b, k, 0)), (lambda b, k: (b, 0, 0))
    dwg, tg = _mm(pfx + "_dwg", [(dg_all, (None, tk, f), bmap, xn, (tk, d), xmap)], (nb, f, d), (None, f, d), omap,
                  (nb, t // tk), 1, ta=True, twin=True)
    dwu, tu = _mm(pfx + "_dwu", [(du_all, (None, tk, f), bmap, xn, (tk, d), xmap)], (nb, f, d), (None, f, d), omap,
                  (nb, t // tk), 1, ta=True, twin=True)
    dwd, td = _mm(pfx + "_dwd", [(a_all, (None, tk, f), bmap, dyb, (tk, d), xmap)], (nb, f, d), (None, f, d), omap,
                  (nb, t // tk), 1, ta=True, scale=0.5, twin=True)
    return (dh, dgamma, dwg, dwu, dwd), res[6:], (tg, tu, td)


def _ffn_bwd_late_dx(pfx, dh_out, h, gamma, xn, g_all, u_all, wg, wu, wd, plan_gu, plans_dw, make_plan_dx):
    t, d = h.shape
    nb, f, _ = wg.shape
    tm = _pick(t, (1024, 512, 256))
    tk = _pick(t, (4096, 2048, 1024, 512, 256))
    nt = t // tm
    host_a = _Hosted(plan_gu, 4, 4)

    def body_a(*refs):
        (dy_ref, wd_ref, g_ref, u_ref), (dg_ref, du_ref, a_ref, dyb_ref), _, prefs = host_a.split(refs)
        i, j = pl.program_id(0), pl.program_id(1)
        host_a.run(0, (i == 0) & (j == 0), prefs)
        host_a.run(1, (i == 0) & (j == 0), prefs)

        @pl.when(j == 0)
        def _():
            dyb_ref[...] = dy_ref[...].astype(BF16)

        for rows in _row_chains(tm):
            da = _dot(dy_ref[rows, :], wd_ref[...], 1, 1) * 0.5
            g = g_ref[rows, :].astype(F32)
            u = u_ref[rows, :].astype(F32)
            s = _sigmoid(g)
            sl = g * s
            du_ref[rows, :] = (da * sl).astype(BF16)
            dg_ref[rows, :] = (da * u * (s + sl * (1.0 - s))).astype(BF16)
            a_ref[rows, :] = (sl * u).astype(BF16)
        host_a.run(2, (i == nt - 1) & (j == nb - 1), prefs)

    row = pl.BlockSpec((tm, d), lambda i, j: (i, 0))
    vec = pl.BlockSpec((1, d), lambda i, j: (0, 0))
    blk = pl.BlockSpec((None, tm, f), lambda i, j: (j, i, 0))
    wspec = pl.BlockSpec((None, f, d), lambda i, j: (j, 0, 0))
    hc = host_a.call_args()
    res_a = pl.pallas_call(
        body_a, name=pfx + "_bwd_gu", grid=(nt, nb), in_specs=[row, wspec, blk, blk] + hc["in_specs"],
        out_specs=[blk] * 3 + [row] + hc["out_specs"],
        out_shape=[jax.ShapeDtypeStruct((nb, t, f), BF16)] * 3 + [jax.ShapeDtypeStruct((t, d), BF16)] + hc["out_shape"],
        scratch_shapes=hc["scratch"], input_output_aliases=hc["aliases"], compiler_params=_cparams(("arbitrary", "arbitrary")),
    )(dh_out, wd, g_all, u_all, *hc["args"])
    dg_all, du_all, a_all, dyb = res_a[:4]

    xmap, bmap, omap = (lambda b, k: (k, 0)), (lambda b, k: (b, k, 0)), (lambda b, k: (b, 0, 0))
    def dw(name, a, b, plan, scale=None):
        r = _mm(pfx + name, [(a, (None, tk, f), bmap, b, (tk, d), xmap)], (nb, f, d), (None, f, d), omap, (nb, t // tk), 1,
                ta=True, scale=scale, plan=plan, twin=True)
        return r if plan is not None else (r, ())

    (dwd, td), out_d = dw("_dwd", a_all, dyb, plans_dw[0], 0.5)
    (dwg, tg), out_g = dw("_dwg", dg_all, xn, plans_dw[1])
    (dwu, tu), out_u = dw("_dwu", du_all, xn, plans_dw[2])

    plan_dx = make_plan_dx((dwg, dwu, dwd), (tg, tu, td))
    host_b = _Hosted(plan_dx, 7, 2)

    def body_b(*refs):
        (dy_ref, h_ref, gam_ref, wg_ref, wu_ref, dg_ref, du_ref), (dh_ref, dgam_ref), (acc_ref,), prefs = host_b.split(refs)
        i, j = pl.program_id(0), pl.program_id(1)
        host_b.run(0, (i == 0) & (j == 0), prefs)
        host_b.run(1, (i == 0) & (j == 0), prefs)

        @pl.when((i == 0) & (j == 0))
        def _():
            dgam_ref[...] = jnp.zeros_like(dgam_ref)

        @pl.when(j == 0)
        def _():
            acc_ref[...] = jnp.zeros_like(acc_ref)

        acc_ref[...] += _dot(dg_ref[...], wg_ref[...], 1, 0) + _dot(du_ref[...], wu_ref[...], 1, 0)

        @pl.when(j == nb - 1)
        def _():
            dx, dgam = _rms_bwd_val(acc_ref[...], h_ref[...], gam_ref[...])
            dh_ref[...] = dy_ref[...] + dx
            dgam_ref[...] += dgam

        host_b.run(2, (i == nt - 1) & (j == nb - 1), prefs)

    hc = host_b.call_args()
    res_b = pl.pallas_call(
        body_b, name=pfx + "_bwd_dx", grid=(nt, nb), in_specs=[row, row, vec, wspec, wspec, blk, blk] + hc["in_specs"],
        out_specs=[row, vec] + hc["out_specs"],
        out_shape=[jax.ShapeDtypeStruct((t, d), F32), jax.ShapeDtypeStruct((1, d), F32)] + hc["out_shape"],
        scratch_shapes=[pltpu.VMEM((tm, d), F32)] + hc["scratch"], input_output_aliases=hc["aliases"],
        compiler_params=_cparams(("arbitrary", "arbitrary")),
    )(dh_out, h, gamma, wg, wu, dg_all, du_all, *hc["args"])
    return (res_b[0], res_b[1], dwg, dwu, dwd), (res_a[4:], out_d, out_g, out_u, res_b[2:])


HALO = 16


def _silu_grad(y):
    s = _sigmoid(y)
    return s * (1.0 + y * (1.0 - s))


def _with_halo(ref, i, n_tiles, tm, before, after):
    t = ref.shape[0]
    r0 = pl.multiple_of(i * tm, tm)
    parts = [ref[pl.ds(r0, tm), :].astype(F32)]
    if before:
        prev = ref[pl.ds(pl.multiple_of(jnp.maximum(r0 - HALO, 0), HALO), HALO), :].astype(F32)
        parts.insert(0, jnp.where(i > 0, prev, 0.0))
    if after:
        nxt = ref[pl.ds(pl.multiple_of(jnp.minimum(r0 + tm, t - HALO), HALO), HALO), :].astype(F32)
        parts.append(jnp.where(i < n_tiles - 1, nxt, 0.0))
    return jnp.concatenate(parts, axis=0)


def _conv_fwd(zbig, w):
    t, c = zbig.shape[0], w.shape[1]
    tm = _pick(t, (512, 256))
    nt = t // tm

    def body(x_ref, w_ref, o_ref):
        xe = _with_halo(x_ref, pl.program_id(0), nt, tm, True, False)
        wv = w_ref[...]
        y = xe * wv[3:4, :]
        for i in range(CONV_W - 1):
            y = y + pltpu.roll(xe, CONV_W - 1 - i, 0) * wv[i:i + 1, :]
        y = y[HALO:, :]
        o_ref[...] = (y * _sigmoid(y)).astype(o_ref.dtype)

    return pl.pallas_call(
        body, name="conv_fwd", grid=(nt,),
        in_specs=[pl.BlockSpec((t, c), lambda i: (0, 0)), pl.BlockSpec(w.shape, lambda i: (0, 0))],
        out_specs=pl.BlockSpec((tm, c), lambda i: (i, 0)), out_shape=jax.ShapeDtypeStruct((t, c), BF16),
        compiler_params=_cparams(("parallel",)),
    )(zbig, w)


def _conv_bwd(zbig, dact, w):
    t, c = dact.shape
    tm = _pick(t, (512, 256))
    nt = t // tm
    n = tm + HALO

    def body(x_ref, d_ref, w_ref, dx_ref, dw_ref):
        xe = _with_halo(x_ref, pl.program_id(0), nt, tm, True, True)
        de = _with_halo(d_ref, pl.program_id(0), nt, tm, False, True)
        wv = w_ref[...]
        sh = [pltpu.roll(xe, CONV_W - 1 - i, 0)[HALO:, :] if i < CONV_W - 1 else xe[HALO:, :] for i in range(CONV_W)]
        y = sh[0] * wv[0:1, :]
        for i in range(1, CONV_W):
            y = y + sh[i] * wv[i:i + 1, :]
        dy = de * _silu_grad(y)
        dx = dy * wv[3:4, :]
        for i in range(CONV_W - 1):
            dx = dx + pltpu.roll(dy, n - (CONV_W - 1 - i), 0) * wv[i:i + 1, :]
        dx_ref[...] = dx[:tm, :].astype(dx_ref.dtype)
        dyc = dy[:tm, :]
        dwp = jnp.concatenate([_colsum(dyc * sh[i][:tm, :]) for i in range(CONV_W)], axis=0)

        @pl.when(pl.program_id(0) == 0)
        def _():
            dw_ref[...] = jnp.zeros_like(dw_ref)
        dw_ref[...] += dwp

    return pl.pallas_call(
        body, name="conv_bwd", grid=(nt,),
        in_specs=[pl.BlockSpec((t, c), lambda i: (0, 0)), pl.BlockSpec((t, c), lambda i: (0, 0)),
                  pl.BlockSpec(w.shape, lambda i: (0, 0))],
        out_specs=[pl.BlockSpec((tm, c), lambda i: (i, 0)), pl.BlockSpec(w.shape, lambda i: (0, 0))],
        out_shape=[jax.ShapeDtypeStruct((t, c), BF16), jax.ShapeDtypeStruct(w.shape, F32)],
        compiler_params=_cparams(("arbitrary",)),
    )(zbig, dact, w)


LM = 256
HI = lax.Precision.HIGHEST


def _logsig(x):
    return jnp.minimum(x, 0.0) - jnp.log(1.0 + jnp.exp(-jnp.abs(x)))


def _tri(n, lower):
    r = lax.broadcasted_iota(jnp.int32, (n, n), 0)
    c = lax.broadcasted_iota(jnp.int32, (n, n), 1)
    return (r >= c) if lower else (r <= c)


def _f32dot(a, b):
    return lax.dot_general(a, b, (((1,), (0,)), ((), ())), precision=HI, preferred_element_type=F32)


def _tri_dot(a, b, a_is_tri):
    tri = (a if a_is_tri else b).astype(BF16)
    parts = _split3(b if a_is_tri else a)
    outs = [_dot(tri, p, 1, 0) if a_is_tri else _dot(p, tri, 1, 0) for p in parts]
    return (outs[0] + outs[1]) + outs[2]


def _mlstm_decays(zs_ref, zsr_ref, bc_ref, br_ref):
    l = LM
    lf_c = _logsig(zs_ref[:, 0:2 * NH_M] + bc_ref[...])
    lf_r = _logsig(zsr_ref[...] + br_ref[...])
    low, up = _tri(l, True), _tri(l, False)
    return _tri_dot(low, lf_c, True), _tri_dot(lf_r, up, False), low, up


def _mlstm_chunk(h, q_ref, k_ref, v_ref, zs_ref, zsr_ref, bc_ref, br_ref, c_prev, m_prev, decays):
    l = LM
    q = q_ref[:, h * DK_M:(h + 1) * DK_M].astype(F32) * (DK_M ** -0.5)
    k = k_ref[:, h * DK_M:(h + 1) * DK_M]
    v = v_ref[:, h * DV_M:(h + 1) * DV_M]
    lane = lax.broadcasted_iota(jnp.int32, (l, DV_M), 1)
    v1 = jnp.concatenate([v, (lane == 0).astype(v.dtype)], axis=1)
    zs, zsr = zs_ref[...], zsr_ref[...]
    li_c = zs[:, h:h + 1] + bc_ref[:, h:h + 1]
    fp_c = zs[:, NH_M + h:NH_M + h + 1] + bc_ref[:, NH_M + h:NH_M + h + 1]
    li_r = zsr[h:h + 1, :] + br_ref[h:h + 1, :]
    fp_r = zsr[NH_M + h:NH_M + h + 1, :] + br_ref[NH_M + h:NH_M + h + 1, :]
    low = decays[2]
    b_c = decays[0][:, NH_M + h:NH_M + h + 1]
    b_r = decays[1][NH_M + h:NH_M + h + 1, :]
    g = b_r[:, l - 1:l]
    dmat = jnp.where(low, b_c - b_r + li_r, -jnp.inf)
    inter = b_c + m_prev
    m_t = jnp.maximum(inter, jnp.max(dmat, axis=1, keepdims=True))
    w_inter = jnp.exp(inter - m_t)
    amat = jnp.exp(dmat - m_t)
    s = _dot(q, k, 1, 1)
    p = amat * s
    qc = _dot(q, c_prev, 1, 0)
    qc_w = w_inter * qc
    num1 = qc_w + _dot(p, v1, 1, 0)
    den = num1[:, DV_M:DV_M + 1]
    mx = jnp.maximum(jnp.abs(den), jnp.exp(-m_t))
    hh = num1[:, :DV_M] / mx
    a_c = g - b_c + li_c
    return dict(q=q, k=k, v1=v1, fp_c=fp_c, fp_r=fp_r, b_c=b_c, g=g, m_t=m_t, w_inter=w_inter, amat=amat, s=s, p=p,
                qc_w=qc_w, den=den, mx=mx, hh=hh, a_c=a_c)


def _mlstm_fwd(qk, zbig, zs, zsr, bc, br, gm):
    t = zs.shape[0]
    l = LM
    nc = t // l
    dm = NH_M * DV_M

    def body(q_ref, k_ref, v_ref, o_ref, zs_ref, zsr_ref, bc_ref, br_ref, gm_ref, y_ref, cst_ref, mst_ref, c_scr, m_scr):
        @pl.when(pl.program_id(0) == 0)
        def _():
            c_scr[...] = jnp.zeros_like(c_scr)
            m_scr[...] = jnp.zeros_like(m_scr)

        cst_ref[...] = c_scr[...]
        mst_ref[...] = m_scr[...]
        ys = []
        decays = _mlstm_decays(zs_ref, zsr_ref, bc_ref, br_ref)
        for h in range(NH_M):
            c_prev = c_scr[h]
            m_prev = m_scr[h:h + 1, 0:1]
            r = _mlstm_chunk(h, q_ref, k_ref, v_ref, zs_ref, zsr_ref, bc_ref, br_ref, c_prev, m_prev, decays)
            hh = r["hh"]
            gh = gm_ref[:, h * DV_M:(h + 1) * DV_M]
            hn = hh * lax.rsqrt(jnp.mean(hh * hh, axis=-1, keepdims=True) + EPS) * gh
            og = o_ref[:, h * DV_M:(h + 1) * DV_M].astype(F32)
            ys.append(hn * _sigmoid(og))
            m_new = jnp.maximum(r["g"] + m_prev, jnp.max(r["a_c"], axis=0, keepdims=True))
            decay = jnp.exp(r["g"] + m_prev - m_new)
            wk = r["k"].astype(F32) * jnp.exp(r["a_c"] - m_new)
            c_scr[h] = decay * c_prev + _dot(wk, r["v1"], 0, 0)
            m_scr[h:h + 1, :] = jnp.broadcast_to(m_new, (1, 128))
        y_ref[...] = jnp.concatenate(ys, axis=1).astype(y_ref.dtype)

    return pl.pallas_call(
        body, name="mlstm_fwd", grid=(nc,),
        in_specs=[pl.BlockSpec((l, NH_M * DK_M), lambda i: (i, 0)), pl.BlockSpec((l, NH_M * DK_M), lambda i: (i, 1)),
                  pl.BlockSpec((l, dm), lambda i: (i, 1)), pl.BlockSpec((l, dm), lambda i: (i, 2)),
                  pl.BlockSpec((l, 128), lambda i: (i, 0)), pl.BlockSpec((8, l), lambda i: (0, i)),
                  pl.BlockSpec((1, 8), lambda i: (0, 0)), pl.BlockSpec((8, 1), lambda i: (0, 0)),
                  pl.BlockSpec((1, dm), lambda i: (0, 0))],
        out_specs=[pl.BlockSpec((l, dm), lambda i: (i, 0)), pl.BlockSpec((None, NH_M, DK_M, 2 * DV_M), lambda i: (i, 0, 0, 0)),
                   pl.BlockSpec((None, 8, 128), lambda i: (i, 0, 0))],
        out_shape=[jax.ShapeDtypeStruct((t, dm), BF16), jax.ShapeDtypeStruct((nc, NH_M, DK_M, 2 * DV_M), F32),
                   jax.ShapeDtypeStruct((nc, 8, 128), F32)],
        scratch_shapes=[pltpu.VMEM((NH_M, DK_M, 2 * DV_M), F32), pltpu.VMEM((8, 128), F32)],
        compiler_params=_cparams(("arbitrary",)),
    )(qk, qk, zbig, zbig, zs, zsr, bc, br, gm)


def _mlstm_bwd(qk, zbig, zs, zsr, bc, br, gm, cst, mst, dycat):
    t = zs.shape[0]
    l = LM
    nc = t // l
    dm = NH_M * DV_M

    def body(q_ref, k_ref, v_ref, o_ref, zs_ref, zsr_ref, bc_ref, br_ref, gm_ref, cst_ref, mst_ref, cnx_ref, mnx_ref,
             dy_ref, dqk_ref, dv_ref, do_ref, dzs_ref, dzr_ref, dgm_ref, dc_scr):
        @pl.when(pl.program_id(0) == 0)
        def _():
            dc_scr[...] = jnp.zeros_like(dc_scr)
            dgm_ref[...] = jnp.zeros_like(dgm_ref)

        lane = lax.broadcasted_iota(jnp.int32, (l, 128), 1)
        db_all, sig_c, carries = jnp.zeros((l, 128), F32), jnp.zeros((l, 128), F32), jnp.zeros((1, 128), F32)
        decays = _mlstm_decays(zs_ref, zsr_ref, bc_ref, br_ref)
        lower, upper = decays[2], decays[3]
        dzr_rows = [None] * 8
        dvs, dos, dgs, dqs, dks = [], [], [], [], []
        dzs = jnp.zeros((l, 128), F32)
        for h in range(NH_M):
            c_prev = cst_ref[h]
            m_prev = mst_ref[h:h + 1, 0:1]
            r = _mlstm_chunk(h, q_ref, k_ref, v_ref, zs_ref, zsr_ref, bc_ref, br_ref, c_prev, m_prev, decays)
            hh, mx, den, m_t, v1, amat = r["hh"], r["mx"], r["den"], r["m_t"], r["v1"], r["amat"]
            gh = gm_ref[:, h * DV_M:(h + 1) * DV_M]
            rs = lax.rsqrt(jnp.mean(hh * hh, axis=-1, keepdims=True) + EPS)
            xh = hh * rs
            sg = _sigmoid(o_ref[:, h * DV_M:(h + 1) * DV_M].astype(F32))
            dyh = dy_ref[:, h * DV_M:(h + 1) * DV_M]
            dos.append(dyh * xh * gh * sg * (1.0 - sg))
            dhn = dyh * sg
            dgs.append(_colsum(dhn * xh))
            dxh = dhn * gh
            dh = rs * (dxh - xh * jnp.mean(dxh * xh, axis=-1, keepdims=True))
            g1 = dh / mx
            hd = jnp.sum(hh * dh, axis=-1, keepdims=True)
            dden = jnp.where(jnp.abs(den) > jnp.exp(-m_t), -hd / mx * jnp.sign(den), 0.0)
            g256 = jnp.concatenate([g1, jnp.where(lane == 0, dden, 0.0)], axis=1)
            dc_h = dc_scr[h]
            ea = jnp.exp(r["a_c"])
            dp = _dot(g256, v1, 1, 1)
            ds = dp * amat
            dqs.append((r["w_inter"] * _dot(g256, c_prev, 1, 1) + _dot(ds, r["k"], 1, 0)) * (DK_M ** -0.5))
            dks.append(_dot(ds, r["q"], 0, 0) + ea * _dot(v1, dc_h, 1, 1))
            dv_st = ea * _dot(r["k"], dc_h, 1, 0)
            dv1 = _dot(r["p"], g256, 0, 0) + dv_st
            dvs.append(dv1[:, :DV_M])
            wmat = dp * r["p"]
            c_in = _colsum(wmat)
            c_st = jnp.sum(v1.astype(F32) * dv_st, axis=-1, keepdims=True)
            r_t = jnp.sum(wmat, axis=1, keepdims=True) + jnp.sum(g256 * r["qc_w"], axis=-1, keepdims=True)
            db = r_t - c_st
            carry = jnp.exp(mnx_ref[h:h + 1, 0:1]) * jnp.sum(
                jnp.sum(dc_h * cnx_ref[h], axis=1, keepdims=True), axis=0, keepdims=True)
            db_all = db_all + jnp.where(lane == NH_M + h, db, 0.0)
            sig_c = sig_c + jnp.where(lane == NH_M + h, _sigmoid(-r["fp_c"]), 0.0)
            carries = carries + jnp.where(lane[0:1, :] == NH_M + h, carry, 0.0)
            dzs = dzs + jnp.where(lane == h, c_st, 0.0)
            dzr_rows[h] = c_in
            dzr_rows[NH_M + h] = _sigmoid(-r["fp_r"])
            wq = r["q"] * jnp.exp(r["b_c"] - m_t)
            dc_scr[h] = jnp.exp(r["g"]) * dc_h + _dot(wq, g256, 0, 0)
        dzs = dzs + (_tri_dot(upper, db_all, True) + carries) * sig_c
        c_in4 = jnp.concatenate(dzr_rows[:NH_M], axis=0)
        dlf_r4 = -_tri_dot(c_in4, lower, False)
        dzr_rows = dzr_rows[:NH_M] + [dlf_r4[h:h + 1, :] * dzr_rows[NH_M + h] for h in range(NH_M)]
        dqk_ref[...] = jnp.concatenate(dqs + dks, axis=1)
        dv_ref[...] = jnp.concatenate(dvs, axis=1).astype(dv_ref.dtype)
        do_ref[...] = jnp.concatenate(dos, axis=1).astype(do_ref.dtype)
        dzs_ref[...] = dzs
        dzr_ref[...] = jnp.concatenate(dzr_rows, axis=0)
        dgm_ref[...] += jnp.concatenate(dgs, axis=1)

    rev = lambda i: nc - 1 - i
    nxt = lambda i: jnp.minimum(nc - i, nc - 1)
    return pl.pallas_call(
        body, name="mlstm_bwd", grid=(nc,),
        in_specs=[pl.BlockSpec((l, NH_M * DK_M), lambda i: (rev(i), 0)), pl.BlockSpec((l, NH_M * DK_M), lambda i: (rev(i), 1)),
                  pl.BlockSpec((l, dm), lambda i: (rev(i), 1)), pl.BlockSpec((l, dm), lambda i: (rev(i), 2)),
                  pl.BlockSpec((l, 128), lambda i: (rev(i), 0)), pl.BlockSpec((8, l), lambda i: (0, rev(i))),
                  pl.BlockSpec((1, 8), lambda i: (0, 0)), pl.BlockSpec((8, 1), lambda i: (0, 0)),
                  pl.BlockSpec((1, dm), lambda i: (0, 0)),
                  pl.BlockSpec((None, NH_M, DK_M, 2 * DV_M), lambda i: (rev(i), 0, 0, 0)),
                  pl.BlockSpec((None, 8, 128), lambda i: (rev(i), 0, 0)),
                  pl.BlockSpec((None, NH_M, DK_M, 2 * DV_M), lambda i: (nxt(i), 0, 0, 0)),
                  pl.BlockSpec((None, 8, 128), lambda i: (nxt(i), 0, 0)),
                  pl.BlockSpec((l, dm), lambda i: (rev(i), 0))],
        out_specs=[pl.BlockSpec((l, dm), lambda i: (rev(i), 0)),
                   pl.BlockSpec((l, dm), lambda i: (rev(i), 0)), pl.BlockSpec((l, dm), lambda i: (rev(i), 0)),
                   pl.BlockSpec((l, 128), lambda i: (rev(i), 0)), pl.BlockSpec((8, l), lambda i: (0, rev(i))),
                   pl.BlockSpec((1, dm), lambda i: (0, 0))],
        out_shape=[jax.ShapeDtypeStruct((t, dm), F32),
                   jax.ShapeDtypeStruct((t, dm), BF16), jax.ShapeDtypeStruct((t, dm), BF16),
                   jax.ShapeDtypeStruct((t, 128), F32), jax.ShapeDtypeStruct((8, t), F32),
                   jax.ShapeDtypeStruct((1, dm), F32)],
        scratch_shapes=[pltpu.VMEM((NH_M, DK_M, 2 * DV_M), F32)],
        compiler_params=_cparams(("arbitrary",)),
    )(qk, qk, zbig, zbig, zs, zsr, bc, br, gm, cst, mst, cst, mst, dycat)


def _fox_cumsum(zsr, bf_r):
    t = zsr.shape[1]
    cw = _pick(t, (512, 256))

    def body(z_ref, b_ref, c_ref):
        up = _tri(cw, False).astype(F32)
        carry = jnp.zeros((NH_F, 1), F32)
        for j in range(t // cw):
            cs = _f32dot(_logsig(z_ref[:, j * cw:(j + 1) * cw] + b_ref[...]), up) + carry
            c_ref[:, j * cw:(j + 1) * cw] = cs
            carry = cs[:, cw - 1:cw]

    return pl.pallas_call(
        body, name="fox_cumsum", grid=(1,),
        in_specs=[pl.BlockSpec((NH_F, t), lambda i: (1, 0)), pl.BlockSpec((NH_F, 1), lambda i: (0, 0))],
        out_specs=pl.BlockSpec((NH_F, t), lambda i: (0, 0)), out_shape=jax.ShapeDtypeStruct((NH_F, t), F32),
        compiler_params=_cparams(("arbitrary",)),
    )(zsr, bf_r)


def _fox_gate_bwd(zsr, bf_r, dc):
    t = zsr.shape[1]
    cw = _pick(t, (512, 256))

    def body(z_ref, b_ref, dc_ref, o_ref):
        low = _tri(cw, True).astype(F32)
        carry = jnp.zeros((NH_F, 1), F32)
        for j in reversed(range(t // cw)):
            sl = slice(j * cw, (j + 1) * cw)
            dlf = _f32dot(dc_ref[:, sl], low) + carry
            o_ref[:, sl] = dlf * _sigmoid(-(z_ref[:, sl] + b_ref[...]))
            carry = dlf[:, 0:1]

    return pl.pallas_call(
        body, name="fox_gate_bwd", grid=(1,),
        in_specs=[pl.BlockSpec((NH_F, t), lambda i: (1, 0)), pl.BlockSpec((NH_F, 1), lambda i: (0, 0)),
                  pl.BlockSpec((NH_F, t), lambda i: (0, 0))],
        out_specs=pl.BlockSpec((NH_F, t), lambda i: (0, 0)), out_shape=jax.ShapeDtypeStruct((NH_F, t), F32),
        compiler_params=_cparams(("arbitrary",)),
    )(zsr, bf_r, dc)


def _causal_mask(n):
    return _tri(n, True)


AUG = 64


def _split3(c):
    hi = c.astype(BF16).astype(F32)
    r1 = c - hi
    mid = r1.astype(BF16).astype(F32)
    return hi, mid, r1 - mid


def _fox_prep(zbig, ct):
    t = zbig.shape[0]
    tm = _pick(t, (1024, 512, 256))

    def body(q_ref, k_ref, v_ref, c_ref, qo_ref, ko_ref, vo_ref):
        lane = lax.broadcasted_iota(jnp.int32, (tm, AUG), 1)
        qv, kv, vv, cv = q_ref[...], k_ref[...], v_ref[...], c_ref[...]
        one = (lane == 0).astype(BF16)
        for h in range(NH_F):
            hi, mid, lo = _split3(cv[:, h:h + 1])
            aq = jnp.where(lane == 0, hi, jnp.where(lane == 1, mid, jnp.where(lane == 2, lo, jnp.where(lane < 6, 1.0, 0.0))))
            ak = jnp.where(lane < 3, 1.0, jnp.where(lane == 3, -hi, jnp.where(lane == 4, -mid, jnp.where(lane == 5, -lo, 0.0))))
            sl = slice(h * DH_F, (h + 1) * DH_F)
            qo_ref[h] = jnp.concatenate([qv[:, sl] * (DH_F ** -0.5), aq.astype(BF16)], axis=1).astype(BF16)
            ko_ref[h] = jnp.concatenate([kv[:, sl], ak.astype(BF16)], axis=1)
            vo_ref[h] = jnp.concatenate([vv[:, sl], one], axis=1)

    ospec = pl.BlockSpec((NH_F, tm, 128), lambda i: (0, i, 0))
    return pl.pallas_call(
        body, name="fox_prep", grid=(t // tm,),
        in_specs=[pl.BlockSpec((tm, 512), lambda i: (i, 3)), pl.BlockSpec((tm, 512), lambda i: (i, 4)),
                  pl.BlockSpec((tm, 512), lambda i: (i, 5)), pl.BlockSpec((tm, NH_F), lambda i: (i, 0))],
        out_specs=[ospec] * 3, out_shape=[jax.ShapeDtypeStruct((NH_F, t, 128), BF16)] * 3,
        compiler_params=_cparams(("parallel",)),
    )(zbig, zbig, zbig, ct)


def _fox_fwd2(qa, ka, va, gf, plan=None):
    nh, t, _ = qa.shape
    tq = _pick(t, (512, 256))
    nq = t // tq
    group = 4
    host = _Hosted(plan, 4, 3)

    def body(*refs):
        (q_ref, k_ref, v_ref, g_ref), (y_ref, o_ref, lse_ref), _, prefs = host.split(refs)
        i = pl.program_id(0)
        host.run(0, i == 0, prefs)
        host.run(1, i == max(nq - 2, 0), prefs)
        lane = lax.broadcasted_iota(jnp.int32, (tq, 128), 1)
        causal = _causal_mask(tq)
        ys, os_ = [], []
        lse_all = jnp.zeros((tq, 128), F32)
        for h0 in range(0, nh, group):
            heads = range(h0, h0 + group)
            qvs = [q_ref[h] for h in heads]

            def blk(j, carry, masked, heads=heads, qvs=qvs):
                k0 = pl.multiple_of(j * tq, tq)
                out = []
                for (m, acc), h, qv in zip(carry, heads, qvs):
                    s = lax.dot_general(qv, k_ref[h, pl.ds(k0, tq), :], (((1,), (1,)), ((), ())), preferred_element_type=F32)
                    if masked:
                        s = jnp.where(causal, s, -jnp.inf)
                    m_new = jnp.maximum(m, jnp.max(s, axis=1, keepdims=True))
                    p = jnp.exp(s - m_new).astype(BF16)
                    pv = lax.dot_general(p, v_ref[h, pl.ds(k0, tq), :], (((1,), (0,)), ((), ())), preferred_element_type=F32)
                    out.append((m_new, jnp.exp(m - m_new) * acc + pv))
                return tuple(out)

            init = tuple((jnp.full((tq, 1), -jnp.inf, F32), jnp.zeros((tq, 128), F32)) for _ in heads)
            carry = lax.fori_loop(0, i, lambda j, c: blk(j, c, False), init)
            for (m, acc), h in zip(blk(i, carry, True), heads):
                l = acc[:, DH_F:DH_F + 1]
                o = acc[:, :DH_F] / l
                os_.append(o)
                gh = g_ref[:, h * DH_F:(h + 1) * DH_F]
                ys.append(o * lax.rsqrt(jnp.mean(o * o, axis=-1, keepdims=True) + EPS) * gh)
                lse_all = lse_all + jnp.where(lane == h, m + jnp.log(l), 0.0)
        y_ref[...] = jnp.concatenate(ys, axis=1).astype(y_ref.dtype)
        o_ref[...] = jnp.concatenate(os_, axis=1)
        lse_ref[...] = lse_all
        host.run(2, i == nq - 1, prefs)

    full = pl.BlockSpec((nh, t, 128), lambda i: (0, 0, 0))
    hc = host.call_args()
    res = pl.pallas_call(
        body, name="fox_fwd", grid=(nq,),
        in_specs=[pl.BlockSpec((nh, tq, 128), lambda i: (0, i, 0)), full, full, pl.BlockSpec((1, nh * DH_F), lambda i: (0, 0))]
        + hc["in_specs"],
        out_specs=[pl.BlockSpec((tq, nh * DH_F), lambda i: (i, 0)), pl.BlockSpec((tq, nh * DH_F), lambda i: (i, 0)),
                   pl.BlockSpec((tq, 128), lambda i: (i, 0))] + hc["out_specs"],
        out_shape=[jax.ShapeDtypeStruct((t, nh * DH_F), BF16), jax.ShapeDtypeStruct((t, nh * DH_F), F32),
                   jax.ShapeDtypeStruct((t, 128), F32)] + hc["out_shape"],
        scratch_shapes=hc["scratch"], input_output_aliases=hc["aliases"], compiler_params=_cparams(("arbitrary",)),
    )(qa, ka, va, gf, *hc["args"])
    return res[:3], res[3:]


def _fox_bwd_prep(dycat, o, gf):
    t = o.shape[0]
    tm = _pick(t, (1024, 512, 256))

    def body(dy_ref, o_ref, g_ref, do_ref, dl_ref, dg_ref):
        lane = lax.broadcasted_iota(jnp.int32, (tm, 128), 1)
        dyv, ov, gv = dy_ref[...], o_ref[...], g_ref[...]
        dgs = []
        dl = jnp.zeros((tm, 128), F32)
        pad = jnp.zeros((tm, AUG), BF16)
        for h in range(NH_F):
            sl = slice(h * DH_F, (h + 1) * DH_F)
            dx, dg = _rms_bwd_val(dyv[:, sl], ov[:, sl], gv[:, sl])
            dgs.append(dg)
            do_ref[h] = jnp.concatenate([dx.astype(BF16), pad], axis=1)
            dl = dl + jnp.where(lane == h, jnp.sum(dx * ov[:, sl], axis=-1, keepdims=True), 0.0)
        dl_ref[...] = dl

        @pl.when(pl.program_id(0) == 0)
        def _():
            dg_ref[...] = jnp.zeros_like(dg_ref)
        dg_ref[...] += jnp.concatenate(dgs, axis=1)

    return pl.pallas_call(
        body, name="fox_bwd_prep", grid=(t // tm,),
        in_specs=[pl.BlockSpec((tm, 512), lambda i: (i, 1)), pl.BlockSpec((tm, 512), lambda i: (i, 0)),
                  pl.BlockSpec((1, 512), lambda i: (0, 0))],
        out_specs=[pl.BlockSpec((NH_F, tm, 128), lambda i: (0, i, 0)), pl.BlockSpec((tm, 128), lambda i: (i, 0)),
                   pl.BlockSpec((1, 512), lambda i: (0, 0))],
        out_shape=[jax.ShapeDtypeStruct((NH_F, t, 128), BF16), jax.ShapeDtypeStruct((t, 128), F32),
                   jax.ShapeDtypeStruct((1, 512), F32)],
        compiler_params=_cparams(("arbitrary",)),
    )(dycat, o, gf)


def _fox_bwd2(qa, ka, va, doa, lse, delta, plan=None):
    nh, t, _ = qa.shape
    tq = _pick(t, (512, 256))
    nq = t // tq

    group = 2

    def tdot(a, b, cb):
        return lax.dot_general(a, b, (((0,), (cb,)), ((), ())), preferred_element_type=F32)

    host = _Hosted(plan, 6, 3)
    ng = nh // group

    def body(*refs):
        (q_ref, k_ref, v_ref, do_ref, lse_ref, dl_ref), (dq_ref, dk_ref, dv_ref), _, prefs = host.split(refs)
        hp, j = pl.program_id(0), pl.program_id(1)
        host.run(0, (hp == 0) & (j == 0), prefs)
        host.run(1, (hp == 0) & (j == 0), prefs)

        @pl.when(j == 0)
        def _():
            dq_ref[...] = jnp.zeros_like(dq_ref)

        lane = lax.broadcasted_iota(jnp.int32, (tq, 128), 1)
        causal = _causal_mask(tq)

        def blk(i, carry, masked):
            rows = pl.ds(pl.multiple_of(i * tq, tq), tq)
            lse_t, dl_t = lse_ref[rows, :], dl_ref[rows, :]
            out = []
            for g, (dk, dv) in enumerate(carry):
                h = hp * group + g
                kb, vb = k_ref[g], v_ref[g]
                qb, dob = q_ref[g, rows, :], do_ref[g, rows, :]
                lse_h = jnp.sum(jnp.where(lane == h, lse_t, 0.0), axis=1, keepdims=True)
                dl_h = jnp.sum(jnp.where(lane == h, dl_t, 0.0), axis=1, keepdims=True)
                s = lax.dot_general(qb, kb, (((1,), (1,)), ((), ())), preferred_element_type=F32)
                if masked:
                    s = jnp.where(causal, s, -jnp.inf)
                p = jnp.exp(s - lse_h)
                dp = lax.dot_general(dob, vb, (((1,), (1,)), ((), ())), preferred_element_type=F32)
                ds = (p * (dp - dl_h)).astype(BF16)
                dv = dv + tdot(dob, p.astype(BF16), 0)
                dk = dk + tdot(qb, ds, 0)
                dq_ref[g, :, rows] += tdot(kb, ds, 1)
                out.append((dk, dv))
            return tuple(out)

        init = tuple((jnp.zeros((128, tq), F32), jnp.zeros((128, tq), F32)) for _ in range(group))
        carry = blk(j, init, True)
        carry = lax.fori_loop(j + 1, nq, lambda i, c: blk(i, c, False), carry)
        for g, (dk, dv) in enumerate(carry):
            dk_ref[g] = dk
            dv_ref[g] = dv
        host.run(2, (hp == ng - 1) & (j == nq - 1), prefs)

    full = pl.BlockSpec((group, t, 128), lambda h, j: (h, 0, 0))
    tile = pl.BlockSpec((group, tq, 128), lambda h, j: (h, j, 0))
    cols = pl.BlockSpec((t, 128), lambda h, j: (0, 0))
    full_t = pl.BlockSpec((group, 128, t), lambda h, j: (h, 0, 0))
    tile_t = pl.BlockSpec((group, 128, tq), lambda h, j: (h, 0, j))
    hc = host.call_args()
    res = pl.pallas_call(
        body, name="fox_bwd", grid=(ng, nq), in_specs=[full, tile, tile, full, cols, cols] + hc["in_specs"],
        out_specs=[full_t, tile_t, tile_t] + hc["out_specs"],
        out_shape=[jax.ShapeDtypeStruct((nh, 128, t), F32)] * 3 + hc["out_shape"], scratch_shapes=hc["scratch"],
        input_output_aliases=hc["aliases"], compiler_params=_cparams(("arbitrary", "arbitrary")),
    )(qa, ka, va, doa, lse, delta, *hc["args"])
    return res[:3], res[3:]


def _fox_bwd_post(dqa, dka, dva):
    nh, _, t = dqa.shape
    tm = _pick(t, (1024, 512, 256))

    def body(dq_ref, dk_ref, dv_ref, oq_ref, ok_ref, ov_ref, dc_ref):
        qs, ks, vs, dcs = [], [], [], []
        for h in range(nh):
            dq, dk = dq_ref[h], dk_ref[h]
            qs.append(dq.T[:, :DH_F] * (DH_F ** -0.5))
            ks.append(dk.T[:, :DH_F])
            vs.append(dv_ref[h].T[:, :DH_F])
            dcs.append(dq[DH_F:DH_F + 1, :] - dk[DH_F + 3:DH_F + 4, :])
        oq_ref[...] = jnp.concatenate(qs, axis=1).astype(BF16)
        ok_ref[...] = jnp.concatenate(ks, axis=1).astype(BF16)
        ov_ref[...] = jnp.concatenate(vs, axis=1).astype(BF16)
        dc_ref[...] = jnp.concatenate(dcs, axis=0)

    ispec = pl.BlockSpec((nh, 128, tm), lambda i: (0, 0, i))
    ospec = pl.BlockSpec((tm, nh * DH_F), lambda i: (i, 0))
    return pl.pallas_call(
        body, name="fox_bwd_post", grid=(t // tm,), in_specs=[ispec] * 3,
        out_specs=[ospec] * 3 + [pl.BlockSpec((nh, tm), lambda i: (0, i))],
        out_shape=[jax.ShapeDtypeStruct((t, nh * DH_F), BF16)] * 3 + [jax.ShapeDtypeStruct((nh, t), F32)],
        compiler_params=_cparams(("parallel",)),
    )(dqa, dka, dva)


IN_OFF = (0, 512, 1024, 1544, 2056, 2568)
W_BIG_COLS = 512 * len(IN_OFF)
IN_GATES = (1536, 3080)


FFN1 = ("ffn1_w_gate", "ffn1_w_up", "ffn1_w_down")
REST = ("w_in", "w_out", "ffn2_w_gate", "ffn2_w_up", "ffn2_w_down", "w_ple_gate", "w_ple_proj")
SPLIT = {n: 1 if n == "w_in" else 0 for n in FFN1 + REST}
SAME_SHAPE = (FFN1, ("ffn2_w_gate", "ffn2_w_up", "ffn2_w_down"), ("w_out", "w_ple_gate"), ("w_in",), ("w_ple_proj",))


def _grouped(names):
    return [tuple(n for n in grp if n in names) for grp in SAME_SHAPE if any(n in names for n in grp)]


def _rs_partials(names, gw, c_idx, twins, run_swap=None):
    wire = [twins[n] if n in twins else _cast_other_half("rs_cast_" + n, gw[n], c_idx, SPLIT[n]) for n in names]
    plan = _swap_plan(wire, [SPLIT[n] if n in twins else None for n in names])
    swapped = dict(zip(names, run_swap(plan) if run_swap else _run_plan("rs_swap_" + names[0], plan)))
    out = {}
    for grp in _grouped(names):
        res = _add_my_half("rs_add_" + grp[0], [gw[n] for n in grp], [swapped[n] for n in grp], c_idx, SPLIT[grp[0]])
        out.update(zip(grp, res))
    return [out[n] for n in names]


def _local_step(x, p, tgt, sp, wg1, wu1, wd1, rest_slots, c_idx, place):
    t, d = x.shape
    slot = dict(zip(REST + ("conv_qk",), rest_slots))
    (h1, xn1, g1, u1), (w_in, conv_w) = _ffn_fwd(
        "ffn1", x, sp["ffn1_norm"], wg1, wu1, wd1, plan=_gather_plan([slot["w_in"], slot["conv_qk"]], [SPLIT["w_in"], None]))
    w_in, conv_w = w_in.reshape(-1, d), _from_chip_blocks(conv_w)
    w_big = jnp.concatenate([w_in[o:o + 512] for o in IN_OFF], axis=0)
    w_small = jnp.concatenate([w_in[IN_GATES[0]:IN_GATES[0] + 8], w_in[IN_GATES[1]:IN_GATES[1] + 8],
                               jnp.zeros((112, d), w_in.dtype)], axis=0)
    u, zbig, zs = _norm_mm("in_big", h1, sp["mix_norm"], w_big, True, BF16, w_side=w_small)
    zsr = zs.T
    qk_act = _conv_fwd(zbig, conv_w)
    bm_c, bf_c = sp["b_mlstm_gates"], sp["b_fox_f"]
    y_m, cst, mst = _mlstm_fwd(qk_act, zbig, zs, zsr, bm_c, bm_c.T, sp["mlstm_out_norm"])
    c = _fox_cumsum(zsr, bf_c.T)
    qa, ka, va = _fox_prep(zbig, c.T)
    (y_ft, o_f, lse), late = _fox_fwd2(qa, ka, va, sp["fox_out_norm"],
                                       plan=_gather_plan([slot[n] for n in REST[1:]], [SPLIT[n] for n in REST[1:]]))
    full = dict(zip(REST[1:], late))
    w_out, w_pg = (full[n].reshape(-1, d) for n in ("w_out", "w_ple_gate"))
    wg2, wu2, wd2 = full["ffn2_w_gate"], full["ffn2_w_up"], full["ffn2_w_down"]
    w_pp = _from_chip_blocks(full["w_ple_proj"])
    tm = _pick(t, (1024, 512, 256))
    h2 = _mm("out_proj", [(y_m, (tm, 512), lambda i, j, k: (i, 0), w_out, (512, d), lambda i, j, k: (0, 0)),
                          (y_ft, (tm, 512), lambda i, j, k: (i, 0), w_out, (512, d), lambda i, j, k: (1, 0))],
             (t, d), (tm, d), lambda i, j, k: (i, 0), (t // tm, 1, 1), 2, res=h1)
    (h3, xn2, g2, u2), _ = _ffn_fwd("ffn2", h2, sp["ffn2_norm"], wg2, wu2, wd2)
    hn3, gate_pre = _norm_mm("ple_gate", h3, sp["ple_gate_norm"], w_pg, False, F32)
    pp = _mm_nn("ple_proj", p, w_pp, tm=1024)

    def head_fn(h3_t, gp_t, pp_t, tgt_t, g_pp, g_fin):
        gate = _sigmoid(gp_t)
        ppn = _rms_fwd_val(pp_t, g_pp)
        h4 = h3_t + gate * ppn
        err = _rms_fwd_val(h4, g_fin) - tgt_t
        loss = 0.5 * jnp.sum(jnp.mean(err * err, axis=-1, keepdims=True), axis=0, keepdims=True)
        dh4, dg_fin = _rms_bwd_val(err * (1.0 / d), h4, g_fin)
        dpp, dg_pp = _rms_bwd_val(dh4 * gate, pp_t, g_pp)
        dgp = dh4 * ppn * gate * (1.0 - gate)
        return dh4, dgp, dpp, jnp.broadcast_to(loss, (1, 128)), dg_fin, dg_pp

    dh4, dgp, dpp, loss_part, dg_fin, dg_pp = _rowwise(
        "loss_head", head_fn, [h3, gate_pre, pp, tgt], [sp["ple_proj_norm"], sp["final_norm"]],
        [(d, F32), (d, BF16), (d, BF16)], [((1, 128), F32), ((1, d), F32), ((1, d), F32)])
    gw, gs = {}, {"final_norm": dg_fin, "ple_proj_norm": dg_pp}
    gw["w_ple_gate"] = _mm_tn("d_w_pg", hn3, dgp, tm=1024, tn=1024)
    gw["w_ple_proj"] = _mm_tn("d_w_pp", p, dpp, tn=1024)
    dhn3 = _mm_nt("d_hn3", dgp, w_pg, tm=1024, tn=1024, tk=1024)

    def res_norm_bwd(dn_t, h_t, dres_t, g):
        dx, dg = _rms_bwd_val(dn_t, h_t, g)
        return dres_t + dx, dg

    dh3, gs["ple_gate_norm"] = _rowwise("ple_norm_bwd", res_norm_bwd, [dhn3, h3, dh4], [sp["ple_gate_norm"]],
                                        [(d, F32)], [((1, d), F32)])
    (dh2, gs["ffn2_norm"], gw["ffn2_w_gate"], gw["ffn2_w_up"], gw["ffn2_w_down"]), _, twins2 = _ffn_bwd(
        "ffn2", dh3, h2, sp["ffn2_norm"], xn2, g2, u2, wg2, wu2, wd2)
    ffn2_names = ("ffn2_w_gate", "ffn2_w_up", "ffn2_w_down")
    early = []

    def swap_in_d_ycat(plan):
        dyc, swapped = _mm_nt("d_ycat", dh2, w_out, tm=1024, tn=1024, tk=1024, plan=plan)
        early.append(dyc)
        return swapped

    part_ffn2 = dict(zip(ffn2_names, _rs_partials(ffn2_names, gw, c_idx, dict(zip(ffn2_names, twins2)), swap_in_d_ycat)))
    dycat = early[0]
    gw["w_out"] = jnp.concatenate([_mm_tn("d_w_out_m", y_m, dh2, tn=1024, tk=2048),
                                   _mm_tn("d_w_out_f", y_ft, dh2, tn=1024, tk=2048)], axis=0)
    doa, delta, gs["fox_out_norm"] = _fox_bwd_prep(dycat, o_f, sp["fox_out_norm"])
    dqkv_t, landed_ffn2 = _fox_bwd2(qa, ka, va, doa, lse, delta, plan=_scatter_plan([part_ffn2[n][1] for n in ffn2_names]))
    dq_f, dk_f, dv_f, dct = _fox_bwd_post(*dqkv_t)
    dfp = _fox_gate_bwd(zsr, bf_c.T, dct)
    dact, dv_m, do_m, dzs_m, dzr_m, gs["mlstm_out_norm"] = _mlstm_bwd(
        qk_act, zbig, zs, zsr, bm_c, bm_c.T, sp["mlstm_out_norm"], cst, mst, dycat)
    dqk, gw["conv_qk"] = _conv_bwd(zbig, dact, conv_w)
    dz_big = jnp.concatenate([dqk, dv_m, do_m, dq_f, dk_f, dv_f], axis=1)
    dzs = dzs_m + jnp.pad(jnp.concatenate([dzr_m, dfp], axis=0).T, ((0, 0), (0, 112)))
    dw_big = _mm_tn("d_w_big", dz_big, u, tn=1024)
    dw_small = _mm_tn("d_w_small", dzs, u, tn=1024)
    gw["w_in"] = jnp.concatenate([dw_big[0:1536], dw_small[0:8], dw_big[1536:3072], dw_small[8:16]], axis=0)
    tmu = _pick(t, (1024, 512, 256))
    row = lambda i, j, k: (i, 0)
    whole = lambda i, j, k: (0, 0)
    du = _mm("d_u", [(dz_big, (tmu, W_BIG_COLS), row, w_big, (W_BIG_COLS, d), whole),
                     (dzs, (tmu, 128), row, w_small, (128, d), whole)],
             (t, d), (tmu, d), row, (t // tmu, 1, 1), 2)

    def mix_norm_bwd(da_t, h_t, dres_t, dzs_t, g):
        dx, dg = _rms_bwd_val(da_t, h_t, g)
        return dres_t + dx, dg, _colsum(dzs_t)

    conv_grad = gw.pop("conv_qk")
    gw["w_ple_proj"] = _chip_blocks(gw["w_ple_proj"])
    for n in ("w_in", "w_out", "w_ple_gate"):
        gw[n] = gw[n].reshape(4, -1, gw[n].shape[-1])
    mix = []

    def swap_in_mix_norm_bwd(plan):
        res, swapped = _rowwise("mix_norm_bwd", mix_norm_bwd, [du, h1, dh2, dzs], [sp["mix_norm"]],
                                [(d, F32)], [((1, d), F32), ((1, 128), F32)], plan=plan)
        mix.extend(res)
        return swapped

    light = ("w_in", "w_out", "w_ple_gate", "w_ple_proj")
    part_light = dict(zip(light, _rs_partials(light, gw, c_idx, {}, swap_in_mix_norm_bwd)))
    dh1, gs["mix_norm"], dbias = mix
    gs["b_mlstm_gates"], gs["b_fox_f"] = dbias[:, 0:8], dbias[:, 8:16]
    part_ffn1 = []

    def own_plan(dws, dw_twins):
        part_ffn1.extend(_rs_partials(FFN1, dict(zip(FFN1, dws)), c_idx, dict(zip(FFN1, dw_twins))))
        return _scatter_plan([pb for _, pb in part_ffn1])

    (grad_x, gs["ffn1_norm"], _, _, _), (l_light, _, _, _, landed_ffn1) = _ffn_bwd_late_dx(
        "ffn1", dh1, x, sp["ffn1_norm"], xn1, g1, u1, wg1, wu1, wd1,
        _scatter_plan([part_light[n][1] for n in light]), [None] * 3, own_plan)
    names = REST + FFN1
    parts = {**part_light, **part_ffn2, **dict(zip(FFN1, part_ffn1))}
    landed = {**dict(zip(light, l_light)), **dict(zip(ffn2_names, landed_ffn2)), **dict(zip(FFN1, landed_ffn1))}
    mine = {}
    for grp in _grouped(names):
        res = _sum4("rs_sum_" + grp[0], [landed[n] for n in grp], [parts[n][0] for n in grp], place, SPLIT[grp[0]])
        mine.update(zip(grp, res))
    grads = dict(zip(names, _join_halves("rs_join", [mine[n] for n in names], [SPLIT[n] for n in names])))
    return loss_part, grad_x, grads, gs, conv_grad


ANY = pl.BlockSpec(memory_space=pl.ANY)
MESH = pl.DeviceIdType.MESH


def _place():
    x, y, c = lax.axis_index("x"), lax.axis_index("y"), lax.axis_index("c")
    chips = [(1 - x, y), (x, 1 - y), (1 - x, 1 - y)]
    return x, y, c, 2 * x + y, (x, y, 1 - c), chips


def _rcopy(src, dst, ssem, rsem, dev):
    return pltpu.make_async_remote_copy(src_ref=src, dst_ref=dst, send_sem=ssem, recv_sem=rsem, device_id=dev,
                                        device_id_type=MESH)


def _half(ref, lead, axis, idx, half):
    return ref.at[(slice(None),) * (lead + axis) + (pl.ds(idx * half, half),)]


def _to_slot(name, arrs, me_idx, dtype):
    n = len(arrs)
    r, cdim = arrs[0].shape
    tr = _pick(r, (352, 256, 176, 128, 64))

    def body(me_ref, *refs):
        for k in range(n):
            refs[n + k][...] = refs[k][...].astype(dtype)

    return pl.pallas_call(
        body, name=name,
        grid_spec=pltpu.PrefetchScalarGridSpec(
            num_scalar_prefetch=1, grid=(r // tr,), in_specs=[pl.BlockSpec((tr, cdim), lambda i, me_ref: (i, 0))] * n,
            out_specs=[pl.BlockSpec((None, tr, cdim), lambda i, me_ref: (me_ref[0], i, 0))] * n),
        out_shape=[jax.ShapeDtypeStruct((4, r, cdim), dtype)] * n, compiler_params=_cparams(("parallel",)),
    )(me_idx, *arrs)


def _gather4(name, bufs, split):
    return _run_plan(name, _gather_plan(bufs, split))


def _gather_plan(bufs, split):
    n = len(bufs)
    shapes = [b.shape[1:] for b in bufs]

    def ctx(outs):
        x, y, c, me, sib, chips = _place()

        def part(ref, a, which):
            if split[a] is None:
                return ref
            return _half(ref, 0, split[a], which, shapes[a][split[a]] // 2)

        return c, me, sib, chips, part

    def ici(outs, sems, a, j, chip, c, me, part):
        mine = part(outs[a].at[me], a, c)
        return _rcopy(mine, mine, sems[0].at[3 * a + j], sems[1].at[3 * a + j], (*chip, c))

    def fwd(outs, sems, a, j, chip, c, sib, part, which):
        blk = part(outs[a].at[2 * chip[0] + chip[1]], a, which)
        return _rcopy(blk, blk, sems[2].at[3 * a + j], sems[3].at[3 * a + j], sib)

    def start(ins, outs, sems):
        c, me, sib, chips, part = ctx(outs)
        for a in range(n):
            for j, chip in enumerate(chips):
                ici(outs, sems, a, j, chip, c, me, part).start()

    def mid(ins, outs, sems):
        c, me, sib, chips, part = ctx(outs)
        for j, chip in enumerate(chips):
            for a in range(n):
                blk = part(outs[a].at[2 * chip[0] + chip[1]], a, c)
                _rcopy(blk, blk, sems[0].at[3 * a + j], sems[1].at[3 * a + j], sib).wait_recv()
                if split[a] is not None:
                    fwd(outs, sems, a, j, chip, c, sib, part, c).start()

    def end(ins, outs, sems):
        c, me, sib, chips, part = ctx(outs)
        for j, chip in enumerate(chips):
            for a in range(n):
                if split[a] is not None:
                    fwd(outs, sems, a, j, chip, c, sib, part, 1 - c).wait_recv()
        for a in range(n):
            for j, chip in enumerate(chips):
                ici(outs, sems, a, j, chip, c, me, part).wait_send()
                if split[a] is not None:
                    fwd(outs, sems, a, j, chip, c, sib, part, c).wait_send()

    return dict(ins=list(bufs), outs=[jax.ShapeDtypeStruct(b.shape, b.dtype) for b in bufs], alias=True,
                sems=[pltpu.SemaphoreType.DMA((3 * n,))] * 4, phases=(start, mid, end))


def _run_plan(name, plan):
    ni, no = len(plan["ins"]), len(plan["outs"])

    def body(*refs):
        ins, outs, sems = refs[:ni], refs[ni:ni + no], refs[ni + no:]
        for phase in plan["phases"]:
            phase(ins, outs, sems)

    return pl.pallas_call(
        body, name=name, in_specs=[ANY] * ni, out_specs=[ANY] * no, out_shape=plan["outs"],
        input_output_aliases={a: a for a in range(ni)} if plan["alias"] else {}, scratch_shapes=plan["sems"],
    )(*plan["ins"])


class _Hosted:
    def __init__(self, plan, n_in, n_out):
        self.plan, self.n_in, self.n_out = plan, n_in, n_out
        self.ni, self.no, self.ns = (len(plan["ins"]) if plan else 0, len(plan["outs"]) if plan else 0,
                                     len(plan["sems"]) if plan else 0)

    def split(self, refs):
        a, b = self.n_in, self.n_in + self.ni
        c, d = b + self.n_out, b + self.n_out + self.no
        e = len(refs) - self.ns
        return refs[:a], refs[b:c], refs[d:e], (refs[a:b], refs[c:d], refs[e:])

    def run(self, k, cond, prefs):
        if self.plan is not None:
            @pl.when(cond)
            def _():
                self.plan["phases"][k](*prefs)

    def call_args(self):
        p = self.plan
        if p is None:
            return dict(in_specs=[], out_specs=[], out_shape=[], scratch=[], aliases={}, args=[])
        al = {self.n_in + a: self.n_out + a for a in range(self.ni)} if p["alias"] else {}
        return dict(in_specs=[ANY] * self.ni, out_specs=[ANY] * self.no, out_shape=list(p["outs"]), scratch=list(p["sems"]),
                    aliases=al, args=list(p["ins"]))


def _swap(name, arrs, halve):
    return _run_plan(name, _swap_plan(arrs, halve))


def _swap_plan(arrs, halve):
    n = len(arrs)

    def half_shape(a, ax):
        return a.shape if ax is None else (a.shape[0],) + tuple(d // 2 if i == ax else d for i, d in enumerate(a.shape[1:]))

    def copies(ins, outs, sems):
        x, y, c, me, sib, chips = _place()
        cps = []
        for a in range(n):
            src = ins[a] if halve[a] is None else _half(ins[a], 1, halve[a], 1 - c, arrs[a].shape[1 + halve[a]] // 2)
            cps.append(_rcopy(src, outs[a], sems[0].at[a], sems[1].at[a], sib))
        return cps

    def start(ins, outs, sems):
        for cp in copies(ins, outs, sems):
            cp.start()

    def mid(ins, outs, sems):
        pass

    def end(ins, outs, sems):
        for cp in copies(ins, outs, sems):
            cp.wait()

    return dict(ins=list(arrs), outs=[jax.ShapeDtypeStruct(half_shape(a, ax), a.dtype) for a, ax in zip(arrs, halve)],
                alias=False, sems=[pltpu.SemaphoreType.DMA((n,))] * 2, phases=(start, mid, end))


def _scatter4(name, arrs):
    return _run_plan(name, _scatter_plan(arrs))


def _scatter_plan(arrs):
    n = len(arrs)

    def send(ins, outs, sems, a, j, chip, c, me):
        return _rcopy(ins[a].at[2 * chip[0] + chip[1]], outs[a].at[me], sems[0].at[3 * a + j], sems[1].at[3 * a + j], (*chip, c))

    def start(ins, outs, sems):
        x, y, c, me, sib, chips = _place()
        for a in range(n):
            for j, chip in enumerate(chips):
                send(ins, outs, sems, a, j, chip, c, me).start()

    def mid(ins, outs, sems):
        pass

    def end(ins, outs, sems):
        x, y, c, me, sib, chips = _place()
        for a in range(n):
            for j, chip in enumerate(chips):
                blk = outs[a].at[2 * chip[0] + chip[1]]
                _rcopy(blk, blk, sems[0].at[3 * a + j], sems[1].at[3 * a + j], sib).wait_recv()
        for a in range(n):
            for j, chip in enumerate(chips):
                send(ins, outs, sems, a, j, chip, c, me).wait_send()

    return dict(ins=list(arrs), outs=[jax.ShapeDtypeStruct(a.shape, a.dtype) for a in arrs], alias=False,
                sems=[pltpu.SemaphoreType.DMA((3 * n,))] * 2, phases=(start, mid, end))


def _join_halves(name, arrs, split):
    n = len(arrs)

    def body(*refs):
        outs = refs[n:2 * n]
        ssem, rsem = refs[2 * n:]
        x, y, c, me, sib, chips = _place()
        cps = []
        for a in range(n):
            mine = _half(outs[a], 0, split[a], c, arrs[a].shape[split[a]] // 2)
            cp = _rcopy(mine, mine, ssem.at[a], rsem.at[a], sib)
            cp.start()
            cps.append(cp)
        for a in range(n):
            blk = _half(outs[a], 0, split[a], 1 - c, arrs[a].shape[split[a]] // 2)
            _rcopy(blk, blk, ssem.at[a], rsem.at[a], sib).wait_recv()
        for cp in cps:
            cp.wait_send()

    return pl.pallas_call(
        body, name=name, in_specs=[ANY] * n, out_specs=[ANY] * n,
        out_shape=[jax.ShapeDtypeStruct(a.shape, a.dtype) for a in arrs],
        input_output_aliases={a: a for a in range(n)}, scratch_shapes=[pltpu.SemaphoreType.DMA((n,))] * 2,
    )(*arrs)


def _allreduce_small(s):
    r, cdim = s.shape

    def body(s_ref, o_ref, buf, ssem, rsem):
        x, y, c, me, sib, chips = _place()
        me8 = 4 * x + 2 * y + c
        buf[me8] = s_ref[...]
        flips = [(fx, fy, fc) for fx in (0, 1) for fy in (0, 1) for fc in (0, 1)][1:]
        cps = []
        for k, (fx, fy, fc) in enumerate(flips):
            peer = (x ^ fx if fx else x, y ^ fy if fy else y, c ^ fc if fc else c)
            cp = _rcopy(s_ref, buf.at[me8], ssem.at[k], rsem.at[k], peer)
            cp.start()
            cps.append(cp)
        for k, (fx, fy, fc) in enumerate(flips):
            src = 4 * (x ^ fx if fx else x) + 2 * (y ^ fy if fy else y) + (c ^ fc if fc else c)
            _rcopy(s_ref, buf.at[src], ssem.at[k], rsem.at[k], sib).wait_recv()
        for cp in cps:
            cp.wait_send()
        acc = buf[0]
        for k in range(1, 8):
            acc = acc + buf[k]
        o_ref[...] = acc

    vm = pl.BlockSpec(memory_space=pltpu.VMEM)
    return pl.pallas_call(
        body, name="allreduce_small", in_specs=[vm], out_specs=vm, out_shape=jax.ShapeDtypeStruct((r, cdim), F32),
        scratch_shapes=[pltpu.VMEM((8, r, cdim), F32), pltpu.SemaphoreType.DMA((7,)), pltpu.SemaphoreType.DMA((7,))],
    )(s)


def _add_my_half(name, gs, recvs, c_idx, axis):
    n = len(gs)
    nb, hr, hc = recvs[0].shape
    tr = _pick(hr, (256, 176, 128, 64))
    if axis == 0:
        g4s = [g.reshape(nb, 2, hr, hc) for g in gs]
        gspec = pl.BlockSpec((None, None, tr, hc), lambda b, i, c_ref: (b, c_ref[0], i, 0))
    else:
        g4s = list(gs)
        gspec = pl.BlockSpec((None, tr, hc), lambda b, i, c_ref: (b, i, c_ref[0]))

    def body(c_ref, *refs):
        for k in range(n):
            s = refs[k][...] + refs[n + k][...].astype(F32)
            refs[2 * n + 2 * k][...] = s
            refs[2 * n + 2 * k + 1][...] = s.astype(BF16)

    ospec = pl.BlockSpec((None, tr, hc), lambda b, i, c_ref: (b, i, 0))
    res = pl.pallas_call(
        body, name=name,
        grid_spec=pltpu.PrefetchScalarGridSpec(
            num_scalar_prefetch=1, grid=(nb, hr // tr), in_specs=[gspec] * n + [ospec] * n, out_specs=[ospec] * (2 * n)),
        out_shape=[jax.ShapeDtypeStruct((nb, hr, hc), F32), jax.ShapeDtypeStruct((nb, hr, hc), BF16)] * n,
        compiler_params=_cparams(("parallel", "parallel")),
    )(c_idx, *g4s, *recvs)
    return [(res[2 * k], res[2 * k + 1]) for k in range(n)]


def _sum4(name, landeds, owns, place, axis):
    n = len(landeds)
    nb, h, cdim = landeds[0].shape
    tr = _pick(h, (256, 176, 128, 64))
    nt = h // tr

    def body(p_ref, *refs):
        for k in range(n):
            a1, a2, a3, own = refs[4 * k:4 * k + 4]
            refs[4 * n + k][...] = ((own[...] + a1[...].astype(F32)) + a2[...].astype(F32)) + a3[...].astype(F32)

    def nxt(k):
        return pl.BlockSpec((None, tr, cdim), lambda i, p_ref: ((p_ref[0] + k) % nb, i, 0))

    if axis == 0:
        ospec = pl.BlockSpec((tr, cdim), lambda i, p_ref: (p_ref[1] * nt + i, 0))
        oshape = (2 * h, cdim)
    else:
        ospec = pl.BlockSpec((tr, cdim), lambda i, p_ref: (i, p_ref[1]))
        oshape = (h, 2 * cdim)
    args = []
    for landed, own in zip(landeds, owns):
        args += [landed, landed, landed, own]
    return pl.pallas_call(
        body, name=name,
        grid_spec=pltpu.PrefetchScalarGridSpec(
            num_scalar_prefetch=1, grid=(nt,), in_specs=[nxt(1), nxt(2), nxt(3), nxt(0)] * n, out_specs=[ospec] * n),
        out_shape=[jax.ShapeDtypeStruct(oshape, F32)] * n, compiler_params=_cparams(("parallel",)),
    )(place, *args)


def _cast_other_half(name, g, c_idx, axis):
    nb, r, cdim = g.shape
    hr, hc = (r // 2, cdim) if axis == 0 else (r, cdim // 2)
    tr = _pick(hr, (256, 176, 128, 64))
    if axis == 0:
        g4 = g.reshape(nb, 2, hr, hc)
        gspec = pl.BlockSpec((None, None, tr, hc), lambda b, i, c_ref: (b, 1 - c_ref[0], i, 0))
    else:
        g4 = g
        gspec = pl.BlockSpec((None, tr, hc), lambda b, i, c_ref: (b, i, 1 - c_ref[0]))

    def body(c_ref, g_ref, o_ref):
        o_ref[...] = g_ref[...].astype(BF16)

    return pl.pallas_call(
        body, name=name,
        grid_spec=pltpu.PrefetchScalarGridSpec(
            num_scalar_prefetch=1, grid=(nb, hr // tr), in_specs=[gspec],
            out_specs=pl.BlockSpec((None, tr, hc), lambda b, i, c_ref: (b, i, 0))),
        out_shape=jax.ShapeDtypeStruct((nb, hr, hc), BF16), compiler_params=_cparams(("parallel", "parallel")),
    )(c_idx, g4)


def _adamw(name, ws, gs, ms, vs):
    n = len(ws)
    c1 = 1.0 - ADAM_B1 ** ADAM_STEP
    c2 = 1.0 - ADAM_B2 ** ADAM_STEP

    def fn(*tiles):
        out = []
        for k in range(n):
            w_t, g_t, m_t, v_t = tiles[4 * k:4 * k + 4]
            m_n = ADAM_B1 * m_t + (1.0 - ADAM_B1) * g_t
            v_n = ADAM_B2 * v_t + (1.0 - ADAM_B2) * (g_t * g_t)
            out += [-ADAM_LR * ((m_n / c1) / (jnp.sqrt(v_n / c2) + ADAM_EPS) + ADAM_WD * w_t), m_n, v_n]
        return out

    rows, cdim = ws[0].shape
    tiled = [a for quad in zip(ws, gs, ms, vs) for a in quad]
    pref = (512, 352, 256, 128, 64, 8) if n == 1 else (176, 128, 64, 8)
    res = _rowwise(name, fn, tiled, [], [(cdim, F32)] * (3 * n), tm=_pick(rows, pref))
    return [tuple(res[3 * k:3 * k + 3]) for k in range(n)]


BIG = ("ffn1_w_gate", "ffn1_w_up", "ffn1_w_down", "w_in", "w_out", "ffn2_w_gate", "ffn2_w_up", "ffn2_w_down",
       "w_ple_gate", "w_ple_proj")
SMALL = ("ffn1_norm", "mix_norm", "b_mlstm_gates", "b_fox_f", "mlstm_out_norm", "fox_out_norm", "ffn2_norm",
         "ple_gate_norm", "ple_proj_norm", "final_norm")
WEIGHTS = ("ffn1_norm", "ffn1_w_gate", "ffn1_w_up", "ffn1_w_down", "mix_norm", "w_in", "conv_qk", "b_mlstm_gates",
           "b_fox_f", "mlstm_out_norm", "fox_out_norm", "w_out", "ffn2_norm", "ffn2_w_gate", "ffn2_w_up", "ffn2_w_down",
           "ple_gate_norm", "w_ple_gate", "w_ple_proj", "ple_proj_norm", "final_norm")
TRANSPOSED = ("ffn1_w_gate", "ffn1_w_up", "w_in", "ffn2_w_gate", "ffn2_w_up")
PACK_W = 1024


def _chip_blocks(a):
    r, c4 = a.shape
    return a.reshape(r, 4, c4 // 4).transpose(1, 0, 2)


def _from_chip_blocks(a):
    nb, r, c = a.shape
    return a.transpose(1, 0, 2).reshape(r, nb * c)


def kernel(x, p, ffn1_norm, ffn1_w_gate, ffn1_w_up, ffn1_w_down, mix_norm, w_in, conv_qk, b_mlstm_gates, b_fox_f, mlstm_out_norm, fox_out_norm, w_out, ffn2_norm, ffn2_w_gate, ffn2_w_up, ffn2_w_down, ple_gate_norm, w_ple_gate, w_ple_proj, ple_proj_norm, final_norm, loss_target, m_ffn1_norm, m_ffn1_w_gate, m_ffn1_w_up, m_ffn1_w_down, m_mix_norm, m_w_in, m_conv_qk, m_b_mlstm_gates, m_b_fox_f, m_mlstm_out_norm, m_fox_out_norm, m_w_out, m_ffn2_norm, m_ffn2_w_gate, m_ffn2_w_up, m_ffn2_w_down, m_ple_gate_norm, m_w_ple_gate, m_w_ple_proj, m_ple_proj_norm, m_final_norm, v_ffn1_norm, v_ffn1_w_gate, v_ffn1_w_up, v_ffn1_w_down, v_mix_norm, v_w_in, v_conv_qk, v_b_mlstm_gates, v_b_fox_f, v_mlstm_out_norm, v_fox_out_norm, v_w_out, v_ffn2_norm, v_ffn2_w_gate, v_ffn2_w_up, v_ffn2_w_down, v_ple_gate_norm, v_w_ple_gate, v_w_ple_proj, v_ple_proj_norm, v_final_norm):
    w = dict(ffn1_norm=ffn1_norm, ffn1_w_gate=ffn1_w_gate, ffn1_w_up=ffn1_w_up, ffn1_w_down=ffn1_w_down, mix_norm=mix_norm,
             w_in=w_in, conv_qk=conv_qk, b_mlstm_gates=b_mlstm_gates, b_fox_f=b_fox_f, mlstm_out_norm=mlstm_out_norm,
             fox_out_norm=fox_out_norm, w_out=w_out, ffn2_norm=ffn2_norm, ffn2_w_gate=ffn2_w_gate, ffn2_w_up=ffn2_w_up,
             ffn2_w_down=ffn2_w_down, ple_gate_norm=ple_gate_norm, w_ple_gate=w_ple_gate, w_ple_proj=w_ple_proj,
             ple_proj_norm=ple_proj_norm, final_norm=final_norm)
    m = dict(ffn1_norm=m_ffn1_norm, ffn1_w_gate=m_ffn1_w_gate, ffn1_w_up=m_ffn1_w_up, ffn1_w_down=m_ffn1_w_down,
             mix_norm=m_mix_norm, w_in=m_w_in, conv_qk=m_conv_qk, b_mlstm_gates=m_b_mlstm_gates, b_fox_f=m_b_fox_f,
             mlstm_out_norm=m_mlstm_out_norm, fox_out_norm=m_fox_out_norm, w_out=m_w_out, ffn2_norm=m_ffn2_norm,
             ffn2_w_gate=m_ffn2_w_gate, ffn2_w_up=m_ffn2_w_up, ffn2_w_down=m_ffn2_w_down, ple_gate_norm=m_ple_gate_norm,
             w_ple_gate=m_w_ple_gate, w_ple_proj=m_w_ple_proj, ple_proj_norm=m_ple_proj_norm, final_norm=m_final_norm)
    v = dict(ffn1_norm=v_ffn1_norm, ffn1_w_gate=v_ffn1_w_gate, ffn1_w_up=v_ffn1_w_up, ffn1_w_down=v_ffn1_w_down,
             mix_norm=v_mix_norm, w_in=v_w_in, conv_qk=v_conv_qk, b_mlstm_gates=v_b_mlstm_gates, b_fox_f=v_b_fox_f,
             mlstm_out_norm=v_mlstm_out_norm, fox_out_norm=v_fox_out_norm, w_out=v_w_out, ffn2_norm=v_ffn2_norm,
             ffn2_w_gate=v_ffn2_w_gate, ffn2_w_up=v_ffn2_w_up, ffn2_w_down=v_ffn2_w_down, ple_gate_norm=v_ple_gate_norm,
             w_ple_gate=v_w_ple_gate, w_ple_proj=v_w_ple_proj, ple_proj_norm=v_ple_proj_norm, final_norm=v_final_norm)
    shapes = {n: w[n].shape for n in WEIGHTS}

    def view(a, n):
        return a[0].T if n in TRANSPOSED else a.reshape(-1, a.shape[-1])

    def unview(a, n):
        return (a.T if n in TRANSPOSED else a).reshape(shapes[n])

    w2, m2, v2 = ({n: view(a, n) for n, a in d.items()} for d in (w, m, v))

    c_idx = lax.axis_index("c").astype(jnp.int32).reshape(1)
    me_idx = (2 * lax.axis_index("x") + lax.axis_index("y")).astype(jnp.int32).reshape(1)
    place = jnp.concatenate([me_idx, c_idx])
    slot = {}
    for grp in SAME_SHAPE:
        slot.update(zip(grp, _to_slot("slot_" + grp[0], [w2[n] for n in grp], me_idx, BF16)))
    slot["conv_qk"] = _to_slot("slot_conv_qk", [w2["conv_qk"]], me_idx, F32)[0]
    wg1, wu1, wd1 = _gather4("gather_ffn1", [slot[n] for n in FFN1], [SPLIT[n] for n in FFN1])
    sp = {n: w2[n] for n in SMALL}
    loss_part, grad_x, grads, gs, conv_grad = _local_step(
        x[0], p[0, 0], loss_target[0], sp, wg1, wu1, wd1, [slot[n] for n in REST + ("conv_qk",)], c_idx, place)

    small = [gs[n].reshape(1, -1) for n in SMALL] + [conv_grad, loss_part]
    rows = [jnp.pad(a, ((0, 0), (0, PACK_W - a.shape[1]))) for a in small]
    packed = jnp.concatenate(rows, axis=0)
    packed = jnp.pad(packed, ((0, -packed.shape[0] % 8), (0, 0)))
    red = _allreduce_small(packed)
    loss = red[len(SMALL) + CONV_W, 0]
    for i, n in enumerate(SMALL):
        grads[n] = red[i:i + 1, :gs[n].size]
    dconv = red[len(SMALL):len(SMALL) + CONV_W, :conv_grad.shape[1]]
    cw = conv_qk.shape[-1]
    grads["conv_qk"] = lax.dynamic_slice_in_dim(dconv, (2 * lax.axis_index("x") + lax.axis_index("y")) * cw, cw, axis=1)

    outs = {}
    for grp in SAME_SHAPE + tuple((n,) for n in WEIGHTS if n not in BIG):
        g2s = [grads[n].reshape(w2[n].shape) for n in grp]
        res = _adamw("adamw_" + grp[0], [w2[n] for n in grp], g2s, [m2[n] for n in grp], [v2[n] for n in grp])
        for n, g2, (d, nm, nv) in zip(grp, g2s, res):
            outs[n] = tuple(unview(a, n) for a in (g2, d, nm, nv))
    return (loss, grad_x[None], *[outs[n][0] for n in WEIGHTS], *[outs[n][1] for n in WEIGHTS],
            *[outs[n][2] for n in WEIGHTS], *[outs[n][3] for n in WEIGHTS])
```
